```python
import math
import jax, jax.numpy as jnp
from jax import lax
import numpy as np

D_MODEL = 1024
BATCH = 16
SEQ = 2048
DEPTH = 4

N_A = DEPTH // 2
N_B = DEPTH - N_A
CONV_WIDTH = 31
FFN_CONV_WIDTH = 3
D_FF = 2816
N_HEADS = 8
HEAD_DIM = D_MODEL // N_HEADS
WINDOWS = (128, 512, 2048)
DILATIONS = (1, 4, 16)
N_GROUPS = len(WINDOWS)
Q_WIDTH = N_GROUPS * N_HEADS * HEAD_DIM
BLOCK = 128
EPS = 1e-6
NEG_INF = -1e30

kernel_name = "yoco_conformer_dilated_hybrid"


def rms_norm(x, g):
    xf = x.astype(jnp.float32)
    y = xf * lax.rsqrt(jnp.mean(xf * xf, axis=-1, keepdims=True) + EPS)
    return (y * g.astype(jnp.float32)).astype(x.dtype)


def layer_norm(x, g, b):
    xf = x.astype(jnp.float32)
    mu = jnp.mean(xf, axis=-1, keepdims=True)
    var = jnp.mean(jnp.square(xf - mu), axis=-1, keepdims=True)
    y = (xf - mu) * lax.rsqrt(var + EPS)
    return (y * g.astype(jnp.float32) + b.astype(jnp.float32)).astype(x.dtype)


def causal_dwconv(x, w, b):
    k, c = w.shape
    y = lax.conv_general_dilated(
        x, w[:, None, :].astype(x.dtype), window_strides=(1,), padding=[(k - 1, 0)],
        dimension_numbers=("NWC", "WIO", "NWC"), feature_group_count=c)
    return y + b


def conformer_conv_module(h, w_in, b_in, dw, dw_b, ln_g, ln_b, w_out, b_out):
    u = h @ w_in + b_in
    u = u[..., :D_MODEL] * jax.nn.sigmoid(u[..., D_MODEL:])
    u = causal_dwconv(u, dw, dw_b)
    u = layer_norm(u, ln_g, ln_b)
    u = jax.nn.silu(u)
    return u @ w_out + b_out


def conv_ffn(h, w_in, dw, dw_b, w_out):
    u = causal_dwconv(h @ w_in, dw, dw_b)
    a, g = u[..., :D_FF], u[..., D_FF:]
    return (jax.nn.silu(g) * a) @ w_out


def dilated_branch(q, k, v, window, dil):
    b, s, h, dh = q.shape
    steps = window // dil
    assert steps <= BLOCK and s % dil == 0
    L = s // dil
    nblk = -(-L // BLOCK)
    lp = nblk * BLOCK

    def to_sub(t):
        t = t.reshape(b, L, dil, h, dh).transpose(0, 2, 3, 1, 4)
        return jnp.pad(t, ((0, 0), (0, 0), (0, 0), (0, lp - L), (0, 0)))

    qs, ks, vs = to_sub(q), to_sub(k), to_sub(v)
    q_blk = qs.reshape(b, dil, h, nblk, BLOCK, dh)

    def band(t):
        tp = jnp.pad(t, ((0, 0), (0, 0), (0, 0), (BLOCK, 0), (0, 0)))
        prev = tp[:, :, :, :lp].reshape(b, dil, h, nblk, BLOCK, dh)
        cur = tp[:, :, :, BLOCK:].reshape(b, dil, h, nblk, BLOCK, dh)
        return jnp.concatenate([prev, cur], axis=-2)

    k_band, v_band = band(ks), band(vs)
    scores = jnp.einsum("brhnqc,brhnkc->brhnqk", q_blk, k_band).astype(jnp.float32)
    scores = scores * (1.0 / math.sqrt(dh))

    blk_i = jnp.arange(nblk)[:, None, None]
    qi = jnp.arange(BLOCK)[None, :, None]
    kk = jnp.arange(2 * BLOCK)[None, None, :]
    dist = qi + BLOCK - kk
    key_pos = blk_i * BLOCK + kk - BLOCK
    valid = (dist >= 0) & (dist <= steps) & (key_pos >= 0)
    scores = jnp.where(valid, scores, NEG_INF)

    m = jnp.max(scores, axis=-1, keepdims=True)
    p = jnp.exp(scores - m)
    den = jnp.sum(p, axis=-1, keepdims=True)
    out = jnp.einsum("brhnqk,brhnkc->brhnqc", p.astype(v.dtype), v_band)
    out = out / den.astype(out.dtype)
    lse = (m + jnp.log(den))[..., 0]

    out = out.reshape(b, dil, h, lp, dh)[:, :, :, :L]
    out = out.transpose(0, 3, 1, 2, 4).reshape(b, s, h, dh)
    lse = lse.reshape(b, dil, h, lp)[:, :, :, :L]
    lse = lse.transpose(0, 3, 1, 2).reshape(b, s, h)
    return out, lse


def dilated_mixture_attention(h, w_q, w_o, k, v):
    b, s, _ = h.shape
    q = (h @ w_q).reshape(b, s, N_GROUPS, N_HEADS, HEAD_DIM)
    outs, lses = [], []
    for g in range(N_GROUPS):
        o, l = dilated_branch(q[:, :, g], k[:, :, g], v[:, :, g], WINDOWS[g], DILATIONS[g])
        outs.append(o)
        lses.append(l)
    outs = jnp.stack(outs, axis=0)
    wts = jax.nn.softmax(jnp.stack(lses, axis=0), axis=0)
    merged = jnp.sum(wts[..., None].astype(outs.dtype) * outs, axis=0)
    return merged.reshape(b, s, N_HEADS * HEAD_DIM) @ w_o


def _fwd_setup_inputs(seed: int = 0) -> dict:
    key = jax.random.key(seed)
    ks = iter(jax.random.split(key, 32))

    def nrm(shape, fan_in):
        return jax.random.normal(next(ks), shape, jnp.float32) * (fan_in ** -0.5)

    def gain(shape):
        return 1.0 + 0.02 * jax.random.normal(next(ks), shape, jnp.float32)

    def bias(shape):
        return 0.01 * jax.random.normal(next(ks), shape, jnp.float32)

    d = D_MODEL
    return {
        "x": jax.random.normal(next(ks), (BATCH, SEQ, d), jnp.float32),
        "mix_pre_g": gain((DEPTH, d)),
        "mix_post_g": gain((DEPTH, d)),
        "ffn_pre_g": gain((DEPTH, d)),
        "ffn_post_g": gain((DEPTH, d)),
        "cm_w_in": nrm((N_A, d, 2 * d), d),
        "cm_b_in": bias((N_A, 2 * d)),
        "cm_dw": nrm((N_A, CONV_WIDTH, d), CONV_WIDTH),
        "cm_dw_b": bias((N_A, d)),
        "cm_ln_g": gain((N_A, d)),
        "cm_ln_b": bias((N_A, d)),
        "cm_w_out": nrm((N_A, d, d), d),
        "cm_b_out": bias((N_A, d)),
        "kv_norm_g": gain((d,)),
        "w_kv": nrm((d, 2 * Q_WIDTH), d),
        "w_q": nrm((N_B, d, Q_WIDTH), d),
        "w_o": nrm((N_B, N_HEADS * HEAD_DIM, d), N_HEADS * HEAD_DIM),
        "ffn_w_in": nrm((DEPTH, d, 2 * D_FF), d),
        "ffn_dw": nrm((DEPTH, FFN_CONV_WIDTH, 2 * D_FF), FFN_CONV_WIDTH),
        "ffn_dw_b": bias((DEPTH, 2 * D_FF)),
        "ffn_w_out": nrm((DEPTH, D_FF, d), D_FF),
    }


def _fwd_reference(x, mix_pre_g, mix_post_g, ffn_pre_g, ffn_post_g,
              cm_w_in, cm_b_in, cm_dw, cm_dw_b, cm_ln_g, cm_ln_b, cm_w_out, cm_b_out,
              kv_norm_g, w_kv, w_q, w_o,
              ffn_w_in, ffn_dw, ffn_dw_b, ffn_w_out):
    b, s, _ = x.shape
    k_sh = v_sh = None
    for i in range(DEPTH):
        h = rms_norm(x, mix_pre_g[i])
        if i < N_A:
            y = conformer_conv_module(h, cm_w_in[i], cm_b_in[i], cm_dw[i], cm_dw_b[i],
                                      cm_ln_g[i], cm_ln_b[i], cm_w_out[i], cm_b_out[i])
        else:
            j = i - N_A
            y = dilated_mixture_attention(h, w_q[j], w_o[j], k_sh, v_sh)
        x = x + rms_norm(y, mix_post_g[i])
        h = rms_norm(x, ffn_pre_g[i])
        y = conv_ffn(h, ffn_w_in[i], ffn_dw[i], ffn_dw_b[i], ffn_w_out[i])
        x = x + rms_norm(y, ffn_post_g[i])
        if i == N_A - 1:
            kv = (rms_norm(x, kv_norm_g) @ w_kv).reshape(b, s, 2, N_GROUPS, N_HEADS, HEAD_DIM)
            k_sh, v_sh = kv[:, :, 0], kv[:, :, 1]
    return x


import jax as _jax
import jax.numpy as _jnp

TWIN_FORMAT = 'train_step'
FWD_PARAMS = ['x', 'mix_pre_g', 'mix_post_g', 'ffn_pre_g', 'ffn_post_g', 'cm_w_in', 'cm_b_in', 'cm_dw', 'cm_dw_b', 'cm_ln_g', 'cm_ln_b', 'cm_w_out', 'cm_b_out', 'kv_norm_g', 'w_kv', 'w_q', 'w_o', 'ffn_w_in', 'ffn_dw', 'ffn_dw_b', 'ffn_w_out']
TWIN_WEIGHTS = ['mix_pre_g', 'mix_post_g', 'ffn_pre_g', 'ffn_post_g', 'cm_w_in', 'cm_b_in', 'cm_dw', 'cm_dw_b', 'cm_ln_g', 'cm_ln_b', 'cm_w_out', 'cm_b_out', 'kv_norm_g', 'w_kv', 'w_q', 'w_o', 'ffn_w_in', 'ffn_dw', 'ffn_dw_b', 'ffn_w_out']
TWIN_DIFF_INPUT = 'x'
TWIN_INPUTS = ['x', 'mix_pre_g', 'mix_post_g', 'ffn_pre_g', 'ffn_post_g', 'cm_w_in', 'cm_b_in', 'cm_dw', 'cm_dw_b', 'cm_ln_g', 'cm_ln_b', 'cm_w_out', 'cm_b_out', 'kv_norm_g', 'w_kv', 'w_q', 'w_o', 'ffn_w_in', 'ffn_dw', 'ffn_dw_b', 'ffn_w_out', 'loss_target', 'm_mix_pre_g', 'm_mix_post_g', 'm_ffn_pre_g', 'm_ffn_post_g', 'm_cm_w_in', 'm_cm_b_in', 'm_cm_dw', 'm_cm_dw_b', 'm_cm_ln_g', 'm_cm_ln_b', 'm_cm_w_out', 'm_cm_b_out', 'm_kv_norm_g', 'm_w_kv', 'm_w_q', 'm_w_o', 'm_ffn_w_in', 'm_ffn_dw', 'm_ffn_dw_b', 'm_ffn_w_out', 'v_mix_pre_g', 'v_mix_post_g', 'v_ffn_pre_g', 'v_ffn_post_g', 'v_cm_w_in', 'v_cm_b_in', 'v_cm_dw', 'v_cm_dw_b', 'v_cm_ln_g', 'v_cm_ln_b', 'v_cm_w_out', 'v_cm_b_out', 'v_kv_norm_g', 'v_w_kv', 'v_w_q', 'v_w_o', 'v_ffn_w_in', 'v_ffn_dw', 'v_ffn_dw_b', 'v_ffn_w_out']
TWIN_OUTPUTS = ['loss', 'grad_x', 'grad_mix_pre_g', 'grad_mix_post_g', 'grad_ffn_pre_g', 'grad_ffn_post_g', 'grad_cm_w_in', 'grad_cm_b_in', 'grad_cm_dw', 'grad_cm_dw_b', 'grad_cm_ln_g', 'grad_cm_ln_b', 'grad_cm_w_out', 'grad_cm_b_out', 'grad_kv_norm_g', 'grad_w_kv', 'grad_w_q', 'grad_w_o', 'grad_ffn_w_in', 'grad_ffn_dw', 'grad_ffn_dw_b', 'grad_ffn_w_out', 'delta_mix_pre_g', 'delta_mix_post_g', 'delta_ffn_pre_g', 'delta_ffn_post_g', 'delta_cm_w_in', 'delta_cm_b_in', 'delta_cm_dw', 'delta_cm_dw_b', 'delta_cm_ln_g', 'delta_cm_ln_b', 'delta_cm_w_out', 'delta_cm_b_out', 'delta_kv_norm_g', 'delta_w_kv', 'delta_w_q', 'delta_w_o', 'delta_ffn_w_in', 'delta_ffn_dw', 'delta_ffn_dw_b', 'delta_ffn_w_out', 'new_m_mix_pre_g', 'new_m_mix_post_g', 'new_m_ffn_pre_g', 'new_m_ffn_post_g', 'new_m_cm_w_in', 'new_m_cm_b_in', 'new_m_cm_dw', 'new_m_cm_dw_b', 'new_m_cm_ln_g', 'new_m_cm_ln_b', 'new_m_cm_w_out', 'new_m_cm_b_out', 'new_m_kv_norm_g', 'new_m_w_kv', 'new_m_w_q', 'new_m_w_o', 'new_m_ffn_w_in', 'new_m_ffn_dw', 'new_m_ffn_dw_b', 'new_m_ffn_w_out', 'new_v_mix_pre_g', 'new_v_mix_post_g', 'new_v_ffn_pre_g', 'new_v_ffn_post_g', 'new_v_cm_w_in', 'new_v_cm_b_in', 'new_v_cm_dw', 'new_v_cm_dw_b', 'new_v_cm_ln_g', 'new_v_cm_ln_b', 'new_v_cm_w_out', 'new_v_cm_b_out', 'new_v_kv_norm_g', 'new_v_w_kv', 'new_v_w_q', 'new_v_w_o', 'new_v_ffn_w_in', 'new_v_ffn_dw', 'new_v_ffn_dw_b', 'new_v_ffn_w_out']
TWIN_LEAF_KINDS = {'loss': 'loss', 'grad_x': 'grad_x', 'grad_mix_pre_g': 'grad_w', 'grad_mix_post_g': 'grad_w', 'grad_ffn_pre_g': 'grad_w', 'grad_ffn_post_g': 'grad_w', 'grad_cm_w_in': 'grad_w', 'grad_cm_b_in': 'grad_w', 'grad_cm_dw': 'grad_w', 'grad_cm_dw_b': 'grad_w', 'grad_cm_ln_g': 'grad_w', 'grad_cm_ln_b': 'grad_w', 'grad_cm_w_out': 'grad_w', 'grad_cm_b_out': 'grad_w', 'grad_kv_norm_g': 'grad_w', 'grad_w_kv': 'grad_w', 'grad_w_q': 'grad_w', 'grad_w_o': 'grad_w', 'grad_ffn_w_in': 'grad_w', 'grad_ffn_dw': 'grad_w', 'grad_ffn_dw_b': 'grad_w', 'grad_ffn_w_out': 'grad_w', 'delta_mix_pre_g': 'delta_w', 'delta_mix_post_g': 'delta_w', 'delta_ffn_pre_g': 'delta_w', 'delta_ffn_post_g': 'delta_w', 'delta_cm_w_in': 'delta_w', 'delta_cm_b_in': 'delta_w', 'delta_cm_dw': 'delta_w', 'delta_cm_dw_b': 'delta_w', 'delta_cm_ln_g': 'delta_w', 'delta_cm_ln_b': 'delta_w', 'delta_cm_w_out': 'delta_w', 'delta_cm_b_out': 'delta_w', 'delta_kv_norm_g': 'delta_w', 'delta_w_kv': 'delta_w', 'delta_w_q': 'delta_w', 'delta_w_o': 'delta_w', 'delta_ffn_w_in': 'delta_w', 'delta_ffn_dw': 'delta_w', 'delta_ffn_dw_b': 'delta_w', 'delta_ffn_w_out': 'delta_w', 'new_m_mix_pre_g': 'new_m', 'new_m_mix_post_g': 'new_m', 'new_m_ffn_pre_g': 'new_m', 'new_m_ffn_post_g': 'new_m', 'new_m_cm_w_in': 'new_m', 'new_m_cm_b_in': 'new_m', 'new_m_cm_dw': 'new_m', 'new_m_cm_dw_b': 'new_m', 'new_m_cm_ln_g': 'new_m', 'new_m_cm_ln_b': 'new_m', 'new_m_cm_w_out': 'new_m', 'new_m_cm_b_out': 'new_m', 'new_m_kv_norm_g': 'new_m', 'new_m_w_kv': 'new_m', 'new_m_w_q': 'new_m', 'new_m_w_o': 'new_m', 'new_m_ffn_w_in': 'new_m', 'new_m_ffn_dw': 'new_m', 'new_m_ffn_dw_b': 'new_m', 'new_m_ffn_w_out': 'new_m', 'new_v_mix_pre_g': 'new_v', 'new_v_mix_post_g': 'new_v', 'new_v_ffn_pre_g': 'new_v', 'new_v_ffn_post_g': 'new_v', 'new_v_cm_w_in': 'new_v', 'new_v_cm_b_in': 'new_v', 'new_v_cm_dw': 'new_v', 'new_v_cm_dw_b': 'new_v', 'new_v_cm_ln_g': 'new_v', 'new_v_cm_ln_b': 'new_v', 'new_v_cm_w_out': 'new_v', 'new_v_cm_b_out': 'new_v', 'new_v_kv_norm_g': 'new_v', 'new_v_w_kv': 'new_v', 'new_v_w_q': 'new_v', 'new_v_w_o': 'new_v', 'new_v_ffn_w_in': 'new_v', 'new_v_ffn_dw': 'new_v', 'new_v_ffn_dw_b': 'new_v', 'new_v_ffn_w_out': 'new_v'}


def _forward(args):
    return _fwd_reference(*[args[k] for k in FWD_PARAMS])


def _output_shape():
    out = _jax.eval_shape(lambda: _forward(_fwd_setup_inputs(0)))
    return out.shape, out.dtype

N_MICROBATCH = 1
ADAM_LR = 0.001
ADAM_B1 = 0.9
ADAM_B2 = 0.999
ADAM_EPS = 1e-08
ADAM_WD = 0.01
ADAM_STEP = 10
PER_EXAMPLE_BATCH_AXIS = {'x': 0, 'loss_target': 0}
SHARED_INPUTS = []
_WEIGHT_DTYPES = {'mix_pre_g': _jnp.float32, 'mix_post_g': _jnp.float32, 'ffn_pre_g': _jnp.float32, 'ffn_post_g': _jnp.float32, 'cm_w_in': _jnp.float32, 'cm_b_in': _jnp.float32, 'cm_dw': _jnp.float32, 'cm_dw_b': _jnp.float32, 'cm_ln_g': _jnp.float32, 'cm_ln_b': _jnp.float32, 'cm_w_out': _jnp.float32, 'cm_b_out': _jnp.float32, 'kv_norm_g': _jnp.float32, 'w_kv': _jnp.float32, 'w_q': _jnp.float32, 'w_o': _jnp.float32, 'ffn_w_in': _jnp.float32, 'ffn_dw': _jnp.float32, 'ffn_dw_b': _jnp.float32, 'ffn_w_out': _jnp.float32}
MOMENT_SCALE = {'mix_pre_g': 6.825322e+00, 'mix_post_g': 4.973338e+01, 'ffn_pre_g': 1.113079e+01, 'ffn_post_g': 3.319250e+01, 'cm_w_in': 6.486503e+00, 'cm_b_in': 4.767903e+01, 'cm_dw': 1.275226e+01, 'cm_dw_b': 1.381499e+02, 'cm_ln_g': 5.318045e+01, 'cm_ln_b': 7.390504e+01, 'cm_w_out': 3.335164e+01, 'cm_b_out': 1.474830e+02, 'kv_norm_g': 3.778318e+01, 'w_kv': 1.610173e+01, 'w_q': 8.147933e-01, 'w_o': 2.762416e+01, 'ffn_w_in': 4.569722e+00, 'ffn_dw': 5.376631e+00, 'ffn_dw_b': 1.555645e+01, 'ffn_w_out': 9.641086e+00}


def _to_microbatches(a, axis):
    t = _jnp.moveaxis(a, axis, 0)
    t = t.reshape((N_MICROBATCH, t.shape[0] // N_MICROBATCH) + t.shape[1:])
    return _jnp.moveaxis(t, 1, axis + 1)


def setup_inputs(seed: int = 0) -> dict:
    inp = _fwd_setup_inputs(seed)
    key = _jax.random.fold_in(_jax.random.key(seed), 7919)
    shape, _ = _output_shape()
    out = dict(inp)
    out["loss_target"] = _jax.random.normal(_jax.random.fold_in(key, 0), shape, _jnp.float32)
    for i, name in enumerate(TWIN_WEIGHTS):
        w = inp[name].astype(_jnp.float32)
        if MOMENT_SCALE is None:
            s = _jnp.sqrt(_jnp.mean(_jnp.square(w)) + 1e-30)
        else:
            s = MOMENT_SCALE[name]
        km, kv = _jax.random.split(_jax.random.fold_in(key, i + 1))
        out[name] = w
        out["m_" + name] = s * _jax.random.normal(km, w.shape, _jnp.float32)
        out["v_" + name] = (s * s) * _jax.random.uniform(kv, w.shape, _jnp.float32, 0.5, 1.5)
    if N_MICROBATCH > 1:
        for name, axis in PER_EXAMPLE_BATCH_AXIS.items():
            out[name] = _to_microbatches(out[name], axis)
    return {'x': out['x'], 'mix_pre_g': out['mix_pre_g'], 'mix_post_g': out['mix_post_g'], 'ffn_pre_g': out['ffn_pre_g'], 'ffn_post_g': out['ffn_post_g'], 'cm_w_in': out['cm_w_in'], 'cm_b_in': out['cm_b_in'], 'cm_dw': out['cm_dw'], 'cm_dw_b': out['cm_dw_b'], 'cm_ln_g': out['cm_ln_g'], 'cm_ln_b': out['cm_ln_b'], 'cm_w_out': out['cm_w_out'], 'cm_b_out': out['cm_b_out'], 'kv_norm_g': out['kv_norm_g'], 'w_kv': out['w_kv'], 'w_q': out['w_q'], 'w_o': out['w_o'], 'ffn_w_in': out['ffn_w_in'], 'ffn_dw': out['ffn_dw'], 'ffn_dw_b': out['ffn_dw_b'], 'ffn_w_out': out['ffn_w_out'], 'loss_target': out['loss_target'], 'm_mix_pre_g': out['m_mix_pre_g'], 'm_mix_post_g': out['m_mix_post_g'], 'm_ffn_pre_g': out['m_ffn_pre_g'], 'm_ffn_post_g': out['m_ffn_post_g'], 'm_cm_w_in': out['m_cm_w_in'], 'm_cm_b_in': out['m_cm_b_in'], 'm_cm_dw': out['m_cm_dw'], 'm_cm_dw_b': out['m_cm_dw_b'], 'm_cm_ln_g': out['m_cm_ln_g'], 'm_cm_ln_b': out['m_cm_ln_b'], 'm_cm_w_out': out['m_cm_w_out'], 'm_cm_b_out': out['m_cm_b_out'], 'm_kv_norm_g': out['m_kv_norm_g'], 'm_w_kv': out['m_w_kv'], 'm_w_q': out['m_w_q'], 'm_w_o': out['m_w_o'], 'm_ffn_w_in': out['m_ffn_w_in'], 'm_ffn_dw': out['m_ffn_dw'], 'm_ffn_dw_b': out['m_ffn_dw_b'], 'm_ffn_w_out': out['m_ffn_w_out'], 'v_mix_pre_g': out['v_mix_pre_g'], 'v_mix_post_g': out['v_mix_post_g'], 'v_ffn_pre_g': out['v_ffn_pre_g'], 'v_ffn_post_g': out['v_ffn_post_g'], 'v_cm_w_in': out['v_cm_w_in'], 'v_cm_b_in': out['v_cm_b_in'], 'v_cm_dw': out['v_cm_dw'], 'v_cm_dw_b': out['v_cm_dw_b'], 'v_cm_ln_g': out['v_cm_ln_g'], 'v_cm_ln_b': out['v_cm_ln_b'], 'v_cm_w_out': out['v_cm_w_out'], 'v_cm_b_out': out['v_cm_b_out'], 'v_kv_norm_g': out['v_kv_norm_g'], 'v_w_kv': out['v_w_kv'], 'v_w_q': out['v_w_q'], 'v_w_o': out['v_w_o'], 'v_ffn_w_in': out['v_ffn_w_in'], 'v_ffn_dw': out['v_ffn_dw'], 'v_ffn_dw_b': out['v_ffn_dw_b'], 'v_ffn_w_out': out['v_ffn_w_out']}


def _loss(weights, diff, rest, loss_target):
    with _jax.named_scope("forward"):
        args = {**rest, TWIN_DIFF_INPUT: diff, **{k: w.astype(_WEIGHT_DTYPES[k]) for k, w in weights.items()}}
        y = _forward(args)
    with _jax.named_scope("loss_head"):
        err = _jnp.square(y.astype(_jnp.float32) - loss_target)
        return 0.5 * _jnp.sum(_jnp.mean(err, axis=-1)) if err.ndim else 0.5 * err


def _adamw(w, g, m, v):
    m = ADAM_B1 * m + (1.0 - ADAM_B1) * g
    v = ADAM_B2 * v + (1.0 - ADAM_B2) * _jnp.square(g)
    m_hat = m / (1.0 - ADAM_B1 ** ADAM_STEP)
    v_hat = v / (1.0 - ADAM_B2 ** ADAM_STEP)
    delta = -ADAM_LR * (m_hat / (_jnp.sqrt(v_hat) + ADAM_EPS) + ADAM_WD * w)
    return delta, m, v


def reference(x, mix_pre_g, mix_post_g, ffn_pre_g, ffn_post_g, cm_w_in, cm_b_in, cm_dw, cm_dw_b, cm_ln_g, cm_ln_b, cm_w_out, cm_b_out, kv_norm_g, w_kv, w_q, w_o, ffn_w_in, ffn_dw, ffn_dw_b, ffn_w_out, loss_target, m_mix_pre_g, m_mix_post_g, m_ffn_pre_g, m_ffn_post_g, m_cm_w_in, m_cm_b_in, m_cm_dw, m_cm_dw_b, m_cm_ln_g, m_cm_ln_b, m_cm_w_out, m_cm_b_out, m_kv_norm_g, m_w_kv, m_w_q, m_w_o, m_ffn_w_in, m_ffn_dw, m_ffn_dw_b, m_ffn_w_out, v_mix_pre_g, v_mix_post_g, v_ffn_pre_g, v_ffn_post_g, v_cm_w_in, v_cm_b_in, v_cm_dw, v_cm_dw_b, v_cm_ln_g, v_cm_ln_b, v_cm_w_out, v_cm_b_out, v_kv_norm_g, v_w_kv, v_w_q, v_w_o, v_ffn_w_in, v_ffn_dw, v_ffn_dw_b, v_ffn_w_out):
    given = dict(x=x, mix_pre_g=mix_pre_g, mix_post_g=mix_post_g, ffn_pre_g=ffn_pre_g, ffn_post_g=ffn_post_g, cm_w_in=cm_w_in, cm_b_in=cm_b_in, cm_dw=cm_dw, cm_dw_b=cm_dw_b, cm_ln_g=cm_ln_g, cm_ln_b=cm_ln_b, cm_w_out=cm_w_out, cm_b_out=cm_b_out, kv_norm_g=kv_norm_g, w_kv=w_kv, w_q=w_q, w_o=w_o, ffn_w_in=ffn_w_in, ffn_dw=ffn_dw, ffn_dw_b=ffn_dw_b, ffn_w_out=ffn_w_out, loss_target=loss_target, m_mix_pre_g=m_mix_pre_g, m_mix_post_g=m_mix_post_g, m_ffn_pre_g=m_ffn_pre_g, m_ffn_post_g=m_ffn_post_g, m_cm_w_in=m_cm_w_in, m_cm_b_in=m_cm_b_in, m_cm_dw=m_cm_dw, m_cm_dw_b=m_cm_dw_b, m_cm_ln_g=m_cm_ln_g, m_cm_ln_b=m_cm_ln_b, m_cm_w_out=m_cm_w_out, m_cm_b_out=m_cm_b_out, m_kv_norm_g=m_kv_norm_g, m_w_kv=m_w_kv, m_w_q=m_w_q, m_w_o=m_w_o, m_ffn_w_in=m_ffn_w_in, m_ffn_dw=m_ffn_dw, m_ffn_dw_b=m_ffn_dw_b, m_ffn_w_out=m_ffn_w_out, v_mix_pre_g=v_mix_pre_g, v_mix_post_g=v_mix_post_g, v_ffn_pre_g=v_ffn_pre_g, v_ffn_post_g=v_ffn_post_g, v_cm_w_in=v_cm_w_in, v_cm_b_in=v_cm_b_in, v_cm_dw=v_cm_dw, v_cm_dw_b=v_cm_dw_b, v_cm_ln_g=v_cm_ln_g, v_cm_ln_b=v_cm_ln_b, v_cm_w_out=v_cm_w_out, v_cm_b_out=v_cm_b_out, v_kv_norm_g=v_kv_norm_g, v_w_kv=v_w_kv, v_w_q=v_w_q, v_w_o=v_w_o, v_ffn_w_in=v_ffn_w_in, v_ffn_dw=v_ffn_dw, v_ffn_dw_b=v_ffn_dw_b, v_ffn_w_out=v_ffn_w_out)
    weights = {n: given[n] for n in TWIN_WEIGHTS}
    shared = {n: given[n] for n in SHARED_INPUTS}
    per_example = {n: given[n] for n in ['x']}
    grad_fn = _jax.value_and_grad(_loss, argnums=(0, 1))

    def one_microbatch(ex, loss_target):
        ex = dict(ex)
        diff = ex.pop(TWIN_DIFF_INPUT)
        return grad_fn(weights, diff, {**shared, **ex}, loss_target)

    if N_MICROBATCH == 1:
        loss, (grad_w, grad_x) = one_microbatch(per_example, given["loss_target"])
    else:
        def body(carry, xs):
            loss_sum, grad_sum = carry
            l_k, (gw_k, gx_k) = one_microbatch(xs[0], xs[1])
            with _jax.named_scope("update"):
                return (loss_sum + l_k, _jax.tree.map(_jnp.add, grad_sum, gw_k)), gx_k

        init = (_jnp.zeros((), _jnp.float32), _jax.tree.map(_jnp.zeros_like, weights))
        (loss, grad_w), grad_x = _jax.lax.scan(body, init, (per_example, given["loss_target"]))
    with _jax.named_scope("update"):
        delta_w, new_m, new_v = {}, {}, {}
        for n in TWIN_WEIGHTS:
            delta_w[n], new_m[n], new_v[n] = _adamw(weights[n], grad_w[n], given["m_" + n], given["v_" + n])
    return (loss, grad_x, *[grad_w[n] for n in TWIN_WEIGHTS], *[delta_w[n] for n in TWIN_WEIGHTS],
            *[new_m[n] for n in TWIN_WEIGHTS], *[new_v[n] for n in TWIN_WEIGHTS])
```

```python
import functools
import math

import jax
import jax.numpy as jnp
from jax import lax
from jax.experimental import pallas as pl
from jax.experimental.pallas import tpu as pltpu

F32 = jnp.float32
BF16 = jnp.bfloat16
EPS = 1e-6
NEG_INF = -1e30
N_A = 2
DEPTH = 4
N_GROUPS = 3
DILATIONS = (1, 4, 16)
HEAD_DIM = 128
BLK = 128
LANES = 128
N_CHIPS = 4
N_DEV = 8
VMEM_LIMIT_V7X = 56 * 1024 * 1024

ADAM_LR = 0.001
ADAM_B1 = 0.9
ADAM_B2 = 0.999
ADAM_EPS = 1e-08
ADAM_WD = 0.01
ADAM_STEP = 10

MESH = pl.DeviceIdType.MESH


def _cp(*sem):
    return pltpu.CompilerParams(dimension_semantics=sem if sem else None, vmem_limit_bytes=VMEM_LIMIT_V7X)


def _dot(a, b):
    return jnp.dot(a, b, preferred_element_type=F32)


def _dot_nt(a, b):
    return lax.dot_general(a, b, (((1,), (1,)), ((), ())), preferred_element_type=F32)


def _dot_tn(a, b):
    return lax.dot_general(a, b, (((0,), (0,)), ((), ())), preferred_element_type=F32)


def _sigmoid(x):
    return 1.0 / (1.0 + jnp.exp(-x))


def _row_tile(n, want):
    if n <= want:
        return n
    for t in range(want - want % 8, 7, -8):
        if n % t == 0:
            return t
    raise ValueError(f"no row tile for {n} rows")


def mm_nn(a, w, nsh, stride, layer, bias=None, out_dtype=F32, name="mm_nn", tm=512):
    m, k = a.shape
    _, k2, ns = w.shape
    assert k == k2
    tm = _row_tile(m, tm)
    has_bias = bias is not None

    def body(*refs):
        if has_bias:
            a_ref, w_ref, b_ref, o_ref = refs
        else:
            a_ref, w_ref, o_ref = refs
        acc = _dot(a_ref[...].astype(BF16), w_ref[...])
        if has_bias:
            acc = acc + b_ref[...]
        o_ref[...] = acc.astype(out_dtype)

    in_specs = [
        pl.BlockSpec((tm, k), lambda j, i: (i, 0)),
        pl.BlockSpec((None, k, ns), lambda j, i: (j * stride + layer, 0, 0)),
    ]
    args = [a, w]
    if has_bias:
        in_specs.append(pl.BlockSpec((1, ns), lambda j, i: (0, j)))
        args.append(bias)
    return pl.pallas_call(
        body,
        name=name,
        grid=(nsh, m // tm),
        in_specs=in_specs,
        out_specs=pl.BlockSpec((tm, ns), lambda j, i: (i, j)),
        out_shape=jax.ShapeDtypeStruct((m, nsh * ns), out_dtype),
        compiler_params=_cp("parallel", "parallel"),
    )(*args)


def mm_nt(dy, w, nsh, stride, layer, out_dtype=F32, name="mm_nt", tm=512):
    m, n = dy.shape
    _, k, ns = w.shape
    assert n == nsh * ns
    tm = _row_tile(m, tm)

    def body(dy_ref, w_ref, o_ref, acc_ref):
        j = pl.program_id(1)
        part = _dot_nt(dy_ref[...].astype(BF16), w_ref[...])

        @pl.when(j == 0)
        def _():
            acc_ref[...] = part

        @pl.when(j > 0)
        def _():
            acc_ref[...] += part

        @pl.when(j == nsh - 1)
        def _():
            o_ref[...] = acc_ref[...].astype(out_dtype)

    return pl.pallas_call(
        body,
        name=name,
        grid=(m // tm, nsh),
        in_specs=[
            pl.BlockSpec((tm, ns), lambda i, j: (i, j)),
            pl.BlockSpec((None, k, ns), lambda i, j: (j * stride + layer, 0, 0)),
        ],
        out_specs=pl.BlockSpec((tm, k), lambda i, j: (i, 0)),
        out_shape=jax.ShapeDtypeStruct((m, k), out_dtype),
        scratch_shapes=[pltpu.VMEM((tm, k), F32)],
        compiler_params=_cp("parallel", "arbitrary"),
    )(dy, w)


def mm_tn(a, dy, nsh, name="mm_tn", tm=512):
    m, k = a.shape
    _, n = dy.shape
    ns = n // nsh
    tm = _row_tile(m, tm)
    nt = m // tm

    def body(a_ref, dy_ref, o_ref, acc_ref):
        i = pl.program_id(1)
        part = _dot_tn(a_ref[...].astype(BF16), dy_ref[...].astype(BF16))

        @pl.when(i == 0)
        def _():
            acc_ref[...] = part

        @pl.when(i > 0)
        def _():
            acc_ref[...] += part

        @pl.when(i == nt - 1)
        def _():
            o_ref[...] = acc_ref[...].astype(BF16)

    return pl.pallas_call(
        body,
        name=name,
        grid=(nsh, nt),
        in_specs=[
            pl.BlockSpec((tm, k), lambda j, i: (i, 0)),
            pl.BlockSpec((tm, ns), lambda j, i: (i, j)),
        ],
        out_specs=pl.BlockSpec((None, k, ns), lambda j, i: (j, 0, 0)),
        out_shape=jax.ShapeDtypeStruct((nsh, k, ns), BF16),
        scratch_shapes=[pltpu.VMEM((k, ns), F32)],
        compiler_params=_cp("parallel", "arbitrary"),
    )(a, dy)


def resid_norm_fwd(x, y, g_post, next_gains, name, tm=256):
    t, d = x.shape
    tm = _row_tile(t, tm)
    has_y = y is not None
    n_next = len(next_gains)

    def body(*refs):
        x_ref = refs[0]
        pos = 1
        if has_y:
            y_ref, gp_ref = refs[1], refs[2]
            pos = 3
        gn_refs = refs[pos:pos + n_next]
        outs = refs[pos + n_next:]
        xv = x_ref[...]
        o = 0
        if has_y:
            yv = y_ref[...]
            r = lax.rsqrt(jnp.mean(yv * yv, axis=-1, keepdims=True) + EPS)
            xv = xv + (yv * r) * gp_ref[...]
            outs[0][...] = xv
            o = 1
        if n_next:
            xn = xv * lax.rsqrt(jnp.mean(xv * xv, axis=-1, keepdims=True) + EPS)
            for k in range(n_next):
                outs[o + k][...] = (xn * gn_refs[k][...]).astype(BF16)

    row = pl.BlockSpec((tm, d), lambda i: (i, 0))
    vec = pl.BlockSpec((1, d), lambda i: (0, 0))
    args, in_specs = [x], [row]
    if has_y:
        args += [y, g_post]
        in_specs += [row, vec]
    args += list(next_gains)
    in_specs += [vec] * n_next
    out_shape, out_specs = [], []
    if has_y:
        out_shape.append(jax.ShapeDtypeStruct((t, d), F32))
        out_specs.append(row)
    for _ in range(n_next):
        out_shape.append(jax.ShapeDtypeStruct((t, d), BF16))
        out_specs.append(row)
    return pl.pallas_call(
        body, name=name, grid=(t // tm,), in_specs=in_specs, out_specs=out_specs, out_shape=out_shape,
        compiler_params=_cp("parallel"),
    )(*args)


def norm_bwd(x, g, dy, add=None, out_dtype=F32, name="norm_bwd", tm=256):
    t, d = x.shape
    tm = _row_tile(t, tm)
    has_add = add is not None

    def body(*refs):
        if has_add:
            x_ref, g_ref, dy_ref, add_ref, dx_ref, dg_ref, cs_ref = refs
        else:
            x_ref, g_ref, dy_ref, dx_ref, dg_ref, cs_ref = refs
        i = pl.program_id(0)
        xv = x_ref[...]
        dyv = dy_ref[...].astype(F32)
        r = lax.rsqrt(jnp.mean(xv * xv, axis=-1, keepdims=True) + EPS)
        gd = dyv * g_ref[...]
        dx = r * gd - xv * ((r * r * r) * jnp.mean(xv * gd, axis=-1, keepdims=True))
        if has_add:
            dx = dx + add_ref[...]
        dx_ref[...] = dx.astype(out_dtype)
        dg = jnp.sum(dyv * (xv * r), axis=0, keepdims=True)
        cs = jnp.sum(dx, axis=0, keepdims=True)

        @pl.when(i == 0)
        def _():
            dg_ref[...] = dg
            cs_ref[...] = cs

        @pl.when(i > 0)
        def _():
            dg_ref[...] += dg
            cs_ref[...] += cs

    row = pl.BlockSpec((tm, d), lambda i: (i, 0))
    vec = pl.BlockSpec((1, d), lambda i: (0, 0))
    args, in_specs = [x, g, dy], [row, vec, row]
    if has_add:
        args.append(add)
        in_specs.append(row)
    return pl.pallas_call(
        body, name=name, grid=(t // tm,), in_specs=in_specs,
        out_specs=[row, vec, vec],
        out_shape=[jax.ShapeDtypeStruct((t, d), out_dtype), jax.ShapeDtypeStruct((1, d), F32),
                   jax.ShapeDtypeStruct((1, d), F32)],
        compiler_params=_cp("arbitrary"),
    )(*args)


def loss_fwd_bwd(x, target, name="loss", tm=256):
    t, d = x.shape
    tm = _row_tile(t, tm)

    def body(x_ref, t_ref, dx_ref, l_ref):
        i = pl.program_id(0)
        err = x_ref[...] - t_ref[...]
        dx_ref[...] = err * (1.0 / d)
        part = 0.5 * jnp.sum(jnp.mean(err * err, axis=-1, keepdims=True), axis=0, keepdims=True)
        part = jnp.broadcast_to(part, l_ref.shape)

        @pl.when(i == 0)
        def _():
            l_ref[...] = part

        @pl.when(i > 0)
        def _():
            l_ref[...] += part

    row = pl.BlockSpec((tm, d), lambda i: (i, 0))
    return pl.pallas_call(
        body, name=name, grid=(t // tm,), in_specs=[row, row],
        out_specs=[row, pl.BlockSpec((8, LANES), lambda i: (0, 0))],
        out_shape=[jax.ShapeDtypeStruct((t, d), F32), jax.ShapeDtypeStruct((8, LANES), F32)],
        compiler_params=_cp("arbitrary"),
    )(x, target)


CONV_HALO = 32
CONV_CHUNK = 128


def glu_conv_fwd(z, dw, dwb, name, tc=128):
    b, s, c2 = z.shape
    c = c2 // 2
    kw = dw.shape[0]
    tc = min(tc, c)
    nc = c // tc
    ch = min(CONV_CHUNK, s)
    halo = CONV_HALO
    assert kw - 1 <= halo and s % ch == 0

    def body(a_ref, g_ref, w_ref, b_ref, o_ref, pad_ref):
        pad_ref[0:halo, :] = jnp.zeros((halo, tc), F32)

        def fill(ci, carry):
            r0 = pl.multiple_of(ci * ch, ch)
            pad_ref[pl.ds(halo + r0, ch), :] = a_ref[pl.ds(r0, ch), :] * _sigmoid(g_ref[pl.ds(r0, ch), :])
            return carry

        lax.fori_loop(0, s // ch, fill, 0)

        def chunk(ci, carry):
            r0 = pl.multiple_of(ci * ch, ch)
            win = pad_ref[pl.ds(r0, ch + halo), :]
            acc = b_ref[...] + w_ref[kw - 1:kw, :] * win[halo:, :]
            for k in range(kw - 2, -1, -1):
                win = pltpu.roll(win, 1, 0)
                acc = acc + w_ref[k:k + 1, :] * win[halo:, :]
            o_ref[pl.ds(r0, ch), :] = acc
            return carry

        lax.fori_loop(0, s // ch, chunk, 0)

    return pl.pallas_call(
        body, name=name, grid=(b, nc),
        in_specs=[
            pl.BlockSpec((None, s, tc), lambda bi, i: (bi, 0, i)),
            pl.BlockSpec((None, s, tc), lambda bi, i: (bi, 0, i + nc)),
            pl.BlockSpec((kw, tc), lambda bi, i: (0, i)),
            pl.BlockSpec((1, tc), lambda bi, i: (0, i)),
        ],
        out_specs=pl.BlockSpec((None, s, tc), lambda bi, i: (bi, 0, i)),
        out_shape=jax.ShapeDtypeStruct((b, s, c), F32),
        scratch_shapes=[pltpu.VMEM((s + halo, tc), F32)],
        compiler_params=_cp("parallel", "parallel"),
    )(z, z, dw, dwb)


def glu_conv_bwd(z, dw, du2, name, tc=128):
    b, s, c2 = z.shape
    c = c2 // 2
    kw = dw.shape[0]
    tc = min(tc, c)
    nc = c // tc
    ch = min(CONV_CHUNK, s)
    halo = CONV_HALO

    def body(a_ref, g_ref, w_ref, du_ref, dza_ref, dzg_ref, ddw_ref, ddwb_ref, dba_ref, dbg_ref, upad_ref, dpad_ref):
        bi = pl.program_id(1)

        @pl.when(bi == 0)
        def _():
            ddw_ref[...] = jnp.zeros(ddw_ref.shape, F32)
            ddwb_ref[...] = jnp.zeros(ddwb_ref.shape, F32)
            dba_ref[...] = jnp.zeros(dba_ref.shape, F32)
            dbg_ref[...] = jnp.zeros(dbg_ref.shape, F32)

        upad_ref[0:halo, :] = jnp.zeros((halo, tc), F32)
        dpad_ref[s:s + halo, :] = jnp.zeros((halo, tc), F32)

        def fill(ci, carry):
            r0 = pl.multiple_of(ci * ch, ch)
            upad_ref[pl.ds(halo + r0, ch), :] = a_ref[pl.ds(r0, ch), :] * _sigmoid(g_ref[pl.ds(r0, ch), :])
            dpad_ref[pl.ds(r0, ch), :] = du_ref[pl.ds(r0, ch), :]
            return carry

        lax.fori_loop(0, s // ch, fill, 0)

        def chunk(ci, carry):
            r0 = pl.multiple_of(ci * ch, ch)
            du_c = du_ref[pl.ds(r0, ch), :]
            wd = dpad_ref[pl.ds(r0, ch + halo), :]
            wu = upad_ref[pl.ds(r0, ch + halo), :]
            du1 = w_ref[kw - 1:kw, :] * wd[:ch, :]
            ddw_ref[kw - 1] += jnp.sum((du_c * wu[halo:, :]).reshape(ch // 8, 8, tc), axis=0)
            for j in range(1, kw):
                wd = pltpu.roll(wd, ch + halo - 1, 0)
                wu = pltpu.roll(wu, 1, 0)
                du1 = du1 + w_ref[kw - 1 - j:kw - j, :] * wd[:ch, :]
                ddw_ref[kw - 1 - j] += jnp.sum((du_c * wu[halo:, :]).reshape(ch // 8, 8, tc), axis=0)
            av = a_ref[pl.ds(r0, ch), :]
            sg = _sigmoid(g_ref[pl.ds(r0, ch), :])
            dza = du1 * sg
            dzg = du1 * av * (sg * (1.0 - sg))
            dza_ref[pl.ds(r0, ch), :] = dza.astype(BF16)
            dzg_ref[pl.ds(r0, ch), :] = dzg.astype(BF16)
            dba_ref[...] += jnp.sum(dza, axis=0, keepdims=True)
            dbg_ref[...] += jnp.sum(dzg, axis=0, keepdims=True)
            ddwb_ref[...] += jnp.sum(du_c, axis=0, keepdims=True)
            return carry

        lax.fori_loop(0, s // ch, chunk, 0)

    blk = lambda off: pl.BlockSpec((None, s, tc), lambda i, bi: (bi, 0, i + off))
    vec = pl.BlockSpec((1, tc), lambda i, bi: (0, i))
    return pl.pallas_call(
        body, name=name, grid=(nc, b),
        in_specs=[blk(0), blk(nc), pl.BlockSpec((kw, tc), lambda i, bi: (0, i)), blk(0)],
        out_specs=[blk(0), blk(0), pl.BlockSpec((kw, 8, tc), lambda i, bi: (0, 0, i)), vec, vec, vec],
        out_shape=[
            jax.ShapeDtypeStruct((b, s, c), BF16), jax.ShapeDtypeStruct((b, s, c), BF16),
            jax.ShapeDtypeStruct((kw, 8, c), F32), jax.ShapeDtypeStruct((1, c), F32),
            jax.ShapeDtypeStruct((1, c), F32), jax.ShapeDtypeStruct((1, c), F32),
        ],
        scratch_shapes=[pltpu.VMEM((s + halo, tc), F32), pltpu.VMEM((s + halo, tc), F32)],
        compiler_params=_cp("parallel", "arbitrary"),
    )(z, z, dw, du2)


def ln_silu_fwd(u, g, bvec, name, tm=256):
    t, d = u.shape
    tm = _row_tile(t, tm)

    def body(u_ref, g_ref, b_ref, o_ref):
        uv = u_ref[...]
        mu = jnp.mean(uv, axis=-1, keepdims=True)
        xc = uv - mu
        var = jnp.mean(xc * xc, axis=-1, keepdims=True)
        v = (xc * lax.rsqrt(var + EPS)) * g_ref[...] + b_ref[...]
        o_ref[...] = (v * _sigmoid(v)).astype(BF16)

    row = pl.BlockSpec((tm, d), lambda i: (i, 0))
    vec = pl.BlockSpec((1, d), lambda i: (0, 0))
    return pl.pallas_call(
        body, name=name, grid=(t // tm,), in_specs=[row, vec, vec], out_specs=row,
        out_shape=jax.ShapeDtypeStruct((t, d), BF16), compiler_params=_cp("parallel"),
    )(u, g, bvec)


def ln_silu_bwd(u, g, bvec, dout, name, tm=256):
    t, d = u.shape
    tm = _row_tile(t, tm)

    def body(u_ref, g_ref, b_ref, do_ref, du_ref, dg_ref, db_ref):
        i = pl.program_id(0)
        uv = u_ref[...]
        mu = jnp.mean(uv, axis=-1, keepdims=True)
        xc = uv - mu
        var = jnp.mean(xc * xc, axis=-1, keepdims=True)
        rstd = lax.rsqrt(var + EPS)
        n = xc * rstd
        v = n * g_ref[...] + b_ref[...]
        sg = _sigmoid(v)
        dv = do_ref[...].astype(F32) * (sg * (1.0 + v * (1.0 - sg)))
        dn = dv * g_ref[...]
        du_ref[...] = rstd * (dn - jnp.mean(dn, axis=-1, keepdims=True) - n * jnp.mean(dn * n, axis=-1, keepdims=True))
        dg = jnp.sum(dv * n, axis=0, keepdims=True)
        db = jnp.sum(dv, axis=0, keepdims=True)

        @pl.when(i == 0)
        def _():
            dg_ref[...] = dg
            db_ref[...] = db

        @pl.when(i > 0)
        def _():
            dg_ref[...] += dg
            db_ref[...] += db

    row = pl.BlockSpec((tm, d), lambda i: (i, 0))
    vec = pl.BlockSpec((1, d), lambda i: (0, 0))
    return pl.pallas_call(
        body, name=name, grid=(t // tm,), in_specs=[row, vec, vec, row], out_specs=[row, vec, vec],
        out_shape=[jax.ShapeDtypeStruct((t, d), F32), jax.ShapeDtypeStruct((1, d), F32), jax.ShapeDtypeStruct((1, d), F32)],
        compiler_params=_cp("arbitrary"),
    )(u, g, bvec, dout)


FFN_HALO = 8


def _conv3(win, w_ref, b_ref, ch):
    kw = w_ref.shape[0]
    acc = b_ref[...] + w_ref[kw - 1:kw, :] * win[FFN_HALO:, :]
    for k in range(kw - 2, -1, -1):
        win = pltpu.roll(win, 1, 0)
        acc = acc + w_ref[k:k + 1, :] * win[FFN_HALO:, :]
    return acc


def ffn_mid_fwd(p, dw, dwb, name, tc=256):
    b, s, f2 = p.shape
    f = f2 // 2
    kw = dw.shape[0]
    tc = min(tc, f)
    nf = f // tc
    ch = min(CONV_CHUNK, s)
    halo = FFN_HALO

    def body(pa_ref, pg_ref, wa_ref, wg_ref, ba_ref, bg_ref, o_ref, apad_ref, gpad_ref):
        apad_ref[0:halo, :] = jnp.zeros((halo, tc), F32)
        gpad_ref[0:halo, :] = jnp.zeros((halo, tc), F32)

        def fill(ci, carry):
            r0 = pl.multiple_of(ci * ch, ch)
            apad_ref[pl.ds(halo + r0, ch), :] = pa_ref[pl.ds(r0, ch), :]
            gpad_ref[pl.ds(halo + r0, ch), :] = pg_ref[pl.ds(r0, ch), :]
            return carry

        lax.fori_loop(0, s // ch, fill, 0)

        def chunk(ci, carry):
            r0 = pl.multiple_of(ci * ch, ch)
            ca = _conv3(apad_ref[pl.ds(r0, ch + halo), :], wa_ref, ba_ref, ch)
            cg = _conv3(gpad_ref[pl.ds(r0, ch + halo), :], wg_ref, bg_ref, ch)
            o_ref[pl.ds(r0, ch), :] = ((cg * _sigmoid(cg)) * ca).astype(BF16)
            return carry

        lax.fori_loop(0, s // ch, chunk, 0)

    blk = lambda off: pl.BlockSpec((None, s, tc), lambda bi, i: (bi, 0, i + off))
    wsp = lambda off: pl.BlockSpec((kw, tc), lambda bi, i: (0, i + off))
    bsp = lambda off: pl.BlockSpec((1, tc), lambda bi, i: (0, i + off))
    return pl.pallas_call(
        body, name=name, grid=(b, nf),
        in_specs=[blk(0), blk(nf), wsp(0), wsp(nf), bsp(0), bsp(nf)],
        out_specs=pl.BlockSpec((None, s, tc), lambda bi, i: (bi, 0, i)),
        out_shape=jax.ShapeDtypeStruct((b, s, f), BF16),
        scratch_shapes=[pltpu.VMEM((s + halo, tc), F32), pltpu.VMEM((s + halo, tc), F32)],
        compiler_params=_cp("parallel", "parallel"),
    )(p, p, dw, dw, dwb, dwb)


def ffn_mid_bwd(p, dw, dwb, ds, name, tc=256):
    b, s, f2 = p.shape
    f = f2 // 2
    kw = dw.shape[0]
    tc = min(tc, f)
    nf = f // tc
    ch = min(CONV_CHUNK, s)
    halo = FFN_HALO

    def body(pa_ref, pg_ref, wa_ref, wg_ref, ba_ref, bg_ref, ds_ref, dp_ref, ddw_ref, ddwb_ref,
             apad_ref, gpad_ref, dcpad_ref):
        hf = pl.program_id(0)
        bi = pl.program_id(2)
        gate_half = hf == 1

        @pl.when(bi == 0)
        def _():
            ddw_ref[...] = jnp.zeros(ddw_ref.shape, F32)
            ddwb_ref[...] = jnp.zeros(ddwb_ref.shape, F32)

        apad_ref[0:halo, :] = jnp.zeros((halo, tc), F32)
        gpad_ref[0:halo, :] = jnp.zeros((halo, tc), F32)
        dcpad_ref[s:s + halo, :] = jnp.zeros((halo, tc), F32)

        def fill(ci, carry):
            r0 = pl.multiple_of(ci * ch, ch)
            apad_ref[pl.ds(halo + r0, ch), :] = pa_ref[pl.ds(r0, ch), :]
            gpad_ref[pl.ds(halo + r0, ch), :] = pg_ref[pl.ds(r0, ch), :]
            return carry

        lax.fori_loop(0, s // ch, fill, 0)

        def grads(ci, carry):
            r0 = pl.multiple_of(ci * ch, ch)
            wa_win = apad_ref[pl.ds(r0, ch + halo), :]
            wg_win = gpad_ref[pl.ds(r0, ch + halo), :]
            ca = _conv3(wa_win, wa_ref, ba_ref, ch)
            cg = _conv3(wg_win, wg_ref, bg_ref, ch)
            sg = _sigmoid(cg)
            dsv = ds_ref[pl.ds(r0, ch), :].astype(F32)
            dc = jnp.where(gate_half, dsv * ca * (sg * (1.0 + cg * (1.0 - sg))), dsv * (cg * sg))
            dcpad_ref[pl.ds(r0, ch), :] = dc
            win = jnp.where(gate_half, wg_win, wa_win)
            ddw_ref[kw - 1] += jnp.sum((dc * win[halo:, :]).reshape(ch // 8, 8, tc), axis=0)
            for j in range(1, kw):
                win = pltpu.roll(win, 1, 0)
                ddw_ref[kw - 1 - j] += jnp.sum((dc * win[halo:, :]).reshape(ch // 8, 8, tc), axis=0)
            ddwb_ref[...] += jnp.sum(dc, axis=0, keepdims=True)
            return carry

        lax.fori_loop(0, s // ch, grads, 0)

        def back(ci, carry):
            r0 = pl.multiple_of(ci * ch, ch)
            wd = dcpad_ref[pl.ds(r0, ch + halo), :]
            acc = jnp.where(gate_half, wg_ref[kw - 1:kw, :], wa_ref[kw - 1:kw, :]) * wd[:ch, :]
            for j in range(1, kw):
                wd = pltpu.roll(wd, ch + halo - 1, 0)
                acc = acc + jnp.where(gate_half, wg_ref[kw - 1 - j:kw - j, :], wa_ref[kw - 1 - j:kw - j, :]) * wd[:ch, :]
            dp_ref[pl.ds(r0, ch), :] = acc.astype(BF16)
            return carry

        lax.fori_loop(0, s // ch, back, 0)

    blk = lambda off: pl.BlockSpec((None, s, tc), lambda hf, i, bi: (bi, 0, i + off))
    wsp = lambda off: pl.BlockSpec((kw, tc), lambda hf, i, bi: (0, i + off))
    bsp = lambda off: pl.BlockSpec((1, tc), lambda hf, i, bi: (0, i + off))
    return pl.pallas_call(
        body, name=name, grid=(2, nf, b),
        in_specs=[blk(0), blk(nf), wsp(0), wsp(nf), bsp(0), bsp(nf), blk(0)],
        out_specs=[
            pl.BlockSpec((None, s, tc), lambda hf, i, bi: (bi, 0, hf * nf + i)),
            pl.BlockSpec((kw, 8, tc), lambda hf, i, bi: (0, 0, hf * nf + i)),
            pl.BlockSpec((1, tc), lambda hf, i, bi: (0, hf * nf + i)),
        ],
        out_shape=[jax.ShapeDtypeStruct((b, s, f2), BF16), jax.ShapeDtypeStruct((kw, 8, f2), F32),
                   jax.ShapeDtypeStruct((1, f2), F32)],
        scratch_shapes=[pltpu.VMEM((s + halo, tc), F32)] * 3,
        compiler_params=_cp("parallel", "parallel", "arbitrary"),
    )(p, p, dw, dw, dwb, dwb, ds)


def perm_rows(x, col_blk, width, dil, inverse, out_dtype, name, tc=LANES):
    b, s, _ = x.shape
    tc = min(tc, width)
    nc = width // tc
    ln = s // dil

    def body(x_ref, o_ref):
        if dil == 1:
            o_ref[...] = x_ref[...].astype(out_dtype)
            return
        for r in range(dil):
            if inverse:
                o_ref[pl.ds(r, ln, stride=dil), :] = x_ref[r * ln:(r + 1) * ln, :].astype(out_dtype)
            else:
                o_ref[r * ln:(r + 1) * ln, :] = x_ref[pl.ds(r, ln, stride=dil), :].astype(out_dtype)

    return pl.pallas_call(
        body, name=name, grid=(b, nc),
        in_specs=[pl.BlockSpec((None, s, tc), lambda bi, i: (bi, 0, col_blk * nc + i))],
        out_specs=pl.BlockSpec((None, s, tc), lambda bi, i: (bi, 0, i)),
        out_shape=jax.ShapeDtypeStruct((b, s, width), out_dtype),
        compiler_params=_cp("parallel", "parallel"),
    )(x)


def _attn_masks(has_other):
    qi = lax.broadcasted_iota(jnp.int32, (BLK, BLK), 0)
    kk = lax.broadcasted_iota(jnp.int32, (BLK, BLK), 1)
    return kk <= qi, jnp.logical_and(kk >= qi, has_other)


def attn_fwd(q, q_blk, k, k_blk, v, v_blk, nblk, hw, name):
    b, s, _ = q.shape
    nh = hw // HEAD_DIM
    scale = 1.0 / math.sqrt(HEAD_DIM)

    def body(q_ref, kc_ref, kp_ref, vc_ref, vp_ref, o_ref, lse_ref):
        t = pl.program_id(1)
        mask_c, mask_p = _attn_masks((t % nblk) != 0)
        lane = lax.broadcasted_iota(jnp.int32, (BLK, LANES), 1)
        lse_acc = jnp.zeros((BLK, LANES), F32)
        for h in range(nh):
            sl = slice(h * HEAD_DIM, (h + 1) * HEAD_DIM)
            qh = q_ref[:, sl].astype(BF16)
            kc = kc_ref[:, sl].astype(BF16)
            kp = kp_ref[:, sl].astype(BF16)
            s_c = jnp.where(mask_c, _dot_nt(qh, kc) * scale, NEG_INF)
            s_p = jnp.where(mask_p, _dot_nt(qh, kp) * scale, NEG_INF)
            m = jnp.maximum(jnp.max(s_c, axis=-1, keepdims=True), jnp.max(s_p, axis=-1, keepdims=True))
            p_c = jnp.exp(s_c - m)
            p_p = jnp.exp(s_p - m)
            den = jnp.sum(p_c, axis=-1, keepdims=True) + jnp.sum(p_p, axis=-1, keepdims=True)
            o = _dot(p_c.astype(BF16), vc_ref[:, sl].astype(BF16)) + _dot(p_p.astype(BF16), vp_ref[:, sl].astype(BF16))
            o_ref[:, sl] = o / den
            lse_acc = jnp.where(lane == h, m + jnp.log(den), lse_acc)
        lse_ref[...] = lse_acc

    cur = lambda cb: pl.BlockSpec((None, BLK, hw), lambda bi, t: (bi, t, cb))
    prev = lambda cb: pl.BlockSpec((None, BLK, hw), lambda bi, t: (bi, jnp.where(t % nblk == 0, t, t - 1), cb))
    return pl.pallas_call(
        body, name=name, grid=(b, s // BLK),
        in_specs=[cur(q_blk), cur(k_blk), prev(k_blk), cur(v_blk), prev(v_blk)],
        out_specs=[pl.BlockSpec((None, BLK, hw), lambda bi, t: (bi, t, 0)),
                   pl.BlockSpec((None, BLK, LANES), lambda bi, t: (bi, t, 0))],
        out_shape=[jax.ShapeDtypeStruct((b, s, hw), F32), jax.ShapeDtypeStruct((b, s, LANES), F32)],
        compiler_params=_cp("parallel", "parallel"),
    )(q, k, k, v, v)


def attn_merge(outs, lses, name, tm=256):
    t, hw = outs[0].shape
    nh = hw // HEAD_DIM
    tm = _row_tile(t, tm)
    ng = len(outs)

    def body(*refs):
        o_refs, l_refs = refs[:ng], refs[ng:2 * ng]
        m_ref, lj_ref = refs[2 * ng:]
        lane = lax.broadcasted_iota(jnp.int32, (tm, LANES), 1)
        lj = jnp.zeros((tm, LANES), F32)
        for h in range(nh):
            sl = slice(h * HEAD_DIM, (h + 1) * HEAD_DIM)
            ls = [l_refs[g][:, h:h + 1] for g in range(ng)]
            mx = functools.reduce(jnp.maximum, ls)
            es = [jnp.exp(l - mx) for l in ls]
            tot = functools.reduce(lambda a, c: a + c, es)
            acc = (es[0] / tot) * o_refs[0][:, sl]
            for g in range(1, ng):
                acc = acc + (es[g] / tot) * o_refs[g][:, sl]
            m_ref[:, sl] = acc.astype(BF16)
            lj = jnp.where(lane == h, mx + jnp.log(tot), lj)
        lj_ref[...] = lj

    row = pl.BlockSpec((tm, hw), lambda i: (i, 0))
    st = pl.BlockSpec((tm, LANES), lambda i: (i, 0))
    return pl.pallas_call(
        body, name=name, grid=(t // tm,), in_specs=[row] * ng + [st] * ng, out_specs=[row, st],
        out_shape=[jax.ShapeDtypeStruct((t, hw), BF16), jax.ShapeDtypeStruct((t, LANES), F32)],
        compiler_params=_cp("parallel"),
    )(*outs, *lses)


def attn_bwd_prep(dmerged, merged, name, tm=256):
    t, hw = merged.shape
    nh = hw // HEAD_DIM
    tm = _row_tile(t, tm)

    def body(d_ref, m_ref, o_ref):
        lane = lax.broadcasted_iota(jnp.int32, (tm, LANES), 1)
        acc = jnp.zeros((tm, LANES), F32)
        for h in range(nh):
            sl = slice(h * HEAD_DIM, (h + 1) * HEAD_DIM)
            dsum = jnp.sum(d_ref[:, sl] * m_ref[:, sl].astype(F32), axis=-1, keepdims=True)
            acc = jnp.where(lane == h, dsum, acc)
        o_ref[...] = acc

    row = pl.BlockSpec((tm, hw), lambda i: (i, 0))
    return pl.pallas_call(
        body, name=name, grid=(t // tm,), in_specs=[row, row], out_specs=pl.BlockSpec((tm, LANES), lambda i: (i, 0)),
        out_shape=jax.ShapeDtypeStruct((t, LANES), F32), compiler_params=_cp("parallel"),
    )(dmerged, merged)


def attn_bwd(q, q_blk, k, k_blk, v, v_blk, do, lsej, dm, dk_add, dv_add, nblk, hw, name):
    b, s, _ = q.shape
    nh = hw // HEAD_DIM
    scale = 1.0 / math.sqrt(HEAD_DIM)
    has_add = dk_add is not None

    def body(*refs):
        (qm_ref, qn_ref, kc_ref, kp_ref, vc_ref, vp_ref, dom_ref, don_ref, lm_ref, ln_ref, dmm_ref, dmn_ref) = refs[:12]
        if has_add:
            dka_ref, dva_ref = refs[12:14]
            dq_ref, dk_ref, dv_ref = refs[14:]
        else:
            dq_ref, dk_ref, dv_ref = refs[12:]
        t = pl.program_id(1)
        n = t % nblk
        mask_c, mask_p = _attn_masks(n != 0)
        _, mask_n = _attn_masks(n != nblk - 1)
        for h in range(nh):
            sl = slice(h * HEAD_DIM, (h + 1) * HEAD_DIM)
            qm = qm_ref[:, sl].astype(BF16)
            qn = qn_ref[:, sl].astype(BF16)
            kc = kc_ref[:, sl].astype(BF16)
            kp = kp_ref[:, sl].astype(BF16)
            vc = vc_ref[:, sl].astype(BF16)
            vp = vp_ref[:, sl].astype(BF16)
            dom = dom_ref[:, sl].astype(BF16)
            don = don_ref[:, sl].astype(BF16)
            lm, lnx = lm_ref[:, h:h + 1], ln_ref[:, h:h + 1]
            dmm, dmn = dmm_ref[:, h:h + 1], dmn_ref[:, h:h + 1]
            pa = jnp.exp(jnp.where(mask_c, _dot_nt(qm, kc) * scale, NEG_INF) - lm)
            pb = jnp.exp(jnp.where(mask_p, _dot_nt(qm, kp) * scale, NEG_INF) - lm)
            pc = jnp.exp(jnp.where(mask_n, _dot_nt(qn, kc) * scale, NEG_INF) - lnx)
            dsa = (pa * (_dot_nt(dom, vc) - dmm)).astype(BF16)
            dsb = (pb * (_dot_nt(dom, vp) - dmm)).astype(BF16)
            dsc = (pc * (_dot_nt(don, vc) - dmn)).astype(BF16)
            dq_ref[:, sl] = (_dot(dsa, kc) + _dot(dsb, kp)) * scale
            dk = (_dot_tn(dsa, qm) + _dot_tn(dsc, qn)) * scale
            dv = _dot_tn(pa.astype(BF16), dom) + _dot_tn(pc.astype(BF16), don)
            if has_add:
                dk = dk + dka_ref[:, sl]
                dv = dv + dva_ref[:, sl]
            dk_ref[:, sl] = dk
            dv_ref[:, sl] = dv

    def spec(cb, w, which):
        if which == "cur":
            return pl.BlockSpec((None, BLK, w), lambda bi, t: (bi, t, cb))
        if which == "prev":
            return pl.BlockSpec((None, BLK, w), lambda bi, t: (bi, jnp.where(t % nblk == 0, t, t - 1), cb))
        return pl.BlockSpec((None, BLK, w), lambda bi, t: (bi, jnp.where(t % nblk == nblk - 1, t, t + 1), cb))

    in_specs = [
        spec(q_blk, hw, "cur"), spec(q_blk, hw, "next"), spec(k_blk, hw, "cur"), spec(k_blk, hw, "prev"),
        spec(v_blk, hw, "cur"), spec(v_blk, hw, "prev"), spec(0, hw, "cur"), spec(0, hw, "next"),
        spec(0, LANES, "cur"), spec(0, LANES, "next"), spec(0, LANES, "cur"), spec(0, LANES, "next"),
    ]
    args = [q, q, k, k, v, v, do, do, lsej, lsej, dm, dm]
    if has_add:
        in_specs += [spec(0, hw, "cur"), spec(0, hw, "cur")]
        args += [dk_add, dv_add]
    out = pl.BlockSpec((None, BLK, hw), lambda bi, t: (bi, t, 0))
    return pl.pallas_call(
        body, name=name, grid=(b, s // BLK), in_specs=in_specs, out_specs=[out, out, out],
        out_shape=[jax.ShapeDtypeStruct((b, s, hw), F32)] * 3,
        compiler_params=_cp("parallel", "parallel"),
    )(*args)


def sum_parts(r, name, tm=256):
    n, rows, c = r.shape
    tm = _row_tile(rows, tm)

    def body(r_ref, o_ref):
        acc = r_ref[0].astype(F32)
        for j in range(1, n):
            acc = acc + r_ref[j].astype(F32)
        o_ref[...] = acc

    return pl.pallas_call(
        body, name=name, grid=(rows // tm,),
        in_specs=[pl.BlockSpec((n, tm, c), lambda i: (0, i, 0))],
        out_specs=pl.BlockSpec((tm, c), lambda i: (i, 0)),
        out_shape=jax.ShapeDtypeStruct((rows, c), F32), compiler_params=_cp("parallel"),
    )(r)


def adamw(w, m, v, g_parts, name, tm=256):
    rows, c = w.shape
    tm = _row_tile(rows, tm)
    npart = len(g_parts)

    def body(*refs):
        w_ref, m_ref, v_ref = refs[:3]
        g_refs = refs[3:3 + npart]
        go_ref, d_ref, mo_ref, vo_ref = refs[3 + npart:]
        g = g_refs[0][...]
        for k in range(1, npart):
            g = g + g_refs[k][...]
        mn = ADAM_B1 * m_ref[...] + (1.0 - ADAM_B1) * g
        vn = ADAM_B2 * v_ref[...] + (1.0 - ADAM_B2) * (g * g)
        m_hat = mn / (1.0 - ADAM_B1 ** ADAM_STEP)
        v_hat = vn / (1.0 - ADAM_B2 ** ADAM_STEP)
        go_ref[...] = g
        d_ref[...] = -ADAM_LR * (m_hat / (jnp.sqrt(v_hat) + ADAM_EPS) + ADAM_WD * w_ref[...])
        mo_ref[...] = mn
        vo_ref[...] = vn

    row = pl.BlockSpec((tm, c), lambda i: (i, 0))
    return pl.pallas_call(
        body, name=name, grid=(rows // tm,), in_specs=[row] * (3 + npart), out_specs=[row] * 4,
        out_shape=[jax.ShapeDtypeStruct((rows, c), F32)] * 4, compiler_params=_cp("parallel"),
    )(w, m, v, *g_parts)


def _place():
    return lax.axis_index("x"), lax.axis_index("y"), lax.axis_index("c")


def _other_chips(x, y, c):
    return [(1 - x, y, c), (x, 1 - y, c), (1 - x, 1 - y, c)]


def _chip_of(px, py):
    return 2 * px + py


def _blk(ref, kind, j):
    return ref.at[j] if kind == "lead" else ref.at[:, j]


def allgather_shards(shards, kinds, name="allgather_shards"):
    n = len(shards)
    out_shape = []
    for sh, kind in zip(shards, kinds):
        shape = (N_CHIPS, *sh.shape) if kind == "lead" else (sh.shape[0], N_CHIPS, *sh.shape[1:])
        out_shape.append(jax.ShapeDtypeStruct(shape, sh.dtype))

    def body(*refs):
        ins, outs = refs[:n], refs[n:2 * n]
        send_sems, recv_sems, local_sems = refs[2 * n:]
        x, y, c = _place()
        me = _chip_of(x, y)
        peers = _other_chips(x, y, c)
        copies = []
        for k in range(n):
            loc = pltpu.make_async_copy(ins[k], _blk(outs[k], kinds[k], me), local_sems.at[k])
            loc.start()
            copies.append(loc)
            for r, peer in enumerate(peers):
                cp = pltpu.make_async_remote_copy(
                    src_ref=ins[k], dst_ref=_blk(outs[k], kinds[k], me),
                    send_sem=send_sems.at[3 * k + r], recv_sem=recv_sems.at[3 * k + r],
                    device_id=peer, device_id_type=MESH)
                cp.start()
                copies.append(cp)
        for cp in copies:
            cp.wait()

    any_spec = pl.BlockSpec(memory_space=pl.ANY)
    return pl.pallas_call(
        body, name=name, in_specs=[any_spec] * n, out_specs=[any_spec] * n, out_shape=out_shape,
        scratch_shapes=[pltpu.SemaphoreType.DMA((3 * n,)), pltpu.SemaphoreType.DMA((3 * n,)),
                        pltpu.SemaphoreType.DMA((n,))],
        compiler_params=pltpu.CompilerParams(has_side_effects=True),
    )(*shards)


def scatter_grad_blocks(grads, kinds, shard_shapes, name="scatter_grad_blocks"):
    n = len(grads)
    out_shape = [jax.ShapeDtypeStruct((N_CHIPS, *ss), g.dtype) for g, ss in zip(grads, shard_shapes)]

    def body(*refs):
        ins, outs = refs[:n], refs[n:2 * n]
        send_sems, recv_sems, local_sems = refs[2 * n:]
        x, y, c = _place()
        me = _chip_of(x, y)
        peers = _other_chips(x, y, c)
        copies = []
        for k in range(n):
            loc = pltpu.make_async_copy(_blk(ins[k], kinds[k], me), outs[k].at[me], local_sems.at[k])
            loc.start()
            copies.append(loc)
            for r, peer in enumerate(peers):
                pj = _chip_of(peer[0], peer[1])
                cp = pltpu.make_async_remote_copy(
                    src_ref=_blk(ins[k], kinds[k], pj), dst_ref=outs[k].at[me],
                    send_sem=send_sems.at[3 * k + r], recv_sem=recv_sems.at[3 * k + r],
                    device_id=peer, device_id_type=MESH)
                cp.start()
                copies.append(cp)
        for cp in copies:
            cp.wait()

    any_spec = pl.BlockSpec(memory_space=pl.ANY)
    return pl.pallas_call(
        body, name=name, in_specs=[any_spec] * n, out_specs=[any_spec] * n, out_shape=out_shape,
        scratch_shapes=[pltpu.SemaphoreType.DMA((3 * n,)), pltpu.SemaphoreType.DMA((3 * n,)),
                        pltpu.SemaphoreType.DMA((n,))],
        compiler_params=pltpu.CompilerParams(has_side_effects=True),
    )(*grads)


def swap_with_sibling(parts, name="swap_with_sibling"):
    n = len(parts)

    def body(*refs):
        ins, outs = refs[:n], refs[n:2 * n]
        send_sems, recv_sems = refs[2 * n:]
        x, y, c = _place()
        copies = []
        for k in range(n):
            cp = pltpu.make_async_remote_copy(
                src_ref=ins[k], dst_ref=outs[k], send_sem=send_sems.at[k], recv_sem=recv_sems.at[k],
                device_id=(x, y, 1 - c), device_id_type=MESH)
            cp.start()
            copies.append(cp)
        for cp in copies:
            cp.wait()

    any_spec = pl.BlockSpec(memory_space=pl.ANY)
    return pl.pallas_call(
        body, name=name, in_specs=[any_spec] * n, out_specs=[any_spec] * n,
        out_shape=[jax.ShapeDtypeStruct(p.shape, p.dtype) for p in parts],
        scratch_shapes=[pltpu.SemaphoreType.DMA((n,)), pltpu.SemaphoreType.DMA((n,))],
        compiler_params=pltpu.CompilerParams(has_side_effects=True),
    )(*parts)


def allreduce_small(vec, name="allreduce_small"):
    rows, lanes = vec.shape

    def body(v_ref, o_ref, buf_ref, send_sems, recv_sems):
        x, y, c = _place()
        me = 4 * x + 2 * y + c
        buf_ref[me] = v_ref[...]
        copies = []
        for k in range(1, N_DEV):
            px, py, pc = x ^ ((k >> 2) & 1), y ^ ((k >> 1) & 1), c ^ (k & 1)
            peer_slot = 4 * px + 2 * py + pc
            cp = pltpu.make_async_remote_copy(
                src_ref=buf_ref.at[me], dst_ref=buf_ref.at[me], send_sem=send_sems.at[k - 1],
                recv_sem=recv_sems.at[k - 1], device_id=(px, py, pc), device_id_type=MESH)
            cp.start()
            copies.append(pltpu.make_async_remote_copy(
                src_ref=buf_ref.at[me], dst_ref=buf_ref.at[peer_slot], send_sem=send_sems.at[k - 1],
                recv_sem=recv_sems.at[k - 1], device_id=(px, py, pc), device_id_type=MESH))
        for cp in copies:
            cp.wait()
        acc = buf_ref[0]
        for j in range(1, N_DEV):
            acc = acc + buf_ref[j]
        o_ref[...] = acc

    vm = pl.BlockSpec(memory_space=pltpu.VMEM)
    return pl.pallas_call(
        body, name=name, in_specs=[vm], out_specs=vm, out_shape=jax.ShapeDtypeStruct((rows, lanes), F32),
        scratch_shapes=[pltpu.VMEM((N_DEV, rows, lanes), F32), pltpu.SemaphoreType.DMA((N_DEV - 1,)),
                        pltpu.SemaphoreType.DMA((N_DEV - 1,))],
        compiler_params=pltpu.CompilerParams(has_side_effects=True, vmem_limit_bytes=VMEM_LIMIT_V7X),
    )(vec)


def _pack(arrays):
    flat = jnp.concatenate([a.reshape(-1).astype(F32) for a in arrays])
    n = flat.shape[0]
    rows = -(-n // LANES)
    rows = -(-rows // 8) * 8
    return jnp.pad(flat, (0, rows * LANES - n)).reshape(rows, LANES)


def _unpack(packed, shapes, lead=()):
    flat = packed.reshape(*lead, -1)
    out, off = [], 0
    for shp in shapes:
        n = math.prod(shp)
        out.append(flat[..., off:off + n].reshape(*lead, *shp))
        off += n
    return out


def _row(vec):
    return vec.reshape(1, -1)


def kernel(x, mix_pre_g, mix_post_g, ffn_pre_g, ffn_post_g, cm_w_in, cm_b_in, cm_dw, cm_dw_b, cm_ln_g, cm_ln_b, cm_w_out, cm_b_out, kv_norm_g, w_kv, w_q, w_o, ffn_w_in, ffn_dw, ffn_dw_b, ffn_w_out, loss_target, m_mix_pre_g, m_mix_post_g, m_ffn_pre_g, m_ffn_post_g, m_cm_w_in, m_cm_b_in, m_cm_dw, m_cm_dw_b, m_cm_ln_g, m_cm_ln_b, m_cm_w_out, m_cm_b_out, m_kv_norm_g, m_w_kv, m_w_q, m_w_o, m_ffn_w_in, m_ffn_dw, m_ffn_dw_b, m_ffn_w_out, v_mix_pre_g, v_mix_post_g, v_ffn_pre_g, v_ffn_post_g, v_cm_w_in, v_cm_b_in, v_cm_dw, v_cm_dw_b, v_cm_ln_g, v_cm_ln_b, v_cm_w_out, v_cm_b_out, v_kv_norm_g, v_w_kv, v_w_q, v_w_o, v_ffn_w_in, v_ffn_dw, v_ffn_dw_b, v_ffn_w_out):
    names = ["mix_pre_g", "mix_post_g", "ffn_pre_g", "ffn_post_g", "cm_w_in", "cm_b_in", "cm_dw", "cm_dw_b", "cm_ln_g",
             "cm_ln_b", "cm_w_out", "cm_b_out", "kv_norm_g", "w_kv", "w_q", "w_o", "ffn_w_in", "ffn_dw", "ffn_dw_b",
             "ffn_w_out"]
    w_in = dict(zip(names, [mix_pre_g, mix_post_g, ffn_pre_g, ffn_post_g, cm_w_in, cm_b_in, cm_dw, cm_dw_b, cm_ln_g,
                            cm_ln_b, cm_w_out, cm_b_out, kv_norm_g, w_kv, w_q, w_o, ffn_w_in, ffn_dw, ffn_dw_b, ffn_w_out]))
    m_in = dict(zip(names, [m_mix_pre_g, m_mix_post_g, m_ffn_pre_g, m_ffn_post_g, m_cm_w_in, m_cm_b_in, m_cm_dw, m_cm_dw_b,
                            m_cm_ln_g, m_cm_ln_b, m_cm_w_out, m_cm_b_out, m_kv_norm_g, m_w_kv, m_w_q, m_w_o, m_ffn_w_in,
                            m_ffn_dw, m_ffn_dw_b, m_ffn_w_out]))
    v_in = dict(zip(names, [v_mix_pre_g, v_mix_post_g, v_ffn_pre_g, v_ffn_post_g, v_cm_w_in, v_cm_b_in, v_cm_dw, v_cm_dw_b,
                            v_cm_ln_g, v_cm_ln_b, v_cm_w_out, v_cm_b_out, v_kv_norm_g, v_w_kv, v_w_q, v_w_o, v_ffn_w_in,
                            v_ffn_dw, v_ffn_dw_b, v_ffn_w_out]))

    bsz, seq, d = x.shape
    t = bsz * seq
    n_b = DEPTH - N_A
    hw = w_o.shape[-1]
    qw = N_GROUPS * hw
    f2 = ffn_dw_b.shape[-1]
    f = f2 // 2
    me_chip = _chip_of(lax.axis_index("x"), lax.axis_index("y"))

    big = ["cm_w_in", "cm_w_out", "w_kv", "w_q", "w_o", "ffn_w_in", "ffn_w_out"]
    big_kind = {"cm_w_in": "lead", "cm_w_out": "second", "w_kv": "lead", "w_q": "lead", "w_o": "second",
                "ffn_w_in": "lead", "ffn_w_out": "second"}
    small_sharded = ["cm_b_in", "cm_dw", "cm_dw_b", "cm_ln_g", "cm_ln_b", "cm_b_out", "ffn_dw"]
    small_pack = _pack([w_in[n] for n in small_sharded])
    gathered = allgather_shards([w_in[n].astype(BF16) for n in big] + [small_pack],
                                [big_kind[n] for n in big] + ["lead"])
    wg = dict(zip(big, gathered[:-1]))
    small_full = {}
    for n, arr4 in zip(small_sharded, _unpack(gathered[-1], [w_in[n].shape for n in small_sharded], lead=(N_CHIPS,))):
        shp = w_in[n].shape
        small_full[n] = jnp.moveaxis(arr4, 0, -2).reshape(*shp[:-1], N_CHIPS * shp[-1])

    wcin = wg["cm_w_in"].reshape(N_CHIPS * N_A, d, -1)
    wcout = wg["cm_w_out"].reshape(N_A, d, d)
    wkv = wg["w_kv"]
    wq = wg["w_q"].reshape(N_CHIPS * n_b, d, -1)
    wo = wg["w_o"].reshape(n_b, hw, d)
    wfin = wg["ffn_w_in"].reshape(N_CHIPS * DEPTH, d, -1)
    wfout = wg["ffn_w_out"].reshape(DEPTH, f, d)

    x2d = x.reshape(t, d)
    saved = []
    (h1,) = resid_norm_fwd(x2d, None, None, [_row(mix_pre_g[0])], name="norm_in")
    xcur = x2d
    kv_state = None
    for i in range(DEPTH):
        sv = {"x_in": xcur, "h1": h1}
        if i < N_A:
            z = mm_nn(h1, wcin, N_CHIPS, N_A, i, bias=_row(small_full["cm_b_in"][i]), name=f"cm_in{i}")
            u2 = glu_conv_fwd(z.reshape(bsz, seq, 2 * d), small_full["cm_dw"][i], _row(small_full["cm_dw_b"][i]),
                              name=f"glu_conv{i}").reshape(t, d)
            u4 = ln_silu_fwd(u2, _row(small_full["cm_ln_g"][i]), _row(small_full["cm_ln_b"][i]), name=f"ln_silu{i}")
            y = mm_nn(u4, wcout, 1, 1, i, bias=_row(small_full["cm_b_out"][i]), name=f"cm_out{i}")
            sv.update(z=z, u2=u2, u4=u4)
        else:
            j = i - N_A
            q = mm_nn(h1, wq, N_CHIPS, n_b, j, name=f"q_proj{j}").reshape(bsz, seq, qw)
            outs, lses, q_subs = [], [], []
            for g, dil in enumerate(DILATIONS):
                nblk = seq // dil // BLK
                if dil == 1:
                    o_s, l_s = attn_fwd(q, g, kv_state["kv"], g, kv_state["kv"], N_GROUPS + g, nblk, hw, name=f"attn_fwd{j}_{g}")
                    q_subs.append(None)
                    outs.append(o_s.reshape(t, hw))
                    lses.append(l_s.reshape(t, LANES))
                else:
                    q_s = perm_rows(q, g, hw, dil, False, BF16, name=f"q_sub{j}_{g}")
                    o_s, l_s = attn_fwd(q_s, 0, kv_state["k_sub"][g], 0, kv_state["v_sub"][g], 0, nblk, hw, name=f"attn_fwd{j}_{g}")
                    q_subs.append(q_s)
                    outs.append(perm_rows(o_s, 0, hw, dil, True, F32, name=f"o_tok{j}_{g}").reshape(t, hw))
                    lses.append(perm_rows(l_s, 0, LANES, dil, True, F32, name=f"lse_tok{j}_{g}").reshape(t, LANES))
            merged, lsej = attn_merge(outs, lses, name=f"attn_merge{j}")
            y = mm_nn(merged, wo, 1, 1, j, name=f"o_proj{j}")
            sv.update(q=q, q_subs=q_subs, merged=merged, lsej=lsej)
        x1, h2 = resid_norm_fwd(xcur, y, _row(mix_post_g[i]), [_row(ffn_pre_g[i])], name=f"resid_mix{i}")
        p = mm_nn(h2, wfin, N_CHIPS, DEPTH, i, name=f"ffn_in{i}")
        s_act = ffn_mid_fwd(p.reshape(bsz, seq, f2), small_full["ffn_dw"][i], _row(ffn_dw_b[i]), name=f"ffn_mid{i}").reshape(t, f)
        y2 = mm_nn(s_act, wfout, 1, 1, i, name=f"ffn_out{i}")
        next_gains = []
        if i + 1 < DEPTH:
            next_gains.append(_row(mix_pre_g[i + 1]))
        if i == N_A - 1:
            next_gains.append(_row(kv_norm_g))
        res = resid_norm_fwd(x1, y2, _row(ffn_post_g[i]), next_gains, name=f"resid_ffn{i}")
        sv.update(y=y, x1=x1, h2=h2, p=p, s=s_act, y2=y2)
        saved.append(sv)
        xcur = res[0]
        if i + 1 < DEPTH:
            h1 = res[1]
        if i == N_A - 1:
            kvn = res[2]
            kv = mm_nn(kvn, wkv, N_CHIPS, 1, 0, name="kv_proj").reshape(bsz, seq, 2 * qw)
            k_sub, v_sub = {}, {}
            for g, dil in enumerate(DILATIONS):
                if dil > 1:
                    k_sub[g] = perm_rows(kv, g, hw, dil, False, BF16, name=f"k_sub{g}")
                    v_sub[g] = perm_rows(kv, N_GROUPS + g, hw, dil, False, BF16, name=f"v_sub{g}")
            kv_state = {"kv": kv, "k_sub": k_sub, "v_sub": v_sub, "kvn": kvn, "x_a": xcur}

    dx, loss_tile = loss_fwd_bwd(xcur, loss_target.reshape(t, d))
    loss = lax.psum(loss_tile[0, 0], ("x", "y", "c"))

    gsm = {n: [None] * w_in[n].shape[0] for n in
           ["mix_pre_g", "mix_post_g", "ffn_pre_g", "ffn_post_g", "cm_b_in", "cm_dw", "cm_dw_b", "cm_ln_g", "cm_ln_b",
            "cm_b_out", "ffn_dw", "ffn_dw_b"]}
    gbig = {n: [None] * (w_in[n].shape[0] if w_in[n].ndim == 3 else 1) for n in big}
    dk_acc = {g: None for g in range(N_GROUPS)}
    dv_acc = {g: None for g in range(N_GROUPS)}
    for i in range(DEPTH - 1, -1, -1):
        sv = saved[i]
        dy2, dg, _ = norm_bwd(sv["y2"], _row(ffn_post_g[i]), dx, out_dtype=BF16, name=f"bwd_ffn_post{i}")
        gsm["ffn_post_g"][i] = dg
        ds = mm_nt(dy2, wfout, 1, 1, i, name=f"bwd_ffn_out_dx{i}")
        gbig["ffn_w_out"][i] = mm_tn(sv["s"], dy2, 1, name=f"bwd_ffn_out_dw{i}").reshape(N_CHIPS, f // N_CHIPS, d)
        dp, ddw, ddwb = ffn_mid_bwd(sv["p"].reshape(bsz, seq, f2), small_full["ffn_dw"][i], _row(ffn_dw_b[i]), ds.reshape(bsz, seq, f),
                                    name=f"bwd_ffn_mid{i}")
        gsm["ffn_dw"][i] = jnp.sum(ddw, axis=1)
        gsm["ffn_dw_b"][i] = ddwb
        dp = dp.reshape(t, f2)
        dh2 = mm_nt(dp, wfin, N_CHIPS, DEPTH, i, name=f"bwd_ffn_in_dx{i}")
        gbig["ffn_w_in"][i] = mm_tn(sv["h2"], dp, N_CHIPS, name=f"bwd_ffn_in_dw{i}")
        dx1, dg, _ = norm_bwd(sv["x1"], _row(ffn_pre_g[i]), dh2, add=dx, name=f"bwd_ffn_pre{i}")
        gsm["ffn_pre_g"][i] = dg
        dy, dg, dbias = norm_bwd(sv["y"], _row(mix_post_g[i]), dx1, out_dtype=BF16, name=f"bwd_mix_post{i}")
        gsm["mix_post_g"][i] = dg
        if i < N_A:
            gsm["cm_b_out"][i] = dbias
            du4 = mm_nt(dy, wcout, 1, 1, i, name=f"bwd_cm_out_dx{i}")
            gbig["cm_w_out"][i] = mm_tn(sv["u4"], dy, 1, name=f"bwd_cm_out_dw{i}").reshape(N_CHIPS, d // N_CHIPS, d)
            du2, dlg, dlb = ln_silu_bwd(sv["u2"], _row(small_full["cm_ln_g"][i]), _row(small_full["cm_ln_b"][i]), du4,
                                        name=f"bwd_ln_silu{i}")
            gsm["cm_ln_g"][i], gsm["cm_ln_b"][i] = dlg, dlb
            dza, dzg, ddw, ddwb, dba, dbg = glu_conv_bwd(sv["z"].reshape(bsz, seq, 2 * d), small_full["cm_dw"][i],
                                                         du2.reshape(bsz, seq, d), name=f"bwd_glu_conv{i}")
            gsm["cm_dw"][i] = jnp.sum(ddw, axis=1)
            gsm["cm_dw_b"][i] = ddwb
            gsm["cm_b_in"][i] = jnp.concatenate([dba, dbg], axis=-1)
            dz = jnp.concatenate([dza, dzg], axis=-1).reshape(t, 2 * d)
            dh1 = mm_nt(dz, wcin, N_CHIPS, N_A, i, name=f"bwd_cm_in_dx{i}")
            gbig["cm_w_in"][i] = mm_tn(sv["h1"], dz, N_CHIPS, name=f"bwd_cm_in_dw{i}")
        else:
            j = i - N_A
            dmerged = mm_nt(dy, wo, 1, 1, j, name=f"bwd_o_proj_dx{j}")
            gbig["w_o"][j] = mm_tn(sv["merged"], dy, 1, name=f"bwd_o_proj_dw{j}").reshape(N_CHIPS, hw // N_CHIPS, d)
            dmt = attn_bwd_prep(dmerged, sv["merged"], name=f"bwd_attn_prep{j}")
            dq_parts = []
            for g, dil in enumerate(DILATIONS):
                nblk = seq // dil // BLK
                do3 = dmerged.reshape(bsz, seq, hw)
                lj3 = sv["lsej"].reshape(bsz, seq, LANES)
                dm3 = dmt.reshape(bsz, seq, LANES)
                if dil == 1:
                    dq_g, dk_g, dv_g = attn_bwd(sv["q"], g, kv_state["kv"], g, kv_state["kv"], N_GROUPS + g, do3, lj3, dm3,
                                                dk_acc[g], dv_acc[g], nblk, hw, name=f"attn_bwd{j}_{g}")
                    dq_parts.append(dq_g.reshape(t, hw))
                else:
                    do_s = perm_rows(do3, 0, hw, dil, False, BF16, name=f"do_sub{j}_{g}")
                    lj_s = perm_rows(lj3, 0, LANES, dil, False, F32, name=f"lsej_sub{j}_{g}", tc=LANES)
                    dm_s = perm_rows(dm3, 0, LANES, dil, False, F32, name=f"dm_sub{j}_{g}", tc=LANES)
                    dq_g, dk_g, dv_g = attn_bwd(sv["q_subs"][g], 0, kv_state["k_sub"][g], 0, kv_state["v_sub"][g], 0, do_s,
                                                lj_s, dm_s, dk_acc[g], dv_acc[g], nblk, hw, name=f"attn_bwd{j}_{g}")
                    dq_parts.append(perm_rows(dq_g, 0, hw, dil, True, F32, name=f"dq_tok{j}_{g}").reshape(t, hw))
                dk_acc[g], dv_acc[g] = dk_g, dv_g
            dq = jnp.concatenate(dq_parts, axis=-1).astype(BF16)
            dh1 = mm_nt(dq, wq, N_CHIPS, n_b, j, name=f"bwd_q_proj_dx{j}")
            gbig["w_q"][j] = mm_tn(sv["h1"], dq, N_CHIPS, name=f"bwd_q_proj_dw{j}")
        dx, dg, _ = norm_bwd(sv["x_in"], _row(mix_pre_g[i]), dh1, add=dx1, name=f"bwd_mix_pre{i}")
        gsm["mix_pre_g"][i] = dg
        if i == N_A:
            dkv_parts = []
            for acc in (dk_acc, dv_acc):
                for g, dil in enumerate(DILATIONS):
                    part = acc[g]
                    if dil > 1:
                        part = perm_rows(part, 0, hw, dil, True, F32, name=f"dkv_tok{len(dkv_parts)}")
                    dkv_parts.append(part.reshape(t, hw))
            dkv = jnp.concatenate(dkv_parts, axis=-1).astype(BF16)
            dkvn = mm_nt(dkv, wkv, N_CHIPS, 1, 0, name="bwd_kv_proj_dx")
            gbig["w_kv"][0] = mm_tn(kv_state["kvn"], dkv, N_CHIPS, name="bwd_kv_proj_dw")
            dx, dg_kv, _ = norm_bwd(kv_state["x_a"], _row(kv_norm_g), dkvn, add=dx, name="bwd_kv_norm")
    grad_x = dx.reshape(bsz, seq, d)

    stacked = []
    for n in big:
        parts = gbig[n]
        if w_in[n].ndim == 2:
            stacked.append(parts[0])
        elif big_kind[n] == "lead":
            stacked.append(jnp.stack(parts, axis=1))
        else:
            stacked.append(jnp.stack(parts, axis=0))
    recv = scatter_grad_blocks(stacked, [big_kind[n] for n in big], [w_in[n].shape for n in big])
    plane = [sum_parts(r.reshape(N_CHIPS, -1, r.shape[-1]), name=f"sum_chips_{n}") for n, r in zip(big, recv)]
    other = swap_with_sibling(plane)

    outs_g, outs_d, outs_m, outs_v = {}, {}, {}, {}
    for n, p_mine, p_other in zip(big, plane, other):
        shp = w_in[n].shape
        flat = lambda a: a.reshape(-1, shp[-1])
        g_, d_, m_, v_ = adamw(flat(w_in[n]), flat(m_in[n]), flat(v_in[n]), [p_mine, p_other], name=f"adamw_{n}")
        outs_g[n], outs_d[n], outs_m[n], outs_v[n] = (a.reshape(shp) for a in (g_, d_, m_, v_))

    small_names = [n for n in names if n not in big]
    small_shapes_full = {}
    small_grads_full = []
    for n in small_names:
        if n == "kv_norm_g":
            gfull = dg_kv.reshape(-1)
        elif n in ("cm_dw", "ffn_dw"):
            gfull = jnp.stack(gsm[n], axis=0)
        else:
            gfull = jnp.stack([a.reshape(-1) for a in gsm[n]], axis=0)
        small_shapes_full[n] = gfull.shape
        small_grads_full.append(gfull)
    summed = allreduce_small(_pack(small_grads_full))
    g_full = dict(zip(small_names, _unpack(summed, [small_shapes_full[n] for n in small_names])))
    g_loc = {}
    for n in small_names:
        if n in small_sharded:
            width = w_in[n].shape[-1]
            g_loc[n] = lax.dynamic_slice_in_dim(g_full[n], me_chip * width, width, axis=g_full[n].ndim - 1)
        else:
            g_loc[n] = g_full[n]
    res = adamw(_pack([w_in[n] for n in small_names]), _pack([m_in[n] for n in small_names]),
                _pack([v_in[n] for n in small_names]), [_pack([g_loc[n] for n in small_names])], name="adamw_small")
    shapes_loc = [w_in[n].shape for n in small_names]
    for dst, packed in zip((outs_g, outs_d, outs_m, outs_v), res):
        for n, a in zip(small_names, _unpack(packed, shapes_loc)):
            dst[n] = a

    return (loss, grad_x, *[outs_g[n] for n in names], *[outs_d[n] for n in names],
            *[outs_m[n] for n in names], *[outs_v[n] for n in names])
```

```python
import functools
import math

import jax
import jax.numpy as jnp
from jax import lax
from jax.experimental import pallas as pl
from jax.experimental.pallas import tpu as pltpu

F32 = jnp.float32
BF16 = jnp.bfloat16
EPS = 1e-6
NEG_INF = -1e30
N_A = 2
DEPTH = 4
N_GROUPS = 3
DILATIONS = (1, 4, 16)
HEAD_DIM = 128
BLK = 128
LANES = 128
N_CHIPS = 4
N_DEV = 8
VMEM_LIMIT_V7X = 56 * 1024 * 1024

ADAM_LR = 0.001
ADAM_B1 = 0.9
ADAM_B2 = 0.999
ADAM_EPS = 1e-08
ADAM_WD = 0.01
ADAM_STEP = 10

MESH = pl.DeviceIdType.MESH


def _cp(*sem):
    return pltpu.CompilerParams(dimension_semantics=sem if sem else None, vmem_limit_bytes=VMEM_LIMIT_V7X)


def _dot(a, b):
    return jnp.dot(a, b, preferred_element_type=F32)


def _dot_nt(a, b):
    return lax.dot_general(a, b, (((1,), (1,)), ((), ())), preferred_element_type=F32)


def _dot_tn(a, b):
    return lax.dot_general(a, b, (((0,), (0,)), ((), ())), preferred_element_type=F32)


def _sigmoid(x):
    return 1.0 / (1.0 + jnp.exp(-x))


def _row_tile(n, want):
    if n <= want:
        return n
    for t in range(want - want % 8, 7, -8):
        if n % t == 0:
            return t
    raise ValueError(f"no row tile for {n} rows")


def mm_nn(a, w, nsh, stride, layer, bias=None, out_dtype=F32, name="mm_nn", tm=512):
    m, k = a.shape
    _, k2, ns = w.shape
    assert k == k2
    tm = _row_tile(m, tm)
    has_bias = bias is not None

    def body(*refs):
        if has_bias:
            a_ref, w_ref, b_ref, o_ref = refs
        else:
            a_ref, w_ref, o_ref = refs
        acc = _dot(a_ref[...].astype(BF16), w_ref[...])
        if has_bias:
            acc = acc + b_ref[...]
        o_ref[...] = acc.astype(out_dtype)

    in_specs = [
        pl.BlockSpec((tm, k), lambda j, i: (i, 0)),
        pl.BlockSpec((None, k, ns), lambda j, i: (j * stride + layer, 0, 0)),
    ]
    args = [a, w]
    if has_bias:
        in_specs.append(pl.BlockSpec((1, ns), lambda j, i: (0, j)))
        args.append(bias)
    return pl.pallas_call(
        body,
        name=name,
        grid=(nsh, m // tm),
        in_specs=in_specs,
        out_specs=pl.BlockSpec((tm, ns), lambda j, i: (i, j)),
        out_shape=jax.ShapeDtypeStruct((m, nsh * ns), out_dtype),
        compiler_params=_cp("parallel", "parallel"),
    )(*args)


def mm_nt(dy, w, nsh, stride, layer, out_dtype=F32, name="mm_nt", tm=512):
    m, n = dy.shape
    _, k, ns = w.shape
    assert n == nsh * ns
    tm = _row_tile(m, tm)

    def body(dy_ref, w_ref, o_ref, acc_ref):
        j = pl.program_id(1)
        part = _dot_nt(dy_ref[...].astype(BF16), w_ref[...])

        @pl.when(j == 0)
        def _():
            acc_ref[...] = part

        @pl.when(j > 0)
        def _():
            acc_ref[...] += part

        @pl.when(j == nsh - 1)
        def _():
            o_ref[...] = acc_ref[...].astype(out_dtype)

    return pl.pallas_call(
        body,
        name=name,
        grid=(m // tm, nsh),
        in_specs=[
            pl.BlockSpec((tm, ns), lambda i, j: (i, j)),
            pl.BlockSpec((None, k, ns), lambda i, j: (j * stride + layer, 0, 0)),
        ],
        out_specs=pl.BlockSpec((tm, k), lambda i, j: (i, 0)),
        out_shape=jax.ShapeDtypeStruct((m, k), out_dtype),
        scratch_shapes=[pltpu.VMEM((tm, k), F32)],
        compiler_params=_cp("parallel", "arbitrary"),
    )(dy, w)


def mm_tn(a, dy, nsh, name="mm_tn", tm=512):
    m, k = a.shape
    _, n = dy.shape
    ns = n // nsh
    tm = _row_tile(m, tm)
    nt = m // tm

    def body(a_ref, dy_ref, o_ref, acc_ref):
        i = pl.program_id(1)
        part = _dot_tn(a_ref[...].astype(BF16), dy_ref[...].astype(BF16))

        @pl.when(i == 0)
        def _():
            acc_ref[...] = part

        @pl.when(i > 0)
        def _():
            acc_ref[...] += part

        @pl.when(i == nt - 1)
        def _():
            o_ref[...] = acc_ref[...].astype(BF16)

    return pl.pallas_call(
        body,
        name=name,
        grid=(nsh, nt),
        in_specs=[
            pl.BlockSpec((tm, k), lambda j, i: (i, 0)),
            pl.BlockSpec((tm, ns), lambda j, i: (i, j)),
        ],
        out_specs=pl.BlockSpec((None, k, ns), lambda j, i: (j, 0, 0)),
        out_shape=jax.ShapeDtypeStruct((nsh, k, ns), BF16),
        scratch_shapes=[pltpu.VMEM((k, ns), F32)],
        compiler_params=_cp("parallel", "arbitrary"),
    )(a, dy)


DEP_SPEC_SHAPE = (8, LANES)


def resid_norm_fwd(x, y, g_post, next_gains, name, tm=256, dep=None):
    t, d = x.shape
    tm = _row_tile(t, tm)
    has_y = y is not None
    n_next = len(next_gains)
    n_dep = 0 if dep is None else 1

    def body(*refs):
        x_ref = refs[0]
        pos = 1
        if has_y:
            y_ref, gp_ref = refs[1], refs[2]
            pos = 3
        gn_refs = refs[pos:pos + n_next]
        outs = refs[pos + n_next + n_dep:]
        xv = x_ref[...]
        o = 0
        if has_y:
            yv = y_ref[...]
            r = lax.rsqrt(jnp.mean(yv * yv, axis=-1, keepdims=True) + EPS)
            xv = xv + (yv * r) * gp_ref[...]
            outs[0][...] = xv
            o = 1
        if n_next:
            xn = xv * lax.rsqrt(jnp.mean(xv * xv, axis=-1, keepdims=True) + EPS)
            for k in range(n_next):
                outs[o + k][...] = (xn * gn_refs[k][...]).astype(BF16)

    row = pl.BlockSpec((tm, d), lambda i: (i, 0))
    vec = pl.BlockSpec((1, d), lambda i: (0, 0))
    args, in_specs = [x], [row]
    if has_y:
        args += [y, g_post]
        in_specs += [row, vec]
    args += list(next_gains)
    in_specs += [vec] * n_next
    if n_dep:
        args.append(dep)
        in_specs.append(pl.BlockSpec(DEP_SPEC_SHAPE, lambda i: (0, 0)))
    out_shape, out_specs = [], []
    if has_y:
        out_shape.append(jax.ShapeDtypeStruct((t, d), F32))
        out_specs.append(row)
    for _ in range(n_next):
        out_shape.append(jax.ShapeDtypeStruct((t, d), BF16))
        out_specs.append(row)
    return pl.pallas_call(
        body, name=name, grid=(t // tm,), in_specs=in_specs, out_specs=out_specs, out_shape=out_shape,
        compiler_params=_cp("parallel"),
    )(*args)


def norm_bwd(x, g, dy, add=None, out_dtype=F32, name="norm_bwd", tm=256, dep=None):
    t, d = x.shape
    tm = _row_tile(t, tm)
    has_add = add is not None

    def body(*refs):
        x_ref, g_ref, dy_ref = refs[:3]
        add_ref = refs[3] if has_add else None
        dx_ref, dg_ref, cs_ref = refs[-3:]
        i = pl.program_id(0)
        xv = x_ref[...]
        dyv = dy_ref[...].astype(F32)
        r = lax.rsqrt(jnp.mean(xv * xv, axis=-1, keepdims=True) + EPS)
        gd = dyv * g_ref[...]
        dx = r * gd - xv * ((r * r * r) * jnp.mean(xv * gd, axis=-1, keepdims=True))
        if has_add:
            dx = dx + add_ref[...]
        dx_ref[...] = dx.astype(out_dtype)
        dg = jnp.sum(dyv * (xv * r), axis=0, keepdims=True)
        cs = jnp.sum(dx, axis=0, keepdims=True)

        @pl.when(i == 0)
        def _():
            dg_ref[...] = dg
            cs_ref[...] = cs

        @pl.when(i > 0)
        def _():
            dg_ref[...] += dg
            cs_ref[...] += cs

    row = pl.BlockSpec((tm, d), lambda i: (i, 0))
    vec = pl.BlockSpec((1, d), lambda i: (0, 0))
    args, in_specs = [x, g, dy], [row, vec, row]
    if has_add:
        args.append(add)
        in_specs.append(row)
    if dep is not None:
        args.append(dep)
        in_specs.append(pl.BlockSpec(DEP_SPEC_SHAPE, lambda i: (0, 0)))
    return pl.pallas_call(
        body, name=name, grid=(t // tm,), in_specs=in_specs,
        out_specs=[row, vec, vec],
        out_shape=[jax.ShapeDtypeStruct((t, d), out_dtype), jax.ShapeDtypeStruct((1, d), F32),
                   jax.ShapeDtypeStruct((1, d), F32)],
        compiler_params=_cp("arbitrary"),
    )(*args)


def loss_fwd_bwd(x, target, name="loss", tm=256):
    t, d = x.shape
    tm = _row_tile(t, tm)

    def body(x_ref, t_ref, dx_ref, l_ref):
        i = pl.program_id(0)
        err = x_ref[...] - t_ref[...]
        dx_ref[...] = err * (1.0 / d)
        part = 0.5 * jnp.sum(jnp.mean(err * err, axis=-1, keepdims=True), axis=0, keepdims=True)
        part = jnp.broadcast_to(part, l_ref.shape)

        @pl.when(i == 0)
        def _():
            l_ref[...] = part

        @pl.when(i > 0)
        def _():
            l_ref[...] += part

    row = pl.BlockSpec((tm, d), lambda i: (i, 0))
    return pl.pallas_call(
        body, name=name, grid=(t // tm,), in_specs=[row, row],
        out_specs=[row, pl.BlockSpec((8, LANES), lambda i: (0, 0))],
        out_shape=[jax.ShapeDtypeStruct((t, d), F32), jax.ShapeDtypeStruct((8, LANES), F32)],
        compiler_params=_cp("arbitrary"),
    )(x, target)


CONV_HALO = 32
CONV_CHUNK = 128


def glu_conv_fwd(z, dw, dwb, name, tc=128):
    b, s, c2 = z.shape
    c = c2 // 2
    kw = dw.shape[0]
    tc = min(tc, c)
    nc = c // tc
    ch = min(CONV_CHUNK, s)
    halo = CONV_HALO
    assert kw - 1 <= halo and s % ch == 0

    def body(a_ref, g_ref, w_ref, b_ref, o_ref, pad_ref):
        pad_ref[0:halo, :] = jnp.zeros((halo, tc), F32)

        def fill(ci, carry):
            r0 = pl.multiple_of(ci * ch, ch)
            pad_ref[pl.ds(halo + r0, ch), :] = a_ref[pl.ds(r0, ch), :] * _sigmoid(g_ref[pl.ds(r0, ch), :])
            return carry

        lax.fori_loop(0, s // ch, fill, 0)

        def chunk(ci, carry):
            r0 = pl.multiple_of(ci * ch, ch)
            win = pad_ref[pl.ds(r0, ch + halo), :]
            acc = b_ref[...] + w_ref[kw - 1:kw, :] * win[halo:, :]
            for k in range(kw - 2, -1, -1):
                win = pltpu.roll(win, 1, 0)
                acc = acc + w_ref[k:k + 1, :] * win[halo:, :]
            o_ref[pl.ds(r0, ch), :] = acc
            return carry

        lax.fori_loop(0, s // ch, chunk, 0)

    return pl.pallas_call(
        body, name=name, grid=(b, nc),
        in_specs=[
            pl.BlockSpec((None, s, tc), lambda bi, i: (bi, 0, i)),
            pl.BlockSpec((None, s, tc), lambda bi, i: (bi, 0, i + nc)),
            pl.BlockSpec((kw, tc), lambda bi, i: (0, i)),
            pl.BlockSpec((1, tc), lambda bi, i: (0, i)),
        ],
        out_specs=pl.BlockSpec((None, s, tc), lambda bi, i: (bi, 0, i)),
        out_shape=jax.ShapeDtypeStruct((b, s, c), F32),
        scratch_shapes=[pltpu.VMEM((s + halo, tc), F32)],
        compiler_params=_cp("parallel", "parallel"),
    )(z, z, dw, dwb)


def glu_conv_bwd(z, dw, du2, name, tc=128):
    b, s, c2 = z.shape
    c = c2 // 2
    kw = dw.shape[0]
    tc = min(tc, c)
    nc = c // tc
    ch = min(CONV_CHUNK, s)
    halo = CONV_HALO

    def body(a_ref, g_ref, w_ref, du_ref, dza_ref, dzg_ref, ddw_ref, ddwb_ref, dba_ref, dbg_ref, upad_ref, dpad_ref):
        bi = pl.program_id(1)

        @pl.when(bi == 0)
        def _():
            ddw_ref[...] = jnp.zeros(ddw_ref.shape, F32)
            ddwb_ref[...] = jnp.zeros(ddwb_ref.shape, F32)
            dba_ref[...] = jnp.zeros(dba_ref.shape, F32)
            dbg_ref[...] = jnp.zeros(dbg_ref.shape, F32)

        upad_ref[0:halo, :] = jnp.zeros((halo, tc), F32)
        dpad_ref[s:s + halo, :] = jnp.zeros((halo, tc), F32)

        def fill(ci, carry):
            r0 = pl.multiple_of(ci * ch, ch)
            upad_ref[pl.ds(halo + r0, ch), :] = a_ref[pl.ds(r0, ch), :] * _sigmoid(g_ref[pl.ds(r0, ch), :])
            dpad_ref[pl.ds(r0, ch), :] = du_ref[pl.ds(r0, ch), :]
            return carry

        lax.fori_loop(0, s // ch, fill, 0)

        def chunk(ci, carry):
            r0 = pl.multiple_of(ci * ch, ch)
            du_c = du_ref[pl.ds(r0, ch), :]
            wd = dpad_ref[pl.ds(r0, ch + halo), :]
            wu = upad_ref[pl.ds(r0, ch + halo), :]
            du1 = w_ref[kw - 1:kw, :] * wd[:ch, :]
            ddw_ref[kw - 1] += jnp.sum((du_c * wu[halo:, :]).reshape(ch // 8, 8, tc), axis=0)
            for j in range(1, kw):
                wd = pltpu.roll(wd, ch + halo - 1, 0)
                wu = pltpu.roll(wu, 1, 0)
                du1 = du1 + w_ref[kw - 1 - j:kw - j, :] * wd[:ch, :]
                ddw_ref[kw - 1 - j] += jnp.sum((du_c * wu[halo:, :]).reshape(ch // 8, 8, tc), axis=0)
            av = a_ref[pl.ds(r0, ch), :]
            sg = _sigmoid(g_ref[pl.ds(r0, ch), :])
            dza = du1 * sg
            dzg = du1 * av * (sg * (1.0 - sg))
            dza_ref[pl.ds(r0, ch), :] = dza.astype(BF16)
            dzg_ref[pl.ds(r0, ch), :] = dzg.astype(BF16)
            dba_ref[...] += jnp.sum(dza, axis=0, keepdims=True)
            dbg_ref[...] += jnp.sum(dzg, axis=0, keepdims=True)
            ddwb_ref[...] += jnp.sum(du_c, axis=0, keepdims=True)
            return carry

        lax.fori_loop(0, s // ch, chunk, 0)

    blk = lambda off: pl.BlockSpec((None, s, tc), lambda i, bi: (bi, 0, i + off))
    vec = pl.BlockSpec((1, tc), lambda i, bi: (0, i))
    return pl.pallas_call(
        body, name=name, grid=(nc, b),
        in_specs=[blk(0), blk(nc), pl.BlockSpec((kw, tc), lambda i, bi: (0, i)), blk(0)],
        out_specs=[blk(0), blk(0), pl.BlockSpec((kw, 8, tc), lambda i, bi: (0, 0, i)), vec, vec, vec],
        out_shape=[
            jax.ShapeDtypeStruct((b, s, c), BF16), jax.ShapeDtypeStruct((b, s, c), BF16),
            jax.ShapeDtypeStruct((kw, 8, c), F32), jax.ShapeDtypeStruct((1, c), F32),
            jax.ShapeDtypeStruct((1, c), F32), jax.ShapeDtypeStruct((1, c), F32),
        ],
        scratch_shapes=[pltpu.VMEM((s + halo, tc), F32), pltpu.VMEM((s + halo, tc), F32)],
        compiler_params=_cp("parallel", "arbitrary"),
    )(z, z, dw, du2)


def ln_silu_fwd(u, g, bvec, name, tm=256):
    t, d = u.shape
    tm = _row_tile(t, tm)

    def body(u_ref, g_ref, b_ref, o_ref):
        uv = u_ref[...]
        mu = jnp.mean(uv, axis=-1, keepdims=True)
        xc = uv - mu
        var = jnp.mean(xc * xc, axis=-1, keepdims=True)
        v = (xc * lax.rsqrt(var + EPS)) * g_ref[...] + b_ref[...]
        o_ref[...] = (v * _sigmoid(v)).astype(BF16)

    row = pl.BlockSpec((tm, d), lambda i: (i, 0))
    vec = pl.BlockSpec((1, d), lambda i: (0, 0))
    return pl.pallas_call(
        body, name=name, grid=(t // tm,), in_specs=[row, vec, vec], out_specs=row,
        out_shape=jax.ShapeDtypeStruct((t, d), BF16), compiler_params=_cp("parallel"),
    )(u, g, bvec)


def ln_silu_bwd(u, g, bvec, dout, name, tm=256):
    t, d = u.shape
    tm = _row_tile(t, tm)

    def body(u_ref, g_ref, b_ref, do_ref, du_ref, dg_ref, db_ref):
        i = pl.program_id(0)
        uv = u_ref[...]
        mu = jnp.mean(uv, axis=-1, keepdims=True)
        xc = uv - mu
        var = jnp.mean(xc * xc, axis=-1, keepdims=True)
        rstd = lax.rsqrt(var + EPS)
        n = xc * rstd
        v = n * g_ref[...] + b_ref[...]
        sg = _sigmoid(v)
        dv = do_ref[...].astype(F32) * (sg * (1.0 + v * (1.0 - sg)))
        dn = dv * g_ref[...]
        du_ref[...] = rstd * (dn - jnp.mean(dn, axis=-1, keepdims=True) - n * jnp.mean(dn * n, axis=-1, keepdims=True))
        dg = jnp.sum(dv * n, axis=0, keepdims=True)
        db = jnp.sum(dv, axis=0, keepdims=True)

        @pl.when(i == 0)
        def _():
            dg_ref[...] = dg
            db_ref[...] = db

        @pl.when(i > 0)
        def _():
            dg_ref[...] += dg
            db_ref[...] += db

    row = pl.BlockSpec((tm, d), lambda i: (i, 0))
    vec = pl.BlockSpec((1, d), lambda i: (0, 0))
    return pl.pallas_call(
        body, name=name, grid=(t // tm,), in_specs=[row, vec, vec, row], out_specs=[row, vec, vec],
        out_shape=[jax.ShapeDtypeStruct((t, d), F32), jax.ShapeDtypeStruct((1, d), F32), jax.ShapeDtypeStruct((1, d), F32)],
        compiler_params=_cp("arbitrary"),
    )(u, g, bvec, dout)


FFN_HALO = 8


def _conv3(win, w_ref, b_ref, ch):
    kw = w_ref.shape[0]
    acc = b_ref[...] + w_ref[kw - 1:kw, :] * win[FFN_HALO:, :]
    for k in range(kw - 2, -1, -1):
        win = pltpu.roll(win, 1, 0)
        acc = acc + w_ref[k:k + 1, :] * win[FFN_HALO:, :]
    return acc


def ffn_mid_fwd(p, dw, dwb, name, tc=256):
    b, s, f2 = p.shape
    f = f2 // 2
    kw = dw.shape[0]
    tc = min(tc, f)
    nf = f // tc
    ch = min(CONV_CHUNK, s)
    halo = FFN_HALO

    def body(pa_ref, pg_ref, wa_ref, wg_ref, ba_ref, bg_ref, o_ref, apad_ref, gpad_ref):
        apad_ref[0:halo, :] = jnp.zeros((halo, tc), F32)
        gpad_ref[0:halo, :] = jnp.zeros((halo, tc), F32)

        def fill(ci, carry):
            r0 = pl.multiple_of(ci * ch, ch)
            apad_ref[pl.ds(halo + r0, ch), :] = pa_ref[pl.ds(r0, ch), :]
            gpad_ref[pl.ds(halo + r0, ch), :] = pg_ref[pl.ds(r0, ch), :]
            return carry

        lax.fori_loop(0, s // ch, fill, 0)

        def chunk(ci, carry):
            r0 = pl.multiple_of(ci * ch, ch)
            ca = _conv3(apad_ref[pl.ds(r0, ch + halo), :], wa_ref, ba_ref, ch)
            cg = _conv3(gpad_ref[pl.ds(r0, ch + halo), :], wg_ref, bg_ref, ch)
            o_ref[pl.ds(r0, ch), :] = ((cg * _sigmoid(cg)) * ca).astype(BF16)
            return carry

        lax.fori_loop(0, s // ch, chunk, 0)

    blk = lambda off: pl.BlockSpec((None, s, tc), lambda bi, i: (bi, 0, i + off))
    wsp = lambda off: pl.BlockSpec((kw, tc), lambda bi, i: (0, i + off))
    bsp = lambda off: pl.BlockSpec((1, tc), lambda bi, i: (0, i + off))
    return pl.pallas_call(
        body, name=name, grid=(b, nf),
        in_specs=[blk(0), blk(nf), wsp(0), wsp(nf), bsp(0), bsp(nf)],
        out_specs=pl.BlockSpec((None, s, tc), lambda bi, i: (bi, 0, i)),
        out_shape=jax.ShapeDtypeStruct((b, s, f), BF16),
        scratch_shapes=[pltpu.VMEM((s + halo, tc), F32), pltpu.VMEM((s + halo, tc), F32)],
        compiler_params=_cp("parallel", "parallel"),
    )(p, p, dw, dw, dwb, dwb)


def ffn_mid_bwd(p, dw, dwb, ds, name, tc=256):
    b, s, f2 = p.shape
    f = f2 // 2
    kw = dw.shape[0]
    tc = min(tc, f)
    nf = f // tc
    ch = min(CONV_CHUNK, s)
    halo = FFN_HALO

    def body(pa_ref, pg_ref, wa_ref, wg_ref, ba_ref, bg_ref, ds_ref, dp_ref, ddw_ref, ddwb_ref,
             apad_ref, gpad_ref, dcpad_ref):
        hf = pl.program_id(0)
        bi = pl.program_id(2)
        gate_half = hf == 1

        @pl.when(bi == 0)
        def _():
            ddw_ref[...] = jnp.zeros(ddw_ref.shape, F32)
            ddwb_ref[...] = jnp.zeros(ddwb_ref.shape, F32)

        apad_ref[0:halo, :] = jnp.zeros((halo, tc), F32)
        gpad_ref[0:halo, :] = jnp.zeros((halo, tc), F32)
        dcpad_ref[s:s + halo, :] = jnp.zeros((halo, tc), F32)

        def fill(ci, carry):
            r0 = pl.multiple_of(ci * ch, ch)
            apad_ref[pl.ds(halo + r0, ch), :] = pa_ref[pl.ds(r0, ch), :]
            gpad_ref[pl.ds(halo + r0, ch), :] = pg_ref[pl.ds(r0, ch), :]
            return carry

        lax.fori_loop(0, s // ch, fill, 0)

        def grads(ci, carry):
            r0 = pl.multiple_of(ci * ch, ch)
            wa_win = apad_ref[pl.ds(r0, ch + halo), :]
            wg_win = gpad_ref[pl.ds(r0, ch + halo), :]
            ca = _conv3(wa_win, wa_ref, ba_ref, ch)
            cg = _conv3(wg_win, wg_ref, bg_ref, ch)
            sg = _sigmoid(cg)
            dsv = ds_ref[pl.ds(r0, ch), :].astype(F32)
            dc = jnp.where(gate_half, dsv * ca * (sg * (1.0 + cg * (1.0 - sg))), dsv * (cg * sg))
            dcpad_ref[pl.ds(r0, ch), :] = dc
            win = jnp.where(gate_half, wg_win, wa_win)
            ddw_ref[kw - 1] += jnp.sum((dc * win[halo:, :]).reshape(ch // 8, 8, tc), axis=0)
            for j in range(1, kw):
                win = pltpu.roll(win, 1, 0)
                ddw_ref[kw - 1 - j] += jnp.sum((dc * win[halo:, :]).reshape(ch // 8, 8, tc), axis=0)
            ddwb_ref[...] += jnp.sum(dc, axis=0, keepdims=True)
            return carry

        lax.fori_loop(0, s // ch, grads, 0)

        def back(ci, carry):
            r0 = pl.multiple_of(ci * ch, ch)
            wd = dcpad_ref[pl.ds(r0, ch + halo), :]
            acc = jnp.where(gate_half, wg_ref[kw - 1:kw, :], wa_ref[kw - 1:kw, :]) * wd[:ch, :]
            for j in range(1, kw):
                wd = pltpu.roll(wd, ch + halo - 1, 0)
                acc = acc + jnp.where(gate_half, wg_ref[kw - 1 - j:kw - j, :], wa_ref[kw - 1 - j:kw - j, :]) * wd[:ch, :]
            dp_ref[pl.ds(r0, ch), :] = acc.astype(BF16)
            return carry

        lax.fori_loop(0, s // ch, back, 0)

    blk = lambda off: pl.BlockSpec((None, s, tc), lambda hf, i, bi: (bi, 0, i + off))
    wsp = lambda off: pl.BlockSpec((kw, tc), lambda hf, i, bi: (0, i + off))
    bsp = lambda off: pl.BlockSpec((1, tc), lambda hf, i, bi: (0, i + off))
    return pl.pallas_call(
        body, name=name, grid=(2, nf, b),
        in_specs=[blk(0), blk(nf), wsp(0), wsp(nf), bsp(0), bsp(nf), blk(0)],
        out_specs=[
            pl.BlockSpec((None, s, tc), lambda hf, i, bi: (bi, 0, hf * nf + i)),
            pl.BlockSpec((kw, 8, tc), lambda hf, i, bi: (0, 0, hf * nf + i)),
            pl.BlockSpec((1, tc), lambda hf, i, bi: (0, hf * nf + i)),
        ],
        out_shape=[jax.ShapeDtypeStruct((b, s, f2), BF16), jax.ShapeDtypeStruct((kw, 8, f2), F32),
                   jax.ShapeDtypeStruct((1, f2), F32)],
        scratch_shapes=[pltpu.VMEM((s + halo, tc), F32)] * 3,
        compiler_params=_cp("parallel", "parallel", "arbitrary"),
    )(p, p, dw, dw, dwb, dwb, ds)


def perm_rows(x, col_blk, width, dil, inverse, out_dtype, name, tc=LANES):
    b, s, _ = x.shape
    tc = min(tc, width)
    nc = width // tc
    ln = s // dil

    def body(x_ref, o_ref):
        if dil == 1:
            o_ref[...] = x_ref[...].astype(out_dtype)
            return
        for r in range(dil):
            if inverse:
                o_ref[pl.ds(r, ln, stride=dil), :] = x_ref[r * ln:(r + 1) * ln, :].astype(out_dtype)
            else:
                o_ref[r * ln:(r + 1) * ln, :] = x_ref[pl.ds(r, ln, stride=dil), :].astype(out_dtype)

    return pl.pallas_call(
        body, name=name, grid=(b, nc),
        in_specs=[pl.BlockSpec((None, s, tc), lambda bi, i: (bi, 0, col_blk * nc + i))],
        out_specs=pl.BlockSpec((None, s, tc), lambda bi, i: (bi, 0, i)),
        out_shape=jax.ShapeDtypeStruct((b, s, width), out_dtype),
        compiler_params=_cp("parallel", "parallel"),
    )(x)


def _attn_masks(has_other):
    qi = lax.broadcasted_iota(jnp.int32, (BLK, BLK), 0)
    kk = lax.broadcasted_iota(jnp.int32, (BLK, BLK), 1)
    return kk <= qi, jnp.logical_and(kk >= qi, has_other)


def attn_fwd(q, q_blk, k, k_blk, v, v_blk, nblk, hw, name):
    b, s, _ = q.shape
    nh = hw // HEAD_DIM
    scale = 1.0 / math.sqrt(HEAD_DIM)

    def body(q_ref, kc_ref, kp_ref, vc_ref, vp_ref, o_ref, lse_ref):
        t = pl.program_id(1)
        mask_c, mask_p = _attn_masks((t % nblk) != 0)
        lane = lax.broadcasted_iota(jnp.int32, (BLK, LANES), 1)
        lse_acc = jnp.zeros((BLK, LANES), F32)
        for h in range(nh):
            sl = slice(h * HEAD_DIM, (h + 1) * HEAD_DIM)
            qh = q_ref[:, sl].astype(BF16)
            kc = kc_ref[:, sl].astype(BF16)
            kp = kp_ref[:, sl].astype(BF16)
            s_c = jnp.where(mask_c, _dot_nt(qh, kc) * scale, NEG_INF)
            s_p = jnp.where(mask_p, _dot_nt(qh, kp) * scale, NEG_INF)
            m = jnp.maximum(jnp.max(s_c, axis=-1, keepdims=True), jnp.max(s_p, axis=-1, keepdims=True))
            p_c = jnp.exp(s_c - m)
            p_p = jnp.exp(s_p - m)
            den = jnp.sum(p_c, axis=-1, keepdims=True) + jnp.sum(p_p, axis=-1, keepdims=True)
            o = _dot(p_c.astype(BF16), vc_ref[:, sl].astype(BF16)) + _dot(p_p.astype(BF16), vp_ref[:, sl].astype(BF16))
            o_ref[:, sl] = o / den
            lse_acc = jnp.where(lane == h, m + jnp.log(den), lse_acc)
        lse_ref[...] = lse_acc

    cur = lambda cb: pl.BlockSpec((None, BLK, hw), lambda bi, t: (bi, t, cb))
    prev = lambda cb: pl.BlockSpec((None, BLK, hw), lambda bi, t: (bi, jnp.where(t % nblk == 0, t, t - 1), cb))
    return pl.pallas_call(
        body, name=name, grid=(b, s // BLK),
        in_specs=[cur(q_blk), cur(k_blk), prev(k_blk), cur(v_blk), prev(v_blk)],
        out_specs=[pl.BlockSpec((None, BLK, hw), lambda bi, t: (bi, t, 0)),
                   pl.BlockSpec((None, BLK, LANES), lambda bi, t: (bi, t, 0))],
        out_shape=[jax.ShapeDtypeStruct((b, s, hw), F32), jax.ShapeDtypeStruct((b, s, LANES), F32)],
        compiler_params=_cp("parallel", "parallel"),
    )(q, k, k, v, v)


def attn_merge(outs, lses, name, tm=256):
    t, hw = outs[0].shape
    nh = hw // HEAD_DIM
    tm = _row_tile(t, tm)
    ng = len(outs)

    def body(*refs):
        o_refs, l_refs = refs[:ng], refs[ng:2 * ng]
        m_ref, lj_ref = refs[2 * ng:]
        lane = lax.broadcasted_iota(jnp.int32, (tm, LANES), 1)
        lj = jnp.zeros((tm, LANES), F32)
        for h in range(nh):
            sl = slice(h * HEAD_DIM, (h + 1) * HEAD_DIM)
            ls = [l_refs[g][:, h:h + 1] for g in range(ng)]
            mx = functools.reduce(jnp.maximum, ls)
            es = [jnp.exp(l - mx) for l in ls]
            tot = functools.reduce(lambda a, c: a + c, es)
            acc = (es[0] / tot) * o_refs[0][:, sl]
            for g in range(1, ng):
                acc = acc + (es[g] / tot) * o_refs[g][:, sl]
            m_ref[:, sl] = acc.astype(BF16)
            lj = jnp.where(lane == h, mx + jnp.log(tot), lj)
        lj_ref[...] = lj

    row = pl.BlockSpec((tm, hw), lambda i: (i, 0))
    st = pl.BlockSpec((tm, LANES), lambda i: (i, 0))
    return pl.pallas_call(
        body, name=name, grid=(t // tm,), in_specs=[row] * ng + [st] * ng, out_specs=[row, st],
        out_shape=[jax.ShapeDtypeStruct((t, hw), BF16), jax.ShapeDtypeStruct((t, LANES), F32)],
        compiler_params=_cp("parallel"),
    )(*outs, *lses)


def attn_bwd_prep(dmerged, merged, name, tm=256):
    t, hw = merged.shape
    nh = hw // HEAD_DIM
    tm = _row_tile(t, tm)

    def body(d_ref, m_ref, o_ref):
        lane = lax.broadcasted_iota(jnp.int32, (tm, LANES), 1)
        acc = jnp.zeros((tm, LANES), F32)
        for h in range(nh):
            sl = slice(h * HEAD_DIM, (h + 1) * HEAD_DIM)
            dsum = jnp.sum(d_ref[:, sl] * m_ref[:, sl].astype(F32), axis=-1, keepdims=True)
            acc = jnp.where(lane == h, dsum, acc)
        o_ref[...] = acc

    row = pl.BlockSpec((tm, hw), lambda i: (i, 0))
    return pl.pallas_call(
        body, name=name, grid=(t // tm,), in_specs=[row, row], out_specs=pl.BlockSpec((tm, LANES), lambda i: (i, 0)),
        out_shape=jax.ShapeDtypeStruct((t, LANES), F32), compiler_params=_cp("parallel"),
    )(dmerged, merged)


def attn_bwd(q, q_blk, k, k_blk, v, v_blk, do, lsej, dm, dk_add, dv_add, nblk, hw, name):
    b, s, _ = q.shape
    nh = hw // HEAD_DIM
    scale = 1.0 / math.sqrt(HEAD_DIM)
    has_add = dk_add is not None

    def body(*refs):
        (qm_ref, qn_ref, kc_ref, kp_ref, vc_ref, vp_ref, dom_ref, don_ref, lm_ref, ln_ref, dmm_ref, dmn_ref) = refs[:12]
        if has_add:
            dka_ref, dva_ref = refs[12:14]
            dq_ref, dk_ref, dv_ref = refs[14:]
        else:
            dq_ref, dk_ref, dv_ref = refs[12:]
        t = pl.program_id(1)
        n = t % nblk
        mask_c, mask_p = _attn_masks(n != 0)
        _, mask_n = _attn_masks(n != nblk - 1)
        for h in range(nh):
            sl = slice(h * HEAD_DIM, (h + 1) * HEAD_DIM)
            qm = qm_ref[:, sl].astype(BF16)
            qn = qn_ref[:, sl].astype(BF16)
            kc = kc_ref[:, sl].astype(BF16)
            kp = kp_ref[:, sl].astype(BF16)
            vc = vc_ref[:, sl].astype(BF16)
            vp = vp_ref[:, sl].astype(BF16)
            dom = dom_ref[:, sl].astype(BF16)
            don = don_ref[:, sl].astype(BF16)
            lm, lnx = lm_ref[:, h:h + 1], ln_ref[:, h:h + 1]
            dmm, dmn = dmm_ref[:, h:h + 1], dmn_ref[:, h:h + 1]
            pa = jnp.exp(jnp.where(mask_c, _dot_nt(qm, kc) * scale, NEG_INF) - lm)
            pb = jnp.exp(jnp.where(mask_p, _dot_nt(qm, kp) * scale, NEG_INF) - lm)
            pc = jnp.exp(jnp.where(mask_n, _dot_nt(qn, kc) * scale, NEG_INF) - lnx)
            dsa = (pa * (_dot_nt(dom, vc) - dmm)).astype(BF16)
            dsb = (pb * (_dot_nt(dom, vp) - dmm)).astype(BF16)
            dsc = (pc * (_dot_nt(don, vc) - dmn)).astype(BF16)
            dq_ref[:, sl] = (_dot(dsa, kc) + _dot(dsb, kp)) * scale
            dk = (_dot_tn(dsa, qm) + _dot_tn(dsc, qn)) * scale
            dv = _dot_tn(pa.astype(BF16), dom) + _dot_tn(pc.astype(BF16), don)
            if has_add:
                dk = dk + dka_ref[:, sl]
                dv = dv + dva_ref[:, sl]
            dk_ref[:, sl] = dk
            dv_ref[:, sl] = dv

    def spec(cb, w, which):
        if which == "cur":
            return pl.BlockSpec((None, BLK, w), lambda bi, t: (bi, t, cb))
        if which == "prev":
            return pl.BlockSpec((None, BLK, w), lambda bi, t: (bi, jnp.where(t % nblk == 0, t, t - 1), cb))
        return pl.BlockSpec((None, BLK, w), lambda bi, t: (bi, jnp.where(t % nblk == nblk - 1, t, t + 1), cb))

    in_specs = [
        spec(q_blk, hw, "cur"), spec(q_blk, hw, "next"), spec(k_blk, hw, "cur"), spec(k_blk, hw, "prev"),
        spec(v_blk, hw, "cur"), spec(v_blk, hw, "prev"), spec(0, hw, "cur"), spec(0, hw, "next"),
        spec(0, LANES, "cur"), spec(0, LANES, "next"), spec(0, LANES, "cur"), spec(0, LANES, "next"),
    ]
    args = [q, q, k, k, v, v, do, do, lsej, lsej, dm, dm]
    if has_add:
        in_specs += [spec(0, hw, "cur"), spec(0, hw, "cur")]
        args += [dk_add, dv_add]
    out = pl.BlockSpec((None, BLK, hw), lambda bi, t: (bi, t, 0))
    return pl.pallas_call(
        body, name=name, grid=(b, s // BLK), in_specs=in_specs, out_specs=[out, out, out],
        out_shape=[jax.ShapeDtypeStruct((b, s, hw), F32)] * 3,
        compiler_params=_cp("parallel", "parallel"),
    )(*args)


def sum_parts(g, recv, me, name, tm=256):
    _, rows, c = g.shape
    n = recv.shape[0]
    tm = _row_tile(rows, tm)

    def body(me_ref, g_ref, r_ref, o_ref):
        acc = g_ref[...].astype(F32)
        for j in range(n):
            acc = acc + r_ref[j].astype(F32)
        o_ref[...] = acc

    return pl.pallas_call(
        body, name=name,
        grid_spec=pltpu.PrefetchScalarGridSpec(
            num_scalar_prefetch=1, grid=(rows // tm,),
            in_specs=[pl.BlockSpec((None, tm, c), lambda i, me_ref: (me_ref[0], i, 0)),
                      pl.BlockSpec((n, tm, c), lambda i, me_ref: (0, i, 0))],
            out_specs=pl.BlockSpec((tm, c), lambda i, me_ref: (i, 0))),
        out_shape=jax.ShapeDtypeStruct((rows, c), F32), compiler_params=_cp("parallel"),
    )(me, g, recv)


def adamw(w, m, v, g_parts, name, tm=256):
    rows, c = w.shape
    tm = _row_tile(rows, tm)
    npart = len(g_parts)

    def body(*refs):
        w_ref, m_ref, v_ref = refs[:3]
        g_refs = refs[3:3 + npart]
        go_ref, d_ref, mo_ref, vo_ref = refs[3 + npart:]
        g = g_refs[0][...]
        for k in range(1, npart):
            g = g + g_refs[k][...]
        mn = ADAM_B1 * m_ref[...] + (1.0 - ADAM_B1) * g
        vn = ADAM_B2 * v_ref[...] + (1.0 - ADAM_B2) * (g * g)
        m_hat = mn / (1.0 - ADAM_B1 ** ADAM_STEP)
        v_hat = vn / (1.0 - ADAM_B2 ** ADAM_STEP)
        go_ref[...] = g
        d_ref[...] = -ADAM_LR * (m_hat / (jnp.sqrt(v_hat) + ADAM_EPS) + ADAM_WD * w_ref[...])
        mo_ref[...] = mn
        vo_ref[...] = vn

    row = pl.BlockSpec((tm, c), lambda i: (i, 0))
    return pl.pallas_call(
        body, name=name, grid=(rows // tm,), in_specs=[row] * (3 + npart), out_specs=[row] * 4,
        out_shape=[jax.ShapeDtypeStruct((rows, c), F32)] * 4, compiler_params=_cp("parallel"),
    )(w, m, v, *g_parts)


def _place():
    return lax.axis_index("x"), lax.axis_index("y"), lax.axis_index("c")


def _other_chips(x, y, c):
    return [(1 - x, y, c), (x, 1 - y, c), (1 - x, 1 - y, c)]


def _chip_of(px, py):
    return 2 * px + py


HBM_SPEC = pl.BlockSpec(memory_space=pltpu.HBM)
SEM_SPEC = pl.BlockSpec(memory_space=pltpu.SEMAPHORE)
ANY_SPEC = pl.BlockSpec(memory_space=pl.ANY)
DATAFLOW = pltpu.SideEffectType.DATAFLOW_SIDE_EFFECTING
N_PEER_CHIPS = N_CHIPS - 1


def _hbm(a):
    return pltpu.with_memory_space_constraint(a, pltpu.HBM)


def _hbm_like(arrays):
    return [pltpu.HBM(a.shape, a.dtype) for a in arrays]


def place_own(shards, name="place_own"):
    n = len(shards)

    def body(*refs):
        ins, outs, sems = refs[:n], refs[n:2 * n], refs[2 * n]
        x, y, _ = _place()
        me = _chip_of(x, y)
        copies = [pltpu.make_async_copy(ins[k], outs[k].at[me], sems.at[k]) for k in range(n)]
        for cp in copies:
            cp.start()
        for cp in copies:
            cp.wait()

    return pl.pallas_call(
        body, name=name, in_specs=[ANY_SPEC] * n, out_specs=[ANY_SPEC] * n,
        out_shape=[jax.ShapeDtypeStruct((N_CHIPS, *s.shape), s.dtype) for s in shards],
        scratch_shapes=[pltpu.SemaphoreType.DMA((n,))],
    )(*shards)


def gather_start(shards, lands, chunk_sizes, name="gather_start"):
    n = len(shards)
    nch = len(chunk_sizes)
    assert sum(chunk_sizes) == n

    def body(*refs):
        src_refs, land_refs = refs[:n], refs[n:2 * n]
        outs = refs[2 * n:]
        send_sems, recv_sems = outs[:nch], outs[nch:2 * nch]
        token = outs[-1]
        x, y, c = _place()
        me = _chip_of(x, y)
        peers = _other_chips(x, y, c)
        k = 0
        for ck, size in enumerate(chunk_sizes):
            for pos in range(size):
                for r, peer in enumerate(peers):
                    pltpu.make_async_remote_copy(
                        src_ref=src_refs[k], dst_ref=land_refs[k].at[me],
                        send_sem=send_sems[ck].at[N_PEER_CHIPS * pos + r], recv_sem=recv_sems[ck].at[N_PEER_CHIPS * pos + r],
                        device_id=peer, device_id_type=MESH).start()
                k += 1
        token[...] = jnp.zeros(token.shape, F32)

    sems = [pltpu.SemaphoreType.DMA((N_PEER_CHIPS * s,)) for s in chunk_sizes]
    res = pl.pallas_call(
        body, name=name,
        out_shape=(*sems, *sems, *_hbm_like(shards), *_hbm_like(lands), jax.ShapeDtypeStruct(DEP_SPEC_SHAPE, F32)),
        in_specs=[HBM_SPEC] * (2 * n),
        out_specs=(*[SEM_SPEC] * (2 * nch), *[HBM_SPEC] * (2 * n), pl.BlockSpec(memory_space=pltpu.VMEM)),
        input_output_aliases={k: 2 * nch + k for k in range(2 * n)},
        compiler_params=pltpu.CompilerParams(has_side_effects=DATAFLOW),
    )(*[_hbm(a) for a in shards], *[_hbm(a) for a in lands])
    return res[:nch], res[nch:2 * nch], res[2 * nch:2 * nch + n], res[2 * nch + n:2 * nch + 2 * n], res[-1]


def gather_wait(send_sem, recv_sem, shards, lands, after, name):
    n = len(shards)

    def body(*refs):
        src_refs, land_refs = refs[:n], refs[n:2 * n]
        ssem, rsem = refs[2 * n], refs[2 * n + 1]
        x, y, c = _place()
        for pos in range(n):
            for r, peer in enumerate(_other_chips(x, y, c)):
                cp = pltpu.make_async_remote_copy(
                    src_ref=src_refs[pos], dst_ref=land_refs[pos].at[_chip_of(peer[0], peer[1])],
                    send_sem=ssem.at[N_PEER_CHIPS * pos + r], recv_sem=rsem.at[N_PEER_CHIPS * pos + r],
                    device_id=peer, device_id_type=MESH)
                cp.wait_send()
                cp.wait_recv()

    res = pl.pallas_call(
        body, name=name, out_shape=(*_hbm_like(shards), *_hbm_like(lands)),
        in_specs=[*[HBM_SPEC] * (2 * n), SEM_SPEC, SEM_SPEC, ANY_SPEC], out_specs=[HBM_SPEC] * (2 * n),
        input_output_aliases={k: k for k in range(2 * n)},
        compiler_params=pltpu.CompilerParams(has_side_effects=DATAFLOW),
    )(*shards, *lands, send_sem, recv_sem, after)
    return res[n:]


def scatter_start(grads, name):
    n = len(grads)
    recvs = [lax.empty((N_PEER_CHIPS, *g.shape[1:]), g.dtype) for g in grads]

    def body(*refs):
        g_refs, r_refs = refs[:n], refs[n:2 * n]
        send_sems, recv_sems = refs[2 * n], refs[2 * n + 1]
        token = refs[-1]
        x, y, c = _place()
        for k in range(n):
            for r, peer in enumerate(_other_chips(x, y, c)):
                pltpu.make_async_remote_copy(
                    src_ref=g_refs[k].at[_chip_of(peer[0], peer[1])], dst_ref=r_refs[k].at[r],
                    send_sem=send_sems.at[N_PEER_CHIPS * k + r], recv_sem=recv_sems.at[N_PEER_CHIPS * k + r],
                    device_id=peer, device_id_type=MESH).start()
        token[...] = jnp.zeros(token.shape, F32)

    sem = pltpu.SemaphoreType.DMA((N_PEER_CHIPS * n,))
    res = pl.pallas_call(
        body, name=name,
        out_shape=(sem, sem, *_hbm_like(grads), *_hbm_like(recvs), jax.ShapeDtypeStruct(DEP_SPEC_SHAPE, F32)),
        in_specs=[HBM_SPEC] * (2 * n),
        out_specs=(SEM_SPEC, SEM_SPEC, *[HBM_SPEC] * (2 * n), pl.BlockSpec(memory_space=pltpu.VMEM)),
        input_output_aliases={k: 2 + k for k in range(2 * n)},
        compiler_params=pltpu.CompilerParams(has_side_effects=DATAFLOW),
    )(*[_hbm(a) for a in grads], *[_hbm(a) for a in recvs])
    return res[0], res[1], res[2:2 + n], res[2 + n:2 + 2 * n], res[-1]


def scatter_wait(send_sem, recv_sem, grads, recvs, after, name):
    n = len(grads)

    def body(*refs):
        g_refs, r_refs = refs[:n], refs[n:2 * n]
        ssem, rsem = refs[2 * n], refs[2 * n + 1]
        x, y, c = _place()
        for k in range(n):
            for r, peer in enumerate(_other_chips(x, y, c)):
                cp = pltpu.make_async_remote_copy(
                    src_ref=g_refs[k].at[_chip_of(peer[0], peer[1])], dst_ref=r_refs[k].at[r],
                    send_sem=ssem.at[N_PEER_CHIPS * k + r], recv_sem=rsem.at[N_PEER_CHIPS * k + r],
                    device_id=peer, device_id_type=MESH)
                cp.wait_send()
                cp.wait_recv()

    res = pl.pallas_call(
        body, name=name, out_shape=(*_hbm_like(grads), *_hbm_like(recvs)),
        in_specs=[*[HBM_SPEC] * (2 * n), SEM_SPEC, SEM_SPEC, ANY_SPEC], out_specs=[HBM_SPEC] * (2 * n),
        input_output_aliases={k: k for k in range(2 * n)},
        compiler_params=pltpu.CompilerParams(has_side_effects=DATAFLOW),
    )(*grads, *recvs, send_sem, recv_sem, after)
    return res[:n], res[n:]


def swap_with_sibling(parts, name="swap_with_sibling"):
    n = len(parts)

    def body(*refs):
        ins, outs = refs[:n], refs[n:2 * n]
        send_sems, recv_sems = refs[2 * n:]
        x, y, c = _place()
        copies = []
        for k in range(n):
            cp = pltpu.make_async_remote_copy(
                src_ref=ins[k], dst_ref=outs[k], send_sem=send_sems.at[k], recv_sem=recv_sems.at[k],
                device_id=(x, y, 1 - c), device_id_type=MESH)
            cp.start()
            copies.append(cp)
        for cp in copies:
            cp.wait()

    any_spec = pl.BlockSpec(memory_space=pl.ANY)
    return pl.pallas_call(
        body, name=name, in_specs=[any_spec] * n, out_specs=[any_spec] * n,
        out_shape=[jax.ShapeDtypeStruct(p.shape, p.dtype) for p in parts],
        scratch_shapes=[pltpu.SemaphoreType.DMA((n,)), pltpu.SemaphoreType.DMA((n,))],
        compiler_params=pltpu.CompilerParams(has_side_effects=True),
    )(*parts)


def allreduce_small(vec, name="allreduce_small"):
    rows, lanes = vec.shape

    def body(v_ref, o_ref, buf_ref, send_sems, recv_sems):
        x, y, c = _place()
        me = 4 * x + 2 * y + c
        buf_ref[me] = v_ref[...]
        copies = []
        for k in range(1, N_DEV):
            px, py, pc = x ^ ((k >> 2) & 1), y ^ ((k >> 1) & 1), c ^ (k & 1)
            peer_slot = 4 * px + 2 * py + pc
            cp = pltpu.make_async_remote_copy(
                src_ref=buf_ref.at[me], dst_ref=buf_ref.at[me], send_sem=send_sems.at[k - 1],
                recv_sem=recv_sems.at[k - 1], device_id=(px, py, pc), device_id_type=MESH)
            cp.start()
            copies.append(pltpu.make_async_remote_copy(
                src_ref=buf_ref.at[me], dst_ref=buf_ref.at[peer_slot], send_sem=send_sems.at[k - 1],
                recv_sem=recv_sems.at[k - 1], device_id=(px, py, pc), device_id_type=MESH))
        for cp in copies:
            cp.wait()
        acc = buf_ref[0]
        for j in range(1, N_DEV):
            acc = acc + buf_ref[j]
        o_ref[...] = acc

    vm = pl.BlockSpec(memory_space=pltpu.VMEM)
    return pl.pallas_call(
        body, name=name, in_specs=[vm], out_specs=vm, out_shape=jax.ShapeDtypeStruct((rows, lanes), F32),
        scratch_shapes=[pltpu.VMEM((N_DEV, rows, lanes), F32), pltpu.SemaphoreType.DMA((N_DEV - 1,)),
                        pltpu.SemaphoreType.DMA((N_DEV - 1,))],
        compiler_params=pltpu.CompilerParams(has_side_effects=True, vmem_limit_bytes=VMEM_LIMIT_V7X),
    )(vec)


def _pack(arrays):
    flat = jnp.concatenate([a.reshape(-1).astype(F32) for a in arrays])
    n = flat.shape[0]
    rows = -(-n // LANES)
    rows = -(-rows // 8) * 8
    return jnp.pad(flat, (0, rows * LANES - n)).reshape(rows, LANES)


def _unpack(packed, shapes, lead=()):
    flat = packed.reshape(*lead, -1)
    out, off = [], 0
    for shp in shapes:
        n = math.prod(shp)
        out.append(flat[..., off:off + n].reshape(*lead, *shp))
        off += n
    return out


def _row(vec):
    return vec.reshape(1, -1)


def kernel(x, mix_pre_g, mix_post_g, ffn_pre_g, ffn_post_g, cm_w_in, cm_b_in, cm_dw, cm_dw_b, cm_ln_g, cm_ln_b, cm_w_out, cm_b_out, kv_norm_g, w_kv, w_q, w_o, ffn_w_in, ffn_dw, ffn_dw_b, ffn_w_out, loss_target, m_mix_pre_g, m_mix_post_g, m_ffn_pre_g, m_ffn_post_g, m_cm_w_in, m_cm_b_in, m_cm_dw, m_cm_dw_b, m_cm_ln_g, m_cm_ln_b, m_cm_w_out, m_cm_b_out, m_kv_norm_g, m_w_kv, m_w_q, m_w_o, m_ffn_w_in, m_ffn_dw, m_ffn_dw_b, m_ffn_w_out, v_mix_pre_g, v_mix_post_g, v_ffn_pre_g, v_ffn_post_g, v_cm_w_in, v_cm_b_in, v_cm_dw, v_cm_dw_b, v_cm_ln_g, v_cm_ln_b, v_cm_w_out, v_cm_b_out, v_kv_norm_g, v_w_kv, v_w_q, v_w_o, v_ffn_w_in, v_ffn_dw, v_ffn_dw_b, v_ffn_w_out):
    names = ["mix_pre_g", "mix_post_g", "ffn_pre_g", "ffn_post_g", "cm_w_in", "cm_b_in", "cm_dw", "cm_dw_b", "cm_ln_g",
             "cm_ln_b", "cm_w_out", "cm_b_out", "kv_norm_g", "w_kv", "w_q", "w_o", "ffn_w_in", "ffn_dw", "ffn_dw_b",
             "ffn_w_out"]
    w_in = dict(zip(names, [mix_pre_g, mix_post_g, ffn_pre_g, ffn_post_g, cm_w_in, cm_b_in, cm_dw, cm_dw_b, cm_ln_g,
                            cm_ln_b, cm_w_out, cm_b_out, kv_norm_g, w_kv, w_q, w_o, ffn_w_in, ffn_dw, ffn_dw_b, ffn_w_out]))
    m_in = dict(zip(names, [m_mix_pre_g, m_mix_post_g, m_ffn_pre_g, m_ffn_post_g, m_cm_w_in, m_cm_b_in, m_cm_dw, m_cm_dw_b,
                            m_cm_ln_g, m_cm_ln_b, m_cm_w_out, m_cm_b_out, m_kv_norm_g, m_w_kv, m_w_q, m_w_o, m_ffn_w_in,
                            m_ffn_dw, m_ffn_dw_b, m_ffn_w_out]))
    v_in = dict(zip(names, [v_mix_pre_g, v_mix_post_g, v_ffn_pre_g, v_ffn_post_g, v_cm_w_in, v_cm_b_in, v_cm_dw, v_cm_dw_b,
                            v_cm_ln_g, v_cm_ln_b, v_cm_w_out, v_cm_b_out, v_kv_norm_g, v_w_kv, v_w_q, v_w_o, v_ffn_w_in,
                            v_ffn_dw, v_ffn_dw_b, v_ffn_w_out]))

    bsz, seq, d = x.shape
    t = bsz * seq
    n_b = DEPTH - N_A
    hw = w_o.shape[-1]
    qw = N_GROUPS * hw
    f2 = ffn_dw_b.shape[-1]
    f = f2 // 2
    me_chip = _chip_of(lax.axis_index("x"), lax.axis_index("y"))

    big = ["cm_w_in", "cm_w_out", "w_kv", "w_q", "w_o", "ffn_w_in", "ffn_w_out"]
    row_sharded = ("cm_w_out", "w_o", "ffn_w_out")
    small_sharded = ["cm_b_in", "cm_dw", "cm_dw_b", "cm_ln_g", "cm_ln_b", "cm_b_out", "ffn_dw"]
    small_pack = _pack([w_in[n] for n in small_sharded])
    chunks = [
        [("cm_w_in", 0), ("cm_w_out", 0), ("small", None)],
        [("ffn_w_in", 0), ("ffn_w_out", 0), ("cm_w_in", 1), ("cm_w_out", 1)],
        [("ffn_w_in", 1), ("ffn_w_out", 1), ("w_kv", None)],
        [("w_q", 0), ("w_o", 0), ("ffn_w_in", 2), ("ffn_w_out", 2)],
        [("w_q", 1), ("w_o", 1), ("ffn_w_in", 3), ("ffn_w_out", 3)],
    ]
    pieces = [pc for ch in chunks for pc in ch]

    def shard_of(pc):
        n, l = pc
        if n == "small":
            return small_pack
        return (w_in[n] if l is None else w_in[n][l]).astype(BF16)

    shards = [shard_of(pc) for pc in pieces]
    lands = place_own(shards)
    g_send, g_recv, shards_f, lands_f, token = gather_start(shards, lands, [len(ch) for ch in chunks])
    weights = {}

    def finish_chunk(ck, after):
        lo = sum(len(ch) for ch in chunks[:ck])
        hi = lo + len(chunks[ck])
        got = gather_wait(g_send[ck], g_recv[ck], shards_f[lo:hi], lands_f[lo:hi], after, name=f"gather_wait{ck}")
        for pc, arr in zip(chunks[ck], got):
            weights[pc] = arr.reshape(1, -1, arr.shape[-1]) if pc[0] in row_sharded else arr

    def wmat(n, l=None):
        arr = weights[(n, l)]
        return arr, arr.shape[0]

    finish_chunk(0, token)
    small_full = {}
    for n, arr4 in zip(small_sharded, _unpack(weights[("small", None)], [w_in[n].shape for n in small_sharded], lead=(N_CHIPS,))):
        shp = w_in[n].shape
        small_full[n] = jnp.moveaxis(arr4, 0, -2).reshape(*shp[:-1], N_CHIPS * shp[-1])

    x2d = x.reshape(t, d)
    saved = []
    (h1,) = resid_norm_fwd(x2d, None, None, [_row(mix_pre_g[0])], name="norm_in", dep=token)
    xcur = x2d
    kv_state = None
    for i in range(DEPTH):
        sv = {"x_in": xcur, "h1": h1}
        if i < N_A:
            z = mm_nn(h1, *wmat("cm_w_in", i), 1, 0, bias=_row(small_full["cm_b_in"][i]), name=f"cm_in{i}")
            u2 = glu_conv_fwd(z.reshape(bsz, seq, 2 * d), small_full["cm_dw"][i], _row(small_full["cm_dw_b"][i]),
                              name=f"glu_conv{i}").reshape(t, d)
            u4 = ln_silu_fwd(u2, _row(small_full["cm_ln_g"][i]), _row(small_full["cm_ln_b"][i]), name=f"ln_silu{i}")
            y = mm_nn(u4, *wmat("cm_w_out", i), 1, 0, bias=_row(small_full["cm_b_out"][i]), name=f"cm_out{i}")
            sv.update(z=z, u2=u2, u4=u4)
        else:
            j = i - N_A
            finish_chunk(3 + j, h1)
            q = mm_nn(h1, *wmat("w_q", j), 1, 0, name=f"q_proj{j}").reshape(bsz, seq, qw)
            outs, lses, q_subs = [], [], []
            for g, dil in enumerate(DILATIONS):
                nblk = seq // dil // BLK
                if dil == 1:
                    o_s, l_s = attn_fwd(q, g, kv_state["kv"], g, kv_state["kv"], N_GROUPS + g, nblk, hw, name=f"attn_fwd{j}_{g}")
                    q_subs.append(None)
                    outs.append(o_s.reshape(t, hw))
                    lses.append(l_s.reshape(t, LANES))
                else:
                    q_s = perm_rows(q, g, hw, dil, False, BF16, name=f"q_sub{j}_{g}")
                    o_s, l_s = attn_fwd(q_s, 0, kv_state["k_sub"][g], 0, kv_state["v_sub"][g], 0, nblk, hw, name=f"attn_fwd{j}_{g}")
                    q_subs.append(q_s)
                    outs.append(perm_rows(o_s, 0, hw, dil, True, F32, name=f"o_tok{j}_{g}").reshape(t, hw))
                    lses.append(perm_rows(l_s, 0, LANES, dil, True, F32, name=f"lse_tok{j}_{g}").reshape(t, LANES))
            merged, lsej = attn_merge(outs, lses, name=f"attn_merge{j}")
            y = mm_nn(merged, *wmat("w_o", j), 1, 0, name=f"o_proj{j}")
            sv.update(q=q, q_subs=q_subs, merged=merged, lsej=lsej)
        x1, h2 = resid_norm_fwd(xcur, y, _row(mix_post_g[i]), [_row(ffn_pre_g[i])], name=f"resid_mix{i}")
        if i < N_A:
            finish_chunk(1 + i, h2)
        p = mm_nn(h2, *wmat("ffn_w_in", i), 1, 0, name=f"ffn_in{i}")
        s_act = ffn_mid_fwd(p.reshape(bsz, seq, f2), small_full["ffn_dw"][i], _row(ffn_dw_b[i]), name=f"ffn_mid{i}").reshape(t, f)
        y2 = mm_nn(s_act, *wmat("ffn_w_out", i), 1, 0, name=f"ffn_out{i}")
        next_gains = []
        if i + 1 < DEPTH:
            next_gains.append(_row(mix_pre_g[i + 1]))
        if i == N_A - 1:
            next_gains.append(_row(kv_norm_g))
        res = resid_norm_fwd(x1, y2, _row(ffn_post_g[i]), next_gains, name=f"resid_ffn{i}")
        sv.update(y=y, x1=x1, h2=h2, p=p, s=s_act, y2=y2)
        saved.append(sv)
        xcur = res[0]
        if i + 1 < DEPTH:
            h1 = res[1]
        if i == N_A - 1:
            kvn = res[2]
            kv = mm_nn(kvn, *wmat("w_kv"), 1, 0, name="kv_proj").reshape(bsz, seq, 2 * qw)
            k_sub, v_sub = {}, {}
            for g, dil in enumerate(DILATIONS):
                if dil > 1:
                    k_sub[g] = perm_rows(kv, g, hw, dil, False, BF16, name=f"k_sub{g}")
                    v_sub[g] = perm_rows(kv, N_GROUPS + g, hw, dil, False, BF16, name=f"v_sub{g}")
            kv_state = {"kv": kv, "k_sub": k_sub, "v_sub": v_sub, "kvn": kvn, "x_a": xcur}

    dx, loss_tile = loss_fwd_bwd(xcur, loss_target.reshape(t, d))
    loss = lax.psum(loss_tile[0, 0], ("x", "y", "c"))

    gsm = {n: [None] * w_in[n].shape[0] for n in
           ["mix_pre_g", "mix_post_g", "ffn_pre_g", "ffn_post_g", "cm_b_in", "cm_dw", "cm_dw_b", "cm_ln_g", "cm_ln_b",
            "cm_b_out", "ffn_dw", "ffn_dw_b"]}
    gbig = {}
    in_flight = []
    dep = None

    def start_scatter(pcs, tag):
        ssem, rsem, g_f, r_f, tok = scatter_start([gbig[pc] for pc in pcs], name=f"scatter_start_{tag}")
        in_flight.append((pcs, ssem, rsem, g_f, r_f))
        return tok

    dk_acc = {g: None for g in range(N_GROUPS)}
    dv_acc = {g: None for g in range(N_GROUPS)}
    for i in range(DEPTH - 1, -1, -1):
        sv = saved[i]
        dy2, dg, _ = norm_bwd(sv["y2"], _row(ffn_post_g[i]), dx, out_dtype=BF16, name=f"bwd_ffn_post{i}", dep=dep)
        gsm["ffn_post_g"][i] = dg
        ds = mm_nt(dy2, *wmat("ffn_w_out", i), 1, 0, name=f"bwd_ffn_out_dx{i}")
        gbig[("ffn_w_out", i)] = mm_tn(sv["s"], dy2, 1, name=f"bwd_ffn_out_dw{i}").reshape(N_CHIPS, f // N_CHIPS, d)
        dp, ddw, ddwb = ffn_mid_bwd(sv["p"].reshape(bsz, seq, f2), small_full["ffn_dw"][i], _row(ffn_dw_b[i]), ds.reshape(bsz, seq, f),
                                    name=f"bwd_ffn_mid{i}")
        gsm["ffn_dw"][i] = jnp.sum(ddw, axis=1)
        gsm["ffn_dw_b"][i] = ddwb
        dp = dp.reshape(t, f2)
        dh2 = mm_nt(dp, *wmat("ffn_w_in", i), 1, 0, name=f"bwd_ffn_in_dx{i}")
        gbig[("ffn_w_in", i)] = mm_tn(sv["h2"], dp, N_CHIPS, name=f"bwd_ffn_in_dw{i}")
        dx1, dg, _ = norm_bwd(sv["x1"], _row(ffn_pre_g[i]), dh2, add=dx, name=f"bwd_ffn_pre{i}")
        gsm["ffn_pre_g"][i] = dg
        dep = start_scatter([("ffn_w_in", 0), ("ffn_w_out", 0)], "ffn0") if i == 0 else None
        dy, dg, dbias = norm_bwd(sv["y"], _row(mix_post_g[i]), dx1, out_dtype=BF16, name=f"bwd_mix_post{i}", dep=dep)
        gsm["mix_post_g"][i] = dg
        if i < N_A:
            gsm["cm_b_out"][i] = dbias
            du4 = mm_nt(dy, *wmat("cm_w_out", i), 1, 0, name=f"bwd_cm_out_dx{i}")
            gbig[("cm_w_out", i)] = mm_tn(sv["u4"], dy, 1, name=f"bwd_cm_out_dw{i}").reshape(N_CHIPS, d // N_CHIPS, d)
            du2, dlg, dlb = ln_silu_bwd(sv["u2"], _row(small_full["cm_ln_g"][i]), _row(small_full["cm_ln_b"][i]), du4,
                                        name=f"bwd_ln_silu{i}")
            gsm["cm_ln_g"][i], gsm["cm_ln_b"][i] = dlg, dlb
            dza, dzg, ddw, ddwb, dba, dbg = glu_conv_bwd(sv["z"].reshape(bsz, seq, 2 * d), small_full["cm_dw"][i],
                                                         du2.reshape(bsz, seq, d), name=f"bwd_glu_conv{i}")
            gsm["cm_dw"][i] = jnp.sum(ddw, axis=1)
            gsm["cm_dw_b"][i] = ddwb
            gsm["cm_b_in"][i] = jnp.concatenate([dba, dbg], axis=-1)
            dz = jnp.concatenate([dza, dzg], axis=-1).reshape(t, 2 * d)
            dh1 = mm_nt(dz, *wmat("cm_w_in", i), 1, 0, name=f"bwd_cm_in_dx{i}")
            gbig[("cm_w_in", i)] = mm_tn(sv["h1"], dz, N_CHIPS, name=f"bwd_cm_in_dw{i}")
        else:
            j = i - N_A
            dmerged = mm_nt(dy, *wmat("w_o", j), 1, 0, name=f"bwd_o_proj_dx{j}")
            gbig[("w_o", j)] = mm_tn(sv["merged"], dy, 1, name=f"bwd_o_proj_dw{j}").reshape(N_CHIPS, hw // N_CHIPS, d)
            dmt = attn_bwd_prep(dmerged, sv["merged"], name=f"bwd_attn_prep{j}")
            dq_parts = []
            for g, dil in enumerate(DILATIONS):
                nblk = seq // dil // BLK
                do3 = dmerged.reshape(bsz, seq, hw)
                lj3 = sv["lsej"].reshape(bsz, seq, LANES)
                dm3 = dmt.reshape(bsz, seq, LANES)
                if dil == 1:
                    dq_g, dk_g, dv_g = attn_bwd(sv["q"], g, kv_state["kv"], g, kv_state["kv"], N_GROUPS + g, do3, lj3, dm3,
                                                dk_acc[g], dv_acc[g], nblk, hw, name=f"attn_bwd{j}_{g}")
                    dq_parts.append(dq_g.reshape(t, hw))
                else:
                    do_s = perm_rows(do3, 0, hw, dil, False, BF16, name=f"do_sub{j}_{g}")
                    lj_s = perm_rows(lj3, 0, LANES, dil, False, F32, name=f"lsej_sub{j}_{g}", tc=LANES)
                    dm_s = perm_rows(dm3, 0, LANES, dil, False, F32, name=f"dm_sub{j}_{g}", tc=LANES)
                    dq_g, dk_g, dv_g = attn_bwd(sv["q_subs"][g], 0, kv_state["k_sub"][g], 0, kv_state["v_sub"][g], 0, do_s,
                                                lj_s, dm_s, dk_acc[g], dv_acc[g], nblk, hw, name=f"attn_bwd{j}_{g}")
                    dq_parts.append(perm_rows(dq_g, 0, hw, dil, True, F32, name=f"dq_tok{j}_{g}").reshape(t, hw))
                dk_acc[g], dv_acc[g] = dk_g, dv_g
            dq = jnp.concatenate(dq_parts, axis=-1).astype(BF16)
            dh1 = mm_nt(dq, *wmat("w_q", j), 1, 0, name=f"bwd_q_proj_dx{j}")
            gbig[("w_q", j)] = mm_tn(sv["h1"], dq, N_CHIPS, name=f"bwd_q_proj_dw{j}")
        dx, dg, _ = norm_bwd(sv["x_in"], _row(mix_pre_g[i]), dh1, add=dx1, name=f"bwd_mix_pre{i}")
        gsm["mix_pre_g"][i] = dg
        if i > N_A:
            dep = start_scatter([("ffn_w_in", i), ("ffn_w_out", i), ("w_q", i - N_A), ("w_o", i - N_A)], f"l{i}")
        elif 0 < i < N_A:
            dep = start_scatter([("ffn_w_in", i), ("ffn_w_out", i), ("cm_w_in", i), ("cm_w_out", i)], f"l{i}")
        elif i == 0:
            start_scatter([("cm_w_in", 0), ("cm_w_out", 0)], "cm0")
        if i == N_A:
            dkv_parts = []
            for acc in (dk_acc, dv_acc):
                for g, dil in enumerate(DILATIONS):
                    part = acc[g]
                    if dil > 1:
                        part = perm_rows(part, 0, hw, dil, True, F32, name=f"dkv_tok{len(dkv_parts)}")
                    dkv_parts.append(part.reshape(t, hw))
            dkv = jnp.concatenate(dkv_parts, axis=-1).astype(BF16)
            dkvn = mm_nt(dkv, *wmat("w_kv"), 1, 0, name="bwd_kv_proj_dx")
            gbig[("w_kv", None)] = mm_tn(kv_state["kvn"], dkv, N_CHIPS, name="bwd_kv_proj_dw")
            dx, dg_kv, _ = norm_bwd(kv_state["x_a"], _row(kv_norm_g), dkvn, add=dx, name="bwd_kv_norm")
            dep = start_scatter([("ffn_w_in", i), ("ffn_w_out", i), ("w_q", 0), ("w_o", 0), ("w_kv", None)], f"l{i}")
    grad_x = dx.reshape(bsz, seq, d)

    me_arr = me_chip.astype(jnp.int32).reshape(1)
    plane_of = {}
    for k, (pcs, ssem, rsem, g_f, r_f) in enumerate(in_flight):
        g_done, r_done = scatter_wait(ssem, rsem, g_f, r_f, dx, name=f"scatter_wait{k}")
        for pc, g_arr, r_arr in zip(pcs, g_done, r_done):
            plane_of[pc] = sum_parts(g_arr, r_arr, me_arr, name=f"sum_chips_{pc[0]}_{pc[1]}")
    plane = []
    for n in big:
        if w_in[n].ndim == 2:
            plane.append(plane_of[(n, None)])
        else:
            plane.append(jnp.concatenate([plane_of[(n, l)] for l in range(w_in[n].shape[0])], axis=0))
    other = swap_with_sibling(plane)

    outs_g, outs_d, outs_m, outs_v = {}, {}, {}, {}
    for n, p_mine, p_other in zip(big, plane, other):
        shp = w_in[n].shape
        flat = lambda a: a.reshape(-1, shp[-1])
        g_, d_, m_, v_ = adamw(flat(w_in[n]), flat(m_in[n]), flat(v_in[n]), [p_mine, p_other], name=f"adamw_{n}")
        outs_g[n], outs_d[n], outs_m[n], outs_v[n] = (a.reshape(shp) for a in (g_, d_, m_, v_))

    small_names = [n for n in names if n not in big]
    small_shapes_full = {}
    small_grads_full = []
    for n in small_names:
        if n == "kv_norm_g":
            gfull = dg_kv.reshape(-1)
        elif n in ("cm_dw", "ffn_dw"):
            gfull = jnp.stack(gsm[n], axis=0)
        else:
            gfull = jnp.stack([a.reshape(-1) for a in gsm[n]], axis=0)
        small_shapes_full[n] = gfull.shape
        small_grads_full.append(gfull)
    summed = allreduce_small(_pack(small_grads_full))
    g_full = dict(zip(small_names, _unpack(summed, [small_shapes_full[n] for n in small_names])))
    g_loc = {}
    for n in small_names:
        if n in small_sharded:
            width = w_in[n].shape[-1]
            g_loc[n] = lax.dynamic_slice_in_dim(g_full[n], me_chip * width, width, axis=g_full[n].ndim - 1)
        else:
            g_loc[n] = g_full[n]
    res = adamw(_pack([w_in[n] for n in small_names]), _pack([m_in[n] for n in small_names]),
                _pack([v_in[n] for n in small_names]), [_pack([g_loc[n] for n in small_names])], name="adamw_small")
    shapes_loc = [w_in[n].shape for n in small_names]
    for dst, packed in zip((outs_g, outs_d, outs_m, outs_v), res):
        for n, a in zip(small_names, _unpack(packed, shapes_loc)):
            dst[n] = a

    return (loss, grad_x, *[outs_g[n] for n in names], *[outs_d[n] for n in names],
            *[outs_m[n] for n in names], *[outs_v[n] for n in names])
```

```python
import functools
import math

import jax
import jax.numpy as jnp
from jax import lax
from jax.experimental import pallas as pl
from jax.experimental.pallas import tpu as pltpu

F32 = jnp.float32
BF16 = jnp.bfloat16
EPS = 1e-6
NEG_INF = -1e30
N_A = 2
DEPTH = 4
N_GROUPS = 3
DILATIONS = (1, 4, 16)
HEAD_DIM = 128
BLK = 128
LANES = 128
N_CHIPS = 4
N_DEV = 8
VMEM_LIMIT_V7X = 56 * 1024 * 1024

ADAM_LR = 0.001
ADAM_B1 = 0.9
ADAM_B2 = 0.999
ADAM_EPS = 1e-08
ADAM_WD = 0.01
ADAM_STEP = 10

MESH = pl.DeviceIdType.MESH


def _cp(*sem):
    return pltpu.CompilerParams(dimension_semantics=sem if sem else None, vmem_limit_bytes=VMEM_LIMIT_V7X)


def _dot(a, b):
    return jnp.dot(a, b, preferred_element_type=F32)


def _dot_nt(a, b):
    return lax.dot_general(a, b, (((1,), (1,)), ((), ())), preferred_element_type=F32)


def _dot_tn(a, b):
    return lax.dot_general(a, b, (((0,), (0,)), ((), ())), preferred_element_type=F32)


def _sigmoid(x):
    return 1.0 / (1.0 + jnp.exp(-x))


def _row_tile(n, want):
    if n <= want:
        return n
    for t in range(want - want % 8, 7, -8):
        if n % t == 0:
            return t
    raise ValueError(f"no row tile for {n} rows")


def mm_nn(a, w, nsh, stride, layer, bias=None, out_dtype=F32, name="mm_nn", tm=512):
    m, k = a.shape
    _, k2, ns = w.shape
    assert k == k2
    tm = _row_tile(m, tm)
    has_bias = bias is not None

    def body(*refs):
        if has_bias:
            a_ref, w_ref, b_ref, o_ref = refs
        else:
            a_ref, w_ref, o_ref = refs
        acc = _dot(a_ref[...].astype(BF16), w_ref[...])
        if has_bias:
            acc = acc + b_ref[...]
        o_ref[...] = acc.astype(out_dtype)

    in_specs = [
        pl.BlockSpec((tm, k), lambda j, i: (i, 0)),
        pl.BlockSpec((None, k, ns), lambda j, i: (j * stride + layer, 0, 0)),
    ]
    args = [a, w]
    if has_bias:
        in_specs.append(pl.BlockSpec((1, ns), lambda j, i: (0, j)))
        args.append(bias)
    return pl.pallas_call(
        body,
        name=name,
        grid=(nsh, m // tm),
        in_specs=in_specs,
        out_specs=pl.BlockSpec((tm, ns), lambda j, i: (i, j)),
        out_shape=jax.ShapeDtypeStruct((m, nsh * ns), out_dtype),
        compiler_params=_cp("parallel", "parallel"),
    )(*args)


def mm_nt(dy, w, nsh, stride, layer, out_dtype=F32, name="mm_nt", tm=512):
    m, n = dy.shape
    _, k, ns = w.shape
    assert n == nsh * ns
    tm = _row_tile(m, tm)

    def body(dy_ref, w_ref, o_ref, acc_ref):
        j = pl.program_id(1)
        part = _dot_nt(dy_ref[...].astype(BF16), w_ref[...])

        @pl.when(j == 0)
        def _():
            acc_ref[...] = part

        @pl.when(j > 0)
        def _():
            acc_ref[...] += part

        @pl.when(j == nsh - 1)
        def _():
            o_ref[...] = acc_ref[...].astype(out_dtype)

    return pl.pallas_call(
        body,
        name=name,
        grid=(m // tm, nsh),
        in_specs=[
            pl.BlockSpec((tm, ns), lambda i, j: (i, j)),
            pl.BlockSpec((None, k, ns), lambda i, j: (j * stride + layer, 0, 0)),
        ],
        out_specs=pl.BlockSpec((tm, k), lambda i, j: (i, 0)),
        out_shape=jax.ShapeDtypeStruct((m, k), out_dtype),
        scratch_shapes=[pltpu.VMEM((tm, k), F32)],
        compiler_params=_cp("parallel", "arbitrary"),
    )(dy, w)


def mm_tn(a, dy, nsh, name="mm_tn", tm=512):
    m, k = a.shape
    _, n = dy.shape
    ns = n // nsh
    tm = _row_tile(m, tm)
    nt = m // tm

    def body(a_ref, dy_ref, o_ref, acc_ref):
        i = pl.program_id(1)
        part = _dot_tn(a_ref[...].astype(BF16), dy_ref[...].astype(BF16))

        @pl.when(i == 0)
        def _():
            acc_ref[...] = part

        @pl.when(i > 0)
        def _():
            acc_ref[...] += part

        @pl.when(i == nt - 1)
        def _():
            o_ref[...] = acc_ref[...].astype(BF16)

    return pl.pallas_call(
        body,
        name=name,
        grid=(nsh, nt),
        in_specs=[
            pl.BlockSpec((tm, k), lambda j, i: (i, 0)),
            pl.BlockSpec((tm, ns), lambda j, i: (i, j)),
        ],
        out_specs=pl.BlockSpec((None, k, ns), lambda j, i: (j, 0, 0)),
        out_shape=jax.ShapeDtypeStruct((nsh, k, ns), BF16),
        scratch_shapes=[pltpu.VMEM((k, ns), F32)],
        compiler_params=_cp("parallel", "arbitrary"),
    )(a, dy)


DEP_SPEC_SHAPE = (8, LANES)


def resid_norm_fwd(x, y, g_post, next_gains, name, tm=256, dep=None):
    t, d = x.shape
    tm = _row_tile(t, tm)
    has_y = y is not None
    n_next = len(next_gains)
    n_dep = 0 if dep is None else 1

    def body(*refs):
        x_ref = refs[0]
        pos = 1
        if has_y:
            y_ref, gp_ref = refs[1], refs[2]
            pos = 3
        gn_refs = refs[pos:pos + n_next]
        outs = refs[pos + n_next + n_dep:]
        xv = x_ref[...]
        o = 0
        if has_y:
            yv = y_ref[...]
            r = lax.rsqrt(jnp.mean(yv * yv, axis=-1, keepdims=True) + EPS)
            xv = xv + (yv * r) * gp_ref[...]
            outs[0][...] = xv
            o = 1
        if n_next:
            xn = xv * lax.rsqrt(jnp.mean(xv * xv, axis=-1, keepdims=True) + EPS)
            for k in range(n_next):
                outs[o + k][...] = (xn * gn_refs[k][...]).astype(BF16)

    row = pl.BlockSpec((tm, d), lambda i: (i, 0))
    vec = pl.BlockSpec((1, d), lambda i: (0, 0))
    args, in_specs = [x], [row]
    if has_y:
        args += [y, g_post]
        in_specs += [row, vec]
    args += list(next_gains)
    in_specs += [vec] * n_next
    if n_dep:
        args.append(dep)
        in_specs.append(pl.BlockSpec(DEP_SPEC_SHAPE, lambda i: (0, 0)))
    out_shape, out_specs = [], []
    if has_y:
        out_shape.append(jax.ShapeDtypeStruct((t, d), F32))
        out_specs.append(row)
    for _ in range(n_next):
        out_shape.append(jax.ShapeDtypeStruct((t, d), BF16))
        out_specs.append(row)
    return pl.pallas_call(
        body, name=name, grid=(t // tm,), in_specs=in_specs, out_specs=out_specs, out_shape=out_shape,
        compiler_params=_cp("parallel"),
    )(*args)


def norm_bwd(x, g, dy, add=None, out_dtype=F32, name="norm_bwd", tm=256, dep=None):
    t, d = x.shape
    tm = _row_tile(t, tm)
    has_add = add is not None

    def body(*refs):
        x_ref, g_ref, dy_ref = refs[:3]
        add_ref = refs[3] if has_add else None
        dx_ref, dg_ref, cs_ref = refs[-3:]
        i = pl.program_id(0)
        xv = x_ref[...]
        dyv = dy_ref[...].astype(F32)
        r = lax.rsqrt(jnp.mean(xv * xv, axis=-1, keepdims=True) + EPS)
        gd = dyv * g_ref[...]
        dx = r * gd - xv * ((r * r * r) * jnp.mean(xv * gd, axis=-1, keepdims=True))
        if has_add:
            dx = dx + add_ref[...]
        dx_ref[...] = dx.astype(out_dtype)
        dg = jnp.sum(dyv * (xv * r), axis=0, keepdims=True)
        cs = jnp.sum(dx, axis=0, keepdims=True)

        @pl.when(i == 0)
        def _():
            dg_ref[...] = dg
            cs_ref[...] = cs

        @pl.when(i > 0)
        def _():
            dg_ref[...] += dg
            cs_ref[...] += cs

    row = pl.BlockSpec((tm, d), lambda i: (i, 0))
    vec = pl.BlockSpec((1, d), lambda i: (0, 0))
    args, in_specs = [x, g, dy], [row, vec, row]
    if has_add:
        args.append(add)
        in_specs.append(row)
    if dep is not None:
        args.append(dep)
        in_specs.append(pl.BlockSpec(DEP_SPEC_SHAPE, lambda i: (0, 0)))
    return pl.pallas_call(
        body, name=name, grid=(t // tm,), in_specs=in_specs,
        out_specs=[row, vec, vec],
        out_shape=[jax.ShapeDtypeStruct((t, d), out_dtype), jax.ShapeDtypeStruct((1, d), F32),
                   jax.ShapeDtypeStruct((1, d), F32)],
        compiler_params=_cp("arbitrary"),
    )(*args)


def loss_fwd_bwd(x, target, name="loss", tm=256):
    t, d = x.shape
    tm = _row_tile(t, tm)

    def body(x_ref, t_ref, dx_ref, l_ref):
        i = pl.program_id(0)
        err = x_ref[...] - t_ref[...]
        dx_ref[...] = err * (1.0 / d)
        part = 0.5 * jnp.sum(jnp.mean(err * err, axis=-1, keepdims=True), axis=0, keepdims=True)
        part = jnp.broadcast_to(part, l_ref.shape)

        @pl.when(i == 0)
        def _():
            l_ref[...] = part

        @pl.when(i > 0)
        def _():
            l_ref[...] += part

    row = pl.BlockSpec((tm, d), lambda i: (i, 0))
    return pl.pallas_call(
        body, name=name, grid=(t // tm,), in_specs=[row, row],
        out_specs=[row, pl.BlockSpec((8, LANES), lambda i: (0, 0))],
        out_shape=[jax.ShapeDtypeStruct((t, d), F32), jax.ShapeDtypeStruct((8, LANES), F32)],
        compiler_params=_cp("arbitrary"),
    )(x, target)


CONV_HALO = 32
CONV_CHUNK = 128


def glu_conv_fwd(z, dw, dwb, name, tc=128):
    b, s, c2 = z.shape
    c = c2 // 2
    kw = dw.shape[0]
    tc = min(tc, c)
    nc = c // tc
    ch = min(CONV_CHUNK, s)
    halo = CONV_HALO
    assert kw - 1 <= halo and s % ch == 0

    def body(a_ref, g_ref, w_ref, b_ref, o_ref, pad_ref):
        pad_ref[0:halo, :] = jnp.zeros((halo, tc), F32)

        def fill(ci, carry):
            r0 = pl.multiple_of(ci * ch, ch)
            pad_ref[pl.ds(halo + r0, ch), :] = a_ref[pl.ds(r0, ch), :] * _sigmoid(g_ref[pl.ds(r0, ch), :])
            return carry

        lax.fori_loop(0, s // ch, fill, 0)

        def chunk(ci, carry):
            r0 = pl.multiple_of(ci * ch, ch)
            win = pad_ref[pl.ds(r0, ch + halo), :]
            acc = b_ref[...] + w_ref[kw - 1:kw, :] * win[halo:, :]
            for k in range(kw - 2, -1, -1):
                win = pltpu.roll(win, 1, 0)
                acc = acc + w_ref[k:k + 1, :] * win[halo:, :]
            o_ref[pl.ds(r0, ch), :] = acc
            return carry

        lax.fori_loop(0, s // ch, chunk, 0)

    return pl.pallas_call(
        body, name=name, grid=(b, nc),
        in_specs=[
            pl.BlockSpec((None, s, tc), lambda bi, i: (bi, 0, i)),
            pl.BlockSpec((None, s, tc), lambda bi, i: (bi, 0, i + nc)),
            pl.BlockSpec((kw, tc), lambda bi, i: (0, i)),
            pl.BlockSpec((1, tc), lambda bi, i: (0, i)),
        ],
        out_specs=pl.BlockSpec((None, s, tc), lambda bi, i: (bi, 0, i)),
        out_shape=jax.ShapeDtypeStruct((b, s, c), F32),
        scratch_shapes=[pltpu.VMEM((s + halo, tc), F32)],
        compiler_params=_cp("parallel", "parallel"),
    )(z, z, dw, dwb)


def glu_conv_bwd(z, dw, du2, name, tc=128):
    b, s, c2 = z.shape
    c = c2 // 2
    kw = dw.shape[0]
    tc = min(tc, c)
    nc = c // tc
    ch = min(CONV_CHUNK, s)
    halo = CONV_HALO

    def body(a_ref, g_ref, w_ref, du_ref, dza_ref, dzg_ref, ddw_ref, ddwb_ref, dba_ref, dbg_ref, upad_ref, dpad_ref):
        bi = pl.program_id(1)

        @pl.when(bi == 0)
        def _():
            ddw_ref[...] = jnp.zeros(ddw_ref.shape, F32)
            ddwb_ref[...] = jnp.zeros(ddwb_ref.shape, F32)
            dba_ref[...] = jnp.zeros(dba_ref.shape, F32)
            dbg_ref[...] = jnp.zeros(dbg_ref.shape, F32)

        upad_ref[0:halo, :] = jnp.zeros((halo, tc), F32)
        dpad_ref[s:s + halo, :] = jnp.zeros((halo, tc), F32)

        def fill(ci, carry):
            r0 = pl.multiple_of(ci * ch, ch)
            upad_ref[pl.ds(halo + r0, ch), :] = a_ref[pl.ds(r0, ch), :] * _sigmoid(g_ref[pl.ds(r0, ch), :])
            dpad_ref[pl.ds(r0, ch), :] = du_ref[pl.ds(r0, ch), :]
            return carry

        lax.fori_loop(0, s // ch, fill, 0)

        def chunk(ci, carry):
            r0 = pl.multiple_of(ci * ch, ch)
            du_c = du_ref[pl.ds(r0, ch), :]
            wd = dpad_ref[pl.ds(r0, ch + halo), :]
            wu = upad_ref[pl.ds(r0, ch + halo), :]
            du1 = w_ref[kw - 1:kw, :] * wd[:ch, :]
            ddw_ref[kw - 1] += jnp.sum((du_c * wu[halo:, :]).reshape(ch // 8, 8, tc), axis=0)
            for j in range(1, kw):
                wd = pltpu.roll(wd, ch + halo - 1, 0)
                wu = pltpu.roll(wu, 1, 0)
                du1 = du1 + w_ref[kw - 1 - j:kw - j, :] * wd[:ch, :]
                ddw_ref[kw - 1 - j] += jnp.sum((du_c * wu[halo:, :]).reshape(ch // 8, 8, tc), axis=0)
            av = a_ref[pl.ds(r0, ch), :]
            sg = _sigmoid(g_ref[pl.ds(r0, ch), :])
            dza = du1 * sg
            dzg = du1 * av * (sg * (1.0 - sg))
            dza_ref[pl.ds(r0, ch), :] = dza.astype(BF16)
            dzg_ref[pl.ds(r0, ch), :] = dzg.astype(BF16)
            dba_ref[...] += jnp.sum(dza, axis=0, keepdims=True)
            dbg_ref[...] += jnp.sum(dzg, axis=0, keepdims=True)
            ddwb_ref[...] += jnp.sum(du_c, axis=0, keepdims=True)
            return carry

        lax.fori_loop(0, s // ch, chunk, 0)

    blk = lambda off: pl.BlockSpec((None, s, tc), lambda i, bi: (bi, 0, i + off))
    vec = pl.BlockSpec((1, tc), lambda i, bi: (0, i))
    return pl.pallas_call(
        body, name=name, grid=(nc, b),
        in_specs=[blk(0), blk(nc), pl.BlockSpec((kw, tc), lambda i, bi: (0, i)), blk(0)],
        out_specs=[blk(0), blk(0), pl.BlockSpec((kw, 8, tc), lambda i, bi: (0, 0, i)), vec, vec, vec],
        out_shape=[
            jax.ShapeDtypeStruct((b, s, c), BF16), jax.ShapeDtypeStruct((b, s, c), BF16),
            jax.ShapeDtypeStruct((kw, 8, c), F32), jax.ShapeDtypeStruct((1, c), F32),
            jax.ShapeDtypeStruct((1, c), F32), jax.ShapeDtypeStruct((1, c), F32),
        ],
        scratch_shapes=[pltpu.VMEM((s + halo, tc), F32), pltpu.VMEM((s + halo, tc), F32)],
        compiler_params=_cp("parallel", "arbitrary"),
    )(z, z, dw, du2)


def ln_silu_fwd(u, g, bvec, name, tm=256):
    t, d = u.shape
    tm = _row_tile(t, tm)

    def body(u_ref, g_ref, b_ref, o_ref):
        uv = u_ref[...]
        mu = jnp.mean(uv, axis=-1, keepdims=True)
        xc = uv - mu
        var = jnp.mean(xc * xc, axis=-1, keepdims=True)
        v = (xc * lax.rsqrt(var + EPS)) * g_ref[...] + b_ref[...]
        o_ref[...] = (v * _sigmoid(v)).astype(BF16)

    row = pl.BlockSpec((tm, d), lambda i: (i, 0))
    vec = pl.BlockSpec((1, d), lambda i: (0, 0))
    return pl.pallas_call(
        body, name=name, grid=(t // tm,), in_specs=[row, vec, vec], out_specs=row,
        out_shape=jax.ShapeDtypeStruct((t, d), BF16), compiler_params=_cp("parallel"),
    )(u, g, bvec)


def ln_silu_bwd(u, g, bvec, dout, name, tm=256):
    t, d = u.shape
    tm = _row_tile(t, tm)

    def body(u_ref, g_ref, b_ref, do_ref, du_ref, dg_ref, db_ref):
        i = pl.program_id(0)
        uv = u_ref[...]
        mu = jnp.mean(uv, axis=-1, keepdims=True)
        xc = uv - mu
        var = jnp.mean(xc * xc, axis=-1, keepdims=True)
        rstd = lax.rsqrt(var + EPS)
        n = xc * rstd
        v = n * g_ref[...] + b_ref[...]
        sg = _sigmoid(v)
        dv = do_ref[...].astype(F32) * (sg * (1.0 + v * (1.0 - sg)))
        dn = dv * g_ref[...]
        du_ref[...] = rstd * (dn - jnp.mean(dn, axis=-1, keepdims=True) - n * jnp.mean(dn * n, axis=-1, keepdims=True))
        dg = jnp.sum(dv * n, axis=0, keepdims=True)
        db = jnp.sum(dv, axis=0, keepdims=True)

        @pl.when(i == 0)
        def _():
            dg_ref[...] = dg
            db_ref[...] = db

        @pl.when(i > 0)
        def _():
            dg_ref[...] += dg
            db_ref[...] += db

    row = pl.BlockSpec((tm, d), lambda i: (i, 0))
    vec = pl.BlockSpec((1, d), lambda i: (0, 0))
    return pl.pallas_call(
        body, name=name, grid=(t // tm,), in_specs=[row, vec, vec, row], out_specs=[row, vec, vec],
        out_shape=[jax.ShapeDtypeStruct((t, d), F32), jax.ShapeDtypeStruct((1, d), F32), jax.ShapeDtypeStruct((1, d), F32)],
        compiler_params=_cp("arbitrary"),
    )(u, g, bvec, dout)


FFN_HALO = 8


def _conv3(win, w_ref, b_ref, ch):
    kw = w_ref.shape[0]
    acc = b_ref[...] + w_ref[kw - 1:kw, :] * win[FFN_HALO:, :]
    for k in range(kw - 2, -1, -1):
        win = pltpu.roll(win, 1, 0)
        acc = acc + w_ref[k:k + 1, :] * win[FFN_HALO:, :]
    return acc


def ffn_mid_fwd(p, dw, dwb, name, tc=256):
    b, s, f2 = p.shape
    f = f2 // 2
    kw = dw.shape[0]
    tc = min(tc, f)
    nf = f // tc
    ch = min(CONV_CHUNK, s)
    halo = FFN_HALO

    def body(pa_ref, pg_ref, wa_ref, wg_ref, ba_ref, bg_ref, o_ref, apad_ref, gpad_ref):
        apad_ref[0:halo, :] = jnp.zeros((halo, tc), F32)
        gpad_ref[0:halo, :] = jnp.zeros((halo, tc), F32)

        def fill(ci, carry):
            r0 = pl.multiple_of(ci * ch, ch)
            apad_ref[pl.ds(halo + r0, ch), :] = pa_ref[pl.ds(r0, ch), :]
            gpad_ref[pl.ds(halo + r0, ch), :] = pg_ref[pl.ds(r0, ch), :]
            return carry

        lax.fori_loop(0, s // ch, fill, 0)

        def chunk(ci, carry):
            r0 = pl.multiple_of(ci * ch, ch)
            ca = _conv3(apad_ref[pl.ds(r0, ch + halo), :], wa_ref, ba_ref, ch)
            cg = _conv3(gpad_ref[pl.ds(r0, ch + halo), :], wg_ref, bg_ref, ch)
            o_ref[pl.ds(r0, ch), :] = ((cg * _sigmoid(cg)) * ca).astype(BF16)
            return carry

        lax.fori_loop(0, s // ch, chunk, 0)

    blk = lambda off: pl.BlockSpec((None, s, tc), lambda bi, i: (bi, 0, i + off))
    wsp = lambda off: pl.BlockSpec((kw, tc), lambda bi, i: (0, i + off))
    bsp = lambda off: pl.BlockSpec((1, tc), lambda bi, i: (0, i + off))
    return pl.pallas_call(
        body, name=name, grid=(b, nf),
        in_specs=[blk(0), blk(nf), wsp(0), wsp(nf), bsp(0), bsp(nf)],
        out_specs=pl.BlockSpec((None, s, tc), lambda bi, i: (bi, 0, i)),
        out_shape=jax.ShapeDtypeStruct((b, s, f), BF16),
        scratch_shapes=[pltpu.VMEM((s + halo, tc), F32), pltpu.VMEM((s + halo, tc), F32)],
        compiler_params=_cp("parallel", "parallel"),
    )(p, p, dw, dw, dwb, dwb)


def ffn_mid_bwd(p, dw, dwb, ds, name, tc=256):
    b, s, f2 = p.shape
    f = f2 // 2
    kw = dw.shape[0]
    tc = min(tc, f)
    nf = f // tc
    ch = min(CONV_CHUNK, s)
    halo = FFN_HALO

    def body(pa_ref, pg_ref, wa_ref, wg_ref, ba_ref, bg_ref, ds_ref, dp_ref, ddw_ref, ddwb_ref,
             apad_ref, gpad_ref, dcpad_ref):
        hf = pl.program_id(0)
        bi = pl.program_id(2)
        gate_half = hf == 1

        @pl.when(bi == 0)
        def _():
            ddw_ref[...] = jnp.zeros(ddw_ref.shape, F32)
            ddwb_ref[...] = jnp.zeros(ddwb_ref.shape, F32)

        apad_ref[0:halo, :] = jnp.zeros((halo, tc), F32)
        gpad_ref[0:halo, :] = jnp.zeros((halo, tc), F32)
        dcpad_ref[s:s + halo, :] = jnp.zeros((halo, tc), F32)

        def fill(ci, carry):
            r0 = pl.multiple_of(ci * ch, ch)
            apad_ref[pl.ds(halo + r0, ch), :] = pa_ref[pl.ds(r0, ch), :]
            gpad_ref[pl.ds(halo + r0, ch), :] = pg_ref[pl.ds(r0, ch), :]
            return carry

        lax.fori_loop(0, s // ch, fill, 0)

        def grads(ci, carry):
            r0 = pl.multiple_of(ci * ch, ch)
            wa_win = apad_ref[pl.ds(r0, ch + halo), :]
            wg_win = gpad_ref[pl.ds(r0, ch + halo), :]
            ca = _conv3(wa_win, wa_ref, ba_ref, ch)
            cg = _conv3(wg_win, wg_ref, bg_ref, ch)
            sg = _sigmoid(cg)
            dsv = ds_ref[pl.ds(r0, ch), :].astype(F32)
            dc = jnp.where(gate_half, dsv * ca * (sg * (1.0 + cg * (1.0 - sg))), dsv * (cg * sg))
            dcpad_ref[pl.ds(r0, ch), :] = dc
            win = jnp.where(gate_half, wg_win, wa_win)
            ddw_ref[kw - 1] += jnp.sum((dc * win[halo:, :]).reshape(ch // 8, 8, tc), axis=0)
            for j in range(1, kw):
                win = pltpu.roll(win, 1, 0)
                ddw_ref[kw - 1 - j] += jnp.sum((dc * win[halo:, :]).reshape(ch // 8, 8, tc), axis=0)
            ddwb_ref[...] += jnp.sum(dc, axis=0, keepdims=True)
            return carry

        lax.fori_loop(0, s // ch, grads, 0)

        def back(ci, carry):
            r0 = pl.multiple_of(ci * ch, ch)
            wd = dcpad_ref[pl.ds(r0, ch + halo), :]
            acc = jnp.where(gate_half, wg_ref[kw - 1:kw, :], wa_ref[kw - 1:kw, :]) * wd[:ch, :]
            for j in range(1, kw):
                wd = pltpu.roll(wd, ch + halo - 1, 0)
                acc = acc + jnp.where(gate_half, wg_ref[kw - 1 - j:kw - j, :], wa_ref[kw - 1 - j:kw - j, :]) * wd[:ch, :]
            dp_ref[pl.ds(r0, ch), :] = acc.astype(BF16)
            return carry

        lax.fori_loop(0, s // ch, back, 0)

    blk = lambda off: pl.BlockSpec((None, s, tc), lambda hf, i, bi: (bi, 0, i + off))
    wsp = lambda off: pl.BlockSpec((kw, tc), lambda hf, i, bi: (0, i + off))
    bsp = lambda off: pl.BlockSpec((1, tc), lambda hf, i, bi: (0, i + off))
    return pl.pallas_call(
        body, name=name, grid=(2, nf, b),
        in_specs=[blk(0), blk(nf), wsp(0), wsp(nf), bsp(0), bsp(nf), blk(0)],
        out_specs=[
            pl.BlockSpec((None, s, tc), lambda hf, i, bi: (bi, 0, hf * nf + i)),
            pl.BlockSpec((kw, 8, tc), lambda hf, i, bi: (0, 0, hf * nf + i)),
            pl.BlockSpec((1, tc), lambda hf, i, bi: (0, hf * nf + i)),
        ],
        out_shape=[jax.ShapeDtypeStruct((b, s, f2), BF16), jax.ShapeDtypeStruct((kw, 8, f2), F32),
                   jax.ShapeDtypeStruct((1, f2), F32)],
        scratch_shapes=[pltpu.VMEM((s + halo, tc), F32)] * 3,
        compiler_params=_cp("parallel", "parallel", "arbitrary"),
    )(p, p, dw, dw, dwb, dwb, ds)


def perm_rows(x, col_blk, width, dil, inverse, out_dtype, name, tc=LANES):
    b, s, _ = x.shape
    tc = min(tc, width)
    nc = width // tc
    ln = s // dil

    def body(x_ref, o_ref):
        if dil == 1:
            o_ref[...] = x_ref[...].astype(out_dtype)
            return
        for r in range(dil):
            if inverse:
                o_ref[pl.ds(r, ln, stride=dil), :] = x_ref[r * ln:(r + 1) * ln, :].astype(out_dtype)
            else:
                o_ref[r * ln:(r + 1) * ln, :] = x_ref[pl.ds(r, ln, stride=dil), :].astype(out_dtype)

    return pl.pallas_call(
        body, name=name, grid=(b, nc),
        in_specs=[pl.BlockSpec((None, s, tc), lambda bi, i: (bi, 0, col_blk * nc + i))],
        out_specs=pl.BlockSpec((None, s, tc), lambda bi, i: (bi, 0, i)),
        out_shape=jax.ShapeDtypeStruct((b, s, width), out_dtype),
        compiler_params=_cp("parallel", "parallel"),
    )(x)


def _attn_masks(has_other):
    qi = lax.broadcasted_iota(jnp.int32, (BLK, BLK), 0)
    kk = lax.broadcasted_iota(jnp.int32, (BLK, BLK), 1)
    return kk <= qi, jnp.logical_and(kk >= qi, has_other)


def attn_fwd(q, q_blk, k, k_blk, v, v_blk, nblk, hw, name):
    b, s, _ = q.shape
    nh = hw // HEAD_DIM
    scale = 1.0 / math.sqrt(HEAD_DIM)

    def body(q_ref, kc_ref, kp_ref, vc_ref, vp_ref, o_ref, lse_ref):
        t = pl.program_id(1)
        mask_c, mask_p = _attn_masks((t % nblk) != 0)
        lane = lax.broadcasted_iota(jnp.int32, (BLK, LANES), 1)
        lse_acc = jnp.zeros((BLK, LANES), F32)
        for h in range(nh):
            sl = slice(h * HEAD_DIM, (h + 1) * HEAD_DIM)
            qh = q_ref[:, sl].astype(BF16)
            kc = kc_ref[:, sl].astype(BF16)
            kp = kp_ref[:, sl].astype(BF16)
            s_c = jnp.where(mask_c, _dot_nt(qh, kc) * scale, NEG_INF)
            s_p = jnp.where(mask_p, _dot_nt(qh, kp) * scale, NEG_INF)
            m = jnp.maximum(jnp.max(s_c, axis=-1, keepdims=True), jnp.max(s_p, axis=-1, keepdims=True))
            p_c = jnp.exp(s_c - m)
            p_p = jnp.exp(s_p - m)
            den = jnp.sum(p_c, axis=-1, keepdims=True) + jnp.sum(p_p, axis=-1, keepdims=True)
            o = _dot(p_c.astype(BF16), vc_ref[:, sl].astype(BF16)) + _dot(p_p.astype(BF16), vp_ref[:, sl].astype(BF16))
            o_ref[:, sl] = o / den
            lse_acc = jnp.where(lane == h, m + jnp.log(den), lse_acc)
        lse_ref[...] = lse_acc

    cur = lambda cb: pl.BlockSpec((None, BLK, hw), lambda bi, t: (bi, t, cb))
    prev = lambda cb: pl.BlockSpec((None, BLK, hw), lambda bi, t: (bi, jnp.where(t % nblk == 0, t, t - 1), cb))
    return pl.pallas_call(
        body, name=name, grid=(b, s // BLK),
        in_specs=[cur(q_blk), cur(k_blk), prev(k_blk), cur(v_blk), prev(v_blk)],
        out_specs=[pl.BlockSpec((None, BLK, hw), lambda bi, t: (bi, t, 0)),
                   pl.BlockSpec((None, BLK, LANES), lambda bi, t: (bi, t, 0))],
        out_shape=[jax.ShapeDtypeStruct((b, s, hw), F32), jax.ShapeDtypeStruct((b, s, LANES), F32)],
        compiler_params=_cp("parallel", "parallel"),
    )(q, k, k, v, v)


def attn_merge(outs, lses, name, tm=256):
    t, hw = outs[0].shape
    nh = hw // HEAD_DIM
    tm = _row_tile(t, tm)
    ng = len(outs)

    def body(*refs):
        o_refs, l_refs = refs[:ng], refs[ng:2 * ng]
        m_ref, lj_ref = refs[2 * ng:]
        lane = lax.broadcasted_iota(jnp.int32, (tm, LANES), 1)
        lj = jnp.zeros((tm, LANES), F32)
        for h in range(nh):
            sl = slice(h * HEAD_DIM, (h + 1) * HEAD_DIM)
            ls = [l_refs[g][:, h:h + 1] for g in range(ng)]
            mx = functools.reduce(jnp.maximum, ls)
            es = [jnp.exp(l - mx) for l in ls]
            tot = functools.reduce(lambda a, c: a + c, es)
            acc = (es[0] / tot) * o_refs[0][:, sl]
            for g in range(1, ng):
                acc = acc + (es[g] / tot) * o_refs[g][:, sl]
            m_ref[:, sl] = acc.astype(BF16)
            lj = jnp.where(lane == h, mx + jnp.log(tot), lj)
        lj_ref[...] = lj

    row = pl.BlockSpec((tm, hw), lambda i: (i, 0))
    st = pl.BlockSpec((tm, LANES), lambda i: (i, 0))
    return pl.pallas_call(
        body, name=name, grid=(t // tm,), in_specs=[row] * ng + [st] * ng, out_specs=[row, st],
        out_shape=[jax.ShapeDtypeStruct((t, hw), BF16), jax.ShapeDtypeStruct((t, LANES), F32)],
        compiler_params=_cp("parallel"),
    )(*outs, *lses)


def attn_bwd_prep(dmerged, merged, name, tm=256):
    t, hw = merged.shape
    nh = hw // HEAD_DIM
    tm = _row_tile(t, tm)

    def body(d_ref, m_ref, o_ref):
        lane = lax.broadcasted_iota(jnp.int32, (tm, LANES), 1)
        acc = jnp.zeros((tm, LANES), F32)
        for h in range(nh):
            sl = slice(h * HEAD_DIM, (h + 1) * HEAD_DIM)
            dsum = jnp.sum(d_ref[:, sl] * m_ref[:, sl].astype(F32), axis=-1, keepdims=True)
            acc = jnp.where(lane == h, dsum, acc)
        o_ref[...] = acc

    row = pl.BlockSpec((tm, hw), lambda i: (i, 0))
    return pl.pallas_call(
        body, name=name, grid=(t // tm,), in_specs=[row, row], out_specs=pl.BlockSpec((tm, LANES), lambda i: (i, 0)),
        out_shape=jax.ShapeDtypeStruct((t, LANES), F32), compiler_params=_cp("parallel"),
    )(dmerged, merged)


def attn_bwd(q, q_blk, k, k_blk, v, v_blk, do, lsej, dm, dk_add, dv_add, nblk, hw, name):
    b, s, _ = q.shape
    nh = hw // HEAD_DIM
    scale = 1.0 / math.sqrt(HEAD_DIM)
    has_add = dk_add is not None

    def body(*refs):
        (qm_ref, qn_ref, kc_ref, kp_ref, vc_ref, vp_ref, dom_ref, don_ref, lm_ref, ln_ref, dmm_ref, dmn_ref) = refs[:12]
        if has_add:
            dka_ref, dva_ref = refs[12:14]
            dq_ref, dk_ref, dv_ref = refs[14:]
        else:
            dq_ref, dk_ref, dv_ref = refs[12:]
        t = pl.program_id(1)
        n = t % nblk
        mask_c, mask_p = _attn_masks(n != 0)
        _, mask_n = _attn_masks(n != nblk - 1)
        for h in range(nh):
            sl = slice(h * HEAD_DIM, (h + 1) * HEAD_DIM)
            qm = qm_ref[:, sl].astype(BF16)
            qn = qn_ref[:, sl].astype(BF16)
            kc = kc_ref[:, sl].astype(BF16)
            kp = kp_ref[:, sl].astype(BF16)
            vc = vc_ref[:, sl].astype(BF16)
            vp = vp_ref[:, sl].astype(BF16)
            dom = dom_ref[:, sl].astype(BF16)
            don = don_ref[:, sl].astype(BF16)
            lm, lnx = lm_ref[:, h:h + 1], ln_ref[:, h:h + 1]
            dmm, dmn = dmm_ref[:, h:h + 1], dmn_ref[:, h:h + 1]
            pa = jnp.exp(jnp.where(mask_c, _dot_nt(qm, kc) * scale, NEG_INF) - lm)
            pb = jnp.exp(jnp.where(mask_p, _dot_nt(qm, kp) * scale, NEG_INF) - lm)
            pc = jnp.exp(jnp.where(mask_n, _dot_nt(qn, kc) * scale, NEG_INF) - lnx)
            dsa = (pa * (_dot_nt(dom, vc) - dmm)).astype(BF16)
            dsb = (pb * (_dot_nt(dom, vp) - dmm)).astype(BF16)
            dsc = (pc * (_dot_nt(don, vc) - dmn)).astype(BF16)
            dq_ref[:, sl] = (_dot(dsa, kc) + _dot(dsb, kp)) * scale
            dk = (_dot_tn(dsa, qm) + _dot_tn(dsc, qn)) * scale
            dv = _dot_tn(pa.astype(BF16), dom) + _dot_tn(pc.astype(BF16), don)
            if has_add:
                dk = dk + dka_ref[:, sl]
                dv = dv + dva_ref[:, sl]
            dk_ref[:, sl] = dk
            dv_ref[:, sl] = dv

    def spec(cb, w, which):
        if which == "cur":
            return pl.BlockSpec((None, BLK, w), lambda bi, t: (bi, t, cb))
        if which == "prev":
            return pl.BlockSpec((None, BLK, w), lambda bi, t: (bi, jnp.where(t % nblk == 0, t, t - 1), cb))
        return pl.BlockSpec((None, BLK, w), lambda bi, t: (bi, jnp.where(t % nblk == nblk - 1, t, t + 1), cb))

    in_specs = [
        spec(q_blk, hw, "cur"), spec(q_blk, hw, "next"), spec(k_blk, hw, "cur"), spec(k_blk, hw, "prev"),
        spec(v_blk, hw, "cur"), spec(v_blk, hw, "prev"), spec(0, hw, "cur"), spec(0, hw, "next"),
        spec(0, LANES, "cur"), spec(0, LANES, "next"), spec(0, LANES, "cur"), spec(0, LANES, "next"),
    ]
    args = [q, q, k, k, v, v, do, do, lsej, lsej, dm, dm]
    if has_add:
        in_specs += [spec(0, hw, "cur"), spec(0, hw, "cur")]
        args += [dk_add, dv_add]
    out = pl.BlockSpec((None, BLK, hw), lambda bi, t: (bi, t, 0))
    return pl.pallas_call(
        body, name=name, grid=(b, s // BLK), in_specs=in_specs, out_specs=[out, out, out],
        out_shape=[jax.ShapeDtypeStruct((b, s, hw), F32)] * 3,
        compiler_params=_cp("parallel", "parallel"),
    )(*args)


def sum_parts(g, recv, me, name, tm=256):
    _, rows, c = g.shape
    n = recv.shape[0]
    tm = _row_tile(rows, tm)

    def body(me_ref, g_ref, r_ref, o_ref):
        acc = g_ref[...].astype(F32)
        for j in range(n):
            acc = acc + r_ref[j].astype(F32)
        o_ref[...] = acc

    return pl.pallas_call(
        body, name=name,
        grid_spec=pltpu.PrefetchScalarGridSpec(
            num_scalar_prefetch=1, grid=(rows // tm,),
            in_specs=[pl.BlockSpec((None, tm, c), lambda i, me_ref: (me_ref[0], i, 0)),
                      pl.BlockSpec((n, tm, c), lambda i, me_ref: (0, i, 0))],
            out_specs=pl.BlockSpec((tm, c), lambda i, me_ref: (i, 0))),
        out_shape=jax.ShapeDtypeStruct((rows, c), F32), compiler_params=_cp("parallel"),
    )(me, g, recv)


def adamw(w, m, v, g_parts, name, tm=256):
    rows, c = w.shape
    tm = _row_tile(rows, tm)
    npart = len(g_parts)

    def body(*refs):
        w_ref, m_ref, v_ref = refs[:3]
        g_refs = refs[3:3 + npart]
        go_ref, d_ref, mo_ref, vo_ref = refs[3 + npart:]
        g = g_refs[0][...]
        for k in range(1, npart):
            g = g + g_refs[k][...]
        mn = ADAM_B1 * m_ref[...] + (1.0 - ADAM_B1) * g
        vn = ADAM_B2 * v_ref[...] + (1.0 - ADAM_B2) * (g * g)
        m_hat = mn / (1.0 - ADAM_B1 ** ADAM_STEP)
        v_hat = vn / (1.0 - ADAM_B2 ** ADAM_STEP)
        go_ref[...] = g
        d_ref[...] = -ADAM_LR * (m_hat / (jnp.sqrt(v_hat) + ADAM_EPS) + ADAM_WD * w_ref[...])
        mo_ref[...] = mn
        vo_ref[...] = vn

    row = pl.BlockSpec((tm, c), lambda i: (i, 0))
    return pl.pallas_call(
        body, name=name, grid=(rows // tm,), in_specs=[row] * (3 + npart), out_specs=[row] * 4,
        out_shape=[jax.ShapeDtypeStruct((rows, c), F32)] * 4, compiler_params=_cp("parallel"),
    )(w, m, v, *g_parts)


def _place():
    return lax.axis_index("x"), lax.axis_index("y"), lax.axis_index("c")


def _other_chips(x, y, c):
    return [(1 - x, y, c), (x, 1 - y, c), (1 - x, 1 - y, c)]


def _chip_of(px, py):
    return 2 * px + py


HBM_SPEC = pl.BlockSpec(memory_space=pltpu.HBM)
SEM_SPEC = pl.BlockSpec(memory_space=pltpu.SEMAPHORE)
ANY_SPEC = pl.BlockSpec(memory_space=pl.ANY)
DATAFLOW = pltpu.SideEffectType.DATAFLOW_SIDE_EFFECTING
N_PEER_CHIPS = N_CHIPS - 1


def _hbm(a):
    return pltpu.with_memory_space_constraint(a, pltpu.HBM)


def _hbm_like(arrays):
    return [pltpu.HBM(a.shape, a.dtype) for a in arrays]


def cast_place(w, layer, me, out_dtype, name, tm=256):
    rows, c = w.shape[-2:]
    tm = _row_tile(rows, tm)

    def body(me_ref, w_ref, o_ref):
        o_ref[...] = w_ref[...].astype(out_dtype)

    if layer is None:
        in_spec = pl.BlockSpec((tm, c), lambda i, me_ref: (i, 0))
    else:
        in_spec = pl.BlockSpec((None, tm, c), lambda i, me_ref: (layer, i, 0))
    return pl.pallas_call(
        body, name=name,
        grid_spec=pltpu.PrefetchScalarGridSpec(
            num_scalar_prefetch=1, grid=(rows // tm,), in_specs=[in_spec],
            out_specs=pl.BlockSpec((None, tm, c), lambda i, me_ref: (me_ref[0], i, 0))),
        out_shape=jax.ShapeDtypeStruct((N_CHIPS, rows, c), out_dtype), compiler_params=_cp("parallel"),
    )(me, w)


def gather_start(lands, chunk_sizes, name="gather_start"):
    n = len(lands)
    nch = len(chunk_sizes)
    assert sum(chunk_sizes) == n

    def body(*refs):
        land_refs = refs[:n]
        outs = refs[n:]
        send_sems, recv_sems = outs[:nch], outs[nch:2 * nch]
        token = outs[-1]
        x, y, c = _place()
        me = _chip_of(x, y)
        peers = _other_chips(x, y, c)
        k = 0
        for ck, size in enumerate(chunk_sizes):
            for pos in range(size):
                for r, peer in enumerate(peers):
                    pltpu.make_async_remote_copy(
                        src_ref=land_refs[k].at[me], dst_ref=land_refs[k].at[me],
                        send_sem=send_sems[ck].at[N_PEER_CHIPS * pos + r], recv_sem=recv_sems[ck].at[N_PEER_CHIPS * pos + r],
                        device_id=peer, device_id_type=MESH).start()
                k += 1
        token[...] = jnp.zeros(token.shape, F32)

    sems = [pltpu.SemaphoreType.DMA((N_PEER_CHIPS * s,)) for s in chunk_sizes]
    res = pl.pallas_call(
        body, name=name,
        out_shape=(*sems, *sems, *_hbm_like(lands), jax.ShapeDtypeStruct(DEP_SPEC_SHAPE, F32)),
        in_specs=[HBM_SPEC] * n,
        out_specs=(*[SEM_SPEC] * (2 * nch), *[HBM_SPEC] * n, pl.BlockSpec(memory_space=pltpu.VMEM)),
        input_output_aliases={k: 2 * nch + k for k in range(n)},
        compiler_params=pltpu.CompilerParams(has_side_effects=DATAFLOW),
    )(*[_hbm(a) for a in lands])
    return res[:nch], res[nch:2 * nch], res[2 * nch:2 * nch + n], res[-1]


def gather_wait(send_sem, recv_sem, lands, after, name):
    n = len(lands)

    def body(*refs):
        land_refs = refs[:n]
        ssem, rsem = refs[n], refs[n + 1]
        x, y, c = _place()
        me = _chip_of(x, y)
        for pos in range(n):
            for r, peer in enumerate(_other_chips(x, y, c)):
                cp = pltpu.make_async_remote_copy(
                    src_ref=land_refs[pos].at[me], dst_ref=land_refs[pos].at[_chip_of(peer[0], peer[1])],
                    send_sem=ssem.at[N_PEER_CHIPS * pos + r], recv_sem=rsem.at[N_PEER_CHIPS * pos + r],
                    device_id=peer, device_id_type=MESH)
                cp.wait_send()
                cp.wait_recv()

    return pl.pallas_call(
        body, name=name, out_shape=tuple(_hbm_like(lands)),
        in_specs=[*[HBM_SPEC] * n, SEM_SPEC, SEM_SPEC, ANY_SPEC], out_specs=[HBM_SPEC] * n,
        input_output_aliases={k: k for k in range(n)},
        compiler_params=pltpu.CompilerParams(has_side_effects=DATAFLOW),
    )(*lands, send_sem, recv_sem, after)


def scatter_start(grads, name):
    n = len(grads)
    recvs = [lax.empty((N_PEER_CHIPS, *g.shape[1:]), g.dtype) for g in grads]

    def body(*refs):
        g_refs, r_refs = refs[:n], refs[n:2 * n]
        send_sems, recv_sems = refs[2 * n], refs[2 * n + 1]
        token = refs[-1]
        x, y, c = _place()
        for k in range(n):
            for r, peer in enumerate(_other_chips(x, y, c)):
                pltpu.make_async_remote_copy(
                    src_ref=g_refs[k].at[_chip_of(peer[0], peer[1])], dst_ref=r_refs[k].at[r],
                    send_sem=send_sems.at[N_PEER_CHIPS * k + r], recv_sem=recv_sems.at[N_PEER_CHIPS * k + r],
                    device_id=peer, device_id_type=MESH).start()
        token[...] = jnp.zeros(token.shape, F32)

    sem = pltpu.SemaphoreType.DMA((N_PEER_CHIPS * n,))
    res = pl.pallas_call(
        body, name=name,
        out_shape=(sem, sem, *_hbm_like(grads), *_hbm_like(recvs), jax.ShapeDtypeStruct(DEP_SPEC_SHAPE, F32)),
        in_specs=[HBM_SPEC] * (2 * n),
        out_specs=(SEM_SPEC, SEM_SPEC, *[HBM_SPEC] * (2 * n), pl.BlockSpec(memory_space=pltpu.VMEM)),
        input_output_aliases={k: 2 + k for k in range(2 * n)},
        compiler_params=pltpu.CompilerParams(has_side_effects=DATAFLOW),
    )(*[_hbm(a) for a in grads], *[_hbm(a) for a in recvs])
    return res[0], res[1], res[2:2 + n], res[2 + n:2 + 2 * n], res[-1]


def scatter_wait(send_sem, recv_sem, grads, recvs, after, name):
    n = len(grads)

    def body(*refs):
        g_refs, r_refs = refs[:n], refs[n:2 * n]
        ssem, rsem = refs[2 * n], refs[2 * n + 1]
        x, y, c = _place()
        for k in range(n):
            for r, peer in enumerate(_other_chips(x, y, c)):
                cp = pltpu.make_async_remote_copy(
                    src_ref=g_refs[k].at[_chip_of(peer[0], peer[1])], dst_ref=r_refs[k].at[r],
                    send_sem=ssem.at[N_PEER_CHIPS * k + r], recv_sem=rsem.at[N_PEER_CHIPS * k + r],
                    device_id=peer, device_id_type=MESH)
                cp.wait_send()
                cp.wait_recv()

    res = pl.pallas_call(
        body, name=name, out_shape=(*_hbm_like(grads), *_hbm_like(recvs)),
        in_specs=[*[HBM_SPEC] * (2 * n), SEM_SPEC, SEM_SPEC, ANY_SPEC], out_specs=[HBM_SPEC] * (2 * n),
        input_output_aliases={k: k for k in range(2 * n)},
        compiler_params=pltpu.CompilerParams(has_side_effects=DATAFLOW),
    )(*grads, *recvs, send_sem, recv_sem, after)
    return res[:n], res[n:]


def swap_with_sibling(parts, name="swap_with_sibling"):
    n = len(parts)

    def body(*refs):
        ins, outs = refs[:n], refs[n:2 * n]
        send_sems, recv_sems = refs[2 * n:]
        x, y, c = _place()
        copies = []
        for k in range(n):
            cp = pltpu.make_async_remote_copy(
                src_ref=ins[k], dst_ref=outs[k], send_sem=send_sems.at[k], recv_sem=recv_sems.at[k],
                device_id=(x, y, 1 - c), device_id_type=MESH)
            cp.start()
            copies.append(cp)
        for cp in copies:
            cp.wait()

    any_spec = pl.BlockSpec(memory_space=pl.ANY)
    return pl.pallas_call(
        body, name=name, in_specs=[any_spec] * n, out_specs=[any_spec] * n,
        out_shape=[jax.ShapeDtypeStruct(p.shape, p.dtype) for p in parts],
        scratch_shapes=[pltpu.SemaphoreType.DMA((n,)), pltpu.SemaphoreType.DMA((n,))],
        compiler_params=pltpu.CompilerParams(has_side_effects=True),
    )(*parts)


def allreduce_small(vec, name="allreduce_small"):
    rows, lanes = vec.shape

    def body(v_ref, o_ref, buf_ref, send_sems, recv_sems):
        x, y, c = _place()
        me = 4 * x + 2 * y + c
        buf_ref[me] = v_ref[...]
        copies = []
        for k in range(1, N_DEV):
            px, py, pc = x ^ ((k >> 2) & 1), y ^ ((k >> 1) & 1), c ^ (k & 1)
            peer_slot = 4 * px + 2 * py + pc
            cp = pltpu.make_async_remote_copy(
                src_ref=buf_ref.at[me], dst_ref=buf_ref.at[me], send_sem=send_sems.at[k - 1],
                recv_sem=recv_sems.at[k - 1], device_id=(px, py, pc), device_id_type=MESH)
            cp.start()
            copies.append(pltpu.make_async_remote_copy(
                src_ref=buf_ref.at[me], dst_ref=buf_ref.at[peer_slot], send_sem=send_sems.at[k - 1],
                recv_sem=recv_sems.at[k - 1], device_id=(px, py, pc), device_id_type=MESH))
        for cp in copies:
            cp.wait()
        acc = buf_ref[0]
        for j in range(1, N_DEV):
            acc = acc + buf_ref[j]
        o_ref[...] = acc

    vm = pl.BlockSpec(memory_space=pltpu.VMEM)
    return pl.pallas_call(
        body, name=name, in_specs=[vm], out_specs=vm, out_shape=jax.ShapeDtypeStruct((rows, lanes), F32),
        scratch_shapes=[pltpu.VMEM((N_DEV, rows, lanes), F32), pltpu.SemaphoreType.DMA((N_DEV - 1,)),
                        pltpu.SemaphoreType.DMA((N_DEV - 1,))],
        compiler_params=pltpu.CompilerParams(has_side_effects=True, vmem_limit_bytes=VMEM_LIMIT_V7X),
    )(vec)


def _pack(arrays):
    flat = jnp.concatenate([a.reshape(-1).astype(F32) for a in arrays])
    n = flat.shape[0]
    rows = -(-n // LANES)
    rows = -(-rows // 8) * 8
    return jnp.pad(flat, (0, rows * LANES - n)).reshape(rows, LANES)


def _unpack(packed, shapes, lead=()):
    flat = packed.reshape(*lead, -1)
    out, off = [], 0
    for shp in shapes:
        n = math.prod(shp)
        out.append(flat[..., off:off + n].reshape(*lead, *shp))
        off += n
    return out


def _row(vec):
    return vec.reshape(1, -1)


def kernel(x, mix_pre_g, mix_post_g, ffn_pre_g, ffn_post_g, cm_w_in, cm_b_in, cm_dw, cm_dw_b, cm_ln_g, cm_ln_b, cm_w_out, cm_b_out, kv_norm_g, w_kv, w_q, w_o, ffn_w_in, ffn_dw, ffn_dw_b, ffn_w_out, loss_target, m_mix_pre_g, m_mix_post_g, m_ffn_pre_g, m_ffn_post_g, m_cm_w_in, m_cm_b_in, m_cm_dw, m_cm_dw_b, m_cm_ln_g, m_cm_ln_b, m_cm_w_out, m_cm_b_out, m_kv_norm_g, m_w_kv, m_w_q, m_w_o, m_ffn_w_in, m_ffn_dw, m_ffn_dw_b, m_ffn_w_out, v_mix_pre_g, v_mix_post_g, v_ffn_pre_g, v_ffn_post_g, v_cm_w_in, v_cm_b_in, v_cm_dw, v_cm_dw_b, v_cm_ln_g, v_cm_ln_b, v_cm_w_out, v_cm_b_out, v_kv_norm_g, v_w_kv, v_w_q, v_w_o, v_ffn_w_in, v_ffn_dw, v_ffn_dw_b, v_ffn_w_out):
    names = ["mix_pre_g", "mix_post_g", "ffn_pre_g", "ffn_post_g", "cm_w_in", "cm_b_in", "cm_dw", "cm_dw_b", "cm_ln_g",
             "cm_ln_b", "cm_w_out", "cm_b_out", "kv_norm_g", "w_kv", "w_q", "w_o", "ffn_w_in", "ffn_dw", "ffn_dw_b",
             "ffn_w_out"]
    w_in = dict(zip(names, [mix_pre_g, mix_post_g, ffn_pre_g, ffn_post_g, cm_w_in, cm_b_in, cm_dw, cm_dw_b, cm_ln_g,
                            cm_ln_b, cm_w_out, cm_b_out, kv_norm_g, w_kv, w_q, w_o, ffn_w_in, ffn_dw, ffn_dw_b, ffn_w_out]))
    m_in = dict(zip(names, [m_mix_pre_g, m_mix_post_g, m_ffn_pre_g, m_ffn_post_g, m_cm_w_in, m_cm_b_in, m_cm_dw, m_cm_dw_b,
                            m_cm_ln_g, m_cm_ln_b, m_cm_w_out, m_cm_b_out, m_kv_norm_g, m_w_kv, m_w_q, m_w_o, m_ffn_w_in,
                            m_ffn_dw, m_ffn_dw_b, m_ffn_w_out]))
    v_in = dict(zip(names, [v_mix_pre_g, v_mix_post_g, v_ffn_pre_g, v_ffn_post_g, v_cm_w_in, v_cm_b_in, v_cm_dw, v_cm_dw_b,
                            v_cm_ln_g, v_cm_ln_b, v_cm_w_out, v_cm_b_out, v_kv_norm_g, v_w_kv, v_w_q, v_w_o, v_ffn_w_in,
                            v_ffn_dw, v_ffn_dw_b, v_ffn_w_out]))

    bsz, seq, d = x.shape
    t = bsz * seq
    n_b = DEPTH - N_A
    hw = w_o.shape[-1]
    qw = N_GROUPS * hw
    f2 = ffn_dw_b.shape[-1]
    f = f2 // 2
    me_chip = _chip_of(lax.axis_index("x"), lax.axis_index("y"))

    big = ["cm_w_in", "cm_w_out", "w_kv", "w_q", "w_o", "ffn_w_in", "ffn_w_out"]
    row_sharded = ("cm_w_out", "w_o", "ffn_w_out")
    small_sharded = ["cm_b_in", "cm_dw", "cm_dw_b", "cm_ln_g", "cm_ln_b", "cm_b_out", "ffn_dw"]
    small_pack = _pack([w_in[n] for n in small_sharded])
    chunks = [
        [("cm_w_in", 0), ("cm_w_out", 0), ("small", None)],
        [("ffn_w_in", 0), ("ffn_w_out", 0), ("cm_w_in", 1), ("cm_w_out", 1)],
        [("ffn_w_in", 1), ("ffn_w_out", 1), ("w_kv", None)],
        [("w_q", 0), ("w_o", 0), ("ffn_w_in", 2), ("ffn_w_out", 2)],
        [("w_q", 1), ("w_o", 1), ("ffn_w_in", 3), ("ffn_w_out", 3)],
    ]
    pieces = [pc for ch in chunks for pc in ch]

    me_arr = me_chip.astype(jnp.int32).reshape(1)

    def land_of(pc):
        n, l = pc
        if n == "small":
            return cast_place(small_pack, None, me_arr, F32, name="place_small")
        return cast_place(w_in[n], l, me_arr, BF16, name=f"place_{n}_{l}")

    lands = [land_of(pc) for pc in pieces]
    g_send, g_recv, lands_f, token = gather_start(lands, [len(ch) for ch in chunks])
    weights = {}

    def finish_chunk(ck, after):
        lo = sum(len(ch) for ch in chunks[:ck])
        hi = lo + len(chunks[ck])
        got = gather_wait(g_send[ck], g_recv[ck], lands_f[lo:hi], after, name=f"gather_wait{ck}")
        for pc, arr in zip(chunks[ck], got):
            weights[pc] = arr.reshape(1, -1, arr.shape[-1]) if pc[0] in row_sharded else arr

    def wmat(n, l=None):
        arr = weights[(n, l)]
        return arr, arr.shape[0]

    finish_chunk(0, token)
    small_full = {}
    for n, arr4 in zip(small_sharded, _unpack(weights[("small", None)], [w_in[n].shape for n in small_sharded], lead=(N_CHIPS,))):
        shp = w_in[n].shape
        small_full[n] = jnp.moveaxis(arr4, 0, -2).reshape(*shp[:-1], N_CHIPS * shp[-1])

    x2d = x.reshape(t, d)
    saved = []
    (h1,) = resid_norm_fwd(x2d, None, None, [_row(mix_pre_g[0])], name="norm_in", dep=token)
    xcur = x2d
    kv_state = None
    for i in range(DEPTH):
        sv = {"x_in": xcur, "h1": h1}
        if i < N_A:
            z = mm_nn(h1, *wmat("cm_w_in", i), 1, 0, bias=_row(small_full["cm_b_in"][i]), name=f"cm_in{i}")
            u2 = glu_conv_fwd(z.reshape(bsz, seq, 2 * d), small_full["cm_dw"][i], _row(small_full["cm_dw_b"][i]),
                              name=f"glu_conv{i}").reshape(t, d)
            u4 = ln_silu_fwd(u2, _row(small_full["cm_ln_g"][i]), _row(small_full["cm_ln_b"][i]), name=f"ln_silu{i}")
            y = mm_nn(u4, *wmat("cm_w_out", i), 1, 0, bias=_row(small_full["cm_b_out"][i]), name=f"cm_out{i}")
            sv.update(z=z, u2=u2, u4=u4)
        else:
            j = i - N_A
            finish_chunk(3 + j, h1)
            q = mm_nn(h1, *wmat("w_q", j), 1, 0, name=f"q_proj{j}").reshape(bsz, seq, qw)
            outs, lses, q_subs = [], [], []
            for g, dil in enumerate(DILATIONS):
                nblk = seq // dil // BLK
                if dil == 1:
                    o_s, l_s = attn_fwd(q, g, kv_state["kv"], g, kv_state["kv"], N_GROUPS + g, nblk, hw, name=f"attn_fwd{j}_{g}")
                    q_subs.append(None)
                    outs.append(o_s.reshape(t, hw))
                    lses.append(l_s.reshape(t, LANES))
                else:
                    q_s = perm_rows(q, g, hw, dil, False, BF16, name=f"q_sub{j}_{g}")
                    o_s, l_s = attn_fwd(q_s, 0, kv_state["k_sub"][g], 0, kv_state["v_sub"][g], 0, nblk, hw, name=f"attn_fwd{j}_{g}")
                    q_subs.append(q_s)
                    outs.append(perm_rows(o_s, 0, hw, dil, True, F32, name=f"o_tok{j}_{g}").reshape(t, hw))
                    lses.append(perm_rows(l_s, 0, LANES, dil, True, F32, name=f"lse_tok{j}_{g}").reshape(t, LANES))
            merged, lsej = attn_merge(outs, lses, name=f"attn_merge{j}")
            y = mm_nn(merged, *wmat("w_o", j), 1, 0, name=f"o_proj{j}")
            sv.update(q=q, q_subs=q_subs, merged=merged, lsej=lsej)
        x1, h2 = resid_norm_fwd(xcur, y, _row(mix_post_g[i]), [_row(ffn_pre_g[i])], name=f"resid_mix{i}")
        if i < N_A:
            finish_chunk(1 + i, h2)
        p = mm_nn(h2, *wmat("ffn_w_in", i), 1, 0, name=f"ffn_in{i}")
        s_act = ffn_mid_fwd(p.reshape(bsz, seq, f2), small_full["ffn_dw"][i], _row(ffn_dw_b[i]), name=f"ffn_mid{i}").reshape(t, f)
        y2 = mm_nn(s_act, *wmat("ffn_w_out", i), 1, 0, name=f"ffn_out{i}")
        next_gains = []
        if i + 1 < DEPTH:
            next_gains.append(_row(mix_pre_g[i + 1]))
        if i == N_A - 1:
            next_gains.append(_row(kv_norm_g))
        res = resid_norm_fwd(x1, y2, _row(ffn_post_g[i]), next_gains, name=f"resid_ffn{i}")
        sv.update(y=y, x1=x1, h2=h2, p=p, s=s_act, y2=y2)
        saved.append(sv)
        xcur = res[0]
        if i + 1 < DEPTH:
            h1 = res[1]
        if i == N_A - 1:
            kvn = res[2]
            kv = mm_nn(kvn, *wmat("w_kv"), 1, 0, name="kv_proj").reshape(bsz, seq, 2 * qw)
            k_sub, v_sub = {}, {}
            for g, dil in enumerate(DILATIONS):
                if dil > 1:
                    k_sub[g] = perm_rows(kv, g, hw, dil, False, BF16, name=f"k_sub{g}")
                    v_sub[g] = perm_rows(kv, N_GROUPS + g, hw, dil, False, BF16, name=f"v_sub{g}")
            kv_state = {"kv": kv, "k_sub": k_sub, "v_sub": v_sub, "kvn": kvn, "x_a": xcur}

    dx, loss_tile = loss_fwd_bwd(xcur, loss_target.reshape(t, d))
    loss = lax.psum(loss_tile[0, 0], ("x", "y", "c"))

    gsm = {n: [None] * w_in[n].shape[0] for n in
           ["mix_pre_g", "mix_post_g", "ffn_pre_g", "ffn_post_g", "cm_b_in", "cm_dw", "cm_dw_b", "cm_ln_g", "cm_ln_b",
            "cm_b_out", "ffn_dw", "ffn_dw_b"]}
    gbig = {}
    in_flight = []
    dep = None

    def start_scatter(pcs, tag):
        ssem, rsem, g_f, r_f, tok = scatter_start([gbig[pc] for pc in pcs], name=f"scatter_start_{tag}")
        in_flight.append((pcs, ssem, rsem, g_f, r_f))
        return tok

    dk_acc = {g: None for g in range(N_GROUPS)}
    dv_acc = {g: None for g in range(N_GROUPS)}
    for i in range(DEPTH - 1, -1, -1):
        sv = saved[i]
        dy2, dg, _ = norm_bwd(sv["y2"], _row(ffn_post_g[i]), dx, out_dtype=BF16, name=f"bwd_ffn_post{i}", dep=dep)
        gsm["ffn_post_g"][i] = dg
        ds = mm_nt(dy2, *wmat("ffn_w_out", i), 1, 0, name=f"bwd_ffn_out_dx{i}")
        gbig[("ffn_w_out", i)] = mm_tn(sv["s"], dy2, 1, name=f"bwd_ffn_out_dw{i}").reshape(N_CHIPS, f // N_CHIPS, d)
        dp, ddw, ddwb = ffn_mid_bwd(sv["p"].reshape(bsz, seq, f2), small_full["ffn_dw"][i], _row(ffn_dw_b[i]), ds.reshape(bsz, seq, f),
                                    name=f"bwd_ffn_mid{i}")
        gsm["ffn_dw"][i] = jnp.sum(ddw, axis=1)
        gsm["ffn_dw_b"][i] = ddwb
        dp = dp.reshape(t, f2)
        dh2 = mm_nt(dp, *wmat("ffn_w_in", i), 1, 0, name=f"bwd_ffn_in_dx{i}")
        gbig[("ffn_w_in", i)] = mm_tn(sv["h2"], dp, N_CHIPS, name=f"bwd_ffn_in_dw{i}")
        dx1, dg, _ = norm_bwd(sv["x1"], _row(ffn_pre_g[i]), dh2, add=dx, name=f"bwd_ffn_pre{i}")
        gsm["ffn_pre_g"][i] = dg
        dep = start_scatter([("ffn_w_in", 0), ("ffn_w_out", 0)], "ffn0") if i == 0 else None
        dy, dg, dbias = norm_bwd(sv["y"], _row(mix_post_g[i]), dx1, out_dtype=BF16, name=f"bwd_mix_post{i}", dep=dep)
        gsm["mix_post_g"][i] = dg
        if i < N_A:
            gsm["cm_b_out"][i] = dbias
            du4 = mm_nt(dy, *wmat("cm_w_out", i), 1, 0, name=f"bwd_cm_out_dx{i}")
            gbig[("cm_w_out", i)] = mm_tn(sv["u4"], dy, 1, name=f"bwd_cm_out_dw{i}").reshape(N_CHIPS, d // N_CHIPS, d)
            du2, dlg, dlb = ln_silu_bwd(sv["u2"], _row(small_full["cm_ln_g"][i]), _row(small_full["cm_ln_b"][i]), du4,
                                        name=f"bwd_ln_silu{i}")
            gsm["cm_ln_g"][i], gsm["cm_ln_b"][i] = dlg, dlb
            dza, dzg, ddw, ddwb, dba, dbg = glu_conv_bwd(sv["z"].reshape(bsz, seq, 2 * d), small_full["cm_dw"][i],
                                                         du2.reshape(bsz, seq, d), name=f"bwd_glu_conv{i}")
            gsm["cm_dw"][i] = jnp.sum(ddw, axis=1)
            gsm["cm_dw_b"][i] = ddwb
            gsm["cm_b_in"][i] = jnp.concatenate([dba, dbg], axis=-1)
            dz = jnp.concatenate([dza, dzg], axis=-1).reshape(t, 2 * d)
            dh1 = mm_nt(dz, *wmat("cm_w_in", i), 1, 0, name=f"bwd_cm_in_dx{i}")
            gbig[("cm_w_in", i)] = mm_tn(sv["h1"], dz, N_CHIPS, name=f"bwd_cm_in_dw{i}")
        else:
            j = i - N_A
            dmerged = mm_nt(dy, *wmat("w_o", j), 1, 0, name=f"bwd_o_proj_dx{j}")
            gbig[("w_o", j)] = mm_tn(sv["merged"], dy, 1, name=f"bwd_o_proj_dw{j}").reshape(N_CHIPS, hw // N_CHIPS, d)
            dmt = attn_bwd_prep(dmerged, sv["merged"], name=f"bwd_attn_prep{j}")
            dq_parts = []
            for g, dil in enumerate(DILATIONS):
                nblk = seq // dil // BLK
                do3 = dmerged.reshape(bsz, seq, hw)
                lj3 = sv["lsej"].reshape(bsz, seq, LANES)
                dm3 = dmt.reshape(bsz, seq, LANES)
                if dil == 1:
                    dq_g, dk_g, dv_g = attn_bwd(sv["q"], g, kv_state["kv"], g, kv_state["kv"], N_GROUPS + g, do3, lj3, dm3,
                                                dk_acc[g], dv_acc[g], nblk, hw, name=f"attn_bwd{j}_{g}")
                    dq_parts.append(dq_g.reshape(t, hw))
                else:
                    do_s = perm_rows(do3, 0, hw, dil, False, BF16, name=f"do_sub{j}_{g}")
                    lj_s = perm_rows(lj3, 0, LANES, dil, False, F32, name=f"lsej_sub{j}_{g}", tc=LANES)
                    dm_s = perm_rows(dm3, 0, LANES, dil, False, F32, name=f"dm_sub{j}_{g}", tc=LANES)
                    dq_g, dk_g, dv_g = attn_bwd(sv["q_subs"][g], 0, kv_state["k_sub"][g], 0, kv_state["v_sub"][g], 0, do_s,
                                                lj_s, dm_s, dk_acc[g], dv_acc[g], nblk, hw, name=f"attn_bwd{j}_{g}")
                    dq_parts.append(perm_rows(dq_g, 0, hw, dil, True, F32, name=f"dq_tok{j}_{g}").reshape(t, hw))
                dk_acc[g], dv_acc[g] = dk_g, dv_g
            dq = jnp.concatenate(dq_parts, axis=-1).astype(BF16)
            dh1 = mm_nt(dq, *wmat("w_q", j), 1, 0, name=f"bwd_q_proj_dx{j}")
            gbig[("w_q", j)] = mm_tn(sv["h1"], dq, N_CHIPS, name=f"bwd_q_proj_dw{j}")
        dx, dg, _ = norm_bwd(sv["x_in"], _row(mix_pre_g[i]), dh1, add=dx1, name=f"bwd_mix_pre{i}")
        gsm["mix_pre_g"][i] = dg
        if i > N_A:
            dep = start_scatter([("ffn_w_in", i), ("ffn_w_out", i), ("w_q", i - N_A), ("w_o", i - N_A)], f"l{i}")
        elif 0 < i < N_A:
            dep = start_scatter([("ffn_w_in", i), ("ffn_w_out", i), ("cm_w_in", i), ("cm_w_out", i)], f"l{i}")
        elif i == 0:
            start_scatter([("cm_w_in", 0), ("cm_w_out", 0)], "cm0")
        if i == N_A:
            dkv_parts = []
            for acc in (dk_acc, dv_acc):
                for g, dil in enumerate(DILATIONS):
                    part = acc[g]
                    if dil > 1:
                        part = perm_rows(part, 0, hw, dil, True, F32, name=f"dkv_tok{len(dkv_parts)}")
                    dkv_parts.append(part.reshape(t, hw))
            dkv = jnp.concatenate(dkv_parts, axis=-1).astype(BF16)
            dkvn = mm_nt(dkv, *wmat("w_kv"), 1, 0, name="bwd_kv_proj_dx")
            gbig[("w_kv", None)] = mm_tn(kv_state["kvn"], dkv, N_CHIPS, name="bwd_kv_proj_dw")
            dx, dg_kv, _ = norm_bwd(kv_state["x_a"], _row(kv_norm_g), dkvn, add=dx, name="bwd_kv_norm")
            dep = start_scatter([("ffn_w_in", i), ("ffn_w_out", i), ("w_q", 0), ("w_o", 0), ("w_kv", None)], f"l{i}")
    grad_x = dx.reshape(bsz, seq, d)

    plane_of = {}
    for k, (pcs, ssem, rsem, g_f, r_f) in enumerate(in_flight):
        g_done, r_done = scatter_wait(ssem, rsem, g_f, r_f, dx, name=f"scatter_wait{k}")
        for pc, g_arr, r_arr in zip(pcs, g_done, r_done):
            plane_of[pc] = sum_parts(g_arr, r_arr, me_arr, name=f"sum_chips_{pc[0]}_{pc[1]}")
    plane = []
    for n in big:
        if w_in[n].ndim == 2:
            plane.append(plane_of[(n, None)])
        else:
            plane.append(jnp.concatenate([plane_of[(n, l)] for l in range(w_in[n].shape[0])], axis=0))
    other = swap_with_sibling(plane)

    outs_g, outs_d, outs_m, outs_v = {}, {}, {}, {}
    for n, p_mine, p_other in zip(big, plane, other):
        shp = w_in[n].shape
        flat = lambda a: a.reshape(-1, shp[-1])
        g_, d_, m_, v_ = adamw(flat(w_in[n]), flat(m_in[n]), flat(v_in[n]), [p_mine, p_other], name=f"adamw_{n}")
        outs_g[n], outs_d[n], outs_m[n], outs_v[n] = (a.reshape(shp) for a in (g_, d_, m_, v_))

    small_names = [n for n in names if n not in big]
    small_shapes_full = {}
    small_grads_full = []
    for n in small_names:
        if n == "kv_norm_g":
            gfull = dg_kv.reshape(-1)
        elif n in ("cm_dw", "ffn_dw"):
            gfull = jnp.stack(gsm[n], axis=0)
        else:
            gfull = jnp.stack([a.reshape(-1) for a in gsm[n]], axis=0)
        small_shapes_full[n] = gfull.shape
        small_grads_full.append(gfull)
    summed = allreduce_small(_pack(small_grads_full))
    g_full = dict(zip(small_names, _unpack(summed, [small_shapes_full[n] for n in small_names])))
    g_loc = {}
    for n in small_names:
        if n in small_sharded:
            width = w_in[n].shape[-1]
            g_loc[n] = lax.dynamic_slice_in_dim(g_full[n], me_chip * width, width, axis=g_full[n].ndim - 1)
        else:
            g_loc[n] = g_full[n]
    res = adamw(_pack([w_in[n] for n in small_names]), _pack([m_in[n] for n in small_names]),
                _pack([v_in[n] for n in small_names]), [_pack([g_loc[n] for n in small_names])], name="adamw_small")
    shapes_loc = [w_in[n].shape for n in small_names]
    for dst, packed in zip((outs_g, outs_d, outs_m, outs_v), res):
        for n, a in zip(small_names, _unpack(packed, shapes_loc)):
            dst[n] = a

    return (loss, grad_x, *[outs_g[n] for n in names], *[outs_d[n] for n in names],
            *[outs_m[n] for n in names], *[outs_v[n] for n in names])
```

```python
import functools
import math

import jax
import jax.numpy as jnp
from jax import lax
from jax.experimental import pallas as pl
from jax.experimental.pallas import tpu as pltpu

F32 = jnp.float32
BF16 = jnp.bfloat16
EPS = 1e-6
NEG_INF = -1e30
N_A = 2
DEPTH = 4
N_GROUPS = 3
DILATIONS = (1, 4, 16)
HEAD_DIM = 128
BLK = 128
LANES = 128
N_CHIPS = 4
N_DEV = 8
VMEM_LIMIT_V7X = 56 * 1024 * 1024

ADAM_LR = 0.001
ADAM_B1 = 0.9
ADAM_B2 = 0.999
ADAM_EPS = 1e-08
ADAM_WD = 0.01
ADAM_STEP = 10

MESH = pl.DeviceIdType.MESH


def _cp(*sem):
    return pltpu.CompilerParams(dimension_semantics=sem if sem else None, vmem_limit_bytes=VMEM_LIMIT_V7X)


def _dot(a, b):
    return jnp.dot(a, b, preferred_element_type=F32)


def _dot_nt(a, b):
    return lax.dot_general(a, b, (((1,), (1,)), ((), ())), preferred_element_type=F32)


def _dot_tn(a, b):
    return lax.dot_general(a, b, (((0,), (0,)), ((), ())), preferred_element_type=F32)


def _sigmoid(x):
    return 1.0 / (1.0 + jnp.exp(-x))


def _row_tile(n, want):
    if n <= want:
        return n
    for t in range(want - want % 8, 7, -8):
        if n % t == 0:
            return t
    raise ValueError(f"no row tile for {n} rows")


def mm_nn(a, w, nsh, stride, layer, bias=None, out_dtype=F32, name="mm_nn", tm=512):
    m, k = a.shape
    _, k2, ns = w.shape
    assert k == k2
    tm = _row_tile(m, tm)
    has_bias = bias is not None

    def body(*refs):
        if has_bias:
            a_ref, w_ref, b_ref, o_ref = refs
        else:
            a_ref, w_ref, o_ref = refs
        acc = _dot(a_ref[...].astype(BF16), w_ref[...])
        if has_bias:
            acc = acc + b_ref[...]
        o_ref[...] = acc.astype(out_dtype)

    in_specs = [
        pl.BlockSpec((tm, k), lambda j, i: (i, 0)),
        pl.BlockSpec((None, k, ns), lambda j, i: (j * stride + layer, 0, 0)),
    ]
    args = [a, w]
    if has_bias:
        in_specs.append(pl.BlockSpec((1, ns), lambda j, i: (0, j)))
        args.append(bias)
    return pl.pallas_call(
        body,
        name=name,
        grid=(nsh, m // tm),
        in_specs=in_specs,
        out_specs=pl.BlockSpec((tm, ns), lambda j, i: (i, j)),
        out_shape=jax.ShapeDtypeStruct((m, nsh * ns), out_dtype),
        compiler_params=_cp("parallel", "parallel"),
    )(*args)


def mm_nt(dy, w, nsh, stride, layer, out_dtype=F32, name="mm_nt", tm=512):
    dys = list(dy) if isinstance(dy, (list, tuple)) else [dy]
    npart = len(dys)
    per = nsh // npart
    m = dys[0].shape[0]
    _, k, ns = w.shape
    assert all(d.shape == (m, per * ns) for d in dys)
    tm = _row_tile(m, tm)

    def body(*refs):
        dy_refs = refs[:npart]
        w_ref, o_ref, acc_ref = refs[npart:]
        j = pl.program_id(1)

        @pl.when(j == 0)
        def _():
            acc_ref[...] = jnp.zeros(acc_ref.shape, F32)

        for pi in range(npart):
            @pl.when(j // per == pi)
            def _(pi=pi):
                acc_ref[...] += _dot_nt(dy_refs[pi][...].astype(BF16), w_ref[...])

        @pl.when(j == nsh - 1)
        def _():
            o_ref[...] = acc_ref[...].astype(out_dtype)

    dy_specs = [pl.BlockSpec((tm, ns), lambda i, j, pi=pi: (i, jnp.clip(j - pi * per, 0, per - 1))) for pi in range(npart)]
    return pl.pallas_call(
        body,
        name=name,
        grid=(m // tm, nsh),
        in_specs=[*dy_specs, pl.BlockSpec((None, k, ns), lambda i, j: (j * stride + layer, 0, 0))],
        out_specs=pl.BlockSpec((tm, k), lambda i, j: (i, 0)),
        out_shape=jax.ShapeDtypeStruct((m, k), out_dtype),
        scratch_shapes=[pltpu.VMEM((tm, k), F32)],
        compiler_params=_cp("parallel", "arbitrary"),
    )(*dys, w)


def mm_tn(a, dy, nsh, name="mm_tn", tm=512):
    dys = list(dy) if isinstance(dy, (list, tuple)) else [dy]
    npart = len(dys)
    per = nsh // npart
    m, k = a.shape
    ns = dys[0].shape[1] // per
    assert all(d.shape == (m, per * ns) for d in dys)
    tm = _row_tile(m, tm)
    nt = m // tm

    def body(*refs):
        a_ref = refs[0]
        dy_refs = refs[1:1 + npart]
        o_ref, acc_ref = refs[1 + npart:]
        j = pl.program_id(0)
        i = pl.program_id(1)

        @pl.when(i == 0)
        def _():
            acc_ref[...] = jnp.zeros(acc_ref.shape, F32)

        for pi in range(npart):
            @pl.when(j // per == pi)
            def _(pi=pi):
                acc_ref[...] += _dot_tn(a_ref[...].astype(BF16), dy_refs[pi][...].astype(BF16))

        @pl.when(i == nt - 1)
        def _():
            o_ref[...] = acc_ref[...].astype(BF16)

    dy_specs = [
        pl.BlockSpec((tm, ns), lambda j, i, pi=pi: (jnp.where(j // per == pi, i, 0), jnp.clip(j - pi * per, 0, per - 1)))
        for pi in range(npart)
    ]
    return pl.pallas_call(
        body,
        name=name,
        grid=(nsh, nt),
        in_specs=[pl.BlockSpec((tm, k), lambda j, i: (i, 0)), *dy_specs],
        out_specs=pl.BlockSpec((None, k, ns), lambda j, i: (j, 0, 0)),
        out_shape=jax.ShapeDtypeStruct((nsh, k, ns), BF16),
        scratch_shapes=[pltpu.VMEM((k, ns), F32)],
        compiler_params=_cp("parallel", "arbitrary"),
    )(a, *dys)


DEP_SPEC_SHAPE = (8, LANES)


def resid_norm_fwd(x, y, g_post, next_gains, name, tm=256, dep=None):
    t, d = x.shape
    tm = _row_tile(t, tm)
    has_y = y is not None
    n_next = len(next_gains)
    n_dep = 0 if dep is None else 1

    def body(*refs):
        x_ref = refs[0]
        pos = 1
        if has_y:
            y_ref, gp_ref = refs[1], refs[2]
            pos = 3
        gn_refs = refs[pos:pos + n_next]
        outs = refs[pos + n_next + n_dep:]
        xv = x_ref[...]
        o = 0
        if has_y:
            yv = y_ref[...]
            r = lax.rsqrt(jnp.mean(yv * yv, axis=-1, keepdims=True) + EPS)
            xv = xv + (yv * r) * gp_ref[...]
            outs[0][...] = xv
            o = 1
        if n_next:
            xn = xv * lax.rsqrt(jnp.mean(xv * xv, axis=-1, keepdims=True) + EPS)
            for k in range(n_next):
                outs[o + k][...] = (xn * gn_refs[k][...]).astype(BF16)

    row = pl.BlockSpec((tm, d), lambda i: (i, 0))
    vec = pl.BlockSpec((1, d), lambda i: (0, 0))
    args, in_specs = [x], [row]
    if has_y:
        args += [y, g_post]
        in_specs += [row, vec]
    args += list(next_gains)
    in_specs += [vec] * n_next
    if n_dep:
        args.append(dep)
        in_specs.append(pl.BlockSpec(DEP_SPEC_SHAPE, lambda i: (0, 0)))
    out_shape, out_specs = [], []
    if has_y:
        out_shape.append(jax.ShapeDtypeStruct((t, d), F32))
        out_specs.append(row)
    for _ in range(n_next):
        out_shape.append(jax.ShapeDtypeStruct((t, d), BF16))
        out_specs.append(row)
    return pl.pallas_call(
        body, name=name, grid=(t // tm,), in_specs=in_specs, out_specs=out_specs, out_shape=out_shape,
        compiler_params=_cp("parallel"),
    )(*args)


def norm_bwd(x, g, dy, add=None, out_dtype=F32, name="norm_bwd", tm=256, dep=None):
    t, d = x.shape
    tm = _row_tile(t, tm)
    has_add = add is not None

    def body(*refs):
        x_ref, g_ref, dy_ref = refs[:3]
        add_ref = refs[3] if has_add else None
        dx_ref, dg_ref, cs_ref = refs[-3:]
        i = pl.program_id(0)
        xv = x_ref[...]
        dyv = dy_ref[...].astype(F32)
        r = lax.rsqrt(jnp.mean(xv * xv, axis=-1, keepdims=True) + EPS)
        gd = dyv * g_ref[...]
        dx = r * gd - xv * ((r * r * r) * jnp.mean(xv * gd, axis=-1, keepdims=True))
        if has_add:
            dx = dx + add_ref[...]
        dx_ref[...] = dx.astype(out_dtype)
        dg = jnp.sum(dyv * (xv * r), axis=0, keepdims=True)
        cs = jnp.sum(dx, axis=0, keepdims=True)

        @pl.when(i == 0)
        def _():
            dg_ref[...] = dg
            cs_ref[...] = cs

        @pl.when(i > 0)
        def _():
            dg_ref[...] += dg
            cs_ref[...] += cs

    row = pl.BlockSpec((tm, d), lambda i: (i, 0))
    vec = pl.BlockSpec((1, d), lambda i: (0, 0))
    args, in_specs = [x, g, dy], [row, vec, row]
    if has_add:
        args.append(add)
        in_specs.append(row)
    if dep is not None:
        args.append(dep)
        in_specs.append(pl.BlockSpec(DEP_SPEC_SHAPE, lambda i: (0, 0)))
    return pl.pallas_call(
        body, name=name, grid=(t // tm,), in_specs=in_specs,
        out_specs=[row, vec, vec],
        out_shape=[jax.ShapeDtypeStruct((t, d), out_dtype), jax.ShapeDtypeStruct((1, d), F32),
                   jax.ShapeDtypeStruct((1, d), F32)],
        compiler_params=_cp("arbitrary"),
    )(*args)


def loss_fwd_bwd(x, target, name="loss", tm=256):
    t, d = x.shape
    tm = _row_tile(t, tm)

    def body(x_ref, t_ref, dx_ref, l_ref):
        i = pl.program_id(0)
        err = x_ref[...] - t_ref[...]
        dx_ref[...] = err * (1.0 / d)
        part = 0.5 * jnp.sum(jnp.mean(err * err, axis=-1, keepdims=True), axis=0, keepdims=True)
        part = jnp.broadcast_to(part, l_ref.shape)

        @pl.when(i == 0)
        def _():
            l_ref[...] = part

        @pl.when(i > 0)
        def _():
            l_ref[...] += part

    row = pl.BlockSpec((tm, d), lambda i: (i, 0))
    return pl.pallas_call(
        body, name=name, grid=(t // tm,), in_specs=[row, row],
        out_specs=[row, pl.BlockSpec((8, LANES), lambda i: (0, 0))],
        out_shape=[jax.ShapeDtypeStruct((t, d), F32), jax.ShapeDtypeStruct((8, LANES), F32)],
        compiler_params=_cp("arbitrary"),
    )(x, target)


CONV_HALO = 32
CONV_CHUNK = 128


def glu_conv_fwd(z, dw, dwb, name, tc=128):
    b, s, c2 = z.shape
    c = c2 // 2
    kw = dw.shape[0]
    tc = min(tc, c)
    nc = c // tc
    ch = min(CONV_CHUNK, s)
    halo = CONV_HALO
    assert kw - 1 <= halo and s % ch == 0

    nch = s // ch

    def body(a_ref, g_ref, w_ref, b_ref, o_ref, pad_ref):
        _fill_glu_slabs(a_ref, g_ref, pad_ref, nch, ch, halo)

        def chunk(ci, carry):
            r0 = pl.multiple_of(ci * ch, ch)
            acc = b_ref[...]
            for k, tap in enumerate(_taps_front(pad_ref, ci, kw, ch, halo)):
                acc = acc + w_ref[k:k + 1, :] * tap
            o_ref[pl.ds(r0, ch), :] = acc
            return carry

        lax.fori_loop(0, nch, chunk, 0)

    return pl.pallas_call(
        body, name=name, grid=(b, nc),
        in_specs=[
            pl.BlockSpec((None, s, tc), lambda bi, i: (bi, 0, i)),
            pl.BlockSpec((None, s, tc), lambda bi, i: (bi, 0, i + nc)),
            pl.BlockSpec((kw, tc), lambda bi, i: (0, i)),
            pl.BlockSpec((1, tc), lambda bi, i: (0, i)),
        ],
        out_specs=pl.BlockSpec((None, s, tc), lambda bi, i: (bi, 0, i)),
        out_shape=jax.ShapeDtypeStruct((b, s, c), F32),
        scratch_shapes=[pltpu.VMEM((nch, ch + halo, tc), F32)],
        compiler_params=_cp("parallel", "parallel"),
    )(z, z, dw, dwb)


def glu_conv_bwd(z, dw, du2, name, tc=128):
    b, s, c2 = z.shape
    c = c2 // 2
    kw = dw.shape[0]
    tc = min(tc, c)
    nc = c // tc
    ch = min(CONV_CHUNK, s)
    nch = s // ch
    halo = CONV_HALO

    def body(a_ref, g_ref, w_ref, du_ref, dza_ref, dzg_ref, ddw_ref, ddwb_ref, dba_ref, dbg_ref, upad_ref, dpad_ref):
        bi = pl.program_id(1)

        @pl.when(bi == 0)
        def _():
            ddw_ref[...] = jnp.zeros(ddw_ref.shape, F32)
            ddwb_ref[...] = jnp.zeros(ddwb_ref.shape, F32)
            dba_ref[...] = jnp.zeros(dba_ref.shape, F32)
            dbg_ref[...] = jnp.zeros(dbg_ref.shape, F32)

        _fill_glu_slabs(a_ref, g_ref, upad_ref, nch, ch, halo)
        dpad_ref[nch - 1, ch:ch + halo, :] = jnp.zeros((halo, tc), F32)
        dpad_ref[nch - 1, 0:ch, :] = du_ref[s - ch:s, :]

        def fill(ci, carry):
            r0 = pl.multiple_of(ci * ch, ch)
            dpad_ref[ci, :, :] = du_ref[pl.ds(r0, ch + halo), :]
            return carry

        lax.fori_loop(0, nch - 1, fill, 0)

        def chunk(ci, carry):
            r0 = pl.multiple_of(ci * ch, ch)
            du_c = du_ref[pl.ds(r0, ch), :]
            taps_u = _taps_front(upad_ref, ci, kw, ch, halo)
            du1 = w_ref[kw - 1:kw, :] * du_c
            ddw_ref[kw - 1] += jnp.sum((du_c * taps_u[kw - 1]).reshape(ch // 8, 8, tc), axis=0)
            for j in range(1, kw):
                du1 = du1 + w_ref[kw - 1 - j:kw - j, :] * dpad_ref[ci, j:j + ch, :]
                ddw_ref[kw - 1 - j] += jnp.sum((du_c * taps_u[kw - 1 - j]).reshape(ch // 8, 8, tc), axis=0)
            av = a_ref[pl.ds(r0, ch), :]
            sg = _sigmoid(g_ref[pl.ds(r0, ch), :])
            dza = du1 * sg
            dzg = du1 * av * (sg * (1.0 - sg))
            dza_ref[pl.ds(r0, ch), :] = dza.astype(BF16)
            dzg_ref[pl.ds(r0, ch), :] = dzg.astype(BF16)
            dba_ref[...] += jnp.sum(dza, axis=0, keepdims=True)
            dbg_ref[...] += jnp.sum(dzg, axis=0, keepdims=True)
            ddwb_ref[...] += jnp.sum(du_c, axis=0, keepdims=True)
            return carry

        lax.fori_loop(0, s // ch, chunk, 0)

    blk = lambda off: pl.BlockSpec((None, s, tc), lambda i, bi: (bi, 0, i + off))
    vec = pl.BlockSpec((1, tc), lambda i, bi: (0, i))
    return pl.pallas_call(
        body, name=name, grid=(nc, b),
        in_specs=[blk(0), blk(nc), pl.BlockSpec((kw, tc), lambda i, bi: (0, i)), blk(0)],
        out_specs=[blk(0), blk(0), pl.BlockSpec((kw, 8, tc), lambda i, bi: (0, 0, i)), vec, vec, vec],
        out_shape=[
            jax.ShapeDtypeStruct((b, s, c), BF16), jax.ShapeDtypeStruct((b, s, c), BF16),
            jax.ShapeDtypeStruct((kw, 8, c), F32), jax.ShapeDtypeStruct((1, c), F32),
            jax.ShapeDtypeStruct((1, c), F32), jax.ShapeDtypeStruct((1, c), F32),
        ],
        scratch_shapes=[pltpu.VMEM((nch, ch + halo, tc), F32), pltpu.VMEM((nch, ch + halo, tc), F32)],
        compiler_params=_cp("parallel", "arbitrary"),
    )(z, z, dw, du2)


def _fill_glu_slabs(a_ref, g_ref, pad_ref, nch, ch, halo):
    tc = a_ref.shape[-1]
    pad_ref[0, 0:halo, :] = jnp.zeros((halo, tc), F32)
    pad_ref[0, halo:halo + ch, :] = a_ref[0:ch, :] * _sigmoid(g_ref[0:ch, :])

    def fill(ci, carry):
        r0 = pl.multiple_of(ci * ch, ch)
        pad_ref[ci, 0:halo, :] = pad_ref[ci - 1, ch:ch + halo, :]
        pad_ref[ci, halo:halo + ch, :] = a_ref[pl.ds(r0, ch), :] * _sigmoid(g_ref[pl.ds(r0, ch), :])
        return carry

    lax.fori_loop(1, nch, fill, 0)


def ln_silu_fwd(u, g, bvec, name, tm=256):
    t, d = u.shape
    tm = _row_tile(t, tm)

    def body(u_ref, g_ref, b_ref, o_ref):
        uv = u_ref[...]
        mu = jnp.mean(uv, axis=-1, keepdims=True)
        xc = uv - mu
        var = jnp.mean(xc * xc, axis=-1, keepdims=True)
        v = (xc * lax.rsqrt(var + EPS)) * g_ref[...] + b_ref[...]
        o_ref[...] = (v * _sigmoid(v)).astype(BF16)

    row = pl.BlockSpec((tm, d), lambda i: (i, 0))
    vec = pl.BlockSpec((1, d), lambda i: (0, 0))
    return pl.pallas_call(
        body, name=name, grid=(t // tm,), in_specs=[row, vec, vec], out_specs=row,
        out_shape=jax.ShapeDtypeStruct((t, d), BF16), compiler_params=_cp("parallel"),
    )(u, g, bvec)


def ln_silu_bwd(u, g, bvec, dout, name, tm=256):
    t, d = u.shape
    tm = _row_tile(t, tm)

    def body(u_ref, g_ref, b_ref, do_ref, du_ref, dg_ref, db_ref):
        i = pl.program_id(0)
        uv = u_ref[...]
        mu = jnp.mean(uv, axis=-1, keepdims=True)
        xc = uv - mu
        var = jnp.mean(xc * xc, axis=-1, keepdims=True)
        rstd = lax.rsqrt(var + EPS)
        n = xc * rstd
        v = n * g_ref[...] + b_ref[...]
        sg = _sigmoid(v)
        dv = do_ref[...].astype(F32) * (sg * (1.0 + v * (1.0 - sg)))
        dn = dv * g_ref[...]
        du_ref[...] = rstd * (dn - jnp.mean(dn, axis=-1, keepdims=True) - n * jnp.mean(dn * n, axis=-1, keepdims=True))
        dg = jnp.sum(dv * n, axis=0, keepdims=True)
        db = jnp.sum(dv, axis=0, keepdims=True)

        @pl.when(i == 0)
        def _():
            dg_ref[...] = dg
            db_ref[...] = db

        @pl.when(i > 0)
        def _():
            dg_ref[...] += dg
            db_ref[...] += db

    row = pl.BlockSpec((tm, d), lambda i: (i, 0))
    vec = pl.BlockSpec((1, d), lambda i: (0, 0))
    return pl.pallas_call(
        body, name=name, grid=(t // tm,), in_specs=[row, vec, vec, row], out_specs=[row, vec, vec],
        out_shape=[jax.ShapeDtypeStruct((t, d), F32), jax.ShapeDtypeStruct((1, d), F32), jax.ShapeDtypeStruct((1, d), F32)],
        compiler_params=_cp("arbitrary"),
    )(u, g, bvec, dout)


FFN_HALO = 8


def _fill_front_halo(src_ref, pad_ref, nch, ch, halo):
    tc = src_ref.shape[-1]
    pad_ref[0, 0:halo, :] = jnp.zeros((halo, tc), F32)
    pad_ref[0, halo:halo + ch, :] = src_ref[0:ch, :].astype(F32)

    def fill(ci, carry):
        r0 = pl.multiple_of(ci * ch, ch)
        pad_ref[ci, 0:halo, :] = src_ref[pl.ds(r0 - halo, halo), :].astype(F32)
        pad_ref[ci, halo:halo + ch, :] = src_ref[pl.ds(r0, ch), :].astype(F32)
        return carry

    lax.fori_loop(1, nch, fill, 0)


def _taps_front(pad_ref, ci, kw, ch, halo):
    return [pad_ref[ci, halo - (kw - 1 - k):halo - (kw - 1 - k) + ch, :] for k in range(kw)]


def ffn_mid_fwd(p, dw, dwb, name, tc=256):
    b, s, f2 = p.shape
    f = f2 // 2
    kw = dw.shape[0]
    tc = min(tc, f)
    nf = f // tc
    ch = min(CONV_CHUNK, s)
    nch = s // ch
    halo = FFN_HALO

    def body(pa_ref, pg_ref, wa_ref, wg_ref, ba_ref, bg_ref, o_ref, apad_ref, gpad_ref):
        _fill_front_halo(pa_ref, apad_ref, nch, ch, halo)
        _fill_front_halo(pg_ref, gpad_ref, nch, ch, halo)

        def chunk(ci, carry):
            r0 = pl.multiple_of(ci * ch, ch)
            ca = ba_ref[...]
            cg = bg_ref[...]
            taps = zip(_taps_front(apad_ref, ci, kw, ch, halo), _taps_front(gpad_ref, ci, kw, ch, halo))
            for k, (ta, tg) in enumerate(taps):
                ca = ca + wa_ref[k:k + 1, :] * ta
                cg = cg + wg_ref[k:k + 1, :] * tg
            o_ref[pl.ds(r0, ch), :] = ((cg * _sigmoid(cg)) * ca).astype(BF16)
            return carry

        lax.fori_loop(0, nch, chunk, 0)

    blk = lambda off: pl.BlockSpec((None, s, tc), lambda bi, i: (bi, 0, i + off))
    wsp = lambda off: pl.BlockSpec((kw, tc), lambda bi, i: (0, i + off))
    bsp = lambda off: pl.BlockSpec((1, tc), lambda bi, i: (0, i + off))
    return pl.pallas_call(
        body, name=name, grid=(b, nf),
        in_specs=[blk(0), blk(nf), wsp(0), wsp(nf), bsp(0), bsp(nf)],
        out_specs=pl.BlockSpec((None, s, tc), lambda bi, i: (bi, 0, i)),
        out_shape=jax.ShapeDtypeStruct((b, s, f), BF16),
        scratch_shapes=[pltpu.VMEM((nch, ch + halo, tc), F32)] * 2,
        compiler_params=_cp("parallel", "parallel"),
    )(p, p, dw, dw, dwb, dwb)


def ffn_mid_bwd(p, dw, dwb, ds, name, tc=256):
    b, s, f2 = p.shape
    f = f2 // 2
    kw = dw.shape[0]
    tc = min(tc, f)
    nf = f // tc
    ch = min(CONV_CHUNK, s)
    nch = s // ch
    halo = FFN_HALO

    def sum8(v):
        return jnp.sum(v.reshape(ch // 8, 8, tc), axis=0)

    def body(pa_ref, pg_ref, wa_ref, wg_ref, ba_ref, bg_ref, ds_ref, dpa_ref, dpg_ref, ddwa_ref, ddwg_ref, dba_ref, dbg_ref,
             apad_ref, gpad_ref, dca_ref, dcg_ref):
        bi = pl.program_id(1)

        @pl.when(bi == 0)
        def _():
            ddwa_ref[...] = jnp.zeros(ddwa_ref.shape, F32)
            ddwg_ref[...] = jnp.zeros(ddwg_ref.shape, F32)
            dba_ref[...] = jnp.zeros(dba_ref.shape, F32)
            dbg_ref[...] = jnp.zeros(dbg_ref.shape, F32)

        _fill_front_halo(pa_ref, apad_ref, nch, ch, halo)
        _fill_front_halo(pg_ref, gpad_ref, nch, ch, halo)
        dca_ref[nch - 1, ch:ch + halo, :] = jnp.zeros((halo, tc), F32)
        dcg_ref[nch - 1, ch:ch + halo, :] = jnp.zeros((halo, tc), F32)

        def grads(ci, carry):
            acc_a, acc_g, sb_a, sb_g = carry
            r0 = pl.multiple_of(ci * ch, ch)
            taps_a = _taps_front(apad_ref, ci, kw, ch, halo)
            taps_g = _taps_front(gpad_ref, ci, kw, ch, halo)
            ca = ba_ref[...]
            cg = bg_ref[...]
            for k in range(kw):
                ca = ca + wa_ref[k:k + 1, :] * taps_a[k]
                cg = cg + wg_ref[k:k + 1, :] * taps_g[k]
            sg = _sigmoid(cg)
            dsv = ds_ref[pl.ds(r0, ch), :].astype(F32)
            dca = dsv * (cg * sg)
            dcg = dsv * ca * (sg * (1.0 + cg * (1.0 - sg)))
            dca_ref[ci, 0:ch, :] = dca
            dcg_ref[ci, 0:ch, :] = dcg

            prev = jnp.maximum(ci - 1, 0)

            @pl.when(ci > 0)
            def _():
                dca_ref[prev, ch:ch + halo, :] = dca[0:halo, :]
                dcg_ref[prev, ch:ch + halo, :] = dcg[0:halo, :]

            acc_a = tuple(acc_a[k] + sum8(dca * taps_a[k]) for k in range(kw))
            acc_g = tuple(acc_g[k] + sum8(dcg * taps_g[k]) for k in range(kw))
            return acc_a, acc_g, sb_a + sum8(dca), sb_g + sum8(dcg)

        z8 = jnp.zeros((8, tc), F32)
        acc_a, acc_g, sb_a, sb_g = lax.fori_loop(0, nch, grads, ((z8,) * kw, (z8,) * kw, z8, z8))
        for k in range(kw):
            ddwa_ref[k] += acc_a[k]
            ddwg_ref[k] += acc_g[k]
        dba_ref[...] += jnp.sum(sb_a, axis=0, keepdims=True)
        dbg_ref[...] += jnp.sum(sb_g, axis=0, keepdims=True)

        def back(ci, carry):
            r0 = pl.multiple_of(ci * ch, ch)
            da = wa_ref[kw - 1:kw, :] * dca_ref[ci, 0:ch, :]
            dg = wg_ref[kw - 1:kw, :] * dcg_ref[ci, 0:ch, :]
            for j in range(1, kw):
                da = da + wa_ref[kw - 1 - j:kw - j, :] * dca_ref[ci, j:j + ch, :]
                dg = dg + wg_ref[kw - 1 - j:kw - j, :] * dcg_ref[ci, j:j + ch, :]
            dpa_ref[pl.ds(r0, ch), :] = da.astype(BF16)
            dpg_ref[pl.ds(r0, ch), :] = dg.astype(BF16)
            return carry

        lax.fori_loop(0, nch, back, 0)

    blk = lambda off: pl.BlockSpec((None, s, tc), lambda i, bi: (bi, 0, i + off))
    wsp = lambda off: pl.BlockSpec((kw, tc), lambda i, bi: (0, i + off))
    bsp = lambda off: pl.BlockSpec((1, tc), lambda i, bi: (0, i + off))
    acc3 = pl.BlockSpec((kw, 8, tc), lambda i, bi: (0, 0, i))
    vec = pl.BlockSpec((1, tc), lambda i, bi: (0, i))
    return pl.pallas_call(
        body, name=name, grid=(nf, b),
        in_specs=[blk(0), blk(nf), wsp(0), wsp(nf), bsp(0), bsp(nf), blk(0)],
        out_specs=[blk(0), blk(0), acc3, acc3, vec, vec],
        out_shape=[jax.ShapeDtypeStruct((b, s, f), BF16), jax.ShapeDtypeStruct((b, s, f), BF16),
                   jax.ShapeDtypeStruct((kw, 8, f), F32), jax.ShapeDtypeStruct((kw, 8, f), F32),
                   jax.ShapeDtypeStruct((1, f), F32), jax.ShapeDtypeStruct((1, f), F32)],
        scratch_shapes=[pltpu.VMEM((nch, ch + halo, tc), F32)] * 4,
        compiler_params=_cp("parallel", "arbitrary"),
    )(p, p, dw, dw, dwb, dwb, ds)


def perm_rows(x, col_blk, width, dil, inverse, out_dtype, name, tc=LANES):
    b, s, _ = x.shape
    tc = min(tc, width)
    nc = width // tc
    ln = s // dil

    def body(x_ref, o_ref):
        if dil == 1:
            o_ref[...] = x_ref[...].astype(out_dtype)
            return
        for r in range(dil):
            if inverse:
                o_ref[pl.ds(r, ln, stride=dil), :] = x_ref[r * ln:(r + 1) * ln, :].astype(out_dtype)
            else:
                o_ref[r * ln:(r + 1) * ln, :] = x_ref[pl.ds(r, ln, stride=dil), :].astype(out_dtype)

    return pl.pallas_call(
        body, name=name, grid=(b, nc),
        in_specs=[pl.BlockSpec((None, s, tc), lambda bi, i: (bi, 0, col_blk * nc + i))],
        out_specs=pl.BlockSpec((None, s, tc), lambda bi, i: (bi, 0, i)),
        out_shape=jax.ShapeDtypeStruct((b, s, width), out_dtype),
        compiler_params=_cp("parallel", "parallel"),
    )(x)


def _attn_masks(has_other):
    qi = lax.broadcasted_iota(jnp.int32, (BLK, BLK), 0)
    kk = lax.broadcasted_iota(jnp.int32, (BLK, BLK), 1)
    return kk <= qi, jnp.logical_and(kk >= qi, has_other)


def attn_fwd(q, q_blk, k, k_blk, v, v_blk, nblk, hw, name):
    b, s, _ = q.shape
    nh = hw // HEAD_DIM
    scale = 1.0 / math.sqrt(HEAD_DIM)

    def body(q_ref, kc_ref, kp_ref, vc_ref, vp_ref, o_ref, lse_ref):
        t = pl.program_id(1)
        mask_c, mask_p = _attn_masks((t % nblk) != 0)
        lane = lax.broadcasted_iota(jnp.int32, (BLK, LANES), 1)
        lse_acc = jnp.zeros((BLK, LANES), F32)
        for h in range(nh):
            sl = slice(h * HEAD_DIM, (h + 1) * HEAD_DIM)
            qh = q_ref[:, sl].astype(BF16)
            kc = kc_ref[:, sl].astype(BF16)
            kp = kp_ref[:, sl].astype(BF16)
            s_c = jnp.where(mask_c, _dot_nt(qh, kc) * scale, NEG_INF)
            s_p = jnp.where(mask_p, _dot_nt(qh, kp) * scale, NEG_INF)
            m = jnp.maximum(jnp.max(s_c, axis=-1, keepdims=True), jnp.max(s_p, axis=-1, keepdims=True))
            p_c = jnp.exp(s_c - m)
            p_p = jnp.exp(s_p - m)
            den = jnp.sum(p_c, axis=-1, keepdims=True) + jnp.sum(p_p, axis=-1, keepdims=True)
            o = _dot(p_c.astype(BF16), vc_ref[:, sl].astype(BF16)) + _dot(p_p.astype(BF16), vp_ref[:, sl].astype(BF16))
            o_ref[:, sl] = o / den
            lse_acc = jnp.where(lane == h, m + jnp.log(den), lse_acc)
        lse_ref[...] = lse_acc

    cur = lambda cb: pl.BlockSpec((None, BLK, hw), lambda bi, t: (bi, t, cb))
    prev = lambda cb: pl.BlockSpec((None, BLK, hw), lambda bi, t: (bi, jnp.where(t % nblk == 0, t, t - 1), cb))
    return pl.pallas_call(
        body, name=name, grid=(b, s // BLK),
        in_specs=[cur(q_blk), cur(k_blk), prev(k_blk), cur(v_blk), prev(v_blk)],
        out_specs=[pl.BlockSpec((None, BLK, hw), lambda bi, t: (bi, t, 0)),
                   pl.BlockSpec((None, BLK, LANES), lambda bi, t: (bi, t, 0))],
        out_shape=[jax.ShapeDtypeStruct((b, s, hw), F32), jax.ShapeDtypeStruct((b, s, LANES), F32)],
        compiler_params=_cp("parallel", "parallel"),
    )(q, k, k, v, v)


def attn_merge(outs, lses, name, tm=256):
    t, hw = outs[0].shape
    nh = hw // HEAD_DIM
    tm = _row_tile(t, tm)
    ng = len(outs)

    def body(*refs):
        o_refs, l_refs = refs[:ng], refs[ng:2 * ng]
        m_ref, lj_ref = refs[2 * ng:]
        lane = lax.broadcasted_iota(jnp.int32, (tm, LANES), 1)
        lj = jnp.zeros((tm, LANES), F32)
        for h in range(nh):
            sl = slice(h * HEAD_DIM, (h + 1) * HEAD_DIM)
            ls = [l_refs[g][:, h:h + 1] for g in range(ng)]
            mx = functools.reduce(jnp.maximum, ls)
            es = [jnp.exp(l - mx) for l in ls]
            tot = functools.reduce(lambda a, c: a + c, es)
            acc = (es[0] / tot) * o_refs[0][:, sl]
            for g in range(1, ng):
                acc = acc + (es[g] / tot) * o_refs[g][:, sl]
            m_ref[:, sl] = acc.astype(BF16)
            lj = jnp.where(lane == h, mx + jnp.log(tot), lj)
        lj_ref[...] = lj

    row = pl.BlockSpec((tm, hw), lambda i: (i, 0))
    st = pl.BlockSpec((tm, LANES), lambda i: (i, 0))
    return pl.pallas_call(
        body, name=name, grid=(t // tm,), in_specs=[row] * ng + [st] * ng, out_specs=[row, st],
        out_shape=[jax.ShapeDtypeStruct((t, hw), BF16), jax.ShapeDtypeStruct((t, LANES), F32)],
        compiler_params=_cp("parallel"),
    )(*outs, *lses)


def attn_bwd_prep(dmerged, merged, name, tm=256):
    t, hw = merged.shape
    nh = hw // HEAD_DIM
    tm = _row_tile(t, tm)

    def body(d_ref, m_ref, o_ref):
        lane = lax.broadcasted_iota(jnp.int32, (tm, LANES), 1)
        acc = jnp.zeros((tm, LANES), F32)
        for h in range(nh):
            sl = slice(h * HEAD_DIM, (h + 1) * HEAD_DIM)
            dsum = jnp.sum(d_ref[:, sl] * m_ref[:, sl].astype(F32), axis=-1, keepdims=True)
            acc = jnp.where(lane == h, dsum, acc)
        o_ref[...] = acc

    row = pl.BlockSpec((tm, hw), lambda i: (i, 0))
    return pl.pallas_call(
        body, name=name, grid=(t // tm,), in_specs=[row, row], out_specs=pl.BlockSpec((tm, LANES), lambda i: (i, 0)),
        out_shape=jax.ShapeDtypeStruct((t, LANES), F32), compiler_params=_cp("parallel"),
    )(dmerged, merged)


def attn_bwd(q, q_blk, k, k_blk, v, v_blk, do, lsej, dm, dk_add, dv_add, nblk, hw, name):
    b, s, _ = q.shape
    nh = hw // HEAD_DIM
    scale = 1.0 / math.sqrt(HEAD_DIM)
    has_add = dk_add is not None

    def body(*refs):
        (qm_ref, qn_ref, kc_ref, kp_ref, vc_ref, vp_ref, dom_ref, don_ref, lm_ref, ln_ref, dmm_ref, dmn_ref) = refs[:12]
        if has_add:
            dka_ref, dva_ref = refs[12:14]
            dq_ref, dk_ref, dv_ref = refs[14:]
        else:
            dq_ref, dk_ref, dv_ref = refs[12:]
        t = pl.program_id(1)
        n = t % nblk
        mask_c, mask_p = _attn_masks(n != 0)
        _, mask_n = _attn_masks(n != nblk - 1)
        for h in range(nh):
            sl = slice(h * HEAD_DIM, (h + 1) * HEAD_DIM)
            qm = qm_ref[:, sl].astype(BF16)
            qn = qn_ref[:, sl].astype(BF16)
            kc = kc_ref[:, sl].astype(BF16)
            kp = kp_ref[:, sl].astype(BF16)
            vc = vc_ref[:, sl].astype(BF16)
            vp = vp_ref[:, sl].astype(BF16)
            dom = dom_ref[:, sl].astype(BF16)
            don = don_ref[:, sl].astype(BF16)
            lm, lnx = lm_ref[:, h:h + 1], ln_ref[:, h:h + 1]
            dmm, dmn = dmm_ref[:, h:h + 1], dmn_ref[:, h:h + 1]
            pa = jnp.exp(jnp.where(mask_c, _dot_nt(qm, kc) * scale, NEG_INF) - lm)
            pb = jnp.exp(jnp.where(mask_p, _dot_nt(qm, kp) * scale, NEG_INF) - lm)
            pc = jnp.exp(jnp.where(mask_n, _dot_nt(qn, kc) * scale, NEG_INF) - lnx)
            dsa = (pa * (_dot_nt(dom, vc) - dmm)).astype(BF16)
            dsb = (pb * (_dot_nt(dom, vp) - dmm)).astype(BF16)
            dsc = (pc * (_dot_nt(don, vc) - dmn)).astype(BF16)
            dq_ref[:, sl] = (_dot(dsa, kc) + _dot(dsb, kp)) * scale
            dk = (_dot_tn(dsa, qm) + _dot_tn(dsc, qn)) * scale
            dv = _dot_tn(pa.astype(BF16), dom) + _dot_tn(pc.astype(BF16), don)
            if has_add:
                dk = dk + dka_ref[:, sl]
                dv = dv + dva_ref[:, sl]
            dk_ref[:, sl] = dk
            dv_ref[:, sl] = dv

    def spec(cb, w, which):
        if which == "cur":
            return pl.BlockSpec((None, BLK, w), lambda bi, t: (bi, t, cb))
        if which == "prev":
            return pl.BlockSpec((None, BLK, w), lambda bi, t: (bi, jnp.where(t % nblk == 0, t, t - 1), cb))
        return pl.BlockSpec((None, BLK, w), lambda bi, t: (bi, jnp.where(t % nblk == nblk - 1, t, t + 1), cb))

    in_specs = [
        spec(q_blk, hw, "cur"), spec(q_blk, hw, "next"), spec(k_blk, hw, "cur"), spec(k_blk, hw, "prev"),
        spec(v_blk, hw, "cur"), spec(v_blk, hw, "prev"), spec(0, hw, "cur"), spec(0, hw, "next"),
        spec(0, LANES, "cur"), spec(0, LANES, "next"), spec(0, LANES, "cur"), spec(0, LANES, "next"),
    ]
    args = [q, q, k, k, v, v, do, do, lsej, lsej, dm, dm]
    if has_add:
        in_specs += [spec(0, hw, "cur"), spec(0, hw, "cur")]
        args += [dk_add, dv_add]
    out = pl.BlockSpec((None, BLK, hw), lambda bi, t: (bi, t, 0))
    return pl.pallas_call(
        body, name=name, grid=(b, s // BLK), in_specs=in_specs, out_specs=[out, out, out],
        out_shape=[jax.ShapeDtypeStruct((b, s, hw), F32)] * 3,
        compiler_params=_cp("parallel", "parallel"),
    )(*args)


def sum_parts(g, recv, me, name, tm=256):
    _, rows, c = g.shape
    n = recv.shape[0]
    tm = _row_tile(rows, tm)

    def body(me_ref, g_ref, r_ref, o_ref):
        acc = g_ref[...].astype(F32)
        for j in range(n):
            acc = acc + r_ref[j].astype(F32)
        o_ref[...] = acc

    return pl.pallas_call(
        body, name=name,
        grid_spec=pltpu.PrefetchScalarGridSpec(
            num_scalar_prefetch=1, grid=(rows // tm,),
            in_specs=[pl.BlockSpec((None, tm, c), lambda i, me_ref: (me_ref[0], i, 0)),
                      pl.BlockSpec((n, tm, c), lambda i, me_ref: (0, i, 0))],
            out_specs=pl.BlockSpec((tm, c), lambda i, me_ref: (i, 0))),
        out_shape=jax.ShapeDtypeStruct((rows, c), F32), compiler_params=_cp("parallel"),
    )(me, g, recv)


def adamw(w, m, v, g_parts, name, tm=256):
    rows, c = w.shape
    tm = _row_tile(rows, tm)
    npart = len(g_parts)

    def body(*refs):
        w_ref, m_ref, v_ref = refs[:3]
        g_refs = refs[3:3 + npart]
        go_ref, d_ref, mo_ref, vo_ref = refs[3 + npart:]
        g = g_refs[0][...]
        for k in range(1, npart):
            g = g + g_refs[k][...]
        mn = ADAM_B1 * m_ref[...] + (1.0 - ADAM_B1) * g
        vn = ADAM_B2 * v_ref[...] + (1.0 - ADAM_B2) * (g * g)
        m_hat = mn / (1.0 - ADAM_B1 ** ADAM_STEP)
        v_hat = vn / (1.0 - ADAM_B2 ** ADAM_STEP)
        go_ref[...] = g
        d_ref[...] = -ADAM_LR * (m_hat / (jnp.sqrt(v_hat) + ADAM_EPS) + ADAM_WD * w_ref[...])
        mo_ref[...] = mn
        vo_ref[...] = vn

    row = pl.BlockSpec((tm, c), lambda i: (i, 0))
    return pl.pallas_call(
        body, name=name, grid=(rows // tm,), in_specs=[row] * (3 + npart), out_specs=[row] * 4,
        out_shape=[jax.ShapeDtypeStruct((rows, c), F32)] * 4, compiler_params=_cp("parallel"),
    )(w, m, v, *g_parts)


def _place():
    return lax.axis_index("x"), lax.axis_index("y"), lax.axis_index("c")


def _other_chips(x, y, c):
    return [(1 - x, y, c), (x, 1 - y, c), (1 - x, 1 - y, c)]


def _chip_of(px, py):
    return 2 * px + py


HBM_SPEC = pl.BlockSpec(memory_space=pltpu.HBM)
SEM_SPEC = pl.BlockSpec(memory_space=pltpu.SEMAPHORE)
ANY_SPEC = pl.BlockSpec(memory_space=pl.ANY)
DATAFLOW = pltpu.SideEffectType.DATAFLOW_SIDE_EFFECTING
N_PEER_CHIPS = N_CHIPS - 1


def _hbm(a):
    return pltpu.with_memory_space_constraint(a, pltpu.HBM)


def _hbm_like(arrays):
    return [pltpu.HBM(a.shape, a.dtype) for a in arrays]


def cast_place(w, layer, me, out_dtype, name, tm=256):
    rows, c = w.shape[-2:]
    tm = _row_tile(rows, tm)

    def body(me_ref, w_ref, o_ref):
        o_ref[...] = w_ref[...].astype(out_dtype)

    if layer is None:
        in_spec = pl.BlockSpec((tm, c), lambda i, me_ref: (i, 0))
    else:
        in_spec = pl.BlockSpec((None, tm, c), lambda i, me_ref: (layer, i, 0))
    return pl.pallas_call(
        body, name=name,
        grid_spec=pltpu.PrefetchScalarGridSpec(
            num_scalar_prefetch=1, grid=(rows // tm,), in_specs=[in_spec],
            out_specs=pl.BlockSpec((None, tm, c), lambda i, me_ref: (me_ref[0], i, 0))),
        out_shape=jax.ShapeDtypeStruct((N_CHIPS, rows, c), out_dtype), compiler_params=_cp("parallel"),
    )(me, w)


def gather_start(lands, chunk_sizes, name="gather_start"):
    n = len(lands)
    nch = len(chunk_sizes)
    assert sum(chunk_sizes) == n

    def body(*refs):
        land_refs = refs[:n]
        outs = refs[n:]
        send_sems, recv_sems = outs[:nch], outs[nch:2 * nch]
        token = outs[-1]
        x, y, c = _place()
        me = _chip_of(x, y)
        peers = _other_chips(x, y, c)
        k = 0
        for ck, size in enumerate(chunk_sizes):
            for pos in range(size):
                for r, peer in enumerate(peers):
                    pltpu.make_async_remote_copy(
                        src_ref=land_refs[k].at[me], dst_ref=land_refs[k].at[me],
                        send_sem=send_sems[ck].at[N_PEER_CHIPS * pos + r], recv_sem=recv_sems[ck].at[N_PEER_CHIPS * pos + r],
                        device_id=peer, device_id_type=MESH).start()
                k += 1
        token[...] = jnp.zeros(token.shape, F32)

    sems = [pltpu.SemaphoreType.DMA((N_PEER_CHIPS * s,)) for s in chunk_sizes]
    res = pl.pallas_call(
        body, name=name,
        out_shape=(*sems, *sems, *_hbm_like(lands), jax.ShapeDtypeStruct(DEP_SPEC_SHAPE, F32)),
        in_specs=[HBM_SPEC] * n,
        out_specs=(*[SEM_SPEC] * (2 * nch), *[HBM_SPEC] * n, pl.BlockSpec(memory_space=pltpu.VMEM)),
        input_output_aliases={k: 2 * nch + k for k in range(n)},
        compiler_params=pltpu.CompilerParams(has_side_effects=DATAFLOW),
    )(*[_hbm(a) for a in lands])
    return res[:nch], res[nch:2 * nch], res[2 * nch:2 * nch + n], res[-1]


def gather_wait(send_sem, recv_sem, lands, after, name):
    n = len(lands)

    def body(*refs):
        land_refs = refs[:n]
        ssem, rsem = refs[n], refs[n + 1]
        x, y, c = _place()
        me = _chip_of(x, y)
        for pos in range(n):
            for r, peer in enumerate(_other_chips(x, y, c)):
                cp = pltpu.make_async_remote_copy(
                    src_ref=land_refs[pos].at[me], dst_ref=land_refs[pos].at[_chip_of(peer[0], peer[1])],
                    send_sem=ssem.at[N_PEER_CHIPS * pos + r], recv_sem=rsem.at[N_PEER_CHIPS * pos + r],
                    device_id=peer, device_id_type=MESH)
                cp.wait_send()
                cp.wait_recv()

    return pl.pallas_call(
        body, name=name, out_shape=tuple(_hbm_like(lands)),
        in_specs=[*[HBM_SPEC] * n, SEM_SPEC, SEM_SPEC, ANY_SPEC], out_specs=[HBM_SPEC] * n,
        input_output_aliases={k: k for k in range(n)},
        compiler_params=pltpu.CompilerParams(has_side_effects=DATAFLOW),
    )(*lands, send_sem, recv_sem, after)


def scatter_start(grads, name):
    n = len(grads)
    recvs = [lax.empty((N_PEER_CHIPS, *g.shape[1:]), g.dtype) for g in grads]

    def body(*refs):
        g_refs, r_refs = refs[:n], refs[n:2 * n]
        send_sems, recv_sems = refs[2 * n], refs[2 * n + 1]
        token = refs[-1]
        x, y, c = _place()
        for k in range(n):
            for r, peer in enumerate(_other_chips(x, y, c)):
                pltpu.make_async_remote_copy(
                    src_ref=g_refs[k].at[_chip_of(peer[0], peer[1])], dst_ref=r_refs[k].at[r],
                    send_sem=send_sems.at[N_PEER_CHIPS * k + r], recv_sem=recv_sems.at[N_PEER_CHIPS * k + r],
                    device_id=peer, device_id_type=MESH).start()
        token[...] = jnp.zeros(token.shape, F32)

    sem = pltpu.SemaphoreType.DMA((N_PEER_CHIPS * n,))
    res = pl.pallas_call(
        body, name=name,
        out_shape=(sem, sem, *_hbm_like(grads), *_hbm_like(recvs), jax.ShapeDtypeStruct(DEP_SPEC_SHAPE, F32)),
        in_specs=[HBM_SPEC] * (2 * n),
        out_specs=(SEM_SPEC, SEM_SPEC, *[HBM_SPEC] * (2 * n), pl.BlockSpec(memory_space=pltpu.VMEM)),
        input_output_aliases={k: 2 + k for k in range(2 * n)},
        compiler_params=pltpu.CompilerParams(has_side_effects=DATAFLOW),
    )(*[_hbm(a) for a in grads], *[_hbm(a) for a in recvs])
    return res[0], res[1], res[2:2 + n], res[2 + n:2 + 2 * n], res[-1]


def scatter_wait(send_sem, recv_sem, grads, recvs, after, name):
    n = len(grads)

    def body(*refs):
        g_refs, r_refs = refs[:n], refs[n:2 * n]
        ssem, rsem = refs[2 * n], refs[2 * n + 1]
        x, y, c = _place()
        for k in range(n):
            for r, peer in enumerate(_other_chips(x, y, c)):
                cp = pltpu.make_async_remote_copy(
                    src_ref=g_refs[k].at[_chip_of(peer[0], peer[1])], dst_ref=r_refs[k].at[r],
                    send_sem=ssem.at[N_PEER_CHIPS * k + r], recv_sem=rsem.at[N_PEER_CHIPS * k + r],
                    device_id=peer, device_id_type=MESH)
                cp.wait_send()
                cp.wait_recv()

    res = pl.pallas_call(
        body, name=name, out_shape=(*_hbm_like(grads), *_hbm_like(recvs)),
        in_specs=[*[HBM_SPEC] * (2 * n), SEM_SPEC, SEM_SPEC, ANY_SPEC], out_specs=[HBM_SPEC] * (2 * n),
        input_output_aliases={k: k for k in range(2 * n)},
        compiler_params=pltpu.CompilerParams(has_side_effects=DATAFLOW),
    )(*grads, *recvs, send_sem, recv_sem, after)
    return res[:n], res[n:]


def swap_with_sibling(parts, name="swap_with_sibling"):
    n = len(parts)

    def body(*refs):
        ins, outs = refs[:n], refs[n:2 * n]
        send_sems, recv_sems = refs[2 * n:]
        x, y, c = _place()
        copies = []
        for k in range(n):
            cp = pltpu.make_async_remote_copy(
                src_ref=ins[k], dst_ref=outs[k], send_sem=send_sems.at[k], recv_sem=recv_sems.at[k],
                device_id=(x, y, 1 - c), device_id_type=MESH)
            cp.start()
            copies.append(cp)
        for cp in copies:
            cp.wait()

    any_spec = pl.BlockSpec(memory_space=pl.ANY)
    return pl.pallas_call(
        body, name=name, in_specs=[any_spec] * n, out_specs=[any_spec] * n,
        out_shape=[jax.ShapeDtypeStruct(p.shape, p.dtype) for p in parts],
        scratch_shapes=[pltpu.SemaphoreType.DMA((n,)), pltpu.SemaphoreType.DMA((n,))],
        compiler_params=pltpu.CompilerParams(has_side_effects=True),
    )(*parts)


def allreduce_small(vec, name="allreduce_small"):
    rows, lanes = vec.shape

    def body(v_ref, o_ref, buf_ref, send_sems, recv_sems):
        x, y, c = _place()
        me = 4 * x + 2 * y + c
        buf_ref[me] = v_ref[...]
        copies = []
        for k in range(1, N_DEV):
            px, py, pc = x ^ ((k >> 2) & 1), y ^ ((k >> 1) & 1), c ^ (k & 1)
            peer_slot = 4 * px + 2 * py + pc
            cp = pltpu.make_async_remote_copy(
                src_ref=buf_ref.at[me], dst_ref=buf_ref.at[me], send_sem=send_sems.at[k - 1],
                recv_sem=recv_sems.at[k - 1], device_id=(px, py, pc), device_id_type=MESH)
            cp.start()
            copies.append(pltpu.make_async_remote_copy(
                src_ref=buf_ref.at[me], dst_ref=buf_ref.at[peer_slot], send_sem=send_sems.at[k - 1],
                recv_sem=recv_sems.at[k - 1], device_id=(px, py, pc), device_id_type=MESH))
        for cp in copies:
            cp.wait()
        acc = buf_ref[0]
        for j in range(1, N_DEV):
            acc = acc + buf_ref[j]
        o_ref[...] = acc

    vm = pl.BlockSpec(memory_space=pltpu.VMEM)
    return pl.pallas_call(
        body, name=name, in_specs=[vm], out_specs=vm, out_shape=jax.ShapeDtypeStruct((rows, lanes), F32),
        scratch_shapes=[pltpu.VMEM((N_DEV, rows, lanes), F32), pltpu.SemaphoreType.DMA((N_DEV - 1,)),
                        pltpu.SemaphoreType.DMA((N_DEV - 1,))],
        compiler_params=pltpu.CompilerParams(has_side_effects=True, vmem_limit_bytes=VMEM_LIMIT_V7X),
    )(vec)


def _pack(arrays):
    flat = jnp.concatenate([a.reshape(-1).astype(F32) for a in arrays])
    n = flat.shape[0]
    rows = -(-n // LANES)
    rows = -(-rows // 8) * 8
    return jnp.pad(flat, (0, rows * LANES - n)).reshape(rows, LANES)


def _unpack(packed, shapes, lead=()):
    flat = packed.reshape(*lead, -1)
    out, off = [], 0
    for shp in shapes:
        n = math.prod(shp)
        out.append(flat[..., off:off + n].reshape(*lead, *shp))
        off += n
    return out


def _row(vec):
    return vec.reshape(1, -1)


def kernel(x, mix_pre_g, mix_post_g, ffn_pre_g, ffn_post_g, cm_w_in, cm_b_in, cm_dw, cm_dw_b, cm_ln_g, cm_ln_b, cm_w_out, cm_b_out, kv_norm_g, w_kv, w_q, w_o, ffn_w_in, ffn_dw, ffn_dw_b, ffn_w_out, loss_target, m_mix_pre_g, m_mix_post_g, m_ffn_pre_g, m_ffn_post_g, m_cm_w_in, m_cm_b_in, m_cm_dw, m_cm_dw_b, m_cm_ln_g, m_cm_ln_b, m_cm_w_out, m_cm_b_out, m_kv_norm_g, m_w_kv, m_w_q, m_w_o, m_ffn_w_in, m_ffn_dw, m_ffn_dw_b, m_ffn_w_out, v_mix_pre_g, v_mix_post_g, v_ffn_pre_g, v_ffn_post_g, v_cm_w_in, v_cm_b_in, v_cm_dw, v_cm_dw_b, v_cm_ln_g, v_cm_ln_b, v_cm_w_out, v_cm_b_out, v_kv_norm_g, v_w_kv, v_w_q, v_w_o, v_ffn_w_in, v_ffn_dw, v_ffn_dw_b, v_ffn_w_out):
    names = ["mix_pre_g", "mix_post_g", "ffn_pre_g", "ffn_post_g", "cm_w_in", "cm_b_in", "cm_dw", "cm_dw_b", "cm_ln_g",
             "cm_ln_b", "cm_w_out", "cm_b_out", "kv_norm_g", "w_kv", "w_q", "w_o", "ffn_w_in", "ffn_dw", "ffn_dw_b",
             "ffn_w_out"]
    w_in = dict(zip(names, [mix_pre_g, mix_post_g, ffn_pre_g, ffn_post_g, cm_w_in, cm_b_in, cm_dw, cm_dw_b, cm_ln_g,
                            cm_ln_b, cm_w_out, cm_b_out, kv_norm_g, w_kv, w_q, w_o, ffn_w_in, ffn_dw, ffn_dw_b, ffn_w_out]))
    m_in = dict(zip(names, [m_mix_pre_g, m_mix_post_g, m_ffn_pre_g, m_ffn_post_g, m_cm_w_in, m_cm_b_in, m_cm_dw, m_cm_dw_b,
                            m_cm_ln_g, m_cm_ln_b, m_cm_w_out, m_cm_b_out, m_kv_norm_g, m_w_kv, m_w_q, m_w_o, m_ffn_w_in,
                            m_ffn_dw, m_ffn_dw_b, m_ffn_w_out]))
    v_in = dict(zip(names, [v_mix_pre_g, v_mix_post_g, v_ffn_pre_g, v_ffn_post_g, v_cm_w_in, v_cm_b_in, v_cm_dw, v_cm_dw_b,
                            v_cm_ln_g, v_cm_ln_b, v_cm_w_out, v_cm_b_out, v_kv_norm_g, v_w_kv, v_w_q, v_w_o, v_ffn_w_in,
                            v_ffn_dw, v_ffn_dw_b, v_ffn_w_out]))

    bsz, seq, d = x.shape
    t = bsz * seq
    n_b = DEPTH - N_A
    hw = w_o.shape[-1]
    qw = N_GROUPS * hw
    f2 = ffn_dw_b.shape[-1]
    f = f2 // 2
    me_chip = _chip_of(lax.axis_index("x"), lax.axis_index("y"))

    big = ["cm_w_in", "cm_w_out", "w_kv", "w_q", "w_o", "ffn_w_in", "ffn_w_out"]
    row_sharded = ("cm_w_out", "w_o", "ffn_w_out")
    small_sharded = ["cm_b_in", "cm_dw", "cm_dw_b", "cm_ln_g", "cm_ln_b", "cm_b_out", "ffn_dw"]
    small_pack = _pack([w_in[n] for n in small_sharded])
    chunks = [
        [("cm_w_in", 0), ("cm_w_out", 0), ("small", None)],
        [("ffn_w_in", 0), ("ffn_w_out", 0), ("cm_w_in", 1), ("cm_w_out", 1)],
        [("ffn_w_in", 1), ("ffn_w_out", 1), ("w_kv", None)],
        [("w_q", 0), ("w_o", 0), ("ffn_w_in", 2), ("ffn_w_out", 2)],
        [("w_q", 1), ("w_o", 1), ("ffn_w_in", 3), ("ffn_w_out", 3)],
    ]
    pieces = [pc for ch in chunks for pc in ch]

    me_arr = me_chip.astype(jnp.int32).reshape(1)

    def land_of(pc):
        n, l = pc
        if n == "small":
            return cast_place(small_pack, None, me_arr, F32, name="place_small")
        return cast_place(w_in[n], l, me_arr, BF16, name=f"place_{n}_{l}")

    lands = [land_of(pc) for pc in pieces]
    g_send, g_recv, lands_f, token = gather_start(lands, [len(ch) for ch in chunks])
    weights = {}

    def finish_chunk(ck, after):
        lo = sum(len(ch) for ch in chunks[:ck])
        hi = lo + len(chunks[ck])
        got = gather_wait(g_send[ck], g_recv[ck], lands_f[lo:hi], after, name=f"gather_wait{ck}")
        for pc, arr in zip(chunks[ck], got):
            weights[pc] = arr.reshape(1, -1, arr.shape[-1]) if pc[0] in row_sharded else arr

    def wmat(n, l=None):
        arr = weights[(n, l)]
        return arr, arr.shape[0]

    finish_chunk(0, token)
    small_full = {}
    for n, arr4 in zip(small_sharded, _unpack(weights[("small", None)], [w_in[n].shape for n in small_sharded], lead=(N_CHIPS,))):
        shp = w_in[n].shape
        small_full[n] = jnp.moveaxis(arr4, 0, -2).reshape(*shp[:-1], N_CHIPS * shp[-1])

    x2d = x.reshape(t, d)
    saved = []
    (h1,) = resid_norm_fwd(x2d, None, None, [_row(mix_pre_g[0])], name="norm_in", dep=token)
    xcur = x2d
    kv_state = None
    for i in range(DEPTH):
        sv = {"x_in": xcur, "h1": h1}
        if i < N_A:
            z = mm_nn(h1, *wmat("cm_w_in", i), 1, 0, bias=_row(small_full["cm_b_in"][i]), name=f"cm_in{i}")
            u2 = glu_conv_fwd(z.reshape(bsz, seq, 2 * d), small_full["cm_dw"][i], _row(small_full["cm_dw_b"][i]),
                              name=f"glu_conv{i}").reshape(t, d)
            u4 = ln_silu_fwd(u2, _row(small_full["cm_ln_g"][i]), _row(small_full["cm_ln_b"][i]), name=f"ln_silu{i}")
            y = mm_nn(u4, *wmat("cm_w_out", i), 1, 0, bias=_row(small_full["cm_b_out"][i]), name=f"cm_out{i}")
            sv.update(z=z, u2=u2, u4=u4)
        else:
            j = i - N_A
            finish_chunk(3 + j, h1)
            q = mm_nn(h1, *wmat("w_q", j), 1, 0, name=f"q_proj{j}").reshape(bsz, seq, qw)
            outs, lses, q_subs = [], [], []
            for g, dil in enumerate(DILATIONS):
                nblk = seq // dil // BLK
                if dil == 1:
                    o_s, l_s = attn_fwd(q, g, kv_state["kv"], g, kv_state["kv"], N_GROUPS + g, nblk, hw, name=f"attn_fwd{j}_{g}")
                    q_subs.append(None)
                    outs.append(o_s.reshape(t, hw))
                    lses.append(l_s.reshape(t, LANES))
                else:
                    q_s = perm_rows(q, g, hw, dil, False, BF16, name=f"q_sub{j}_{g}")
                    o_s, l_s = attn_fwd(q_s, 0, kv_state["k_sub"][g], 0, kv_state["v_sub"][g], 0, nblk, hw, name=f"attn_fwd{j}_{g}")
                    q_subs.append(q_s)
                    outs.append(perm_rows(o_s, 0, hw, dil, True, F32, name=f"o_tok{j}_{g}").reshape(t, hw))
                    lses.append(perm_rows(l_s, 0, LANES, dil, True, F32, name=f"lse_tok{j}_{g}").reshape(t, LANES))
            merged, lsej = attn_merge(outs, lses, name=f"attn_merge{j}")
            y = mm_nn(merged, *wmat("w_o", j), 1, 0, name=f"o_proj{j}")
            sv.update(q=q, q_subs=q_subs, merged=merged, lsej=lsej)
        x1, h2 = resid_norm_fwd(xcur, y, _row(mix_post_g[i]), [_row(ffn_pre_g[i])], name=f"resid_mix{i}")
        if i < N_A:
            finish_chunk(1 + i, h2)
        p = mm_nn(h2, *wmat("ffn_w_in", i), 1, 0, name=f"ffn_in{i}")
        s_act = ffn_mid_fwd(p.reshape(bsz, seq, f2), small_full["ffn_dw"][i], _row(ffn_dw_b[i]), name=f"ffn_mid{i}").reshape(t, f)
        y2 = mm_nn(s_act, *wmat("ffn_w_out", i), 1, 0, name=f"ffn_out{i}")
        next_gains = []
        if i + 1 < DEPTH:
            next_gains.append(_row(mix_pre_g[i + 1]))
        if i == N_A - 1:
            next_gains.append(_row(kv_norm_g))
        res = resid_norm_fwd(x1, y2, _row(ffn_post_g[i]), next_gains, name=f"resid_ffn{i}")
        sv.update(y=y, x1=x1, h2=h2, p=p, s=s_act, y2=y2)
        saved.append(sv)
        xcur = res[0]
        if i + 1 < DEPTH:
            h1 = res[1]
        if i == N_A - 1:
            kvn = res[2]
            kv = mm_nn(kvn, *wmat("w_kv"), 1, 0, name="kv_proj").reshape(bsz, seq, 2 * qw)
            k_sub, v_sub = {}, {}
            for g, dil in enumerate(DILATIONS):
                if dil > 1:
                    k_sub[g] = perm_rows(kv, g, hw, dil, False, BF16, name=f"k_sub{g}")
                    v_sub[g] = perm_rows(kv, N_GROUPS + g, hw, dil, False, BF16, name=f"v_sub{g}")
            kv_state = {"kv": kv, "k_sub": k_sub, "v_sub": v_sub, "kvn": kvn, "x_a": xcur}

    dx, loss_tile = loss_fwd_bwd(xcur, loss_target.reshape(t, d))
    loss = lax.psum(loss_tile[0, 0], ("x", "y", "c"))

    gsm = {n: [None] * w_in[n].shape[0] for n in
           ["mix_pre_g", "mix_post_g", "ffn_pre_g", "ffn_post_g", "cm_b_in", "cm_dw", "cm_dw_b", "cm_ln_g", "cm_ln_b",
            "cm_b_out", "ffn_dw", "ffn_dw_b"]}
    gbig = {}
    in_flight = []
    dep = None

    def start_scatter(pcs, tag):
        ssem, rsem, g_f, r_f, tok = scatter_start([gbig[pc] for pc in pcs], name=f"scatter_start_{tag}")
        in_flight.append((pcs, ssem, rsem, g_f, r_f))
        return tok

    dk_acc = {g: None for g in range(N_GROUPS)}
    dv_acc = {g: None for g in range(N_GROUPS)}
    for i in range(DEPTH - 1, -1, -1):
        sv = saved[i]
        dy2, dg, _ = norm_bwd(sv["y2"], _row(ffn_post_g[i]), dx, out_dtype=BF16, name=f"bwd_ffn_post{i}", dep=dep)
        gsm["ffn_post_g"][i] = dg
        ds = mm_nt(dy2, *wmat("ffn_w_out", i), 1, 0, name=f"bwd_ffn_out_dx{i}")
        gbig[("ffn_w_out", i)] = mm_tn(sv["s"], dy2, 1, name=f"bwd_ffn_out_dw{i}").reshape(N_CHIPS, f // N_CHIPS, d)
        dpa, dpg, ddwa, ddwg, ddba, ddbg = ffn_mid_bwd(sv["p"].reshape(bsz, seq, f2), small_full["ffn_dw"][i], _row(ffn_dw_b[i]),
                                                       ds.reshape(bsz, seq, f), name=f"bwd_ffn_mid{i}")
        gsm["ffn_dw"][i] = jnp.concatenate([jnp.sum(ddwa, axis=1), jnp.sum(ddwg, axis=1)], axis=-1)
        gsm["ffn_dw_b"][i] = jnp.concatenate([ddba, ddbg], axis=-1)
        dp = [dpa.reshape(t, f), dpg.reshape(t, f)]
        dh2 = mm_nt(dp, *wmat("ffn_w_in", i), 1, 0, name=f"bwd_ffn_in_dx{i}")
        gbig[("ffn_w_in", i)] = mm_tn(sv["h2"], dp, N_CHIPS, name=f"bwd_ffn_in_dw{i}")
        dx1, dg, _ = norm_bwd(sv["x1"], _row(ffn_pre_g[i]), dh2, add=dx, name=f"bwd_ffn_pre{i}")
        gsm["ffn_pre_g"][i] = dg
        dep = start_scatter([("ffn_w_in", 0), ("ffn_w_out", 0)], "ffn0") if i == 0 else None
        dy, dg, dbias = norm_bwd(sv["y"], _row(mix_post_g[i]), dx1, out_dtype=BF16, name=f"bwd_mix_post{i}", dep=dep)
        gsm["mix_post_g"][i] = dg
        if i < N_A:
            gsm["cm_b_out"][i] = dbias
            du4 = mm_nt(dy, *wmat("cm_w_out", i), 1, 0, name=f"bwd_cm_out_dx{i}")
            gbig[("cm_w_out", i)] = mm_tn(sv["u4"], dy, 1, name=f"bwd_cm_out_dw{i}").reshape(N_CHIPS, d // N_CHIPS, d)
            du2, dlg, dlb = ln_silu_bwd(sv["u2"], _row(small_full["cm_ln_g"][i]), _row(small_full["cm_ln_b"][i]), du4,
                                        name=f"bwd_ln_silu{i}")
            gsm["cm_ln_g"][i], gsm["cm_ln_b"][i] = dlg, dlb
            dza, dzg, ddw, ddwb, dba, dbg = glu_conv_bwd(sv["z"].reshape(bsz, seq, 2 * d), small_full["cm_dw"][i],
                                                         du2.reshape(bsz, seq, d), name=f"bwd_glu_conv{i}")
            gsm["cm_dw"][i] = jnp.sum(ddw, axis=1)
            gsm["cm_dw_b"][i] = ddwb
            gsm["cm_b_in"][i] = jnp.concatenate([dba, dbg], axis=-1)
            dz = [dza.reshape(t, d), dzg.reshape(t, d)]
            dh1 = mm_nt(dz, *wmat("cm_w_in", i), 1, 0, name=f"bwd_cm_in_dx{i}")
            gbig[("cm_w_in", i)] = mm_tn(sv["h1"], dz, N_CHIPS, name=f"bwd_cm_in_dw{i}")
        else:
            j = i - N_A
            dmerged = mm_nt(dy, *wmat("w_o", j), 1, 0, name=f"bwd_o_proj_dx{j}")
            gbig[("w_o", j)] = mm_tn(sv["merged"], dy, 1, name=f"bwd_o_proj_dw{j}").reshape(N_CHIPS, hw // N_CHIPS, d)
            dmt = attn_bwd_prep(dmerged, sv["merged"], name=f"bwd_attn_prep{j}")
            dq_parts = []
            for g, dil in enumerate(DILATIONS):
                nblk = seq // dil // BLK
                do3 = dmerged.reshape(bsz, seq, hw)
                lj3 = sv["lsej"].reshape(bsz, seq, LANES)
                dm3 = dmt.reshape(bsz, seq, LANES)
                if dil == 1:
                    dq_g, dk_g, dv_g = attn_bwd(sv["q"], g, kv_state["kv"], g, kv_state["kv"], N_GROUPS + g, do3, lj3, dm3,
                                                dk_acc[g], dv_acc[g], nblk, hw, name=f"attn_bwd{j}_{g}")
                    dq_parts.append(dq_g.reshape(t, hw))
                else:
                    do_s = perm_rows(do3, 0, hw, dil, False, BF16, name=f"do_sub{j}_{g}")
                    lj_s = perm_rows(lj3, 0, LANES, dil, False, F32, name=f"lsej_sub{j}_{g}", tc=LANES)
                    dm_s = perm_rows(dm3, 0, LANES, dil, False, F32, name=f"dm_sub{j}_{g}", tc=LANES)
                    dq_g, dk_g, dv_g = attn_bwd(sv["q_subs"][g], 0, kv_state["k_sub"][g], 0, kv_state["v_sub"][g], 0, do_s,
                                                lj_s, dm_s, dk_acc[g], dv_acc[g], nblk, hw, name=f"attn_bwd{j}_{g}")
                    dq_parts.append(perm_rows(dq_g, 0, hw, dil, True, F32, name=f"dq_tok{j}_{g}").reshape(t, hw))
                dk_acc[g], dv_acc[g] = dk_g, dv_g
            dq = jnp.concatenate(dq_parts, axis=-1).astype(BF16)
            dh1 = mm_nt(dq, *wmat("w_q", j), 1, 0, name=f"bwd_q_proj_dx{j}")
            gbig[("w_q", j)] = mm_tn(sv["h1"], dq, N_CHIPS, name=f"bwd_q_proj_dw{j}")
        dx, dg, _ = norm_bwd(sv["x_in"], _row(mix_pre_g[i]), dh1, add=dx1, name=f"bwd_mix_pre{i}")
        gsm["mix_pre_g"][i] = dg
        if i > N_A:
            dep = start_scatter([("ffn_w_in", i), ("ffn_w_out", i), ("w_q", i - N_A), ("w_o", i - N_A)], f"l{i}")
        elif 0 < i < N_A:
            dep = start_scatter([("ffn_w_in", i), ("ffn_w_out", i), ("cm_w_in", i), ("cm_w_out", i)], f"l{i}")
        elif i == 0:
            start_scatter([("cm_w_in", 0), ("cm_w_out", 0)], "cm0")
        if i == N_A:
            dkv_parts = []
            for acc in (dk_acc, dv_acc):
                for g, dil in enumerate(DILATIONS):
                    part = acc[g]
                    if dil > 1:
                        part = perm_rows(part, 0, hw, dil, True, F32, name=f"dkv_tok{len(dkv_parts)}")
                    dkv_parts.append(part.reshape(t, hw))
            dkv = jnp.concatenate(dkv_parts, axis=-1).astype(BF16)
            dkvn = mm_nt(dkv, *wmat("w_kv"), 1, 0, name="bwd_kv_proj_dx")
            gbig[("w_kv", None)] = mm_tn(kv_state["kvn"], dkv, N_CHIPS, name="bwd_kv_proj_dw")
            dx, dg_kv, _ = norm_bwd(kv_state["x_a"], _row(kv_norm_g), dkvn, add=dx, name="bwd_kv_norm")
            dep = start_scatter([("ffn_w_in", i), ("ffn_w_out", i), ("w_q", 0), ("w_o", 0), ("w_kv", None)], f"l{i}")
    grad_x = dx.reshape(bsz, seq, d)

    plane_of = {}
    for k, (pcs, ssem, rsem, g_f, r_f) in enumerate(in_flight):
        g_done, r_done = scatter_wait(ssem, rsem, g_f, r_f, dx, name=f"scatter_wait{k}")
        for pc, g_arr, r_arr in zip(pcs, g_done, r_done):
            plane_of[pc] = sum_parts(g_arr, r_arr, me_arr, name=f"sum_chips_{pc[0]}_{pc[1]}")
    plane = []
    for n in big:
        if w_in[n].ndim == 2:
            plane.append(plane_of[(n, None)])
        else:
            plane.append(jnp.concatenate([plane_of[(n, l)] for l in range(w_in[n].shape[0])], axis=0))
    other = swap_with_sibling(plane)

    outs_g, outs_d, outs_m, outs_v = {}, {}, {}, {}
    for n, p_mine, p_other in zip(big, plane, other):
        shp = w_in[n].shape
        flat = lambda a: a.reshape(-1, shp[-1])
        g_, d_, m_, v_ = adamw(flat(w_in[n]), flat(m_in[n]), flat(v_in[n]), [p_mine, p_other], name=f"adamw_{n}")
        outs_g[n], outs_d[n], outs_m[n], outs_v[n] = (a.reshape(shp) for a in (g_, d_, m_, v_))

    small_names = [n for n in names if n not in big]
    small_shapes_full = {}
    small_grads_full = []
    for n in small_names:
        if n == "kv_norm_g":
            gfull = dg_kv.reshape(-1)
        elif n in ("cm_dw", "ffn_dw"):
            gfull = jnp.stack(gsm[n], axis=0)
        else:
            gfull = jnp.stack([a.reshape(-1) for a in gsm[n]], axis=0)
        small_shapes_full[n] = gfull.shape
        small_grads_full.append(gfull)
    summed = allreduce_small(_pack(small_grads_full))
    g_full = dict(zip(small_names, _unpack(summed, [small_shapes_full[n] for n in small_names])))
    g_loc = {}
    for n in small_names:
        if n in small_sharded:
            width = w_in[n].shape[-1]
            g_loc[n] = lax.dynamic_slice_in_dim(g_full[n], me_chip * width, width, axis=g_full[n].ndim - 1)
        else:
            g_loc[n] = g_full[n]
    res = adamw(_pack([w_in[n] for n in small_names]), _pack([m_in[n] for n in small_names]),
                _pack([v_in[n] for n in small_names]), [_pack([g_loc[n] for n in small_names])], name="adamw_small")
    shapes_loc = [w_in[n].shape for n in small_names]
    for dst, packed in zip((outs_g, outs_d, outs_m, outs_v), res):
        for n, a in zip(small_names, _unpack(packed, shapes_loc)):
            dst[n] = a

    return (loss, grad_x, *[outs_g[n] for n in names], *[outs_d[n] for n in names],
            *[outs_m[n] for n in names], *[outs_v[n] for n in names])
```

```python
import functools
import math

import jax
import jax.numpy as jnp
from jax import lax
from jax.experimental import pallas as pl
from jax.experimental.pallas import tpu as pltpu

F32 = jnp.float32
BF16 = jnp.bfloat16
EPS = 1e-6
NEG_INF = -1e30
N_A = 2
DEPTH = 4
N_GROUPS = 3
DILATIONS = (1, 4, 16)
HEAD_DIM = 128
BLK = 128
LANES = 128
N_CHIPS = 4
N_DEV = 8
VMEM_LIMIT_V7X = 56 * 1024 * 1024

ADAM_LR = 0.001
ADAM_B1 = 0.9
ADAM_B2 = 0.999
ADAM_EPS = 1e-08
ADAM_WD = 0.01
ADAM_STEP = 10

MESH = pl.DeviceIdType.MESH


def _cp(*sem):
    return pltpu.CompilerParams(dimension_semantics=sem if sem else None, vmem_limit_bytes=VMEM_LIMIT_V7X)


def _dot(a, b):
    return jnp.dot(a, b, preferred_element_type=F32)


def _dot_nt(a, b):
    return lax.dot_general(a, b, (((1,), (1,)), ((), ())), preferred_element_type=F32)


def _dot_tn(a, b):
    return lax.dot_general(a, b, (((0,), (0,)), ((), ())), preferred_element_type=F32)


def _sigmoid(x):
    return 1.0 / (1.0 + jnp.exp(-x))


def _row_tile(n, want):
    if n <= want:
        return n
    for t in range(want - want % 8, 7, -8):
        if n % t == 0:
            return t
    raise ValueError(f"no row tile for {n} rows")


def mm_nn(a, w, nsh, stride, layer, bias=None, out_dtype=F32, name="mm_nn", tm=512):
    m, k = a.shape
    _, k2, ns = w.shape
    assert k == k2
    tm = _row_tile(m, tm)
    has_bias = bias is not None

    def body(*refs):
        if has_bias:
            a_ref, w_ref, b_ref, o_ref = refs
        else:
            a_ref, w_ref, o_ref = refs
        acc = _dot(a_ref[...].astype(BF16), w_ref[...])
        if has_bias:
            acc = acc + b_ref[...]
        o_ref[...] = acc.astype(out_dtype)

    in_specs = [
        pl.BlockSpec((tm, k), lambda j, i: (i, 0)),
        pl.BlockSpec((None, k, ns), lambda j, i: (j * stride + layer, 0, 0)),
    ]
    args = [a, w]
    if has_bias:
        in_specs.append(pl.BlockSpec((1, ns), lambda j, i: (0, j)))
        args.append(bias)
    return pl.pallas_call(
        body,
        name=name,
        grid=(nsh, m // tm),
        in_specs=in_specs,
        out_specs=pl.BlockSpec((tm, ns), lambda j, i: (i, j)),
        out_shape=jax.ShapeDtypeStruct((m, nsh * ns), out_dtype),
        compiler_params=_cp("parallel", "parallel"),
    )(*args)


def mm_nt(dy, w, nsh, stride, layer, out_dtype=F32, name="mm_nt", tm=512):
    dys = list(dy) if isinstance(dy, (list, tuple)) else [dy]
    npart = len(dys)
    per = nsh // npart
    m = dys[0].shape[0]
    _, k, ns = w.shape
    assert all(d.shape == (m, per * ns) for d in dys)
    tm = _row_tile(m, tm)

    def body(*refs):
        dy_refs = refs[:npart]
        w_ref, o_ref, acc_ref = refs[npart:]
        j = pl.program_id(1)

        @pl.when(j == 0)
        def _():
            acc_ref[...] = jnp.zeros(acc_ref.shape, F32)

        for pi in range(npart):
            @pl.when(j // per == pi)
            def _(pi=pi):
                acc_ref[...] += _dot_nt(dy_refs[pi][...].astype(BF16), w_ref[...])

        @pl.when(j == nsh - 1)
        def _():
            o_ref[...] = acc_ref[...].astype(out_dtype)

    dy_specs = [pl.BlockSpec((tm, ns), lambda i, j, pi=pi: (i, jnp.clip(j - pi * per, 0, per - 1))) for pi in range(npart)]
    return pl.pallas_call(
        body,
        name=name,
        grid=(m // tm, nsh),
        in_specs=[*dy_specs, pl.BlockSpec((None, k, ns), lambda i, j: (j * stride + layer, 0, 0))],
        out_specs=pl.BlockSpec((tm, k), lambda i, j: (i, 0)),
        out_shape=jax.ShapeDtypeStruct((m, k), out_dtype),
        scratch_shapes=[pltpu.VMEM((tm, k), F32)],
        compiler_params=_cp("parallel", "arbitrary"),
    )(*dys, w)


def mm_tn(a, dy, nsh, name="mm_tn", tm=512):
    dys = list(dy) if isinstance(dy, (list, tuple)) else [dy]
    npart = len(dys)
    per = nsh // npart
    m, k = a.shape
    ns = dys[0].shape[1] // per
    assert all(d.shape == (m, per * ns) for d in dys)
    tm = _row_tile(m, tm)
    nt = m // tm

    def body(*refs):
        a_ref = refs[0]
        dy_refs = refs[1:1 + npart]
        o_ref, acc_ref = refs[1 + npart:]
        j = pl.program_id(0)
        i = pl.program_id(1)

        @pl.when(i == 0)
        def _():
            acc_ref[...] = jnp.zeros(acc_ref.shape, F32)

        for pi in range(npart):
            @pl.when(j // per == pi)
            def _(pi=pi):
                acc_ref[...] += _dot_tn(a_ref[...].astype(BF16), dy_refs[pi][...].astype(BF16))

        @pl.when(i == nt - 1)
        def _():
            o_ref[...] = acc_ref[...].astype(BF16)

    dy_specs = [
        pl.BlockSpec((tm, ns), lambda j, i, pi=pi: (jnp.where(j // per == pi, i, 0), jnp.clip(j - pi * per, 0, per - 1)))
        for pi in range(npart)
    ]
    return pl.pallas_call(
        body,
        name=name,
        grid=(nsh, nt),
        in_specs=[pl.BlockSpec((tm, k), lambda j, i: (i, 0)), *dy_specs],
        out_specs=pl.BlockSpec((None, k, ns), lambda j, i: (j, 0, 0)),
        out_shape=jax.ShapeDtypeStruct((nsh, k, ns), BF16),
        scratch_shapes=[pltpu.VMEM((k, ns), F32)],
        compiler_params=_cp("parallel", "arbitrary"),
    )(a, *dys)


DEP_SPEC_SHAPE = (8, LANES)


def resid_norm_fwd(x, y, g_post, next_gains, name, tm=256, dep=None):
    t, d = x.shape
    tm = _row_tile(t, tm)
    has_y = y is not None
    n_next = len(next_gains)
    n_dep = 0 if dep is None else 1

    def body(*refs):
        x_ref = refs[0]
        pos = 1
        if has_y:
            y_ref, gp_ref = refs[1], refs[2]
            pos = 3
        gn_refs = refs[pos:pos + n_next]
        outs = refs[pos + n_next + n_dep:]
        xv = x_ref[...]
        o = 0
        if has_y:
            yv = y_ref[...]
            r = lax.rsqrt(jnp.mean(yv * yv, axis=-1, keepdims=True) + EPS)
            xv = xv + (yv * r) * gp_ref[...]
            outs[0][...] = xv
            o = 1
        if n_next:
            xn = xv * lax.rsqrt(jnp.mean(xv * xv, axis=-1, keepdims=True) + EPS)
            for k in range(n_next):
                outs[o + k][...] = (xn * gn_refs[k][...]).astype(BF16)

    row = pl.BlockSpec((tm, d), lambda i: (i, 0))
    vec = pl.BlockSpec((1, d), lambda i: (0, 0))
    args, in_specs = [x], [row]
    if has_y:
        args += [y, g_post]
        in_specs += [row, vec]
    args += list(next_gains)
    in_specs += [vec] * n_next
    if n_dep:
        args.append(dep)
        in_specs.append(pl.BlockSpec(DEP_SPEC_SHAPE, lambda i: (0, 0)))
    out_shape, out_specs = [], []
    if has_y:
        out_shape.append(jax.ShapeDtypeStruct((t, d), F32))
        out_specs.append(row)
    for _ in range(n_next):
        out_shape.append(jax.ShapeDtypeStruct((t, d), BF16))
        out_specs.append(row)
    return pl.pallas_call(
        body, name=name, grid=(t // tm,), in_specs=in_specs, out_specs=out_specs, out_shape=out_shape,
        compiler_params=_cp("parallel"),
    )(*args)


def norm_bwd(x, g, dy, add=None, out_dtype=F32, name="norm_bwd", tm=256, dep=None):
    t, d = x.shape
    tm = _row_tile(t, tm)
    has_add = add is not None

    def body(*refs):
        x_ref, g_ref, dy_ref = refs[:3]
        add_ref = refs[3] if has_add else None
        dx_ref, dg_ref, cs_ref = refs[-3:]
        i = pl.program_id(0)
        xv = x_ref[...]
        dyv = dy_ref[...].astype(F32)
        r = lax.rsqrt(jnp.mean(xv * xv, axis=-1, keepdims=True) + EPS)
        gd = dyv * g_ref[...]
        dx = r * gd - xv * ((r * r * r) * jnp.mean(xv * gd, axis=-1, keepdims=True))
        if has_add:
            dx = dx + add_ref[...]
        dx_ref[...] = dx.astype(out_dtype)
        dg = jnp.sum(dyv * (xv * r), axis=0, keepdims=True)
        cs = jnp.sum(dx, axis=0, keepdims=True)

        @pl.when(i == 0)
        def _():
            dg_ref[...] = dg
            cs_ref[...] = cs

        @pl.when(i > 0)
        def _():
            dg_ref[...] += dg
            cs_ref[...] += cs

    row = pl.BlockSpec((tm, d), lambda i: (i, 0))
    vec = pl.BlockSpec((1, d), lambda i: (0, 0))
    args, in_specs = [x, g, dy], [row, vec, row]
    if has_add:
        args.append(add)
        in_specs.append(row)
    if dep is not None:
        args.append(dep)
        in_specs.append(pl.BlockSpec(DEP_SPEC_SHAPE, lambda i: (0, 0)))
    return pl.pallas_call(
        body, name=name, grid=(t // tm,), in_specs=in_specs,
        out_specs=[row, vec, vec],
        out_shape=[jax.ShapeDtypeStruct((t, d), out_dtype), jax.ShapeDtypeStruct((1, d), F32),
                   jax.ShapeDtypeStruct((1, d), F32)],
        compiler_params=_cp("arbitrary"),
    )(*args)


def loss_fwd_bwd(x, target, name="loss", tm=256):
    t, d = x.shape
    tm = _row_tile(t, tm)

    def body(x_ref, t_ref, dx_ref, l_ref):
        i = pl.program_id(0)
        err = x_ref[...] - t_ref[...]
        dx_ref[...] = err * (1.0 / d)
        part = 0.5 * jnp.sum(jnp.mean(err * err, axis=-1, keepdims=True), axis=0, keepdims=True)
        part = jnp.broadcast_to(part, l_ref.shape)

        @pl.when(i == 0)
        def _():
            l_ref[...] = part

        @pl.when(i > 0)
        def _():
            l_ref[...] += part

    row = pl.BlockSpec((tm, d), lambda i: (i, 0))
    return pl.pallas_call(
        body, name=name, grid=(t // tm,), in_specs=[row, row],
        out_specs=[row, pl.BlockSpec((8, LANES), lambda i: (0, 0))],
        out_shape=[jax.ShapeDtypeStruct((t, d), F32), jax.ShapeDtypeStruct((8, LANES), F32)],
        compiler_params=_cp("arbitrary"),
    )(x, target)


CONV_HALO = 32
CONV_CHUNK = 128


def glu_conv_fwd(z, dw, dwb, name, tc=128):
    b, s, c2 = z.shape
    c = c2 // 2
    kw = dw.shape[0]
    tc = min(tc, c)
    nc = c // tc
    ch = min(CONV_CHUNK, s)
    halo = CONV_HALO
    assert kw - 1 <= halo and s % ch == 0

    nch = s // ch

    def body(a_ref, g_ref, w_ref, b_ref, o_ref, pad_ref):
        _fill_glu_slabs(a_ref, g_ref, pad_ref, nch, ch, halo)

        def chunk(ci, carry):
            r0 = pl.multiple_of(ci * ch, ch)
            acc = b_ref[...]
            for k, tap in enumerate(_taps_front(pad_ref, ci, kw, ch, halo)):
                acc = acc + w_ref[k:k + 1, :] * tap
            o_ref[pl.ds(r0, ch), :] = acc
            return carry

        lax.fori_loop(0, nch, chunk, 0)

    return pl.pallas_call(
        body, name=name, grid=(b, nc),
        in_specs=[
            pl.BlockSpec((None, s, tc), lambda bi, i: (bi, 0, i)),
            pl.BlockSpec((None, s, tc), lambda bi, i: (bi, 0, i + nc)),
            pl.BlockSpec((kw, tc), lambda bi, i: (0, i)),
            pl.BlockSpec((1, tc), lambda bi, i: (0, i)),
        ],
        out_specs=pl.BlockSpec((None, s, tc), lambda bi, i: (bi, 0, i)),
        out_shape=jax.ShapeDtypeStruct((b, s, c), F32),
        scratch_shapes=[pltpu.VMEM((nch, ch + halo, tc), F32)],
        compiler_params=_cp("parallel", "parallel"),
    )(z, z, dw, dwb)


def glu_conv_bwd(z, dw, du2, name, tc=128):
    b, s, c2 = z.shape
    c = c2 // 2
    kw = dw.shape[0]
    tc = min(tc, c)
    nc = c // tc
    ch = min(CONV_CHUNK, s)
    nch = s // ch
    halo = CONV_HALO

    def body(a_ref, g_ref, w_ref, du_ref, dza_ref, dzg_ref, ddw_ref, ddwb_ref, dba_ref, dbg_ref, upad_ref, dpad_ref):
        bi = pl.program_id(1)

        @pl.when(bi == 0)
        def _():
            ddw_ref[...] = jnp.zeros(ddw_ref.shape, F32)
            ddwb_ref[...] = jnp.zeros(ddwb_ref.shape, F32)
            dba_ref[...] = jnp.zeros(dba_ref.shape, F32)
            dbg_ref[...] = jnp.zeros(dbg_ref.shape, F32)

        _fill_glu_slabs(a_ref, g_ref, upad_ref, nch, ch, halo)
        dpad_ref[nch - 1, ch:ch + halo, :] = jnp.zeros((halo, tc), F32)
        dpad_ref[nch - 1, 0:ch, :] = du_ref[s - ch:s, :]

        def fill(ci, carry):
            r0 = pl.multiple_of(ci * ch, ch)
            dpad_ref[ci, :, :] = du_ref[pl.ds(r0, ch + halo), :]
            return carry

        lax.fori_loop(0, nch - 1, fill, 0)

        def chunk(ci, carry):
            r0 = pl.multiple_of(ci * ch, ch)
            du_c = du_ref[pl.ds(r0, ch), :]
            taps_u = _taps_front(upad_ref, ci, kw, ch, halo)
            du1 = w_ref[kw - 1:kw, :] * du_c
            ddw_ref[kw - 1] += jnp.sum((du_c * taps_u[kw - 1]).reshape(ch // 8, 8, tc), axis=0)
            for j in range(1, kw):
                du1 = du1 + w_ref[kw - 1 - j:kw - j, :] * dpad_ref[ci, j:j + ch, :]
                ddw_ref[kw - 1 - j] += jnp.sum((du_c * taps_u[kw - 1 - j]).reshape(ch // 8, 8, tc), axis=0)
            av = a_ref[pl.ds(r0, ch), :]
            sg = _sigmoid(g_ref[pl.ds(r0, ch), :])
            dza = du1 * sg
            dzg = du1 * av * (sg * (1.0 - sg))
            dza_ref[pl.ds(r0, ch), :] = dza.astype(BF16)
            dzg_ref[pl.ds(r0, ch), :] = dzg.astype(BF16)
            dba_ref[...] += jnp.sum(dza, axis=0, keepdims=True)
            dbg_ref[...] += jnp.sum(dzg, axis=0, keepdims=True)
            ddwb_ref[...] += jnp.sum(du_c, axis=0, keepdims=True)
            return carry

        lax.fori_loop(0, s // ch, chunk, 0)

    blk = lambda off: pl.BlockSpec((None, s, tc), lambda i, bi: (bi, 0, i + off))
    vec = pl.BlockSpec((1, tc), lambda i, bi: (0, i))
    return pl.pallas_call(
        body, name=name, grid=(nc, b),
        in_specs=[blk(0), blk(nc), pl.BlockSpec((kw, tc), lambda i, bi: (0, i)), blk(0)],
        out_specs=[blk(0), blk(0), pl.BlockSpec((kw, 8, tc), lambda i, bi: (0, 0, i)), vec, vec, vec],
        out_shape=[
            jax.ShapeDtypeStruct((b, s, c), BF16), jax.ShapeDtypeStruct((b, s, c), BF16),
            jax.ShapeDtypeStruct((kw, 8, c), F32), jax.ShapeDtypeStruct((1, c), F32),
            jax.ShapeDtypeStruct((1, c), F32), jax.ShapeDtypeStruct((1, c), F32),
        ],
        scratch_shapes=[pltpu.VMEM((nch, ch + halo, tc), F32), pltpu.VMEM((nch, ch + halo, tc), F32)],
        compiler_params=_cp("parallel", "arbitrary"),
    )(z, z, dw, du2)


def _fill_glu_slabs(a_ref, g_ref, pad_ref, nch, ch, halo):
    tc = a_ref.shape[-1]
    pad_ref[0, 0:halo, :] = jnp.zeros((halo, tc), F32)
    pad_ref[0, halo:halo + ch, :] = a_ref[0:ch, :] * _sigmoid(g_ref[0:ch, :])

    def fill(ci, carry):
        r0 = pl.multiple_of(ci * ch, ch)
        pad_ref[ci, 0:halo, :] = pad_ref[ci - 1, ch:ch + halo, :]
        pad_ref[ci, halo:halo + ch, :] = a_ref[pl.ds(r0, ch), :] * _sigmoid(g_ref[pl.ds(r0, ch), :])
        return carry

    lax.fori_loop(1, nch, fill, 0)


def ln_silu_fwd(u, g, bvec, name, tm=256):
    t, d = u.shape
    tm = _row_tile(t, tm)

    def body(u_ref, g_ref, b_ref, o_ref):
        uv = u_ref[...]
        mu = jnp.mean(uv, axis=-1, keepdims=True)
        xc = uv - mu
        var = jnp.mean(xc * xc, axis=-1, keepdims=True)
        v = (xc * lax.rsqrt(var + EPS)) * g_ref[...] + b_ref[...]
        o_ref[...] = (v * _sigmoid(v)).astype(BF16)

    row = pl.BlockSpec((tm, d), lambda i: (i, 0))
    vec = pl.BlockSpec((1, d), lambda i: (0, 0))
    return pl.pallas_call(
        body, name=name, grid=(t // tm,), in_specs=[row, vec, vec], out_specs=row,
        out_shape=jax.ShapeDtypeStruct((t, d), BF16), compiler_params=_cp("parallel"),
    )(u, g, bvec)


def ln_silu_bwd(u, g, bvec, dout, name, tm=256):
    t, d = u.shape
    tm = _row_tile(t, tm)

    def body(u_ref, g_ref, b_ref, do_ref, du_ref, dg_ref, db_ref):
        i = pl.program_id(0)
        uv = u_ref[...]
        mu = jnp.mean(uv, axis=-1, keepdims=True)
        xc = uv - mu
        var = jnp.mean(xc * xc, axis=-1, keepdims=True)
        rstd = lax.rsqrt(var + EPS)
        n = xc * rstd
        v = n * g_ref[...] + b_ref[...]
        sg = _sigmoid(v)
        dv = do_ref[...].astype(F32) * (sg * (1.0 + v * (1.0 - sg)))
        dn = dv * g_ref[...]
        du_ref[...] = rstd * (dn - jnp.mean(dn, axis=-1, keepdims=True) - n * jnp.mean(dn * n, axis=-1, keepdims=True))
        dg = jnp.sum(dv * n, axis=0, keepdims=True)
        db = jnp.sum(dv, axis=0, keepdims=True)

        @pl.when(i == 0)
        def _():
            dg_ref[...] = dg
            db_ref[...] = db

        @pl.when(i > 0)
        def _():
            dg_ref[...] += dg
            db_ref[...] += db

    row = pl.BlockSpec((tm, d), lambda i: (i, 0))
    vec = pl.BlockSpec((1, d), lambda i: (0, 0))
    return pl.pallas_call(
        body, name=name, grid=(t // tm,), in_specs=[row, vec, vec, row], out_specs=[row, vec, vec],
        out_shape=[jax.ShapeDtypeStruct((t, d), F32), jax.ShapeDtypeStruct((1, d), F32), jax.ShapeDtypeStruct((1, d), F32)],
        compiler_params=_cp("arbitrary"),
    )(u, g, bvec, dout)


FFN_HALO = 8


def _fill_front_halo(src_ref, pad_ref, nch, ch, halo):
    tc = src_ref.shape[-1]
    pad_ref[0, 0:halo, :] = jnp.zeros((halo, tc), F32)
    pad_ref[0, halo:halo + ch, :] = src_ref[0:ch, :].astype(F32)

    def fill(ci, carry):
        r0 = pl.multiple_of(ci * ch, ch)
        pad_ref[ci, 0:halo, :] = src_ref[pl.ds(r0 - halo, halo), :].astype(F32)
        pad_ref[ci, halo:halo + ch, :] = src_ref[pl.ds(r0, ch), :].astype(F32)
        return carry

    lax.fori_loop(1, nch, fill, 0)


def _taps_front(pad_ref, ci, kw, ch, halo):
    return [pad_ref[ci, halo - (kw - 1 - k):halo - (kw - 1 - k) + ch, :] for k in range(kw)]


def ffn_mid_fwd(p, dw, dwb, name, tc=256):
    b, s, f2 = p.shape
    f = f2 // 2
    kw = dw.shape[0]
    tc = min(tc, f)
    nf = f // tc
    ch = min(CONV_CHUNK, s)
    nch = s // ch
    halo = FFN_HALO

    def body(pa_ref, pg_ref, wa_ref, wg_ref, ba_ref, bg_ref, o_ref, apad_ref, gpad_ref):
        _fill_front_halo(pa_ref, apad_ref, nch, ch, halo)
        _fill_front_halo(pg_ref, gpad_ref, nch, ch, halo)

        def chunk(ci, carry):
            r0 = pl.multiple_of(ci * ch, ch)
            ca = ba_ref[...]
            cg = bg_ref[...]
            taps = zip(_taps_front(apad_ref, ci, kw, ch, halo), _taps_front(gpad_ref, ci, kw, ch, halo))
            for k, (ta, tg) in enumerate(taps):
                ca = ca + wa_ref[k:k + 1, :] * ta
                cg = cg + wg_ref[k:k + 1, :] * tg
            o_ref[pl.ds(r0, ch), :] = ((cg * _sigmoid(cg)) * ca).astype(BF16)
            return carry

        lax.fori_loop(0, nch, chunk, 0)

    blk = lambda off: pl.BlockSpec((None, s, tc), lambda bi, i: (bi, 0, i + off))
    wsp = lambda off: pl.BlockSpec((kw, tc), lambda bi, i: (0, i + off))
    bsp = lambda off: pl.BlockSpec((1, tc), lambda bi, i: (0, i + off))
    return pl.pallas_call(
        body, name=name, grid=(b, nf),
        in_specs=[blk(0), blk(nf), wsp(0), wsp(nf), bsp(0), bsp(nf)],
        out_specs=pl.BlockSpec((None, s, tc), lambda bi, i: (bi, 0, i)),
        out_shape=jax.ShapeDtypeStruct((b, s, f), BF16),
        scratch_shapes=[pltpu.VMEM((nch, ch + halo, tc), F32)] * 2,
        compiler_params=_cp("parallel", "parallel"),
    )(p, p, dw, dw, dwb, dwb)


def ffn_mid_bwd(p, dw, dwb, ds, name, tc=256):
    b, s, f2 = p.shape
    f = f2 // 2
    kw = dw.shape[0]
    tc = min(tc, f)
    nf = f // tc
    ch = min(CONV_CHUNK, s)
    nch = s // ch
    halo = FFN_HALO

    def sum8(v):
        return jnp.sum(v.reshape(ch // 8, 8, tc), axis=0)

    def body(pa_ref, pg_ref, wa_ref, wg_ref, ba_ref, bg_ref, ds_ref, dpa_ref, dpg_ref, ddwa_ref, ddwg_ref, dba_ref, dbg_ref,
             apad_ref, gpad_ref, dca_ref, dcg_ref):
        bi = pl.program_id(1)

        @pl.when(bi == 0)
        def _():
            ddwa_ref[...] = jnp.zeros(ddwa_ref.shape, F32)
            ddwg_ref[...] = jnp.zeros(ddwg_ref.shape, F32)
            dba_ref[...] = jnp.zeros(dba_ref.shape, F32)
            dbg_ref[...] = jnp.zeros(dbg_ref.shape, F32)

        _fill_front_halo(pa_ref, apad_ref, nch, ch, halo)
        _fill_front_halo(pg_ref, gpad_ref, nch, ch, halo)
        dca_ref[nch - 1, ch:ch + halo, :] = jnp.zeros((halo, tc), F32)
        dcg_ref[nch - 1, ch:ch + halo, :] = jnp.zeros((halo, tc), F32)

        def grads(ci, carry):
            acc_a, acc_g, sb_a, sb_g = carry
            r0 = pl.multiple_of(ci * ch, ch)
            taps_a = _taps_front(apad_ref, ci, kw, ch, halo)
            taps_g = _taps_front(gpad_ref, ci, kw, ch, halo)
            ca = ba_ref[...]
            cg = bg_ref[...]
            for k in range(kw):
                ca = ca + wa_ref[k:k + 1, :] * taps_a[k]
                cg = cg + wg_ref[k:k + 1, :] * taps_g[k]
            sg = _sigmoid(cg)
            dsv = ds_ref[pl.ds(r0, ch), :].astype(F32)
            dca = dsv * (cg * sg)
            dcg = dsv * ca * (sg * (1.0 + cg * (1.0 - sg)))
            dca_ref[ci, 0:ch, :] = dca
            dcg_ref[ci, 0:ch, :] = dcg

            prev = jnp.maximum(ci - 1, 0)

            @pl.when(ci > 0)
            def _():
                dca_ref[prev, ch:ch + halo, :] = dca[0:halo, :]
                dcg_ref[prev, ch:ch + halo, :] = dcg[0:halo, :]

            acc_a = tuple(acc_a[k] + sum8(dca * taps_a[k]) for k in range(kw))
            acc_g = tuple(acc_g[k] + sum8(dcg * taps_g[k]) for k in range(kw))
            return acc_a, acc_g, sb_a + sum8(dca), sb_g + sum8(dcg)

        z8 = jnp.zeros((8, tc), F32)
        acc_a, acc_g, sb_a, sb_g = lax.fori_loop(0, nch, grads, ((z8,) * kw, (z8,) * kw, z8, z8))
        for k in range(kw):
            ddwa_ref[k] += acc_a[k]
            ddwg_ref[k] += acc_g[k]
        dba_ref[...] += jnp.sum(sb_a, axis=0, keepdims=True)
        dbg_ref[...] += jnp.sum(sb_g, axis=0, keepdims=True)

        def back(ci, carry):
            r0 = pl.multiple_of(ci * ch, ch)
            da = wa_ref[kw - 1:kw, :] * dca_ref[ci, 0:ch, :]
            dg = wg_ref[kw - 1:kw, :] * dcg_ref[ci, 0:ch, :]
            for j in range(1, kw):
                da = da + wa_ref[kw - 1 - j:kw - j, :] * dca_ref[ci, j:j + ch, :]
                dg = dg + wg_ref[kw - 1 - j:kw - j, :] * dcg_ref[ci, j:j + ch, :]
            dpa_ref[pl.ds(r0, ch), :] = da.astype(BF16)
            dpg_ref[pl.ds(r0, ch), :] = dg.astype(BF16)
            return carry

        lax.fori_loop(0, nch, back, 0)

    blk = lambda off: pl.BlockSpec((None, s, tc), lambda i, bi: (bi, 0, i + off))
    wsp = lambda off: pl.BlockSpec((kw, tc), lambda i, bi: (0, i + off))
    bsp = lambda off: pl.BlockSpec((1, tc), lambda i, bi: (0, i + off))
    acc3 = pl.BlockSpec((kw, 8, tc), lambda i, bi: (0, 0, i))
    vec = pl.BlockSpec((1, tc), lambda i, bi: (0, i))
    return pl.pallas_call(
        body, name=name, grid=(nf, b),
        in_specs=[blk(0), blk(nf), wsp(0), wsp(nf), bsp(0), bsp(nf), blk(0)],
        out_specs=[blk(0), blk(0), acc3, acc3, vec, vec],
        out_shape=[jax.ShapeDtypeStruct((b, s, f), BF16), jax.ShapeDtypeStruct((b, s, f), BF16),
                   jax.ShapeDtypeStruct((kw, 8, f), F32), jax.ShapeDtypeStruct((kw, 8, f), F32),
                   jax.ShapeDtypeStruct((1, f), F32), jax.ShapeDtypeStruct((1, f), F32)],
        scratch_shapes=[pltpu.VMEM((nch, ch + halo, tc), F32)] * 4,
        compiler_params=_cp("parallel", "arbitrary"),
    )(p, p, dw, dw, dwb, dwb, ds)


def perm_rows(x, col_blk, width, dil, inverse, out_dtype, name, tc=LANES):
    b, s, _ = x.shape
    tc = min(tc, width)
    nc = width // tc
    ln = s // dil

    def body(x_ref, o_ref):
        if dil == 1:
            o_ref[...] = x_ref[...].astype(out_dtype)
            return
        for r in range(dil):
            if inverse:
                o_ref[pl.ds(r, ln, stride=dil), :] = x_ref[r * ln:(r + 1) * ln, :].astype(out_dtype)
            else:
                o_ref[r * ln:(r + 1) * ln, :] = x_ref[pl.ds(r, ln, stride=dil), :].astype(out_dtype)

    return pl.pallas_call(
        body, name=name, grid=(b, nc),
        in_specs=[pl.BlockSpec((None, s, tc), lambda bi, i: (bi, 0, col_blk * nc + i))],
        out_specs=pl.BlockSpec((None, s, tc), lambda bi, i: (bi, 0, i)),
        out_shape=jax.ShapeDtypeStruct((b, s, width), out_dtype),
        compiler_params=_cp("parallel", "parallel"),
    )(x)


def _attn_masks(has_other):
    qi = lax.broadcasted_iota(jnp.int32, (BLK, BLK), 0)
    kk = lax.broadcasted_iota(jnp.int32, (BLK, BLK), 1)
    return kk <= qi, jnp.logical_and(kk >= qi, has_other)


def attn_fwd(q, q_blk, k, k_blk, v, v_blk, nblk, hw, name):
    b, s, _ = q.shape
    nh = hw // HEAD_DIM
    scale = 1.0 / math.sqrt(HEAD_DIM)

    def body(q_ref, kc_ref, kp_ref, vc_ref, vp_ref, o_ref, lse_ref):
        t = pl.program_id(1)
        mask_c, mask_p = _attn_masks((t % nblk) != 0)
        lane = lax.broadcasted_iota(jnp.int32, (BLK, LANES), 1)
        lse_acc = jnp.zeros((BLK, LANES), F32)
        for h in range(nh):
            sl = slice(h * HEAD_DIM, (h + 1) * HEAD_DIM)
            qh = q_ref[:, sl].astype(BF16)
            kc = kc_ref[:, sl].astype(BF16)
            kp = kp_ref[:, sl].astype(BF16)
            s_c = jnp.where(mask_c, _dot_nt(qh, kc) * scale, NEG_INF)
            s_p = jnp.where(mask_p, _dot_nt(qh, kp) * scale, NEG_INF)
            m = jnp.maximum(jnp.max(s_c, axis=-1, keepdims=True), jnp.max(s_p, axis=-1, keepdims=True))
            p_c = jnp.exp(s_c - m)
            p_p = jnp.exp(s_p - m)
            den = jnp.sum(p_c, axis=-1, keepdims=True) + jnp.sum(p_p, axis=-1, keepdims=True)
            o = _dot(p_c.astype(BF16), vc_ref[:, sl].astype(BF16)) + _dot(p_p.astype(BF16), vp_ref[:, sl].astype(BF16))
            o_ref[:, sl] = o / den
            lse_acc = jnp.where(lane == h, m + jnp.log(den), lse_acc)
        lse_ref[...] = lse_acc

    cur = lambda cb: pl.BlockSpec((None, BLK, hw), lambda bi, t: (bi, t, cb))
    prev = lambda cb: pl.BlockSpec((None, BLK, hw), lambda bi, t: (bi, jnp.where(t % nblk == 0, t, t - 1), cb))
    return pl.pallas_call(
        body, name=name, grid=(b, s // BLK),
        in_specs=[cur(q_blk), cur(k_blk), prev(k_blk), cur(v_blk), prev(v_blk)],
        out_specs=[pl.BlockSpec((None, BLK, hw), lambda bi, t: (bi, t, 0)),
                   pl.BlockSpec((None, BLK, LANES), lambda bi, t: (bi, t, 0))],
        out_shape=[jax.ShapeDtypeStruct((b, s, hw), F32), jax.ShapeDtypeStruct((b, s, LANES), F32)],
        compiler_params=_cp("parallel", "parallel"),
    )(q, k, k, v, v)


def attn_merge(outs, lses, name, tm=256):
    t, hw = outs[0].shape
    nh = hw // HEAD_DIM
    tm = _row_tile(t, tm)
    ng = len(outs)

    def body(*refs):
        o_refs, l_refs = refs[:ng], refs[ng:2 * ng]
        m_ref, lj_ref = refs[2 * ng:]
        lane = lax.broadcasted_iota(jnp.int32, (tm, LANES), 1)
        lj = jnp.zeros((tm, LANES), F32)
        for h in range(nh):
            sl = slice(h * HEAD_DIM, (h + 1) * HEAD_DIM)
            ls = [l_refs[g][:, h:h + 1] for g in range(ng)]
            mx = functools.reduce(jnp.maximum, ls)
            es = [jnp.exp(l - mx) for l in ls]
            tot = functools.reduce(lambda a, c: a + c, es)
            acc = (es[0] / tot) * o_refs[0][:, sl]
            for g in range(1, ng):
                acc = acc + (es[g] / tot) * o_refs[g][:, sl]
            m_ref[:, sl] = acc.astype(BF16)
            lj = jnp.where(lane == h, mx + jnp.log(tot), lj)
        lj_ref[...] = lj

    row = pl.BlockSpec((tm, hw), lambda i: (i, 0))
    st = pl.BlockSpec((tm, LANES), lambda i: (i, 0))
    return pl.pallas_call(
        body, name=name, grid=(t // tm,), in_specs=[row] * ng + [st] * ng, out_specs=[row, st],
        out_shape=[jax.ShapeDtypeStruct((t, hw), BF16), jax.ShapeDtypeStruct((t, LANES), F32)],
        compiler_params=_cp("parallel"),
    )(*outs, *lses)


def attn_bwd_prep(dmerged, merged, name, tm=256):
    t, hw = merged.shape
    nh = hw // HEAD_DIM
    tm = _row_tile(t, tm)

    def body(d_ref, m_ref, o_ref):
        lane = lax.broadcasted_iota(jnp.int32, (tm, LANES), 1)
        acc = jnp.zeros((tm, LANES), F32)
        for h in range(nh):
            sl = slice(h * HEAD_DIM, (h + 1) * HEAD_DIM)
            dsum = jnp.sum(d_ref[:, sl] * m_ref[:, sl].astype(F32), axis=-1, keepdims=True)
            acc = jnp.where(lane == h, dsum, acc)
        o_ref[...] = acc

    row = pl.BlockSpec((tm, hw), lambda i: (i, 0))
    return pl.pallas_call(
        body, name=name, grid=(t // tm,), in_specs=[row, row], out_specs=pl.BlockSpec((tm, LANES), lambda i: (i, 0)),
        out_shape=jax.ShapeDtypeStruct((t, LANES), F32), compiler_params=_cp("parallel"),
    )(dmerged, merged)


def attn_bwd(q, q_blk, k, k_blk, v, v_blk, do, lsej, dm, dk_add, dv_add, nblk, hw, name):
    b, s, _ = q.shape
    nh = hw // HEAD_DIM
    scale = 1.0 / math.sqrt(HEAD_DIM)
    has_add = dk_add is not None

    def body(*refs):
        (qm_ref, qn_ref, kc_ref, kp_ref, vc_ref, vp_ref, dom_ref, don_ref, lm_ref, ln_ref, dmm_ref, dmn_ref) = refs[:12]
        if has_add:
            dka_ref, dva_ref = refs[12:14]
            dq_ref, dk_ref, dv_ref = refs[14:]
        else:
            dq_ref, dk_ref, dv_ref = refs[12:]
        t = pl.program_id(1)
        n = t % nblk
        mask_c, mask_p = _attn_masks(n != 0)
        _, mask_n = _attn_masks(n != nblk - 1)
        for h in range(nh):
            sl = slice(h * HEAD_DIM, (h + 1) * HEAD_DIM)
            qm = qm_ref[:, sl].astype(BF16)
            qn = qn_ref[:, sl].astype(BF16)
            kc = kc_ref[:, sl].astype(BF16)
            kp = kp_ref[:, sl].astype(BF16)
            vc = vc_ref[:, sl].astype(BF16)
            vp = vp_ref[:, sl].astype(BF16)
            dom = dom_ref[:, sl].astype(BF16)
            don = don_ref[:, sl].astype(BF16)
            lm, lnx = lm_ref[:, h:h + 1], ln_ref[:, h:h + 1]
            dmm, dmn = dmm_ref[:, h:h + 1], dmn_ref[:, h:h + 1]
            pa = jnp.exp(jnp.where(mask_c, _dot_nt(qm, kc) * scale, NEG_INF) - lm)
            pb = jnp.exp(jnp.where(mask_p, _dot_nt(qm, kp) * scale, NEG_INF) - lm)
            pc = jnp.exp(jnp.where(mask_n, _dot_nt(qn, kc) * scale, NEG_INF) - lnx)
            dsa = (pa * (_dot_nt(dom, vc) - dmm)).astype(BF16)
            dsb = (pb * (_dot_nt(dom, vp) - dmm)).astype(BF16)
            dsc = (pc * (_dot_nt(don, vc) - dmn)).astype(BF16)
            dq_ref[:, sl] = (_dot(dsa, kc) + _dot(dsb, kp)) * scale
            dk = (_dot_tn(dsa, qm) + _dot_tn(dsc, qn)) * scale
            dv = _dot_tn(pa.astype(BF16), dom) + _dot_tn(pc.astype(BF16), don)
            if has_add:
                dk = dk + dka_ref[:, sl]
                dv = dv + dva_ref[:, sl]
            dk_ref[:, sl] = dk
            dv_ref[:, sl] = dv

    def spec(cb, w, which):
        if which == "cur":
            return pl.BlockSpec((None, BLK, w), lambda bi, t: (bi, t, cb))
        if which == "prev":
            return pl.BlockSpec((None, BLK, w), lambda bi, t: (bi, jnp.where(t % nblk == 0, t, t - 1), cb))
        return pl.BlockSpec((None, BLK, w), lambda bi, t: (bi, jnp.where(t % nblk == nblk - 1, t, t + 1), cb))

    in_specs = [
        spec(q_blk, hw, "cur"), spec(q_blk, hw, "next"), spec(k_blk, hw, "cur"), spec(k_blk, hw, "prev"),
        spec(v_blk, hw, "cur"), spec(v_blk, hw, "prev"), spec(0, hw, "cur"), spec(0, hw, "next"),
        spec(0, LANES, "cur"), spec(0, LANES, "next"), spec(0, LANES, "cur"), spec(0, LANES, "next"),
    ]
    args = [q, q, k, k, v, v, do, do, lsej, lsej, dm, dm]
    if has_add:
        in_specs += [spec(0, hw, "cur"), spec(0, hw, "cur")]
        args += [dk_add, dv_add]
    out = pl.BlockSpec((None, BLK, hw), lambda bi, t: (bi, t, 0))
    return pl.pallas_call(
        body, name=name, grid=(b, s // BLK), in_specs=in_specs, out_specs=[out, out, out],
        out_shape=[jax.ShapeDtypeStruct((b, s, hw), F32)] * 3,
        compiler_params=_cp("parallel", "parallel"),
    )(*args)


def sum_parts(g, recv, me, name, tm=256):
    _, rows, c = g.shape
    n = recv.shape[0]
    tm = _row_tile(rows, tm)

    def body(me_ref, g_ref, r_ref, o_ref):
        acc = g_ref[...].astype(F32)
        for j in range(n):
            acc = acc + r_ref[j].astype(F32)
        o_ref[...] = acc

    return pl.pallas_call(
        body, name=name,
        grid_spec=pltpu.PrefetchScalarGridSpec(
            num_scalar_prefetch=1, grid=(rows // tm,),
            in_specs=[pl.BlockSpec((None, tm, c), lambda i, me_ref: (me_ref[0], i, 0)),
                      pl.BlockSpec((n, tm, c), lambda i, me_ref: (0, i, 0))],
            out_specs=pl.BlockSpec((tm, c), lambda i, me_ref: (i, 0))),
        out_shape=jax.ShapeDtypeStruct((rows, c), F32), compiler_params=_cp("parallel"),
    )(me, g, recv)


def adamw(w, m, v, g_parts, name, tm=256):
    rows, c = w.shape
    tm = _row_tile(rows, tm)
    npart = len(g_parts)

    def body(*refs):
        w_ref, m_ref, v_ref = refs[:3]
        g_refs = refs[3:3 + npart]
        go_ref, d_ref, mo_ref, vo_ref = refs[3 + npart:]
        g = g_refs[0][...]
        for k in range(1, npart):
            g = g + g_refs[k][...]
        mn = ADAM_B1 * m_ref[...] + (1.0 - ADAM_B1) * g
        vn = ADAM_B2 * v_ref[...] + (1.0 - ADAM_B2) * (g * g)
        m_hat = mn / (1.0 - ADAM_B1 ** ADAM_STEP)
        v_hat = vn / (1.0 - ADAM_B2 ** ADAM_STEP)
        go_ref[...] = g
        d_ref[...] = -ADAM_LR * (m_hat / (jnp.sqrt(v_hat) + ADAM_EPS) + ADAM_WD * w_ref[...])
        mo_ref[...] = mn
        vo_ref[...] = vn

    row = pl.BlockSpec((tm, c), lambda i: (i, 0))
    return pl.pallas_call(
        body, name=name, grid=(rows // tm,), in_specs=[row] * (3 + npart), out_specs=[row] * 4,
        out_shape=[jax.ShapeDtypeStruct((rows, c), F32)] * 4, compiler_params=_cp("parallel"),
    )(w, m, v, *g_parts)


def _place():
    return lax.axis_index("x"), lax.axis_index("y"), lax.axis_index("c")


def _other_chips(x, y, c):
    return [(1 - x, y, c), (x, 1 - y, c), (1 - x, 1 - y, c)]


def _chip_of(px, py):
    return 2 * px + py


HBM_SPEC = pl.BlockSpec(memory_space=pltpu.HBM)
SEM_SPEC = pl.BlockSpec(memory_space=pltpu.SEMAPHORE)
ANY_SPEC = pl.BlockSpec(memory_space=pl.ANY)
DATAFLOW = pltpu.SideEffectType.DATAFLOW_SIDE_EFFECTING
N_PEER_CHIPS = N_CHIPS - 1


def _hbm(a):
    return pltpu.with_memory_space_constraint(a, pltpu.HBM)


def _hbm_like(arrays):
    return [pltpu.HBM(a.shape, a.dtype) for a in arrays]


def cast_place(w, layer, me, out_dtype, name, tm=256):
    rows, c = w.shape[-2:]
    tm = _row_tile(rows, tm)

    def body(me_ref, w_ref, o_ref):
        o_ref[...] = w_ref[...].astype(out_dtype)

    if layer is None:
        in_spec = pl.BlockSpec((tm, c), lambda i, me_ref: (i, 0))
    else:
        in_spec = pl.BlockSpec((None, tm, c), lambda i, me_ref: (layer, i, 0))
    return pl.pallas_call(
        body, name=name,
        grid_spec=pltpu.PrefetchScalarGridSpec(
            num_scalar_prefetch=1, grid=(rows // tm,), in_specs=[in_spec],
            out_specs=pl.BlockSpec((None, tm, c), lambda i, me_ref: (me_ref[0], i, 0))),
        out_shape=jax.ShapeDtypeStruct((N_CHIPS, rows, c), out_dtype), compiler_params=_cp("parallel"),
    )(me, w)


def gather_start(lands, chunk_sizes, name="gather_start"):
    n = len(lands)
    nch = len(chunk_sizes)
    assert sum(chunk_sizes) == n

    def body(*refs):
        land_refs = refs[:n]
        outs = refs[n:]
        send_sems, recv_sems = outs[:nch], outs[nch:2 * nch]
        token = outs[-1]
        x, y, c = _place()
        me = _chip_of(x, y)
        peers = _other_chips(x, y, c)
        k = 0
        for ck, size in enumerate(chunk_sizes):
            for pos in range(size):
                for r, peer in enumerate(peers):
                    pltpu.make_async_remote_copy(
                        src_ref=land_refs[k].at[me], dst_ref=land_refs[k].at[me],
                        send_sem=send_sems[ck].at[N_PEER_CHIPS * pos + r], recv_sem=recv_sems[ck].at[N_PEER_CHIPS * pos + r],
                        device_id=peer, device_id_type=MESH).start()
                k += 1
        token[...] = jnp.zeros(token.shape, F32)

    sems = [pltpu.SemaphoreType.DMA((N_PEER_CHIPS * s,)) for s in chunk_sizes]
    res = pl.pallas_call(
        body, name=name,
        out_shape=(*sems, *sems, *_hbm_like(lands), jax.ShapeDtypeStruct(DEP_SPEC_SHAPE, F32)),
        in_specs=[HBM_SPEC] * n,
        out_specs=(*[SEM_SPEC] * (2 * nch), *[HBM_SPEC] * n, pl.BlockSpec(memory_space=pltpu.VMEM)),
        input_output_aliases={k: 2 * nch + k for k in range(n)},
        compiler_params=pltpu.CompilerParams(has_side_effects=DATAFLOW),
    )(*[_hbm(a) for a in lands])
    return res[:nch], res[nch:2 * nch], res[2 * nch:2 * nch + n], res[-1]


def gather_wait(send_sem, recv_sem, lands, after, name):
    n = len(lands)

    def body(*refs):
        land_refs = refs[:n]
        ssem, rsem = refs[n], refs[n + 1]
        x, y, c = _place()
        me = _chip_of(x, y)
        for pos in range(n):
            for r, peer in enumerate(_other_chips(x, y, c)):
                cp = pltpu.make_async_remote_copy(
                    src_ref=land_refs[pos].at[me], dst_ref=land_refs[pos].at[_chip_of(peer[0], peer[1])],
                    send_sem=ssem.at[N_PEER_CHIPS * pos + r], recv_sem=rsem.at[N_PEER_CHIPS * pos + r],
                    device_id=peer, device_id_type=MESH)
                cp.wait_send()
                cp.wait_recv()

    return pl.pallas_call(
        body, name=name, out_shape=tuple(_hbm_like(lands)),
        in_specs=[*[HBM_SPEC] * n, SEM_SPEC, SEM_SPEC, ANY_SPEC], out_specs=[HBM_SPEC] * n,
        input_output_aliases={k: k for k in range(n)},
        compiler_params=pltpu.CompilerParams(has_side_effects=DATAFLOW),
    )(*lands, send_sem, recv_sem, after)


def scatter_start(grads, name):
    n = len(grads)
    recvs = [lax.empty((N_PEER_CHIPS, *g.shape[1:]), g.dtype) for g in grads]

    def body(*refs):
        g_refs, r_refs = refs[:n], refs[n:2 * n]
        send_sems, recv_sems = refs[2 * n], refs[2 * n + 1]
        token = refs[-1]
        x, y, c = _place()
        for k in range(n):
            for r, peer in enumerate(_other_chips(x, y, c)):
                pltpu.make_async_remote_copy(
                    src_ref=g_refs[k].at[_chip_of(peer[0], peer[1])], dst_ref=r_refs[k].at[r],
                    send_sem=send_sems.at[N_PEER_CHIPS * k + r], recv_sem=recv_sems.at[N_PEER_CHIPS * k + r],
                    device_id=peer, device_id_type=MESH).start()
        token[...] = jnp.zeros(token.shape, F32)

    sem = pltpu.SemaphoreType.DMA((N_PEER_CHIPS * n,))
    res = pl.pallas_call(
        body, name=name,
        out_shape=(sem, sem, *_hbm_like(grads), *_hbm_like(recvs), jax.ShapeDtypeStruct(DEP_SPEC_SHAPE, F32)),
        in_specs=[HBM_SPEC] * (2 * n),
        out_specs=(SEM_SPEC, SEM_SPEC, *[HBM_SPEC] * (2 * n), pl.BlockSpec(memory_space=pltpu.VMEM)),
        input_output_aliases={k: 2 + k for k in range(2 * n)},
        compiler_params=pltpu.CompilerParams(has_side_effects=DATAFLOW),
    )(*[_hbm(a) for a in grads], *[_hbm(a) for a in recvs])
    return res[0], res[1], res[2:2 + n], res[2 + n:2 + 2 * n], res[-1]


def scatter_wait(send_sem, recv_sem, grads, recvs, after, name):
    n = len(grads)

    def body(*refs):
        g_refs, r_refs = refs[:n], refs[n:2 * n]
        ssem, rsem = refs[2 * n], refs[2 * n + 1]
        x, y, c = _place()
        for k in range(n):
            for r, peer in enumerate(_other_chips(x, y, c)):
                cp = pltpu.make_async_remote_copy(
                    src_ref=g_refs[k].at[_chip_of(peer[0], peer[1])], dst_ref=r_refs[k].at[r],
                    send_sem=ssem.at[N_PEER_CHIPS * k + r], recv_sem=rsem.at[N_PEER_CHIPS * k + r],
                    device_id=peer, device_id_type=MESH)
                cp.wait_send()
                cp.wait_recv()

    res = pl.pallas_call(
        body, name=name, out_shape=(*_hbm_like(grads), *_hbm_like(recvs)),
        in_specs=[*[HBM_SPEC] * (2 * n), SEM_SPEC, SEM_SPEC, ANY_SPEC], out_specs=[HBM_SPEC] * (2 * n),
        input_output_aliases={k: k for k in range(2 * n)},
        compiler_params=pltpu.CompilerParams(has_side_effects=DATAFLOW),
    )(*grads, *recvs, send_sem, recv_sem, after)
    return res[:n], res[n:]


def swap_with_sibling(parts, name="swap_with_sibling"):
    n = len(parts)

    def body(*refs):
        ins, outs = refs[:n], refs[n:2 * n]
        send_sems, recv_sems = refs[2 * n:]
        x, y, c = _place()
        copies = []
        for k in range(n):
            cp = pltpu.make_async_remote_copy(
                src_ref=ins[k], dst_ref=outs[k], send_sem=send_sems.at[k], recv_sem=recv_sems.at[k],
                device_id=(x, y, 1 - c), device_id_type=MESH)
            cp.start()
            copies.append(cp)
        for cp in copies:
            cp.wait()

    any_spec = pl.BlockSpec(memory_space=pl.ANY)
    return pl.pallas_call(
        body, name=name, in_specs=[any_spec] * n, out_specs=[any_spec] * n,
        out_shape=[jax.ShapeDtypeStruct(p.shape, p.dtype) for p in parts],
        scratch_shapes=[pltpu.SemaphoreType.DMA((n,)), pltpu.SemaphoreType.DMA((n,))],
        compiler_params=pltpu.CompilerParams(has_side_effects=True),
    )(*parts)


def allreduce_small(vec, name="allreduce_small"):
    rows, lanes = vec.shape

    def body(v_ref, o_ref, buf_ref, send_sems, recv_sems):
        x, y, c = _place()
        me = 4 * x + 2 * y + c
        buf_ref[me] = v_ref[...]
        copies = []
        for k in range(1, N_DEV):
            px, py, pc = x ^ ((k >> 2) & 1), y ^ ((k >> 1) & 1), c ^ (k & 1)
            peer_slot = 4 * px + 2 * py + pc
            cp = pltpu.make_async_remote_copy(
                src_ref=buf_ref.at[me], dst_ref=buf_ref.at[me], send_sem=send_sems.at[k - 1],
                recv_sem=recv_sems.at[k - 1], device_id=(px, py, pc), device_id_type=MESH)
            cp.start()
            copies.append(pltpu.make_async_remote_copy(
                src_ref=buf_ref.at[me], dst_ref=buf_ref.at[peer_slot], send_sem=send_sems.at[k - 1],
                recv_sem=recv_sems.at[k - 1], device_id=(px, py, pc), device_id_type=MESH))
        for cp in copies:
            cp.wait()
        acc = buf_ref[0]
        for j in range(1, N_DEV):
            acc = acc + buf_ref[j]
        o_ref[...] = acc

    vm = pl.BlockSpec(memory_space=pltpu.VMEM)
    return pl.pallas_call(
        body, name=name, in_specs=[vm], out_specs=vm, out_shape=jax.ShapeDtypeStruct((rows, lanes), F32),
        scratch_shapes=[pltpu.VMEM((N_DEV, rows, lanes), F32), pltpu.SemaphoreType.DMA((N_DEV - 1,)),
                        pltpu.SemaphoreType.DMA((N_DEV - 1,))],
        compiler_params=pltpu.CompilerParams(has_side_effects=True, vmem_limit_bytes=VMEM_LIMIT_V7X),
    )(vec)


def _pack(arrays):
    flat = jnp.concatenate([a.reshape(-1).astype(F32) for a in arrays])
    n = flat.shape[0]
    rows = -(-n // LANES)
    rows = -(-rows // 8) * 8
    return jnp.pad(flat, (0, rows * LANES - n)).reshape(rows, LANES)


def _unpack(packed, shapes, lead=()):
    flat = packed.reshape(*lead, -1)
    out, off = [], 0
    for shp in shapes:
        n = math.prod(shp)
        out.append(flat[..., off:off + n].reshape(*lead, *shp))
        off += n
    return out


def _row(vec):
    return vec.reshape(1, -1)


def kernel(x, mix_pre_g, mix_post_g, ffn_pre_g, ffn_post_g, cm_w_in, cm_b_in, cm_dw, cm_dw_b, cm_ln_g, cm_ln_b, cm_w_out, cm_b_out, kv_norm_g, w_kv, w_q, w_o, ffn_w_in, ffn_dw, ffn_dw_b, ffn_w_out, loss_target, m_mix_pre_g, m_mix_post_g, m_ffn_pre_g, m_ffn_post_g, m_cm_w_in, m_cm_b_in, m_cm_dw, m_cm_dw_b, m_cm_ln_g, m_cm_ln_b, m_cm_w_out, m_cm_b_out, m_kv_norm_g, m_w_kv, m_w_q, m_w_o, m_ffn_w_in, m_ffn_dw, m_ffn_dw_b, m_ffn_w_out, v_mix_pre_g, v_mix_post_g, v_ffn_pre_g, v_ffn_post_g, v_cm_w_in, v_cm_b_in, v_cm_dw, v_cm_dw_b, v_cm_ln_g, v_cm_ln_b, v_cm_w_out, v_cm_b_out, v_kv_norm_g, v_w_kv, v_w_q, v_w_o, v_ffn_w_in, v_ffn_dw, v_ffn_dw_b, v_ffn_w_out):
    names = ["mix_pre_g", "mix_post_g", "ffn_pre_g", "ffn_post_g", "cm_w_in", "cm_b_in", "cm_dw", "cm_dw_b", "cm_ln_g",
             "cm_ln_b", "cm_w_out", "cm_b_out", "kv_norm_g", "w_kv", "w_q", "w_o", "ffn_w_in", "ffn_dw", "ffn_dw_b",
             "ffn_w_out"]
    w_in = dict(zip(names, [mix_pre_g, mix_post_g, ffn_pre_g, ffn_post_g, cm_w_in, cm_b_in, cm_dw, cm_dw_b, cm_ln_g,
                            cm_ln_b, cm_w_out, cm_b_out, kv_norm_g, w_kv, w_q, w_o, ffn_w_in, ffn_dw, ffn_dw_b, ffn_w_out]))
    m_in = dict(zip(names, [m_mix_pre_g, m_mix_post_g, m_ffn_pre_g, m_ffn_post_g, m_cm_w_in, m_cm_b_in, m_cm_dw, m_cm_dw_b,
                            m_cm_ln_g, m_cm_ln_b, m_cm_w_out, m_cm_b_out, m_kv_norm_g, m_w_kv, m_w_q, m_w_o, m_ffn_w_in,
                            m_ffn_dw, m_ffn_dw_b, m_ffn_w_out]))
    v_in = dict(zip(names, [v_mix_pre_g, v_mix_post_g, v_ffn_pre_g, v_ffn_post_g, v_cm_w_in, v_cm_b_in, v_cm_dw, v_cm_dw_b,
                            v_cm_ln_g, v_cm_ln_b, v_cm_w_out, v_cm_b_out, v_kv_norm_g, v_w_kv, v_w_q, v_w_o, v_ffn_w_in,
                            v_ffn_dw, v_ffn_dw_b, v_ffn_w_out]))

    bsz, seq, d = x.shape
    t = bsz * seq
    n_b = DEPTH - N_A
    hw = w_o.shape[-1]
    qw = N_GROUPS * hw
    f2 = ffn_dw_b.shape[-1]
    f = f2 // 2
    me_chip = _chip_of(lax.axis_index("x"), lax.axis_index("y"))

    big = ["cm_w_in", "cm_w_out", "w_kv", "w_q", "w_o", "ffn_w_in", "ffn_w_out"]
    row_sharded = ("cm_w_out", "w_o", "ffn_w_out")
    small_sharded = ["cm_b_in", "cm_dw", "cm_dw_b", "cm_ln_g", "cm_ln_b", "cm_b_out", "ffn_dw"]
    small_pack = _pack([w_in[n] for n in small_sharded])
    chunks = [
        [("cm_w_in", 0), ("small", None)],
        [("cm_w_out", 0)],
        [("ffn_w_in", 0), ("ffn_w_out", 0)],
        [("cm_w_in", 1), ("cm_w_out", 1)],
        [("ffn_w_in", 1), ("ffn_w_out", 1)],
        [("w_kv", None)],
        [("w_q", 0), ("w_o", 0)],
        [("ffn_w_in", 2), ("ffn_w_out", 2)],
        [("w_q", 1), ("w_o", 1)],
        [("ffn_w_in", 3), ("ffn_w_out", 3)],
    ]
    pieces = [pc for ch in chunks for pc in ch]
    chunk_of = {pc: ck for ck, ch in enumerate(chunks) for pc in ch}

    me_arr = me_chip.astype(jnp.int32).reshape(1)

    def land_of(pc):
        n, l = pc
        if n == "small":
            return cast_place(small_pack, None, me_arr, F32, name="place_small")
        return cast_place(w_in[n], l, me_arr, BF16, name=f"place_{n}_{l}")

    lands = [land_of(pc) for pc in pieces]
    g_send, g_recv, lands_f, token = gather_start(lands, [len(ch) for ch in chunks])
    weights = {}

    def finish_chunk(ck, after):
        lo = sum(len(ch) for ch in chunks[:ck])
        hi = lo + len(chunks[ck])
        got = gather_wait(g_send[ck], g_recv[ck], lands_f[lo:hi], after, name=f"gather_wait{ck}")
        for pc, arr in zip(chunks[ck], got):
            weights[pc] = arr.reshape(1, -1, arr.shape[-1]) if pc[0] in row_sharded else arr

    def wmat(n, l=None, after=None):
        if (n, l) not in weights:
            finish_chunk(chunk_of[(n, l)], after)
        arr = weights[(n, l)]
        return arr, arr.shape[0]

    finish_chunk(0, token)
    small_full = {}
    for n, arr4 in zip(small_sharded, _unpack(weights[("small", None)], [w_in[n].shape for n in small_sharded], lead=(N_CHIPS,))):
        shp = w_in[n].shape
        small_full[n] = jnp.moveaxis(arr4, 0, -2).reshape(*shp[:-1], N_CHIPS * shp[-1])

    x2d = x.reshape(t, d)
    saved = []
    (h1,) = resid_norm_fwd(x2d, None, None, [_row(mix_pre_g[0])], name="norm_in", dep=token)
    xcur = x2d
    kv_state = None
    for i in range(DEPTH):
        sv = {"x_in": xcur, "h1": h1}
        if i < N_A:
            z = mm_nn(h1, *wmat("cm_w_in", i, h1), 1, 0, bias=_row(small_full["cm_b_in"][i]), name=f"cm_in{i}")
            u2 = glu_conv_fwd(z.reshape(bsz, seq, 2 * d), small_full["cm_dw"][i], _row(small_full["cm_dw_b"][i]),
                              name=f"glu_conv{i}").reshape(t, d)
            u4 = ln_silu_fwd(u2, _row(small_full["cm_ln_g"][i]), _row(small_full["cm_ln_b"][i]), name=f"ln_silu{i}")
            y = mm_nn(u4, *wmat("cm_w_out", i, u4), 1, 0, bias=_row(small_full["cm_b_out"][i]), name=f"cm_out{i}")
            sv.update(z=z, u2=u2, u4=u4)
        else:
            j = i - N_A
            q = mm_nn(h1, *wmat("w_q", j, h1), 1, 0, name=f"q_proj{j}").reshape(bsz, seq, qw)
            outs, lses, q_subs = [], [], []
            for g, dil in enumerate(DILATIONS):
                nblk = seq // dil // BLK
                if dil == 1:
                    o_s, l_s = attn_fwd(q, g, kv_state["kv"], g, kv_state["kv"], N_GROUPS + g, nblk, hw, name=f"attn_fwd{j}_{g}")
                    q_subs.append(None)
                    outs.append(o_s.reshape(t, hw))
                    lses.append(l_s.reshape(t, LANES))
                else:
                    q_s = perm_rows(q, g, hw, dil, False, BF16, name=f"q_sub{j}_{g}")
                    o_s, l_s = attn_fwd(q_s, 0, kv_state["k_sub"][g], 0, kv_state["v_sub"][g], 0, nblk, hw, name=f"attn_fwd{j}_{g}")
                    q_subs.append(q_s)
                    outs.append(perm_rows(o_s, 0, hw, dil, True, F32, name=f"o_tok{j}_{g}").reshape(t, hw))
                    lses.append(perm_rows(l_s, 0, LANES, dil, True, F32, name=f"lse_tok{j}_{g}").reshape(t, LANES))
            merged, lsej = attn_merge(outs, lses, name=f"attn_merge{j}")
            y = mm_nn(merged, *wmat("w_o", j, merged), 1, 0, name=f"o_proj{j}")
            sv.update(q=q, q_subs=q_subs, merged=merged, lsej=lsej)
        x1, h2 = resid_norm_fwd(xcur, y, _row(mix_post_g[i]), [_row(ffn_pre_g[i])], name=f"resid_mix{i}")
        p = mm_nn(h2, *wmat("ffn_w_in", i, h2), 1, 0, name=f"ffn_in{i}")
        s_act = ffn_mid_fwd(p.reshape(bsz, seq, f2), small_full["ffn_dw"][i], _row(ffn_dw_b[i]), name=f"ffn_mid{i}").reshape(t, f)
        y2 = mm_nn(s_act, *wmat("ffn_w_out", i), 1, 0, name=f"ffn_out{i}")
        next_gains = []
        if i + 1 < DEPTH:
            next_gains.append(_row(mix_pre_g[i + 1]))
        if i == N_A - 1:
            next_gains.append(_row(kv_norm_g))
        res = resid_norm_fwd(x1, y2, _row(ffn_post_g[i]), next_gains, name=f"resid_ffn{i}")
        sv.update(y=y, x1=x1, h2=h2, p=p, s=s_act, y2=y2)
        saved.append(sv)
        xcur = res[0]
        if i + 1 < DEPTH:
            h1 = res[1]
        if i == N_A - 1:
            kvn = res[2]
            kv = mm_nn(kvn, *wmat("w_kv", None, kvn), 1, 0, name="kv_proj").reshape(bsz, seq, 2 * qw)
            k_sub, v_sub = {}, {}
            for g, dil in enumerate(DILATIONS):
                if dil > 1:
                    k_sub[g] = perm_rows(kv, g, hw, dil, False, BF16, name=f"k_sub{g}")
                    v_sub[g] = perm_rows(kv, N_GROUPS + g, hw, dil, False, BF16, name=f"v_sub{g}")
            kv_state = {"kv": kv, "k_sub": k_sub, "v_sub": v_sub, "kvn": kvn, "x_a": xcur}

    dx, loss_tile = loss_fwd_bwd(xcur, loss_target.reshape(t, d))
    loss = lax.psum(loss_tile[0, 0], ("x", "y", "c"))

    gsm = {n: [None] * w_in[n].shape[0] for n in
           ["mix_pre_g", "mix_post_g", "ffn_pre_g", "ffn_post_g", "cm_b_in", "cm_dw", "cm_dw_b", "cm_ln_g", "cm_ln_b",
            "cm_b_out", "ffn_dw", "ffn_dw_b"]}
    gbig = {}
    in_flight = []
    dep = None

    def start_scatter(pcs, tag):
        ssem, rsem, g_f, r_f, tok = scatter_start([gbig[pc] for pc in pcs], name=f"scatter_start_{tag}")
        in_flight.append((pcs, ssem, rsem, g_f, r_f))
        return tok

    dk_acc = {g: None for g in range(N_GROUPS)}
    dv_acc = {g: None for g in range(N_GROUPS)}
    for i in range(DEPTH - 1, -1, -1):
        sv = saved[i]
        dy2, dg, _ = norm_bwd(sv["y2"], _row(ffn_post_g[i]), dx, out_dtype=BF16, name=f"bwd_ffn_post{i}", dep=dep)
        gsm["ffn_post_g"][i] = dg
        ds = mm_nt(dy2, *wmat("ffn_w_out", i), 1, 0, name=f"bwd_ffn_out_dx{i}")
        gbig[("ffn_w_out", i)] = mm_tn(sv["s"], dy2, 1, name=f"bwd_ffn_out_dw{i}").reshape(N_CHIPS, f // N_CHIPS, d)
        dpa, dpg, ddwa, ddwg, ddba, ddbg = ffn_mid_bwd(sv["p"].reshape(bsz, seq, f2), small_full["ffn_dw"][i], _row(ffn_dw_b[i]),
                                                       ds.reshape(bsz, seq, f), name=f"bwd_ffn_mid{i}")
        gsm["ffn_dw"][i] = jnp.concatenate([jnp.sum(ddwa, axis=1), jnp.sum(ddwg, axis=1)], axis=-1)
        gsm["ffn_dw_b"][i] = jnp.concatenate([ddba, ddbg], axis=-1)
        dp = [dpa.reshape(t, f), dpg.reshape(t, f)]
        dh2 = mm_nt(dp, *wmat("ffn_w_in", i), 1, 0, name=f"bwd_ffn_in_dx{i}")
        gbig[("ffn_w_in", i)] = mm_tn(sv["h2"], dp, N_CHIPS, name=f"bwd_ffn_in_dw{i}")
        dx1, dg, _ = norm_bwd(sv["x1"], _row(ffn_pre_g[i]), dh2, add=dx, name=f"bwd_ffn_pre{i}")
        gsm["ffn_pre_g"][i] = dg
        dep = start_scatter([("ffn_w_in", 0), ("ffn_w_out", 0)], "ffn0") if i == 0 else None
        dy, dg, dbias = norm_bwd(sv["y"], _row(mix_post_g[i]), dx1, out_dtype=BF16, name=f"bwd_mix_post{i}", dep=dep)
        gsm["mix_post_g"][i] = dg
        if i < N_A:
            gsm["cm_b_out"][i] = dbias
            du4 = mm_nt(dy, *wmat("cm_w_out", i), 1, 0, name=f"bwd_cm_out_dx{i}")
            gbig[("cm_w_out", i)] = mm_tn(sv["u4"], dy, 1, name=f"bwd_cm_out_dw{i}").reshape(N_CHIPS, d // N_CHIPS, d)
            du2, dlg, dlb = ln_silu_bwd(sv["u2"], _row(small_full["cm_ln_g"][i]), _row(small_full["cm_ln_b"][i]), du4,
                                        name=f"bwd_ln_silu{i}")
            gsm["cm_ln_g"][i], gsm["cm_ln_b"][i] = dlg, dlb
            dza, dzg, ddw, ddwb, dba, dbg = glu_conv_bwd(sv["z"].reshape(bsz, seq, 2 * d), small_full["cm_dw"][i],
                                                         du2.reshape(bsz, seq, d), name=f"bwd_glu_conv{i}")
            gsm["cm_dw"][i] = jnp.sum(ddw, axis=1)
            gsm["cm_dw_b"][i] = ddwb
            gsm["cm_b_in"][i] = jnp.concatenate([dba, dbg], axis=-1)
            dz = [dza.reshape(t, d), dzg.reshape(t, d)]
            dh1 = mm_nt(dz, *wmat("cm_w_in", i), 1, 0, name=f"bwd_cm_in_dx{i}")
            gbig[("cm_w_in", i)] = mm_tn(sv["h1"], dz, N_CHIPS, name=f"bwd_cm_in_dw{i}")
        else:
            j = i - N_A
            dmerged = mm_nt(dy, *wmat("w_o", j), 1, 0, name=f"bwd_o_proj_dx{j}")
            gbig[("w_o", j)] = mm_tn(sv["merged"], dy, 1, name=f"bwd_o_proj_dw{j}").reshape(N_CHIPS, hw // N_CHIPS, d)
            dmt = attn_bwd_prep(dmerged, sv["merged"], name=f"bwd_attn_prep{j}")
            dq_parts = []
            for g, dil in enumerate(DILATIONS):
                nblk = seq // dil // BLK
                do3 = dmerged.reshape(bsz, seq, hw)
                lj3 = sv["lsej"].reshape(bsz, seq, LANES)
                dm3 = dmt.reshape(bsz, seq, LANES)
                if dil == 1:
                    dq_g, dk_g, dv_g = attn_bwd(sv["q"], g, kv_state["kv"], g, kv_state["kv"], N_GROUPS + g, do3, lj3, dm3,
                                                dk_acc[g], dv_acc[g], nblk, hw, name=f"attn_bwd{j}_{g}")
                    dq_parts.append(dq_g.reshape(t, hw))
                else:
                    do_s = perm_rows(do3, 0, hw, dil, False, BF16, name=f"do_sub{j}_{g}")
                    lj_s = perm_rows(lj3, 0, LANES, dil, False, F32, name=f"lsej_sub{j}_{g}", tc=LANES)
                    dm_s = perm_rows(dm3, 0, LANES, dil, False, F32, name=f"dm_sub{j}_{g}", tc=LANES)
                    dq_g, dk_g, dv_g = attn_bwd(sv["q_subs"][g], 0, kv_state["k_sub"][g], 0, kv_state["v_sub"][g], 0, do_s,
                                                lj_s, dm_s, dk_acc[g], dv_acc[g], nblk, hw, name=f"attn_bwd{j}_{g}")
                    dq_parts.append(perm_rows(dq_g, 0, hw, dil, True, F32, name=f"dq_tok{j}_{g}").reshape(t, hw))
                dk_acc[g], dv_acc[g] = dk_g, dv_g
            dq = jnp.concatenate(dq_parts, axis=-1).astype(BF16)
            dh1 = mm_nt(dq, *wmat("w_q", j), 1, 0, name=f"bwd_q_proj_dx{j}")
            gbig[("w_q", j)] = mm_tn(sv["h1"], dq, N_CHIPS, name=f"bwd_q_proj_dw{j}")
        dx, dg, _ = norm_bwd(sv["x_in"], _row(mix_pre_g[i]), dh1, add=dx1, name=f"bwd_mix_pre{i}")
        gsm["mix_pre_g"][i] = dg
        if i > N_A:
            dep = start_scatter([("ffn_w_in", i), ("ffn_w_out", i), ("w_q", i - N_A), ("w_o", i - N_A)], f"l{i}")
        elif 0 < i < N_A:
            dep = start_scatter([("ffn_w_in", i), ("ffn_w_out", i), ("cm_w_in", i), ("cm_w_out", i)], f"l{i}")
        elif i == 0:
            last_token = start_scatter([("cm_w_in", 0), ("cm_w_out", 0)], "cm0")
        if i == N_A:
            dkv_parts = []
            for acc in (dk_acc, dv_acc):
                for g, dil in enumerate(DILATIONS):
                    part = acc[g]
                    if dil > 1:
                        part = perm_rows(part, 0, hw, dil, True, F32, name=f"dkv_tok{len(dkv_parts)}")
                    dkv_parts.append(part.reshape(t, hw))
            dkv = jnp.concatenate(dkv_parts, axis=-1).astype(BF16)
            dkvn = mm_nt(dkv, *wmat("w_kv"), 1, 0, name="bwd_kv_proj_dx")
            gbig[("w_kv", None)] = mm_tn(kv_state["kvn"], dkv, N_CHIPS, name="bwd_kv_proj_dw")
            dx, dg_kv, _ = norm_bwd(kv_state["x_a"], _row(kv_norm_g), dkvn, add=dx, name="bwd_kv_norm")
            dep = start_scatter([("ffn_w_in", i), ("ffn_w_out", i), ("w_q", 0), ("w_o", 0), ("w_kv", None)], f"l{i}")
    grad_x = dx.reshape(bsz, seq, d)

    plane_of = {}
    outs_g, outs_d, outs_m, outs_v = {}, {}, {}, {}

    def finish_scatter(k, after):
        pcs, ssem, rsem, g_f, r_f = in_flight[k]
        g_done, r_done = scatter_wait(ssem, rsem, g_f, r_f, after, name=f"scatter_wait{k}")
        for pc, g_arr, r_arr in zip(pcs, g_done, r_done):
            plane_of[pc] = sum_parts(g_arr, r_arr, me_arr, name=f"sum_chips_{pc[0]}_{pc[1]}")

    def update(group, tag):
        plane = []
        for n in group:
            if w_in[n].ndim == 2:
                plane.append(plane_of[(n, None)])
            else:
                plane.append(jnp.concatenate([plane_of[(n, l)] for l in range(w_in[n].shape[0])], axis=0))
        other = swap_with_sibling(plane, name=f"swap_with_sibling_{tag}")
        for n, p_mine, p_other in zip(group, plane, other):
            shp = w_in[n].shape
            flat = lambda a: a.reshape(-1, shp[-1])
            g_, d_, m_, v_ = adamw(flat(w_in[n]), flat(m_in[n]), flat(v_in[n]), [p_mine, p_other], name=f"adamw_{n}")
            outs_g[n], outs_d[n], outs_m[n], outs_v[n] = (a.reshape(shp) for a in (g_, d_, m_, v_))

    for k in range(len(in_flight) - 1):
        finish_scatter(k, last_token)
    update(["w_kv", "w_q", "w_o", "ffn_w_in", "ffn_w_out"], "a")
    finish_scatter(len(in_flight) - 1, outs_v["ffn_w_out"])
    update(["cm_w_in", "cm_w_out"], "b")

    small_names = [n for n in names if n not in big]
    small_shapes_full = {}
    small_grads_full = []
    for n in small_names:
        if n == "kv_norm_g":
            gfull = dg_kv.reshape(-1)
        elif n in ("cm_dw", "ffn_dw"):
            gfull = jnp.stack(gsm[n], axis=0)
        else:
            gfull = jnp.stack([a.reshape(-1) for a in gsm[n]], axis=0)
        small_shapes_full[n] = gfull.shape
        small_grads_full.append(gfull)
    summed = allreduce_small(_pack(small_grads_full))
    g_full = dict(zip(small_names, _unpack(summed, [small_shapes_full[n] for n in small_names])))
    g_loc = {}
    for n in small_names:
        if n in small_sharded:
            width = w_in[n].shape[-1]
            g_loc[n] = lax.dynamic_slice_in_dim(g_full[n], me_chip * width, width, axis=g_full[n].ndim - 1)
        else:
            g_loc[n] = g_full[n]
    res = adamw(_pack([w_in[n] for n in small_names]), _pack([m_in[n] for n in small_names]),
                _pack([v_in[n] for n in small_names]), [_pack([g_loc[n] for n in small_names])], name="adamw_small")
    shapes_loc = [w_in[n].shape for n in small_names]
    for dst, packed in zip((outs_g, outs_d, outs_m, outs_v), res):
        for n, a in zip(small_names, _unpack(packed, shapes_loc)):
            dst[n] = a

    return (loss, grad_x, *[outs_g[n] for n in names], *[outs_d[n] for n in names],
            *[outs_m[n] for n in names], *[outs_v[n] for n in names])
```

```python
import functools
import math

import jax
import jax.numpy as jnp
from jax import lax
from jax.experimental import pallas as pl
from jax.experimental.pallas import tpu as pltpu

F32 = jnp.float32
BF16 = jnp.bfloat16
EPS = 1e-6
NEG_INF = -1e30
N_A = 2
DEPTH = 4
N_GROUPS = 3
DILATIONS = (1, 4, 16)
HEAD_DIM = 128
BLK = 128
LANES = 128
N_CHIPS = 4
N_DEV = 8
VMEM_LIMIT_V7X = 56 * 1024 * 1024

ADAM_LR = 0.001
ADAM_B1 = 0.9
ADAM_B2 = 0.999
ADAM_EPS = 1e-08
ADAM_WD = 0.01
ADAM_STEP = 10

MESH = pl.DeviceIdType.MESH


def _cp(*sem):
    return pltpu.CompilerParams(dimension_semantics=sem if sem else None, vmem_limit_bytes=VMEM_LIMIT_V7X)


def _dot(a, b):
    return jnp.dot(a, b, preferred_element_type=F32)


def _dot_nt(a, b):
    return lax.dot_general(a, b, (((1,), (1,)), ((), ())), preferred_element_type=F32)


def _dot_tn(a, b):
    return lax.dot_general(a, b, (((0,), (0,)), ((), ())), preferred_element_type=F32)


def _sigmoid(x):
    return 1.0 / (1.0 + jnp.exp(-x))


def _row_tile(n, want):
    if n <= want:
        return n
    for t in range(want - want % 8, 7, -8):
        if n % t == 0:
            return t
    raise ValueError(f"no row tile for {n} rows")


def mm_nn(a, w, nsh, stride, layer, bias=None, out_dtype=F32, name="mm_nn", tm=512):
    m, k = a.shape
    _, k2, ns = w.shape
    assert k == k2
    tm = _row_tile(m, tm)
    has_bias = bias is not None

    def body(*refs):
        if has_bias:
            a_ref, w_ref, b_ref, o_ref = refs
        else:
            a_ref, w_ref, o_ref = refs
        acc = _dot(a_ref[...].astype(BF16), w_ref[...])
        if has_bias:
            acc = acc + b_ref[...]
        o_ref[...] = acc.astype(out_dtype)

    in_specs = [
        pl.BlockSpec((tm, k), lambda j, i: (i, 0)),
        pl.BlockSpec((None, k, ns), lambda j, i: (j * stride + layer, 0, 0)),
    ]
    args = [a, w]
    if has_bias:
        in_specs.append(pl.BlockSpec((1, ns), lambda j, i: (0, j)))
        args.append(bias)
    return pl.pallas_call(
        body,
        name=name,
        grid=(nsh, m // tm),
        in_specs=in_specs,
        out_specs=pl.BlockSpec((tm, ns), lambda j, i: (i, j)),
        out_shape=jax.ShapeDtypeStruct((m, nsh * ns), out_dtype),
        compiler_params=_cp("parallel", "parallel"),
    )(*args)


def mm_nt(dy, w, nsh, stride, layer, out_dtype=F32, name="mm_nt", tm=512):
    dys = list(dy) if isinstance(dy, (list, tuple)) else [dy]
    npart = len(dys)
    per = nsh // npart
    m = dys[0].shape[0]
    _, k, ns = w.shape
    assert all(d.shape == (m, per * ns) for d in dys)
    tm = _row_tile(m, tm)

    def body(*refs):
        dy_refs = refs[:npart]
        w_ref, o_ref, acc_ref = refs[npart:]
        j = pl.program_id(1)

        @pl.when(j == 0)
        def _():
            acc_ref[...] = jnp.zeros(acc_ref.shape, F32)

        for pi in range(npart):
            @pl.when(j // per == pi)
            def _(pi=pi):
                acc_ref[...] += _dot_nt(dy_refs[pi][...].astype(BF16), w_ref[...])

        @pl.when(j == nsh - 1)
        def _():
            o_ref[...] = acc_ref[...].astype(out_dtype)

    dy_specs = [pl.BlockSpec((tm, ns), lambda i, j, pi=pi: (i, jnp.clip(j - pi * per, 0, per - 1))) for pi in range(npart)]
    return pl.pallas_call(
        body,
        name=name,
        grid=(m // tm, nsh),
        in_specs=[*dy_specs, pl.BlockSpec((None, k, ns), lambda i, j: (j * stride + layer, 0, 0))],
        out_specs=pl.BlockSpec((tm, k), lambda i, j: (i, 0)),
        out_shape=jax.ShapeDtypeStruct((m, k), out_dtype),
        scratch_shapes=[pltpu.VMEM((tm, k), F32)],
        compiler_params=_cp("parallel", "arbitrary"),
    )(*dys, w)


def mm_tn(a, dy, nsh, name="mm_tn", tm=512):
    dys = list(dy) if isinstance(dy, (list, tuple)) else [dy]
    npart = len(dys)
    per = nsh // npart
    m, k = a.shape
    ns = dys[0].shape[1] // per
    assert all(d.shape == (m, per * ns) for d in dys)
    tm = _row_tile(m, tm)
    nt = m // tm

    def body(*refs):
        a_ref = refs[0]
        dy_refs = refs[1:1 + npart]
        o_ref, acc_ref = refs[1 + npart:]
        j = pl.program_id(0)
        i = pl.program_id(1)

        @pl.when(i == 0)
        def _():
            acc_ref[...] = jnp.zeros(acc_ref.shape, F32)

        for pi in range(npart):
            @pl.when(j // per == pi)
            def _(pi=pi):
                acc_ref[...] += _dot_tn(a_ref[...].astype(BF16), dy_refs[pi][...].astype(BF16))

        @pl.when(i == nt - 1)
        def _():
            o_ref[...] = acc_ref[...].astype(BF16)

    dy_specs = [
        pl.BlockSpec((tm, ns), lambda j, i, pi=pi: (jnp.where(j // per == pi, i, 0), jnp.clip(j - pi * per, 0, per - 1)))
        for pi in range(npart)
    ]
    return pl.pallas_call(
        body,
        name=name,
        grid=(nsh, nt),
        in_specs=[pl.BlockSpec((tm, k), lambda j, i: (i, 0)), *dy_specs],
        out_specs=pl.BlockSpec((None, k, ns), lambda j, i: (j, 0, 0)),
        out_shape=jax.ShapeDtypeStruct((nsh, k, ns), BF16),
        scratch_shapes=[pltpu.VMEM((k, ns), F32)],
        compiler_params=_cp("parallel", "arbitrary"),
    )(a, *dys)


DEP_SPEC_SHAPE = (8, LANES)


def resid_norm_fwd(x, y, g_post, next_gains, name, tm=256, dep=None):
    t, d = x.shape
    tm = _row_tile(t, tm)
    has_y = y is not None
    n_next = len(next_gains)
    n_dep = 0 if dep is None else 1

    def body(*refs):
        x_ref = refs[0]
        pos = 1
        if has_y:
            y_ref, gp_ref = refs[1], refs[2]
            pos = 3
        gn_refs = refs[pos:pos + n_next]
        outs = refs[pos + n_next + n_dep:]
        xv = x_ref[...]
        o = 0
        if has_y:
            yv = y_ref[...]
            r = lax.rsqrt(jnp.mean(yv * yv, axis=-1, keepdims=True) + EPS)
            xv = xv + (yv * r) * gp_ref[...]
            outs[0][...] = xv
            o = 1
        if n_next:
            xn = xv * lax.rsqrt(jnp.mean(xv * xv, axis=-1, keepdims=True) + EPS)
            for k in range(n_next):
                outs[o + k][...] = (xn * gn_refs[k][...]).astype(BF16)

    row = pl.BlockSpec((tm, d), lambda i: (i, 0))
    vec = pl.BlockSpec((1, d), lambda i: (0, 0))
    args, in_specs = [x], [row]
    if has_y:
        args += [y, g_post]
        in_specs += [row, vec]
    args += list(next_gains)
    in_specs += [vec] * n_next
    if n_dep:
        args.append(dep)
        in_specs.append(pl.BlockSpec(DEP_SPEC_SHAPE, lambda i: (0, 0)))
    out_shape, out_specs = [], []
    if has_y:
        out_shape.append(jax.ShapeDtypeStruct((t, d), F32))
        out_specs.append(row)
    for _ in range(n_next):
        out_shape.append(jax.ShapeDtypeStruct((t, d), BF16))
        out_specs.append(row)
    return pl.pallas_call(
        body, name=name, grid=(t // tm,), in_specs=in_specs, out_specs=out_specs, out_shape=out_shape,
        compiler_params=_cp("parallel"),
    )(*args)


def norm_bwd(x, g, dy, add=None, out_dtype=F32, name="norm_bwd", tm=256, dep=None):
    t, d = x.shape
    tm = _row_tile(t, tm)
    has_add = add is not None

    def body(*refs):
        x_ref, g_ref, dy_ref = refs[:3]
        add_ref = refs[3] if has_add else None
        dx_ref, dg_ref, cs_ref = refs[-3:]
        i = pl.program_id(0)
        xv = x_ref[...]
        dyv = dy_ref[...].astype(F32)
        r = lax.rsqrt(jnp.mean(xv * xv, axis=-1, keepdims=True) + EPS)
        gd = dyv * g_ref[...]
        dx = r * gd - xv * ((r * r * r) * jnp.mean(xv * gd, axis=-1, keepdims=True))
        if has_add:
            dx = dx + add_ref[...]
        dx_ref[...] = dx.astype(out_dtype)
        dg = jnp.sum(dyv * (xv * r), axis=0, keepdims=True)
        cs = jnp.sum(dx, axis=0, keepdims=True)

        @pl.when(i == 0)
        def _():
            dg_ref[...] = dg
            cs_ref[...] = cs

        @pl.when(i > 0)
        def _():
            dg_ref[...] += dg
            cs_ref[...] += cs

    row = pl.BlockSpec((tm, d), lambda i: (i, 0))
    vec = pl.BlockSpec((1, d), lambda i: (0, 0))
    args, in_specs = [x, g, dy], [row, vec, row]
    if has_add:
        args.append(add)
        in_specs.append(row)
    if dep is not None:
        args.append(dep)
        in_specs.append(pl.BlockSpec(DEP_SPEC_SHAPE, lambda i: (0, 0)))
    return pl.pallas_call(
        body, name=name, grid=(t // tm,), in_specs=in_specs,
        out_specs=[row, vec, vec],
        out_shape=[jax.ShapeDtypeStruct((t, d), out_dtype), jax.ShapeDtypeStruct((1, d), F32),
                   jax.ShapeDtypeStruct((1, d), F32)],
        compiler_params=_cp("arbitrary"),
    )(*args)


def loss_fwd_bwd(x, target, name="loss", tm=256):
    t, d = x.shape
    tm = _row_tile(t, tm)

    def body(x_ref, t_ref, dx_ref, l_ref):
        i = pl.program_id(0)
        err = x_ref[...] - t_ref[...]
        dx_ref[...] = err * (1.0 / d)
        part = 0.5 * jnp.sum(jnp.mean(err * err, axis=-1, keepdims=True), axis=0, keepdims=True)
        part = jnp.broadcast_to(part, l_ref.shape)

        @pl.when(i == 0)
        def _():
            l_ref[...] = part

        @pl.when(i > 0)
        def _():
            l_ref[...] += part

    row = pl.BlockSpec((tm, d), lambda i: (i, 0))
    return pl.pallas_call(
        body, name=name, grid=(t // tm,), in_specs=[row, row],
        out_specs=[row, pl.BlockSpec((8, LANES), lambda i: (0, 0))],
        out_shape=[jax.ShapeDtypeStruct((t, d), F32), jax.ShapeDtypeStruct((8, LANES), F32)],
        compiler_params=_cp("arbitrary"),
    )(x, target)


CONV_HALO = 32
CONV_CHUNK = 128


def glu_conv_fwd(z, dw, dwb, name, tc=128):
    b, s, c2 = z.shape
    c = c2 // 2
    kw = dw.shape[0]
    tc = min(tc, c)
    nc = c // tc
    ch = min(CONV_CHUNK, s)
    halo = CONV_HALO
    assert kw - 1 <= halo and s % ch == 0

    nch = s // ch

    def body(a_ref, g_ref, w_ref, b_ref, o_ref, pad_ref):
        _fill_glu_slabs(a_ref, g_ref, pad_ref, nch, ch, halo)

        def chunk(ci, carry):
            r0 = pl.multiple_of(ci * ch, ch)
            acc = b_ref[...]
            for k, tap in enumerate(_taps_front(pad_ref, ci, kw, ch, halo)):
                acc = acc + w_ref[k:k + 1, :] * tap
            o_ref[pl.ds(r0, ch), :] = acc
            return carry

        lax.fori_loop(0, nch, chunk, 0)

    return pl.pallas_call(
        body, name=name, grid=(b, nc),
        in_specs=[
            pl.BlockSpec((None, s, tc), lambda bi, i: (bi, 0, i)),
            pl.BlockSpec((None, s, tc), lambda bi, i: (bi, 0, i + nc)),
            pl.BlockSpec((kw, tc), lambda bi, i: (0, i)),
            pl.BlockSpec((1, tc), lambda bi, i: (0, i)),
        ],
        out_specs=pl.BlockSpec((None, s, tc), lambda bi, i: (bi, 0, i)),
        out_shape=jax.ShapeDtypeStruct((b, s, c), F32),
        scratch_shapes=[pltpu.VMEM((nch, ch + halo, tc), F32)],
        compiler_params=_cp("parallel", "parallel"),
    )(z, z, dw, dwb)


def glu_conv_bwd(z, dw, du2, name, tc=128):
    b, s, c2 = z.shape
    c = c2 // 2
    kw = dw.shape[0]
    tc = min(tc, c)
    nc = c // tc
    ch = min(CONV_CHUNK, s)
    nch = s // ch
    halo = CONV_HALO

    def body(a_ref, g_ref, w_ref, du_ref, dza_ref, dzg_ref, ddw_ref, ddwb_ref, dba_ref, dbg_ref, upad_ref, dpad_ref):
        bi = pl.program_id(1)

        @pl.when(bi == 0)
        def _():
            ddw_ref[...] = jnp.zeros(ddw_ref.shape, F32)
            ddwb_ref[...] = jnp.zeros(ddwb_ref.shape, F32)
            dba_ref[...] = jnp.zeros(dba_ref.shape, F32)
            dbg_ref[...] = jnp.zeros(dbg_ref.shape, F32)

        _fill_glu_slabs(a_ref, g_ref, upad_ref, nch, ch, halo)
        dpad_ref[nch - 1, ch:ch + halo, :] = jnp.zeros((halo, tc), F32)
        dpad_ref[nch - 1, 0:ch, :] = du_ref[s - ch:s, :]

        def fill(ci, carry):
            r0 = pl.multiple_of(ci * ch, ch)
            dpad_ref[ci, :, :] = du_ref[pl.ds(r0, ch + halo), :]
            return carry

        lax.fori_loop(0, nch - 1, fill, 0)

        def chunk(ci, carry):
            r0 = pl.multiple_of(ci * ch, ch)
            du_c = du_ref[pl.ds(r0, ch), :]
            taps_u = _taps_front(upad_ref, ci, kw, ch, halo)
            du1 = w_ref[kw - 1:kw, :] * du_c
            ddw_ref[kw - 1] += jnp.sum((du_c * taps_u[kw - 1]).reshape(ch // 8, 8, tc), axis=0)
            for j in range(1, kw):
                du1 = du1 + w_ref[kw - 1 - j:kw - j, :] * dpad_ref[ci, j:j + ch, :]
                ddw_ref[kw - 1 - j] += jnp.sum((du_c * taps_u[kw - 1 - j]).reshape(ch // 8, 8, tc), axis=0)
            av = a_ref[pl.ds(r0, ch), :]
            sg = _sigmoid(g_ref[pl.ds(r0, ch), :])
            dza = du1 * sg
            dzg = du1 * av * (sg * (1.0 - sg))
            dza_ref[pl.ds(r0, ch), :] = dza.astype(BF16)
            dzg_ref[pl.ds(r0, ch), :] = dzg.astype(BF16)
            dba_ref[...] += jnp.sum(dza, axis=0, keepdims=True)
            dbg_ref[...] += jnp.sum(dzg, axis=0, keepdims=True)
            ddwb_ref[...] += jnp.sum(du_c, axis=0, keepdims=True)
            return carry

        lax.fori_loop(0, s // ch, chunk, 0)

    blk = lambda off: pl.BlockSpec((None, s, tc), lambda i, bi: (bi, 0, i + off))
    vec = pl.BlockSpec((1, tc), lambda i, bi: (0, i))
    return pl.pallas_call(
        body, name=name, grid=(nc, b),
        in_specs=[blk(0), blk(nc), pl.BlockSpec((kw, tc), lambda i, bi: (0, i)), blk(0)],
        out_specs=[blk(0), blk(0), pl.BlockSpec((kw, 8, tc), lambda i, bi: (0, 0, i)), vec, vec, vec],
        out_shape=[
            jax.ShapeDtypeStruct((b, s, c), BF16), jax.ShapeDtypeStruct((b, s, c), BF16),
            jax.ShapeDtypeStruct((kw, 8, c), F32), jax.ShapeDtypeStruct((1, c), F32),
            jax.ShapeDtypeStruct((1, c), F32), jax.ShapeDtypeStruct((1, c), F32),
        ],
        scratch_shapes=[pltpu.VMEM((nch, ch + halo, tc), F32), pltpu.VMEM((nch, ch + halo, tc), F32)],
        compiler_params=_cp("parallel", "arbitrary"),
    )(z, z, dw, du2)


def _fill_glu_slabs(a_ref, g_ref, pad_ref, nch, ch, halo):
    tc = a_ref.shape[-1]
    pad_ref[0, 0:halo, :] = jnp.zeros((halo, tc), F32)
    pad_ref[0, halo:halo + ch, :] = a_ref[0:ch, :] * _sigmoid(g_ref[0:ch, :])

    def fill(ci, carry):
        r0 = pl.multiple_of(ci * ch, ch)
        pad_ref[ci, 0:halo, :] = pad_ref[ci - 1, ch:ch + halo, :]
        pad_ref[ci, halo:halo + ch, :] = a_ref[pl.ds(r0, ch), :] * _sigmoid(g_ref[pl.ds(r0, ch), :])
        return carry

    lax.fori_loop(1, nch, fill, 0)


def ln_silu_fwd(u, g, bvec, name, tm=256):
    t, d = u.shape
    tm = _row_tile(t, tm)

    def body(u_ref, g_ref, b_ref, o_ref):
        uv = u_ref[...]
        mu = jnp.mean(uv, axis=-1, keepdims=True)
        xc = uv - mu
        var = jnp.mean(xc * xc, axis=-1, keepdims=True)
        v = (xc * lax.rsqrt(var + EPS)) * g_ref[...] + b_ref[...]
        o_ref[...] = (v * _sigmoid(v)).astype(BF16)

    row = pl.BlockSpec((tm, d), lambda i: (i, 0))
    vec = pl.BlockSpec((1, d), lambda i: (0, 0))
    return pl.pallas_call(
        body, name=name, grid=(t // tm,), in_specs=[row, vec, vec], out_specs=row,
        out_shape=jax.ShapeDtypeStruct((t, d), BF16), compiler_params=_cp("parallel"),
    )(u, g, bvec)


def ln_silu_bwd(u, g, bvec, dout, name, tm=256):
    t, d = u.shape
    tm = _row_tile(t, tm)

    def body(u_ref, g_ref, b_ref, do_ref, du_ref, dg_ref, db_ref):
        i = pl.program_id(0)
        uv = u_ref[...]
        mu = jnp.mean(uv, axis=-1, keepdims=True)
        xc = uv - mu
        var = jnp.mean(xc * xc, axis=-1, keepdims=True)
        rstd = lax.rsqrt(var + EPS)
        n = xc * rstd
        v = n * g_ref[...] + b_ref[...]
        sg = _sigmoid(v)
        dv = do_ref[...].astype(F32) * (sg * (1.0 + v * (1.0 - sg)))
        dn = dv * g_ref[...]
        du_ref[...] = rstd * (dn - jnp.mean(dn, axis=-1, keepdims=True) - n * jnp.mean(dn * n, axis=-1, keepdims=True))
        dg = jnp.sum(dv * n, axis=0, keepdims=True)
        db = jnp.sum(dv, axis=0, keepdims=True)

        @pl.when(i == 0)
        def _():
            dg_ref[...] = dg
            db_ref[...] = db

        @pl.when(i > 0)
        def _():
            dg_ref[...] += dg
            db_ref[...] += db

    row = pl.BlockSpec((tm, d), lambda i: (i, 0))
    vec = pl.BlockSpec((1, d), lambda i: (0, 0))
    return pl.pallas_call(
        body, name=name, grid=(t // tm,), in_specs=[row, vec, vec, row], out_specs=[row, vec, vec],
        out_shape=[jax.ShapeDtypeStruct((t, d), F32), jax.ShapeDtypeStruct((1, d), F32), jax.ShapeDtypeStruct((1, d), F32)],
        compiler_params=_cp("arbitrary"),
    )(u, g, bvec, dout)


FFN_HALO = 8


def _fill_front_halo(src_ref, pad_ref, nch, ch, halo):
    tc = src_ref.shape[-1]
    pad_ref[0, 0:halo, :] = jnp.zeros((halo, tc), F32)
    pad_ref[0, halo:halo + ch, :] = src_ref[0:ch, :].astype(F32)

    def fill(ci, carry):
        r0 = pl.multiple_of(ci * ch, ch)
        pad_ref[ci, 0:halo, :] = src_ref[pl.ds(r0 - halo, halo), :].astype(F32)
        pad_ref[ci, halo:halo + ch, :] = src_ref[pl.ds(r0, ch), :].astype(F32)
        return carry

    lax.fori_loop(1, nch, fill, 0)


def _taps_front(pad_ref, ci, kw, ch, halo):
    return [pad_ref[ci, halo - (kw - 1 - k):halo - (kw - 1 - k) + ch, :] for k in range(kw)]


def ffn_mid_fwd(p, dw, dwb, name, tc=256):
    b, s, f2 = p.shape
    f = f2 // 2
    kw = dw.shape[0]
    tc = min(tc, f)
    nf = f // tc
    ch = min(CONV_CHUNK, s)
    nch = s // ch
    halo = FFN_HALO

    def body(pa_ref, pg_ref, wa_ref, wg_ref, ba_ref, bg_ref, o_ref, apad_ref, gpad_ref):
        _fill_front_halo(pa_ref, apad_ref, nch, ch, halo)
        _fill_front_halo(pg_ref, gpad_ref, nch, ch, halo)

        def chunk(ci, carry):
            r0 = pl.multiple_of(ci * ch, ch)
            ca = ba_ref[...]
            cg = bg_ref[...]
            taps = zip(_taps_front(apad_ref, ci, kw, ch, halo), _taps_front(gpad_ref, ci, kw, ch, halo))
            for k, (ta, tg) in enumerate(taps):
                ca = ca + wa_ref[k:k + 1, :] * ta
                cg = cg + wg_ref[k:k + 1, :] * tg
            o_ref[pl.ds(r0, ch), :] = ((cg * _sigmoid(cg)) * ca).astype(BF16)
            return carry

        lax.fori_loop(0, nch, chunk, 0)

    blk = lambda off: pl.BlockSpec((None, s, tc), lambda bi, i: (bi, 0, i + off))
    wsp = lambda off: pl.BlockSpec((kw, tc), lambda bi, i: (0, i + off))
    bsp = lambda off: pl.BlockSpec((1, tc), lambda bi, i: (0, i + off))
    return pl.pallas_call(
        body, name=name, grid=(b, nf),
        in_specs=[blk(0), blk(nf), wsp(0), wsp(nf), bsp(0), bsp(nf)],
        out_specs=pl.BlockSpec((None, s, tc), lambda bi, i: (bi, 0, i)),
        out_shape=jax.ShapeDtypeStruct((b, s, f), BF16),
        scratch_shapes=[pltpu.VMEM((nch, ch + halo, tc), F32)] * 2,
        compiler_params=_cp("parallel", "parallel"),
    )(p, p, dw, dw, dwb, dwb)


def ffn_mid_bwd(p, dw, dwb, ds, name, tc=256):
    b, s, f2 = p.shape
    f = f2 // 2
    kw = dw.shape[0]
    tc = min(tc, f)
    nf = f // tc
    ch = min(CONV_CHUNK, s)
    nch = s // ch
    halo = FFN_HALO

    def sum8(v):
        return jnp.sum(v.reshape(ch // 8, 8, tc), axis=0)

    def body(pa_ref, pg_ref, wa_ref, wg_ref, ba_ref, bg_ref, ds_ref, dpa_ref, dpg_ref, ddwa_ref, ddwg_ref, dba_ref, dbg_ref,
             apad_ref, gpad_ref, dca_ref, dcg_ref):
        bi = pl.program_id(1)

        @pl.when(bi == 0)
        def _():
            ddwa_ref[...] = jnp.zeros(ddwa_ref.shape, F32)
            ddwg_ref[...] = jnp.zeros(ddwg_ref.shape, F32)
            dba_ref[...] = jnp.zeros(dba_ref.shape, F32)
            dbg_ref[...] = jnp.zeros(dbg_ref.shape, F32)

        _fill_front_halo(pa_ref, apad_ref, nch, ch, halo)
        _fill_front_halo(pg_ref, gpad_ref, nch, ch, halo)
        dca_ref[nch - 1, ch:ch + halo, :] = jnp.zeros((halo, tc), F32)
        dcg_ref[nch - 1, ch:ch + halo, :] = jnp.zeros((halo, tc), F32)

        def grads(ci, carry):
            acc_a, acc_g, sb_a, sb_g = carry
            r0 = pl.multiple_of(ci * ch, ch)
            taps_a = _taps_front(apad_ref, ci, kw, ch, halo)
            taps_g = _taps_front(gpad_ref, ci, kw, ch, halo)
            ca = ba_ref[...]
            cg = bg_ref[...]
            for k in range(kw):
                ca = ca + wa_ref[k:k + 1, :] * taps_a[k]
                cg = cg + wg_ref[k:k + 1, :] * taps_g[k]
            sg = _sigmoid(cg)
            dsv = ds_ref[pl.ds(r0, ch), :].astype(F32)
            dca = dsv * (cg * sg)
            dcg = dsv * ca * (sg * (1.0 + cg * (1.0 - sg)))
            dca_ref[ci, 0:ch, :] = dca
            dcg_ref[ci, 0:ch, :] = dcg

            prev = jnp.maximum(ci - 1, 0)

            @pl.when(ci > 0)
            def _():
                dca_ref[prev, ch:ch + halo, :] = dca[0:halo, :]
                dcg_ref[prev, ch:ch + halo, :] = dcg[0:halo, :]

            acc_a = tuple(acc_a[k] + sum8(dca * taps_a[k]) for k in range(kw))
            acc_g = tuple(acc_g[k] + sum8(dcg * taps_g[k]) for k in range(kw))
            return acc_a, acc_g, sb_a + sum8(dca), sb_g + sum8(dcg)

        z8 = jnp.zeros((8, tc), F32)
        acc_a, acc_g, sb_a, sb_g = lax.fori_loop(0, nch, grads, ((z8,) * kw, (z8,) * kw, z8, z8))
        for k in range(kw):
            ddwa_ref[k] += acc_a[k]
            ddwg_ref[k] += acc_g[k]
        dba_ref[...] += jnp.sum(sb_a, axis=0, keepdims=True)
        dbg_ref[...] += jnp.sum(sb_g, axis=0, keepdims=True)

        def back(ci, carry):
            r0 = pl.multiple_of(ci * ch, ch)
            da = wa_ref[kw - 1:kw, :] * dca_ref[ci, 0:ch, :]
            dg = wg_ref[kw - 1:kw, :] * dcg_ref[ci, 0:ch, :]
            for j in range(1, kw):
                da = da + wa_ref[kw - 1 - j:kw - j, :] * dca_ref[ci, j:j + ch, :]
                dg = dg + wg_ref[kw - 1 - j:kw - j, :] * dcg_ref[ci, j:j + ch, :]
            dpa_ref[pl.ds(r0, ch), :] = da.astype(BF16)
            dpg_ref[pl.ds(r0, ch), :] = dg.astype(BF16)
            return carry

        lax.fori_loop(0, nch, back, 0)

    blk = lambda off: pl.BlockSpec((None, s, tc), lambda i, bi: (bi, 0, i + off))
    wsp = lambda off: pl.BlockSpec((kw, tc), lambda i, bi: (0, i + off))
    bsp = lambda off: pl.BlockSpec((1, tc), lambda i, bi: (0, i + off))
    acc3 = pl.BlockSpec((kw, 8, tc), lambda i, bi: (0, 0, i))
    vec = pl.BlockSpec((1, tc), lambda i, bi: (0, i))
    return pl.pallas_call(
        body, name=name, grid=(nf, b),
        in_specs=[blk(0), blk(nf), wsp(0), wsp(nf), bsp(0), bsp(nf), blk(0)],
        out_specs=[blk(0), blk(0), acc3, acc3, vec, vec],
        out_shape=[jax.ShapeDtypeStruct((b, s, f), BF16), jax.ShapeDtypeStruct((b, s, f), BF16),
                   jax.ShapeDtypeStruct((kw, 8, f), F32), jax.ShapeDtypeStruct((kw, 8, f), F32),
                   jax.ShapeDtypeStruct((1, f), F32), jax.ShapeDtypeStruct((1, f), F32)],
        scratch_shapes=[pltpu.VMEM((nch, ch + halo, tc), F32)] * 4,
        compiler_params=_cp("parallel", "arbitrary"),
    )(p, p, dw, dw, dwb, dwb, ds)


def _tile_rows(r, n, dil):
    start = r + n * BLK * dil
    return pl.ds(start, BLK, stride=dil) if dil > 1 else pl.ds(start, BLK)


def _band_masks():
    qi = lax.broadcasted_iota(jnp.int32, (BLK, 2 * BLK), 0)
    kk = lax.broadcasted_iota(jnp.int32, (BLK, 2 * BLK), 1)
    both = jnp.logical_or(jnp.logical_and(kk < BLK, kk >= qi), jnp.logical_and(kk >= BLK, kk - BLK <= qi))
    return both, kk[:, :BLK] <= qi[:, :BLK]


def attn_fwd(q, kv, g, dil, hw, name):
    b, s, _ = q.shape
    nh = hw // HEAD_DIM
    nblk = s // dil // BLK
    scale = 1.0 / math.sqrt(HEAD_DIM)

    def body(q_ref, k_ref, v_ref, o_ref, lse_ref):
        mask2, mask1 = _band_masks()
        for r in range(dil):
            kp = vp = None
            for n in range(nblk):
                rs = _tile_rows(r, n, dil)
                qt = q_ref[rs, :].astype(BF16)
                kc = k_ref[rs, :].astype(BF16)
                vc = v_ref[rs, :].astype(BF16)
                if n == 0:
                    kcat, vcat, mask = kc, vc, mask1
                else:
                    kcat, vcat, mask = jnp.concatenate([kp, kc], axis=0), jnp.concatenate([vp, vc], axis=0), mask2
                sc = jnp.where(mask, _dot_nt(qt, kcat) * scale, NEG_INF)
                m = jnp.max(sc, axis=-1, keepdims=True)
                p = jnp.exp(sc - m)
                den = jnp.sum(p, axis=-1, keepdims=True)
                o_ref[rs, :] = _dot(p.astype(BF16), vcat) / den
                lse_ref[rs, :] = jnp.broadcast_to(m + jnp.log(den), (BLK, HEAD_DIM))
                kp, vp = kc, vc

    col = lambda base: pl.BlockSpec((None, s, HEAD_DIM), lambda bi, h: (bi, 0, base + h))
    return pl.pallas_call(
        body, name=name, grid=(b, nh),
        in_specs=[col(g * nh), col(g * nh), col((N_GROUPS + g) * nh)],
        out_specs=[col(0), col(0)],
        out_shape=[jax.ShapeDtypeStruct((b, s, hw), F32), jax.ShapeDtypeStruct((b, s, hw), F32)],
        compiler_params=_cp("parallel", "parallel"),
    )(q, kv, kv)


def attn_merge(outs, lses, name, tm=256):
    t, hw = outs[0].shape
    tm = _row_tile(t, tm)
    ng = len(outs)

    def body(*refs):
        o_refs, l_refs = refs[:ng], refs[ng:2 * ng]
        m_ref, lj_ref = refs[2 * ng:]
        ls = [l_refs[g][...] for g in range(ng)]
        mx = ls[0]
        for g in range(1, ng):
            mx = jnp.maximum(mx, ls[g])
        es = [jnp.exp(l - mx) for l in ls]
        tot = es[0]
        for g in range(1, ng):
            tot = tot + es[g]
        acc = (es[0] / tot) * o_refs[0][...]
        for g in range(1, ng):
            acc = acc + (es[g] / tot) * o_refs[g][...]
        m_ref[...] = acc.astype(BF16)
        lj_ref[...] = mx + jnp.log(tot)

    row = pl.BlockSpec((tm, hw), lambda i: (i, 0))
    return pl.pallas_call(
        body, name=name, grid=(t // tm,), in_specs=[row] * (2 * ng), out_specs=[row, row],
        out_shape=[jax.ShapeDtypeStruct((t, hw), BF16), jax.ShapeDtypeStruct((t, hw), F32)],
        compiler_params=_cp("parallel"),
    )(*outs, *lses)


def attn_bwd_prep(dmerged, merged, name, tm=256):
    t, hw = merged.shape
    nh = hw // HEAD_DIM
    tm = _row_tile(t, tm)

    def body(d_ref, m_ref, o_ref):
        for h in range(nh):
            sl = slice(h * HEAD_DIM, (h + 1) * HEAD_DIM)
            dsum = jnp.sum(d_ref[:, sl] * m_ref[:, sl].astype(F32), axis=-1, keepdims=True)
            o_ref[:, sl] = jnp.broadcast_to(dsum, (tm, HEAD_DIM))

    row = pl.BlockSpec((tm, hw), lambda i: (i, 0))
    return pl.pallas_call(
        body, name=name, grid=(t // tm,), in_specs=[row, row], out_specs=row,
        out_shape=jax.ShapeDtypeStruct((t, hw), F32), compiler_params=_cp("parallel"),
    )(dmerged, merged)


def attn_bwd(q, kv, g, dil, do, lsej, dm, dq_buf, dk_buf, dv_buf, accumulate, hw, name):
    b, s, _ = q.shape
    nh = hw // HEAD_DIM
    nblk = s // dil // BLK
    scale = 1.0 / math.sqrt(HEAD_DIM)
    assert dk_buf is not None or not accumulate
    kv_at = 6 + (dq_buf is not None)

    def body(*refs):
        q_ref, k_ref, v_ref, do_ref, lj_ref, dm_ref = refs[:6]
        dq_ref, dk_ref, dv_ref = refs[-3:]
        dki_ref, dvi_ref = (refs[kv_at], refs[kv_at + 1]) if accumulate else (None, None)
        mask2, mask1 = _band_masks()

        def put(rs, dk, dv):
            if accumulate:
                dk = dk + dki_ref[rs, :]
                dv = dv + dvi_ref[rs, :]
            dk_ref[rs, :] = dk
            dv_ref[rs, :] = dv

        for r in range(dil):
            kp = vp = hold_k = hold_v = rs_prev = None
            for n in range(nblk):
                rs = _tile_rows(r, n, dil)
                qt = q_ref[rs, :].astype(BF16)
                kc = k_ref[rs, :].astype(BF16)
                vc = v_ref[rs, :].astype(BF16)
                dot = do_ref[rs, :].astype(BF16)
                lm = lj_ref[rs, :]
                dmm = dm_ref[rs, :]
                if n == 0:
                    kcat, vcat, mask = kc, vc, mask1
                else:
                    kcat, vcat, mask = jnp.concatenate([kp, kc], axis=0), jnp.concatenate([vp, vc], axis=0), mask2
                    lm = jnp.concatenate([lm, lm], axis=1)
                    dmm = jnp.concatenate([dmm, dmm], axis=1)
                p = jnp.exp(jnp.where(mask, _dot_nt(qt, kcat) * scale, NEG_INF) - lm)
                ds = (p * (_dot_nt(dot, vcat) - dmm)).astype(BF16)
                dq_ref[rs, :] = _dot(ds, kcat) * scale
                dkc = _dot_tn(ds, qt) * scale
                dvc = _dot_tn(p.astype(BF16), dot)
                if n > 0:
                    put(rs_prev, hold_k + dkc[:BLK, :], hold_v + dvc[:BLK, :])
                    dkc, dvc = dkc[BLK:, :], dvc[BLK:, :]
                hold_k, hold_v, kp, vp, rs_prev = dkc, dvc, kc, vc, rs
            put(rs_prev, hold_k, hold_v)

    col = lambda base: pl.BlockSpec((None, s, HEAD_DIM), lambda bi, h: (bi, 0, base + h))
    any_spec = pl.BlockSpec(memory_space=pl.ANY)
    in_specs = [col(g * nh), col(g * nh), col((N_GROUPS + g) * nh), col(0), col(0), col(0)]
    args = [q, kv, kv, do, lsej, dm]
    aliases = {}
    if dq_buf is not None:
        in_specs.append(any_spec)
        args.append(dq_buf)
        aliases[6] = 0
    if dk_buf is not None:
        in_specs += [col(g * nh) if accumulate else any_spec] * 2
        args += [dk_buf, dv_buf]
        aliases.update({kv_at: 1, kv_at + 1: 2})
    shape = jax.ShapeDtypeStruct((b, s, N_GROUPS * hw), F32)
    return pl.pallas_call(
        body, name=name, grid=(b, nh), in_specs=in_specs, out_specs=[col(g * nh)] * 3, out_shape=[shape] * 3,
        input_output_aliases=aliases, compiler_params=_cp("parallel", "parallel"),
    )(*args)


def sum_parts(g, recv, me, name, tm=256):
    _, rows, c = g.shape
    n = recv.shape[0]
    tm = _row_tile(rows, tm)

    def body(me_ref, g_ref, r_ref, o_ref):
        acc = g_ref[...].astype(F32)
        for j in range(n):
            acc = acc + r_ref[j].astype(F32)
        o_ref[...] = acc

    return pl.pallas_call(
        body, name=name,
        grid_spec=pltpu.PrefetchScalarGridSpec(
            num_scalar_prefetch=1, grid=(rows // tm,),
            in_specs=[pl.BlockSpec((None, tm, c), lambda i, me_ref: (me_ref[0], i, 0)),
                      pl.BlockSpec((n, tm, c), lambda i, me_ref: (0, i, 0))],
            out_specs=pl.BlockSpec((tm, c), lambda i, me_ref: (i, 0))),
        out_shape=jax.ShapeDtypeStruct((rows, c), F32), compiler_params=_cp("parallel"),
    )(me, g, recv)


def adamw(w, m, v, g_parts, name, tm=256):
    rows, c = w.shape
    tm = _row_tile(rows, tm)
    npart = len(g_parts)

    def body(*refs):
        w_ref, m_ref, v_ref = refs[:3]
        g_refs = refs[3:3 + npart]
        go_ref, d_ref, mo_ref, vo_ref = refs[3 + npart:]
        g = g_refs[0][...]
        for k in range(1, npart):
            g = g + g_refs[k][...]
        mn = ADAM_B1 * m_ref[...] + (1.0 - ADAM_B1) * g
        vn = ADAM_B2 * v_ref[...] + (1.0 - ADAM_B2) * (g * g)
        m_hat = mn / (1.0 - ADAM_B1 ** ADAM_STEP)
        v_hat = vn / (1.0 - ADAM_B2 ** ADAM_STEP)
        go_ref[...] = g
        d_ref[...] = -ADAM_LR * (m_hat / (jnp.sqrt(v_hat) + ADAM_EPS) + ADAM_WD * w_ref[...])
        mo_ref[...] = mn
        vo_ref[...] = vn

    row = pl.BlockSpec((tm, c), lambda i: (i, 0))
    return pl.pallas_call(
        body, name=name, grid=(rows // tm,), in_specs=[row] * (3 + npart), out_specs=[row] * 4,
        out_shape=[jax.ShapeDtypeStruct((rows, c), F32)] * 4, compiler_params=_cp("parallel"),
    )(w, m, v, *g_parts)


def _place():
    return lax.axis_index("x"), lax.axis_index("y"), lax.axis_index("c")


def _other_chips(x, y, c):
    return [(1 - x, y, c), (x, 1 - y, c), (1 - x, 1 - y, c)]


def _chip_of(px, py):
    return 2 * px + py


HBM_SPEC = pl.BlockSpec(memory_space=pltpu.HBM)
SEM_SPEC = pl.BlockSpec(memory_space=pltpu.SEMAPHORE)
ANY_SPEC = pl.BlockSpec(memory_space=pl.ANY)
DATAFLOW = pltpu.SideEffectType.DATAFLOW_SIDE_EFFECTING
N_PEER_CHIPS = N_CHIPS - 1


def _hbm(a):
    return pltpu.with_memory_space_constraint(a, pltpu.HBM)


def _hbm_like(arrays):
    return [pltpu.HBM(a.shape, a.dtype) for a in arrays]


def cast_place(w, layer, me, out_dtype, name, tm=256):
    rows, c = w.shape[-2:]
    tm = _row_tile(rows, tm)

    def body(me_ref, w_ref, o_ref):
        o_ref[...] = w_ref[...].astype(out_dtype)

    if layer is None:
        in_spec = pl.BlockSpec((tm, c), lambda i, me_ref: (i, 0))
    else:
        in_spec = pl.BlockSpec((None, tm, c), lambda i, me_ref: (layer, i, 0))
    return pl.pallas_call(
        body, name=name,
        grid_spec=pltpu.PrefetchScalarGridSpec(
            num_scalar_prefetch=1, grid=(rows // tm,), in_specs=[in_spec],
            out_specs=pl.BlockSpec((None, tm, c), lambda i, me_ref: (me_ref[0], i, 0))),
        out_shape=jax.ShapeDtypeStruct((N_CHIPS, rows, c), out_dtype), compiler_params=_cp("parallel"),
    )(me, w)


def gather_start(lands, chunk_sizes, name="gather_start"):
    n = len(lands)
    nch = len(chunk_sizes)
    assert sum(chunk_sizes) == n

    def body(*refs):
        land_refs = refs[:n]
        outs = refs[n:]
        send_sems, recv_sems = outs[:nch], outs[nch:2 * nch]
        token = outs[-1]
        x, y, c = _place()
        me = _chip_of(x, y)
        peers = _other_chips(x, y, c)
        k = 0
        for ck, size in enumerate(chunk_sizes):
            for pos in range(size):
                for r, peer in enumerate(peers):
                    pltpu.make_async_remote_copy(
                        src_ref=land_refs[k].at[me], dst_ref=land_refs[k].at[me],
                        send_sem=send_sems[ck].at[N_PEER_CHIPS * pos + r], recv_sem=recv_sems[ck].at[N_PEER_CHIPS * pos + r],
                        device_id=peer, device_id_type=MESH).start()
                k += 1
        token[...] = jnp.zeros(token.shape, F32)

    sems = [pltpu.SemaphoreType.DMA((N_PEER_CHIPS * s,)) for s in chunk_sizes]
    res = pl.pallas_call(
        body, name=name,
        out_shape=(*sems, *sems, *_hbm_like(lands), jax.ShapeDtypeStruct(DEP_SPEC_SHAPE, F32)),
        in_specs=[HBM_SPEC] * n,
        out_specs=(*[SEM_SPEC] * (2 * nch), *[HBM_SPEC] * n, pl.BlockSpec(memory_space=pltpu.VMEM)),
        input_output_aliases={k: 2 * nch + k for k in range(n)},
        compiler_params=pltpu.CompilerParams(has_side_effects=DATAFLOW),
    )(*[_hbm(a) for a in lands])
    return res[:nch], res[nch:2 * nch], res[2 * nch:2 * nch + n], res[-1]


def gather_wait(send_sem, recv_sem, lands, after, name):
    n = len(lands)

    def body(*refs):
        land_refs = refs[:n]
        ssem, rsem = refs[n], refs[n + 1]
        x, y, c = _place()
        me = _chip_of(x, y)
        for pos in range(n):
            for r, peer in enumerate(_other_chips(x, y, c)):
                cp = pltpu.make_async_remote_copy(
                    src_ref=land_refs[pos].at[me], dst_ref=land_refs[pos].at[_chip_of(peer[0], peer[1])],
                    send_sem=ssem.at[N_PEER_CHIPS * pos + r], recv_sem=rsem.at[N_PEER_CHIPS * pos + r],
                    device_id=peer, device_id_type=MESH)
                cp.wait_send()
                cp.wait_recv()

    return pl.pallas_call(
        body, name=name, out_shape=tuple(_hbm_like(lands)),
        in_specs=[*[HBM_SPEC] * n, SEM_SPEC, SEM_SPEC, ANY_SPEC], out_specs=[HBM_SPEC] * n,
        input_output_aliases={k: k for k in range(n)},
        compiler_params=pltpu.CompilerParams(has_side_effects=DATAFLOW),
    )(*lands, send_sem, recv_sem, after)


def scatter_start(grads, name):
    n = len(grads)
    recvs = [lax.empty((N_PEER_CHIPS, *g.shape[1:]), g.dtype) for g in grads]

    def body(*refs):
        g_refs, r_refs = refs[:n], refs[n:2 * n]
        send_sems, recv_sems = refs[2 * n], refs[2 * n + 1]
        token = refs[-1]
        x, y, c = _place()
        for k in range(n):
            for r, peer in enumerate(_other_chips(x, y, c)):
                pltpu.make_async_remote_copy(
                    src_ref=g_refs[k].at[_chip_of(peer[0], peer[1])], dst_ref=r_refs[k].at[r],
                    send_sem=send_sems.at[N_PEER_CHIPS * k + r], recv_sem=recv_sems.at[N_PEER_CHIPS * k + r],
                    device_id=peer, device_id_type=MESH).start()
        token[...] = jnp.zeros(token.shape, F32)

    sem = pltpu.SemaphoreType.DMA((N_PEER_CHIPS * n,))
    res = pl.pallas_call(
        body, name=name,
        out_shape=(sem, sem, *_hbm_like(grads), *_hbm_like(recvs), jax.ShapeDtypeStruct(DEP_SPEC_SHAPE, F32)),
        in_specs=[HBM_SPEC] * (2 * n),
        out_specs=(SEM_SPEC, SEM_SPEC, *[HBM_SPEC] * (2 * n), pl.BlockSpec(memory_space=pltpu.VMEM)),
        input_output_aliases={k: 2 + k for k in range(2 * n)},
        compiler_params=pltpu.CompilerParams(has_side_effects=DATAFLOW),
    )(*[_hbm(a) for a in grads], *[_hbm(a) for a in recvs])
    return res[0], res[1], res[2:2 + n], res[2 + n:2 + 2 * n], res[-1]


def scatter_wait(send_sem, recv_sem, grads, recvs, after, name):
    n = len(grads)

    def body(*refs):
        g_refs, r_refs = refs[:n], refs[n:2 * n]
        ssem, rsem = refs[2 * n], refs[2 * n + 1]
        x, y, c = _place()
        for k in range(n):
            for r, peer in enumerate(_other_chips(x, y, c)):
                cp = pltpu.make_async_remote_copy(
                    src_ref=g_refs[k].at[_chip_of(peer[0], peer[1])], dst_ref=r_refs[k].at[r],
                    send_sem=ssem.at[N_PEER_CHIPS * k + r], recv_sem=rsem.at[N_PEER_CHIPS * k + r],
                    device_id=peer, device_id_type=MESH)
                cp.wait_send()
                cp.wait_recv()

    res = pl.pallas_call(
        body, name=name, out_shape=(*_hbm_like(grads), *_hbm_like(recvs)),
        in_specs=[*[HBM_SPEC] * (2 * n), SEM_SPEC, SEM_SPEC, ANY_SPEC], out_specs=[HBM_SPEC] * (2 * n),
        input_output_aliases={k: k for k in range(2 * n)},
        compiler_params=pltpu.CompilerParams(has_side_effects=DATAFLOW),
    )(*grads, *recvs, send_sem, recv_sem, after)
    return res[:n], res[n:]


def swap_with_sibling(parts, name="swap_with_sibling"):
    n = len(parts)

    def body(*refs):
        ins, outs = refs[:n], refs[n:2 * n]
        send_sems, recv_sems = refs[2 * n:]
        x, y, c = _place()
        copies = []
        for k in range(n):
            cp = pltpu.make_async_remote_copy(
                src_ref=ins[k], dst_ref=outs[k], send_sem=send_sems.at[k], recv_sem=recv_sems.at[k],
                device_id=(x, y, 1 - c), device_id_type=MESH)
            cp.start()
            copies.append(cp)
        for cp in copies:
            cp.wait()

    any_spec = pl.BlockSpec(memory_space=pl.ANY)
    return pl.pallas_call(
        body, name=name, in_specs=[any_spec] * n, out_specs=[any_spec] * n,
        out_shape=[jax.ShapeDtypeStruct(p.shape, p.dtype) for p in parts],
        scratch_shapes=[pltpu.SemaphoreType.DMA((n,)), pltpu.SemaphoreType.DMA((n,))],
        compiler_params=pltpu.CompilerParams(has_side_effects=True),
    )(*parts)


def allreduce_small(vec, name="allreduce_small"):
    rows, lanes = vec.shape

    def body(v_ref, o_ref, buf_ref, send_sems, recv_sems):
        x, y, c = _place()
        me = 4 * x + 2 * y + c
        buf_ref[me] = v_ref[...]
        copies = []
        for k in range(1, N_DEV):
            px, py, pc = x ^ ((k >> 2) & 1), y ^ ((k >> 1) & 1), c ^ (k & 1)
            peer_slot = 4 * px + 2 * py + pc
            cp = pltpu.make_async_remote_copy(
                src_ref=buf_ref.at[me], dst_ref=buf_ref.at[me], send_sem=send_sems.at[k - 1],
                recv_sem=recv_sems.at[k - 1], device_id=(px, py, pc), device_id_type=MESH)
            cp.start()
            copies.append(pltpu.make_async_remote_copy(
                src_ref=buf_ref.at[me], dst_ref=buf_ref.at[peer_slot], send_sem=send_sems.at[k - 1],
                recv_sem=recv_sems.at[k - 1], device_id=(px, py, pc), device_id_type=MESH))
        for cp in copies:
            cp.wait()
        acc = buf_ref[0]
        for j in range(1, N_DEV):
            acc = acc + buf_ref[j]
        o_ref[...] = acc

    vm = pl.BlockSpec(memory_space=pltpu.VMEM)
    return pl.pallas_call(
        body, name=name, in_specs=[vm], out_specs=vm, out_shape=jax.ShapeDtypeStruct((rows, lanes), F32),
        scratch_shapes=[pltpu.VMEM((N_DEV, rows, lanes), F32), pltpu.SemaphoreType.DMA((N_DEV - 1,)),
                        pltpu.SemaphoreType.DMA((N_DEV - 1,))],
        compiler_params=pltpu.CompilerParams(has_side_effects=True, vmem_limit_bytes=VMEM_LIMIT_V7X),
    )(vec)


def _pack(arrays):
    flat = jnp.concatenate([a.reshape(-1).astype(F32) for a in arrays])
    n = flat.shape[0]
    rows = -(-n // LANES)
    rows = -(-rows // 8) * 8
    return jnp.pad(flat, (0, rows * LANES - n)).reshape(rows, LANES)


def _unpack(packed, shapes, lead=()):
    flat = packed.reshape(*lead, -1)
    out, off = [], 0
    for shp in shapes:
        n = math.prod(shp)
        out.append(flat[..., off:off + n].reshape(*lead, *shp))
        off += n
    return out


def _row(vec):
    return vec.reshape(1, -1)


def kernel(x, mix_pre_g, mix_post_g, ffn_pre_g, ffn_post_g, cm_w_in, cm_b_in, cm_dw, cm_dw_b, cm_ln_g, cm_ln_b, cm_w_out, cm_b_out, kv_norm_g, w_kv, w_q, w_o, ffn_w_in, ffn_dw, ffn_dw_b, ffn_w_out, loss_target, m_mix_pre_g, m_mix_post_g, m_ffn_pre_g, m_ffn_post_g, m_cm_w_in, m_cm_b_in, m_cm_dw, m_cm_dw_b, m_cm_ln_g, m_cm_ln_b, m_cm_w_out, m_cm_b_out, m_kv_norm_g, m_w_kv, m_w_q, m_w_o, m_ffn_w_in, m_ffn_dw, m_ffn_dw_b, m_ffn_w_out, v_mix_pre_g, v_mix_post_g, v_ffn_pre_g, v_ffn_post_g, v_cm_w_in, v_cm_b_in, v_cm_dw, v_cm_dw_b, v_cm_ln_g, v_cm_ln_b, v_cm_w_out, v_cm_b_out, v_kv_norm_g, v_w_kv, v_w_q, v_w_o, v_ffn_w_in, v_ffn_dw, v_ffn_dw_b, v_ffn_w_out):
    names = ["mix_pre_g", "mix_post_g", "ffn_pre_g", "ffn_post_g", "cm_w_in", "cm_b_in", "cm_dw", "cm_dw_b", "cm_ln_g",
             "cm_ln_b", "cm_w_out", "cm_b_out", "kv_norm_g", "w_kv", "w_q", "w_o", "ffn_w_in", "ffn_dw", "ffn_dw_b",
             "ffn_w_out"]
    w_in = dict(zip(names, [mix_pre_g, mix_post_g, ffn_pre_g, ffn_post_g, cm_w_in, cm_b_in, cm_dw, cm_dw_b, cm_ln_g,
                            cm_ln_b, cm_w_out, cm_b_out, kv_norm_g, w_kv, w_q, w_o, ffn_w_in, ffn_dw, ffn_dw_b, ffn_w_out]))
    m_in = dict(zip(names, [m_mix_pre_g, m_mix_post_g, m_ffn_pre_g, m_ffn_post_g, m_cm_w_in, m_cm_b_in, m_cm_dw, m_cm_dw_b,
                            m_cm_ln_g, m_cm_ln_b, m_cm_w_out, m_cm_b_out, m_kv_norm_g, m_w_kv, m_w_q, m_w_o, m_ffn_w_in,
                            m_ffn_dw, m_ffn_dw_b, m_ffn_w_out]))
    v_in = dict(zip(names, [v_mix_pre_g, v_mix_post_g, v_ffn_pre_g, v_ffn_post_g, v_cm_w_in, v_cm_b_in, v_cm_dw, v_cm_dw_b,
                            v_cm_ln_g, v_cm_ln_b, v_cm_w_out, v_cm_b_out, v_kv_norm_g, v_w_kv, v_w_q, v_w_o, v_ffn_w_in,
                            v_ffn_dw, v_ffn_dw_b, v_ffn_w_out]))

    bsz, seq, d = x.shape
    t = bsz * seq
    n_b = DEPTH - N_A
    hw = w_o.shape[-1]
    qw = N_GROUPS * hw
    f2 = ffn_dw_b.shape[-1]
    f = f2 // 2
    me_chip = _chip_of(lax.axis_index("x"), lax.axis_index("y"))

    big = ["cm_w_in", "cm_w_out", "w_kv", "w_q", "w_o", "ffn_w_in", "ffn_w_out"]
    row_sharded = ("cm_w_out", "w_o", "ffn_w_out")
    small_sharded = ["cm_b_in", "cm_dw", "cm_dw_b", "cm_ln_g", "cm_ln_b", "cm_b_out", "ffn_dw"]
    small_pack = _pack([w_in[n] for n in small_sharded])
    chunks = [
        [("cm_w_in", 0), ("small", None)],
        [("cm_w_out", 0)],
        [("ffn_w_in", 0), ("ffn_w_out", 0)],
        [("cm_w_in", 1), ("cm_w_out", 1)],
        [("ffn_w_in", 1), ("ffn_w_out", 1)],
        [("w_kv", None)],
        [("w_q", 0), ("w_o", 0)],
        [("ffn_w_in", 2), ("ffn_w_out", 2)],
        [("w_q", 1), ("w_o", 1)],
        [("ffn_w_in", 3), ("ffn_w_out", 3)],
    ]
    pieces = [pc for ch in chunks for pc in ch]
    chunk_of = {pc: ck for ck, ch in enumerate(chunks) for pc in ch}

    me_arr = me_chip.astype(jnp.int32).reshape(1)

    def land_of(pc):
        n, l = pc
        if n == "small":
            return cast_place(small_pack, None, me_arr, F32, name="place_small")
        return cast_place(w_in[n], l, me_arr, BF16, name=f"place_{n}_{l}")

    lands = [land_of(pc) for pc in pieces]
    g_send, g_recv, lands_f, token = gather_start(lands, [len(ch) for ch in chunks])
    weights = {}

    def finish_chunk(ck, after):
        lo = sum(len(ch) for ch in chunks[:ck])
        hi = lo + len(chunks[ck])
        got = gather_wait(g_send[ck], g_recv[ck], lands_f[lo:hi], after, name=f"gather_wait{ck}")
        for pc, arr in zip(chunks[ck], got):
            weights[pc] = arr.reshape(1, -1, arr.shape[-1]) if pc[0] in row_sharded else arr

    def wmat(n, l=None, after=None):
        if (n, l) not in weights:
            finish_chunk(chunk_of[(n, l)], after)
        arr = weights[(n, l)]
        return arr, arr.shape[0]

    finish_chunk(0, token)
    small_full = {}
    for n, arr4 in zip(small_sharded, _unpack(weights[("small", None)], [w_in[n].shape for n in small_sharded], lead=(N_CHIPS,))):
        shp = w_in[n].shape
        small_full[n] = jnp.moveaxis(arr4, 0, -2).reshape(*shp[:-1], N_CHIPS * shp[-1])

    x2d = x.reshape(t, d)
    saved = []
    (h1,) = resid_norm_fwd(x2d, None, None, [_row(mix_pre_g[0])], name="norm_in", dep=token)
    xcur = x2d
    kv_state = None
    for i in range(DEPTH):
        sv = {"x_in": xcur, "h1": h1}
        if i < N_A:
            z = mm_nn(h1, *wmat("cm_w_in", i, h1), 1, 0, bias=_row(small_full["cm_b_in"][i]), name=f"cm_in{i}")
            u2 = glu_conv_fwd(z.reshape(bsz, seq, 2 * d), small_full["cm_dw"][i], _row(small_full["cm_dw_b"][i]),
                              name=f"glu_conv{i}").reshape(t, d)
            u4 = ln_silu_fwd(u2, _row(small_full["cm_ln_g"][i]), _row(small_full["cm_ln_b"][i]), name=f"ln_silu{i}")
            y = mm_nn(u4, *wmat("cm_w_out", i, u4), 1, 0, bias=_row(small_full["cm_b_out"][i]), name=f"cm_out{i}")
            sv.update(z=z, u2=u2, u4=u4)
        else:
            j = i - N_A
            q = mm_nn(h1, *wmat("w_q", j, h1), 1, 0, name=f"q_proj{j}").reshape(bsz, seq, qw)
            outs, lses = [], []
            for g, dil in enumerate(DILATIONS):
                o_g, l_g = attn_fwd(q, kv_state["kv"], g, dil, hw, name=f"attn_fwd{j}_{g}")
                outs.append(o_g.reshape(t, hw))
                lses.append(l_g.reshape(t, hw))
            merged, lsej = attn_merge(outs, lses, name=f"attn_merge{j}")
            y = mm_nn(merged, *wmat("w_o", j, merged), 1, 0, name=f"o_proj{j}")
            sv.update(q=q, merged=merged, lsej=lsej)
        x1, h2 = resid_norm_fwd(xcur, y, _row(mix_post_g[i]), [_row(ffn_pre_g[i])], name=f"resid_mix{i}")
        p = mm_nn(h2, *wmat("ffn_w_in", i, h2), 1, 0, name=f"ffn_in{i}")
        s_act = ffn_mid_fwd(p.reshape(bsz, seq, f2), small_full["ffn_dw"][i], _row(ffn_dw_b[i]), name=f"ffn_mid{i}").reshape(t, f)
        y2 = mm_nn(s_act, *wmat("ffn_w_out", i), 1, 0, name=f"ffn_out{i}")
        next_gains = []
        if i + 1 < DEPTH:
            next_gains.append(_row(mix_pre_g[i + 1]))
        if i == N_A - 1:
            next_gains.append(_row(kv_norm_g))
        res = resid_norm_fwd(x1, y2, _row(ffn_post_g[i]), next_gains, name=f"resid_ffn{i}")
        sv.update(y=y, x1=x1, h2=h2, p=p, s=s_act, y2=y2)
        saved.append(sv)
        xcur = res[0]
        if i + 1 < DEPTH:
            h1 = res[1]
        if i == N_A - 1:
            kvn = res[2]
            kv = mm_nn(kvn, *wmat("w_kv", None, kvn), 1, 0, name="kv_proj").reshape(bsz, seq, 2 * qw)
            kv_state = {"kv": kv, "kvn": kvn, "x_a": xcur}

    dx, loss_tile = loss_fwd_bwd(xcur, loss_target.reshape(t, d))
    loss = lax.psum(loss_tile[0, 0], ("x", "y", "c"))

    gsm = {n: [None] * w_in[n].shape[0] for n in
           ["mix_pre_g", "mix_post_g", "ffn_pre_g", "ffn_post_g", "cm_b_in", "cm_dw", "cm_dw_b", "cm_ln_g", "cm_ln_b",
            "cm_b_out", "ffn_dw", "ffn_dw_b"]}
    gbig = {}
    in_flight = []
    dep = None

    def start_scatter(pcs, tag):
        ssem, rsem, g_f, r_f, tok = scatter_start([gbig[pc] for pc in pcs], name=f"scatter_start_{tag}")
        in_flight.append((pcs, ssem, rsem, g_f, r_f))
        return tok

    dk_buf = dv_buf = None
    for i in range(DEPTH - 1, -1, -1):
        sv = saved[i]
        dy2, dg, _ = norm_bwd(sv["y2"], _row(ffn_post_g[i]), dx, out_dtype=BF16, name=f"bwd_ffn_post{i}", dep=dep)
        gsm["ffn_post_g"][i] = dg
        ds = mm_nt(dy2, *wmat("ffn_w_out", i), 1, 0, name=f"bwd_ffn_out_dx{i}")
        gbig[("ffn_w_out", i)] = mm_tn(sv["s"], dy2, 1, name=f"bwd_ffn_out_dw{i}").reshape(N_CHIPS, f // N_CHIPS, d)
        dpa, dpg, ddwa, ddwg, ddba, ddbg = ffn_mid_bwd(sv["p"].reshape(bsz, seq, f2), small_full["ffn_dw"][i], _row(ffn_dw_b[i]),
                                                       ds.reshape(bsz, seq, f), name=f"bwd_ffn_mid{i}")
        gsm["ffn_dw"][i] = jnp.concatenate([jnp.sum(ddwa, axis=1), jnp.sum(ddwg, axis=1)], axis=-1)
        gsm["ffn_dw_b"][i] = jnp.concatenate([ddba, ddbg], axis=-1)
        dp = [dpa.reshape(t, f), dpg.reshape(t, f)]
        dh2 = mm_nt(dp, *wmat("ffn_w_in", i), 1, 0, name=f"bwd_ffn_in_dx{i}")
        gbig[("ffn_w_in", i)] = mm_tn(sv["h2"], dp, N_CHIPS, name=f"bwd_ffn_in_dw{i}")
        dx1, dg, _ = norm_bwd(sv["x1"], _row(ffn_pre_g[i]), dh2, add=dx, name=f"bwd_ffn_pre{i}")
        gsm["ffn_pre_g"][i] = dg
        dep = start_scatter([("ffn_w_in", 0), ("ffn_w_out", 0)], "ffn0") if i == 0 else None
        dy, dg, dbias = norm_bwd(sv["y"], _row(mix_post_g[i]), dx1, out_dtype=BF16, name=f"bwd_mix_post{i}", dep=dep)
        gsm["mix_post_g"][i] = dg
        if i < N_A:
            gsm["cm_b_out"][i] = dbias
            du4 = mm_nt(dy, *wmat("cm_w_out", i), 1, 0, name=f"bwd_cm_out_dx{i}")
            gbig[("cm_w_out", i)] = mm_tn(sv["u4"], dy, 1, name=f"bwd_cm_out_dw{i}").reshape(N_CHIPS, d // N_CHIPS, d)
            du2, dlg, dlb = ln_silu_bwd(sv["u2"], _row(small_full["cm_ln_g"][i]), _row(small_full["cm_ln_b"][i]), du4,
                                        name=f"bwd_ln_silu{i}")
            gsm["cm_ln_g"][i], gsm["cm_ln_b"][i] = dlg, dlb
            dza, dzg, ddw, ddwb, dba, dbg = glu_conv_bwd(sv["z"].reshape(bsz, seq, 2 * d), small_full["cm_dw"][i],
                                                         du2.reshape(bsz, seq, d), name=f"bwd_glu_conv{i}")
            gsm["cm_dw"][i] = jnp.sum(ddw, axis=1)
            gsm["cm_dw_b"][i] = ddwb
            gsm["cm_b_in"][i] = jnp.concatenate([dba, dbg], axis=-1)
            dz = [dza.reshape(t, d), dzg.reshape(t, d)]
            dh1 = mm_nt(dz, *wmat("cm_w_in", i), 1, 0, name=f"bwd_cm_in_dx{i}")
            gbig[("cm_w_in", i)] = mm_tn(sv["h1"], dz, N_CHIPS, name=f"bwd_cm_in_dw{i}")
        else:
            j = i - N_A
            dmerged = mm_nt(dy, *wmat("w_o", j), 1, 0, name=f"bwd_o_proj_dx{j}")
            gbig[("w_o", j)] = mm_tn(sv["merged"], dy, 1, name=f"bwd_o_proj_dw{j}").reshape(N_CHIPS, hw // N_CHIPS, d)
            dmt = attn_bwd_prep(dmerged, sv["merged"], name=f"bwd_attn_prep{j}")
            dq_buf = None
            add_to_kv = dk_buf is not None
            for g, dil in enumerate(DILATIONS):
                dq_buf, dk_buf, dv_buf = attn_bwd(
                    sv["q"], kv_state["kv"], g, dil, dmerged.reshape(bsz, seq, hw), sv["lsej"].reshape(bsz, seq, hw),
                    dmt.reshape(bsz, seq, hw), dq_buf, dk_buf, dv_buf, add_to_kv, hw, name=f"attn_bwd{j}_{g}")
            dq = dq_buf.reshape(t, qw)
            dh1 = mm_nt(dq, *wmat("w_q", j), 1, 0, name=f"bwd_q_proj_dx{j}")
            gbig[("w_q", j)] = mm_tn(sv["h1"], dq, N_CHIPS, name=f"bwd_q_proj_dw{j}")
        dx, dg, _ = norm_bwd(sv["x_in"], _row(mix_pre_g[i]), dh1, add=dx1, name=f"bwd_mix_pre{i}")
        gsm["mix_pre_g"][i] = dg
        if i > N_A:
            dep = start_scatter([("ffn_w_in", i), ("ffn_w_out", i), ("w_q", i - N_A), ("w_o", i - N_A)], f"l{i}")
        elif 0 < i < N_A:
            dep = start_scatter([("ffn_w_in", i), ("ffn_w_out", i), ("cm_w_in", i), ("cm_w_out", i)], f"l{i}")
        elif i == 0:
            last_token = start_scatter([("cm_w_in", 0), ("cm_w_out", 0)], "cm0")
        if i == N_A:
            dkv = [dk_buf.reshape(t, qw), dv_buf.reshape(t, qw)]
            dkvn = mm_nt(dkv, *wmat("w_kv"), 1, 0, name="bwd_kv_proj_dx")
            gbig[("w_kv", None)] = mm_tn(kv_state["kvn"], dkv, N_CHIPS, name="bwd_kv_proj_dw")
            dx, dg_kv, _ = norm_bwd(kv_state["x_a"], _row(kv_norm_g), dkvn, add=dx, name="bwd_kv_norm")
            dep = start_scatter([("ffn_w_in", i), ("ffn_w_out", i), ("w_q", 0), ("w_o", 0), ("w_kv", None)], f"l{i}")
    grad_x = dx.reshape(bsz, seq, d)

    plane_of = {}
    outs_g, outs_d, outs_m, outs_v = {}, {}, {}, {}

    def finish_scatter(k, after):
        pcs, ssem, rsem, g_f, r_f = in_flight[k]
        g_done, r_done = scatter_wait(ssem, rsem, g_f, r_f, after, name=f"scatter_wait{k}")
        for pc, g_arr, r_arr in zip(pcs, g_done, r_done):
            plane_of[pc] = sum_parts(g_arr, r_arr, me_arr, name=f"sum_chips_{pc[0]}_{pc[1]}")

    def update(group, tag):
        plane = []
        for n in group:
            if w_in[n].ndim == 2:
                plane.append(plane_of[(n, None)])
            else:
                plane.append(jnp.concatenate([plane_of[(n, l)] for l in range(w_in[n].shape[0])], axis=0))
        other = swap_with_sibling(plane, name=f"swap_with_sibling_{tag}")
        for n, p_mine, p_other in zip(group, plane, other):
            shp = w_in[n].shape
            flat = lambda a: a.reshape(-1, shp[-1])
            g_, d_, m_, v_ = adamw(flat(w_in[n]), flat(m_in[n]), flat(v_in[n]), [p_mine, p_other], name=f"adamw_{n}")
            outs_g[n], outs_d[n], outs_m[n], outs_v[n] = (a.reshape(shp) for a in (g_, d_, m_, v_))

    for k in range(len(in_flight) - 1):
        finish_scatter(k, last_token)
    update(["w_kv", "w_q", "w_o", "ffn_w_in", "ffn_w_out"], "a")
    finish_scatter(len(in_flight) - 1, outs_v["ffn_w_out"])
    update(["cm_w_in", "cm_w_out"], "b")

    small_names = [n for n in names if n not in big]
    small_shapes_full = {}
    small_grads_full = []
    for n in small_names:
        if n == "kv_norm_g":
            gfull = dg_kv.reshape(-1)
        elif n in ("cm_dw", "ffn_dw"):
            gfull = jnp.stack(gsm[n], axis=0)
        else:
            gfull = jnp.stack([a.reshape(-1) for a in gsm[n]], axis=0)
        small_shapes_full[n] = gfull.shape
        small_grads_full.append(gfull)
    summed = allreduce_small(_pack(small_grads_full))
    g_full = dict(zip(small_names, _unpack(summed, [small_shapes_full[n] for n in small_names])))
    g_loc = {}
    for n in small_names:
        if n in small_sharded:
            width = w_in[n].shape[-1]
            g_loc[n] = lax.dynamic_slice_in_dim(g_full[n], me_chip * width, width, axis=g_full[n].ndim - 1)
        else:
            g_loc[n] = g_full[n]
    res = adamw(_pack([w_in[n] for n in small_names]), _pack([m_in[n] for n in small_names]),
                _pack([v_in[n] for n in small_names]), [_pack([g_loc[n] for n in small_names])], name="adamw_small")
    shapes_loc = [w_in[n].shape for n in small_names]
    for dst, packed in zip((outs_g, outs_d, outs_m, outs_v), res):
        for n, a in zip(small_names, _unpack(packed, shapes_loc)):
            dst[n] = a

    return (loss, grad_x, *[outs_g[n] for n in names], *[outs_d[n] for n in names],
            *[outs_m[n] for n in names], *[outs_v[n] for n in names])
```

```python
import functools
import math

import jax
import jax.numpy as jnp
from jax import lax
from jax.experimental import pallas as pl
from jax.experimental.pallas import tpu as pltpu

F32 = jnp.float32
BF16 = jnp.bfloat16
EPS = 1e-6
NEG_INF = -1e30
N_A = 2
DEPTH = 4
N_GROUPS = 3
DILATIONS = (1, 4, 16)
HEAD_DIM = 128
BLK = 128
LANES = 128
N_CHIPS = 4
N_DEV = 8
VMEM_LIMIT_V7X = 56 * 1024 * 1024

ADAM_LR = 0.001
ADAM_B1 = 0.9
ADAM_B2 = 0.999
ADAM_EPS = 1e-08
ADAM_WD = 0.01
ADAM_STEP = 10

MESH = pl.DeviceIdType.MESH


def _cp(*sem, **kw):
    return pltpu.CompilerParams(dimension_semantics=sem if sem else None, vmem_limit_bytes=VMEM_LIMIT_V7X, **kw)


def _dot(a, b):
    return jnp.dot(a, b, preferred_element_type=F32)


def _dot_nt(a, b):
    return lax.dot_general(a, b, (((1,), (1,)), ((), ())), preferred_element_type=F32)


def _dot_tn(a, b):
    return lax.dot_general(a, b, (((0,), (0,)), ((), ())), preferred_element_type=F32)


def _sigmoid(x):
    return 1.0 / (1.0 + jnp.exp(-x))


def _row_tile(n, want):
    if n <= want:
        return n
    for t in range(want - want % 8, 7, -8):
        if n % t == 0:
            return t
    raise ValueError(f"no row tile for {n} rows")


def mm_nn(a, w, nsh, stride, layer, bias=None, out_dtype=F32, name="mm_nn", tm=512):
    m, k = a.shape
    _, k2, ns = w.shape
    assert k == k2
    tm = _row_tile(m, tm)
    has_bias = bias is not None

    def body(*refs):
        if has_bias:
            a_ref, w_ref, b_ref, o_ref = refs
        else:
            a_ref, w_ref, o_ref = refs
        acc = _dot(a_ref[...].astype(BF16), w_ref[...])
        if has_bias:
            acc = acc + b_ref[...]
        o_ref[...] = acc.astype(out_dtype)

    in_specs = [
        pl.BlockSpec((tm, k), lambda j, i: (i, 0)),
        pl.BlockSpec((None, k, ns), lambda j, i: (j * stride + layer, 0, 0)),
    ]
    args = [a, w]
    if has_bias:
        in_specs.append(pl.BlockSpec((1, ns), lambda j, i: (0, j)))
        args.append(bias)
    return pl.pallas_call(
        body,
        name=name,
        grid=(nsh, m // tm),
        in_specs=in_specs,
        out_specs=pl.BlockSpec((tm, ns), lambda j, i: (i, j)),
        out_shape=jax.ShapeDtypeStruct((m, nsh * ns), out_dtype),
        compiler_params=_cp("parallel", "parallel"),
    )(*args)


def mm_nt(dy, w, nsh, stride, layer, out_dtype=F32, name="mm_nt", tm=1024):
    dys = list(dy) if isinstance(dy, (list, tuple)) else [dy]
    npart = len(dys)
    per = nsh // npart
    m = dys[0].shape[0]
    _, k, ns = w.shape
    assert all(d.shape == (m, per * ns) for d in dys)
    tm = _row_tile(m, tm)

    def body(*refs):
        dy_refs = refs[:npart]
        w_ref, o_ref, acc_ref = refs[npart:]
        j = pl.program_id(1)

        @pl.when(j == 0)
        def _():
            acc_ref[...] = jnp.zeros(acc_ref.shape, F32)

        for pi in range(npart):
            @pl.when(j // per == pi)
            def _(pi=pi):
                acc_ref[...] += _dot_nt(dy_refs[pi][...].astype(BF16), w_ref[...])

        @pl.when(j == nsh - 1)
        def _():
            o_ref[...] = acc_ref[...].astype(out_dtype)

    dy_specs = [pl.BlockSpec((tm, ns), lambda i, j, pi=pi: (i, jnp.clip(j - pi * per, 0, per - 1))) for pi in range(npart)]
    return pl.pallas_call(
        body,
        name=name,
        grid=(m // tm, nsh),
        in_specs=[*dy_specs, pl.BlockSpec((None, k, ns), lambda i, j: (j * stride + layer, 0, 0))],
        out_specs=pl.BlockSpec((tm, k), lambda i, j: (i, 0)),
        out_shape=jax.ShapeDtypeStruct((m, k), out_dtype),
        scratch_shapes=[pltpu.VMEM((tm, k), F32)],
        compiler_params=_cp("parallel", "arbitrary"),
    )(*dys, w)


def mm_tn(a, dy, nsh, name="mm_tn", tm=512):
    dys = list(dy) if isinstance(dy, (list, tuple)) else [dy]
    npart = len(dys)
    per = nsh // npart
    m, k = a.shape
    ns = dys[0].shape[1] // per
    assert all(d.shape == (m, per * ns) for d in dys)
    tm = _row_tile(m, tm)
    nt = m // tm

    def body(*refs):
        a_ref = refs[0]
        dy_refs = refs[1:1 + npart]
        o_ref, acc_ref = refs[1 + npart:]
        j = pl.program_id(0)
        i = pl.program_id(1)

        @pl.when(i == 0)
        def _():
            acc_ref[...] = jnp.zeros(acc_ref.shape, F32)

        for pi in range(npart):
            @pl.when(j // per == pi)
            def _(pi=pi):
                acc_ref[...] += _dot_tn(a_ref[...].astype(BF16), dy_refs[pi][...].astype(BF16))

        @pl.when(i == nt - 1)
        def _():
            o_ref[...] = acc_ref[...].astype(BF16)

    dy_specs = [
        pl.BlockSpec((tm, ns), lambda j, i, pi=pi: (jnp.where(j // per == pi, i, 0), jnp.clip(j - pi * per, 0, per - 1)))
        for pi in range(npart)
    ]
    return pl.pallas_call(
        body,
        name=name,
        grid=(nsh, nt),
        in_specs=[pl.BlockSpec((tm, k), lambda j, i: (i, 0)), *dy_specs],
        out_specs=pl.BlockSpec((None, k, ns), lambda j, i: (j, 0, 0)),
        out_shape=jax.ShapeDtypeStruct((nsh, k, ns), BF16),
        scratch_shapes=[pltpu.VMEM((k, ns), F32)],
        compiler_params=_cp("parallel", "arbitrary"),
    )(a, *dys)


DEP_SPEC_SHAPE = (8, LANES)


def resid_norm_fwd(x, y, g_post, next_gains, name, tm=256, dep=None):
    t, d = x.shape
    tm = _row_tile(t, tm)
    has_y = y is not None
    n_next = len(next_gains)
    n_dep = 0 if dep is None else 1

    def body(*refs):
        x_ref = refs[0]
        pos = 1
        if has_y:
            y_ref, gp_ref = refs[1], refs[2]
            pos = 3
        gn_refs = refs[pos:pos + n_next]
        outs = refs[pos + n_next + n_dep:]
        xv = x_ref[...]
        o = 0
        if has_y:
            yv = y_ref[...]
            r = lax.rsqrt(jnp.mean(yv * yv, axis=-1, keepdims=True) + EPS)
            xv = xv + (yv * r) * gp_ref[...]
            outs[0][...] = xv
            o = 1
        if n_next:
            xn = xv * lax.rsqrt(jnp.mean(xv * xv, axis=-1, keepdims=True) + EPS)
            for k in range(n_next):
                outs[o + k][...] = (xn * gn_refs[k][...]).astype(BF16)

    row = pl.BlockSpec((tm, d), lambda i: (i, 0))
    vec = pl.BlockSpec((1, d), lambda i: (0, 0))
    args, in_specs = [x], [row]
    if has_y:
        args += [y, g_post]
        in_specs += [row, vec]
    args += list(next_gains)
    in_specs += [vec] * n_next
    if n_dep:
        args.append(dep)
        in_specs.append(pl.BlockSpec(DEP_SPEC_SHAPE, lambda i: (0, 0)))
    out_shape, out_specs = [], []
    if has_y:
        out_shape.append(jax.ShapeDtypeStruct((t, d), F32))
        out_specs.append(row)
    for _ in range(n_next):
        out_shape.append(jax.ShapeDtypeStruct((t, d), BF16))
        out_specs.append(row)
    return pl.pallas_call(
        body, name=name, grid=(t // tm,), in_specs=in_specs, out_specs=out_specs, out_shape=out_shape,
        compiler_params=_cp("parallel"),
    )(*args)


def norm_bwd(x, g, dy, add=None, out_dtype=F32, name="norm_bwd", tm=256, dep=None):
    t, d = x.shape
    tm = _row_tile(t, tm)
    has_add = add is not None

    def body(*refs):
        x_ref, g_ref, dy_ref = refs[:3]
        add_ref = refs[3] if has_add else None
        dx_ref, dg_ref, cs_ref = refs[-3:]
        i = pl.program_id(0)
        xv = x_ref[...]
        dyv = dy_ref[...].astype(F32)
        r = lax.rsqrt(jnp.mean(xv * xv, axis=-1, keepdims=True) + EPS)
        gd = dyv * g_ref[...]
        dx = r * gd - xv * ((r * r * r) * jnp.mean(xv * gd, axis=-1, keepdims=True))
        if has_add:
            dx = dx + add_ref[...]
        dx_ref[...] = dx.astype(out_dtype)
        dg = jnp.sum(dyv * (xv * r), axis=0, keepdims=True)
        cs = jnp.sum(dx, axis=0, keepdims=True)

        @pl.when(i == 0)
        def _():
            dg_ref[...] = dg
            cs_ref[...] = cs

        @pl.when(i > 0)
        def _():
            dg_ref[...] += dg
            cs_ref[...] += cs

    row = pl.BlockSpec((tm, d), lambda i: (i, 0))
    vec = pl.BlockSpec((1, d), lambda i: (0, 0))
    args, in_specs = [x, g, dy], [row, vec, row]
    if has_add:
        args.append(add)
        in_specs.append(row)
    if dep is not None:
        args.append(dep)
        in_specs.append(pl.BlockSpec(DEP_SPEC_SHAPE, lambda i: (0, 0)))
    return pl.pallas_call(
        body, name=name, grid=(t // tm,), in_specs=in_specs,
        out_specs=[row, vec, vec],
        out_shape=[jax.ShapeDtypeStruct((t, d), out_dtype), jax.ShapeDtypeStruct((1, d), F32),
                   jax.ShapeDtypeStruct((1, d), F32)],
        compiler_params=_cp("arbitrary"),
    )(*args)


def loss_fwd_bwd(x, target, name="loss", tm=256):
    t, d = x.shape
    tm = _row_tile(t, tm)

    def body(x_ref, t_ref, dx_ref, l_ref):
        i = pl.program_id(0)
        err = x_ref[...] - t_ref[...]
        dx_ref[...] = err * (1.0 / d)
        part = 0.5 * jnp.sum(jnp.mean(err * err, axis=-1, keepdims=True), axis=0, keepdims=True)
        part = jnp.broadcast_to(part, l_ref.shape)

        @pl.when(i == 0)
        def _():
            l_ref[...] = part

        @pl.when(i > 0)
        def _():
            l_ref[...] += part

    row = pl.BlockSpec((tm, d), lambda i: (i, 0))
    return pl.pallas_call(
        body, name=name, grid=(t // tm,), in_specs=[row, row],
        out_specs=[row, pl.BlockSpec((8, LANES), lambda i: (0, 0))],
        out_shape=[jax.ShapeDtypeStruct((t, d), F32), jax.ShapeDtypeStruct((8, LANES), F32)],
        compiler_params=_cp("arbitrary"),
    )(x, target)


CONV_HALO = 32
CONV_CHUNK = 128


def glu_conv_fwd(z, dw, dwb, name, tc=128):
    b, s, c2 = z.shape
    c = c2 // 2
    kw = dw.shape[0]
    tc = min(tc, c)
    nc = c // tc
    ch = min(CONV_CHUNK, s)
    halo = CONV_HALO
    assert kw - 1 <= halo and s % ch == 0

    nch = s // ch

    def body(a_ref, g_ref, w_ref, b_ref, o_ref, pad_ref):
        _fill_glu_slabs(a_ref, g_ref, pad_ref, nch, ch, halo)

        def chunk(ci, carry):
            r0 = pl.multiple_of(ci * ch, ch)
            acc = b_ref[...]
            for k, tap in enumerate(_taps_front(pad_ref, ci, kw, ch, halo)):
                acc = acc + w_ref[k:k + 1, :] * tap
            o_ref[pl.ds(r0, ch), :] = acc
            return carry

        lax.fori_loop(0, nch, chunk, 0)

    return pl.pallas_call(
        body, name=name, grid=(b, nc),
        in_specs=[
            pl.BlockSpec((None, s, tc), lambda bi, i: (bi, 0, i)),
            pl.BlockSpec((None, s, tc), lambda bi, i: (bi, 0, i + nc)),
            pl.BlockSpec((kw, tc), lambda bi, i: (0, i)),
            pl.BlockSpec((1, tc), lambda bi, i: (0, i)),
        ],
        out_specs=pl.BlockSpec((None, s, tc), lambda bi, i: (bi, 0, i)),
        out_shape=jax.ShapeDtypeStruct((b, s, c), F32),
        scratch_shapes=[pltpu.VMEM((nch, ch + halo, tc), F32)],
        compiler_params=_cp("parallel", "parallel"),
    )(z, z, dw, dwb)


def glu_conv_bwd(z, dw, du2, name, tc=128):
    b, s, c2 = z.shape
    c = c2 // 2
    kw = dw.shape[0]
    tc = min(tc, c)
    nc = c // tc
    ch = min(CONV_CHUNK, s)
    nch = s // ch
    halo = CONV_HALO

    def body(a_ref, g_ref, w_ref, du_ref, dza_ref, dzg_ref, ddw_ref, ddwb_ref, dba_ref, dbg_ref, upad_ref, dpad_ref):
        bi = pl.program_id(1)

        @pl.when(bi == 0)
        def _():
            ddw_ref[...] = jnp.zeros(ddw_ref.shape, F32)
            ddwb_ref[...] = jnp.zeros(ddwb_ref.shape, F32)
            dba_ref[...] = jnp.zeros(dba_ref.shape, F32)
            dbg_ref[...] = jnp.zeros(dbg_ref.shape, F32)

        _fill_glu_slabs(a_ref, g_ref, upad_ref, nch, ch, halo)
        dpad_ref[nch - 1, ch:ch + halo, :] = jnp.zeros((halo, tc), F32)
        dpad_ref[nch - 1, 0:ch, :] = du_ref[s - ch:s, :]

        def fill(ci, carry):
            r0 = pl.multiple_of(ci * ch, ch)
            dpad_ref[ci, :, :] = du_ref[pl.ds(r0, ch + halo), :]
            return carry

        lax.fori_loop(0, nch - 1, fill, 0)

        def chunk(ci, carry):
            r0 = pl.multiple_of(ci * ch, ch)
            du_c = du_ref[pl.ds(r0, ch), :]
            taps_u = _taps_front(upad_ref, ci, kw, ch, halo)
            du1 = w_ref[kw - 1:kw, :] * du_c
            ddw_ref[kw - 1] += jnp.sum((du_c * taps_u[kw - 1]).reshape(ch // 8, 8, tc), axis=0)
            for j in range(1, kw):
                du1 = du1 + w_ref[kw - 1 - j:kw - j, :] * dpad_ref[ci, j:j + ch, :]
                ddw_ref[kw - 1 - j] += jnp.sum((du_c * taps_u[kw - 1 - j]).reshape(ch // 8, 8, tc), axis=0)
            av = a_ref[pl.ds(r0, ch), :]
            sg = _sigmoid(g_ref[pl.ds(r0, ch), :])
            dza = du1 * sg
            dzg = du1 * av * (sg * (1.0 - sg))
            dza_ref[pl.ds(r0, ch), :] = dza.astype(BF16)
            dzg_ref[pl.ds(r0, ch), :] = dzg.astype(BF16)
            dba_ref[...] += jnp.sum(dza, axis=0, keepdims=True)
            dbg_ref[...] += jnp.sum(dzg, axis=0, keepdims=True)
            ddwb_ref[...] += jnp.sum(du_c, axis=0, keepdims=True)
            return carry

        lax.fori_loop(0, s // ch, chunk, 0)

    blk = lambda off: pl.BlockSpec((None, s, tc), lambda i, bi: (bi, 0, i + off))
    vec = pl.BlockSpec((1, tc), lambda i, bi: (0, i))
    return pl.pallas_call(
        body, name=name, grid=(nc, b),
        in_specs=[blk(0), blk(nc), pl.BlockSpec((kw, tc), lambda i, bi: (0, i)), blk(0)],
        out_specs=[blk(0), blk(0), pl.BlockSpec((kw, 8, tc), lambda i, bi: (0, 0, i)), vec, vec, vec],
        out_shape=[
            jax.ShapeDtypeStruct((b, s, c), BF16), jax.ShapeDtypeStruct((b, s, c), BF16),
            jax.ShapeDtypeStruct((kw, 8, c), F32), jax.ShapeDtypeStruct((1, c), F32),
            jax.ShapeDtypeStruct((1, c), F32), jax.ShapeDtypeStruct((1, c), F32),
        ],
        scratch_shapes=[pltpu.VMEM((nch, ch + halo, tc), F32), pltpu.VMEM((nch, ch + halo, tc), F32)],
        compiler_params=_cp("parallel", "arbitrary"),
    )(z, z, dw, du2)


def _fill_glu_slabs(a_ref, g_ref, pad_ref, nch, ch, halo):
    tc = a_ref.shape[-1]
    pad_ref[0, 0:halo, :] = jnp.zeros((halo, tc), F32)
    pad_ref[0, halo:halo + ch, :] = a_ref[0:ch, :] * _sigmoid(g_ref[0:ch, :])

    def fill(ci, carry):
        r0 = pl.multiple_of(ci * ch, ch)
        pad_ref[ci, 0:halo, :] = pad_ref[ci - 1, ch:ch + halo, :]
        pad_ref[ci, halo:halo + ch, :] = a_ref[pl.ds(r0, ch), :] * _sigmoid(g_ref[pl.ds(r0, ch), :])
        return carry

    lax.fori_loop(1, nch, fill, 0)


def ln_silu_fwd(u, g, bvec, name, tm=256):
    t, d = u.shape
    tm = _row_tile(t, tm)

    def body(u_ref, g_ref, b_ref, o_ref):
        uv = u_ref[...]
        mu = jnp.mean(uv, axis=-1, keepdims=True)
        xc = uv - mu
        var = jnp.mean(xc * xc, axis=-1, keepdims=True)
        v = (xc * lax.rsqrt(var + EPS)) * g_ref[...] + b_ref[...]
        o_ref[...] = (v * _sigmoid(v)).astype(BF16)

    row = pl.BlockSpec((tm, d), lambda i: (i, 0))
    vec = pl.BlockSpec((1, d), lambda i: (0, 0))
    return pl.pallas_call(
        body, name=name, grid=(t // tm,), in_specs=[row, vec, vec], out_specs=row,
        out_shape=jax.ShapeDtypeStruct((t, d), BF16), compiler_params=_cp("parallel"),
    )(u, g, bvec)


def ln_silu_bwd(u, g, bvec, dout, name, tm=256):
    t, d = u.shape
    tm = _row_tile(t, tm)

    def body(u_ref, g_ref, b_ref, do_ref, du_ref, dg_ref, db_ref):
        i = pl.program_id(0)
        uv = u_ref[...]
        mu = jnp.mean(uv, axis=-1, keepdims=True)
        xc = uv - mu
        var = jnp.mean(xc * xc, axis=-1, keepdims=True)
        rstd = lax.rsqrt(var + EPS)
        n = xc * rstd
        v = n * g_ref[...] + b_ref[...]
        sg = _sigmoid(v)
        dv = do_ref[...].astype(F32) * (sg * (1.0 + v * (1.0 - sg)))
        dn = dv * g_ref[...]
        du_ref[...] = rstd * (dn - jnp.mean(dn, axis=-1, keepdims=True) - n * jnp.mean(dn * n, axis=-1, keepdims=True))
        dg = jnp.sum(dv * n, axis=0, keepdims=True)
        db = jnp.sum(dv, axis=0, keepdims=True)

        @pl.when(i == 0)
        def _():
            dg_ref[...] = dg
            db_ref[...] = db

        @pl.when(i > 0)
        def _():
            dg_ref[...] += dg
            db_ref[...] += db

    row = pl.BlockSpec((tm, d), lambda i: (i, 0))
    vec = pl.BlockSpec((1, d), lambda i: (0, 0))
    return pl.pallas_call(
        body, name=name, grid=(t // tm,), in_specs=[row, vec, vec, row], out_specs=[row, vec, vec],
        out_shape=[jax.ShapeDtypeStruct((t, d), F32), jax.ShapeDtypeStruct((1, d), F32), jax.ShapeDtypeStruct((1, d), F32)],
        compiler_params=_cp("arbitrary"),
    )(u, g, bvec, dout)


FFN_HALO = 8


def _fill_front_halo(src_ref, pad_ref, nch, ch, halo):
    tc = src_ref.shape[-1]
    pad_ref[0, 0:halo, :] = jnp.zeros((halo, tc), F32)
    pad_ref[0, halo:halo + ch, :] = src_ref[0:ch, :].astype(F32)

    def fill(ci, carry):
        r0 = pl.multiple_of(ci * ch, ch)
        pad_ref[ci, 0:halo, :] = src_ref[pl.ds(r0 - 2 * halo, 2 * halo), :].astype(F32)[halo:, :]
        pad_ref[ci, halo:halo + ch, :] = src_ref[pl.ds(r0, ch), :].astype(F32)
        return carry

    lax.fori_loop(1, nch, fill, 0)


def _taps_front(pad_ref, ci, kw, ch, halo):
    return [pad_ref[ci, halo - (kw - 1 - k):halo - (kw - 1 - k) + ch, :] for k in range(kw)]


def ffn_mid_fwd(p, dw, dwb, name, tc=256):
    b, s, f2 = p.shape
    f = f2 // 2
    kw = dw.shape[0]
    tc = min(tc, f)
    nf = f // tc
    ch = min(CONV_CHUNK, s)
    nch = s // ch
    halo = FFN_HALO

    def body(pa_ref, pg_ref, wa_ref, wg_ref, ba_ref, bg_ref, o_ref, apad_ref, gpad_ref):
        _fill_front_halo(pa_ref, apad_ref, nch, ch, halo)
        _fill_front_halo(pg_ref, gpad_ref, nch, ch, halo)

        def chunk(ci, carry):
            r0 = pl.multiple_of(ci * ch, ch)
            ca = ba_ref[...]
            cg = bg_ref[...]
            taps = zip(_taps_front(apad_ref, ci, kw, ch, halo), _taps_front(gpad_ref, ci, kw, ch, halo))
            for k, (ta, tg) in enumerate(taps):
                ca = ca + wa_ref[k:k + 1, :] * ta
                cg = cg + wg_ref[k:k + 1, :] * tg
            o_ref[pl.ds(r0, ch), :] = ((cg * _sigmoid(cg)) * ca).astype(BF16)
            return carry

        lax.fori_loop(0, nch, chunk, 0)

    blk = lambda off: pl.BlockSpec((None, s, tc), lambda bi, i: (bi, 0, i + off))
    wsp = lambda off: pl.BlockSpec((kw, tc), lambda bi, i: (0, i + off))
    bsp = lambda off: pl.BlockSpec((1, tc), lambda bi, i: (0, i + off))
    return pl.pallas_call(
        body, name=name, grid=(b, nf),
        in_specs=[blk(0), blk(nf), wsp(0), wsp(nf), bsp(0), bsp(nf)],
        out_specs=pl.BlockSpec((None, s, tc), lambda bi, i: (bi, 0, i)),
        out_shape=jax.ShapeDtypeStruct((b, s, f), BF16),
        scratch_shapes=[pltpu.VMEM((nch, ch + halo, tc), F32)] * 2,
        compiler_params=_cp("parallel", "parallel"),
    )(p, p, dw, dw, dwb, dwb)


def ffn_mid_bwd(p, dw, dwb, ds, name, tc=256):
    b, s, f2 = p.shape
    f = f2 // 2
    kw = dw.shape[0]
    tc = min(tc, f)
    nf = f // tc
    ch = min(CONV_CHUNK, s)
    nch = s // ch
    halo = FFN_HALO

    def sum8(v):
        return jnp.sum(v.reshape(ch // 8, 8, tc), axis=0)

    def body(pa_ref, pg_ref, wa_ref, wg_ref, ba_ref, bg_ref, ds_ref, dpa_ref, dpg_ref, ddwa_ref, ddwg_ref, dba_ref, dbg_ref,
             apad_ref, gpad_ref, dca_ref, dcg_ref):
        bi = pl.program_id(1)

        @pl.when(bi == 0)
        def _():
            ddwa_ref[...] = jnp.zeros(ddwa_ref.shape, F32)
            ddwg_ref[...] = jnp.zeros(ddwg_ref.shape, F32)
            dba_ref[...] = jnp.zeros(dba_ref.shape, F32)
            dbg_ref[...] = jnp.zeros(dbg_ref.shape, F32)

        _fill_front_halo(pa_ref, apad_ref, nch, ch, halo)
        _fill_front_halo(pg_ref, gpad_ref, nch, ch, halo)
        dca_ref[nch - 1, ch:ch + halo, :] = jnp.zeros((halo, tc), F32)
        dcg_ref[nch - 1, ch:ch + halo, :] = jnp.zeros((halo, tc), F32)

        def grads(ci, carry):
            acc_a, acc_g, sb_a, sb_g = carry
            r0 = pl.multiple_of(ci * ch, ch)
            taps_a = _taps_front(apad_ref, ci, kw, ch, halo)
            taps_g = _taps_front(gpad_ref, ci, kw, ch, halo)
            ca = ba_ref[...]
            cg = bg_ref[...]
            for k in range(kw):
                ca = ca + wa_ref[k:k + 1, :] * taps_a[k]
                cg = cg + wg_ref[k:k + 1, :] * taps_g[k]
            sg = _sigmoid(cg)
            dsv = ds_ref[pl.ds(r0, ch), :].astype(F32)
            dca = dsv * (cg * sg)
            dcg = dsv * ca * (sg * (1.0 + cg * (1.0 - sg)))
            dca_ref[ci, 0:ch, :] = dca
            dcg_ref[ci, 0:ch, :] = dcg

            prev = jnp.maximum(ci - 1, 0)

            @pl.when(ci > 0)
            def _():
                dca_ref[prev, ch:ch + halo, :] = dca[0:halo, :]
                dcg_ref[prev, ch:ch + halo, :] = dcg[0:halo, :]

            acc_a = tuple(acc_a[k] + sum8(dca * taps_a[k]) for k in range(kw))
            acc_g = tuple(acc_g[k] + sum8(dcg * taps_g[k]) for k in range(kw))
            return acc_a, acc_g, sb_a + sum8(dca), sb_g + sum8(dcg)

        z8 = jnp.zeros((8, tc), F32)
        acc_a, acc_g, sb_a, sb_g = lax.fori_loop(0, nch, grads, ((z8,) * kw, (z8,) * kw, z8, z8))
        for k in range(kw):
            ddwa_ref[k] += acc_a[k]
            ddwg_ref[k] += acc_g[k]
        dba_ref[...] += jnp.sum(sb_a, axis=0, keepdims=True)
        dbg_ref[...] += jnp.sum(sb_g, axis=0, keepdims=True)

        def back(ci, carry):
            r0 = pl.multiple_of(ci * ch, ch)
            da = wa_ref[kw - 1:kw, :] * dca_ref[ci, 0:ch, :]
            dg = wg_ref[kw - 1:kw, :] * dcg_ref[ci, 0:ch, :]
            for j in range(1, kw):
                da = da + wa_ref[kw - 1 - j:kw - j, :] * dca_ref[ci, j:j + ch, :]
                dg = dg + wg_ref[kw - 1 - j:kw - j, :] * dcg_ref[ci, j:j + ch, :]
            dpa_ref[pl.ds(r0, ch), :] = da.astype(BF16)
            dpg_ref[pl.ds(r0, ch), :] = dg.astype(BF16)
            return carry

        lax.fori_loop(0, nch, back, 0)

    blk = lambda off: pl.BlockSpec((None, s, tc), lambda i, bi: (bi, 0, i + off))
    wsp = lambda off: pl.BlockSpec((kw, tc), lambda i, bi: (0, i + off))
    bsp = lambda off: pl.BlockSpec((1, tc), lambda i, bi: (0, i + off))
    acc3 = pl.BlockSpec((kw, 8, tc), lambda i, bi: (0, 0, i))
    vec = pl.BlockSpec((1, tc), lambda i, bi: (0, i))
    return pl.pallas_call(
        body, name=name, grid=(nf, b),
        in_specs=[blk(0), blk(nf), wsp(0), wsp(nf), bsp(0), bsp(nf), blk(0)],
        out_specs=[blk(0), blk(0), acc3, acc3, vec, vec],
        out_shape=[jax.ShapeDtypeStruct((b, s, f), BF16), jax.ShapeDtypeStruct((b, s, f), BF16),
                   jax.ShapeDtypeStruct((kw, 8, f), F32), jax.ShapeDtypeStruct((kw, 8, f), F32),
                   jax.ShapeDtypeStruct((1, f), F32), jax.ShapeDtypeStruct((1, f), F32)],
        scratch_shapes=[pltpu.VMEM((nch, ch + halo, tc), F32)] * 4,
        compiler_params=_cp("parallel", "arbitrary"),
    )(p, p, dw, dw, dwb, dwb, ds)


def _tile_rows(r, n, dil):
    start = r + n * BLK * dil
    return pl.ds(start, BLK, stride=dil) if dil > 1 else pl.ds(start, BLK)


def _band_masks():
    qi = lax.broadcasted_iota(jnp.int32, (BLK, 2 * BLK), 0)
    kk = lax.broadcasted_iota(jnp.int32, (BLK, 2 * BLK), 1)
    both = jnp.logical_or(jnp.logical_and(kk < BLK, kk >= qi), jnp.logical_and(kk >= BLK, kk - BLK <= qi))
    return both, kk[:, :BLK] <= qi[:, :BLK]


def attn_fwd(q, kv, g, dil, hw, name):
    b, s, _ = q.shape
    nh = hw // HEAD_DIM
    nblk = s // dil // BLK
    scale = 1.0 / math.sqrt(HEAD_DIM)

    def body(q_ref, k_ref, v_ref, o_ref, lse_ref):
        mask2, mask1 = _band_masks()
        for r in range(dil):
            kp = vp = None
            for n in range(nblk):
                rs = _tile_rows(r, n, dil)
                qt = q_ref[rs, :].astype(BF16)
                kc = k_ref[rs, :].astype(BF16)
                vc = v_ref[rs, :].astype(BF16)
                if n == 0:
                    kcat, vcat, mask = kc, vc, mask1
                else:
                    kcat, vcat, mask = jnp.concatenate([kp, kc], axis=0), jnp.concatenate([vp, vc], axis=0), mask2
                sc = jnp.where(mask, _dot_nt(qt, kcat) * scale, NEG_INF)
                m = jnp.max(sc, axis=-1, keepdims=True)
                p = jnp.exp(sc - m)
                den = jnp.sum(p, axis=-1, keepdims=True)
                o_ref[rs, :] = _dot(p.astype(BF16), vcat) / den
                lse_ref[rs, :] = jnp.broadcast_to(m + jnp.log(den), (BLK, HEAD_DIM))
                kp, vp = kc, vc

    col = lambda base: pl.BlockSpec((None, s, HEAD_DIM), lambda bi, h: (bi, 0, base + h))
    return pl.pallas_call(
        body, name=name, grid=(b, nh),
        in_specs=[col(g * nh), col(g * nh), col((N_GROUPS + g) * nh)],
        out_specs=[col(0), col(0)],
        out_shape=[jax.ShapeDtypeStruct((b, s, hw), F32), jax.ShapeDtypeStruct((b, s, hw), F32)],
        compiler_params=_cp("parallel", "parallel"),
    )(q, kv, kv)


def attn_merge(outs, lses, name, tm=256):
    t, hw = outs[0].shape
    tm = _row_tile(t, tm)
    ng = len(outs)

    def body(*refs):
        o_refs, l_refs = refs[:ng], refs[ng:2 * ng]
        m_ref, lj_ref = refs[2 * ng:]
        ls = [l_refs[g][...] for g in range(ng)]
        mx = ls[0]
        for g in range(1, ng):
            mx = jnp.maximum(mx, ls[g])
        es = [jnp.exp(l - mx) for l in ls]
        tot = es[0]
        for g in range(1, ng):
            tot = tot + es[g]
        acc = (es[0] / tot) * o_refs[0][...]
        for g in range(1, ng):
            acc = acc + (es[g] / tot) * o_refs[g][...]
        m_ref[...] = acc.astype(BF16)
        lj_ref[...] = mx + jnp.log(tot)

    row = pl.BlockSpec((tm, hw), lambda i: (i, 0))
    return pl.pallas_call(
        body, name=name, grid=(t // tm,), in_specs=[row] * (2 * ng), out_specs=[row, row],
        out_shape=[jax.ShapeDtypeStruct((t, hw), BF16), jax.ShapeDtypeStruct((t, hw), F32)],
        compiler_params=_cp("parallel"),
    )(*outs, *lses)


def attn_bwd_prep(dmerged, merged, name, tm=256):
    t, hw = merged.shape
    nh = hw // HEAD_DIM
    tm = _row_tile(t, tm)

    def body(d_ref, m_ref, o_ref):
        for h in range(nh):
            sl = slice(h * HEAD_DIM, (h + 1) * HEAD_DIM)
            dsum = jnp.sum(d_ref[:, sl] * m_ref[:, sl].astype(F32), axis=-1, keepdims=True)
            o_ref[:, sl] = jnp.broadcast_to(dsum, (tm, HEAD_DIM))

    row = pl.BlockSpec((tm, hw), lambda i: (i, 0))
    return pl.pallas_call(
        body, name=name, grid=(t // tm,), in_specs=[row, row], out_specs=row,
        out_shape=jax.ShapeDtypeStruct((t, hw), F32), compiler_params=_cp("parallel"),
    )(dmerged, merged)


def attn_bwd(q, kv, g, dil, do, lsej, dm, dq_buf, dk_buf, dv_buf, accumulate, hw, name):
    b, s, _ = q.shape
    nh = hw // HEAD_DIM
    nblk = s // dil // BLK
    scale = 1.0 / math.sqrt(HEAD_DIM)
    assert dk_buf is not None or not accumulate
    kv_at = 6 + (dq_buf is not None)

    def body(*refs):
        q_ref, k_ref, v_ref, do_ref, lj_ref, dm_ref = refs[:6]
        dq_ref, dk_ref, dv_ref = refs[-3:]
        dki_ref, dvi_ref = (refs[kv_at], refs[kv_at + 1]) if accumulate else (None, None)
        mask2, mask1 = _band_masks()

        def put(rs, dk, dv):
            if accumulate:
                dk = dk + dki_ref[rs, :]
                dv = dv + dvi_ref[rs, :]
            dk_ref[rs, :] = dk
            dv_ref[rs, :] = dv

        for r in range(dil):
            kp = vp = hold_k = hold_v = rs_prev = None
            for n in range(nblk):
                rs = _tile_rows(r, n, dil)
                qt = q_ref[rs, :].astype(BF16)
                kc = k_ref[rs, :].astype(BF16)
                vc = v_ref[rs, :].astype(BF16)
                dot = do_ref[rs, :].astype(BF16)
                lm = lj_ref[rs, :]
                dmm = dm_ref[rs, :]
                if n == 0:
                    kcat, vcat, mask = kc, vc, mask1
                else:
                    kcat, vcat, mask = jnp.concatenate([kp, kc], axis=0), jnp.concatenate([vp, vc], axis=0), mask2
                    lm = jnp.concatenate([lm, lm], axis=1)
                    dmm = jnp.concatenate([dmm, dmm], axis=1)
                p = jnp.exp(jnp.where(mask, _dot_nt(qt, kcat) * scale, NEG_INF) - lm)
                ds = (p * (_dot_nt(dot, vcat) - dmm)).astype(BF16)
                dq_ref[rs, :] = _dot(ds, kcat) * scale
                dkc = _dot_tn(ds, qt) * scale
                dvc = _dot_tn(p.astype(BF16), dot)
                if n > 0:
                    put(rs_prev, hold_k + dkc[:BLK, :], hold_v + dvc[:BLK, :])
                    dkc, dvc = dkc[BLK:, :], dvc[BLK:, :]
                hold_k, hold_v, kp, vp, rs_prev = dkc, dvc, kc, vc, rs
            put(rs_prev, hold_k, hold_v)

    col = lambda base: pl.BlockSpec((None, s, HEAD_DIM), lambda bi, h: (bi, 0, base + h))
    any_spec = pl.BlockSpec(memory_space=pl.ANY)
    in_specs = [col(g * nh), col(g * nh), col((N_GROUPS + g) * nh), col(0), col(0), col(0)]
    args = [q, kv, kv, do, lsej, dm]
    aliases = {}
    if dq_buf is not None:
        in_specs.append(any_spec)
        args.append(dq_buf)
        aliases[6] = 0
    if dk_buf is not None:
        in_specs += [col(g * nh) if accumulate else any_spec] * 2
        args += [dk_buf, dv_buf]
        aliases.update({kv_at: 1, kv_at + 1: 2})
    shape = jax.ShapeDtypeStruct((b, s, N_GROUPS * hw), F32)
    return pl.pallas_call(
        body, name=name, grid=(b, nh), in_specs=in_specs, out_specs=[col(g * nh)] * 3, out_shape=[shape] * 3,
        input_output_aliases=aliases, compiler_params=_cp("parallel", "parallel"),
    )(*args)


def sum_parts(g, recv, me, name, tm=256):
    _, rows, c = g.shape
    n = recv.shape[0]
    tm = _row_tile(rows, tm)

    def body(me_ref, g_ref, r_ref, o_ref):
        acc = g_ref[...].astype(F32)
        for j in range(n):
            acc = acc + r_ref[j].astype(F32)
        o_ref[...] = acc

    return pl.pallas_call(
        body, name=name,
        grid_spec=pltpu.PrefetchScalarGridSpec(
            num_scalar_prefetch=1, grid=(rows // tm,),
            in_specs=[pl.BlockSpec((None, tm, c), lambda i, me_ref: (me_ref[0], i, 0)),
                      pl.BlockSpec((n, tm, c), lambda i, me_ref: (0, i, 0))],
            out_specs=pl.BlockSpec((tm, c), lambda i, me_ref: (i, 0))),
        out_shape=jax.ShapeDtypeStruct((rows, c), F32), compiler_params=_cp("parallel"),
    )(me, g, recv)


def adamw(w, m, v, g_parts, name, tm=256):
    rows, c = w.shape
    tm = _row_tile(rows, tm)
    npart = len(g_parts)

    def body(*refs):
        w_ref, m_ref, v_ref = refs[:3]
        g_refs = refs[3:3 + npart]
        go_ref, d_ref, mo_ref, vo_ref = refs[3 + npart:]
        g = g_refs[0][...]
        for k in range(1, npart):
            g = g + g_refs[k][...]
        mn = ADAM_B1 * m_ref[...] + (1.0 - ADAM_B1) * g
        vn = ADAM_B2 * v_ref[...] + (1.0 - ADAM_B2) * (g * g)
        m_hat = mn / (1.0 - ADAM_B1 ** ADAM_STEP)
        v_hat = vn / (1.0 - ADAM_B2 ** ADAM_STEP)
        go_ref[...] = g
        d_ref[...] = -ADAM_LR * (m_hat / (jnp.sqrt(v_hat) + ADAM_EPS) + ADAM_WD * w_ref[...])
        mo_ref[...] = mn
        vo_ref[...] = vn

    row = pl.BlockSpec((tm, c), lambda i: (i, 0))
    return pl.pallas_call(
        body, name=name, grid=(rows // tm,), in_specs=[row] * (3 + npart), out_specs=[row] * 4,
        out_shape=[jax.ShapeDtypeStruct((rows, c), F32)] * 4, compiler_params=_cp("parallel"),
    )(w, m, v, *g_parts)


def _place():
    return lax.axis_index("x"), lax.axis_index("y"), lax.axis_index("c")


def _other_chips(x, y, c):
    return [(1 - x, y, c), (x, 1 - y, c), (1 - x, 1 - y, c)]


def _chip_of(px, py):
    return 2 * px + py


HBM_SPEC = pl.BlockSpec(memory_space=pltpu.HBM)
SEM_SPEC = pl.BlockSpec(memory_space=pltpu.SEMAPHORE)
ANY_SPEC = pl.BlockSpec(memory_space=pl.ANY)
DATAFLOW = pltpu.SideEffectType.DATAFLOW_SIDE_EFFECTING
N_PEER_CHIPS = N_CHIPS - 1


def _hbm(a):
    return pltpu.with_memory_space_constraint(a, pltpu.HBM)


def _hbm_like(arrays):
    return [pltpu.HBM(a.shape, a.dtype) for a in arrays]


def cast_place(w, layer, me, out_dtype, name, tm=256):
    rows, c = w.shape[-2:]
    tm = _row_tile(rows, tm)

    def body(me_ref, w_ref, o_ref):
        o_ref[...] = w_ref[...].astype(out_dtype)

    if layer is None:
        in_spec = pl.BlockSpec((tm, c), lambda i, me_ref: (i, 0))
    else:
        in_spec = pl.BlockSpec((None, tm, c), lambda i, me_ref: (layer, i, 0))
    return pl.pallas_call(
        body, name=name,
        grid_spec=pltpu.PrefetchScalarGridSpec(
            num_scalar_prefetch=1, grid=(rows // tm,), in_specs=[in_spec],
            out_specs=pl.BlockSpec((None, tm, c), lambda i, me_ref: (me_ref[0], i, 0))),
        out_shape=jax.ShapeDtypeStruct((N_CHIPS, rows, c), out_dtype), compiler_params=_cp("parallel"),
    )(me, w)


def gather_start(lands, chunk_sizes, name="gather_start"):
    n = len(lands)
    nch = len(chunk_sizes)
    assert sum(chunk_sizes) == n

    def body(*refs):
        land_refs = refs[:n]
        outs = refs[n:]
        send_sems, recv_sems = outs[:nch], outs[nch:2 * nch]
        token = outs[-1]
        x, y, c = _place()
        me = _chip_of(x, y)
        peers = _other_chips(x, y, c)
        k = 0
        for ck, size in enumerate(chunk_sizes):
            for pos in range(size):
                for r, peer in enumerate(peers):
                    pltpu.make_async_remote_copy(
                        src_ref=land_refs[k].at[me], dst_ref=land_refs[k].at[me],
                        send_sem=send_sems[ck].at[N_PEER_CHIPS * pos + r], recv_sem=recv_sems[ck].at[N_PEER_CHIPS * pos + r],
                        device_id=peer, device_id_type=MESH).start()
                k += 1
        token[...] = jnp.zeros(token.shape, F32)

    sems = [pltpu.SemaphoreType.DMA((N_PEER_CHIPS * s,)) for s in chunk_sizes]
    res = pl.pallas_call(
        body, name=name,
        out_shape=(*sems, *sems, *_hbm_like(lands), jax.ShapeDtypeStruct(DEP_SPEC_SHAPE, F32)),
        in_specs=[HBM_SPEC] * n,
        out_specs=(*[SEM_SPEC] * (2 * nch), *[HBM_SPEC] * n, pl.BlockSpec(memory_space=pltpu.VMEM)),
        input_output_aliases={k: 2 * nch + k for k in range(n)},
        compiler_params=pltpu.CompilerParams(has_side_effects=DATAFLOW),
    )(*[_hbm(a) for a in lands])
    return res[:nch], res[nch:2 * nch], res[2 * nch:2 * nch + n], res[-1]


def gather_wait(send_sem, recv_sem, lands, after, name):
    n = len(lands)

    def body(*refs):
        land_refs = refs[:n]
        ssem, rsem = refs[n], refs[n + 1]
        x, y, c = _place()
        me = _chip_of(x, y)
        for pos in range(n):
            for r, peer in enumerate(_other_chips(x, y, c)):
                cp = pltpu.make_async_remote_copy(
                    src_ref=land_refs[pos].at[me], dst_ref=land_refs[pos].at[_chip_of(peer[0], peer[1])],
                    send_sem=ssem.at[N_PEER_CHIPS * pos + r], recv_sem=rsem.at[N_PEER_CHIPS * pos + r],
                    device_id=peer, device_id_type=MESH)
                cp.wait_send()
                cp.wait_recv()

    return pl.pallas_call(
        body, name=name, out_shape=tuple(_hbm_like(lands)),
        in_specs=[*[HBM_SPEC] * n, SEM_SPEC, SEM_SPEC, ANY_SPEC], out_specs=[HBM_SPEC] * n,
        input_output_aliases={k: k for k in range(n)},
        compiler_params=pltpu.CompilerParams(has_side_effects=DATAFLOW),
    )(*lands, send_sem, recv_sem, after)


def scatter_start(grads, name):
    n = len(grads)
    recvs = [lax.empty((N_PEER_CHIPS, *g.shape[1:]), g.dtype) for g in grads]

    def body(*refs):
        g_refs, r_refs = refs[:n], refs[n:2 * n]
        send_sems, recv_sems = refs[2 * n], refs[2 * n + 1]
        token = refs[-1]
        x, y, c = _place()
        for k in range(n):
            for r, peer in enumerate(_other_chips(x, y, c)):
                pltpu.make_async_remote_copy(
                    src_ref=g_refs[k].at[_chip_of(peer[0], peer[1])], dst_ref=r_refs[k].at[r],
                    send_sem=send_sems.at[N_PEER_CHIPS * k + r], recv_sem=recv_sems.at[N_PEER_CHIPS * k + r],
                    device_id=peer, device_id_type=MESH).start()
        token[...] = jnp.zeros(token.shape, F32)

    sem = pltpu.SemaphoreType.DMA((N_PEER_CHIPS * n,))
    res = pl.pallas_call(
        body, name=name,
        out_shape=(sem, sem, *_hbm_like(grads), *_hbm_like(recvs), jax.ShapeDtypeStruct(DEP_SPEC_SHAPE, F32)),
        in_specs=[HBM_SPEC] * (2 * n),
        out_specs=(SEM_SPEC, SEM_SPEC, *[HBM_SPEC] * (2 * n), pl.BlockSpec(memory_space=pltpu.VMEM)),
        input_output_aliases={k: 2 + k for k in range(2 * n)},
        compiler_params=pltpu.CompilerParams(has_side_effects=DATAFLOW),
    )(*[_hbm(a) for a in grads], *[_hbm(a) for a in recvs])
    return res[0], res[1], res[2:2 + n], res[2 + n:2 + 2 * n], res[-1]


def scatter_wait(send_sem, recv_sem, grads, recvs, after, name):
    n = len(grads)

    def body(*refs):
        g_refs, r_refs = refs[:n], refs[n:2 * n]
        ssem, rsem = refs[2 * n], refs[2 * n + 1]
        x, y, c = _place()
        for k in range(n):
            for r, peer in enumerate(_other_chips(x, y, c)):
                cp = pltpu.make_async_remote_copy(
                    src_ref=g_refs[k].at[_chip_of(peer[0], peer[1])], dst_ref=r_refs[k].at[r],
                    send_sem=ssem.at[N_PEER_CHIPS * k + r], recv_sem=rsem.at[N_PEER_CHIPS * k + r],
                    device_id=peer, device_id_type=MESH)
                cp.wait_send()
                cp.wait_recv()

    res = pl.pallas_call(
        body, name=name, out_shape=(*_hbm_like(grads), *_hbm_like(recvs)),
        in_specs=[*[HBM_SPEC] * (2 * n), SEM_SPEC, SEM_SPEC, ANY_SPEC], out_specs=[HBM_SPEC] * (2 * n),
        input_output_aliases={k: k for k in range(2 * n)},
        compiler_params=pltpu.CompilerParams(has_side_effects=DATAFLOW),
    )(*grads, *recvs, send_sem, recv_sem, after)
    return res[:n], res[n:]


def swap_with_sibling(parts, name="swap_with_sibling"):
    n = len(parts)

    def body(*refs):
        ins, outs = refs[:n], refs[n:2 * n]
        send_sems, recv_sems = refs[2 * n:]
        x, y, c = _place()
        copies = []
        for k in range(n):
            cp = pltpu.make_async_remote_copy(
                src_ref=ins[k], dst_ref=outs[k], send_sem=send_sems.at[k], recv_sem=recv_sems.at[k],
                device_id=(x, y, 1 - c), device_id_type=MESH)
            cp.start()
            copies.append(cp)
        for cp in copies:
            cp.wait()

    any_spec = pl.BlockSpec(memory_space=pl.ANY)
    return pl.pallas_call(
        body, name=name, in_specs=[any_spec] * n, out_specs=[any_spec] * n,
        out_shape=[jax.ShapeDtypeStruct(p.shape, p.dtype) for p in parts],
        scratch_shapes=[pltpu.SemaphoreType.DMA((n,)), pltpu.SemaphoreType.DMA((n,))],
        compiler_params=pltpu.CompilerParams(has_side_effects=True),
    )(*parts)


def allreduce_small(vec, name="allreduce_small"):
    rows, lanes = vec.shape

    def body(v_ref, o_ref, buf_ref, send_sems, recv_sems):
        x, y, c = _place()
        me = 4 * x + 2 * y + c
        buf_ref[me] = v_ref[...]
        copies = []
        for k in range(1, N_DEV):
            px, py, pc = x ^ ((k >> 2) & 1), y ^ ((k >> 1) & 1), c ^ (k & 1)
            peer_slot = 4 * px + 2 * py + pc
            cp = pltpu.make_async_remote_copy(
                src_ref=buf_ref.at[me], dst_ref=buf_ref.at[me], send_sem=send_sems.at[k - 1],
                recv_sem=recv_sems.at[k - 1], device_id=(px, py, pc), device_id_type=MESH)
            cp.start()
            copies.append(pltpu.make_async_remote_copy(
                src_ref=buf_ref.at[me], dst_ref=buf_ref.at[peer_slot], send_sem=send_sems.at[k - 1],
                recv_sem=recv_sems.at[k - 1], device_id=(px, py, pc), device_id_type=MESH))
        for cp in copies:
            cp.wait()
        acc = buf_ref[0]
        for j in range(1, N_DEV):
            acc = acc + buf_ref[j]
        o_ref[...] = acc

    vm = pl.BlockSpec(memory_space=pltpu.VMEM)
    return pl.pallas_call(
        body, name=name, in_specs=[vm], out_specs=vm, out_shape=jax.ShapeDtypeStruct((rows, lanes), F32),
        scratch_shapes=[pltpu.VMEM((N_DEV, rows, lanes), F32), pltpu.SemaphoreType.DMA((N_DEV - 1,)),
                        pltpu.SemaphoreType.DMA((N_DEV - 1,))],
        compiler_params=pltpu.CompilerParams(has_side_effects=True, vmem_limit_bytes=VMEM_LIMIT_V7X),
    )(vec)


def _pack(arrays):
    flat = jnp.concatenate([a.reshape(-1).astype(F32) for a in arrays])
    n = flat.shape[0]
    rows = -(-n // LANES)
    rows = -(-rows // 8) * 8
    return jnp.pad(flat, (0, rows * LANES - n)).reshape(rows, LANES)


def _unpack(packed, shapes, lead=()):
    flat = packed.reshape(*lead, -1)
    out, off = [], 0
    for shp in shapes:
        n = math.prod(shp)
        out.append(flat[..., off:off + n].reshape(*lead, *shp))
        off += n
    return out


def _row(vec):
    return vec.reshape(1, -1)


def kernel(x, mix_pre_g, mix_post_g, ffn_pre_g, ffn_post_g, cm_w_in, cm_b_in, cm_dw, cm_dw_b, cm_ln_g, cm_ln_b, cm_w_out, cm_b_out, kv_norm_g, w_kv, w_q, w_o, ffn_w_in, ffn_dw, ffn_dw_b, ffn_w_out, loss_target, m_mix_pre_g, m_mix_post_g, m_ffn_pre_g, m_ffn_post_g, m_cm_w_in, m_cm_b_in, m_cm_dw, m_cm_dw_b, m_cm_ln_g, m_cm_ln_b, m_cm_w_out, m_cm_b_out, m_kv_norm_g, m_w_kv, m_w_q, m_w_o, m_ffn_w_in, m_ffn_dw, m_ffn_dw_b, m_ffn_w_out, v_mix_pre_g, v_mix_post_g, v_ffn_pre_g, v_ffn_post_g, v_cm_w_in, v_cm_b_in, v_cm_dw, v_cm_dw_b, v_cm_ln_g, v_cm_ln_b, v_cm_w_out, v_cm_b_out, v_kv_norm_g, v_w_kv, v_w_q, v_w_o, v_ffn_w_in, v_ffn_dw, v_ffn_dw_b, v_ffn_w_out):
    names = ["mix_pre_g", "mix_post_g", "ffn_pre_g", "ffn_post_g", "cm_w_in", "cm_b_in", "cm_dw", "cm_dw_b", "cm_ln_g",
             "cm_ln_b", "cm_w_out", "cm_b_out", "kv_norm_g", "w_kv", "w_q", "w_o", "ffn_w_in", "ffn_dw", "ffn_dw_b",
             "ffn_w_out"]
    w_in = dict(zip(names, [mix_pre_g, mix_post_g, ffn_pre_g, ffn_post_g, cm_w_in, cm_b_in, cm_dw, cm_dw_b, cm_ln_g,
                            cm_ln_b, cm_w_out, cm_b_out, kv_norm_g, w_kv, w_q, w_o, ffn_w_in, ffn_dw, ffn_dw_b, ffn_w_out]))
    m_in = dict(zip(names, [m_mix_pre_g, m_mix_post_g, m_ffn_pre_g, m_ffn_post_g, m_cm_w_in, m_cm_b_in, m_cm_dw, m_cm_dw_b,
                            m_cm_ln_g, m_cm_ln_b, m_cm_w_out, m_cm_b_out, m_kv_norm_g, m_w_kv, m_w_q, m_w_o, m_ffn_w_in,
                            m_ffn_dw, m_ffn_dw_b, m_ffn_w_out]))
    v_in = dict(zip(names, [v_mix_pre_g, v_mix_post_g, v_ffn_pre_g, v_ffn_post_g, v_cm_w_in, v_cm_b_in, v_cm_dw, v_cm_dw_b,
                            v_cm_ln_g, v_cm_ln_b, v_cm_w_out, v_cm_b_out, v_kv_norm_g, v_w_kv, v_w_q, v_w_o, v_ffn_w_in,
                            v_ffn_dw, v_ffn_dw_b, v_ffn_w_out]))

    bsz, seq, d = x.shape
    t = bsz * seq
    n_b = DEPTH - N_A
    hw = w_o.shape[-1]
    qw = N_GROUPS * hw
    f2 = ffn_dw_b.shape[-1]
    f = f2 // 2
    me_chip = _chip_of(lax.axis_index("x"), lax.axis_index("y"))

    big = ["cm_w_in", "cm_w_out", "w_kv", "w_q", "w_o", "ffn_w_in", "ffn_w_out"]
    row_sharded = ("cm_w_out", "w_o", "ffn_w_out")
    small_sharded = ["cm_b_in", "cm_dw", "cm_dw_b", "cm_ln_g", "cm_ln_b", "cm_b_out", "ffn_dw"]
    small_pack = _pack([w_in[n] for n in small_sharded])
    chunks = [
        [("cm_w_in", 0), ("small", None)],
        [("cm_w_out", 0)],
        [("ffn_w_in", 0), ("ffn_w_out", 0)],
        [("cm_w_in", 1), ("cm_w_out", 1)],
        [("ffn_w_in", 1), ("ffn_w_out", 1)],
        [("w_kv", None)],
        [("w_q", 0), ("w_o", 0)],
        [("ffn_w_in", 2), ("ffn_w_out", 2)],
        [("w_q", 1), ("w_o", 1)],
        [("ffn_w_in", 3), ("ffn_w_out", 3)],
    ]
    pieces = [pc for ch in chunks for pc in ch]
    chunk_of = {pc: ck for ck, ch in enumerate(chunks) for pc in ch}

    me_arr = me_chip.astype(jnp.int32).reshape(1)

    def land_of(pc):
        n, l = pc
        if n == "small":
            return cast_place(small_pack, None, me_arr, F32, name="place_small")
        return cast_place(w_in[n], l, me_arr, BF16, name=f"place_{n}_{l}")

    lands = [land_of(pc) for pc in pieces]
    g_send, g_recv, lands_f, token = gather_start(lands, [len(ch) for ch in chunks])
    weights = {}

    def finish_chunk(ck, after):
        lo = sum(len(ch) for ch in chunks[:ck])
        hi = lo + len(chunks[ck])
        got = gather_wait(g_send[ck], g_recv[ck], lands_f[lo:hi], after, name=f"gather_wait{ck}")
        for pc, arr in zip(chunks[ck], got):
            weights[pc] = arr.reshape(1, -1, arr.shape[-1]) if pc[0] in row_sharded else arr

    def wmat(n, l=None, after=None):
        if (n, l) not in weights:
            finish_chunk(chunk_of[(n, l)], after)
        arr = weights[(n, l)]
        return arr, arr.shape[0]

    finish_chunk(0, token)
    small_full = {}
    for n, arr4 in zip(small_sharded, _unpack(weights[("small", None)], [w_in[n].shape for n in small_sharded], lead=(N_CHIPS,))):
        shp = w_in[n].shape
        small_full[n] = jnp.moveaxis(arr4, 0, -2).reshape(*shp[:-1], N_CHIPS * shp[-1])

    x2d = x.reshape(t, d)
    saved = []
    (h1,) = resid_norm_fwd(x2d, None, None, [_row(mix_pre_g[0])], name="norm_in", dep=token)
    xcur = x2d
    kv_state = None
    for i in range(DEPTH):
        sv = {"x_in": xcur, "h1": h1}
        if i < N_A:
            z = mm_nn(h1, *wmat("cm_w_in", i, h1), 1, 0, bias=_row(small_full["cm_b_in"][i]), name=f"cm_in{i}")
            u2 = glu_conv_fwd(z.reshape(bsz, seq, 2 * d), small_full["cm_dw"][i], _row(small_full["cm_dw_b"][i]),
                              name=f"glu_conv{i}").reshape(t, d)
            u4 = ln_silu_fwd(u2, _row(small_full["cm_ln_g"][i]), _row(small_full["cm_ln_b"][i]), name=f"ln_silu{i}")
            y = mm_nn(u4, *wmat("cm_w_out", i, u4), 1, 0, bias=_row(small_full["cm_b_out"][i]), name=f"cm_out{i}")
            sv.update(z=z, u2=u2, u4=u4)
        else:
            j = i - N_A
            q = mm_nn(h1, *wmat("w_q", j, h1), 1, 0, name=f"q_proj{j}").reshape(bsz, seq, qw)
            outs, lses = [], []
            for g, dil in enumerate(DILATIONS):
                o_g, l_g = attn_fwd(q, kv_state["kv"], g, dil, hw, name=f"attn_fwd{j}_{g}")
                outs.append(o_g.reshape(t, hw))
                lses.append(l_g.reshape(t, hw))
            merged, lsej = attn_merge(outs, lses, name=f"attn_merge{j}")
            y = mm_nn(merged, *wmat("w_o", j, merged), 1, 0, name=f"o_proj{j}")
            sv.update(q=q, merged=merged, lsej=lsej)
        x1, h2 = resid_norm_fwd(xcur, y, _row(mix_post_g[i]), [_row(ffn_pre_g[i])], name=f"resid_mix{i}")
        p = mm_nn(h2, *wmat("ffn_w_in", i, h2), 1, 0, out_dtype=BF16, name=f"ffn_in{i}")
        s_act = ffn_mid_fwd(p.reshape(bsz, seq, f2), small_full["ffn_dw"][i], _row(ffn_dw_b[i]), name=f"ffn_mid{i}").reshape(t, f)
        y2 = mm_nn(s_act, *wmat("ffn_w_out", i), 1, 0, name=f"ffn_out{i}")
        next_gains = []
        if i + 1 < DEPTH:
            next_gains.append(_row(mix_pre_g[i + 1]))
        if i == N_A - 1:
            next_gains.append(_row(kv_norm_g))
        res = resid_norm_fwd(x1, y2, _row(ffn_post_g[i]), next_gains, name=f"resid_ffn{i}")
        sv.update(y=y, x1=x1, h2=h2, p=p, s=s_act, y2=y2)
        saved.append(sv)
        xcur = res[0]
        if i + 1 < DEPTH:
            h1 = res[1]
        if i == N_A - 1:
            kvn = res[2]
            kv = mm_nn(kvn, *wmat("w_kv", None, kvn), 1, 0, name="kv_proj").reshape(bsz, seq, 2 * qw)
            kv_state = {"kv": kv, "kvn": kvn, "x_a": xcur}

    dx, loss_tile = loss_fwd_bwd(xcur, loss_target.reshape(t, d))
    loss = lax.psum(loss_tile[0, 0], ("x", "y", "c"))

    gsm = {n: [None] * w_in[n].shape[0] for n in
           ["mix_pre_g", "mix_post_g", "ffn_pre_g", "ffn_post_g", "cm_b_in", "cm_dw", "cm_dw_b", "cm_ln_g", "cm_ln_b",
            "cm_b_out", "ffn_dw", "ffn_dw_b"]}
    gbig = {}
    in_flight = []
    dep = None

    def start_scatter(pcs, tag):
        ssem, rsem, g_f, r_f, tok = scatter_start([gbig[pc] for pc in pcs], name=f"scatter_start_{tag}")
        in_flight.append((pcs, ssem, rsem, g_f, r_f))
        return tok

    dk_buf = dv_buf = None
    for i in range(DEPTH - 1, -1, -1):
        sv = saved[i]
        dy2, dg, _ = norm_bwd(sv["y2"], _row(ffn_post_g[i]), dx, out_dtype=BF16, name=f"bwd_ffn_post{i}", dep=dep)
        gsm["ffn_post_g"][i] = dg
        ds = mm_nt(dy2, *wmat("ffn_w_out", i), 1, 0, out_dtype=BF16, name=f"bwd_ffn_out_dx{i}")
        gbig[("ffn_w_out", i)] = mm_tn(sv["s"], dy2, 1, name=f"bwd_ffn_out_dw{i}").reshape(N_CHIPS, f // N_CHIPS, d)
        dpa, dpg, ddwa, ddwg, ddba, ddbg = ffn_mid_bwd(sv["p"].reshape(bsz, seq, f2), small_full["ffn_dw"][i], _row(ffn_dw_b[i]),
                                                       ds.reshape(bsz, seq, f), name=f"bwd_ffn_mid{i}")
        gsm["ffn_dw"][i] = jnp.concatenate([jnp.sum(ddwa, axis=1), jnp.sum(ddwg, axis=1)], axis=-1)
        gsm["ffn_dw_b"][i] = jnp.concatenate([ddba, ddbg], axis=-1)
        dp = [dpa.reshape(t, f), dpg.reshape(t, f)]
        dh2 = mm_nt(dp, *wmat("ffn_w_in", i), 1, 0, name=f"bwd_ffn_in_dx{i}")
        gbig[("ffn_w_in", i)] = mm_tn(sv["h2"], dp, N_CHIPS, name=f"bwd_ffn_in_dw{i}")
        dx1, dg, _ = norm_bwd(sv["x1"], _row(ffn_pre_g[i]), dh2, add=dx, name=f"bwd_ffn_pre{i}")
        gsm["ffn_pre_g"][i] = dg
        dep = start_scatter([("ffn_w_in", 0), ("ffn_w_out", 0)], "ffn0") if i == 0 else None
        dy, dg, dbias = norm_bwd(sv["y"], _row(mix_post_g[i]), dx1, out_dtype=BF16, name=f"bwd_mix_post{i}", dep=dep)
        gsm["mix_post_g"][i] = dg
        if i < N_A:
            gsm["cm_b_out"][i] = dbias
            du4 = mm_nt(dy, *wmat("cm_w_out", i), 1, 0, name=f"bwd_cm_out_dx{i}")
            gbig[("cm_w_out", i)] = mm_tn(sv["u4"], dy, 1, name=f"bwd_cm_out_dw{i}").reshape(N_CHIPS, d // N_CHIPS, d)
            du2, dlg, dlb = ln_silu_bwd(sv["u2"], _row(small_full["cm_ln_g"][i]), _row(small_full["cm_ln_b"][i]), du4,
                                        name=f"bwd_ln_silu{i}")
            gsm["cm_ln_g"][i], gsm["cm_ln_b"][i] = dlg, dlb
            dza, dzg, ddw, ddwb, dba, dbg = glu_conv_bwd(sv["z"].reshape(bsz, seq, 2 * d), small_full["cm_dw"][i],
                                                         du2.reshape(bsz, seq, d), name=f"bwd_glu_conv{i}")
            gsm["cm_dw"][i] = jnp.sum(ddw, axis=1)
            gsm["cm_dw_b"][i] = ddwb
            gsm["cm_b_in"][i] = jnp.concatenate([dba, dbg], axis=-1)
            dz = [dza.reshape(t, d), dzg.reshape(t, d)]
            dh1 = mm_nt(dz, *wmat("cm_w_in", i), 1, 0, name=f"bwd_cm_in_dx{i}")
            gbig[("cm_w_in", i)] = mm_tn(sv["h1"], dz, N_CHIPS, name=f"bwd_cm_in_dw{i}")
        else:
            j = i - N_A
            dmerged = mm_nt(dy, *wmat("w_o", j), 1, 0, name=f"bwd_o_proj_dx{j}")
            gbig[("w_o", j)] = mm_tn(sv["merged"], dy, 1, name=f"bwd_o_proj_dw{j}").reshape(N_CHIPS, hw // N_CHIPS, d)
            dmt = attn_bwd_prep(dmerged, sv["merged"], name=f"bwd_attn_prep{j}")
            dq_buf = None
            add_to_kv = dk_buf is not None
            for g, dil in enumerate(DILATIONS):
                dq_buf, dk_buf, dv_buf = attn_bwd(
                    sv["q"], kv_state["kv"], g, dil, dmerged.reshape(bsz, seq, hw), sv["lsej"].reshape(bsz, seq, hw),
                    dmt.reshape(bsz, seq, hw), dq_buf, dk_buf, dv_buf, add_to_kv, hw, name=f"attn_bwd{j}_{g}")
            dq = dq_buf.reshape(t, qw)
            dh1 = mm_nt(dq, *wmat("w_q", j), 1, 0, name=f"bwd_q_proj_dx{j}")
            gbig[("w_q", j)] = mm_tn(sv["h1"], dq, N_CHIPS, name=f"bwd_q_proj_dw{j}")
        dx, dg, _ = norm_bwd(sv["x_in"], _row(mix_pre_g[i]), dh1, add=dx1, name=f"bwd_mix_pre{i}")
        gsm["mix_pre_g"][i] = dg
        if i > N_A:
            dep = start_scatter([("ffn_w_in", i), ("ffn_w_out", i), ("w_q", i - N_A), ("w_o", i - N_A)], f"l{i}")
        elif 0 < i < N_A:
            dep = start_scatter([("ffn_w_in", i), ("ffn_w_out", i), ("cm_w_in", i), ("cm_w_out", i)], f"l{i}")
        elif i == 0:
            last_token = start_scatter([("cm_w_in", 0), ("cm_w_out", 0)], "cm0")
        if i == N_A:
            dkv = [dk_buf.reshape(t, qw), dv_buf.reshape(t, qw)]
            dkvn = mm_nt(dkv, *wmat("w_kv"), 1, 0, name="bwd_kv_proj_dx")
            gbig[("w_kv", None)] = mm_tn(kv_state["kvn"], dkv, N_CHIPS, name="bwd_kv_proj_dw")
            dx, dg_kv, _ = norm_bwd(kv_state["x_a"], _row(kv_norm_g), dkvn, add=dx, name="bwd_kv_norm")
            dep = start_scatter([("ffn_w_in", i), ("ffn_w_out", i), ("w_q", 0), ("w_o", 0), ("w_kv", None)], f"l{i}")
    grad_x = dx.reshape(bsz, seq, d)

    plane_of = {}
    outs_g, outs_d, outs_m, outs_v = {}, {}, {}, {}

    def finish_scatter(k, after):
        pcs, ssem, rsem, g_f, r_f = in_flight[k]
        g_done, r_done = scatter_wait(ssem, rsem, g_f, r_f, after, name=f"scatter_wait{k}")
        for pc, g_arr, r_arr in zip(pcs, g_done, r_done):
            plane_of[pc] = sum_parts(g_arr, r_arr, me_arr, name=f"sum_chips_{pc[0]}_{pc[1]}")

    def update(group, tag):
        plane = []
        for n in group:
            if w_in[n].ndim == 2:
                plane.append(plane_of[(n, None)])
            else:
                plane.append(jnp.concatenate([plane_of[(n, l)] for l in range(w_in[n].shape[0])], axis=0))
        other = swap_with_sibling(plane, name=f"swap_with_sibling_{tag}")
        for n, p_mine, p_other in zip(group, plane, other):
            shp = w_in[n].shape
            flat = lambda a: a.reshape(-1, shp[-1])
            g_, d_, m_, v_ = adamw(flat(w_in[n]), flat(m_in[n]), flat(v_in[n]), [p_mine, p_other], name=f"adamw_{n}")
            outs_g[n], outs_d[n], outs_m[n], outs_v[n] = (a.reshape(shp) for a in (g_, d_, m_, v_))

    for k in range(len(in_flight) - 1):
        finish_scatter(k, last_token)
    update(["w_kv", "w_q", "w_o", "ffn_w_in", "ffn_w_out"], "a")
    finish_scatter(len(in_flight) - 1, outs_v["ffn_w_out"])
    update(["cm_w_in", "cm_w_out"], "b")

    small_names = [n for n in names if n not in big]
    small_shapes_full = {}
    small_grads_full = []
    for n in small_names:
        if n == "kv_norm_g":
            gfull = dg_kv.reshape(-1)
        elif n in ("cm_dw", "ffn_dw"):
            gfull = jnp.stack(gsm[n], axis=0)
        else:
            gfull = jnp.stack([a.reshape(-1) for a in gsm[n]], axis=0)
        small_shapes_full[n] = gfull.shape
        small_grads_full.append(gfull)
    summed = allreduce_small(_pack(small_grads_full))
    g_full = dict(zip(small_names, _unpack(summed, [small_shapes_full[n] for n in small_names])))
    g_loc = {}
    for n in small_names:
        if n in small_sharded:
            width = w_in[n].shape[-1]
            g_loc[n] = lax.dynamic_slice_in_dim(g_full[n], me_chip * width, width, axis=g_full[n].ndim - 1)
        else:
            g_loc[n] = g_full[n]
    res = adamw(_pack([w_in[n] for n in small_names]), _pack([m_in[n] for n in small_names]),
                _pack([v_in[n] for n in small_names]), [_pack([g_loc[n] for n in small_names])], name="adamw_small")
    shapes_loc = [w_in[n].shape for n in small_names]
    for dst, packed in zip((outs_g, outs_d, outs_m, outs_v), res):
        for n, a in zip(small_names, _unpack(packed, shapes_loc)):
            dst[n] = a

    return (loss, grad_x, *[outs_g[n] for n in names], *[outs_d[n] for n in names],
            *[outs_m[n] for n in names], *[outs_v[n] for n in names])
```

```python
import functools
import math

import jax
import jax.numpy as jnp
from jax import lax
from jax.experimental import pallas as pl
from jax.experimental.pallas import tpu as pltpu

F32 = jnp.float32
BF16 = jnp.bfloat16
EPS = 1e-6
NEG_INF = -1e30
N_A = 2
DEPTH = 4
N_GROUPS = 3
DILATIONS = (1, 4, 16)
HEAD_DIM = 128
BLK = 128
LANES = 128
N_CHIPS = 4
N_DEV = 8
VMEM_LIMIT_V7X = 56 * 1024 * 1024

ADAM_LR = 0.001
ADAM_B1 = 0.9
ADAM_B2 = 0.999
ADAM_EPS = 1e-08
ADAM_WD = 0.01
ADAM_STEP = 10

MESH = pl.DeviceIdType.MESH


def _cp(*sem, **kw):
    return pltpu.CompilerParams(dimension_semantics=sem if sem else None, vmem_limit_bytes=VMEM_LIMIT_V7X, **kw)


def _dot(a, b):
    return jnp.dot(a, b, preferred_element_type=F32)


def _dot_nt(a, b):
    return lax.dot_general(a, b, (((1,), (1,)), ((), ())), preferred_element_type=F32)


def _dot_tn(a, b):
    return lax.dot_general(a, b, (((0,), (0,)), ((), ())), preferred_element_type=F32)


def _sigmoid(x):
    return 1.0 / (1.0 + jnp.exp(-x))


def _row_tile(n, want):
    if n <= want:
        return n
    for t in range(want - want % 8, 7, -8):
        if n % t == 0:
            return t
    raise ValueError(f"no row tile for {n} rows")


def mm_nn(a, w, nsh, stride, layer, bias=None, out_dtype=F32, name="mm_nn", tm=1024):
    m, k = a.shape
    _, k2, ns = w.shape
    assert k == k2
    tm = _row_tile(m, tm)
    has_bias = bias is not None

    def body(*refs):
        if has_bias:
            a_ref, w_ref, b_ref, o_ref = refs
        else:
            a_ref, w_ref, o_ref = refs
        acc = _dot(a_ref[...].astype(BF16), w_ref[...])
        if has_bias:
            acc = acc + b_ref[...]
        o_ref[...] = acc.astype(out_dtype)

    in_specs = [
        pl.BlockSpec((tm, k), lambda j, i: (i, 0)),
        pl.BlockSpec((None, k, ns), lambda j, i: (j * stride + layer, 0, 0)),
    ]
    args = [a, w]
    if has_bias:
        in_specs.append(pl.BlockSpec((1, ns), lambda j, i: (0, j)))
        args.append(bias)
    return pl.pallas_call(
        body,
        name=name,
        grid=(nsh, m // tm),
        in_specs=in_specs,
        out_specs=pl.BlockSpec((tm, ns), lambda j, i: (i, j)),
        out_shape=jax.ShapeDtypeStruct((m, nsh * ns), out_dtype),
        compiler_params=_cp("parallel", "parallel"),
    )(*args)


def mm_nt(dy, w, nsh, stride, layer, out_dtype=F32, name="mm_nt", tm=1024):
    dys = list(dy) if isinstance(dy, (list, tuple)) else [dy]
    npart = len(dys)
    per = nsh // npart
    m = dys[0].shape[0]
    _, k, ns = w.shape
    assert all(d.shape == (m, per * ns) for d in dys)
    tm = _row_tile(m, tm)

    def body(*refs):
        dy_refs = refs[:npart]
        w_ref, o_ref, acc_ref = refs[npart:]
        j = pl.program_id(1)

        @pl.when(j == 0)
        def _():
            acc_ref[...] = jnp.zeros(acc_ref.shape, F32)

        for pi in range(npart):
            @pl.when(j // per == pi)
            def _(pi=pi):
                acc_ref[...] += _dot_nt(dy_refs[pi][...].astype(BF16), w_ref[...])

        @pl.when(j == nsh - 1)
        def _():
            o_ref[...] = acc_ref[...].astype(out_dtype)

    dy_specs = [pl.BlockSpec((tm, ns), lambda i, j, pi=pi: (i, jnp.clip(j - pi * per, 0, per - 1))) for pi in range(npart)]
    return pl.pallas_call(
        body,
        name=name,
        grid=(m // tm, nsh),
        in_specs=[*dy_specs, pl.BlockSpec((None, k, ns), lambda i, j: (j * stride + layer, 0, 0))],
        out_specs=pl.BlockSpec((tm, k), lambda i, j: (i, 0)),
        out_shape=jax.ShapeDtypeStruct((m, k), out_dtype),
        scratch_shapes=[pltpu.VMEM((tm, k), F32)],
        compiler_params=_cp("parallel", "arbitrary"),
    )(*dys, w)


def mm_tn(a, dy, nsh, name="mm_tn", tm=1024):
    dys = list(dy) if isinstance(dy, (list, tuple)) else [dy]
    npart = len(dys)
    per = nsh // npart
    m, k = a.shape
    ns = dys[0].shape[1] // per
    assert all(d.shape == (m, per * ns) for d in dys)
    tm = _row_tile(m, tm)
    nt = m // tm

    def body(*refs):
        a_ref = refs[0]
        dy_refs = refs[1:1 + npart]
        o_ref, acc_ref = refs[1 + npart:]
        j = pl.program_id(0)
        i = pl.program_id(1)

        @pl.when(i == 0)
        def _():
            acc_ref[...] = jnp.zeros(acc_ref.shape, F32)

        for pi in range(npart):
            @pl.when(j // per == pi)
            def _(pi=pi):
                acc_ref[...] += _dot_tn(a_ref[...].astype(BF16), dy_refs[pi][...].astype(BF16))

        @pl.when(i == nt - 1)
        def _():
            o_ref[...] = acc_ref[...].astype(BF16)

    dy_specs = [
        pl.BlockSpec((tm, ns), lambda j, i, pi=pi: (jnp.where(j // per == pi, i, 0), jnp.clip(j - pi * per, 0, per - 1)))
        for pi in range(npart)
    ]
    return pl.pallas_call(
        body,
        name=name,
        grid=(nsh, nt),
        in_specs=[pl.BlockSpec((tm, k), lambda j, i: (i, 0)), *dy_specs],
        out_specs=pl.BlockSpec((None, k, ns), lambda j, i: (j, 0, 0)),
        out_shape=jax.ShapeDtypeStruct((nsh, k, ns), BF16),
        scratch_shapes=[pltpu.VMEM((k, ns), F32)],
        compiler_params=_cp("parallel", "arbitrary"),
    )(a, *dys)


DEP_SPEC_SHAPE = (8, LANES)


def resid_norm_fwd(x, y, g_post, next_gains, name, tm=256, dep=None):
    t, d = x.shape
    tm = _row_tile(t, tm)
    has_y = y is not None
    n_next = len(next_gains)
    n_dep = 0 if dep is None else 1

    def body(*refs):
        x_ref = refs[0]
        pos = 1
        if has_y:
            y_ref, gp_ref = refs[1], refs[2]
            pos = 3
        gn_refs = refs[pos:pos + n_next]
        outs = refs[pos + n_next + n_dep:]
        xv = x_ref[...]
        o = 0
        if has_y:
            yv = y_ref[...]
            r = lax.rsqrt(jnp.mean(yv * yv, axis=-1, keepdims=True) + EPS)
            xv = xv + (yv * r) * gp_ref[...]
            outs[0][...] = xv
            o = 1
        if n_next:
            xn = xv * lax.rsqrt(jnp.mean(xv * xv, axis=-1, keepdims=True) + EPS)
            for k in range(n_next):
                outs[o + k][...] = (xn * gn_refs[k][...]).astype(BF16)

    row = pl.BlockSpec((tm, d), lambda i: (i, 0))
    vec = pl.BlockSpec((1, d), lambda i: (0, 0))
    args, in_specs = [x], [row]
    if has_y:
        args += [y, g_post]
        in_specs += [row, vec]
    args += list(next_gains)
    in_specs += [vec] * n_next
    if n_dep:
        args.append(dep)
        in_specs.append(pl.BlockSpec(DEP_SPEC_SHAPE, lambda i: (0, 0)))
    out_shape, out_specs = [], []
    if has_y:
        out_shape.append(jax.ShapeDtypeStruct((t, d), F32))
        out_specs.append(row)
    for _ in range(n_next):
        out_shape.append(jax.ShapeDtypeStruct((t, d), BF16))
        out_specs.append(row)
    return pl.pallas_call(
        body, name=name, grid=(t // tm,), in_specs=in_specs, out_specs=out_specs, out_shape=out_shape,
        compiler_params=_cp("parallel"),
    )(*args)


def norm_bwd(x, g, dy, add=None, out_dtype=F32, name="norm_bwd", tm=256, dep=None):
    t, d = x.shape
    tm = _row_tile(t, tm)
    has_add = add is not None

    def body(*refs):
        x_ref, g_ref, dy_ref = refs[:3]
        add_ref = refs[3] if has_add else None
        dx_ref, dg_ref, cs_ref = refs[-3:]
        i = pl.program_id(0)
        xv = x_ref[...]
        dyv = dy_ref[...].astype(F32)
        r = lax.rsqrt(jnp.mean(xv * xv, axis=-1, keepdims=True) + EPS)
        gd = dyv * g_ref[...]
        dx = r * gd - xv * ((r * r * r) * jnp.mean(xv * gd, axis=-1, keepdims=True))
        if has_add:
            dx = dx + add_ref[...]
        dx_ref[...] = dx.astype(out_dtype)
        dg = jnp.sum(dyv * (xv * r), axis=0, keepdims=True)
        cs = jnp.sum(dx, axis=0, keepdims=True)

        @pl.when(i == 0)
        def _():
            dg_ref[...] = dg
            cs_ref[...] = cs

        @pl.when(i > 0)
        def _():
            dg_ref[...] += dg
            cs_ref[...] += cs

    row = pl.BlockSpec((tm, d), lambda i: (i, 0))
    vec = pl.BlockSpec((1, d), lambda i: (0, 0))
    args, in_specs = [x, g, dy], [row, vec, row]
    if has_add:
        args.append(add)
        in_specs.append(row)
    if dep is not None:
        args.append(dep)
        in_specs.append(pl.BlockSpec(DEP_SPEC_SHAPE, lambda i: (0, 0)))
    return pl.pallas_call(
        body, name=name, grid=(t // tm,), in_specs=in_specs,
        out_specs=[row, vec, vec],
        out_shape=[jax.ShapeDtypeStruct((t, d), out_dtype), jax.ShapeDtypeStruct((1, d), F32),
                   jax.ShapeDtypeStruct((1, d), F32)],
        compiler_params=_cp("arbitrary"),
    )(*args)


def loss_fwd_bwd(x, target, name="loss", tm=256):
    t, d = x.shape
    tm = _row_tile(t, tm)

    def body(x_ref, t_ref, dx_ref, l_ref):
        i = pl.program_id(0)
        err = x_ref[...] - t_ref[...]
        dx_ref[...] = err * (1.0 / d)
        part = 0.5 * jnp.sum(jnp.mean(err * err, axis=-1, keepdims=True), axis=0, keepdims=True)
        part = jnp.broadcast_to(part, l_ref.shape)

        @pl.when(i == 0)
        def _():
            l_ref[...] = part

        @pl.when(i > 0)
        def _():
            l_ref[...] += part

    row = pl.BlockSpec((tm, d), lambda i: (i, 0))
    return pl.pallas_call(
        body, name=name, grid=(t // tm,), in_specs=[row, row],
        out_specs=[row, pl.BlockSpec((8, LANES), lambda i: (0, 0))],
        out_shape=[jax.ShapeDtypeStruct((t, d), F32), jax.ShapeDtypeStruct((8, LANES), F32)],
        compiler_params=_cp("arbitrary"),
    )(x, target)


CONV_HALO = 32
CONV_CHUNK = 128


def glu_conv_fwd(z, dw, dwb, name, tc=128):
    b, s, c2 = z.shape
    c = c2 // 2
    kw = dw.shape[0]
    tc = min(tc, c)
    nc = c // tc
    ch = min(CONV_CHUNK, s)
    halo = CONV_HALO
    assert kw - 1 <= halo and s % ch == 0

    nch = s // ch

    def body(a_ref, g_ref, w_ref, b_ref, o_ref, pad_ref):
        _fill_glu_slabs(a_ref, g_ref, pad_ref, nch, ch, halo)

        def chunk(ci, carry):
            r0 = pl.multiple_of(ci * ch, ch)
            acc = b_ref[...]
            for k, tap in enumerate(_taps_front(pad_ref, ci, kw, ch, halo)):
                acc = acc + w_ref[k:k + 1, :] * tap
            o_ref[pl.ds(r0, ch), :] = acc
            return carry

        lax.fori_loop(0, nch, chunk, 0)

    return pl.pallas_call(
        body, name=name, grid=(b, nc),
        in_specs=[
            pl.BlockSpec((None, s, tc), lambda bi, i: (bi, 0, i)),
            pl.BlockSpec((None, s, tc), lambda bi, i: (bi, 0, i + nc)),
            pl.BlockSpec((kw, tc), lambda bi, i: (0, i)),
            pl.BlockSpec((1, tc), lambda bi, i: (0, i)),
        ],
        out_specs=pl.BlockSpec((None, s, tc), lambda bi, i: (bi, 0, i)),
        out_shape=jax.ShapeDtypeStruct((b, s, c), F32),
        scratch_shapes=[pltpu.VMEM((nch, ch + halo, tc), F32)],
        compiler_params=_cp("parallel", "parallel"),
    )(z, z, dw, dwb)


def glu_conv_bwd(z, dw, du2, name, tc=128):
    b, s, c2 = z.shape
    c = c2 // 2
    kw = dw.shape[0]
    tc = min(tc, c)
    nc = c // tc
    ch = min(CONV_CHUNK, s)
    nch = s // ch
    halo = CONV_HALO

    def body(a_ref, g_ref, w_ref, du_ref, dza_ref, dzg_ref, ddw_ref, ddwb_ref, dba_ref, dbg_ref, upad_ref, dpad_ref):
        bi = pl.program_id(1)

        @pl.when(bi == 0)
        def _():
            ddw_ref[...] = jnp.zeros(ddw_ref.shape, F32)
            ddwb_ref[...] = jnp.zeros(ddwb_ref.shape, F32)
            dba_ref[...] = jnp.zeros(dba_ref.shape, F32)
            dbg_ref[...] = jnp.zeros(dbg_ref.shape, F32)

        _fill_glu_slabs(a_ref, g_ref, upad_ref, nch, ch, halo)
        dpad_ref[nch - 1, ch:ch + halo, :] = jnp.zeros((halo, tc), F32)
        dpad_ref[nch - 1, 0:ch, :] = du_ref[s - ch:s, :]

        def fill(ci, carry):
            r0 = pl.multiple_of(ci * ch, ch)
            dpad_ref[ci, :, :] = du_ref[pl.ds(r0, ch + halo), :]
            return carry

        lax.fori_loop(0, nch - 1, fill, 0)

        def chunk(ci, carry):
            r0 = pl.multiple_of(ci * ch, ch)
            du_c = du_ref[pl.ds(r0, ch), :]
            taps_u = _taps_front(upad_ref, ci, kw, ch, halo)
            du1 = w_ref[kw - 1:kw, :] * du_c
            ddw_ref[kw - 1] += jnp.sum((du_c * taps_u[kw - 1]).reshape(ch // 8, 8, tc), axis=0)
            for j in range(1, kw):
                du1 = du1 + w_ref[kw - 1 - j:kw - j, :] * dpad_ref[ci, j:j + ch, :]
                ddw_ref[kw - 1 - j] += jnp.sum((du_c * taps_u[kw - 1 - j]).reshape(ch // 8, 8, tc), axis=0)
            av = a_ref[pl.ds(r0, ch), :]
            sg = _sigmoid(g_ref[pl.ds(r0, ch), :])
            dza = du1 * sg
            dzg = du1 * av * (sg * (1.0 - sg))
            dza_ref[pl.ds(r0, ch), :] = dza.astype(BF16)
            dzg_ref[pl.ds(r0, ch), :] = dzg.astype(BF16)
            dba_ref[...] += jnp.sum(dza, axis=0, keepdims=True)
            dbg_ref[...] += jnp.sum(dzg, axis=0, keepdims=True)
            ddwb_ref[...] += jnp.sum(du_c, axis=0, keepdims=True)
            return carry

        lax.fori_loop(0, s // ch, chunk, 0)

    blk = lambda off: pl.BlockSpec((None, s, tc), lambda i, bi: (bi, 0, i + off))
    vec = pl.BlockSpec((1, tc), lambda i, bi: (0, i))
    return pl.pallas_call(
        body, name=name, grid=(nc, b),
        in_specs=[blk(0), blk(nc), pl.BlockSpec((kw, tc), lambda i, bi: (0, i)), blk(0)],
        out_specs=[blk(0), blk(0), pl.BlockSpec((kw, 8, tc), lambda i, bi: (0, 0, i)), vec, vec, vec],
        out_shape=[
            jax.ShapeDtypeStruct((b, s, c), BF16), jax.ShapeDtypeStruct((b, s, c), BF16),
            jax.ShapeDtypeStruct((kw, 8, c), F32), jax.ShapeDtypeStruct((1, c), F32),
            jax.ShapeDtypeStruct((1, c), F32), jax.ShapeDtypeStruct((1, c), F32),
        ],
        scratch_shapes=[pltpu.VMEM((nch, ch + halo, tc), F32), pltpu.VMEM((nch, ch + halo, tc), F32)],
        compiler_params=_cp("parallel", "arbitrary"),
    )(z, z, dw, du2)


def _fill_glu_slabs(a_ref, g_ref, pad_ref, nch, ch, halo):
    tc = a_ref.shape[-1]
    pad_ref[0, 0:halo, :] = jnp.zeros((halo, tc), F32)
    pad_ref[0, halo:halo + ch, :] = a_ref[0:ch, :] * _sigmoid(g_ref[0:ch, :])

    def fill(ci, carry):
        r0 = pl.multiple_of(ci * ch, ch)
        pad_ref[ci, 0:halo, :] = pad_ref[ci - 1, ch:ch + halo, :]
        pad_ref[ci, halo:halo + ch, :] = a_ref[pl.ds(r0, ch), :] * _sigmoid(g_ref[pl.ds(r0, ch), :])
        return carry

    lax.fori_loop(1, nch, fill, 0)


def ln_silu_fwd(u, g, bvec, name, tm=256):
    t, d = u.shape
    tm = _row_tile(t, tm)

    def body(u_ref, g_ref, b_ref, o_ref):
        uv = u_ref[...]
        mu = jnp.mean(uv, axis=-1, keepdims=True)
        xc = uv - mu
        var = jnp.mean(xc * xc, axis=-1, keepdims=True)
        v = (xc * lax.rsqrt(var + EPS)) * g_ref[...] + b_ref[...]
        o_ref[...] = (v * _sigmoid(v)).astype(BF16)

    row = pl.BlockSpec((tm, d), lambda i: (i, 0))
    vec = pl.BlockSpec((1, d), lambda i: (0, 0))
    return pl.pallas_call(
        body, name=name, grid=(t // tm,), in_specs=[row, vec, vec], out_specs=row,
        out_shape=jax.ShapeDtypeStruct((t, d), BF16), compiler_params=_cp("parallel"),
    )(u, g, bvec)


def ln_silu_bwd(u, g, bvec, dout, name, tm=256):
    t, d = u.shape
    tm = _row_tile(t, tm)

    def body(u_ref, g_ref, b_ref, do_ref, du_ref, dg_ref, db_ref):
        i = pl.program_id(0)
        uv = u_ref[...]
        mu = jnp.mean(uv, axis=-1, keepdims=True)
        xc = uv - mu
        var = jnp.mean(xc * xc, axis=-1, keepdims=True)
        rstd = lax.rsqrt(var + EPS)
        n = xc * rstd
        v = n * g_ref[...] + b_ref[...]
        sg = _sigmoid(v)
        dv = do_ref[...].astype(F32) * (sg * (1.0 + v * (1.0 - sg)))
        dn = dv * g_ref[...]
        du_ref[...] = rstd * (dn - jnp.mean(dn, axis=-1, keepdims=True) - n * jnp.mean(dn * n, axis=-1, keepdims=True))
        dg = jnp.sum(dv * n, axis=0, keepdims=True)
        db = jnp.sum(dv, axis=0, keepdims=True)

        @pl.when(i == 0)
        def _():
            dg_ref[...] = dg
            db_ref[...] = db

        @pl.when(i > 0)
        def _():
            dg_ref[...] += dg
            db_ref[...] += db

    row = pl.BlockSpec((tm, d), lambda i: (i, 0))
    vec = pl.BlockSpec((1, d), lambda i: (0, 0))
    return pl.pallas_call(
        body, name=name, grid=(t // tm,), in_specs=[row, vec, vec, row], out_specs=[row, vec, vec],
        out_shape=[jax.ShapeDtypeStruct((t, d), F32), jax.ShapeDtypeStruct((1, d), F32), jax.ShapeDtypeStruct((1, d), F32)],
        compiler_params=_cp("arbitrary"),
    )(u, g, bvec, dout)


FFN_HALO = 8


def _fill_front_halo(src_ref, pad_ref, nch, ch, halo):
    tc = src_ref.shape[-1]
    pad_ref[0, 0:halo, :] = jnp.zeros((halo, tc), F32)
    pad_ref[0, halo:halo + ch, :] = src_ref[0:ch, :].astype(F32)

    def fill(ci, carry):
        r0 = pl.multiple_of(ci * ch, ch)
        pad_ref[ci, 0:halo, :] = src_ref[pl.ds(r0 - 2 * halo, 2 * halo), :].astype(F32)[halo:, :]
        pad_ref[ci, halo:halo + ch, :] = src_ref[pl.ds(r0, ch), :].astype(F32)
        return carry

    lax.fori_loop(1, nch, fill, 0)


def _taps_front(pad_ref, ci, kw, ch, halo):
    return [pad_ref[ci, halo - (kw - 1 - k):halo - (kw - 1 - k) + ch, :] for k in range(kw)]


def ffn_mid_fwd(p, dw, dwb, name, tc=256):
    b, s, f2 = p.shape
    f = f2 // 2
    kw = dw.shape[0]
    tc = min(tc, f)
    nf = f // tc
    ch = min(CONV_CHUNK, s)
    nch = s // ch
    halo = FFN_HALO

    def body(pa_ref, pg_ref, wa_ref, wg_ref, ba_ref, bg_ref, o_ref, apad_ref, gpad_ref):
        _fill_front_halo(pa_ref, apad_ref, nch, ch, halo)
        _fill_front_halo(pg_ref, gpad_ref, nch, ch, halo)

        def chunk(ci, carry):
            r0 = pl.multiple_of(ci * ch, ch)
            ca = ba_ref[...]
            cg = bg_ref[...]
            taps = zip(_taps_front(apad_ref, ci, kw, ch, halo), _taps_front(gpad_ref, ci, kw, ch, halo))
            for k, (ta, tg) in enumerate(taps):
                ca = ca + wa_ref[k:k + 1, :] * ta
                cg = cg + wg_ref[k:k + 1, :] * tg
            o_ref[pl.ds(r0, ch), :] = ((cg * _sigmoid(cg)) * ca).astype(BF16)
            return carry

        lax.fori_loop(0, nch, chunk, 0)

    blk = lambda off: pl.BlockSpec((None, s, tc), lambda bi, i: (bi, 0, i + off))
    wsp = lambda off: pl.BlockSpec((kw, tc), lambda bi, i: (0, i + off))
    bsp = lambda off: pl.BlockSpec((1, tc), lambda bi, i: (0, i + off))
    return pl.pallas_call(
        body, name=name, grid=(b, nf),
        in_specs=[blk(0), blk(nf), wsp(0), wsp(nf), bsp(0), bsp(nf)],
        out_specs=pl.BlockSpec((None, s, tc), lambda bi, i: (bi, 0, i)),
        out_shape=jax.ShapeDtypeStruct((b, s, f), BF16),
        scratch_shapes=[pltpu.VMEM((nch, ch + halo, tc), F32)] * 2,
        compiler_params=_cp("parallel", "parallel"),
    )(p, p, dw, dw, dwb, dwb)


def ffn_mid_bwd(p, dw, dwb, ds, name, tc=256):
    b, s, f2 = p.shape
    f = f2 // 2
    kw = dw.shape[0]
    tc = min(tc, f)
    nf = f // tc
    ch = min(CONV_CHUNK, s)
    nch = s // ch
    halo = FFN_HALO

    def sum8(v):
        return jnp.sum(v.reshape(ch // 8, 8, tc), axis=0)

    def body(pa_ref, pg_ref, wa_ref, wg_ref, ba_ref, bg_ref, ds_ref, dpa_ref, dpg_ref, ddwa_ref, ddwg_ref, dba_ref, dbg_ref,
             apad_ref, gpad_ref, dca_ref, dcg_ref):
        bi = pl.program_id(1)

        @pl.when(bi == 0)
        def _():
            ddwa_ref[...] = jnp.zeros(ddwa_ref.shape, F32)
            ddwg_ref[...] = jnp.zeros(ddwg_ref.shape, F32)
            dba_ref[...] = jnp.zeros(dba_ref.shape, F32)
            dbg_ref[...] = jnp.zeros(dbg_ref.shape, F32)

        _fill_front_halo(pa_ref, apad_ref, nch, ch, halo)
        _fill_front_halo(pg_ref, gpad_ref, nch, ch, halo)
        dca_ref[nch - 1, ch:ch + halo, :] = jnp.zeros((halo, tc), F32)
        dcg_ref[nch - 1, ch:ch + halo, :] = jnp.zeros((halo, tc), F32)

        def grads(ci, carry):
            acc_a, acc_g, sb_a, sb_g = carry
            r0 = pl.multiple_of(ci * ch, ch)
            taps_a = _taps_front(apad_ref, ci, kw, ch, halo)
            taps_g = _taps_front(gpad_ref, ci, kw, ch, halo)
            ca = ba_ref[...]
            cg = bg_ref[...]
            for k in range(kw):
                ca = ca + wa_ref[k:k + 1, :] * taps_a[k]
                cg = cg + wg_ref[k:k + 1, :] * taps_g[k]
            sg = _sigmoid(cg)
            dsv = ds_ref[pl.ds(r0, ch), :].astype(F32)
            dca = dsv * (cg * sg)
            dcg = dsv * ca * (sg * (1.0 + cg * (1.0 - sg)))
            dca_ref[ci, 0:ch, :] = dca
            dcg_ref[ci, 0:ch, :] = dcg

            prev = jnp.maximum(ci - 1, 0)

            @pl.when(ci > 0)
            def _():
                dca_ref[prev, ch:ch + halo, :] = dca[0:halo, :]
                dcg_ref[prev, ch:ch + halo, :] = dcg[0:halo, :]

            acc_a = tuple(acc_a[k] + sum8(dca * taps_a[k]) for k in range(kw))
            acc_g = tuple(acc_g[k] + sum8(dcg * taps_g[k]) for k in range(kw))
            return acc_a, acc_g, sb_a + sum8(dca), sb_g + sum8(dcg)

        z8 = jnp.zeros((8, tc), F32)
        acc_a, acc_g, sb_a, sb_g = lax.fori_loop(0, nch, grads, ((z8,) * kw, (z8,) * kw, z8, z8))
        for k in range(kw):
            ddwa_ref[k] += acc_a[k]
            ddwg_ref[k] += acc_g[k]
        dba_ref[...] += jnp.sum(sb_a, axis=0, keepdims=True)
        dbg_ref[...] += jnp.sum(sb_g, axis=0, keepdims=True)

        def back(ci, carry):
            r0 = pl.multiple_of(ci * ch, ch)
            da = wa_ref[kw - 1:kw, :] * dca_ref[ci, 0:ch, :]
            dg = wg_ref[kw - 1:kw, :] * dcg_ref[ci, 0:ch, :]
            for j in range(1, kw):
                da = da + wa_ref[kw - 1 - j:kw - j, :] * dca_ref[ci, j:j + ch, :]
                dg = dg + wg_ref[kw - 1 - j:kw - j, :] * dcg_ref[ci, j:j + ch, :]
            dpa_ref[pl.ds(r0, ch), :] = da.astype(BF16)
            dpg_ref[pl.ds(r0, ch), :] = dg.astype(BF16)
            return carry

        lax.fori_loop(0, nch, back, 0)

    blk = lambda off: pl.BlockSpec((None, s, tc), lambda i, bi: (bi, 0, i + off))
    wsp = lambda off: pl.BlockSpec((kw, tc), lambda i, bi: (0, i + off))
    bsp = lambda off: pl.BlockSpec((1, tc), lambda i, bi: (0, i + off))
    acc3 = pl.BlockSpec((kw, 8, tc), lambda i, bi: (0, 0, i))
    vec = pl.BlockSpec((1, tc), lambda i, bi: (0, i))
    return pl.pallas_call(
        body, name=name, grid=(nf, b),
        in_specs=[blk(0), blk(nf), wsp(0), wsp(nf), bsp(0), bsp(nf), blk(0)],
        out_specs=[blk(0), blk(0), acc3, acc3, vec, vec],
        out_shape=[jax.ShapeDtypeStruct((b, s, f), BF16), jax.ShapeDtypeStruct((b, s, f), BF16),
                   jax.ShapeDtypeStruct((kw, 8, f), F32), jax.ShapeDtypeStruct((kw, 8, f), F32),
                   jax.ShapeDtypeStruct((1, f), F32), jax.ShapeDtypeStruct((1, f), F32)],
        scratch_shapes=[pltpu.VMEM((nch, ch + halo, tc), F32)] * 4,
        compiler_params=_cp("parallel", "arbitrary"),
    )(p, p, dw, dw, dwb, dwb, ds)


def _tile_rows(r, n, dil):
    start = r + n * BLK * dil
    return pl.ds(start, BLK, stride=dil) if dil > 1 else pl.ds(start, BLK)


def _band_masks():
    qi = lax.broadcasted_iota(jnp.int32, (BLK, 2 * BLK), 0)
    kk = lax.broadcasted_iota(jnp.int32, (BLK, 2 * BLK), 1)
    both = jnp.logical_or(jnp.logical_and(kk < BLK, kk >= qi), jnp.logical_and(kk >= BLK, kk - BLK <= qi))
    return both, kk[:, :BLK] <= qi[:, :BLK]


def attn_fwd(q, kv, g, dil, hw, name):
    b, s, _ = q.shape
    nh = hw // HEAD_DIM
    nblk = s // dil // BLK
    scale = 1.0 / math.sqrt(HEAD_DIM)

    def body(q_ref, k_ref, v_ref, o_ref, lse_ref):
        mask2, mask1 = _band_masks()
        for r in range(dil):
            kp = vp = None
            for n in range(nblk):
                rs = _tile_rows(r, n, dil)
                qt = q_ref[rs, :].astype(BF16)
                kc = k_ref[rs, :].astype(BF16)
                vc = v_ref[rs, :].astype(BF16)
                if n == 0:
                    kcat, vcat, mask = kc, vc, mask1
                else:
                    kcat, vcat, mask = jnp.concatenate([kp, kc], axis=0), jnp.concatenate([vp, vc], axis=0), mask2
                sc = jnp.where(mask, _dot_nt(qt, kcat) * scale, NEG_INF)
                m = jnp.max(sc, axis=-1, keepdims=True)
                p = jnp.exp(sc - m)
                den = jnp.sum(p, axis=-1, keepdims=True)
                o_ref[rs, :] = _dot(p.astype(BF16), vcat) / den
                lse_ref[rs, :] = jnp.broadcast_to(m + jnp.log(den), (BLK, HEAD_DIM))
                kp, vp = kc, vc

    col = lambda base: pl.BlockSpec((None, s, HEAD_DIM), lambda bi, h: (bi, 0, base + h))
    return pl.pallas_call(
        body, name=name, grid=(b, nh),
        in_specs=[col(g * nh), col(g * nh), col((N_GROUPS + g) * nh)],
        out_specs=[col(0), col(0)],
        out_shape=[jax.ShapeDtypeStruct((b, s, hw), F32), jax.ShapeDtypeStruct((b, s, hw), F32)],
        compiler_params=_cp("parallel", "parallel"),
    )(q, kv, kv)


def attn_merge(outs, lses, name, tm=256):
    t, hw = outs[0].shape
    tm = _row_tile(t, tm)
    ng = len(outs)

    def body(*refs):
        o_refs, l_refs = refs[:ng], refs[ng:2 * ng]
        m_ref, lj_ref = refs[2 * ng:]
        ls = [l_refs[g][...] for g in range(ng)]
        mx = ls[0]
        for g in range(1, ng):
            mx = jnp.maximum(mx, ls[g])
        es = [jnp.exp(l - mx) for l in ls]
        tot = es[0]
        for g in range(1, ng):
            tot = tot + es[g]
        acc = (es[0] / tot) * o_refs[0][...]
        for g in range(1, ng):
            acc = acc + (es[g] / tot) * o_refs[g][...]
        m_ref[...] = acc.astype(BF16)
        lj_ref[...] = mx + jnp.log(tot)

    row = pl.BlockSpec((tm, hw), lambda i: (i, 0))
    return pl.pallas_call(
        body, name=name, grid=(t // tm,), in_specs=[row] * (2 * ng), out_specs=[row, row],
        out_shape=[jax.ShapeDtypeStruct((t, hw), BF16), jax.ShapeDtypeStruct((t, hw), F32)],
        compiler_params=_cp("parallel"),
    )(*outs, *lses)


def attn_bwd_prep(dmerged, merged, name, tm=256):
    t, hw = merged.shape
    nh = hw // HEAD_DIM
    tm = _row_tile(t, tm)

    def body(d_ref, m_ref, o_ref):
        for h in range(nh):
            sl = slice(h * HEAD_DIM, (h + 1) * HEAD_DIM)
            dsum = jnp.sum(d_ref[:, sl] * m_ref[:, sl].astype(F32), axis=-1, keepdims=True)
            o_ref[:, sl] = jnp.broadcast_to(dsum, (tm, HEAD_DIM))

    row = pl.BlockSpec((tm, hw), lambda i: (i, 0))
    return pl.pallas_call(
        body, name=name, grid=(t // tm,), in_specs=[row, row], out_specs=row,
        out_shape=jax.ShapeDtypeStruct((t, hw), F32), compiler_params=_cp("parallel"),
    )(dmerged, merged)


def attn_bwd(q, kv, g, dil, do, lsej, dm, dq_buf, dk_buf, dv_buf, accumulate, hw, name):
    b, s, _ = q.shape
    nh = hw // HEAD_DIM
    nblk = s // dil // BLK
    scale = 1.0 / math.sqrt(HEAD_DIM)
    assert dk_buf is not None or not accumulate
    kv_at = 6 + (dq_buf is not None)

    def body(*refs):
        q_ref, k_ref, v_ref, do_ref, lj_ref, dm_ref = refs[:6]
        dq_ref, dk_ref, dv_ref = refs[-3:]
        dki_ref, dvi_ref = (refs[kv_at], refs[kv_at + 1]) if accumulate else (None, None)
        mask2, mask1 = _band_masks()

        def put(rs, dk, dv):
            if accumulate:
                dk = dk + dki_ref[rs, :]
                dv = dv + dvi_ref[rs, :]
            dk_ref[rs, :] = dk
            dv_ref[rs, :] = dv

        for r in range(dil):
            kp = vp = hold_k = hold_v = rs_prev = None
            for n in range(nblk):
                rs = _tile_rows(r, n, dil)
                qt = q_ref[rs, :].astype(BF16)
                kc = k_ref[rs, :].astype(BF16)
                vc = v_ref[rs, :].astype(BF16)
                dot = do_ref[rs, :].astype(BF16)
                lm = lj_ref[rs, :]
                dmm = dm_ref[rs, :]
                if n == 0:
                    kcat, vcat, mask = kc, vc, mask1
                else:
                    kcat, vcat, mask = jnp.concatenate([kp, kc], axis=0), jnp.concatenate([vp, vc], axis=0), mask2
                    lm = jnp.concatenate([lm, lm], axis=1)
                    dmm = jnp.concatenate([dmm, dmm], axis=1)
                p = jnp.exp(jnp.where(mask, _dot_nt(qt, kcat) * scale, NEG_INF) - lm)
                ds = (p * (_dot_nt(dot, vcat) - dmm)).astype(BF16)
                dq_ref[rs, :] = _dot(ds, kcat) * scale
                dkc = _dot_tn(ds, qt) * scale
                dvc = _dot_tn(p.astype(BF16), dot)
                if n > 0:
                    put(rs_prev, hold_k + dkc[:BLK, :], hold_v + dvc[:BLK, :])
                    dkc, dvc = dkc[BLK:, :], dvc[BLK:, :]
                hold_k, hold_v, kp, vp, rs_prev = dkc, dvc, kc, vc, rs
            put(rs_prev, hold_k, hold_v)

    col = lambda base: pl.BlockSpec((None, s, HEAD_DIM), lambda bi, h: (bi, 0, base + h))
    any_spec = pl.BlockSpec(memory_space=pl.ANY)
    in_specs = [col(g * nh), col(g * nh), col((N_GROUPS + g) * nh), col(0), col(0), col(0)]
    args = [q, kv, kv, do, lsej, dm]
    aliases = {}
    if dq_buf is not None:
        in_specs.append(any_spec)
        args.append(dq_buf)
        aliases[6] = 0
    if dk_buf is not None:
        in_specs += [col(g * nh) if accumulate else any_spec] * 2
        args += [dk_buf, dv_buf]
        aliases.update({kv_at: 1, kv_at + 1: 2})
    shape = jax.ShapeDtypeStruct((b, s, N_GROUPS * hw), F32)
    return pl.pallas_call(
        body, name=name, grid=(b, nh), in_specs=in_specs, out_specs=[col(g * nh)] * 3, out_shape=[shape] * 3,
        input_output_aliases=aliases, compiler_params=_cp("parallel", "parallel"),
    )(*args)


def sum_parts(g, recv, me, name, tm=256):
    _, rows, c = g.shape
    n = recv.shape[0]
    tm = _row_tile(rows, tm)

    def body(me_ref, g_ref, r_ref, o_ref):
        acc = g_ref[...].astype(F32)
        for j in range(n):
            acc = acc + r_ref[j].astype(F32)
        o_ref[...] = acc

    return pl.pallas_call(
        body, name=name,
        grid_spec=pltpu.PrefetchScalarGridSpec(
            num_scalar_prefetch=1, grid=(rows // tm,),
            in_specs=[pl.BlockSpec((None, tm, c), lambda i, me_ref: (me_ref[0], i, 0)),
                      pl.BlockSpec((n, tm, c), lambda i, me_ref: (0, i, 0))],
            out_specs=pl.BlockSpec((tm, c), lambda i, me_ref: (i, 0))),
        out_shape=jax.ShapeDtypeStruct((rows, c), F32), compiler_params=_cp("parallel"),
    )(me, g, recv)


def adamw(w, m, v, g_parts, name, tm=256):
    rows, c = w.shape
    tm = _row_tile(rows, tm)
    npart = len(g_parts)

    def body(*refs):
        w_ref, m_ref, v_ref = refs[:3]
        g_refs = refs[3:3 + npart]
        go_ref, d_ref, mo_ref, vo_ref = refs[3 + npart:]
        g = g_refs[0][...]
        for k in range(1, npart):
            g = g + g_refs[k][...]
        mn = ADAM_B1 * m_ref[...] + (1.0 - ADAM_B1) * g
        vn = ADAM_B2 * v_ref[...] + (1.0 - ADAM_B2) * (g * g)
        m_hat = mn / (1.0 - ADAM_B1 ** ADAM_STEP)
        v_hat = vn / (1.0 - ADAM_B2 ** ADAM_STEP)
        go_ref[...] = g
        d_ref[...] = -ADAM_LR * (m_hat / (jnp.sqrt(v_hat) + ADAM_EPS) + ADAM_WD * w_ref[...])
        mo_ref[...] = mn
        vo_ref[...] = vn

    row = pl.BlockSpec((tm, c), lambda i: (i, 0))
    return pl.pallas_call(
        body, name=name, grid=(rows // tm,), in_specs=[row] * (3 + npart), out_specs=[row] * 4,
        out_shape=[jax.ShapeDtypeStruct((rows, c), F32)] * 4, compiler_params=_cp("parallel"),
    )(w, m, v, *g_parts)


def _place():
    return lax.axis_index("x"), lax.axis_index("y"), lax.axis_index("c")


def _other_chips(x, y, c):
    return [(1 - x, y, c), (x, 1 - y, c), (1 - x, 1 - y, c)]


def _chip_of(px, py):
    return 2 * px + py


HBM_SPEC = pl.BlockSpec(memory_space=pltpu.HBM)
SEM_SPEC = pl.BlockSpec(memory_space=pltpu.SEMAPHORE)
ANY_SPEC = pl.BlockSpec(memory_space=pl.ANY)
DATAFLOW = pltpu.SideEffectType.DATAFLOW_SIDE_EFFECTING
N_PEER_CHIPS = N_CHIPS - 1


def _hbm(a):
    return pltpu.with_memory_space_constraint(a, pltpu.HBM)


def _hbm_like(arrays):
    return [pltpu.HBM(a.shape, a.dtype) for a in arrays]


def cast_place(w, layer, me, out_dtype, name, tm=256, nslots=N_CHIPS):
    rows, c = w.shape[-2:]
    tm = _row_tile(rows, tm)

    def body(me_ref, w_ref, o_ref):
        o_ref[...] = w_ref[...].astype(out_dtype)

    if layer is None:
        in_spec = pl.BlockSpec((tm, c), lambda i, me_ref: (i, 0))
    else:
        in_spec = pl.BlockSpec((None, tm, c), lambda i, me_ref: (layer, i, 0))
    return pl.pallas_call(
        body, name=name,
        grid_spec=pltpu.PrefetchScalarGridSpec(
            num_scalar_prefetch=1, grid=(rows // tm,), in_specs=[in_spec],
            out_specs=pl.BlockSpec((None, tm, c), lambda i, me_ref: (me_ref[0], i, 0))),
        out_shape=jax.ShapeDtypeStruct((nslots, rows, c), out_dtype), compiler_params=_cp("parallel"),
    )(me, w)


def gather_start(lands, chunk_sizes, name="gather_start"):
    n = len(lands)
    nch = len(chunk_sizes)
    assert sum(chunk_sizes) == n

    def body(*refs):
        land_refs = refs[:n]
        outs = refs[n:]
        send_sems, recv_sems = outs[:nch], outs[nch:2 * nch]
        token = outs[-1]
        x, y, c = _place()
        me = _chip_of(x, y)
        peers = _other_chips(x, y, c)
        k = 0
        for ck, size in enumerate(chunk_sizes):
            for pos in range(size):
                for r, peer in enumerate(peers):
                    pltpu.make_async_remote_copy(
                        src_ref=land_refs[k].at[me], dst_ref=land_refs[k].at[me],
                        send_sem=send_sems[ck].at[N_PEER_CHIPS * pos + r], recv_sem=recv_sems[ck].at[N_PEER_CHIPS * pos + r],
                        device_id=peer, device_id_type=MESH).start()
                k += 1
        token[...] = jnp.zeros(token.shape, F32)

    sems = [pltpu.SemaphoreType.DMA((N_PEER_CHIPS * s,)) for s in chunk_sizes]
    res = pl.pallas_call(
        body, name=name,
        out_shape=(*sems, *sems, *_hbm_like(lands), jax.ShapeDtypeStruct(DEP_SPEC_SHAPE, F32)),
        in_specs=[HBM_SPEC] * n,
        out_specs=(*[SEM_SPEC] * (2 * nch), *[HBM_SPEC] * n, pl.BlockSpec(memory_space=pltpu.VMEM)),
        input_output_aliases={k: 2 * nch + k for k in range(n)},
        compiler_params=pltpu.CompilerParams(has_side_effects=DATAFLOW),
    )(*[_hbm(a) for a in lands])
    return res[:nch], res[nch:2 * nch], res[2 * nch:2 * nch + n], res[-1]


def gather_wait(send_sem, recv_sem, lands, after, name):
    n = len(lands)

    def body(*refs):
        land_refs = refs[:n]
        ssem, rsem = refs[n], refs[n + 1]
        x, y, c = _place()
        me = _chip_of(x, y)
        for pos in range(n):
            for r, peer in enumerate(_other_chips(x, y, c)):
                cp = pltpu.make_async_remote_copy(
                    src_ref=land_refs[pos].at[me], dst_ref=land_refs[pos].at[_chip_of(peer[0], peer[1])],
                    send_sem=ssem.at[N_PEER_CHIPS * pos + r], recv_sem=rsem.at[N_PEER_CHIPS * pos + r],
                    device_id=peer, device_id_type=MESH)
                cp.wait_send()
                cp.wait_recv()

    return pl.pallas_call(
        body, name=name, out_shape=tuple(_hbm_like(lands)),
        in_specs=[*[HBM_SPEC] * n, SEM_SPEC, SEM_SPEC, ANY_SPEC], out_specs=[HBM_SPEC] * n,
        input_output_aliases={k: k for k in range(n)},
        compiler_params=pltpu.CompilerParams(has_side_effects=DATAFLOW),
    )(*lands, send_sem, recv_sem, after)


def scatter_start(grads, name):
    n = len(grads)
    recvs = [lax.empty((N_PEER_CHIPS, *g.shape[1:]), g.dtype) for g in grads]

    def body(*refs):
        g_refs, r_refs = refs[:n], refs[n:2 * n]
        send_sems, recv_sems = refs[2 * n], refs[2 * n + 1]
        token = refs[-1]
        x, y, c = _place()
        for k in range(n):
            for r, peer in enumerate(_other_chips(x, y, c)):
                pltpu.make_async_remote_copy(
                    src_ref=g_refs[k].at[_chip_of(peer[0], peer[1])], dst_ref=r_refs[k].at[r],
                    send_sem=send_sems.at[N_PEER_CHIPS * k + r], recv_sem=recv_sems.at[N_PEER_CHIPS * k + r],
                    device_id=peer, device_id_type=MESH).start()
        token[...] = jnp.zeros(token.shape, F32)

    sem = pltpu.SemaphoreType.DMA((N_PEER_CHIPS * n,))
    res = pl.pallas_call(
        body, name=name,
        out_shape=(sem, sem, *_hbm_like(grads), *_hbm_like(recvs), jax.ShapeDtypeStruct(DEP_SPEC_SHAPE, F32)),
        in_specs=[HBM_SPEC] * (2 * n),
        out_specs=(SEM_SPEC, SEM_SPEC, *[HBM_SPEC] * (2 * n), pl.BlockSpec(memory_space=pltpu.VMEM)),
        input_output_aliases={k: 2 + k for k in range(2 * n)},
        compiler_params=pltpu.CompilerParams(has_side_effects=DATAFLOW),
    )(*[_hbm(a) for a in grads], *[_hbm(a) for a in recvs])
    return res[0], res[1], res[2:2 + n], res[2 + n:2 + 2 * n], res[-1]


def scatter_wait(send_sem, recv_sem, grads, recvs, after, name):
    n = len(grads)

    def body(*refs):
        g_refs, r_refs = refs[:n], refs[n:2 * n]
        ssem, rsem = refs[2 * n], refs[2 * n + 1]
        x, y, c = _place()
        for k in range(n):
            for r, peer in enumerate(_other_chips(x, y, c)):
                cp = pltpu.make_async_remote_copy(
                    src_ref=g_refs[k].at[_chip_of(peer[0], peer[1])], dst_ref=r_refs[k].at[r],
                    send_sem=ssem.at[N_PEER_CHIPS * k + r], recv_sem=rsem.at[N_PEER_CHIPS * k + r],
                    device_id=peer, device_id_type=MESH)
                cp.wait_send()
                cp.wait_recv()

    res = pl.pallas_call(
        body, name=name, out_shape=(*_hbm_like(grads), *_hbm_like(recvs)),
        in_specs=[*[HBM_SPEC] * (2 * n), SEM_SPEC, SEM_SPEC, ANY_SPEC], out_specs=[HBM_SPEC] * (2 * n),
        input_output_aliases={k: k for k in range(2 * n)},
        compiler_params=pltpu.CompilerParams(has_side_effects=DATAFLOW),
    )(*grads, *recvs, send_sem, recv_sem, after)
    return res[:n], res[n:]


def swap_start(parts, name):
    n = len(parts)
    lands = [lax.empty(p.shape, p.dtype) for p in parts]

    def body(*refs):
        p_refs, l_refs = refs[:n], refs[n:2 * n]
        sems = refs[2 * n:4 * n]
        token = refs[-1]
        x, y, c = _place()
        for k in range(n):
            pltpu.make_async_remote_copy(
                src_ref=p_refs[k], dst_ref=l_refs[k], send_sem=sems[k], recv_sem=sems[n + k],
                device_id=(x, y, 1 - c), device_id_type=MESH).start()
        token[...] = jnp.zeros(token.shape, F32)

    sem = pltpu.SemaphoreType.DMA(())
    res = pl.pallas_call(
        body, name=name,
        out_shape=(*[sem] * (2 * n), *_hbm_like(parts), *_hbm_like(lands), jax.ShapeDtypeStruct(DEP_SPEC_SHAPE, F32)),
        in_specs=[HBM_SPEC] * (2 * n),
        out_specs=(*[SEM_SPEC] * (2 * n), *[HBM_SPEC] * (2 * n), pl.BlockSpec(memory_space=pltpu.VMEM)),
        input_output_aliases={k: 2 * n + k for k in range(2 * n)},
        compiler_params=pltpu.CompilerParams(has_side_effects=DATAFLOW),
    )(*[_hbm(a) for a in parts], *[_hbm(a) for a in lands])
    return res[:n], res[n:2 * n], res[2 * n:3 * n], res[3 * n:4 * n], res[-1]


def swap_wait(send_sem, recv_sem, part, land, after, name):
    def body(p_ref, l_ref, ssem, rsem, after_ref, p_out, l_out):
        x, y, c = _place()
        cp = pltpu.make_async_remote_copy(src_ref=p_ref, dst_ref=l_ref, send_sem=ssem, recv_sem=rsem,
                                          device_id=(x, y, 1 - c), device_id_type=MESH)
        cp.wait_send()
        cp.wait_recv()

    return pl.pallas_call(
        body, name=name, out_shape=tuple(_hbm_like([part, land])),
        in_specs=[HBM_SPEC, HBM_SPEC, SEM_SPEC, SEM_SPEC, ANY_SPEC], out_specs=[HBM_SPEC, HBM_SPEC],
        input_output_aliases={0: 0, 1: 1},
        compiler_params=pltpu.CompilerParams(has_side_effects=DATAFLOW),
    )(part, land, send_sem, recv_sem, after)


def _xor_peer(x, y, c, k):
    px, py, pc = x ^ ((k >> 2) & 1), y ^ ((k >> 1) & 1), c ^ (k & 1)
    return (px, py, pc), 4 * px + 2 * py + pc


def small_start(land, name="small_start"):
    def body(l_ref, ssem, rsem, l_out, token):
        x, y, c = _place()
        me = 4 * x + 2 * y + c
        for k in range(1, N_DEV):
            peer, _ = _xor_peer(x, y, c, k)
            pltpu.make_async_remote_copy(
                src_ref=l_ref.at[me], dst_ref=l_ref.at[me], send_sem=ssem.at[k - 1], recv_sem=rsem.at[k - 1],
                device_id=peer, device_id_type=MESH).start()
        token[...] = jnp.zeros(token.shape, F32)

    sem = pltpu.SemaphoreType.DMA((N_DEV - 1,))
    return pl.pallas_call(
        body, name=name,
        out_shape=(sem, sem, pltpu.HBM(land.shape, land.dtype), jax.ShapeDtypeStruct(DEP_SPEC_SHAPE, F32)),
        in_specs=[HBM_SPEC], out_specs=(SEM_SPEC, SEM_SPEC, HBM_SPEC, pl.BlockSpec(memory_space=pltpu.VMEM)),
        input_output_aliases={0: 2}, compiler_params=pltpu.CompilerParams(has_side_effects=DATAFLOW),
    )(_hbm(land))


def small_wait(send_sem, recv_sem, land, after, name="small_wait"):
    def body(l_ref, ssem, rsem, after_ref, l_out):
        x, y, c = _place()
        me = 4 * x + 2 * y + c
        for k in range(1, N_DEV):
            peer, slot = _xor_peer(x, y, c, k)
            cp = pltpu.make_async_remote_copy(
                src_ref=l_ref.at[me], dst_ref=l_ref.at[slot], send_sem=ssem.at[k - 1], recv_sem=rsem.at[k - 1],
                device_id=peer, device_id_type=MESH)
            cp.wait_send()
            cp.wait_recv()

    return pl.pallas_call(
        body, name=name, out_shape=pltpu.HBM(land.shape, land.dtype),
        in_specs=[HBM_SPEC, SEM_SPEC, SEM_SPEC, ANY_SPEC], out_specs=HBM_SPEC, input_output_aliases={0: 0},
        compiler_params=pltpu.CompilerParams(has_side_effects=DATAFLOW),
    )(land, send_sem, recv_sem, after)


def sum_slots(land, name="sum_slots", tm=256):
    n, rows, c = land.shape
    tm = _row_tile(rows, tm)

    def body(l_ref, o_ref):
        acc = l_ref[0]
        for j in range(1, n):
            acc = acc + l_ref[j]
        o_ref[...] = acc

    return pl.pallas_call(
        body, name=name, grid=(rows // tm,), in_specs=[pl.BlockSpec((n, tm, c), lambda i: (0, i, 0))],
        out_specs=pl.BlockSpec((tm, c), lambda i: (i, 0)), out_shape=jax.ShapeDtypeStruct((rows, c), F32),
        compiler_params=_cp("parallel"),
    )(land)


def _pack(arrays):
    flat = jnp.concatenate([a.reshape(-1).astype(F32) for a in arrays])
    n = flat.shape[0]
    rows = -(-n // LANES)
    rows = -(-rows // 8) * 8
    return jnp.pad(flat, (0, rows * LANES - n)).reshape(rows, LANES)


def _unpack(packed, shapes, lead=()):
    flat = packed.reshape(*lead, -1)
    out, off = [], 0
    for shp in shapes:
        n = math.prod(shp)
        out.append(flat[..., off:off + n].reshape(*lead, *shp))
        off += n
    return out


def _row(vec):
    return vec.reshape(1, -1)


def kernel(x, mix_pre_g, mix_post_g, ffn_pre_g, ffn_post_g, cm_w_in, cm_b_in, cm_dw, cm_dw_b, cm_ln_g, cm_ln_b, cm_w_out, cm_b_out, kv_norm_g, w_kv, w_q, w_o, ffn_w_in, ffn_dw, ffn_dw_b, ffn_w_out, loss_target, m_mix_pre_g, m_mix_post_g, m_ffn_pre_g, m_ffn_post_g, m_cm_w_in, m_cm_b_in, m_cm_dw, m_cm_dw_b, m_cm_ln_g, m_cm_ln_b, m_cm_w_out, m_cm_b_out, m_kv_norm_g, m_w_kv, m_w_q, m_w_o, m_ffn_w_in, m_ffn_dw, m_ffn_dw_b, m_ffn_w_out, v_mix_pre_g, v_mix_post_g, v_ffn_pre_g, v_ffn_post_g, v_cm_w_in, v_cm_b_in, v_cm_dw, v_cm_dw_b, v_cm_ln_g, v_cm_ln_b, v_cm_w_out, v_cm_b_out, v_kv_norm_g, v_w_kv, v_w_q, v_w_o, v_ffn_w_in, v_ffn_dw, v_ffn_dw_b, v_ffn_w_out):
    names = ["mix_pre_g", "mix_post_g", "ffn_pre_g", "ffn_post_g", "cm_w_in", "cm_b_in", "cm_dw", "cm_dw_b", "cm_ln_g",
             "cm_ln_b", "cm_w_out", "cm_b_out", "kv_norm_g", "w_kv", "w_q", "w_o", "ffn_w_in", "ffn_dw", "ffn_dw_b",
             "ffn_w_out"]
    w_in = dict(zip(names, [mix_pre_g, mix_post_g, ffn_pre_g, ffn_post_g, cm_w_in, cm_b_in, cm_dw, cm_dw_b, cm_ln_g,
                            cm_ln_b, cm_w_out, cm_b_out, kv_norm_g, w_kv, w_q, w_o, ffn_w_in, ffn_dw, ffn_dw_b, ffn_w_out]))
    m_in = dict(zip(names, [m_mix_pre_g, m_mix_post_g, m_ffn_pre_g, m_ffn_post_g, m_cm_w_in, m_cm_b_in, m_cm_dw, m_cm_dw_b,
                            m_cm_ln_g, m_cm_ln_b, m_cm_w_out, m_cm_b_out, m_kv_norm_g, m_w_kv, m_w_q, m_w_o, m_ffn_w_in,
                            m_ffn_dw, m_ffn_dw_b, m_ffn_w_out]))
    v_in = dict(zip(names, [v_mix_pre_g, v_mix_post_g, v_ffn_pre_g, v_ffn_post_g, v_cm_w_in, v_cm_b_in, v_cm_dw, v_cm_dw_b,
                            v_cm_ln_g, v_cm_ln_b, v_cm_w_out, v_cm_b_out, v_kv_norm_g, v_w_kv, v_w_q, v_w_o, v_ffn_w_in,
                            v_ffn_dw, v_ffn_dw_b, v_ffn_w_out]))

    bsz, seq, d = x.shape
    t = bsz * seq
    n_b = DEPTH - N_A
    hw = w_o.shape[-1]
    qw = N_GROUPS * hw
    f2 = ffn_dw_b.shape[-1]
    f = f2 // 2
    me_chip = _chip_of(lax.axis_index("x"), lax.axis_index("y"))

    big = ["cm_w_in", "cm_w_out", "w_kv", "w_q", "w_o", "ffn_w_in", "ffn_w_out"]
    row_sharded = ("cm_w_out", "w_o", "ffn_w_out")
    small_sharded = ["cm_b_in", "cm_dw", "cm_dw_b", "cm_ln_g", "cm_ln_b", "cm_b_out", "ffn_dw"]
    small_pack = _pack([w_in[n] for n in small_sharded])
    chunks = [
        [("cm_w_in", 0), ("small", None)],
        [("cm_w_out", 0)],
        [("ffn_w_in", 0), ("ffn_w_out", 0)],
        [("cm_w_in", 1), ("cm_w_out", 1)],
        [("ffn_w_in", 1), ("ffn_w_out", 1)],
        [("w_kv", None)],
        [("w_q", 0), ("w_o", 0)],
        [("ffn_w_in", 2), ("ffn_w_out", 2)],
        [("w_q", 1), ("w_o", 1)],
        [("ffn_w_in", 3), ("ffn_w_out", 3)],
    ]
    pieces = [pc for ch in chunks for pc in ch]
    chunk_of = {pc: ck for ck, ch in enumerate(chunks) for pc in ch}

    me_arr = me_chip.astype(jnp.int32).reshape(1)

    def land_of(pc):
        n, l = pc
        if n == "small":
            return cast_place(small_pack, None, me_arr, F32, name="place_small")
        return cast_place(w_in[n], l, me_arr, BF16, name=f"place_{n}_{l}")

    lands = [land_of(pc) for pc in pieces]
    g_send, g_recv, lands_f, token = gather_start(lands, [len(ch) for ch in chunks])
    weights = {}

    def finish_chunk(ck, after):
        lo = sum(len(ch) for ch in chunks[:ck])
        hi = lo + len(chunks[ck])
        got = gather_wait(g_send[ck], g_recv[ck], lands_f[lo:hi], after, name=f"gather_wait{ck}")
        for pc, arr in zip(chunks[ck], got):
            weights[pc] = arr.reshape(1, -1, arr.shape[-1]) if pc[0] in row_sharded else arr

    def wmat(n, l=None, after=None):
        if (n, l) not in weights:
            finish_chunk(chunk_of[(n, l)], after)
        arr = weights[(n, l)]
        return arr, arr.shape[0]

    finish_chunk(0, token)
    small_full = {}
    for n, arr4 in zip(small_sharded, _unpack(weights[("small", None)], [w_in[n].shape for n in small_sharded], lead=(N_CHIPS,))):
        shp = w_in[n].shape
        small_full[n] = jnp.moveaxis(arr4, 0, -2).reshape(*shp[:-1], N_CHIPS * shp[-1])

    x2d = x.reshape(t, d)
    saved = []
    (h1,) = resid_norm_fwd(x2d, None, None, [_row(mix_pre_g[0])], name="norm_in", dep=token)
    xcur = x2d
    kv_state = None
    for i in range(DEPTH):
        sv = {"x_in": xcur, "h1": h1}
        if i < N_A:
            z = mm_nn(h1, *wmat("cm_w_in", i, h1), 1, 0, bias=_row(small_full["cm_b_in"][i]), name=f"cm_in{i}")
            u2 = glu_conv_fwd(z.reshape(bsz, seq, 2 * d), small_full["cm_dw"][i], _row(small_full["cm_dw_b"][i]),
                              name=f"glu_conv{i}").reshape(t, d)
            u4 = ln_silu_fwd(u2, _row(small_full["cm_ln_g"][i]), _row(small_full["cm_ln_b"][i]), name=f"ln_silu{i}")
            y = mm_nn(u4, *wmat("cm_w_out", i, u4), 1, 0, bias=_row(small_full["cm_b_out"][i]), name=f"cm_out{i}")
            sv.update(z=z, u2=u2, u4=u4)
        else:
            j = i - N_A
            q = mm_nn(h1, *wmat("w_q", j, h1), 1, 0, name=f"q_proj{j}").reshape(bsz, seq, qw)
            outs, lses = [], []
            for g, dil in enumerate(DILATIONS):
                o_g, l_g = attn_fwd(q, kv_state["kv"], g, dil, hw, name=f"attn_fwd{j}_{g}")
                outs.append(o_g.reshape(t, hw))
                lses.append(l_g.reshape(t, hw))
            merged, lsej = attn_merge(outs, lses, name=f"attn_merge{j}")
            y = mm_nn(merged, *wmat("w_o", j, merged), 1, 0, name=f"o_proj{j}")
            sv.update(q=q, merged=merged, lsej=lsej)
        x1, h2 = resid_norm_fwd(xcur, y, _row(mix_post_g[i]), [_row(ffn_pre_g[i])], name=f"resid_mix{i}")
        p = mm_nn(h2, *wmat("ffn_w_in", i, h2), 1, 0, out_dtype=BF16, name=f"ffn_in{i}")
        s_act = ffn_mid_fwd(p.reshape(bsz, seq, f2), small_full["ffn_dw"][i], _row(ffn_dw_b[i]), name=f"ffn_mid{i}").reshape(t, f)
        y2 = mm_nn(s_act, *wmat("ffn_w_out", i), 1, 0, name=f"ffn_out{i}")
        next_gains = []
        if i + 1 < DEPTH:
            next_gains.append(_row(mix_pre_g[i + 1]))
        if i == N_A - 1:
            next_gains.append(_row(kv_norm_g))
        res = resid_norm_fwd(x1, y2, _row(ffn_post_g[i]), next_gains, name=f"resid_ffn{i}")
        sv.update(y=y, x1=x1, h2=h2, p=p, s=s_act, y2=y2)
        saved.append(sv)
        xcur = res[0]
        if i + 1 < DEPTH:
            h1 = res[1]
        if i == N_A - 1:
            kvn = res[2]
            kv = mm_nn(kvn, *wmat("w_kv", None, kvn), 1, 0, name="kv_proj").reshape(bsz, seq, 2 * qw)
            kv_state = {"kv": kv, "kvn": kvn, "x_a": xcur}

    dx, loss_tile = loss_fwd_bwd(xcur, loss_target.reshape(t, d))
    loss = lax.psum(loss_tile[0, 0], ("x", "y", "c"))

    gsm = {n: [None] * w_in[n].shape[0] for n in
           ["mix_pre_g", "mix_post_g", "ffn_pre_g", "ffn_post_g", "cm_b_in", "cm_dw", "cm_dw_b", "cm_ln_g", "cm_ln_b",
            "cm_b_out", "ffn_dw", "ffn_dw_b"]}
    gbig = {}
    in_flight = []
    dep = None

    def start_scatter(pcs, tag):
        ssem, rsem, g_f, r_f, tok = scatter_start([gbig[pc] for pc in pcs], name=f"scatter_start_{tag}")
        in_flight.append((pcs, ssem, rsem, g_f, r_f))
        return tok

    dk_buf = dv_buf = None
    for i in range(DEPTH - 1, -1, -1):
        sv = saved[i]
        dy2, dg, _ = norm_bwd(sv["y2"], _row(ffn_post_g[i]), dx, out_dtype=BF16, name=f"bwd_ffn_post{i}", dep=dep)
        gsm["ffn_post_g"][i] = dg
        ds = mm_nt(dy2, *wmat("ffn_w_out", i), 1, 0, out_dtype=BF16, name=f"bwd_ffn_out_dx{i}")
        gbig[("ffn_w_out", i)] = mm_tn(sv["s"], dy2, 1, name=f"bwd_ffn_out_dw{i}").reshape(N_CHIPS, f // N_CHIPS, d)
        dpa, dpg, ddwa, ddwg, ddba, ddbg = ffn_mid_bwd(sv["p"].reshape(bsz, seq, f2), small_full["ffn_dw"][i], _row(ffn_dw_b[i]),
                                                       ds.reshape(bsz, seq, f), name=f"bwd_ffn_mid{i}")
        gsm["ffn_dw"][i] = jnp.concatenate([jnp.sum(ddwa, axis=1), jnp.sum(ddwg, axis=1)], axis=-1)
        gsm["ffn_dw_b"][i] = jnp.concatenate([ddba, ddbg], axis=-1)
        dp = [dpa.reshape(t, f), dpg.reshape(t, f)]
        dh2 = mm_nt(dp, *wmat("ffn_w_in", i), 1, 0, name=f"bwd_ffn_in_dx{i}")
        gbig[("ffn_w_in", i)] = mm_tn(sv["h2"], dp, N_CHIPS, name=f"bwd_ffn_in_dw{i}")
        dx1, dg, _ = norm_bwd(sv["x1"], _row(ffn_pre_g[i]), dh2, add=dx, name=f"bwd_ffn_pre{i}")
        gsm["ffn_pre_g"][i] = dg
        dep = start_scatter([("ffn_w_in", 0), ("ffn_w_out", 0)], "ffn0") if i == 0 else None
        dy, dg, dbias = norm_bwd(sv["y"], _row(mix_post_g[i]), dx1, out_dtype=BF16, name=f"bwd_mix_post{i}", dep=dep)
        gsm["mix_post_g"][i] = dg
        if i < N_A:
            gsm["cm_b_out"][i] = dbias
            du4 = mm_nt(dy, *wmat("cm_w_out", i), 1, 0, name=f"bwd_cm_out_dx{i}")
            gbig[("cm_w_out", i)] = mm_tn(sv["u4"], dy, 1, name=f"bwd_cm_out_dw{i}").reshape(N_CHIPS, d // N_CHIPS, d)
            du2, dlg, dlb = ln_silu_bwd(sv["u2"], _row(small_full["cm_ln_g"][i]), _row(small_full["cm_ln_b"][i]), du4,
                                        name=f"bwd_ln_silu{i}")
            gsm["cm_ln_g"][i], gsm["cm_ln_b"][i] = dlg, dlb
            dza, dzg, ddw, ddwb, dba, dbg = glu_conv_bwd(sv["z"].reshape(bsz, seq, 2 * d), small_full["cm_dw"][i],
                                                         du2.reshape(bsz, seq, d), name=f"bwd_glu_conv{i}")
            gsm["cm_dw"][i] = jnp.sum(ddw, axis=1)
            gsm["cm_dw_b"][i] = ddwb
            gsm["cm_b_in"][i] = jnp.concatenate([dba, dbg], axis=-1)
            dz = [dza.reshape(t, d), dzg.reshape(t, d)]
            dh1 = mm_nt(dz, *wmat("cm_w_in", i), 1, 0, name=f"bwd_cm_in_dx{i}")
            gbig[("cm_w_in", i)] = mm_tn(sv["h1"], dz, N_CHIPS, name=f"bwd_cm_in_dw{i}")
        else:
            j = i - N_A
            dmerged = mm_nt(dy, *wmat("w_o", j), 1, 0, name=f"bwd_o_proj_dx{j}")
            gbig[("w_o", j)] = mm_tn(sv["merged"], dy, 1, name=f"bwd_o_proj_dw{j}").reshape(N_CHIPS, hw // N_CHIPS, d)
            dmt = attn_bwd_prep(dmerged, sv["merged"], name=f"bwd_attn_prep{j}")
            dq_buf = None
            add_to_kv = dk_buf is not None
            for g, dil in enumerate(DILATIONS):
                dq_buf, dk_buf, dv_buf = attn_bwd(
                    sv["q"], kv_state["kv"], g, dil, dmerged.reshape(bsz, seq, hw), sv["lsej"].reshape(bsz, seq, hw),
                    dmt.reshape(bsz, seq, hw), dq_buf, dk_buf, dv_buf, add_to_kv, hw, name=f"attn_bwd{j}_{g}")
            dq = dq_buf.reshape(t, qw)
            dh1 = mm_nt(dq, *wmat("w_q", j), 1, 0, name=f"bwd_q_proj_dx{j}")
            gbig[("w_q", j)] = mm_tn(sv["h1"], dq, N_CHIPS, name=f"bwd_q_proj_dw{j}")
        dx, dg, _ = norm_bwd(sv["x_in"], _row(mix_pre_g[i]), dh1, add=dx1, name=f"bwd_mix_pre{i}")
        gsm["mix_pre_g"][i] = dg
        if i > N_A:
            dep = start_scatter([("ffn_w_in", i), ("ffn_w_out", i), ("w_q", i - N_A), ("w_o", i - N_A)], f"l{i}")
        elif 0 < i < N_A:
            dep = start_scatter([("ffn_w_in", i), ("ffn_w_out", i), ("cm_w_in", i), ("cm_w_out", i)], f"l{i}")
        elif i == 0:
            last_token = start_scatter([("cm_w_in", 0), ("cm_w_out", 0)], "cm0")
        if i == N_A:
            dkv = [dk_buf.reshape(t, qw), dv_buf.reshape(t, qw)]
            dkvn = mm_nt(dkv, *wmat("w_kv"), 1, 0, name="bwd_kv_proj_dx")
            gbig[("w_kv", None)] = mm_tn(kv_state["kvn"], dkv, N_CHIPS, name="bwd_kv_proj_dw")
            dx, dg_kv, _ = norm_bwd(kv_state["x_a"], _row(kv_norm_g), dkvn, add=dx, name="bwd_kv_norm")
            dep = start_scatter([("ffn_w_in", i), ("ffn_w_out", i), ("w_q", 0), ("w_o", 0), ("w_kv", None)], f"l{i}")
    grad_x = dx.reshape(bsz, seq, d)

    plane_of = {}
    outs_g, outs_d, outs_m, outs_v = {}, {}, {}, {}

    def finish_scatter(k, after):
        pcs, ssem, rsem, g_f, r_f = in_flight[k]
        g_done, r_done = scatter_wait(ssem, rsem, g_f, r_f, after, name=f"scatter_wait{k}")
        for pc, g_arr, r_arr in zip(pcs, g_done, r_done):
            plane_of[pc] = sum_parts(g_arr, r_arr, me_arr, name=f"sum_chips_{pc[0]}_{pc[1]}")

    def update(group, tag, after):
        plane = []
        for n in group:
            if w_in[n].ndim == 2:
                plane.append(plane_of[(n, None)])
            else:
                plane.append(jnp.concatenate([plane_of[(n, l)] for l in range(w_in[n].shape[0])], axis=0))
        ssems, rsems, plane_f, land_f, _ = swap_start(plane, name=f"swap_start_{tag}")
        for k, n in enumerate(group):
            p_mine, p_other = swap_wait(ssems[k], rsems[k], plane_f[k], land_f[k], after, name=f"swap_wait_{n}")
            shp = w_in[n].shape
            flat = lambda a: a.reshape(-1, shp[-1])
            g_, d_, m_, v_ = adamw(flat(w_in[n]), flat(m_in[n]), flat(v_in[n]), [p_mine, p_other], name=f"adamw_{n}")
            outs_g[n], outs_d[n], outs_m[n], outs_v[n] = (a.reshape(shp) for a in (g_, d_, m_, v_))
            after = v_
        return after

    small_names = [n for n in names if n not in big]
    small_shapes_full = {}
    small_grads_full = []
    for n in small_names:
        if n == "kv_norm_g":
            gfull = dg_kv.reshape(-1)
        elif n in ("cm_dw", "ffn_dw"):
            gfull = jnp.stack(gsm[n], axis=0)
        else:
            gfull = jnp.stack([a.reshape(-1) for a in gsm[n]], axis=0)
        small_shapes_full[n] = gfull.shape
        small_grads_full.append(gfull)
    dev_arr = (4 * lax.axis_index("x") + 2 * lax.axis_index("y") + lax.axis_index("c")).astype(jnp.int32).reshape(1)
    small_land = cast_place(_pack(small_grads_full) + last_token[0, 0], None, dev_arr, F32, name="place_small_grads",
                            nslots=N_DEV)
    sm_send, sm_recv, small_land, small_token = small_start(small_land)

    for k in range(len(in_flight) - 1):
        finish_scatter(k, small_token)
    done = update(["w_kv", "w_q", "w_o", "ffn_w_in", "ffn_w_out"], "a", small_token)
    finish_scatter(len(in_flight) - 1, done)
    done = update(["cm_w_in", "cm_w_out"], "b", done)

    summed = sum_slots(small_wait(sm_send, sm_recv, small_land, done))
    g_full = dict(zip(small_names, _unpack(summed, [small_shapes_full[n] for n in small_names])))
    g_loc = {}
    for n in small_names:
        if n in small_sharded:
            width = w_in[n].shape[-1]
            g_loc[n] = lax.dynamic_slice_in_dim(g_full[n], me_chip * width, width, axis=g_full[n].ndim - 1)
        else:
            g_loc[n] = g_full[n]
    res = adamw(_pack([w_in[n] for n in small_names]), _pack([m_in[n] for n in small_names]),
                _pack([v_in[n] for n in small_names]), [_pack([g_loc[n] for n in small_names])], name="adamw_small")
    shapes_loc = [w_in[n].shape for n in small_names]
    for dst, packed in zip((outs_g, outs_d, outs_m, outs_v), res):
        for n, a in zip(small_names, _unpack(packed, shapes_loc)):
            dst[n] = a

    return (loss, grad_x, *[outs_g[n] for n in names], *[outs_d[n] for n in names],
            *[outs_m[n] for n in names], *[outs_v[n] for n in names])
```

```python
import functools
import math

import jax
import jax.numpy as jnp
from jax import lax
from jax.experimental import pallas as pl
from jax.experimental.pallas import tpu as pltpu

F32 = jnp.float32
BF16 = jnp.bfloat16
EPS = 1e-6
NEG_INF = -1e30
N_A = 2
DEPTH = 4
N_GROUPS = 3
DILATIONS = (1, 4, 16)
HEAD_DIM = 128
BLK = 128
LANES = 128
N_CHIPS = 4
N_DEV = 8
VMEM_LIMIT_V7X = 56 * 1024 * 1024

ADAM_LR = 0.001
ADAM_B1 = 0.9
ADAM_B2 = 0.999
ADAM_EPS = 1e-08
ADAM_WD = 0.01
ADAM_STEP = 10

MESH = pl.DeviceIdType.MESH


def _cp(*sem, **kw):
    return pltpu.CompilerParams(dimension_semantics=sem if sem else None, vmem_limit_bytes=VMEM_LIMIT_V7X, **kw)


def _dot(a, b):
    return jnp.dot(a, b, preferred_element_type=F32)


def _dot_nt(a, b):
    return lax.dot_general(a, b, (((1,), (1,)), ((), ())), preferred_element_type=F32)


def _dot_tn(a, b):
    return lax.dot_general(a, b, (((0,), (0,)), ((), ())), preferred_element_type=F32)


def _sigmoid(x):
    return 1.0 / (1.0 + jnp.exp(-x))


def _row_tile(n, want):
    if n <= want:
        return n
    for t in range(want - want % 8, 7, -8):
        if n % t == 0:
            return t
    raise ValueError(f"no row tile for {n} rows")


def mm_nn(a, w, nsh, stride, layer, bias=None, out_dtype=F32, name="mm_nn", tm=1024):
    m, k = a.shape
    _, k2, ns = w.shape
    assert k == k2
    tm = _row_tile(m, tm)
    has_bias = bias is not None

    def body(*refs):
        if has_bias:
            a_ref, w_ref, b_ref, o_ref = refs
        else:
            a_ref, w_ref, o_ref = refs
        acc = _dot(a_ref[...].astype(BF16), w_ref[...])
        if has_bias:
            acc = acc + b_ref[...]
        o_ref[...] = acc.astype(out_dtype)

    in_specs = [
        pl.BlockSpec((tm, k), lambda j, i: (i, 0)),
        pl.BlockSpec((None, k, ns), lambda j, i: (j * stride + layer, 0, 0)),
    ]
    args = [a, w]
    if has_bias:
        in_specs.append(pl.BlockSpec((1, ns), lambda j, i: (0, j)))
        args.append(bias)
    return pl.pallas_call(
        body,
        name=name,
        grid=(nsh, m // tm),
        in_specs=in_specs,
        out_specs=pl.BlockSpec((tm, ns), lambda j, i: (i, j)),
        out_shape=jax.ShapeDtypeStruct((m, nsh * ns), out_dtype),
        compiler_params=_cp("parallel", "parallel"),
    )(*args)


def mm_nt(dy, w, nsh, stride, layer, out_dtype=F32, name="mm_nt", tm=1024):
    dys = list(dy) if isinstance(dy, (list, tuple)) else [dy]
    npart = len(dys)
    per = nsh // npart
    m = dys[0].shape[0]
    _, k, ns = w.shape
    assert all(d.shape == (m, per * ns) for d in dys)
    tm = _row_tile(m, tm)

    def body(*refs):
        dy_refs = refs[:npart]
        w_ref, o_ref, acc_ref = refs[npart:]
        j = pl.program_id(1)

        @pl.when(j == 0)
        def _():
            acc_ref[...] = jnp.zeros(acc_ref.shape, F32)

        for pi in range(npart):
            @pl.when(j // per == pi)
            def _(pi=pi):
                acc_ref[...] += _dot_nt(dy_refs[pi][...].astype(BF16), w_ref[...])

        @pl.when(j == nsh - 1)
        def _():
            o_ref[...] = acc_ref[...].astype(out_dtype)

    dy_specs = [pl.BlockSpec((tm, ns), lambda i, j, pi=pi: (i, jnp.clip(j - pi * per, 0, per - 1))) for pi in range(npart)]
    return pl.pallas_call(
        body,
        name=name,
        grid=(m // tm, nsh),
        in_specs=[*dy_specs, pl.BlockSpec((None, k, ns), lambda i, j: (j * stride + layer, 0, 0))],
        out_specs=pl.BlockSpec((tm, k), lambda i, j: (i, 0)),
        out_shape=jax.ShapeDtypeStruct((m, k), out_dtype),
        scratch_shapes=[pltpu.VMEM((tm, k), F32)],
        compiler_params=_cp("parallel", "arbitrary"),
    )(*dys, w)


def mm_tn(a, dy, nsh, name="mm_tn", tm=1024):
    dys = list(dy) if isinstance(dy, (list, tuple)) else [dy]
    npart = len(dys)
    per = nsh // npart
    m, k = a.shape
    ns = dys[0].shape[1] // per
    assert all(d.shape == (m, per * ns) for d in dys)
    tm = _row_tile(m, tm)
    nt = m // tm

    def body(*refs):
        a_ref = refs[0]
        dy_refs = refs[1:1 + npart]
        o_ref, acc_ref = refs[1 + npart:]
        j = pl.program_id(0)
        i = pl.program_id(1)

        @pl.when(i == 0)
        def _():
            acc_ref[...] = jnp.zeros(acc_ref.shape, F32)

        for pi in range(npart):
            @pl.when(j // per == pi)
            def _(pi=pi):
                acc_ref[...] += _dot_tn(a_ref[...].astype(BF16), dy_refs[pi][...].astype(BF16))

        @pl.when(i == nt - 1)
        def _():
            o_ref[...] = acc_ref[...].astype(BF16)

    dy_specs = [
        pl.BlockSpec((tm, ns), lambda j, i, pi=pi: (jnp.where(j // per == pi, i, 0), jnp.clip(j - pi * per, 0, per - 1)))
        for pi in range(npart)
    ]
    return pl.pallas_call(
        body,
        name=name,
        grid=(nsh, nt),
        in_specs=[pl.BlockSpec((tm, k), lambda j, i: (i, 0)), *dy_specs],
        out_specs=pl.BlockSpec((None, k, ns), lambda j, i: (j, 0, 0)),
        out_shape=jax.ShapeDtypeStruct((nsh, k, ns), BF16),
        scratch_shapes=[pltpu.VMEM((k, ns), F32)],
        compiler_params=_cp("parallel", "arbitrary"),
    )(a, *dys)


DEP_SPEC_SHAPE = (8, LANES)


def resid_norm_fwd(x, y, g_post, next_gains, name, tm=256, dep=None):
    t, d = x.shape
    tm = _row_tile(t, tm)
    has_y = y is not None
    n_next = len(next_gains)
    n_dep = 0 if dep is None else 1

    def body(*refs):
        x_ref = refs[0]
        pos = 1
        if has_y:
            y_ref, gp_ref = refs[1], refs[2]
            pos = 3
        gn_refs = refs[pos:pos + n_next]
        outs = refs[pos + n_next + n_dep:]
        xv = x_ref[...]
        o = 0
        if has_y:
            yv = y_ref[...]
            r = lax.rsqrt(jnp.mean(yv * yv, axis=-1, keepdims=True) + EPS)
            xv = xv + (yv * r) * gp_ref[...]
            outs[0][...] = xv
            o = 1
        if n_next:
            xn = xv * lax.rsqrt(jnp.mean(xv * xv, axis=-1, keepdims=True) + EPS)
            for k in range(n_next):
                outs[o + k][...] = (xn * gn_refs[k][...]).astype(BF16)

    row = pl.BlockSpec((tm, d), lambda i: (i, 0))
    vec = pl.BlockSpec((1, d), lambda i: (0, 0))
    args, in_specs = [x], [row]
    if has_y:
        args += [y, g_post]
        in_specs += [row, vec]
    args += list(next_gains)
    in_specs += [vec] * n_next
    if n_dep:
        args.append(dep)
        in_specs.append(pl.BlockSpec(DEP_SPEC_SHAPE, lambda i: (0, 0)))
    out_shape, out_specs = [], []
    if has_y:
        out_shape.append(jax.ShapeDtypeStruct((t, d), F32))
        out_specs.append(row)
    for _ in range(n_next):
        out_shape.append(jax.ShapeDtypeStruct((t, d), BF16))
        out_specs.append(row)
    return pl.pallas_call(
        body, name=name, grid=(t // tm,), in_specs=in_specs, out_specs=out_specs, out_shape=out_shape,
        compiler_params=_cp("parallel"),
    )(*args)


def norm_bwd(x, g, dy, add=None, out_dtype=F32, name="norm_bwd", tm=256, dep=None):
    t, d = x.shape
    tm = _row_tile(t, tm)
    has_add = add is not None

    def body(*refs):
        x_ref, g_ref, dy_ref = refs[:3]
        add_ref = refs[3] if has_add else None
        dx_ref, dg_ref, cs_ref = refs[-3:]
        i = pl.program_id(0)
        xv = x_ref[...]
        dyv = dy_ref[...].astype(F32)
        r = lax.rsqrt(jnp.mean(xv * xv, axis=-1, keepdims=True) + EPS)
        gd = dyv * g_ref[...]
        dx = r * gd - xv * ((r * r * r) * jnp.mean(xv * gd, axis=-1, keepdims=True))
        if has_add:
            dx = dx + add_ref[...]
        dx_ref[...] = dx.astype(out_dtype)
        dg = jnp.sum(dyv * (xv * r), axis=0, keepdims=True)
        cs = jnp.sum(dx, axis=0, keepdims=True)

        @pl.when(i == 0)
        def _():
            dg_ref[...] = dg
            cs_ref[...] = cs

        @pl.when(i > 0)
        def _():
            dg_ref[...] += dg
            cs_ref[...] += cs

    row = pl.BlockSpec((tm, d), lambda i: (i, 0))
    vec = pl.BlockSpec((1, d), lambda i: (0, 0))
    args, in_specs = [x, g, dy], [row, vec, row]
    if has_add:
        args.append(add)
        in_specs.append(row)
    if dep is not None:
        args.append(dep)
        in_specs.append(pl.BlockSpec(DEP_SPEC_SHAPE, lambda i: (0, 0)))
    return pl.pallas_call(
        body, name=name, grid=(t // tm,), in_specs=in_specs,
        out_specs=[row, vec, vec],
        out_shape=[jax.ShapeDtypeStruct((t, d), out_dtype), jax.ShapeDtypeStruct((1, d), F32),
                   jax.ShapeDtypeStruct((1, d), F32)],
        compiler_params=_cp("arbitrary"),
    )(*args)


def loss_fwd_bwd(x, target, name="loss", tm=256):
    t, d = x.shape
    tm = _row_tile(t, tm)

    def body(x_ref, t_ref, dx_ref, l_ref):
        i = pl.program_id(0)
        err = x_ref[...] - t_ref[...]
        dx_ref[...] = err * (1.0 / d)
        part = 0.5 * jnp.sum(jnp.mean(err * err, axis=-1, keepdims=True), axis=0, keepdims=True)
        part = jnp.broadcast_to(part, l_ref.shape)

        @pl.when(i == 0)
        def _():
            l_ref[...] = part

        @pl.when(i > 0)
        def _():
            l_ref[...] += part

    row = pl.BlockSpec((tm, d), lambda i: (i, 0))
    return pl.pallas_call(
        body, name=name, grid=(t // tm,), in_specs=[row, row],
        out_specs=[row, pl.BlockSpec((8, LANES), lambda i: (0, 0))],
        out_shape=[jax.ShapeDtypeStruct((t, d), F32), jax.ShapeDtypeStruct((8, LANES), F32)],
        compiler_params=_cp("arbitrary"),
    )(x, target)


CONV_HALO = 32
CONV_CHUNK = 128


def glu_conv_fwd(z, dw, dwb, name, tc=128):
    b, s, c2 = z.shape
    c = c2 // 2
    kw = dw.shape[0]
    tc = min(tc, c)
    nc = c // tc
    ch = min(CONV_CHUNK, s)
    halo = CONV_HALO
    assert kw - 1 <= halo and s % ch == 0

    nch = s // ch

    def body(a_ref, g_ref, w_ref, b_ref, o_ref, pad_ref):
        _fill_glu_slabs(a_ref, g_ref, pad_ref, nch, ch, halo)

        def chunk(ci, carry):
            r0 = pl.multiple_of(ci * ch, ch)
            acc = b_ref[...]
            for k, tap in enumerate(_taps_front(pad_ref, ci, kw, ch, halo)):
                acc = acc + w_ref[k:k + 1, :] * tap
            o_ref[pl.ds(r0, ch), :] = acc
            return carry

        lax.fori_loop(0, nch, chunk, 0)

    return pl.pallas_call(
        body, name=name, grid=(b, nc),
        in_specs=[
            pl.BlockSpec((None, s, tc), lambda bi, i: (bi, 0, i)),
            pl.BlockSpec((None, s, tc), lambda bi, i: (bi, 0, i + nc)),
            pl.BlockSpec((kw, tc), lambda bi, i: (0, i)),
            pl.BlockSpec((1, tc), lambda bi, i: (0, i)),
        ],
        out_specs=pl.BlockSpec((None, s, tc), lambda bi, i: (bi, 0, i)),
        out_shape=jax.ShapeDtypeStruct((b, s, c), F32),
        scratch_shapes=[pltpu.VMEM((nch, ch + halo, tc), F32)],
        compiler_params=_cp("parallel", "parallel"),
    )(z, z, dw, dwb)


def glu_conv_bwd(z, dw, du2, name, tc=128):
    b, s, c2 = z.shape
    c = c2 // 2
    kw = dw.shape[0]
    tc = min(tc, c)
    nc = c // tc
    ch = min(CONV_CHUNK, s)
    nch = s // ch
    halo = CONV_HALO

    def body(a_ref, g_ref, w_ref, du_ref, dza_ref, dzg_ref, ddw_ref, ddwb_ref, dba_ref, dbg_ref, upad_ref, dpad_ref):
        bi = pl.program_id(1)

        @pl.when(bi == 0)
        def _():
            ddw_ref[...] = jnp.zeros(ddw_ref.shape, F32)
            ddwb_ref[...] = jnp.zeros(ddwb_ref.shape, F32)
            dba_ref[...] = jnp.zeros(dba_ref.shape, F32)
            dbg_ref[...] = jnp.zeros(dbg_ref.shape, F32)

        _fill_glu_slabs(a_ref, g_ref, upad_ref, nch, ch, halo)
        dpad_ref[nch - 1, ch:ch + halo, :] = jnp.zeros((halo, tc), F32)
        dpad_ref[nch - 1, 0:ch, :] = du_ref[s - ch:s, :]

        def fill(ci, carry):
            r0 = pl.multiple_of(ci * ch, ch)
            dpad_ref[ci, :, :] = du_ref[pl.ds(r0, ch + halo), :]
            return carry

        lax.fori_loop(0, nch - 1, fill, 0)

        def chunk(ci, carry):
            r0 = pl.multiple_of(ci * ch, ch)
            du_c = du_ref[pl.ds(r0, ch), :]
            taps_u = _taps_front(upad_ref, ci, kw, ch, halo)
            du1 = w_ref[kw - 1:kw, :] * du_c
            ddw_ref[kw - 1] += jnp.sum((du_c * taps_u[kw - 1]).reshape(ch // 8, 8, tc), axis=0)
            for j in range(1, kw):
                du1 = du1 + w_ref[kw - 1 - j:kw - j, :] * dpad_ref[ci, j:j + ch, :]
                ddw_ref[kw - 1 - j] += jnp.sum((du_c * taps_u[kw - 1 - j]).reshape(ch // 8, 8, tc), axis=0)
            av = a_ref[pl.ds(r0, ch), :]
            sg = _sigmoid(g_ref[pl.ds(r0, ch), :])
            dza = du1 * sg
            dzg = du1 * av * (sg * (1.0 - sg))
            dza_ref[pl.ds(r0, ch), :] = dza.astype(BF16)
            dzg_ref[pl.ds(r0, ch), :] = dzg.astype(BF16)
            dba_ref[...] += jnp.sum(dza, axis=0, keepdims=True)
            dbg_ref[...] += jnp.sum(dzg, axis=0, keepdims=True)
            ddwb_ref[...] += jnp.sum(du_c, axis=0, keepdims=True)
            return carry

        lax.fori_loop(0, s // ch, chunk, 0)

    blk = lambda off: pl.BlockSpec((None, s, tc), lambda i, bi: (bi, 0, i + off))
    vec = pl.BlockSpec((1, tc), lambda i, bi: (0, i))
    return pl.pallas_call(
        body, name=name, grid=(nc, b),
        in_specs=[blk(0), blk(nc), pl.BlockSpec((kw, tc), lambda i, bi: (0, i)), blk(0)],
        out_specs=[blk(0), blk(0), pl.BlockSpec((kw, 8, tc), lambda i, bi: (0, 0, i)), vec, vec, vec],
        out_shape=[
            jax.ShapeDtypeStruct((b, s, c), BF16), jax.ShapeDtypeStruct((b, s, c), BF16),
            jax.ShapeDtypeStruct((kw, 8, c), F32), jax.ShapeDtypeStruct((1, c), F32),
            jax.ShapeDtypeStruct((1, c), F32), jax.ShapeDtypeStruct((1, c), F32),
        ],
        scratch_shapes=[pltpu.VMEM((nch, ch + halo, tc), F32), pltpu.VMEM((nch, ch + halo, tc), F32)],
        compiler_params=_cp("parallel", "arbitrary"),
    )(z, z, dw, du2)


def _fill_glu_slabs(a_ref, g_ref, pad_ref, nch, ch, halo):
    tc = a_ref.shape[-1]
    pad_ref[0, 0:halo, :] = jnp.zeros((halo, tc), F32)
    pad_ref[0, halo:halo + ch, :] = a_ref[0:ch, :] * _sigmoid(g_ref[0:ch, :])

    def fill(ci, carry):
        r0 = pl.multiple_of(ci * ch, ch)
        pad_ref[ci, 0:halo, :] = pad_ref[ci - 1, ch:ch + halo, :]
        pad_ref[ci, halo:halo + ch, :] = a_ref[pl.ds(r0, ch), :] * _sigmoid(g_ref[pl.ds(r0, ch), :])
        return carry

    lax.fori_loop(1, nch, fill, 0)


def ln_silu_fwd(u, g, bvec, name, tm=256):
    t, d = u.shape
    tm = _row_tile(t, tm)

    def body(u_ref, g_ref, b_ref, o_ref):
        uv = u_ref[...]
        mu = jnp.mean(uv, axis=-1, keepdims=True)
        xc = uv - mu
        var = jnp.mean(xc * xc, axis=-1, keepdims=True)
        v = (xc * lax.rsqrt(var + EPS)) * g_ref[...] + b_ref[...]
        o_ref[...] = (v * _sigmoid(v)).astype(BF16)

    row = pl.BlockSpec((tm, d), lambda i: (i, 0))
    vec = pl.BlockSpec((1, d), lambda i: (0, 0))
    return pl.pallas_call(
        body, name=name, grid=(t // tm,), in_specs=[row, vec, vec], out_specs=row,
        out_shape=jax.ShapeDtypeStruct((t, d), BF16), compiler_params=_cp("parallel"),
    )(u, g, bvec)


def ln_silu_bwd(u, g, bvec, dout, name, tm=256):
    t, d = u.shape
    tm = _row_tile(t, tm)

    def body(u_ref, g_ref, b_ref, do_ref, du_ref, dg_ref, db_ref):
        i = pl.program_id(0)
        uv = u_ref[...]
        mu = jnp.mean(uv, axis=-1, keepdims=True)
        xc = uv - mu
        var = jnp.mean(xc * xc, axis=-1, keepdims=True)
        rstd = lax.rsqrt(var + EPS)
        n = xc * rstd
        v = n * g_ref[...] + b_ref[...]
        sg = _sigmoid(v)
        dv = do_ref[...].astype(F32) * (sg * (1.0 + v * (1.0 - sg)))
        dn = dv * g_ref[...]
        du_ref[...] = rstd * (dn - jnp.mean(dn, axis=-1, keepdims=True) - n * jnp.mean(dn * n, axis=-1, keepdims=True))
        dg = jnp.sum(dv * n, axis=0, keepdims=True)
        db = jnp.sum(dv, axis=0, keepdims=True)

        @pl.when(i == 0)
        def _():
            dg_ref[...] = dg
            db_ref[...] = db

        @pl.when(i > 0)
        def _():
            dg_ref[...] += dg
            db_ref[...] += db

    row = pl.BlockSpec((tm, d), lambda i: (i, 0))
    vec = pl.BlockSpec((1, d), lambda i: (0, 0))
    return pl.pallas_call(
        body, name=name, grid=(t // tm,), in_specs=[row, vec, vec, row], out_specs=[row, vec, vec],
        out_shape=[jax.ShapeDtypeStruct((t, d), F32), jax.ShapeDtypeStruct((1, d), F32), jax.ShapeDtypeStruct((1, d), F32)],
        compiler_params=_cp("arbitrary"),
    )(u, g, bvec, dout)


FFN_HALO = 8


def _fill_front_halo(src_ref, pad_ref, nch, ch, halo):
    tc = src_ref.shape[-1]
    pad_ref[0, 0:halo, :] = jnp.zeros((halo, tc), F32)
    pad_ref[0, halo:halo + ch, :] = src_ref[0:ch, :].astype(F32)

    def fill(ci, carry):
        r0 = pl.multiple_of(ci * ch, ch)
        pad_ref[ci, 0:halo, :] = src_ref[pl.ds(r0 - 2 * halo, 2 * halo), :].astype(F32)[halo:, :]
        pad_ref[ci, halo:halo + ch, :] = src_ref[pl.ds(r0, ch), :].astype(F32)
        return carry

    lax.fori_loop(1, nch, fill, 0)


def _taps_front(pad_ref, ci, kw, ch, halo):
    return [pad_ref[ci, halo - (kw - 1 - k):halo - (kw - 1 - k) + ch, :] for k in range(kw)]


def ffn_mid_fwd(p, dw, dwb, name, tc=256):
    b, s, f2 = p.shape
    f = f2 // 2
    kw = dw.shape[0]
    tc = min(tc, f)
    nf = f // tc
    ch = min(CONV_CHUNK, s)
    nch = s // ch
    halo = FFN_HALO

    def body(pa_ref, pg_ref, wa_ref, wg_ref, ba_ref, bg_ref, o_ref, apad_ref, gpad_ref):
        _fill_front_halo(pa_ref, apad_ref, nch, ch, halo)
        _fill_front_halo(pg_ref, gpad_ref, nch, ch, halo)

        def chunk(ci, carry):
            r0 = pl.multiple_of(ci * ch, ch)
            ca = ba_ref[...]
            cg = bg_ref[...]
            taps = zip(_taps_front(apad_ref, ci, kw, ch, halo), _taps_front(gpad_ref, ci, kw, ch, halo))
            for k, (ta, tg) in enumerate(taps):
                ca = ca + wa_ref[k:k + 1, :] * ta
                cg = cg + wg_ref[k:k + 1, :] * tg
            o_ref[pl.ds(r0, ch), :] = ((cg * _sigmoid(cg)) * ca).astype(BF16)
            return carry

        lax.fori_loop(0, nch, chunk, 0)

    blk = lambda off: pl.BlockSpec((None, s, tc), lambda bi, i: (bi, 0, i + off))
    wsp = lambda off: pl.BlockSpec((kw, tc), lambda bi, i: (0, i + off))
    bsp = lambda off: pl.BlockSpec((1, tc), lambda bi, i: (0, i + off))
    return pl.pallas_call(
        body, name=name, grid=(b, nf),
        in_specs=[blk(0), blk(nf), wsp(0), wsp(nf), bsp(0), bsp(nf)],
        out_specs=pl.BlockSpec((None, s, tc), lambda bi, i: (bi, 0, i)),
        out_shape=jax.ShapeDtypeStruct((b, s, f), BF16),
        scratch_shapes=[pltpu.VMEM((nch, ch + halo, tc), F32)] * 2,
        compiler_params=_cp("parallel", "parallel"),
    )(p, p, dw, dw, dwb, dwb)


def ffn_mid_bwd(p, dw, dwb, ds, name, tc=256):
    b, s, f2 = p.shape
    f = f2 // 2
    kw = dw.shape[0]
    tc = min(tc, f)
    nf = f // tc
    ch = min(CONV_CHUNK, s)
    nch = s // ch
    halo = FFN_HALO

    def sum8(v):
        return jnp.sum(v.reshape(ch // 8, 8, tc), axis=0)

    def body(pa_ref, pg_ref, wa_ref, wg_ref, ba_ref, bg_ref, ds_ref, dpa_ref, dpg_ref, ddwa_ref, ddwg_ref, dba_ref, dbg_ref,
             apad_ref, gpad_ref, dca_ref, dcg_ref):
        bi = pl.program_id(1)

        @pl.when(bi == 0)
        def _():
            ddwa_ref[...] = jnp.zeros(ddwa_ref.shape, F32)
            ddwg_ref[...] = jnp.zeros(ddwg_ref.shape, F32)
            dba_ref[...] = jnp.zeros(dba_ref.shape, F32)
            dbg_ref[...] = jnp.zeros(dbg_ref.shape, F32)

        _fill_front_halo(pa_ref, apad_ref, nch, ch, halo)
        _fill_front_halo(pg_ref, gpad_ref, nch, ch, halo)
        dca_ref[nch - 1, ch:ch + halo, :] = jnp.zeros((halo, tc), F32)
        dcg_ref[nch - 1, ch:ch + halo, :] = jnp.zeros((halo, tc), F32)

        def grads(ci, carry):
            acc_a, acc_g, sb_a, sb_g = carry
            r0 = pl.multiple_of(ci * ch, ch)
            taps_a = _taps_front(apad_ref, ci, kw, ch, halo)
            taps_g = _taps_front(gpad_ref, ci, kw, ch, halo)
            ca = ba_ref[...]
            cg = bg_ref[...]
            for k in range(kw):
                ca = ca + wa_ref[k:k + 1, :] * taps_a[k]
                cg = cg + wg_ref[k:k + 1, :] * taps_g[k]
            sg = _sigmoid(cg)
            dsv = ds_ref[pl.ds(r0, ch), :].astype(F32)
            dca = dsv * (cg * sg)
            dcg = dsv * ca * (sg * (1.0 + cg * (1.0 - sg)))
            dca_ref[ci, 0:ch, :] = dca
            dcg_ref[ci, 0:ch, :] = dcg

            prev = jnp.maximum(ci - 1, 0)

            @pl.when(ci > 0)
            def _():
                dca_ref[prev, ch:ch + halo, :] = dca[0:halo, :]
                dcg_ref[prev, ch:ch + halo, :] = dcg[0:halo, :]

            acc_a = tuple(acc_a[k] + sum8(dca * taps_a[k]) for k in range(kw))
            acc_g = tuple(acc_g[k] + sum8(dcg * taps_g[k]) for k in range(kw))
            return acc_a, acc_g, sb_a + sum8(dca), sb_g + sum8(dcg)

        z8 = jnp.zeros((8, tc), F32)
        acc_a, acc_g, sb_a, sb_g = lax.fori_loop(0, nch, grads, ((z8,) * kw, (z8,) * kw, z8, z8))
        for k in range(kw):
            ddwa_ref[k] += acc_a[k]
            ddwg_ref[k] += acc_g[k]
        dba_ref[...] += jnp.sum(sb_a, axis=0, keepdims=True)
        dbg_ref[...] += jnp.sum(sb_g, axis=0, keepdims=True)

        def back(ci, carry):
            r0 = pl.multiple_of(ci * ch, ch)
            da = wa_ref[kw - 1:kw, :] * dca_ref[ci, 0:ch, :]
            dg = wg_ref[kw - 1:kw, :] * dcg_ref[ci, 0:ch, :]
            for j in range(1, kw):
                da = da + wa_ref[kw - 1 - j:kw - j, :] * dca_ref[ci, j:j + ch, :]
                dg = dg + wg_ref[kw - 1 - j:kw - j, :] * dcg_ref[ci, j:j + ch, :]
            dpa_ref[pl.ds(r0, ch), :] = da.astype(BF16)
            dpg_ref[pl.ds(r0, ch), :] = dg.astype(BF16)
            return carry

        lax.fori_loop(0, nch, back, 0)

    blk = lambda off: pl.BlockSpec((None, s, tc), lambda i, bi: (bi, 0, i + off))
    wsp = lambda off: pl.BlockSpec((kw, tc), lambda i, bi: (0, i + off))
    bsp = lambda off: pl.BlockSpec((1, tc), lambda i, bi: (0, i + off))
    acc3 = pl.BlockSpec((kw, 8, tc), lambda i, bi: (0, 0, i))
    vec = pl.BlockSpec((1, tc), lambda i, bi: (0, i))
    return pl.pallas_call(
        body, name=name, grid=(nf, b),
        in_specs=[blk(0), blk(nf), wsp(0), wsp(nf), bsp(0), bsp(nf), blk(0)],
        out_specs=[blk(0), blk(0), acc3, acc3, vec, vec],
        out_shape=[jax.ShapeDtypeStruct((b, s, f), BF16), jax.ShapeDtypeStruct((b, s, f), BF16),
                   jax.ShapeDtypeStruct((kw, 8, f), F32), jax.ShapeDtypeStruct((kw, 8, f), F32),
                   jax.ShapeDtypeStruct((1, f), F32), jax.ShapeDtypeStruct((1, f), F32)],
        scratch_shapes=[pltpu.VMEM((nch, ch + halo, tc), F32)] * 4,
        compiler_params=_cp("parallel", "arbitrary"),
    )(p, p, dw, dw, dwb, dwb, ds)


def _tile_rows(r, n, dil):
    start = r + n * BLK * dil
    return pl.ds(start, BLK, stride=dil) if dil > 1 else pl.ds(start, BLK)


def _band_masks():
    qi = lax.broadcasted_iota(jnp.int32, (BLK, 2 * BLK), 0)
    kk = lax.broadcasted_iota(jnp.int32, (BLK, 2 * BLK), 1)
    both = jnp.logical_or(jnp.logical_and(kk < BLK, kk >= qi), jnp.logical_and(kk >= BLK, kk - BLK <= qi))
    return both, kk[:, :BLK] <= qi[:, :BLK]


def attn_fwd(q, kv, g, dil, hw, name):
    b, s, _ = q.shape
    nh = hw // HEAD_DIM
    nblk = s // dil // BLK
    scale = 1.0 / math.sqrt(HEAD_DIM)

    def body(q_ref, k_ref, v_ref, o_ref, lse_ref):
        h = pl.program_id(1)
        mask2, mask1 = _band_masks()
        mine = lax.broadcasted_iota(jnp.int32, (BLK, LANES), 1) == h

        @pl.when(h == 0)
        def _():
            lse_ref[...] = jnp.zeros(lse_ref.shape, F32)

        for r in range(dil):
            kp = vp = None
            for n in range(nblk):
                rs = _tile_rows(r, n, dil)
                qt = q_ref[rs, :].astype(BF16)
                kc = k_ref[rs, :].astype(BF16)
                vc = v_ref[rs, :].astype(BF16)
                if n == 0:
                    kcat, vcat, mask = kc, vc, mask1
                else:
                    kcat, vcat, mask = jnp.concatenate([kp, kc], axis=0), jnp.concatenate([vp, vc], axis=0), mask2
                sc = jnp.where(mask, _dot_nt(qt, kcat) * scale, NEG_INF)
                m = jnp.max(sc, axis=-1, keepdims=True)
                p = jnp.exp(sc - m)
                den = jnp.sum(p, axis=-1, keepdims=True)
                o_ref[rs, :] = _dot(p.astype(BF16), vcat) / den
                lse_ref[rs, :] = jnp.where(mine, m + jnp.log(den), lse_ref[rs, :])
                kp, vp = kc, vc

    col = lambda base: pl.BlockSpec((None, s, HEAD_DIM), lambda bi, h: (bi, 0, base + h))
    return pl.pallas_call(
        body, name=name, grid=(b, nh),
        in_specs=[col(g * nh), col(g * nh), col((N_GROUPS + g) * nh)],
        out_specs=[col(0), pl.BlockSpec((None, s, LANES), lambda bi, h: (bi, 0, 0))],
        out_shape=[jax.ShapeDtypeStruct((b, s, hw), F32), jax.ShapeDtypeStruct((b, s, LANES), F32)],
        compiler_params=_cp("parallel", "arbitrary"),
    )(q, kv, kv)


def attn_merge(outs, lses, name, tm=256):
    t, hw = outs[0].shape
    nh = hw // HEAD_DIM
    tm = _row_tile(t, tm)
    ng = len(outs)

    def body(*refs):
        o_refs, l_refs = refs[:ng], refs[ng:2 * ng]
        m_ref, lj_ref = refs[2 * ng:]
        ls = [l_refs[g][...] for g in range(ng)]
        mx = ls[0]
        for g in range(1, ng):
            mx = jnp.maximum(mx, ls[g])
        es = [jnp.exp(l - mx) for l in ls]
        tot = es[0]
        for g in range(1, ng):
            tot = tot + es[g]
        ws = [e / tot for e in es]
        lj_ref[...] = mx + jnp.log(tot)
        for h in range(nh):
            sl = slice(h * HEAD_DIM, (h + 1) * HEAD_DIM)
            acc = ws[0][:, h:h + 1] * o_refs[0][:, sl]
            for g in range(1, ng):
                acc = acc + ws[g][:, h:h + 1] * o_refs[g][:, sl]
            m_ref[:, sl] = acc.astype(BF16)

    row = pl.BlockSpec((tm, hw), lambda i: (i, 0))
    st = pl.BlockSpec((tm, LANES), lambda i: (i, 0))
    return pl.pallas_call(
        body, name=name, grid=(t // tm,), in_specs=[row] * ng + [st] * ng, out_specs=[row, st],
        out_shape=[jax.ShapeDtypeStruct((t, hw), BF16), jax.ShapeDtypeStruct((t, LANES), F32)],
        compiler_params=_cp("parallel"),
    )(*outs, *lses)


def attn_bwd_prep(dmerged, merged, name, tm=256):
    t, hw = merged.shape
    nh = hw // HEAD_DIM
    tm = _row_tile(t, tm)

    def body(d_ref, m_ref, o_ref):
        lane = lax.broadcasted_iota(jnp.int32, (tm, LANES), 1)
        acc = jnp.zeros((tm, LANES), F32)
        for h in range(nh):
            sl = slice(h * HEAD_DIM, (h + 1) * HEAD_DIM)
            dsum = jnp.sum(d_ref[:, sl] * m_ref[:, sl].astype(F32), axis=-1, keepdims=True)
            acc = jnp.where(lane == h, dsum, acc)
        o_ref[...] = acc

    row = pl.BlockSpec((tm, hw), lambda i: (i, 0))
    return pl.pallas_call(
        body, name=name, grid=(t // tm,), in_specs=[row, row], out_specs=pl.BlockSpec((tm, LANES), lambda i: (i, 0)),
        out_shape=jax.ShapeDtypeStruct((t, LANES), F32), compiler_params=_cp("parallel"),
    )(dmerged, merged)


def attn_bwd(q, kv, g, dil, do, lsej, dm, dq_buf, dk_buf, dv_buf, accumulate, hw, name):
    b, s, _ = q.shape
    nh = hw // HEAD_DIM
    nblk = s // dil // BLK
    scale = 1.0 / math.sqrt(HEAD_DIM)
    assert dk_buf is not None or not accumulate
    kv_at = 6 + (dq_buf is not None)

    def body(*refs):
        q_ref, k_ref, v_ref, do_ref, lj_ref, dm_ref = refs[:6]
        dq_ref, dk_ref, dv_ref = refs[-3:]
        dki_ref, dvi_ref = (refs[kv_at], refs[kv_at + 1]) if accumulate else (None, None)
        mask2, mask1 = _band_masks()
        mine = lax.broadcasted_iota(jnp.int32, (BLK, LANES), 1) == pl.program_id(1)

        def my_lane(v):
            return jnp.sum(jnp.where(mine, v, 0.0), axis=-1, keepdims=True)

        def put(rs, dk, dv):
            if accumulate:
                dk = dk + dki_ref[rs, :]
                dv = dv + dvi_ref[rs, :]
            dk_ref[rs, :] = dk
            dv_ref[rs, :] = dv

        for r in range(dil):
            kp = vp = hold_k = hold_v = rs_prev = None
            for n in range(nblk):
                rs = _tile_rows(r, n, dil)
                qt = q_ref[rs, :].astype(BF16)
                kc = k_ref[rs, :].astype(BF16)
                vc = v_ref[rs, :].astype(BF16)
                dot = do_ref[rs, :].astype(BF16)
                lm = my_lane(lj_ref[rs, :])
                dmm = my_lane(dm_ref[rs, :])
                if n == 0:
                    kcat, vcat, mask = kc, vc, mask1
                else:
                    kcat, vcat, mask = jnp.concatenate([kp, kc], axis=0), jnp.concatenate([vp, vc], axis=0), mask2
                p = jnp.exp(jnp.where(mask, _dot_nt(qt, kcat) * scale, NEG_INF) - lm)
                ds = (p * (_dot_nt(dot, vcat) - dmm)).astype(BF16)
                dq_ref[rs, :] = _dot(ds, kcat) * scale
                dkc = _dot_tn(ds, qt) * scale
                dvc = _dot_tn(p.astype(BF16), dot)
                if n > 0:
                    put(rs_prev, hold_k + dkc[:BLK, :], hold_v + dvc[:BLK, :])
                    dkc, dvc = dkc[BLK:, :], dvc[BLK:, :]
                hold_k, hold_v, kp, vp, rs_prev = dkc, dvc, kc, vc, rs
            put(rs_prev, hold_k, hold_v)

    col = lambda base: pl.BlockSpec((None, s, HEAD_DIM), lambda bi, h: (bi, 0, base + h))
    any_spec = pl.BlockSpec(memory_space=pl.ANY)
    stat = pl.BlockSpec((None, s, LANES), lambda bi, h: (bi, 0, 0))
    in_specs = [col(g * nh), col(g * nh), col((N_GROUPS + g) * nh), col(0), stat, stat]
    args = [q, kv, kv, do, lsej, dm]
    aliases = {}
    if dq_buf is not None:
        in_specs.append(any_spec)
        args.append(dq_buf)
        aliases[6] = 0
    if dk_buf is not None:
        in_specs += [col(g * nh) if accumulate else any_spec] * 2
        args += [dk_buf, dv_buf]
        aliases.update({kv_at: 1, kv_at + 1: 2})
    shape = jax.ShapeDtypeStruct((b, s, N_GROUPS * hw), F32)
    return pl.pallas_call(
        body, name=name, grid=(b, nh), in_specs=in_specs, out_specs=[col(g * nh)] * 3, out_shape=[shape] * 3,
        input_output_aliases=aliases, compiler_params=_cp("parallel", "parallel"),
    )(*args)


def sum_parts(g, recv, me, name, tm=256):
    _, rows, c = g.shape
    n = recv.shape[0]
    tm = _row_tile(rows, tm)

    def body(me_ref, g_ref, r_ref, o_ref):
        acc = g_ref[...].astype(F32)
        for j in range(n):
            acc = acc + r_ref[j].astype(F32)
        o_ref[...] = acc

    return pl.pallas_call(
        body, name=name,
        grid_spec=pltpu.PrefetchScalarGridSpec(
            num_scalar_prefetch=1, grid=(rows // tm,),
            in_specs=[pl.BlockSpec((None, tm, c), lambda i, me_ref: (me_ref[0], i, 0)),
                      pl.BlockSpec((n, tm, c), lambda i, me_ref: (0, i, 0))],
            out_specs=pl.BlockSpec((tm, c), lambda i, me_ref: (i, 0))),
        out_shape=jax.ShapeDtypeStruct((rows, c), F32), compiler_params=_cp("parallel"),
    )(me, g, recv)


def adamw(w, m, v, g_parts, name, tm=256):
    rows, c = w.shape
    tm = _row_tile(rows, tm)
    npart = len(g_parts)

    def body(*refs):
        w_ref, m_ref, v_ref = refs[:3]
        g_refs = refs[3:3 + npart]
        go_ref, d_ref, mo_ref, vo_ref = refs[3 + npart:]
        g = g_refs[0][...]
        for k in range(1, npart):
            g = g + g_refs[k][...]
        mn = ADAM_B1 * m_ref[...] + (1.0 - ADAM_B1) * g
        vn = ADAM_B2 * v_ref[...] + (1.0 - ADAM_B2) * (g * g)
        m_hat = mn / (1.0 - ADAM_B1 ** ADAM_STEP)
        v_hat = vn / (1.0 - ADAM_B2 ** ADAM_STEP)
        go_ref[...] = g
        d_ref[...] = -ADAM_LR * (m_hat / (jnp.sqrt(v_hat) + ADAM_EPS) + ADAM_WD * w_ref[...])
        mo_ref[...] = mn
        vo_ref[...] = vn

    row = pl.BlockSpec((tm, c), lambda i: (i, 0))
    return pl.pallas_call(
        body, name=name, grid=(rows // tm,), in_specs=[row] * (3 + npart), out_specs=[row] * 4,
        out_shape=[jax.ShapeDtypeStruct((rows, c), F32)] * 4, compiler_params=_cp("parallel"),
    )(w, m, v, *g_parts)


def _place():
    return lax.axis_index("x"), lax.axis_index("y"), lax.axis_index("c")


def _other_chips(x, y, c):
    return [(1 - x, y, c), (x, 1 - y, c), (1 - x, 1 - y, c)]


def _chip_of(px, py):
    return 2 * px + py


HBM_SPEC = pl.BlockSpec(memory_space=pltpu.HBM)
SEM_SPEC = pl.BlockSpec(memory_space=pltpu.SEMAPHORE)
ANY_SPEC = pl.BlockSpec(memory_space=pl.ANY)
DATAFLOW = pltpu.SideEffectType.DATAFLOW_SIDE_EFFECTING
N_PEER_CHIPS = N_CHIPS - 1


def _hbm(a):
    return pltpu.with_memory_space_constraint(a, pltpu.HBM)


def _hbm_like(arrays):
    return [pltpu.HBM(a.shape, a.dtype) for a in arrays]


def cast_place(w, layer, me, out_dtype, name, tm=256, nslots=N_CHIPS):
    rows, c = w.shape[-2:]
    tm = _row_tile(rows, tm)

    def body(me_ref, w_ref, o_ref):
        o_ref[...] = w_ref[...].astype(out_dtype)

    if layer is None:
        in_spec = pl.BlockSpec((tm, c), lambda i, me_ref: (i, 0))
    else:
        in_spec = pl.BlockSpec((None, tm, c), lambda i, me_ref: (layer, i, 0))
    return pl.pallas_call(
        body, name=name,
        grid_spec=pltpu.PrefetchScalarGridSpec(
            num_scalar_prefetch=1, grid=(rows // tm,), in_specs=[in_spec],
            out_specs=pl.BlockSpec((None, tm, c), lambda i, me_ref: (me_ref[0], i, 0))),
        out_shape=jax.ShapeDtypeStruct((nslots, rows, c), out_dtype), compiler_params=_cp("parallel"),
    )(me, w)


def gather_start(lands, chunk_sizes, name="gather_start"):
    n = len(lands)
    nch = len(chunk_sizes)
    assert sum(chunk_sizes) == n

    def body(*refs):
        land_refs = refs[:n]
        outs = refs[n:]
        send_sems, recv_sems = outs[:nch], outs[nch:2 * nch]
        token = outs[-1]
        x, y, c = _place()
        me = _chip_of(x, y)
        peers = _other_chips(x, y, c)
        k = 0
        for ck, size in enumerate(chunk_sizes):
            for pos in range(size):
                for r, peer in enumerate(peers):
                    pltpu.make_async_remote_copy(
                        src_ref=land_refs[k].at[me], dst_ref=land_refs[k].at[me],
                        send_sem=send_sems[ck].at[N_PEER_CHIPS * pos + r], recv_sem=recv_sems[ck].at[N_PEER_CHIPS * pos + r],
                        device_id=peer, device_id_type=MESH).start()
                k += 1
        token[...] = jnp.zeros(token.shape, F32)

    sems = [pltpu.SemaphoreType.DMA((N_PEER_CHIPS * s,)) for s in chunk_sizes]
    res = pl.pallas_call(
        body, name=name,
        out_shape=(*sems, *sems, *_hbm_like(lands), jax.ShapeDtypeStruct(DEP_SPEC_SHAPE, F32)),
        in_specs=[HBM_SPEC] * n,
        out_specs=(*[SEM_SPEC] * (2 * nch), *[HBM_SPEC] * n, pl.BlockSpec(memory_space=pltpu.VMEM)),
        input_output_aliases={k: 2 * nch + k for k in range(n)},
        compiler_params=pltpu.CompilerParams(has_side_effects=DATAFLOW),
    )(*[_hbm(a) for a in lands])
    return res[:nch], res[nch:2 * nch], res[2 * nch:2 * nch + n], res[-1]


def gather_wait(send_sem, recv_sem, lands, after, name):
    n = len(lands)

    def body(*refs):
        land_refs = refs[:n]
        ssem, rsem = refs[n], refs[n + 1]
        x, y, c = _place()
        me = _chip_of(x, y)
        for pos in range(n):
            for r, peer in enumerate(_other_chips(x, y, c)):
                cp = pltpu.make_async_remote_copy(
                    src_ref=land_refs[pos].at[me], dst_ref=land_refs[pos].at[_chip_of(peer[0], peer[1])],
                    send_sem=ssem.at[N_PEER_CHIPS * pos + r], recv_sem=rsem.at[N_PEER_CHIPS * pos + r],
                    device_id=peer, device_id_type=MESH)
                cp.wait_send()
                cp.wait_recv()

    return pl.pallas_call(
        body, name=name, out_shape=tuple(_hbm_like(lands)),
        in_specs=[*[HBM_SPEC] * n, SEM_SPEC, SEM_SPEC, ANY_SPEC], out_specs=[HBM_SPEC] * n,
        input_output_aliases={k: k for k in range(n)},
        compiler_params=pltpu.CompilerParams(has_side_effects=DATAFLOW),
    )(*lands, send_sem, recv_sem, after)


def scatter_start(grads, name):
    n = len(grads)
    recvs = [lax.empty((N_PEER_CHIPS, *g.shape[1:]), g.dtype) for g in grads]

    def body(*refs):
        g_refs, r_refs = refs[:n], refs[n:2 * n]
        send_sems, recv_sems = refs[2 * n], refs[2 * n + 1]
        token = refs[-1]
        x, y, c = _place()
        for k in range(n):
            for r, peer in enumerate(_other_chips(x, y, c)):
                pltpu.make_async_remote_copy(
                    src_ref=g_refs[k].at[_chip_of(peer[0], peer[1])], dst_ref=r_refs[k].at[r],
                    send_sem=send_sems.at[N_PEER_CHIPS * k + r], recv_sem=recv_sems.at[N_PEER_CHIPS * k + r],
                    device_id=peer, device_id_type=MESH).start()
        token[...] = jnp.zeros(token.shape, F32)

    sem = pltpu.SemaphoreType.DMA((N_PEER_CHIPS * n,))
    res = pl.pallas_call(
        body, name=name,
        out_shape=(sem, sem, *_hbm_like(grads), *_hbm_like(recvs), jax.ShapeDtypeStruct(DEP_SPEC_SHAPE, F32)),
        in_specs=[HBM_SPEC] * (2 * n),
        out_specs=(SEM_SPEC, SEM_SPEC, *[HBM_SPEC] * (2 * n), pl.BlockSpec(memory_space=pltpu.VMEM)),
        input_output_aliases={k: 2 + k for k in range(2 * n)},
        compiler_params=pltpu.CompilerParams(has_side_effects=DATAFLOW),
    )(*[_hbm(a) for a in grads], *[_hbm(a) for a in recvs])
    return res[0], res[1], res[2:2 + n], res[2 + n:2 + 2 * n], res[-1]


def scatter_wait(send_sem, recv_sem, grads, recvs, after, name):
    n = len(grads)

    def body(*refs):
        g_refs, r_refs = refs[:n], refs[n:2 * n]
        ssem, rsem = refs[2 * n], refs[2 * n + 1]
        x, y, c = _place()
        for k in range(n):
            for r, peer in enumerate(_other_chips(x, y, c)):
                cp = pltpu.make_async_remote_copy(
                    src_ref=g_refs[k].at[_chip_of(peer[0], peer[1])], dst_ref=r_refs[k].at[r],
                    send_sem=ssem.at[N_PEER_CHIPS * k + r], recv_sem=rsem.at[N_PEER_CHIPS * k + r],
                    device_id=peer, device_id_type=MESH)
                cp.wait_send()
                cp.wait_recv()

    res = pl.pallas_call(
        body, name=name, out_shape=(*_hbm_like(grads), *_hbm_like(recvs)),
        in_specs=[*[HBM_SPEC] * (2 * n), SEM_SPEC, SEM_SPEC, ANY_SPEC], out_specs=[HBM_SPEC] * (2 * n),
        input_output_aliases={k: k for k in range(2 * n)},
        compiler_params=pltpu.CompilerParams(has_side_effects=DATAFLOW),
    )(*grads, *recvs, send_sem, recv_sem, after)
    return res[:n], res[n:]


def swap_start(parts, name):
    n = len(parts)
    lands = [lax.empty(p.shape, p.dtype) for p in parts]

    def body(*refs):
        p_refs, l_refs = refs[:n], refs[n:2 * n]
        sems = refs[2 * n:4 * n]
        token = refs[-1]
        x, y, c = _place()
        for k in range(n):
            pltpu.make_async_remote_copy(
                src_ref=p_refs[k], dst_ref=l_refs[k], send_sem=sems[k], recv_sem=sems[n + k],
                device_id=(x, y, 1 - c), device_id_type=MESH).start()
        token[...] = jnp.zeros(token.shape, F32)

    sem = pltpu.SemaphoreType.DMA(())
    res = pl.pallas_call(
        body, name=name,
        out_shape=(*[sem] * (2 * n), *_hbm_like(parts), *_hbm_like(lands), jax.ShapeDtypeStruct(DEP_SPEC_SHAPE, F32)),
        in_specs=[HBM_SPEC] * (2 * n),
        out_specs=(*[SEM_SPEC] * (2 * n), *[HBM_SPEC] * (2 * n), pl.BlockSpec(memory_space=pltpu.VMEM)),
        input_output_aliases={k: 2 * n + k for k in range(2 * n)},
        compiler_params=pltpu.CompilerParams(has_side_effects=DATAFLOW),
    )(*[_hbm(a) for a in parts], *[_hbm(a) for a in lands])
    return res[:n], res[n:2 * n], res[2 * n:3 * n], res[3 * n:4 * n], res[-1]


def swap_wait(send_sem, recv_sem, part, land, after, name):
    def body(p_ref, l_ref, ssem, rsem, after_ref, p_out, l_out):
        x, y, c = _place()
        cp = pltpu.make_async_remote_copy(src_ref=p_ref, dst_ref=l_ref, send_sem=ssem, recv_sem=rsem,
                                          device_id=(x, y, 1 - c), device_id_type=MESH)
        cp.wait_send()
        cp.wait_recv()

    return pl.pallas_call(
        body, name=name, out_shape=tuple(_hbm_like([part, land])),
        in_specs=[HBM_SPEC, HBM_SPEC, SEM_SPEC, SEM_SPEC, ANY_SPEC], out_specs=[HBM_SPEC, HBM_SPEC],
        input_output_aliases={0: 0, 1: 1},
        compiler_params=pltpu.CompilerParams(has_side_effects=DATAFLOW),
    )(part, land, send_sem, recv_sem, after)


def _xor_peer(x, y, c, k):
    px, py, pc = x ^ ((k >> 2) & 1), y ^ ((k >> 1) & 1), c ^ (k & 1)
    return (px, py, pc), 4 * px + 2 * py + pc


def small_start(land, name="small_start"):
    def body(l_ref, ssem, rsem, l_out, token):
        x, y, c = _place()
        me = 4 * x + 2 * y + c
        for k in range(1, N_DEV):
            peer, _ = _xor_peer(x, y, c, k)
            pltpu.make_async_remote_copy(
                src_ref=l_ref.at[me], dst_ref=l_ref.at[me], send_sem=ssem.at[k - 1], recv_sem=rsem.at[k - 1],
                device_id=peer, device_id_type=MESH).start()
        token[...] = jnp.zeros(token.shape, F32)

    sem = pltpu.SemaphoreType.DMA((N_DEV - 1,))
    return pl.pallas_call(
        body, name=name,
        out_shape=(sem, sem, pltpu.HBM(land.shape, land.dtype), jax.ShapeDtypeStruct(DEP_SPEC_SHAPE, F32)),
        in_specs=[HBM_SPEC], out_specs=(SEM_SPEC, SEM_SPEC, HBM_SPEC, pl.BlockSpec(memory_space=pltpu.VMEM)),
        input_output_aliases={0: 2}, compiler_params=pltpu.CompilerParams(has_side_effects=DATAFLOW),
    )(_hbm(land))


def small_wait(send_sem, recv_sem, land, after, name="small_wait"):
    def body(l_ref, ssem, rsem, after_ref, l_out):
        x, y, c = _place()
        me = 4 * x + 2 * y + c
        for k in range(1, N_DEV):
            peer, slot = _xor_peer(x, y, c, k)
            cp = pltpu.make_async_remote_copy(
                src_ref=l_ref.at[me], dst_ref=l_ref.at[slot], send_sem=ssem.at[k - 1], recv_sem=rsem.at[k - 1],
                device_id=peer, device_id_type=MESH)
            cp.wait_send()
            cp.wait_recv()

    return pl.pallas_call(
        body, name=name, out_shape=pltpu.HBM(land.shape, land.dtype),
        in_specs=[HBM_SPEC, SEM_SPEC, SEM_SPEC, ANY_SPEC], out_specs=HBM_SPEC, input_output_aliases={0: 0},
        compiler_params=pltpu.CompilerParams(has_side_effects=DATAFLOW),
    )(land, send_sem, recv_sem, after)


def sum_slots(land, name="sum_slots", tm=256):
    n, rows, c = land.shape
    tm = _row_tile(rows, tm)

    def body(l_ref, o_ref):
        acc = l_ref[0]
        for j in range(1, n):
            acc = acc + l_ref[j]
        o_ref[...] = acc

    return pl.pallas_call(
        body, name=name, grid=(rows // tm,), in_specs=[pl.BlockSpec((n, tm, c), lambda i: (0, i, 0))],
        out_specs=pl.BlockSpec((tm, c), lambda i: (i, 0)), out_shape=jax.ShapeDtypeStruct((rows, c), F32),
        compiler_params=_cp("parallel"),
    )(land)


PACK_ROW_TILE = 256


def _pack(arrays):
    flat = jnp.concatenate([a.reshape(-1).astype(F32) for a in arrays])
    n = flat.shape[0]
    rows = -(-n // LANES)
    rows = -(-rows // PACK_ROW_TILE) * PACK_ROW_TILE
    return jnp.pad(flat, (0, rows * LANES - n)).reshape(rows, LANES)


def _unpack(packed, shapes, lead=()):
    flat = packed.reshape(*lead, -1)
    out, off = [], 0
    for shp in shapes:
        n = math.prod(shp)
        out.append(flat[..., off:off + n].reshape(*lead, *shp))
        off += n
    return out


def _row(vec):
    return vec.reshape(1, -1)


def kernel(x, mix_pre_g, mix_post_g, ffn_pre_g, ffn_post_g, cm_w_in, cm_b_in, cm_dw, cm_dw_b, cm_ln_g, cm_ln_b, cm_w_out, cm_b_out, kv_norm_g, w_kv, w_q, w_o, ffn_w_in, ffn_dw, ffn_dw_b, ffn_w_out, loss_target, m_mix_pre_g, m_mix_post_g, m_ffn_pre_g, m_ffn_post_g, m_cm_w_in, m_cm_b_in, m_cm_dw, m_cm_dw_b, m_cm_ln_g, m_cm_ln_b, m_cm_w_out, m_cm_b_out, m_kv_norm_g, m_w_kv, m_w_q, m_w_o, m_ffn_w_in, m_ffn_dw, m_ffn_dw_b, m_ffn_w_out, v_mix_pre_g, v_mix_post_g, v_ffn_pre_g, v_ffn_post_g, v_cm_w_in, v_cm_b_in, v_cm_dw, v_cm_dw_b, v_cm_ln_g, v_cm_ln_b, v_cm_w_out, v_cm_b_out, v_kv_norm_g, v_w_kv, v_w_q, v_w_o, v_ffn_w_in, v_ffn_dw, v_ffn_dw_b, v_ffn_w_out):
    names = ["mix_pre_g", "mix_post_g", "ffn_pre_g", "ffn_post_g", "cm_w_in", "cm_b_in", "cm_dw", "cm_dw_b", "cm_ln_g",
             "cm_ln_b", "cm_w_out", "cm_b_out", "kv_norm_g", "w_kv", "w_q", "w_o", "ffn_w_in", "ffn_dw", "ffn_dw_b",
             "ffn_w_out"]
    w_in = dict(zip(names, [mix_pre_g, mix_post_g, ffn_pre_g, ffn_post_g, cm_w_in, cm_b_in, cm_dw, cm_dw_b, cm_ln_g,
                            cm_ln_b, cm_w_out, cm_b_out, kv_norm_g, w_kv, w_q, w_o, ffn_w_in, ffn_dw, ffn_dw_b, ffn_w_out]))
    m_in = dict(zip(names, [m_mix_pre_g, m_mix_post_g, m_ffn_pre_g, m_ffn_post_g, m_cm_w_in, m_cm_b_in, m_cm_dw, m_cm_dw_b,
                            m_cm_ln_g, m_cm_ln_b, m_cm_w_out, m_cm_b_out, m_kv_norm_g, m_w_kv, m_w_q, m_w_o, m_ffn_w_in,
                            m_ffn_dw, m_ffn_dw_b, m_ffn_w_out]))
    v_in = dict(zip(names, [v_mix_pre_g, v_mix_post_g, v_ffn_pre_g, v_ffn_post_g, v_cm_w_in, v_cm_b_in, v_cm_dw, v_cm_dw_b,
                            v_cm_ln_g, v_cm_ln_b, v_cm_w_out, v_cm_b_out, v_kv_norm_g, v_w_kv, v_w_q, v_w_o, v_ffn_w_in,
                            v_ffn_dw, v_ffn_dw_b, v_ffn_w_out]))

    bsz, seq, d = x.shape
    t = bsz * seq
    n_b = DEPTH - N_A
    hw = w_o.shape[-1]
    qw = N_GROUPS * hw
    f2 = ffn_dw_b.shape[-1]
    f = f2 // 2
    me_chip = _chip_of(lax.axis_index("x"), lax.axis_index("y"))

    big = ["cm_w_in", "cm_w_out", "w_kv", "w_q", "w_o", "ffn_w_in", "ffn_w_out"]
    row_sharded = ("cm_w_out", "w_o", "ffn_w_out")
    small_sharded = ["cm_b_in", "cm_dw", "cm_dw_b", "cm_ln_g", "cm_ln_b", "cm_b_out", "ffn_dw"]
    small_pack = _pack([w_in[n] for n in small_sharded])
    chunks = [
        [("cm_w_in", 0), ("small", None)],
        [("cm_w_out", 0)],
        [("ffn_w_in", 0), ("ffn_w_out", 0)],
        [("cm_w_in", 1), ("cm_w_out", 1)],
        [("ffn_w_in", 1), ("ffn_w_out", 1)],
        [("w_kv", None)],
        [("w_q", 0), ("w_o", 0)],
        [("ffn_w_in", 2), ("ffn_w_out", 2)],
        [("w_q", 1), ("w_o", 1)],
        [("ffn_w_in", 3), ("ffn_w_out", 3)],
    ]
    pieces = [pc for ch in chunks for pc in ch]
    chunk_of = {pc: ck for ck, ch in enumerate(chunks) for pc in ch}

    me_arr = me_chip.astype(jnp.int32).reshape(1)

    def land_of(pc):
        n, l = pc
        if n == "small":
            return cast_place(small_pack, None, me_arr, F32, name="place_small")
        return cast_place(w_in[n], l, me_arr, BF16, name=f"place_{n}_{l}")

    lands = [land_of(pc) for pc in pieces]
    g_send, g_recv, lands_f, token = gather_start(lands, [len(ch) for ch in chunks])
    weights = {}

    def finish_chunk(ck, after):
        lo = sum(len(ch) for ch in chunks[:ck])
        hi = lo + len(chunks[ck])
        got = gather_wait(g_send[ck], g_recv[ck], lands_f[lo:hi], after, name=f"gather_wait{ck}")
        for pc, arr in zip(chunks[ck], got):
            weights[pc] = arr.reshape(1, -1, arr.shape[-1]) if pc[0] in row_sharded else arr

    def wmat(n, l=None, after=None):
        if (n, l) not in weights:
            finish_chunk(chunk_of[(n, l)], after)
        arr = weights[(n, l)]
        return arr, arr.shape[0]

    finish_chunk(0, token)
    small_full = {}
    for n, arr4 in zip(small_sharded, _unpack(weights[("small", None)], [w_in[n].shape for n in small_sharded], lead=(N_CHIPS,))):
        shp = w_in[n].shape
        small_full[n] = jnp.moveaxis(arr4, 0, -2).reshape(*shp[:-1], N_CHIPS * shp[-1])

    x2d = x.reshape(t, d)
    saved = []
    (h1,) = resid_norm_fwd(x2d, None, None, [_row(mix_pre_g[0])], name="norm_in", dep=token)
    xcur = x2d
    kv_state = None
    for i in range(DEPTH):
        sv = {"x_in": xcur, "h1": h1}
        if i < N_A:
            z = mm_nn(h1, *wmat("cm_w_in", i, h1), 1, 0, bias=_row(small_full["cm_b_in"][i]), name=f"cm_in{i}")
            u2 = glu_conv_fwd(z.reshape(bsz, seq, 2 * d), small_full["cm_dw"][i], _row(small_full["cm_dw_b"][i]),
                              name=f"glu_conv{i}").reshape(t, d)
            u4 = ln_silu_fwd(u2, _row(small_full["cm_ln_g"][i]), _row(small_full["cm_ln_b"][i]), name=f"ln_silu{i}")
            y = mm_nn(u4, *wmat("cm_w_out", i, u4), 1, 0, bias=_row(small_full["cm_b_out"][i]), name=f"cm_out{i}")
            sv.update(z=z, u2=u2, u4=u4)
        else:
            j = i - N_A
            q = mm_nn(h1, *wmat("w_q", j, h1), 1, 0, name=f"q_proj{j}").reshape(bsz, seq, qw)
            outs, lses = [], []
            for g, dil in enumerate(DILATIONS):
                o_g, l_g = attn_fwd(q, kv_state["kv"], g, dil, hw, name=f"attn_fwd{j}_{g}")
                outs.append(o_g.reshape(t, hw))
                lses.append(l_g.reshape(t, LANES))
            merged, lsej = attn_merge(outs, lses, name=f"attn_merge{j}")
            y = mm_nn(merged, *wmat("w_o", j, merged), 1, 0, name=f"o_proj{j}")
            sv.update(q=q, merged=merged, lsej=lsej)
        x1, h2 = resid_norm_fwd(xcur, y, _row(mix_post_g[i]), [_row(ffn_pre_g[i])], name=f"resid_mix{i}")
        p = mm_nn(h2, *wmat("ffn_w_in", i, h2), 1, 0, out_dtype=BF16, name=f"ffn_in{i}")
        s_act = ffn_mid_fwd(p.reshape(bsz, seq, f2), small_full["ffn_dw"][i], _row(ffn_dw_b[i]), name=f"ffn_mid{i}").reshape(t, f)
        y2 = mm_nn(s_act, *wmat("ffn_w_out", i), 1, 0, name=f"ffn_out{i}")
        next_gains = []
        if i + 1 < DEPTH:
            next_gains.append(_row(mix_pre_g[i + 1]))
        if i == N_A - 1:
            next_gains.append(_row(kv_norm_g))
        res = resid_norm_fwd(x1, y2, _row(ffn_post_g[i]), next_gains, name=f"resid_ffn{i}")
        sv.update(y=y, x1=x1, h2=h2, p=p, s=s_act, y2=y2)
        saved.append(sv)
        xcur = res[0]
        if i + 1 < DEPTH:
            h1 = res[1]
        if i == N_A - 1:
            kvn = res[2]
            kv = mm_nn(kvn, *wmat("w_kv", None, kvn), 1, 0, name="kv_proj").reshape(bsz, seq, 2 * qw)
            kv_state = {"kv": kv, "kvn": kvn, "x_a": xcur}

    dx, loss_tile = loss_fwd_bwd(xcur, loss_target.reshape(t, d))
    loss = lax.psum(loss_tile[0, 0], ("x", "y", "c"))

    gsm = {n: [None] * w_in[n].shape[0] for n in
           ["mix_pre_g", "mix_post_g", "ffn_pre_g", "ffn_post_g", "cm_b_in", "cm_dw", "cm_dw_b", "cm_ln_g", "cm_ln_b",
            "cm_b_out", "ffn_dw", "ffn_dw_b"]}
    gbig = {}
    in_flight = []
    dep = None

    def start_scatter(pcs, tag):
        ssem, rsem, g_f, r_f, tok = scatter_start([gbig[pc] for pc in pcs], name=f"scatter_start_{tag}")
        in_flight.append((pcs, ssem, rsem, g_f, r_f))
        return tok

    dk_buf = dv_buf = None
    for i in range(DEPTH - 1, -1, -1):
        sv = saved[i]
        dy2, dg, _ = norm_bwd(sv["y2"], _row(ffn_post_g[i]), dx, out_dtype=BF16, name=f"bwd_ffn_post{i}", dep=dep)
        gsm["ffn_post_g"][i] = dg
        ds = mm_nt(dy2, *wmat("ffn_w_out", i), 1, 0, out_dtype=BF16, name=f"bwd_ffn_out_dx{i}")
        gbig[("ffn_w_out", i)] = mm_tn(sv["s"], dy2, 1, name=f"bwd_ffn_out_dw{i}").reshape(N_CHIPS, f // N_CHIPS, d)
        dpa, dpg, ddwa, ddwg, ddba, ddbg = ffn_mid_bwd(sv["p"].reshape(bsz, seq, f2), small_full["ffn_dw"][i], _row(ffn_dw_b[i]),
                                                       ds.reshape(bsz, seq, f), name=f"bwd_ffn_mid{i}")
        gsm["ffn_dw"][i] = jnp.concatenate([jnp.sum(ddwa, axis=1), jnp.sum(ddwg, axis=1)], axis=-1)
        gsm["ffn_dw_b"][i] = jnp.concatenate([ddba, ddbg], axis=-1)
        dp = [dpa.reshape(t, f), dpg.reshape(t, f)]
        dh2 = mm_nt(dp, *wmat("ffn_w_in", i), 1, 0, name=f"bwd_ffn_in_dx{i}")
        gbig[("ffn_w_in", i)] = mm_tn(sv["h2"], dp, N_CHIPS, name=f"bwd_ffn_in_dw{i}")
        dx1, dg, _ = norm_bwd(sv["x1"], _row(ffn_pre_g[i]), dh2, add=dx, name=f"bwd_ffn_pre{i}")
        gsm["ffn_pre_g"][i] = dg
        dep = start_scatter([("ffn_w_in", 0), ("ffn_w_out", 0)], "ffn0") if i == 0 else None
        dy, dg, dbias = norm_bwd(sv["y"], _row(mix_post_g[i]), dx1, out_dtype=BF16, name=f"bwd_mix_post{i}", dep=dep)
        gsm["mix_post_g"][i] = dg
        if i < N_A:
            gsm["cm_b_out"][i] = dbias
            du4 = mm_nt(dy, *wmat("cm_w_out", i), 1, 0, name=f"bwd_cm_out_dx{i}")
            gbig[("cm_w_out", i)] = mm_tn(sv["u4"], dy, 1, name=f"bwd_cm_out_dw{i}").reshape(N_CHIPS, d // N_CHIPS, d)
            du2, dlg, dlb = ln_silu_bwd(sv["u2"], _row(small_full["cm_ln_g"][i]), _row(small_full["cm_ln_b"][i]), du4,
                                        name=f"bwd_ln_silu{i}")
            gsm["cm_ln_g"][i], gsm["cm_ln_b"][i] = dlg, dlb
            dza, dzg, ddw, ddwb, dba, dbg = glu_conv_bwd(sv["z"].reshape(bsz, seq, 2 * d), small_full["cm_dw"][i],
                                                         du2.reshape(bsz, seq, d), name=f"bwd_glu_conv{i}")
            gsm["cm_dw"][i] = jnp.sum(ddw, axis=1)
            gsm["cm_dw_b"][i] = ddwb
            gsm["cm_b_in"][i] = jnp.concatenate([dba, dbg], axis=-1)
            dz = [dza.reshape(t, d), dzg.reshape(t, d)]
            dh1 = mm_nt(dz, *wmat("cm_w_in", i), 1, 0, name=f"bwd_cm_in_dx{i}")
            gbig[("cm_w_in", i)] = mm_tn(sv["h1"], dz, N_CHIPS, name=f"bwd_cm_in_dw{i}")
        else:
            j = i - N_A
            dmerged = mm_nt(dy, *wmat("w_o", j), 1, 0, name=f"bwd_o_proj_dx{j}")
            gbig[("w_o", j)] = mm_tn(sv["merged"], dy, 1, name=f"bwd_o_proj_dw{j}").reshape(N_CHIPS, hw // N_CHIPS, d)
            dmt = attn_bwd_prep(dmerged, sv["merged"], name=f"bwd_attn_prep{j}")
            dq_buf = None
            add_to_kv = dk_buf is not None
            for g, dil in enumerate(DILATIONS):
                dq_buf, dk_buf, dv_buf = attn_bwd(
                    sv["q"], kv_state["kv"], g, dil, dmerged.reshape(bsz, seq, hw), sv["lsej"].reshape(bsz, seq, LANES),
                    dmt.reshape(bsz, seq, LANES), dq_buf, dk_buf, dv_buf, add_to_kv, hw, name=f"attn_bwd{j}_{g}")
            dq = dq_buf.reshape(t, qw)
            dh1 = mm_nt(dq, *wmat("w_q", j), 1, 0, name=f"bwd_q_proj_dx{j}")
            gbig[("w_q", j)] = mm_tn(sv["h1"], dq, N_CHIPS, name=f"bwd_q_proj_dw{j}")
        dx, dg, _ = norm_bwd(sv["x_in"], _row(mix_pre_g[i]), dh1, add=dx1, name=f"bwd_mix_pre{i}")
        gsm["mix_pre_g"][i] = dg
        if i > N_A:
            dep = start_scatter([("ffn_w_in", i), ("ffn_w_out", i), ("w_q", i - N_A), ("w_o", i - N_A)], f"l{i}")
        elif 0 < i < N_A:
            dep = start_scatter([("ffn_w_in", i), ("ffn_w_out", i), ("cm_w_in", i), ("cm_w_out", i)], f"l{i}")
        elif i == 0:
            last_token = start_scatter([("cm_w_in", 0), ("cm_w_out", 0)], "cm0")
        if i == N_A:
            dkv = [dk_buf.reshape(t, qw), dv_buf.reshape(t, qw)]
            dkvn = mm_nt(dkv, *wmat("w_kv"), 1, 0, name="bwd_kv_proj_dx")
            gbig[("w_kv", None)] = mm_tn(kv_state["kvn"], dkv, N_CHIPS, name="bwd_kv_proj_dw")
            dx, dg_kv, _ = norm_bwd(kv_state["x_a"], _row(kv_norm_g), dkvn, add=dx, name="bwd_kv_norm")
            dep = start_scatter([("ffn_w_in", i), ("ffn_w_out", i), ("w_q", 0), ("w_o", 0), ("w_kv", None)], f"l{i}")
    grad_x = dx.reshape(bsz, seq, d)

    plane_of = {}
    outs_g, outs_d, outs_m, outs_v = {}, {}, {}, {}

    def finish_scatter(k, after):
        pcs, ssem, rsem, g_f, r_f = in_flight[k]
        g_done, r_done = scatter_wait(ssem, rsem, g_f, r_f, after, name=f"scatter_wait{k}")
        for pc, g_arr, r_arr in zip(pcs, g_done, r_done):
            plane_of[pc] = sum_parts(g_arr, r_arr, me_arr, name=f"sum_chips_{pc[0]}_{pc[1]}")

    def update(group, tag, after):
        plane = []
        for n in group:
            if w_in[n].ndim == 2:
                plane.append(plane_of[(n, None)])
            else:
                plane.append(jnp.concatenate([plane_of[(n, l)] for l in range(w_in[n].shape[0])], axis=0))
        ssems, rsems, plane_f, land_f, _ = swap_start(plane, name=f"swap_start_{tag}")
        for k, n in enumerate(group):
            p_mine, p_other = swap_wait(ssems[k], rsems[k], plane_f[k], land_f[k], after, name=f"swap_wait_{n}")
            shp = w_in[n].shape
            flat = lambda a: a.reshape(-1, shp[-1])
            g_, d_, m_, v_ = adamw(flat(w_in[n]), flat(m_in[n]), flat(v_in[n]), [p_mine, p_other], name=f"adamw_{n}")
            outs_g[n], outs_d[n], outs_m[n], outs_v[n] = (a.reshape(shp) for a in (g_, d_, m_, v_))
            after = v_
        return after

    small_names = [n for n in names if n not in big]
    small_shapes_full = {}
    small_grads_full = []
    for n in small_names:
        if n == "kv_norm_g":
            gfull = dg_kv.reshape(-1)
        elif n in ("cm_dw", "ffn_dw"):
            gfull = jnp.stack(gsm[n], axis=0)
        else:
            gfull = jnp.stack([a.reshape(-1) for a in gsm[n]], axis=0)
        small_shapes_full[n] = gfull.shape
        small_grads_full.append(gfull)
    dev_arr = (4 * lax.axis_index("x") + 2 * lax.axis_index("y") + lax.axis_index("c")).astype(jnp.int32).reshape(1)
    small_land = cast_place(_pack(small_grads_full) + last_token[0, 0], None, dev_arr, F32, name="place_small_grads",
                            nslots=N_DEV)
    sm_send, sm_recv, small_land, small_token = small_start(small_land)

    for k in range(len(in_flight) - 1):
        finish_scatter(k, small_token)
    done = update(["w_kv", "w_q", "w_o", "ffn_w_in", "ffn_w_out"], "a", small_token)
    finish_scatter(len(in_flight) - 1, done)
    done = update(["cm_w_in", "cm_w_out"], "b", done)

    summed = sum_slots(small_wait(sm_send, sm_recv, small_land, done))
    g_full = dict(zip(small_names, _unpack(summed, [small_shapes_full[n] for n in small_names])))
    g_loc = {}
    for n in small_names:
        if n in small_sharded:
            width = w_in[n].shape[-1]
            g_loc[n] = lax.dynamic_slice_in_dim(g_full[n], me_chip * width, width, axis=g_full[n].ndim - 1)
        else:
            g_loc[n] = g_full[n]
    res = adamw(_pack([w_in[n] for n in small_names]), _pack([m_in[n] for n in small_names]),
                _pack([v_in[n] for n in small_names]), [_pack([g_loc[n] for n in small_names])], name="adamw_small")
    shapes_loc = [w_in[n].shape for n in small_names]
    for dst, packed in zip((outs_g, outs_d, outs_m, outs_v), res):
        for n, a in zip(small_names, _unpack(packed, shapes_loc)):
            dst[n] = a

    return (loss, grad_x, *[outs_g[n] for n in names], *[outs_d[n] for n in names],
            *[outs_m[n] for n in names], *[outs_v[n] for n in names])
```

```python
import functools
import math

import jax
import jax.numpy as jnp
from jax import lax
from jax.experimental import pallas as pl
from jax.experimental.pallas import tpu as pltpu

F32 = jnp.float32
BF16 = jnp.bfloat16
EPS = 1e-6
NEG_INF = -1e30
N_A = 2
DEPTH = 4
N_GROUPS = 3
DILATIONS = (1, 4, 16)
HEAD_DIM = 128
BLK = 128
LANES = 128
N_CHIPS = 4
N_DEV = 8
VMEM_LIMIT_V7X = 56 * 1024 * 1024

ADAM_LR = 0.001
ADAM_B1 = 0.9
ADAM_B2 = 0.999
ADAM_EPS = 1e-08
ADAM_WD = 0.01
ADAM_STEP = 10

MESH = pl.DeviceIdType.MESH


def _cp(*sem, **kw):
    return pltpu.CompilerParams(dimension_semantics=sem if sem else None, vmem_limit_bytes=VMEM_LIMIT_V7X, **kw)


def _dot(a, b):
    return jnp.dot(a, b, preferred_element_type=F32)


def _dot_nt(a, b):
    return lax.dot_general(a, b, (((1,), (1,)), ((), ())), preferred_element_type=F32)


def _dot_tn(a, b):
    return lax.dot_general(a, b, (((0,), (0,)), ((), ())), preferred_element_type=F32)


def _sigmoid(x):
    return 1.0 / (1.0 + jnp.exp(-x))


def _row_tile(n, want):
    if n <= want:
        return n
    for t in range(want - want % 8, 7, -8):
        if n % t == 0:
            return t
    raise ValueError(f"no row tile for {n} rows")


def mm_nn(a, w, nsh, stride, layer, bias=None, out_dtype=F32, name="mm_nn", tm=1024, head_major=False):
    m, k = a.shape
    _, k2, ns = w.shape
    assert k == k2
    tm = _row_tile(m, tm)
    has_bias = bias is not None
    hps = ns // HEAD_DIM

    def body(*refs):
        if has_bias:
            a_ref, w_ref, b_ref, o_ref = refs
        else:
            a_ref, w_ref, o_ref = refs
        acc = _dot(a_ref[...].astype(BF16), w_ref[...])
        if has_bias:
            acc = acc + b_ref[...]
        if head_major:
            for hh in range(hps):
                o_ref[hh] = acc[:, hh * HEAD_DIM:(hh + 1) * HEAD_DIM].astype(out_dtype)
        else:
            o_ref[...] = acc.astype(out_dtype)

    in_specs = [
        pl.BlockSpec((tm, k), lambda j, i: (i, 0)),
        pl.BlockSpec((None, k, ns), lambda j, i: (j * stride + layer, 0, 0)),
    ]
    args = [a, w]
    if has_bias:
        in_specs.append(pl.BlockSpec((1, ns), lambda j, i: (0, j)))
        args.append(bias)
    return pl.pallas_call(
        body,
        name=name,
        grid=(nsh, m // tm),
        in_specs=in_specs,
        out_specs=(pl.BlockSpec((hps, tm, HEAD_DIM), lambda j, i: (j, i, 0)) if head_major
                   else pl.BlockSpec((tm, ns), lambda j, i: (i, j))),
        out_shape=jax.ShapeDtypeStruct((nsh * hps, m, HEAD_DIM) if head_major else (m, nsh * ns), out_dtype),
        compiler_params=_cp("parallel", "parallel"),
    )(*args)


def mm_nt(dy, w, nsh, stride, layer, out_dtype=F32, name="mm_nt", tm=1024):
    dys = list(dy) if isinstance(dy, (list, tuple)) else [dy]
    npart = len(dys)
    per = nsh // npart
    m = dys[0].shape[0]
    _, k, ns = w.shape
    assert all(d.shape == (m, per * ns) for d in dys)
    tm = _row_tile(m, tm)

    def body(*refs):
        dy_refs = refs[:npart]
        w_ref, o_ref, acc_ref = refs[npart:]
        j = pl.program_id(1)

        @pl.when(j == 0)
        def _():
            acc_ref[...] = jnp.zeros(acc_ref.shape, F32)

        for pi in range(npart):
            @pl.when(j // per == pi)
            def _(pi=pi):
                acc_ref[...] += _dot_nt(dy_refs[pi][...].astype(BF16), w_ref[...])

        @pl.when(j == nsh - 1)
        def _():
            o_ref[...] = acc_ref[...].astype(out_dtype)

    dy_specs = [pl.BlockSpec((tm, ns), lambda i, j, pi=pi: (i, jnp.clip(j - pi * per, 0, per - 1))) for pi in range(npart)]
    return pl.pallas_call(
        body,
        name=name,
        grid=(m // tm, nsh),
        in_specs=[*dy_specs, pl.BlockSpec((None, k, ns), lambda i, j: (j * stride + layer, 0, 0))],
        out_specs=pl.BlockSpec((tm, k), lambda i, j: (i, 0)),
        out_shape=jax.ShapeDtypeStruct((m, k), out_dtype),
        scratch_shapes=[pltpu.VMEM((tm, k), F32)],
        compiler_params=_cp("parallel", "arbitrary"),
    )(*dys, w)


def mm_tn(a, dy, nsh, name="mm_tn", tm=1024):
    dys = list(dy) if isinstance(dy, (list, tuple)) else [dy]
    npart = len(dys)
    per = nsh // npart
    m, k = a.shape
    ns = dys[0].shape[1] // per
    assert all(d.shape == (m, per * ns) for d in dys)
    tm = _row_tile(m, tm)
    nt = m // tm

    def body(*refs):
        a_ref = refs[0]
        dy_refs = refs[1:1 + npart]
        o_ref, acc_ref = refs[1 + npart:]
        j = pl.program_id(0)
        i = pl.program_id(1)

        @pl.when(i == 0)
        def _():
            acc_ref[...] = jnp.zeros(acc_ref.shape, F32)

        for pi in range(npart):
            @pl.when(j // per == pi)
            def _(pi=pi):
                acc_ref[...] += _dot_tn(a_ref[...].astype(BF16), dy_refs[pi][...].astype(BF16))

        @pl.when(i == nt - 1)
        def _():
            o_ref[...] = acc_ref[...].astype(BF16)

    dy_specs = [
        pl.BlockSpec((tm, ns), lambda j, i, pi=pi: (jnp.where(j // per == pi, i, 0), jnp.clip(j - pi * per, 0, per - 1)))
        for pi in range(npart)
    ]
    return pl.pallas_call(
        body,
        name=name,
        grid=(nsh, nt),
        in_specs=[pl.BlockSpec((tm, k), lambda j, i: (i, 0)), *dy_specs],
        out_specs=pl.BlockSpec((None, k, ns), lambda j, i: (j, 0, 0)),
        out_shape=jax.ShapeDtypeStruct((nsh, k, ns), BF16),
        scratch_shapes=[pltpu.VMEM((k, ns), F32)],
        compiler_params=_cp("parallel", "arbitrary"),
    )(a, *dys)


DEP_SPEC_SHAPE = (8, LANES)


def resid_norm_fwd(x, y, g_post, next_gains, name, tm=256, dep=None):
    t, d = x.shape
    tm = _row_tile(t, tm)
    has_y = y is not None
    n_next = len(next_gains)
    n_dep = 0 if dep is None else 1

    def body(*refs):
        x_ref = refs[0]
        pos = 1
        if has_y:
            y_ref, gp_ref = refs[1], refs[2]
            pos = 3
        gn_refs = refs[pos:pos + n_next]
        outs = refs[pos + n_next + n_dep:]
        xv = x_ref[...]
        o = 0
        if has_y:
            yv = y_ref[...]
            r = lax.rsqrt(jnp.mean(yv * yv, axis=-1, keepdims=True) + EPS)
            xv = xv + (yv * r) * gp_ref[...]
            outs[0][...] = xv
            o = 1
        if n_next:
            xn = xv * lax.rsqrt(jnp.mean(xv * xv, axis=-1, keepdims=True) + EPS)
            for k in range(n_next):
                outs[o + k][...] = (xn * gn_refs[k][...]).astype(BF16)

    row = pl.BlockSpec((tm, d), lambda i: (i, 0))
    vec = pl.BlockSpec((1, d), lambda i: (0, 0))
    args, in_specs = [x], [row]
    if has_y:
        args += [y, g_post]
        in_specs += [row, vec]
    args += list(next_gains)
    in_specs += [vec] * n_next
    if n_dep:
        args.append(dep)
        in_specs.append(pl.BlockSpec(DEP_SPEC_SHAPE, lambda i: (0, 0)))
    out_shape, out_specs = [], []
    if has_y:
        out_shape.append(jax.ShapeDtypeStruct((t, d), F32))
        out_specs.append(row)
    for _ in range(n_next):
        out_shape.append(jax.ShapeDtypeStruct((t, d), BF16))
        out_specs.append(row)
    return pl.pallas_call(
        body, name=name, grid=(t // tm,), in_specs=in_specs, out_specs=out_specs, out_shape=out_shape,
        compiler_params=_cp("parallel"),
    )(*args)


def norm_bwd(x, g, dy, add=None, out_dtype=F32, name="norm_bwd", tm=256, dep=None):
    t, d = x.shape
    tm = _row_tile(t, tm)
    has_add = add is not None

    def body(*refs):
        x_ref, g_ref, dy_ref = refs[:3]
        add_ref = refs[3] if has_add else None
        dx_ref, dg_ref, cs_ref = refs[-3:]
        i = pl.program_id(0)
        xv = x_ref[...]
        dyv = dy_ref[...].astype(F32)
        r = lax.rsqrt(jnp.mean(xv * xv, axis=-1, keepdims=True) + EPS)
        gd = dyv * g_ref[...]
        dx = r * gd - xv * ((r * r * r) * jnp.mean(xv * gd, axis=-1, keepdims=True))
        if has_add:
            dx = dx + add_ref[...]
        dx_ref[...] = dx.astype(out_dtype)
        dg = jnp.sum(dyv * (xv * r), axis=0, keepdims=True)
        cs = jnp.sum(dx, axis=0, keepdims=True)

        @pl.when(i == 0)
        def _():
            dg_ref[...] = dg
            cs_ref[...] = cs

        @pl.when(i > 0)
        def _():
            dg_ref[...] += dg
            cs_ref[...] += cs

    row = pl.BlockSpec((tm, d), lambda i: (i, 0))
    vec = pl.BlockSpec((1, d), lambda i: (0, 0))
    args, in_specs = [x, g, dy], [row, vec, row]
    if has_add:
        args.append(add)
        in_specs.append(row)
    if dep is not None:
        args.append(dep)
        in_specs.append(pl.BlockSpec(DEP_SPEC_SHAPE, lambda i: (0, 0)))
    return pl.pallas_call(
        body, name=name, grid=(t // tm,), in_specs=in_specs,
        out_specs=[row, vec, vec],
        out_shape=[jax.ShapeDtypeStruct((t, d), out_dtype), jax.ShapeDtypeStruct((1, d), F32),
                   jax.ShapeDtypeStruct((1, d), F32)],
        compiler_params=_cp("arbitrary"),
    )(*args)


def loss_fwd_bwd(x, target, name="loss", tm=256):
    t, d = x.shape
    tm = _row_tile(t, tm)

    def body(x_ref, t_ref, dx_ref, l_ref):
        i = pl.program_id(0)
        err = x_ref[...] - t_ref[...]
        dx_ref[...] = err * (1.0 / d)
        part = 0.5 * jnp.sum(jnp.mean(err * err, axis=-1, keepdims=True), axis=0, keepdims=True)
        part = jnp.broadcast_to(part, l_ref.shape)

        @pl.when(i == 0)
        def _():
            l_ref[...] = part

        @pl.when(i > 0)
        def _():
            l_ref[...] += part

    row = pl.BlockSpec((tm, d), lambda i: (i, 0))
    return pl.pallas_call(
        body, name=name, grid=(t // tm,), in_specs=[row, row],
        out_specs=[row, pl.BlockSpec((8, LANES), lambda i: (0, 0))],
        out_shape=[jax.ShapeDtypeStruct((t, d), F32), jax.ShapeDtypeStruct((8, LANES), F32)],
        compiler_params=_cp("arbitrary"),
    )(x, target)


CONV_HALO = 32
CONV_CHUNK = 128


def glu_conv_fwd(z, dw, dwb, name, tc=128):
    b, s, c2 = z.shape
    c = c2 // 2
    kw = dw.shape[0]
    tc = min(tc, c)
    nc = c // tc
    ch = min(CONV_CHUNK, s)
    halo = CONV_HALO
    assert kw - 1 <= halo and s % ch == 0

    nch = s // ch

    def body(a_ref, g_ref, w_ref, b_ref, o_ref, pad_ref):
        _fill_glu_slabs(a_ref, g_ref, pad_ref, nch, ch, halo)

        def chunk(ci, carry):
            r0 = pl.multiple_of(ci * ch, ch)
            acc = b_ref[...]
            for k, tap in enumerate(_taps_front(pad_ref, ci, kw, ch, halo)):
                acc = acc + w_ref[k:k + 1, :] * tap
            o_ref[pl.ds(r0, ch), :] = acc
            return carry

        lax.fori_loop(0, nch, chunk, 0)

    return pl.pallas_call(
        body, name=name, grid=(b, nc),
        in_specs=[
            pl.BlockSpec((None, s, tc), lambda bi, i: (bi, 0, i)),
            pl.BlockSpec((None, s, tc), lambda bi, i: (bi, 0, i + nc)),
            pl.BlockSpec((kw, tc), lambda bi, i: (0, i)),
            pl.BlockSpec((1, tc), lambda bi, i: (0, i)),
        ],
        out_specs=pl.BlockSpec((None, s, tc), lambda bi, i: (bi, 0, i)),
        out_shape=jax.ShapeDtypeStruct((b, s, c), F32),
        scratch_shapes=[pltpu.VMEM((nch, ch + halo, tc), F32)],
        compiler_params=_cp("parallel", "parallel"),
    )(z, z, dw, dwb)


def glu_conv_bwd(z, dw, du2, name, tc=128):
    b, s, c2 = z.shape
    c = c2 // 2
    kw = dw.shape[0]
    tc = min(tc, c)
    nc = c // tc
    ch = min(CONV_CHUNK, s)
    nch = s // ch
    halo = CONV_HALO

    def body(a_ref, g_ref, w_ref, du_ref, dza_ref, dzg_ref, ddw_ref, ddwb_ref, dba_ref, dbg_ref, upad_ref, dpad_ref):
        bi = pl.program_id(1)

        @pl.when(bi == 0)
        def _():
            ddw_ref[...] = jnp.zeros(ddw_ref.shape, F32)
            ddwb_ref[...] = jnp.zeros(ddwb_ref.shape, F32)
            dba_ref[...] = jnp.zeros(dba_ref.shape, F32)
            dbg_ref[...] = jnp.zeros(dbg_ref.shape, F32)

        _fill_glu_slabs(a_ref, g_ref, upad_ref, nch, ch, halo)
        dpad_ref[nch - 1, ch:ch + halo, :] = jnp.zeros((halo, tc), F32)
        dpad_ref[nch - 1, 0:ch, :] = du_ref[s - ch:s, :]

        def fill(ci, carry):
            r0 = pl.multiple_of(ci * ch, ch)
            dpad_ref[ci, :, :] = du_ref[pl.ds(r0, ch + halo), :]
            return carry

        lax.fori_loop(0, nch - 1, fill, 0)

        def chunk(ci, carry):
            r0 = pl.multiple_of(ci * ch, ch)
            du_c = du_ref[pl.ds(r0, ch), :]
            taps_u = _taps_front(upad_ref, ci, kw, ch, halo)
            du1 = w_ref[kw - 1:kw, :] * du_c
            ddw_ref[kw - 1] += jnp.sum((du_c * taps_u[kw - 1]).reshape(ch // 8, 8, tc), axis=0)
            for j in range(1, kw):
                du1 = du1 + w_ref[kw - 1 - j:kw - j, :] * dpad_ref[ci, j:j + ch, :]
                ddw_ref[kw - 1 - j] += jnp.sum((du_c * taps_u[kw - 1 - j]).reshape(ch // 8, 8, tc), axis=0)
            av = a_ref[pl.ds(r0, ch), :]
            sg = _sigmoid(g_ref[pl.ds(r0, ch), :])
            dza = du1 * sg
            dzg = du1 * av * (sg * (1.0 - sg))
            dza_ref[pl.ds(r0, ch), :] = dza.astype(BF16)
            dzg_ref[pl.ds(r0, ch), :] = dzg.astype(BF16)
            dba_ref[...] += jnp.sum(dza, axis=0, keepdims=True)
            dbg_ref[...] += jnp.sum(dzg, axis=0, keepdims=True)
            ddwb_ref[...] += jnp.sum(du_c, axis=0, keepdims=True)
            return carry

        lax.fori_loop(0, s // ch, chunk, 0)

    blk = lambda off: pl.BlockSpec((None, s, tc), lambda i, bi: (bi, 0, i + off))
    vec = pl.BlockSpec((1, tc), lambda i, bi: (0, i))
    return pl.pallas_call(
        body, name=name, grid=(nc, b),
        in_specs=[blk(0), blk(nc), pl.BlockSpec((kw, tc), lambda i, bi: (0, i)), blk(0)],
        out_specs=[blk(0), blk(0), pl.BlockSpec((kw, 8, tc), lambda i, bi: (0, 0, i)), vec, vec, vec],
        out_shape=[
            jax.ShapeDtypeStruct((b, s, c), BF16), jax.ShapeDtypeStruct((b, s, c), BF16),
            jax.ShapeDtypeStruct((kw, 8, c), F32), jax.ShapeDtypeStruct((1, c), F32),
            jax.ShapeDtypeStruct((1, c), F32), jax.ShapeDtypeStruct((1, c), F32),
        ],
        scratch_shapes=[pltpu.VMEM((nch, ch + halo, tc), F32), pltpu.VMEM((nch, ch + halo, tc), F32)],
        compiler_params=_cp("parallel", "arbitrary"),
    )(z, z, dw, du2)


def _fill_glu_slabs(a_ref, g_ref, pad_ref, nch, ch, halo):
    tc = a_ref.shape[-1]
    pad_ref[0, 0:halo, :] = jnp.zeros((halo, tc), F32)
    pad_ref[0, halo:halo + ch, :] = a_ref[0:ch, :] * _sigmoid(g_ref[0:ch, :])

    def fill(ci, carry):
        r0 = pl.multiple_of(ci * ch, ch)
        pad_ref[ci, 0:halo, :] = pad_ref[ci - 1, ch:ch + halo, :]
        pad_ref[ci, halo:halo + ch, :] = a_ref[pl.ds(r0, ch), :] * _sigmoid(g_ref[pl.ds(r0, ch), :])
        return carry

    lax.fori_loop(1, nch, fill, 0)


def ln_silu_fwd(u, g, bvec, name, tm=256):
    t, d = u.shape
    tm = _row_tile(t, tm)

    def body(u_ref, g_ref, b_ref, o_ref):
        uv = u_ref[...]
        mu = jnp.mean(uv, axis=-1, keepdims=True)
        xc = uv - mu
        var = jnp.mean(xc * xc, axis=-1, keepdims=True)
        v = (xc * lax.rsqrt(var + EPS)) * g_ref[...] + b_ref[...]
        o_ref[...] = (v * _sigmoid(v)).astype(BF16)

    row = pl.BlockSpec((tm, d), lambda i: (i, 0))
    vec = pl.BlockSpec((1, d), lambda i: (0, 0))
    return pl.pallas_call(
        body, name=name, grid=(t // tm,), in_specs=[row, vec, vec], out_specs=row,
        out_shape=jax.ShapeDtypeStruct((t, d), BF16), compiler_params=_cp("parallel"),
    )(u, g, bvec)


def ln_silu_bwd(u, g, bvec, dout, name, tm=256):
    t, d = u.shape
    tm = _row_tile(t, tm)

    def body(u_ref, g_ref, b_ref, do_ref, du_ref, dg_ref, db_ref):
        i = pl.program_id(0)
        uv = u_ref[...]
        mu = jnp.mean(uv, axis=-1, keepdims=True)
        xc = uv - mu
        var = jnp.mean(xc * xc, axis=-1, keepdims=True)
        rstd = lax.rsqrt(var + EPS)
        n = xc * rstd
        v = n * g_ref[...] + b_ref[...]
        sg = _sigmoid(v)
        dv = do_ref[...].astype(F32) * (sg * (1.0 + v * (1.0 - sg)))
        dn = dv * g_ref[...]
        du_ref[...] = rstd * (dn - jnp.mean(dn, axis=-1, keepdims=True) - n * jnp.mean(dn * n, axis=-1, keepdims=True))
        dg = jnp.sum(dv * n, axis=0, keepdims=True)
        db = jnp.sum(dv, axis=0, keepdims=True)

        @pl.when(i == 0)
        def _():
            dg_ref[...] = dg
            db_ref[...] = db

        @pl.when(i > 0)
        def _():
            dg_ref[...] += dg
            db_ref[...] += db

    row = pl.BlockSpec((tm, d), lambda i: (i, 0))
    vec = pl.BlockSpec((1, d), lambda i: (0, 0))
    return pl.pallas_call(
        body, name=name, grid=(t // tm,), in_specs=[row, vec, vec, row], out_specs=[row, vec, vec],
        out_shape=[jax.ShapeDtypeStruct((t, d), F32), jax.ShapeDtypeStruct((1, d), F32), jax.ShapeDtypeStruct((1, d), F32)],
        compiler_params=_cp("arbitrary"),
    )(u, g, bvec, dout)


FFN_HALO = 8


def _fill_front_halo(src_ref, pad_ref, nch, ch, halo):
    tc = src_ref.shape[-1]
    pad_ref[0, 0:halo, :] = jnp.zeros((halo, tc), F32)
    pad_ref[0, halo:halo + ch, :] = src_ref[0:ch, :].astype(F32)

    def fill(ci, carry):
        r0 = pl.multiple_of(ci * ch, ch)
        pad_ref[ci, 0:halo, :] = src_ref[pl.ds(r0 - 2 * halo, 2 * halo), :].astype(F32)[halo:, :]
        pad_ref[ci, halo:halo + ch, :] = src_ref[pl.ds(r0, ch), :].astype(F32)
        return carry

    lax.fori_loop(1, nch, fill, 0)


def _taps_front(pad_ref, ci, kw, ch, halo):
    return [pad_ref[ci, halo - (kw - 1 - k):halo - (kw - 1 - k) + ch, :] for k in range(kw)]


def ffn_mid_fwd(p, dw, dwb, name, tc=256):
    b, s, f2 = p.shape
    f = f2 // 2
    kw = dw.shape[0]
    tc = min(tc, f)
    nf = f // tc
    ch = min(CONV_CHUNK, s)
    nch = s // ch
    halo = FFN_HALO

    def body(pa_ref, pg_ref, wa_ref, wg_ref, ba_ref, bg_ref, o_ref, apad_ref, gpad_ref):
        _fill_front_halo(pa_ref, apad_ref, nch, ch, halo)
        _fill_front_halo(pg_ref, gpad_ref, nch, ch, halo)

        def chunk(ci, carry):
            r0 = pl.multiple_of(ci * ch, ch)
            ca = ba_ref[...]
            cg = bg_ref[...]
            taps = zip(_taps_front(apad_ref, ci, kw, ch, halo), _taps_front(gpad_ref, ci, kw, ch, halo))
            for k, (ta, tg) in enumerate(taps):
                ca = ca + wa_ref[k:k + 1, :] * ta
                cg = cg + wg_ref[k:k + 1, :] * tg
            o_ref[pl.ds(r0, ch), :] = ((cg * _sigmoid(cg)) * ca).astype(BF16)
            return carry

        lax.fori_loop(0, nch, chunk, 0)

    blk = lambda off: pl.BlockSpec((None, s, tc), lambda bi, i: (bi, 0, i + off))
    wsp = lambda off: pl.BlockSpec((kw, tc), lambda bi, i: (0, i + off))
    bsp = lambda off: pl.BlockSpec((1, tc), lambda bi, i: (0, i + off))
    return pl.pallas_call(
        body, name=name, grid=(b, nf),
        in_specs=[blk(0), blk(nf), wsp(0), wsp(nf), bsp(0), bsp(nf)],
        out_specs=pl.BlockSpec((None, s, tc), lambda bi, i: (bi, 0, i)),
        out_shape=jax.ShapeDtypeStruct((b, s, f), BF16),
        scratch_shapes=[pltpu.VMEM((nch, ch + halo, tc), F32)] * 2,
        compiler_params=_cp("parallel", "parallel"),
    )(p, p, dw, dw, dwb, dwb)


def ffn_mid_bwd(p, dw, dwb, ds, name, tc=256):
    b, s, f2 = p.shape
    f = f2 // 2
    kw = dw.shape[0]
    tc = min(tc, f)
    nf = f // tc
    ch = min(CONV_CHUNK, s)
    nch = s // ch
    halo = FFN_HALO

    def sum8(v):
        return jnp.sum(v.reshape(ch // 8, 8, tc), axis=0)

    def body(pa_ref, pg_ref, wa_ref, wg_ref, ba_ref, bg_ref, ds_ref, dpa_ref, dpg_ref, ddwa_ref, ddwg_ref, dba_ref, dbg_ref,
             apad_ref, gpad_ref, dca_ref, dcg_ref):
        bi = pl.program_id(1)

        @pl.when(bi == 0)
        def _():
            ddwa_ref[...] = jnp.zeros(ddwa_ref.shape, F32)
            ddwg_ref[...] = jnp.zeros(ddwg_ref.shape, F32)
            dba_ref[...] = jnp.zeros(dba_ref.shape, F32)
            dbg_ref[...] = jnp.zeros(dbg_ref.shape, F32)

        _fill_front_halo(pa_ref, apad_ref, nch, ch, halo)
        _fill_front_halo(pg_ref, gpad_ref, nch, ch, halo)
        dca_ref[nch - 1, ch:ch + halo, :] = jnp.zeros((halo, tc), F32)
        dcg_ref[nch - 1, ch:ch + halo, :] = jnp.zeros((halo, tc), F32)

        def grads(ci, carry):
            acc_a, acc_g, sb_a, sb_g = carry
            r0 = pl.multiple_of(ci * ch, ch)
            taps_a = _taps_front(apad_ref, ci, kw, ch, halo)
            taps_g = _taps_front(gpad_ref, ci, kw, ch, halo)
            ca = ba_ref[...]
            cg = bg_ref[...]
            for k in range(kw):
                ca = ca + wa_ref[k:k + 1, :] * taps_a[k]
                cg = cg + wg_ref[k:k + 1, :] * taps_g[k]
            sg = _sigmoid(cg)
            dsv = ds_ref[pl.ds(r0, ch), :].astype(F32)
            dca = dsv * (cg * sg)
            dcg = dsv * ca * (sg * (1.0 + cg * (1.0 - sg)))
            dca_ref[ci, 0:ch, :] = dca
            dcg_ref[ci, 0:ch, :] = dcg

            prev = jnp.maximum(ci - 1, 0)

            @pl.when(ci > 0)
            def _():
                dca_ref[prev, ch:ch + halo, :] = dca[0:halo, :]
                dcg_ref[prev, ch:ch + halo, :] = dcg[0:halo, :]

            acc_a = tuple(acc_a[k] + sum8(dca * taps_a[k]) for k in range(kw))
            acc_g = tuple(acc_g[k] + sum8(dcg * taps_g[k]) for k in range(kw))
            return acc_a, acc_g, sb_a + sum8(dca), sb_g + sum8(dcg)

        z8 = jnp.zeros((8, tc), F32)
        acc_a, acc_g, sb_a, sb_g = lax.fori_loop(0, nch, grads, ((z8,) * kw, (z8,) * kw, z8, z8))
        for k in range(kw):
            ddwa_ref[k] += acc_a[k]
            ddwg_ref[k] += acc_g[k]
        dba_ref[...] += jnp.sum(sb_a, axis=0, keepdims=True)
        dbg_ref[...] += jnp.sum(sb_g, axis=0, keepdims=True)

        def back(ci, carry):
            r0 = pl.multiple_of(ci * ch, ch)
            da = wa_ref[kw - 1:kw, :] * dca_ref[ci, 0:ch, :]
            dg = wg_ref[kw - 1:kw, :] * dcg_ref[ci, 0:ch, :]
            for j in range(1, kw):
                da = da + wa_ref[kw - 1 - j:kw - j, :] * dca_ref[ci, j:j + ch, :]
                dg = dg + wg_ref[kw - 1 - j:kw - j, :] * dcg_ref[ci, j:j + ch, :]
            dpa_ref[pl.ds(r0, ch), :] = da.astype(BF16)
            dpg_ref[pl.ds(r0, ch), :] = dg.astype(BF16)
            return carry

        lax.fori_loop(0, nch, back, 0)

    blk = lambda off: pl.BlockSpec((None, s, tc), lambda i, bi: (bi, 0, i + off))
    wsp = lambda off: pl.BlockSpec((kw, tc), lambda i, bi: (0, i + off))
    bsp = lambda off: pl.BlockSpec((1, tc), lambda i, bi: (0, i + off))
    acc3 = pl.BlockSpec((kw, 8, tc), lambda i, bi: (0, 0, i))
    vec = pl.BlockSpec((1, tc), lambda i, bi: (0, i))
    return pl.pallas_call(
        body, name=name, grid=(nf, b),
        in_specs=[blk(0), blk(nf), wsp(0), wsp(nf), bsp(0), bsp(nf), blk(0)],
        out_specs=[blk(0), blk(0), acc3, acc3, vec, vec],
        out_shape=[jax.ShapeDtypeStruct((b, s, f), BF16), jax.ShapeDtypeStruct((b, s, f), BF16),
                   jax.ShapeDtypeStruct((kw, 8, f), F32), jax.ShapeDtypeStruct((kw, 8, f), F32),
                   jax.ShapeDtypeStruct((1, f), F32), jax.ShapeDtypeStruct((1, f), F32)],
        scratch_shapes=[pltpu.VMEM((nch, ch + halo, tc), F32)] * 4,
        compiler_params=_cp("parallel", "arbitrary"),
    )(p, p, dw, dw, dwb, dwb, ds)


def _tile_rows(r, n, dil):
    start = r + n * BLK * dil
    return pl.ds(start, BLK, stride=dil) if dil > 1 else pl.ds(start, BLK)


def _band_masks():
    qi = lax.broadcasted_iota(jnp.int32, (BLK, 2 * BLK), 0)
    kk = lax.broadcasted_iota(jnp.int32, (BLK, 2 * BLK), 1)
    both = jnp.logical_or(jnp.logical_and(kk < BLK, kk >= qi), jnp.logical_and(kk >= BLK, kk - BLK <= qi))
    return both, kk[:, :BLK] <= qi[:, :BLK]


def attn_fwd(q, kv, g, dil, hw, name):
    _, b, s, _ = q.shape
    nh = hw // HEAD_DIM
    nblk = s // dil // BLK
    scale = 1.0 / math.sqrt(HEAD_DIM)

    def body(q_ref, k_ref, v_ref, o_ref, lse_ref):
        h = pl.program_id(1)
        mask2, mask1 = _band_masks()
        mine = lax.broadcasted_iota(jnp.int32, (BLK, LANES), 1) == h

        @pl.when(h == 0)
        def _():
            lse_ref[...] = jnp.zeros(lse_ref.shape, F32)

        for r in range(dil):
            kp = vp = None
            for n in range(nblk):
                rs = _tile_rows(r, n, dil)
                qt = q_ref[rs, :].astype(BF16)
                kc = k_ref[rs, :].astype(BF16)
                vc = v_ref[rs, :].astype(BF16)
                if n == 0:
                    kcat, vcat, mask = kc, vc, mask1
                else:
                    kcat, vcat, mask = jnp.concatenate([kp, kc], axis=0), jnp.concatenate([vp, vc], axis=0), mask2
                sc = jnp.where(mask, _dot_nt(qt, kcat) * scale, NEG_INF)
                m = jnp.max(sc, axis=-1, keepdims=True)
                p = jnp.exp(sc - m)
                den = jnp.sum(p, axis=-1, keepdims=True)
                o_ref[rs, :] = _dot(p.astype(BF16), vcat) / den
                lse_ref[rs, :] = jnp.where(mine, m + jnp.log(den), lse_ref[rs, :])
                kp, vp = kc, vc

    col = lambda base: pl.BlockSpec((None, s, HEAD_DIM), lambda bi, h: (bi, 0, base + h))
    head = lambda base: pl.BlockSpec((None, None, s, HEAD_DIM), lambda bi, h: (base + h, bi, 0, 0))
    return pl.pallas_call(
        body, name=name, grid=(b, nh),
        in_specs=[head(g * nh), head(g * nh), head((N_GROUPS + g) * nh)],
        out_specs=[col(0), pl.BlockSpec((None, s, LANES), lambda bi, h: (bi, 0, 0))],
        out_shape=[jax.ShapeDtypeStruct((b, s, hw), F32), jax.ShapeDtypeStruct((b, s, LANES), F32)],
        compiler_params=_cp("parallel", "arbitrary"),
    )(q, kv, kv)


def attn_merge(outs, lses, name, tm=256):
    t, hw = outs[0].shape
    nh = hw // HEAD_DIM
    tm = _row_tile(t, tm)
    ng = len(outs)

    def body(*refs):
        o_refs, l_refs = refs[:ng], refs[ng:2 * ng]
        m_ref, lj_ref = refs[2 * ng:]
        ls = [l_refs[g][...] for g in range(ng)]
        mx = ls[0]
        for g in range(1, ng):
            mx = jnp.maximum(mx, ls[g])
        es = [jnp.exp(l - mx) for l in ls]
        tot = es[0]
        for g in range(1, ng):
            tot = tot + es[g]
        ws = [e / tot for e in es]
        lj_ref[...] = mx + jnp.log(tot)
        for h in range(nh):
            sl = slice(h * HEAD_DIM, (h + 1) * HEAD_DIM)
            acc = ws[0][:, h:h + 1] * o_refs[0][:, sl]
            for g in range(1, ng):
                acc = acc + ws[g][:, h:h + 1] * o_refs[g][:, sl]
            m_ref[:, sl] = acc.astype(BF16)

    row = pl.BlockSpec((tm, hw), lambda i: (i, 0))
    st = pl.BlockSpec((tm, LANES), lambda i: (i, 0))
    return pl.pallas_call(
        body, name=name, grid=(t // tm,), in_specs=[row] * ng + [st] * ng, out_specs=[row, st],
        out_shape=[jax.ShapeDtypeStruct((t, hw), BF16), jax.ShapeDtypeStruct((t, LANES), F32)],
        compiler_params=_cp("parallel"),
    )(*outs, *lses)


def attn_bwd_prep(dmerged, merged, name, tm=256):
    t, hw = merged.shape
    nh = hw // HEAD_DIM
    tm = _row_tile(t, tm)

    def body(d_ref, m_ref, o_ref):
        lane = lax.broadcasted_iota(jnp.int32, (tm, LANES), 1)
        acc = jnp.zeros((tm, LANES), F32)
        for h in range(nh):
            sl = slice(h * HEAD_DIM, (h + 1) * HEAD_DIM)
            dsum = jnp.sum(d_ref[:, sl] * m_ref[:, sl].astype(F32), axis=-1, keepdims=True)
            acc = jnp.where(lane == h, dsum, acc)
        o_ref[...] = acc

    row = pl.BlockSpec((tm, hw), lambda i: (i, 0))
    return pl.pallas_call(
        body, name=name, grid=(t // tm,), in_specs=[row, row], out_specs=pl.BlockSpec((tm, LANES), lambda i: (i, 0)),
        out_shape=jax.ShapeDtypeStruct((t, LANES), F32), compiler_params=_cp("parallel"),
    )(dmerged, merged)


def attn_bwd(q, kv, g, dil, do, lsej, dm, dq_buf, dk_buf, dv_buf, accumulate, hw, name):
    _, b, s, _ = q.shape
    nh = hw // HEAD_DIM
    nblk = s // dil // BLK
    scale = 1.0 / math.sqrt(HEAD_DIM)
    assert dk_buf is not None or not accumulate
    kv_at = 6 + (dq_buf is not None)

    def body(*refs):
        q_ref, k_ref, v_ref, do_ref, lj_ref, dm_ref = refs[:6]
        dq_ref, dk_ref, dv_ref = refs[-3:]
        dki_ref, dvi_ref = (refs[kv_at], refs[kv_at + 1]) if accumulate else (None, None)
        mask2, mask1 = _band_masks()
        mine = lax.broadcasted_iota(jnp.int32, (BLK, LANES), 1) == pl.program_id(1)

        def my_lane(v):
            return jnp.sum(jnp.where(mine, v, 0.0), axis=-1, keepdims=True)

        def put(rs, dk, dv):
            if accumulate:
                dk = dk + dki_ref[rs, :]
                dv = dv + dvi_ref[rs, :]
            dk_ref[rs, :] = dk
            dv_ref[rs, :] = dv

        for r in range(dil):
            kp = vp = hold_k = hold_v = rs_prev = None
            for n in range(nblk):
                rs = _tile_rows(r, n, dil)
                qt = q_ref[rs, :].astype(BF16)
                kc = k_ref[rs, :].astype(BF16)
                vc = v_ref[rs, :].astype(BF16)
                dot = do_ref[rs, :].astype(BF16)
                lm = my_lane(lj_ref[rs, :])
                dmm = my_lane(dm_ref[rs, :])
                if n == 0:
                    kcat, vcat, mask = kc, vc, mask1
                else:
                    kcat, vcat, mask = jnp.concatenate([kp, kc], axis=0), jnp.concatenate([vp, vc], axis=0), mask2
                p = jnp.exp(jnp.where(mask, _dot_nt(qt, kcat) * scale, NEG_INF) - lm)
                ds = (p * (_dot_nt(dot, vcat) - dmm)).astype(BF16)
                dq_ref[rs, :] = _dot(ds, kcat) * scale
                dkc = _dot_tn(ds, qt) * scale
                dvc = _dot_tn(p.astype(BF16), dot)
                if n > 0:
                    put(rs_prev, hold_k + dkc[:BLK, :], hold_v + dvc[:BLK, :])
                    dkc, dvc = dkc[BLK:, :], dvc[BLK:, :]
                hold_k, hold_v, kp, vp, rs_prev = dkc, dvc, kc, vc, rs
            put(rs_prev, hold_k, hold_v)

    col = lambda base: pl.BlockSpec((None, s, HEAD_DIM), lambda bi, h: (bi, 0, base + h))
    any_spec = pl.BlockSpec(memory_space=pl.ANY)
    stat = pl.BlockSpec((None, s, LANES), lambda bi, h: (bi, 0, 0))
    head = lambda base: pl.BlockSpec((None, None, s, HEAD_DIM), lambda bi, h: (base + h, bi, 0, 0))
    in_specs = [head(g * nh), head(g * nh), head((N_GROUPS + g) * nh), col(0), stat, stat]
    args = [q, kv, kv, do, lsej, dm]
    aliases = {}
    if dq_buf is not None:
        in_specs.append(any_spec)
        args.append(dq_buf)
        aliases[6] = 0
    if dk_buf is not None:
        in_specs += [col(g * nh) if accumulate else any_spec] * 2
        args += [dk_buf, dv_buf]
        aliases.update({kv_at: 1, kv_at + 1: 2})
    shape = jax.ShapeDtypeStruct((b, s, N_GROUPS * hw), F32)
    return pl.pallas_call(
        body, name=name, grid=(b, nh), in_specs=in_specs, out_specs=[col(g * nh)] * 3, out_shape=[shape] * 3,
        input_output_aliases=aliases, compiler_params=_cp("parallel", "parallel"),
    )(*args)


def sum_parts(g, recv, me, name, tm=256):
    _, rows, c = g.shape
    n = recv.shape[0]
    tm = _row_tile(rows, tm)

    def body(me_ref, g_ref, r_ref, o_ref):
        acc = g_ref[...].astype(F32)
        for j in range(n):
            acc = acc + r_ref[j].astype(F32)
        o_ref[...] = acc

    return pl.pallas_call(
        body, name=name,
        grid_spec=pltpu.PrefetchScalarGridSpec(
            num_scalar_prefetch=1, grid=(rows // tm,),
            in_specs=[pl.BlockSpec((None, tm, c), lambda i, me_ref: (me_ref[0], i, 0)),
                      pl.BlockSpec((n, tm, c), lambda i, me_ref: (0, i, 0))],
            out_specs=pl.BlockSpec((tm, c), lambda i, me_ref: (i, 0))),
        out_shape=jax.ShapeDtypeStruct((rows, c), F32), compiler_params=_cp("parallel"),
    )(me, g, recv)


def adamw(w, m, v, g_parts, name, tm=256):
    rows, c = w.shape
    tm = _row_tile(rows, tm)
    npart = len(g_parts)

    def body(*refs):
        w_ref, m_ref, v_ref = refs[:3]
        g_refs = refs[3:3 + npart]
        go_ref, d_ref, mo_ref, vo_ref = refs[3 + npart:]
        g = g_refs[0][...]
        for k in range(1, npart):
            g = g + g_refs[k][...]
        mn = ADAM_B1 * m_ref[...] + (1.0 - ADAM_B1) * g
        vn = ADAM_B2 * v_ref[...] + (1.0 - ADAM_B2) * (g * g)
        m_hat = mn / (1.0 - ADAM_B1 ** ADAM_STEP)
        v_hat = vn / (1.0 - ADAM_B2 ** ADAM_STEP)
        go_ref[...] = g
        d_ref[...] = -ADAM_LR * (m_hat / (jnp.sqrt(v_hat) + ADAM_EPS) + ADAM_WD * w_ref[...])
        mo_ref[...] = mn
        vo_ref[...] = vn

    row = pl.BlockSpec((tm, c), lambda i: (i, 0))
    return pl.pallas_call(
        body, name=name, grid=(rows // tm,), in_specs=[row] * (3 + npart), out_specs=[row] * 4,
        out_shape=[jax.ShapeDtypeStruct((rows, c), F32)] * 4, compiler_params=_cp("parallel"),
    )(w, m, v, *g_parts)


def _place():
    return lax.axis_index("x"), lax.axis_index("y"), lax.axis_index("c")


def _other_chips(x, y, c):
    return [(1 - x, y, c), (x, 1 - y, c), (1 - x, 1 - y, c)]


def _chip_of(px, py):
    return 2 * px + py


HBM_SPEC = pl.BlockSpec(memory_space=pltpu.HBM)
SEM_SPEC = pl.BlockSpec(memory_space=pltpu.SEMAPHORE)
ANY_SPEC = pl.BlockSpec(memory_space=pl.ANY)
DATAFLOW = pltpu.SideEffectType.DATAFLOW_SIDE_EFFECTING
N_PEER_CHIPS = N_CHIPS - 1


def _hbm(a):
    return pltpu.with_memory_space_constraint(a, pltpu.HBM)


def _hbm_like(arrays):
    return [pltpu.HBM(a.shape, a.dtype) for a in arrays]


def cast_place(w, layer, me, out_dtype, name, tm=256, nslots=N_CHIPS):
    rows, c = w.shape[-2:]
    tm = _row_tile(rows, tm)

    def body(me_ref, w_ref, o_ref):
        o_ref[...] = w_ref[...].astype(out_dtype)

    if layer is None:
        in_spec = pl.BlockSpec((tm, c), lambda i, me_ref: (i, 0))
    else:
        in_spec = pl.BlockSpec((None, tm, c), lambda i, me_ref: (layer, i, 0))
    return pl.pallas_call(
        body, name=name,
        grid_spec=pltpu.PrefetchScalarGridSpec(
            num_scalar_prefetch=1, grid=(rows // tm,), in_specs=[in_spec],
            out_specs=pl.BlockSpec((None, tm, c), lambda i, me_ref: (me_ref[0], i, 0))),
        out_shape=jax.ShapeDtypeStruct((nslots, rows, c), out_dtype), compiler_params=_cp("parallel"),
    )(me, w)


def gather_start(lands, chunk_sizes, name="gather_start"):
    n = len(lands)
    nch = len(chunk_sizes)
    assert sum(chunk_sizes) == n

    def body(*refs):
        land_refs = refs[:n]
        outs = refs[n:]
        send_sems, recv_sems = outs[:nch], outs[nch:2 * nch]
        token = outs[-1]
        x, y, c = _place()
        me = _chip_of(x, y)
        peers = _other_chips(x, y, c)
        k = 0
        for ck, size in enumerate(chunk_sizes):
            for pos in range(size):
                for r, peer in enumerate(peers):
                    pltpu.make_async_remote_copy(
                        src_ref=land_refs[k].at[me], dst_ref=land_refs[k].at[me],
                        send_sem=send_sems[ck].at[N_PEER_CHIPS * pos + r], recv_sem=recv_sems[ck].at[N_PEER_CHIPS * pos + r],
                        device_id=peer, device_id_type=MESH).start()
                k += 1
        token[...] = jnp.zeros(token.shape, F32)

    sems = [pltpu.SemaphoreType.DMA((N_PEER_CHIPS * s,)) for s in chunk_sizes]
    res = pl.pallas_call(
        body, name=name,
        out_shape=(*sems, *sems, *_hbm_like(lands), jax.ShapeDtypeStruct(DEP_SPEC_SHAPE, F32)),
        in_specs=[HBM_SPEC] * n,
        out_specs=(*[SEM_SPEC] * (2 * nch), *[HBM_SPEC] * n, pl.BlockSpec(memory_space=pltpu.VMEM)),
        input_output_aliases={k: 2 * nch + k for k in range(n)},
        compiler_params=pltpu.CompilerParams(has_side_effects=DATAFLOW),
    )(*[_hbm(a) for a in lands])
    return res[:nch], res[nch:2 * nch], res[2 * nch:2 * nch + n], res[-1]


def gather_wait(send_sem, recv_sem, lands, after, name):
    n = len(lands)

    def body(*refs):
        land_refs = refs[:n]
        ssem, rsem = refs[n], refs[n + 1]
        x, y, c = _place()
        me = _chip_of(x, y)
        for pos in range(n):
            for r, peer in enumerate(_other_chips(x, y, c)):
                cp = pltpu.make_async_remote_copy(
                    src_ref=land_refs[pos].at[me], dst_ref=land_refs[pos].at[_chip_of(peer[0], peer[1])],
                    send_sem=ssem.at[N_PEER_CHIPS * pos + r], recv_sem=rsem.at[N_PEER_CHIPS * pos + r],
                    device_id=peer, device_id_type=MESH)
                cp.wait_send()
                cp.wait_recv()

    return pl.pallas_call(
        body, name=name, out_shape=tuple(_hbm_like(lands)),
        in_specs=[*[HBM_SPEC] * n, SEM_SPEC, SEM_SPEC, ANY_SPEC], out_specs=[HBM_SPEC] * n,
        input_output_aliases={k: k for k in range(n)},
        compiler_params=pltpu.CompilerParams(has_side_effects=DATAFLOW),
    )(*lands, send_sem, recv_sem, after)


def scatter_start(grads, name):
    n = len(grads)
    recvs = [lax.empty((N_PEER_CHIPS, *g.shape[1:]), g.dtype) for g in grads]

    def body(*refs):
        g_refs, r_refs = refs[:n], refs[n:2 * n]
        send_sems, recv_sems = refs[2 * n], refs[2 * n + 1]
        token = refs[-1]
        x, y, c = _place()
        for k in range(n):
            for r, peer in enumerate(_other_chips(x, y, c)):
                pltpu.make_async_remote_copy(
                    src_ref=g_refs[k].at[_chip_of(peer[0], peer[1])], dst_ref=r_refs[k].at[r],
                    send_sem=send_sems.at[N_PEER_CHIPS * k + r], recv_sem=recv_sems.at[N_PEER_CHIPS * k + r],
                    device_id=peer, device_id_type=MESH).start()
        token[...] = jnp.zeros(token.shape, F32)

    sem = pltpu.SemaphoreType.DMA((N_PEER_CHIPS * n,))
    res = pl.pallas_call(
        body, name=name,
        out_shape=(sem, sem, *_hbm_like(grads), *_hbm_like(recvs), jax.ShapeDtypeStruct(DEP_SPEC_SHAPE, F32)),
        in_specs=[HBM_SPEC] * (2 * n),
        out_specs=(SEM_SPEC, SEM_SPEC, *[HBM_SPEC] * (2 * n), pl.BlockSpec(memory_space=pltpu.VMEM)),
        input_output_aliases={k: 2 + k for k in range(2 * n)},
        compiler_params=pltpu.CompilerParams(has_side_effects=DATAFLOW),
    )(*[_hbm(a) for a in grads], *[_hbm(a) for a in recvs])
    return res[0], res[1], res[2:2 + n], res[2 + n:2 + 2 * n], res[-1]


def scatter_wait(send_sem, recv_sem, grads, recvs, after, name):
    n = len(grads)

    def body(*refs):
        g_refs, r_refs = refs[:n], refs[n:2 * n]
        ssem, rsem = refs[2 * n], refs[2 * n + 1]
        x, y, c = _place()
        for k in range(n):
            for r, peer in enumerate(_other_chips(x, y, c)):
                cp = pltpu.make_async_remote_copy(
                    src_ref=g_refs[k].at[_chip_of(peer[0], peer[1])], dst_ref=r_refs[k].at[r],
                    send_sem=ssem.at[N_PEER_CHIPS * k + r], recv_sem=rsem.at[N_PEER_CHIPS * k + r],
                    device_id=peer, device_id_type=MESH)
                cp.wait_send()
                cp.wait_recv()

    res = pl.pallas_call(
        body, name=name, out_shape=(*_hbm_like(grads), *_hbm_like(recvs)),
        in_specs=[*[HBM_SPEC] * (2 * n), SEM_SPEC, SEM_SPEC, ANY_SPEC], out_specs=[HBM_SPEC] * (2 * n),
        input_output_aliases={k: k for k in range(2 * n)},
        compiler_params=pltpu.CompilerParams(has_side_effects=DATAFLOW),
    )(*grads, *recvs, send_sem, recv_sem, after)
    return res[:n], res[n:]


def swap_start(parts, name):
    n = len(parts)
    lands = [lax.empty(p.shape, p.dtype) for p in parts]

    def body(*refs):
        p_refs, l_refs = refs[:n], refs[n:2 * n]
        sems = refs[2 * n:4 * n]
        token = refs[-1]
        x, y, c = _place()
        for k in range(n):
            pltpu.make_async_remote_copy(
                src_ref=p_refs[k], dst_ref=l_refs[k], send_sem=sems[k], recv_sem=sems[n + k],
                device_id=(x, y, 1 - c), device_id_type=MESH).start()
        token[...] = jnp.zeros(token.shape, F32)

    sem = pltpu.SemaphoreType.DMA(())
    res = pl.pallas_call(
        body, name=name,
        out_shape=(*[sem] * (2 * n), *_hbm_like(parts), *_hbm_like(lands), jax.ShapeDtypeStruct(DEP_SPEC_SHAPE, F32)),
        in_specs=[HBM_SPEC] * (2 * n),
        out_specs=(*[SEM_SPEC] * (2 * n), *[HBM_SPEC] * (2 * n), pl.BlockSpec(memory_space=pltpu.VMEM)),
        input_output_aliases={k: 2 * n + k for k in range(2 * n)},
        compiler_params=pltpu.CompilerParams(has_side_effects=DATAFLOW),
    )(*[_hbm(a) for a in parts], *[_hbm(a) for a in lands])
    return res[:n], res[n:2 * n], res[2 * n:3 * n], res[3 * n:4 * n], res[-1]


def swap_wait(send_sem, recv_sem, part, land, after, name):
    def body(p_ref, l_ref, ssem, rsem, after_ref, p_out, l_out):
        x, y, c = _place()
        cp = pltpu.make_async_remote_copy(src_ref=p_ref, dst_ref=l_ref, send_sem=ssem, recv_sem=rsem,
                                          device_id=(x, y, 1 - c), device_id_type=MESH)
        cp.wait_send()
        cp.wait_recv()

    return pl.pallas_call(
        body, name=name, out_shape=tuple(_hbm_like([part, land])),
        in_specs=[HBM_SPEC, HBM_SPEC, SEM_SPEC, SEM_SPEC, ANY_SPEC], out_specs=[HBM_SPEC, HBM_SPEC],
        input_output_aliases={0: 0, 1: 1},
        compiler_params=pltpu.CompilerParams(has_side_effects=DATAFLOW),
    )(part, land, send_sem, recv_sem, after)


def _xor_peer(x, y, c, k):
    px, py, pc = x ^ ((k >> 2) & 1), y ^ ((k >> 1) & 1), c ^ (k & 1)
    return (px, py, pc), 4 * px + 2 * py + pc


def small_start(land, name="small_start"):
    def body(l_ref, ssem, rsem, l_out, token):
        x, y, c = _place()
        me = 4 * x + 2 * y + c
        for k in range(1, N_DEV):
            peer, _ = _xor_peer(x, y, c, k)
            pltpu.make_async_remote_copy(
                src_ref=l_ref.at[me], dst_ref=l_ref.at[me], send_sem=ssem.at[k - 1], recv_sem=rsem.at[k - 1],
                device_id=peer, device_id_type=MESH).start()
        token[...] = jnp.zeros(token.shape, F32)

    sem = pltpu.SemaphoreType.DMA((N_DEV - 1,))
    return pl.pallas_call(
        body, name=name,
        out_shape=(sem, sem, pltpu.HBM(land.shape, land.dtype), jax.ShapeDtypeStruct(DEP_SPEC_SHAPE, F32)),
        in_specs=[HBM_SPEC], out_specs=(SEM_SPEC, SEM_SPEC, HBM_SPEC, pl.BlockSpec(memory_space=pltpu.VMEM)),
        input_output_aliases={0: 2}, compiler_params=pltpu.CompilerParams(has_side_effects=DATAFLOW),
    )(_hbm(land))


def small_wait(send_sem, recv_sem, land, after, name="small_wait"):
    def body(l_ref, ssem, rsem, after_ref, l_out):
        x, y, c = _place()
        me = 4 * x + 2 * y + c
        for k in range(1, N_DEV):
            peer, slot = _xor_peer(x, y, c, k)
            cp = pltpu.make_async_remote_copy(
                src_ref=l_ref.at[me], dst_ref=l_ref.at[slot], send_sem=ssem.at[k - 1], recv_sem=rsem.at[k - 1],
                device_id=peer, device_id_type=MESH)
            cp.wait_send()
            cp.wait_recv()

    return pl.pallas_call(
        body, name=name, out_shape=pltpu.HBM(land.shape, land.dtype),
        in_specs=[HBM_SPEC, SEM_SPEC, SEM_SPEC, ANY_SPEC], out_specs=HBM_SPEC, input_output_aliases={0: 0},
        compiler_params=pltpu.CompilerParams(has_side_effects=DATAFLOW),
    )(land, send_sem, recv_sem, after)


def sum_slots(land, name="sum_slots", tm=256):
    n, rows, c = land.shape
    tm = _row_tile(rows, tm)

    def body(l_ref, o_ref):
        acc = l_ref[0]
        for j in range(1, n):
            acc = acc + l_ref[j]
        o_ref[...] = acc

    return pl.pallas_call(
        body, name=name, grid=(rows // tm,), in_specs=[pl.BlockSpec((n, tm, c), lambda i: (0, i, 0))],
        out_specs=pl.BlockSpec((tm, c), lambda i: (i, 0)), out_shape=jax.ShapeDtypeStruct((rows, c), F32),
        compiler_params=_cp("parallel"),
    )(land)


PACK_ROW_TILE = 256


def _pack(arrays):
    flat = jnp.concatenate([a.reshape(-1).astype(F32) for a in arrays])
    n = flat.shape[0]
    rows = -(-n // LANES)
    rows = -(-rows // PACK_ROW_TILE) * PACK_ROW_TILE
    return jnp.pad(flat, (0, rows * LANES - n)).reshape(rows, LANES)


def _unpack(packed, shapes, lead=()):
    flat = packed.reshape(*lead, -1)
    out, off = [], 0
    for shp in shapes:
        n = math.prod(shp)
        out.append(flat[..., off:off + n].reshape(*lead, *shp))
        off += n
    return out


def _row(vec):
    return vec.reshape(1, -1)


def kernel(x, mix_pre_g, mix_post_g, ffn_pre_g, ffn_post_g, cm_w_in, cm_b_in, cm_dw, cm_dw_b, cm_ln_g, cm_ln_b, cm_w_out, cm_b_out, kv_norm_g, w_kv, w_q, w_o, ffn_w_in, ffn_dw, ffn_dw_b, ffn_w_out, loss_target, m_mix_pre_g, m_mix_post_g, m_ffn_pre_g, m_ffn_post_g, m_cm_w_in, m_cm_b_in, m_cm_dw, m_cm_dw_b, m_cm_ln_g, m_cm_ln_b, m_cm_w_out, m_cm_b_out, m_kv_norm_g, m_w_kv, m_w_q, m_w_o, m_ffn_w_in, m_ffn_dw, m_ffn_dw_b, m_ffn_w_out, v_mix_pre_g, v_mix_post_g, v_ffn_pre_g, v_ffn_post_g, v_cm_w_in, v_cm_b_in, v_cm_dw, v_cm_dw_b, v_cm_ln_g, v_cm_ln_b, v_cm_w_out, v_cm_b_out, v_kv_norm_g, v_w_kv, v_w_q, v_w_o, v_ffn_w_in, v_ffn_dw, v_ffn_dw_b, v_ffn_w_out):
    names = ["mix_pre_g", "mix_post_g", "ffn_pre_g", "ffn_post_g", "cm_w_in", "cm_b_in", "cm_dw", "cm_dw_b", "cm_ln_g",
             "cm_ln_b", "cm_w_out", "cm_b_out", "kv_norm_g", "w_kv", "w_q", "w_o", "ffn_w_in", "ffn_dw", "ffn_dw_b",
             "ffn_w_out"]
    w_in = dict(zip(names, [mix_pre_g, mix_post_g, ffn_pre_g, ffn_post_g, cm_w_in, cm_b_in, cm_dw, cm_dw_b, cm_ln_g,
                            cm_ln_b, cm_w_out, cm_b_out, kv_norm_g, w_kv, w_q, w_o, ffn_w_in, ffn_dw, ffn_dw_b, ffn_w_out]))
    m_in = dict(zip(names, [m_mix_pre_g, m_mix_post_g, m_ffn_pre_g, m_ffn_post_g, m_cm_w_in, m_cm_b_in, m_cm_dw, m_cm_dw_b,
                            m_cm_ln_g, m_cm_ln_b, m_cm_w_out, m_cm_b_out, m_kv_norm_g, m_w_kv, m_w_q, m_w_o, m_ffn_w_in,
                            m_ffn_dw, m_ffn_dw_b, m_ffn_w_out]))
    v_in = dict(zip(names, [v_mix_pre_g, v_mix_post_g, v_ffn_pre_g, v_ffn_post_g, v_cm_w_in, v_cm_b_in, v_cm_dw, v_cm_dw_b,
                            v_cm_ln_g, v_cm_ln_b, v_cm_w_out, v_cm_b_out, v_kv_norm_g, v_w_kv, v_w_q, v_w_o, v_ffn_w_in,
                            v_ffn_dw, v_ffn_dw_b, v_ffn_w_out]))

    bsz, seq, d = x.shape
    t = bsz * seq
    n_b = DEPTH - N_A
    hw = w_o.shape[-1]
    qw = N_GROUPS * hw
    f2 = ffn_dw_b.shape[-1]
    f = f2 // 2
    me_chip = _chip_of(lax.axis_index("x"), lax.axis_index("y"))

    big = ["cm_w_in", "cm_w_out", "w_kv", "w_q", "w_o", "ffn_w_in", "ffn_w_out"]
    row_sharded = ("cm_w_out", "w_o", "ffn_w_out")
    small_sharded = ["cm_b_in", "cm_dw", "cm_dw_b", "cm_ln_g", "cm_ln_b", "cm_b_out", "ffn_dw"]
    small_pack = _pack([w_in[n] for n in small_sharded])
    chunks = [
        [("cm_w_in", 0), ("small", None)],
        [("cm_w_out", 0)],
        [("ffn_w_in", 0), ("ffn_w_out", 0)],
        [("cm_w_in", 1), ("cm_w_out", 1)],
        [("ffn_w_in", 1), ("ffn_w_out", 1)],
        [("w_kv", None)],
        [("w_q", 0), ("w_o", 0)],
        [("ffn_w_in", 2), ("ffn_w_out", 2)],
        [("w_q", 1), ("w_o", 1)],
        [("ffn_w_in", 3), ("ffn_w_out", 3)],
    ]
    pieces = [pc for ch in chunks for pc in ch]
    chunk_of = {pc: ck for ck, ch in enumerate(chunks) for pc in ch}

    me_arr = me_chip.astype(jnp.int32).reshape(1)

    def land_of(pc):
        n, l = pc
        if n == "small":
            return cast_place(small_pack, None, me_arr, F32, name="place_small")
        return cast_place(w_in[n], l, me_arr, BF16, name=f"place_{n}_{l}")

    lands = [land_of(pc) for pc in pieces]
    g_send, g_recv, lands_f, token = gather_start(lands, [len(ch) for ch in chunks])
    weights = {}

    def finish_chunk(ck, after):
        lo = sum(len(ch) for ch in chunks[:ck])
        hi = lo + len(chunks[ck])
        got = gather_wait(g_send[ck], g_recv[ck], lands_f[lo:hi], after, name=f"gather_wait{ck}")
        for pc, arr in zip(chunks[ck], got):
            weights[pc] = arr.reshape(1, -1, arr.shape[-1]) if pc[0] in row_sharded else arr

    def wmat(n, l=None, after=None):
        if (n, l) not in weights:
            finish_chunk(chunk_of[(n, l)], after)
        arr = weights[(n, l)]
        return arr, arr.shape[0]

    finish_chunk(0, token)
    small_full = {}
    for n, arr4 in zip(small_sharded, _unpack(weights[("small", None)], [w_in[n].shape for n in small_sharded], lead=(N_CHIPS,))):
        shp = w_in[n].shape
        small_full[n] = jnp.moveaxis(arr4, 0, -2).reshape(*shp[:-1], N_CHIPS * shp[-1])

    x2d = x.reshape(t, d)
    saved = []
    (h1,) = resid_norm_fwd(x2d, None, None, [_row(mix_pre_g[0])], name="norm_in", dep=token)
    xcur = x2d
    kv_state = None
    for i in range(DEPTH):
        sv = {"x_in": xcur, "h1": h1}
        if i < N_A:
            z = mm_nn(h1, *wmat("cm_w_in", i, h1), 1, 0, bias=_row(small_full["cm_b_in"][i]), name=f"cm_in{i}")
            u2 = glu_conv_fwd(z.reshape(bsz, seq, 2 * d), small_full["cm_dw"][i], _row(small_full["cm_dw_b"][i]),
                              name=f"glu_conv{i}").reshape(t, d)
            u4 = ln_silu_fwd(u2, _row(small_full["cm_ln_g"][i]), _row(small_full["cm_ln_b"][i]), name=f"ln_silu{i}")
            y = mm_nn(u4, *wmat("cm_w_out", i, u4), 1, 0, bias=_row(small_full["cm_b_out"][i]), name=f"cm_out{i}")
            sv.update(z=z, u2=u2, u4=u4)
        else:
            j = i - N_A
            q = mm_nn(h1, *wmat("w_q", j, h1), 1, 0, name=f"q_proj{j}", head_major=True).reshape(-1, bsz, seq, HEAD_DIM)
            outs, lses = [], []
            for g, dil in enumerate(DILATIONS):
                o_g, l_g = attn_fwd(q, kv_state["kv"], g, dil, hw, name=f"attn_fwd{j}_{g}")
                outs.append(o_g.reshape(t, hw))
                lses.append(l_g.reshape(t, LANES))
            merged, lsej = attn_merge(outs, lses, name=f"attn_merge{j}")
            y = mm_nn(merged, *wmat("w_o", j, merged), 1, 0, name=f"o_proj{j}")
            sv.update(q=q, merged=merged, lsej=lsej)
        x1, h2 = resid_norm_fwd(xcur, y, _row(mix_post_g[i]), [_row(ffn_pre_g[i])], name=f"resid_mix{i}")
        p = mm_nn(h2, *wmat("ffn_w_in", i, h2), 1, 0, out_dtype=BF16, name=f"ffn_in{i}")
        s_act = ffn_mid_fwd(p.reshape(bsz, seq, f2), small_full["ffn_dw"][i], _row(ffn_dw_b[i]), name=f"ffn_mid{i}").reshape(t, f)
        y2 = mm_nn(s_act, *wmat("ffn_w_out", i), 1, 0, name=f"ffn_out{i}")
        next_gains = []
        if i + 1 < DEPTH:
            next_gains.append(_row(mix_pre_g[i + 1]))
        if i == N_A - 1:
            next_gains.append(_row(kv_norm_g))
        res = resid_norm_fwd(x1, y2, _row(ffn_post_g[i]), next_gains, name=f"resid_ffn{i}")
        sv.update(y=y, x1=x1, h2=h2, p=p, s=s_act, y2=y2)
        saved.append(sv)
        xcur = res[0]
        if i + 1 < DEPTH:
            h1 = res[1]
        if i == N_A - 1:
            kvn = res[2]
            kv = mm_nn(kvn, *wmat("w_kv", None, kvn), 1, 0, name="kv_proj", head_major=True).reshape(-1, bsz, seq, HEAD_DIM)
            kv_state = {"kv": kv, "kvn": kvn, "x_a": xcur}

    dx, loss_tile = loss_fwd_bwd(xcur, loss_target.reshape(t, d))
    loss = lax.psum(loss_tile[0, 0], ("x", "y", "c"))

    gsm = {n: [None] * w_in[n].shape[0] for n in
           ["mix_pre_g", "mix_post_g", "ffn_pre_g", "ffn_post_g", "cm_b_in", "cm_dw", "cm_dw_b", "cm_ln_g", "cm_ln_b",
            "cm_b_out", "ffn_dw", "ffn_dw_b"]}
    gbig = {}
    in_flight = []
    dep = None

    def start_scatter(pcs, tag):
        ssem, rsem, g_f, r_f, tok = scatter_start([gbig[pc] for pc in pcs], name=f"scatter_start_{tag}")
        in_flight.append((pcs, ssem, rsem, g_f, r_f))
        return tok

    dk_buf = dv_buf = None
    for i in range(DEPTH - 1, -1, -1):
        sv = saved[i]
        dy2, dg, _ = norm_bwd(sv["y2"], _row(ffn_post_g[i]), dx, out_dtype=BF16, name=f"bwd_ffn_post{i}", dep=dep)
        gsm["ffn_post_g"][i] = dg
        ds = mm_nt(dy2, *wmat("ffn_w_out", i), 1, 0, out_dtype=BF16, name=f"bwd_ffn_out_dx{i}")
        gbig[("ffn_w_out", i)] = mm_tn(sv["s"], dy2, 1, name=f"bwd_ffn_out_dw{i}").reshape(N_CHIPS, f // N_CHIPS, d)
        dpa, dpg, ddwa, ddwg, ddba, ddbg = ffn_mid_bwd(sv["p"].reshape(bsz, seq, f2), small_full["ffn_dw"][i], _row(ffn_dw_b[i]),
                                                       ds.reshape(bsz, seq, f), name=f"bwd_ffn_mid{i}")
        gsm["ffn_dw"][i] = jnp.concatenate([jnp.sum(ddwa, axis=1), jnp.sum(ddwg, axis=1)], axis=-1)
        gsm["ffn_dw_b"][i] = jnp.concatenate([ddba, ddbg], axis=-1)
        dp = [dpa.reshape(t, f), dpg.reshape(t, f)]
        dh2 = mm_nt(dp, *wmat("ffn_w_in", i), 1, 0, name=f"bwd_ffn_in_dx{i}")
        gbig[("ffn_w_in", i)] = mm_tn(sv["h2"], dp, N_CHIPS, name=f"bwd_ffn_in_dw{i}")
        dx1, dg, _ = norm_bwd(sv["x1"], _row(ffn_pre_g[i]), dh2, add=dx, name=f"bwd_ffn_pre{i}")
        gsm["ffn_pre_g"][i] = dg
        dep = start_scatter([("ffn_w_in", 0), ("ffn_w_out", 0)], "ffn0") if i == 0 else None
        dy, dg, dbias = norm_bwd(sv["y"], _row(mix_post_g[i]), dx1, out_dtype=BF16, name=f"bwd_mix_post{i}", dep=dep)
        gsm["mix_post_g"][i] = dg
        if i < N_A:
            gsm["cm_b_out"][i] = dbias
            du4 = mm_nt(dy, *wmat("cm_w_out", i), 1, 0, name=f"bwd_cm_out_dx{i}")
            gbig[("cm_w_out", i)] = mm_tn(sv["u4"], dy, 1, name=f"bwd_cm_out_dw{i}").reshape(N_CHIPS, d // N_CHIPS, d)
            du2, dlg, dlb = ln_silu_bwd(sv["u2"], _row(small_full["cm_ln_g"][i]), _row(small_full["cm_ln_b"][i]), du4,
                                        name=f"bwd_ln_silu{i}")
            gsm["cm_ln_g"][i], gsm["cm_ln_b"][i] = dlg, dlb
            dza, dzg, ddw, ddwb, dba, dbg = glu_conv_bwd(sv["z"].reshape(bsz, seq, 2 * d), small_full["cm_dw"][i],
                                                         du2.reshape(bsz, seq, d), name=f"bwd_glu_conv{i}")
            gsm["cm_dw"][i] = jnp.sum(ddw, axis=1)
            gsm["cm_dw_b"][i] = ddwb
            gsm["cm_b_in"][i] = jnp.concatenate([dba, dbg], axis=-1)
            dz = [dza.reshape(t, d), dzg.reshape(t, d)]
            dh1 = mm_nt(dz, *wmat("cm_w_in", i), 1, 0, name=f"bwd_cm_in_dx{i}")
            gbig[("cm_w_in", i)] = mm_tn(sv["h1"], dz, N_CHIPS, name=f"bwd_cm_in_dw{i}")
        else:
            j = i - N_A
            dmerged = mm_nt(dy, *wmat("w_o", j), 1, 0, name=f"bwd_o_proj_dx{j}")
            gbig[("w_o", j)] = mm_tn(sv["merged"], dy, 1, name=f"bwd_o_proj_dw{j}").reshape(N_CHIPS, hw // N_CHIPS, d)
            dmt = attn_bwd_prep(dmerged, sv["merged"], name=f"bwd_attn_prep{j}")
            dq_buf = None
            add_to_kv = dk_buf is not None
            for g, dil in enumerate(DILATIONS):
                dq_buf, dk_buf, dv_buf = attn_bwd(
                    sv["q"], kv_state["kv"], g, dil, dmerged.reshape(bsz, seq, hw), sv["lsej"].reshape(bsz, seq, LANES),
                    dmt.reshape(bsz, seq, LANES), dq_buf, dk_buf, dv_buf, add_to_kv, hw, name=f"attn_bwd{j}_{g}")
            dq = dq_buf.reshape(t, qw)
            dh1 = mm_nt(dq, *wmat("w_q", j), 1, 0, name=f"bwd_q_proj_dx{j}")
            gbig[("w_q", j)] = mm_tn(sv["h1"], dq, N_CHIPS, name=f"bwd_q_proj_dw{j}")
        dx, dg, _ = norm_bwd(sv["x_in"], _row(mix_pre_g[i]), dh1, add=dx1, name=f"bwd_mix_pre{i}")
        gsm["mix_pre_g"][i] = dg
        if i > N_A:
            dep = start_scatter([("ffn_w_in", i), ("ffn_w_out", i), ("w_q", i - N_A), ("w_o", i - N_A)], f"l{i}")
        elif 0 < i < N_A:
            dep = start_scatter([("ffn_w_in", i), ("ffn_w_out", i), ("cm_w_in", i), ("cm_w_out", i)], f"l{i}")
        elif i == 0:
            last_token = start_scatter([("cm_w_in", 0), ("cm_w_out", 0)], "cm0")
        if i == N_A:
            dkv = [dk_buf.reshape(t, qw), dv_buf.reshape(t, qw)]
            dkvn = mm_nt(dkv, *wmat("w_kv"), 1, 0, name="bwd_kv_proj_dx")
            gbig[("w_kv", None)] = mm_tn(kv_state["kvn"], dkv, N_CHIPS, name="bwd_kv_proj_dw")
            dx, dg_kv, _ = norm_bwd(kv_state["x_a"], _row(kv_norm_g), dkvn, add=dx, name="bwd_kv_norm")
            dep = start_scatter([("ffn_w_in", i), ("ffn_w_out", i), ("w_q", 0), ("w_o", 0), ("w_kv", None)], f"l{i}")
    grad_x = dx.reshape(bsz, seq, d)

    plane_of = {}
    outs_g, outs_d, outs_m, outs_v = {}, {}, {}, {}

    def finish_scatter(k, after):
        pcs, ssem, rsem, g_f, r_f = in_flight[k]
        g_done, r_done = scatter_wait(ssem, rsem, g_f, r_f, after, name=f"scatter_wait{k}")
        for pc, g_arr, r_arr in zip(pcs, g_done, r_done):
            plane_of[pc] = sum_parts(g_arr, r_arr, me_arr, name=f"sum_chips_{pc[0]}_{pc[1]}")

    def update(group, tag, after):
        plane = []
        for n in group:
            if w_in[n].ndim == 2:
                plane.append(plane_of[(n, None)])
            else:
                plane.append(jnp.concatenate([plane_of[(n, l)] for l in range(w_in[n].shape[0])], axis=0))
        ssems, rsems, plane_f, land_f, _ = swap_start(plane, name=f"swap_start_{tag}")
        for k, n in enumerate(group):
            p_mine, p_other = swap_wait(ssems[k], rsems[k], plane_f[k], land_f[k], after, name=f"swap_wait_{n}")
            shp = w_in[n].shape
            flat = lambda a: a.reshape(-1, shp[-1])
            g_, d_, m_, v_ = adamw(flat(w_in[n]), flat(m_in[n]), flat(v_in[n]), [p_mine, p_other], name=f"adamw_{n}")
            outs_g[n], outs_d[n], outs_m[n], outs_v[n] = (a.reshape(shp) for a in (g_, d_, m_, v_))
            after = v_
        return after

    small_names = [n for n in names if n not in big]
    small_shapes_full = {}
    small_grads_full = []
    for n in small_names:
        if n == "kv_norm_g":
            gfull = dg_kv.reshape(-1)
        elif n in ("cm_dw", "ffn_dw"):
            gfull = jnp.stack(gsm[n], axis=0)
        else:
            gfull = jnp.stack([a.reshape(-1) for a in gsm[n]], axis=0)
        small_shapes_full[n] = gfull.shape
        small_grads_full.append(gfull)
    dev_arr = (4 * lax.axis_index("x") + 2 * lax.axis_index("y") + lax.axis_index("c")).astype(jnp.int32).reshape(1)
    small_land = cast_place(_pack(small_grads_full) + last_token[0, 0], None, dev_arr, F32, name="place_small_grads",
                            nslots=N_DEV)
    sm_send, sm_recv, small_land, small_token = small_start(small_land)

    for k in range(len(in_flight) - 1):
        finish_scatter(k, small_token)
    done = update(["w_kv", "w_q", "w_o", "ffn_w_in", "ffn_w_out"], "a", small_token)
    finish_scatter(len(in_flight) - 1, done)
    done = update(["cm_w_in", "cm_w_out"], "b", done)

    summed = sum_slots(small_wait(sm_send, sm_recv, small_land, done))
    g_full = dict(zip(small_names, _unpack(summed, [small_shapes_full[n] for n in small_names])))
    g_loc = {}
    for n in small_names:
        if n in small_sharded:
            width = w_in[n].shape[-1]
            g_loc[n] = lax.dynamic_slice_in_dim(g_full[n], me_chip * width, width, axis=g_full[n].ndim - 1)
        else:
            g_loc[n] = g_full[n]
    res = adamw(_pack([w_in[n] for n in small_names]), _pack([m_in[n] for n in small_names]),
                _pack([v_in[n] for n in small_names]), [_pack([g_loc[n] for n in small_names])], name="adamw_small")
    shapes_loc = [w_in[n].shape for n in small_names]
    for dst, packed in zip((outs_g, outs_d, outs_m, outs_v), res):
        for n, a in zip(small_names, _unpack(packed, shapes_loc)):
            dst[n] = a

    return (loss, grad_x, *[outs_g[n] for n in names], *[outs_d[n] for n in names],
            *[outs_m[n] for n in names], *[outs_v[n] for n in names])
```

```python
import functools
import math

import jax
import jax.numpy as jnp
from jax import lax
from jax.experimental import pallas as pl
from jax.experimental.pallas import tpu as pltpu

F32 = jnp.float32
BF16 = jnp.bfloat16
EPS = 1e-6
NEG_INF = -1e30
N_A = 2
DEPTH = 4
N_GROUPS = 3
DILATIONS = (1, 4, 16)
HEAD_DIM = 128
BLK = 128
LANES = 128
N_CHIPS = 4
N_DEV = 8
VMEM_LIMIT_V7X = 56 * 1024 * 1024

ADAM_LR = 0.001
ADAM_B1 = 0.9
ADAM_B2 = 0.999
ADAM_EPS = 1e-08
ADAM_WD = 0.01
ADAM_STEP = 10

MESH = pl.DeviceIdType.MESH


def _cp(*sem, **kw):
    return pltpu.CompilerParams(dimension_semantics=sem if sem else None, vmem_limit_bytes=VMEM_LIMIT_V7X, **kw)


def _dot(a, b):
    return jnp.dot(a, b, preferred_element_type=F32)


def _dot_nt(a, b):
    return lax.dot_general(a, b, (((1,), (1,)), ((), ())), preferred_element_type=F32)


def _dot_tn(a, b):
    return lax.dot_general(a, b, (((0,), (0,)), ((), ())), preferred_element_type=F32)


def _sigmoid(x):
    return 1.0 / (1.0 + jnp.exp(-x))


def _row_tile(n, want):
    if n <= want:
        return n
    for t in range(want - want % 8, 7, -8):
        if n % t == 0:
            return t
    raise ValueError(f"no row tile for {n} rows")


def mm_nn(a, w, nsh, stride, layer, bias=None, out_dtype=F32, name="mm_nn", tm=1024, head_major=False):
    m, k = a.shape
    _, k2, ns = w.shape
    assert k == k2
    tm = _row_tile(m, tm)
    has_bias = bias is not None
    hps = ns // HEAD_DIM

    def body(*refs):
        if has_bias:
            a_ref, w_ref, b_ref, o_ref = refs
        else:
            a_ref, w_ref, o_ref = refs
        acc = _dot(a_ref[...].astype(BF16), w_ref[...])
        if has_bias:
            acc = acc + b_ref[...]
        if head_major:
            for hh in range(hps):
                o_ref[hh] = acc[:, hh * HEAD_DIM:(hh + 1) * HEAD_DIM].astype(out_dtype)
        else:
            o_ref[...] = acc.astype(out_dtype)

    in_specs = [
        pl.BlockSpec((tm, k), lambda j, i: (i, 0)),
        pl.BlockSpec((None, k, ns), lambda j, i: (j * stride + layer, 0, 0)),
    ]
    args = [a, w]
    if has_bias:
        in_specs.append(pl.BlockSpec((1, ns), lambda j, i: (0, j)))
        args.append(bias)
    return pl.pallas_call(
        body,
        name=name,
        grid=(nsh, m // tm),
        in_specs=in_specs,
        out_specs=(pl.BlockSpec((hps, tm, HEAD_DIM), lambda j, i: (j, i, 0)) if head_major
                   else pl.BlockSpec((tm, ns), lambda j, i: (i, j))),
        out_shape=jax.ShapeDtypeStruct((nsh * hps, m, HEAD_DIM) if head_major else (m, nsh * ns), out_dtype),
        compiler_params=_cp("parallel", "parallel"),
    )(*args)


def mm_nt(dy, w, nsh, stride, layer, out_dtype=F32, name="mm_nt", tm=1024):
    dys = list(dy) if isinstance(dy, (list, tuple)) else [dy]
    npart = len(dys)
    per = nsh // npart
    m = dys[0].shape[0]
    _, k, ns = w.shape
    assert all(d.shape == (m, per * ns) for d in dys)
    tm = _row_tile(m, tm)

    def body(*refs):
        dy_refs = refs[:npart]
        w_ref, o_ref, acc_ref = refs[npart:]
        j = pl.program_id(1)

        @pl.when(j == 0)
        def _():
            acc_ref[...] = jnp.zeros(acc_ref.shape, F32)

        for pi in range(npart):
            @pl.when(j // per == pi)
            def _(pi=pi):
                acc_ref[...] += _dot_nt(dy_refs[pi][...].astype(BF16), w_ref[...])

        @pl.when(j == nsh - 1)
        def _():
            o_ref[...] = acc_ref[...].astype(out_dtype)

    dy_specs = [pl.BlockSpec((tm, ns), lambda i, j, pi=pi: (i, jnp.clip(j - pi * per, 0, per - 1))) for pi in range(npart)]
    return pl.pallas_call(
        body,
        name=name,
        grid=(m // tm, nsh),
        in_specs=[*dy_specs, pl.BlockSpec((None, k, ns), lambda i, j: (j * stride + layer, 0, 0))],
        out_specs=pl.BlockSpec((tm, k), lambda i, j: (i, 0)),
        out_shape=jax.ShapeDtypeStruct((m, k), out_dtype),
        scratch_shapes=[pltpu.VMEM((tm, k), F32)],
        compiler_params=_cp("parallel", "arbitrary"),
    )(*dys, w)


def mm_tn(a, dy, nsh, name="mm_tn", tm=1024):
    dys = list(dy) if isinstance(dy, (list, tuple)) else [dy]
    npart = len(dys)
    per = nsh // npart
    m, k = a.shape
    ns = dys[0].shape[1] // per
    assert all(d.shape == (m, per * ns) for d in dys)
    tm = _row_tile(m, tm)
    nt = m // tm

    def body(*refs):
        a_ref = refs[0]
        dy_refs = refs[1:1 + npart]
        o_ref, acc_ref = refs[1 + npart:]
        j = pl.program_id(0)
        i = pl.program_id(1)

        @pl.when(i == 0)
        def _():
            acc_ref[...] = jnp.zeros(acc_ref.shape, F32)

        for pi in range(npart):
            @pl.when(j // per == pi)
            def _(pi=pi):
                acc_ref[...] += _dot_tn(a_ref[...].astype(BF16), dy_refs[pi][...].astype(BF16))

        @pl.when(i == nt - 1)
        def _():
            o_ref[...] = acc_ref[...].astype(BF16)

    dy_specs = [
        pl.BlockSpec((tm, ns), lambda j, i, pi=pi: (jnp.where(j // per == pi, i, 0), jnp.clip(j - pi * per, 0, per - 1)))
        for pi in range(npart)
    ]
    return pl.pallas_call(
        body,
        name=name,
        grid=(nsh, nt),
        in_specs=[pl.BlockSpec((tm, k), lambda j, i: (i, 0)), *dy_specs],
        out_specs=pl.BlockSpec((None, k, ns), lambda j, i: (j, 0, 0)),
        out_shape=jax.ShapeDtypeStruct((nsh, k, ns), BF16),
        scratch_shapes=[pltpu.VMEM((k, ns), F32)],
        compiler_params=_cp("parallel", "arbitrary"),
    )(a, *dys)


DEP_SPEC_SHAPE = (8, LANES)


def resid_norm_fwd(x, y, g_post, next_gains, name, tm=256, dep=None):
    t, d = x.shape
    tm = _row_tile(t, tm)
    has_y = y is not None
    n_next = len(next_gains)
    n_dep = 0 if dep is None else 1

    def body(*refs):
        x_ref = refs[0]
        pos = 1
        if has_y:
            y_ref, gp_ref = refs[1], refs[2]
            pos = 3
        gn_refs = refs[pos:pos + n_next]
        outs = refs[pos + n_next + n_dep:]
        xv = x_ref[...]
        o = 0
        if has_y:
            yv = y_ref[...].astype(F32)
            r = lax.rsqrt(jnp.mean(yv * yv, axis=-1, keepdims=True) + EPS)
            xv = xv + (yv * r) * gp_ref[...]
            outs[0][...] = xv
            o = 1
        if n_next:
            xn = xv * lax.rsqrt(jnp.mean(xv * xv, axis=-1, keepdims=True) + EPS)
            for k in range(n_next):
                outs[o + k][...] = (xn * gn_refs[k][...]).astype(BF16)

    row = pl.BlockSpec((tm, d), lambda i: (i, 0))
    vec = pl.BlockSpec((1, d), lambda i: (0, 0))
    args, in_specs = [x], [row]
    if has_y:
        args += [y, g_post]
        in_specs += [row, vec]
    args += list(next_gains)
    in_specs += [vec] * n_next
    if n_dep:
        args.append(dep)
        in_specs.append(pl.BlockSpec(DEP_SPEC_SHAPE, lambda i: (0, 0)))
    out_shape, out_specs = [], []
    if has_y:
        out_shape.append(jax.ShapeDtypeStruct((t, d), F32))
        out_specs.append(row)
    for _ in range(n_next):
        out_shape.append(jax.ShapeDtypeStruct((t, d), BF16))
        out_specs.append(row)
    return pl.pallas_call(
        body, name=name, grid=(t // tm,), in_specs=in_specs, out_specs=out_specs, out_shape=out_shape,
        compiler_params=_cp("parallel"),
    )(*args)


def norm_bwd(x, g, dy, add=None, out_dtype=F32, name="norm_bwd", tm=256, dep=None):
    t, d = x.shape
    tm = _row_tile(t, tm)
    has_add = add is not None

    def body(*refs):
        x_ref, g_ref, dy_ref = refs[:3]
        add_ref = refs[3] if has_add else None
        dx_ref, dg_ref, cs_ref = refs[-3:]
        i = pl.program_id(0)
        xv = x_ref[...].astype(F32)
        dyv = dy_ref[...].astype(F32)
        r = lax.rsqrt(jnp.mean(xv * xv, axis=-1, keepdims=True) + EPS)
        gd = dyv * g_ref[...]
        dx = r * gd - xv * ((r * r * r) * jnp.mean(xv * gd, axis=-1, keepdims=True))
        if has_add:
            dx = dx + add_ref[...]
        dx_ref[...] = dx.astype(out_dtype)
        dg = jnp.sum(dyv * (xv * r), axis=0, keepdims=True)
        cs = jnp.sum(dx, axis=0, keepdims=True)

        @pl.when(i == 0)
        def _():
            dg_ref[...] = dg
            cs_ref[...] = cs

        @pl.when(i > 0)
        def _():
            dg_ref[...] += dg
            cs_ref[...] += cs

    row = pl.BlockSpec((tm, d), lambda i: (i, 0))
    vec = pl.BlockSpec((1, d), lambda i: (0, 0))
    args, in_specs = [x, g, dy], [row, vec, row]
    if has_add:
        args.append(add)
        in_specs.append(row)
    if dep is not None:
        args.append(dep)
        in_specs.append(pl.BlockSpec(DEP_SPEC_SHAPE, lambda i: (0, 0)))
    return pl.pallas_call(
        body, name=name, grid=(t // tm,), in_specs=in_specs,
        out_specs=[row, vec, vec],
        out_shape=[jax.ShapeDtypeStruct((t, d), out_dtype), jax.ShapeDtypeStruct((1, d), F32),
                   jax.ShapeDtypeStruct((1, d), F32)],
        compiler_params=_cp("arbitrary"),
    )(*args)


def loss_fwd_bwd(x, target, name="loss", tm=256):
    t, d = x.shape
    tm = _row_tile(t, tm)

    def body(x_ref, t_ref, dx_ref, l_ref):
        i = pl.program_id(0)
        err = x_ref[...] - t_ref[...]
        dx_ref[...] = err * (1.0 / d)
        part = 0.5 * jnp.sum(jnp.mean(err * err, axis=-1, keepdims=True), axis=0, keepdims=True)
        part = jnp.broadcast_to(part, l_ref.shape)

        @pl.when(i == 0)
        def _():
            l_ref[...] = part

        @pl.when(i > 0)
        def _():
            l_ref[...] += part

    row = pl.BlockSpec((tm, d), lambda i: (i, 0))
    return pl.pallas_call(
        body, name=name, grid=(t // tm,), in_specs=[row, row],
        out_specs=[row, pl.BlockSpec((8, LANES), lambda i: (0, 0))],
        out_shape=[jax.ShapeDtypeStruct((t, d), F32), jax.ShapeDtypeStruct((8, LANES), F32)],
        compiler_params=_cp("arbitrary"),
    )(x, target)


CONV_HALO = 32
CONV_CHUNK = 128


def glu_conv_fwd(z, dw, dwb, name, tc=128):
    b, s, c2 = z.shape
    c = c2 // 2
    kw = dw.shape[0]
    tc = min(tc, c)
    nc = c // tc
    ch = min(CONV_CHUNK, s)
    halo = CONV_HALO
    assert kw - 1 <= halo and s % ch == 0

    nch = s // ch

    def body(a_ref, g_ref, w_ref, b_ref, o_ref, pad_ref):
        _fill_glu_slabs(a_ref, g_ref, pad_ref, nch, ch, halo)

        def chunk(ci, carry):
            r0 = pl.multiple_of(ci * ch, ch)
            acc = b_ref[...]
            for k, tap in enumerate(_taps_front(pad_ref, ci, kw, ch, halo)):
                acc = acc + w_ref[k:k + 1, :] * tap
            o_ref[pl.ds(r0, ch), :] = acc
            return carry

        lax.fori_loop(0, nch, chunk, 0)

    return pl.pallas_call(
        body, name=name, grid=(b, nc),
        in_specs=[
            pl.BlockSpec((None, s, tc), lambda bi, i: (bi, 0, i)),
            pl.BlockSpec((None, s, tc), lambda bi, i: (bi, 0, i + nc)),
            pl.BlockSpec((kw, tc), lambda bi, i: (0, i)),
            pl.BlockSpec((1, tc), lambda bi, i: (0, i)),
        ],
        out_specs=pl.BlockSpec((None, s, tc), lambda bi, i: (bi, 0, i)),
        out_shape=jax.ShapeDtypeStruct((b, s, c), F32),
        scratch_shapes=[pltpu.VMEM((nch, ch + halo, tc), F32)],
        compiler_params=_cp("parallel", "parallel"),
    )(z, z, dw, dwb)


def glu_conv_bwd(z, dw, du2, name, tc=128):
    b, s, c2 = z.shape
    c = c2 // 2
    kw = dw.shape[0]
    tc = min(tc, c)
    nc = c // tc
    ch = min(CONV_CHUNK, s)
    nch = s // ch
    halo = CONV_HALO

    def body(a_ref, g_ref, w_ref, du_ref, dza_ref, dzg_ref, ddw_ref, ddwb_ref, dba_ref, dbg_ref, upad_ref, dpad_ref):
        bi = pl.program_id(1)

        @pl.when(bi == 0)
        def _():
            ddw_ref[...] = jnp.zeros(ddw_ref.shape, F32)
            ddwb_ref[...] = jnp.zeros(ddwb_ref.shape, F32)
            dba_ref[...] = jnp.zeros(dba_ref.shape, F32)
            dbg_ref[...] = jnp.zeros(dbg_ref.shape, F32)

        _fill_glu_slabs(a_ref, g_ref, upad_ref, nch, ch, halo)
        dpad_ref[nch - 1, ch:ch + halo, :] = jnp.zeros((halo, tc), F32)
        dpad_ref[nch - 1, 0:ch, :] = du_ref[s - ch:s, :]

        def fill(ci, carry):
            r0 = pl.multiple_of(ci * ch, ch)
            dpad_ref[ci, :, :] = du_ref[pl.ds(r0, ch + halo), :]
            return carry

        lax.fori_loop(0, nch - 1, fill, 0)

        def chunk(ci, carry):
            r0 = pl.multiple_of(ci * ch, ch)
            du_c = du_ref[pl.ds(r0, ch), :]
            taps_u = _taps_front(upad_ref, ci, kw, ch, halo)
            du1 = w_ref[kw - 1:kw, :] * du_c
            ddw_ref[kw - 1] += jnp.sum((du_c * taps_u[kw - 1]).reshape(ch // 8, 8, tc), axis=0)
            for j in range(1, kw):
                du1 = du1 + w_ref[kw - 1 - j:kw - j, :] * dpad_ref[ci, j:j + ch, :]
                ddw_ref[kw - 1 - j] += jnp.sum((du_c * taps_u[kw - 1 - j]).reshape(ch // 8, 8, tc), axis=0)
            av = a_ref[pl.ds(r0, ch), :]
            sg = _sigmoid(g_ref[pl.ds(r0, ch), :])
            dza = du1 * sg
            dzg = du1 * av * (sg * (1.0 - sg))
            dza_ref[pl.ds(r0, ch), :] = dza.astype(BF16)
            dzg_ref[pl.ds(r0, ch), :] = dzg.astype(BF16)
            dba_ref[...] += jnp.sum(dza, axis=0, keepdims=True)
            dbg_ref[...] += jnp.sum(dzg, axis=0, keepdims=True)
            ddwb_ref[...] += jnp.sum(du_c, axis=0, keepdims=True)
            return carry

        lax.fori_loop(0, s // ch, chunk, 0)

    blk = lambda off: pl.BlockSpec((None, s, tc), lambda i, bi: (bi, 0, i + off))
    vec = pl.BlockSpec((1, tc), lambda i, bi: (0, i))
    return pl.pallas_call(
        body, name=name, grid=(nc, b),
        in_specs=[blk(0), blk(nc), pl.BlockSpec((kw, tc), lambda i, bi: (0, i)), blk(0)],
        out_specs=[blk(0), blk(0), pl.BlockSpec((kw, 8, tc), lambda i, bi: (0, 0, i)), vec, vec, vec],
        out_shape=[
            jax.ShapeDtypeStruct((b, s, c), BF16), jax.ShapeDtypeStruct((b, s, c), BF16),
            jax.ShapeDtypeStruct((kw, 8, c), F32), jax.ShapeDtypeStruct((1, c), F32),
            jax.ShapeDtypeStruct((1, c), F32), jax.ShapeDtypeStruct((1, c), F32),
        ],
        scratch_shapes=[pltpu.VMEM((nch, ch + halo, tc), F32), pltpu.VMEM((nch, ch + halo, tc), F32)],
        compiler_params=_cp("parallel", "arbitrary"),
    )(z, z, dw, du2)


def _fill_glu_slabs(a_ref, g_ref, pad_ref, nch, ch, halo):
    tc = a_ref.shape[-1]
    pad_ref[0, 0:halo, :] = jnp.zeros((halo, tc), F32)
    pad_ref[0, halo:halo + ch, :] = a_ref[0:ch, :] * _sigmoid(g_ref[0:ch, :])

    def fill(ci, carry):
        r0 = pl.multiple_of(ci * ch, ch)
        pad_ref[ci, 0:halo, :] = pad_ref[ci - 1, ch:ch + halo, :]
        pad_ref[ci, halo:halo + ch, :] = a_ref[pl.ds(r0, ch), :] * _sigmoid(g_ref[pl.ds(r0, ch), :])
        return carry

    lax.fori_loop(1, nch, fill, 0)


def ln_silu_fwd(u, g, bvec, name, tm=256):
    t, d = u.shape
    tm = _row_tile(t, tm)

    def body(u_ref, g_ref, b_ref, o_ref):
        uv = u_ref[...]
        mu = jnp.mean(uv, axis=-1, keepdims=True)
        xc = uv - mu
        var = jnp.mean(xc * xc, axis=-1, keepdims=True)
        v = (xc * lax.rsqrt(var + EPS)) * g_ref[...] + b_ref[...]
        o_ref[...] = (v * _sigmoid(v)).astype(BF16)

    row = pl.BlockSpec((tm, d), lambda i: (i, 0))
    vec = pl.BlockSpec((1, d), lambda i: (0, 0))
    return pl.pallas_call(
        body, name=name, grid=(t // tm,), in_specs=[row, vec, vec], out_specs=row,
        out_shape=jax.ShapeDtypeStruct((t, d), BF16), compiler_params=_cp("parallel"),
    )(u, g, bvec)


def ln_silu_bwd(u, g, bvec, dout, name, tm=256):
    t, d = u.shape
    tm = _row_tile(t, tm)

    def body(u_ref, g_ref, b_ref, do_ref, du_ref, dg_ref, db_ref):
        i = pl.program_id(0)
        uv = u_ref[...]
        mu = jnp.mean(uv, axis=-1, keepdims=True)
        xc = uv - mu
        var = jnp.mean(xc * xc, axis=-1, keepdims=True)
        rstd = lax.rsqrt(var + EPS)
        n = xc * rstd
        v = n * g_ref[...] + b_ref[...]
        sg = _sigmoid(v)
        dv = do_ref[...].astype(F32) * (sg * (1.0 + v * (1.0 - sg)))
        dn = dv * g_ref[...]
        du_ref[...] = rstd * (dn - jnp.mean(dn, axis=-1, keepdims=True) - n * jnp.mean(dn * n, axis=-1, keepdims=True))
        dg = jnp.sum(dv * n, axis=0, keepdims=True)
        db = jnp.sum(dv, axis=0, keepdims=True)

        @pl.when(i == 0)
        def _():
            dg_ref[...] = dg
            db_ref[...] = db

        @pl.when(i > 0)
        def _():
            dg_ref[...] += dg
            db_ref[...] += db

    row = pl.BlockSpec((tm, d), lambda i: (i, 0))
    vec = pl.BlockSpec((1, d), lambda i: (0, 0))
    return pl.pallas_call(
        body, name=name, grid=(t // tm,), in_specs=[row, vec, vec, row], out_specs=[row, vec, vec],
        out_shape=[jax.ShapeDtypeStruct((t, d), F32), jax.ShapeDtypeStruct((1, d), F32), jax.ShapeDtypeStruct((1, d), F32)],
        compiler_params=_cp("arbitrary"),
    )(u, g, bvec, dout)


FFN_HALO = 8


def _fill_front_halo(src_ref, pad_ref, nch, ch, halo):
    tc = src_ref.shape[-1]
    pad_ref[0, 0:halo, :] = jnp.zeros((halo, tc), F32)
    pad_ref[0, halo:halo + ch, :] = src_ref[0:ch, :].astype(F32)

    def fill(ci, carry):
        r0 = pl.multiple_of(ci * ch, ch)
        pad_ref[ci, 0:halo, :] = src_ref[pl.ds(r0 - 2 * halo, 2 * halo), :].astype(F32)[halo:, :]
        pad_ref[ci, halo:halo + ch, :] = src_ref[pl.ds(r0, ch), :].astype(F32)
        return carry

    lax.fori_loop(1, nch, fill, 0)


def _taps_front(pad_ref, ci, kw, ch, halo):
    return [pad_ref[ci, halo - (kw - 1 - k):halo - (kw - 1 - k) + ch, :] for k in range(kw)]


def ffn_mid_fwd(p, dw, dwb, name, tc=256):
    b, s, f2 = p.shape
    f = f2 // 2
    kw = dw.shape[0]
    tc = min(tc, f)
    nf = f // tc
    ch = min(CONV_CHUNK, s)
    nch = s // ch
    halo = FFN_HALO

    def body(pa_ref, pg_ref, wa_ref, wg_ref, ba_ref, bg_ref, o_ref, apad_ref, gpad_ref):
        _fill_front_halo(pa_ref, apad_ref, nch, ch, halo)
        _fill_front_halo(pg_ref, gpad_ref, nch, ch, halo)

        def chunk(ci, carry):
            r0 = pl.multiple_of(ci * ch, ch)
            ca = ba_ref[...]
            cg = bg_ref[...]
            taps = zip(_taps_front(apad_ref, ci, kw, ch, halo), _taps_front(gpad_ref, ci, kw, ch, halo))
            for k, (ta, tg) in enumerate(taps):
                ca = ca + wa_ref[k:k + 1, :] * ta
                cg = cg + wg_ref[k:k + 1, :] * tg
            o_ref[pl.ds(r0, ch), :] = ((cg * _sigmoid(cg)) * ca).astype(BF16)
            return carry

        lax.fori_loop(0, nch, chunk, 0)

    blk = lambda off: pl.BlockSpec((None, s, tc), lambda bi, i: (bi, 0, i + off))
    wsp = lambda off: pl.BlockSpec((kw, tc), lambda bi, i: (0, i + off))
    bsp = lambda off: pl.BlockSpec((1, tc), lambda bi, i: (0, i + off))
    return pl.pallas_call(
        body, name=name, grid=(b, nf),
        in_specs=[blk(0), blk(nf), wsp(0), wsp(nf), bsp(0), bsp(nf)],
        out_specs=pl.BlockSpec((None, s, tc), lambda bi, i: (bi, 0, i)),
        out_shape=jax.ShapeDtypeStruct((b, s, f), BF16),
        scratch_shapes=[pltpu.VMEM((nch, ch + halo, tc), F32)] * 2,
        compiler_params=_cp("parallel", "parallel"),
    )(p, p, dw, dw, dwb, dwb)


def ffn_mid_bwd(p, dw, dwb, ds, name, tc=256):
    b, s, f2 = p.shape
    f = f2 // 2
    kw = dw.shape[0]
    tc = min(tc, f)
    nf = f // tc
    ch = min(CONV_CHUNK, s)
    nch = s // ch
    halo = FFN_HALO

    def sum8(v):
        return jnp.sum(v.reshape(ch // 8, 8, tc), axis=0)

    def body(pa_ref, pg_ref, wa_ref, wg_ref, ba_ref, bg_ref, ds_ref, dpa_ref, dpg_ref, ddwa_ref, ddwg_ref, dba_ref, dbg_ref,
             apad_ref, gpad_ref, dca_ref, dcg_ref):
        bi = pl.program_id(1)

        @pl.when(bi == 0)
        def _():
            ddwa_ref[...] = jnp.zeros(ddwa_ref.shape, F32)
            ddwg_ref[...] = jnp.zeros(ddwg_ref.shape, F32)
            dba_ref[...] = jnp.zeros(dba_ref.shape, F32)
            dbg_ref[...] = jnp.zeros(dbg_ref.shape, F32)

        _fill_front_halo(pa_ref, apad_ref, nch, ch, halo)
        _fill_front_halo(pg_ref, gpad_ref, nch, ch, halo)
        dca_ref[nch - 1, ch:ch + halo, :] = jnp.zeros((halo, tc), F32)
        dcg_ref[nch - 1, ch:ch + halo, :] = jnp.zeros((halo, tc), F32)

        def grads(ci, carry):
            acc_a, acc_g, sb_a, sb_g = carry
            r0 = pl.multiple_of(ci * ch, ch)
            taps_a = _taps_front(apad_ref, ci, kw, ch, halo)
            taps_g = _taps_front(gpad_ref, ci, kw, ch, halo)
            ca = ba_ref[...]
            cg = bg_ref[...]
            for k in range(kw):
                ca = ca + wa_ref[k:k + 1, :] * taps_a[k]
                cg = cg + wg_ref[k:k + 1, :] * taps_g[k]
            sg = _sigmoid(cg)
            dsv = ds_ref[pl.ds(r0, ch), :].astype(F32)
            dca = dsv * (cg * sg)
            dcg = dsv * ca * (sg * (1.0 + cg * (1.0 - sg)))
            dca_ref[ci, 0:ch, :] = dca
            dcg_ref[ci, 0:ch, :] = dcg

            prev = jnp.maximum(ci - 1, 0)

            @pl.when(ci > 0)
            def _():
                dca_ref[prev, ch:ch + halo, :] = dca[0:halo, :]
                dcg_ref[prev, ch:ch + halo, :] = dcg[0:halo, :]

            acc_a = tuple(acc_a[k] + sum8(dca * taps_a[k]) for k in range(kw))
            acc_g = tuple(acc_g[k] + sum8(dcg * taps_g[k]) for k in range(kw))
            return acc_a, acc_g, sb_a + sum8(dca), sb_g + sum8(dcg)

        z8 = jnp.zeros((8, tc), F32)
        acc_a, acc_g, sb_a, sb_g = lax.fori_loop(0, nch, grads, ((z8,) * kw, (z8,) * kw, z8, z8))
        for k in range(kw):
            ddwa_ref[k] += acc_a[k]
            ddwg_ref[k] += acc_g[k]
        dba_ref[...] += jnp.sum(sb_a, axis=0, keepdims=True)
        dbg_ref[...] += jnp.sum(sb_g, axis=0, keepdims=True)

        def back(ci, carry):
            r0 = pl.multiple_of(ci * ch, ch)
            da = wa_ref[kw - 1:kw, :] * dca_ref[ci, 0:ch, :]
            dg = wg_ref[kw - 1:kw, :] * dcg_ref[ci, 0:ch, :]
            for j in range(1, kw):
                da = da + wa_ref[kw - 1 - j:kw - j, :] * dca_ref[ci, j:j + ch, :]
                dg = dg + wg_ref[kw - 1 - j:kw - j, :] * dcg_ref[ci, j:j + ch, :]
            dpa_ref[pl.ds(r0, ch), :] = da.astype(BF16)
            dpg_ref[pl.ds(r0, ch), :] = dg.astype(BF16)
            return carry

        lax.fori_loop(0, nch, back, 0)

    blk = lambda off: pl.BlockSpec((None, s, tc), lambda i, bi: (bi, 0, i + off))
    wsp = lambda off: pl.BlockSpec((kw, tc), lambda i, bi: (0, i + off))
    bsp = lambda off: pl.BlockSpec((1, tc), lambda i, bi: (0, i + off))
    acc3 = pl.BlockSpec((kw, 8, tc), lambda i, bi: (0, 0, i))
    vec = pl.BlockSpec((1, tc), lambda i, bi: (0, i))
    return pl.pallas_call(
        body, name=name, grid=(nf, b),
        in_specs=[blk(0), blk(nf), wsp(0), wsp(nf), bsp(0), bsp(nf), blk(0)],
        out_specs=[blk(0), blk(0), acc3, acc3, vec, vec],
        out_shape=[jax.ShapeDtypeStruct((b, s, f), BF16), jax.ShapeDtypeStruct((b, s, f), BF16),
                   jax.ShapeDtypeStruct((kw, 8, f), F32), jax.ShapeDtypeStruct((kw, 8, f), F32),
                   jax.ShapeDtypeStruct((1, f), F32), jax.ShapeDtypeStruct((1, f), F32)],
        scratch_shapes=[pltpu.VMEM((nch, ch + halo, tc), F32)] * 4,
        compiler_params=_cp("parallel", "arbitrary"),
    )(p, p, dw, dw, dwb, dwb, ds)


def _tile_rows(r, n, dil):
    start = r + n * BLK * dil
    return pl.ds(start, BLK, stride=dil) if dil > 1 else pl.ds(start, BLK)


def _band_masks():
    qi = lax.broadcasted_iota(jnp.int32, (BLK, 2 * BLK), 0)
    kk = lax.broadcasted_iota(jnp.int32, (BLK, 2 * BLK), 1)
    both = jnp.logical_or(jnp.logical_and(kk < BLK, kk >= qi), jnp.logical_and(kk >= BLK, kk - BLK <= qi))
    return both, kk[:, :BLK] <= qi[:, :BLK]


def attn_fwd(q, kv, g, dil, hw, name):
    _, b, s, _ = q.shape
    nh = hw // HEAD_DIM
    nblk = s // dil // BLK
    scale = 1.0 / math.sqrt(HEAD_DIM)

    def body(q_ref, k_ref, v_ref, o_ref, lse_ref):
        h = pl.program_id(1)
        mask2, mask1 = _band_masks()
        mine = lax.broadcasted_iota(jnp.int32, (BLK, LANES), 1) == h

        @pl.when(h == 0)
        def _():
            lse_ref[...] = jnp.zeros(lse_ref.shape, F32)

        for r in range(dil):
            kp = vp = None
            for n in range(nblk):
                rs = _tile_rows(r, n, dil)
                qt = q_ref[rs, :].astype(BF16)
                kc = k_ref[rs, :].astype(BF16)
                vc = v_ref[rs, :].astype(BF16)
                if n == 0:
                    kcat, vcat, mask = kc, vc, mask1
                else:
                    kcat, vcat, mask = jnp.concatenate([kp, kc], axis=0), jnp.concatenate([vp, vc], axis=0), mask2
                sc = jnp.where(mask, _dot_nt(qt, kcat) * scale, NEG_INF)
                m = jnp.max(sc, axis=-1, keepdims=True)
                p = jnp.exp(sc - m)
                den = jnp.sum(p, axis=-1, keepdims=True)
                o_ref[rs, :] = _dot(p.astype(BF16), vcat) / den
                lse_ref[rs, :] = jnp.where(mine, m + jnp.log(den), lse_ref[rs, :])
                kp, vp = kc, vc

    col = lambda base: pl.BlockSpec((None, s, HEAD_DIM), lambda bi, h: (bi, 0, base + h))
    head = lambda base: pl.BlockSpec((None, None, s, HEAD_DIM), lambda bi, h: (base + h, bi, 0, 0))
    return pl.pallas_call(
        body, name=name, grid=(b, nh),
        in_specs=[head(g * nh), head(g * nh), head((N_GROUPS + g) * nh)],
        out_specs=[col(0), pl.BlockSpec((None, s, LANES), lambda bi, h: (bi, 0, 0))],
        out_shape=[jax.ShapeDtypeStruct((b, s, hw), F32), jax.ShapeDtypeStruct((b, s, LANES), F32)],
        compiler_params=_cp("parallel", "arbitrary"),
    )(q, kv, kv)


def attn_merge(outs, lses, name, tm=256):
    t, hw = outs[0].shape
    nh = hw // HEAD_DIM
    tm = _row_tile(t, tm)
    ng = len(outs)

    def body(*refs):
        o_refs, l_refs = refs[:ng], refs[ng:2 * ng]
        m_ref, lj_ref = refs[2 * ng:]
        ls = [l_refs[g][...] for g in range(ng)]
        mx = ls[0]
        for g in range(1, ng):
            mx = jnp.maximum(mx, ls[g])
        es = [jnp.exp(l - mx) for l in ls]
        tot = es[0]
        for g in range(1, ng):
            tot = tot + es[g]
        ws = [e / tot for e in es]
        lj_ref[...] = mx + jnp.log(tot)
        for h in range(nh):
            sl = slice(h * HEAD_DIM, (h + 1) * HEAD_DIM)
            acc = ws[0][:, h:h + 1] * o_refs[0][:, sl]
            for g in range(1, ng):
                acc = acc + ws[g][:, h:h + 1] * o_refs[g][:, sl]
            m_ref[:, sl] = acc.astype(BF16)

    row = pl.BlockSpec((tm, hw), lambda i: (i, 0))
    st = pl.BlockSpec((tm, LANES), lambda i: (i, 0))
    return pl.pallas_call(
        body, name=name, grid=(t // tm,), in_specs=[row] * ng + [st] * ng, out_specs=[row, st],
        out_shape=[jax.ShapeDtypeStruct((t, hw), BF16), jax.ShapeDtypeStruct((t, LANES), F32)],
        compiler_params=_cp("parallel"),
    )(*outs, *lses)


def attn_bwd_prep(dmerged, merged, name, tm=256):
    t, hw = merged.shape
    nh = hw // HEAD_DIM
    tm = _row_tile(t, tm)

    def body(d_ref, m_ref, o_ref):
        lane = lax.broadcasted_iota(jnp.int32, (tm, LANES), 1)
        acc = jnp.zeros((tm, LANES), F32)
        for h in range(nh):
            sl = slice(h * HEAD_DIM, (h + 1) * HEAD_DIM)
            dsum = jnp.sum(d_ref[:, sl] * m_ref[:, sl].astype(F32), axis=-1, keepdims=True)
            acc = jnp.where(lane == h, dsum, acc)
        o_ref[...] = acc

    row = pl.BlockSpec((tm, hw), lambda i: (i, 0))
    return pl.pallas_call(
        body, name=name, grid=(t // tm,), in_specs=[row, row], out_specs=pl.BlockSpec((tm, LANES), lambda i: (i, 0)),
        out_shape=jax.ShapeDtypeStruct((t, LANES), F32), compiler_params=_cp("parallel"),
    )(dmerged, merged)


def attn_bwd(q, kv, g, dil, do, lsej, dm, dq_buf, dk_buf, dv_buf, accumulate, hw, name):
    _, b, s, _ = q.shape
    nh = hw // HEAD_DIM
    nblk = s // dil // BLK
    scale = 1.0 / math.sqrt(HEAD_DIM)
    assert dk_buf is not None or not accumulate
    kv_at = 6 + (dq_buf is not None)

    def body(*refs):
        q_ref, k_ref, v_ref, do_ref, lj_ref, dm_ref = refs[:6]
        dq_ref, dk_ref, dv_ref = refs[-3:]
        dki_ref, dvi_ref = (refs[kv_at], refs[kv_at + 1]) if accumulate else (None, None)
        mask2, mask1 = _band_masks()
        mine = lax.broadcasted_iota(jnp.int32, (BLK, LANES), 1) == pl.program_id(1)

        def my_lane(v):
            return jnp.sum(jnp.where(mine, v, 0.0), axis=-1, keepdims=True)

        def put(rs, dk, dv):
            if accumulate:
                dk = dk + dki_ref[rs, :]
                dv = dv + dvi_ref[rs, :]
            dk_ref[rs, :] = dk
            dv_ref[rs, :] = dv

        for r in range(dil):
            kp = vp = hold_k = hold_v = rs_prev = None
            for n in range(nblk):
                rs = _tile_rows(r, n, dil)
                qt = q_ref[rs, :].astype(BF16)
                kc = k_ref[rs, :].astype(BF16)
                vc = v_ref[rs, :].astype(BF16)
                dot = do_ref[rs, :].astype(BF16)
                lm = my_lane(lj_ref[rs, :])
                dmm = my_lane(dm_ref[rs, :])
                if n == 0:
                    kcat, vcat, mask = kc, vc, mask1
                else:
                    kcat, vcat, mask = jnp.concatenate([kp, kc], axis=0), jnp.concatenate([vp, vc], axis=0), mask2
                p = jnp.exp(jnp.where(mask, _dot_nt(qt, kcat) * scale, NEG_INF) - lm)
                ds = (p * (_dot_nt(dot, vcat) - dmm)).astype(BF16)
                dq_ref[rs, :] = _dot(ds, kcat) * scale
                dkc = _dot_tn(ds, qt) * scale
                dvc = _dot_tn(p.astype(BF16), dot)
                if n > 0:
                    put(rs_prev, hold_k + dkc[:BLK, :], hold_v + dvc[:BLK, :])
                    dkc, dvc = dkc[BLK:, :], dvc[BLK:, :]
                hold_k, hold_v, kp, vp, rs_prev = dkc, dvc, kc, vc, rs
            put(rs_prev, hold_k, hold_v)

    col = lambda base: pl.BlockSpec((None, s, HEAD_DIM), lambda bi, h: (bi, 0, base + h))
    any_spec = pl.BlockSpec(memory_space=pl.ANY)
    stat = pl.BlockSpec((None, s, LANES), lambda bi, h: (bi, 0, 0))
    head = lambda base: pl.BlockSpec((None, None, s, HEAD_DIM), lambda bi, h: (base + h, bi, 0, 0))
    in_specs = [head(g * nh), head(g * nh), head((N_GROUPS + g) * nh), col(0), stat, stat]
    args = [q, kv, kv, do, lsej, dm]
    aliases = {}
    if dq_buf is not None:
        in_specs.append(any_spec)
        args.append(dq_buf)
        aliases[6] = 0
    if dk_buf is not None:
        in_specs += [col(g * nh) if accumulate else any_spec] * 2
        args += [dk_buf, dv_buf]
        aliases.update({kv_at: 1, kv_at + 1: 2})
    shape = jax.ShapeDtypeStruct((b, s, N_GROUPS * hw), F32)
    return pl.pallas_call(
        body, name=name, grid=(b, nh), in_specs=in_specs, out_specs=[col(g * nh)] * 3, out_shape=[shape] * 3,
        input_output_aliases=aliases, compiler_params=_cp("parallel", "parallel"),
    )(*args)


def sum_parts(g, recv, me, name, tm=256):
    _, rows, c = g.shape
    n = recv.shape[0]
    tm = _row_tile(rows, tm)

    def body(me_ref, g_ref, r_ref, o_ref):
        acc = g_ref[...].astype(F32)
        for j in range(n):
            acc = acc + r_ref[j].astype(F32)
        o_ref[...] = acc

    return pl.pallas_call(
        body, name=name,
        grid_spec=pltpu.PrefetchScalarGridSpec(
            num_scalar_prefetch=1, grid=(rows // tm,),
            in_specs=[pl.BlockSpec((None, tm, c), lambda i, me_ref: (me_ref[0], i, 0)),
                      pl.BlockSpec((n, tm, c), lambda i, me_ref: (0, i, 0))],
            out_specs=pl.BlockSpec((tm, c), lambda i, me_ref: (i, 0))),
        out_shape=jax.ShapeDtypeStruct((rows, c), F32), compiler_params=_cp("parallel"),
    )(me, g, recv)


def adamw(w, m, v, g_parts, name, tm=256):
    rows, c = w.shape
    tm = _row_tile(rows, tm)
    npart = len(g_parts)

    def body(*refs):
        w_ref, m_ref, v_ref = refs[:3]
        g_refs = refs[3:3 + npart]
        go_ref, d_ref, mo_ref, vo_ref = refs[3 + npart:]
        g = g_refs[0][...]
        for k in range(1, npart):
            g = g + g_refs[k][...]
        mn = ADAM_B1 * m_ref[...] + (1.0 - ADAM_B1) * g
        vn = ADAM_B2 * v_ref[...] + (1.0 - ADAM_B2) * (g * g)
        m_hat = mn / (1.0 - ADAM_B1 ** ADAM_STEP)
        v_hat = vn / (1.0 - ADAM_B2 ** ADAM_STEP)
        go_ref[...] = g
        d_ref[...] = -ADAM_LR * (m_hat / (jnp.sqrt(v_hat) + ADAM_EPS) + ADAM_WD * w_ref[...])
        mo_ref[...] = mn
        vo_ref[...] = vn

    row = pl.BlockSpec((tm, c), lambda i: (i, 0))
    return pl.pallas_call(
        body, name=name, grid=(rows // tm,), in_specs=[row] * (3 + npart), out_specs=[row] * 4,
        out_shape=[jax.ShapeDtypeStruct((rows, c), F32)] * 4, compiler_params=_cp("parallel"),
    )(w, m, v, *g_parts)


def _place():
    return lax.axis_index("x"), lax.axis_index("y"), lax.axis_index("c")


def _other_chips(x, y, c):
    return [(1 - x, y, c), (x, 1 - y, c), (1 - x, 1 - y, c)]


def _chip_of(px, py):
    return 2 * px + py


HBM_SPEC = pl.BlockSpec(memory_space=pltpu.HBM)
SEM_SPEC = pl.BlockSpec(memory_space=pltpu.SEMAPHORE)
ANY_SPEC = pl.BlockSpec(memory_space=pl.ANY)
DATAFLOW = pltpu.SideEffectType.DATAFLOW_SIDE_EFFECTING
N_PEER_CHIPS = N_CHIPS - 1


def _hbm(a):
    return pltpu.with_memory_space_constraint(a, pltpu.HBM)


def _hbm_like(arrays):
    return [pltpu.HBM(a.shape, a.dtype) for a in arrays]


def cast_place(w, layer, me, out_dtype, name, tm=256, nslots=N_CHIPS, dep=None):
    rows, c = w.shape[-2:]
    tm = _row_tile(rows, tm)

    def body(me_ref, w_ref, *rest):
        rest[-1][...] = w_ref[...].astype(out_dtype)

    if layer is None:
        in_specs = [pl.BlockSpec((tm, c), lambda i, me_ref: (i, 0))]
    else:
        in_specs = [pl.BlockSpec((None, tm, c), lambda i, me_ref: (layer, i, 0))]
    args = [me, w]
    if dep is not None:
        in_specs.append(pl.BlockSpec(DEP_SPEC_SHAPE, lambda i, me_ref: (0, 0)))
        args.append(dep)
    return pl.pallas_call(
        body, name=name,
        grid_spec=pltpu.PrefetchScalarGridSpec(
            num_scalar_prefetch=1, grid=(rows // tm,), in_specs=in_specs,
            out_specs=pl.BlockSpec((None, tm, c), lambda i, me_ref: (me_ref[0], i, 0))),
        out_shape=jax.ShapeDtypeStruct((nslots, rows, c), out_dtype), compiler_params=_cp("parallel"),
    )(*args)


def gather_start(lands, chunk_sizes, name="gather_start"):
    n = len(lands)
    nch = len(chunk_sizes)
    assert sum(chunk_sizes) == n

    def body(*refs):
        land_refs = refs[:n]
        outs = refs[n:]
        send_sems, recv_sems = outs[:nch], outs[nch:2 * nch]
        token = outs[-1]
        x, y, c = _place()
        me = _chip_of(x, y)
        peers = _other_chips(x, y, c)
        k = 0
        for ck, size in enumerate(chunk_sizes):
            for pos in range(size):
                for r, peer in enumerate(peers):
                    pltpu.make_async_remote_copy(
                        src_ref=land_refs[k].at[me], dst_ref=land_refs[k].at[me],
                        send_sem=send_sems[ck].at[N_PEER_CHIPS * pos + r], recv_sem=recv_sems[ck].at[N_PEER_CHIPS * pos + r],
                        device_id=peer, device_id_type=MESH).start()
                k += 1
        token[...] = jnp.zeros(token.shape, F32)

    sems = [pltpu.SemaphoreType.DMA((N_PEER_CHIPS * s,)) for s in chunk_sizes]
    res = pl.pallas_call(
        body, name=name,
        out_shape=(*sems, *sems, *_hbm_like(lands), jax.ShapeDtypeStruct(DEP_SPEC_SHAPE, F32)),
        in_specs=[HBM_SPEC] * n,
        out_specs=(*[SEM_SPEC] * (2 * nch), *[HBM_SPEC] * n, pl.BlockSpec(memory_space=pltpu.VMEM)),
        input_output_aliases={k: 2 * nch + k for k in range(n)},
        compiler_params=pltpu.CompilerParams(has_side_effects=DATAFLOW),
    )(*[_hbm(a) for a in lands])
    return res[:nch], res[nch:2 * nch], res[2 * nch:2 * nch + n], res[-1]


def gather_wait(send_sem, recv_sem, lands, after, name):
    n = len(lands)

    def body(*refs):
        land_refs = refs[:n]
        ssem, rsem = refs[n], refs[n + 1]
        x, y, c = _place()
        me = _chip_of(x, y)
        for pos in range(n):
            for r, peer in enumerate(_other_chips(x, y, c)):
                cp = pltpu.make_async_remote_copy(
                    src_ref=land_refs[pos].at[me], dst_ref=land_refs[pos].at[_chip_of(peer[0], peer[1])],
                    send_sem=ssem.at[N_PEER_CHIPS * pos + r], recv_sem=rsem.at[N_PEER_CHIPS * pos + r],
                    device_id=peer, device_id_type=MESH)
                cp.wait_send()
                cp.wait_recv()

    return pl.pallas_call(
        body, name=name, out_shape=tuple(_hbm_like(lands)),
        in_specs=[*[HBM_SPEC] * n, SEM_SPEC, SEM_SPEC, ANY_SPEC], out_specs=[HBM_SPEC] * n,
        input_output_aliases={k: k for k in range(n)},
        compiler_params=pltpu.CompilerParams(has_side_effects=DATAFLOW),
    )(*lands, send_sem, recv_sem, after)


def scatter_start(grads, name):
    n = len(grads)
    recvs = [lax.empty((N_PEER_CHIPS, *g.shape[1:]), g.dtype) for g in grads]

    def body(*refs):
        g_refs, r_refs = refs[:n], refs[n:2 * n]
        send_sems, recv_sems = refs[2 * n], refs[2 * n + 1]
        token = refs[-1]
        x, y, c = _place()
        for k in range(n):
            for r, peer in enumerate(_other_chips(x, y, c)):
                pltpu.make_async_remote_copy(
                    src_ref=g_refs[k].at[_chip_of(peer[0], peer[1])], dst_ref=r_refs[k].at[r],
                    send_sem=send_sems.at[N_PEER_CHIPS * k + r], recv_sem=recv_sems.at[N_PEER_CHIPS * k + r],
                    device_id=peer, device_id_type=MESH).start()
        token[...] = jnp.zeros(token.shape, F32)

    sem = pltpu.SemaphoreType.DMA((N_PEER_CHIPS * n,))
    res = pl.pallas_call(
        body, name=name,
        out_shape=(sem, sem, *_hbm_like(grads), *_hbm_like(recvs), jax.ShapeDtypeStruct(DEP_SPEC_SHAPE, F32)),
        in_specs=[HBM_SPEC] * (2 * n),
        out_specs=(SEM_SPEC, SEM_SPEC, *[HBM_SPEC] * (2 * n), pl.BlockSpec(memory_space=pltpu.VMEM)),
        input_output_aliases={k: 2 + k for k in range(2 * n)},
        compiler_params=pltpu.CompilerParams(has_side_effects=DATAFLOW),
    )(*[_hbm(a) for a in grads], *[_hbm(a) for a in recvs])
    return res[0], res[1], res[2:2 + n], res[2 + n:2 + 2 * n], res[-1]


def scatter_wait(send_sem, recv_sem, grads, recvs, after, name):
    n = len(grads)

    def body(*refs):
        g_refs, r_refs = refs[:n], refs[n:2 * n]
        ssem, rsem = refs[2 * n], refs[2 * n + 1]
        x, y, c = _place()
        for k in range(n):
            for r, peer in enumerate(_other_chips(x, y, c)):
                cp = pltpu.make_async_remote_copy(
                    src_ref=g_refs[k].at[_chip_of(peer[0], peer[1])], dst_ref=r_refs[k].at[r],
                    send_sem=ssem.at[N_PEER_CHIPS * k + r], recv_sem=rsem.at[N_PEER_CHIPS * k + r],
                    device_id=peer, device_id_type=MESH)
                cp.wait_send()
                cp.wait_recv()

    res = pl.pallas_call(
        body, name=name, out_shape=(*_hbm_like(grads), *_hbm_like(recvs)),
        in_specs=[*[HBM_SPEC] * (2 * n), SEM_SPEC, SEM_SPEC, ANY_SPEC], out_specs=[HBM_SPEC] * (2 * n),
        input_output_aliases={k: k for k in range(2 * n)},
        compiler_params=pltpu.CompilerParams(has_side_effects=DATAFLOW),
    )(*grads, *recvs, send_sem, recv_sem, after)
    return res[:n], res[n:]


def swap_start(parts, name):
    n = len(parts)
    lands = [lax.empty(p.shape, p.dtype) for p in parts]

    def body(*refs):
        p_refs, l_refs = refs[:n], refs[n:2 * n]
        sems = refs[2 * n:4 * n]
        token = refs[-1]
        x, y, c = _place()
        for k in range(n):
            pltpu.make_async_remote_copy(
                src_ref=p_refs[k], dst_ref=l_refs[k], send_sem=sems[k], recv_sem=sems[n + k],
                device_id=(x, y, 1 - c), device_id_type=MESH).start()
        token[...] = jnp.zeros(token.shape, F32)

    sem = pltpu.SemaphoreType.DMA(())
    res = pl.pallas_call(
        body, name=name,
        out_shape=(*[sem] * (2 * n), *_hbm_like(parts), *_hbm_like(lands), jax.ShapeDtypeStruct(DEP_SPEC_SHAPE, F32)),
        in_specs=[HBM_SPEC] * (2 * n),
        out_specs=(*[SEM_SPEC] * (2 * n), *[HBM_SPEC] * (2 * n), pl.BlockSpec(memory_space=pltpu.VMEM)),
        input_output_aliases={k: 2 * n + k for k in range(2 * n)},
        compiler_params=pltpu.CompilerParams(has_side_effects=DATAFLOW),
    )(*[_hbm(a) for a in parts], *[_hbm(a) for a in lands])
    return res[:n], res[n:2 * n], res[2 * n:3 * n], res[3 * n:4 * n], res[-1]


def swap_wait(send_sem, recv_sem, part, land, after, name):
    def body(p_ref, l_ref, ssem, rsem, after_ref, p_out, l_out):
        x, y, c = _place()
        cp = pltpu.make_async_remote_copy(src_ref=p_ref, dst_ref=l_ref, send_sem=ssem, recv_sem=rsem,
                                          device_id=(x, y, 1 - c), device_id_type=MESH)
        cp.wait_send()
        cp.wait_recv()

    return pl.pallas_call(
        body, name=name, out_shape=tuple(_hbm_like([part, land])),
        in_specs=[HBM_SPEC, HBM_SPEC, SEM_SPEC, SEM_SPEC, ANY_SPEC], out_specs=[HBM_SPEC, HBM_SPEC],
        input_output_aliases={0: 0, 1: 1},
        compiler_params=pltpu.CompilerParams(has_side_effects=DATAFLOW),
    )(part, land, send_sem, recv_sem, after)


def _xor_peer(x, y, c, k):
    px, py, pc = x ^ ((k >> 2) & 1), y ^ ((k >> 1) & 1), c ^ (k & 1)
    return (px, py, pc), 4 * px + 2 * py + pc


def small_start(land, name="small_start"):
    def body(l_ref, ssem, rsem, l_out, token):
        x, y, c = _place()
        me = 4 * x + 2 * y + c
        for k in range(1, N_DEV):
            peer, _ = _xor_peer(x, y, c, k)
            pltpu.make_async_remote_copy(
                src_ref=l_ref.at[me], dst_ref=l_ref.at[me], send_sem=ssem.at[k - 1], recv_sem=rsem.at[k - 1],
                device_id=peer, device_id_type=MESH).start()
        token[...] = jnp.zeros(token.shape, F32)

    sem = pltpu.SemaphoreType.DMA((N_DEV - 1,))
    return pl.pallas_call(
        body, name=name,
        out_shape=(sem, sem, pltpu.HBM(land.shape, land.dtype), jax.ShapeDtypeStruct(DEP_SPEC_SHAPE, F32)),
        in_specs=[HBM_SPEC], out_specs=(SEM_SPEC, SEM_SPEC, HBM_SPEC, pl.BlockSpec(memory_space=pltpu.VMEM)),
        input_output_aliases={0: 2}, compiler_params=pltpu.CompilerParams(has_side_effects=DATAFLOW),
    )(_hbm(land))


def small_wait(send_sem, recv_sem, land, after, name="small_wait"):
    def body(l_ref, ssem, rsem, after_ref, l_out):
        x, y, c = _place()
        me = 4 * x + 2 * y + c
        for k in range(1, N_DEV):
            peer, slot = _xor_peer(x, y, c, k)
            cp = pltpu.make_async_remote_copy(
                src_ref=l_ref.at[me], dst_ref=l_ref.at[slot], send_sem=ssem.at[k - 1], recv_sem=rsem.at[k - 1],
                device_id=peer, device_id_type=MESH)
            cp.wait_send()
            cp.wait_recv()

    return pl.pallas_call(
        body, name=name, out_shape=pltpu.HBM(land.shape, land.dtype),
        in_specs=[HBM_SPEC, SEM_SPEC, SEM_SPEC, ANY_SPEC], out_specs=HBM_SPEC, input_output_aliases={0: 0},
        compiler_params=pltpu.CompilerParams(has_side_effects=DATAFLOW),
    )(land, send_sem, recv_sem, after)


def sum_slots(land, name="sum_slots", tm=256):
    n, rows, c = land.shape
    tm = _row_tile(rows, tm)

    def body(l_ref, o_ref):
        acc = l_ref[0]
        for j in range(1, n):
            acc = acc + l_ref[j]
        o_ref[...] = acc

    return pl.pallas_call(
        body, name=name, grid=(rows // tm,), in_specs=[pl.BlockSpec((n, tm, c), lambda i: (0, i, 0))],
        out_specs=pl.BlockSpec((tm, c), lambda i: (i, 0)), out_shape=jax.ShapeDtypeStruct((rows, c), F32),
        compiler_params=_cp("parallel"),
    )(land)


PACK_ROW_TILE = 256


def _pack(arrays):
    flat = jnp.concatenate([a.reshape(-1).astype(F32) for a in arrays])
    n = flat.shape[0]
    rows = -(-n // LANES)
    rows = -(-rows // PACK_ROW_TILE) * PACK_ROW_TILE
    return jnp.pad(flat, (0, rows * LANES - n)).reshape(rows, LANES)


def _unpack(packed, shapes, lead=()):
    flat = packed.reshape(*lead, -1)
    out, off = [], 0
    for shp in shapes:
        n = math.prod(shp)
        out.append(flat[..., off:off + n].reshape(*lead, *shp))
        off += n
    return out


def _row(vec):
    return vec.reshape(1, -1)


def kernel(x, mix_pre_g, mix_post_g, ffn_pre_g, ffn_post_g, cm_w_in, cm_b_in, cm_dw, cm_dw_b, cm_ln_g, cm_ln_b, cm_w_out, cm_b_out, kv_norm_g, w_kv, w_q, w_o, ffn_w_in, ffn_dw, ffn_dw_b, ffn_w_out, loss_target, m_mix_pre_g, m_mix_post_g, m_ffn_pre_g, m_ffn_post_g, m_cm_w_in, m_cm_b_in, m_cm_dw, m_cm_dw_b, m_cm_ln_g, m_cm_ln_b, m_cm_w_out, m_cm_b_out, m_kv_norm_g, m_w_kv, m_w_q, m_w_o, m_ffn_w_in, m_ffn_dw, m_ffn_dw_b, m_ffn_w_out, v_mix_pre_g, v_mix_post_g, v_ffn_pre_g, v_ffn_post_g, v_cm_w_in, v_cm_b_in, v_cm_dw, v_cm_dw_b, v_cm_ln_g, v_cm_ln_b, v_cm_w_out, v_cm_b_out, v_kv_norm_g, v_w_kv, v_w_q, v_w_o, v_ffn_w_in, v_ffn_dw, v_ffn_dw_b, v_ffn_w_out):
    names = ["mix_pre_g", "mix_post_g", "ffn_pre_g", "ffn_post_g", "cm_w_in", "cm_b_in", "cm_dw", "cm_dw_b", "cm_ln_g",
             "cm_ln_b", "cm_w_out", "cm_b_out", "kv_norm_g", "w_kv", "w_q", "w_o", "ffn_w_in", "ffn_dw", "ffn_dw_b",
             "ffn_w_out"]
    w_in = dict(zip(names, [mix_pre_g, mix_post_g, ffn_pre_g, ffn_post_g, cm_w_in, cm_b_in, cm_dw, cm_dw_b, cm_ln_g,
                            cm_ln_b, cm_w_out, cm_b_out, kv_norm_g, w_kv, w_q, w_o, ffn_w_in, ffn_dw, ffn_dw_b, ffn_w_out]))
    m_in = dict(zip(names, [m_mix_pre_g, m_mix_post_g, m_ffn_pre_g, m_ffn_post_g, m_cm_w_in, m_cm_b_in, m_cm_dw, m_cm_dw_b,
                            m_cm_ln_g, m_cm_ln_b, m_cm_w_out, m_cm_b_out, m_kv_norm_g, m_w_kv, m_w_q, m_w_o, m_ffn_w_in,
                            m_ffn_dw, m_ffn_dw_b, m_ffn_w_out]))
    v_in = dict(zip(names, [v_mix_pre_g, v_mix_post_g, v_ffn_pre_g, v_ffn_post_g, v_cm_w_in, v_cm_b_in, v_cm_dw, v_cm_dw_b,
                            v_cm_ln_g, v_cm_ln_b, v_cm_w_out, v_cm_b_out, v_kv_norm_g, v_w_kv, v_w_q, v_w_o, v_ffn_w_in,
                            v_ffn_dw, v_ffn_dw_b, v_ffn_w_out]))

    bsz, seq, d = x.shape
    t = bsz * seq
    n_b = DEPTH - N_A
    hw = w_o.shape[-1]
    qw = N_GROUPS * hw
    f2 = ffn_dw_b.shape[-1]
    f = f2 // 2
    me_chip = _chip_of(lax.axis_index("x"), lax.axis_index("y"))

    big = ["cm_w_in", "cm_w_out", "w_kv", "w_q", "w_o", "ffn_w_in", "ffn_w_out"]
    row_sharded = ("cm_w_out", "w_o", "ffn_w_out")
    small_sharded = ["cm_b_in", "cm_dw", "cm_dw_b", "cm_ln_g", "cm_ln_b", "cm_b_out", "ffn_dw"]
    small_pack = _pack([w_in[n] for n in small_sharded])
    chunks = [
        [("cm_w_in", 0), ("small", None)],
        [("cm_w_out", 0)],
        [("ffn_w_in", 0), ("ffn_w_out", 0)],
        [("cm_w_in", 1), ("cm_w_out", 1)],
        [("ffn_w_in", 1), ("ffn_w_out", 1)],
        [("w_kv", None)],
        [("w_q", 0), ("w_o", 0)],
        [("ffn_w_in", 2), ("ffn_w_out", 2)],
        [("w_q", 1), ("w_o", 1)],
        [("ffn_w_in", 3), ("ffn_w_out", 3)],
    ]
    pieces = [pc for ch in chunks for pc in ch]
    chunk_of = {pc: ck for ck, ch in enumerate(chunks) for pc in ch}

    me_arr = me_chip.astype(jnp.int32).reshape(1)

    def land_of(pc, dep=None):
        n, l = pc
        if n == "small":
            return cast_place(small_pack, None, me_arr, F32, name="place_small", dep=dep)
        return cast_place(w_in[n], l, me_arr, BF16, name=f"place_{n}_{l}", dep=dep)

    n_early = len(chunks[0]) + len(chunks[1])
    lands_a = [land_of(pc) for pc in pieces[:n_early]]
    send_a, recv_a, lands_fa, token_a = gather_start(lands_a, [len(chunks[0]), len(chunks[1])], name="gather_start_a")
    lands_b = [land_of(pc, token_a if k == 0 else None) for k, pc in enumerate(pieces[n_early:])]
    send_b, recv_b, lands_fb, token = gather_start(lands_b, [len(ch) for ch in chunks[2:]], name="gather_start_b")
    g_send, g_recv, lands_f = [*send_a, *send_b], [*recv_a, *recv_b], [*lands_fa, *lands_fb]
    weights = {}

    def finish_chunk(ck, after):
        lo = sum(len(ch) for ch in chunks[:ck])
        hi = lo + len(chunks[ck])
        got = gather_wait(g_send[ck], g_recv[ck], lands_f[lo:hi], after, name=f"gather_wait{ck}")
        for pc, arr in zip(chunks[ck], got):
            weights[pc] = arr.reshape(1, -1, arr.shape[-1]) if pc[0] in row_sharded else arr

    def wmat(n, l=None, after=None):
        if (n, l) not in weights:
            finish_chunk(chunk_of[(n, l)], after)
        arr = weights[(n, l)]
        return arr, arr.shape[0]

    finish_chunk(0, token)
    small_full = {}
    for n, arr4 in zip(small_sharded, _unpack(weights[("small", None)], [w_in[n].shape for n in small_sharded], lead=(N_CHIPS,))):
        shp = w_in[n].shape
        small_full[n] = jnp.moveaxis(arr4, 0, -2).reshape(*shp[:-1], N_CHIPS * shp[-1])

    x2d = x.reshape(t, d)
    saved = []
    (h1,) = resid_norm_fwd(x2d, None, None, [_row(mix_pre_g[0])], name="norm_in", dep=token)
    xcur = x2d
    kv_state = None
    for i in range(DEPTH):
        sv = {"x_in": xcur, "h1": h1}
        if i < N_A:
            z = mm_nn(h1, *wmat("cm_w_in", i, h1), 1, 0, bias=_row(small_full["cm_b_in"][i]), name=f"cm_in{i}")
            u2 = glu_conv_fwd(z.reshape(bsz, seq, 2 * d), small_full["cm_dw"][i], _row(small_full["cm_dw_b"][i]),
                              name=f"glu_conv{i}").reshape(t, d)
            u4 = ln_silu_fwd(u2, _row(small_full["cm_ln_g"][i]), _row(small_full["cm_ln_b"][i]), name=f"ln_silu{i}")
            y = mm_nn(u4, *wmat("cm_w_out", i, u4), 1, 0, bias=_row(small_full["cm_b_out"][i]), out_dtype=BF16,
                      name=f"cm_out{i}")
            sv.update(z=z, u2=u2, u4=u4)
        else:
            j = i - N_A
            q = mm_nn(h1, *wmat("w_q", j, h1), 1, 0, name=f"q_proj{j}", head_major=True).reshape(-1, bsz, seq, HEAD_DIM)
            outs, lses = [], []
            for g, dil in enumerate(DILATIONS):
                o_g, l_g = attn_fwd(q, kv_state["kv"], g, dil, hw, name=f"attn_fwd{j}_{g}")
                outs.append(o_g.reshape(t, hw))
                lses.append(l_g.reshape(t, LANES))
            merged, lsej = attn_merge(outs, lses, name=f"attn_merge{j}")
            y = mm_nn(merged, *wmat("w_o", j, merged), 1, 0, out_dtype=BF16, name=f"o_proj{j}")
            sv.update(q=q, merged=merged, lsej=lsej)
        x1, h2 = resid_norm_fwd(xcur, y, _row(mix_post_g[i]), [_row(ffn_pre_g[i])], name=f"resid_mix{i}")
        p = mm_nn(h2, *wmat("ffn_w_in", i, h2), 1, 0, out_dtype=BF16, name=f"ffn_in{i}")
        s_act = ffn_mid_fwd(p.reshape(bsz, seq, f2), small_full["ffn_dw"][i], _row(ffn_dw_b[i]), name=f"ffn_mid{i}").reshape(t, f)
        y2 = mm_nn(s_act, *wmat("ffn_w_out", i), 1, 0, out_dtype=BF16, name=f"ffn_out{i}")
        next_gains = []
        if i + 1 < DEPTH:
            next_gains.append(_row(mix_pre_g[i + 1]))
        if i == N_A - 1:
            next_gains.append(_row(kv_norm_g))
        res = resid_norm_fwd(x1, y2, _row(ffn_post_g[i]), next_gains, name=f"resid_ffn{i}")
        sv.update(y=y, x1=x1, h2=h2, p=p, s=s_act, y2=y2)
        saved.append(sv)
        xcur = res[0]
        if i + 1 < DEPTH:
            h1 = res[1]
        if i == N_A - 1:
            kvn = res[2]
            kv = mm_nn(kvn, *wmat("w_kv", None, kvn), 1, 0, name="kv_proj", head_major=True).reshape(-1, bsz, seq, HEAD_DIM)
            kv_state = {"kv": kv, "kvn": kvn, "x_a": xcur}

    dx, loss_tile = loss_fwd_bwd(xcur, loss_target.reshape(t, d))
    loss = lax.psum(loss_tile[0, 0], ("x", "y", "c"))

    gsm = {n: [None] * w_in[n].shape[0] for n in
           ["mix_pre_g", "mix_post_g", "ffn_pre_g", "ffn_post_g", "cm_b_in", "cm_dw", "cm_dw_b", "cm_ln_g", "cm_ln_b",
            "cm_b_out", "ffn_dw", "ffn_dw_b"]}
    gbig = {}
    in_flight = []
    dep = None

    def start_scatter(pcs, tag):
        ssem, rsem, g_f, r_f, tok = scatter_start([gbig[pc] for pc in pcs], name=f"scatter_start_{tag}")
        in_flight.append((pcs, ssem, rsem, g_f, r_f))
        return tok

    dk_buf = dv_buf = None
    for i in range(DEPTH - 1, -1, -1):
        sv = saved[i]
        dy2, dg, _ = norm_bwd(sv["y2"], _row(ffn_post_g[i]), dx, out_dtype=BF16, name=f"bwd_ffn_post{i}", dep=dep)
        gsm["ffn_post_g"][i] = dg
        ds = mm_nt(dy2, *wmat("ffn_w_out", i), 1, 0, out_dtype=BF16, name=f"bwd_ffn_out_dx{i}")
        gbig[("ffn_w_out", i)] = mm_tn(sv["s"], dy2, 1, name=f"bwd_ffn_out_dw{i}").reshape(N_CHIPS, f // N_CHIPS, d)
        dpa, dpg, ddwa, ddwg, ddba, ddbg = ffn_mid_bwd(sv["p"].reshape(bsz, seq, f2), small_full["ffn_dw"][i], _row(ffn_dw_b[i]),
                                                       ds.reshape(bsz, seq, f), name=f"bwd_ffn_mid{i}")
        gsm["ffn_dw"][i] = jnp.concatenate([jnp.sum(ddwa, axis=1), jnp.sum(ddwg, axis=1)], axis=-1)
        gsm["ffn_dw_b"][i] = jnp.concatenate([ddba, ddbg], axis=-1)
        dp = [dpa.reshape(t, f), dpg.reshape(t, f)]
        dh2 = mm_nt(dp, *wmat("ffn_w_in", i), 1, 0, name=f"bwd_ffn_in_dx{i}")
        gbig[("ffn_w_in", i)] = mm_tn(sv["h2"], dp, N_CHIPS, name=f"bwd_ffn_in_dw{i}")
        dx1, dg, _ = norm_bwd(sv["x1"], _row(ffn_pre_g[i]), dh2, add=dx, name=f"bwd_ffn_pre{i}")
        gsm["ffn_pre_g"][i] = dg
        dep = start_scatter([("ffn_w_in", 0), ("ffn_w_out", 0)], "ffn0") if i == 0 else None
        dy, dg, dbias = norm_bwd(sv["y"], _row(mix_post_g[i]), dx1, out_dtype=BF16, name=f"bwd_mix_post{i}", dep=dep)
        gsm["mix_post_g"][i] = dg
        if i < N_A:
            gsm["cm_b_out"][i] = dbias
            du4 = mm_nt(dy, *wmat("cm_w_out", i), 1, 0, name=f"bwd_cm_out_dx{i}")
            gbig[("cm_w_out", i)] = mm_tn(sv["u4"], dy, 1, name=f"bwd_cm_out_dw{i}").reshape(N_CHIPS, d // N_CHIPS, d)
            du2, dlg, dlb = ln_silu_bwd(sv["u2"], _row(small_full["cm_ln_g"][i]), _row(small_full["cm_ln_b"][i]), du4,
                                        name=f"bwd_ln_silu{i}")
            gsm["cm_ln_g"][i], gsm["cm_ln_b"][i] = dlg, dlb
            dza, dzg, ddw, ddwb, dba, dbg = glu_conv_bwd(sv["z"].reshape(bsz, seq, 2 * d), small_full["cm_dw"][i],
                                                         du2.reshape(bsz, seq, d), name=f"bwd_glu_conv{i}")
            gsm["cm_dw"][i] = jnp.sum(ddw, axis=1)
            gsm["cm_dw_b"][i] = ddwb
            gsm["cm_b_in"][i] = jnp.concatenate([dba, dbg], axis=-1)
            dz = [dza.reshape(t, d), dzg.reshape(t, d)]
            dh1 = mm_nt(dz, *wmat("cm_w_in", i), 1, 0, name=f"bwd_cm_in_dx{i}")
            gbig[("cm_w_in", i)] = mm_tn(sv["h1"], dz, N_CHIPS, name=f"bwd_cm_in_dw{i}")
        else:
            j = i - N_A
            dmerged = mm_nt(dy, *wmat("w_o", j), 1, 0, name=f"bwd_o_proj_dx{j}")
            gbig[("w_o", j)] = mm_tn(sv["merged"], dy, 1, name=f"bwd_o_proj_dw{j}").reshape(N_CHIPS, hw // N_CHIPS, d)
            dmt = attn_bwd_prep(dmerged, sv["merged"], name=f"bwd_attn_prep{j}")
            dq_buf = None
            add_to_kv = dk_buf is not None
            for g, dil in enumerate(DILATIONS):
                dq_buf, dk_buf, dv_buf = attn_bwd(
                    sv["q"], kv_state["kv"], g, dil, dmerged.reshape(bsz, seq, hw), sv["lsej"].reshape(bsz, seq, LANES),
                    dmt.reshape(bsz, seq, LANES), dq_buf, dk_buf, dv_buf, add_to_kv, hw, name=f"attn_bwd{j}_{g}")
            dq = dq_buf.reshape(t, qw)
            dh1 = mm_nt(dq, *wmat("w_q", j), 1, 0, name=f"bwd_q_proj_dx{j}")
            gbig[("w_q", j)] = mm_tn(sv["h1"], dq, N_CHIPS, name=f"bwd_q_proj_dw{j}")
        dx, dg, _ = norm_bwd(sv["x_in"], _row(mix_pre_g[i]), dh1, add=dx1, name=f"bwd_mix_pre{i}")
        gsm["mix_pre_g"][i] = dg
        if i > N_A:
            dep = start_scatter([("ffn_w_in", i), ("ffn_w_out", i), ("w_q", i - N_A), ("w_o", i - N_A)], f"l{i}")
        elif 0 < i < N_A:
            dep = start_scatter([("ffn_w_in", i), ("ffn_w_out", i), ("cm_w_in", i), ("cm_w_out", i)], f"l{i}")
        elif i == 0:
            last_token = start_scatter([("cm_w_in", 0), ("cm_w_out", 0)], "cm0")
        if i == N_A:
            dkv = [dk_buf.reshape(t, qw), dv_buf.reshape(t, qw)]
            dkvn = mm_nt(dkv, *wmat("w_kv"), 1, 0, name="bwd_kv_proj_dx")
            gbig[("w_kv", None)] = mm_tn(kv_state["kvn"], dkv, N_CHIPS, name="bwd_kv_proj_dw")
            dx, dg_kv, _ = norm_bwd(kv_state["x_a"], _row(kv_norm_g), dkvn, add=dx, name="bwd_kv_norm")
            dep = start_scatter([("ffn_w_in", i), ("ffn_w_out", i), ("w_q", 0), ("w_o", 0), ("w_kv", None)], f"l{i}")
    grad_x = dx.reshape(bsz, seq, d)

    plane_of = {}
    outs_g, outs_d, outs_m, outs_v = {}, {}, {}, {}

    def finish_scatter(k, after):
        pcs, ssem, rsem, g_f, r_f = in_flight[k]
        g_done, r_done = scatter_wait(ssem, rsem, g_f, r_f, after, name=f"scatter_wait{k}")
        for pc, g_arr, r_arr in zip(pcs, g_done, r_done):
            plane_of[pc] = sum_parts(g_arr, r_arr, me_arr, name=f"sum_chips_{pc[0]}_{pc[1]}")

    def update(group, tag, after):
        plane = []
        for n in group:
            if w_in[n].ndim == 2:
                plane.append(plane_of[(n, None)])
            else:
                plane.append(jnp.concatenate([plane_of[(n, l)] for l in range(w_in[n].shape[0])], axis=0))
        ssems, rsems, plane_f, land_f, _ = swap_start(plane, name=f"swap_start_{tag}")
        for k, n in enumerate(group):
            p_mine, p_other = swap_wait(ssems[k], rsems[k], plane_f[k], land_f[k], after, name=f"swap_wait_{n}")
            shp = w_in[n].shape
            flat = lambda a: a.reshape(-1, shp[-1])
            g_, d_, m_, v_ = adamw(flat(w_in[n]), flat(m_in[n]), flat(v_in[n]), [p_mine, p_other], name=f"adamw_{n}")
            outs_g[n], outs_d[n], outs_m[n], outs_v[n] = (a.reshape(shp) for a in (g_, d_, m_, v_))
            after = v_
        return after

    small_names = [n for n in names if n not in big]
    small_shapes_full = {}
    small_grads_full = []
    for n in small_names:
        if n == "kv_norm_g":
            gfull = dg_kv.reshape(-1)
        elif n in ("cm_dw", "ffn_dw"):
            gfull = jnp.stack(gsm[n], axis=0)
        else:
            gfull = jnp.stack([a.reshape(-1) for a in gsm[n]], axis=0)
        small_shapes_full[n] = gfull.shape
        small_grads_full.append(gfull)
    dev_arr = (4 * lax.axis_index("x") + 2 * lax.axis_index("y") + lax.axis_index("c")).astype(jnp.int32).reshape(1)
    small_land = cast_place(_pack(small_grads_full) + last_token[0, 0], None, dev_arr, F32, name="place_small_grads",
                            nslots=N_DEV)
    sm_send, sm_recv, small_land, small_token = small_start(small_land)

    for k in range(len(in_flight) - 1):
        finish_scatter(k, small_token)
    done = update(["w_kv", "w_q", "w_o", "ffn_w_in", "ffn_w_out"], "a", small_token)
    finish_scatter(len(in_flight) - 1, done)
    done = update(["cm_w_in", "cm_w_out"], "b", done)

    summed = sum_slots(small_wait(sm_send, sm_recv, small_land, done))
    g_full = dict(zip(small_names, _unpack(summed, [small_shapes_full[n] for n in small_names])))
    g_loc = {}
    for n in small_names:
        if n in small_sharded:
            width = w_in[n].shape[-1]
            g_loc[n] = lax.dynamic_slice_in_dim(g_full[n], me_chip * width, width, axis=g_full[n].ndim - 1)
        else:
            g_loc[n] = g_full[n]
    res = adamw(_pack([w_in[n] for n in small_names]), _pack([m_in[n] for n in small_names]),
                _pack([v_in[n] for n in small_names]), [_pack([g_loc[n] for n in small_names])], name="adamw_small")
    shapes_loc = [w_in[n].shape for n in small_names]
    for dst, packed in zip((outs_g, outs_d, outs_m, outs_v), res):
        for n, a in zip(small_names, _unpack(packed, shapes_loc)):
            dst[n] = a

    return (loss, grad_x, *[outs_g[n] for n in names], *[outs_d[n] for n in names],
            *[outs_m[n] for n in names], *[outs_v[n] for n in names])
```

```python
import functools
import math

import jax
import jax.numpy as jnp
from jax import lax
from jax.experimental import pallas as pl
from jax.experimental.pallas import tpu as pltpu

F32 = jnp.float32
BF16 = jnp.bfloat16
EPS = 1e-6
NEG_INF = -1e30
N_A = 2
DEPTH = 4
N_GROUPS = 3
DILATIONS = (1, 4, 16)
HEAD_DIM = 128
BLK = 128
LANES = 128
N_CHIPS = 4
N_DEV = 8
VMEM_LIMIT_V7X = 56 * 1024 * 1024

ADAM_LR = 0.001
ADAM_B1 = 0.9
ADAM_B2 = 0.999
ADAM_EPS = 1e-08
ADAM_WD = 0.01
ADAM_STEP = 10

MESH = pl.DeviceIdType.MESH


def _cp(*sem, **kw):
    return pltpu.CompilerParams(dimension_semantics=sem if sem else None, vmem_limit_bytes=VMEM_LIMIT_V7X, **kw)


def _dot(a, b):
    return jnp.dot(a, b, preferred_element_type=F32)


def _dot_nt(a, b):
    return lax.dot_general(a, b, (((1,), (1,)), ((), ())), preferred_element_type=F32)


def _dot_tn(a, b):
    return lax.dot_general(a, b, (((0,), (0,)), ((), ())), preferred_element_type=F32)


def _sigmoid(x):
    return 1.0 / (1.0 + jnp.exp(-x))


def _row_tile(n, want):
    if n <= want:
        return n
    for t in range(want - want % 8, 7, -8):
        if n % t == 0:
            return t
    raise ValueError(f"no row tile for {n} rows")


def mm_nn(a, w, nsh, stride, layer, bias=None, out_dtype=F32, name="mm_nn", tm=1024, head_major=False):
    m, k = a.shape
    _, k2, ns = w.shape
    assert k == k2
    tm = _row_tile(m, tm)
    has_bias = bias is not None
    hps = ns // HEAD_DIM

    def body(*refs):
        if has_bias:
            a_ref, w_ref, b_ref, o_ref = refs
        else:
            a_ref, w_ref, o_ref = refs
        acc = _dot(a_ref[...].astype(BF16), w_ref[...])
        if has_bias:
            acc = acc + b_ref[...]
        if head_major:
            for hh in range(hps):
                o_ref[hh] = acc[:, hh * HEAD_DIM:(hh + 1) * HEAD_DIM].astype(out_dtype)
        else:
            o_ref[...] = acc.astype(out_dtype)

    in_specs = [
        pl.BlockSpec((tm, k), lambda j, i: (i, 0)),
        pl.BlockSpec((None, k, ns), lambda j, i: (j * stride + layer, 0, 0)),
    ]
    args = [a, w]
    if has_bias:
        in_specs.append(pl.BlockSpec((1, ns), lambda j, i: (0, j)))
        args.append(bias)
    return pl.pallas_call(
        body,
        name=name,
        grid=(nsh, m // tm),
        in_specs=in_specs,
        out_specs=(pl.BlockSpec((hps, tm, HEAD_DIM), lambda j, i: (j, i, 0)) if head_major
                   else pl.BlockSpec((tm, ns), lambda j, i: (i, j))),
        out_shape=jax.ShapeDtypeStruct((nsh * hps, m, HEAD_DIM) if head_major else (m, nsh * ns), out_dtype),
        compiler_params=_cp("parallel", "parallel"),
    )(*args)


def mm_nt(dy, w, nsh, stride, layer, out_dtype=F32, name="mm_nt", tm=1024):
    dys = list(dy) if isinstance(dy, (list, tuple)) else [dy]
    npart = len(dys)
    per = nsh // npart
    m = dys[0].shape[0]
    _, k, ns = w.shape
    assert all(d.shape == (m, per * ns) for d in dys)
    tm = _row_tile(m, tm)

    def body(*refs):
        dy_refs = refs[:npart]
        w_ref, o_ref, acc_ref = refs[npart:]
        j = pl.program_id(1)

        @pl.when(j == 0)
        def _():
            acc_ref[...] = jnp.zeros(acc_ref.shape, F32)

        for pi in range(npart):
            @pl.when(j // per == pi)
            def _(pi=pi):
                acc_ref[...] += _dot_nt(dy_refs[pi][...].astype(BF16), w_ref[...])

        @pl.when(j == nsh - 1)
        def _():
            o_ref[...] = acc_ref[...].astype(out_dtype)

    dy_specs = [pl.BlockSpec((tm, ns), lambda i, j, pi=pi: (i, jnp.clip(j - pi * per, 0, per - 1))) for pi in range(npart)]
    return pl.pallas_call(
        body,
        name=name,
        grid=(m // tm, nsh),
        in_specs=[*dy_specs, pl.BlockSpec((None, k, ns), lambda i, j: (j * stride + layer, 0, 0))],
        out_specs=pl.BlockSpec((tm, k), lambda i, j: (i, 0)),
        out_shape=jax.ShapeDtypeStruct((m, k), out_dtype),
        scratch_shapes=[pltpu.VMEM((tm, k), F32)],
        compiler_params=_cp("parallel", "arbitrary"),
    )(*dys, w)


def mm_tn(a, dy, nsh, name="mm_tn", tm=1024):
    dys = list(dy) if isinstance(dy, (list, tuple)) else [dy]
    npart = len(dys)
    per = nsh // npart
    m, k = a.shape
    ns = dys[0].shape[1] // per
    assert all(d.shape == (m, per * ns) for d in dys)
    tm = _row_tile(m, tm)
    nt = m // tm

    def body(*refs):
        a_ref = refs[0]
        dy_refs = refs[1:1 + npart]
        o_ref, acc_ref = refs[1 + npart:]
        j = pl.program_id(0)
        i = pl.program_id(1)

        @pl.when(i == 0)
        def _():
            acc_ref[...] = jnp.zeros(acc_ref.shape, F32)

        for pi in range(npart):
            @pl.when(j // per == pi)
            def _(pi=pi):
                acc_ref[...] += _dot_tn(a_ref[...].astype(BF16), dy_refs[pi][...].astype(BF16))

        @pl.when(i == nt - 1)
        def _():
            o_ref[...] = acc_ref[...].astype(BF16)

    dy_specs = [
        pl.BlockSpec((tm, ns), lambda j, i, pi=pi: (jnp.where(j // per == pi, i, 0), jnp.clip(j - pi * per, 0, per - 1)))
        for pi in range(npart)
    ]
    return pl.pallas_call(
        body,
        name=name,
        grid=(nsh, nt),
        in_specs=[pl.BlockSpec((tm, k), lambda j, i: (i, 0)), *dy_specs],
        out_specs=pl.BlockSpec((None, k, ns), lambda j, i: (j, 0, 0)),
        out_shape=jax.ShapeDtypeStruct((nsh, k, ns), BF16),
        scratch_shapes=[pltpu.VMEM((k, ns), F32)],
        compiler_params=_cp("parallel", "arbitrary"),
    )(a, *dys)


DEP_SPEC_SHAPE = (8, LANES)


def resid_norm_fwd(x, y, g_post, next_gains, name, tm=256, dep=None):
    t, d = x.shape
    tm = _row_tile(t, tm)
    has_y = y is not None
    n_next = len(next_gains)
    n_dep = 0 if dep is None else 1

    def body(*refs):
        x_ref = refs[0]
        pos = 1
        if has_y:
            y_ref, gp_ref = refs[1], refs[2]
            pos = 3
        gn_refs = refs[pos:pos + n_next]
        outs = refs[pos + n_next + n_dep:]
        xv = x_ref[...]
        o = 0
        if has_y:
            yv = y_ref[...].astype(F32)
            r = lax.rsqrt(jnp.mean(yv * yv, axis=-1, keepdims=True) + EPS)
            xv = xv + (yv * r) * gp_ref[...]
            outs[0][...] = xv
            o = 1
        if n_next:
            xn = xv * lax.rsqrt(jnp.mean(xv * xv, axis=-1, keepdims=True) + EPS)
            for k in range(n_next):
                outs[o + k][...] = (xn * gn_refs[k][...]).astype(BF16)

    row = pl.BlockSpec((tm, d), lambda i: (i, 0))
    vec = pl.BlockSpec((1, d), lambda i: (0, 0))
    args, in_specs = [x], [row]
    if has_y:
        args += [y, g_post]
        in_specs += [row, vec]
    args += list(next_gains)
    in_specs += [vec] * n_next
    if n_dep:
        args.append(dep)
        in_specs.append(pl.BlockSpec(DEP_SPEC_SHAPE, lambda i: (0, 0)))
    out_shape, out_specs = [], []
    if has_y:
        out_shape.append(jax.ShapeDtypeStruct((t, d), F32))
        out_specs.append(row)
    for _ in range(n_next):
        out_shape.append(jax.ShapeDtypeStruct((t, d), BF16))
        out_specs.append(row)
    return pl.pallas_call(
        body, name=name, grid=(t // tm,), in_specs=in_specs, out_specs=out_specs, out_shape=out_shape,
        compiler_params=_cp("parallel"),
    )(*args)


def norm_bwd(x, g, dy, add=None, out_dtype=F32, name="norm_bwd", tm=256, dep=None):
    t, d = x.shape
    tm = _row_tile(t, tm)
    has_add = add is not None

    def body(*refs):
        x_ref, g_ref, dy_ref = refs[:3]
        add_ref = refs[3] if has_add else None
        dx_ref, dg_ref, cs_ref = refs[-3:]
        i = pl.program_id(0)
        xv = x_ref[...].astype(F32)
        dyv = dy_ref[...].astype(F32)
        r = lax.rsqrt(jnp.mean(xv * xv, axis=-1, keepdims=True) + EPS)
        gd = dyv * g_ref[...]
        dx = r * gd - xv * ((r * r * r) * jnp.mean(xv * gd, axis=-1, keepdims=True))
        if has_add:
            dx = dx + add_ref[...]
        dx_ref[...] = dx.astype(out_dtype)
        dg = jnp.sum(dyv * (xv * r), axis=0, keepdims=True)
        cs = jnp.sum(dx, axis=0, keepdims=True)

        @pl.when(i == 0)
        def _():
            dg_ref[...] = dg
            cs_ref[...] = cs

        @pl.when(i > 0)
        def _():
            dg_ref[...] += dg
            cs_ref[...] += cs

    row = pl.BlockSpec((tm, d), lambda i: (i, 0))
    vec = pl.BlockSpec((1, d), lambda i: (0, 0))
    args, in_specs = [x, g, dy], [row, vec, row]
    if has_add:
        args.append(add)
        in_specs.append(row)
    if dep is not None:
        args.append(dep)
        in_specs.append(pl.BlockSpec(DEP_SPEC_SHAPE, lambda i: (0, 0)))
    return pl.pallas_call(
        body, name=name, grid=(t // tm,), in_specs=in_specs,
        out_specs=[row, vec, vec],
        out_shape=[jax.ShapeDtypeStruct((t, d), out_dtype), jax.ShapeDtypeStruct((1, d), F32),
                   jax.ShapeDtypeStruct((1, d), F32)],
        compiler_params=_cp("arbitrary"),
    )(*args)


def loss_fwd_bwd(x, target, name="loss", tm=256):
    t, d = x.shape
    tm = _row_tile(t, tm)

    def body(x_ref, t_ref, dx_ref, l_ref):
        i = pl.program_id(0)
        err = x_ref[...] - t_ref[...]
        dx_ref[...] = err * (1.0 / d)
        part = 0.5 * jnp.sum(jnp.mean(err * err, axis=-1, keepdims=True), axis=0, keepdims=True)
        part = jnp.broadcast_to(part, l_ref.shape)

        @pl.when(i == 0)
        def _():
            l_ref[...] = part

        @pl.when(i > 0)
        def _():
            l_ref[...] += part

    row = pl.BlockSpec((tm, d), lambda i: (i, 0))
    return pl.pallas_call(
        body, name=name, grid=(t // tm,), in_specs=[row, row],
        out_specs=[row, pl.BlockSpec((8, LANES), lambda i: (0, 0))],
        out_shape=[jax.ShapeDtypeStruct((t, d), F32), jax.ShapeDtypeStruct((8, LANES), F32)],
        compiler_params=_cp("arbitrary"),
    )(x, target)


CONV_HALO = 32
CONV_CHUNK = 128


def glu_conv_fwd(z, dw, dwb, name, tc=128):
    b, s, c2 = z.shape
    c = c2 // 2
    kw = dw.shape[0]
    tc = min(tc, c)
    nc = c // tc
    ch = min(CONV_CHUNK, s)
    halo = CONV_HALO
    assert kw - 1 <= halo and s % ch == 0

    nch = s // ch

    def body(a_ref, g_ref, w_ref, b_ref, o_ref, pad_ref):
        _fill_glu_slabs(a_ref, g_ref, pad_ref, nch, ch, halo)

        def chunk(ci, carry):
            r0 = pl.multiple_of(ci * ch, ch)
            acc = b_ref[...]
            for k, tap in enumerate(_taps_front(pad_ref, ci, kw, ch, halo)):
                acc = acc + w_ref[k:k + 1, :] * tap
            o_ref[pl.ds(r0, ch), :] = acc
            return carry

        lax.fori_loop(0, nch, chunk, 0)

    return pl.pallas_call(
        body, name=name, grid=(b, nc),
        in_specs=[
            pl.BlockSpec((None, s, tc), lambda bi, i: (bi, 0, i)),
            pl.BlockSpec((None, s, tc), lambda bi, i: (bi, 0, i + nc)),
            pl.BlockSpec((kw, tc), lambda bi, i: (0, i)),
            pl.BlockSpec((1, tc), lambda bi, i: (0, i)),
        ],
        out_specs=pl.BlockSpec((None, s, tc), lambda bi, i: (bi, 0, i)),
        out_shape=jax.ShapeDtypeStruct((b, s, c), F32),
        scratch_shapes=[pltpu.VMEM((nch, ch + halo, tc), F32)],
        compiler_params=_cp("parallel", "parallel"),
    )(z, z, dw, dwb)


def glu_conv_bwd(z, dw, du2, name, tc=128):
    b, s, c2 = z.shape
    c = c2 // 2
    kw = dw.shape[0]
    tc = min(tc, c)
    nc = c // tc
    ch = min(CONV_CHUNK, s)
    nch = s // ch
    halo = CONV_HALO

    def body(a_ref, g_ref, w_ref, du_ref, dza_ref, dzg_ref, ddw_ref, ddwb_ref, dba_ref, dbg_ref, upad_ref, dpad_ref):
        bi = pl.program_id(1)

        @pl.when(bi == 0)
        def _():
            ddw_ref[...] = jnp.zeros(ddw_ref.shape, F32)
            ddwb_ref[...] = jnp.zeros(ddwb_ref.shape, F32)
            dba_ref[...] = jnp.zeros(dba_ref.shape, F32)
            dbg_ref[...] = jnp.zeros(dbg_ref.shape, F32)

        _fill_glu_slabs(a_ref, g_ref, upad_ref, nch, ch, halo)
        dpad_ref[nch - 1, ch:ch + halo, :] = jnp.zeros((halo, tc), F32)
        dpad_ref[nch - 1, 0:ch, :] = du_ref[s - ch:s, :]

        def fill(ci, carry):
            r0 = pl.multiple_of(ci * ch, ch)
            dpad_ref[ci, :, :] = du_ref[pl.ds(r0, ch + halo), :]
            return carry

        lax.fori_loop(0, nch - 1, fill, 0)

        def chunk(ci, carry):
            r0 = pl.multiple_of(ci * ch, ch)
            du_c = du_ref[pl.ds(r0, ch), :]
            taps_u = _taps_front(upad_ref, ci, kw, ch, halo)
            taps_d = _taps_at(dpad_ref, ci, list(range(kw)), ch)
            du1 = w_ref[kw - 1:kw, :] * du_c
            ddw_ref[kw - 1] += jnp.sum((du_c * taps_u[kw - 1]).reshape(ch // 8, 8, tc), axis=0)
            for j in range(1, kw):
                du1 = du1 + w_ref[kw - 1 - j:kw - j, :] * taps_d[j]
                ddw_ref[kw - 1 - j] += jnp.sum((du_c * taps_u[kw - 1 - j]).reshape(ch // 8, 8, tc), axis=0)
            av = a_ref[pl.ds(r0, ch), :]
            sg = _sigmoid(g_ref[pl.ds(r0, ch), :])
            dza = du1 * sg
            dzg = du1 * av * (sg * (1.0 - sg))
            dza_ref[pl.ds(r0, ch), :] = dza.astype(BF16)
            dzg_ref[pl.ds(r0, ch), :] = dzg.astype(BF16)
            dba_ref[...] += jnp.sum(dza, axis=0, keepdims=True)
            dbg_ref[...] += jnp.sum(dzg, axis=0, keepdims=True)
            ddwb_ref[...] += jnp.sum(du_c, axis=0, keepdims=True)
            return carry

        lax.fori_loop(0, s // ch, chunk, 0)

    blk = lambda off: pl.BlockSpec((None, s, tc), lambda i, bi: (bi, 0, i + off))
    vec = pl.BlockSpec((1, tc), lambda i, bi: (0, i))
    return pl.pallas_call(
        body, name=name, grid=(nc, b),
        in_specs=[blk(0), blk(nc), pl.BlockSpec((kw, tc), lambda i, bi: (0, i)), blk(0)],
        out_specs=[blk(0), blk(0), pl.BlockSpec((kw, 8, tc), lambda i, bi: (0, 0, i)), vec, vec, vec],
        out_shape=[
            jax.ShapeDtypeStruct((b, s, c), BF16), jax.ShapeDtypeStruct((b, s, c), BF16),
            jax.ShapeDtypeStruct((kw, 8, c), F32), jax.ShapeDtypeStruct((1, c), F32),
            jax.ShapeDtypeStruct((1, c), F32), jax.ShapeDtypeStruct((1, c), F32),
        ],
        scratch_shapes=[pltpu.VMEM((nch, ch + halo, tc), F32), pltpu.VMEM((nch, ch + halo, tc), F32)],
        compiler_params=_cp("parallel", "arbitrary"),
    )(z, z, dw, du2)


def _fill_glu_slabs(a_ref, g_ref, pad_ref, nch, ch, halo):
    tc = a_ref.shape[-1]
    pad_ref[0, 0:halo, :] = jnp.zeros((halo, tc), F32)
    pad_ref[0, halo:halo + ch, :] = a_ref[0:ch, :] * _sigmoid(g_ref[0:ch, :])

    def fill(ci, carry):
        r0 = pl.multiple_of(ci * ch, ch)
        pad_ref[ci, 0:halo, :] = pad_ref[ci - 1, ch:ch + halo, :]
        pad_ref[ci, halo:halo + ch, :] = a_ref[pl.ds(r0, ch), :] * _sigmoid(g_ref[pl.ds(r0, ch), :])
        return carry

    lax.fori_loop(1, nch, fill, 0)


def ln_silu_fwd(u, g, bvec, name, tm=256):
    t, d = u.shape
    tm = _row_tile(t, tm)

    def body(u_ref, g_ref, b_ref, o_ref):
        uv = u_ref[...]
        mu = jnp.mean(uv, axis=-1, keepdims=True)
        xc = uv - mu
        var = jnp.mean(xc * xc, axis=-1, keepdims=True)
        v = (xc * lax.rsqrt(var + EPS)) * g_ref[...] + b_ref[...]
        o_ref[...] = (v * _sigmoid(v)).astype(BF16)

    row = pl.BlockSpec((tm, d), lambda i: (i, 0))
    vec = pl.BlockSpec((1, d), lambda i: (0, 0))
    return pl.pallas_call(
        body, name=name, grid=(t // tm,), in_specs=[row, vec, vec], out_specs=row,
        out_shape=jax.ShapeDtypeStruct((t, d), BF16), compiler_params=_cp("parallel"),
    )(u, g, bvec)


def ln_silu_bwd(u, g, bvec, dout, name, tm=256):
    t, d = u.shape
    tm = _row_tile(t, tm)

    def body(u_ref, g_ref, b_ref, do_ref, du_ref, dg_ref, db_ref):
        i = pl.program_id(0)
        uv = u_ref[...]
        mu = jnp.mean(uv, axis=-1, keepdims=True)
        xc = uv - mu
        var = jnp.mean(xc * xc, axis=-1, keepdims=True)
        rstd = lax.rsqrt(var + EPS)
        n = xc * rstd
        v = n * g_ref[...] + b_ref[...]
        sg = _sigmoid(v)
        dv = do_ref[...].astype(F32) * (sg * (1.0 + v * (1.0 - sg)))
        dn = dv * g_ref[...]
        du_ref[...] = rstd * (dn - jnp.mean(dn, axis=-1, keepdims=True) - n * jnp.mean(dn * n, axis=-1, keepdims=True))
        dg = jnp.sum(dv * n, axis=0, keepdims=True)
        db = jnp.sum(dv, axis=0, keepdims=True)

        @pl.when(i == 0)
        def _():
            dg_ref[...] = dg
            db_ref[...] = db

        @pl.when(i > 0)
        def _():
            dg_ref[...] += dg
            db_ref[...] += db

    row = pl.BlockSpec((tm, d), lambda i: (i, 0))
    vec = pl.BlockSpec((1, d), lambda i: (0, 0))
    return pl.pallas_call(
        body, name=name, grid=(t // tm,), in_specs=[row, vec, vec, row], out_specs=[row, vec, vec],
        out_shape=[jax.ShapeDtypeStruct((t, d), F32), jax.ShapeDtypeStruct((1, d), F32), jax.ShapeDtypeStruct((1, d), F32)],
        compiler_params=_cp("arbitrary"),
    )(u, g, bvec, dout)


FFN_HALO = 8


def _fill_front_halo(src_ref, pad_ref, nch, ch, halo):
    tc = src_ref.shape[-1]
    pad_ref[0, 0:halo, :] = jnp.zeros((halo, tc), F32)
    pad_ref[0, halo:halo + ch, :] = src_ref[0:ch, :].astype(F32)

    def fill(ci, carry):
        r0 = pl.multiple_of(ci * ch, ch)
        pad_ref[ci, 0:halo, :] = src_ref[pl.ds(r0 - 2 * halo, 2 * halo), :].astype(F32)[halo:, :]
        pad_ref[ci, halo:halo + ch, :] = src_ref[pl.ds(r0, ch), :].astype(F32)
        return carry

    lax.fori_loop(1, nch, fill, 0)


def _taps_at(pad_ref, ci, offsets, ch):
    windows = {}
    for b in sorted({o % 8 for o in offsets}):
        top = max(o for o in offsets if o % 8 == b)
        windows[b] = pad_ref[ci, b:top + ch, :]
    return [windows[o % 8][o - o % 8:o - o % 8 + ch, :] for o in offsets]


def _taps_front(pad_ref, ci, kw, ch, halo):
    return _taps_at(pad_ref, ci, [halo - (kw - 1 - k) for k in range(kw)], ch)


def ffn_mid_fwd(p, dw, dwb, name, tc=256):
    b, s, f2 = p.shape
    f = f2 // 2
    kw = dw.shape[0]
    tc = min(tc, f)
    nf = f // tc
    ch = min(CONV_CHUNK, s)
    nch = s // ch
    halo = FFN_HALO

    def body(pa_ref, pg_ref, wa_ref, wg_ref, ba_ref, bg_ref, o_ref, apad_ref, gpad_ref):
        _fill_front_halo(pa_ref, apad_ref, nch, ch, halo)
        _fill_front_halo(pg_ref, gpad_ref, nch, ch, halo)

        def chunk(ci, carry):
            r0 = pl.multiple_of(ci * ch, ch)
            ca = ba_ref[...]
            cg = bg_ref[...]
            taps = zip(_taps_front(apad_ref, ci, kw, ch, halo), _taps_front(gpad_ref, ci, kw, ch, halo))
            for k, (ta, tg) in enumerate(taps):
                ca = ca + wa_ref[k:k + 1, :] * ta
                cg = cg + wg_ref[k:k + 1, :] * tg
            o_ref[pl.ds(r0, ch), :] = ((cg * _sigmoid(cg)) * ca).astype(BF16)
            return carry

        lax.fori_loop(0, nch, chunk, 0)

    blk = lambda off: pl.BlockSpec((None, s, tc), lambda bi, i: (bi, 0, i + off))
    wsp = lambda off: pl.BlockSpec((kw, tc), lambda bi, i: (0, i + off))
    bsp = lambda off: pl.BlockSpec((1, tc), lambda bi, i: (0, i + off))
    return pl.pallas_call(
        body, name=name, grid=(b, nf),
        in_specs=[blk(0), blk(nf), wsp(0), wsp(nf), bsp(0), bsp(nf)],
        out_specs=pl.BlockSpec((None, s, tc), lambda bi, i: (bi, 0, i)),
        out_shape=jax.ShapeDtypeStruct((b, s, f), BF16),
        scratch_shapes=[pltpu.VMEM((nch, ch + halo, tc), F32)] * 2,
        compiler_params=_cp("parallel", "parallel"),
    )(p, p, dw, dw, dwb, dwb)


def ffn_mid_bwd(p, dw, dwb, ds, name, tc=256):
    b, s, f2 = p.shape
    f = f2 // 2
    kw = dw.shape[0]
    tc = min(tc, f)
    nf = f // tc
    ch = min(CONV_CHUNK, s)
    nch = s // ch
    halo = FFN_HALO

    def sum8(v):
        return jnp.sum(v.reshape(ch // 8, 8, tc), axis=0)

    def body(pa_ref, pg_ref, wa_ref, wg_ref, ba_ref, bg_ref, ds_ref, dpa_ref, dpg_ref, ddwa_ref, ddwg_ref, dba_ref, dbg_ref,
             apad_ref, gpad_ref, dca_ref, dcg_ref):
        bi = pl.program_id(1)

        @pl.when(bi == 0)
        def _():
            ddwa_ref[...] = jnp.zeros(ddwa_ref.shape, F32)
            ddwg_ref[...] = jnp.zeros(ddwg_ref.shape, F32)
            dba_ref[...] = jnp.zeros(dba_ref.shape, F32)
            dbg_ref[...] = jnp.zeros(dbg_ref.shape, F32)

        _fill_front_halo(pa_ref, apad_ref, nch, ch, halo)
        _fill_front_halo(pg_ref, gpad_ref, nch, ch, halo)
        dca_ref[nch - 1, ch:ch + halo, :] = jnp.zeros((halo, tc), F32)
        dcg_ref[nch - 1, ch:ch + halo, :] = jnp.zeros((halo, tc), F32)

        def grads(ci, carry):
            acc_a, acc_g, sb_a, sb_g = carry
            r0 = pl.multiple_of(ci * ch, ch)
            taps_a = _taps_front(apad_ref, ci, kw, ch, halo)
            taps_g = _taps_front(gpad_ref, ci, kw, ch, halo)
            ca = ba_ref[...]
            cg = bg_ref[...]
            for k in range(kw):
                ca = ca + wa_ref[k:k + 1, :] * taps_a[k]
                cg = cg + wg_ref[k:k + 1, :] * taps_g[k]
            sg = _sigmoid(cg)
            dsv = ds_ref[pl.ds(r0, ch), :].astype(F32)
            dca = dsv * (cg * sg)
            dcg = dsv * ca * (sg * (1.0 + cg * (1.0 - sg)))
            dca_ref[ci, 0:ch, :] = dca
            dcg_ref[ci, 0:ch, :] = dcg

            prev = jnp.maximum(ci - 1, 0)

            @pl.when(ci > 0)
            def _():
                dca_ref[prev, ch:ch + halo, :] = dca[0:halo, :]
                dcg_ref[prev, ch:ch + halo, :] = dcg[0:halo, :]

            acc_a = tuple(acc_a[k] + sum8(dca * taps_a[k]) for k in range(kw))
            acc_g = tuple(acc_g[k] + sum8(dcg * taps_g[k]) for k in range(kw))
            return acc_a, acc_g, sb_a + sum8(dca), sb_g + sum8(dcg)

        z8 = jnp.zeros((8, tc), F32)
        acc_a, acc_g, sb_a, sb_g = lax.fori_loop(0, nch, grads, ((z8,) * kw, (z8,) * kw, z8, z8))
        for k in range(kw):
            ddwa_ref[k] += acc_a[k]
            ddwg_ref[k] += acc_g[k]
        dba_ref[...] += jnp.sum(sb_a, axis=0, keepdims=True)
        dbg_ref[...] += jnp.sum(sb_g, axis=0, keepdims=True)

        def back(ci, carry):
            r0 = pl.multiple_of(ci * ch, ch)
            da = wa_ref[kw - 1:kw, :] * dca_ref[ci, 0:ch, :]
            dg = wg_ref[kw - 1:kw, :] * dcg_ref[ci, 0:ch, :]
            for j in range(1, kw):
                da = da + wa_ref[kw - 1 - j:kw - j, :] * dca_ref[ci, j:j + ch, :]
                dg = dg + wg_ref[kw - 1 - j:kw - j, :] * dcg_ref[ci, j:j + ch, :]
            dpa_ref[pl.ds(r0, ch), :] = da.astype(BF16)
            dpg_ref[pl.ds(r0, ch), :] = dg.astype(BF16)
            return carry

        lax.fori_loop(0, nch, back, 0)

    blk = lambda off: pl.BlockSpec((None, s, tc), lambda i, bi: (bi, 0, i + off))
    wsp = lambda off: pl.BlockSpec((kw, tc), lambda i, bi: (0, i + off))
    bsp = lambda off: pl.BlockSpec((1, tc), lambda i, bi: (0, i + off))
    acc3 = pl.BlockSpec((kw, 8, tc), lambda i, bi: (0, 0, i))
    vec = pl.BlockSpec((1, tc), lambda i, bi: (0, i))
    return pl.pallas_call(
        body, name=name, grid=(nf, b),
        in_specs=[blk(0), blk(nf), wsp(0), wsp(nf), bsp(0), bsp(nf), blk(0)],
        out_specs=[blk(0), blk(0), acc3, acc3, vec, vec],
        out_shape=[jax.ShapeDtypeStruct((b, s, f), BF16), jax.ShapeDtypeStruct((b, s, f), BF16),
                   jax.ShapeDtypeStruct((kw, 8, f), F32), jax.ShapeDtypeStruct((kw, 8, f), F32),
                   jax.ShapeDtypeStruct((1, f), F32), jax.ShapeDtypeStruct((1, f), F32)],
        scratch_shapes=[pltpu.VMEM((nch, ch + halo, tc), F32)] * 4,
        compiler_params=_cp("parallel", "arbitrary"),
    )(p, p, dw, dw, dwb, dwb, ds)


def _tile_rows(r, n, dil):
    start = r + n * BLK * dil
    return pl.ds(start, BLK, stride=dil) if dil > 1 else pl.ds(start, BLK)


def _band_masks():
    qi = lax.broadcasted_iota(jnp.int32, (BLK, 2 * BLK), 0)
    kk = lax.broadcasted_iota(jnp.int32, (BLK, 2 * BLK), 1)
    both = jnp.logical_or(jnp.logical_and(kk < BLK, kk >= qi), jnp.logical_and(kk >= BLK, kk - BLK <= qi))
    return both, kk[:, :BLK] <= qi[:, :BLK]


def attn_fwd(q, kv, g, dil, hw, name):
    _, b, s, _ = q.shape
    nh = hw // HEAD_DIM
    nblk = s // dil // BLK
    scale = 1.0 / math.sqrt(HEAD_DIM)

    def body(q_ref, k_ref, v_ref, o_ref, lse_ref):
        h = pl.program_id(1)
        mask2, mask1 = _band_masks()
        mine = lax.broadcasted_iota(jnp.int32, (BLK, LANES), 1) == h

        @pl.when(h == 0)
        def _():
            lse_ref[...] = jnp.zeros(lse_ref.shape, F32)

        for r in range(dil):
            kp = vp = None
            for n in range(nblk):
                rs = _tile_rows(r, n, dil)
                qt = q_ref[rs, :].astype(BF16)
                kc = k_ref[rs, :].astype(BF16)
                vc = v_ref[rs, :].astype(BF16)
                if n == 0:
                    kcat, vcat, mask = kc, vc, mask1
                else:
                    kcat, vcat, mask = jnp.concatenate([kp, kc], axis=0), jnp.concatenate([vp, vc], axis=0), mask2
                sc = jnp.where(mask, _dot_nt(qt, kcat) * scale, NEG_INF)
                m = jnp.max(sc, axis=-1, keepdims=True)
                p = jnp.exp(sc - m)
                den = jnp.sum(p, axis=-1, keepdims=True)
                o_ref[rs, :] = _dot(p.astype(BF16), vcat) / den
                lse_ref[rs, :] = jnp.where(mine, m + jnp.log(den), lse_ref[rs, :])
                kp, vp = kc, vc

    col = lambda base: pl.BlockSpec((None, s, HEAD_DIM), lambda bi, h: (bi, 0, base + h))
    head = lambda base: pl.BlockSpec((None, None, s, HEAD_DIM), lambda bi, h: (base + h, bi, 0, 0))
    return pl.pallas_call(
        body, name=name, grid=(b, nh),
        in_specs=[head(g * nh), head(g * nh), head((N_GROUPS + g) * nh)],
        out_specs=[col(0), pl.BlockSpec((None, s, LANES), lambda bi, h: (bi, 0, 0))],
        out_shape=[jax.ShapeDtypeStruct((b, s, hw), F32), jax.ShapeDtypeStruct((b, s, LANES), F32)],
        compiler_params=_cp("parallel", "arbitrary"),
    )(q, kv, kv)


def attn_merge(outs, lses, name, tm=256):
    t, hw = outs[0].shape
    nh = hw // HEAD_DIM
    tm = _row_tile(t, tm)
    ng = len(outs)

    def body(*refs):
        o_refs, l_refs = refs[:ng], refs[ng:2 * ng]
        m_ref, lj_ref = refs[2 * ng:]
        ls = [l_refs[g][...] for g in range(ng)]
        mx = ls[0]
        for g in range(1, ng):
            mx = jnp.maximum(mx, ls[g])
        es = [jnp.exp(l - mx) for l in ls]
        tot = es[0]
        for g in range(1, ng):
            tot = tot + es[g]
        ws = [e / tot for e in es]
        lj_ref[...] = mx + jnp.log(tot)
        for h in range(nh):
            sl = slice(h * HEAD_DIM, (h + 1) * HEAD_DIM)
            acc = ws[0][:, h:h + 1] * o_refs[0][:, sl]
            for g in range(1, ng):
                acc = acc + ws[g][:, h:h + 1] * o_refs[g][:, sl]
            m_ref[:, sl] = acc.astype(BF16)

    row = pl.BlockSpec((tm, hw), lambda i: (i, 0))
    st = pl.BlockSpec((tm, LANES), lambda i: (i, 0))
    return pl.pallas_call(
        body, name=name, grid=(t // tm,), in_specs=[row] * ng + [st] * ng, out_specs=[row, st],
        out_shape=[jax.ShapeDtypeStruct((t, hw), BF16), jax.ShapeDtypeStruct((t, LANES), F32)],
        compiler_params=_cp("parallel"),
    )(*outs, *lses)


def attn_bwd_prep(dmerged, merged, name, tm=256):
    t, hw = merged.shape
    nh = hw // HEAD_DIM
    tm = _row_tile(t, tm)

    def body(d_ref, m_ref, o_ref):
        lane = lax.broadcasted_iota(jnp.int32, (tm, LANES), 1)
        acc = jnp.zeros((tm, LANES), F32)
        for h in range(nh):
            sl = slice(h * HEAD_DIM, (h + 1) * HEAD_DIM)
            dsum = jnp.sum(d_ref[:, sl] * m_ref[:, sl].astype(F32), axis=-1, keepdims=True)
            acc = jnp.where(lane == h, dsum, acc)
        o_ref[...] = acc

    row = pl.BlockSpec((tm, hw), lambda i: (i, 0))
    return pl.pallas_call(
        body, name=name, grid=(t // tm,), in_specs=[row, row], out_specs=pl.BlockSpec((tm, LANES), lambda i: (i, 0)),
        out_shape=jax.ShapeDtypeStruct((t, LANES), F32), compiler_params=_cp("parallel"),
    )(dmerged, merged)


def attn_bwd(q, kv, g, dil, do, lsej, dm, dq_buf, dk_buf, dv_buf, accumulate, hw, name):
    _, b, s, _ = q.shape
    nh = hw // HEAD_DIM
    nblk = s // dil // BLK
    scale = 1.0 / math.sqrt(HEAD_DIM)
    assert dk_buf is not None or not accumulate
    kv_at = 6 + (dq_buf is not None)

    def body(*refs):
        q_ref, k_ref, v_ref, do_ref, lj_ref, dm_ref = refs[:6]
        dq_ref, dk_ref, dv_ref = refs[-3:]
        dki_ref, dvi_ref = (refs[kv_at], refs[kv_at + 1]) if accumulate else (None, None)
        mask2, mask1 = _band_masks()
        mine = lax.broadcasted_iota(jnp.int32, (BLK, LANES), 1) == pl.program_id(1)

        def my_lane(v):
            return jnp.sum(jnp.where(mine, v, 0.0), axis=-1, keepdims=True)

        def put(rs, dk, dv):
            if accumulate:
                dk = dk + dki_ref[rs, :]
                dv = dv + dvi_ref[rs, :]
            dk_ref[rs, :] = dk
            dv_ref[rs, :] = dv

        for r in range(dil):
            kp = vp = hold_k = hold_v = rs_prev = None
            for n in range(nblk):
                rs = _tile_rows(r, n, dil)
                qt = q_ref[rs, :].astype(BF16)
                kc = k_ref[rs, :].astype(BF16)
                vc = v_ref[rs, :].astype(BF16)
                dot = do_ref[rs, :].astype(BF16)
                lm = my_lane(lj_ref[rs, :])
                dmm = my_lane(dm_ref[rs, :])
                if n == 0:
                    kcat, vcat, mask = kc, vc, mask1
                else:
                    kcat, vcat, mask = jnp.concatenate([kp, kc], axis=0), jnp.concatenate([vp, vc], axis=0), mask2
                p = jnp.exp(jnp.where(mask, _dot_nt(qt, kcat) * scale, NEG_INF) - lm)
                ds = (p * (_dot_nt(dot, vcat) - dmm)).astype(BF16)
                dq_ref[rs, :] = _dot(ds, kcat) * scale
                dkc = _dot_tn(ds, qt) * scale
                dvc = _dot_tn(p.astype(BF16), dot)
                if n > 0:
                    put(rs_prev, hold_k + dkc[:BLK, :], hold_v + dvc[:BLK, :])
                    dkc, dvc = dkc[BLK:, :], dvc[BLK:, :]
                hold_k, hold_v, kp, vp, rs_prev = dkc, dvc, kc, vc, rs
            put(rs_prev, hold_k, hold_v)

    col = lambda base: pl.BlockSpec((None, s, HEAD_DIM), lambda bi, h: (bi, 0, base + h))
    any_spec = pl.BlockSpec(memory_space=pl.ANY)
    stat = pl.BlockSpec((None, s, LANES), lambda bi, h: (bi, 0, 0))
    head = lambda base: pl.BlockSpec((None, None, s, HEAD_DIM), lambda bi, h: (base + h, bi, 0, 0))
    in_specs = [head(g * nh), head(g * nh), head((N_GROUPS + g) * nh), col(0), stat, stat]
    args = [q, kv, kv, do, lsej, dm]
    aliases = {}
    if dq_buf is not None:
        in_specs.append(any_spec)
        args.append(dq_buf)
        aliases[6] = 0
    if dk_buf is not None:
        in_specs += [col(g * nh) if accumulate else any_spec] * 2
        args += [dk_buf, dv_buf]
        aliases.update({kv_at: 1, kv_at + 1: 2})
    shape = jax.ShapeDtypeStruct((b, s, N_GROUPS * hw), F32)
    return pl.pallas_call(
        body, name=name, grid=(b, nh), in_specs=in_specs, out_specs=[col(g * nh)] * 3, out_shape=[shape] * 3,
        input_output_aliases=aliases, compiler_params=_cp("parallel", "parallel"),
    )(*args)


def sum_parts(g, recv, me, name, tm=256):
    _, rows, c = g.shape
    n = recv.shape[0]
    tm = _row_tile(rows, tm)

    def body(me_ref, g_ref, r_ref, o_ref):
        acc = g_ref[...].astype(F32)
        for j in range(n):
            acc = acc + r_ref[j].astype(F32)
        o_ref[...] = acc

    return pl.pallas_call(
        body, name=name,
        grid_spec=pltpu.PrefetchScalarGridSpec(
            num_scalar_prefetch=1, grid=(rows // tm,),
            in_specs=[pl.BlockSpec((None, tm, c), lambda i, me_ref: (me_ref[0], i, 0)),
                      pl.BlockSpec((n, tm, c), lambda i, me_ref: (0, i, 0))],
            out_specs=pl.BlockSpec((tm, c), lambda i, me_ref: (i, 0))),
        out_shape=jax.ShapeDtypeStruct((rows, c), F32), compiler_params=_cp("parallel"),
    )(me, g, recv)


def adamw(w, m, v, g_parts, name, tm=256):
    rows, c = w.shape
    tm = _row_tile(rows, tm)
    npart = len(g_parts)

    def body(*refs):
        w_ref, m_ref, v_ref = refs[:3]
        g_refs = refs[3:3 + npart]
        go_ref, d_ref, mo_ref, vo_ref = refs[3 + npart:]
        g = g_refs[0][...]
        for k in range(1, npart):
            g = g + g_refs[k][...]
        mn = ADAM_B1 * m_ref[...] + (1.0 - ADAM_B1) * g
        vn = ADAM_B2 * v_ref[...] + (1.0 - ADAM_B2) * (g * g)
        m_hat = mn / (1.0 - ADAM_B1 ** ADAM_STEP)
        v_hat = vn / (1.0 - ADAM_B2 ** ADAM_STEP)
        go_ref[...] = g
        d_ref[...] = -ADAM_LR * (m_hat / (jnp.sqrt(v_hat) + ADAM_EPS) + ADAM_WD * w_ref[...])
        mo_ref[...] = mn
        vo_ref[...] = vn

    row = pl.BlockSpec((tm, c), lambda i: (i, 0))
    return pl.pallas_call(
        body, name=name, grid=(rows // tm,), in_specs=[row] * (3 + npart), out_specs=[row] * 4,
        out_shape=[jax.ShapeDtypeStruct((rows, c), F32)] * 4, compiler_params=_cp("parallel"),
    )(w, m, v, *g_parts)


def _place():
    return lax.axis_index("x"), lax.axis_index("y"), lax.axis_index("c")


def _other_chips(x, y, c):
    return [(1 - x, y, c), (x, 1 - y, c), (1 - x, 1 - y, c)]


def _chip_of(px, py):
    return 2 * px + py


HBM_SPEC = pl.BlockSpec(memory_space=pltpu.HBM)
SEM_SPEC = pl.BlockSpec(memory_space=pltpu.SEMAPHORE)
ANY_SPEC = pl.BlockSpec(memory_space=pl.ANY)
DATAFLOW = pltpu.SideEffectType.DATAFLOW_SIDE_EFFECTING
N_PEER_CHIPS = N_CHIPS - 1


def _hbm(a):
    return pltpu.with_memory_space_constraint(a, pltpu.HBM)


def _hbm_like(arrays):
    return [pltpu.HBM(a.shape, a.dtype) for a in arrays]


def cast_place(w, layer, me, out_dtype, name, tm=256, nslots=N_CHIPS, dep=None):
    rows, c = w.shape[-2:]
    tm = _row_tile(rows, tm)

    def body(me_ref, w_ref, *rest):
        rest[-1][...] = w_ref[...].astype(out_dtype)

    if layer is None:
        in_specs = [pl.BlockSpec((tm, c), lambda i, me_ref: (i, 0))]
    else:
        in_specs = [pl.BlockSpec((None, tm, c), lambda i, me_ref: (layer, i, 0))]
    args = [me, w]
    if dep is not None:
        in_specs.append(pl.BlockSpec(DEP_SPEC_SHAPE, lambda i, me_ref: (0, 0)))
        args.append(dep)
    return pl.pallas_call(
        body, name=name,
        grid_spec=pltpu.PrefetchScalarGridSpec(
            num_scalar_prefetch=1, grid=(rows // tm,), in_specs=in_specs,
            out_specs=pl.BlockSpec((None, tm, c), lambda i, me_ref: (me_ref[0], i, 0))),
        out_shape=jax.ShapeDtypeStruct((nslots, rows, c), out_dtype), compiler_params=_cp("parallel"),
    )(*args)


def gather_start(lands, chunk_sizes, name="gather_start"):
    n = len(lands)
    nch = len(chunk_sizes)
    assert sum(chunk_sizes) == n

    def body(*refs):
        land_refs = refs[:n]
        outs = refs[n:]
        send_sems, recv_sems = outs[:nch], outs[nch:2 * nch]
        token = outs[-1]
        x, y, c = _place()
        me = _chip_of(x, y)
        peers = _other_chips(x, y, c)
        k = 0
        for ck, size in enumerate(chunk_sizes):
            for pos in range(size):
                for r, peer in enumerate(peers):
                    pltpu.make_async_remote_copy(
                        src_ref=land_refs[k].at[me], dst_ref=land_refs[k].at[me],
                        send_sem=send_sems[ck].at[N_PEER_CHIPS * pos + r], recv_sem=recv_sems[ck].at[N_PEER_CHIPS * pos + r],
                        device_id=peer, device_id_type=MESH).start()
                k += 1
        token[...] = jnp.zeros(token.shape, F32)

    sems = [pltpu.SemaphoreType.DMA((N_PEER_CHIPS * s,)) for s in chunk_sizes]
    res = pl.pallas_call(
        body, name=name,
        out_shape=(*sems, *sems, *_hbm_like(lands), jax.ShapeDtypeStruct(DEP_SPEC_SHAPE, F32)),
        in_specs=[HBM_SPEC] * n,
        out_specs=(*[SEM_SPEC] * (2 * nch), *[HBM_SPEC] * n, pl.BlockSpec(memory_space=pltpu.VMEM)),
        input_output_aliases={k: 2 * nch + k for k in range(n)},
        compiler_params=pltpu.CompilerParams(has_side_effects=DATAFLOW),
    )(*[_hbm(a) for a in lands])
    return res[:nch], res[nch:2 * nch], res[2 * nch:2 * nch + n], res[-1]


def gather_wait(send_sem, recv_sem, lands, after, name):
    n = len(lands)

    def body(*refs):
        land_refs = refs[:n]
        ssem, rsem = refs[n], refs[n + 1]
        x, y, c = _place()
        me = _chip_of(x, y)
        for pos in range(n):
            for r, peer in enumerate(_other_chips(x, y, c)):
                cp = pltpu.make_async_remote_copy(
                    src_ref=land_refs[pos].at[me], dst_ref=land_refs[pos].at[_chip_of(peer[0], peer[1])],
                    send_sem=ssem.at[N_PEER_CHIPS * pos + r], recv_sem=rsem.at[N_PEER_CHIPS * pos + r],
                    device_id=peer, device_id_type=MESH)
                cp.wait_send()
                cp.wait_recv()

    return pl.pallas_call(
        body, name=name, out_shape=tuple(_hbm_like(lands)),
        in_specs=[*[HBM_SPEC] * n, SEM_SPEC, SEM_SPEC, ANY_SPEC], out_specs=[HBM_SPEC] * n,
        input_output_aliases={k: k for k in range(n)},
        compiler_params=pltpu.CompilerParams(has_side_effects=DATAFLOW),
    )(*lands, send_sem, recv_sem, after)


def scatter_start(grads, name):
    n = len(grads)
    recvs = [lax.empty((N_PEER_CHIPS, *g.shape[1:]), g.dtype) for g in grads]

    def body(*refs):
        g_refs, r_refs = refs[:n], refs[n:2 * n]
        send_sems, recv_sems = refs[2 * n], refs[2 * n + 1]
        token = refs[-1]
        x, y, c = _place()
        for k in range(n):
            for r, peer in enumerate(_other_chips(x, y, c)):
                pltpu.make_async_remote_copy(
                    src_ref=g_refs[k].at[_chip_of(peer[0], peer[1])], dst_ref=r_refs[k].at[r],
                    send_sem=send_sems.at[N_PEER_CHIPS * k + r], recv_sem=recv_sems.at[N_PEER_CHIPS * k + r],
                    device_id=peer, device_id_type=MESH).start()
        token[...] = jnp.zeros(token.shape, F32)

    sem = pltpu.SemaphoreType.DMA((N_PEER_CHIPS * n,))
    res = pl.pallas_call(
        body, name=name,
        out_shape=(sem, sem, *_hbm_like(grads), *_hbm_like(recvs), jax.ShapeDtypeStruct(DEP_SPEC_SHAPE, F32)),
        in_specs=[HBM_SPEC] * (2 * n),
        out_specs=(SEM_SPEC, SEM_SPEC, *[HBM_SPEC] * (2 * n), pl.BlockSpec(memory_space=pltpu.VMEM)),
        input_output_aliases={k: 2 + k for k in range(2 * n)},
        compiler_params=pltpu.CompilerParams(has_side_effects=DATAFLOW),
    )(*[_hbm(a) for a in grads], *[_hbm(a) for a in recvs])
    return res[0], res[1], res[2:2 + n], res[2 + n:2 + 2 * n], res[-1]


def scatter_wait(send_sem, recv_sem, grads, recvs, after, name):
    n = len(grads)

    def body(*refs):
        g_refs, r_refs = refs[:n], refs[n:2 * n]
        ssem, rsem = refs[2 * n], refs[2 * n + 1]
        x, y, c = _place()
        for k in range(n):
            for r, peer in enumerate(_other_chips(x, y, c)):
                cp = pltpu.make_async_remote_copy(
                    src_ref=g_refs[k].at[_chip_of(peer[0], peer[1])], dst_ref=r_refs[k].at[r],
                    send_sem=ssem.at[N_PEER_CHIPS * k + r], recv_sem=rsem.at[N_PEER_CHIPS * k + r],
                    device_id=peer, device_id_type=MESH)
                cp.wait_send()
                cp.wait_recv()

    res = pl.pallas_call(
        body, name=name, out_shape=(*_hbm_like(grads), *_hbm_like(recvs)),
        in_specs=[*[HBM_SPEC] * (2 * n), SEM_SPEC, SEM_SPEC, ANY_SPEC], out_specs=[HBM_SPEC] * (2 * n),
        input_output_aliases={k: k for k in range(2 * n)},
        compiler_params=pltpu.CompilerParams(has_side_effects=DATAFLOW),
    )(*grads, *recvs, send_sem, recv_sem, after)
    return res[:n], res[n:]


def swap_start(parts, name):
    n = len(parts)
    lands = [lax.empty(p.shape, p.dtype) for p in parts]

    def body(*refs):
        p_refs, l_refs = refs[:n], refs[n:2 * n]
        sems = refs[2 * n:4 * n]
        token = refs[-1]
        x, y, c = _place()
        for k in range(n):
            pltpu.make_async_remote_copy(
                src_ref=p_refs[k], dst_ref=l_refs[k], send_sem=sems[k], recv_sem=sems[n + k],
                device_id=(x, y, 1 - c), device_id_type=MESH).start()
        token[...] = jnp.zeros(token.shape, F32)

    sem = pltpu.SemaphoreType.DMA(())
    res = pl.pallas_call(
        body, name=name,
        out_shape=(*[sem] * (2 * n), *_hbm_like(parts), *_hbm_like(lands), jax.ShapeDtypeStruct(DEP_SPEC_SHAPE, F32)),
        in_specs=[HBM_SPEC] * (2 * n),
        out_specs=(*[SEM_SPEC] * (2 * n), *[HBM_SPEC] * (2 * n), pl.BlockSpec(memory_space=pltpu.VMEM)),
        input_output_aliases={k: 2 * n + k for k in range(2 * n)},
        compiler_params=pltpu.CompilerParams(has_side_effects=DATAFLOW),
    )(*[_hbm(a) for a in parts], *[_hbm(a) for a in lands])
    return res[:n], res[n:2 * n], res[2 * n:3 * n], res[3 * n:4 * n], res[-1]


def swap_wait(send_sem, recv_sem, part, land, after, name):
    def body(p_ref, l_ref, ssem, rsem, after_ref, p_out, l_out):
        x, y, c = _place()
        cp = pltpu.make_async_remote_copy(src_ref=p_ref, dst_ref=l_ref, send_sem=ssem, recv_sem=rsem,
                                          device_id=(x, y, 1 - c), device_id_type=MESH)
        cp.wait_send()
        cp.wait_recv()

    return pl.pallas_call(
        body, name=name, out_shape=tuple(_hbm_like([part, land])),
        in_specs=[HBM_SPEC, HBM_SPEC, SEM_SPEC, SEM_SPEC, ANY_SPEC], out_specs=[HBM_SPEC, HBM_SPEC],
        input_output_aliases={0: 0, 1: 1},
        compiler_params=pltpu.CompilerParams(has_side_effects=DATAFLOW),
    )(part, land, send_sem, recv_sem, after)


def _xor_peer(x, y, c, k):
    px, py, pc = x ^ ((k >> 2) & 1), y ^ ((k >> 1) & 1), c ^ (k & 1)
    return (px, py, pc), 4 * px + 2 * py + pc


def small_start(land, name="small_start"):
    def body(l_ref, ssem, rsem, l_out, token):
        x, y, c = _place()
        me = 4 * x + 2 * y + c
        for k in range(1, N_DEV):
            peer, _ = _xor_peer(x, y, c, k)
            pltpu.make_async_remote_copy(
                src_ref=l_ref.at[me], dst_ref=l_ref.at[me], send_sem=ssem.at[k - 1], recv_sem=rsem.at[k - 1],
                device_id=peer, device_id_type=MESH).start()
        token[...] = jnp.zeros(token.shape, F32)

    sem = pltpu.SemaphoreType.DMA((N_DEV - 1,))
    return pl.pallas_call(
        body, name=name,
        out_shape=(sem, sem, pltpu.HBM(land.shape, land.dtype), jax.ShapeDtypeStruct(DEP_SPEC_SHAPE, F32)),
        in_specs=[HBM_SPEC], out_specs=(SEM_SPEC, SEM_SPEC, HBM_SPEC, pl.BlockSpec(memory_space=pltpu.VMEM)),
        input_output_aliases={0: 2}, compiler_params=pltpu.CompilerParams(has_side_effects=DATAFLOW),
    )(_hbm(land))


def small_wait(send_sem, recv_sem, land, after, name="small_wait"):
    def body(l_ref, ssem, rsem, after_ref, l_out):
        x, y, c = _place()
        me = 4 * x + 2 * y + c
        for k in range(1, N_DEV):
            peer, slot = _xor_peer(x, y, c, k)
            cp = pltpu.make_async_remote_copy(
                src_ref=l_ref.at[me], dst_ref=l_ref.at[slot], send_sem=ssem.at[k - 1], recv_sem=rsem.at[k - 1],
                device_id=peer, device_id_type=MESH)
            cp.wait_send()
            cp.wait_recv()

    return pl.pallas_call(
        body, name=name, out_shape=pltpu.HBM(land.shape, land.dtype),
        in_specs=[HBM_SPEC, SEM_SPEC, SEM_SPEC, ANY_SPEC], out_specs=HBM_SPEC, input_output_aliases={0: 0},
        compiler_params=pltpu.CompilerParams(has_side_effects=DATAFLOW),
    )(land, send_sem, recv_sem, after)


def sum_slots(land, name="sum_slots", tm=256):
    n, rows, c = land.shape
    tm = _row_tile(rows, tm)

    def body(l_ref, o_ref):
        acc = l_ref[0]
        for j in range(1, n):
            acc = acc + l_ref[j]
        o_ref[...] = acc

    return pl.pallas_call(
        body, name=name, grid=(rows // tm,), in_specs=[pl.BlockSpec((n, tm, c), lambda i: (0, i, 0))],
        out_specs=pl.BlockSpec((tm, c), lambda i: (i, 0)), out_shape=jax.ShapeDtypeStruct((rows, c), F32),
        compiler_params=_cp("parallel"),
    )(land)


PACK_ROW_TILE = 256


def _pack(arrays):
    flat = jnp.concatenate([a.reshape(-1).astype(F32) for a in arrays])
    n = flat.shape[0]
    rows = -(-n // LANES)
    rows = -(-rows // PACK_ROW_TILE) * PACK_ROW_TILE
    return jnp.pad(flat, (0, rows * LANES - n)).reshape(rows, LANES)


def _unpack(packed, shapes, lead=()):
    flat = packed.reshape(*lead, -1)
    out, off = [], 0
    for shp in shapes:
        n = math.prod(shp)
        out.append(flat[..., off:off + n].reshape(*lead, *shp))
        off += n
    return out


def _row(vec):
    return vec.reshape(1, -1)


def kernel(x, mix_pre_g, mix_post_g, ffn_pre_g, ffn_post_g, cm_w_in, cm_b_in, cm_dw, cm_dw_b, cm_ln_g, cm_ln_b, cm_w_out, cm_b_out, kv_norm_g, w_kv, w_q, w_o, ffn_w_in, ffn_dw, ffn_dw_b, ffn_w_out, loss_target, m_mix_pre_g, m_mix_post_g, m_ffn_pre_g, m_ffn_post_g, m_cm_w_in, m_cm_b_in, m_cm_dw, m_cm_dw_b, m_cm_ln_g, m_cm_ln_b, m_cm_w_out, m_cm_b_out, m_kv_norm_g, m_w_kv, m_w_q, m_w_o, m_ffn_w_in, m_ffn_dw, m_ffn_dw_b, m_ffn_w_out, v_mix_pre_g, v_mix_post_g, v_ffn_pre_g, v_ffn_post_g, v_cm_w_in, v_cm_b_in, v_cm_dw, v_cm_dw_b, v_cm_ln_g, v_cm_ln_b, v_cm_w_out, v_cm_b_out, v_kv_norm_g, v_w_kv, v_w_q, v_w_o, v_ffn_w_in, v_ffn_dw, v_ffn_dw_b, v_ffn_w_out):
    names = ["mix_pre_g", "mix_post_g", "ffn_pre_g", "ffn_post_g", "cm_w_in", "cm_b_in", "cm_dw", "cm_dw_b", "cm_ln_g",
             "cm_ln_b", "cm_w_out", "cm_b_out", "kv_norm_g", "w_kv", "w_q", "w_o", "ffn_w_in", "ffn_dw", "ffn_dw_b",
             "ffn_w_out"]
    w_in = dict(zip(names, [mix_pre_g, mix_post_g, ffn_pre_g, ffn_post_g, cm_w_in, cm_b_in, cm_dw, cm_dw_b, cm_ln_g,
                            cm_ln_b, cm_w_out, cm_b_out, kv_norm_g, w_kv, w_q, w_o, ffn_w_in, ffn_dw, ffn_dw_b, ffn_w_out]))
    m_in = dict(zip(names, [m_mix_pre_g, m_mix_post_g, m_ffn_pre_g, m_ffn_post_g, m_cm_w_in, m_cm_b_in, m_cm_dw, m_cm_dw_b,
                            m_cm_ln_g, m_cm_ln_b, m_cm_w_out, m_cm_b_out, m_kv_norm_g, m_w_kv, m_w_q, m_w_o, m_ffn_w_in,
                            m_ffn_dw, m_ffn_dw_b, m_ffn_w_out]))
    v_in = dict(zip(names, [v_mix_pre_g, v_mix_post_g, v_ffn_pre_g, v_ffn_post_g, v_cm_w_in, v_cm_b_in, v_cm_dw, v_cm_dw_b,
                            v_cm_ln_g, v_cm_ln_b, v_cm_w_out, v_cm_b_out, v_kv_norm_g, v_w_kv, v_w_q, v_w_o, v_ffn_w_in,
                            v_ffn_dw, v_ffn_dw_b, v_ffn_w_out]))

    bsz, seq, d = x.shape
    t = bsz * seq
    n_b = DEPTH - N_A
    hw = w_o.shape[-1]
    qw = N_GROUPS * hw
    f2 = ffn_dw_b.shape[-1]
    f = f2 // 2
    me_chip = _chip_of(lax.axis_index("x"), lax.axis_index("y"))

    big = ["cm_w_in", "cm_w_out", "w_kv", "w_q", "w_o", "ffn_w_in", "ffn_w_out"]
    row_sharded = ("cm_w_out", "w_o", "ffn_w_out")
    small_sharded = ["cm_b_in", "cm_dw", "cm_dw_b", "cm_ln_g", "cm_ln_b", "cm_b_out", "ffn_dw"]
    small_pack = _pack([w_in[n] for n in small_sharded])
    chunks = [
        [("cm_w_in", 0), ("small", None)],
        [("cm_w_out", 0)],
        [("ffn_w_in", 0), ("ffn_w_out", 0)],
        [("cm_w_in", 1), ("cm_w_out", 1)],
        [("ffn_w_in", 1), ("ffn_w_out", 1)],
        [("w_kv", None)],
        [("w_q", 0), ("w_o", 0)],
        [("ffn_w_in", 2), ("ffn_w_out", 2)],
        [("w_q", 1), ("w_o", 1)],
        [("ffn_w_in", 3), ("ffn_w_out", 3)],
    ]
    pieces = [pc for ch in chunks for pc in ch]
    chunk_of = {pc: ck for ck, ch in enumerate(chunks) for pc in ch}

    me_arr = me_chip.astype(jnp.int32).reshape(1)

    def land_of(pc, dep=None):
        n, l = pc
        if n == "small":
            return cast_place(small_pack, None, me_arr, F32, name="place_small", dep=dep)
        return cast_place(w_in[n], l, me_arr, BF16, name=f"place_{n}_{l}", dep=dep)

    n_early = len(chunks[0]) + len(chunks[1])
    lands_a = [land_of(pc) for pc in pieces[:n_early]]
    send_a, recv_a, lands_fa, token_a = gather_start(lands_a, [len(chunks[0]), len(chunks[1])], name="gather_start_a")
    lands_b = [land_of(pc, token_a if k == 0 else None) for k, pc in enumerate(pieces[n_early:])]
    send_b, recv_b, lands_fb, token = gather_start(lands_b, [len(ch) for ch in chunks[2:]], name="gather_start_b")
    g_send, g_recv, lands_f = [*send_a, *send_b], [*recv_a, *recv_b], [*lands_fa, *lands_fb]
    weights = {}

    def finish_chunk(ck, after):
        lo = sum(len(ch) for ch in chunks[:ck])
        hi = lo + len(chunks[ck])
        got = gather_wait(g_send[ck], g_recv[ck], lands_f[lo:hi], after, name=f"gather_wait{ck}")
        for pc, arr in zip(chunks[ck], got):
            weights[pc] = arr.reshape(1, -1, arr.shape[-1]) if pc[0] in row_sharded else arr

    def wmat(n, l=None, after=None):
        if (n, l) not in weights:
            finish_chunk(chunk_of[(n, l)], after)
        arr = weights[(n, l)]
        return arr, arr.shape[0]

    finish_chunk(0, token)
    small_full = {}
    for n, arr4 in zip(small_sharded, _unpack(weights[("small", None)], [w_in[n].shape for n in small_sharded], lead=(N_CHIPS,))):
        shp = w_in[n].shape
        small_full[n] = jnp.moveaxis(arr4, 0, -2).reshape(*shp[:-1], N_CHIPS * shp[-1])

    x2d = x.reshape(t, d)
    saved = []
    (h1,) = resid_norm_fwd(x2d, None, None, [_row(mix_pre_g[0])], name="norm_in", dep=token)
    xcur = x2d
    kv_state = None
    for i in range(DEPTH):
        sv = {"x_in": xcur, "h1": h1}
        if i < N_A:
            z = mm_nn(h1, *wmat("cm_w_in", i, h1), 1, 0, bias=_row(small_full["cm_b_in"][i]), name=f"cm_in{i}")
            u2 = glu_conv_fwd(z.reshape(bsz, seq, 2 * d), small_full["cm_dw"][i], _row(small_full["cm_dw_b"][i]),
                              name=f"glu_conv{i}").reshape(t, d)
            u4 = ln_silu_fwd(u2, _row(small_full["cm_ln_g"][i]), _row(small_full["cm_ln_b"][i]), name=f"ln_silu{i}")
            y = mm_nn(u4, *wmat("cm_w_out", i, u4), 1, 0, bias=_row(small_full["cm_b_out"][i]), out_dtype=BF16,
                      name=f"cm_out{i}")
            sv.update(z=z, u2=u2, u4=u4)
        else:
            j = i - N_A
            q = mm_nn(h1, *wmat("w_q", j, h1), 1, 0, name=f"q_proj{j}", head_major=True).reshape(-1, bsz, seq, HEAD_DIM)
            outs, lses = [], []
            for g, dil in enumerate(DILATIONS):
                o_g, l_g = attn_fwd(q, kv_state["kv"], g, dil, hw, name=f"attn_fwd{j}_{g}")
                outs.append(o_g.reshape(t, hw))
                lses.append(l_g.reshape(t, LANES))
            merged, lsej = attn_merge(outs, lses, name=f"attn_merge{j}")
            y = mm_nn(merged, *wmat("w_o", j, merged), 1, 0, out_dtype=BF16, name=f"o_proj{j}")
            sv.update(q=q, merged=merged, lsej=lsej)
        x1, h2 = resid_norm_fwd(xcur, y, _row(mix_post_g[i]), [_row(ffn_pre_g[i])], name=f"resid_mix{i}")
        p = mm_nn(h2, *wmat("ffn_w_in", i, h2), 1, 0, out_dtype=BF16, name=f"ffn_in{i}")
        s_act = ffn_mid_fwd(p.reshape(bsz, seq, f2), small_full["ffn_dw"][i], _row(ffn_dw_b[i]), name=f"ffn_mid{i}").reshape(t, f)
        y2 = mm_nn(s_act, *wmat("ffn_w_out", i), 1, 0, out_dtype=BF16, name=f"ffn_out{i}")
        next_gains = []
        if i + 1 < DEPTH:
            next_gains.append(_row(mix_pre_g[i + 1]))
        if i == N_A - 1:
            next_gains.append(_row(kv_norm_g))
        res = resid_norm_fwd(x1, y2, _row(ffn_post_g[i]), next_gains, name=f"resid_ffn{i}")
        sv.update(y=y, x1=x1, h2=h2, p=p, s=s_act, y2=y2)
        saved.append(sv)
        xcur = res[0]
        if i + 1 < DEPTH:
            h1 = res[1]
        if i == N_A - 1:
            kvn = res[2]
            kv = mm_nn(kvn, *wmat("w_kv", None, kvn), 1, 0, name="kv_proj", head_major=True).reshape(-1, bsz, seq, HEAD_DIM)
            kv_state = {"kv": kv, "kvn": kvn, "x_a": xcur}

    dx, loss_tile = loss_fwd_bwd(xcur, loss_target.reshape(t, d))
    loss = lax.psum(loss_tile[0, 0], ("x", "y", "c"))

    gsm = {n: [None] * w_in[n].shape[0] for n in
           ["mix_pre_g", "mix_post_g", "ffn_pre_g", "ffn_post_g", "cm_b_in", "cm_dw", "cm_dw_b", "cm_ln_g", "cm_ln_b",
            "cm_b_out", "ffn_dw", "ffn_dw_b"]}
    gbig = {}
    in_flight = []
    dep = None

    def start_scatter(pcs, tag):
        ssem, rsem, g_f, r_f, tok = scatter_start([gbig[pc] for pc in pcs], name=f"scatter_start_{tag}")
        in_flight.append((pcs, ssem, rsem, g_f, r_f))
        return tok

    dk_buf = dv_buf = None
    for i in range(DEPTH - 1, -1, -1):
        sv = saved[i]
        dy2, dg, _ = norm_bwd(sv["y2"], _row(ffn_post_g[i]), dx, out_dtype=BF16, name=f"bwd_ffn_post{i}", dep=dep)
        gsm["ffn_post_g"][i] = dg
        ds = mm_nt(dy2, *wmat("ffn_w_out", i), 1, 0, out_dtype=BF16, name=f"bwd_ffn_out_dx{i}")
        gbig[("ffn_w_out", i)] = mm_tn(sv["s"], dy2, 1, name=f"bwd_ffn_out_dw{i}").reshape(N_CHIPS, f // N_CHIPS, d)
        dpa, dpg, ddwa, ddwg, ddba, ddbg = ffn_mid_bwd(sv["p"].reshape(bsz, seq, f2), small_full["ffn_dw"][i], _row(ffn_dw_b[i]),
                                                       ds.reshape(bsz, seq, f), name=f"bwd_ffn_mid{i}")
        gsm["ffn_dw"][i] = jnp.concatenate([jnp.sum(ddwa, axis=1), jnp.sum(ddwg, axis=1)], axis=-1)
        gsm["ffn_dw_b"][i] = jnp.concatenate([ddba, ddbg], axis=-1)
        dp = [dpa.reshape(t, f), dpg.reshape(t, f)]
        dh2 = mm_nt(dp, *wmat("ffn_w_in", i), 1, 0, name=f"bwd_ffn_in_dx{i}")
        gbig[("ffn_w_in", i)] = mm_tn(sv["h2"], dp, N_CHIPS, name=f"bwd_ffn_in_dw{i}")
        dx1, dg, _ = norm_bwd(sv["x1"], _row(ffn_pre_g[i]), dh2, add=dx, name=f"bwd_ffn_pre{i}")
        gsm["ffn_pre_g"][i] = dg
        dep = start_scatter([("ffn_w_in", 0), ("ffn_w_out", 0)], "ffn0") if i == 0 else None
        dy, dg, dbias = norm_bwd(sv["y"], _row(mix_post_g[i]), dx1, out_dtype=BF16, name=f"bwd_mix_post{i}", dep=dep)
        gsm["mix_post_g"][i] = dg
        if i < N_A:
            gsm["cm_b_out"][i] = dbias
            du4 = mm_nt(dy, *wmat("cm_w_out", i), 1, 0, name=f"bwd_cm_out_dx{i}")
            gbig[("cm_w_out", i)] = mm_tn(sv["u4"], dy, 1, name=f"bwd_cm_out_dw{i}").reshape(N_CHIPS, d // N_CHIPS, d)
            du2, dlg, dlb = ln_silu_bwd(sv["u2"], _row(small_full["cm_ln_g"][i]), _row(small_full["cm_ln_b"][i]), du4,
                                        name=f"bwd_ln_silu{i}")
            gsm["cm_ln_g"][i], gsm["cm_ln_b"][i] = dlg, dlb
            dza, dzg, ddw, ddwb, dba, dbg = glu_conv_bwd(sv["z"].reshape(bsz, seq, 2 * d), small_full["cm_dw"][i],
                                                         du2.reshape(bsz, seq, d), name=f"bwd_glu_conv{i}")
            gsm["cm_dw"][i] = jnp.sum(ddw, axis=1)
            gsm["cm_dw_b"][i] = ddwb
            gsm["cm_b_in"][i] = jnp.concatenate([dba, dbg], axis=-1)
            dz = [dza.reshape(t, d), dzg.reshape(t, d)]
            dh1 = mm_nt(dz, *wmat("cm_w_in", i), 1, 0, name=f"bwd_cm_in_dx{i}")
            gbig[("cm_w_in", i)] = mm_tn(sv["h1"], dz, N_CHIPS, name=f"bwd_cm_in_dw{i}")
        else:
            j = i - N_A
            dmerged = mm_nt(dy, *wmat("w_o", j), 1, 0, name=f"bwd_o_proj_dx{j}")
            gbig[("w_o", j)] = mm_tn(sv["merged"], dy, 1, name=f"bwd_o_proj_dw{j}").reshape(N_CHIPS, hw // N_CHIPS, d)
            dmt = attn_bwd_prep(dmerged, sv["merged"], name=f"bwd_attn_prep{j}")
            dq_buf = None
            add_to_kv = dk_buf is not None
            for g, dil in enumerate(DILATIONS):
                dq_buf, dk_buf, dv_buf = attn_bwd(
                    sv["q"], kv_state["kv"], g, dil, dmerged.reshape(bsz, seq, hw), sv["lsej"].reshape(bsz, seq, LANES),
                    dmt.reshape(bsz, seq, LANES), dq_buf, dk_buf, dv_buf, add_to_kv, hw, name=f"attn_bwd{j}_{g}")
            dq = dq_buf.reshape(t, qw)
            dh1 = mm_nt(dq, *wmat("w_q", j), 1, 0, name=f"bwd_q_proj_dx{j}")
            gbig[("w_q", j)] = mm_tn(sv["h1"], dq, N_CHIPS, name=f"bwd_q_proj_dw{j}")
        dx, dg, _ = norm_bwd(sv["x_in"], _row(mix_pre_g[i]), dh1, add=dx1, name=f"bwd_mix_pre{i}")
        gsm["mix_pre_g"][i] = dg
        if i > N_A:
            dep = start_scatter([("ffn_w_in", i), ("ffn_w_out", i), ("w_q", i - N_A), ("w_o", i - N_A)], f"l{i}")
        elif 0 < i < N_A:
            dep = start_scatter([("ffn_w_in", i), ("ffn_w_out", i), ("cm_w_in", i), ("cm_w_out", i)], f"l{i}")
        elif i == 0:
            last_token = start_scatter([("cm_w_in", 0), ("cm_w_out", 0)], "cm0")
        if i == N_A:
            dkv = [dk_buf.reshape(t, qw), dv_buf.reshape(t, qw)]
            dkvn = mm_nt(dkv, *wmat("w_kv"), 1, 0, name="bwd_kv_proj_dx")
            gbig[("w_kv", None)] = mm_tn(kv_state["kvn"], dkv, N_CHIPS, name="bwd_kv_proj_dw")
            dx, dg_kv, _ = norm_bwd(kv_state["x_a"], _row(kv_norm_g), dkvn, add=dx, name="bwd_kv_norm")
            dep = start_scatter([("ffn_w_in", i), ("ffn_w_out", i), ("w_q", 0), ("w_o", 0), ("w_kv", None)], f"l{i}")
    grad_x = dx.reshape(bsz, seq, d)

    plane_of = {}
    outs_g, outs_d, outs_m, outs_v = {}, {}, {}, {}

    def finish_scatter(k, after):
        pcs, ssem, rsem, g_f, r_f = in_flight[k]
        g_done, r_done = scatter_wait(ssem, rsem, g_f, r_f, after, name=f"scatter_wait{k}")
        for pc, g_arr, r_arr in zip(pcs, g_done, r_done):
            plane_of[pc] = sum_parts(g_arr, r_arr, me_arr, name=f"sum_chips_{pc[0]}_{pc[1]}")

    def update(group, tag, after):
        plane = []
        for n in group:
            if w_in[n].ndim == 2:
                plane.append(plane_of[(n, None)])
            else:
                plane.append(jnp.concatenate([plane_of[(n, l)] for l in range(w_in[n].shape[0])], axis=0))
        ssems, rsems, plane_f, land_f, _ = swap_start(plane, name=f"swap_start_{tag}")
        for k, n in enumerate(group):
            p_mine, p_other = swap_wait(ssems[k], rsems[k], plane_f[k], land_f[k], after, name=f"swap_wait_{n}")
            shp = w_in[n].shape
            flat = lambda a: a.reshape(-1, shp[-1])
            g_, d_, m_, v_ = adamw(flat(w_in[n]), flat(m_in[n]), flat(v_in[n]), [p_mine, p_other], name=f"adamw_{n}")
            outs_g[n], outs_d[n], outs_m[n], outs_v[n] = (a.reshape(shp) for a in (g_, d_, m_, v_))
            after = v_
        return after

    small_names = [n for n in names if n not in big]
    small_shapes_full = {}
    small_grads_full = []
    for n in small_names:
        if n == "kv_norm_g":
            gfull = dg_kv.reshape(-1)
        elif n in ("cm_dw", "ffn_dw"):
            gfull = jnp.stack(gsm[n], axis=0)
        else:
            gfull = jnp.stack([a.reshape(-1) for a in gsm[n]], axis=0)
        small_shapes_full[n] = gfull.shape
        small_grads_full.append(gfull)
    dev_arr = (4 * lax.axis_index("x") + 2 * lax.axis_index("y") + lax.axis_index("c")).astype(jnp.int32).reshape(1)
    small_land = cast_place(_pack(small_grads_full) + last_token[0, 0], None, dev_arr, F32, name="place_small_grads",
                            nslots=N_DEV)
    sm_send, sm_recv, small_land, small_token = small_start(small_land)

    for k in range(len(in_flight) - 1):
        finish_scatter(k, small_token)
    done = update(["w_kv", "w_q", "w_o", "ffn_w_in", "ffn_w_out"], "a", small_token)
    finish_scatter(len(in_flight) - 1, done)
    done = update(["cm_w_in", "cm_w_out"], "b", done)

    summed = sum_slots(small_wait(sm_send, sm_recv, small_land, done))
    g_full = dict(zip(small_names, _unpack(summed, [small_shapes_full[n] for n in small_names])))
    g_loc = {}
    for n in small_names:
        if n in small_sharded:
            width = w_in[n].shape[-1]
            g_loc[n] = lax.dynamic_slice_in_dim(g_full[n], me_chip * width, width, axis=g_full[n].ndim - 1)
        else:
            g_loc[n] = g_full[n]
    res = adamw(_pack([w_in[n] for n in small_names]), _pack([m_in[n] for n in small_names]),
                _pack([v_in[n] for n in small_names]), [_pack([g_loc[n] for n in small_names])], name="adamw_small")
    shapes_loc = [w_in[n].shape for n in small_names]
    for dst, packed in zip((outs_g, outs_d, outs_m, outs_v), res):
        for n, a in zip(small_names, _unpack(packed, shapes_loc)):
            dst[n] = a

    return (loss, grad_x, *[outs_g[n] for n in names], *[outs_d[n] for n in names],
            *[outs_m[n] for n in names], *[outs_v[n] for n in names])
```

```python
import functools
import math

import jax
import jax.numpy as jnp
from jax import lax
from jax.experimental import pallas as pl
from jax.experimental.pallas import tpu as pltpu

F32 = jnp.float32
BF16 = jnp.bfloat16
EPS = 1e-6
NEG_INF = -1e30
N_A = 2
DEPTH = 4
N_GROUPS = 3
DILATIONS = (1, 4, 16)
HEAD_DIM = 128
BLK = 128
LANES = 128
N_CHIPS = 4
N_DEV = 8
VMEM_LIMIT_V7X = 56 * 1024 * 1024

ADAM_LR = 0.001
ADAM_B1 = 0.9
ADAM_B2 = 0.999
ADAM_EPS = 1e-08
ADAM_WD = 0.01
ADAM_STEP = 10

MESH = pl.DeviceIdType.MESH


def _cp(*sem, **kw):
    return pltpu.CompilerParams(dimension_semantics=sem if sem else None, vmem_limit_bytes=VMEM_LIMIT_V7X, **kw)


def _dot(a, b):
    return jnp.dot(a, b, preferred_element_type=F32)


def _dot_nt(a, b):
    return lax.dot_general(a, b, (((1,), (1,)), ((), ())), preferred_element_type=F32)


def _dot_tn(a, b):
    return lax.dot_general(a, b, (((0,), (0,)), ((), ())), preferred_element_type=F32)


def _sigmoid(x):
    return 1.0 / (1.0 + jnp.exp(-x))


def _row_tile(n, want):
    if n <= want:
        return n
    for t in range(want - want % 8, 7, -8):
        if n % t == 0:
            return t
    raise ValueError(f"no row tile for {n} rows")


def mm_nn(a, w, nsh, stride, layer, bias=None, out_dtype=F32, name="mm_nn", tm=1024, head_major=False):
    m, k = a.shape
    _, k2, ns = w.shape
    assert k == k2
    tm = _row_tile(m, tm)
    has_bias = bias is not None
    hps = ns // HEAD_DIM

    def body(*refs):
        if has_bias:
            a_ref, w_ref, b_ref, o_ref = refs
        else:
            a_ref, w_ref, o_ref = refs
        acc = _dot(a_ref[...].astype(BF16), w_ref[...])
        if has_bias:
            acc = acc + b_ref[...]
        if head_major:
            for hh in range(hps):
                o_ref[hh] = acc[:, hh * HEAD_DIM:(hh + 1) * HEAD_DIM].astype(out_dtype)
        else:
            o_ref[...] = acc.astype(out_dtype)

    in_specs = [
        pl.BlockSpec((tm, k), lambda j, i: (i, 0)),
        pl.BlockSpec((None, k, ns), lambda j, i: (j * stride + layer, 0, 0)),
    ]
    args = [a, w]
    if has_bias:
        in_specs.append(pl.BlockSpec((1, ns), lambda j, i: (0, j)))
        args.append(bias)
    return pl.pallas_call(
        body,
        name=name,
        grid=(nsh, m // tm),
        in_specs=in_specs,
        out_specs=(pl.BlockSpec((hps, tm, HEAD_DIM), lambda j, i: (j, i, 0)) if head_major
                   else pl.BlockSpec((tm, ns), lambda j, i: (i, j))),
        out_shape=jax.ShapeDtypeStruct((nsh * hps, m, HEAD_DIM) if head_major else (m, nsh * ns), out_dtype),
        compiler_params=_cp("parallel", "parallel"),
    )(*args)


def mm_nt(dy, w, nsh, stride, layer, out_dtype=F32, name="mm_nt", tm=1024):
    dys = list(dy) if isinstance(dy, (list, tuple)) else [dy]
    npart = len(dys)
    per = nsh // npart
    m = dys[0].shape[0]
    _, k, ns = w.shape
    assert all(d.shape == (m, per * ns) for d in dys)
    tm = _row_tile(m, tm)

    def body(*refs):
        dy_refs = refs[:npart]
        w_ref, o_ref, acc_ref = refs[npart:]
        j = pl.program_id(1)

        @pl.when(j == 0)
        def _():
            acc_ref[...] = jnp.zeros(acc_ref.shape, F32)

        for pi in range(npart):
            @pl.when(j // per == pi)
            def _(pi=pi):
                acc_ref[...] += _dot_nt(dy_refs[pi][...].astype(BF16), w_ref[...])

        @pl.when(j == nsh - 1)
        def _():
            o_ref[...] = acc_ref[...].astype(out_dtype)

    dy_specs = [pl.BlockSpec((tm, ns), lambda i, j, pi=pi: (i, jnp.clip(j - pi * per, 0, per - 1))) for pi in range(npart)]
    return pl.pallas_call(
        body,
        name=name,
        grid=(m // tm, nsh),
        in_specs=[*dy_specs, pl.BlockSpec((None, k, ns), lambda i, j: (j * stride + layer, 0, 0))],
        out_specs=pl.BlockSpec((tm, k), lambda i, j: (i, 0)),
        out_shape=jax.ShapeDtypeStruct((m, k), out_dtype),
        scratch_shapes=[pltpu.VMEM((tm, k), F32)],
        compiler_params=_cp("parallel", "arbitrary"),
    )(*dys, w)


def mm_tn(a, dy, nsh, name="mm_tn", tm=1024):
    dys = list(dy) if isinstance(dy, (list, tuple)) else [dy]
    npart = len(dys)
    per = nsh // npart
    m, k = a.shape
    ns = dys[0].shape[1] // per
    assert all(d.shape == (m, per * ns) for d in dys)
    tm = _row_tile(m, tm)
    nt = m // tm

    def body(*refs):
        a_ref = refs[0]
        dy_refs = refs[1:1 + npart]
        o_ref, acc_ref = refs[1 + npart:]
        j = pl.program_id(0)
        i = pl.program_id(1)

        @pl.when(i == 0)
        def _():
            acc_ref[...] = jnp.zeros(acc_ref.shape, F32)

        for pi in range(npart):
            @pl.when(j // per == pi)
            def _(pi=pi):
                acc_ref[...] += _dot_tn(a_ref[...].astype(BF16), dy_refs[pi][...].astype(BF16))

        @pl.when(i == nt - 1)
        def _():
            o_ref[...] = acc_ref[...].astype(BF16)

    dy_specs = [
        pl.BlockSpec((tm, ns), lambda j, i, pi=pi: (jnp.where(j // per == pi, i, 0), jnp.clip(j - pi * per, 0, per - 1)))
        for pi in range(npart)
    ]
    return pl.pallas_call(
        body,
        name=name,
        grid=(nsh, nt),
        in_specs=[pl.BlockSpec((tm, k), lambda j, i: (i, 0)), *dy_specs],
        out_specs=pl.BlockSpec((None, k, ns), lambda j, i: (j, 0, 0)),
        out_shape=jax.ShapeDtypeStruct((nsh, k, ns), BF16),
        scratch_shapes=[pltpu.VMEM((k, ns), F32)],
        compiler_params=_cp("parallel", "arbitrary"),
    )(a, *dys)


DEP_SPEC_SHAPE = (8, LANES)


def resid_norm_fwd(x, y, g_post, next_gains, name, tm=512, dep=None):
    t, d = x.shape
    tm = _row_tile(t, tm)
    has_y = y is not None
    n_next = len(next_gains)
    n_dep = 0 if dep is None else 1

    def body(*refs):
        x_ref = refs[0]
        pos = 1
        if has_y:
            y_ref, gp_ref = refs[1], refs[2]
            pos = 3
        gn_refs = refs[pos:pos + n_next]
        outs = refs[pos + n_next + n_dep:]
        xv = x_ref[...]
        o = 0
        if has_y:
            yv = y_ref[...].astype(F32)
            r = lax.rsqrt(jnp.mean(yv * yv, axis=-1, keepdims=True) + EPS)
            xv = xv + (yv * r) * gp_ref[...]
            outs[0][...] = xv
            o = 1
        if n_next:
            xn = xv * lax.rsqrt(jnp.mean(xv * xv, axis=-1, keepdims=True) + EPS)
            for k in range(n_next):
                outs[o + k][...] = (xn * gn_refs[k][...]).astype(BF16)

    row = pl.BlockSpec((tm, d), lambda i: (i, 0))
    vec = pl.BlockSpec((1, d), lambda i: (0, 0))
    args, in_specs = [x], [row]
    if has_y:
        args += [y, g_post]
        in_specs += [row, vec]
    args += list(next_gains)
    in_specs += [vec] * n_next
    if n_dep:
        args.append(dep)
        in_specs.append(pl.BlockSpec(DEP_SPEC_SHAPE, lambda i: (0, 0)))
    out_shape, out_specs = [], []
    if has_y:
        out_shape.append(jax.ShapeDtypeStruct((t, d), F32))
        out_specs.append(row)
    for _ in range(n_next):
        out_shape.append(jax.ShapeDtypeStruct((t, d), BF16))
        out_specs.append(row)
    return pl.pallas_call(
        body, name=name, grid=(t // tm,), in_specs=in_specs, out_specs=out_specs, out_shape=out_shape,
        compiler_params=_cp("parallel"),
    )(*args)


def norm_bwd(x, g, dy, add=None, out_dtype=F32, name="norm_bwd", tm=512, dep=None):
    t, d = x.shape
    tm = _row_tile(t, tm)
    has_add = add is not None

    def body(*refs):
        x_ref, g_ref, dy_ref = refs[:3]
        add_ref = refs[3] if has_add else None
        dx_ref, dg_ref, cs_ref = refs[-3:]
        i = pl.program_id(0)
        xv = x_ref[...].astype(F32)
        dyv = dy_ref[...].astype(F32)
        r = lax.rsqrt(jnp.mean(xv * xv, axis=-1, keepdims=True) + EPS)
        gd = dyv * g_ref[...]
        dx = r * gd - xv * ((r * r * r) * jnp.mean(xv * gd, axis=-1, keepdims=True))
        if has_add:
            dx = dx + add_ref[...]
        dx_ref[...] = dx.astype(out_dtype)
        dg = jnp.sum(dyv * (xv * r), axis=0, keepdims=True)
        cs = jnp.sum(dx, axis=0, keepdims=True)

        @pl.when(i == 0)
        def _():
            dg_ref[...] = dg
            cs_ref[...] = cs

        @pl.when(i > 0)
        def _():
            dg_ref[...] += dg
            cs_ref[...] += cs

    row = pl.BlockSpec((tm, d), lambda i: (i, 0))
    vec = pl.BlockSpec((1, d), lambda i: (0, 0))
    args, in_specs = [x, g, dy], [row, vec, row]
    if has_add:
        args.append(add)
        in_specs.append(row)
    if dep is not None:
        args.append(dep)
        in_specs.append(pl.BlockSpec(DEP_SPEC_SHAPE, lambda i: (0, 0)))
    return pl.pallas_call(
        body, name=name, grid=(t // tm,), in_specs=in_specs,
        out_specs=[row, vec, vec],
        out_shape=[jax.ShapeDtypeStruct((t, d), out_dtype), jax.ShapeDtypeStruct((1, d), F32),
                   jax.ShapeDtypeStruct((1, d), F32)],
        compiler_params=_cp("arbitrary"),
    )(*args)


def loss_fwd_bwd(x, target, name="loss", tm=512):
    t, d = x.shape
    tm = _row_tile(t, tm)

    def body(x_ref, t_ref, dx_ref, l_ref):
        i = pl.program_id(0)
        err = x_ref[...] - t_ref[...]
        dx_ref[...] = err * (1.0 / d)
        part = 0.5 * jnp.sum(jnp.mean(err * err, axis=-1, keepdims=True), axis=0, keepdims=True)
        part = jnp.broadcast_to(part, l_ref.shape)

        @pl.when(i == 0)
        def _():
            l_ref[...] = part

        @pl.when(i > 0)
        def _():
            l_ref[...] += part

    row = pl.BlockSpec((tm, d), lambda i: (i, 0))
    return pl.pallas_call(
        body, name=name, grid=(t // tm,), in_specs=[row, row],
        out_specs=[row, pl.BlockSpec((8, LANES), lambda i: (0, 0))],
        out_shape=[jax.ShapeDtypeStruct((t, d), F32), jax.ShapeDtypeStruct((8, LANES), F32)],
        compiler_params=_cp("arbitrary"),
    )(x, target)


CONV_HALO = 32
CONV_CHUNK = 128


def glu_conv_fwd(z, dw, dwb, name, tc=128):
    b, s, c2 = z.shape
    c = c2 // 2
    kw = dw.shape[0]
    tc = min(tc, c)
    nc = c // tc
    ch = min(CONV_CHUNK, s)
    halo = CONV_HALO
    assert kw - 1 <= halo and s % ch == 0

    nch = s // ch

    def body(a_ref, g_ref, w_ref, b_ref, o_ref, pad_ref):
        _fill_glu_slabs(a_ref, g_ref, pad_ref, nch, ch, halo)

        def chunk(ci, carry):
            r0 = pl.multiple_of(ci * ch, ch)
            acc = b_ref[...]
            for k, tap in enumerate(_taps_front(pad_ref, ci, kw, ch, halo)):
                acc = acc + w_ref[k:k + 1, :] * tap
            o_ref[pl.ds(r0, ch), :] = acc
            return carry

        lax.fori_loop(0, nch, chunk, 0)

    return pl.pallas_call(
        body, name=name, grid=(b, nc),
        in_specs=[
            pl.BlockSpec((None, s, tc), lambda bi, i: (bi, 0, i)),
            pl.BlockSpec((None, s, tc), lambda bi, i: (bi, 0, i + nc)),
            pl.BlockSpec((kw, tc), lambda bi, i: (0, i)),
            pl.BlockSpec((1, tc), lambda bi, i: (0, i)),
        ],
        out_specs=pl.BlockSpec((None, s, tc), lambda bi, i: (bi, 0, i)),
        out_shape=jax.ShapeDtypeStruct((b, s, c), F32),
        scratch_shapes=[pltpu.VMEM((nch, ch + halo, tc), F32)],
        compiler_params=_cp("parallel", "parallel"),
    )(z, z, dw, dwb)


def glu_conv_bwd(z, dw, du2, name, tc=128):
    b, s, c2 = z.shape
    c = c2 // 2
    kw = dw.shape[0]
    tc = min(tc, c)
    nc = c // tc
    ch = min(CONV_CHUNK, s)
    nch = s // ch
    halo = CONV_HALO

    def body(a_ref, g_ref, w_ref, du_ref, dza_ref, dzg_ref, ddw_ref, ddwb_ref, dba_ref, dbg_ref, upad_ref, dpad_ref):
        bi = pl.program_id(1)

        @pl.when(bi == 0)
        def _():
            ddw_ref[...] = jnp.zeros(ddw_ref.shape, F32)
            ddwb_ref[...] = jnp.zeros(ddwb_ref.shape, F32)
            dba_ref[...] = jnp.zeros(dba_ref.shape, F32)
            dbg_ref[...] = jnp.zeros(dbg_ref.shape, F32)

        _fill_glu_slabs(a_ref, g_ref, upad_ref, nch, ch, halo)
        dpad_ref[nch - 1, ch:ch + halo, :] = jnp.zeros((halo, tc), F32)
        dpad_ref[nch - 1, 0:ch, :] = du_ref[s - ch:s, :]

        def fill(ci, carry):
            r0 = pl.multiple_of(ci * ch, ch)
            dpad_ref[ci, :, :] = du_ref[pl.ds(r0, ch + halo), :]
            return carry

        lax.fori_loop(0, nch - 1, fill, 0)

        def chunk(ci, carry):
            r0 = pl.multiple_of(ci * ch, ch)
            du_c = du_ref[pl.ds(r0, ch), :]
            taps_u = _taps_front(upad_ref, ci, kw, ch, halo)
            taps_d = _taps_at(dpad_ref, ci, list(range(kw)), ch)
            du1 = w_ref[kw - 1:kw, :] * du_c
            ddw_ref[kw - 1] += jnp.sum((du_c * taps_u[kw - 1]).reshape(ch // 8, 8, tc), axis=0)
            for j in range(1, kw):
                du1 = du1 + w_ref[kw - 1 - j:kw - j, :] * taps_d[j]
                ddw_ref[kw - 1 - j] += jnp.sum((du_c * taps_u[kw - 1 - j]).reshape(ch // 8, 8, tc), axis=0)
            av = a_ref[pl.ds(r0, ch), :]
            sg = _sigmoid(g_ref[pl.ds(r0, ch), :])
            dza = du1 * sg
            dzg = du1 * av * (sg * (1.0 - sg))
            dza_ref[pl.ds(r0, ch), :] = dza.astype(BF16)
            dzg_ref[pl.ds(r0, ch), :] = dzg.astype(BF16)
            dba_ref[...] += jnp.sum(dza, axis=0, keepdims=True)
            dbg_ref[...] += jnp.sum(dzg, axis=0, keepdims=True)
            ddwb_ref[...] += jnp.sum(du_c, axis=0, keepdims=True)
            return carry

        lax.fori_loop(0, s // ch, chunk, 0)

    blk = lambda off: pl.BlockSpec((None, s, tc), lambda i, bi: (bi, 0, i + off))
    vec = pl.BlockSpec((1, tc), lambda i, bi: (0, i))
    return pl.pallas_call(
        body, name=name, grid=(nc, b),
        in_specs=[blk(0), blk(nc), pl.BlockSpec((kw, tc), lambda i, bi: (0, i)), blk(0)],
        out_specs=[blk(0), blk(0), pl.BlockSpec((kw, 8, tc), lambda i, bi: (0, 0, i)), vec, vec, vec],
        out_shape=[
            jax.ShapeDtypeStruct((b, s, c), BF16), jax.ShapeDtypeStruct((b, s, c), BF16),
            jax.ShapeDtypeStruct((kw, 8, c), F32), jax.ShapeDtypeStruct((1, c), F32),
            jax.ShapeDtypeStruct((1, c), F32), jax.ShapeDtypeStruct((1, c), F32),
        ],
        scratch_shapes=[pltpu.VMEM((nch, ch + halo, tc), F32), pltpu.VMEM((nch, ch + halo, tc), F32)],
        compiler_params=_cp("parallel", "arbitrary"),
    )(z, z, dw, du2)


def _fill_glu_slabs(a_ref, g_ref, pad_ref, nch, ch, halo):
    tc = a_ref.shape[-1]
    pad_ref[0, 0:halo, :] = jnp.zeros((halo, tc), F32)
    pad_ref[0, halo:halo + ch, :] = a_ref[0:ch, :] * _sigmoid(g_ref[0:ch, :])

    def fill(ci, carry):
        r0 = pl.multiple_of(ci * ch, ch)
        pad_ref[ci, 0:halo, :] = pad_ref[ci - 1, ch:ch + halo, :]
        pad_ref[ci, halo:halo + ch, :] = a_ref[pl.ds(r0, ch), :] * _sigmoid(g_ref[pl.ds(r0, ch), :])
        return carry

    lax.fori_loop(1, nch, fill, 0)


def ln_silu_fwd(u, g, bvec, name, tm=512):
    t, d = u.shape
    tm = _row_tile(t, tm)

    def body(u_ref, g_ref, b_ref, o_ref):
        uv = u_ref[...]
        mu = jnp.mean(uv, axis=-1, keepdims=True)
        xc = uv - mu
        var = jnp.mean(xc * xc, axis=-1, keepdims=True)
        v = (xc * lax.rsqrt(var + EPS)) * g_ref[...] + b_ref[...]
        o_ref[...] = (v * _sigmoid(v)).astype(BF16)

    row = pl.BlockSpec((tm, d), lambda i: (i, 0))
    vec = pl.BlockSpec((1, d), lambda i: (0, 0))
    return pl.pallas_call(
        body, name=name, grid=(t // tm,), in_specs=[row, vec, vec], out_specs=row,
        out_shape=jax.ShapeDtypeStruct((t, d), BF16), compiler_params=_cp("parallel"),
    )(u, g, bvec)


def ln_silu_bwd(u, g, bvec, dout, name, tm=512):
    t, d = u.shape
    tm = _row_tile(t, tm)

    def body(u_ref, g_ref, b_ref, do_ref, du_ref, dg_ref, db_ref):
        i = pl.program_id(0)
        uv = u_ref[...]
        mu = jnp.mean(uv, axis=-1, keepdims=True)
        xc = uv - mu
        var = jnp.mean(xc * xc, axis=-1, keepdims=True)
        rstd = lax.rsqrt(var + EPS)
        n = xc * rstd
        v = n * g_ref[...] + b_ref[...]
        sg = _sigmoid(v)
        dv = do_ref[...].astype(F32) * (sg * (1.0 + v * (1.0 - sg)))
        dn = dv * g_ref[...]
        du_ref[...] = rstd * (dn - jnp.mean(dn, axis=-1, keepdims=True) - n * jnp.mean(dn * n, axis=-1, keepdims=True))
        dg = jnp.sum(dv * n, axis=0, keepdims=True)
        db = jnp.sum(dv, axis=0, keepdims=True)

        @pl.when(i == 0)
        def _():
            dg_ref[...] = dg
            db_ref[...] = db

        @pl.when(i > 0)
        def _():
            dg_ref[...] += dg
            db_ref[...] += db

    row = pl.BlockSpec((tm, d), lambda i: (i, 0))
    vec = pl.BlockSpec((1, d), lambda i: (0, 0))
    return pl.pallas_call(
        body, name=name, grid=(t // tm,), in_specs=[row, vec, vec, row], out_specs=[row, vec, vec],
        out_shape=[jax.ShapeDtypeStruct((t, d), F32), jax.ShapeDtypeStruct((1, d), F32), jax.ShapeDtypeStruct((1, d), F32)],
        compiler_params=_cp("arbitrary"),
    )(u, g, bvec, dout)


FFN_HALO = 8


def _fill_front_halo(src_ref, pad_ref, nch, ch, halo):
    tc = src_ref.shape[-1]
    pad_ref[0, 0:halo, :] = jnp.zeros((halo, tc), F32)
    pad_ref[0, halo:halo + ch, :] = src_ref[0:ch, :].astype(F32)

    def fill(ci, carry):
        r0 = pl.multiple_of(ci * ch, ch)
        pad_ref[ci, 0:halo, :] = src_ref[pl.ds(r0 - 2 * halo, 2 * halo), :].astype(F32)[halo:, :]
        pad_ref[ci, halo:halo + ch, :] = src_ref[pl.ds(r0, ch), :].astype(F32)
        return carry

    lax.fori_loop(1, nch, fill, 0)


def _taps_at(pad_ref, ci, offsets, ch):
    windows = {}
    for b in sorted({o % 8 for o in offsets}):
        top = max(o for o in offsets if o % 8 == b)
        windows[b] = pad_ref[ci, b:top + ch, :]
    return [windows[o % 8][o - o % 8:o - o % 8 + ch, :] for o in offsets]


def _taps_front(pad_ref, ci, kw, ch, halo):
    return _taps_at(pad_ref, ci, [halo - (kw - 1 - k) for k in range(kw)], ch)


def ffn_mid_fwd(p, dw, dwb, name, tc=256):
    b, s, f2 = p.shape
    f = f2 // 2
    kw = dw.shape[0]
    tc = min(tc, f)
    nf = f // tc
    ch = min(CONV_CHUNK, s)
    nch = s // ch
    halo = FFN_HALO

    def body(pa_ref, pg_ref, wa_ref, wg_ref, ba_ref, bg_ref, o_ref, apad_ref, gpad_ref):
        _fill_front_halo(pa_ref, apad_ref, nch, ch, halo)
        _fill_front_halo(pg_ref, gpad_ref, nch, ch, halo)

        def chunk(ci, carry):
            r0 = pl.multiple_of(ci * ch, ch)
            ca = ba_ref[...]
            cg = bg_ref[...]
            taps = zip(_taps_front(apad_ref, ci, kw, ch, halo), _taps_front(gpad_ref, ci, kw, ch, halo))
            for k, (ta, tg) in enumerate(taps):
                ca = ca + wa_ref[k:k + 1, :] * ta
                cg = cg + wg_ref[k:k + 1, :] * tg
            o_ref[pl.ds(r0, ch), :] = ((cg * _sigmoid(cg)) * ca).astype(BF16)
            return carry

        lax.fori_loop(0, nch, chunk, 0)

    blk = lambda off: pl.BlockSpec((None, s, tc), lambda bi, i: (bi, 0, i + off))
    wsp = lambda off: pl.BlockSpec((kw, tc), lambda bi, i: (0, i + off))
    bsp = lambda off: pl.BlockSpec((1, tc), lambda bi, i: (0, i + off))
    return pl.pallas_call(
        body, name=name, grid=(b, nf),
        in_specs=[blk(0), blk(nf), wsp(0), wsp(nf), bsp(0), bsp(nf)],
        out_specs=pl.BlockSpec((None, s, tc), lambda bi, i: (bi, 0, i)),
        out_shape=jax.ShapeDtypeStruct((b, s, f), BF16),
        scratch_shapes=[pltpu.VMEM((nch, ch + halo, tc), F32)] * 2,
        compiler_params=_cp("parallel", "parallel"),
    )(p, p, dw, dw, dwb, dwb)


def ffn_mid_bwd(p, dw, dwb, ds, name, tc=256):
    b, s, f2 = p.shape
    f = f2 // 2
    kw = dw.shape[0]
    tc = min(tc, f)
    nf = f // tc
    ch = min(CONV_CHUNK, s)
    nch = s // ch
    halo = FFN_HALO

    def sum8(v):
        return jnp.sum(v.reshape(ch // 8, 8, tc), axis=0)

    def body(pa_ref, pg_ref, wa_ref, wg_ref, ba_ref, bg_ref, ds_ref, dpa_ref, dpg_ref, ddwa_ref, ddwg_ref, dba_ref, dbg_ref,
             apad_ref, gpad_ref, dca_ref, dcg_ref):
        bi = pl.program_id(1)

        @pl.when(bi == 0)
        def _():
            ddwa_ref[...] = jnp.zeros(ddwa_ref.shape, F32)
            ddwg_ref[...] = jnp.zeros(ddwg_ref.shape, F32)
            dba_ref[...] = jnp.zeros(dba_ref.shape, F32)
            dbg_ref[...] = jnp.zeros(dbg_ref.shape, F32)

        _fill_front_halo(pa_ref, apad_ref, nch, ch, halo)
        _fill_front_halo(pg_ref, gpad_ref, nch, ch, halo)
        dca_ref[nch - 1, ch:ch + halo, :] = jnp.zeros((halo, tc), F32)
        dcg_ref[nch - 1, ch:ch + halo, :] = jnp.zeros((halo, tc), F32)

        def grads(ci, carry):
            acc_a, acc_g, sb_a, sb_g = carry
            r0 = pl.multiple_of(ci * ch, ch)
            taps_a = _taps_front(apad_ref, ci, kw, ch, halo)
            taps_g = _taps_front(gpad_ref, ci, kw, ch, halo)
            ca = ba_ref[...]
            cg = bg_ref[...]
            for k in range(kw):
                ca = ca + wa_ref[k:k + 1, :] * taps_a[k]
                cg = cg + wg_ref[k:k + 1, :] * taps_g[k]
            sg = _sigmoid(cg)
            dsv = ds_ref[pl.ds(r0, ch), :].astype(F32)
            dca = dsv * (cg * sg)
            dcg = dsv * ca * (sg * (1.0 + cg * (1.0 - sg)))
            dca_ref[ci, 0:ch, :] = dca
            dcg_ref[ci, 0:ch, :] = dcg

            prev = jnp.maximum(ci - 1, 0)

            @pl.when(ci > 0)
            def _():
                dca_ref[prev, ch:ch + halo, :] = dca[0:halo, :]
                dcg_ref[prev, ch:ch + halo, :] = dcg[0:halo, :]

            acc_a = tuple(acc_a[k] + sum8(dca * taps_a[k]) for k in range(kw))
            acc_g = tuple(acc_g[k] + sum8(dcg * taps_g[k]) for k in range(kw))
            return acc_a, acc_g, sb_a + sum8(dca), sb_g + sum8(dcg)

        z8 = jnp.zeros((8, tc), F32)
        acc_a, acc_g, sb_a, sb_g = lax.fori_loop(0, nch, grads, ((z8,) * kw, (z8,) * kw, z8, z8))
        for k in range(kw):
            ddwa_ref[k] += acc_a[k]
            ddwg_ref[k] += acc_g[k]
        dba_ref[...] += jnp.sum(sb_a, axis=0, keepdims=True)
        dbg_ref[...] += jnp.sum(sb_g, axis=0, keepdims=True)

        def back(ci, carry):
            r0 = pl.multiple_of(ci * ch, ch)
            da = wa_ref[kw - 1:kw, :] * dca_ref[ci, 0:ch, :]
            dg = wg_ref[kw - 1:kw, :] * dcg_ref[ci, 0:ch, :]
            for j in range(1, kw):
                da = da + wa_ref[kw - 1 - j:kw - j, :] * dca_ref[ci, j:j + ch, :]
                dg = dg + wg_ref[kw - 1 - j:kw - j, :] * dcg_ref[ci, j:j + ch, :]
            dpa_ref[pl.ds(r0, ch), :] = da.astype(BF16)
            dpg_ref[pl.ds(r0, ch), :] = dg.astype(BF16)
            return carry

        lax.fori_loop(0, nch, back, 0)

    blk = lambda off: pl.BlockSpec((None, s, tc), lambda i, bi: (bi, 0, i + off))
    wsp = lambda off: pl.BlockSpec((kw, tc), lambda i, bi: (0, i + off))
    bsp = lambda off: pl.BlockSpec((1, tc), lambda i, bi: (0, i + off))
    acc3 = pl.BlockSpec((kw, 8, tc), lambda i, bi: (0, 0, i))
    vec = pl.BlockSpec((1, tc), lambda i, bi: (0, i))
    return pl.pallas_call(
        body, name=name, grid=(nf, b),
        in_specs=[blk(0), blk(nf), wsp(0), wsp(nf), bsp(0), bsp(nf), blk(0)],
        out_specs=[blk(0), blk(0), acc3, acc3, vec, vec],
        out_shape=[jax.ShapeDtypeStruct((b, s, f), BF16), jax.ShapeDtypeStruct((b, s, f), BF16),
                   jax.ShapeDtypeStruct((kw, 8, f), F32), jax.ShapeDtypeStruct((kw, 8, f), F32),
                   jax.ShapeDtypeStruct((1, f), F32), jax.ShapeDtypeStruct((1, f), F32)],
        scratch_shapes=[pltpu.VMEM((nch, ch + halo, tc), F32)] * 4,
        compiler_params=_cp("parallel", "arbitrary"),
    )(p, p, dw, dw, dwb, dwb, ds)


def _tile_rows(r, n, dil):
    start = r + n * BLK * dil
    return pl.ds(start, BLK, stride=dil) if dil > 1 else pl.ds(start, BLK)


def _band_masks():
    qi = lax.broadcasted_iota(jnp.int32, (BLK, 2 * BLK), 0)
    kk = lax.broadcasted_iota(jnp.int32, (BLK, 2 * BLK), 1)
    both = jnp.logical_or(jnp.logical_and(kk < BLK, kk >= qi), jnp.logical_and(kk >= BLK, kk - BLK <= qi))
    return both, kk[:, :BLK] <= qi[:, :BLK]


def attn_fwd(q, kv, g, dil, hw, name):
    _, b, s, _ = q.shape
    nh = hw // HEAD_DIM
    nblk = s // dil // BLK
    scale = 1.0 / math.sqrt(HEAD_DIM)

    def body(q_ref, k_ref, v_ref, o_ref, lse_ref):
        h = pl.program_id(1)
        mask2, mask1 = _band_masks()
        mine = lax.broadcasted_iota(jnp.int32, (BLK, LANES), 1) == h

        @pl.when(h == 0)
        def _():
            lse_ref[...] = jnp.zeros(lse_ref.shape, F32)

        for r in range(dil):
            kp = vp = None
            for n in range(nblk):
                rs = _tile_rows(r, n, dil)
                qt = q_ref[rs, :].astype(BF16)
                kc = k_ref[rs, :].astype(BF16)
                vc = v_ref[rs, :].astype(BF16)
                if n == 0:
                    kcat, vcat, mask = kc, vc, mask1
                else:
                    kcat, vcat, mask = jnp.concatenate([kp, kc], axis=0), jnp.concatenate([vp, vc], axis=0), mask2
                sc = jnp.where(mask, _dot_nt(qt, kcat) * scale, NEG_INF)
                m = jnp.max(sc, axis=-1, keepdims=True)
                p = jnp.exp(sc - m)
                den = jnp.sum(p, axis=-1, keepdims=True)
                o_ref[rs, :] = _dot(p.astype(BF16), vcat) / den
                lse_ref[rs, :] = jnp.where(mine, m + jnp.log(den), lse_ref[rs, :])
                kp, vp = kc, vc

    col = lambda base: pl.BlockSpec((None, s, HEAD_DIM), lambda bi, h: (bi, 0, base + h))
    head = lambda base: pl.BlockSpec((None, None, s, HEAD_DIM), lambda bi, h: (base + h, bi, 0, 0))
    return pl.pallas_call(
        body, name=name, grid=(b, nh),
        in_specs=[head(g * nh), head(g * nh), head((N_GROUPS + g) * nh)],
        out_specs=[col(0), pl.BlockSpec((None, s, LANES), lambda bi, h: (bi, 0, 0))],
        out_shape=[jax.ShapeDtypeStruct((b, s, hw), F32), jax.ShapeDtypeStruct((b, s, LANES), F32)],
        compiler_params=_cp("parallel", "arbitrary"),
    )(q, kv, kv)


def attn_merge(outs, lses, name, tm=512):
    t, hw = outs[0].shape
    nh = hw // HEAD_DIM
    tm = _row_tile(t, tm)
    ng = len(outs)

    def body(*refs):
        o_refs, l_refs = refs[:ng], refs[ng:2 * ng]
        m_ref, lj_ref = refs[2 * ng:]
        ls = [l_refs[g][...] for g in range(ng)]
        mx = ls[0]
        for g in range(1, ng):
            mx = jnp.maximum(mx, ls[g])
        es = [jnp.exp(l - mx) for l in ls]
        tot = es[0]
        for g in range(1, ng):
            tot = tot + es[g]
        ws = [e / tot for e in es]
        lj_ref[...] = mx + jnp.log(tot)
        for h in range(nh):
            sl = slice(h * HEAD_DIM, (h + 1) * HEAD_DIM)
            acc = ws[0][:, h:h + 1] * o_refs[0][:, sl]
            for g in range(1, ng):
                acc = acc + ws[g][:, h:h + 1] * o_refs[g][:, sl]
            m_ref[:, sl] = acc.astype(BF16)

    row = pl.BlockSpec((tm, hw), lambda i: (i, 0))
    st = pl.BlockSpec((tm, LANES), lambda i: (i, 0))
    return pl.pallas_call(
        body, name=name, grid=(t // tm,), in_specs=[row] * ng + [st] * ng, out_specs=[row, st],
        out_shape=[jax.ShapeDtypeStruct((t, hw), BF16), jax.ShapeDtypeStruct((t, LANES), F32)],
        compiler_params=_cp("parallel"),
    )(*outs, *lses)


def attn_bwd_prep(dmerged, merged, name, tm=512):
    t, hw = merged.shape
    nh = hw // HEAD_DIM
    tm = _row_tile(t, tm)

    def body(d_ref, m_ref, o_ref):
        lane = lax.broadcasted_iota(jnp.int32, (tm, LANES), 1)
        acc = jnp.zeros((tm, LANES), F32)
        for h in range(nh):
            sl = slice(h * HEAD_DIM, (h + 1) * HEAD_DIM)
            dsum = jnp.sum(d_ref[:, sl] * m_ref[:, sl].astype(F32), axis=-1, keepdims=True)
            acc = jnp.where(lane == h, dsum, acc)
        o_ref[...] = acc

    row = pl.BlockSpec((tm, hw), lambda i: (i, 0))
    return pl.pallas_call(
        body, name=name, grid=(t // tm,), in_specs=[row, row], out_specs=pl.BlockSpec((tm, LANES), lambda i: (i, 0)),
        out_shape=jax.ShapeDtypeStruct((t, LANES), F32), compiler_params=_cp("parallel"),
    )(dmerged, merged)


def attn_bwd(q, kv, g, dil, do, lsej, dm, dq_buf, dk_buf, dv_buf, accumulate, hw, name):
    _, b, s, _ = q.shape
    nh = hw // HEAD_DIM
    nblk = s // dil // BLK
    scale = 1.0 / math.sqrt(HEAD_DIM)
    assert dk_buf is not None or not accumulate
    kv_at = 6 + (dq_buf is not None)

    def body(*refs):
        q_ref, k_ref, v_ref, do_ref, lj_ref, dm_ref = refs[:6]
        dq_ref, dk_ref, dv_ref = refs[-3:]
        dki_ref, dvi_ref = (refs[kv_at], refs[kv_at + 1]) if accumulate else (None, None)
        mask2, mask1 = _band_masks()
        mine = lax.broadcasted_iota(jnp.int32, (BLK, LANES), 1) == pl.program_id(1)

        def my_lane(v):
            return jnp.sum(jnp.where(mine, v, 0.0), axis=-1, keepdims=True)

        def put(rs, dk, dv):
            if accumulate:
                dk = dk + dki_ref[rs, :]
                dv = dv + dvi_ref[rs, :]
            dk_ref[rs, :] = dk
            dv_ref[rs, :] = dv

        for r in range(dil):
            kp = vp = hold_k = hold_v = rs_prev = None
            for n in range(nblk):
                rs = _tile_rows(r, n, dil)
                qt = q_ref[rs, :].astype(BF16)
                kc = k_ref[rs, :].astype(BF16)
                vc = v_ref[rs, :].astype(BF16)
                dot = do_ref[rs, :].astype(BF16)
                lm = my_lane(lj_ref[rs, :])
                dmm = my_lane(dm_ref[rs, :])
                if n == 0:
                    kcat, vcat, mask = kc, vc, mask1
                else:
                    kcat, vcat, mask = jnp.concatenate([kp, kc], axis=0), jnp.concatenate([vp, vc], axis=0), mask2
                p = jnp.exp(jnp.where(mask, _dot_nt(qt, kcat) * scale, NEG_INF) - lm)
                ds = (p * (_dot_nt(dot, vcat) - dmm)).astype(BF16)
                dq_ref[rs, :] = _dot(ds, kcat) * scale
                dkc = _dot_tn(ds, qt) * scale
                dvc = _dot_tn(p.astype(BF16), dot)
                if n > 0:
                    put(rs_prev, hold_k + dkc[:BLK, :], hold_v + dvc[:BLK, :])
                    dkc, dvc = dkc[BLK:, :], dvc[BLK:, :]
                hold_k, hold_v, kp, vp, rs_prev = dkc, dvc, kc, vc, rs
            put(rs_prev, hold_k, hold_v)

    col = lambda base: pl.BlockSpec((None, s, HEAD_DIM), lambda bi, h: (bi, 0, base + h))
    any_spec = pl.BlockSpec(memory_space=pl.ANY)
    stat = pl.BlockSpec((None, s, LANES), lambda bi, h: (bi, 0, 0))
    head = lambda base: pl.BlockSpec((None, None, s, HEAD_DIM), lambda bi, h: (base + h, bi, 0, 0))
    in_specs = [head(g * nh), head(g * nh), head((N_GROUPS + g) * nh), col(0), stat, stat]
    args = [q, kv, kv, do, lsej, dm]
    aliases = {}
    if dq_buf is not None:
        in_specs.append(any_spec)
        args.append(dq_buf)
        aliases[6] = 0
    if dk_buf is not None:
        in_specs += [col(g * nh) if accumulate else any_spec] * 2
        args += [dk_buf, dv_buf]
        aliases.update({kv_at: 1, kv_at + 1: 2})
    shape = jax.ShapeDtypeStruct((b, s, N_GROUPS * hw), F32)
    return pl.pallas_call(
        body, name=name, grid=(b, nh), in_specs=in_specs, out_specs=[col(g * nh)] * 3, out_shape=[shape] * 3,
        input_output_aliases=aliases, compiler_params=_cp("parallel", "parallel"),
    )(*args)


def sum_parts(g, recv, me, name, tm=512):
    _, rows, c = g.shape
    n = recv.shape[0]
    tm = _row_tile(rows, tm)

    def body(me_ref, g_ref, r_ref, o_ref):
        acc = g_ref[...].astype(F32)
        for j in range(n):
            acc = acc + r_ref[j].astype(F32)
        o_ref[...] = acc

    return pl.pallas_call(
        body, name=name,
        grid_spec=pltpu.PrefetchScalarGridSpec(
            num_scalar_prefetch=1, grid=(rows // tm,),
            in_specs=[pl.BlockSpec((None, tm, c), lambda i, me_ref: (me_ref[0], i, 0)),
                      pl.BlockSpec((n, tm, c), lambda i, me_ref: (0, i, 0))],
            out_specs=pl.BlockSpec((tm, c), lambda i, me_ref: (i, 0))),
        out_shape=jax.ShapeDtypeStruct((rows, c), F32), compiler_params=_cp("parallel"),
    )(me, g, recv)


def adamw(w, m, v, g_parts, name, tm=256):
    rows, c = w.shape
    tm = _row_tile(rows, tm)
    npart = len(g_parts)

    def body(*refs):
        w_ref, m_ref, v_ref = refs[:3]
        g_refs = refs[3:3 + npart]
        go_ref, d_ref, mo_ref, vo_ref = refs[3 + npart:]
        g = g_refs[0][...]
        for k in range(1, npart):
            g = g + g_refs[k][...]
        mn = ADAM_B1 * m_ref[...] + (1.0 - ADAM_B1) * g
        vn = ADAM_B2 * v_ref[...] + (1.0 - ADAM_B2) * (g * g)
        m_hat = mn / (1.0 - ADAM_B1 ** ADAM_STEP)
        v_hat = vn / (1.0 - ADAM_B2 ** ADAM_STEP)
        go_ref[...] = g
        d_ref[...] = -ADAM_LR * (m_hat / (jnp.sqrt(v_hat) + ADAM_EPS) + ADAM_WD * w_ref[...])
        mo_ref[...] = mn
        vo_ref[...] = vn

    row = pl.BlockSpec((tm, c), lambda i: (i, 0))
    return pl.pallas_call(
        body, name=name, grid=(rows // tm,), in_specs=[row] * (3 + npart), out_specs=[row] * 4,
        out_shape=[jax.ShapeDtypeStruct((rows, c), F32)] * 4, compiler_params=_cp("parallel"),
    )(w, m, v, *g_parts)


def _place():
    return lax.axis_index("x"), lax.axis_index("y"), lax.axis_index("c")


def _other_chips(x, y, c):
    return [(1 - x, y, c), (x, 1 - y, c), (1 - x, 1 - y, c)]


def _chip_of(px, py):
    return 2 * px + py


HBM_SPEC = pl.BlockSpec(memory_space=pltpu.HBM)
SEM_SPEC = pl.BlockSpec(memory_space=pltpu.SEMAPHORE)
ANY_SPEC = pl.BlockSpec(memory_space=pl.ANY)
DATAFLOW = pltpu.SideEffectType.DATAFLOW_SIDE_EFFECTING
N_PEER_CHIPS = N_CHIPS - 1


def _hbm(a):
    return pltpu.with_memory_space_constraint(a, pltpu.HBM)


def _hbm_like(arrays):
    return [pltpu.HBM(a.shape, a.dtype) for a in arrays]


def cast_place(w, layer, me, out_dtype, name, tm=512, nslots=N_CHIPS, dep=None):
    rows, c = w.shape[-2:]
    tm = _row_tile(rows, tm)

    def body(me_ref, w_ref, *rest):
        rest[-1][...] = w_ref[...].astype(out_dtype)

    if layer is None:
        in_specs = [pl.BlockSpec((tm, c), lambda i, me_ref: (i, 0))]
    else:
        in_specs = [pl.BlockSpec((None, tm, c), lambda i, me_ref: (layer, i, 0))]
    args = [me, w]
    if dep is not None:
        in_specs.append(pl.BlockSpec(DEP_SPEC_SHAPE, lambda i, me_ref: (0, 0)))
        args.append(dep)
    return pl.pallas_call(
        body, name=name,
        grid_spec=pltpu.PrefetchScalarGridSpec(
            num_scalar_prefetch=1, grid=(rows // tm,), in_specs=in_specs,
            out_specs=pl.BlockSpec((None, tm, c), lambda i, me_ref: (me_ref[0], i, 0))),
        out_shape=jax.ShapeDtypeStruct((nslots, rows, c), out_dtype), compiler_params=_cp("parallel"),
    )(*args)


def gather_start(lands, chunk_sizes, name="gather_start"):
    n = len(lands)
    nch = len(chunk_sizes)
    assert sum(chunk_sizes) == n

    def body(*refs):
        land_refs = refs[:n]
        outs = refs[n:]
        send_sems, recv_sems = outs[:nch], outs[nch:2 * nch]
        token = outs[-1]
        x, y, c = _place()
        me = _chip_of(x, y)
        peers = _other_chips(x, y, c)
        k = 0
        for ck, size in enumerate(chunk_sizes):
            for pos in range(size):
                for r, peer in enumerate(peers):
                    pltpu.make_async_remote_copy(
                        src_ref=land_refs[k].at[me], dst_ref=land_refs[k].at[me],
                        send_sem=send_sems[ck].at[N_PEER_CHIPS * pos + r], recv_sem=recv_sems[ck].at[N_PEER_CHIPS * pos + r],
                        device_id=peer, device_id_type=MESH).start()
                k += 1
        token[...] = jnp.zeros(token.shape, F32)

    sems = [pltpu.SemaphoreType.DMA((N_PEER_CHIPS * s,)) for s in chunk_sizes]
    res = pl.pallas_call(
        body, name=name,
        out_shape=(*sems, *sems, *_hbm_like(lands), jax.ShapeDtypeStruct(DEP_SPEC_SHAPE, F32)),
        in_specs=[HBM_SPEC] * n,
        out_specs=(*[SEM_SPEC] * (2 * nch), *[HBM_SPEC] * n, pl.BlockSpec(memory_space=pltpu.VMEM)),
        input_output_aliases={k: 2 * nch + k for k in range(n)},
        compiler_params=pltpu.CompilerParams(has_side_effects=DATAFLOW),
    )(*[_hbm(a) for a in lands])
    return res[:nch], res[nch:2 * nch], res[2 * nch:2 * nch + n], res[-1]


def gather_wait(send_sem, recv_sem, lands, after, name):
    n = len(lands)

    def body(*refs):
        land_refs = refs[:n]
        ssem, rsem = refs[n], refs[n + 1]
        x, y, c = _place()
        me = _chip_of(x, y)
        for pos in range(n):
            for r, peer in enumerate(_other_chips(x, y, c)):
                cp = pltpu.make_async_remote_copy(
                    src_ref=land_refs[pos].at[me], dst_ref=land_refs[pos].at[_chip_of(peer[0], peer[1])],
                    send_sem=ssem.at[N_PEER_CHIPS * pos + r], recv_sem=rsem.at[N_PEER_CHIPS * pos + r],
                    device_id=peer, device_id_type=MESH)
                cp.wait_send()
                cp.wait_recv()

    return pl.pallas_call(
        body, name=name, out_shape=tuple(_hbm_like(lands)),
        in_specs=[*[HBM_SPEC] * n, SEM_SPEC, SEM_SPEC, ANY_SPEC], out_specs=[HBM_SPEC] * n,
        input_output_aliases={k: k for k in range(n)},
        compiler_params=pltpu.CompilerParams(has_side_effects=DATAFLOW),
    )(*lands, send_sem, recv_sem, after)


def scatter_start(grads, name):
    n = len(grads)
    recvs = [lax.empty((N_PEER_CHIPS, *g.shape[1:]), g.dtype) for g in grads]

    def body(*refs):
        g_refs, r_refs = refs[:n], refs[n:2 * n]
        send_sems, recv_sems = refs[2 * n], refs[2 * n + 1]
        token = refs[-1]
        x, y, c = _place()
        for k in range(n):
            for r, peer in enumerate(_other_chips(x, y, c)):
                pltpu.make_async_remote_copy(
                    src_ref=g_refs[k].at[_chip_of(peer[0], peer[1])], dst_ref=r_refs[k].at[r],
                    send_sem=send_sems.at[N_PEER_CHIPS * k + r], recv_sem=recv_sems.at[N_PEER_CHIPS * k + r],
                    device_id=peer, device_id_type=MESH).start()
        token[...] = jnp.zeros(token.shape, F32)

    sem = pltpu.SemaphoreType.DMA((N_PEER_CHIPS * n,))
    res = pl.pallas_call(
        body, name=name,
        out_shape=(sem, sem, *_hbm_like(grads), *_hbm_like(recvs), jax.ShapeDtypeStruct(DEP_SPEC_SHAPE, F32)),
        in_specs=[HBM_SPEC] * (2 * n),
        out_specs=(SEM_SPEC, SEM_SPEC, *[HBM_SPEC] * (2 * n), pl.BlockSpec(memory_space=pltpu.VMEM)),
        input_output_aliases={k: 2 + k for k in range(2 * n)},
        compiler_params=pltpu.CompilerParams(has_side_effects=DATAFLOW),
    )(*[_hbm(a) for a in grads], *[_hbm(a) for a in recvs])
    return res[0], res[1], res[2:2 + n], res[2 + n:2 + 2 * n], res[-1]


def scatter_wait(send_sem, recv_sem, grads, recvs, after, name):
    n = len(grads)

    def body(*refs):
        g_refs, r_refs = refs[:n], refs[n:2 * n]
        ssem, rsem = refs[2 * n], refs[2 * n + 1]
        x, y, c = _place()
        for k in range(n):
            for r, peer in enumerate(_other_chips(x, y, c)):
                cp = pltpu.make_async_remote_copy(
                    src_ref=g_refs[k].at[_chip_of(peer[0], peer[1])], dst_ref=r_refs[k].at[r],
                    send_sem=ssem.at[N_PEER_CHIPS * k + r], recv_sem=rsem.at[N_PEER_CHIPS * k + r],
                    device_id=peer, device_id_type=MESH)
                cp.wait_send()
                cp.wait_recv()

    res = pl.pallas_call(
        body, name=name, out_shape=(*_hbm_like(grads), *_hbm_like(recvs)),
        in_specs=[*[HBM_SPEC] * (2 * n), SEM_SPEC, SEM_SPEC, ANY_SPEC], out_specs=[HBM_SPEC] * (2 * n),
        input_output_aliases={k: k for k in range(2 * n)},
        compiler_params=pltpu.CompilerParams(has_side_effects=DATAFLOW),
    )(*grads, *recvs, send_sem, recv_sem, after)
    return res[:n], res[n:]


def swap_start(parts, name):
    n = len(parts)
    lands = [lax.empty(p.shape, p.dtype) for p in parts]

    def body(*refs):
        p_refs, l_refs = refs[:n], refs[n:2 * n]
        sems = refs[2 * n:4 * n]
        token = refs[-1]
        x, y, c = _place()
        for k in range(n):
            pltpu.make_async_remote_copy(
                src_ref=p_refs[k], dst_ref=l_refs[k], send_sem=sems[k], recv_sem=sems[n + k],
                device_id=(x, y, 1 - c), device_id_type=MESH).start()
        token[...] = jnp.zeros(token.shape, F32)

    sem = pltpu.SemaphoreType.DMA(())
    res = pl.pallas_call(
        body, name=name,
        out_shape=(*[sem] * (2 * n), *_hbm_like(parts), *_hbm_like(lands), jax.ShapeDtypeStruct(DEP_SPEC_SHAPE, F32)),
        in_specs=[HBM_SPEC] * (2 * n),
        out_specs=(*[SEM_SPEC] * (2 * n), *[HBM_SPEC] * (2 * n), pl.BlockSpec(memory_space=pltpu.VMEM)),
        input_output_aliases={k: 2 * n + k for k in range(2 * n)},
        compiler_params=pltpu.CompilerParams(has_side_effects=DATAFLOW),
    )(*[_hbm(a) for a in parts], *[_hbm(a) for a in lands])
    return res[:n], res[n:2 * n], res[2 * n:3 * n], res[3 * n:4 * n], res[-1]


def swap_wait(send_sem, recv_sem, part, land, after, name):
    def body(p_ref, l_ref, ssem, rsem, after_ref, p_out, l_out):
        x, y, c = _place()
        cp = pltpu.make_async_remote_copy(src_ref=p_ref, dst_ref=l_ref, send_sem=ssem, recv_sem=rsem,
                                          device_id=(x, y, 1 - c), device_id_type=MESH)
        cp.wait_send()
        cp.wait_recv()

    return pl.pallas_call(
        body, name=name, out_shape=tuple(_hbm_like([part, land])),
        in_specs=[HBM_SPEC, HBM_SPEC, SEM_SPEC, SEM_SPEC, ANY_SPEC], out_specs=[HBM_SPEC, HBM_SPEC],
        input_output_aliases={0: 0, 1: 1},
        compiler_params=pltpu.CompilerParams(has_side_effects=DATAFLOW),
    )(part, land, send_sem, recv_sem, after)


def _xor_peer(x, y, c, k):
    px, py, pc = x ^ ((k >> 2) & 1), y ^ ((k >> 1) & 1), c ^ (k & 1)
    return (px, py, pc), 4 * px + 2 * py + pc


def small_start(land, name="small_start"):
    def body(l_ref, ssem, rsem, l_out, token):
        x, y, c = _place()
        me = 4 * x + 2 * y + c
        for k in range(1, N_DEV):
            peer, _ = _xor_peer(x, y, c, k)
            pltpu.make_async_remote_copy(
                src_ref=l_ref.at[me], dst_ref=l_ref.at[me], send_sem=ssem.at[k - 1], recv_sem=rsem.at[k - 1],
                device_id=peer, device_id_type=MESH).start()
        token[...] = jnp.zeros(token.shape, F32)

    sem = pltpu.SemaphoreType.DMA((N_DEV - 1,))
    return pl.pallas_call(
        body, name=name,
        out_shape=(sem, sem, pltpu.HBM(land.shape, land.dtype), jax.ShapeDtypeStruct(DEP_SPEC_SHAPE, F32)),
        in_specs=[HBM_SPEC], out_specs=(SEM_SPEC, SEM_SPEC, HBM_SPEC, pl.BlockSpec(memory_space=pltpu.VMEM)),
        input_output_aliases={0: 2}, compiler_params=pltpu.CompilerParams(has_side_effects=DATAFLOW),
    )(_hbm(land))


def small_wait(send_sem, recv_sem, land, after, name="small_wait"):
    def body(l_ref, ssem, rsem, after_ref, l_out):
        x, y, c = _place()
        me = 4 * x + 2 * y + c
        for k in range(1, N_DEV):
            peer, slot = _xor_peer(x, y, c, k)
            cp = pltpu.make_async_remote_copy(
                src_ref=l_ref.at[me], dst_ref=l_ref.at[slot], send_sem=ssem.at[k - 1], recv_sem=rsem.at[k - 1],
                device_id=peer, device_id_type=MESH)
            cp.wait_send()
            cp.wait_recv()

    return pl.pallas_call(
        body, name=name, out_shape=pltpu.HBM(land.shape, land.dtype),
        in_specs=[HBM_SPEC, SEM_SPEC, SEM_SPEC, ANY_SPEC], out_specs=HBM_SPEC, input_output_aliases={0: 0},
        compiler_params=pltpu.CompilerParams(has_side_effects=DATAFLOW),
    )(land, send_sem, recv_sem, after)


def sum_slots(land, name="sum_slots", tm=256):
    n, rows, c = land.shape
    tm = _row_tile(rows, tm)

    def body(l_ref, o_ref):
        acc = l_ref[0]
        for j in range(1, n):
            acc = acc + l_ref[j]
        o_ref[...] = acc

    return pl.pallas_call(
        body, name=name, grid=(rows // tm,), in_specs=[pl.BlockSpec((n, tm, c), lambda i: (0, i, 0))],
        out_specs=pl.BlockSpec((tm, c), lambda i: (i, 0)), out_shape=jax.ShapeDtypeStruct((rows, c), F32),
        compiler_params=_cp("parallel"),
    )(land)


PACK_ROW_TILE = 256


def _pack(arrays):
    flat = jnp.concatenate([a.reshape(-1).astype(F32) for a in arrays])
    n = flat.shape[0]
    rows = -(-n // LANES)
    rows = -(-rows // PACK_ROW_TILE) * PACK_ROW_TILE
    return jnp.pad(flat, (0, rows * LANES - n)).reshape(rows, LANES)


def _unpack(packed, shapes, lead=()):
    flat = packed.reshape(*lead, -1)
    out, off = [], 0
    for shp in shapes:
        n = math.prod(shp)
        out.append(flat[..., off:off + n].reshape(*lead, *shp))
        off += n
    return out


def _row(vec):
    return vec.reshape(1, -1)


def kernel(x, mix_pre_g, mix_post_g, ffn_pre_g, ffn_post_g, cm_w_in, cm_b_in, cm_dw, cm_dw_b, cm_ln_g, cm_ln_b, cm_w_out, cm_b_out, kv_norm_g, w_kv, w_q, w_o, ffn_w_in, ffn_dw, ffn_dw_b, ffn_w_out, loss_target, m_mix_pre_g, m_mix_post_g, m_ffn_pre_g, m_ffn_post_g, m_cm_w_in, m_cm_b_in, m_cm_dw, m_cm_dw_b, m_cm_ln_g, m_cm_ln_b, m_cm_w_out, m_cm_b_out, m_kv_norm_g, m_w_kv, m_w_q, m_w_o, m_ffn_w_in, m_ffn_dw, m_ffn_dw_b, m_ffn_w_out, v_mix_pre_g, v_mix_post_g, v_ffn_pre_g, v_ffn_post_g, v_cm_w_in, v_cm_b_in, v_cm_dw, v_cm_dw_b, v_cm_ln_g, v_cm_ln_b, v_cm_w_out, v_cm_b_out, v_kv_norm_g, v_w_kv, v_w_q, v_w_o, v_ffn_w_in, v_ffn_dw, v_ffn_dw_b, v_ffn_w_out):
    names = ["mix_pre_g", "mix_post_g", "ffn_pre_g", "ffn_post_g", "cm_w_in", "cm_b_in", "cm_dw", "cm_dw_b", "cm_ln_g",
             "cm_ln_b", "cm_w_out", "cm_b_out", "kv_norm_g", "w_kv", "w_q", "w_o", "ffn_w_in", "ffn_dw", "ffn_dw_b",
             "ffn_w_out"]
    w_in = dict(zip(names, [mix_pre_g, mix_post_g, ffn_pre_g, ffn_post_g, cm_w_in, cm_b_in, cm_dw, cm_dw_b, cm_ln_g,
                            cm_ln_b, cm_w_out, cm_b_out, kv_norm_g, w_kv, w_q, w_o, ffn_w_in, ffn_dw, ffn_dw_b, ffn_w_out]))
    m_in = dict(zip(names, [m_mix_pre_g, m_mix_post_g, m_ffn_pre_g, m_ffn_post_g, m_cm_w_in, m_cm_b_in, m_cm_dw, m_cm_dw_b,
                            m_cm_ln_g, m_cm_ln_b, m_cm_w_out, m_cm_b_out, m_kv_norm_g, m_w_kv, m_w_q, m_w_o, m_ffn_w_in,
                            m_ffn_dw, m_ffn_dw_b, m_ffn_w_out]))
    v_in = dict(zip(names, [v_mix_pre_g, v_mix_post_g, v_ffn_pre_g, v_ffn_post_g, v_cm_w_in, v_cm_b_in, v_cm_dw, v_cm_dw_b,
                            v_cm_ln_g, v_cm_ln_b, v_cm_w_out, v_cm_b_out, v_kv_norm_g, v_w_kv, v_w_q, v_w_o, v_ffn_w_in,
                            v_ffn_dw, v_ffn_dw_b, v_ffn_w_out]))

    bsz, seq, d = x.shape
    t = bsz * seq
    n_b = DEPTH - N_A
    hw = w_o.shape[-1]
    qw = N_GROUPS * hw
    f2 = ffn_dw_b.shape[-1]
    f = f2 // 2
    me_chip = _chip_of(lax.axis_index("x"), lax.axis_index("y"))

    big = ["cm_w_in", "cm_w_out", "w_kv", "w_q", "w_o", "ffn_w_in", "ffn_w_out"]
    row_sharded = ("cm_w_out", "w_o", "ffn_w_out")
    small_sharded = ["cm_b_in", "cm_dw", "cm_dw_b", "cm_ln_g", "cm_ln_b", "cm_b_out", "ffn_dw"]
    small_pack = _pack([w_in[n] for n in small_sharded])
    chunks = [
        [("cm_w_in", 0), ("small", None)],
        [("cm_w_out", 0)],
        [("ffn_w_in", 0), ("ffn_w_out", 0)],
        [("cm_w_in", 1), ("cm_w_out", 1)],
        [("ffn_w_in", 1), ("ffn_w_out", 1)],
        [("w_kv", None)],
        [("w_q", 0), ("w_o", 0)],
        [("ffn_w_in", 2), ("ffn_w_out", 2)],
        [("w_q", 1), ("w_o", 1)],
        [("ffn_w_in", 3), ("ffn_w_out", 3)],
    ]
    pieces = [pc for ch in chunks for pc in ch]
    chunk_of = {pc: ck for ck, ch in enumerate(chunks) for pc in ch}

    me_arr = me_chip.astype(jnp.int32).reshape(1)

    def land_of(pc, dep=None):
        n, l = pc
        if n == "small":
            return cast_place(small_pack, None, me_arr, F32, name="place_small", dep=dep)
        return cast_place(w_in[n], l, me_arr, BF16, name=f"place_{n}_{l}", dep=dep)

    n_early = len(chunks[0]) + len(chunks[1])
    lands_a = [land_of(pc) for pc in pieces[:n_early]]
    send_a, recv_a, lands_fa, token_a = gather_start(lands_a, [len(chunks[0]), len(chunks[1])], name="gather_start_a")
    lands_b = [land_of(pc, token_a if k == 0 else None) for k, pc in enumerate(pieces[n_early:])]
    send_b, recv_b, lands_fb, token = gather_start(lands_b, [len(ch) for ch in chunks[2:]], name="gather_start_b")
    g_send, g_recv, lands_f = [*send_a, *send_b], [*recv_a, *recv_b], [*lands_fa, *lands_fb]
    weights = {}

    def finish_chunk(ck, after):
        lo = sum(len(ch) for ch in chunks[:ck])
        hi = lo + len(chunks[ck])
        got = gather_wait(g_send[ck], g_recv[ck], lands_f[lo:hi], after, name=f"gather_wait{ck}")
        for pc, arr in zip(chunks[ck], got):
            weights[pc] = arr.reshape(1, -1, arr.shape[-1]) if pc[0] in row_sharded else arr

    def wmat(n, l=None, after=None):
        if (n, l) not in weights:
            finish_chunk(chunk_of[(n, l)], after)
        arr = weights[(n, l)]
        return arr, arr.shape[0]

    finish_chunk(0, token)
    small_full = {}
    for n, arr4 in zip(small_sharded, _unpack(weights[("small", None)], [w_in[n].shape for n in small_sharded], lead=(N_CHIPS,))):
        shp = w_in[n].shape
        small_full[n] = jnp.moveaxis(arr4, 0, -2).reshape(*shp[:-1], N_CHIPS * shp[-1])

    x2d = x.reshape(t, d)
    saved = []
    (h1,) = resid_norm_fwd(x2d, None, None, [_row(mix_pre_g[0])], name="norm_in", dep=token)
    xcur = x2d
    kv_state = None
    for i in range(DEPTH):
        sv = {"x_in": xcur, "h1": h1}
        if i < N_A:
            z = mm_nn(h1, *wmat("cm_w_in", i, h1), 1, 0, bias=_row(small_full["cm_b_in"][i]), name=f"cm_in{i}")
            u2 = glu_conv_fwd(z.reshape(bsz, seq, 2 * d), small_full["cm_dw"][i], _row(small_full["cm_dw_b"][i]),
                              name=f"glu_conv{i}").reshape(t, d)
            u4 = ln_silu_fwd(u2, _row(small_full["cm_ln_g"][i]), _row(small_full["cm_ln_b"][i]), name=f"ln_silu{i}")
            y = mm_nn(u4, *wmat("cm_w_out", i, u4), 1, 0, bias=_row(small_full["cm_b_out"][i]), out_dtype=BF16,
                      name=f"cm_out{i}")
            sv.update(z=z, u2=u2, u4=u4)
        else:
            j = i - N_A
            q = mm_nn(h1, *wmat("w_q", j, h1), 1, 0, name=f"q_proj{j}", head_major=True).reshape(-1, bsz, seq, HEAD_DIM)
            outs, lses = [], []
            for g, dil in enumerate(DILATIONS):
                o_g, l_g = attn_fwd(q, kv_state["kv"], g, dil, hw, name=f"attn_fwd{j}_{g}")
                outs.append(o_g.reshape(t, hw))
                lses.append(l_g.reshape(t, LANES))
            merged, lsej = attn_merge(outs, lses, name=f"attn_merge{j}")
            y = mm_nn(merged, *wmat("w_o", j, merged), 1, 0, out_dtype=BF16, name=f"o_proj{j}")
            sv.update(q=q, merged=merged, lsej=lsej)
        x1, h2 = resid_norm_fwd(xcur, y, _row(mix_post_g[i]), [_row(ffn_pre_g[i])], name=f"resid_mix{i}")
        p = mm_nn(h2, *wmat("ffn_w_in", i, h2), 1, 0, out_dtype=BF16, name=f"ffn_in{i}", tm=2048)
        s_act = ffn_mid_fwd(p.reshape(bsz, seq, f2), small_full["ffn_dw"][i], _row(ffn_dw_b[i]), name=f"ffn_mid{i}").reshape(t, f)
        y2 = mm_nn(s_act, *wmat("ffn_w_out", i), 1, 0, out_dtype=BF16, name=f"ffn_out{i}", tm=2048)
        next_gains = []
        if i + 1 < DEPTH:
            next_gains.append(_row(mix_pre_g[i + 1]))
        if i == N_A - 1:
            next_gains.append(_row(kv_norm_g))
        res = resid_norm_fwd(x1, y2, _row(ffn_post_g[i]), next_gains, name=f"resid_ffn{i}")
        sv.update(y=y, x1=x1, h2=h2, p=p, s=s_act, y2=y2)
        saved.append(sv)
        xcur = res[0]
        if i + 1 < DEPTH:
            h1 = res[1]
        if i == N_A - 1:
            kvn = res[2]
            kv = mm_nn(kvn, *wmat("w_kv", None, kvn), 1, 0, name="kv_proj", head_major=True).reshape(-1, bsz, seq, HEAD_DIM)
            kv_state = {"kv": kv, "kvn": kvn, "x_a": xcur}

    dx, loss_tile = loss_fwd_bwd(xcur, loss_target.reshape(t, d))
    loss = lax.psum(loss_tile[0, 0], ("x", "y", "c"))

    gsm = {n: [None] * w_in[n].shape[0] for n in
           ["mix_pre_g", "mix_post_g", "ffn_pre_g", "ffn_post_g", "cm_b_in", "cm_dw", "cm_dw_b", "cm_ln_g", "cm_ln_b",
            "cm_b_out", "ffn_dw", "ffn_dw_b"]}
    gbig = {}
    in_flight = []
    dep = None

    def start_scatter(pcs, tag):
        ssem, rsem, g_f, r_f, tok = scatter_start([gbig[pc] for pc in pcs], name=f"scatter_start_{tag}")
        in_flight.append((pcs, ssem, rsem, g_f, r_f))
        return tok

    dk_buf = dv_buf = None
    for i in range(DEPTH - 1, -1, -1):
        sv = saved[i]
        dy2, dg, _ = norm_bwd(sv["y2"], _row(ffn_post_g[i]), dx, out_dtype=BF16, name=f"bwd_ffn_post{i}", dep=dep)
        gsm["ffn_post_g"][i] = dg
        ds = mm_nt(dy2, *wmat("ffn_w_out", i), 1, 0, out_dtype=BF16, name=f"bwd_ffn_out_dx{i}")
        gbig[("ffn_w_out", i)] = mm_tn(sv["s"], dy2, 1, name=f"bwd_ffn_out_dw{i}").reshape(N_CHIPS, f // N_CHIPS, d)
        dpa, dpg, ddwa, ddwg, ddba, ddbg = ffn_mid_bwd(sv["p"].reshape(bsz, seq, f2), small_full["ffn_dw"][i], _row(ffn_dw_b[i]),
                                                       ds.reshape(bsz, seq, f), name=f"bwd_ffn_mid{i}")
        gsm["ffn_dw"][i] = jnp.concatenate([jnp.sum(ddwa, axis=1), jnp.sum(ddwg, axis=1)], axis=-1)
        gsm["ffn_dw_b"][i] = jnp.concatenate([ddba, ddbg], axis=-1)
        dp = [dpa.reshape(t, f), dpg.reshape(t, f)]
        dh2 = mm_nt(dp, *wmat("ffn_w_in", i), 1, 0, name=f"bwd_ffn_in_dx{i}")
        gbig[("ffn_w_in", i)] = mm_tn(sv["h2"], dp, N_CHIPS, name=f"bwd_ffn_in_dw{i}")
        dx1, dg, _ = norm_bwd(sv["x1"], _row(ffn_pre_g[i]), dh2, add=dx, name=f"bwd_ffn_pre{i}")
        gsm["ffn_pre_g"][i] = dg
        dep = start_scatter([("ffn_w_in", 0), ("ffn_w_out", 0)], "ffn0") if i == 0 else None
        dy, dg, dbias = norm_bwd(sv["y"], _row(mix_post_g[i]), dx1, out_dtype=BF16, name=f"bwd_mix_post{i}", dep=dep)
        gsm["mix_post_g"][i] = dg
        if i < N_A:
            gsm["cm_b_out"][i] = dbias
            du4 = mm_nt(dy, *wmat("cm_w_out", i), 1, 0, name=f"bwd_cm_out_dx{i}")
            gbig[("cm_w_out", i)] = mm_tn(sv["u4"], dy, 1, name=f"bwd_cm_out_dw{i}").reshape(N_CHIPS, d // N_CHIPS, d)
            du2, dlg, dlb = ln_silu_bwd(sv["u2"], _row(small_full["cm_ln_g"][i]), _row(small_full["cm_ln_b"][i]), du4,
                                        name=f"bwd_ln_silu{i}")
            gsm["cm_ln_g"][i], gsm["cm_ln_b"][i] = dlg, dlb
            dza, dzg, ddw, ddwb, dba, dbg = glu_conv_bwd(sv["z"].reshape(bsz, seq, 2 * d), small_full["cm_dw"][i],
                                                         du2.reshape(bsz, seq, d), name=f"bwd_glu_conv{i}")
            gsm["cm_dw"][i] = jnp.sum(ddw, axis=1)
            gsm["cm_dw_b"][i] = ddwb
            gsm["cm_b_in"][i] = jnp.concatenate([dba, dbg], axis=-1)
            dz = [dza.reshape(t, d), dzg.reshape(t, d)]
            dh1 = mm_nt(dz, *wmat("cm_w_in", i), 1, 0, name=f"bwd_cm_in_dx{i}")
            gbig[("cm_w_in", i)] = mm_tn(sv["h1"], dz, N_CHIPS, name=f"bwd_cm_in_dw{i}")
        else:
            j = i - N_A
            dmerged = mm_nt(dy, *wmat("w_o", j), 1, 0, name=f"bwd_o_proj_dx{j}")
            gbig[("w_o", j)] = mm_tn(sv["merged"], dy, 1, name=f"bwd_o_proj_dw{j}").reshape(N_CHIPS, hw // N_CHIPS, d)
            dmt = attn_bwd_prep(dmerged, sv["merged"], name=f"bwd_attn_prep{j}")
            dq_buf = None
            add_to_kv = dk_buf is not None
            for g, dil in enumerate(DILATIONS):
                dq_buf, dk_buf, dv_buf = attn_bwd(
                    sv["q"], kv_state["kv"], g, dil, dmerged.reshape(bsz, seq, hw), sv["lsej"].reshape(bsz, seq, LANES),
                    dmt.reshape(bsz, seq, LANES), dq_buf, dk_buf, dv_buf, add_to_kv, hw, name=f"attn_bwd{j}_{g}")
            dq = dq_buf.reshape(t, qw)
            dh1 = mm_nt(dq, *wmat("w_q", j), 1, 0, name=f"bwd_q_proj_dx{j}")
            gbig[("w_q", j)] = mm_tn(sv["h1"], dq, N_CHIPS, name=f"bwd_q_proj_dw{j}")
        dx, dg, _ = norm_bwd(sv["x_in"], _row(mix_pre_g[i]), dh1, add=dx1, name=f"bwd_mix_pre{i}")
        gsm["mix_pre_g"][i] = dg
        if i > N_A:
            dep = start_scatter([("ffn_w_in", i), ("ffn_w_out", i), ("w_q", i - N_A), ("w_o", i - N_A)], f"l{i}")
        elif 0 < i < N_A:
            dep = start_scatter([("ffn_w_in", i), ("ffn_w_out", i), ("cm_w_in", i), ("cm_w_out", i)], f"l{i}")
        elif i == 0:
            last_token = start_scatter([("cm_w_in", 0), ("cm_w_out", 0)], "cm0")
        if i == N_A:
            dkv = [dk_buf.reshape(t, qw), dv_buf.reshape(t, qw)]
            dkvn = mm_nt(dkv, *wmat("w_kv"), 1, 0, name="bwd_kv_proj_dx")
            gbig[("w_kv", None)] = mm_tn(kv_state["kvn"], dkv, N_CHIPS, name="bwd_kv_proj_dw")
            dx, dg_kv, _ = norm_bwd(kv_state["x_a"], _row(kv_norm_g), dkvn, add=dx, name="bwd_kv_norm")
            dep = start_scatter([("ffn_w_in", i), ("ffn_w_out", i), ("w_q", 0), ("w_o", 0), ("w_kv", None)], f"l{i}")
    grad_x = dx.reshape(bsz, seq, d)

    plane_of = {}
    outs_g, outs_d, outs_m, outs_v = {}, {}, {}, {}

    def finish_scatter(k, after):
        pcs, ssem, rsem, g_f, r_f = in_flight[k]
        g_done, r_done = scatter_wait(ssem, rsem, g_f, r_f, after, name=f"scatter_wait{k}")
        for pc, g_arr, r_arr in zip(pcs, g_done, r_done):
            plane_of[pc] = sum_parts(g_arr, r_arr, me_arr, name=f"sum_chips_{pc[0]}_{pc[1]}")

    def update(group, tag, after):
        plane = []
        for n in group:
            if w_in[n].ndim == 2:
                plane.append(plane_of[(n, None)])
            else:
                plane.append(jnp.concatenate([plane_of[(n, l)] for l in range(w_in[n].shape[0])], axis=0))
        ssems, rsems, plane_f, land_f, _ = swap_start(plane, name=f"swap_start_{tag}")
        for k, n in enumerate(group):
            p_mine, p_other = swap_wait(ssems[k], rsems[k], plane_f[k], land_f[k], after, name=f"swap_wait_{n}")
            shp = w_in[n].shape
            flat = lambda a: a.reshape(-1, shp[-1])
            g_, d_, m_, v_ = adamw(flat(w_in[n]), flat(m_in[n]), flat(v_in[n]), [p_mine, p_other], name=f"adamw_{n}")
            outs_g[n], outs_d[n], outs_m[n], outs_v[n] = (a.reshape(shp) for a in (g_, d_, m_, v_))
            after = v_
        return after

    small_names = [n for n in names if n not in big]
    small_shapes_full = {}
    small_grads_full = []
    for n in small_names:
        if n == "kv_norm_g":
            gfull = dg_kv.reshape(-1)
        elif n in ("cm_dw", "ffn_dw"):
            gfull = jnp.stack(gsm[n], axis=0)
        else:
            gfull = jnp.stack([a.reshape(-1) for a in gsm[n]], axis=0)
        small_shapes_full[n] = gfull.shape
        small_grads_full.append(gfull)
    dev_arr = (4 * lax.axis_index("x") + 2 * lax.axis_index("y") + lax.axis_index("c")).astype(jnp.int32).reshape(1)
    small_land = cast_place(_pack(small_grads_full) + last_token[0, 0], None, dev_arr, F32, name="place_small_grads",
                            nslots=N_DEV)
    sm_send, sm_recv, small_land, small_token = small_start(small_land)

    for k in range(len(in_flight) - 1):
        finish_scatter(k, small_token)
    done = update(["w_kv", "w_q", "w_o", "ffn_w_in", "ffn_w_out"], "a", small_token)
    finish_scatter(len(in_flight) - 1, done)
    done = update(["cm_w_in", "cm_w_out"], "b", done)

    summed = sum_slots(small_wait(sm_send, sm_recv, small_land, done))
    g_full = dict(zip(small_names, _unpack(summed, [small_shapes_full[n] for n in small_names])))
    g_loc = {}
    for n in small_names:
        if n in small_sharded:
            width = w_in[n].shape[-1]
            g_loc[n] = lax.dynamic_slice_in_dim(g_full[n], me_chip * width, width, axis=g_full[n].ndim - 1)
        else:
            g_loc[n] = g_full[n]
    res = adamw(_pack([w_in[n] for n in small_names]), _pack([m_in[n] for n in small_names]),
                _pack([v_in[n] for n in small_names]), [_pack([g_loc[n] for n in small_names])], name="adamw_small")
    shapes_loc = [w_in[n].shape for n in small_names]
    for dst, packed in zip((outs_g, outs_d, outs_m, outs_v), res):
        for n, a in zip(small_names, _unpack(packed, shapes_loc)):
            dst[n] = a

    return (loss, grad_x, *[outs_g[n] for n in names], *[outs_d[n] for n in names],
            *[outs_m[n] for n in names], *[outs_v[n] for n in names])
```

```python
import functools
import math

import jax
import jax.numpy as jnp
from jax import lax
from jax.experimental import pallas as pl
from jax.experimental.pallas import tpu as pltpu

F32 = jnp.float32
BF16 = jnp.bfloat16
EPS = 1e-6
NEG_INF = -1e30
N_A = 2
DEPTH = 4
N_GROUPS = 3
DILATIONS = (1, 4, 16)
HEAD_DIM = 128
BLK = 128
LANES = 128
N_CHIPS = 4
N_DEV = 8
VMEM_LIMIT_V7X = 56 * 1024 * 1024

ADAM_LR = 0.001
ADAM_B1 = 0.9
ADAM_B2 = 0.999
ADAM_EPS = 1e-08
ADAM_WD = 0.01
ADAM_STEP = 10

MESH = pl.DeviceIdType.MESH


def _cp(*sem, **kw):
    return pltpu.CompilerParams(dimension_semantics=sem if sem else None, vmem_limit_bytes=VMEM_LIMIT_V7X, **kw)


def _dot(a, b):
    return jnp.dot(a, b, preferred_element_type=F32)


def _dot_nt(a, b):
    return lax.dot_general(a, b, (((1,), (1,)), ((), ())), preferred_element_type=F32)


def _dot_tn(a, b):
    return lax.dot_general(a, b, (((0,), (0,)), ((), ())), preferred_element_type=F32)


def _sigmoid(x):
    return 1.0 / (1.0 + jnp.exp(-x))


def _row_tile(n, want):
    if n <= want:
        return n
    for t in range(want - want % 8, 7, -8):
        if n % t == 0:
            return t
    raise ValueError(f"no row tile for {n} rows")


def mm_nn(a, w, nsh, stride, layer, bias=None, out_dtype=F32, name="mm_nn", tm=1024, head_major=False):
    m, k = a.shape
    _, k2, ns = w.shape
    assert k == k2
    tm = _row_tile(m, tm)
    has_bias = bias is not None
    hps = ns // HEAD_DIM

    def body(*refs):
        if has_bias:
            a_ref, w_ref, b_ref, o_ref = refs
        else:
            a_ref, w_ref, o_ref = refs
        acc = _dot(a_ref[...].astype(BF16), w_ref[...])
        if has_bias:
            acc = acc + b_ref[...]
        if head_major:
            for hh in range(hps):
                o_ref[hh] = acc[:, hh * HEAD_DIM:(hh + 1) * HEAD_DIM].astype(out_dtype)
        else:
            o_ref[...] = acc.astype(out_dtype)

    in_specs = [
        pl.BlockSpec((tm, k), lambda j, i: (i, 0)),
        pl.BlockSpec((None, k, ns), lambda j, i: (j * stride + layer, 0, 0)),
    ]
    args = [a, w]
    if has_bias:
        in_specs.append(pl.BlockSpec((1, ns), lambda j, i: (0, j)))
        args.append(bias)
    return pl.pallas_call(
        body,
        name=name,
        grid=(nsh, m // tm),
        in_specs=in_specs,
        out_specs=(pl.BlockSpec((hps, tm, HEAD_DIM), lambda j, i: (j, i, 0)) if head_major
                   else pl.BlockSpec((tm, ns), lambda j, i: (i, j))),
        out_shape=jax.ShapeDtypeStruct((nsh * hps, m, HEAD_DIM) if head_major else (m, nsh * ns), out_dtype),
        compiler_params=_cp("parallel", "parallel"),
    )(*args)


def mm_nt(dy, w, nsh, stride, layer, out_dtype=F32, name="mm_nt", tm=1024):
    dys = list(dy) if isinstance(dy, (list, tuple)) else [dy]
    npart = len(dys)
    per = nsh // npart
    m = dys[0].shape[0]
    _, k, ns = w.shape
    assert all(d.shape == (m, per * ns) for d in dys)
    tm = _row_tile(m, tm)

    def body(*refs):
        dy_refs = refs[:npart]
        w_ref, o_ref, acc_ref = refs[npart:]
        j = pl.program_id(1)

        @pl.when(j == 0)
        def _():
            acc_ref[...] = jnp.zeros(acc_ref.shape, F32)

        for pi in range(npart):
            @pl.when(j // per == pi)
            def _(pi=pi):
                acc_ref[...] += _dot_nt(dy_refs[pi][...].astype(BF16), w_ref[...])

        @pl.when(j == nsh - 1)
        def _():
            o_ref[...] = acc_ref[...].astype(out_dtype)

    dy_specs = [pl.BlockSpec((tm, ns), lambda i, j, pi=pi: (i, jnp.clip(j - pi * per, 0, per - 1))) for pi in range(npart)]
    return pl.pallas_call(
        body,
        name=name,
        grid=(m // tm, nsh),
        in_specs=[*dy_specs, pl.BlockSpec((None, k, ns), lambda i, j: (j * stride + layer, 0, 0))],
        out_specs=pl.BlockSpec((tm, k), lambda i, j: (i, 0)),
        out_shape=jax.ShapeDtypeStruct((m, k), out_dtype),
        scratch_shapes=[pltpu.VMEM((tm, k), F32)],
        compiler_params=_cp("parallel", "arbitrary"),
    )(*dys, w)


def mm_tn(a, dy, nsh, name="mm_tn", tm=1024):
    dys = list(dy) if isinstance(dy, (list, tuple)) else [dy]
    npart = len(dys)
    per = nsh // npart
    m, k = a.shape
    ns = dys[0].shape[1] // per
    assert all(d.shape == (m, per * ns) for d in dys)
    tm = _row_tile(m, tm)
    nt = m // tm

    def body(*refs):
        a_ref = refs[0]
        dy_refs = refs[1:1 + npart]
        o_ref, acc_ref = refs[1 + npart:]
        j = pl.program_id(0)
        i = pl.program_id(1)

        @pl.when(i == 0)
        def _():
            acc_ref[...] = jnp.zeros(acc_ref.shape, F32)

        for pi in range(npart):
            @pl.when(j // per == pi)
            def _(pi=pi):
                acc_ref[...] += _dot_tn(a_ref[...].astype(BF16), dy_refs[pi][...].astype(BF16))

        @pl.when(i == nt - 1)
        def _():
            o_ref[...] = acc_ref[...].astype(BF16)

    dy_specs = [
        pl.BlockSpec((tm, ns), lambda j, i, pi=pi: (jnp.where(j // per == pi, i, 0), jnp.clip(j - pi * per, 0, per - 1)))
        for pi in range(npart)
    ]
    return pl.pallas_call(
        body,
        name=name,
        grid=(nsh, nt),
        in_specs=[pl.BlockSpec((tm, k), lambda j, i: (i, 0)), *dy_specs],
        out_specs=pl.BlockSpec((None, k, ns), lambda j, i: (j, 0, 0)),
        out_shape=jax.ShapeDtypeStruct((nsh, k, ns), BF16),
        scratch_shapes=[pltpu.VMEM((k, ns), F32)],
        compiler_params=_cp("parallel", "arbitrary"),
    )(a, *dys)


DEP_SPEC_SHAPE = (8, LANES)


def resid_norm_fwd(x, y, g_post, next_gains, name, tm=1024, dep=None):
    t, d = x.shape
    tm = _row_tile(t, tm)
    has_y = y is not None
    n_next = len(next_gains)
    n_dep = 0 if dep is None else 1

    def body(*refs):
        x_ref = refs[0]
        pos = 1
        if has_y:
            y_ref, gp_ref = refs[1], refs[2]
            pos = 3
        gn_refs = refs[pos:pos + n_next]
        outs = refs[pos + n_next + n_dep:]
        xv = x_ref[...]
        o = 0
        if has_y:
            yv = y_ref[...].astype(F32)
            r = lax.rsqrt(jnp.mean(yv * yv, axis=-1, keepdims=True) + EPS)
            xv = xv + (yv * r) * gp_ref[...]
            outs[0][...] = xv
            o = 1
        if n_next:
            xn = xv * lax.rsqrt(jnp.mean(xv * xv, axis=-1, keepdims=True) + EPS)
            for k in range(n_next):
                outs[o + k][...] = (xn * gn_refs[k][...]).astype(BF16)

    row = pl.BlockSpec((tm, d), lambda i: (i, 0))
    vec = pl.BlockSpec((1, d), lambda i: (0, 0))
    args, in_specs = [x], [row]
    if has_y:
        args += [y, g_post]
        in_specs += [row, vec]
    args += list(next_gains)
    in_specs += [vec] * n_next
    if n_dep:
        args.append(dep)
        in_specs.append(pl.BlockSpec(DEP_SPEC_SHAPE, lambda i: (0, 0)))
    out_shape, out_specs = [], []
    if has_y:
        out_shape.append(jax.ShapeDtypeStruct((t, d), F32))
        out_specs.append(row)
    for _ in range(n_next):
        out_shape.append(jax.ShapeDtypeStruct((t, d), BF16))
        out_specs.append(row)
    return pl.pallas_call(
        body, name=name, grid=(t // tm,), in_specs=in_specs, out_specs=out_specs, out_shape=out_shape,
        compiler_params=_cp("parallel"),
    )(*args)


def norm_bwd(x, g, dy, add=None, out_dtype=F32, name="norm_bwd", tm=1024, dep=None):
    t, d = x.shape
    tm = _row_tile(t, tm)
    has_add = add is not None

    def body(*refs):
        x_ref, g_ref, dy_ref = refs[:3]
        add_ref = refs[3] if has_add else None
        dx_ref, dg_ref, cs_ref = refs[-3:]
        i = pl.program_id(0)
        xv = x_ref[...].astype(F32)
        dyv = dy_ref[...].astype(F32)
        r = lax.rsqrt(jnp.mean(xv * xv, axis=-1, keepdims=True) + EPS)
        gd = dyv * g_ref[...]
        dx = r * gd - xv * ((r * r * r) * jnp.mean(xv * gd, axis=-1, keepdims=True))
        if has_add:
            dx = dx + add_ref[...]
        dx_ref[...] = dx.astype(out_dtype)
        dg = jnp.sum(dyv * (xv * r), axis=0, keepdims=True)
        cs = jnp.sum(dx, axis=0, keepdims=True)

        @pl.when(i == 0)
        def _():
            dg_ref[...] = dg
            cs_ref[...] = cs

        @pl.when(i > 0)
        def _():
            dg_ref[...] += dg
            cs_ref[...] += cs

    row = pl.BlockSpec((tm, d), lambda i: (i, 0))
    vec = pl.BlockSpec((1, d), lambda i: (0, 0))
    args, in_specs = [x, g, dy], [row, vec, row]
    if has_add:
        args.append(add)
        in_specs.append(row)
    if dep is not None:
        args.append(dep)
        in_specs.append(pl.BlockSpec(DEP_SPEC_SHAPE, lambda i: (0, 0)))
    return pl.pallas_call(
        body, name=name, grid=(t // tm,), in_specs=in_specs,
        out_specs=[row, vec, vec],
        out_shape=[jax.ShapeDtypeStruct((t, d), out_dtype), jax.ShapeDtypeStruct((1, d), F32),
                   jax.ShapeDtypeStruct((1, d), F32)],
        compiler_params=_cp("arbitrary"),
    )(*args)


def loss_fwd_bwd(x, target, name="loss", tm=1024):
    t, d = x.shape
    tm = _row_tile(t, tm)

    def body(x_ref, t_ref, dx_ref, l_ref):
        i = pl.program_id(0)
        err = x_ref[...] - t_ref[...]
        dx_ref[...] = err * (1.0 / d)
        part = 0.5 * jnp.sum(jnp.mean(err * err, axis=-1, keepdims=True), axis=0, keepdims=True)
        part = jnp.broadcast_to(part, l_ref.shape)

        @pl.when(i == 0)
        def _():
            l_ref[...] = part

        @pl.when(i > 0)
        def _():
            l_ref[...] += part

    row = pl.BlockSpec((tm, d), lambda i: (i, 0))
    return pl.pallas_call(
        body, name=name, grid=(t // tm,), in_specs=[row, row],
        out_specs=[row, pl.BlockSpec((8, LANES), lambda i: (0, 0))],
        out_shape=[jax.ShapeDtypeStruct((t, d), F32), jax.ShapeDtypeStruct((8, LANES), F32)],
        compiler_params=_cp("arbitrary"),
    )(x, target)


CONV_HALO = 32
CONV_CHUNK = 128


def glu_conv_fwd(z, dw, dwb, name, tc=128):
    b, s, c2 = z.shape
    c = c2 // 2
    kw = dw.shape[0]
    tc = min(tc, c)
    nc = c // tc
    ch = min(CONV_CHUNK, s)
    halo = CONV_HALO
    assert kw - 1 <= halo and s % ch == 0

    nch = s // ch

    def body(a_ref, g_ref, w_ref, b_ref, o_ref, pad_ref):
        _fill_glu_slabs(a_ref, g_ref, pad_ref, nch, ch, halo)

        def chunk(ci, carry):
            r0 = pl.multiple_of(ci * ch, ch)
            acc = b_ref[...]
            for k, tap in enumerate(_taps_front(pad_ref, ci, kw, ch, halo)):
                acc = acc + w_ref[k:k + 1, :] * tap
            o_ref[pl.ds(r0, ch), :] = acc
            return carry

        lax.fori_loop(0, nch, chunk, 0)

    return pl.pallas_call(
        body, name=name, grid=(b, nc),
        in_specs=[
            pl.BlockSpec((None, s, tc), lambda bi, i: (bi, 0, i)),
            pl.BlockSpec((None, s, tc), lambda bi, i: (bi, 0, i + nc)),
            pl.BlockSpec((kw, tc), lambda bi, i: (0, i)),
            pl.BlockSpec((1, tc), lambda bi, i: (0, i)),
        ],
        out_specs=pl.BlockSpec((None, s, tc), lambda bi, i: (bi, 0, i)),
        out_shape=jax.ShapeDtypeStruct((b, s, c), F32),
        scratch_shapes=[pltpu.VMEM((nch, ch + halo, tc), F32)],
        compiler_params=_cp("parallel", "parallel"),
    )(z, z, dw, dwb)


def glu_conv_bwd(z, dw, du2, name, tc=128):
    b, s, c2 = z.shape
    c = c2 // 2
    kw = dw.shape[0]
    tc = min(tc, c)
    nc = c // tc
    ch = min(CONV_CHUNK, s)
    nch = s // ch
    halo = CONV_HALO

    def body(a_ref, g_ref, w_ref, du_ref, dza_ref, dzg_ref, ddw_ref, ddwb_ref, dba_ref, dbg_ref, upad_ref, dpad_ref):
        bi = pl.program_id(1)

        @pl.when(bi == 0)
        def _():
            ddw_ref[...] = jnp.zeros(ddw_ref.shape, F32)
            ddwb_ref[...] = jnp.zeros(ddwb_ref.shape, F32)
            dba_ref[...] = jnp.zeros(dba_ref.shape, F32)
            dbg_ref[...] = jnp.zeros(dbg_ref.shape, F32)

        _fill_glu_slabs(a_ref, g_ref, upad_ref, nch, ch, halo)
        dpad_ref[nch - 1, ch:ch + halo, :] = jnp.zeros((halo, tc), F32)
        dpad_ref[nch - 1, 0:ch, :] = du_ref[s - ch:s, :]

        def fill(ci, carry):
            r0 = pl.multiple_of(ci * ch, ch)
            dpad_ref[ci, :, :] = du_ref[pl.ds(r0, ch + halo), :]
            return carry

        lax.fori_loop(0, nch - 1, fill, 0)

        def chunk(ci, carry):
            r0 = pl.multiple_of(ci * ch, ch)
            du_c = du_ref[pl.ds(r0, ch), :]
            taps_u = _taps_front(upad_ref, ci, kw, ch, halo)
            taps_d = _taps_at(dpad_ref, ci, list(range(kw)), ch)
            du1 = w_ref[kw - 1:kw, :] * du_c
            ddw_ref[kw - 1] += jnp.sum((du_c * taps_u[kw - 1]).reshape(ch // 8, 8, tc), axis=0)
            for j in range(1, kw):
                du1 = du1 + w_ref[kw - 1 - j:kw - j, :] * taps_d[j]
                ddw_ref[kw - 1 - j] += jnp.sum((du_c * taps_u[kw - 1 - j]).reshape(ch // 8, 8, tc), axis=0)
            av = a_ref[pl.ds(r0, ch), :]
            sg = _sigmoid(g_ref[pl.ds(r0, ch), :])
            dza = du1 * sg
            dzg = du1 * av * (sg * (1.0 - sg))
            dza_ref[pl.ds(r0, ch), :] = dza.astype(BF16)
            dzg_ref[pl.ds(r0, ch), :] = dzg.astype(BF16)
            dba_ref[...] += jnp.sum(dza, axis=0, keepdims=True)
            dbg_ref[...] += jnp.sum(dzg, axis=0, keepdims=True)
            ddwb_ref[...] += jnp.sum(du_c, axis=0, keepdims=True)
            return carry

        lax.fori_loop(0, s // ch, chunk, 0)

    blk = lambda off: pl.BlockSpec((None, s, tc), lambda i, bi: (bi, 0, i + off))
    vec = pl.BlockSpec((1, tc), lambda i, bi: (0, i))
    return pl.pallas_call(
        body, name=name, grid=(nc, b),
        in_specs=[blk(0), blk(nc), pl.BlockSpec((kw, tc), lambda i, bi: (0, i)), blk(0)],
        out_specs=[blk(0), blk(0), pl.BlockSpec((kw, 8, tc), lambda i, bi: (0, 0, i)), vec, vec, vec],
        out_shape=[
            jax.ShapeDtypeStruct((b, s, c), BF16), jax.ShapeDtypeStruct((b, s, c), BF16),
            jax.ShapeDtypeStruct((kw, 8, c), F32), jax.ShapeDtypeStruct((1, c), F32),
            jax.ShapeDtypeStruct((1, c), F32), jax.ShapeDtypeStruct((1, c), F32),
        ],
        scratch_shapes=[pltpu.VMEM((nch, ch + halo, tc), F32), pltpu.VMEM((nch, ch + halo, tc), F32)],
        compiler_params=_cp("parallel", "arbitrary"),
    )(z, z, dw, du2)


def _fill_glu_slabs(a_ref, g_ref, pad_ref, nch, ch, halo):
    tc = a_ref.shape[-1]
    pad_ref[0, 0:halo, :] = jnp.zeros((halo, tc), F32)
    pad_ref[0, halo:halo + ch, :] = a_ref[0:ch, :] * _sigmoid(g_ref[0:ch, :])

    def fill(ci, carry):
        r0 = pl.multiple_of(ci * ch, ch)
        pad_ref[ci, 0:halo, :] = pad_ref[ci - 1, ch:ch + halo, :]
        pad_ref[ci, halo:halo + ch, :] = a_ref[pl.ds(r0, ch), :] * _sigmoid(g_ref[pl.ds(r0, ch), :])
        return carry

    lax.fori_loop(1, nch, fill, 0)


def ln_silu_fwd(u, g, bvec, name, tm=1024):
    t, d = u.shape
    tm = _row_tile(t, tm)

    def body(u_ref, g_ref, b_ref, o_ref):
        uv = u_ref[...]
        mu = jnp.mean(uv, axis=-1, keepdims=True)
        xc = uv - mu
        var = jnp.mean(xc * xc, axis=-1, keepdims=True)
        v = (xc * lax.rsqrt(var + EPS)) * g_ref[...] + b_ref[...]
        o_ref[...] = (v * _sigmoid(v)).astype(BF16)

    row = pl.BlockSpec((tm, d), lambda i: (i, 0))
    vec = pl.BlockSpec((1, d), lambda i: (0, 0))
    return pl.pallas_call(
        body, name=name, grid=(t // tm,), in_specs=[row, vec, vec], out_specs=row,
        out_shape=jax.ShapeDtypeStruct((t, d), BF16), compiler_params=_cp("parallel"),
    )(u, g, bvec)


def ln_silu_bwd(u, g, bvec, dout, name, tm=1024):
    t, d = u.shape
    tm = _row_tile(t, tm)

    def body(u_ref, g_ref, b_ref, do_ref, du_ref, dg_ref, db_ref):
        i = pl.program_id(0)
        uv = u_ref[...]
        mu = jnp.mean(uv, axis=-1, keepdims=True)
        xc = uv - mu
        var = jnp.mean(xc * xc, axis=-1, keepdims=True)
        rstd = lax.rsqrt(var + EPS)
        n = xc * rstd
        v = n * g_ref[...] + b_ref[...]
        sg = _sigmoid(v)
        dv = do_ref[...].astype(F32) * (sg * (1.0 + v * (1.0 - sg)))
        dn = dv * g_ref[...]
        du_ref[...] = rstd * (dn - jnp.mean(dn, axis=-1, keepdims=True) - n * jnp.mean(dn * n, axis=-1, keepdims=True))
        dg = jnp.sum(dv * n, axis=0, keepdims=True)
        db = jnp.sum(dv, axis=0, keepdims=True)

        @pl.when(i == 0)
        def _():
            dg_ref[...] = dg
            db_ref[...] = db

        @pl.when(i > 0)
        def _():
            dg_ref[...] += dg
            db_ref[...] += db

    row = pl.BlockSpec((tm, d), lambda i: (i, 0))
    vec = pl.BlockSpec((1, d), lambda i: (0, 0))
    return pl.pallas_call(
        body, name=name, grid=(t // tm,), in_specs=[row, vec, vec, row], out_specs=[row, vec, vec],
        out_shape=[jax.ShapeDtypeStruct((t, d), F32), jax.ShapeDtypeStruct((1, d), F32), jax.ShapeDtypeStruct((1, d), F32)],
        compiler_params=_cp("arbitrary"),
    )(u, g, bvec, dout)


FFN_HALO = 8


def _fill_front_halo(src_ref, pad_ref, nch, ch, halo):
    tc = src_ref.shape[-1]
    pad_ref[0, 0:halo, :] = jnp.zeros((halo, tc), F32)
    pad_ref[0, halo:halo + ch, :] = src_ref[0:ch, :].astype(F32)

    def fill(ci, carry):
        r0 = pl.multiple_of(ci * ch, ch)
        pad_ref[ci, 0:halo, :] = src_ref[pl.ds(r0 - 2 * halo, 2 * halo), :].astype(F32)[halo:, :]
        pad_ref[ci, halo:halo + ch, :] = src_ref[pl.ds(r0, ch), :].astype(F32)
        return carry

    lax.fori_loop(1, nch, fill, 0)


def _taps_at(pad_ref, ci, offsets, ch):
    windows = {}
    for b in sorted({o % 8 for o in offsets}):
        top = max(o for o in offsets if o % 8 == b)
        windows[b] = pad_ref[ci, b:top + ch, :]
    return [windows[o % 8][o - o % 8:o - o % 8 + ch, :] for o in offsets]


def _taps_front(pad_ref, ci, kw, ch, halo):
    return _taps_at(pad_ref, ci, [halo - (kw - 1 - k) for k in range(kw)], ch)


def ffn_mid_fwd(p, dw, dwb, name, tc=256):
    b, s, f2 = p.shape
    f = f2 // 2
    kw = dw.shape[0]
    tc = min(tc, f)
    nf = f // tc
    ch = min(CONV_CHUNK, s)
    nch = s // ch
    halo = FFN_HALO

    def body(pa_ref, pg_ref, wa_ref, wg_ref, ba_ref, bg_ref, o_ref, apad_ref, gpad_ref):
        _fill_front_halo(pa_ref, apad_ref, nch, ch, halo)
        _fill_front_halo(pg_ref, gpad_ref, nch, ch, halo)

        def chunk(ci, carry):
            r0 = pl.multiple_of(ci * ch, ch)
            ca = ba_ref[...]
            cg = bg_ref[...]
            taps = zip(_taps_front(apad_ref, ci, kw, ch, halo), _taps_front(gpad_ref, ci, kw, ch, halo))
            for k, (ta, tg) in enumerate(taps):
                ca = ca + wa_ref[k:k + 1, :] * ta
                cg = cg + wg_ref[k:k + 1, :] * tg
            o_ref[pl.ds(r0, ch), :] = ((cg * _sigmoid(cg)) * ca).astype(BF16)
            return carry

        lax.fori_loop(0, nch, chunk, 0)

    blk = lambda off: pl.BlockSpec((None, s, tc), lambda bi, i: (bi, 0, i + off))
    wsp = lambda off: pl.BlockSpec((kw, tc), lambda bi, i: (0, i + off))
    bsp = lambda off: pl.BlockSpec((1, tc), lambda bi, i: (0, i + off))
    return pl.pallas_call(
        body, name=name, grid=(b, nf),
        in_specs=[blk(0), blk(nf), wsp(0), wsp(nf), bsp(0), bsp(nf)],
        out_specs=pl.BlockSpec((None, s, tc), lambda bi, i: (bi, 0, i)),
        out_shape=jax.ShapeDtypeStruct((b, s, f), BF16),
        scratch_shapes=[pltpu.VMEM((nch, ch + halo, tc), F32)] * 2,
        compiler_params=_cp("parallel", "parallel"),
    )(p, p, dw, dw, dwb, dwb)


def ffn_mid_bwd(p, dw, dwb, ds, name, tc=256):
    b, s, f2 = p.shape
    f = f2 // 2
    kw = dw.shape[0]
    tc = min(tc, f)
    nf = f // tc
    ch = min(CONV_CHUNK, s)
    nch = s // ch
    halo = FFN_HALO

    def sum8(v):
        return jnp.sum(v.reshape(ch // 8, 8, tc), axis=0)

    def body(pa_ref, pg_ref, wa_ref, wg_ref, ba_ref, bg_ref, ds_ref, dpa_ref, dpg_ref, ddwa_ref, ddwg_ref, dba_ref, dbg_ref,
             apad_ref, gpad_ref, dca_ref, dcg_ref):
        bi = pl.program_id(1)

        @pl.when(bi == 0)
        def _():
            ddwa_ref[...] = jnp.zeros(ddwa_ref.shape, F32)
            ddwg_ref[...] = jnp.zeros(ddwg_ref.shape, F32)
            dba_ref[...] = jnp.zeros(dba_ref.shape, F32)
            dbg_ref[...] = jnp.zeros(dbg_ref.shape, F32)

        _fill_front_halo(pa_ref, apad_ref, nch, ch, halo)
        _fill_front_halo(pg_ref, gpad_ref, nch, ch, halo)
        dca_ref[nch - 1, ch:ch + halo, :] = jnp.zeros((halo, tc), F32)
        dcg_ref[nch - 1, ch:ch + halo, :] = jnp.zeros((halo, tc), F32)

        def grads(ci, carry):
            acc_a, acc_g, sb_a, sb_g = carry
            r0 = pl.multiple_of(ci * ch, ch)
            taps_a = _taps_front(apad_ref, ci, kw, ch, halo)
            taps_g = _taps_front(gpad_ref, ci, kw, ch, halo)
            ca = ba_ref[...]
            cg = bg_ref[...]
            for k in range(kw):
                ca = ca + wa_ref[k:k + 1, :] * taps_a[k]
                cg = cg + wg_ref[k:k + 1, :] * taps_g[k]
            sg = _sigmoid(cg)
            dsv = ds_ref[pl.ds(r0, ch), :].astype(F32)
            dca = dsv * (cg * sg)
            dcg = dsv * ca * (sg * (1.0 + cg * (1.0 - sg)))
            dca_ref[ci, 0:ch, :] = dca
            dcg_ref[ci, 0:ch, :] = dcg

            prev = jnp.maximum(ci - 1, 0)

            @pl.when(ci > 0)
            def _():
                dca_ref[prev, ch:ch + halo, :] = dca[0:halo, :]
                dcg_ref[prev, ch:ch + halo, :] = dcg[0:halo, :]

            acc_a = tuple(acc_a[k] + sum8(dca * taps_a[k]) for k in range(kw))
            acc_g = tuple(acc_g[k] + sum8(dcg * taps_g[k]) for k in range(kw))
            return acc_a, acc_g, sb_a + sum8(dca), sb_g + sum8(dcg)

        z8 = jnp.zeros((8, tc), F32)
        acc_a, acc_g, sb_a, sb_g = lax.fori_loop(0, nch, grads, ((z8,) * kw, (z8,) * kw, z8, z8))
        for k in range(kw):
            ddwa_ref[k] += acc_a[k]
            ddwg_ref[k] += acc_g[k]
        dba_ref[...] += jnp.sum(sb_a, axis=0, keepdims=True)
        dbg_ref[...] += jnp.sum(sb_g, axis=0, keepdims=True)

        def back(ci, carry):
            r0 = pl.multiple_of(ci * ch, ch)
            da = wa_ref[kw - 1:kw, :] * dca_ref[ci, 0:ch, :]
            dg = wg_ref[kw - 1:kw, :] * dcg_ref[ci, 0:ch, :]
            for j in range(1, kw):
                da = da + wa_ref[kw - 1 - j:kw - j, :] * dca_ref[ci, j:j + ch, :]
                dg = dg + wg_ref[kw - 1 - j:kw - j, :] * dcg_ref[ci, j:j + ch, :]
            dpa_ref[pl.ds(r0, ch), :] = da.astype(BF16)
            dpg_ref[pl.ds(r0, ch), :] = dg.astype(BF16)
            return carry

        lax.fori_loop(0, nch, back, 0)

    blk = lambda off: pl.BlockSpec((None, s, tc), lambda i, bi: (bi, 0, i + off))
    wsp = lambda off: pl.BlockSpec((kw, tc), lambda i, bi: (0, i + off))
    bsp = lambda off: pl.BlockSpec((1, tc), lambda i, bi: (0, i + off))
    acc3 = pl.BlockSpec((kw, 8, tc), lambda i, bi: (0, 0, i))
    vec = pl.BlockSpec((1, tc), lambda i, bi: (0, i))
    return pl.pallas_call(
        body, name=name, grid=(nf, b),
        in_specs=[blk(0), blk(nf), wsp(0), wsp(nf), bsp(0), bsp(nf), blk(0)],
        out_specs=[blk(0), blk(0), acc3, acc3, vec, vec],
        out_shape=[jax.ShapeDtypeStruct((b, s, f), BF16), jax.ShapeDtypeStruct((b, s, f), BF16),
                   jax.ShapeDtypeStruct((kw, 8, f), F32), jax.ShapeDtypeStruct((kw, 8, f), F32),
                   jax.ShapeDtypeStruct((1, f), F32), jax.ShapeDtypeStruct((1, f), F32)],
        scratch_shapes=[pltpu.VMEM((nch, ch + halo, tc), F32)] * 4,
        compiler_params=_cp("parallel", "arbitrary"),
    )(p, p, dw, dw, dwb, dwb, ds)


def _tile_rows(r, n, dil):
    start = r + n * BLK * dil
    return pl.ds(start, BLK, stride=dil) if dil > 1 else pl.ds(start, BLK)


def _band_masks():
    qi = lax.broadcasted_iota(jnp.int32, (BLK, 2 * BLK), 0)
    kk = lax.broadcasted_iota(jnp.int32, (BLK, 2 * BLK), 1)
    both = jnp.logical_or(jnp.logical_and(kk < BLK, kk >= qi), jnp.logical_and(kk >= BLK, kk - BLK <= qi))
    return both, kk[:, :BLK] <= qi[:, :BLK]


def attn_fwd(q, kv, g, dil, hw, name):
    _, b, s, _ = q.shape
    nh = hw // HEAD_DIM
    nblk = s // dil // BLK
    scale = 1.0 / math.sqrt(HEAD_DIM)

    def body(q_ref, k_ref, v_ref, o_ref, lse_ref):
        h = pl.program_id(1)
        mask2, mask1 = _band_masks()
        mine = lax.broadcasted_iota(jnp.int32, (BLK, LANES), 1) == h

        @pl.when(h == 0)
        def _():
            lse_ref[...] = jnp.zeros(lse_ref.shape, F32)

        for r in range(dil):
            kp = vp = None
            for n in range(nblk):
                rs = _tile_rows(r, n, dil)
                qt = q_ref[rs, :].astype(BF16)
                kc = k_ref[rs, :].astype(BF16)
                vc = v_ref[rs, :].astype(BF16)
                if n == 0:
                    kcat, vcat, mask = kc, vc, mask1
                else:
                    kcat, vcat, mask = jnp.concatenate([kp, kc], axis=0), jnp.concatenate([vp, vc], axis=0), mask2
                sc = jnp.where(mask, _dot_nt(qt, kcat) * scale, NEG_INF)
                m = jnp.max(sc, axis=-1, keepdims=True)
                p = jnp.exp(sc - m)
                den = jnp.sum(p, axis=-1, keepdims=True)
                o_ref[rs, :] = _dot(p.astype(BF16), vcat) / den
                lse_ref[rs, :] = jnp.where(mine, m + jnp.log(den), lse_ref[rs, :])
                kp, vp = kc, vc

    col = lambda base: pl.BlockSpec((None, s, HEAD_DIM), lambda bi, h: (bi, 0, base + h))
    head = lambda base: pl.BlockSpec((None, None, s, HEAD_DIM), lambda bi, h: (base + h, bi, 0, 0))
    return pl.pallas_call(
        body, name=name, grid=(b, nh),
        in_specs=[head(g * nh), head(g * nh), head((N_GROUPS + g) * nh)],
        out_specs=[col(0), pl.BlockSpec((None, s, LANES), lambda bi, h: (bi, 0, 0))],
        out_shape=[jax.ShapeDtypeStruct((b, s, hw), F32), jax.ShapeDtypeStruct((b, s, LANES), F32)],
        compiler_params=_cp("parallel", "arbitrary"),
    )(q, kv, kv)


def attn_merge(outs, lses, name, tm=512):
    t, hw = outs[0].shape
    nh = hw // HEAD_DIM
    tm = _row_tile(t, tm)
    ng = len(outs)

    def body(*refs):
        o_refs, l_refs = refs[:ng], refs[ng:2 * ng]
        m_ref, lj_ref = refs[2 * ng:]
        ls = [l_refs[g][...] for g in range(ng)]
        mx = ls[0]
        for g in range(1, ng):
            mx = jnp.maximum(mx, ls[g])
        es = [jnp.exp(l - mx) for l in ls]
        tot = es[0]
        for g in range(1, ng):
            tot = tot + es[g]
        ws = [e / tot for e in es]
        lj_ref[...] = mx + jnp.log(tot)
        for h in range(nh):
            sl = slice(h * HEAD_DIM, (h + 1) * HEAD_DIM)
            acc = ws[0][:, h:h + 1] * o_refs[0][:, sl]
            for g in range(1, ng):
                acc = acc + ws[g][:, h:h + 1] * o_refs[g][:, sl]
            m_ref[:, sl] = acc.astype(BF16)

    row = pl.BlockSpec((tm, hw), lambda i: (i, 0))
    st = pl.BlockSpec((tm, LANES), lambda i: (i, 0))
    return pl.pallas_call(
        body, name=name, grid=(t // tm,), in_specs=[row] * ng + [st] * ng, out_specs=[row, st],
        out_shape=[jax.ShapeDtypeStruct((t, hw), BF16), jax.ShapeDtypeStruct((t, LANES), F32)],
        compiler_params=_cp("parallel"),
    )(*outs, *lses)


def attn_bwd_prep(dmerged, merged, name, tm=512):
    t, hw = merged.shape
    nh = hw // HEAD_DIM
    tm = _row_tile(t, tm)

    def body(d_ref, m_ref, o_ref):
        lane = lax.broadcasted_iota(jnp.int32, (tm, LANES), 1)
        acc = jnp.zeros((tm, LANES), F32)
        for h in range(nh):
            sl = slice(h * HEAD_DIM, (h + 1) * HEAD_DIM)
            dsum = jnp.sum(d_ref[:, sl] * m_ref[:, sl].astype(F32), axis=-1, keepdims=True)
            acc = jnp.where(lane == h, dsum, acc)
        o_ref[...] = acc

    row = pl.BlockSpec((tm, hw), lambda i: (i, 0))
    return pl.pallas_call(
        body, name=name, grid=(t // tm,), in_specs=[row, row], out_specs=pl.BlockSpec((tm, LANES), lambda i: (i, 0)),
        out_shape=jax.ShapeDtypeStruct((t, LANES), F32), compiler_params=_cp("parallel"),
    )(dmerged, merged)


def attn_bwd(q, kv, g, dil, do, lsej, dm, dq_buf, dk_buf, dv_buf, accumulate, hw, name):
    _, b, s, _ = q.shape
    nh = hw // HEAD_DIM
    nblk = s // dil // BLK
    scale = 1.0 / math.sqrt(HEAD_DIM)
    assert dk_buf is not None or not accumulate
    kv_at = 6 + (dq_buf is not None)

    def body(*refs):
        q_ref, k_ref, v_ref, do_ref, lj_ref, dm_ref = refs[:6]
        dq_ref, dk_ref, dv_ref = refs[-3:]
        dki_ref, dvi_ref = (refs[kv_at], refs[kv_at + 1]) if accumulate else (None, None)
        mask2, mask1 = _band_masks()
        mine = lax.broadcasted_iota(jnp.int32, (BLK, LANES), 1) == pl.program_id(1)

        def my_lane(v):
            return jnp.sum(jnp.where(mine, v, 0.0), axis=-1, keepdims=True)

        def put(rs, dk, dv):
            if accumulate:
                dk = dk + dki_ref[rs, :]
                dv = dv + dvi_ref[rs, :]
            dk_ref[rs, :] = dk
            dv_ref[rs, :] = dv

        for r in range(dil):
            kp = vp = hold_k = hold_v = rs_prev = None
            for n in range(nblk):
                rs = _tile_rows(r, n, dil)
                qt = q_ref[rs, :].astype(BF16)
                kc = k_ref[rs, :].astype(BF16)
                vc = v_ref[rs, :].astype(BF16)
                dot = do_ref[rs, :].astype(BF16)
                lm = my_lane(lj_ref[rs, :])
                dmm = my_lane(dm_ref[rs, :])
                if n == 0:
                    kcat, vcat, mask = kc, vc, mask1
                else:
                    kcat, vcat, mask = jnp.concatenate([kp, kc], axis=0), jnp.concatenate([vp, vc], axis=0), mask2
                p = jnp.exp(jnp.where(mask, _dot_nt(qt, kcat) * scale, NEG_INF) - lm)
                ds = (p * (_dot_nt(dot, vcat) - dmm)).astype(BF16)
                dq_ref[rs, :] = _dot(ds, kcat) * scale
                dkc = _dot_tn(ds, qt) * scale
                dvc = _dot_tn(p.astype(BF16), dot)
                if n > 0:
                    put(rs_prev, hold_k + dkc[:BLK, :], hold_v + dvc[:BLK, :])
                    dkc, dvc = dkc[BLK:, :], dvc[BLK:, :]
                hold_k, hold_v, kp, vp, rs_prev = dkc, dvc, kc, vc, rs
            put(rs_prev, hold_k, hold_v)

    col = lambda base: pl.BlockSpec((None, s, HEAD_DIM), lambda bi, h: (bi, 0, base + h))
    any_spec = pl.BlockSpec(memory_space=pl.ANY)
    stat = pl.BlockSpec((None, s, LANES), lambda bi, h: (bi, 0, 0))
    head = lambda base: pl.BlockSpec((None, None, s, HEAD_DIM), lambda bi, h: (base + h, bi, 0, 0))
    in_specs = [head(g * nh), head(g * nh), head((N_GROUPS + g) * nh), col(0), stat, stat]
    args = [q, kv, kv, do, lsej, dm]
    aliases = {}
    if dq_buf is not None:
        in_specs.append(any_spec)
        args.append(dq_buf)
        aliases[6] = 0
    if dk_buf is not None:
        in_specs += [col(g * nh) if accumulate else any_spec] * 2
        args += [dk_buf, dv_buf]
        aliases.update({kv_at: 1, kv_at + 1: 2})
    shape = jax.ShapeDtypeStruct((b, s, N_GROUPS * hw), F32)
    return pl.pallas_call(
        body, name=name, grid=(b, nh), in_specs=in_specs, out_specs=[col(g * nh)] * 3, out_shape=[shape] * 3,
        input_output_aliases=aliases, compiler_params=_cp("parallel", "parallel"),
    )(*args)


def sum_parts(g, recv, me, name, tm=512):
    _, rows, c = g.shape
    n = recv.shape[0]
    tm = _row_tile(rows, tm)

    def body(me_ref, g_ref, r_ref, o_ref):
        acc = g_ref[...].astype(F32)
        for j in range(n):
            acc = acc + r_ref[j].astype(F32)
        o_ref[...] = acc

    return pl.pallas_call(
        body, name=name,
        grid_spec=pltpu.PrefetchScalarGridSpec(
            num_scalar_prefetch=1, grid=(rows // tm,),
            in_specs=[pl.BlockSpec((None, tm, c), lambda i, me_ref: (me_ref[0], i, 0)),
                      pl.BlockSpec((n, tm, c), lambda i, me_ref: (0, i, 0))],
            out_specs=pl.BlockSpec((tm, c), lambda i, me_ref: (i, 0))),
        out_shape=jax.ShapeDtypeStruct((rows, c), F32), compiler_params=_cp("parallel"),
    )(me, g, recv)


def adamw(w, m, v, g_parts, name, tm=256):
    rows, c = w.shape
    tm = _row_tile(rows, tm)
    npart = len(g_parts)

    def body(*refs):
        w_ref, m_ref, v_ref = refs[:3]
        g_refs = refs[3:3 + npart]
        go_ref, d_ref, mo_ref, vo_ref = refs[3 + npart:]
        g = g_refs[0][...]
        for k in range(1, npart):
            g = g + g_refs[k][...]
        mn = ADAM_B1 * m_ref[...] + (1.0 - ADAM_B1) * g
        vn = ADAM_B2 * v_ref[...] + (1.0 - ADAM_B2) * (g * g)
        m_hat = mn / (1.0 - ADAM_B1 ** ADAM_STEP)
        v_hat = vn / (1.0 - ADAM_B2 ** ADAM_STEP)
        go_ref[...] = g
        d_ref[...] = -ADAM_LR * (m_hat / (jnp.sqrt(v_hat) + ADAM_EPS) + ADAM_WD * w_ref[...])
        mo_ref[...] = mn
        vo_ref[...] = vn

    row = pl.BlockSpec((tm, c), lambda i: (i, 0))
    return pl.pallas_call(
        body, name=name, grid=(rows // tm,), in_specs=[row] * (3 + npart), out_specs=[row] * 4,
        out_shape=[jax.ShapeDtypeStruct((rows, c), F32)] * 4, compiler_params=_cp("parallel"),
    )(w, m, v, *g_parts)


def _place():
    return lax.axis_index("x"), lax.axis_index("y"), lax.axis_index("c")


def _other_chips(x, y, c):
    return [(1 - x, y, c), (x, 1 - y, c), (1 - x, 1 - y, c)]


def _chip_of(px, py):
    return 2 * px + py


HBM_SPEC = pl.BlockSpec(memory_space=pltpu.HBM)
SEM_SPEC = pl.BlockSpec(memory_space=pltpu.SEMAPHORE)
ANY_SPEC = pl.BlockSpec(memory_space=pl.ANY)
DATAFLOW = pltpu.SideEffectType.DATAFLOW_SIDE_EFFECTING
N_PEER_CHIPS = N_CHIPS - 1


def _hbm(a):
    return pltpu.with_memory_space_constraint(a, pltpu.HBM)


def _hbm_like(arrays):
    return [pltpu.HBM(a.shape, a.dtype) for a in arrays]


def cast_place(w, layer, me, out_dtype, name, tm=512, nslots=N_CHIPS, dep=None):
    rows, c = w.shape[-2:]
    tm = _row_tile(rows, tm)

    def body(me_ref, w_ref, *rest):
        rest[-1][...] = w_ref[...].astype(out_dtype)

    if layer is None:
        in_specs = [pl.BlockSpec((tm, c), lambda i, me_ref: (i, 0))]
    else:
        in_specs = [pl.BlockSpec((None, tm, c), lambda i, me_ref: (layer, i, 0))]
    args = [me, w]
    if dep is not None:
        in_specs.append(pl.BlockSpec(DEP_SPEC_SHAPE, lambda i, me_ref: (0, 0)))
        args.append(dep)
    return pl.pallas_call(
        body, name=name,
        grid_spec=pltpu.PrefetchScalarGridSpec(
            num_scalar_prefetch=1, grid=(rows // tm,), in_specs=in_specs,
            out_specs=pl.BlockSpec((None, tm, c), lambda i, me_ref: (me_ref[0], i, 0))),
        out_shape=jax.ShapeDtypeStruct((nslots, rows, c), out_dtype), compiler_params=_cp("parallel"),
    )(*args)


def gather_start(lands, chunk_sizes, name="gather_start"):
    n = len(lands)
    nch = len(chunk_sizes)
    assert sum(chunk_sizes) == n

    def body(*refs):
        land_refs = refs[:n]
        outs = refs[n:]
        send_sems, recv_sems = outs[:nch], outs[nch:2 * nch]
        token = outs[-1]
        x, y, c = _place()
        me = _chip_of(x, y)
        peers = _other_chips(x, y, c)
        k = 0
        for ck, size in enumerate(chunk_sizes):
            for pos in range(size):
                for r, peer in enumerate(peers):
                    pltpu.make_async_remote_copy(
                        src_ref=land_refs[k].at[me], dst_ref=land_refs[k].at[me],
                        send_sem=send_sems[ck].at[N_PEER_CHIPS * pos + r], recv_sem=recv_sems[ck].at[N_PEER_CHIPS * pos + r],
                        device_id=peer, device_id_type=MESH).start()
                k += 1
        token[...] = jnp.zeros(token.shape, F32)

    sems = [pltpu.SemaphoreType.DMA((N_PEER_CHIPS * s,)) for s in chunk_sizes]
    res = pl.pallas_call(
        body, name=name,
        out_shape=(*sems, *sems, *_hbm_like(lands), jax.ShapeDtypeStruct(DEP_SPEC_SHAPE, F32)),
        in_specs=[HBM_SPEC] * n,
        out_specs=(*[SEM_SPEC] * (2 * nch), *[HBM_SPEC] * n, pl.BlockSpec(memory_space=pltpu.VMEM)),
        input_output_aliases={k: 2 * nch + k for k in range(n)},
        compiler_params=pltpu.CompilerParams(has_side_effects=DATAFLOW),
    )(*[_hbm(a) for a in lands])
    return res[:nch], res[nch:2 * nch], res[2 * nch:2 * nch + n], res[-1]


def gather_wait(send_sem, recv_sem, lands, after, name):
    n = len(lands)

    def body(*refs):
        land_refs = refs[:n]
        ssem, rsem = refs[n], refs[n + 1]
        x, y, c = _place()
        me = _chip_of(x, y)
        for pos in range(n):
            for r, peer in enumerate(_other_chips(x, y, c)):
                cp = pltpu.make_async_remote_copy(
                    src_ref=land_refs[pos].at[me], dst_ref=land_refs[pos].at[_chip_of(peer[0], peer[1])],
                    send_sem=ssem.at[N_PEER_CHIPS * pos + r], recv_sem=rsem.at[N_PEER_CHIPS * pos + r],
                    device_id=peer, device_id_type=MESH)
                cp.wait_send()
                cp.wait_recv()

    return pl.pallas_call(
        body, name=name, out_shape=tuple(_hbm_like(lands)),
        in_specs=[*[HBM_SPEC] * n, SEM_SPEC, SEM_SPEC, ANY_SPEC], out_specs=[HBM_SPEC] * n,
        input_output_aliases={k: k for k in range(n)},
        compiler_params=pltpu.CompilerParams(has_side_effects=DATAFLOW),
    )(*lands, send_sem, recv_sem, after)


def scatter_start(grads, name):
    n = len(grads)
    recvs = [lax.empty((N_PEER_CHIPS, *g.shape[1:]), g.dtype) for g in grads]

    def body(*refs):
        g_refs, r_refs = refs[:n], refs[n:2 * n]
        send_sems, recv_sems = refs[2 * n], refs[2 * n + 1]
        token = refs[-1]
        x, y, c = _place()
        for k in range(n):
            for r, peer in enumerate(_other_chips(x, y, c)):
                pltpu.make_async_remote_copy(
                    src_ref=g_refs[k].at[_chip_of(peer[0], peer[1])], dst_ref=r_refs[k].at[r],
                    send_sem=send_sems.at[N_PEER_CHIPS * k + r], recv_sem=recv_sems.at[N_PEER_CHIPS * k + r],
                    device_id=peer, device_id_type=MESH).start()
        token[...] = jnp.zeros(token.shape, F32)

    sem = pltpu.SemaphoreType.DMA((N_PEER_CHIPS * n,))
    res = pl.pallas_call(
        body, name=name,
        out_shape=(sem, sem, *_hbm_like(grads), *_hbm_like(recvs), jax.ShapeDtypeStruct(DEP_SPEC_SHAPE, F32)),
        in_specs=[HBM_SPEC] * (2 * n),
        out_specs=(SEM_SPEC, SEM_SPEC, *[HBM_SPEC] * (2 * n), pl.BlockSpec(memory_space=pltpu.VMEM)),
        input_output_aliases={k: 2 + k for k in range(2 * n)},
        compiler_params=pltpu.CompilerParams(has_side_effects=DATAFLOW),
    )(*[_hbm(a) for a in grads], *[_hbm(a) for a in recvs])
    return res[0], res[1], res[2:2 + n], res[2 + n:2 + 2 * n], res[-1]


def scatter_wait(send_sem, recv_sem, grads, recvs, after, name):
    n = len(grads)

    def body(*refs):
        g_refs, r_refs = refs[:n], refs[n:2 * n]
        ssem, rsem = refs[2 * n], refs[2 * n + 1]
        x, y, c = _place()
        for k in range(n):
            for r, peer in enumerate(_other_chips(x, y, c)):
                cp = pltpu.make_async_remote_copy(
                    src_ref=g_refs[k].at[_chip_of(peer[0], peer[1])], dst_ref=r_refs[k].at[r],
                    send_sem=ssem.at[N_PEER_CHIPS * k + r], recv_sem=rsem.at[N_PEER_CHIPS * k + r],
                    device_id=peer, device_id_type=MESH)
                cp.wait_send()
                cp.wait_recv()

    res = pl.pallas_call(
        body, name=name, out_shape=(*_hbm_like(grads), *_hbm_like(recvs)),
        in_specs=[*[HBM_SPEC] * (2 * n), SEM_SPEC, SEM_SPEC, ANY_SPEC], out_specs=[HBM_SPEC] * (2 * n),
        input_output_aliases={k: k for k in range(2 * n)},
        compiler_params=pltpu.CompilerParams(has_side_effects=DATAFLOW),
    )(*grads, *recvs, send_sem, recv_sem, after)
    return res[:n], res[n:]


def swap_start(parts, name):
    n = len(parts)
    lands = [lax.empty(p.shape, p.dtype) for p in parts]

    def body(*refs):
        p_refs, l_refs = refs[:n], refs[n:2 * n]
        sems = refs[2 * n:4 * n]
        token = refs[-1]
        x, y, c = _place()
        for k in range(n):
            pltpu.make_async_remote_copy(
                src_ref=p_refs[k], dst_ref=l_refs[k], send_sem=sems[k], recv_sem=sems[n + k],
                device_id=(x, y, 1 - c), device_id_type=MESH).start()
        token[...] = jnp.zeros(token.shape, F32)

    sem = pltpu.SemaphoreType.DMA(())
    res = pl.pallas_call(
        body, name=name,
        out_shape=(*[sem] * (2 * n), *_hbm_like(parts), *_hbm_like(lands), jax.ShapeDtypeStruct(DEP_SPEC_SHAPE, F32)),
        in_specs=[HBM_SPEC] * (2 * n),
        out_specs=(*[SEM_SPEC] * (2 * n), *[HBM_SPEC] * (2 * n), pl.BlockSpec(memory_space=pltpu.VMEM)),
        input_output_aliases={k: 2 * n + k for k in range(2 * n)},
        compiler_params=pltpu.CompilerParams(has_side_effects=DATAFLOW),
    )(*[_hbm(a) for a in parts], *[_hbm(a) for a in lands])
    return res[:n], res[n:2 * n], res[2 * n:3 * n], res[3 * n:4 * n], res[-1]


def swap_wait(send_sem, recv_sem, part, land, after, name):
    def body(p_ref, l_ref, ssem, rsem, after_ref, p_out, l_out):
        x, y, c = _place()
        cp = pltpu.make_async_remote_copy(src_ref=p_ref, dst_ref=l_ref, send_sem=ssem, recv_sem=rsem,
                                          device_id=(x, y, 1 - c), device_id_type=MESH)
        cp.wait_send()
        cp.wait_recv()

    return pl.pallas_call(
        body, name=name, out_shape=tuple(_hbm_like([part, land])),
        in_specs=[HBM_SPEC, HBM_SPEC, SEM_SPEC, SEM_SPEC, ANY_SPEC], out_specs=[HBM_SPEC, HBM_SPEC],
        input_output_aliases={0: 0, 1: 1},
        compiler_params=pltpu.CompilerParams(has_side_effects=DATAFLOW),
    )(part, land, send_sem, recv_sem, after)


def _xor_peer(x, y, c, k):
    px, py, pc = x ^ ((k >> 2) & 1), y ^ ((k >> 1) & 1), c ^ (k & 1)
    return (px, py, pc), 4 * px + 2 * py + pc


def small_start(land, name="small_start"):
    def body(l_ref, ssem, rsem, l_out, token):
        x, y, c = _place()
        me = 4 * x + 2 * y + c
        for k in range(1, N_DEV):
            peer, _ = _xor_peer(x, y, c, k)
            pltpu.make_async_remote_copy(
                src_ref=l_ref.at[me], dst_ref=l_ref.at[me], send_sem=ssem.at[k - 1], recv_sem=rsem.at[k - 1],
                device_id=peer, device_id_type=MESH).start()
        token[...] = jnp.zeros(token.shape, F32)

    sem = pltpu.SemaphoreType.DMA((N_DEV - 1,))
    return pl.pallas_call(
        body, name=name,
        out_shape=(sem, sem, pltpu.HBM(land.shape, land.dtype), jax.ShapeDtypeStruct(DEP_SPEC_SHAPE, F32)),
        in_specs=[HBM_SPEC], out_specs=(SEM_SPEC, SEM_SPEC, HBM_SPEC, pl.BlockSpec(memory_space=pltpu.VMEM)),
        input_output_aliases={0: 2}, compiler_params=pltpu.CompilerParams(has_side_effects=DATAFLOW),
    )(_hbm(land))


def small_wait(send_sem, recv_sem, land, after, name="small_wait"):
    def body(l_ref, ssem, rsem, after_ref, l_out):
        x, y, c = _place()
        me = 4 * x + 2 * y + c
        for k in range(1, N_DEV):
            peer, slot = _xor_peer(x, y, c, k)
            cp = pltpu.make_async_remote_copy(
                src_ref=l_ref.at[me], dst_ref=l_ref.at[slot], send_sem=ssem.at[k - 1], recv_sem=rsem.at[k - 1],
                device_id=peer, device_id_type=MESH)
            cp.wait_send()
            cp.wait_recv()

    return pl.pallas_call(
        body, name=name, out_shape=pltpu.HBM(land.shape, land.dtype),
        in_specs=[HBM_SPEC, SEM_SPEC, SEM_SPEC, ANY_SPEC], out_specs=HBM_SPEC, input_output_aliases={0: 0},
        compiler_params=pltpu.CompilerParams(has_side_effects=DATAFLOW),
    )(land, send_sem, recv_sem, after)


def sum_slots(land, name="sum_slots", tm=256):
    n, rows, c = land.shape
    tm = _row_tile(rows, tm)

    def body(l_ref, o_ref):
        acc = l_ref[0]
        for j in range(1, n):
            acc = acc + l_ref[j]
        o_ref[...] = acc

    return pl.pallas_call(
        body, name=name, grid=(rows // tm,), in_specs=[pl.BlockSpec((n, tm, c), lambda i: (0, i, 0))],
        out_specs=pl.BlockSpec((tm, c), lambda i: (i, 0)), out_shape=jax.ShapeDtypeStruct((rows, c), F32),
        compiler_params=_cp("parallel"),
    )(land)


PACK_ROW_TILE = 256


def _pack(arrays):
    flat = jnp.concatenate([a.reshape(-1).astype(F32) for a in arrays])
    n = flat.shape[0]
    rows = -(-n // LANES)
    rows = -(-rows // PACK_ROW_TILE) * PACK_ROW_TILE
    return jnp.pad(flat, (0, rows * LANES - n)).reshape(rows, LANES)


def _unpack(packed, shapes, lead=()):
    flat = packed.reshape(*lead, -1)
    out, off = [], 0
    for shp in shapes:
        n = math.prod(shp)
        out.append(flat[..., off:off + n].reshape(*lead, *shp))
        off += n
    return out


def _row(vec):
    return vec.reshape(1, -1)


def kernel(x, mix_pre_g, mix_post_g, ffn_pre_g, ffn_post_g, cm_w_in, cm_b_in, cm_dw, cm_dw_b, cm_ln_g, cm_ln_b, cm_w_out, cm_b_out, kv_norm_g, w_kv, w_q, w_o, ffn_w_in, ffn_dw, ffn_dw_b, ffn_w_out, loss_target, m_mix_pre_g, m_mix_post_g, m_ffn_pre_g, m_ffn_post_g, m_cm_w_in, m_cm_b_in, m_cm_dw, m_cm_dw_b, m_cm_ln_g, m_cm_ln_b, m_cm_w_out, m_cm_b_out, m_kv_norm_g, m_w_kv, m_w_q, m_w_o, m_ffn_w_in, m_ffn_dw, m_ffn_dw_b, m_ffn_w_out, v_mix_pre_g, v_mix_post_g, v_ffn_pre_g, v_ffn_post_g, v_cm_w_in, v_cm_b_in, v_cm_dw, v_cm_dw_b, v_cm_ln_g, v_cm_ln_b, v_cm_w_out, v_cm_b_out, v_kv_norm_g, v_w_kv, v_w_q, v_w_o, v_ffn_w_in, v_ffn_dw, v_ffn_dw_b, v_ffn_w_out):
    names = ["mix_pre_g", "mix_post_g", "ffn_pre_g", "ffn_post_g", "cm_w_in", "cm_b_in", "cm_dw", "cm_dw_b", "cm_ln_g",
             "cm_ln_b", "cm_w_out", "cm_b_out", "kv_norm_g", "w_kv", "w_q", "w_o", "ffn_w_in", "ffn_dw", "ffn_dw_b",
             "ffn_w_out"]
    w_in = dict(zip(names, [mix_pre_g, mix_post_g, ffn_pre_g, ffn_post_g, cm_w_in, cm_b_in, cm_dw, cm_dw_b, cm_ln_g,
                            cm_ln_b, cm_w_out, cm_b_out, kv_norm_g, w_kv, w_q, w_o, ffn_w_in, ffn_dw, ffn_dw_b, ffn_w_out]))
    m_in = dict(zip(names, [m_mix_pre_g, m_mix_post_g, m_ffn_pre_g, m_ffn_post_g, m_cm_w_in, m_cm_b_in, m_cm_dw, m_cm_dw_b,
                            m_cm_ln_g, m_cm_ln_b, m_cm_w_out, m_cm_b_out, m_kv_norm_g, m_w_kv, m_w_q, m_w_o, m_ffn_w_in,
                            m_ffn_dw, m_ffn_dw_b, m_ffn_w_out]))
    v_in = dict(zip(names, [v_mix_pre_g, v_mix_post_g, v_ffn_pre_g, v_ffn_post_g, v_cm_w_in, v_cm_b_in, v_cm_dw, v_cm_dw_b,
                            v_cm_ln_g, v_cm_ln_b, v_cm_w_out, v_cm_b_out, v_kv_norm_g, v_w_kv, v_w_q, v_w_o, v_ffn_w_in,
                            v_ffn_dw, v_ffn_dw_b, v_ffn_w_out]))

    bsz, seq, d = x.shape
    t = bsz * seq
    n_b = DEPTH - N_A
    hw = w_o.shape[-1]
    qw = N_GROUPS * hw
    f2 = ffn_dw_b.shape[-1]
    f = f2 // 2
    me_chip = _chip_of(lax.axis_index("x"), lax.axis_index("y"))

    big = ["cm_w_in", "cm_w_out", "w_kv", "w_q", "w_o", "ffn_w_in", "ffn_w_out"]
    row_sharded = ("cm_w_out", "w_o", "ffn_w_out")
    small_sharded = ["cm_b_in", "cm_dw", "cm_dw_b", "cm_ln_g", "cm_ln_b", "cm_b_out", "ffn_dw"]
    small_pack = _pack([w_in[n] for n in small_sharded])
    chunks = [
        [("cm_w_in", 0), ("small", None)],
        [("cm_w_out", 0)],
        [("ffn_w_in", 0), ("ffn_w_out", 0)],
        [("cm_w_in", 1), ("cm_w_out", 1)],
        [("ffn_w_in", 1), ("ffn_w_out", 1)],
        [("w_kv", None)],
        [("w_q", 0), ("w_o", 0)],
        [("ffn_w_in", 2), ("ffn_w_out", 2)],
        [("w_q", 1), ("w_o", 1)],
        [("ffn_w_in", 3), ("ffn_w_out", 3)],
    ]
    pieces = [pc for ch in chunks for pc in ch]
    chunk_of = {pc: ck for ck, ch in enumerate(chunks) for pc in ch}

    me_arr = me_chip.astype(jnp.int32).reshape(1)

    def land_of(pc, dep=None):
        n, l = pc
        if n == "small":
            return cast_place(small_pack, None, me_arr, F32, name="place_small", dep=dep)
        return cast_place(w_in[n], l, me_arr, BF16, name=f"place_{n}_{l}", dep=dep)

    n_early = len(chunks[0]) + len(chunks[1])
    lands_a = [land_of(pc) for pc in pieces[:n_early]]
    send_a, recv_a, lands_fa, token_a = gather_start(lands_a, [len(chunks[0]), len(chunks[1])], name="gather_start_a")
    lands_b = [land_of(pc, token_a if k == 0 else None) for k, pc in enumerate(pieces[n_early:])]
    send_b, recv_b, lands_fb, token = gather_start(lands_b, [len(ch) for ch in chunks[2:]], name="gather_start_b")
    g_send, g_recv, lands_f = [*send_a, *send_b], [*recv_a, *recv_b], [*lands_fa, *lands_fb]
    weights = {}

    def finish_chunk(ck, after):
        lo = sum(len(ch) for ch in chunks[:ck])
        hi = lo + len(chunks[ck])
        got = gather_wait(g_send[ck], g_recv[ck], lands_f[lo:hi], after, name=f"gather_wait{ck}")
        for pc, arr in zip(chunks[ck], got):
            weights[pc] = arr.reshape(1, -1, arr.shape[-1]) if pc[0] in row_sharded else arr

    def wmat(n, l=None, after=None):
        if (n, l) not in weights:
            finish_chunk(chunk_of[(n, l)], after)
        arr = weights[(n, l)]
        return arr, arr.shape[0]

    finish_chunk(0, token)
    small_full = {}
    for n, arr4 in zip(small_sharded, _unpack(weights[("small", None)], [w_in[n].shape for n in small_sharded], lead=(N_CHIPS,))):
        shp = w_in[n].shape
        small_full[n] = jnp.moveaxis(arr4, 0, -2).reshape(*shp[:-1], N_CHIPS * shp[-1])

    x2d = x.reshape(t, d)
    saved = []
    (h1,) = resid_norm_fwd(x2d, None, None, [_row(mix_pre_g[0])], name="norm_in", dep=token)
    xcur = x2d
    kv_state = None
    for i in range(DEPTH):
        sv = {"x_in": xcur, "h1": h1}
        if i < N_A:
            z = mm_nn(h1, *wmat("cm_w_in", i, h1), 1, 0, bias=_row(small_full["cm_b_in"][i]), name=f"cm_in{i}")
            u2 = glu_conv_fwd(z.reshape(bsz, seq, 2 * d), small_full["cm_dw"][i], _row(small_full["cm_dw_b"][i]),
                              name=f"glu_conv{i}").reshape(t, d)
            u4 = ln_silu_fwd(u2, _row(small_full["cm_ln_g"][i]), _row(small_full["cm_ln_b"][i]), name=f"ln_silu{i}")
            y = mm_nn(u4, *wmat("cm_w_out", i, u4), 1, 0, bias=_row(small_full["cm_b_out"][i]), out_dtype=BF16,
                      name=f"cm_out{i}")
            sv.update(z=z, u2=u2, u4=u4)
        else:
            j = i - N_A
            q = mm_nn(h1, *wmat("w_q", j, h1), 1, 0, name=f"q_proj{j}", head_major=True).reshape(-1, bsz, seq, HEAD_DIM)
            outs, lses = [], []
            for g, dil in enumerate(DILATIONS):
                o_g, l_g = attn_fwd(q, kv_state["kv"], g, dil, hw, name=f"attn_fwd{j}_{g}")
                outs.append(o_g.reshape(t, hw))
                lses.append(l_g.reshape(t, LANES))
            merged, lsej = attn_merge(outs, lses, name=f"attn_merge{j}")
            y = mm_nn(merged, *wmat("w_o", j, merged), 1, 0, out_dtype=BF16, name=f"o_proj{j}")
            sv.update(q=q, merged=merged, lsej=lsej)
        x1, h2 = resid_norm_fwd(xcur, y, _row(mix_post_g[i]), [_row(ffn_pre_g[i])], name=f"resid_mix{i}")
        p = mm_nn(h2, *wmat("ffn_w_in", i, h2), 1, 0, out_dtype=BF16, name=f"ffn_in{i}", tm=2048)
        s_act = ffn_mid_fwd(p.reshape(bsz, seq, f2), small_full["ffn_dw"][i], _row(ffn_dw_b[i]), name=f"ffn_mid{i}").reshape(t, f)
        y2 = mm_nn(s_act, *wmat("ffn_w_out", i), 1, 0, out_dtype=BF16, name=f"ffn_out{i}")
        next_gains = []
        if i + 1 < DEPTH:
            next_gains.append(_row(mix_pre_g[i + 1]))
        if i == N_A - 1:
            next_gains.append(_row(kv_norm_g))
        res = resid_norm_fwd(x1, y2, _row(ffn_post_g[i]), next_gains, name=f"resid_ffn{i}")
        sv.update(y=y, x1=x1, h2=h2, p=p, s=s_act, y2=y2)
        saved.append(sv)
        xcur = res[0]
        if i + 1 < DEPTH:
            h1 = res[1]
        if i == N_A - 1:
            kvn = res[2]
            kv = mm_nn(kvn, *wmat("w_kv", None, kvn), 1, 0, name="kv_proj", head_major=True).reshape(-1, bsz, seq, HEAD_DIM)
            kv_state = {"kv": kv, "kvn": kvn, "x_a": xcur}

    dx, loss_tile = loss_fwd_bwd(xcur, loss_target.reshape(t, d))
    loss = lax.psum(loss_tile[0, 0], ("x", "y", "c"))

    gsm = {n: [None] * w_in[n].shape[0] for n in
           ["mix_pre_g", "mix_post_g", "ffn_pre_g", "ffn_post_g", "cm_b_in", "cm_dw", "cm_dw_b", "cm_ln_g", "cm_ln_b",
            "cm_b_out", "ffn_dw", "ffn_dw_b"]}
    gbig = {}
    in_flight = []
    dep = None

    def start_scatter(pcs, tag):
        ssem, rsem, g_f, r_f, tok = scatter_start([gbig[pc] for pc in pcs], name=f"scatter_start_{tag}")
        in_flight.append((pcs, ssem, rsem, g_f, r_f))
        return tok

    dk_buf = dv_buf = None
    for i in range(DEPTH - 1, -1, -1):
        sv = saved[i]
        dy2, dg, _ = norm_bwd(sv["y2"], _row(ffn_post_g[i]), dx, out_dtype=BF16, name=f"bwd_ffn_post{i}", dep=dep)
        gsm["ffn_post_g"][i] = dg
        ds = mm_nt(dy2, *wmat("ffn_w_out", i), 1, 0, out_dtype=BF16, name=f"bwd_ffn_out_dx{i}")
        gbig[("ffn_w_out", i)] = mm_tn(sv["s"], dy2, 1, name=f"bwd_ffn_out_dw{i}").reshape(N_CHIPS, f // N_CHIPS, d)
        dpa, dpg, ddwa, ddwg, ddba, ddbg = ffn_mid_bwd(sv["p"].reshape(bsz, seq, f2), small_full["ffn_dw"][i], _row(ffn_dw_b[i]),
                                                       ds.reshape(bsz, seq, f), name=f"bwd_ffn_mid{i}")
        gsm["ffn_dw"][i] = jnp.concatenate([jnp.sum(ddwa, axis=1), jnp.sum(ddwg, axis=1)], axis=-1)
        gsm["ffn_dw_b"][i] = jnp.concatenate([ddba, ddbg], axis=-1)
        dp = [dpa.reshape(t, f), dpg.reshape(t, f)]
        dh2 = mm_nt(dp, *wmat("ffn_w_in", i), 1, 0, name=f"bwd_ffn_in_dx{i}")
        gbig[("ffn_w_in", i)] = mm_tn(sv["h2"], dp, N_CHIPS, name=f"bwd_ffn_in_dw{i}")
        dx1, dg, _ = norm_bwd(sv["x1"], _row(ffn_pre_g[i]), dh2, add=dx, name=f"bwd_ffn_pre{i}")
        gsm["ffn_pre_g"][i] = dg
        dep = start_scatter([("ffn_w_in", 0), ("ffn_w_out", 0)], "ffn0") if i == 0 else None
        dy, dg, dbias = norm_bwd(sv["y"], _row(mix_post_g[i]), dx1, out_dtype=BF16, name=f"bwd_mix_post{i}", dep=dep)
        gsm["mix_post_g"][i] = dg
        if i < N_A:
            gsm["cm_b_out"][i] = dbias
            du4 = mm_nt(dy, *wmat("cm_w_out", i), 1, 0, name=f"bwd_cm_out_dx{i}")
            gbig[("cm_w_out", i)] = mm_tn(sv["u4"], dy, 1, name=f"bwd_cm_out_dw{i}").reshape(N_CHIPS, d // N_CHIPS, d)
            du2, dlg, dlb = ln_silu_bwd(sv["u2"], _row(small_full["cm_ln_g"][i]), _row(small_full["cm_ln_b"][i]), du4,
                                        name=f"bwd_ln_silu{i}")
            gsm["cm_ln_g"][i], gsm["cm_ln_b"][i] = dlg, dlb
            dza, dzg, ddw, ddwb, dba, dbg = glu_conv_bwd(sv["z"].reshape(bsz, seq, 2 * d), small_full["cm_dw"][i],
                                                         du2.reshape(bsz, seq, d), name=f"bwd_glu_conv{i}")
            gsm["cm_dw"][i] = jnp.sum(ddw, axis=1)
            gsm["cm_dw_b"][i] = ddwb
            gsm["cm_b_in"][i] = jnp.concatenate([dba, dbg], axis=-1)
            dz = [dza.reshape(t, d), dzg.reshape(t, d)]
            dh1 = mm_nt(dz, *wmat("cm_w_in", i), 1, 0, name=f"bwd_cm_in_dx{i}")
            gbig[("cm_w_in", i)] = mm_tn(sv["h1"], dz, N_CHIPS, name=f"bwd_cm_in_dw{i}")
        else:
            j = i - N_A
            dmerged = mm_nt(dy, *wmat("w_o", j), 1, 0, name=f"bwd_o_proj_dx{j}")
            gbig[("w_o", j)] = mm_tn(sv["merged"], dy, 1, name=f"bwd_o_proj_dw{j}").reshape(N_CHIPS, hw // N_CHIPS, d)
            dmt = attn_bwd_prep(dmerged, sv["merged"], name=f"bwd_attn_prep{j}")
            dq_buf = None
            add_to_kv = dk_buf is not None
            for g, dil in enumerate(DILATIONS):
                dq_buf, dk_buf, dv_buf = attn_bwd(
                    sv["q"], kv_state["kv"], g, dil, dmerged.reshape(bsz, seq, hw), sv["lsej"].reshape(bsz, seq, LANES),
                    dmt.reshape(bsz, seq, LANES), dq_buf, dk_buf, dv_buf, add_to_kv, hw, name=f"attn_bwd{j}_{g}")
            dq = dq_buf.reshape(t, qw)
            dh1 = mm_nt(dq, *wmat("w_q", j), 1, 0, name=f"bwd_q_proj_dx{j}")
            gbig[("w_q", j)] = mm_tn(sv["h1"], dq, N_CHIPS, name=f"bwd_q_proj_dw{j}")
        dx, dg, _ = norm_bwd(sv["x_in"], _row(mix_pre_g[i]), dh1, add=dx1, name=f"bwd_mix_pre{i}")
        gsm["mix_pre_g"][i] = dg
        if i > N_A:
            dep = start_scatter([("ffn_w_in", i), ("ffn_w_out", i), ("w_q", i - N_A), ("w_o", i - N_A)], f"l{i}")
        elif 0 < i < N_A:
            dep = start_scatter([("ffn_w_in", i), ("ffn_w_out", i), ("cm_w_in", i), ("cm_w_out", i)], f"l{i}")
        elif i == 0:
            last_token = start_scatter([("cm_w_in", 0), ("cm_w_out", 0)], "cm0")
        if i == N_A:
            dkv = [dk_buf.reshape(t, qw), dv_buf.reshape(t, qw)]
            dkvn = mm_nt(dkv, *wmat("w_kv"), 1, 0, name="bwd_kv_proj_dx")
            gbig[("w_kv", None)] = mm_tn(kv_state["kvn"], dkv, N_CHIPS, name="bwd_kv_proj_dw")
            dx, dg_kv, _ = norm_bwd(kv_state["x_a"], _row(kv_norm_g), dkvn, add=dx, name="bwd_kv_norm")
            dep = start_scatter([("ffn_w_in", i), ("ffn_w_out", i), ("w_q", 0), ("w_o", 0), ("w_kv", None)], f"l{i}")
    grad_x = dx.reshape(bsz, seq, d)

    plane_of = {}
    outs_g, outs_d, outs_m, outs_v = {}, {}, {}, {}

    def finish_scatter(k, after):
        pcs, ssem, rsem, g_f, r_f = in_flight[k]
        g_done, r_done = scatter_wait(ssem, rsem, g_f, r_f, after, name=f"scatter_wait{k}")
        for pc, g_arr, r_arr in zip(pcs, g_done, r_done):
            plane_of[pc] = sum_parts(g_arr, r_arr, me_arr, name=f"sum_chips_{pc[0]}_{pc[1]}")

    def update(group, tag, after):
        plane = []
        for n in group:
            if w_in[n].ndim == 2:
                plane.append(plane_of[(n, None)])
            else:
                plane.append(jnp.concatenate([plane_of[(n, l)] for l in range(w_in[n].shape[0])], axis=0))
        ssems, rsems, plane_f, land_f, _ = swap_start(plane, name=f"swap_start_{tag}")
        for k, n in enumerate(group):
            p_mine, p_other = swap_wait(ssems[k], rsems[k], plane_f[k], land_f[k], after, name=f"swap_wait_{n}")
            shp = w_in[n].shape
            flat = lambda a: a.reshape(-1, shp[-1])
            g_, d_, m_, v_ = adamw(flat(w_in[n]), flat(m_in[n]), flat(v_in[n]), [p_mine, p_other], name=f"adamw_{n}")
            outs_g[n], outs_d[n], outs_m[n], outs_v[n] = (a.reshape(shp) for a in (g_, d_, m_, v_))
            after = v_
        return after

    small_names = [n for n in names if n not in big]
    small_shapes_full = {}
    small_grads_full = []
    for n in small_names:
        if n == "kv_norm_g":
            gfull = dg_kv.reshape(-1)
        elif n in ("cm_dw", "ffn_dw"):
            gfull = jnp.stack(gsm[n], axis=0)
        else:
            gfull = jnp.stack([a.reshape(-1) for a in gsm[n]], axis=0)
        small_shapes_full[n] = gfull.shape
        small_grads_full.append(gfull)
    dev_arr = (4 * lax.axis_index("x") + 2 * lax.axis_index("y") + lax.axis_index("c")).astype(jnp.int32).reshape(1)
    small_land = cast_place(_pack(small_grads_full) + last_token[0, 0], None, dev_arr, F32, name="place_small_grads",
                            nslots=N_DEV)
    sm_send, sm_recv, small_land, small_token = small_start(small_land)

    for k in range(len(in_flight) - 1):
        finish_scatter(k, small_token)
    done = update(["w_kv", "w_q", "w_o", "ffn_w_in", "ffn_w_out"], "a", small_token)
    finish_scatter(len(in_flight) - 1, done)
    done = update(["cm_w_in", "cm_w_out"], "b", done)

    summed = sum_slots(small_wait(sm_send, sm_recv, small_land, done))
    g_full = dict(zip(small_names, _unpack(summed, [small_shapes_full[n] for n in small_names])))
    g_loc = {}
    for n in small_names:
        if n in small_sharded:
            width = w_in[n].shape[-1]
            g_loc[n] = lax.dynamic_slice_in_dim(g_full[n], me_chip * width, width, axis=g_full[n].ndim - 1)
        else:
            g_loc[n] = g_full[n]
    res = adamw(_pack([w_in[n] for n in small_names]), _pack([m_in[n] for n in small_names]),
                _pack([v_in[n] for n in small_names]), [_pack([g_loc[n] for n in small_names])], name="adamw_small")
    shapes_loc = [w_in[n].shape for n in small_names]
    for dst, packed in zip((outs_g, outs_d, outs_m, outs_v), res):
        for n, a in zip(small_names, _unpack(packed, shapes_loc)):
            dst[n] = a

    return (loss, grad_x, *[outs_g[n] for n in names], *[outs_d[n] for n in names],
            *[outs_m[n] for n in names], *[outs_v[n] for n in names])
```

```python
import functools
import math

import jax
import jax.numpy as jnp
from jax import lax
from jax.experimental import pallas as pl
from jax.experimental.pallas import tpu as pltpu

F32 = jnp.float32
BF16 = jnp.bfloat16
EPS = 1e-6
NEG_INF = -1e30
N_A = 2
DEPTH = 4
N_GROUPS = 3
DILATIONS = (1, 4, 16)
HEAD_DIM = 128
BLK = 128
LANES = 128
N_CHIPS = 4
N_DEV = 8
VMEM_LIMIT_V7X = 56 * 1024 * 1024

ADAM_LR = 0.001
ADAM_B1 = 0.9
ADAM_B2 = 0.999
ADAM_EPS = 1e-08
ADAM_WD = 0.01
ADAM_STEP = 10

MESH = pl.DeviceIdType.MESH


def _cp(*sem, **kw):
    return pltpu.CompilerParams(dimension_semantics=sem if sem else None, vmem_limit_bytes=VMEM_LIMIT_V7X, **kw)


def _dot(a, b):
    return jnp.dot(a, b, preferred_element_type=F32)


def _dot_nt(a, b):
    return lax.dot_general(a, b, (((1,), (1,)), ((), ())), preferred_element_type=F32)


def _dot_tn(a, b):
    return lax.dot_general(a, b, (((0,), (0,)), ((), ())), preferred_element_type=F32)


def _sigmoid(x):
    return 1.0 / (1.0 + jnp.exp(-x))


def _row_tile(n, want):
    if n <= want:
        return n
    for t in range(want - want % 8, 7, -8):
        if n % t == 0:
            return t
    raise ValueError(f"no row tile for {n} rows")


def mm_nn(a, w, nsh, stride, layer, bias=None, out_dtype=F32, name="mm_nn", tm=1024, head_major=False):
    m, k = a.shape
    _, k2, ns = w.shape
    assert k == k2
    tm = _row_tile(m, tm)
    has_bias = bias is not None
    hps = ns // HEAD_DIM

    def body(*refs):
        if has_bias:
            a_ref, w_ref, b_ref, o_ref = refs
        else:
            a_ref, w_ref, o_ref = refs
        acc = _dot(a_ref[...].astype(BF16), w_ref[...])
        if has_bias:
            acc = acc + b_ref[...]
        if head_major:
            for hh in range(hps):
                o_ref[hh] = acc[:, hh * HEAD_DIM:(hh + 1) * HEAD_DIM].astype(out_dtype)
        else:
            o_ref[...] = acc.astype(out_dtype)

    in_specs = [
        pl.BlockSpec((tm, k), lambda j, i: (i, 0)),
        pl.BlockSpec((None, k, ns), lambda j, i: (j * stride + layer, 0, 0)),
    ]
    args = [a, w]
    if has_bias:
        in_specs.append(pl.BlockSpec((1, ns), lambda j, i: (0, j)))
        args.append(bias)
    return pl.pallas_call(
        body,
        name=name,
        grid=(nsh, m // tm),
        in_specs=in_specs,
        out_specs=(pl.BlockSpec((hps, tm, HEAD_DIM), lambda j, i: (j, i, 0)) if head_major
                   else pl.BlockSpec((tm, ns), lambda j, i: (i, j))),
        out_shape=jax.ShapeDtypeStruct((nsh * hps, m, HEAD_DIM) if head_major else (m, nsh * ns), out_dtype),
        compiler_params=_cp("parallel", "parallel"),
    )(*args)


def mm_nt(dy, w, nsh, stride, layer, out_dtype=F32, name="mm_nt", tm=1024):
    dys = list(dy) if isinstance(dy, (list, tuple)) else [dy]
    npart = len(dys)
    per = nsh // npart
    m = dys[0].shape[0]
    _, k, ns = w.shape
    assert all(d.shape == (m, per * ns) for d in dys)
    tm = _row_tile(m, tm)

    def body(*refs):
        dy_refs = refs[:npart]
        w_ref, o_ref, acc_ref = refs[npart:]
        j = pl.program_id(1)

        @pl.when(j == 0)
        def _():
            acc_ref[...] = jnp.zeros(acc_ref.shape, F32)

        for pi in range(npart):
            @pl.when(j // per == pi)
            def _(pi=pi):
                acc_ref[...] += _dot_nt(dy_refs[pi][...].astype(BF16), w_ref[...])

        @pl.when(j == nsh - 1)
        def _():
            o_ref[...] = acc_ref[...].astype(out_dtype)

    dy_specs = [pl.BlockSpec((tm, ns), lambda i, j, pi=pi: (i, jnp.clip(j - pi * per, 0, per - 1))) for pi in range(npart)]
    return pl.pallas_call(
        body,
        name=name,
        grid=(m // tm, nsh),
        in_specs=[*dy_specs, pl.BlockSpec((None, k, ns), lambda i, j: (j * stride + layer, 0, 0))],
        out_specs=pl.BlockSpec((tm, k), lambda i, j: (i, 0)),
        out_shape=jax.ShapeDtypeStruct((m, k), out_dtype),
        scratch_shapes=[pltpu.VMEM((tm, k), F32)],
        compiler_params=_cp("parallel", "arbitrary"),
    )(*dys, w)


def mm_tn(a, dy, nsh, name="mm_tn", tm=1024):
    dys = list(dy) if isinstance(dy, (list, tuple)) else [dy]
    npart = len(dys)
    per = nsh // npart
    m, k = a.shape
    ns = dys[0].shape[1] // per
    assert all(d.shape == (m, per * ns) for d in dys)
    tm = _row_tile(m, tm)
    nt = m // tm

    def body(*refs):
        a_ref = refs[0]
        dy_refs = refs[1:1 + npart]
        o_ref, acc_ref = refs[1 + npart:]
        j = pl.program_id(0)
        i = pl.program_id(1)

        @pl.when(i == 0)
        def _():
            acc_ref[...] = jnp.zeros(acc_ref.shape, F32)

        for pi in range(npart):
            @pl.when(j // per == pi)
            def _(pi=pi):
                acc_ref[...] += _dot_tn(a_ref[...].astype(BF16), dy_refs[pi][...].astype(BF16))

        @pl.when(i == nt - 1)
        def _():
            o_ref[...] = acc_ref[...].astype(BF16)

    dy_specs = [
        pl.BlockSpec((tm, ns), lambda j, i, pi=pi: (jnp.where(j // per == pi, i, 0), jnp.clip(j - pi * per, 0, per - 1)))
        for pi in range(npart)
    ]
    return pl.pallas_call(
        body,
        name=name,
        grid=(nsh, nt),
        in_specs=[pl.BlockSpec((tm, k), lambda j, i: (i, 0)), *dy_specs],
        out_specs=pl.BlockSpec((None, k, ns), lambda j, i: (j, 0, 0)),
        out_shape=jax.ShapeDtypeStruct((nsh, k, ns), BF16),
        scratch_shapes=[pltpu.VMEM((k, ns), F32)],
        compiler_params=_cp("parallel", "arbitrary"),
    )(a, *dys)


DEP_SPEC_SHAPE = (8, LANES)


def resid_norm_fwd(x, y, g_post, next_gains, name, tm=1024, dep=None):
    t, d = x.shape
    tm = _row_tile(t, tm)
    has_y = y is not None
    n_next = len(next_gains)
    n_dep = 0 if dep is None else 1

    def body(*refs):
        x_ref = refs[0]
        pos = 1
        if has_y:
            y_ref, gp_ref = refs[1], refs[2]
            pos = 3
        gn_refs = refs[pos:pos + n_next]
        outs = refs[pos + n_next + n_dep:]
        xv = x_ref[...]
        o = 0
        if has_y:
            yv = y_ref[...].astype(F32)
            r = lax.rsqrt(jnp.mean(yv * yv, axis=-1, keepdims=True) + EPS)
            xv = xv + (yv * r) * gp_ref[...]
            outs[0][...] = xv
            o = 1
        if n_next:
            xn = xv * lax.rsqrt(jnp.mean(xv * xv, axis=-1, keepdims=True) + EPS)
            for k in range(n_next):
                outs[o + k][...] = (xn * gn_refs[k][...]).astype(BF16)

    row = pl.BlockSpec((tm, d), lambda i: (i, 0))
    vec = pl.BlockSpec((1, d), lambda i: (0, 0))
    args, in_specs = [x], [row]
    if has_y:
        args += [y, g_post]
        in_specs += [row, vec]
    args += list(next_gains)
    in_specs += [vec] * n_next
    if n_dep:
        args.append(dep)
        in_specs.append(pl.BlockSpec(DEP_SPEC_SHAPE, lambda i: (0, 0)))
    out_shape, out_specs = [], []
    if has_y:
        out_shape.append(jax.ShapeDtypeStruct((t, d), F32))
        out_specs.append(row)
    for _ in range(n_next):
        out_shape.append(jax.ShapeDtypeStruct((t, d), BF16))
        out_specs.append(row)
    return pl.pallas_call(
        body, name=name, grid=(t // tm,), in_specs=in_specs, out_specs=out_specs, out_shape=out_shape,
        compiler_params=_cp("parallel"),
    )(*args)


def norm_bwd(x, g, dy, add=None, out_dtype=F32, name="norm_bwd", tm=1024, dep=None):
    t, d = x.shape
    tm = _row_tile(t, tm)
    has_add = add is not None

    def body(*refs):
        x_ref, g_ref, dy_ref = refs[:3]
        add_ref = refs[3] if has_add else None
        dx_ref, dg_ref, cs_ref = refs[-3:]
        i = pl.program_id(0)
        xv = x_ref[...].astype(F32)
        dyv = dy_ref[...].astype(F32)
        r = lax.rsqrt(jnp.mean(xv * xv, axis=-1, keepdims=True) + EPS)
        gd = dyv * g_ref[...]
        dx = r * gd - xv * ((r * r * r) * jnp.mean(xv * gd, axis=-1, keepdims=True))
        if has_add:
            dx = dx + add_ref[...]
        dx_ref[...] = dx.astype(out_dtype)
        dg = jnp.sum(dyv * (xv * r), axis=0, keepdims=True)
        cs = jnp.sum(dx, axis=0, keepdims=True)

        @pl.when(i == 0)
        def _():
            dg_ref[...] = dg
            cs_ref[...] = cs

        @pl.when(i > 0)
        def _():
            dg_ref[...] += dg
            cs_ref[...] += cs

    row = pl.BlockSpec((tm, d), lambda i: (i, 0))
    vec = pl.BlockSpec((1, d), lambda i: (0, 0))
    args, in_specs = [x, g, dy], [row, vec, row]
    if has_add:
        args.append(add)
        in_specs.append(row)
    if dep is not None:
        args.append(dep)
        in_specs.append(pl.BlockSpec(DEP_SPEC_SHAPE, lambda i: (0, 0)))
    return pl.pallas_call(
        body, name=name, grid=(t // tm,), in_specs=in_specs,
        out_specs=[row, vec, vec],
        out_shape=[jax.ShapeDtypeStruct((t, d), out_dtype), jax.ShapeDtypeStruct((1, d), F32),
                   jax.ShapeDtypeStruct((1, d), F32)],
        compiler_params=_cp("arbitrary"),
    )(*args)


def loss_fwd_bwd(x, target, name="loss", tm=1024):
    t, d = x.shape
    tm = _row_tile(t, tm)

    def body(x_ref, t_ref, dx_ref, l_ref):
        i = pl.program_id(0)
        err = x_ref[...] - t_ref[...]
        dx_ref[...] = err * (1.0 / d)
        part = 0.5 * jnp.sum(jnp.mean(err * err, axis=-1, keepdims=True), axis=0, keepdims=True)
        part = jnp.broadcast_to(part, l_ref.shape)

        @pl.when(i == 0)
        def _():
            l_ref[...] = part

        @pl.when(i > 0)
        def _():
            l_ref[...] += part

    row = pl.BlockSpec((tm, d), lambda i: (i, 0))
    return pl.pallas_call(
        body, name=name, grid=(t // tm,), in_specs=[row, row],
        out_specs=[row, pl.BlockSpec((8, LANES), lambda i: (0, 0))],
        out_shape=[jax.ShapeDtypeStruct((t, d), F32), jax.ShapeDtypeStruct((8, LANES), F32)],
        compiler_params=_cp("arbitrary"),
    )(x, target)


CONV_HALO = 32
CONV_CHUNK = 128


def glu_conv_fwd(z, dw, dwb, name, tc=128):
    b, s, c2 = z.shape
    c = c2 // 2
    kw = dw.shape[0]
    tc = min(tc, c)
    nc = c // tc
    ch = min(CONV_CHUNK, s)
    halo = CONV_HALO
    assert kw - 1 <= halo and s % ch == 0

    nch = s // ch

    def body(a_ref, g_ref, w_ref, b_ref, o_ref, pad_ref):
        _fill_glu_slabs(a_ref, g_ref, pad_ref, nch, ch, halo)

        def chunk(ci, carry):
            r0 = pl.multiple_of(ci * ch, ch)
            acc = b_ref[...]
            for k, tap in enumerate(_taps_front(pad_ref, ci, kw, ch, halo)):
                acc = acc + w_ref[k:k + 1, :] * tap
            o_ref[pl.ds(r0, ch), :] = acc
            return carry

        lax.fori_loop(0, nch, chunk, 0)

    return pl.pallas_call(
        body, name=name, grid=(b, nc),
        in_specs=[
            pl.BlockSpec((None, s, tc), lambda bi, i: (bi, 0, i)),
            pl.BlockSpec((None, s, tc), lambda bi, i: (bi, 0, i + nc)),
            pl.BlockSpec((kw, tc), lambda bi, i: (0, i)),
            pl.BlockSpec((1, tc), lambda bi, i: (0, i)),
        ],
        out_specs=pl.BlockSpec((None, s, tc), lambda bi, i: (bi, 0, i)),
        out_shape=jax.ShapeDtypeStruct((b, s, c), F32),
        scratch_shapes=[pltpu.VMEM((nch, ch + halo, tc), F32)],
        compiler_params=_cp("parallel", "parallel"),
    )(z, z, dw, dwb)


def glu_conv_bwd(z, dw, du2, name, tc=128):
    b, s, c2 = z.shape
    c = c2 // 2
    kw = dw.shape[0]
    tc = min(tc, c)
    nc = c // tc
    ch = min(CONV_CHUNK, s)
    nch = s // ch
    halo = CONV_HALO

    def body(a_ref, g_ref, w_ref, du_ref, dza_ref, dzg_ref, ddw_ref, ddwb_ref, dba_ref, dbg_ref, upad_ref, dpad_ref):
        bi = pl.program_id(1)

        @pl.when(bi == 0)
        def _():
            ddw_ref[...] = jnp.zeros(ddw_ref.shape, F32)
            ddwb_ref[...] = jnp.zeros(ddwb_ref.shape, F32)
            dba_ref[...] = jnp.zeros(dba_ref.shape, F32)
            dbg_ref[...] = jnp.zeros(dbg_ref.shape, F32)

        _fill_glu_slabs(a_ref, g_ref, upad_ref, nch, ch, halo)
        dpad_ref[nch - 1, ch:ch + halo, :] = jnp.zeros((halo, tc), F32)
        dpad_ref[nch - 1, 0:ch, :] = du_ref[s - ch:s, :]

        def fill(ci, carry):
            r0 = pl.multiple_of(ci * ch, ch)
            dpad_ref[ci, :, :] = du_ref[pl.ds(r0, ch + halo), :]
            return carry

        lax.fori_loop(0, nch - 1, fill, 0)

        def chunk(ci, carry):
            r0 = pl.multiple_of(ci * ch, ch)
            du_c = du_ref[pl.ds(r0, ch), :]
            taps_u = _taps_front(upad_ref, ci, kw, ch, halo)
            taps_d = _taps_at(dpad_ref, ci, list(range(kw)), ch)
            du1 = w_ref[kw - 1:kw, :] * du_c
            ddw_ref[kw - 1] += jnp.sum((du_c * taps_u[kw - 1]).reshape(ch // 8, 8, tc), axis=0)
            for j in range(1, kw):
                du1 = du1 + w_ref[kw - 1 - j:kw - j, :] * taps_d[j]
                ddw_ref[kw - 1 - j] += jnp.sum((du_c * taps_u[kw - 1 - j]).reshape(ch // 8, 8, tc), axis=0)
            av = a_ref[pl.ds(r0, ch), :]
            sg = _sigmoid(g_ref[pl.ds(r0, ch), :])
            dza = du1 * sg
            dzg = du1 * av * (sg * (1.0 - sg))
            dza_ref[pl.ds(r0, ch), :] = dza.astype(BF16)
            dzg_ref[pl.ds(r0, ch), :] = dzg.astype(BF16)
            dba_ref[...] += jnp.sum(dza, axis=0, keepdims=True)
            dbg_ref[...] += jnp.sum(dzg, axis=0, keepdims=True)
            ddwb_ref[...] += jnp.sum(du_c, axis=0, keepdims=True)
            return carry

        lax.fori_loop(0, s // ch, chunk, 0)

    blk = lambda off: pl.BlockSpec((None, s, tc), lambda i, bi: (bi, 0, i + off))
    vec = pl.BlockSpec((1, tc), lambda i, bi: (0, i))
    return pl.pallas_call(
        body, name=name, grid=(nc, b),
        in_specs=[blk(0), blk(nc), pl.BlockSpec((kw, tc), lambda i, bi: (0, i)), blk(0)],
        out_specs=[blk(0), blk(0), pl.BlockSpec((kw, 8, tc), lambda i, bi: (0, 0, i)), vec, vec, vec],
        out_shape=[
            jax.ShapeDtypeStruct((b, s, c), BF16), jax.ShapeDtypeStruct((b, s, c), BF16),
            jax.ShapeDtypeStruct((kw, 8, c), F32), jax.ShapeDtypeStruct((1, c), F32),
            jax.ShapeDtypeStruct((1, c), F32), jax.ShapeDtypeStruct((1, c), F32),
        ],
        scratch_shapes=[pltpu.VMEM((nch, ch + halo, tc), F32), pltpu.VMEM((nch, ch + halo, tc), F32)],
        compiler_params=_cp("parallel", "arbitrary"),
    )(z, z, dw, du2)


def _fill_glu_slabs(a_ref, g_ref, pad_ref, nch, ch, halo):
    tc = a_ref.shape[-1]
    pad_ref[0, 0:halo, :] = jnp.zeros((halo, tc), F32)
    pad_ref[0, halo:halo + ch, :] = a_ref[0:ch, :] * _sigmoid(g_ref[0:ch, :])

    def fill(ci, carry):
        r0 = pl.multiple_of(ci * ch, ch)
        pad_ref[ci, 0:halo, :] = pad_ref[ci - 1, ch:ch + halo, :]
        pad_ref[ci, halo:halo + ch, :] = a_ref[pl.ds(r0, ch), :] * _sigmoid(g_ref[pl.ds(r0, ch), :])
        return carry

    lax.fori_loop(1, nch, fill, 0)


def ln_silu_fwd(u, g, bvec, name, tm=1024):
    t, d = u.shape
    tm = _row_tile(t, tm)

    def body(u_ref, g_ref, b_ref, o_ref):
        uv = u_ref[...]
        mu = jnp.mean(uv, axis=-1, keepdims=True)
        xc = uv - mu
        var = jnp.mean(xc * xc, axis=-1, keepdims=True)
        v = (xc * lax.rsqrt(var + EPS)) * g_ref[...] + b_ref[...]
        o_ref[...] = (v * _sigmoid(v)).astype(BF16)

    row = pl.BlockSpec((tm, d), lambda i: (i, 0))
    vec = pl.BlockSpec((1, d), lambda i: (0, 0))
    return pl.pallas_call(
        body, name=name, grid=(t // tm,), in_specs=[row, vec, vec], out_specs=row,
        out_shape=jax.ShapeDtypeStruct((t, d), BF16), compiler_params=_cp("parallel"),
    )(u, g, bvec)


def ln_silu_bwd(u, g, bvec, dout, name, tm=1024):
    t, d = u.shape
    tm = _row_tile(t, tm)

    def body(u_ref, g_ref, b_ref, do_ref, du_ref, dg_ref, db_ref):
        i = pl.program_id(0)
        uv = u_ref[...]
        mu = jnp.mean(uv, axis=-1, keepdims=True)
        xc = uv - mu
        var = jnp.mean(xc * xc, axis=-1, keepdims=True)
        rstd = lax.rsqrt(var + EPS)
        n = xc * rstd
        v = n * g_ref[...] + b_ref[...]
        sg = _sigmoid(v)
        dv = do_ref[...].astype(F32) * (sg * (1.0 + v * (1.0 - sg)))
        dn = dv * g_ref[...]
        du_ref[...] = rstd * (dn - jnp.mean(dn, axis=-1, keepdims=True) - n * jnp.mean(dn * n, axis=-1, keepdims=True))
        dg = jnp.sum(dv * n, axis=0, keepdims=True)
        db = jnp.sum(dv, axis=0, keepdims=True)

        @pl.when(i == 0)
        def _():
            dg_ref[...] = dg
            db_ref[...] = db

        @pl.when(i > 0)
        def _():
            dg_ref[...] += dg
            db_ref[...] += db

    row = pl.BlockSpec((tm, d), lambda i: (i, 0))
    vec = pl.BlockSpec((1, d), lambda i: (0, 0))
    return pl.pallas_call(
        body, name=name, grid=(t // tm,), in_specs=[row, vec, vec, row], out_specs=[row, vec, vec],
        out_shape=[jax.ShapeDtypeStruct((t, d), F32), jax.ShapeDtypeStruct((1, d), F32), jax.ShapeDtypeStruct((1, d), F32)],
        compiler_params=_cp("arbitrary"),
    )(u, g, bvec, dout)


FFN_HALO = 8


def _fill_front_halo(src_ref, pad_ref, nch, ch, halo):
    tc = src_ref.shape[-1]
    pad_ref[0, 0:halo, :] = jnp.zeros((halo, tc), F32)
    pad_ref[0, halo:halo + ch, :] = src_ref[0:ch, :].astype(F32)

    def fill(ci, carry):
        r0 = pl.multiple_of(ci * ch, ch)
        pad_ref[ci, 0:halo, :] = src_ref[pl.ds(r0 - 2 * halo, 2 * halo), :].astype(F32)[halo:, :]
        pad_ref[ci, halo:halo + ch, :] = src_ref[pl.ds(r0, ch), :].astype(F32)
        return carry

    lax.fori_loop(1, nch, fill, 0)


def _taps_at(pad_ref, ci, offsets, ch):
    windows = {}
    for b in sorted({o % 8 for o in offsets}):
        top = max(o for o in offsets if o % 8 == b)
        windows[b] = pad_ref[ci, b:top + ch, :]
    return [windows[o % 8][o - o % 8:o - o % 8 + ch, :] for o in offsets]


def _taps_front(pad_ref, ci, kw, ch, halo):
    return _taps_at(pad_ref, ci, [halo - (kw - 1 - k) for k in range(kw)], ch)


def ffn_mid_fwd(p, dw, dwb, name, tc=256):
    b, s, f2 = p.shape
    f = f2 // 2
    kw = dw.shape[0]
    tc = min(tc, f)
    nf = f // tc
    ch = min(CONV_CHUNK, s)
    nch = s // ch
    halo = FFN_HALO

    def body(pa_ref, pg_ref, wa_ref, wg_ref, ba_ref, bg_ref, o_ref, apad_ref, gpad_ref):
        _fill_front_halo(pa_ref, apad_ref, nch, ch, halo)
        _fill_front_halo(pg_ref, gpad_ref, nch, ch, halo)

        def chunk(ci, carry):
            r0 = pl.multiple_of(ci * ch, ch)
            ca = ba_ref[...]
            cg = bg_ref[...]
            taps = zip(_taps_front(apad_ref, ci, kw, ch, halo), _taps_front(gpad_ref, ci, kw, ch, halo))
            for k, (ta, tg) in enumerate(taps):
                ca = ca + wa_ref[k:k + 1, :] * ta
                cg = cg + wg_ref[k:k + 1, :] * tg
            o_ref[pl.ds(r0, ch), :] = ((cg * _sigmoid(cg)) * ca).astype(BF16)
            return carry

        lax.fori_loop(0, nch, chunk, 0)

    blk = lambda off: pl.BlockSpec((None, s, tc), lambda bi, i: (bi, 0, i + off))
    wsp = lambda off: pl.BlockSpec((kw, tc), lambda bi, i: (0, i + off))
    bsp = lambda off: pl.BlockSpec((1, tc), lambda bi, i: (0, i + off))
    return pl.pallas_call(
        body, name=name, grid=(b, nf),
        in_specs=[blk(0), blk(nf), wsp(0), wsp(nf), bsp(0), bsp(nf)],
        out_specs=pl.BlockSpec((None, s, tc), lambda bi, i: (bi, 0, i)),
        out_shape=jax.ShapeDtypeStruct((b, s, f), BF16),
        scratch_shapes=[pltpu.VMEM((nch, ch + halo, tc), F32)] * 2,
        compiler_params=_cp("parallel", "parallel"),
    )(p, p, dw, dw, dwb, dwb)


def ffn_mid_bwd(p, dw, dwb, ds, name, tc=256):
    b, s, f2 = p.shape
    f = f2 // 2
    kw = dw.shape[0]
    tc = min(tc, f)
    nf = f // tc
    ch = min(CONV_CHUNK, s)
    nch = s // ch
    halo = FFN_HALO

    def sum8(v):
        return jnp.sum(v.reshape(ch // 8, 8, tc), axis=0)

    def body(pa_ref, pg_ref, wa_ref, wg_ref, ba_ref, bg_ref, ds_ref, dpa_ref, dpg_ref, ddwa_ref, ddwg_ref, dba_ref, dbg_ref,
             apad_ref, gpad_ref, dca_ref, dcg_ref):
        bi = pl.program_id(1)

        @pl.when(bi == 0)
        def _():
            ddwa_ref[...] = jnp.zeros(ddwa_ref.shape, F32)
            ddwg_ref[...] = jnp.zeros(ddwg_ref.shape, F32)
            dba_ref[...] = jnp.zeros(dba_ref.shape, F32)
            dbg_ref[...] = jnp.zeros(dbg_ref.shape, F32)

        _fill_front_halo(pa_ref, apad_ref, nch, ch, halo)
        _fill_front_halo(pg_ref, gpad_ref, nch, ch, halo)
        dca_ref[nch - 1, ch:ch + halo, :] = jnp.zeros((halo, tc), F32)
        dcg_ref[nch - 1, ch:ch + halo, :] = jnp.zeros((halo, tc), F32)

        def grads(ci, carry):
            acc_a, acc_g, sb_a, sb_g = carry
            r0 = pl.multiple_of(ci * ch, ch)
            taps_a = _taps_front(apad_ref, ci, kw, ch, halo)
            taps_g = _taps_front(gpad_ref, ci, kw, ch, halo)
            ca = ba_ref[...]
            cg = bg_ref[...]
            for k in range(kw):
                ca = ca + wa_ref[k:k + 1, :] * taps_a[k]
                cg = cg + wg_ref[k:k + 1, :] * taps_g[k]
            sg = _sigmoid(cg)
            dsv = ds_ref[pl.ds(r0, ch), :].astype(F32)
            dca = dsv * (cg * sg)
            dcg = dsv * ca * (sg * (1.0 + cg * (1.0 - sg)))
            dca_ref[ci, 0:ch, :] = dca
            dcg_ref[ci, 0:ch, :] = dcg

            prev = jnp.maximum(ci - 1, 0)

            @pl.when(ci > 0)
            def _():
                dca_ref[prev, ch:ch + halo, :] = dca[0:halo, :]
                dcg_ref[prev, ch:ch + halo, :] = dcg[0:halo, :]

            acc_a = tuple(acc_a[k] + sum8(dca * taps_a[k]) for k in range(kw))
            acc_g = tuple(acc_g[k] + sum8(dcg * taps_g[k]) for k in range(kw))
            return acc_a, acc_g, sb_a + sum8(dca), sb_g + sum8(dcg)

        z8 = jnp.zeros((8, tc), F32)
        acc_a, acc_g, sb_a, sb_g = lax.fori_loop(0, nch, grads, ((z8,) * kw, (z8,) * kw, z8, z8))
        for k in range(kw):
            ddwa_ref[k] += acc_a[k]
            ddwg_ref[k] += acc_g[k]
        dba_ref[...] += jnp.sum(sb_a, axis=0, keepdims=True)
        dbg_ref[...] += jnp.sum(sb_g, axis=0, keepdims=True)

        def back(ci, carry):
            r0 = pl.multiple_of(ci * ch, ch)
            da = wa_ref[kw - 1:kw, :] * dca_ref[ci, 0:ch, :]
            dg = wg_ref[kw - 1:kw, :] * dcg_ref[ci, 0:ch, :]
            for j in range(1, kw):
                da = da + wa_ref[kw - 1 - j:kw - j, :] * dca_ref[ci, j:j + ch, :]
                dg = dg + wg_ref[kw - 1 - j:kw - j, :] * dcg_ref[ci, j:j + ch, :]
            dpa_ref[pl.ds(r0, ch), :] = da.astype(BF16)
            dpg_ref[pl.ds(r0, ch), :] = dg.astype(BF16)
            return carry

        lax.fori_loop(0, nch, back, 0)

    blk = lambda off: pl.BlockSpec((None, s, tc), lambda i, bi: (bi, 0, i + off))
    wsp = lambda off: pl.BlockSpec((kw, tc), lambda i, bi: (0, i + off))
    bsp = lambda off: pl.BlockSpec((1, tc), lambda i, bi: (0, i + off))
    acc3 = pl.BlockSpec((kw, 8, tc), lambda i, bi: (0, 0, i))
    vec = pl.BlockSpec((1, tc), lambda i, bi: (0, i))
    return pl.pallas_call(
        body, name=name, grid=(nf, b),
        in_specs=[blk(0), blk(nf), wsp(0), wsp(nf), bsp(0), bsp(nf), blk(0)],
        out_specs=[blk(0), blk(0), acc3, acc3, vec, vec],
        out_shape=[jax.ShapeDtypeStruct((b, s, f), BF16), jax.ShapeDtypeStruct((b, s, f), BF16),
                   jax.ShapeDtypeStruct((kw, 8, f), F32), jax.ShapeDtypeStruct((kw, 8, f), F32),
                   jax.ShapeDtypeStruct((1, f), F32), jax.ShapeDtypeStruct((1, f), F32)],
        scratch_shapes=[pltpu.VMEM((nch, ch + halo, tc), F32)] * 4,
        compiler_params=_cp("parallel", "arbitrary"),
    )(p, p, dw, dw, dwb, dwb, ds)


def _tile_rows(r, n, dil):
    start = r + n * BLK * dil
    return pl.ds(start, BLK, stride=dil) if dil > 1 else pl.ds(start, BLK)


def _band_masks():
    qi = lax.broadcasted_iota(jnp.int32, (BLK, 2 * BLK), 0)
    kk = lax.broadcasted_iota(jnp.int32, (BLK, 2 * BLK), 1)
    both = jnp.logical_or(jnp.logical_and(kk < BLK, kk >= qi), jnp.logical_and(kk >= BLK, kk - BLK <= qi))
    return both, kk[:, :BLK] <= qi[:, :BLK]


def attn_fwd(q, kv, g, dil, hw, name):
    _, b, s, _ = q.shape
    nh = hw // HEAD_DIM
    nblk = s // dil // BLK
    scale = 1.0 / math.sqrt(HEAD_DIM)

    def body(q_ref, k_ref, v_ref, o_ref, lse_ref):
        h = pl.program_id(1)
        mask2, mask1 = _band_masks()
        mine = lax.broadcasted_iota(jnp.int32, (BLK, LANES), 1) == h

        @pl.when(h == 0)
        def _():
            lse_ref[...] = jnp.zeros(lse_ref.shape, F32)

        for r in range(dil):
            kp = vp = None
            for n in range(nblk):
                rs = _tile_rows(r, n, dil)
                qt = q_ref[rs, :].astype(BF16)
                kc = k_ref[rs, :].astype(BF16)
                vc = v_ref[rs, :].astype(BF16)
                if n == 0:
                    kcat, vcat, mask = kc, vc, mask1
                else:
                    kcat, vcat, mask = jnp.concatenate([kp, kc], axis=0), jnp.concatenate([vp, vc], axis=0), mask2
                sc = jnp.where(mask, _dot_nt(qt, kcat) * scale, NEG_INF)
                m = jnp.max(sc, axis=-1, keepdims=True)
                p = jnp.exp(sc - m)
                den = jnp.sum(p, axis=-1, keepdims=True)
                o_ref[rs, :] = _dot(p.astype(BF16), vcat) / den
                lse_ref[rs, :] = jnp.where(mine, m + jnp.log(den), lse_ref[rs, :])
                kp, vp = kc, vc

    col = lambda base: pl.BlockSpec((None, s, HEAD_DIM), lambda bi, h: (bi, 0, base + h))
    head = lambda base: pl.BlockSpec((None, None, s, HEAD_DIM), lambda bi, h: (base + h, bi, 0, 0))
    return pl.pallas_call(
        body, name=name, grid=(b, nh),
        in_specs=[head(g * nh), head(g * nh), head((N_GROUPS + g) * nh)],
        out_specs=[col(0), pl.BlockSpec((None, s, LANES), lambda bi, h: (bi, 0, 0))],
        out_shape=[jax.ShapeDtypeStruct((b, s, hw), F32), jax.ShapeDtypeStruct((b, s, LANES), F32)],
        compiler_params=_cp("parallel", "arbitrary"),
    )(q, kv, kv)


def attn_merge(outs, lses, name, tm=512):
    t, hw = outs[0].shape
    nh = hw // HEAD_DIM
    tm = _row_tile(t, tm)
    ng = len(outs)

    def body(*refs):
        o_refs, l_refs = refs[:ng], refs[ng:2 * ng]
        m_ref, lj_ref = refs[2 * ng:]
        ls = [l_refs[g][...] for g in range(ng)]
        mx = ls[0]
        for g in range(1, ng):
            mx = jnp.maximum(mx, ls[g])
        es = [jnp.exp(l - mx) for l in ls]
        tot = es[0]
        for g in range(1, ng):
            tot = tot + es[g]
        ws = [e / tot for e in es]
        lj_ref[...] = mx + jnp.log(tot)
        for h in range(nh):
            sl = slice(h * HEAD_DIM, (h + 1) * HEAD_DIM)
            acc = ws[0][:, h:h + 1] * o_refs[0][:, sl]
            for g in range(1, ng):
                acc = acc + ws[g][:, h:h + 1] * o_refs[g][:, sl]
            m_ref[:, sl] = acc.astype(BF16)

    row = pl.BlockSpec((tm, hw), lambda i: (i, 0))
    st = pl.BlockSpec((tm, LANES), lambda i: (i, 0))
    return pl.pallas_call(
        body, name=name, grid=(t // tm,), in_specs=[row] * ng + [st] * ng, out_specs=[row, st],
        out_shape=[jax.ShapeDtypeStruct((t, hw), BF16), jax.ShapeDtypeStruct((t, LANES), F32)],
        compiler_params=_cp("parallel"),
    )(*outs, *lses)


def attn_bwd_prep(dmerged, merged, name, tm=512):
    t, hw = merged.shape
    nh = hw // HEAD_DIM
    tm = _row_tile(t, tm)

    def body(d_ref, m_ref, o_ref):
        lane = lax.broadcasted_iota(jnp.int32, (tm, LANES), 1)
        acc = jnp.zeros((tm, LANES), F32)
        for h in range(nh):
            sl = slice(h * HEAD_DIM, (h + 1) * HEAD_DIM)
            dsum = jnp.sum(d_ref[:, sl] * m_ref[:, sl].astype(F32), axis=-1, keepdims=True)
            acc = jnp.where(lane == h, dsum, acc)
        o_ref[...] = acc

    row = pl.BlockSpec((tm, hw), lambda i: (i, 0))
    return pl.pallas_call(
        body, name=name, grid=(t // tm,), in_specs=[row, row], out_specs=pl.BlockSpec((tm, LANES), lambda i: (i, 0)),
        out_shape=jax.ShapeDtypeStruct((t, LANES), F32), compiler_params=_cp("parallel"),
    )(dmerged, merged)


def attn_bwd(q, kv, g, dil, do, lsej, dm, dq_buf, dk_buf, dv_buf, accumulate, hw, name):
    _, b, s, _ = q.shape
    nh = hw // HEAD_DIM
    nblk = s // dil // BLK
    scale = 1.0 / math.sqrt(HEAD_DIM)
    assert dk_buf is not None or not accumulate
    kv_at = 6 + (dq_buf is not None)

    def body(*refs):
        q_ref, k_ref, v_ref, do_ref, lj_ref, dm_ref = refs[:6]
        dq_ref, dk_ref, dv_ref = refs[-3:]
        dki_ref, dvi_ref = (refs[kv_at], refs[kv_at + 1]) if accumulate else (None, None)
        mask2, mask1 = _band_masks()
        mine = lax.broadcasted_iota(jnp.int32, (BLK, LANES), 1) == pl.program_id(1)

        def my_lane(v):
            return jnp.sum(jnp.where(mine, v, 0.0), axis=-1, keepdims=True)

        def put(rs, dk, dv):
            if accumulate:
                dk = dk + dki_ref[rs, :]
                dv = dv + dvi_ref[rs, :]
            dk_ref[rs, :] = dk
            dv_ref[rs, :] = dv

        for r in range(dil):
            kp = vp = hold_k = hold_v = rs_prev = None
            for n in range(nblk):
                rs = _tile_rows(r, n, dil)
                qt = q_ref[rs, :].astype(BF16)
                kc = k_ref[rs, :].astype(BF16)
                vc = v_ref[rs, :].astype(BF16)
                dot = do_ref[rs, :].astype(BF16)
                lm = my_lane(lj_ref[rs, :])
                dmm = my_lane(dm_ref[rs, :])
                if n == 0:
                    kcat, vcat, mask = kc, vc, mask1
                else:
                    kcat, vcat, mask = jnp.concatenate([kp, kc], axis=0), jnp.concatenate([vp, vc], axis=0), mask2
                p = jnp.exp(jnp.where(mask, _dot_nt(qt, kcat) * scale, NEG_INF) - lm)
                ds = (p * (_dot_nt(dot, vcat) - dmm)).astype(BF16)
                dq_ref[rs, :] = _dot(ds, kcat) * scale
                dkc = _dot_tn(ds, qt) * scale
                dvc = _dot_tn(p.astype(BF16), dot)
                if n > 0:
                    put(rs_prev, hold_k + dkc[:BLK, :], hold_v + dvc[:BLK, :])
                    dkc, dvc = dkc[BLK:, :], dvc[BLK:, :]
                hold_k, hold_v, kp, vp, rs_prev = dkc, dvc, kc, vc, rs
            put(rs_prev, hold_k, hold_v)

    col = lambda base: pl.BlockSpec((None, s, HEAD_DIM), lambda bi, h: (bi, 0, base + h))
    any_spec = pl.BlockSpec(memory_space=pl.ANY)
    stat = pl.BlockSpec((None, s, LANES), lambda bi, h: (bi, 0, 0))
    head = lambda base: pl.BlockSpec((None, None, s, HEAD_DIM), lambda bi, h: (base + h, bi, 0, 0))
    in_specs = [head(g * nh), head(g * nh), head((N_GROUPS + g) * nh), col(0), stat, stat]
    args = [q, kv, kv, do, lsej, dm]
    aliases = {}
    if dq_buf is not None:
        in_specs.append(any_spec)
        args.append(dq_buf)
        aliases[6] = 0
    if dk_buf is not None:
        in_specs += [col(g * nh) if accumulate else any_spec] * 2
        args += [dk_buf, dv_buf]
        aliases.update({kv_at: 1, kv_at + 1: 2})
    shape = jax.ShapeDtypeStruct((b, s, N_GROUPS * hw), F32)
    return pl.pallas_call(
        body, name=name, grid=(b, nh), in_specs=in_specs, out_specs=[col(g * nh)] * 3, out_shape=[shape] * 3,
        input_output_aliases=aliases, compiler_params=_cp("parallel", "parallel"),
    )(*args)


def sum_parts(g, recv, me, name, tm=512):
    _, rows, c = g.shape
    n = recv.shape[0]
    tm = _row_tile(rows, tm)

    def body(me_ref, g_ref, r_ref, o_ref):
        acc = g_ref[...].astype(F32)
        for j in range(n):
            acc = acc + r_ref[j].astype(F32)
        o_ref[...] = acc

    return pl.pallas_call(
        body, name=name,
        grid_spec=pltpu.PrefetchScalarGridSpec(
            num_scalar_prefetch=1, grid=(rows // tm,),
            in_specs=[pl.BlockSpec((None, tm, c), lambda i, me_ref: (me_ref[0], i, 0)),
                      pl.BlockSpec((n, tm, c), lambda i, me_ref: (0, i, 0))],
            out_specs=pl.BlockSpec((tm, c), lambda i, me_ref: (i, 0))),
        out_shape=jax.ShapeDtypeStruct((rows, c), F32), compiler_params=_cp("parallel"),
    )(me, g, recv)


def adamw(w, m, v, g_parts, name, tm=256):
    rows, c = w.shape
    tm = _row_tile(rows, tm)
    npart = len(g_parts)

    def body(*refs):
        w_ref, m_ref, v_ref = refs[:3]
        g_refs = refs[3:3 + npart]
        go_ref, d_ref, mo_ref, vo_ref = refs[3 + npart:]
        g = g_refs[0][...]
        for k in range(1, npart):
            g = g + g_refs[k][...]
        mn = ADAM_B1 * m_ref[...] + (1.0 - ADAM_B1) * g
        vn = ADAM_B2 * v_ref[...] + (1.0 - ADAM_B2) * (g * g)
        m_hat = mn / (1.0 - ADAM_B1 ** ADAM_STEP)
        v_hat = vn / (1.0 - ADAM_B2 ** ADAM_STEP)
        go_ref[...] = g
        d_ref[...] = -ADAM_LR * (m_hat / (jnp.sqrt(v_hat) + ADAM_EPS) + ADAM_WD * w_ref[...])
        mo_ref[...] = mn
        vo_ref[...] = vn

    row = pl.BlockSpec((tm, c), lambda i: (i, 0))
    return pl.pallas_call(
        body, name=name, grid=(rows // tm,), in_specs=[row] * (3 + npart), out_specs=[row] * 4,
        out_shape=[jax.ShapeDtypeStruct((rows, c), F32)] * 4, compiler_params=_cp("parallel"),
    )(w, m, v, *g_parts)


def _place():
    return lax.axis_index("x"), lax.axis_index("y"), lax.axis_index("c")


def _other_chips(x, y, c):
    return [(1 - x, y, c), (x, 1 - y, c), (1 - x, 1 - y, c)]


def _chip_of(px, py):
    return 2 * px + py


HBM_SPEC = pl.BlockSpec(memory_space=pltpu.HBM)
SEM_SPEC = pl.BlockSpec(memory_space=pltpu.SEMAPHORE)
ANY_SPEC = pl.BlockSpec(memory_space=pl.ANY)
DATAFLOW = pltpu.SideEffectType.DATAFLOW_SIDE_EFFECTING
N_PEER_CHIPS = N_CHIPS - 1


def _hbm(a):
    return pltpu.with_memory_space_constraint(a, pltpu.HBM)


def _hbm_like(arrays):
    return [pltpu.HBM(a.shape, a.dtype) for a in arrays]


def cast_place(w, layer, me, out_dtype, name, tm=512, nslots=N_CHIPS, dep=None):
    rows, c = w.shape[-2:]
    tm = _row_tile(rows, tm)

    def body(me_ref, w_ref, *rest):
        rest[-1][...] = w_ref[...].astype(out_dtype)

    if layer is None:
        in_specs = [pl.BlockSpec((tm, c), lambda i, me_ref: (i, 0))]
    else:
        in_specs = [pl.BlockSpec((None, tm, c), lambda i, me_ref: (layer, i, 0))]
    args = [me, w]
    if dep is not None:
        in_specs.append(pl.BlockSpec(DEP_SPEC_SHAPE, lambda i, me_ref: (0, 0)))
        args.append(dep)
    return pl.pallas_call(
        body, name=name,
        grid_spec=pltpu.PrefetchScalarGridSpec(
            num_scalar_prefetch=1, grid=(rows // tm,), in_specs=in_specs,
            out_specs=pl.BlockSpec((None, tm, c), lambda i, me_ref: (me_ref[0], i, 0))),
        out_shape=jax.ShapeDtypeStruct((nslots, rows, c), out_dtype), compiler_params=_cp("parallel"),
    )(*args)


def gather_start(lands, chunk_sizes, name="gather_start"):
    n = len(lands)
    nch = len(chunk_sizes)
    assert sum(chunk_sizes) == n

    def body(*refs):
        land_refs = refs[:n]
        outs = refs[n:]
        send_sems, recv_sems = outs[:nch], outs[nch:2 * nch]
        token = outs[-1]
        x, y, c = _place()
        me = _chip_of(x, y)
        peers = _other_chips(x, y, c)
        k = 0
        for ck, size in enumerate(chunk_sizes):
            for pos in range(size):
                for r, peer in enumerate(peers):
                    pltpu.make_async_remote_copy(
                        src_ref=land_refs[k].at[me], dst_ref=land_refs[k].at[me],
                        send_sem=send_sems[ck].at[N_PEER_CHIPS * pos + r], recv_sem=recv_sems[ck].at[N_PEER_CHIPS * pos + r],
                        device_id=peer, device_id_type=MESH).start()
                k += 1
        token[...] = jnp.zeros(token.shape, F32)

    sems = [pltpu.SemaphoreType.DMA((N_PEER_CHIPS * s,)) for s in chunk_sizes]
    res = pl.pallas_call(
        body, name=name,
        out_shape=(*sems, *sems, *_hbm_like(lands), jax.ShapeDtypeStruct(DEP_SPEC_SHAPE, F32)),
        in_specs=[HBM_SPEC] * n,
        out_specs=(*[SEM_SPEC] * (2 * nch), *[HBM_SPEC] * n, pl.BlockSpec(memory_space=pltpu.VMEM)),
        input_output_aliases={k: 2 * nch + k for k in range(n)},
        compiler_params=pltpu.CompilerParams(has_side_effects=DATAFLOW),
    )(*[_hbm(a) for a in lands])
    return res[:nch], res[nch:2 * nch], res[2 * nch:2 * nch + n], res[-1]


def gather_wait(send_sem, recv_sem, lands, after, name):
    n = len(lands)

    def body(*refs):
        land_refs = refs[:n]
        ssem, rsem = refs[n], refs[n + 1]
        x, y, c = _place()
        me = _chip_of(x, y)
        for pos in range(n):
            for r, peer in enumerate(_other_chips(x, y, c)):
                cp = pltpu.make_async_remote_copy(
                    src_ref=land_refs[pos].at[me], dst_ref=land_refs[pos].at[_chip_of(peer[0], peer[1])],
                    send_sem=ssem.at[N_PEER_CHIPS * pos + r], recv_sem=rsem.at[N_PEER_CHIPS * pos + r],
                    device_id=peer, device_id_type=MESH)
                cp.wait_send()
                cp.wait_recv()

    return pl.pallas_call(
        body, name=name, out_shape=tuple(_hbm_like(lands)),
        in_specs=[*[HBM_SPEC] * n, SEM_SPEC, SEM_SPEC, ANY_SPEC], out_specs=[HBM_SPEC] * n,
        input_output_aliases={k: k for k in range(n)},
        compiler_params=pltpu.CompilerParams(has_side_effects=DATAFLOW),
    )(*lands, send_sem, recv_sem, after)


def scatter_start(grads, name):
    n = len(grads)
    recvs = [lax.empty((N_PEER_CHIPS, *g.shape[1:]), g.dtype) for g in grads]

    def body(*refs):
        g_refs, r_refs = refs[:n], refs[n:2 * n]
        send_sems, recv_sems = refs[2 * n], refs[2 * n + 1]
        token = refs[-1]
        x, y, c = _place()
        for k in range(n):
            for r, peer in enumerate(_other_chips(x, y, c)):
                pltpu.make_async_remote_copy(
                    src_ref=g_refs[k].at[_chip_of(peer[0], peer[1])], dst_ref=r_refs[k].at[r],
                    send_sem=send_sems.at[N_PEER_CHIPS * k + r], recv_sem=recv_sems.at[N_PEER_CHIPS * k + r],
                    device_id=peer, device_id_type=MESH).start()
        token[...] = jnp.zeros(token.shape, F32)

    sem = pltpu.SemaphoreType.DMA((N_PEER_CHIPS * n,))
    res = pl.pallas_call(
        body, name=name,
        out_shape=(sem, sem, *_hbm_like(grads), *_hbm_like(recvs), jax.ShapeDtypeStruct(DEP_SPEC_SHAPE, F32)),
        in_specs=[HBM_SPEC] * (2 * n),
        out_specs=(SEM_SPEC, SEM_SPEC, *[HBM_SPEC] * (2 * n), pl.BlockSpec(memory_space=pltpu.VMEM)),
        input_output_aliases={k: 2 + k for k in range(2 * n)},
        compiler_params=pltpu.CompilerParams(has_side_effects=DATAFLOW),
    )(*[_hbm(a) for a in grads], *[_hbm(a) for a in recvs])
    return res[0], res[1], res[2:2 + n], res[2 + n:2 + 2 * n], res[-1]


def scatter_wait(send_sem, recv_sem, grads, recvs, after, name):
    n = len(grads)

    def body(*refs):
        g_refs, r_refs = refs[:n], refs[n:2 * n]
        ssem, rsem = refs[2 * n], refs[2 * n + 1]
        x, y, c = _place()
        for k in range(n):
            for r, peer in enumerate(_other_chips(x, y, c)):
                cp = pltpu.make_async_remote_copy(
                    src_ref=g_refs[k].at[_chip_of(peer[0], peer[1])], dst_ref=r_refs[k].at[r],
                    send_sem=ssem.at[N_PEER_CHIPS * k + r], recv_sem=rsem.at[N_PEER_CHIPS * k + r],
                    device_id=peer, device_id_type=MESH)
                cp.wait_send()
                cp.wait_recv()

    res = pl.pallas_call(
        body, name=name, out_shape=(*_hbm_like(grads), *_hbm_like(recvs)),
        in_specs=[*[HBM_SPEC] * (2 * n), SEM_SPEC, SEM_SPEC, ANY_SPEC], out_specs=[HBM_SPEC] * (2 * n),
        input_output_aliases={k: k for k in range(2 * n)},
        compiler_params=pltpu.CompilerParams(has_side_effects=DATAFLOW),
    )(*grads, *recvs, send_sem, recv_sem, after)
    return res[:n], res[n:]


def swap_start(parts, name):
    n = len(parts)
    lands = [lax.empty(p.shape, p.dtype) for p in parts]

    def body(*refs):
        p_refs, l_refs = refs[:n], refs[n:2 * n]
        sems = refs[2 * n:4 * n]
        token = refs[-1]
        x, y, c = _place()
        for k in range(n):
            pltpu.make_async_remote_copy(
                src_ref=p_refs[k], dst_ref=l_refs[k], send_sem=sems[k], recv_sem=sems[n + k],
                device_id=(x, y, 1 - c), device_id_type=MESH).start()
        token[...] = jnp.zeros(token.shape, F32)

    sem = pltpu.SemaphoreType.DMA(())
    res = pl.pallas_call(
        body, name=name,
        out_shape=(*[sem] * (2 * n), *_hbm_like(parts), *_hbm_like(lands), jax.ShapeDtypeStruct(DEP_SPEC_SHAPE, F32)),
        in_specs=[HBM_SPEC] * (2 * n),
        out_specs=(*[SEM_SPEC] * (2 * n), *[HBM_SPEC] * (2 * n), pl.BlockSpec(memory_space=pltpu.VMEM)),
        input_output_aliases={k: 2 * n + k for k in range(2 * n)},
        compiler_params=pltpu.CompilerParams(has_side_effects=DATAFLOW),
    )(*[_hbm(a) for a in parts], *[_hbm(a) for a in lands])
    return res[:n], res[n:2 * n], res[2 * n:3 * n], res[3 * n:4 * n], res[-1]


def swap_wait(send_sem, recv_sem, part, land, after, name):
    def body(p_ref, l_ref, ssem, rsem, after_ref, p_out, l_out):
        x, y, c = _place()
        cp = pltpu.make_async_remote_copy(src_ref=p_ref, dst_ref=l_ref, send_sem=ssem, recv_sem=rsem,
                                          device_id=(x, y, 1 - c), device_id_type=MESH)
        cp.wait_send()
        cp.wait_recv()

    return pl.pallas_call(
        body, name=name, out_shape=tuple(_hbm_like([part, land])),
        in_specs=[HBM_SPEC, HBM_SPEC, SEM_SPEC, SEM_SPEC, ANY_SPEC], out_specs=[HBM_SPEC, HBM_SPEC],
        input_output_aliases={0: 0, 1: 1},
        compiler_params=pltpu.CompilerParams(has_side_effects=DATAFLOW),
    )(part, land, send_sem, recv_sem, after)


def _xor_peer(x, y, c, k):
    px, py, pc = x ^ ((k >> 2) & 1), y ^ ((k >> 1) & 1), c ^ (k & 1)
    return (px, py, pc), 4 * px + 2 * py + pc


def small_start(land, name="small_start"):
    def body(l_ref, ssem, rsem, l_out, token):
        x, y, c = _place()
        me = 4 * x + 2 * y + c
        for k in range(1, N_DEV):
            peer, _ = _xor_peer(x, y, c, k)
            pltpu.make_async_remote_copy(
                src_ref=l_ref.at[me], dst_ref=l_ref.at[me], send_sem=ssem.at[k - 1], recv_sem=rsem.at[k - 1],
                device_id=peer, device_id_type=MESH).start()
        token[...] = jnp.zeros(token.shape, F32)

    sem = pltpu.SemaphoreType.DMA((N_DEV - 1,))
    return pl.pallas_call(
        body, name=name,
        out_shape=(sem, sem, pltpu.HBM(land.shape, land.dtype), jax.ShapeDtypeStruct(DEP_SPEC_SHAPE, F32)),
        in_specs=[HBM_SPEC], out_specs=(SEM_SPEC, SEM_SPEC, HBM_SPEC, pl.BlockSpec(memory_space=pltpu.VMEM)),
        input_output_aliases={0: 2}, compiler_params=pltpu.CompilerParams(has_side_effects=DATAFLOW),
    )(_hbm(land))


def small_wait(send_sem, recv_sem, land, after, name="small_wait"):
    def body(l_ref, ssem, rsem, after_ref, l_out):
        x, y, c = _place()
        me = 4 * x + 2 * y + c
        for k in range(1, N_DEV):
            peer, slot = _xor_peer(x, y, c, k)
            cp = pltpu.make_async_remote_copy(
                src_ref=l_ref.at[me], dst_ref=l_ref.at[slot], send_sem=ssem.at[k - 1], recv_sem=rsem.at[k - 1],
                device_id=peer, device_id_type=MESH)
            cp.wait_send()
            cp.wait_recv()

    return pl.pallas_call(
        body, name=name, out_shape=pltpu.HBM(land.shape, land.dtype),
        in_specs=[HBM_SPEC, SEM_SPEC, SEM_SPEC, ANY_SPEC], out_specs=HBM_SPEC, input_output_aliases={0: 0},
        compiler_params=pltpu.CompilerParams(has_side_effects=DATAFLOW),
    )(land, send_sem, recv_sem, after)


def sum_slots(land, name="sum_slots", tm=256):
    n, rows, c = land.shape
    tm = _row_tile(rows, tm)

    def body(l_ref, o_ref):
        acc = l_ref[0]
        for j in range(1, n):
            acc = acc + l_ref[j]
        o_ref[...] = acc

    return pl.pallas_call(
        body, name=name, grid=(rows // tm,), in_specs=[pl.BlockSpec((n, tm, c), lambda i: (0, i, 0))],
        out_specs=pl.BlockSpec((tm, c), lambda i: (i, 0)), out_shape=jax.ShapeDtypeStruct((rows, c), F32),
        compiler_params=_cp("parallel"),
    )(land)


PACK_ROW_TILE = 256


def _pack(arrays):
    flat = jnp.concatenate([a.reshape(-1).astype(F32) for a in arrays])
    n = flat.shape[0]
    rows = -(-n // LANES)
    rows = -(-rows // PACK_ROW_TILE) * PACK_ROW_TILE
    return jnp.pad(flat, (0, rows * LANES - n)).reshape(rows, LANES)


def _unpack(packed, shapes, lead=()):
    flat = packed.reshape(*lead, -1)
    out, off = [], 0
    for shp in shapes:
        n = math.prod(shp)
        out.append(flat[..., off:off + n].reshape(*lead, *shp))
        off += n
    return out


def _row(vec):
    return vec.reshape(1, -1)


def kernel(x, mix_pre_g, mix_post_g, ffn_pre_g, ffn_post_g, cm_w_in, cm_b_in, cm_dw, cm_dw_b, cm_ln_g, cm_ln_b, cm_w_out, cm_b_out, kv_norm_g, w_kv, w_q, w_o, ffn_w_in, ffn_dw, ffn_dw_b, ffn_w_out, loss_target, m_mix_pre_g, m_mix_post_g, m_ffn_pre_g, m_ffn_post_g, m_cm_w_in, m_cm_b_in, m_cm_dw, m_cm_dw_b, m_cm_ln_g, m_cm_ln_b, m_cm_w_out, m_cm_b_out, m_kv_norm_g, m_w_kv, m_w_q, m_w_o, m_ffn_w_in, m_ffn_dw, m_ffn_dw_b, m_ffn_w_out, v_mix_pre_g, v_mix_post_g, v_ffn_pre_g, v_ffn_post_g, v_cm_w_in, v_cm_b_in, v_cm_dw, v_cm_dw_b, v_cm_ln_g, v_cm_ln_b, v_cm_w_out, v_cm_b_out, v_kv_norm_g, v_w_kv, v_w_q, v_w_o, v_ffn_w_in, v_ffn_dw, v_ffn_dw_b, v_ffn_w_out):
    names = ["mix_pre_g", "mix_post_g", "ffn_pre_g", "ffn_post_g", "cm_w_in", "cm_b_in", "cm_dw", "cm_dw_b", "cm_ln_g",
             "cm_ln_b", "cm_w_out", "cm_b_out", "kv_norm_g", "w_kv", "w_q", "w_o", "ffn_w_in", "ffn_dw", "ffn_dw_b",
             "ffn_w_out"]
    w_in = dict(zip(names, [mix_pre_g, mix_post_g, ffn_pre_g, ffn_post_g, cm_w_in, cm_b_in, cm_dw, cm_dw_b, cm_ln_g,
                            cm_ln_b, cm_w_out, cm_b_out, kv_norm_g, w_kv, w_q, w_o, ffn_w_in, ffn_dw, ffn_dw_b, ffn_w_out]))
    m_in = dict(zip(names, [m_mix_pre_g, m_mix_post_g, m_ffn_pre_g, m_ffn_post_g, m_cm_w_in, m_cm_b_in, m_cm_dw, m_cm_dw_b,
                            m_cm_ln_g, m_cm_ln_b, m_cm_w_out, m_cm_b_out, m_kv_norm_g, m_w_kv, m_w_q, m_w_o, m_ffn_w_in,
                            m_ffn_dw, m_ffn_dw_b, m_ffn_w_out]))
    v_in = dict(zip(names, [v_mix_pre_g, v_mix_post_g, v_ffn_pre_g, v_ffn_post_g, v_cm_w_in, v_cm_b_in, v_cm_dw, v_cm_dw_b,
                            v_cm_ln_g, v_cm_ln_b, v_cm_w_out, v_cm_b_out, v_kv_norm_g, v_w_kv, v_w_q, v_w_o, v_ffn_w_in,
                            v_ffn_dw, v_ffn_dw_b, v_ffn_w_out]))

    bsz, seq, d = x.shape
    t = bsz * seq
    n_b = DEPTH - N_A
    hw = w_o.shape[-1]
    qw = N_GROUPS * hw
    f2 = ffn_dw_b.shape[-1]
    f = f2 // 2
    me_chip = _chip_of(lax.axis_index("x"), lax.axis_index("y"))

    big = ["cm_w_in", "cm_w_out", "w_kv", "w_q", "w_o", "ffn_w_in", "ffn_w_out"]
    row_sharded = ("cm_w_out", "w_o", "ffn_w_out")
    small_sharded = ["cm_b_in", "cm_dw", "cm_dw_b", "cm_ln_g", "cm_ln_b", "cm_b_out", "ffn_dw"]
    small_pack = _pack([w_in[n] for n in small_sharded])
    chunks = [
        [("cm_w_in", 0), ("small", None)],
        [("cm_w_out", 0)],
        [("ffn_w_in", 0), ("ffn_w_out", 0)],
        [("cm_w_in", 1), ("cm_w_out", 1)],
        [("ffn_w_in", 1), ("ffn_w_out", 1)],
        [("w_kv", None)],
        [("w_q", 0), ("w_o", 0)],
        [("ffn_w_in", 2), ("ffn_w_out", 2)],
        [("w_q", 1), ("w_o", 1)],
        [("ffn_w_in", 3), ("ffn_w_out", 3)],
    ]
    pieces = [pc for ch in chunks for pc in ch]
    chunk_of = {pc: ck for ck, ch in enumerate(chunks) for pc in ch}

    me_arr = me_chip.astype(jnp.int32).reshape(1)

    def land_of(pc, dep=None):
        n, l = pc
        if n == "small":
            return cast_place(small_pack, None, me_arr, F32, name="place_small", dep=dep)
        return cast_place(w_in[n], l, me_arr, BF16, name=f"place_{n}_{l}", dep=dep)

    groups = [[0, 1], [2], list(range(3, len(chunks)))]
    g_send, g_recv, lands_f = {}, {}, {}

    def start_group(gi, dep, name):
        cks = groups[gi]
        pcs = [pc for ck in cks for pc in chunks[ck]]
        lands = [land_of(pc, dep if k == 0 else None) for k, pc in enumerate(pcs)]
        send, recv, lands_thru, tok = gather_start(lands, [len(chunks[ck]) for ck in cks], name=name)
        pos = 0
        for j, ck in enumerate(cks):
            g_send[ck], g_recv[ck] = send[j], recv[j]
            lands_f[ck] = lands_thru[pos:pos + len(chunks[ck])]
            pos += len(chunks[ck])
        return tok

    token_a = start_group(0, None, "gather_start_a")
    token = start_group(1, token_a, "gather_start_b")
    weights = {}

    def finish_chunk(ck, after):
        got = gather_wait(g_send[ck], g_recv[ck], lands_f[ck], after, name=f"gather_wait{ck}")
        for pc, arr in zip(chunks[ck], got):
            weights[pc] = arr.reshape(1, -1, arr.shape[-1]) if pc[0] in row_sharded else arr

    def wmat(n, l=None, after=None):
        if (n, l) not in weights:
            finish_chunk(chunk_of[(n, l)], after)
        arr = weights[(n, l)]
        return arr, arr.shape[0]

    finish_chunk(0, token)
    small_full = {}
    for n, arr4 in zip(small_sharded, _unpack(weights[("small", None)], [w_in[n].shape for n in small_sharded], lead=(N_CHIPS,))):
        shp = w_in[n].shape
        small_full[n] = jnp.moveaxis(arr4, 0, -2).reshape(*shp[:-1], N_CHIPS * shp[-1])

    x2d = x.reshape(t, d)
    saved = []
    (h1,) = resid_norm_fwd(x2d, None, None, [_row(mix_pre_g[0])], name="norm_in", dep=token)
    xcur = x2d
    kv_state = None
    for i in range(DEPTH):
        sv = {"x_in": xcur, "h1": h1}
        if i < N_A:
            z = mm_nn(h1, *wmat("cm_w_in", i, h1), 1, 0, bias=_row(small_full["cm_b_in"][i]), name=f"cm_in{i}")
            u2 = glu_conv_fwd(z.reshape(bsz, seq, 2 * d), small_full["cm_dw"][i], _row(small_full["cm_dw_b"][i]),
                              name=f"glu_conv{i}").reshape(t, d)
            u4 = ln_silu_fwd(u2, _row(small_full["cm_ln_g"][i]), _row(small_full["cm_ln_b"][i]), name=f"ln_silu{i}")
            y = mm_nn(u4, *wmat("cm_w_out", i, u4), 1, 0, bias=_row(small_full["cm_b_out"][i]), out_dtype=BF16,
                      name=f"cm_out{i}")
            sv.update(z=z, u2=u2, u4=u4)
        else:
            j = i - N_A
            q = mm_nn(h1, *wmat("w_q", j, h1), 1, 0, name=f"q_proj{j}", head_major=True).reshape(-1, bsz, seq, HEAD_DIM)
            outs, lses = [], []
            for g, dil in enumerate(DILATIONS):
                o_g, l_g = attn_fwd(q, kv_state["kv"], g, dil, hw, name=f"attn_fwd{j}_{g}")
                outs.append(o_g.reshape(t, hw))
                lses.append(l_g.reshape(t, LANES))
            merged, lsej = attn_merge(outs, lses, name=f"attn_merge{j}")
            y = mm_nn(merged, *wmat("w_o", j, merged), 1, 0, out_dtype=BF16, name=f"o_proj{j}")
            sv.update(q=q, merged=merged, lsej=lsej)
        x1, h2 = resid_norm_fwd(xcur, y, _row(mix_post_g[i]), [_row(ffn_pre_g[i])], name=f"resid_mix{i}")
        ffn_after = start_group(2, x1, "gather_start_c") if i == 0 else h2
        p = mm_nn(h2, *wmat("ffn_w_in", i, ffn_after), 1, 0, out_dtype=BF16, name=f"ffn_in{i}", tm=2048)
        s_act = ffn_mid_fwd(p.reshape(bsz, seq, f2), small_full["ffn_dw"][i], _row(ffn_dw_b[i]), name=f"ffn_mid{i}").reshape(t, f)
        y2 = mm_nn(s_act, *wmat("ffn_w_out", i), 1, 0, out_dtype=BF16, name=f"ffn_out{i}")
        next_gains = []
        if i + 1 < DEPTH:
            next_gains.append(_row(mix_pre_g[i + 1]))
        if i == N_A - 1:
            next_gains.append(_row(kv_norm_g))
        res = resid_norm_fwd(x1, y2, _row(ffn_post_g[i]), next_gains, name=f"resid_ffn{i}")
        sv.update(y=y, x1=x1, h2=h2, p=p, s=s_act, y2=y2)
        saved.append(sv)
        xcur = res[0]
        if i + 1 < DEPTH:
            h1 = res[1]
        if i == N_A - 1:
            kvn = res[2]
            kv = mm_nn(kvn, *wmat("w_kv", None, kvn), 1, 0, name="kv_proj", head_major=True).reshape(-1, bsz, seq, HEAD_DIM)
            kv_state = {"kv": kv, "kvn": kvn, "x_a": xcur}

    dx, loss_tile = loss_fwd_bwd(xcur, loss_target.reshape(t, d))
    loss = lax.psum(loss_tile[0, 0], ("x", "y", "c"))

    gsm = {n: [None] * w_in[n].shape[0] for n in
           ["mix_pre_g", "mix_post_g", "ffn_pre_g", "ffn_post_g", "cm_b_in", "cm_dw", "cm_dw_b", "cm_ln_g", "cm_ln_b",
            "cm_b_out", "ffn_dw", "ffn_dw_b"]}
    gbig = {}
    in_flight = []
    dep = None

    def start_scatter(pcs, tag):
        ssem, rsem, g_f, r_f, tok = scatter_start([gbig[pc] for pc in pcs], name=f"scatter_start_{tag}")
        in_flight.append((pcs, ssem, rsem, g_f, r_f))
        return tok

    dk_buf = dv_buf = None
    for i in range(DEPTH - 1, -1, -1):
        sv = saved[i]
        dy2, dg, _ = norm_bwd(sv["y2"], _row(ffn_post_g[i]), dx, out_dtype=BF16, name=f"bwd_ffn_post{i}", dep=dep)
        gsm["ffn_post_g"][i] = dg
        ds = mm_nt(dy2, *wmat("ffn_w_out", i), 1, 0, out_dtype=BF16, name=f"bwd_ffn_out_dx{i}")
        gbig[("ffn_w_out", i)] = mm_tn(sv["s"], dy2, 1, name=f"bwd_ffn_out_dw{i}").reshape(N_CHIPS, f // N_CHIPS, d)
        dpa, dpg, ddwa, ddwg, ddba, ddbg = ffn_mid_bwd(sv["p"].reshape(bsz, seq, f2), small_full["ffn_dw"][i], _row(ffn_dw_b[i]),
                                                       ds.reshape(bsz, seq, f), name=f"bwd_ffn_mid{i}")
        gsm["ffn_dw"][i] = jnp.concatenate([jnp.sum(ddwa, axis=1), jnp.sum(ddwg, axis=1)], axis=-1)
        gsm["ffn_dw_b"][i] = jnp.concatenate([ddba, ddbg], axis=-1)
        dp = [dpa.reshape(t, f), dpg.reshape(t, f)]
        dh2 = mm_nt(dp, *wmat("ffn_w_in", i), 1, 0, name=f"bwd_ffn_in_dx{i}")
        gbig[("ffn_w_in", i)] = mm_tn(sv["h2"], dp, N_CHIPS, name=f"bwd_ffn_in_dw{i}")
        dx1, dg, _ = norm_bwd(sv["x1"], _row(ffn_pre_g[i]), dh2, add=dx, name=f"bwd_ffn_pre{i}")
        gsm["ffn_pre_g"][i] = dg
        dep = start_scatter([("ffn_w_in", 0), ("ffn_w_out", 0)], "ffn0") if i == 0 else None
        dy, dg, dbias = norm_bwd(sv["y"], _row(mix_post_g[i]), dx1, out_dtype=BF16, name=f"bwd_mix_post{i}", dep=dep)
        gsm["mix_post_g"][i] = dg
        if i < N_A:
            gsm["cm_b_out"][i] = dbias
            du4 = mm_nt(dy, *wmat("cm_w_out", i), 1, 0, name=f"bwd_cm_out_dx{i}")
            gbig[("cm_w_out", i)] = mm_tn(sv["u4"], dy, 1, name=f"bwd_cm_out_dw{i}").reshape(N_CHIPS, d // N_CHIPS, d)
            du2, dlg, dlb = ln_silu_bwd(sv["u2"], _row(small_full["cm_ln_g"][i]), _row(small_full["cm_ln_b"][i]), du4,
                                        name=f"bwd_ln_silu{i}")
            gsm["cm_ln_g"][i], gsm["cm_ln_b"][i] = dlg, dlb
            dza, dzg, ddw, ddwb, dba, dbg = glu_conv_bwd(sv["z"].reshape(bsz, seq, 2 * d), small_full["cm_dw"][i],
                                                         du2.reshape(bsz, seq, d), name=f"bwd_glu_conv{i}")
            gsm["cm_dw"][i] = jnp.sum(ddw, axis=1)
            gsm["cm_dw_b"][i] = ddwb
            gsm["cm_b_in"][i] = jnp.concatenate([dba, dbg], axis=-1)
            dz = [dza.reshape(t, d), dzg.reshape(t, d)]
            dh1 = mm_nt(dz, *wmat("cm_w_in", i), 1, 0, name=f"bwd_cm_in_dx{i}")
            gbig[("cm_w_in", i)] = mm_tn(sv["h1"], dz, N_CHIPS, name=f"bwd_cm_in_dw{i}")
        else:
            j = i - N_A
            dmerged = mm_nt(dy, *wmat("w_o", j), 1, 0, name=f"bwd_o_proj_dx{j}")
            gbig[("w_o", j)] = mm_tn(sv["merged"], dy, 1, name=f"bwd_o_proj_dw{j}").reshape(N_CHIPS, hw // N_CHIPS, d)
            dmt = attn_bwd_prep(dmerged, sv["merged"], name=f"bwd_attn_prep{j}")
            dq_buf = None
            add_to_kv = dk_buf is not None
            for g, dil in enumerate(DILATIONS):
                dq_buf, dk_buf, dv_buf = attn_bwd(
                    sv["q"], kv_state["kv"], g, dil, dmerged.reshape(bsz, seq, hw), sv["lsej"].reshape(bsz, seq, LANES),
                    dmt.reshape(bsz, seq, LANES), dq_buf, dk_buf, dv_buf, add_to_kv, hw, name=f"attn_bwd{j}_{g}")
            dq = dq_buf.reshape(t, qw)
            dh1 = mm_nt(dq, *wmat("w_q", j), 1, 0, name=f"bwd_q_proj_dx{j}")
            gbig[("w_q", j)] = mm_tn(sv["h1"], dq, N_CHIPS, name=f"bwd_q_proj_dw{j}")
        dx, dg, _ = norm_bwd(sv["x_in"], _row(mix_pre_g[i]), dh1, add=dx1, name=f"bwd_mix_pre{i}")
        gsm["mix_pre_g"][i] = dg
        if i > N_A:
            dep = start_scatter([("ffn_w_in", i), ("ffn_w_out", i), ("w_q", i - N_A), ("w_o", i - N_A)], f"l{i}")
        elif 0 < i < N_A:
            dep = start_scatter([("ffn_w_in", i), ("ffn_w_out", i), ("cm_w_in", i), ("cm_w_out", i)], f"l{i}")
        elif i == 0:
            last_token = start_scatter([("cm_w_in", 0), ("cm_w_out", 0)], "cm0")
        if i == N_A:
            dkv = [dk_buf.reshape(t, qw), dv_buf.reshape(t, qw)]
            dkvn = mm_nt(dkv, *wmat("w_kv"), 1, 0, name="bwd_kv_proj_dx")
            gbig[("w_kv", None)] = mm_tn(kv_state["kvn"], dkv, N_CHIPS, name="bwd_kv_proj_dw")
            dx, dg_kv, _ = norm_bwd(kv_state["x_a"], _row(kv_norm_g), dkvn, add=dx, name="bwd_kv_norm")
            dep = start_scatter([("ffn_w_in", i), ("ffn_w_out", i), ("w_q", 0), ("w_o", 0), ("w_kv", None)], f"l{i}")
    grad_x = dx.reshape(bsz, seq, d)

    plane_of = {}
    outs_g, outs_d, outs_m, outs_v = {}, {}, {}, {}

    def finish_scatter(k, after):
        pcs, ssem, rsem, g_f, r_f = in_flight[k]
        g_done, r_done = scatter_wait(ssem, rsem, g_f, r_f, after, name=f"scatter_wait{k}")
        for pc, g_arr, r_arr in zip(pcs, g_done, r_done):
            plane_of[pc] = sum_parts(g_arr, r_arr, me_arr, name=f"sum_chips_{pc[0]}_{pc[1]}")

    def update(group, tag, after):
        plane = []
        for n in group:
            if w_in[n].ndim == 2:
                plane.append(plane_of[(n, None)])
            else:
                plane.append(jnp.concatenate([plane_of[(n, l)] for l in range(w_in[n].shape[0])], axis=0))
        ssems, rsems, plane_f, land_f, _ = swap_start(plane, name=f"swap_start_{tag}")
        for k, n in enumerate(group):
            p_mine, p_other = swap_wait(ssems[k], rsems[k], plane_f[k], land_f[k], after, name=f"swap_wait_{n}")
            shp = w_in[n].shape
            flat = lambda a: a.reshape(-1, shp[-1])
            g_, d_, m_, v_ = adamw(flat(w_in[n]), flat(m_in[n]), flat(v_in[n]), [p_mine, p_other], name=f"adamw_{n}")
            outs_g[n], outs_d[n], outs_m[n], outs_v[n] = (a.reshape(shp) for a in (g_, d_, m_, v_))
            after = v_
        return after

    small_names = [n for n in names if n not in big]
    small_shapes_full = {}
    small_grads_full = []
    for n in small_names:
        if n == "kv_norm_g":
            gfull = dg_kv.reshape(-1)
        elif n in ("cm_dw", "ffn_dw"):
            gfull = jnp.stack(gsm[n], axis=0)
        else:
            gfull = jnp.stack([a.reshape(-1) for a in gsm[n]], axis=0)
        small_shapes_full[n] = gfull.shape
        small_grads_full.append(gfull)
    dev_arr = (4 * lax.axis_index("x") + 2 * lax.axis_index("y") + lax.axis_index("c")).astype(jnp.int32).reshape(1)
    small_land = cast_place(_pack(small_grads_full) + last_token[0, 0], None, dev_arr, F32, name="place_small_grads",
                            nslots=N_DEV)
    sm_send, sm_recv, small_land, small_token = small_start(small_land)

    for k in range(len(in_flight) - 1):
        finish_scatter(k, small_token)
    done = update(["w_kv", "w_q", "w_o", "ffn_w_in", "ffn_w_out"], "a", small_token)
    finish_scatter(len(in_flight) - 1, done)
    done = update(["cm_w_in", "cm_w_out"], "b", done)

    summed = sum_slots(small_wait(sm_send, sm_recv, small_land, done))
    g_full = dict(zip(small_names, _unpack(summed, [small_shapes_full[n] for n in small_names])))
    g_loc = {}
    for n in small_names:
        if n in small_sharded:
            width = w_in[n].shape[-1]
            g_loc[n] = lax.dynamic_slice_in_dim(g_full[n], me_chip * width, width, axis=g_full[n].ndim - 1)
        else:
            g_loc[n] = g_full[n]
    res = adamw(_pack([w_in[n] for n in small_names]), _pack([m_in[n] for n in small_names]),
                _pack([v_in[n] for n in small_names]), [_pack([g_loc[n] for n in small_names])], name="adamw_small")
    shapes_loc = [w_in[n].shape for n in small_names]
    for dst, packed in zip((outs_g, outs_d, outs_m, outs_v), res):
        for n, a in zip(small_names, _unpack(packed, shapes_loc)):
            dst[n] = a

    return (loss, grad_x, *[outs_g[n] for n in names], *[outs_d[n] for n in names],
            *[outs_m[n] for n in names], *[outs_v[n] for n in names])
```

```python
import functools
import math

import jax
import jax.numpy as jnp
from jax import lax
from jax.experimental import pallas as pl
from jax.experimental.pallas import tpu as pltpu

F32 = jnp.float32
BF16 = jnp.bfloat16
EPS = 1e-6
NEG_INF = -1e30
N_A = 2
DEPTH = 4
N_GROUPS = 3
DILATIONS = (1, 4, 16)
HEAD_DIM = 128
BLK = 128
LANES = 128
N_CHIPS = 4
N_DEV = 8
VMEM_LIMIT_V7X = 56 * 1024 * 1024

ADAM_LR = 0.001
ADAM_B1 = 0.9
ADAM_B2 = 0.999
ADAM_EPS = 1e-08
ADAM_WD = 0.01
ADAM_STEP = 10

MESH = pl.DeviceIdType.MESH


def _cp(*sem, **kw):
    return pltpu.CompilerParams(dimension_semantics=sem if sem else None, vmem_limit_bytes=VMEM_LIMIT_V7X, **kw)


def _dot(a, b):
    return jnp.dot(a, b, preferred_element_type=F32)


def _dot_nt(a, b):
    return lax.dot_general(a, b, (((1,), (1,)), ((), ())), preferred_element_type=F32)


def _dot_tn(a, b):
    return lax.dot_general(a, b, (((0,), (0,)), ((), ())), preferred_element_type=F32)


def _sigmoid(x):
    return 1.0 / (1.0 + jnp.exp(-x))


def _row_tile(n, want):
    if n <= want:
        return n
    for t in range(want - want % 8, 7, -8):
        if n % t == 0:
            return t
    raise ValueError(f"no row tile for {n} rows")


def mm_nn(a, w, nsh, stride, layer, bias=None, out_dtype=F32, name="mm_nn", tm=1024, head_major=False, spg=1):
    assert stride == 1 and layer == 0 and nsh % spg == 0
    m, k = a.shape
    _, k2, ns = w.shape
    assert k == k2
    tm = _row_tile(m, tm)
    has_bias = bias is not None
    hps = ns // HEAD_DIM

    def body(*refs):
        if has_bias:
            a_ref, w_ref, b_ref, o_ref = refs
        else:
            a_ref, w_ref, o_ref = refs
        av = a_ref[...].astype(BF16)
        for jj in range(spg):
            acc = _dot(av, w_ref[jj])
            if has_bias:
                acc = acc + b_ref[:, jj * ns:(jj + 1) * ns]
            if head_major:
                for hh in range(hps):
                    o_ref[jj * hps + hh] = acc[:, hh * HEAD_DIM:(hh + 1) * HEAD_DIM].astype(out_dtype)
            else:
                o_ref[:, jj * ns:(jj + 1) * ns] = acc.astype(out_dtype)

    in_specs = [
        pl.BlockSpec((tm, k), lambda j, i: (i, 0)),
        pl.BlockSpec((spg, k, ns), lambda j, i: (j, 0, 0)),
    ]
    args = [a, w]
    if has_bias:
        in_specs.append(pl.BlockSpec((1, spg * ns), lambda j, i: (0, j)))
        args.append(bias)
    return pl.pallas_call(
        body,
        name=name,
        grid=(nsh // spg, m // tm),
        in_specs=in_specs,
        out_specs=(pl.BlockSpec((spg * hps, tm, HEAD_DIM), lambda j, i: (j, i, 0)) if head_major
                   else pl.BlockSpec((tm, spg * ns), lambda j, i: (i, j))),
        out_shape=jax.ShapeDtypeStruct((nsh * hps, m, HEAD_DIM) if head_major else (m, nsh * ns), out_dtype),
        compiler_params=_cp("parallel", "parallel"),
    )(*args)


def _split_parts(dy, nsh, spg):
    dys = list(dy) if isinstance(dy, (list, tuple)) else [dy]
    per = nsh // len(dys)
    assert per % spg == 0
    return dys, per, per // spg


def mm_nt(dy, w, nsh, stride, layer, out_dtype=F32, name="mm_nt", tm=1024, spg=1):
    assert stride == 1 and layer == 0
    dys, per, gpp = _split_parts(dy, nsh, spg)
    npart = len(dys)
    m = dys[0].shape[0]
    _, k, ns = w.shape
    assert all(d.shape == (m, per * ns) for d in dys)
    tm = _row_tile(m, tm)
    ngrp = nsh // spg

    def body(*refs):
        dy_refs = refs[:npart]
        w_ref, o_ref, acc_ref = refs[npart:]
        j = pl.program_id(1)

        @pl.when(j == 0)
        def _():
            acc_ref[...] = jnp.zeros(acc_ref.shape, F32)

        for pi in range(npart):
            @pl.when(j // gpp == pi)
            def _(pi=pi):
                part = _dot_nt(dy_refs[pi][:, 0:ns].astype(BF16), w_ref[0])
                for jj in range(1, spg):
                    part = part + _dot_nt(dy_refs[pi][:, jj * ns:(jj + 1) * ns].astype(BF16), w_ref[jj])
                acc_ref[...] += part

        @pl.when(j == ngrp - 1)
        def _():
            o_ref[...] = acc_ref[...].astype(out_dtype)

    dy_specs = [pl.BlockSpec((tm, spg * ns), lambda i, j, pi=pi: (i, jnp.clip(j - pi * gpp, 0, gpp - 1)))
                for pi in range(npart)]
    return pl.pallas_call(
        body,
        name=name,
        grid=(m // tm, ngrp),
        in_specs=[*dy_specs, pl.BlockSpec((spg, k, ns), lambda i, j: (j, 0, 0))],
        out_specs=pl.BlockSpec((tm, k), lambda i, j: (i, 0)),
        out_shape=jax.ShapeDtypeStruct((m, k), out_dtype),
        scratch_shapes=[pltpu.VMEM((tm, k), F32)],
        compiler_params=_cp("parallel", "arbitrary"),
    )(*dys, w)


def mm_tn(a, dy, nsh, name="mm_tn", tm=1024, spg=1):
    dys, per, gpp = _split_parts(dy, nsh, spg)
    npart = len(dys)
    m, k = a.shape
    ns = dys[0].shape[1] // per
    assert all(d.shape == (m, per * ns) for d in dys)
    tm = _row_tile(m, tm)
    nt = m // tm

    def body(*refs):
        a_ref = refs[0]
        dy_refs = refs[1:1 + npart]
        o_ref, acc_ref = refs[1 + npart:]
        j = pl.program_id(0)
        i = pl.program_id(1)

        @pl.when(i == 0)
        def _():
            acc_ref[...] = jnp.zeros(acc_ref.shape, F32)

        for pi in range(npart):
            @pl.when(j // gpp == pi)
            def _(pi=pi):
                at = a_ref[...].astype(BF16).T
                for jj in range(spg):
                    acc_ref[jj] += _dot(at, dy_refs[pi][:, jj * ns:(jj + 1) * ns].astype(BF16))

        @pl.when(i == nt - 1)
        def _():
            o_ref[...] = acc_ref[...].astype(BF16)

    dy_specs = [
        pl.BlockSpec((tm, spg * ns),
                     lambda j, i, pi=pi: (jnp.where(j // gpp == pi, i, 0), jnp.clip(j - pi * gpp, 0, gpp - 1)))
        for pi in range(npart)
    ]
    return pl.pallas_call(
        body,
        name=name,
        grid=(nsh // spg, nt),
        in_specs=[pl.BlockSpec((tm, k), lambda j, i: (i, 0)), *dy_specs],
        out_specs=pl.BlockSpec((spg, k, ns), lambda j, i: (j, 0, 0)),
        out_shape=jax.ShapeDtypeStruct((nsh, k, ns), BF16),
        scratch_shapes=[pltpu.VMEM((spg, k, ns), F32)],
        compiler_params=_cp("parallel", "arbitrary"),
    )(a, *dys)


DEP_SPEC_SHAPE = (8, LANES)


def resid_norm_fwd(x, y, g_post, next_gains, name, tm=1024, dep=None):
    t, d = x.shape
    tm = _row_tile(t, tm)
    has_y = y is not None
    n_next = len(next_gains)
    n_dep = 0 if dep is None else 1

    def body(*refs):
        x_ref = refs[0]
        pos = 1
        if has_y:
            y_ref, gp_ref = refs[1], refs[2]
            pos = 3
        gn_refs = refs[pos:pos + n_next]
        outs = refs[pos + n_next + n_dep:]
        xv = x_ref[...]
        o = 0
        if has_y:
            yv = y_ref[...].astype(F32)
            r = lax.rsqrt(jnp.mean(yv * yv, axis=-1, keepdims=True) + EPS)
            xv = xv + (yv * r) * gp_ref[...]
            outs[0][...] = xv
            o = 1
        if n_next:
            xn = xv * lax.rsqrt(jnp.mean(xv * xv, axis=-1, keepdims=True) + EPS)
            for k in range(n_next):
                outs[o + k][...] = (xn * gn_refs[k][...]).astype(BF16)

    row = pl.BlockSpec((tm, d), lambda i: (i, 0))
    vec = pl.BlockSpec((1, d), lambda i: (0, 0))
    args, in_specs = [x], [row]
    if has_y:
        args += [y, g_post]
        in_specs += [row, vec]
    args += list(next_gains)
    in_specs += [vec] * n_next
    if n_dep:
        args.append(dep)
        in_specs.append(pl.BlockSpec(DEP_SPEC_SHAPE, lambda i: (0, 0)))
    out_shape, out_specs = [], []
    if has_y:
        out_shape.append(jax.ShapeDtypeStruct((t, d), F32))
        out_specs.append(row)
    for _ in range(n_next):
        out_shape.append(jax.ShapeDtypeStruct((t, d), BF16))
        out_specs.append(row)
    return pl.pallas_call(
        body, name=name, grid=(t // tm,), in_specs=in_specs, out_specs=out_specs, out_shape=out_shape,
        compiler_params=_cp("parallel"),
    )(*args)


def norm_bwd(x, g, dy, add=None, out_dtype=F32, name="norm_bwd", tm=1024, dep=None):
    t, d = x.shape
    tm = _row_tile(t, tm)
    has_add = add is not None

    def body(*refs):
        x_ref, g_ref, dy_ref = refs[:3]
        add_ref = refs[3] if has_add else None
        dx_ref, dg_ref, cs_ref = refs[-3:]
        i = pl.program_id(0)
        xv = x_ref[...].astype(F32)
        dyv = dy_ref[...].astype(F32)
        r = lax.rsqrt(jnp.mean(xv * xv, axis=-1, keepdims=True) + EPS)
        gd = dyv * g_ref[...]
        dx = r * gd - xv * ((r * r * r) * jnp.mean(xv * gd, axis=-1, keepdims=True))
        if has_add:
            dx = dx + add_ref[...]
        dx_ref[...] = dx.astype(out_dtype)
        dg = jnp.sum(dyv * (xv * r), axis=0, keepdims=True)
        cs = jnp.sum(dx, axis=0, keepdims=True)

        @pl.when(i == 0)
        def _():
            dg_ref[...] = dg
            cs_ref[...] = cs

        @pl.when(i > 0)
        def _():
            dg_ref[...] += dg
            cs_ref[...] += cs

    row = pl.BlockSpec((tm, d), lambda i: (i, 0))
    vec = pl.BlockSpec((1, d), lambda i: (0, 0))
    args, in_specs = [x, g, dy], [row, vec, row]
    if has_add:
        args.append(add)
        in_specs.append(row)
    if dep is not None:
        args.append(dep)
        in_specs.append(pl.BlockSpec(DEP_SPEC_SHAPE, lambda i: (0, 0)))
    return pl.pallas_call(
        body, name=name, grid=(t // tm,), in_specs=in_specs,
        out_specs=[row, vec, vec],
        out_shape=[jax.ShapeDtypeStruct((t, d), out_dtype), jax.ShapeDtypeStruct((1, d), F32),
                   jax.ShapeDtypeStruct((1, d), F32)],
        compiler_params=_cp("arbitrary"),
    )(*args)


def loss_fwd_bwd(x, target, name="loss", tm=1024):
    t, d = x.shape
    tm = _row_tile(t, tm)

    def body(x_ref, t_ref, dx_ref, l_ref):
        i = pl.program_id(0)
        err = x_ref[...] - t_ref[...]
        dx_ref[...] = err * (1.0 / d)
        part = 0.5 * jnp.sum(jnp.mean(err * err, axis=-1, keepdims=True), axis=0, keepdims=True)
        part = jnp.broadcast_to(part, l_ref.shape)

        @pl.when(i == 0)
        def _():
            l_ref[...] = part

        @pl.when(i > 0)
        def _():
            l_ref[...] += part

    row = pl.BlockSpec((tm, d), lambda i: (i, 0))
    return pl.pallas_call(
        body, name=name, grid=(t // tm,), in_specs=[row, row],
        out_specs=[row, pl.BlockSpec((8, LANES), lambda i: (0, 0))],
        out_shape=[jax.ShapeDtypeStruct((t, d), F32), jax.ShapeDtypeStruct((8, LANES), F32)],
        compiler_params=_cp("arbitrary"),
    )(x, target)


CONV_HALO = 32
CONV_CHUNK = 128


def glu_conv_fwd(z, dw, dwb, name, tc=128):
    b, s, c2 = z.shape
    c = c2 // 2
    kw = dw.shape[0]
    tc = min(tc, c)
    nc = c // tc
    ch = min(CONV_CHUNK, s)
    halo = CONV_HALO
    assert kw - 1 <= halo and s % ch == 0

    nch = s // ch

    def body(a_ref, g_ref, w_ref, b_ref, o_ref, pad_ref):
        _fill_glu_slabs(a_ref, g_ref, pad_ref, nch, ch, halo)

        def chunk(ci, carry):
            r0 = pl.multiple_of(ci * ch, ch)
            acc = b_ref[...]
            for k, tap in enumerate(_taps_front(pad_ref, ci, kw, ch, halo)):
                acc = acc + w_ref[k:k + 1, :] * tap
            o_ref[pl.ds(r0, ch), :] = acc
            return carry

        lax.fori_loop(0, nch, chunk, 0)

    return pl.pallas_call(
        body, name=name, grid=(b, nc),
        in_specs=[
            pl.BlockSpec((None, s, tc), lambda bi, i: (bi, 0, i)),
            pl.BlockSpec((None, s, tc), lambda bi, i: (bi, 0, i + nc)),
            pl.BlockSpec((kw, tc), lambda bi, i: (0, i)),
            pl.BlockSpec((1, tc), lambda bi, i: (0, i)),
        ],
        out_specs=pl.BlockSpec((None, s, tc), lambda bi, i: (bi, 0, i)),
        out_shape=jax.ShapeDtypeStruct((b, s, c), F32),
        scratch_shapes=[pltpu.VMEM((nch, ch + halo, tc), F32)],
        compiler_params=_cp("parallel", "parallel"),
    )(z, z, dw, dwb)


def glu_conv_bwd(z, dw, du2, name, tc=128):
    b, s, c2 = z.shape
    c = c2 // 2
    kw = dw.shape[0]
    tc = min(tc, c)
    nc = c // tc
    ch = min(CONV_CHUNK, s)
    nch = s // ch
    halo = CONV_HALO

    def body(a_ref, g_ref, w_ref, du_ref, dza_ref, dzg_ref, ddw_ref, ddwb_ref, dba_ref, dbg_ref, upad_ref, dpad_ref):
        bi = pl.program_id(1)

        @pl.when(bi == 0)
        def _():
            ddw_ref[...] = jnp.zeros(ddw_ref.shape, F32)
            ddwb_ref[...] = jnp.zeros(ddwb_ref.shape, F32)
            dba_ref[...] = jnp.zeros(dba_ref.shape, F32)
            dbg_ref[...] = jnp.zeros(dbg_ref.shape, F32)

        _fill_glu_slabs(a_ref, g_ref, upad_ref, nch, ch, halo)
        dpad_ref[nch - 1, ch:ch + halo, :] = jnp.zeros((halo, tc), F32)
        dpad_ref[nch - 1, 0:ch, :] = du_ref[s - ch:s, :]

        def fill(ci, carry):
            r0 = pl.multiple_of(ci * ch, ch)
            dpad_ref[ci, :, :] = du_ref[pl.ds(r0, ch + halo), :]
            return carry

        lax.fori_loop(0, nch - 1, fill, 0)

        def chunk(ci, carry):
            r0 = pl.multiple_of(ci * ch, ch)
            du_c = du_ref[pl.ds(r0, ch), :]
            taps_u = _taps_front(upad_ref, ci, kw, ch, halo)
            taps_d = _taps_at(dpad_ref, ci, list(range(kw)), ch)
            du1 = w_ref[kw - 1:kw, :] * du_c
            ddw_ref[kw - 1] += jnp.sum((du_c * taps_u[kw - 1]).reshape(ch // 8, 8, tc), axis=0)
            for j in range(1, kw):
                du1 = du1 + w_ref[kw - 1 - j:kw - j, :] * taps_d[j]
                ddw_ref[kw - 1 - j] += jnp.sum((du_c * taps_u[kw - 1 - j]).reshape(ch // 8, 8, tc), axis=0)
            av = a_ref[pl.ds(r0, ch), :]
            sg = _sigmoid(g_ref[pl.ds(r0, ch), :])
            dza = du1 * sg
            dzg = du1 * av * (sg * (1.0 - sg))
            dza_ref[pl.ds(r0, ch), :] = dza.astype(BF16)
            dzg_ref[pl.ds(r0, ch), :] = dzg.astype(BF16)
            dba_ref[...] += jnp.sum(dza, axis=0, keepdims=True)
            dbg_ref[...] += jnp.sum(dzg, axis=0, keepdims=True)
            ddwb_ref[...] += jnp.sum(du_c, axis=0, keepdims=True)
            return carry

        lax.fori_loop(0, s // ch, chunk, 0)

    blk = lambda off: pl.BlockSpec((None, s, tc), lambda i, bi: (bi, 0, i + off))
    vec = pl.BlockSpec((1, tc), lambda i, bi: (0, i))
    return pl.pallas_call(
        body, name=name, grid=(nc, b),
        in_specs=[blk(0), blk(nc), pl.BlockSpec((kw, tc), lambda i, bi: (0, i)), blk(0)],
        out_specs=[blk(0), blk(0), pl.BlockSpec((kw, 8, tc), lambda i, bi: (0, 0, i)), vec, vec, vec],
        out_shape=[
            jax.ShapeDtypeStruct((b, s, c), BF16), jax.ShapeDtypeStruct((b, s, c), BF16),
            jax.ShapeDtypeStruct((kw, 8, c), F32), jax.ShapeDtypeStruct((1, c), F32),
            jax.ShapeDtypeStruct((1, c), F32), jax.ShapeDtypeStruct((1, c), F32),
        ],
        scratch_shapes=[pltpu.VMEM((nch, ch + halo, tc), F32), pltpu.VMEM((nch, ch + halo, tc), F32)],
        compiler_params=_cp("parallel", "arbitrary"),
    )(z, z, dw, du2)


def _fill_glu_slabs(a_ref, g_ref, pad_ref, nch, ch, halo):
    tc = a_ref.shape[-1]
    pad_ref[0, 0:halo, :] = jnp.zeros((halo, tc), F32)
    pad_ref[0, halo:halo + ch, :] = a_ref[0:ch, :] * _sigmoid(g_ref[0:ch, :])

    def fill(ci, carry):
        r0 = pl.multiple_of(ci * ch, ch)
        pad_ref[ci, 0:halo, :] = pad_ref[ci - 1, ch:ch + halo, :]
        pad_ref[ci, halo:halo + ch, :] = a_ref[pl.ds(r0, ch), :] * _sigmoid(g_ref[pl.ds(r0, ch), :])
        return carry

    lax.fori_loop(1, nch, fill, 0)


def ln_silu_fwd(u, g, bvec, name, tm=1024):
    t, d = u.shape
    tm = _row_tile(t, tm)

    def body(u_ref, g_ref, b_ref, o_ref):
        uv = u_ref[...]
        mu = jnp.mean(uv, axis=-1, keepdims=True)
        xc = uv - mu
        var = jnp.mean(xc * xc, axis=-1, keepdims=True)
        v = (xc * lax.rsqrt(var + EPS)) * g_ref[...] + b_ref[...]
        o_ref[...] = (v * _sigmoid(v)).astype(BF16)

    row = pl.BlockSpec((tm, d), lambda i: (i, 0))
    vec = pl.BlockSpec((1, d), lambda i: (0, 0))
    return pl.pallas_call(
        body, name=name, grid=(t // tm,), in_specs=[row, vec, vec], out_specs=row,
        out_shape=jax.ShapeDtypeStruct((t, d), BF16), compiler_params=_cp("parallel"),
    )(u, g, bvec)


def ln_silu_bwd(u, g, bvec, dout, name, tm=1024):
    t, d = u.shape
    tm = _row_tile(t, tm)

    def body(u_ref, g_ref, b_ref, do_ref, du_ref, dg_ref, db_ref):
        i = pl.program_id(0)
        uv = u_ref[...]
        mu = jnp.mean(uv, axis=-1, keepdims=True)
        xc = uv - mu
        var = jnp.mean(xc * xc, axis=-1, keepdims=True)
        rstd = lax.rsqrt(var + EPS)
        n = xc * rstd
        v = n * g_ref[...] + b_ref[...]
        sg = _sigmoid(v)
        dv = do_ref[...].astype(F32) * (sg * (1.0 + v * (1.0 - sg)))
        dn = dv * g_ref[...]
        du_ref[...] = rstd * (dn - jnp.mean(dn, axis=-1, keepdims=True) - n * jnp.mean(dn * n, axis=-1, keepdims=True))
        dg = jnp.sum(dv * n, axis=0, keepdims=True)
        db = jnp.sum(dv, axis=0, keepdims=True)

        @pl.when(i == 0)
        def _():
            dg_ref[...] = dg
            db_ref[...] = db

        @pl.when(i > 0)
        def _():
            dg_ref[...] += dg
            db_ref[...] += db

    row = pl.BlockSpec((tm, d), lambda i: (i, 0))
    vec = pl.BlockSpec((1, d), lambda i: (0, 0))
    return pl.pallas_call(
        body, name=name, grid=(t // tm,), in_specs=[row, vec, vec, row], out_specs=[row, vec, vec],
        out_shape=[jax.ShapeDtypeStruct((t, d), F32), jax.ShapeDtypeStruct((1, d), F32), jax.ShapeDtypeStruct((1, d), F32)],
        compiler_params=_cp("arbitrary"),
    )(u, g, bvec, dout)


FFN_HALO = 8


def _fill_front_halo(src_ref, pad_ref, nch, ch, halo):
    tc = src_ref.shape[-1]
    pad_ref[0, 0:halo, :] = jnp.zeros((halo, tc), F32)
    pad_ref[0, halo:halo + ch, :] = src_ref[0:ch, :].astype(F32)

    def fill(ci, carry):
        r0 = pl.multiple_of(ci * ch, ch)
        pad_ref[ci, 0:halo, :] = src_ref[pl.ds(r0 - 2 * halo, 2 * halo), :].astype(F32)[halo:, :]
        pad_ref[ci, halo:halo + ch, :] = src_ref[pl.ds(r0, ch), :].astype(F32)
        return carry

    lax.fori_loop(1, nch, fill, 0)


def _taps_at(pad_ref, ci, offsets, ch):
    windows = {}
    for b in sorted({o % 8 for o in offsets}):
        top = max(o for o in offsets if o % 8 == b)
        windows[b] = pad_ref[ci, b:top + ch, :]
    return [windows[o % 8][o - o % 8:o - o % 8 + ch, :] for o in offsets]


def _taps_front(pad_ref, ci, kw, ch, halo):
    return _taps_at(pad_ref, ci, [halo - (kw - 1 - k) for k in range(kw)], ch)


def ffn_mid_fwd(p, dw, dwb, name, tc=256):
    b, s, f2 = p.shape
    f = f2 // 2
    kw = dw.shape[0]
    tc = min(tc, f)
    nf = f // tc
    ch = min(CONV_CHUNK, s)
    nch = s // ch
    halo = FFN_HALO

    def body(pa_ref, pg_ref, wa_ref, wg_ref, ba_ref, bg_ref, o_ref, apad_ref, gpad_ref):
        _fill_front_halo(pa_ref, apad_ref, nch, ch, halo)
        _fill_front_halo(pg_ref, gpad_ref, nch, ch, halo)

        def chunk(ci, carry):
            r0 = pl.multiple_of(ci * ch, ch)
            ca = ba_ref[...]
            cg = bg_ref[...]
            taps = zip(_taps_front(apad_ref, ci, kw, ch, halo), _taps_front(gpad_ref, ci, kw, ch, halo))
            for k, (ta, tg) in enumerate(taps):
                ca = ca + wa_ref[k:k + 1, :] * ta
                cg = cg + wg_ref[k:k + 1, :] * tg
            o_ref[pl.ds(r0, ch), :] = ((cg * _sigmoid(cg)) * ca).astype(BF16)
            return carry

        lax.fori_loop(0, nch, chunk, 0)

    blk = lambda off: pl.BlockSpec((None, s, tc), lambda bi, i: (bi, 0, i + off))
    wsp = lambda off: pl.BlockSpec((kw, tc), lambda bi, i: (0, i + off))
    bsp = lambda off: pl.BlockSpec((1, tc), lambda bi, i: (0, i + off))
    return pl.pallas_call(
        body, name=name, grid=(b, nf),
        in_specs=[blk(0), blk(nf), wsp(0), wsp(nf), bsp(0), bsp(nf)],
        out_specs=pl.BlockSpec((None, s, tc), lambda bi, i: (bi, 0, i)),
        out_shape=jax.ShapeDtypeStruct((b, s, f), BF16),
        scratch_shapes=[pltpu.VMEM((nch, ch + halo, tc), F32)] * 2,
        compiler_params=_cp("parallel", "parallel"),
    )(p, p, dw, dw, dwb, dwb)


def ffn_mid_bwd(p, dw, dwb, ds, name, tc=256):
    b, s, f2 = p.shape
    f = f2 // 2
    kw = dw.shape[0]
    tc = min(tc, f)
    nf = f // tc
    ch = min(CONV_CHUNK, s)
    nch = s // ch
    halo = FFN_HALO

    def sum8(v):
        return jnp.sum(v.reshape(ch // 8, 8, tc), axis=0)

    def body(pa_ref, pg_ref, wa_ref, wg_ref, ba_ref, bg_ref, ds_ref, dpa_ref, dpg_ref, ddwa_ref, ddwg_ref, dba_ref, dbg_ref,
             apad_ref, gpad_ref, dca_ref, dcg_ref):
        bi = pl.program_id(1)

        @pl.when(bi == 0)
        def _():
            ddwa_ref[...] = jnp.zeros(ddwa_ref.shape, F32)
            ddwg_ref[...] = jnp.zeros(ddwg_ref.shape, F32)
            dba_ref[...] = jnp.zeros(dba_ref.shape, F32)
            dbg_ref[...] = jnp.zeros(dbg_ref.shape, F32)

        _fill_front_halo(pa_ref, apad_ref, nch, ch, halo)
        _fill_front_halo(pg_ref, gpad_ref, nch, ch, halo)
        dca_ref[nch - 1, ch:ch + halo, :] = jnp.zeros((halo, tc), F32)
        dcg_ref[nch - 1, ch:ch + halo, :] = jnp.zeros((halo, tc), F32)

        def grads(ci, carry):
            acc_a, acc_g, sb_a, sb_g = carry
            r0 = pl.multiple_of(ci * ch, ch)
            taps_a = _taps_front(apad_ref, ci, kw, ch, halo)
            taps_g = _taps_front(gpad_ref, ci, kw, ch, halo)
            ca = ba_ref[...]
            cg = bg_ref[...]
            for k in range(kw):
                ca = ca + wa_ref[k:k + 1, :] * taps_a[k]
                cg = cg + wg_ref[k:k + 1, :] * taps_g[k]
            sg = _sigmoid(cg)
            dsv = ds_ref[pl.ds(r0, ch), :].astype(F32)
            dca = dsv * (cg * sg)
            dcg = dsv * ca * (sg * (1.0 + cg * (1.0 - sg)))
            dca_ref[ci, 0:ch, :] = dca
            dcg_ref[ci, 0:ch, :] = dcg

            prev = jnp.maximum(ci - 1, 0)

            @pl.when(ci > 0)
            def _():
                dca_ref[prev, ch:ch + halo, :] = dca[0:halo, :]
                dcg_ref[prev, ch:ch + halo, :] = dcg[0:halo, :]

            acc_a = tuple(acc_a[k] + sum8(dca * taps_a[k]) for k in range(kw))
            acc_g = tuple(acc_g[k] + sum8(dcg * taps_g[k]) for k in range(kw))
            return acc_a, acc_g, sb_a + sum8(dca), sb_g + sum8(dcg)

        z8 = jnp.zeros((8, tc), F32)
        acc_a, acc_g, sb_a, sb_g = lax.fori_loop(0, nch, grads, ((z8,) * kw, (z8,) * kw, z8, z8))
        for k in range(kw):
            ddwa_ref[k] += acc_a[k]
            ddwg_ref[k] += acc_g[k]
        dba_ref[...] += jnp.sum(sb_a, axis=0, keepdims=True)
        dbg_ref[...] += jnp.sum(sb_g, axis=0, keepdims=True)

        def back(ci, carry):
            r0 = pl.multiple_of(ci * ch, ch)
            da = wa_ref[kw - 1:kw, :] * dca_ref[ci, 0:ch, :]
            dg = wg_ref[kw - 1:kw, :] * dcg_ref[ci, 0:ch, :]
            for j in range(1, kw):
                da = da + wa_ref[kw - 1 - j:kw - j, :] * dca_ref[ci, j:j + ch, :]
                dg = dg + wg_ref[kw - 1 - j:kw - j, :] * dcg_ref[ci, j:j + ch, :]
            dpa_ref[pl.ds(r0, ch), :] = da.astype(BF16)
            dpg_ref[pl.ds(r0, ch), :] = dg.astype(BF16)
            return carry

        lax.fori_loop(0, nch, back, 0)

    blk = lambda off: pl.BlockSpec((None, s, tc), lambda i, bi: (bi, 0, i + off))
    wsp = lambda off: pl.BlockSpec((kw, tc), lambda i, bi: (0, i + off))
    bsp = lambda off: pl.BlockSpec((1, tc), lambda i, bi: (0, i + off))
    acc3 = pl.BlockSpec((kw, 8, tc), lambda i, bi: (0, 0, i))
    vec = pl.BlockSpec((1, tc), lambda i, bi: (0, i))
    return pl.pallas_call(
        body, name=name, grid=(nf, b),
        in_specs=[blk(0), blk(nf), wsp(0), wsp(nf), bsp(0), bsp(nf), blk(0)],
        out_specs=[blk(0), blk(0), acc3, acc3, vec, vec],
        out_shape=[jax.ShapeDtypeStruct((b, s, f), BF16), jax.ShapeDtypeStruct((b, s, f), BF16),
                   jax.ShapeDtypeStruct((kw, 8, f), F32), jax.ShapeDtypeStruct((kw, 8, f), F32),
                   jax.ShapeDtypeStruct((1, f), F32), jax.ShapeDtypeStruct((1, f), F32)],
        scratch_shapes=[pltpu.VMEM((nch, ch + halo, tc), F32)] * 4,
        compiler_params=_cp("parallel", "arbitrary"),
    )(p, p, dw, dw, dwb, dwb, ds)


def _tile_rows(r, n, dil):
    start = r + n * BLK * dil
    return pl.ds(start, BLK, stride=dil) if dil > 1 else pl.ds(start, BLK)


def _band_masks():
    qi = lax.broadcasted_iota(jnp.int32, (BLK, 2 * BLK), 0)
    kk = lax.broadcasted_iota(jnp.int32, (BLK, 2 * BLK), 1)
    both = jnp.logical_or(jnp.logical_and(kk < BLK, kk >= qi), jnp.logical_and(kk >= BLK, kk - BLK <= qi))
    return both, kk[:, :BLK] <= qi[:, :BLK]


def attn_fwd(q, kv, g, dil, hw, name):
    _, b, s, _ = q.shape
    nh = hw // HEAD_DIM
    nblk = s // dil // BLK
    scale = 1.0 / math.sqrt(HEAD_DIM)

    def body(q_ref, k_ref, v_ref, o_ref, lse_ref):
        h = pl.program_id(1)
        mask2, mask1 = _band_masks()
        mine = lax.broadcasted_iota(jnp.int32, (BLK, LANES), 1) == h

        @pl.when(h == 0)
        def _():
            lse_ref[...] = jnp.zeros(lse_ref.shape, F32)

        for r in range(dil):
            kp = vp = None
            for n in range(nblk):
                rs = _tile_rows(r, n, dil)
                qt = q_ref[rs, :].astype(BF16)
                kc = k_ref[rs, :].astype(BF16)
                vc = v_ref[rs, :].astype(BF16)
                if n == 0:
                    kcat, vcat, mask = kc, vc, mask1
                else:
                    kcat, vcat, mask = jnp.concatenate([kp, kc], axis=0), jnp.concatenate([vp, vc], axis=0), mask2
                sc = jnp.where(mask, _dot_nt(qt, kcat) * scale, NEG_INF)
                m = jnp.max(sc, axis=-1, keepdims=True)
                p = jnp.exp(sc - m)
                den = jnp.sum(p, axis=-1, keepdims=True)
                o_ref[rs, :] = _dot(p.astype(BF16), vcat) / den
                lse_ref[rs, :] = jnp.where(mine, m + jnp.log(den), lse_ref[rs, :])
                kp, vp = kc, vc

    col = lambda base: pl.BlockSpec((None, s, HEAD_DIM), lambda bi, h: (bi, 0, base + h))
    head = lambda base: pl.BlockSpec((None, None, s, HEAD_DIM), lambda bi, h: (base + h, bi, 0, 0))
    return pl.pallas_call(
        body, name=name, grid=(b, nh),
        in_specs=[head(g * nh), head(g * nh), head((N_GROUPS + g) * nh)],
        out_specs=[col(0), pl.BlockSpec((None, s, LANES), lambda bi, h: (bi, 0, 0))],
        out_shape=[jax.ShapeDtypeStruct((b, s, hw), F32), jax.ShapeDtypeStruct((b, s, LANES), F32)],
        compiler_params=_cp("parallel", "arbitrary"),
    )(q, kv, kv)


def attn_merge(outs, lses, name, tm=512):
    t, hw = outs[0].shape
    nh = hw // HEAD_DIM
    tm = _row_tile(t, tm)
    ng = len(outs)

    def body(*refs):
        o_refs, l_refs = refs[:ng], refs[ng:2 * ng]
        m_ref, lj_ref = refs[2 * ng:]
        ls = [l_refs[g][...] for g in range(ng)]
        mx = ls[0]
        for g in range(1, ng):
            mx = jnp.maximum(mx, ls[g])
        es = [jnp.exp(l - mx) for l in ls]
        tot = es[0]
        for g in range(1, ng):
            tot = tot + es[g]
        ws = [e / tot for e in es]
        lj_ref[...] = mx + jnp.log(tot)
        for h in range(nh):
            sl = slice(h * HEAD_DIM, (h + 1) * HEAD_DIM)
            acc = ws[0][:, h:h + 1] * o_refs[0][:, sl]
            for g in range(1, ng):
                acc = acc + ws[g][:, h:h + 1] * o_refs[g][:, sl]
            m_ref[:, sl] = acc.astype(BF16)

    row = pl.BlockSpec((tm, hw), lambda i: (i, 0))
    st = pl.BlockSpec((tm, LANES), lambda i: (i, 0))
    return pl.pallas_call(
        body, name=name, grid=(t // tm,), in_specs=[row] * ng + [st] * ng, out_specs=[row, st],
        out_shape=[jax.ShapeDtypeStruct((t, hw), BF16), jax.ShapeDtypeStruct((t, LANES), F32)],
        compiler_params=_cp("parallel"),
    )(*outs, *lses)


def attn_bwd_prep(dmerged, merged, name, tm=512):
    t, hw = merged.shape
    nh = hw // HEAD_DIM
    tm = _row_tile(t, tm)

    def body(d_ref, m_ref, o_ref):
        lane = lax.broadcasted_iota(jnp.int32, (tm, LANES), 1)
        acc = jnp.zeros((tm, LANES), F32)
        for h in range(nh):
            sl = slice(h * HEAD_DIM, (h + 1) * HEAD_DIM)
            dsum = jnp.sum(d_ref[:, sl] * m_ref[:, sl].astype(F32), axis=-1, keepdims=True)
            acc = jnp.where(lane == h, dsum, acc)
        o_ref[...] = acc

    row = pl.BlockSpec((tm, hw), lambda i: (i, 0))
    return pl.pallas_call(
        body, name=name, grid=(t // tm,), in_specs=[row, row], out_specs=pl.BlockSpec((tm, LANES), lambda i: (i, 0)),
        out_shape=jax.ShapeDtypeStruct((t, LANES), F32), compiler_params=_cp("parallel"),
    )(dmerged, merged)


def attn_bwd(q, kv, g, dil, do, lsej, dm, dq_buf, dk_buf, dv_buf, accumulate, hw, name):
    _, b, s, _ = q.shape
    nh = hw // HEAD_DIM
    nblk = s // dil // BLK
    scale = 1.0 / math.sqrt(HEAD_DIM)
    assert dk_buf is not None or not accumulate
    kv_at = 6 + (dq_buf is not None)

    def body(*refs):
        q_ref, k_ref, v_ref, do_ref, lj_ref, dm_ref = refs[:6]
        dq_ref, dk_ref, dv_ref = refs[-3:]
        dki_ref, dvi_ref = (refs[kv_at], refs[kv_at + 1]) if accumulate else (None, None)
        mask2, mask1 = _band_masks()
        mine = lax.broadcasted_iota(jnp.int32, (BLK, LANES), 1) == pl.program_id(1)

        def my_lane(v):
            return jnp.sum(jnp.where(mine, v, 0.0), axis=-1, keepdims=True)

        def put(rs, dk, dv):
            if accumulate:
                dk = dk + dki_ref[rs, :]
                dv = dv + dvi_ref[rs, :]
            dk_ref[rs, :] = dk
            dv_ref[rs, :] = dv

        for r in range(dil):
            kp = vp = hold_k = hold_v = rs_prev = None
            for n in range(nblk):
                rs = _tile_rows(r, n, dil)
                qt = q_ref[rs, :].astype(BF16)
                kc = k_ref[rs, :].astype(BF16)
                vc = v_ref[rs, :].astype(BF16)
                dot = do_ref[rs, :].astype(BF16)
                lm = my_lane(lj_ref[rs, :])
                dmm = my_lane(dm_ref[rs, :])
                if n == 0:
                    kcat, vcat, mask = kc, vc, mask1
                else:
                    kcat, vcat, mask = jnp.concatenate([kp, kc], axis=0), jnp.concatenate([vp, vc], axis=0), mask2
                p = jnp.exp(jnp.where(mask, _dot_nt(qt, kcat) * scale, NEG_INF) - lm)
                ds = (p * (_dot_nt(dot, vcat) - dmm)).astype(BF16)
                dq_ref[rs, :] = _dot(ds, kcat) * scale
                dkc = _dot_tn(ds, qt) * scale
                dvc = _dot_tn(p.astype(BF16), dot)
                if n > 0:
                    put(rs_prev, hold_k + dkc[:BLK, :], hold_v + dvc[:BLK, :])
                    dkc, dvc = dkc[BLK:, :], dvc[BLK:, :]
                hold_k, hold_v, kp, vp, rs_prev = dkc, dvc, kc, vc, rs
            put(rs_prev, hold_k, hold_v)

    col = lambda base: pl.BlockSpec((None, s, HEAD_DIM), lambda bi, h: (bi, 0, base + h))
    any_spec = pl.BlockSpec(memory_space=pl.ANY)
    stat = pl.BlockSpec((None, s, LANES), lambda bi, h: (bi, 0, 0))
    head = lambda base: pl.BlockSpec((None, None, s, HEAD_DIM), lambda bi, h: (base + h, bi, 0, 0))
    in_specs = [head(g * nh), head(g * nh), head((N_GROUPS + g) * nh), col(0), stat, stat]
    args = [q, kv, kv, do, lsej, dm]
    aliases = {}
    if dq_buf is not None:
        in_specs.append(any_spec)
        args.append(dq_buf)
        aliases[6] = 0
    if dk_buf is not None:
        in_specs += [col(g * nh) if accumulate else any_spec] * 2
        args += [dk_buf, dv_buf]
        aliases.update({kv_at: 1, kv_at + 1: 2})
    shape = jax.ShapeDtypeStruct((b, s, N_GROUPS * hw), F32)
    return pl.pallas_call(
        body, name=name, grid=(b, nh), in_specs=in_specs, out_specs=[col(g * nh)] * 3, out_shape=[shape] * 3,
        input_output_aliases=aliases, compiler_params=_cp("parallel", "parallel"),
    )(*args)


def sum_parts(g, recv, me, name, tm=512):
    _, rows, c = g.shape
    n = recv.shape[0]
    tm = _row_tile(rows, tm)

    def body(me_ref, g_ref, r_ref, o_ref):
        acc = g_ref[...].astype(F32)
        for j in range(n):
            acc = acc + r_ref[j].astype(F32)
        o_ref[...] = acc

    return pl.pallas_call(
        body, name=name,
        grid_spec=pltpu.PrefetchScalarGridSpec(
            num_scalar_prefetch=1, grid=(rows // tm,),
            in_specs=[pl.BlockSpec((None, tm, c), lambda i, me_ref: (me_ref[0], i, 0)),
                      pl.BlockSpec((n, tm, c), lambda i, me_ref: (0, i, 0))],
            out_specs=pl.BlockSpec((tm, c), lambda i, me_ref: (i, 0))),
        out_shape=jax.ShapeDtypeStruct((rows, c), F32), compiler_params=_cp("parallel"),
    )(me, g, recv)


def adamw(w, m, v, g_parts, name, tm=256):
    rows, c = w.shape
    tm = _row_tile(rows, tm)
    npart = len(g_parts)

    def body(*refs):
        w_ref, m_ref, v_ref = refs[:3]
        g_refs = refs[3:3 + npart]
        go_ref, d_ref, mo_ref, vo_ref = refs[3 + npart:]
        g = g_refs[0][...]
        for k in range(1, npart):
            g = g + g_refs[k][...]
        mn = ADAM_B1 * m_ref[...] + (1.0 - ADAM_B1) * g
        vn = ADAM_B2 * v_ref[...] + (1.0 - ADAM_B2) * (g * g)
        m_hat = mn / (1.0 - ADAM_B1 ** ADAM_STEP)
        v_hat = vn / (1.0 - ADAM_B2 ** ADAM_STEP)
        go_ref[...] = g
        d_ref[...] = -ADAM_LR * (m_hat / (jnp.sqrt(v_hat) + ADAM_EPS) + ADAM_WD * w_ref[...])
        mo_ref[...] = mn
        vo_ref[...] = vn

    row = pl.BlockSpec((tm, c), lambda i: (i, 0))
    return pl.pallas_call(
        body, name=name, grid=(rows // tm,), in_specs=[row] * (3 + npart), out_specs=[row] * 4,
        out_shape=[jax.ShapeDtypeStruct((rows, c), F32)] * 4, compiler_params=_cp("parallel"),
    )(w, m, v, *g_parts)


def _place():
    return lax.axis_index("x"), lax.axis_index("y"), lax.axis_index("c")


def _other_chips(x, y, c):
    return [(1 - x, y, c), (x, 1 - y, c), (1 - x, 1 - y, c)]


def _chip_of(px, py):
    return 2 * px + py


HBM_SPEC = pl.BlockSpec(memory_space=pltpu.HBM)
SEM_SPEC = pl.BlockSpec(memory_space=pltpu.SEMAPHORE)
ANY_SPEC = pl.BlockSpec(memory_space=pl.ANY)
DATAFLOW = pltpu.SideEffectType.DATAFLOW_SIDE_EFFECTING
N_PEER_CHIPS = N_CHIPS - 1


def _hbm(a):
    return pltpu.with_memory_space_constraint(a, pltpu.HBM)


def _hbm_like(arrays):
    return [pltpu.HBM(a.shape, a.dtype) for a in arrays]


def cast_place(w, layer, me, out_dtype, name, tm=512, nslots=N_CHIPS, dep=None):
    rows, c = w.shape[-2:]
    tm = _row_tile(rows, tm)

    def body(me_ref, w_ref, *rest):
        rest[-1][...] = w_ref[...].astype(out_dtype)

    if layer is None:
        in_specs = [pl.BlockSpec((tm, c), lambda i, me_ref: (i, 0))]
    else:
        in_specs = [pl.BlockSpec((None, tm, c), lambda i, me_ref: (layer, i, 0))]
    args = [me, w]
    if dep is not None:
        in_specs.append(pl.BlockSpec(DEP_SPEC_SHAPE, lambda i, me_ref: (0, 0)))
        args.append(dep)
    return pl.pallas_call(
        body, name=name,
        grid_spec=pltpu.PrefetchScalarGridSpec(
            num_scalar_prefetch=1, grid=(rows // tm,), in_specs=in_specs,
            out_specs=pl.BlockSpec((None, tm, c), lambda i, me_ref: (me_ref[0], i, 0))),
        out_shape=jax.ShapeDtypeStruct((nslots, rows, c), out_dtype), compiler_params=_cp("parallel"),
    )(*args)


def gather_start(lands, chunk_sizes, name="gather_start"):
    n = len(lands)
    nch = len(chunk_sizes)
    assert sum(chunk_sizes) == n

    def body(*refs):
        land_refs = refs[:n]
        outs = refs[n:]
        send_sems, recv_sems = outs[:nch], outs[nch:2 * nch]
        token = outs[-1]
        x, y, c = _place()
        me = _chip_of(x, y)
        peers = _other_chips(x, y, c)
        k = 0
        for ck, size in enumerate(chunk_sizes):
            for pos in range(size):
                for r, peer in enumerate(peers):
                    pltpu.make_async_remote_copy(
                        src_ref=land_refs[k].at[me], dst_ref=land_refs[k].at[me],
                        send_sem=send_sems[ck].at[N_PEER_CHIPS * pos + r], recv_sem=recv_sems[ck].at[N_PEER_CHIPS * pos + r],
                        device_id=peer, device_id_type=MESH).start()
                k += 1
        token[...] = jnp.zeros(token.shape, F32)

    sems = [pltpu.SemaphoreType.DMA((N_PEER_CHIPS * s,)) for s in chunk_sizes]
    res = pl.pallas_call(
        body, name=name,
        out_shape=(*sems, *sems, *_hbm_like(lands), jax.ShapeDtypeStruct(DEP_SPEC_SHAPE, F32)),
        in_specs=[HBM_SPEC] * n,
        out_specs=(*[SEM_SPEC] * (2 * nch), *[HBM_SPEC] * n, pl.BlockSpec(memory_space=pltpu.VMEM)),
        input_output_aliases={k: 2 * nch + k for k in range(n)},
        compiler_params=pltpu.CompilerParams(has_side_effects=DATAFLOW),
    )(*[_hbm(a) for a in lands])
    return res[:nch], res[nch:2 * nch], res[2 * nch:2 * nch + n], res[-1]


def gather_wait(send_sem, recv_sem, lands, after, name):
    n = len(lands)

    def body(*refs):
        land_refs = refs[:n]
        ssem, rsem = refs[n], refs[n + 1]
        x, y, c = _place()
        me = _chip_of(x, y)
        for pos in range(n):
            for r, peer in enumerate(_other_chips(x, y, c)):
                cp = pltpu.make_async_remote_copy(
                    src_ref=land_refs[pos].at[me], dst_ref=land_refs[pos].at[_chip_of(peer[0], peer[1])],
                    send_sem=ssem.at[N_PEER_CHIPS * pos + r], recv_sem=rsem.at[N_PEER_CHIPS * pos + r],
                    device_id=peer, device_id_type=MESH)
                cp.wait_send()
                cp.wait_recv()

    return pl.pallas_call(
        body, name=name, out_shape=tuple(_hbm_like(lands)),
        in_specs=[*[HBM_SPEC] * n, SEM_SPEC, SEM_SPEC, ANY_SPEC], out_specs=[HBM_SPEC] * n,
        input_output_aliases={k: k for k in range(n)},
        compiler_params=pltpu.CompilerParams(has_side_effects=DATAFLOW),
    )(*lands, send_sem, recv_sem, after)


def scatter_start(grads, name):
    n = len(grads)
    recvs = [lax.empty((N_PEER_CHIPS, *g.shape[1:]), g.dtype) for g in grads]

    def body(*refs):
        g_refs, r_refs = refs[:n], refs[n:2 * n]
        send_sems, recv_sems = refs[2 * n], refs[2 * n + 1]
        token = refs[-1]
        x, y, c = _place()
        for k in range(n):
            for r, peer in enumerate(_other_chips(x, y, c)):
                pltpu.make_async_remote_copy(
                    src_ref=g_refs[k].at[_chip_of(peer[0], peer[1])], dst_ref=r_refs[k].at[r],
                    send_sem=send_sems.at[N_PEER_CHIPS * k + r], recv_sem=recv_sems.at[N_PEER_CHIPS * k + r],
                    device_id=peer, device_id_type=MESH).start()
        token[...] = jnp.zeros(token.shape, F32)

    sem = pltpu.SemaphoreType.DMA((N_PEER_CHIPS * n,))
    res = pl.pallas_call(
        body, name=name,
        out_shape=(sem, sem, *_hbm_like(grads), *_hbm_like(recvs), jax.ShapeDtypeStruct(DEP_SPEC_SHAPE, F32)),
        in_specs=[HBM_SPEC] * (2 * n),
        out_specs=(SEM_SPEC, SEM_SPEC, *[HBM_SPEC] * (2 * n), pl.BlockSpec(memory_space=pltpu.VMEM)),
        input_output_aliases={k: 2 + k for k in range(2 * n)},
        compiler_params=pltpu.CompilerParams(has_side_effects=DATAFLOW),
    )(*[_hbm(a) for a in grads], *[_hbm(a) for a in recvs])
    return res[0], res[1], res[2:2 + n], res[2 + n:2 + 2 * n], res[-1]


def scatter_wait(send_sem, recv_sem, grads, recvs, after, name):
    n = len(grads)

    def body(*refs):
        g_refs, r_refs = refs[:n], refs[n:2 * n]
        ssem, rsem = refs[2 * n], refs[2 * n + 1]
        x, y, c = _place()
        for k in range(n):
            for r, peer in enumerate(_other_chips(x, y, c)):
                cp = pltpu.make_async_remote_copy(
                    src_ref=g_refs[k].at[_chip_of(peer[0], peer[1])], dst_ref=r_refs[k].at[r],
                    send_sem=ssem.at[N_PEER_CHIPS * k + r], recv_sem=rsem.at[N_PEER_CHIPS * k + r],
                    device_id=peer, device_id_type=MESH)
                cp.wait_send()
                cp.wait_recv()

    res = pl.pallas_call(
        body, name=name, out_shape=(*_hbm_like(grads), *_hbm_like(recvs)),
        in_specs=[*[HBM_SPEC] * (2 * n), SEM_SPEC, SEM_SPEC, ANY_SPEC], out_specs=[HBM_SPEC] * (2 * n),
        input_output_aliases={k: k for k in range(2 * n)},
        compiler_params=pltpu.CompilerParams(has_side_effects=DATAFLOW),
    )(*grads, *recvs, send_sem, recv_sem, after)
    return res[:n], res[n:]


def swap_start(parts, name):
    n = len(parts)
    lands = [lax.empty(p.shape, p.dtype) for p in parts]

    def body(*refs):
        p_refs, l_refs = refs[:n], refs[n:2 * n]
        sems = refs[2 * n:4 * n]
        token = refs[-1]
        x, y, c = _place()
        for k in range(n):
            pltpu.make_async_remote_copy(
                src_ref=p_refs[k], dst_ref=l_refs[k], send_sem=sems[k], recv_sem=sems[n + k],
                device_id=(x, y, 1 - c), device_id_type=MESH).start()
        token[...] = jnp.zeros(token.shape, F32)

    sem = pltpu.SemaphoreType.DMA(())
    res = pl.pallas_call(
        body, name=name,
        out_shape=(*[sem] * (2 * n), *_hbm_like(parts), *_hbm_like(lands), jax.ShapeDtypeStruct(DEP_SPEC_SHAPE, F32)),
        in_specs=[HBM_SPEC] * (2 * n),
        out_specs=(*[SEM_SPEC] * (2 * n), *[HBM_SPEC] * (2 * n), pl.BlockSpec(memory_space=pltpu.VMEM)),
        input_output_aliases={k: 2 * n + k for k in range(2 * n)},
        compiler_params=pltpu.CompilerParams(has_side_effects=DATAFLOW),
    )(*[_hbm(a) for a in parts], *[_hbm(a) for a in lands])
    return res[:n], res[n:2 * n], res[2 * n:3 * n], res[3 * n:4 * n], res[-1]


def swap_wait(send_sem, recv_sem, part, land, after, name):
    def body(p_ref, l_ref, ssem, rsem, after_ref, p_out, l_out):
        x, y, c = _place()
        cp = pltpu.make_async_remote_copy(src_ref=p_ref, dst_ref=l_ref, send_sem=ssem, recv_sem=rsem,
                                          device_id=(x, y, 1 - c), device_id_type=MESH)
        cp.wait_send()
        cp.wait_recv()

    return pl.pallas_call(
        body, name=name, out_shape=tuple(_hbm_like([part, land])),
        in_specs=[HBM_SPEC, HBM_SPEC, SEM_SPEC, SEM_SPEC, ANY_SPEC], out_specs=[HBM_SPEC, HBM_SPEC],
        input_output_aliases={0: 0, 1: 1},
        compiler_params=pltpu.CompilerParams(has_side_effects=DATAFLOW),
    )(part, land, send_sem, recv_sem, after)


def _xor_peer(x, y, c, k):
    px, py, pc = x ^ ((k >> 2) & 1), y ^ ((k >> 1) & 1), c ^ (k & 1)
    return (px, py, pc), 4 * px + 2 * py + pc


def small_start(land, name="small_start"):
    def body(l_ref, ssem, rsem, l_out, token):
        x, y, c = _place()
        me = 4 * x + 2 * y + c
        for k in range(1, N_DEV):
            peer, _ = _xor_peer(x, y, c, k)
            pltpu.make_async_remote_copy(
                src_ref=l_ref.at[me], dst_ref=l_ref.at[me], send_sem=ssem.at[k - 1], recv_sem=rsem.at[k - 1],
                device_id=peer, device_id_type=MESH).start()
        token[...] = jnp.zeros(token.shape, F32)

    sem = pltpu.SemaphoreType.DMA((N_DEV - 1,))
    return pl.pallas_call(
        body, name=name,
        out_shape=(sem, sem, pltpu.HBM(land.shape, land.dtype), jax.ShapeDtypeStruct(DEP_SPEC_SHAPE, F32)),
        in_specs=[HBM_SPEC], out_specs=(SEM_SPEC, SEM_SPEC, HBM_SPEC, pl.BlockSpec(memory_space=pltpu.VMEM)),
        input_output_aliases={0: 2}, compiler_params=pltpu.CompilerParams(has_side_effects=DATAFLOW),
    )(_hbm(land))


def small_wait(send_sem, recv_sem, land, after, name="small_wait"):
    def body(l_ref, ssem, rsem, after_ref, l_out):
        x, y, c = _place()
        me = 4 * x + 2 * y + c
        for k in range(1, N_DEV):
            peer, slot = _xor_peer(x, y, c, k)
            cp = pltpu.make_async_remote_copy(
                src_ref=l_ref.at[me], dst_ref=l_ref.at[slot], send_sem=ssem.at[k - 1], recv_sem=rsem.at[k - 1],
                device_id=peer, device_id_type=MESH)
            cp.wait_send()
            cp.wait_recv()

    return pl.pallas_call(
        body, name=name, out_shape=pltpu.HBM(land.shape, land.dtype),
        in_specs=[HBM_SPEC, SEM_SPEC, SEM_SPEC, ANY_SPEC], out_specs=HBM_SPEC, input_output_aliases={0: 0},
        compiler_params=pltpu.CompilerParams(has_side_effects=DATAFLOW),
    )(land, send_sem, recv_sem, after)


def sum_slots(land, name="sum_slots", tm=256):
    n, rows, c = land.shape
    tm = _row_tile(rows, tm)

    def body(l_ref, o_ref):
        acc = l_ref[0]
        for j in range(1, n):
            acc = acc + l_ref[j]
        o_ref[...] = acc

    return pl.pallas_call(
        body, name=name, grid=(rows // tm,), in_specs=[pl.BlockSpec((n, tm, c), lambda i: (0, i, 0))],
        out_specs=pl.BlockSpec((tm, c), lambda i: (i, 0)), out_shape=jax.ShapeDtypeStruct((rows, c), F32),
        compiler_params=_cp("parallel"),
    )(land)


PACK_ROW_TILE = 256


def _pack(arrays):
    flat = jnp.concatenate([a.reshape(-1).astype(F32) for a in arrays])
    n = flat.shape[0]
    rows = -(-n // LANES)
    rows = -(-rows // PACK_ROW_TILE) * PACK_ROW_TILE
    return jnp.pad(flat, (0, rows * LANES - n)).reshape(rows, LANES)


def _unpack(packed, shapes, lead=()):
    flat = packed.reshape(*lead, -1)
    out, off = [], 0
    for shp in shapes:
        n = math.prod(shp)
        out.append(flat[..., off:off + n].reshape(*lead, *shp))
        off += n
    return out


def _row(vec):
    return vec.reshape(1, -1)


def kernel(x, mix_pre_g, mix_post_g, ffn_pre_g, ffn_post_g, cm_w_in, cm_b_in, cm_dw, cm_dw_b, cm_ln_g, cm_ln_b, cm_w_out, cm_b_out, kv_norm_g, w_kv, w_q, w_o, ffn_w_in, ffn_dw, ffn_dw_b, ffn_w_out, loss_target, m_mix_pre_g, m_mix_post_g, m_ffn_pre_g, m_ffn_post_g, m_cm_w_in, m_cm_b_in, m_cm_dw, m_cm_dw_b, m_cm_ln_g, m_cm_ln_b, m_cm_w_out, m_cm_b_out, m_kv_norm_g, m_w_kv, m_w_q, m_w_o, m_ffn_w_in, m_ffn_dw, m_ffn_dw_b, m_ffn_w_out, v_mix_pre_g, v_mix_post_g, v_ffn_pre_g, v_ffn_post_g, v_cm_w_in, v_cm_b_in, v_cm_dw, v_cm_dw_b, v_cm_ln_g, v_cm_ln_b, v_cm_w_out, v_cm_b_out, v_kv_norm_g, v_w_kv, v_w_q, v_w_o, v_ffn_w_in, v_ffn_dw, v_ffn_dw_b, v_ffn_w_out):
    names = ["mix_pre_g", "mix_post_g", "ffn_pre_g", "ffn_post_g", "cm_w_in", "cm_b_in", "cm_dw", "cm_dw_b", "cm_ln_g",
             "cm_ln_b", "cm_w_out", "cm_b_out", "kv_norm_g", "w_kv", "w_q", "w_o", "ffn_w_in", "ffn_dw", "ffn_dw_b",
             "ffn_w_out"]
    w_in = dict(zip(names, [mix_pre_g, mix_post_g, ffn_pre_g, ffn_post_g, cm_w_in, cm_b_in, cm_dw, cm_dw_b, cm_ln_g,
                            cm_ln_b, cm_w_out, cm_b_out, kv_norm_g, w_kv, w_q, w_o, ffn_w_in, ffn_dw, ffn_dw_b, ffn_w_out]))
    m_in = dict(zip(names, [m_mix_pre_g, m_mix_post_g, m_ffn_pre_g, m_ffn_post_g, m_cm_w_in, m_cm_b_in, m_cm_dw, m_cm_dw_b,
                            m_cm_ln_g, m_cm_ln_b, m_cm_w_out, m_cm_b_out, m_kv_norm_g, m_w_kv, m_w_q, m_w_o, m_ffn_w_in,
                            m_ffn_dw, m_ffn_dw_b, m_ffn_w_out]))
    v_in = dict(zip(names, [v_mix_pre_g, v_mix_post_g, v_ffn_pre_g, v_ffn_post_g, v_cm_w_in, v_cm_b_in, v_cm_dw, v_cm_dw_b,
                            v_cm_ln_g, v_cm_ln_b, v_cm_w_out, v_cm_b_out, v_kv_norm_g, v_w_kv, v_w_q, v_w_o, v_ffn_w_in,
                            v_ffn_dw, v_ffn_dw_b, v_ffn_w_out]))

    bsz, seq, d = x.shape
    t = bsz * seq
    n_b = DEPTH - N_A
    hw = w_o.shape[-1]
    qw = N_GROUPS * hw
    f2 = ffn_dw_b.shape[-1]
    f = f2 // 2
    me_chip = _chip_of(lax.axis_index("x"), lax.axis_index("y"))

    big = ["cm_w_in", "cm_w_out", "w_kv", "w_q", "w_o", "ffn_w_in", "ffn_w_out"]
    row_sharded = ("cm_w_out", "w_o", "ffn_w_out")
    small_sharded = ["cm_b_in", "cm_dw", "cm_dw_b", "cm_ln_g", "cm_ln_b", "cm_b_out", "ffn_dw"]
    small_pack = _pack([w_in[n] for n in small_sharded])
    chunks = [
        [("cm_w_in", 0), ("small", None)],
        [("cm_w_out", 0)],
        [("ffn_w_in", 0), ("ffn_w_out", 0)],
        [("cm_w_in", 1), ("cm_w_out", 1)],
        [("ffn_w_in", 1), ("ffn_w_out", 1)],
        [("w_kv", None)],
        [("w_q", 0), ("w_o", 0)],
        [("ffn_w_in", 2), ("ffn_w_out", 2)],
        [("w_q", 1), ("w_o", 1)],
        [("ffn_w_in", 3), ("ffn_w_out", 3)],
    ]
    pieces = [pc for ch in chunks for pc in ch]
    chunk_of = {pc: ck for ck, ch in enumerate(chunks) for pc in ch}

    me_arr = me_chip.astype(jnp.int32).reshape(1)

    def land_of(pc, dep=None):
        n, l = pc
        if n == "small":
            return cast_place(small_pack, None, me_arr, F32, name="place_small", dep=dep)
        return cast_place(w_in[n], l, me_arr, BF16, name=f"place_{n}_{l}", dep=dep)

    groups = [[0, 1], [2], list(range(3, len(chunks)))]
    g_send, g_recv, lands_f = {}, {}, {}

    def start_group(gi, dep, name):
        cks = groups[gi]
        pcs = [pc for ck in cks for pc in chunks[ck]]
        lands = [land_of(pc, dep if k == 0 else None) for k, pc in enumerate(pcs)]
        send, recv, lands_thru, tok = gather_start(lands, [len(chunks[ck]) for ck in cks], name=name)
        pos = 0
        for j, ck in enumerate(cks):
            g_send[ck], g_recv[ck] = send[j], recv[j]
            lands_f[ck] = lands_thru[pos:pos + len(chunks[ck])]
            pos += len(chunks[ck])
        return tok

    token_a = start_group(0, None, "gather_start_a")
    token = start_group(1, token_a, "gather_start_b")
    weights = {}

    def finish_chunk(ck, after):
        got = gather_wait(g_send[ck], g_recv[ck], lands_f[ck], after, name=f"gather_wait{ck}")
        for pc, arr in zip(chunks[ck], got):
            weights[pc] = arr.reshape(1, -1, arr.shape[-1]) if pc[0] in row_sharded else arr

    def wmat(n, l=None, after=None):
        if (n, l) not in weights:
            finish_chunk(chunk_of[(n, l)], after)
        arr = weights[(n, l)]
        return arr, arr.shape[0]

    finish_chunk(0, token)
    small_full = {}
    for n, arr4 in zip(small_sharded, _unpack(weights[("small", None)], [w_in[n].shape for n in small_sharded], lead=(N_CHIPS,))):
        shp = w_in[n].shape
        small_full[n] = jnp.moveaxis(arr4, 0, -2).reshape(*shp[:-1], N_CHIPS * shp[-1])

    x2d = x.reshape(t, d)
    saved = []
    (h1,) = resid_norm_fwd(x2d, None, None, [_row(mix_pre_g[0])], name="norm_in", dep=token)
    xcur = x2d
    kv_state = None
    for i in range(DEPTH):
        sv = {"x_in": xcur, "h1": h1}
        if i < N_A:
            z = mm_nn(h1, *wmat("cm_w_in", i, h1), 1, 0, bias=_row(small_full["cm_b_in"][i]), name=f"cm_in{i}",
                      spg=N_CHIPS)
            u2 = glu_conv_fwd(z.reshape(bsz, seq, 2 * d), small_full["cm_dw"][i], _row(small_full["cm_dw_b"][i]),
                              name=f"glu_conv{i}").reshape(t, d)
            u4 = ln_silu_fwd(u2, _row(small_full["cm_ln_g"][i]), _row(small_full["cm_ln_b"][i]), name=f"ln_silu{i}")
            y = mm_nn(u4, *wmat("cm_w_out", i, u4), 1, 0, bias=_row(small_full["cm_b_out"][i]), out_dtype=BF16,
                      name=f"cm_out{i}")
            sv.update(z=z, u2=u2, u4=u4)
        else:
            j = i - N_A
            q = mm_nn(h1, *wmat("w_q", j, h1), 1, 0, name=f"q_proj{j}", head_major=True, spg=N_CHIPS).reshape(-1, bsz, seq, HEAD_DIM)
            outs, lses = [], []
            for g, dil in enumerate(DILATIONS):
                o_g, l_g = attn_fwd(q, kv_state["kv"], g, dil, hw, name=f"attn_fwd{j}_{g}")
                outs.append(o_g.reshape(t, hw))
                lses.append(l_g.reshape(t, LANES))
            merged, lsej = attn_merge(outs, lses, name=f"attn_merge{j}")
            y = mm_nn(merged, *wmat("w_o", j, merged), 1, 0, out_dtype=BF16, name=f"o_proj{j}")
            sv.update(q=q, merged=merged, lsej=lsej)
        x1, h2 = resid_norm_fwd(xcur, y, _row(mix_post_g[i]), [_row(ffn_pre_g[i])], name=f"resid_mix{i}")
        ffn_after = start_group(2, x1, "gather_start_c") if i == 0 else h2
        p = mm_nn(h2, *wmat("ffn_w_in", i, ffn_after), 1, 0, out_dtype=BF16, name=f"ffn_in{i}", spg=2)
        s_act = ffn_mid_fwd(p.reshape(bsz, seq, f2), small_full["ffn_dw"][i], _row(ffn_dw_b[i]), name=f"ffn_mid{i}").reshape(t, f)
        y2 = mm_nn(s_act, *wmat("ffn_w_out", i), 1, 0, out_dtype=BF16, name=f"ffn_out{i}")
        next_gains = []
        if i + 1 < DEPTH:
            next_gains.append(_row(mix_pre_g[i + 1]))
        if i == N_A - 1:
            next_gains.append(_row(kv_norm_g))
        res = resid_norm_fwd(x1, y2, _row(ffn_post_g[i]), next_gains, name=f"resid_ffn{i}")
        sv.update(y=y, x1=x1, h2=h2, p=p, s=s_act, y2=y2)
        saved.append(sv)
        xcur = res[0]
        if i + 1 < DEPTH:
            h1 = res[1]
        if i == N_A - 1:
            kvn = res[2]
            kv = mm_nn(kvn, *wmat("w_kv", None, kvn), 1, 0, name="kv_proj", head_major=True, spg=2).reshape(-1, bsz, seq, HEAD_DIM)
            kv_state = {"kv": kv, "kvn": kvn, "x_a": xcur}

    dx, loss_tile = loss_fwd_bwd(xcur, loss_target.reshape(t, d))
    loss = lax.psum(loss_tile[0, 0], ("x", "y", "c"))

    gsm = {n: [None] * w_in[n].shape[0] for n in
           ["mix_pre_g", "mix_post_g", "ffn_pre_g", "ffn_post_g", "cm_b_in", "cm_dw", "cm_dw_b", "cm_ln_g", "cm_ln_b",
            "cm_b_out", "ffn_dw", "ffn_dw_b"]}
    gbig = {}
    in_flight = []
    dep = None

    def start_scatter(pcs, tag):
        ssem, rsem, g_f, r_f, tok = scatter_start([gbig[pc] for pc in pcs], name=f"scatter_start_{tag}")
        in_flight.append((pcs, ssem, rsem, g_f, r_f))
        return tok

    dk_buf = dv_buf = None
    for i in range(DEPTH - 1, -1, -1):
        sv = saved[i]
        dy2, dg, _ = norm_bwd(sv["y2"], _row(ffn_post_g[i]), dx, out_dtype=BF16, name=f"bwd_ffn_post{i}", dep=dep)
        gsm["ffn_post_g"][i] = dg
        ds = mm_nt(dy2, *wmat("ffn_w_out", i), 1, 0, out_dtype=BF16, name=f"bwd_ffn_out_dx{i}")
        gbig[("ffn_w_out", i)] = mm_tn(sv["s"], dy2, 1, name=f"bwd_ffn_out_dw{i}").reshape(N_CHIPS, f // N_CHIPS, d)
        dpa, dpg, ddwa, ddwg, ddba, ddbg = ffn_mid_bwd(sv["p"].reshape(bsz, seq, f2), small_full["ffn_dw"][i], _row(ffn_dw_b[i]),
                                                       ds.reshape(bsz, seq, f), name=f"bwd_ffn_mid{i}")
        gsm["ffn_dw"][i] = jnp.concatenate([jnp.sum(ddwa, axis=1), jnp.sum(ddwg, axis=1)], axis=-1)
        gsm["ffn_dw_b"][i] = jnp.concatenate([ddba, ddbg], axis=-1)
        dp = [dpa.reshape(t, f), dpg.reshape(t, f)]
        dh2 = mm_nt(dp, *wmat("ffn_w_in", i), 1, 0, name=f"bwd_ffn_in_dx{i}", spg=2)
        gbig[("ffn_w_in", i)] = mm_tn(sv["h2"], dp, N_CHIPS, name=f"bwd_ffn_in_dw{i}", spg=2)
        dx1, dg, _ = norm_bwd(sv["x1"], _row(ffn_pre_g[i]), dh2, add=dx, name=f"bwd_ffn_pre{i}")
        gsm["ffn_pre_g"][i] = dg
        dep = start_scatter([("ffn_w_in", 0), ("ffn_w_out", 0)], "ffn0") if i == 0 else None
        dy, dg, dbias = norm_bwd(sv["y"], _row(mix_post_g[i]), dx1, out_dtype=BF16, name=f"bwd_mix_post{i}", dep=dep)
        gsm["mix_post_g"][i] = dg
        if i < N_A:
            gsm["cm_b_out"][i] = dbias
            du4 = mm_nt(dy, *wmat("cm_w_out", i), 1, 0, name=f"bwd_cm_out_dx{i}")
            gbig[("cm_w_out", i)] = mm_tn(sv["u4"], dy, 1, name=f"bwd_cm_out_dw{i}").reshape(N_CHIPS, d // N_CHIPS, d)
            du2, dlg, dlb = ln_silu_bwd(sv["u2"], _row(small_full["cm_ln_g"][i]), _row(small_full["cm_ln_b"][i]), du4,
                                        name=f"bwd_ln_silu{i}")
            gsm["cm_ln_g"][i], gsm["cm_ln_b"][i] = dlg, dlb
            dza, dzg, ddw, ddwb, dba, dbg = glu_conv_bwd(sv["z"].reshape(bsz, seq, 2 * d), small_full["cm_dw"][i],
                                                         du2.reshape(bsz, seq, d), name=f"bwd_glu_conv{i}")
            gsm["cm_dw"][i] = jnp.sum(ddw, axis=1)
            gsm["cm_dw_b"][i] = ddwb
            gsm["cm_b_in"][i] = jnp.concatenate([dba, dbg], axis=-1)
            dz = [dza.reshape(t, d), dzg.reshape(t, d)]
            dh1 = mm_nt(dz, *wmat("cm_w_in", i), 1, 0, name=f"bwd_cm_in_dx{i}", spg=2)
            gbig[("cm_w_in", i)] = mm_tn(sv["h1"], dz, N_CHIPS, name=f"bwd_cm_in_dw{i}", spg=2)
        else:
            j = i - N_A
            dmerged = mm_nt(dy, *wmat("w_o", j), 1, 0, name=f"bwd_o_proj_dx{j}")
            gbig[("w_o", j)] = mm_tn(sv["merged"], dy, 1, name=f"bwd_o_proj_dw{j}").reshape(N_CHIPS, hw // N_CHIPS, d)
            dmt = attn_bwd_prep(dmerged, sv["merged"], name=f"bwd_attn_prep{j}")
            dq_buf = None
            add_to_kv = dk_buf is not None
            for g, dil in enumerate(DILATIONS):
                dq_buf, dk_buf, dv_buf = attn_bwd(
                    sv["q"], kv_state["kv"], g, dil, dmerged.reshape(bsz, seq, hw), sv["lsej"].reshape(bsz, seq, LANES),
                    dmt.reshape(bsz, seq, LANES), dq_buf, dk_buf, dv_buf, add_to_kv, hw, name=f"attn_bwd{j}_{g}")
            dq = dq_buf.reshape(t, qw)
            dh1 = mm_nt(dq, *wmat("w_q", j), 1, 0, name=f"bwd_q_proj_dx{j}", tm=512, spg=N_CHIPS)
            gbig[("w_q", j)] = mm_tn(sv["h1"], dq, N_CHIPS, name=f"bwd_q_proj_dw{j}", tm=512, spg=N_CHIPS)
        dx, dg, _ = norm_bwd(sv["x_in"], _row(mix_pre_g[i]), dh1, add=dx1, name=f"bwd_mix_pre{i}")
        gsm["mix_pre_g"][i] = dg
        if i > N_A:
            dep = start_scatter([("ffn_w_in", i), ("ffn_w_out", i), ("w_q", i - N_A), ("w_o", i - N_A)], f"l{i}")
        elif 0 < i < N_A:
            dep = start_scatter([("ffn_w_in", i), ("ffn_w_out", i), ("cm_w_in", i), ("cm_w_out", i)], f"l{i}")
        elif i == 0:
            last_token = start_scatter([("cm_w_in", 0), ("cm_w_out", 0)], "cm0")
        if i == N_A:
            dkv = [dk_buf.reshape(t, qw), dv_buf.reshape(t, qw)]
            dkvn = mm_nt(dkv, *wmat("w_kv"), 1, 0, name="bwd_kv_proj_dx", tm=512, spg=2)
            gbig[("w_kv", None)] = mm_tn(kv_state["kvn"], dkv, N_CHIPS, name="bwd_kv_proj_dw")
            dx, dg_kv, _ = norm_bwd(kv_state["x_a"], _row(kv_norm_g), dkvn, add=dx, name="bwd_kv_norm")
            dep = start_scatter([("ffn_w_in", i), ("ffn_w_out", i), ("w_q", 0), ("w_o", 0), ("w_kv", None)], f"l{i}")
    grad_x = dx.reshape(bsz, seq, d)

    plane_of = {}
    outs_g, outs_d, outs_m, outs_v = {}, {}, {}, {}

    def finish_scatter(k, after):
        pcs, ssem, rsem, g_f, r_f = in_flight[k]
        g_done, r_done = scatter_wait(ssem, rsem, g_f, r_f, after, name=f"scatter_wait{k}")
        for pc, g_arr, r_arr in zip(pcs, g_done, r_done):
            plane_of[pc] = sum_parts(g_arr, r_arr, me_arr, name=f"sum_chips_{pc[0]}_{pc[1]}")

    def update(group, tag, after):
        plane = []
        for n in group:
            if w_in[n].ndim == 2:
                plane.append(plane_of[(n, None)])
            else:
                plane.append(jnp.concatenate([plane_of[(n, l)] for l in range(w_in[n].shape[0])], axis=0))
        ssems, rsems, plane_f, land_f, _ = swap_start(plane, name=f"swap_start_{tag}")
        for k, n in enumerate(group):
            p_mine, p_other = swap_wait(ssems[k], rsems[k], plane_f[k], land_f[k], after, name=f"swap_wait_{n}")
            shp = w_in[n].shape
            flat = lambda a: a.reshape(-1, shp[-1])
            g_, d_, m_, v_ = adamw(flat(w_in[n]), flat(m_in[n]), flat(v_in[n]), [p_mine, p_other], name=f"adamw_{n}")
            outs_g[n], outs_d[n], outs_m[n], outs_v[n] = (a.reshape(shp) for a in (g_, d_, m_, v_))
            after = v_
        return after

    small_names = [n for n in names if n not in big]
    small_shapes_full = {}
    small_grads_full = []
    for n in small_names:
        if n == "kv_norm_g":
            gfull = dg_kv.reshape(-1)
        elif n in ("cm_dw", "ffn_dw"):
            gfull = jnp.stack(gsm[n], axis=0)
        else:
            gfull = jnp.stack([a.reshape(-1) for a in gsm[n]], axis=0)
        small_shapes_full[n] = gfull.shape
        small_grads_full.append(gfull)
    dev_arr = (4 * lax.axis_index("x") + 2 * lax.axis_index("y") + lax.axis_index("c")).astype(jnp.int32).reshape(1)
    small_land = cast_place(_pack(small_grads_full) + last_token[0, 0], None, dev_arr, F32, name="place_small_grads",
                            nslots=N_DEV)
    sm_send, sm_recv, small_land, small_token = small_start(small_land)

    for k in range(len(in_flight) - 1):
        finish_scatter(k, small_token)
    done = update(["w_kv", "w_q", "w_o", "ffn_w_in", "ffn_w_out"], "a", small_token)
    finish_scatter(len(in_flight) - 1, done)
    done = update(["cm_w_in", "cm_w_out"], "b", done)

    summed = sum_slots(small_wait(sm_send, sm_recv, small_land, done))
    g_full = dict(zip(small_names, _unpack(summed, [small_shapes_full[n] for n in small_names])))
    g_loc = {}
    for n in small_names:
        if n in small_sharded:
            width = w_in[n].shape[-1]
            g_loc[n] = lax.dynamic_slice_in_dim(g_full[n], me_chip * width, width, axis=g_full[n].ndim - 1)
        else:
            g_loc[n] = g_full[n]
    res = adamw(_pack([w_in[n] for n in small_names]), _pack([m_in[n] for n in small_names]),
                _pack([v_in[n] for n in small_names]), [_pack([g_loc[n] for n in small_names])], name="adamw_small")
    shapes_loc = [w_in[n].shape for n in small_names]
    for dst, packed in zip((outs_g, outs_d, outs_m, outs_v), res):
        for n, a in zip(small_names, _unpack(packed, shapes_loc)):
            dst[n] = a

    return (loss, grad_x, *[outs_g[n] for n in names], *[outs_d[n] for n in names],
            *[outs_m[n] for n in names], *[outs_v[n] for n in names])
```

```python
import functools
import math

import jax
import jax.numpy as jnp
from jax import lax
from jax.experimental import pallas as pl
from jax.experimental.pallas import tpu as pltpu

F32 = jnp.float32
BF16 = jnp.bfloat16
EPS = 1e-6
NEG_INF = -1e30
N_A = 2
DEPTH = 4
N_GROUPS = 3
DILATIONS = (1, 4, 16)
HEAD_DIM = 128
BLK = 128
LANES = 128
N_CHIPS = 4
N_DEV = 8
VMEM_LIMIT_V7X = 56 * 1024 * 1024

ADAM_LR = 0.001
ADAM_B1 = 0.9
ADAM_B2 = 0.999
ADAM_EPS = 1e-08
ADAM_WD = 0.01
ADAM_STEP = 10

MESH = pl.DeviceIdType.MESH


def _cp(*sem, **kw):
    return pltpu.CompilerParams(dimension_semantics=sem if sem else None, vmem_limit_bytes=VMEM_LIMIT_V7X, **kw)


def _dot(a, b):
    return jnp.dot(a, b, preferred_element_type=F32)


def _dot_nt(a, b):
    return lax.dot_general(a, b, (((1,), (1,)), ((), ())), preferred_element_type=F32)


def _dot_tn(a, b):
    return lax.dot_general(a, b, (((0,), (0,)), ((), ())), preferred_element_type=F32)


def _sigmoid(x):
    return 1.0 / (1.0 + jnp.exp(-x))


def _row_tile(n, want):
    if n <= want:
        return n
    for t in range(want - want % 8, 7, -8):
        if n % t == 0:
            return t
    raise ValueError(f"no row tile for {n} rows")


def mm_nn(a, w, nsh, stride, layer, bias=None, out_dtype=F32, name="mm_nn", tm=1024, head_major=False, spg=1):
    assert stride == 1 and layer == 0 and nsh % spg == 0
    m, k = a.shape
    _, k2, ns = w.shape
    assert k == k2
    tm = _row_tile(m, tm)
    has_bias = bias is not None
    hps = ns // HEAD_DIM

    def body(*refs):
        if has_bias:
            a_ref, w_ref, b_ref, o_ref = refs
        else:
            a_ref, w_ref, o_ref = refs
        av = a_ref[...].astype(BF16)
        for jj in range(spg):
            acc = _dot(av, w_ref[jj])
            if has_bias:
                acc = acc + b_ref[:, jj * ns:(jj + 1) * ns]
            if head_major:
                for hh in range(hps):
                    o_ref[jj * hps + hh] = acc[:, hh * HEAD_DIM:(hh + 1) * HEAD_DIM].astype(out_dtype)
            else:
                o_ref[:, jj * ns:(jj + 1) * ns] = acc.astype(out_dtype)

    in_specs = [
        pl.BlockSpec((tm, k), lambda j, i: (i, 0)),
        pl.BlockSpec((spg, k, ns), lambda j, i: (j, 0, 0)),
    ]
    args = [a, w]
    if has_bias:
        in_specs.append(pl.BlockSpec((1, spg * ns), lambda j, i: (0, j)))
        args.append(bias)
    return pl.pallas_call(
        body,
        name=name,
        grid=(nsh // spg, m // tm),
        in_specs=in_specs,
        out_specs=(pl.BlockSpec((spg * hps, tm, HEAD_DIM), lambda j, i: (j, i, 0)) if head_major
                   else pl.BlockSpec((tm, spg * ns), lambda j, i: (i, j))),
        out_shape=jax.ShapeDtypeStruct((nsh * hps, m, HEAD_DIM) if head_major else (m, nsh * ns), out_dtype),
        compiler_params=_cp("parallel", "parallel"),
    )(*args)


def _split_parts(dy, nsh, spg):
    dys = list(dy) if isinstance(dy, (list, tuple)) else [dy]
    per = nsh // len(dys)
    assert per % spg == 0
    return dys, per, per // spg


def mm_nt(dy, w, nsh, stride, layer, out_dtype=F32, name="mm_nt", tm=1024, spg=1):
    assert stride == 1 and layer == 0
    dys, per, gpp = _split_parts(dy, nsh, spg)
    npart = len(dys)
    m = dys[0].shape[0]
    _, k, ns = w.shape
    assert all(d.shape == (m, per * ns) for d in dys)
    tm = _row_tile(m, tm)
    ngrp = nsh // spg

    def body(*refs):
        dy_refs = refs[:npart]
        w_ref, o_ref, acc_ref = refs[npart:]
        j = pl.program_id(1)

        @pl.when(j == 0)
        def _():
            acc_ref[...] = jnp.zeros(acc_ref.shape, F32)

        for pi in range(npart):
            @pl.when(j // gpp == pi)
            def _(pi=pi):
                part = _dot_nt(dy_refs[pi][:, 0:ns].astype(BF16), w_ref[0])
                for jj in range(1, spg):
                    part = part + _dot_nt(dy_refs[pi][:, jj * ns:(jj + 1) * ns].astype(BF16), w_ref[jj])
                acc_ref[...] += part

        @pl.when(j == ngrp - 1)
        def _():
            o_ref[...] = acc_ref[...].astype(out_dtype)

    dy_specs = [pl.BlockSpec((tm, spg * ns), lambda i, j, pi=pi: (i, jnp.clip(j - pi * gpp, 0, gpp - 1)))
                for pi in range(npart)]
    return pl.pallas_call(
        body,
        name=name,
        grid=(m // tm, ngrp),
        in_specs=[*dy_specs, pl.BlockSpec((spg, k, ns), lambda i, j: (j, 0, 0))],
        out_specs=pl.BlockSpec((tm, k), lambda i, j: (i, 0)),
        out_shape=jax.ShapeDtypeStruct((m, k), out_dtype),
        scratch_shapes=[pltpu.VMEM((tm, k), F32)],
        compiler_params=_cp("parallel", "arbitrary"),
    )(*dys, w)


def mm_tn(a, dy, nsh, name="mm_tn", tm=1024, spg=1):
    dys, per, gpp = _split_parts(dy, nsh, spg)
    npart = len(dys)
    m, k = a.shape
    ns = dys[0].shape[1] // per
    assert all(d.shape == (m, per * ns) for d in dys)
    tm = _row_tile(m, tm)
    nt = m // tm

    def body(*refs):
        a_ref = refs[0]
        dy_refs = refs[1:1 + npart]
        o_ref, acc_ref = refs[1 + npart:]
        j = pl.program_id(0)
        i = pl.program_id(1)

        @pl.when(i == 0)
        def _():
            acc_ref[...] = jnp.zeros(acc_ref.shape, F32)

        for pi in range(npart):
            @pl.when(j // gpp == pi)
            def _(pi=pi):
                at = a_ref[...].astype(BF16).T
                for jj in range(spg):
                    acc_ref[jj] += _dot(at, dy_refs[pi][:, jj * ns:(jj + 1) * ns].astype(BF16))

        @pl.when(i == nt - 1)
        def _():
            o_ref[...] = acc_ref[...].astype(BF16)

    dy_specs = [
        pl.BlockSpec((tm, spg * ns),
                     lambda j, i, pi=pi: (jnp.where(j // gpp == pi, i, 0), jnp.clip(j - pi * gpp, 0, gpp - 1)))
        for pi in range(npart)
    ]
    return pl.pallas_call(
        body,
        name=name,
        grid=(nsh // spg, nt),
        in_specs=[pl.BlockSpec((tm, k), lambda j, i: (i, 0)), *dy_specs],
        out_specs=pl.BlockSpec((spg, k, ns), lambda j, i: (j, 0, 0)),
        out_shape=jax.ShapeDtypeStruct((nsh, k, ns), BF16),
        scratch_shapes=[pltpu.VMEM((spg, k, ns), F32)],
        compiler_params=_cp("parallel", "arbitrary"),
    )(a, *dys)


DEP_SPEC_SHAPE = (8, LANES)


def resid_norm_fwd(x, y, g_post, next_gains, name, tm=1024, dep=None):
    t, d = x.shape
    tm = _row_tile(t, tm)
    has_y = y is not None
    n_next = len(next_gains)
    n_dep = 0 if dep is None else 1

    def body(*refs):
        x_ref = refs[0]
        pos = 1
        if has_y:
            y_ref, gp_ref = refs[1], refs[2]
            pos = 3
        gn_refs = refs[pos:pos + n_next]
        outs = refs[pos + n_next + n_dep:]
        xv = x_ref[...]
        o = 0
        if has_y:
            yv = y_ref[...].astype(F32)
            r = lax.rsqrt(jnp.mean(yv * yv, axis=-1, keepdims=True) + EPS)
            xv = xv + (yv * r) * gp_ref[...]
            outs[0][...] = xv
            o = 1
        if n_next:
            xn = xv * lax.rsqrt(jnp.mean(xv * xv, axis=-1, keepdims=True) + EPS)
            for k in range(n_next):
                outs[o + k][...] = (xn * gn_refs[k][...]).astype(BF16)

    row = pl.BlockSpec((tm, d), lambda i: (i, 0))
    vec = pl.BlockSpec((1, d), lambda i: (0, 0))
    args, in_specs = [x], [row]
    if has_y:
        args += [y, g_post]
        in_specs += [row, vec]
    args += list(next_gains)
    in_specs += [vec] * n_next
    if n_dep:
        args.append(dep)
        in_specs.append(pl.BlockSpec(DEP_SPEC_SHAPE, lambda i: (0, 0)))
    out_shape, out_specs = [], []
    if has_y:
        out_shape.append(jax.ShapeDtypeStruct((t, d), F32))
        out_specs.append(row)
    for _ in range(n_next):
        out_shape.append(jax.ShapeDtypeStruct((t, d), BF16))
        out_specs.append(row)
    return pl.pallas_call(
        body, name=name, grid=(t // tm,), in_specs=in_specs, out_specs=out_specs, out_shape=out_shape,
        compiler_params=_cp("parallel"),
    )(*args)


def norm_bwd(x, g, dy, add=None, out_dtype=F32, name="norm_bwd", tm=1024, dep=None):
    t, d = x.shape
    tm = _row_tile(t, tm)
    has_add = add is not None

    def body(*refs):
        x_ref, g_ref, dy_ref = refs[:3]
        add_ref = refs[3] if has_add else None
        dx_ref, dg_ref, cs_ref = refs[-3:]
        i = pl.program_id(0)
        xv = x_ref[...].astype(F32)
        dyv = dy_ref[...].astype(F32)
        r = lax.rsqrt(jnp.mean(xv * xv, axis=-1, keepdims=True) + EPS)
        gd = dyv * g_ref[...]
        dx = r * gd - xv * ((r * r * r) * jnp.mean(xv * gd, axis=-1, keepdims=True))
        if has_add:
            dx = dx + add_ref[...]
        dx_ref[...] = dx.astype(out_dtype)
        dg = jnp.sum(dyv * (xv * r), axis=0, keepdims=True)
        cs = jnp.sum(dx, axis=0, keepdims=True)

        @pl.when(i == 0)
        def _():
            dg_ref[...] = dg
            cs_ref[...] = cs

        @pl.when(i > 0)
        def _():
            dg_ref[...] += dg
            cs_ref[...] += cs

    row = pl.BlockSpec((tm, d), lambda i: (i, 0))
    vec = pl.BlockSpec((1, d), lambda i: (0, 0))
    args, in_specs = [x, g, dy], [row, vec, row]
    if has_add:
        args.append(add)
        in_specs.append(row)
    if dep is not None:
        args.append(dep)
        in_specs.append(pl.BlockSpec(DEP_SPEC_SHAPE, lambda i: (0, 0)))
    return pl.pallas_call(
        body, name=name, grid=(t // tm,), in_specs=in_specs,
        out_specs=[row, vec, vec],
        out_shape=[jax.ShapeDtypeStruct((t, d), out_dtype), jax.ShapeDtypeStruct((1, d), F32),
                   jax.ShapeDtypeStruct((1, d), F32)],
        compiler_params=_cp("arbitrary"),
    )(*args)


def loss_fwd_bwd(x, target, name="loss", tm=1024):
    t, d = x.shape
    tm = _row_tile(t, tm)

    def body(x_ref, t_ref, dx_ref, l_ref):
        i = pl.program_id(0)
        err = x_ref[...] - t_ref[...]
        dx_ref[...] = err * (1.0 / d)
        part = 0.5 * jnp.sum(jnp.mean(err * err, axis=-1, keepdims=True), axis=0, keepdims=True)
        part = jnp.broadcast_to(part, l_ref.shape)

        @pl.when(i == 0)
        def _():
            l_ref[...] = part

        @pl.when(i > 0)
        def _():
            l_ref[...] += part

    row = pl.BlockSpec((tm, d), lambda i: (i, 0))
    return pl.pallas_call(
        body, name=name, grid=(t // tm,), in_specs=[row, row],
        out_specs=[row, pl.BlockSpec((8, LANES), lambda i: (0, 0))],
        out_shape=[jax.ShapeDtypeStruct((t, d), F32), jax.ShapeDtypeStruct((8, LANES), F32)],
        compiler_params=_cp("arbitrary"),
    )(x, target)


CONV_HALO = 32
CONV_CHUNK = 128


def glu_conv_fwd(z, dw, dwb, name, tc=128):
    b, s, c2 = z.shape
    c = c2 // 2
    kw = dw.shape[0]
    tc = min(tc, c)
    nc = c // tc
    ch = min(CONV_CHUNK, s)
    halo = CONV_HALO
    assert kw - 1 <= halo and s % ch == 0

    nch = s // ch

    def body(a_ref, g_ref, w_ref, b_ref, o_ref, pad_ref):
        _fill_glu_slabs(a_ref, g_ref, pad_ref, nch, ch, halo)

        def chunk(ci, carry):
            r0 = pl.multiple_of(ci * ch, ch)
            acc = b_ref[...]
            for k, tap in enumerate(_taps_front(pad_ref, ci, kw, ch, halo)):
                acc = acc + w_ref[k:k + 1, :] * tap
            o_ref[pl.ds(r0, ch), :] = acc
            return carry

        lax.fori_loop(0, nch, chunk, 0)

    return pl.pallas_call(
        body, name=name, grid=(b, nc),
        in_specs=[
            pl.BlockSpec((None, s, tc), lambda bi, i: (bi, 0, i)),
            pl.BlockSpec((None, s, tc), lambda bi, i: (bi, 0, i + nc)),
            pl.BlockSpec((kw, tc), lambda bi, i: (0, i)),
            pl.BlockSpec((1, tc), lambda bi, i: (0, i)),
        ],
        out_specs=pl.BlockSpec((None, s, tc), lambda bi, i: (bi, 0, i)),
        out_shape=jax.ShapeDtypeStruct((b, s, c), F32),
        scratch_shapes=[pltpu.VMEM((nch, ch + halo, tc), F32)],
        compiler_params=_cp("parallel", "parallel"),
    )(z, z, dw, dwb)


def glu_conv_bwd(z, dw, du2, name, tc=128):
    b, s, c2 = z.shape
    c = c2 // 2
    kw = dw.shape[0]
    tc = min(tc, c)
    nc = c // tc
    ch = min(CONV_CHUNK, s)
    nch = s // ch
    halo = CONV_HALO

    def body(a_ref, g_ref, w_ref, du_ref, dza_ref, dzg_ref, ddw_ref, ddwb_ref, dba_ref, dbg_ref, upad_ref, dpad_ref):
        bi = pl.program_id(1)

        @pl.when(bi == 0)
        def _():
            ddw_ref[...] = jnp.zeros(ddw_ref.shape, F32)
            ddwb_ref[...] = jnp.zeros(ddwb_ref.shape, F32)
            dba_ref[...] = jnp.zeros(dba_ref.shape, F32)
            dbg_ref[...] = jnp.zeros(dbg_ref.shape, F32)

        _fill_glu_slabs(a_ref, g_ref, upad_ref, nch, ch, halo)
        dpad_ref[nch - 1, ch:ch + halo, :] = jnp.zeros((halo, tc), F32)
        dpad_ref[nch - 1, 0:ch, :] = du_ref[s - ch:s, :]

        def fill(ci, carry):
            r0 = pl.multiple_of(ci * ch, ch)
            dpad_ref[ci, :, :] = du_ref[pl.ds(r0, ch + halo), :]
            return carry

        lax.fori_loop(0, nch - 1, fill, 0)

        def chunk(ci, carry):
            r0 = pl.multiple_of(ci * ch, ch)
            du_c = du_ref[pl.ds(r0, ch), :]
            taps_u = _taps_front(upad_ref, ci, kw, ch, halo)
            taps_d = _taps_at(dpad_ref, ci, list(range(kw)), ch)
            du1 = w_ref[kw - 1:kw, :] * du_c
            ddw_ref[kw - 1] += jnp.sum((du_c * taps_u[kw - 1]).reshape(ch // 8, 8, tc), axis=0)
            for j in range(1, kw):
                du1 = du1 + w_ref[kw - 1 - j:kw - j, :] * taps_d[j]
                ddw_ref[kw - 1 - j] += jnp.sum((du_c * taps_u[kw - 1 - j]).reshape(ch // 8, 8, tc), axis=0)
            av = a_ref[pl.ds(r0, ch), :]
            sg = _sigmoid(g_ref[pl.ds(r0, ch), :])
            dza = du1 * sg
            dzg = du1 * av * (sg * (1.0 - sg))
            dza_ref[pl.ds(r0, ch), :] = dza.astype(BF16)
            dzg_ref[pl.ds(r0, ch), :] = dzg.astype(BF16)
            dba_ref[...] += jnp.sum(dza, axis=0, keepdims=True)
            dbg_ref[...] += jnp.sum(dzg, axis=0, keepdims=True)
            ddwb_ref[...] += jnp.sum(du_c, axis=0, keepdims=True)
            return carry

        lax.fori_loop(0, s // ch, chunk, 0)

    blk = lambda off: pl.BlockSpec((None, s, tc), lambda i, bi: (bi, 0, i + off))
    vec = pl.BlockSpec((1, tc), lambda i, bi: (0, i))
    return pl.pallas_call(
        body, name=name, grid=(nc, b),
        in_specs=[blk(0), blk(nc), pl.BlockSpec((kw, tc), lambda i, bi: (0, i)), blk(0)],
        out_specs=[blk(0), blk(0), pl.BlockSpec((kw, 8, tc), lambda i, bi: (0, 0, i)), vec, vec, vec],
        out_shape=[
            jax.ShapeDtypeStruct((b, s, c), BF16), jax.ShapeDtypeStruct((b, s, c), BF16),
            jax.ShapeDtypeStruct((kw, 8, c), F32), jax.ShapeDtypeStruct((1, c), F32),
            jax.ShapeDtypeStruct((1, c), F32), jax.ShapeDtypeStruct((1, c), F32),
        ],
        scratch_shapes=[pltpu.VMEM((nch, ch + halo, tc), F32), pltpu.VMEM((nch, ch + halo, tc), F32)],
        compiler_params=_cp("parallel", "arbitrary"),
    )(z, z, dw, du2)


def _fill_glu_slabs(a_ref, g_ref, pad_ref, nch, ch, halo):
    tc = a_ref.shape[-1]
    pad_ref[0, 0:halo, :] = jnp.zeros((halo, tc), F32)
    pad_ref[0, halo:halo + ch, :] = a_ref[0:ch, :] * _sigmoid(g_ref[0:ch, :])

    def fill(ci, carry):
        r0 = pl.multiple_of(ci * ch, ch)
        pad_ref[ci, 0:halo, :] = pad_ref[ci - 1, ch:ch + halo, :]
        pad_ref[ci, halo:halo + ch, :] = a_ref[pl.ds(r0, ch), :] * _sigmoid(g_ref[pl.ds(r0, ch), :])
        return carry

    lax.fori_loop(1, nch, fill, 0)


def ln_silu_fwd(u, g, bvec, name, tm=1024):
    t, d = u.shape
    tm = _row_tile(t, tm)

    def body(u_ref, g_ref, b_ref, o_ref):
        uv = u_ref[...]
        mu = jnp.mean(uv, axis=-1, keepdims=True)
        xc = uv - mu
        var = jnp.mean(xc * xc, axis=-1, keepdims=True)
        v = (xc * lax.rsqrt(var + EPS)) * g_ref[...] + b_ref[...]
        o_ref[...] = (v * _sigmoid(v)).astype(BF16)

    row = pl.BlockSpec((tm, d), lambda i: (i, 0))
    vec = pl.BlockSpec((1, d), lambda i: (0, 0))
    return pl.pallas_call(
        body, name=name, grid=(t // tm,), in_specs=[row, vec, vec], out_specs=row,
        out_shape=jax.ShapeDtypeStruct((t, d), BF16), compiler_params=_cp("parallel"),
    )(u, g, bvec)


def ln_silu_bwd(u, g, bvec, dout, name, tm=1024):
    t, d = u.shape
    tm = _row_tile(t, tm)

    def body(u_ref, g_ref, b_ref, do_ref, du_ref, dg_ref, db_ref):
        i = pl.program_id(0)
        uv = u_ref[...]
        mu = jnp.mean(uv, axis=-1, keepdims=True)
        xc = uv - mu
        var = jnp.mean(xc * xc, axis=-1, keepdims=True)
        rstd = lax.rsqrt(var + EPS)
        n = xc * rstd
        v = n * g_ref[...] + b_ref[...]
        sg = _sigmoid(v)
        dv = do_ref[...].astype(F32) * (sg * (1.0 + v * (1.0 - sg)))
        dn = dv * g_ref[...]
        du_ref[...] = rstd * (dn - jnp.mean(dn, axis=-1, keepdims=True) - n * jnp.mean(dn * n, axis=-1, keepdims=True))
        dg = jnp.sum(dv * n, axis=0, keepdims=True)
        db = jnp.sum(dv, axis=0, keepdims=True)

        @pl.when(i == 0)
        def _():
            dg_ref[...] = dg
            db_ref[...] = db

        @pl.when(i > 0)
        def _():
            dg_ref[...] += dg
            db_ref[...] += db

    row = pl.BlockSpec((tm, d), lambda i: (i, 0))
    vec = pl.BlockSpec((1, d), lambda i: (0, 0))
    return pl.pallas_call(
        body, name=name, grid=(t // tm,), in_specs=[row, vec, vec, row], out_specs=[row, vec, vec],
        out_shape=[jax.ShapeDtypeStruct((t, d), F32), jax.ShapeDtypeStruct((1, d), F32), jax.ShapeDtypeStruct((1, d), F32)],
        compiler_params=_cp("arbitrary"),
    )(u, g, bvec, dout)


FFN_HALO = 8


def _fill_front_halo(src_ref, pad_ref, nch, ch, halo):
    tc = src_ref.shape[-1]
    pad_ref[0, 0:halo, :] = jnp.zeros((halo, tc), F32)
    pad_ref[0, halo:halo + ch, :] = src_ref[0:ch, :].astype(F32)

    def fill(ci, carry):
        r0 = pl.multiple_of(ci * ch, ch)
        pad_ref[ci, 0:halo, :] = src_ref[pl.ds(r0 - 2 * halo, 2 * halo), :].astype(F32)[halo:, :]
        pad_ref[ci, halo:halo + ch, :] = src_ref[pl.ds(r0, ch), :].astype(F32)
        return carry

    lax.fori_loop(1, nch, fill, 0)


def _taps_at(pad_ref, ci, offsets, ch):
    windows = {}
    for b in sorted({o % 8 for o in offsets}):
        top = max(o for o in offsets if o % 8 == b)
        windows[b] = pad_ref[ci, b:top + ch, :]
    return [windows[o % 8][o - o % 8:o - o % 8 + ch, :] for o in offsets]


def _taps_front(pad_ref, ci, kw, ch, halo):
    return _taps_at(pad_ref, ci, [halo - (kw - 1 - k) for k in range(kw)], ch)


def ffn_mid_fwd(p, dw, dwb, name, tc=256):
    b, s, f2 = p.shape
    f = f2 // 2
    kw = dw.shape[0]
    tc = min(tc, f)
    nf = f // tc
    ch = min(CONV_CHUNK, s)
    nch = s // ch
    halo = FFN_HALO

    def body(pa_ref, pg_ref, wa_ref, wg_ref, ba_ref, bg_ref, o_ref, apad_ref, gpad_ref):
        _fill_front_halo(pa_ref, apad_ref, nch, ch, halo)
        _fill_front_halo(pg_ref, gpad_ref, nch, ch, halo)

        def chunk(ci, carry):
            r0 = pl.multiple_of(ci * ch, ch)
            ca = ba_ref[...]
            cg = bg_ref[...]
            taps = zip(_taps_front(apad_ref, ci, kw, ch, halo), _taps_front(gpad_ref, ci, kw, ch, halo))
            for k, (ta, tg) in enumerate(taps):
                ca = ca + wa_ref[k:k + 1, :] * ta
                cg = cg + wg_ref[k:k + 1, :] * tg
            o_ref[pl.ds(r0, ch), :] = ((cg * _sigmoid(cg)) * ca).astype(BF16)
            return carry

        lax.fori_loop(0, nch, chunk, 0)

    blk = lambda off: pl.BlockSpec((None, s, tc), lambda bi, i: (bi, 0, i + off))
    wsp = lambda off: pl.BlockSpec((kw, tc), lambda bi, i: (0, i + off))
    bsp = lambda off: pl.BlockSpec((1, tc), lambda bi, i: (0, i + off))
    return pl.pallas_call(
        body, name=name, grid=(b, nf),
        in_specs=[blk(0), blk(nf), wsp(0), wsp(nf), bsp(0), bsp(nf)],
        out_specs=pl.BlockSpec((None, s, tc), lambda bi, i: (bi, 0, i)),
        out_shape=jax.ShapeDtypeStruct((b, s, f), BF16),
        scratch_shapes=[pltpu.VMEM((nch, ch + halo, tc), F32)] * 2,
        compiler_params=_cp("parallel", "parallel"),
    )(p, p, dw, dw, dwb, dwb)


def ffn_mid_bwd(p, dw, dwb, ds, name, tc=256):
    b, s, f2 = p.shape
    f = f2 // 2
    kw = dw.shape[0]
    tc = min(tc, f)
    nf = f // tc
    ch = min(CONV_CHUNK, s)
    nch = s // ch
    halo = FFN_HALO

    def sum8(v):
        return jnp.sum(v.reshape(ch // 8, 8, tc), axis=0)

    def body(pa_ref, pg_ref, wa_ref, wg_ref, ba_ref, bg_ref, ds_ref, dpa_ref, dpg_ref, ddwa_ref, ddwg_ref, dba_ref, dbg_ref,
             apad_ref, gpad_ref, dca_ref, dcg_ref):
        bi = pl.program_id(1)

        @pl.when(bi == 0)
        def _():
            ddwa_ref[...] = jnp.zeros(ddwa_ref.shape, F32)
            ddwg_ref[...] = jnp.zeros(ddwg_ref.shape, F32)
            dba_ref[...] = jnp.zeros(dba_ref.shape, F32)
            dbg_ref[...] = jnp.zeros(dbg_ref.shape, F32)

        _fill_front_halo(pa_ref, apad_ref, nch, ch, halo)
        _fill_front_halo(pg_ref, gpad_ref, nch, ch, halo)
        dca_ref[nch - 1, ch:ch + halo, :] = jnp.zeros((halo, tc), F32)
        dcg_ref[nch - 1, ch:ch + halo, :] = jnp.zeros((halo, tc), F32)

        def grads(ci, carry):
            acc_a, acc_g, sb_a, sb_g = carry
            r0 = pl.multiple_of(ci * ch, ch)
            taps_a = _taps_front(apad_ref, ci, kw, ch, halo)
            taps_g = _taps_front(gpad_ref, ci, kw, ch, halo)
            ca = ba_ref[...]
            cg = bg_ref[...]
            for k in range(kw):
                ca = ca + wa_ref[k:k + 1, :] * taps_a[k]
                cg = cg + wg_ref[k:k + 1, :] * taps_g[k]
            sg = _sigmoid(cg)
            dsv = ds_ref[pl.ds(r0, ch), :].astype(F32)
            dca = dsv * (cg * sg)
            dcg = dsv * ca * (sg * (1.0 + cg * (1.0 - sg)))
            dca_ref[ci, 0:ch, :] = dca
            dcg_ref[ci, 0:ch, :] = dcg

            prev = jnp.maximum(ci - 1, 0)

            @pl.when(ci > 0)
            def _():
                dca_ref[prev, ch:ch + halo, :] = dca[0:halo, :]
                dcg_ref[prev, ch:ch + halo, :] = dcg[0:halo, :]

            acc_a = tuple(acc_a[k] + sum8(dca * taps_a[k]) for k in range(kw))
            acc_g = tuple(acc_g[k] + sum8(dcg * taps_g[k]) for k in range(kw))
            return acc_a, acc_g, sb_a + sum8(dca), sb_g + sum8(dcg)

        z8 = jnp.zeros((8, tc), F32)
        acc_a, acc_g, sb_a, sb_g = lax.fori_loop(0, nch, grads, ((z8,) * kw, (z8,) * kw, z8, z8))
        for k in range(kw):
            ddwa_ref[k] += acc_a[k]
            ddwg_ref[k] += acc_g[k]
        dba_ref[...] += jnp.sum(sb_a, axis=0, keepdims=True)
        dbg_ref[...] += jnp.sum(sb_g, axis=0, keepdims=True)

        def back(ci, carry):
            r0 = pl.multiple_of(ci * ch, ch)
            da = wa_ref[kw - 1:kw, :] * dca_ref[ci, 0:ch, :]
            dg = wg_ref[kw - 1:kw, :] * dcg_ref[ci, 0:ch, :]
            for j in range(1, kw):
                da = da + wa_ref[kw - 1 - j:kw - j, :] * dca_ref[ci, j:j + ch, :]
                dg = dg + wg_ref[kw - 1 - j:kw - j, :] * dcg_ref[ci, j:j + ch, :]
            dpa_ref[pl.ds(r0, ch), :] = da.astype(BF16)
            dpg_ref[pl.ds(r0, ch), :] = dg.astype(BF16)
            return carry

        lax.fori_loop(0, nch, back, 0)

    blk = lambda off: pl.BlockSpec((None, s, tc), lambda i, bi: (bi, 0, i + off))
    wsp = lambda off: pl.BlockSpec((kw, tc), lambda i, bi: (0, i + off))
    bsp = lambda off: pl.BlockSpec((1, tc), lambda i, bi: (0, i + off))
    acc3 = pl.BlockSpec((kw, 8, tc), lambda i, bi: (0, 0, i))
    vec = pl.BlockSpec((1, tc), lambda i, bi: (0, i))
    return pl.pallas_call(
        body, name=name, grid=(nf, b),
        in_specs=[blk(0), blk(nf), wsp(0), wsp(nf), bsp(0), bsp(nf), blk(0)],
        out_specs=[blk(0), blk(0), acc3, acc3, vec, vec],
        out_shape=[jax.ShapeDtypeStruct((b, s, f), BF16), jax.ShapeDtypeStruct((b, s, f), BF16),
                   jax.ShapeDtypeStruct((kw, 8, f), F32), jax.ShapeDtypeStruct((kw, 8, f), F32),
                   jax.ShapeDtypeStruct((1, f), F32), jax.ShapeDtypeStruct((1, f), F32)],
        scratch_shapes=[pltpu.VMEM((nch, ch + halo, tc), F32)] * 4,
        compiler_params=_cp("parallel", "arbitrary"),
    )(p, p, dw, dw, dwb, dwb, ds)


def _tile_rows(r, n, dil):
    start = r + n * BLK * dil
    return pl.ds(start, BLK, stride=dil) if dil > 1 else pl.ds(start, BLK)


def _band_masks():
    qi = lax.broadcasted_iota(jnp.int32, (BLK, 2 * BLK), 0)
    kk = lax.broadcasted_iota(jnp.int32, (BLK, 2 * BLK), 1)
    both = jnp.logical_or(jnp.logical_and(kk < BLK, kk >= qi), jnp.logical_and(kk >= BLK, kk - BLK <= qi))
    return both, kk[:, :BLK] <= qi[:, :BLK]


def attn_fwd(q, kv, g, dil, hw, name):
    _, b, s, _ = q.shape
    nh = hw // HEAD_DIM
    nblk = s // dil // BLK
    scale = 1.0 / math.sqrt(HEAD_DIM)

    def body(q_ref, k_ref, v_ref, o_ref, lse_ref):
        h = pl.program_id(1)
        mask2, mask1 = _band_masks()
        mine = lax.broadcasted_iota(jnp.int32, (BLK, LANES), 1) == h

        @pl.when(h == 0)
        def _():
            lse_ref[...] = jnp.zeros(lse_ref.shape, F32)

        for r in range(dil):
            kp = vp = None
            for n in range(nblk):
                rs = _tile_rows(r, n, dil)
                qt = q_ref[rs, :].astype(BF16)
                kc = k_ref[rs, :].astype(BF16)
                vc = v_ref[rs, :].astype(BF16)
                if n == 0:
                    kcat, vcat, mask = kc, vc, mask1
                else:
                    kcat, vcat, mask = jnp.concatenate([kp, kc], axis=0), jnp.concatenate([vp, vc], axis=0), mask2
                sc = jnp.where(mask, _dot_nt(qt, kcat) * scale, NEG_INF)
                m = jnp.max(sc, axis=-1, keepdims=True)
                p = jnp.exp(sc - m)
                den = jnp.sum(p, axis=-1, keepdims=True)
                o_ref[rs, :] = _dot(p.astype(BF16), vcat) / den
                lse_ref[rs, :] = jnp.where(mine, m + jnp.log(den), lse_ref[rs, :])
                kp, vp = kc, vc

    col = lambda base: pl.BlockSpec((None, s, HEAD_DIM), lambda bi, h: (bi, 0, base + h))
    head = lambda base: pl.BlockSpec((None, None, s, HEAD_DIM), lambda bi, h: (base + h, bi, 0, 0))
    return pl.pallas_call(
        body, name=name, grid=(b, nh),
        in_specs=[head(g * nh), head(g * nh), head((N_GROUPS + g) * nh)],
        out_specs=[col(0), pl.BlockSpec((None, s, LANES), lambda bi, h: (bi, 0, 0))],
        out_shape=[jax.ShapeDtypeStruct((b, s, hw), F32), jax.ShapeDtypeStruct((b, s, LANES), F32)],
        compiler_params=_cp("parallel", "arbitrary"),
    )(q, kv, kv)


def attn_merge(outs, lses, name, tm=512):
    t, hw = outs[0].shape
    nh = hw // HEAD_DIM
    tm = _row_tile(t, tm)
    ng = len(outs)

    def body(*refs):
        o_refs, l_refs = refs[:ng], refs[ng:2 * ng]
        m_ref, lj_ref = refs[2 * ng:]
        ls = [l_refs[g][...] for g in range(ng)]
        mx = ls[0]
        for g in range(1, ng):
            mx = jnp.maximum(mx, ls[g])
        es = [jnp.exp(l - mx) for l in ls]
        tot = es[0]
        for g in range(1, ng):
            tot = tot + es[g]
        ws = [e / tot for e in es]
        lj_ref[...] = mx + jnp.log(tot)
        for h in range(nh):
            sl = slice(h * HEAD_DIM, (h + 1) * HEAD_DIM)
            acc = ws[0][:, h:h + 1] * o_refs[0][:, sl]
            for g in range(1, ng):
                acc = acc + ws[g][:, h:h + 1] * o_refs[g][:, sl]
            m_ref[:, sl] = acc.astype(BF16)

    row = pl.BlockSpec((tm, hw), lambda i: (i, 0))
    st = pl.BlockSpec((tm, LANES), lambda i: (i, 0))
    return pl.pallas_call(
        body, name=name, grid=(t // tm,), in_specs=[row] * ng + [st] * ng, out_specs=[row, st],
        out_shape=[jax.ShapeDtypeStruct((t, hw), BF16), jax.ShapeDtypeStruct((t, LANES), F32)],
        compiler_params=_cp("parallel"),
    )(*outs, *lses)


def attn_bwd_prep(dmerged, merged, name, tm=512):
    t, hw = merged.shape
    nh = hw // HEAD_DIM
    tm = _row_tile(t, tm)

    def body(d_ref, m_ref, o_ref):
        lane = lax.broadcasted_iota(jnp.int32, (tm, LANES), 1)
        acc = jnp.zeros((tm, LANES), F32)
        for h in range(nh):
            sl = slice(h * HEAD_DIM, (h + 1) * HEAD_DIM)
            dsum = jnp.sum(d_ref[:, sl] * m_ref[:, sl].astype(F32), axis=-1, keepdims=True)
            acc = jnp.where(lane == h, dsum, acc)
        o_ref[...] = acc

    row = pl.BlockSpec((tm, hw), lambda i: (i, 0))
    return pl.pallas_call(
        body, name=name, grid=(t // tm,), in_specs=[row, row], out_specs=pl.BlockSpec((tm, LANES), lambda i: (i, 0)),
        out_shape=jax.ShapeDtypeStruct((t, LANES), F32), compiler_params=_cp("parallel"),
    )(dmerged, merged)


def attn_bwd(q, kv, g, dil, do, lsej, dm, dq_buf, dk_buf, dv_buf, accumulate, hw, name):
    _, b, s, _ = q.shape
    nh = hw // HEAD_DIM
    nblk = s // dil // BLK
    scale = 1.0 / math.sqrt(HEAD_DIM)
    assert dk_buf is not None or not accumulate
    kv_at = 6 + (dq_buf is not None)

    def body(*refs):
        q_ref, k_ref, v_ref, do_ref, lj_ref, dm_ref = refs[:6]
        dq_ref, dk_ref, dv_ref = refs[-3:]
        dki_ref, dvi_ref = (refs[kv_at], refs[kv_at + 1]) if accumulate else (None, None)
        mask2, mask1 = _band_masks()
        mine = lax.broadcasted_iota(jnp.int32, (BLK, LANES), 1) == pl.program_id(1)

        def my_lane(v):
            return jnp.sum(jnp.where(mine, v, 0.0), axis=-1, keepdims=True)

        def put(rs, dk, dv):
            if accumulate:
                dk = dk + dki_ref[rs, :]
                dv = dv + dvi_ref[rs, :]
            dk_ref[rs, :] = dk
            dv_ref[rs, :] = dv

        for r in range(dil):
            kp = vp = hold_k = hold_v = rs_prev = None
            for n in range(nblk):
                rs = _tile_rows(r, n, dil)
                qt = q_ref[rs, :].astype(BF16)
                kc = k_ref[rs, :].astype(BF16)
                vc = v_ref[rs, :].astype(BF16)
                dot = do_ref[rs, :].astype(BF16)
                lm = my_lane(lj_ref[rs, :])
                dmm = my_lane(dm_ref[rs, :])
                if n == 0:
                    kcat, vcat, mask = kc, vc, mask1
                else:
                    kcat, vcat, mask = jnp.concatenate([kp, kc], axis=0), jnp.concatenate([vp, vc], axis=0), mask2
                p = jnp.exp(jnp.where(mask, _dot_nt(qt, kcat) * scale, NEG_INF) - lm)
                ds = (p * (_dot_nt(dot, vcat) - dmm)).astype(BF16)
                dq_ref[rs, :] = _dot(ds, kcat) * scale
                dkc = _dot_tn(ds, qt) * scale
                dvc = _dot_tn(p.astype(BF16), dot)
                if n > 0:
                    put(rs_prev, hold_k + dkc[:BLK, :], hold_v + dvc[:BLK, :])
                    dkc, dvc = dkc[BLK:, :], dvc[BLK:, :]
                hold_k, hold_v, kp, vp, rs_prev = dkc, dvc, kc, vc, rs
            put(rs_prev, hold_k, hold_v)

    col = lambda base: pl.BlockSpec((None, s, HEAD_DIM), lambda bi, h: (bi, 0, base + h))
    any_spec = pl.BlockSpec(memory_space=pl.ANY)
    stat = pl.BlockSpec((None, s, LANES), lambda bi, h: (bi, 0, 0))
    head = lambda base: pl.BlockSpec((None, None, s, HEAD_DIM), lambda bi, h: (base + h, bi, 0, 0))
    in_specs = [head(g * nh), head(g * nh), head((N_GROUPS + g) * nh), col(0), stat, stat]
    args = [q, kv, kv, do, lsej, dm]
    aliases = {}
    if dq_buf is not None:
        in_specs.append(any_spec)
        args.append(dq_buf)
        aliases[6] = 0
    if dk_buf is not None:
        in_specs += [col(g * nh) if accumulate else any_spec] * 2
        args += [dk_buf, dv_buf]
        aliases.update({kv_at: 1, kv_at + 1: 2})
    shape = jax.ShapeDtypeStruct((b, s, N_GROUPS * hw), F32)
    return pl.pallas_call(
        body, name=name, grid=(b, nh), in_specs=in_specs, out_specs=[col(g * nh)] * 3, out_shape=[shape] * 3,
        input_output_aliases=aliases, compiler_params=_cp("parallel", "parallel"),
    )(*args)


def sum_parts(g, recv, me, name, tm=512):
    _, rows, c = g.shape
    n = recv.shape[0]
    tm = _row_tile(rows, tm)

    def body(me_ref, g_ref, r_ref, o_ref):
        acc = g_ref[...].astype(F32)
        for j in range(n):
            acc = acc + r_ref[j].astype(F32)
        o_ref[...] = acc

    return pl.pallas_call(
        body, name=name,
        grid_spec=pltpu.PrefetchScalarGridSpec(
            num_scalar_prefetch=1, grid=(rows // tm,),
            in_specs=[pl.BlockSpec((None, tm, c), lambda i, me_ref: (me_ref[0], i, 0)),
                      pl.BlockSpec((n, tm, c), lambda i, me_ref: (0, i, 0))],
            out_specs=pl.BlockSpec((tm, c), lambda i, me_ref: (i, 0))),
        out_shape=jax.ShapeDtypeStruct((rows, c), F32), compiler_params=_cp("parallel"),
    )(me, g, recv)


def adamw(w, m, v, g_parts, name, tm=256):
    rows, c = w.shape
    if c <= 8 * LANES:
        tm = 2 * tm
    tm = _row_tile(rows, tm)
    npart = len(g_parts)

    def body(*refs):
        w_ref, m_ref, v_ref = refs[:3]
        g_refs = refs[3:3 + npart]
        go_ref, d_ref, mo_ref, vo_ref = refs[3 + npart:]
        g = g_refs[0][...]
        for k in range(1, npart):
            g = g + g_refs[k][...]
        mn = ADAM_B1 * m_ref[...] + (1.0 - ADAM_B1) * g
        vn = ADAM_B2 * v_ref[...] + (1.0 - ADAM_B2) * (g * g)
        m_hat = mn / (1.0 - ADAM_B1 ** ADAM_STEP)
        v_hat = vn / (1.0 - ADAM_B2 ** ADAM_STEP)
        go_ref[...] = g
        d_ref[...] = -ADAM_LR * (m_hat / (jnp.sqrt(v_hat) + ADAM_EPS) + ADAM_WD * w_ref[...])
        mo_ref[...] = mn
        vo_ref[...] = vn

    row = pl.BlockSpec((tm, c), lambda i: (i, 0))
    return pl.pallas_call(
        body, name=name, grid=(rows // tm,), in_specs=[row] * (3 + npart), out_specs=[row] * 4,
        out_shape=[jax.ShapeDtypeStruct((rows, c), F32)] * 4, compiler_params=_cp("parallel"),
    )(w, m, v, *g_parts)


def _place():
    return lax.axis_index("x"), lax.axis_index("y"), lax.axis_index("c")


def _other_chips(x, y, c):
    return [(1 - x, y, c), (x, 1 - y, c), (1 - x, 1 - y, c)]


def _chip_of(px, py):
    return 2 * px + py


HBM_SPEC = pl.BlockSpec(memory_space=pltpu.HBM)
SEM_SPEC = pl.BlockSpec(memory_space=pltpu.SEMAPHORE)
ANY_SPEC = pl.BlockSpec(memory_space=pl.ANY)
DATAFLOW = pltpu.SideEffectType.DATAFLOW_SIDE_EFFECTING
N_PEER_CHIPS = N_CHIPS - 1


def _hbm(a):
    return pltpu.with_memory_space_constraint(a, pltpu.HBM)


def _hbm_like(arrays):
    return [pltpu.HBM(a.shape, a.dtype) for a in arrays]


def cast_place(w, layer, me, out_dtype, name, tm=512, nslots=N_CHIPS, dep=None):
    rows, c = w.shape[-2:]
    tm = _row_tile(rows, tm)

    def body(me_ref, w_ref, *rest):
        rest[-1][...] = w_ref[...].astype(out_dtype)

    if layer is None:
        in_specs = [pl.BlockSpec((tm, c), lambda i, me_ref: (i, 0))]
    else:
        in_specs = [pl.BlockSpec((None, tm, c), lambda i, me_ref: (layer, i, 0))]
    args = [me, w]
    if dep is not None:
        in_specs.append(pl.BlockSpec(DEP_SPEC_SHAPE, lambda i, me_ref: (0, 0)))
        args.append(dep)
    return pl.pallas_call(
        body, name=name,
        grid_spec=pltpu.PrefetchScalarGridSpec(
            num_scalar_prefetch=1, grid=(rows // tm,), in_specs=in_specs,
            out_specs=pl.BlockSpec((None, tm, c), lambda i, me_ref: (me_ref[0], i, 0))),
        out_shape=jax.ShapeDtypeStruct((nslots, rows, c), out_dtype), compiler_params=_cp("parallel"),
    )(*args)


def gather_start(lands, chunk_sizes, name="gather_start"):
    n = len(lands)
    nch = len(chunk_sizes)
    assert sum(chunk_sizes) == n

    def body(*refs):
        land_refs = refs[:n]
        outs = refs[n:]
        send_sems, recv_sems = outs[:nch], outs[nch:2 * nch]
        token = outs[-1]
        x, y, c = _place()
        me = _chip_of(x, y)
        peers = _other_chips(x, y, c)
        k = 0
        for ck, size in enumerate(chunk_sizes):
            for pos in range(size):
                for r, peer in enumerate(peers):
                    pltpu.make_async_remote_copy(
                        src_ref=land_refs[k].at[me], dst_ref=land_refs[k].at[me],
                        send_sem=send_sems[ck].at[N_PEER_CHIPS * pos + r], recv_sem=recv_sems[ck].at[N_PEER_CHIPS * pos + r],
                        device_id=peer, device_id_type=MESH).start()
                k += 1
        token[...] = jnp.zeros(token.shape, F32)

    sems = [pltpu.SemaphoreType.DMA((N_PEER_CHIPS * s,)) for s in chunk_sizes]
    res = pl.pallas_call(
        body, name=name,
        out_shape=(*sems, *sems, *_hbm_like(lands), jax.ShapeDtypeStruct(DEP_SPEC_SHAPE, F32)),
        in_specs=[HBM_SPEC] * n,
        out_specs=(*[SEM_SPEC] * (2 * nch), *[HBM_SPEC] * n, pl.BlockSpec(memory_space=pltpu.VMEM)),
        input_output_aliases={k: 2 * nch + k for k in range(n)},
        compiler_params=pltpu.CompilerParams(has_side_effects=DATAFLOW),
    )(*[_hbm(a) for a in lands])
    return res[:nch], res[nch:2 * nch], res[2 * nch:2 * nch + n], res[-1]


def gather_wait(send_sem, recv_sem, lands, after, name):
    n = len(lands)

    def body(*refs):
        land_refs = refs[:n]
        ssem, rsem = refs[n], refs[n + 1]
        x, y, c = _place()
        me = _chip_of(x, y)
        for pos in range(n):
            for r, peer in enumerate(_other_chips(x, y, c)):
                cp = pltpu.make_async_remote_copy(
                    src_ref=land_refs[pos].at[me], dst_ref=land_refs[pos].at[_chip_of(peer[0], peer[1])],
                    send_sem=ssem.at[N_PEER_CHIPS * pos + r], recv_sem=rsem.at[N_PEER_CHIPS * pos + r],
                    device_id=peer, device_id_type=MESH)
                cp.wait_send()
                cp.wait_recv()

    return pl.pallas_call(
        body, name=name, out_shape=tuple(_hbm_like(lands)),
        in_specs=[*[HBM_SPEC] * n, SEM_SPEC, SEM_SPEC, ANY_SPEC], out_specs=[HBM_SPEC] * n,
        input_output_aliases={k: k for k in range(n)},
        compiler_params=pltpu.CompilerParams(has_side_effects=DATAFLOW),
    )(*lands, send_sem, recv_sem, after)


def scatter_start(grads, name):
    n = len(grads)
    recvs = [lax.empty((N_PEER_CHIPS, *g.shape[1:]), g.dtype) for g in grads]

    def body(*refs):
        g_refs, r_refs = refs[:n], refs[n:2 * n]
        send_sems, recv_sems = refs[2 * n], refs[2 * n + 1]
        token = refs[-1]
        x, y, c = _place()
        for k in range(n):
            for r, peer in enumerate(_other_chips(x, y, c)):
                pltpu.make_async_remote_copy(
                    src_ref=g_refs[k].at[_chip_of(peer[0], peer[1])], dst_ref=r_refs[k].at[r],
                    send_sem=send_sems.at[N_PEER_CHIPS * k + r], recv_sem=recv_sems.at[N_PEER_CHIPS * k + r],
                    device_id=peer, device_id_type=MESH).start()
        token[...] = jnp.zeros(token.shape, F32)

    sem = pltpu.SemaphoreType.DMA((N_PEER_CHIPS * n,))
    res = pl.pallas_call(
        body, name=name,
        out_shape=(sem, sem, *_hbm_like(grads), *_hbm_like(recvs), jax.ShapeDtypeStruct(DEP_SPEC_SHAPE, F32)),
        in_specs=[HBM_SPEC] * (2 * n),
        out_specs=(SEM_SPEC, SEM_SPEC, *[HBM_SPEC] * (2 * n), pl.BlockSpec(memory_space=pltpu.VMEM)),
        input_output_aliases={k: 2 + k for k in range(2 * n)},
        compiler_params=pltpu.CompilerParams(has_side_effects=DATAFLOW),
    )(*[_hbm(a) for a in grads], *[_hbm(a) for a in recvs])
    return res[0], res[1], res[2:2 + n], res[2 + n:2 + 2 * n], res[-1]


def scatter_wait(send_sem, recv_sem, grads, recvs, after, name):
    n = len(grads)

    def body(*refs):
        g_refs, r_refs = refs[:n], refs[n:2 * n]
        ssem, rsem = refs[2 * n], refs[2 * n + 1]
        x, y, c = _place()
        for k in range(n):
            for r, peer in enumerate(_other_chips(x, y, c)):
                cp = pltpu.make_async_remote_copy(
                    src_ref=g_refs[k].at[_chip_of(peer[0], peer[1])], dst_ref=r_refs[k].at[r],
                    send_sem=ssem.at[N_PEER_CHIPS * k + r], recv_sem=rsem.at[N_PEER_CHIPS * k + r],
                    device_id=peer, device_id_type=MESH)
                cp.wait_send()
                cp.wait_recv()

    res = pl.pallas_call(
        body, name=name, out_shape=(*_hbm_like(grads), *_hbm_like(recvs)),
        in_specs=[*[HBM_SPEC] * (2 * n), SEM_SPEC, SEM_SPEC, ANY_SPEC], out_specs=[HBM_SPEC] * (2 * n),
        input_output_aliases={k: k for k in range(2 * n)},
        compiler_params=pltpu.CompilerParams(has_side_effects=DATAFLOW),
    )(*grads, *recvs, send_sem, recv_sem, after)
    return res[:n], res[n:]


def swap_start(parts, name):
    n = len(parts)
    lands = [lax.empty(p.shape, p.dtype) for p in parts]

    def body(*refs):
        p_refs, l_refs = refs[:n], refs[n:2 * n]
        sems = refs[2 * n:4 * n]
        token = refs[-1]
        x, y, c = _place()
        for k in range(n):
            pltpu.make_async_remote_copy(
                src_ref=p_refs[k], dst_ref=l_refs[k], send_sem=sems[k], recv_sem=sems[n + k],
                device_id=(x, y, 1 - c), device_id_type=MESH).start()
        token[...] = jnp.zeros(token.shape, F32)

    sem = pltpu.SemaphoreType.DMA(())
    res = pl.pallas_call(
        body, name=name,
        out_shape=(*[sem] * (2 * n), *_hbm_like(parts), *_hbm_like(lands), jax.ShapeDtypeStruct(DEP_SPEC_SHAPE, F32)),
        in_specs=[HBM_SPEC] * (2 * n),
        out_specs=(*[SEM_SPEC] * (2 * n), *[HBM_SPEC] * (2 * n), pl.BlockSpec(memory_space=pltpu.VMEM)),
        input_output_aliases={k: 2 * n + k for k in range(2 * n)},
        compiler_params=pltpu.CompilerParams(has_side_effects=DATAFLOW),
    )(*[_hbm(a) for a in parts], *[_hbm(a) for a in lands])
    return res[:n], res[n:2 * n], res[2 * n:3 * n], res[3 * n:4 * n], res[-1]


def swap_wait(send_sem, recv_sem, part, land, after, name):
    def body(p_ref, l_ref, ssem, rsem, after_ref, p_out, l_out):
        x, y, c = _place()
        cp = pltpu.make_async_remote_copy(src_ref=p_ref, dst_ref=l_ref, send_sem=ssem, recv_sem=rsem,
                                          device_id=(x, y, 1 - c), device_id_type=MESH)
        cp.wait_send()
        cp.wait_recv()

    return pl.pallas_call(
        body, name=name, out_shape=tuple(_hbm_like([part, land])),
        in_specs=[HBM_SPEC, HBM_SPEC, SEM_SPEC, SEM_SPEC, ANY_SPEC], out_specs=[HBM_SPEC, HBM_SPEC],
        input_output_aliases={0: 0, 1: 1},
        compiler_params=pltpu.CompilerParams(has_side_effects=DATAFLOW),
    )(part, land, send_sem, recv_sem, after)


def _xor_peer(x, y, c, k):
    px, py, pc = x ^ ((k >> 2) & 1), y ^ ((k >> 1) & 1), c ^ (k & 1)
    return (px, py, pc), 4 * px + 2 * py + pc


def small_start(land, name="small_start"):
    def body(l_ref, ssem, rsem, l_out, token):
        x, y, c = _place()
        me = 4 * x + 2 * y + c
        for k in range(1, N_DEV):
            peer, _ = _xor_peer(x, y, c, k)
            pltpu.make_async_remote_copy(
                src_ref=l_ref.at[me], dst_ref=l_ref.at[me], send_sem=ssem.at[k - 1], recv_sem=rsem.at[k - 1],
                device_id=peer, device_id_type=MESH).start()
        token[...] = jnp.zeros(token.shape, F32)

    sem = pltpu.SemaphoreType.DMA((N_DEV - 1,))
    return pl.pallas_call(
        body, name=name,
        out_shape=(sem, sem, pltpu.HBM(land.shape, land.dtype), jax.ShapeDtypeStruct(DEP_SPEC_SHAPE, F32)),
        in_specs=[HBM_SPEC], out_specs=(SEM_SPEC, SEM_SPEC, HBM_SPEC, pl.BlockSpec(memory_space=pltpu.VMEM)),
        input_output_aliases={0: 2}, compiler_params=pltpu.CompilerParams(has_side_effects=DATAFLOW),
    )(_hbm(land))


def small_wait(send_sem, recv_sem, land, after, name="small_wait"):
    def body(l_ref, ssem, rsem, after_ref, l_out):
        x, y, c = _place()
        me = 4 * x + 2 * y + c
        for k in range(1, N_DEV):
            peer, slot = _xor_peer(x, y, c, k)
            cp = pltpu.make_async_remote_copy(
                src_ref=l_ref.at[me], dst_ref=l_ref.at[slot], send_sem=ssem.at[k - 1], recv_sem=rsem.at[k - 1],
                device_id=peer, device_id_type=MESH)
            cp.wait_send()
            cp.wait_recv()

    return pl.pallas_call(
        body, name=name, out_shape=pltpu.HBM(land.shape, land.dtype),
        in_specs=[HBM_SPEC, SEM_SPEC, SEM_SPEC, ANY_SPEC], out_specs=HBM_SPEC, input_output_aliases={0: 0},
        compiler_params=pltpu.CompilerParams(has_side_effects=DATAFLOW),
    )(land, send_sem, recv_sem, after)


def sum_slots(land, name="sum_slots", tm=256):
    n, rows, c = land.shape
    tm = _row_tile(rows, tm)

    def body(l_ref, o_ref):
        acc = l_ref[0]
        for j in range(1, n):
            acc = acc + l_ref[j]
        o_ref[...] = acc

    return pl.pallas_call(
        body, name=name, grid=(rows // tm,), in_specs=[pl.BlockSpec((n, tm, c), lambda i: (0, i, 0))],
        out_specs=pl.BlockSpec((tm, c), lambda i: (i, 0)), out_shape=jax.ShapeDtypeStruct((rows, c), F32),
        compiler_params=_cp("parallel"),
    )(land)


PACK_ROW_TILE = 256


def _pack(arrays):
    flat = jnp.concatenate([a.reshape(-1).astype(F32) for a in arrays])
    n = flat.shape[0]
    rows = -(-n // LANES)
    rows = -(-rows // PACK_ROW_TILE) * PACK_ROW_TILE
    return jnp.pad(flat, (0, rows * LANES - n)).reshape(rows, LANES)


def _unpack(packed, shapes, lead=()):
    flat = packed.reshape(*lead, -1)
    out, off = [], 0
    for shp in shapes:
        n = math.prod(shp)
        out.append(flat[..., off:off + n].reshape(*lead, *shp))
        off += n
    return out


def _row(vec):
    return vec.reshape(1, -1)


def kernel(x, mix_pre_g, mix_post_g, ffn_pre_g, ffn_post_g, cm_w_in, cm_b_in, cm_dw, cm_dw_b, cm_ln_g, cm_ln_b, cm_w_out, cm_b_out, kv_norm_g, w_kv, w_q, w_o, ffn_w_in, ffn_dw, ffn_dw_b, ffn_w_out, loss_target, m_mix_pre_g, m_mix_post_g, m_ffn_pre_g, m_ffn_post_g, m_cm_w_in, m_cm_b_in, m_cm_dw, m_cm_dw_b, m_cm_ln_g, m_cm_ln_b, m_cm_w_out, m_cm_b_out, m_kv_norm_g, m_w_kv, m_w_q, m_w_o, m_ffn_w_in, m_ffn_dw, m_ffn_dw_b, m_ffn_w_out, v_mix_pre_g, v_mix_post_g, v_ffn_pre_g, v_ffn_post_g, v_cm_w_in, v_cm_b_in, v_cm_dw, v_cm_dw_b, v_cm_ln_g, v_cm_ln_b, v_cm_w_out, v_cm_b_out, v_kv_norm_g, v_w_kv, v_w_q, v_w_o, v_ffn_w_in, v_ffn_dw, v_ffn_dw_b, v_ffn_w_out):
    names = ["mix_pre_g", "mix_post_g", "ffn_pre_g", "ffn_post_g", "cm_w_in", "cm_b_in", "cm_dw", "cm_dw_b", "cm_ln_g",
             "cm_ln_b", "cm_w_out", "cm_b_out", "kv_norm_g", "w_kv", "w_q", "w_o", "ffn_w_in", "ffn_dw", "ffn_dw_b",
             "ffn_w_out"]
    w_in = dict(zip(names, [mix_pre_g, mix_post_g, ffn_pre_g, ffn_post_g, cm_w_in, cm_b_in, cm_dw, cm_dw_b, cm_ln_g,
                            cm_ln_b, cm_w_out, cm_b_out, kv_norm_g, w_kv, w_q, w_o, ffn_w_in, ffn_dw, ffn_dw_b, ffn_w_out]))
    m_in = dict(zip(names, [m_mix_pre_g, m_mix_post_g, m_ffn_pre_g, m_ffn_post_g, m_cm_w_in, m_cm_b_in, m_cm_dw, m_cm_dw_b,
                            m_cm_ln_g, m_cm_ln_b, m_cm_w_out, m_cm_b_out, m_kv_norm_g, m_w_kv, m_w_q, m_w_o, m_ffn_w_in,
                            m_ffn_dw, m_ffn_dw_b, m_ffn_w_out]))
    v_in = dict(zip(names, [v_mix_pre_g, v_mix_post_g, v_ffn_pre_g, v_ffn_post_g, v_cm_w_in, v_cm_b_in, v_cm_dw, v_cm_dw_b,
                            v_cm_ln_g, v_cm_ln_b, v_cm_w_out, v_cm_b_out, v_kv_norm_g, v_w_kv, v_w_q, v_w_o, v_ffn_w_in,
                            v_ffn_dw, v_ffn_dw_b, v_ffn_w_out]))

    bsz, seq, d = x.shape
    t = bsz * seq
    n_b = DEPTH - N_A
    hw = w_o.shape[-1]
    qw = N_GROUPS * hw
    f2 = ffn_dw_b.shape[-1]
    f = f2 // 2
    me_chip = _chip_of(lax.axis_index("x"), lax.axis_index("y"))

    big = ["cm_w_in", "cm_w_out", "w_kv", "w_q", "w_o", "ffn_w_in", "ffn_w_out"]
    row_sharded = ("cm_w_out", "w_o", "ffn_w_out")
    small_sharded = ["cm_b_in", "cm_dw", "cm_dw_b", "cm_ln_g", "cm_ln_b", "cm_b_out", "ffn_dw"]
    small_pack = _pack([w_in[n] for n in small_sharded])
    chunks = [
        [("cm_w_in", 0), ("small", None)],
        [("cm_w_out", 0)],
        [("ffn_w_in", 0), ("ffn_w_out", 0)],
        [("cm_w_in", 1), ("cm_w_out", 1)],
        [("ffn_w_in", 1), ("ffn_w_out", 1)],
        [("w_kv", None)],
        [("w_q", 0), ("w_o", 0)],
        [("ffn_w_in", 2), ("ffn_w_out", 2)],
        [("w_q", 1), ("w_o", 1)],
        [("ffn_w_in", 3), ("ffn_w_out", 3)],
    ]
    pieces = [pc for ch in chunks for pc in ch]
    chunk_of = {pc: ck for ck, ch in enumerate(chunks) for pc in ch}

    me_arr = me_chip.astype(jnp.int32).reshape(1)

    def land_of(pc, dep=None):
        n, l = pc
        if n == "small":
            return cast_place(small_pack, None, me_arr, F32, name="place_small", dep=dep)
        return cast_place(w_in[n], l, me_arr, BF16, name=f"place_{n}_{l}", dep=dep)

    groups = [[0, 1], [2], list(range(3, len(chunks)))]
    g_send, g_recv, lands_f = {}, {}, {}

    def start_group(gi, dep, name):
        cks = groups[gi]
        pcs = [pc for ck in cks for pc in chunks[ck]]
        lands = [land_of(pc, dep if k == 0 else None) for k, pc in enumerate(pcs)]
        send, recv, lands_thru, tok = gather_start(lands, [len(chunks[ck]) for ck in cks], name=name)
        pos = 0
        for j, ck in enumerate(cks):
            g_send[ck], g_recv[ck] = send[j], recv[j]
            lands_f[ck] = lands_thru[pos:pos + len(chunks[ck])]
            pos += len(chunks[ck])
        return tok

    token_a = start_group(0, None, "gather_start_a")
    token = start_group(1, token_a, "gather_start_b")
    weights = {}

    def finish_chunk(ck, after):
        got = gather_wait(g_send[ck], g_recv[ck], lands_f[ck], after, name=f"gather_wait{ck}")
        for pc, arr in zip(chunks[ck], got):
            weights[pc] = arr.reshape(1, -1, arr.shape[-1]) if pc[0] in row_sharded else arr

    def wmat(n, l=None, after=None):
        if (n, l) not in weights:
            finish_chunk(chunk_of[(n, l)], after)
        arr = weights[(n, l)]
        return arr, arr.shape[0]

    finish_chunk(0, token)
    small_full = {}
    for n, arr4 in zip(small_sharded, _unpack(weights[("small", None)], [w_in[n].shape for n in small_sharded], lead=(N_CHIPS,))):
        shp = w_in[n].shape
        small_full[n] = jnp.moveaxis(arr4, 0, -2).reshape(*shp[:-1], N_CHIPS * shp[-1])

    x2d = x.reshape(t, d)
    saved = []
    (h1,) = resid_norm_fwd(x2d, None, None, [_row(mix_pre_g[0])], name="norm_in", dep=token)
    xcur = x2d
    kv_state = None
    for i in range(DEPTH):
        sv = {"x_in": xcur, "h1": h1}
        if i < N_A:
            z = mm_nn(h1, *wmat("cm_w_in", i, h1), 1, 0, bias=_row(small_full["cm_b_in"][i]), name=f"cm_in{i}",
                      spg=N_CHIPS)
            u2 = glu_conv_fwd(z.reshape(bsz, seq, 2 * d), small_full["cm_dw"][i], _row(small_full["cm_dw_b"][i]),
                              name=f"glu_conv{i}").reshape(t, d)
            u4 = ln_silu_fwd(u2, _row(small_full["cm_ln_g"][i]), _row(small_full["cm_ln_b"][i]), name=f"ln_silu{i}")
            y = mm_nn(u4, *wmat("cm_w_out", i, u4), 1, 0, bias=_row(small_full["cm_b_out"][i]), out_dtype=BF16,
                      name=f"cm_out{i}")
            sv.update(z=z, u2=u2, u4=u4)
        else:
            j = i - N_A
            q = mm_nn(h1, *wmat("w_q", j, h1), 1, 0, name=f"q_proj{j}", head_major=True, spg=N_CHIPS).reshape(-1, bsz, seq, HEAD_DIM)
            outs, lses = [], []
            for g, dil in enumerate(DILATIONS):
                o_g, l_g = attn_fwd(q, kv_state["kv"], g, dil, hw, name=f"attn_fwd{j}_{g}")
                outs.append(o_g.reshape(t, hw))
                lses.append(l_g.reshape(t, LANES))
            merged, lsej = attn_merge(outs, lses, name=f"attn_merge{j}")
            y = mm_nn(merged, *wmat("w_o", j, merged), 1, 0, out_dtype=BF16, name=f"o_proj{j}")
            sv.update(q=q, merged=merged, lsej=lsej)
        x1, h2 = resid_norm_fwd(xcur, y, _row(mix_post_g[i]), [_row(ffn_pre_g[i])], name=f"resid_mix{i}")
        ffn_after = start_group(2, x1, "gather_start_c") if i == 0 else h2
        p = mm_nn(h2, *wmat("ffn_w_in", i, ffn_after), 1, 0, out_dtype=BF16, name=f"ffn_in{i}", spg=2)
        s_act = ffn_mid_fwd(p.reshape(bsz, seq, f2), small_full["ffn_dw"][i], _row(ffn_dw_b[i]), name=f"ffn_mid{i}").reshape(t, f)
        y2 = mm_nn(s_act, *wmat("ffn_w_out", i), 1, 0, out_dtype=BF16, name=f"ffn_out{i}")
        next_gains = []
        if i + 1 < DEPTH:
            next_gains.append(_row(mix_pre_g[i + 1]))
        if i == N_A - 1:
            next_gains.append(_row(kv_norm_g))
        res = resid_norm_fwd(x1, y2, _row(ffn_post_g[i]), next_gains, name=f"resid_ffn{i}")
        sv.update(y=y, x1=x1, h2=h2, p=p, s=s_act, y2=y2)
        saved.append(sv)
        xcur = res[0]
        if i + 1 < DEPTH:
            h1 = res[1]
        if i == N_A - 1:
            kvn = res[2]
            kv = mm_nn(kvn, *wmat("w_kv", None, kvn), 1, 0, name="kv_proj", head_major=True, spg=2).reshape(-1, bsz, seq, HEAD_DIM)
            kv_state = {"kv": kv, "kvn": kvn, "x_a": xcur}

    dx, loss_tile = loss_fwd_bwd(xcur, loss_target.reshape(t, d))
    loss = lax.psum(loss_tile[0, 0], ("x", "y", "c"))

    gsm = {n: [None] * w_in[n].shape[0] for n in
           ["mix_pre_g", "mix_post_g", "ffn_pre_g", "ffn_post_g", "cm_b_in", "cm_dw", "cm_dw_b", "cm_ln_g", "cm_ln_b",
            "cm_b_out", "ffn_dw", "ffn_dw_b"]}
    gbig = {}
    in_flight = []
    dep = None

    def start_scatter(pcs, tag):
        ssem, rsem, g_f, r_f, tok = scatter_start([gbig[pc] for pc in pcs], name=f"scatter_start_{tag}")
        in_flight.append((pcs, ssem, rsem, g_f, r_f))
        return tok

    dk_buf = dv_buf = None
    for i in range(DEPTH - 1, -1, -1):
        sv = saved[i]
        dy2, dg, _ = norm_bwd(sv["y2"], _row(ffn_post_g[i]), dx, out_dtype=BF16, name=f"bwd_ffn_post{i}", dep=dep)
        gsm["ffn_post_g"][i] = dg
        ds = mm_nt(dy2, *wmat("ffn_w_out", i), 1, 0, out_dtype=BF16, name=f"bwd_ffn_out_dx{i}")
        gbig[("ffn_w_out", i)] = mm_tn(sv["s"], dy2, 1, name=f"bwd_ffn_out_dw{i}").reshape(N_CHIPS, f // N_CHIPS, d)
        dpa, dpg, ddwa, ddwg, ddba, ddbg = ffn_mid_bwd(sv["p"].reshape(bsz, seq, f2), small_full["ffn_dw"][i], _row(ffn_dw_b[i]),
                                                       ds.reshape(bsz, seq, f), name=f"bwd_ffn_mid{i}")
        gsm["ffn_dw"][i] = jnp.concatenate([jnp.sum(ddwa, axis=1), jnp.sum(ddwg, axis=1)], axis=-1)
        gsm["ffn_dw_b"][i] = jnp.concatenate([ddba, ddbg], axis=-1)
        dp = [dpa.reshape(t, f), dpg.reshape(t, f)]
        dh2 = mm_nt(dp, *wmat("ffn_w_in", i), 1, 0, out_dtype=BF16, name=f"bwd_ffn_in_dx{i}", spg=2)
        gbig[("ffn_w_in", i)] = mm_tn(sv["h2"], dp, N_CHIPS, name=f"bwd_ffn_in_dw{i}", spg=2)
        dx1, dg, _ = norm_bwd(sv["x1"], _row(ffn_pre_g[i]), dh2, add=dx, name=f"bwd_ffn_pre{i}")
        gsm["ffn_pre_g"][i] = dg
        dep = start_scatter([("ffn_w_in", 0), ("ffn_w_out", 0)], "ffn0") if i == 0 else None
        dy, dg, dbias = norm_bwd(sv["y"], _row(mix_post_g[i]), dx1, out_dtype=BF16, name=f"bwd_mix_post{i}", dep=dep)
        gsm["mix_post_g"][i] = dg
        if i < N_A:
            gsm["cm_b_out"][i] = dbias
            du4 = mm_nt(dy, *wmat("cm_w_out", i), 1, 0, out_dtype=BF16, name=f"bwd_cm_out_dx{i}")
            gbig[("cm_w_out", i)] = mm_tn(sv["u4"], dy, 1, name=f"bwd_cm_out_dw{i}").reshape(N_CHIPS, d // N_CHIPS, d)
            du2, dlg, dlb = ln_silu_bwd(sv["u2"], _row(small_full["cm_ln_g"][i]), _row(small_full["cm_ln_b"][i]), du4,
                                        name=f"bwd_ln_silu{i}")
            gsm["cm_ln_g"][i], gsm["cm_ln_b"][i] = dlg, dlb
            dza, dzg, ddw, ddwb, dba, dbg = glu_conv_bwd(sv["z"].reshape(bsz, seq, 2 * d), small_full["cm_dw"][i],
                                                         du2.reshape(bsz, seq, d), name=f"bwd_glu_conv{i}")
            gsm["cm_dw"][i] = jnp.sum(ddw, axis=1)
            gsm["cm_dw_b"][i] = ddwb
            gsm["cm_b_in"][i] = jnp.concatenate([dba, dbg], axis=-1)
            dz = [dza.reshape(t, d), dzg.reshape(t, d)]
            dh1 = mm_nt(dz, *wmat("cm_w_in", i), 1, 0, out_dtype=BF16, name=f"bwd_cm_in_dx{i}", spg=2)
            gbig[("cm_w_in", i)] = mm_tn(sv["h1"], dz, N_CHIPS, name=f"bwd_cm_in_dw{i}", spg=2)
        else:
            j = i - N_A
            dmerged = mm_nt(dy, *wmat("w_o", j), 1, 0, name=f"bwd_o_proj_dx{j}")
            gbig[("w_o", j)] = mm_tn(sv["merged"], dy, 1, name=f"bwd_o_proj_dw{j}").reshape(N_CHIPS, hw // N_CHIPS, d)
            dmt = attn_bwd_prep(dmerged, sv["merged"], name=f"bwd_attn_prep{j}")
            dq_buf = None
            add_to_kv = dk_buf is not None
            for g, dil in enumerate(DILATIONS):
                dq_buf, dk_buf, dv_buf = attn_bwd(
                    sv["q"], kv_state["kv"], g, dil, dmerged.reshape(bsz, seq, hw), sv["lsej"].reshape(bsz, seq, LANES),
                    dmt.reshape(bsz, seq, LANES), dq_buf, dk_buf, dv_buf, add_to_kv, hw, name=f"attn_bwd{j}_{g}")
            dq = dq_buf.reshape(t, qw)
            dh1 = mm_nt(dq, *wmat("w_q", j), 1, 0, out_dtype=BF16, name=f"bwd_q_proj_dx{j}", tm=512, spg=N_CHIPS)
            gbig[("w_q", j)] = mm_tn(sv["h1"], dq, N_CHIPS, name=f"bwd_q_proj_dw{j}", tm=512, spg=N_CHIPS)
        dx, dg, _ = norm_bwd(sv["x_in"], _row(mix_pre_g[i]), dh1, add=dx1, name=f"bwd_mix_pre{i}")
        gsm["mix_pre_g"][i] = dg
        if i > N_A:
            dep = start_scatter([("ffn_w_in", i), ("ffn_w_out", i), ("w_q", i - N_A), ("w_o", i - N_A)], f"l{i}")
        elif 0 < i < N_A:
            dep = start_scatter([("ffn_w_in", i), ("ffn_w_out", i), ("cm_w_in", i), ("cm_w_out", i)], f"l{i}")
        elif i == 0:
            last_token = start_scatter([("cm_w_in", 0), ("cm_w_out", 0)], "cm0")
        if i == N_A:
            dkv = [dk_buf.reshape(t, qw), dv_buf.reshape(t, qw)]
            dkvn = mm_nt(dkv, *wmat("w_kv"), 1, 0, out_dtype=BF16, name="bwd_kv_proj_dx")
            gbig[("w_kv", None)] = mm_tn(kv_state["kvn"], dkv, N_CHIPS, name="bwd_kv_proj_dw")
            dx, dg_kv, _ = norm_bwd(kv_state["x_a"], _row(kv_norm_g), dkvn, add=dx, name="bwd_kv_norm")
            dep = start_scatter([("ffn_w_in", i), ("ffn_w_out", i), ("w_q", 0), ("w_o", 0), ("w_kv", None)], f"l{i}")
    grad_x = dx.reshape(bsz, seq, d)

    plane_of = {}
    outs_g, outs_d, outs_m, outs_v = {}, {}, {}, {}

    def finish_scatter(k, after):
        pcs, ssem, rsem, g_f, r_f = in_flight[k]
        g_done, r_done = scatter_wait(ssem, rsem, g_f, r_f, after, name=f"scatter_wait{k}")
        for pc, g_arr, r_arr in zip(pcs, g_done, r_done):
            plane_of[pc] = sum_parts(g_arr, r_arr, me_arr, name=f"sum_chips_{pc[0]}_{pc[1]}")

    def update(group, tag, after):
        plane = []
        for n in group:
            if w_in[n].ndim == 2:
                plane.append(plane_of[(n, None)])
            else:
                plane.append(jnp.concatenate([plane_of[(n, l)] for l in range(w_in[n].shape[0])], axis=0))
        ssems, rsems, plane_f, land_f, _ = swap_start(plane, name=f"swap_start_{tag}")
        for k, n in enumerate(group):
            p_mine, p_other = swap_wait(ssems[k], rsems[k], plane_f[k], land_f[k], after, name=f"swap_wait_{n}")
            shp = w_in[n].shape
            flat = lambda a: a.reshape(-1, shp[-1])
            g_, d_, m_, v_ = adamw(flat(w_in[n]), flat(m_in[n]), flat(v_in[n]), [p_mine, p_other], name=f"adamw_{n}")
            outs_g[n], outs_d[n], outs_m[n], outs_v[n] = (a.reshape(shp) for a in (g_, d_, m_, v_))
            after = v_
        return after

    small_names = [n for n in names if n not in big]
    small_shapes_full = {}
    small_grads_full = []
    for n in small_names:
        if n == "kv_norm_g":
            gfull = dg_kv.reshape(-1)
        elif n in ("cm_dw", "ffn_dw"):
            gfull = jnp.stack(gsm[n], axis=0)
        else:
            gfull = jnp.stack([a.reshape(-1) for a in gsm[n]], axis=0)
        small_shapes_full[n] = gfull.shape
        small_grads_full.append(gfull)
    dev_arr = (4 * lax.axis_index("x") + 2 * lax.axis_index("y") + lax.axis_index("c")).astype(jnp.int32).reshape(1)
    small_land = cast_place(_pack(small_grads_full) + last_token[0, 0], None, dev_arr, F32, name="place_small_grads",
                            nslots=N_DEV)
    sm_send, sm_recv, small_land, small_token = small_start(small_land)

    for k in range(len(in_flight) - 1):
        finish_scatter(k, small_token)
    done = update(["w_kv", "w_q", "w_o", "ffn_w_in", "ffn_w_out"], "a", small_token)
    finish_scatter(len(in_flight) - 1, done)
    done = update(["cm_w_in", "cm_w_out"], "b", done)

    summed = sum_slots(small_wait(sm_send, sm_recv, small_land, done))
    g_full = dict(zip(small_names, _unpack(summed, [small_shapes_full[n] for n in small_names])))
    g_loc = {}
    for n in small_names:
        if n in small_sharded:
            width = w_in[n].shape[-1]
            g_loc[n] = lax.dynamic_slice_in_dim(g_full[n], me_chip * width, width, axis=g_full[n].ndim - 1)
        else:
            g_loc[n] = g_full[n]
    res = adamw(_pack([w_in[n] for n in small_names]), _pack([m_in[n] for n in small_names]),
                _pack([v_in[n] for n in small_names]), [_pack([g_loc[n] for n in small_names])], name="adamw_small")
    shapes_loc = [w_in[n].shape for n in small_names]
    for dst, packed in zip((outs_g, outs_d, outs_m, outs_v), res):
        for n, a in zip(small_names, _unpack(packed, shapes_loc)):
            dst[n] = a

    return (loss, grad_x, *[outs_g[n] for n in names], *[outs_d[n] for n in names],
            *[outs_m[n] for n in names], *[outs_v[n] for n in names])
```

```python
import functools
import math

import jax
import jax.numpy as jnp
from jax import lax
from jax.experimental import pallas as pl
from jax.experimental.pallas import tpu as pltpu

F32 = jnp.float32
BF16 = jnp.bfloat16
EPS = 1e-6
NEG_INF = -1e30
N_A = 2
DEPTH = 4
N_GROUPS = 3
DILATIONS = (1, 4, 16)
HEAD_DIM = 128
BLK = 128
LANES = 128
N_CHIPS = 4
N_DEV = 8
VMEM_LIMIT_V7X = 56 * 1024 * 1024

ADAM_LR = 0.001
ADAM_B1 = 0.9
ADAM_B2 = 0.999
ADAM_EPS = 1e-08
ADAM_WD = 0.01
ADAM_STEP = 10

MESH = pl.DeviceIdType.MESH


def _cp(*sem, **kw):
    return pltpu.CompilerParams(dimension_semantics=sem if sem else None, vmem_limit_bytes=VMEM_LIMIT_V7X, **kw)


def _dot(a, b):
    return jnp.dot(a, b, preferred_element_type=F32)


def _dot_nt(a, b):
    return lax.dot_general(a, b, (((1,), (1,)), ((), ())), preferred_element_type=F32)


def _dot_tn(a, b):
    return lax.dot_general(a, b, (((0,), (0,)), ((), ())), preferred_element_type=F32)


def _sigmoid(x):
    return 1.0 / (1.0 + jnp.exp(-x))


def _row_tile(n, want):
    if n <= want:
        return n
    for t in range(want - want % 8, 7, -8):
        if n % t == 0:
            return t
    raise ValueError(f"no row tile for {n} rows")


def mm_nn(a, w, nsh, stride, layer, bias=None, out_dtype=F32, name="mm_nn", tm=1024, head_major=False, spg=1):
    assert stride == 1 and layer == 0 and nsh % spg == 0
    m, k = a.shape
    _, k2, ns = w.shape
    assert k == k2
    tm = _row_tile(m, tm)
    has_bias = bias is not None
    hps = ns // HEAD_DIM

    def body(*refs):
        if has_bias:
            a_ref, w_ref, b_ref, o_ref = refs
        else:
            a_ref, w_ref, o_ref = refs
        av = a_ref[...].astype(BF16)
        for jj in range(spg):
            acc = _dot(av, w_ref[jj])
            if has_bias:
                acc = acc + b_ref[:, jj * ns:(jj + 1) * ns]
            if head_major:
                for hh in range(hps):
                    o_ref[jj * hps + hh] = acc[:, hh * HEAD_DIM:(hh + 1) * HEAD_DIM].astype(out_dtype)
            else:
                o_ref[:, jj * ns:(jj + 1) * ns] = acc.astype(out_dtype)

    in_specs = [
        pl.BlockSpec((tm, k), lambda j, i: (i, 0)),
        pl.BlockSpec((spg, k, ns), lambda j, i: (j, 0, 0)),
    ]
    args = [a, w]
    if has_bias:
        in_specs.append(pl.BlockSpec((1, spg * ns), lambda j, i: (0, j)))
        args.append(bias)
    return pl.pallas_call(
        body,
        name=name,
        grid=(nsh // spg, m // tm),
        in_specs=in_specs,
        out_specs=(pl.BlockSpec((spg * hps, tm, HEAD_DIM), lambda j, i: (j, i, 0)) if head_major
                   else pl.BlockSpec((tm, spg * ns), lambda j, i: (i, j))),
        out_shape=jax.ShapeDtypeStruct((nsh * hps, m, HEAD_DIM) if head_major else (m, nsh * ns), out_dtype),
        compiler_params=_cp("parallel", "parallel"),
    )(*args)


def _split_parts(dy, nsh, spg):
    dys = list(dy) if isinstance(dy, (list, tuple)) else [dy]
    per = nsh // len(dys)
    assert per % spg == 0
    return dys, per, per // spg


def mm_nt(dy, w, nsh, stride, layer, out_dtype=F32, name="mm_nt", tm=1024, spg=1):
    assert stride == 1 and layer == 0
    dys, per, gpp = _split_parts(dy, nsh, spg)
    npart = len(dys)
    m = dys[0].shape[0]
    _, k, ns = w.shape
    assert all(d.shape == (m, per * ns) for d in dys)
    tm = _row_tile(m, tm)
    ngrp = nsh // spg

    def body(*refs):
        dy_refs = refs[:npart]
        w_ref, o_ref, acc_ref = refs[npart:]
        j = pl.program_id(1)

        @pl.when(j == 0)
        def _():
            acc_ref[...] = jnp.zeros(acc_ref.shape, F32)

        for pi in range(npart):
            @pl.when(j // gpp == pi)
            def _(pi=pi):
                part = _dot_nt(dy_refs[pi][:, 0:ns].astype(BF16), w_ref[0])
                for jj in range(1, spg):
                    part = part + _dot_nt(dy_refs[pi][:, jj * ns:(jj + 1) * ns].astype(BF16), w_ref[jj])
                acc_ref[...] += part

        @pl.when(j == ngrp - 1)
        def _():
            o_ref[...] = acc_ref[...].astype(out_dtype)

    dy_specs = [pl.BlockSpec((tm, spg * ns), lambda i, j, pi=pi: (i, jnp.clip(j - pi * gpp, 0, gpp - 1)))
                for pi in range(npart)]
    return pl.pallas_call(
        body,
        name=name,
        grid=(m // tm, ngrp),
        in_specs=[*dy_specs, pl.BlockSpec((spg, k, ns), lambda i, j: (j, 0, 0))],
        out_specs=pl.BlockSpec((tm, k), lambda i, j: (i, 0)),
        out_shape=jax.ShapeDtypeStruct((m, k), out_dtype),
        scratch_shapes=[pltpu.VMEM((tm, k), F32)],
        compiler_params=_cp("parallel", "arbitrary"),
    )(*dys, w)


def mm_tn(a, dy, nsh, name="mm_tn", tm=1024, spg=1, a_transposed=False):
    dys, per, gpp = _split_parts(dy, nsh, spg)
    npart = len(dys)
    m, k = a.shape[::-1] if a_transposed else a.shape
    ns = dys[0].shape[1] // per
    assert all(d.shape == (m, per * ns) for d in dys)
    tm = _row_tile(m, tm)
    nt = m // tm

    def body(*refs):
        a_ref = refs[0]
        dy_refs = refs[1:1 + npart]
        o_ref, acc_ref = refs[1 + npart:]
        j = pl.program_id(0)
        i = pl.program_id(1)

        @pl.when(i == 0)
        def _():
            acc_ref[...] = jnp.zeros(acc_ref.shape, F32)

        for pi in range(npart):
            @pl.when(j // gpp == pi)
            def _(pi=pi):
                at = a_ref[...].astype(BF16)
                if not a_transposed:
                    at = at.T
                for jj in range(spg):
                    acc_ref[jj] += _dot(at, dy_refs[pi][:, jj * ns:(jj + 1) * ns].astype(BF16))

        @pl.when(i == nt - 1)
        def _():
            o_ref[...] = acc_ref[...].astype(BF16)

    dy_specs = [
        pl.BlockSpec((tm, spg * ns),
                     lambda j, i, pi=pi: (jnp.where(j // gpp == pi, i, 0), jnp.clip(j - pi * gpp, 0, gpp - 1)))
        for pi in range(npart)
    ]
    return pl.pallas_call(
        body,
        name=name,
        grid=(nsh // spg, nt),
        in_specs=[pl.BlockSpec((k, tm), lambda j, i: (0, i)) if a_transposed else pl.BlockSpec((tm, k), lambda j, i: (i, 0)),
                  *dy_specs],
        out_specs=pl.BlockSpec((spg, k, ns), lambda j, i: (j, 0, 0)),
        out_shape=jax.ShapeDtypeStruct((nsh, k, ns), BF16),
        scratch_shapes=[pltpu.VMEM((spg, k, ns), F32)],
        compiler_params=_cp("parallel", "arbitrary"),
    )(a, *dys)


DEP_SPEC_SHAPE = (8, LANES)


def resid_norm_fwd(x, y, g_post, next_gains, name, tm=1024, dep=None):
    t, d = x.shape
    tm = _row_tile(t, tm)
    has_y = y is not None
    n_next = len(next_gains)
    n_dep = 0 if dep is None else 1

    def body(*refs):
        x_ref = refs[0]
        pos = 1
        if has_y:
            y_ref, gp_ref = refs[1], refs[2]
            pos = 3
        gn_refs = refs[pos:pos + n_next]
        outs = refs[pos + n_next + n_dep:]
        xv = x_ref[...]
        o = 0
        if has_y:
            yv = y_ref[...].astype(F32)
            r = lax.rsqrt(jnp.mean(yv * yv, axis=-1, keepdims=True) + EPS)
            xv = xv + (yv * r) * gp_ref[...]
            outs[0][...] = xv
            o = 1
        if n_next:
            xn = xv * lax.rsqrt(jnp.mean(xv * xv, axis=-1, keepdims=True) + EPS)
            for k in range(n_next):
                hk = (xn * gn_refs[k][...]).astype(BF16)
                outs[o + k][...] = hk
                outs[o + n_next + k][...] = hk.T

    row = pl.BlockSpec((tm, d), lambda i: (i, 0))
    vec = pl.BlockSpec((1, d), lambda i: (0, 0))
    args, in_specs = [x], [row]
    if has_y:
        args += [y, g_post]
        in_specs += [row, vec]
    args += list(next_gains)
    in_specs += [vec] * n_next
    if n_dep:
        args.append(dep)
        in_specs.append(pl.BlockSpec(DEP_SPEC_SHAPE, lambda i: (0, 0)))
    out_shape, out_specs = [], []
    if has_y:
        out_shape.append(jax.ShapeDtypeStruct((t, d), F32))
        out_specs.append(row)
    for _ in range(n_next):
        out_shape.append(jax.ShapeDtypeStruct((t, d), BF16))
        out_specs.append(row)
    for _ in range(n_next):
        out_shape.append(jax.ShapeDtypeStruct((d, t), BF16))
        out_specs.append(pl.BlockSpec((d, tm), lambda i: (0, i)))
    return pl.pallas_call(
        body, name=name, grid=(t // tm,), in_specs=in_specs, out_specs=out_specs, out_shape=out_shape,
        compiler_params=_cp("parallel"),
    )(*args)


def norm_bwd(x, g, dy, add=None, out_dtype=F32, name="norm_bwd", tm=1024, dep=None):
    t, d = x.shape
    tm = _row_tile(t, tm)
    has_add = add is not None

    def body(*refs):
        x_ref, g_ref, dy_ref = refs[:3]
        add_ref = refs[3] if has_add else None
        dx_ref, dg_ref, cs_ref = refs[-3:]
        i = pl.program_id(0)
        xv = x_ref[...].astype(F32)
        dyv = dy_ref[...].astype(F32)
        r = lax.rsqrt(jnp.mean(xv * xv, axis=-1, keepdims=True) + EPS)
        gd = dyv * g_ref[...]
        dx = r * gd - xv * ((r * r * r) * jnp.mean(xv * gd, axis=-1, keepdims=True))
        if has_add:
            dx = dx + add_ref[...]
        dx_ref[...] = dx.astype(out_dtype)
        dg = jnp.sum(dyv * (xv * r), axis=0, keepdims=True)
        cs = jnp.sum(dx, axis=0, keepdims=True)

        @pl.when(i == 0)
        def _():
            dg_ref[...] = dg
            cs_ref[...] = cs

        @pl.when(i > 0)
        def _():
            dg_ref[...] += dg
            cs_ref[...] += cs

    row = pl.BlockSpec((tm, d), lambda i: (i, 0))
    vec = pl.BlockSpec((1, d), lambda i: (0, 0))
    args, in_specs = [x, g, dy], [row, vec, row]
    if has_add:
        args.append(add)
        in_specs.append(row)
    if dep is not None:
        args.append(dep)
        in_specs.append(pl.BlockSpec(DEP_SPEC_SHAPE, lambda i: (0, 0)))
    return pl.pallas_call(
        body, name=name, grid=(t // tm,), in_specs=in_specs,
        out_specs=[row, vec, vec],
        out_shape=[jax.ShapeDtypeStruct((t, d), out_dtype), jax.ShapeDtypeStruct((1, d), F32),
                   jax.ShapeDtypeStruct((1, d), F32)],
        compiler_params=_cp("arbitrary"),
    )(*args)


def loss_fwd_bwd(x, target, name="loss", tm=1024):
    t, d = x.shape
    tm = _row_tile(t, tm)

    def body(x_ref, t_ref, dx_ref, l_ref):
        i = pl.program_id(0)
        err = x_ref[...] - t_ref[...]
        dx_ref[...] = err * (1.0 / d)
        part = 0.5 * jnp.sum(jnp.mean(err * err, axis=-1, keepdims=True), axis=0, keepdims=True)
        part = jnp.broadcast_to(part, l_ref.shape)

        @pl.when(i == 0)
        def _():
            l_ref[...] = part

        @pl.when(i > 0)
        def _():
            l_ref[...] += part

    row = pl.BlockSpec((tm, d), lambda i: (i, 0))
    return pl.pallas_call(
        body, name=name, grid=(t // tm,), in_specs=[row, row],
        out_specs=[row, pl.BlockSpec((8, LANES), lambda i: (0, 0))],
        out_shape=[jax.ShapeDtypeStruct((t, d), F32), jax.ShapeDtypeStruct((8, LANES), F32)],
        compiler_params=_cp("arbitrary"),
    )(x, target)


CONV_HALO = 32
CONV_CHUNK = 128


def glu_conv_fwd(z, dw, dwb, name, tc=128):
    b, s, c2 = z.shape
    c = c2 // 2
    kw = dw.shape[0]
    tc = min(tc, c)
    nc = c // tc
    ch = min(CONV_CHUNK, s)
    halo = CONV_HALO
    assert kw - 1 <= halo and s % ch == 0

    nch = s // ch

    def body(a_ref, g_ref, w_ref, b_ref, o_ref, pad_ref):
        _fill_glu_slabs(a_ref, g_ref, pad_ref, nch, ch, halo)

        def chunk(ci, carry):
            r0 = pl.multiple_of(ci * ch, ch)
            acc = b_ref[...]
            for k, tap in enumerate(_taps_front(pad_ref, ci, kw, ch, halo)):
                acc = acc + w_ref[k:k + 1, :] * tap
            o_ref[pl.ds(r0, ch), :] = acc
            return carry

        lax.fori_loop(0, nch, chunk, 0)

    return pl.pallas_call(
        body, name=name, grid=(b, nc),
        in_specs=[
            pl.BlockSpec((None, s, tc), lambda bi, i: (bi, 0, i)),
            pl.BlockSpec((None, s, tc), lambda bi, i: (bi, 0, i + nc)),
            pl.BlockSpec((kw, tc), lambda bi, i: (0, i)),
            pl.BlockSpec((1, tc), lambda bi, i: (0, i)),
        ],
        out_specs=pl.BlockSpec((None, s, tc), lambda bi, i: (bi, 0, i)),
        out_shape=jax.ShapeDtypeStruct((b, s, c), F32),
        scratch_shapes=[pltpu.VMEM((nch, ch + halo, tc), F32)],
        compiler_params=_cp("parallel", "parallel"),
    )(z, z, dw, dwb)


def glu_conv_bwd(z, dw, du2, name, tc=128):
    b, s, c2 = z.shape
    c = c2 // 2
    kw = dw.shape[0]
    tc = min(tc, c)
    nc = c // tc
    ch = min(CONV_CHUNK, s)
    nch = s // ch
    halo = CONV_HALO

    def body(a_ref, g_ref, w_ref, du_ref, dza_ref, dzg_ref, ddw_ref, ddwb_ref, dba_ref, dbg_ref, upad_ref, dpad_ref):
        bi = pl.program_id(1)

        @pl.when(bi == 0)
        def _():
            ddw_ref[...] = jnp.zeros(ddw_ref.shape, F32)
            ddwb_ref[...] = jnp.zeros(ddwb_ref.shape, F32)
            dba_ref[...] = jnp.zeros(dba_ref.shape, F32)
            dbg_ref[...] = jnp.zeros(dbg_ref.shape, F32)

        _fill_glu_slabs(a_ref, g_ref, upad_ref, nch, ch, halo)
        dpad_ref[nch - 1, ch:ch + halo, :] = jnp.zeros((halo, tc), F32)
        dpad_ref[nch - 1, 0:ch, :] = du_ref[s - ch:s, :]

        def fill(ci, carry):
            r0 = pl.multiple_of(ci * ch, ch)
            dpad_ref[ci, :, :] = du_ref[pl.ds(r0, ch + halo), :]
            return carry

        lax.fori_loop(0, nch - 1, fill, 0)

        def chunk(ci, carry):
            r0 = pl.multiple_of(ci * ch, ch)
            du_c = du_ref[pl.ds(r0, ch), :]
            taps_u = _taps_front(upad_ref, ci, kw, ch, halo)
            taps_d = _taps_at(dpad_ref, ci, list(range(kw)), ch)
            du1 = w_ref[kw - 1:kw, :] * du_c
            ddw_ref[kw - 1] += jnp.sum((du_c * taps_u[kw - 1]).reshape(ch // 8, 8, tc), axis=0)
            for j in range(1, kw):
                du1 = du1 + w_ref[kw - 1 - j:kw - j, :] * taps_d[j]
                ddw_ref[kw - 1 - j] += jnp.sum((du_c * taps_u[kw - 1 - j]).reshape(ch // 8, 8, tc), axis=0)
            av = a_ref[pl.ds(r0, ch), :]
            sg = _sigmoid(g_ref[pl.ds(r0, ch), :])
            dza = du1 * sg
            dzg = du1 * av * (sg * (1.0 - sg))
            dza_ref[pl.ds(r0, ch), :] = dza.astype(BF16)
            dzg_ref[pl.ds(r0, ch), :] = dzg.astype(BF16)
            dba_ref[...] += jnp.sum(dza, axis=0, keepdims=True)
            dbg_ref[...] += jnp.sum(dzg, axis=0, keepdims=True)
            ddwb_ref[...] += jnp.sum(du_c, axis=0, keepdims=True)
            return carry

        lax.fori_loop(0, s // ch, chunk, 0)

    blk = lambda off: pl.BlockSpec((None, s, tc), lambda i, bi: (bi, 0, i + off))
    vec = pl.BlockSpec((1, tc), lambda i, bi: (0, i))
    return pl.pallas_call(
        body, name=name, grid=(nc, b),
        in_specs=[blk(0), blk(nc), pl.BlockSpec((kw, tc), lambda i, bi: (0, i)), blk(0)],
        out_specs=[blk(0), blk(0), pl.BlockSpec((kw, 8, tc), lambda i, bi: (0, 0, i)), vec, vec, vec],
        out_shape=[
            jax.ShapeDtypeStruct((b, s, c), BF16), jax.ShapeDtypeStruct((b, s, c), BF16),
            jax.ShapeDtypeStruct((kw, 8, c), F32), jax.ShapeDtypeStruct((1, c), F32),
            jax.ShapeDtypeStruct((1, c), F32), jax.ShapeDtypeStruct((1, c), F32),
        ],
        scratch_shapes=[pltpu.VMEM((nch, ch + halo, tc), F32), pltpu.VMEM((nch, ch + halo, tc), F32)],
        compiler_params=_cp("parallel", "arbitrary"),
    )(z, z, dw, du2)


def _fill_glu_slabs(a_ref, g_ref, pad_ref, nch, ch, halo):
    tc = a_ref.shape[-1]
    pad_ref[0, 0:halo, :] = jnp.zeros((halo, tc), F32)
    pad_ref[0, halo:halo + ch, :] = a_ref[0:ch, :] * _sigmoid(g_ref[0:ch, :])

    def fill(ci, carry):
        r0 = pl.multiple_of(ci * ch, ch)
        pad_ref[ci, 0:halo, :] = pad_ref[ci - 1, ch:ch + halo, :]
        pad_ref[ci, halo:halo + ch, :] = a_ref[pl.ds(r0, ch), :] * _sigmoid(g_ref[pl.ds(r0, ch), :])
        return carry

    lax.fori_loop(1, nch, fill, 0)


def ln_silu_fwd(u, g, bvec, name, tm=1024):
    t, d = u.shape
    tm = _row_tile(t, tm)

    def body(u_ref, g_ref, b_ref, o_ref):
        uv = u_ref[...]
        mu = jnp.mean(uv, axis=-1, keepdims=True)
        xc = uv - mu
        var = jnp.mean(xc * xc, axis=-1, keepdims=True)
        v = (xc * lax.rsqrt(var + EPS)) * g_ref[...] + b_ref[...]
        o_ref[...] = (v * _sigmoid(v)).astype(BF16)

    row = pl.BlockSpec((tm, d), lambda i: (i, 0))
    vec = pl.BlockSpec((1, d), lambda i: (0, 0))
    return pl.pallas_call(
        body, name=name, grid=(t // tm,), in_specs=[row, vec, vec], out_specs=row,
        out_shape=jax.ShapeDtypeStruct((t, d), BF16), compiler_params=_cp("parallel"),
    )(u, g, bvec)


def ln_silu_bwd(u, g, bvec, dout, name, tm=1024):
    t, d = u.shape
    tm = _row_tile(t, tm)

    def body(u_ref, g_ref, b_ref, do_ref, du_ref, dg_ref, db_ref):
        i = pl.program_id(0)
        uv = u_ref[...]
        mu = jnp.mean(uv, axis=-1, keepdims=True)
        xc = uv - mu
        var = jnp.mean(xc * xc, axis=-1, keepdims=True)
        rstd = lax.rsqrt(var + EPS)
        n = xc * rstd
        v = n * g_ref[...] + b_ref[...]
        sg = _sigmoid(v)
        dv = do_ref[...].astype(F32) * (sg * (1.0 + v * (1.0 - sg)))
        dn = dv * g_ref[...]
        du_ref[...] = rstd * (dn - jnp.mean(dn, axis=-1, keepdims=True) - n * jnp.mean(dn * n, axis=-1, keepdims=True))
        dg = jnp.sum(dv * n, axis=0, keepdims=True)
        db = jnp.sum(dv, axis=0, keepdims=True)

        @pl.when(i == 0)
        def _():
            dg_ref[...] = dg
            db_ref[...] = db

        @pl.when(i > 0)
        def _():
            dg_ref[...] += dg
            db_ref[...] += db

    row = pl.BlockSpec((tm, d), lambda i: (i, 0))
    vec = pl.BlockSpec((1, d), lambda i: (0, 0))
    return pl.pallas_call(
        body, name=name, grid=(t // tm,), in_specs=[row, vec, vec, row], out_specs=[row, vec, vec],
        out_shape=[jax.ShapeDtypeStruct((t, d), F32), jax.ShapeDtypeStruct((1, d), F32), jax.ShapeDtypeStruct((1, d), F32)],
        compiler_params=_cp("arbitrary"),
    )(u, g, bvec, dout)


FFN_HALO = 8


def _fill_front_halo(src_ref, pad_ref, nch, ch, halo):
    tc = src_ref.shape[-1]
    pad_ref[0, 0:halo, :] = jnp.zeros((halo, tc), F32)
    pad_ref[0, halo:halo + ch, :] = src_ref[0:ch, :].astype(F32)

    def fill(ci, carry):
        r0 = pl.multiple_of(ci * ch, ch)
        pad_ref[ci, 0:halo, :] = src_ref[pl.ds(r0 - 2 * halo, 2 * halo), :].astype(F32)[halo:, :]
        pad_ref[ci, halo:halo + ch, :] = src_ref[pl.ds(r0, ch), :].astype(F32)
        return carry

    lax.fori_loop(1, nch, fill, 0)


def _taps_at(pad_ref, ci, offsets, ch):
    windows = {}
    for b in sorted({o % 8 for o in offsets}):
        top = max(o for o in offsets if o % 8 == b)
        windows[b] = pad_ref[ci, b:top + ch, :]
    return [windows[o % 8][o - o % 8:o - o % 8 + ch, :] for o in offsets]


def _taps_front(pad_ref, ci, kw, ch, halo):
    return _taps_at(pad_ref, ci, [halo - (kw - 1 - k) for k in range(kw)], ch)


def ffn_mid_fwd(p, dw, dwb, name, tc=256):
    b, s, f2 = p.shape
    f = f2 // 2
    kw = dw.shape[0]
    tc = min(tc, f)
    nf = f // tc
    ch = min(CONV_CHUNK, s)
    nch = s // ch
    halo = FFN_HALO

    def body(pa_ref, pg_ref, wa_ref, wg_ref, ba_ref, bg_ref, o_ref, apad_ref, gpad_ref):
        _fill_front_halo(pa_ref, apad_ref, nch, ch, halo)
        _fill_front_halo(pg_ref, gpad_ref, nch, ch, halo)

        def chunk(ci, carry):
            r0 = pl.multiple_of(ci * ch, ch)
            ca = ba_ref[...]
            cg = bg_ref[...]
            taps = zip(_taps_front(apad_ref, ci, kw, ch, halo), _taps_front(gpad_ref, ci, kw, ch, halo))
            for k, (ta, tg) in enumerate(taps):
                ca = ca + wa_ref[k:k + 1, :] * ta
                cg = cg + wg_ref[k:k + 1, :] * tg
            o_ref[pl.ds(r0, ch), :] = ((cg * _sigmoid(cg)) * ca).astype(BF16)
            return carry

        lax.fori_loop(0, nch, chunk, 0)

    blk = lambda off: pl.BlockSpec((None, s, tc), lambda bi, i: (bi, 0, i + off))
    wsp = lambda off: pl.BlockSpec((kw, tc), lambda bi, i: (0, i + off))
    bsp = lambda off: pl.BlockSpec((1, tc), lambda bi, i: (0, i + off))
    return pl.pallas_call(
        body, name=name, grid=(b, nf),
        in_specs=[blk(0), blk(nf), wsp(0), wsp(nf), bsp(0), bsp(nf)],
        out_specs=pl.BlockSpec((None, s, tc), lambda bi, i: (bi, 0, i)),
        out_shape=jax.ShapeDtypeStruct((b, s, f), BF16),
        scratch_shapes=[pltpu.VMEM((nch, ch + halo, tc), F32)] * 2,
        compiler_params=_cp("parallel", "parallel"),
    )(p, p, dw, dw, dwb, dwb)


def ffn_mid_bwd(p, dw, dwb, ds, name, tc=256):
    b, s, f2 = p.shape
    f = f2 // 2
    kw = dw.shape[0]
    tc = min(tc, f)
    nf = f // tc
    ch = min(CONV_CHUNK, s)
    nch = s // ch
    halo = FFN_HALO

    def sum8(v):
        return jnp.sum(v.reshape(ch // 8, 8, tc), axis=0)

    def body(pa_ref, pg_ref, wa_ref, wg_ref, ba_ref, bg_ref, ds_ref, dpa_ref, dpg_ref, ddwa_ref, ddwg_ref, dba_ref, dbg_ref,
             apad_ref, gpad_ref, dca_ref, dcg_ref):
        bi = pl.program_id(1)

        @pl.when(bi == 0)
        def _():
            ddwa_ref[...] = jnp.zeros(ddwa_ref.shape, F32)
            ddwg_ref[...] = jnp.zeros(ddwg_ref.shape, F32)
            dba_ref[...] = jnp.zeros(dba_ref.shape, F32)
            dbg_ref[...] = jnp.zeros(dbg_ref.shape, F32)

        _fill_front_halo(pa_ref, apad_ref, nch, ch, halo)
        _fill_front_halo(pg_ref, gpad_ref, nch, ch, halo)
        dca_ref[nch - 1, ch:ch + halo, :] = jnp.zeros((halo, tc), F32)
        dcg_ref[nch - 1, ch:ch + halo, :] = jnp.zeros((halo, tc), F32)

        def grads(ci, carry):
            acc_a, acc_g, sb_a, sb_g = carry
            r0 = pl.multiple_of(ci * ch, ch)
            taps_a = _taps_front(apad_ref, ci, kw, ch, halo)
            taps_g = _taps_front(gpad_ref, ci, kw, ch, halo)
            ca = ba_ref[...]
            cg = bg_ref[...]
            for k in range(kw):
                ca = ca + wa_ref[k:k + 1, :] * taps_a[k]
                cg = cg + wg_ref[k:k + 1, :] * taps_g[k]
            sg = _sigmoid(cg)
            dsv = ds_ref[pl.ds(r0, ch), :].astype(F32)
            dca = dsv * (cg * sg)
            dcg = dsv * ca * (sg * (1.0 + cg * (1.0 - sg)))
            dca_ref[ci, 0:ch, :] = dca
            dcg_ref[ci, 0:ch, :] = dcg

            prev = jnp.maximum(ci - 1, 0)

            @pl.when(ci > 0)
            def _():
                dca_ref[prev, ch:ch + halo, :] = dca[0:halo, :]
                dcg_ref[prev, ch:ch + halo, :] = dcg[0:halo, :]

            acc_a = tuple(acc_a[k] + sum8(dca * taps_a[k]) for k in range(kw))
            acc_g = tuple(acc_g[k] + sum8(dcg * taps_g[k]) for k in range(kw))
            return acc_a, acc_g, sb_a + sum8(dca), sb_g + sum8(dcg)

        z8 = jnp.zeros((8, tc), F32)
        acc_a, acc_g, sb_a, sb_g = lax.fori_loop(0, nch, grads, ((z8,) * kw, (z8,) * kw, z8, z8))
        for k in range(kw):
            ddwa_ref[k] += acc_a[k]
            ddwg_ref[k] += acc_g[k]
        dba_ref[...] += jnp.sum(sb_a, axis=0, keepdims=True)
        dbg_ref[...] += jnp.sum(sb_g, axis=0, keepdims=True)

        def back(ci, carry):
            r0 = pl.multiple_of(ci * ch, ch)
            da = wa_ref[kw - 1:kw, :] * dca_ref[ci, 0:ch, :]
            dg = wg_ref[kw - 1:kw, :] * dcg_ref[ci, 0:ch, :]
            for j in range(1, kw):
                da = da + wa_ref[kw - 1 - j:kw - j, :] * dca_ref[ci, j:j + ch, :]
                dg = dg + wg_ref[kw - 1 - j:kw - j, :] * dcg_ref[ci, j:j + ch, :]
            dpa_ref[pl.ds(r0, ch), :] = da.astype(BF16)
            dpg_ref[pl.ds(r0, ch), :] = dg.astype(BF16)
            return carry

        lax.fori_loop(0, nch, back, 0)

    blk = lambda off: pl.BlockSpec((None, s, tc), lambda i, bi: (bi, 0, i + off))
    wsp = lambda off: pl.BlockSpec((kw, tc), lambda i, bi: (0, i + off))
    bsp = lambda off: pl.BlockSpec((1, tc), lambda i, bi: (0, i + off))
    acc3 = pl.BlockSpec((kw, 8, tc), lambda i, bi: (0, 0, i))
    vec = pl.BlockSpec((1, tc), lambda i, bi: (0, i))
    return pl.pallas_call(
        body, name=name, grid=(nf, b),
        in_specs=[blk(0), blk(nf), wsp(0), wsp(nf), bsp(0), bsp(nf), blk(0)],
        out_specs=[blk(0), blk(0), acc3, acc3, vec, vec],
        out_shape=[jax.ShapeDtypeStruct((b, s, f), BF16), jax.ShapeDtypeStruct((b, s, f), BF16),
                   jax.ShapeDtypeStruct((kw, 8, f), F32), jax.ShapeDtypeStruct((kw, 8, f), F32),
                   jax.ShapeDtypeStruct((1, f), F32), jax.ShapeDtypeStruct((1, f), F32)],
        scratch_shapes=[pltpu.VMEM((nch, ch + halo, tc), F32)] * 4,
        compiler_params=_cp("parallel", "arbitrary"),
    )(p, p, dw, dw, dwb, dwb, ds)


def _tile_rows(r, n, dil):
    start = r + n * BLK * dil
    return pl.ds(start, BLK, stride=dil) if dil > 1 else pl.ds(start, BLK)


def _band_masks():
    qi = lax.broadcasted_iota(jnp.int32, (BLK, 2 * BLK), 0)
    kk = lax.broadcasted_iota(jnp.int32, (BLK, 2 * BLK), 1)
    both = jnp.logical_or(jnp.logical_and(kk < BLK, kk >= qi), jnp.logical_and(kk >= BLK, kk - BLK <= qi))
    return both, kk[:, :BLK] <= qi[:, :BLK]


def attn_fwd(q, kv, g, dil, hw, name):
    _, b, s, _ = q.shape
    nh = hw // HEAD_DIM
    nblk = s // dil // BLK
    scale = 1.0 / math.sqrt(HEAD_DIM)

    def body(q_ref, k_ref, v_ref, o_ref, lse_ref):
        h = pl.program_id(1)
        mask2, mask1 = _band_masks()
        mine = lax.broadcasted_iota(jnp.int32, (BLK, LANES), 1) == h

        @pl.when(h == 0)
        def _():
            lse_ref[...] = jnp.zeros(lse_ref.shape, F32)

        for r in range(dil):
            kp = vp = None
            for n in range(nblk):
                rs = _tile_rows(r, n, dil)
                qt = q_ref[rs, :].astype(BF16)
                kc = k_ref[rs, :].astype(BF16)
                vc = v_ref[rs, :].astype(BF16)
                if n == 0:
                    kcat, vcat, mask = kc, vc, mask1
                else:
                    kcat, vcat, mask = jnp.concatenate([kp, kc], axis=0), jnp.concatenate([vp, vc], axis=0), mask2
                sc = jnp.where(mask, _dot_nt(qt, kcat) * scale, NEG_INF)
                m = jnp.max(sc, axis=-1, keepdims=True)
                p = jnp.exp(sc - m)
                den = jnp.sum(p, axis=-1, keepdims=True)
                o_ref[rs, :] = _dot(p.astype(BF16), vcat) / den
                lse_ref[rs, :] = jnp.where(mine, m + jnp.log(den), lse_ref[rs, :])
                kp, vp = kc, vc

    col = lambda base: pl.BlockSpec((None, s, HEAD_DIM), lambda bi, h: (bi, 0, base + h))
    head = lambda base: pl.BlockSpec((None, None, s, HEAD_DIM), lambda bi, h: (base + h, bi, 0, 0))
    return pl.pallas_call(
        body, name=name, grid=(b, nh),
        in_specs=[head(g * nh), head(g * nh), head((N_GROUPS + g) * nh)],
        out_specs=[col(0), pl.BlockSpec((None, s, LANES), lambda bi, h: (bi, 0, 0))],
        out_shape=[jax.ShapeDtypeStruct((b, s, hw), F32), jax.ShapeDtypeStruct((b, s, LANES), F32)],
        compiler_params=_cp("parallel", "arbitrary"),
    )(q, kv, kv)


def attn_merge(outs, lses, name, tm=512):
    t, hw = outs[0].shape
    nh = hw // HEAD_DIM
    tm = _row_tile(t, tm)
    ng = len(outs)

    def body(*refs):
        o_refs, l_refs = refs[:ng], refs[ng:2 * ng]
        m_ref, lj_ref = refs[2 * ng:]
        ls = [l_refs[g][...] for g in range(ng)]
        mx = ls[0]
        for g in range(1, ng):
            mx = jnp.maximum(mx, ls[g])
        es = [jnp.exp(l - mx) for l in ls]
        tot = es[0]
        for g in range(1, ng):
            tot = tot + es[g]
        ws = [e / tot for e in es]
        lj_ref[...] = mx + jnp.log(tot)
        for h in range(nh):
            sl = slice(h * HEAD_DIM, (h + 1) * HEAD_DIM)
            acc = ws[0][:, h:h + 1] * o_refs[0][:, sl]
            for g in range(1, ng):
                acc = acc + ws[g][:, h:h + 1] * o_refs[g][:, sl]
            m_ref[:, sl] = acc.astype(BF16)

    row = pl.BlockSpec((tm, hw), lambda i: (i, 0))
    st = pl.BlockSpec((tm, LANES), lambda i: (i, 0))
    return pl.pallas_call(
        body, name=name, grid=(t // tm,), in_specs=[row] * ng + [st] * ng, out_specs=[row, st],
        out_shape=[jax.ShapeDtypeStruct((t, hw), BF16), jax.ShapeDtypeStruct((t, LANES), F32)],
        compiler_params=_cp("parallel"),
    )(*outs, *lses)


def attn_bwd_prep(dmerged, merged, name, tm=512):
    t, hw = merged.shape
    nh = hw // HEAD_DIM
    tm = _row_tile(t, tm)

    def body(d_ref, m_ref, o_ref):
        lane = lax.broadcasted_iota(jnp.int32, (tm, LANES), 1)
        acc = jnp.zeros((tm, LANES), F32)
        for h in range(nh):
            sl = slice(h * HEAD_DIM, (h + 1) * HEAD_DIM)
            dsum = jnp.sum(d_ref[:, sl] * m_ref[:, sl].astype(F32), axis=-1, keepdims=True)
            acc = jnp.where(lane == h, dsum, acc)
        o_ref[...] = acc

    row = pl.BlockSpec((tm, hw), lambda i: (i, 0))
    return pl.pallas_call(
        body, name=name, grid=(t // tm,), in_specs=[row, row], out_specs=pl.BlockSpec((tm, LANES), lambda i: (i, 0)),
        out_shape=jax.ShapeDtypeStruct((t, LANES), F32), compiler_params=_cp("parallel"),
    )(dmerged, merged)


def attn_bwd(q, kv, g, dil, do, lsej, dm, dq_buf, dk_buf, dv_buf, accumulate, hw, name):
    _, b, s, _ = q.shape
    nh = hw // HEAD_DIM
    nblk = s // dil // BLK
    scale = 1.0 / math.sqrt(HEAD_DIM)
    assert dk_buf is not None or not accumulate
    kv_at = 6 + (dq_buf is not None)

    def body(*refs):
        q_ref, k_ref, v_ref, do_ref, lj_ref, dm_ref = refs[:6]
        dq_ref, dk_ref, dv_ref = refs[-3:]
        dki_ref, dvi_ref = (refs[kv_at], refs[kv_at + 1]) if accumulate else (None, None)
        mask2, mask1 = _band_masks()
        mine = lax.broadcasted_iota(jnp.int32, (BLK, LANES), 1) == pl.program_id(1)

        def my_lane(v):
            return jnp.sum(jnp.where(mine, v, 0.0), axis=-1, keepdims=True)

        def put(rs, dk, dv):
            if accumulate:
                dk = dk + dki_ref[rs, :]
                dv = dv + dvi_ref[rs, :]
            dk_ref[rs, :] = dk
            dv_ref[rs, :] = dv

        for r in range(dil):
            kp = vp = hold_k = hold_v = rs_prev = None
            for n in range(nblk):
                rs = _tile_rows(r, n, dil)
                qt = q_ref[rs, :].astype(BF16)
                kc = k_ref[rs, :].astype(BF16)
                vc = v_ref[rs, :].astype(BF16)
                dot = do_ref[rs, :].astype(BF16)
                lm = my_lane(lj_ref[rs, :])
                dmm = my_lane(dm_ref[rs, :])
                if n == 0:
                    kcat, vcat, mask = kc, vc, mask1
                else:
                    kcat, vcat, mask = jnp.concatenate([kp, kc], axis=0), jnp.concatenate([vp, vc], axis=0), mask2
                p = jnp.exp(jnp.where(mask, _dot_nt(qt, kcat) * scale, NEG_INF) - lm)
                ds = (p * (_dot_nt(dot, vcat) - dmm)).astype(BF16)
                dq_ref[rs, :] = _dot(ds, kcat) * scale
                dkc = _dot_tn(ds, qt) * scale
                dvc = _dot_tn(p.astype(BF16), dot)
                if n > 0:
                    put(rs_prev, hold_k + dkc[:BLK, :], hold_v + dvc[:BLK, :])
                    dkc, dvc = dkc[BLK:, :], dvc[BLK:, :]
                hold_k, hold_v, kp, vp, rs_prev = dkc, dvc, kc, vc, rs
            put(rs_prev, hold_k, hold_v)

    col = lambda base: pl.BlockSpec((None, s, HEAD_DIM), lambda bi, h: (bi, 0, base + h))
    any_spec = pl.BlockSpec(memory_space=pl.ANY)
    stat = pl.BlockSpec((None, s, LANES), lambda bi, h: (bi, 0, 0))
    head = lambda base: pl.BlockSpec((None, None, s, HEAD_DIM), lambda bi, h: (base + h, bi, 0, 0))
    in_specs = [head(g * nh), head(g * nh), head((N_GROUPS + g) * nh), col(0), stat, stat]
    args = [q, kv, kv, do, lsej, dm]
    aliases = {}
    if dq_buf is not None:
        in_specs.append(any_spec)
        args.append(dq_buf)
        aliases[6] = 0
    if dk_buf is not None:
        in_specs += [col(g * nh) if accumulate else any_spec] * 2
        args += [dk_buf, dv_buf]
        aliases.update({kv_at: 1, kv_at + 1: 2})
    shape = jax.ShapeDtypeStruct((b, s, N_GROUPS * hw), F32)
    return pl.pallas_call(
        body, name=name, grid=(b, nh), in_specs=in_specs, out_specs=[col(g * nh)] * 3, out_shape=[shape] * 3,
        input_output_aliases=aliases, compiler_params=_cp("parallel", "parallel"),
    )(*args)


def sum_parts(g, recv, me, name, tm=512):
    _, rows, c = g.shape
    n = recv.shape[0]
    tm = _row_tile(rows, tm)

    def body(me_ref, g_ref, r_ref, o_ref):
        acc = g_ref[...].astype(F32)
        for j in range(n):
            acc = acc + r_ref[j].astype(F32)
        o_ref[...] = acc

    return pl.pallas_call(
        body, name=name,
        grid_spec=pltpu.PrefetchScalarGridSpec(
            num_scalar_prefetch=1, grid=(rows // tm,),
            in_specs=[pl.BlockSpec((None, tm, c), lambda i, me_ref: (me_ref[0], i, 0)),
                      pl.BlockSpec((n, tm, c), lambda i, me_ref: (0, i, 0))],
            out_specs=pl.BlockSpec((tm, c), lambda i, me_ref: (i, 0))),
        out_shape=jax.ShapeDtypeStruct((rows, c), F32), compiler_params=_cp("parallel"),
    )(me, g, recv)


def adamw(w, m, v, g_parts, name, tm=256):
    rows, c = w.shape
    if c <= 8 * LANES:
        tm = 2 * tm
    tm = _row_tile(rows, tm)
    npart = len(g_parts)

    def body(*refs):
        w_ref, m_ref, v_ref = refs[:3]
        g_refs = refs[3:3 + npart]
        go_ref, d_ref, mo_ref, vo_ref = refs[3 + npart:]
        g = g_refs[0][...]
        for k in range(1, npart):
            g = g + g_refs[k][...]
        mn = ADAM_B1 * m_ref[...] + (1.0 - ADAM_B1) * g
        vn = ADAM_B2 * v_ref[...] + (1.0 - ADAM_B2) * (g * g)
        m_hat = mn / (1.0 - ADAM_B1 ** ADAM_STEP)
        v_hat = vn / (1.0 - ADAM_B2 ** ADAM_STEP)
        go_ref[...] = g
        d_ref[...] = -ADAM_LR * (m_hat / (jnp.sqrt(v_hat) + ADAM_EPS) + ADAM_WD * w_ref[...])
        mo_ref[...] = mn
        vo_ref[...] = vn

    row = pl.BlockSpec((tm, c), lambda i: (i, 0))
    return pl.pallas_call(
        body, name=name, grid=(rows // tm,), in_specs=[row] * (3 + npart), out_specs=[row] * 4,
        out_shape=[jax.ShapeDtypeStruct((rows, c), F32)] * 4, compiler_params=_cp("parallel"),
    )(w, m, v, *g_parts)


def _place():
    return lax.axis_index("x"), lax.axis_index("y"), lax.axis_index("c")


def _other_chips(x, y, c):
    return [(1 - x, y, c), (x, 1 - y, c), (1 - x, 1 - y, c)]


def _chip_of(px, py):
    return 2 * px + py


HBM_SPEC = pl.BlockSpec(memory_space=pltpu.HBM)
SEM_SPEC = pl.BlockSpec(memory_space=pltpu.SEMAPHORE)
ANY_SPEC = pl.BlockSpec(memory_space=pl.ANY)
DATAFLOW = pltpu.SideEffectType.DATAFLOW_SIDE_EFFECTING
N_PEER_CHIPS = N_CHIPS - 1


def _hbm(a):
    return pltpu.with_memory_space_constraint(a, pltpu.HBM)


def _hbm_like(arrays):
    return [pltpu.HBM(a.shape, a.dtype) for a in arrays]


def cast_place(w, layer, me, out_dtype, name, tm=512, nslots=N_CHIPS, dep=None):
    rows, c = w.shape[-2:]
    tm = _row_tile(rows, tm)

    def body(me_ref, w_ref, *rest):
        rest[-1][...] = w_ref[...].astype(out_dtype)

    if layer is None:
        in_specs = [pl.BlockSpec((tm, c), lambda i, me_ref: (i, 0))]
    else:
        in_specs = [pl.BlockSpec((None, tm, c), lambda i, me_ref: (layer, i, 0))]
    args = [me, w]
    if dep is not None:
        in_specs.append(pl.BlockSpec(DEP_SPEC_SHAPE, lambda i, me_ref: (0, 0)))
        args.append(dep)
    return pl.pallas_call(
        body, name=name,
        grid_spec=pltpu.PrefetchScalarGridSpec(
            num_scalar_prefetch=1, grid=(rows // tm,), in_specs=in_specs,
            out_specs=pl.BlockSpec((None, tm, c), lambda i, me_ref: (me_ref[0], i, 0))),
        out_shape=jax.ShapeDtypeStruct((nslots, rows, c), out_dtype), compiler_params=_cp("parallel"),
    )(*args)


def gather_start(lands, chunk_sizes, name="gather_start"):
    n = len(lands)
    nch = len(chunk_sizes)
    assert sum(chunk_sizes) == n

    def body(*refs):
        land_refs = refs[:n]
        outs = refs[n:]
        send_sems, recv_sems = outs[:nch], outs[nch:2 * nch]
        token = outs[-1]
        x, y, c = _place()
        me = _chip_of(x, y)
        peers = _other_chips(x, y, c)
        k = 0
        for ck, size in enumerate(chunk_sizes):
            for pos in range(size):
                for r, peer in enumerate(peers):
                    pltpu.make_async_remote_copy(
                        src_ref=land_refs[k].at[me], dst_ref=land_refs[k].at[me],
                        send_sem=send_sems[ck].at[N_PEER_CHIPS * pos + r], recv_sem=recv_sems[ck].at[N_PEER_CHIPS * pos + r],
                        device_id=peer, device_id_type=MESH).start()
                k += 1
        token[...] = jnp.zeros(token.shape, F32)

    sems = [pltpu.SemaphoreType.DMA((N_PEER_CHIPS * s,)) for s in chunk_sizes]
    res = pl.pallas_call(
        body, name=name,
        out_shape=(*sems, *sems, *_hbm_like(lands), jax.ShapeDtypeStruct(DEP_SPEC_SHAPE, F32)),
        in_specs=[HBM_SPEC] * n,
        out_specs=(*[SEM_SPEC] * (2 * nch), *[HBM_SPEC] * n, pl.BlockSpec(memory_space=pltpu.VMEM)),
        input_output_aliases={k: 2 * nch + k for k in range(n)},
        compiler_params=pltpu.CompilerParams(has_side_effects=DATAFLOW),
    )(*[_hbm(a) for a in lands])
    return res[:nch], res[nch:2 * nch], res[2 * nch:2 * nch + n], res[-1]


def gather_wait(send_sem, recv_sem, lands, after, name):
    n = len(lands)

    def body(*refs):
        land_refs = refs[:n]
        ssem, rsem = refs[n], refs[n + 1]
        x, y, c = _place()
        me = _chip_of(x, y)
        for pos in range(n):
            for r, peer in enumerate(_other_chips(x, y, c)):
                cp = pltpu.make_async_remote_copy(
                    src_ref=land_refs[pos].at[me], dst_ref=land_refs[pos].at[_chip_of(peer[0], peer[1])],
                    send_sem=ssem.at[N_PEER_CHIPS * pos + r], recv_sem=rsem.at[N_PEER_CHIPS * pos + r],
                    device_id=peer, device_id_type=MESH)
                cp.wait_send()
                cp.wait_recv()

    return pl.pallas_call(
        body, name=name, out_shape=tuple(_hbm_like(lands)),
        in_specs=[*[HBM_SPEC] * n, SEM_SPEC, SEM_SPEC, ANY_SPEC], out_specs=[HBM_SPEC] * n,
        input_output_aliases={k: k for k in range(n)},
        compiler_params=pltpu.CompilerParams(has_side_effects=DATAFLOW),
    )(*lands, send_sem, recv_sem, after)


def scatter_start(grads, name):
    n = len(grads)
    recvs = [lax.empty((N_PEER_CHIPS, *g.shape[1:]), g.dtype) for g in grads]

    def body(*refs):
        g_refs, r_refs = refs[:n], refs[n:2 * n]
        send_sems, recv_sems = refs[2 * n], refs[2 * n + 1]
        token = refs[-1]
        x, y, c = _place()
        for k in range(n):
            for r, peer in enumerate(_other_chips(x, y, c)):
                pltpu.make_async_remote_copy(
                    src_ref=g_refs[k].at[_chip_of(peer[0], peer[1])], dst_ref=r_refs[k].at[r],
                    send_sem=send_sems.at[N_PEER_CHIPS * k + r], recv_sem=recv_sems.at[N_PEER_CHIPS * k + r],
                    device_id=peer, device_id_type=MESH).start()
        token[...] = jnp.zeros(token.shape, F32)

    sem = pltpu.SemaphoreType.DMA((N_PEER_CHIPS * n,))
    res = pl.pallas_call(
        body, name=name,
        out_shape=(sem, sem, *_hbm_like(grads), *_hbm_like(recvs), jax.ShapeDtypeStruct(DEP_SPEC_SHAPE, F32)),
        in_specs=[HBM_SPEC] * (2 * n),
        out_specs=(SEM_SPEC, SEM_SPEC, *[HBM_SPEC] * (2 * n), pl.BlockSpec(memory_space=pltpu.VMEM)),
        input_output_aliases={k: 2 + k for k in range(2 * n)},
        compiler_params=pltpu.CompilerParams(has_side_effects=DATAFLOW),
    )(*[_hbm(a) for a in grads], *[_hbm(a) for a in recvs])
    return res[0], res[1], res[2:2 + n], res[2 + n:2 + 2 * n], res[-1]


def scatter_wait(send_sem, recv_sem, grads, recvs, after, name):
    n = len(grads)

    def body(*refs):
        g_refs, r_refs = refs[:n], refs[n:2 * n]
        ssem, rsem = refs[2 * n], refs[2 * n + 1]
        x, y, c = _place()
        for k in range(n):
            for r, peer in enumerate(_other_chips(x, y, c)):
                cp = pltpu.make_async_remote_copy(
                    src_ref=g_refs[k].at[_chip_of(peer[0], peer[1])], dst_ref=r_refs[k].at[r],
                    send_sem=ssem.at[N_PEER_CHIPS * k + r], recv_sem=rsem.at[N_PEER_CHIPS * k + r],
                    device_id=peer, device_id_type=MESH)
                cp.wait_send()
                cp.wait_recv()

    res = pl.pallas_call(
        body, name=name, out_shape=(*_hbm_like(grads), *_hbm_like(recvs)),
        in_specs=[*[HBM_SPEC] * (2 * n), SEM_SPEC, SEM_SPEC, ANY_SPEC], out_specs=[HBM_SPEC] * (2 * n),
        input_output_aliases={k: k for k in range(2 * n)},
        compiler_params=pltpu.CompilerParams(has_side_effects=DATAFLOW),
    )(*grads, *recvs, send_sem, recv_sem, after)
    return res[:n], res[n:]


def swap_start(parts, name):
    n = len(parts)
    lands = [lax.empty(p.shape, p.dtype) for p in parts]

    def body(*refs):
        p_refs, l_refs = refs[:n], refs[n:2 * n]
        sems = refs[2 * n:4 * n]
        token = refs[-1]
        x, y, c = _place()
        for k in range(n):
            pltpu.make_async_remote_copy(
                src_ref=p_refs[k], dst_ref=l_refs[k], send_sem=sems[k], recv_sem=sems[n + k],
                device_id=(x, y, 1 - c), device_id_type=MESH).start()
        token[...] = jnp.zeros(token.shape, F32)

    sem = pltpu.SemaphoreType.DMA(())
    res = pl.pallas_call(
        body, name=name,
        out_shape=(*[sem] * (2 * n), *_hbm_like(parts), *_hbm_like(lands), jax.ShapeDtypeStruct(DEP_SPEC_SHAPE, F32)),
        in_specs=[HBM_SPEC] * (2 * n),
        out_specs=(*[SEM_SPEC] * (2 * n), *[HBM_SPEC] * (2 * n), pl.BlockSpec(memory_space=pltpu.VMEM)),
        input_output_aliases={k: 2 * n + k for k in range(2 * n)},
        compiler_params=pltpu.CompilerParams(has_side_effects=DATAFLOW),
    )(*[_hbm(a) for a in parts], *[_hbm(a) for a in lands])
    return res[:n], res[n:2 * n], res[2 * n:3 * n], res[3 * n:4 * n], res[-1]


def swap_wait(send_sem, recv_sem, part, land, after, name):
    def body(p_ref, l_ref, ssem, rsem, after_ref, p_out, l_out):
        x, y, c = _place()
        cp = pltpu.make_async_remote_copy(src_ref=p_ref, dst_ref=l_ref, send_sem=ssem, recv_sem=rsem,
                                          device_id=(x, y, 1 - c), device_id_type=MESH)
        cp.wait_send()
        cp.wait_recv()

    return pl.pallas_call(
        body, name=name, out_shape=tuple(_hbm_like([part, land])),
        in_specs=[HBM_SPEC, HBM_SPEC, SEM_SPEC, SEM_SPEC, ANY_SPEC], out_specs=[HBM_SPEC, HBM_SPEC],
        input_output_aliases={0: 0, 1: 1},
        compiler_params=pltpu.CompilerParams(has_side_effects=DATAFLOW),
    )(part, land, send_sem, recv_sem, after)


def _xor_peer(x, y, c, k):
    px, py, pc = x ^ ((k >> 2) & 1), y ^ ((k >> 1) & 1), c ^ (k & 1)
    return (px, py, pc), 4 * px + 2 * py + pc


def small_start(land, name="small_start"):
    def body(l_ref, ssem, rsem, l_out, token):
        x, y, c = _place()
        me = 4 * x + 2 * y + c
        for k in range(1, N_DEV):
            peer, _ = _xor_peer(x, y, c, k)
            pltpu.make_async_remote_copy(
                src_ref=l_ref.at[me], dst_ref=l_ref.at[me], send_sem=ssem.at[k - 1], recv_sem=rsem.at[k - 1],
                device_id=peer, device_id_type=MESH).start()
        token[...] = jnp.zeros(token.shape, F32)

    sem = pltpu.SemaphoreType.DMA((N_DEV - 1,))
    return pl.pallas_call(
        body, name=name,
        out_shape=(sem, sem, pltpu.HBM(land.shape, land.dtype), jax.ShapeDtypeStruct(DEP_SPEC_SHAPE, F32)),
        in_specs=[HBM_SPEC], out_specs=(SEM_SPEC, SEM_SPEC, HBM_SPEC, pl.BlockSpec(memory_space=pltpu.VMEM)),
        input_output_aliases={0: 2}, compiler_params=pltpu.CompilerParams(has_side_effects=DATAFLOW),
    )(_hbm(land))


def small_wait(send_sem, recv_sem, land, after, name="small_wait"):
    def body(l_ref, ssem, rsem, after_ref, l_out):
        x, y, c = _place()
        me = 4 * x + 2 * y + c
        for k in range(1, N_DEV):
            peer, slot = _xor_peer(x, y, c, k)
            cp = pltpu.make_async_remote_copy(
                src_ref=l_ref.at[me], dst_ref=l_ref.at[slot], send_sem=ssem.at[k - 1], recv_sem=rsem.at[k - 1],
                device_id=peer, device_id_type=MESH)
            cp.wait_send()
            cp.wait_recv()

    return pl.pallas_call(
        body, name=name, out_shape=pltpu.HBM(land.shape, land.dtype),
        in_specs=[HBM_SPEC, SEM_SPEC, SEM_SPEC, ANY_SPEC], out_specs=HBM_SPEC, input_output_aliases={0: 0},
        compiler_params=pltpu.CompilerParams(has_side_effects=DATAFLOW),
    )(land, send_sem, recv_sem, after)


def sum_slots(land, name="sum_slots", tm=256):
    n, rows, c = land.shape
    tm = _row_tile(rows, tm)

    def body(l_ref, o_ref):
        acc = l_ref[0]
        for j in range(1, n):
            acc = acc + l_ref[j]
        o_ref[...] = acc

    return pl.pallas_call(
        body, name=name, grid=(rows // tm,), in_specs=[pl.BlockSpec((n, tm, c), lambda i: (0, i, 0))],
        out_specs=pl.BlockSpec((tm, c), lambda i: (i, 0)), out_shape=jax.ShapeDtypeStruct((rows, c), F32),
        compiler_params=_cp("parallel"),
    )(land)


PACK_ROW_TILE = 256


def _pack(arrays):
    flat = jnp.concatenate([a.reshape(-1).astype(F32) for a in arrays])
    n = flat.shape[0]
    rows = -(-n // LANES)
    rows = -(-rows // PACK_ROW_TILE) * PACK_ROW_TILE
    return jnp.pad(flat, (0, rows * LANES - n)).reshape(rows, LANES)


def _unpack(packed, shapes, lead=()):
    flat = packed.reshape(*lead, -1)
    out, off = [], 0
    for shp in shapes:
        n = math.prod(shp)
        out.append(flat[..., off:off + n].reshape(*lead, *shp))
        off += n
    return out


def _row(vec):
    return vec.reshape(1, -1)


def kernel(x, mix_pre_g, mix_post_g, ffn_pre_g, ffn_post_g, cm_w_in, cm_b_in, cm_dw, cm_dw_b, cm_ln_g, cm_ln_b, cm_w_out, cm_b_out, kv_norm_g, w_kv, w_q, w_o, ffn_w_in, ffn_dw, ffn_dw_b, ffn_w_out, loss_target, m_mix_pre_g, m_mix_post_g, m_ffn_pre_g, m_ffn_post_g, m_cm_w_in, m_cm_b_in, m_cm_dw, m_cm_dw_b, m_cm_ln_g, m_cm_ln_b, m_cm_w_out, m_cm_b_out, m_kv_norm_g, m_w_kv, m_w_q, m_w_o, m_ffn_w_in, m_ffn_dw, m_ffn_dw_b, m_ffn_w_out, v_mix_pre_g, v_mix_post_g, v_ffn_pre_g, v_ffn_post_g, v_cm_w_in, v_cm_b_in, v_cm_dw, v_cm_dw_b, v_cm_ln_g, v_cm_ln_b, v_cm_w_out, v_cm_b_out, v_kv_norm_g, v_w_kv, v_w_q, v_w_o, v_ffn_w_in, v_ffn_dw, v_ffn_dw_b, v_ffn_w_out):
    names = ["mix_pre_g", "mix_post_g", "ffn_pre_g", "ffn_post_g", "cm_w_in", "cm_b_in", "cm_dw", "cm_dw_b", "cm_ln_g",
             "cm_ln_b", "cm_w_out", "cm_b_out", "kv_norm_g", "w_kv", "w_q", "w_o", "ffn_w_in", "ffn_dw", "ffn_dw_b",
             "ffn_w_out"]
    w_in = dict(zip(names, [mix_pre_g, mix_post_g, ffn_pre_g, ffn_post_g, cm_w_in, cm_b_in, cm_dw, cm_dw_b, cm_ln_g,
                            cm_ln_b, cm_w_out, cm_b_out, kv_norm_g, w_kv, w_q, w_o, ffn_w_in, ffn_dw, ffn_dw_b, ffn_w_out]))
    m_in = dict(zip(names, [m_mix_pre_g, m_mix_post_g, m_ffn_pre_g, m_ffn_post_g, m_cm_w_in, m_cm_b_in, m_cm_dw, m_cm_dw_b,
                            m_cm_ln_g, m_cm_ln_b, m_cm_w_out, m_cm_b_out, m_kv_norm_g, m_w_kv, m_w_q, m_w_o, m_ffn_w_in,
                            m_ffn_dw, m_ffn_dw_b, m_ffn_w_out]))
    v_in = dict(zip(names, [v_mix_pre_g, v_mix_post_g, v_ffn_pre_g, v_ffn_post_g, v_cm_w_in, v_cm_b_in, v_cm_dw, v_cm_dw_b,
                            v_cm_ln_g, v_cm_ln_b, v_cm_w_out, v_cm_b_out, v_kv_norm_g, v_w_kv, v_w_q, v_w_o, v_ffn_w_in,
                            v_ffn_dw, v_ffn_dw_b, v_ffn_w_out]))

    bsz, seq, d = x.shape
    t = bsz * seq
    n_b = DEPTH - N_A
    hw = w_o.shape[-1]
    qw = N_GROUPS * hw
    f2 = ffn_dw_b.shape[-1]
    f = f2 // 2
    me_chip = _chip_of(lax.axis_index("x"), lax.axis_index("y"))

    big = ["cm_w_in", "cm_w_out", "w_kv", "w_q", "w_o", "ffn_w_in", "ffn_w_out"]
    row_sharded = ("cm_w_out", "w_o", "ffn_w_out")
    small_sharded = ["cm_b_in", "cm_dw", "cm_dw_b", "cm_ln_g", "cm_ln_b", "cm_b_out", "ffn_dw"]
    small_pack = _pack([w_in[n] for n in small_sharded])
    chunks = [
        [("cm_w_in", 0), ("small", None)],
        [("cm_w_out", 0)],
        [("ffn_w_in", 0), ("ffn_w_out", 0)],
        [("cm_w_in", 1), ("cm_w_out", 1)],
        [("ffn_w_in", 1), ("ffn_w_out", 1)],
        [("w_kv", None)],
        [("w_q", 0), ("w_o", 0)],
        [("ffn_w_in", 2), ("ffn_w_out", 2)],
        [("w_q", 1), ("w_o", 1)],
        [("ffn_w_in", 3), ("ffn_w_out", 3)],
    ]
    pieces = [pc for ch in chunks for pc in ch]
    chunk_of = {pc: ck for ck, ch in enumerate(chunks) for pc in ch}

    me_arr = me_chip.astype(jnp.int32).reshape(1)

    def land_of(pc, dep=None):
        n, l = pc
        if n == "small":
            return cast_place(small_pack, None, me_arr, F32, name="place_small", dep=dep)
        return cast_place(w_in[n], l, me_arr, BF16, name=f"place_{n}_{l}", dep=dep)

    groups = [[0, 1], [2], list(range(3, len(chunks)))]
    g_send, g_recv, lands_f = {}, {}, {}

    def start_group(gi, dep, name):
        cks = groups[gi]
        pcs = [pc for ck in cks for pc in chunks[ck]]
        lands = [land_of(pc, dep if k == 0 else None) for k, pc in enumerate(pcs)]
        send, recv, lands_thru, tok = gather_start(lands, [len(chunks[ck]) for ck in cks], name=name)
        pos = 0
        for j, ck in enumerate(cks):
            g_send[ck], g_recv[ck] = send[j], recv[j]
            lands_f[ck] = lands_thru[pos:pos + len(chunks[ck])]
            pos += len(chunks[ck])
        return tok

    token_a = start_group(0, None, "gather_start_a")
    token = start_group(1, token_a, "gather_start_b")
    weights = {}

    def finish_chunk(ck, after):
        got = gather_wait(g_send[ck], g_recv[ck], lands_f[ck], after, name=f"gather_wait{ck}")
        for pc, arr in zip(chunks[ck], got):
            weights[pc] = arr.reshape(1, -1, arr.shape[-1]) if pc[0] in row_sharded else arr

    def wmat(n, l=None, after=None):
        if (n, l) not in weights:
            finish_chunk(chunk_of[(n, l)], after)
        arr = weights[(n, l)]
        return arr, arr.shape[0]

    finish_chunk(0, token)
    small_full = {}
    for n, arr4 in zip(small_sharded, _unpack(weights[("small", None)], [w_in[n].shape for n in small_sharded], lead=(N_CHIPS,))):
        shp = w_in[n].shape
        small_full[n] = jnp.moveaxis(arr4, 0, -2).reshape(*shp[:-1], N_CHIPS * shp[-1])

    x2d = x.reshape(t, d)
    saved = []
    h1, h1t = resid_norm_fwd(x2d, None, None, [_row(mix_pre_g[0])], name="norm_in", dep=token)
    xcur = x2d
    kv_state = None
    for i in range(DEPTH):
        sv = {"x_in": xcur, "h1": h1, "h1t": h1t}
        if i < N_A:
            z = mm_nn(h1, *wmat("cm_w_in", i, h1), 1, 0, bias=_row(small_full["cm_b_in"][i]), name=f"cm_in{i}",
                      spg=N_CHIPS)
            u2 = glu_conv_fwd(z.reshape(bsz, seq, 2 * d), small_full["cm_dw"][i], _row(small_full["cm_dw_b"][i]),
                              name=f"glu_conv{i}").reshape(t, d)
            u4 = ln_silu_fwd(u2, _row(small_full["cm_ln_g"][i]), _row(small_full["cm_ln_b"][i]), name=f"ln_silu{i}")
            y = mm_nn(u4, *wmat("cm_w_out", i, u4), 1, 0, bias=_row(small_full["cm_b_out"][i]), out_dtype=BF16,
                      name=f"cm_out{i}")
            sv.update(z=z, u2=u2, u4=u4)
        else:
            j = i - N_A
            q = mm_nn(h1, *wmat("w_q", j, h1), 1, 0, name=f"q_proj{j}", head_major=True, spg=N_CHIPS).reshape(-1, bsz, seq, HEAD_DIM)
            outs, lses = [], []
            for g, dil in enumerate(DILATIONS):
                o_g, l_g = attn_fwd(q, kv_state["kv"], g, dil, hw, name=f"attn_fwd{j}_{g}")
                outs.append(o_g.reshape(t, hw))
                lses.append(l_g.reshape(t, LANES))
            merged, lsej = attn_merge(outs, lses, name=f"attn_merge{j}")
            y = mm_nn(merged, *wmat("w_o", j, merged), 1, 0, out_dtype=BF16, name=f"o_proj{j}")
            sv.update(q=q, merged=merged, lsej=lsej)
        x1, h2, h2t = resid_norm_fwd(xcur, y, _row(mix_post_g[i]), [_row(ffn_pre_g[i])], name=f"resid_mix{i}")
        ffn_after = start_group(2, x1, "gather_start_c") if i == 0 else h2
        p = mm_nn(h2, *wmat("ffn_w_in", i, ffn_after), 1, 0, out_dtype=BF16, name=f"ffn_in{i}", spg=2)
        s_act = ffn_mid_fwd(p.reshape(bsz, seq, f2), small_full["ffn_dw"][i], _row(ffn_dw_b[i]), name=f"ffn_mid{i}").reshape(t, f)
        y2 = mm_nn(s_act, *wmat("ffn_w_out", i), 1, 0, out_dtype=BF16, name=f"ffn_out{i}")
        next_gains = []
        if i + 1 < DEPTH:
            next_gains.append(_row(mix_pre_g[i + 1]))
        if i == N_A - 1:
            next_gains.append(_row(kv_norm_g))
        res = resid_norm_fwd(x1, y2, _row(ffn_post_g[i]), next_gains, name=f"resid_ffn{i}")
        sv.update(y=y, x1=x1, h2t=h2t, p=p, s=s_act, y2=y2)
        saved.append(sv)
        xcur = res[0]
        n_next = len(next_gains)
        if i + 1 < DEPTH:
            h1, h1t = res[1], res[1 + n_next]
        if i == N_A - 1:
            kvn, kvnt = res[2], res[2 + n_next]
            kv = mm_nn(kvn, *wmat("w_kv", None, kvn), 1, 0, name="kv_proj", head_major=True, spg=2).reshape(-1, bsz, seq, HEAD_DIM)
            kv_state = {"kv": kv, "kvnt": kvnt, "x_a": xcur}

    dx, loss_tile = loss_fwd_bwd(xcur, loss_target.reshape(t, d))
    loss = lax.psum(loss_tile[0, 0], ("x", "y", "c"))

    gsm = {n: [None] * w_in[n].shape[0] for n in
           ["mix_pre_g", "mix_post_g", "ffn_pre_g", "ffn_post_g", "cm_b_in", "cm_dw", "cm_dw_b", "cm_ln_g", "cm_ln_b",
            "cm_b_out", "ffn_dw", "ffn_dw_b"]}
    gbig = {}
    in_flight = []
    dep = None

    def start_scatter(pcs, tag):
        ssem, rsem, g_f, r_f, tok = scatter_start([gbig[pc] for pc in pcs], name=f"scatter_start_{tag}")
        in_flight.append((pcs, ssem, rsem, g_f, r_f))
        return tok

    dk_buf = dv_buf = None
    for i in range(DEPTH - 1, -1, -1):
        sv = saved[i]
        dy2, dg, _ = norm_bwd(sv["y2"], _row(ffn_post_g[i]), dx, out_dtype=BF16, name=f"bwd_ffn_post{i}", dep=dep)
        gsm["ffn_post_g"][i] = dg
        ds = mm_nt(dy2, *wmat("ffn_w_out", i), 1, 0, out_dtype=BF16, name=f"bwd_ffn_out_dx{i}")
        gbig[("ffn_w_out", i)] = mm_tn(sv["s"], dy2, 1, name=f"bwd_ffn_out_dw{i}").reshape(N_CHIPS, f // N_CHIPS, d)
        dpa, dpg, ddwa, ddwg, ddba, ddbg = ffn_mid_bwd(sv["p"].reshape(bsz, seq, f2), small_full["ffn_dw"][i], _row(ffn_dw_b[i]),
                                                       ds.reshape(bsz, seq, f), name=f"bwd_ffn_mid{i}")
        gsm["ffn_dw"][i] = jnp.concatenate([jnp.sum(ddwa, axis=1), jnp.sum(ddwg, axis=1)], axis=-1)
        gsm["ffn_dw_b"][i] = jnp.concatenate([ddba, ddbg], axis=-1)
        dp = [dpa.reshape(t, f), dpg.reshape(t, f)]
        dh2 = mm_nt(dp, *wmat("ffn_w_in", i), 1, 0, out_dtype=BF16, name=f"bwd_ffn_in_dx{i}", spg=2)
        gbig[("ffn_w_in", i)] = mm_tn(sv["h2t"], dp, N_CHIPS, name=f"bwd_ffn_in_dw{i}", spg=2, a_transposed=True)
        dx1, dg, _ = norm_bwd(sv["x1"], _row(ffn_pre_g[i]), dh2, add=dx, name=f"bwd_ffn_pre{i}")
        gsm["ffn_pre_g"][i] = dg
        dep = start_scatter([("ffn_w_in", 0), ("ffn_w_out", 0)], "ffn0") if i == 0 else None
        dy, dg, dbias = norm_bwd(sv["y"], _row(mix_post_g[i]), dx1, out_dtype=BF16, name=f"bwd_mix_post{i}", dep=dep)
        gsm["mix_post_g"][i] = dg
        if i < N_A:
            gsm["cm_b_out"][i] = dbias
            du4 = mm_nt(dy, *wmat("cm_w_out", i), 1, 0, out_dtype=BF16, name=f"bwd_cm_out_dx{i}")
            gbig[("cm_w_out", i)] = mm_tn(sv["u4"], dy, 1, name=f"bwd_cm_out_dw{i}").reshape(N_CHIPS, d // N_CHIPS, d)
            du2, dlg, dlb = ln_silu_bwd(sv["u2"], _row(small_full["cm_ln_g"][i]), _row(small_full["cm_ln_b"][i]), du4,
                                        name=f"bwd_ln_silu{i}")
            gsm["cm_ln_g"][i], gsm["cm_ln_b"][i] = dlg, dlb
            dza, dzg, ddw, ddwb, dba, dbg = glu_conv_bwd(sv["z"].reshape(bsz, seq, 2 * d), small_full["cm_dw"][i],
                                                         du2.reshape(bsz, seq, d), name=f"bwd_glu_conv{i}")
            gsm["cm_dw"][i] = jnp.sum(ddw, axis=1)
            gsm["cm_dw_b"][i] = ddwb
            gsm["cm_b_in"][i] = jnp.concatenate([dba, dbg], axis=-1)
            dz = [dza.reshape(t, d), dzg.reshape(t, d)]
            dh1 = mm_nt(dz, *wmat("cm_w_in", i), 1, 0, out_dtype=BF16, name=f"bwd_cm_in_dx{i}", spg=2)
            gbig[("cm_w_in", i)] = mm_tn(sv["h1t"], dz, N_CHIPS, name=f"bwd_cm_in_dw{i}", spg=2, a_transposed=True)
        else:
            j = i - N_A
            dmerged = mm_nt(dy, *wmat("w_o", j), 1, 0, name=f"bwd_o_proj_dx{j}")
            gbig[("w_o", j)] = mm_tn(sv["merged"], dy, 1, name=f"bwd_o_proj_dw{j}").reshape(N_CHIPS, hw // N_CHIPS, d)
            dmt = attn_bwd_prep(dmerged, sv["merged"], name=f"bwd_attn_prep{j}")
            dq_buf = None
            add_to_kv = dk_buf is not None
            for g, dil in enumerate(DILATIONS):
                dq_buf, dk_buf, dv_buf = attn_bwd(
                    sv["q"], kv_state["kv"], g, dil, dmerged.reshape(bsz, seq, hw), sv["lsej"].reshape(bsz, seq, LANES),
                    dmt.reshape(bsz, seq, LANES), dq_buf, dk_buf, dv_buf, add_to_kv, hw, name=f"attn_bwd{j}_{g}")
            dq = dq_buf.reshape(t, qw)
            dh1 = mm_nt(dq, *wmat("w_q", j), 1, 0, out_dtype=BF16, name=f"bwd_q_proj_dx{j}", tm=512, spg=N_CHIPS)
            gbig[("w_q", j)] = mm_tn(sv["h1t"], dq, N_CHIPS, name=f"bwd_q_proj_dw{j}", tm=512, spg=N_CHIPS, a_transposed=True)
        dx, dg, _ = norm_bwd(sv["x_in"], _row(mix_pre_g[i]), dh1, add=dx1, name=f"bwd_mix_pre{i}")
        gsm["mix_pre_g"][i] = dg
        if i > N_A:
            dep = start_scatter([("ffn_w_in", i), ("ffn_w_out", i), ("w_q", i - N_A), ("w_o", i - N_A)], f"l{i}")
        elif 0 < i < N_A:
            dep = start_scatter([("ffn_w_in", i), ("ffn_w_out", i), ("cm_w_in", i), ("cm_w_out", i)], f"l{i}")
        elif i == 0:
            last_token = start_scatter([("cm_w_in", 0), ("cm_w_out", 0)], "cm0")
        if i == N_A:
            dkv = [dk_buf.reshape(t, qw), dv_buf.reshape(t, qw)]
            dkvn = mm_nt(dkv, *wmat("w_kv"), 1, 0, out_dtype=BF16, name="bwd_kv_proj_dx")
            gbig[("w_kv", None)] = mm_tn(kv_state["kvnt"], dkv, N_CHIPS, name="bwd_kv_proj_dw", a_transposed=True)
            dx, dg_kv, _ = norm_bwd(kv_state["x_a"], _row(kv_norm_g), dkvn, add=dx, name="bwd_kv_norm")
            dep = start_scatter([("ffn_w_in", i), ("ffn_w_out", i), ("w_q", 0), ("w_o", 0), ("w_kv", None)], f"l{i}")
    grad_x = dx.reshape(bsz, seq, d)

    plane_of = {}
    outs_g, outs_d, outs_m, outs_v = {}, {}, {}, {}

    def finish_scatter(k, after):
        pcs, ssem, rsem, g_f, r_f = in_flight[k]
        g_done, r_done = scatter_wait(ssem, rsem, g_f, r_f, after, name=f"scatter_wait{k}")
        for pc, g_arr, r_arr in zip(pcs, g_done, r_done):
            plane_of[pc] = sum_parts(g_arr, r_arr, me_arr, name=f"sum_chips_{pc[0]}_{pc[1]}")

    def update(group, tag, after):
        plane = []
        for n in group:
            if w_in[n].ndim == 2:
                plane.append(plane_of[(n, None)])
            else:
                plane.append(jnp.concatenate([plane_of[(n, l)] for l in range(w_in[n].shape[0])], axis=0))
        ssems, rsems, plane_f, land_f, _ = swap_start(plane, name=f"swap_start_{tag}")
        for k, n in enumerate(group):
            p_mine, p_other = swap_wait(ssems[k], rsems[k], plane_f[k], land_f[k], after, name=f"swap_wait_{n}")
            shp = w_in[n].shape
            flat = lambda a: a.reshape(-1, shp[-1])
            g_, d_, m_, v_ = adamw(flat(w_in[n]), flat(m_in[n]), flat(v_in[n]), [p_mine, p_other], name=f"adamw_{n}")
            outs_g[n], outs_d[n], outs_m[n], outs_v[n] = (a.reshape(shp) for a in (g_, d_, m_, v_))
            after = v_
        return after

    small_names = [n for n in names if n not in big]
    small_shapes_full = {}
    small_grads_full = []
    for n in small_names:
        if n == "kv_norm_g":
            gfull = dg_kv.reshape(-1)
        elif n in ("cm_dw", "ffn_dw"):
            gfull = jnp.stack(gsm[n], axis=0)
        else:
            gfull = jnp.stack([a.reshape(-1) for a in gsm[n]], axis=0)
        small_shapes_full[n] = gfull.shape
        small_grads_full.append(gfull)
    dev_arr = (4 * lax.axis_index("x") + 2 * lax.axis_index("y") + lax.axis_index("c")).astype(jnp.int32).reshape(1)
    small_land = cast_place(_pack(small_grads_full) + last_token[0, 0], None, dev_arr, F32, name="place_small_grads",
                            nslots=N_DEV)
    sm_send, sm_recv, small_land, small_token = small_start(small_land)

    for k in range(len(in_flight) - 1):
        finish_scatter(k, small_token)
    done = update(["w_kv", "w_q", "w_o", "ffn_w_in", "ffn_w_out"], "a", small_token)
    finish_scatter(len(in_flight) - 1, done)
    done = update(["cm_w_in", "cm_w_out"], "b", done)

    summed = sum_slots(small_wait(sm_send, sm_recv, small_land, done))
    g_full = dict(zip(small_names, _unpack(summed, [small_shapes_full[n] for n in small_names])))
    g_loc = {}
    for n in small_names:
        if n in small_sharded:
            width = w_in[n].shape[-1]
            g_loc[n] = lax.dynamic_slice_in_dim(g_full[n], me_chip * width, width, axis=g_full[n].ndim - 1)
        else:
            g_loc[n] = g_full[n]
    res = adamw(_pack([w_in[n] for n in small_names]), _pack([m_in[n] for n in small_names]),
                _pack([v_in[n] for n in small_names]), [_pack([g_loc[n] for n in small_names])], name="adamw_small")
    shapes_loc = [w_in[n].shape for n in small_names]
    for dst, packed in zip((outs_g, outs_d, outs_m, outs_v), res):
        for n, a in zip(small_names, _unpack(packed, shapes_loc)):
            dst[n] = a

    return (loss, grad_x, *[outs_g[n] for n in names], *[outs_d[n] for n in names],
            *[outs_m[n] for n in names], *[outs_v[n] for n in names])
```

```python
import functools
import math

import jax
import jax.numpy as jnp
from jax import lax
from jax.experimental import pallas as pl
from jax.experimental.pallas import tpu as pltpu

F32 = jnp.float32
BF16 = jnp.bfloat16
EPS = 1e-6
NEG_INF = -1e30
N_A = 2
DEPTH = 4
N_GROUPS = 3
DILATIONS = (1, 4, 16)
HEAD_DIM = 128
BLK = 128
LANES = 128
N_CHIPS = 4
N_DEV = 8
VMEM_LIMIT_V7X = 56 * 1024 * 1024

ADAM_LR = 0.001
ADAM_B1 = 0.9
ADAM_B2 = 0.999
ADAM_EPS = 1e-08
ADAM_WD = 0.01
ADAM_STEP = 10

MESH = pl.DeviceIdType.MESH


def _cp(*sem, **kw):
    return pltpu.CompilerParams(dimension_semantics=sem if sem else None, vmem_limit_bytes=VMEM_LIMIT_V7X, **kw)


def _dot(a, b):
    return jnp.dot(a, b, preferred_element_type=F32)


def _dot_nt(a, b):
    return lax.dot_general(a, b, (((1,), (1,)), ((), ())), preferred_element_type=F32)


def _dot_tn(a, b):
    return lax.dot_general(a, b, (((0,), (0,)), ((), ())), preferred_element_type=F32)


def _sigmoid(x):
    return 1.0 / (1.0 + jnp.exp(-x))


def _row_tile(n, want):
    if n <= want:
        return n
    for t in range(want - want % 8, 7, -8):
        if n % t == 0:
            return t
    raise ValueError(f"no row tile for {n} rows")


def mm_nn(a, w, nsh, stride, layer, bias=None, out_dtype=F32, name="mm_nn", tm=1024, head_major=False, spg=1):
    assert stride == 1 and layer == 0 and nsh % spg == 0
    m, k = a.shape
    _, k2, ns = w.shape
    assert k == k2
    tm = _row_tile(m, tm)
    has_bias = bias is not None
    hps = ns // HEAD_DIM

    def body(*refs):
        if has_bias:
            a_ref, w_ref, b_ref, o_ref = refs
        else:
            a_ref, w_ref, o_ref = refs
        av = a_ref[...].astype(BF16)
        for jj in range(spg):
            acc = _dot(av, w_ref[jj])
            if has_bias:
                acc = acc + b_ref[:, jj * ns:(jj + 1) * ns]
            if head_major:
                for hh in range(hps):
                    o_ref[jj * hps + hh] = acc[:, hh * HEAD_DIM:(hh + 1) * HEAD_DIM].astype(out_dtype)
            else:
                o_ref[:, jj * ns:(jj + 1) * ns] = acc.astype(out_dtype)

    in_specs = [
        pl.BlockSpec((tm, k), lambda j, i: (i, 0)),
        pl.BlockSpec((spg, k, ns), lambda j, i: (j, 0, 0)),
    ]
    args = [a, w]
    if has_bias:
        in_specs.append(pl.BlockSpec((1, spg * ns), lambda j, i: (0, j)))
        args.append(bias)
    return pl.pallas_call(
        body,
        name=name,
        grid=(nsh // spg, m // tm),
        in_specs=in_specs,
        out_specs=(pl.BlockSpec((spg * hps, tm, HEAD_DIM), lambda j, i: (j, i, 0)) if head_major
                   else pl.BlockSpec((tm, spg * ns), lambda j, i: (i, j))),
        out_shape=jax.ShapeDtypeStruct((nsh * hps, m, HEAD_DIM) if head_major else (m, nsh * ns), out_dtype),
        compiler_params=_cp("parallel", "parallel"),
    )(*args)


def _split_parts(dy, nsh, spg):
    dys = list(dy) if isinstance(dy, (list, tuple)) else [dy]
    per = nsh // len(dys)
    assert per % spg == 0
    return dys, per, per // spg


def mm_nt(dy, w, nsh, stride, layer, out_dtype=F32, name="mm_nt", tm=1024, spg=1):
    assert stride == 1 and layer == 0
    dys, per, gpp = _split_parts(dy, nsh, spg)
    npart = len(dys)
    m = dys[0].shape[0]
    _, k, ns = w.shape
    assert all(d.shape == (m, per * ns) for d in dys)
    tm = _row_tile(m, tm)
    ngrp = nsh // spg

    def body(*refs):
        dy_refs = refs[:npart]
        w_ref, o_ref, acc_ref = refs[npart:]
        j = pl.program_id(1)

        @pl.when(j == 0)
        def _():
            acc_ref[...] = jnp.zeros(acc_ref.shape, F32)

        for pi in range(npart):
            @pl.when(j // gpp == pi)
            def _(pi=pi):
                part = _dot_nt(dy_refs[pi][:, 0:ns].astype(BF16), w_ref[0])
                for jj in range(1, spg):
                    part = part + _dot_nt(dy_refs[pi][:, jj * ns:(jj + 1) * ns].astype(BF16), w_ref[jj])
                acc_ref[...] += part

        @pl.when(j == ngrp - 1)
        def _():
            o_ref[...] = acc_ref[...].astype(out_dtype)

    dy_specs = [pl.BlockSpec((tm, spg * ns), lambda i, j, pi=pi: (i, jnp.clip(j - pi * gpp, 0, gpp - 1)))
                for pi in range(npart)]
    return pl.pallas_call(
        body,
        name=name,
        grid=(m // tm, ngrp),
        in_specs=[*dy_specs, pl.BlockSpec((spg, k, ns), lambda i, j: (j, 0, 0))],
        out_specs=pl.BlockSpec((tm, k), lambda i, j: (i, 0)),
        out_shape=jax.ShapeDtypeStruct((m, k), out_dtype),
        scratch_shapes=[pltpu.VMEM((tm, k), F32)],
        compiler_params=_cp("parallel", "arbitrary"),
    )(*dys, w)


def mm_tn(a, dy, nsh, name="mm_tn", tm=1024, spg=1):
    dys, per, gpp = _split_parts(dy, nsh, spg)
    npart = len(dys)
    m, k = a.shape
    ns = dys[0].shape[1] // per
    assert all(d.shape == (m, per * ns) for d in dys)
    tm = _row_tile(m, tm)
    nt = m // tm

    def body(*refs):
        a_ref = refs[0]
        dy_refs = refs[1:1 + npart]
        o_ref, acc_ref = refs[1 + npart:]
        j = pl.program_id(0)
        i = pl.program_id(1)

        @pl.when(i == 0)
        def _():
            acc_ref[...] = jnp.zeros(acc_ref.shape, F32)

        for pi in range(npart):
            @pl.when(j // gpp == pi)
            def _(pi=pi):
                at = a_ref[...].astype(BF16).T
                for jj in range(spg):
                    acc_ref[jj] += _dot(at, dy_refs[pi][:, jj * ns:(jj + 1) * ns].astype(BF16))

        @pl.when(i == nt - 1)
        def _():
            o_ref[...] = acc_ref[...].astype(BF16)

    dy_specs = [
        pl.BlockSpec((tm, spg * ns),
                     lambda j, i, pi=pi: (jnp.where(j // gpp == pi, i, 0), jnp.clip(j - pi * gpp, 0, gpp - 1)))
        for pi in range(npart)
    ]
    return pl.pallas_call(
        body,
        name=name,
        grid=(nsh // spg, nt),
        in_specs=[pl.BlockSpec((tm, k), lambda j, i: (i, 0)), *dy_specs],
        out_specs=pl.BlockSpec((spg, k, ns), lambda j, i: (j, 0, 0)),
        out_shape=jax.ShapeDtypeStruct((nsh, k, ns), BF16),
        scratch_shapes=[pltpu.VMEM((spg, k, ns), F32)],
        compiler_params=_cp("parallel", "arbitrary"),
    )(a, *dys)


DEP_SPEC_SHAPE = (8, LANES)


def resid_norm_fwd(x, y, g_post, next_gains, name, tm=1024, dep=None):
    t, d = x.shape
    tm = _row_tile(t, tm)
    has_y = y is not None
    n_next = len(next_gains)
    n_dep = 0 if dep is None else 1

    def body(*refs):
        x_ref = refs[0]
        pos = 1
        if has_y:
            y_ref, gp_ref = refs[1], refs[2]
            pos = 3
        gn_refs = refs[pos:pos + n_next]
        outs = refs[pos + n_next + n_dep:]
        xv = x_ref[...]
        o = 0
        if has_y:
            yv = y_ref[...].astype(F32)
            r = lax.rsqrt(jnp.mean(yv * yv, axis=-1, keepdims=True) + EPS)
            xv = xv + (yv * r) * gp_ref[...]
            outs[0][...] = xv
            o = 1
        if n_next:
            xn = xv * lax.rsqrt(jnp.mean(xv * xv, axis=-1, keepdims=True) + EPS)
            for k in range(n_next):
                outs[o + k][...] = (xn * gn_refs[k][...]).astype(BF16)

    row = pl.BlockSpec((tm, d), lambda i: (i, 0))
    vec = pl.BlockSpec((1, d), lambda i: (0, 0))
    args, in_specs = [x], [row]
    if has_y:
        args += [y, g_post]
        in_specs += [row, vec]
    args += list(next_gains)
    in_specs += [vec] * n_next
    if n_dep:
        args.append(dep)
        in_specs.append(pl.BlockSpec(DEP_SPEC_SHAPE, lambda i: (0, 0)))
    out_shape, out_specs = [], []
    if has_y:
        out_shape.append(jax.ShapeDtypeStruct((t, d), F32))
        out_specs.append(row)
    for _ in range(n_next):
        out_shape.append(jax.ShapeDtypeStruct((t, d), BF16))
        out_specs.append(row)
    return pl.pallas_call(
        body, name=name, grid=(t // tm,), in_specs=in_specs, out_specs=out_specs, out_shape=out_shape,
        compiler_params=_cp("parallel"),
    )(*args)


def norm_bwd(x, g, dy, add=None, out_dtype=F32, name="norm_bwd", tm=1024, dep=None):
    t, d = x.shape
    tm = _row_tile(t, tm)
    has_add = add is not None

    def body(*refs):
        x_ref, g_ref, dy_ref = refs[:3]
        add_ref = refs[3] if has_add else None
        dx_ref, dg_ref, cs_ref = refs[-3:]
        i = pl.program_id(0)
        xv = x_ref[...].astype(F32)
        dyv = dy_ref[...].astype(F32)
        r = lax.rsqrt(jnp.mean(xv * xv, axis=-1, keepdims=True) + EPS)
        gd = dyv * g_ref[...]
        dx = r * gd - xv * ((r * r * r) * jnp.mean(xv * gd, axis=-1, keepdims=True))
        if has_add:
            dx = dx + add_ref[...]
        dx_ref[...] = dx.astype(out_dtype)
        dg = jnp.sum(dyv * (xv * r), axis=0, keepdims=True)
        cs = jnp.sum(dx, axis=0, keepdims=True)

        @pl.when(i == 0)
        def _():
            dg_ref[...] = dg
            cs_ref[...] = cs

        @pl.when(i > 0)
        def _():
            dg_ref[...] += dg
            cs_ref[...] += cs

    row = pl.BlockSpec((tm, d), lambda i: (i, 0))
    vec = pl.BlockSpec((1, d), lambda i: (0, 0))
    args, in_specs = [x, g, dy], [row, vec, row]
    if has_add:
        args.append(add)
        in_specs.append(row)
    if dep is not None:
        args.append(dep)
        in_specs.append(pl.BlockSpec(DEP_SPEC_SHAPE, lambda i: (0, 0)))
    return pl.pallas_call(
        body, name=name, grid=(t // tm,), in_specs=in_specs,
        out_specs=[row, vec, vec],
        out_shape=[jax.ShapeDtypeStruct((t, d), out_dtype), jax.ShapeDtypeStruct((1, d), F32),
                   jax.ShapeDtypeStruct((1, d), F32)],
        compiler_params=_cp("arbitrary"),
    )(*args)


def loss_fwd_bwd(x, target, name="loss", tm=1024):
    t, d = x.shape
    tm = _row_tile(t, tm)

    def body(x_ref, t_ref, dx_ref, l_ref):
        i = pl.program_id(0)
        err = x_ref[...] - t_ref[...]
        dx_ref[...] = err * (1.0 / d)
        part = 0.5 * jnp.sum(jnp.mean(err * err, axis=-1, keepdims=True), axis=0, keepdims=True)
        part = jnp.broadcast_to(part, l_ref.shape)

        @pl.when(i == 0)
        def _():
            l_ref[...] = part

        @pl.when(i > 0)
        def _():
            l_ref[...] += part

    row = pl.BlockSpec((tm, d), lambda i: (i, 0))
    return pl.pallas_call(
        body, name=name, grid=(t // tm,), in_specs=[row, row],
        out_specs=[row, pl.BlockSpec((8, LANES), lambda i: (0, 0))],
        out_shape=[jax.ShapeDtypeStruct((t, d), F32), jax.ShapeDtypeStruct((8, LANES), F32)],
        compiler_params=_cp("arbitrary"),
    )(x, target)


CONV_HALO = 32
CONV_CHUNK = 128


def glu_conv_fwd(z, dw, dwb, name, tc=128):
    b, s, c2 = z.shape
    c = c2 // 2
    kw = dw.shape[0]
    tc = min(tc, c)
    nc = c // tc
    ch = min(CONV_CHUNK, s)
    halo = CONV_HALO
    assert kw - 1 <= halo and s % ch == 0

    nch = s // ch

    def body(a_ref, g_ref, w_ref, b_ref, o_ref, pad_ref):
        _fill_glu_slabs(a_ref, g_ref, pad_ref, nch, ch, halo)

        def chunk(ci, carry):
            r0 = pl.multiple_of(ci * ch, ch)
            acc = b_ref[...]
            for k, tap in enumerate(_taps_front(pad_ref, ci, kw, ch, halo)):
                acc = acc + w_ref[k:k + 1, :] * tap
            o_ref[pl.ds(r0, ch), :] = acc
            return carry

        lax.fori_loop(0, nch, chunk, 0)

    return pl.pallas_call(
        body, name=name, grid=(b, nc),
        in_specs=[
            pl.BlockSpec((None, s, tc), lambda bi, i: (bi, 0, i)),
            pl.BlockSpec((None, s, tc), lambda bi, i: (bi, 0, i + nc)),
            pl.BlockSpec((kw, tc), lambda bi, i: (0, i)),
            pl.BlockSpec((1, tc), lambda bi, i: (0, i)),
        ],
        out_specs=pl.BlockSpec((None, s, tc), lambda bi, i: (bi, 0, i)),
        out_shape=jax.ShapeDtypeStruct((b, s, c), F32),
        scratch_shapes=[pltpu.VMEM((nch, ch + halo, tc), F32)],
        compiler_params=_cp("parallel", "parallel"),
    )(z, z, dw, dwb)


def glu_conv_bwd(z, dw, du2, name, tc=128):
    b, s, c2 = z.shape
    c = c2 // 2
    kw = dw.shape[0]
    tc = min(tc, c)
    nc = c // tc
    ch = min(CONV_CHUNK, s)
    nch = s // ch
    halo = CONV_HALO

    def body(a_ref, g_ref, w_ref, du_ref, dza_ref, dzg_ref, ddw_ref, ddwb_ref, dba_ref, dbg_ref, upad_ref, dpad_ref):
        bi = pl.program_id(1)

        @pl.when(bi == 0)
        def _():
            ddw_ref[...] = jnp.zeros(ddw_ref.shape, F32)
            ddwb_ref[...] = jnp.zeros(ddwb_ref.shape, F32)
            dba_ref[...] = jnp.zeros(dba_ref.shape, F32)
            dbg_ref[...] = jnp.zeros(dbg_ref.shape, F32)

        _fill_glu_slabs(a_ref, g_ref, upad_ref, nch, ch, halo)
        dpad_ref[nch - 1, ch:ch + halo, :] = jnp.zeros((halo, tc), F32)
        dpad_ref[nch - 1, 0:ch, :] = du_ref[s - ch:s, :]

        def fill(ci, carry):
            r0 = pl.multiple_of(ci * ch, ch)
            dpad_ref[ci, :, :] = du_ref[pl.ds(r0, ch + halo), :]
            return carry

        lax.fori_loop(0, nch - 1, fill, 0)

        def chunk(ci, carry):
            r0 = pl.multiple_of(ci * ch, ch)
            du_c = du_ref[pl.ds(r0, ch), :]
            taps_u = _taps_front(upad_ref, ci, kw, ch, halo)
            taps_d = _taps_at(dpad_ref, ci, list(range(kw)), ch)
            du1 = w_ref[kw - 1:kw, :] * du_c
            ddw_ref[kw - 1] += jnp.sum((du_c * taps_u[kw - 1]).reshape(ch // 8, 8, tc), axis=0)
            for j in range(1, kw):
                du1 = du1 + w_ref[kw - 1 - j:kw - j, :] * taps_d[j]
                ddw_ref[kw - 1 - j] += jnp.sum((du_c * taps_u[kw - 1 - j]).reshape(ch // 8, 8, tc), axis=0)
            av = a_ref[pl.ds(r0, ch), :]
            sg = _sigmoid(g_ref[pl.ds(r0, ch), :])
            dza = du1 * sg
            dzg = du1 * av * (sg * (1.0 - sg))
            dza_ref[pl.ds(r0, ch), :] = dza.astype(BF16)
            dzg_ref[pl.ds(r0, ch), :] = dzg.astype(BF16)
            dba_ref[...] += jnp.sum(dza, axis=0, keepdims=True)
            dbg_ref[...] += jnp.sum(dzg, axis=0, keepdims=True)
            ddwb_ref[...] += jnp.sum(du_c, axis=0, keepdims=True)
            return carry

        lax.fori_loop(0, s // ch, chunk, 0)

    blk = lambda off: pl.BlockSpec((None, s, tc), lambda i, bi: (bi, 0, i + off))
    vec = pl.BlockSpec((1, tc), lambda i, bi: (0, i))
    return pl.pallas_call(
        body, name=name, grid=(nc, b),
        in_specs=[blk(0), blk(nc), pl.BlockSpec((kw, tc), lambda i, bi: (0, i)), blk(0)],
        out_specs=[blk(0), blk(0), pl.BlockSpec((kw, 8, tc), lambda i, bi: (0, 0, i)), vec, vec, vec],
        out_shape=[
            jax.ShapeDtypeStruct((b, s, c), BF16), jax.ShapeDtypeStruct((b, s, c), BF16),
            jax.ShapeDtypeStruct((kw, 8, c), F32), jax.ShapeDtypeStruct((1, c), F32),
            jax.ShapeDtypeStruct((1, c), F32), jax.ShapeDtypeStruct((1, c), F32),
        ],
        scratch_shapes=[pltpu.VMEM((nch, ch + halo, tc), F32), pltpu.VMEM((nch, ch + halo, tc), F32)],
        compiler_params=_cp("parallel", "arbitrary"),
    )(z, z, dw, du2)


def _fill_glu_slabs(a_ref, g_ref, pad_ref, nch, ch, halo):
    tc = a_ref.shape[-1]
    pad_ref[0, 0:halo, :] = jnp.zeros((halo, tc), F32)
    pad_ref[0, halo:halo + ch, :] = a_ref[0:ch, :] * _sigmoid(g_ref[0:ch, :])

    def fill(ci, carry):
        r0 = pl.multiple_of(ci * ch, ch)
        pad_ref[ci, 0:halo, :] = pad_ref[ci - 1, ch:ch + halo, :]
        pad_ref[ci, halo:halo + ch, :] = a_ref[pl.ds(r0, ch), :] * _sigmoid(g_ref[pl.ds(r0, ch), :])
        return carry

    lax.fori_loop(1, nch, fill, 0)


def ln_silu_fwd(u, g, bvec, name, tm=1024):
    t, d = u.shape
    tm = _row_tile(t, tm)

    def body(u_ref, g_ref, b_ref, o_ref):
        uv = u_ref[...]
        mu = jnp.mean(uv, axis=-1, keepdims=True)
        xc = uv - mu
        var = jnp.mean(xc * xc, axis=-1, keepdims=True)
        v = (xc * lax.rsqrt(var + EPS)) * g_ref[...] + b_ref[...]
        o_ref[...] = (v * _sigmoid(v)).astype(BF16)

    row = pl.BlockSpec((tm, d), lambda i: (i, 0))
    vec = pl.BlockSpec((1, d), lambda i: (0, 0))
    return pl.pallas_call(
        body, name=name, grid=(t // tm,), in_specs=[row, vec, vec], out_specs=row,
        out_shape=jax.ShapeDtypeStruct((t, d), BF16), compiler_params=_cp("parallel"),
    )(u, g, bvec)


def ln_silu_bwd(u, g, bvec, dout, name, tm=1024):
    t, d = u.shape
    tm = _row_tile(t, tm)

    def body(u_ref, g_ref, b_ref, do_ref, du_ref, dg_ref, db_ref):
        i = pl.program_id(0)
        uv = u_ref[...]
        mu = jnp.mean(uv, axis=-1, keepdims=True)
        xc = uv - mu
        var = jnp.mean(xc * xc, axis=-1, keepdims=True)
        rstd = lax.rsqrt(var + EPS)
        n = xc * rstd
        v = n * g_ref[...] + b_ref[...]
        sg = _sigmoid(v)
        dv = do_ref[...].astype(F32) * (sg * (1.0 + v * (1.0 - sg)))
        dn = dv * g_ref[...]
        du_ref[...] = rstd * (dn - jnp.mean(dn, axis=-1, keepdims=True) - n * jnp.mean(dn * n, axis=-1, keepdims=True))
        dg = jnp.sum(dv * n, axis=0, keepdims=True)
        db = jnp.sum(dv, axis=0, keepdims=True)

        @pl.when(i == 0)
        def _():
            dg_ref[...] = dg
            db_ref[...] = db

        @pl.when(i > 0)
        def _():
            dg_ref[...] += dg
            db_ref[...] += db

    row = pl.BlockSpec((tm, d), lambda i: (i, 0))
    vec = pl.BlockSpec((1, d), lambda i: (0, 0))
    return pl.pallas_call(
        body, name=name, grid=(t // tm,), in_specs=[row, vec, vec, row], out_specs=[row, vec, vec],
        out_shape=[jax.ShapeDtypeStruct((t, d), F32), jax.ShapeDtypeStruct((1, d), F32), jax.ShapeDtypeStruct((1, d), F32)],
        compiler_params=_cp("arbitrary"),
    )(u, g, bvec, dout)


FFN_HALO = 8


def _fill_front_halo(src_ref, pad_ref, nch, ch, halo):
    tc = src_ref.shape[-1]
    pad_ref[0, 0:halo, :] = jnp.zeros((halo, tc), F32)
    pad_ref[0, halo:halo + ch, :] = src_ref[0:ch, :].astype(F32)

    def fill(ci, carry):
        r0 = pl.multiple_of(ci * ch, ch)
        pad_ref[ci, 0:halo, :] = src_ref[pl.ds(r0 - 2 * halo, 2 * halo), :].astype(F32)[halo:, :]
        pad_ref[ci, halo:halo + ch, :] = src_ref[pl.ds(r0, ch), :].astype(F32)
        return carry

    lax.fori_loop(1, nch, fill, 0)


def _taps_at(pad_ref, ci, offsets, ch):
    windows = {}
    for b in sorted({o % 8 for o in offsets}):
        top = max(o for o in offsets if o % 8 == b)
        windows[b] = pad_ref[ci, b:top + ch, :]
    return [windows[o % 8][o - o % 8:o - o % 8 + ch, :] for o in offsets]


def _taps_front(pad_ref, ci, kw, ch, halo):
    return _taps_at(pad_ref, ci, [halo - (kw - 1 - k) for k in range(kw)], ch)


def ffn_mid_fwd(p, dw, dwb, name, tc=256):
    b, s, f2 = p.shape
    f = f2 // 2
    kw = dw.shape[0]
    tc = min(tc, f)
    nf = f // tc
    ch = min(CONV_CHUNK, s)
    nch = s // ch
    halo = FFN_HALO

    def body(pa_ref, pg_ref, wa_ref, wg_ref, ba_ref, bg_ref, o_ref, apad_ref, gpad_ref):
        _fill_front_halo(pa_ref, apad_ref, nch, ch, halo)
        _fill_front_halo(pg_ref, gpad_ref, nch, ch, halo)

        def chunk(ci, carry):
            r0 = pl.multiple_of(ci * ch, ch)
            ca = ba_ref[...]
            cg = bg_ref[...]
            taps = zip(_taps_front(apad_ref, ci, kw, ch, halo), _taps_front(gpad_ref, ci, kw, ch, halo))
            for k, (ta, tg) in enumerate(taps):
                ca = ca + wa_ref[k:k + 1, :] * ta
                cg = cg + wg_ref[k:k + 1, :] * tg
            o_ref[pl.ds(r0, ch), :] = ((cg * _sigmoid(cg)) * ca).astype(BF16)
            return carry

        lax.fori_loop(0, nch, chunk, 0)

    blk = lambda off: pl.BlockSpec((None, s, tc), lambda bi, i: (bi, 0, i + off))
    wsp = lambda off: pl.BlockSpec((kw, tc), lambda bi, i: (0, i + off))
    bsp = lambda off: pl.BlockSpec((1, tc), lambda bi, i: (0, i + off))
    return pl.pallas_call(
        body, name=name, grid=(b, nf),
        in_specs=[blk(0), blk(nf), wsp(0), wsp(nf), bsp(0), bsp(nf)],
        out_specs=pl.BlockSpec((None, s, tc), lambda bi, i: (bi, 0, i)),
        out_shape=jax.ShapeDtypeStruct((b, s, f), BF16),
        scratch_shapes=[pltpu.VMEM((nch, ch + halo, tc), F32)] * 2,
        compiler_params=_cp("parallel", "parallel"),
    )(p, p, dw, dw, dwb, dwb)


def ffn_mid_bwd(p, dw, dwb, ds, name, tc=256):
    b, s, f2 = p.shape
    f = f2 // 2
    kw = dw.shape[0]
    tc = min(tc, f)
    nf = f // tc
    ch = min(CONV_CHUNK, s)
    nch = s // ch
    halo = FFN_HALO

    def sum8(v):
        return jnp.sum(v.reshape(ch // 8, 8, tc), axis=0)

    def body(pa_ref, pg_ref, wa_ref, wg_ref, ba_ref, bg_ref, ds_ref, dpa_ref, dpg_ref, ddwa_ref, ddwg_ref, dba_ref, dbg_ref,
             apad_ref, gpad_ref, dca_ref, dcg_ref):
        bi = pl.program_id(1)

        @pl.when(bi == 0)
        def _():
            ddwa_ref[...] = jnp.zeros(ddwa_ref.shape, F32)
            ddwg_ref[...] = jnp.zeros(ddwg_ref.shape, F32)
            dba_ref[...] = jnp.zeros(dba_ref.shape, F32)
            dbg_ref[...] = jnp.zeros(dbg_ref.shape, F32)

        _fill_front_halo(pa_ref, apad_ref, nch, ch, halo)
        _fill_front_halo(pg_ref, gpad_ref, nch, ch, halo)
        dca_ref[nch - 1, ch:ch + halo, :] = jnp.zeros((halo, tc), F32)
        dcg_ref[nch - 1, ch:ch + halo, :] = jnp.zeros((halo, tc), F32)

        def grads(ci, carry):
            acc_a, acc_g, sb_a, sb_g = carry
            r0 = pl.multiple_of(ci * ch, ch)
            taps_a = _taps_front(apad_ref, ci, kw, ch, halo)
            taps_g = _taps_front(gpad_ref, ci, kw, ch, halo)
            ca = ba_ref[...]
            cg = bg_ref[...]
            for k in range(kw):
                ca = ca + wa_ref[k:k + 1, :] * taps_a[k]
                cg = cg + wg_ref[k:k + 1, :] * taps_g[k]
            sg = _sigmoid(cg)
            dsv = ds_ref[pl.ds(r0, ch), :].astype(F32)
            dca = dsv * (cg * sg)
            dcg = dsv * ca * (sg * (1.0 + cg * (1.0 - sg)))
            dca_ref[ci, 0:ch, :] = dca
            dcg_ref[ci, 0:ch, :] = dcg

            prev = jnp.maximum(ci - 1, 0)

            @pl.when(ci > 0)
            def _():
                dca_ref[prev, ch:ch + halo, :] = dca[0:halo, :]
                dcg_ref[prev, ch:ch + halo, :] = dcg[0:halo, :]

            acc_a = tuple(acc_a[k] + sum8(dca * taps_a[k]) for k in range(kw))
            acc_g = tuple(acc_g[k] + sum8(dcg * taps_g[k]) for k in range(kw))
            return acc_a, acc_g, sb_a + sum8(dca), sb_g + sum8(dcg)

        z8 = jnp.zeros((8, tc), F32)
        acc_a, acc_g, sb_a, sb_g = lax.fori_loop(0, nch, grads, ((z8,) * kw, (z8,) * kw, z8, z8))
        for k in range(kw):
            ddwa_ref[k] += acc_a[k]
            ddwg_ref[k] += acc_g[k]
        dba_ref[...] += jnp.sum(sb_a, axis=0, keepdims=True)
        dbg_ref[...] += jnp.sum(sb_g, axis=0, keepdims=True)

        def back(ci, carry):
            r0 = pl.multiple_of(ci * ch, ch)
            da = wa_ref[kw - 1:kw, :] * dca_ref[ci, 0:ch, :]
            dg = wg_ref[kw - 1:kw, :] * dcg_ref[ci, 0:ch, :]
            for j in range(1, kw):
                da = da + wa_ref[kw - 1 - j:kw - j, :] * dca_ref[ci, j:j + ch, :]
                dg = dg + wg_ref[kw - 1 - j:kw - j, :] * dcg_ref[ci, j:j + ch, :]
            dpa_ref[pl.ds(r0, ch), :] = da.astype(BF16)
            dpg_ref[pl.ds(r0, ch), :] = dg.astype(BF16)
            return carry

        lax.fori_loop(0, nch, back, 0)

    blk = lambda off: pl.BlockSpec((None, s, tc), lambda i, bi: (bi, 0, i + off))
    wsp = lambda off: pl.BlockSpec((kw, tc), lambda i, bi: (0, i + off))
    bsp = lambda off: pl.BlockSpec((1, tc), lambda i, bi: (0, i + off))
    acc3 = pl.BlockSpec((kw, 8, tc), lambda i, bi: (0, 0, i))
    vec = pl.BlockSpec((1, tc), lambda i, bi: (0, i))
    return pl.pallas_call(
        body, name=name, grid=(nf, b),
        in_specs=[blk(0), blk(nf), wsp(0), wsp(nf), bsp(0), bsp(nf), blk(0)],
        out_specs=[blk(0), blk(0), acc3, acc3, vec, vec],
        out_shape=[jax.ShapeDtypeStruct((b, s, f), BF16), jax.ShapeDtypeStruct((b, s, f), BF16),
                   jax.ShapeDtypeStruct((kw, 8, f), F32), jax.ShapeDtypeStruct((kw, 8, f), F32),
                   jax.ShapeDtypeStruct((1, f), F32), jax.ShapeDtypeStruct((1, f), F32)],
        scratch_shapes=[pltpu.VMEM((nch, ch + halo, tc), F32)] * 4,
        compiler_params=_cp("parallel", "arbitrary"),
    )(p, p, dw, dw, dwb, dwb, ds)


def _tile_rows(r, n, dil):
    start = r + n * BLK * dil
    return pl.ds(start, BLK, stride=dil) if dil > 1 else pl.ds(start, BLK)


def _band_masks():
    qi = lax.broadcasted_iota(jnp.int32, (BLK, 2 * BLK), 0)
    kk = lax.broadcasted_iota(jnp.int32, (BLK, 2 * BLK), 1)
    both = jnp.logical_or(jnp.logical_and(kk < BLK, kk >= qi), jnp.logical_and(kk >= BLK, kk - BLK <= qi))
    return both, kk[:, :BLK] <= qi[:, :BLK]


def attn_fwd(q, kv, g, dil, hw, name):
    _, b, s, _ = q.shape
    nh = hw // HEAD_DIM
    nblk = s // dil // BLK
    scale = 1.0 / math.sqrt(HEAD_DIM)

    def body(q_ref, k_ref, v_ref, o_ref, lse_ref):
        h = pl.program_id(1)
        mask2, mask1 = _band_masks()
        mine = lax.broadcasted_iota(jnp.int32, (BLK, LANES), 1) == h

        @pl.when(h == 0)
        def _():
            lse_ref[...] = jnp.zeros(lse_ref.shape, F32)

        for r in range(dil):
            kp = vp = None
            for n in range(nblk):
                rs = _tile_rows(r, n, dil)
                qt = q_ref[rs, :].astype(BF16)
                kc = k_ref[rs, :].astype(BF16)
                vc = v_ref[rs, :].astype(BF16)
                if n == 0:
                    kcat, vcat, mask = kc, vc, mask1
                else:
                    kcat, vcat, mask = jnp.concatenate([kp, kc], axis=0), jnp.concatenate([vp, vc], axis=0), mask2
                sc = jnp.where(mask, _dot_nt(qt, kcat) * scale, NEG_INF)
                m = jnp.max(sc, axis=-1, keepdims=True)
                p = jnp.exp(sc - m)
                den = jnp.sum(p, axis=-1, keepdims=True)
                o_ref[rs, :] = _dot(p.astype(BF16), vcat) / den
                lse_ref[rs, :] = jnp.where(mine, m + jnp.log(den), lse_ref[rs, :])
                kp, vp = kc, vc

    col = lambda base: pl.BlockSpec((None, s, HEAD_DIM), lambda bi, h: (bi, 0, base + h))
    head = lambda base: pl.BlockSpec((None, None, s, HEAD_DIM), lambda bi, h: (base + h, bi, 0, 0))
    return pl.pallas_call(
        body, name=name, grid=(b, nh),
        in_specs=[head(g * nh), head(g * nh), head((N_GROUPS + g) * nh)],
        out_specs=[col(0), pl.BlockSpec((None, s, LANES), lambda bi, h: (bi, 0, 0))],
        out_shape=[jax.ShapeDtypeStruct((b, s, hw), F32), jax.ShapeDtypeStruct((b, s, LANES), F32)],
        compiler_params=_cp("parallel", "arbitrary"),
    )(q, kv, kv)


def attn_merge(outs, lses, name, tm=512):
    t, hw = outs[0].shape
    nh = hw // HEAD_DIM
    tm = _row_tile(t, tm)
    ng = len(outs)

    def body(*refs):
        o_refs, l_refs = refs[:ng], refs[ng:2 * ng]
        m_ref, lj_ref = refs[2 * ng:]
        ls = [l_refs[g][...] for g in range(ng)]
        mx = ls[0]
        for g in range(1, ng):
            mx = jnp.maximum(mx, ls[g])
        es = [jnp.exp(l - mx) for l in ls]
        tot = es[0]
        for g in range(1, ng):
            tot = tot + es[g]
        ws = [e / tot for e in es]
        lj_ref[...] = mx + jnp.log(tot)
        for h in range(nh):
            sl = slice(h * HEAD_DIM, (h + 1) * HEAD_DIM)
            acc = ws[0][:, h:h + 1] * o_refs[0][:, sl]
            for g in range(1, ng):
                acc = acc + ws[g][:, h:h + 1] * o_refs[g][:, sl]
            m_ref[:, sl] = acc.astype(BF16)

    row = pl.BlockSpec((tm, hw), lambda i: (i, 0))
    st = pl.BlockSpec((tm, LANES), lambda i: (i, 0))
    return pl.pallas_call(
        body, name=name, grid=(t // tm,), in_specs=[row] * ng + [st] * ng, out_specs=[row, st],
        out_shape=[jax.ShapeDtypeStruct((t, hw), BF16), jax.ShapeDtypeStruct((t, LANES), F32)],
        compiler_params=_cp("parallel"),
    )(*outs, *lses)


def attn_bwd_prep(dmerged, merged, name, tm=512):
    t, hw = merged.shape
    nh = hw // HEAD_DIM
    tm = _row_tile(t, tm)

    def body(d_ref, m_ref, o_ref):
        lane = lax.broadcasted_iota(jnp.int32, (tm, LANES), 1)
        acc = jnp.zeros((tm, LANES), F32)
        for h in range(nh):
            sl = slice(h * HEAD_DIM, (h + 1) * HEAD_DIM)
            dsum = jnp.sum(d_ref[:, sl] * m_ref[:, sl].astype(F32), axis=-1, keepdims=True)
            acc = jnp.where(lane == h, dsum, acc)
        o_ref[...] = acc

    row = pl.BlockSpec((tm, hw), lambda i: (i, 0))
    return pl.pallas_call(
        body, name=name, grid=(t // tm,), in_specs=[row, row], out_specs=pl.BlockSpec((tm, LANES), lambda i: (i, 0)),
        out_shape=jax.ShapeDtypeStruct((t, LANES), F32), compiler_params=_cp("parallel"),
    )(dmerged, merged)


def attn_bwd(q, kv, g, dil, do, lsej, dm, dq_buf, dk_buf, dv_buf, accumulate, hw, name):
    _, b, s, _ = q.shape
    nh = hw // HEAD_DIM
    nblk = s // dil // BLK
    scale = 1.0 / math.sqrt(HEAD_DIM)
    assert dk_buf is not None or not accumulate
    kv_at = 6 + (dq_buf is not None)

    def body(*refs):
        q_ref, k_ref, v_ref, do_ref, lj_ref, dm_ref = refs[:6]
        dq_ref, dk_ref, dv_ref = refs[-3:]
        dki_ref, dvi_ref = (refs[kv_at], refs[kv_at + 1]) if accumulate else (None, None)
        mask2, mask1 = _band_masks()
        mine = lax.broadcasted_iota(jnp.int32, (BLK, LANES), 1) == pl.program_id(1)

        def my_lane(v):
            return jnp.sum(jnp.where(mine, v, 0.0), axis=-1, keepdims=True)

        def put(rs, dk, dv):
            if accumulate:
                dk = dk + dki_ref[rs, :]
                dv = dv + dvi_ref[rs, :]
            dk_ref[rs, :] = dk
            dv_ref[rs, :] = dv

        for r in range(dil):
            kp = vp = hold_k = hold_v = rs_prev = None
            for n in range(nblk):
                rs = _tile_rows(r, n, dil)
                qt = q_ref[rs, :].astype(BF16)
                kc = k_ref[rs, :].astype(BF16)
                vc = v_ref[rs, :].astype(BF16)
                dot = do_ref[rs, :].astype(BF16)
                lm = my_lane(lj_ref[rs, :])
                dmm = my_lane(dm_ref[rs, :])
                if n == 0:
                    kcat, vcat, mask = kc, vc, mask1
                else:
                    kcat, vcat, mask = jnp.concatenate([kp, kc], axis=0), jnp.concatenate([vp, vc], axis=0), mask2
                p = jnp.exp(jnp.where(mask, _dot_nt(qt, kcat) * scale, NEG_INF) - lm)
                ds = (p * (_dot_nt(dot, vcat) - dmm)).astype(BF16)
                dq_ref[rs, :] = _dot(ds, kcat) * scale
                dkc = _dot_tn(ds, qt) * scale
                dvc = _dot_tn(p.astype(BF16), dot)
                if n > 0:
                    put(rs_prev, hold_k + dkc[:BLK, :], hold_v + dvc[:BLK, :])
                    dkc, dvc = dkc[BLK:, :], dvc[BLK:, :]
                hold_k, hold_v, kp, vp, rs_prev = dkc, dvc, kc, vc, rs
            put(rs_prev, hold_k, hold_v)

    col = lambda base: pl.BlockSpec((None, s, HEAD_DIM), lambda bi, h: (bi, 0, base + h))
    any_spec = pl.BlockSpec(memory_space=pl.ANY)
    stat = pl.BlockSpec((None, s, LANES), lambda bi, h: (bi, 0, 0))
    head = lambda base: pl.BlockSpec((None, None, s, HEAD_DIM), lambda bi, h: (base + h, bi, 0, 0))
    in_specs = [head(g * nh), head(g * nh), head((N_GROUPS + g) * nh), col(0), stat, stat]
    args = [q, kv, kv, do, lsej, dm]
    aliases = {}
    if dq_buf is not None:
        in_specs.append(any_spec)
        args.append(dq_buf)
        aliases[6] = 0
    if dk_buf is not None:
        in_specs += [col(g * nh) if accumulate else any_spec] * 2
        args += [dk_buf, dv_buf]
        aliases.update({kv_at: 1, kv_at + 1: 2})
    shape = jax.ShapeDtypeStruct((b, s, N_GROUPS * hw), F32)
    return pl.pallas_call(
        body, name=name, grid=(b, nh), in_specs=in_specs, out_specs=[col(g * nh)] * 3, out_shape=[shape] * 3,
        input_output_aliases=aliases, compiler_params=_cp("parallel", "parallel"),
    )(*args)


def sum_parts(g, recv, me, name, tm=512, layer=0, n_layers=1, out_buf=None):
    _, rows, c = g.shape
    n = recv.shape[0]
    tm = _row_tile(rows, tm)
    steps = rows // tm

    def body(me_ref, g_ref, r_ref, *rest):
        acc = g_ref[...].astype(F32)
        for j in range(n):
            acc = acc + r_ref[j].astype(F32)
        rest[-1][...] = acc

    in_specs = [pl.BlockSpec((None, tm, c), lambda i, me_ref: (me_ref[0], i, 0)),
                pl.BlockSpec((n, tm, c), lambda i, me_ref: (0, i, 0))]
    args = [me, g, recv]
    aliases = {}
    if out_buf is not None:
        in_specs.append(pl.BlockSpec(memory_space=pl.ANY))
        args.append(out_buf)
        aliases = {3: 0}
    return pl.pallas_call(
        body, name=name,
        grid_spec=pltpu.PrefetchScalarGridSpec(
            num_scalar_prefetch=1, grid=(steps,), in_specs=in_specs,
            out_specs=pl.BlockSpec((tm, c), lambda i, me_ref: (layer * steps + i, 0))),
        out_shape=jax.ShapeDtypeStruct((n_layers * rows, c), F32), input_output_aliases=aliases,
        compiler_params=_cp("parallel"),
    )(*args)


def adamw(w, m, v, g_parts, name, tm=256):
    rows, c = w.shape
    if c <= 8 * LANES:
        tm = 2 * tm
    tm = _row_tile(rows, tm)
    npart = len(g_parts)

    def body(*refs):
        w_ref, m_ref, v_ref = refs[:3]
        g_refs = refs[3:3 + npart]
        go_ref, d_ref, mo_ref, vo_ref = refs[3 + npart:]
        g = g_refs[0][...]
        for k in range(1, npart):
            g = g + g_refs[k][...]
        mn = ADAM_B1 * m_ref[...] + (1.0 - ADAM_B1) * g
        vn = ADAM_B2 * v_ref[...] + (1.0 - ADAM_B2) * (g * g)
        m_hat = mn / (1.0 - ADAM_B1 ** ADAM_STEP)
        v_hat = vn / (1.0 - ADAM_B2 ** ADAM_STEP)
        go_ref[...] = g
        d_ref[...] = -ADAM_LR * (m_hat / (jnp.sqrt(v_hat) + ADAM_EPS) + ADAM_WD * w_ref[...])
        mo_ref[...] = mn
        vo_ref[...] = vn

    row = pl.BlockSpec((tm, c), lambda i: (i, 0))
    return pl.pallas_call(
        body, name=name, grid=(rows // tm,), in_specs=[row] * (3 + npart), out_specs=[row] * 4,
        out_shape=[jax.ShapeDtypeStruct((rows, c), F32)] * 4, compiler_params=_cp("parallel"),
    )(w, m, v, *g_parts)


def _place():
    return lax.axis_index("x"), lax.axis_index("y"), lax.axis_index("c")


def _other_chips(x, y, c):
    return [(1 - x, y, c), (x, 1 - y, c), (1 - x, 1 - y, c)]


def _chip_of(px, py):
    return 2 * px + py


HBM_SPEC = pl.BlockSpec(memory_space=pltpu.HBM)
SEM_SPEC = pl.BlockSpec(memory_space=pltpu.SEMAPHORE)
ANY_SPEC = pl.BlockSpec(memory_space=pl.ANY)
DATAFLOW = pltpu.SideEffectType.DATAFLOW_SIDE_EFFECTING
N_PEER_CHIPS = N_CHIPS - 1


def _hbm(a):
    return pltpu.with_memory_space_constraint(a, pltpu.HBM)


def _hbm_like(arrays):
    return [pltpu.HBM(a.shape, a.dtype) for a in arrays]


def cast_place(w, layer, me, out_dtype, name, tm=512, nslots=N_CHIPS, dep=None):
    rows, c = w.shape[-2:]
    tm = _row_tile(rows, tm)

    def body(me_ref, w_ref, *rest):
        rest[-1][...] = w_ref[...].astype(out_dtype)

    if layer is None:
        in_specs = [pl.BlockSpec((tm, c), lambda i, me_ref: (i, 0))]
    else:
        in_specs = [pl.BlockSpec((None, tm, c), lambda i, me_ref: (layer, i, 0))]
    args = [me, w]
    if dep is not None:
        in_specs.append(pl.BlockSpec(DEP_SPEC_SHAPE, lambda i, me_ref: (0, 0)))
        args.append(dep)
    return pl.pallas_call(
        body, name=name,
        grid_spec=pltpu.PrefetchScalarGridSpec(
            num_scalar_prefetch=1, grid=(rows // tm,), in_specs=in_specs,
            out_specs=pl.BlockSpec((None, tm, c), lambda i, me_ref: (me_ref[0], i, 0))),
        out_shape=jax.ShapeDtypeStruct((nslots, rows, c), out_dtype), compiler_params=_cp("parallel"),
    )(*args)


def gather_start(lands, chunk_sizes, name="gather_start"):
    n = len(lands)
    nch = len(chunk_sizes)
    assert sum(chunk_sizes) == n

    def body(*refs):
        land_refs = refs[:n]
        outs = refs[n:]
        send_sems, recv_sems = outs[:nch], outs[nch:2 * nch]
        token = outs[-1]
        x, y, c = _place()
        me = _chip_of(x, y)
        peers = _other_chips(x, y, c)
        k = 0
        for ck, size in enumerate(chunk_sizes):
            for pos in range(size):
                for r, peer in enumerate(peers):
                    pltpu.make_async_remote_copy(
                        src_ref=land_refs[k].at[me], dst_ref=land_refs[k].at[me],
                        send_sem=send_sems[ck].at[N_PEER_CHIPS * pos + r], recv_sem=recv_sems[ck].at[N_PEER_CHIPS * pos + r],
                        device_id=peer, device_id_type=MESH).start()
                k += 1
        token[...] = jnp.zeros(token.shape, F32)

    sems = [pltpu.SemaphoreType.DMA((N_PEER_CHIPS * s,)) for s in chunk_sizes]
    res = pl.pallas_call(
        body, name=name,
        out_shape=(*sems, *sems, *_hbm_like(lands), jax.ShapeDtypeStruct(DEP_SPEC_SHAPE, F32)),
        in_specs=[HBM_SPEC] * n,
        out_specs=(*[SEM_SPEC] * (2 * nch), *[HBM_SPEC] * n, pl.BlockSpec(memory_space=pltpu.VMEM)),
        input_output_aliases={k: 2 * nch + k for k in range(n)},
        compiler_params=pltpu.CompilerParams(has_side_effects=DATAFLOW),
    )(*[_hbm(a) for a in lands])
    return res[:nch], res[nch:2 * nch], res[2 * nch:2 * nch + n], res[-1]


def gather_wait(send_sem, recv_sem, lands, after, name):
    n = len(lands)

    def body(*refs):
        land_refs = refs[:n]
        ssem, rsem = refs[n], refs[n + 1]
        x, y, c = _place()
        me = _chip_of(x, y)
        for pos in range(n):
            for r, peer in enumerate(_other_chips(x, y, c)):
                cp = pltpu.make_async_remote_copy(
                    src_ref=land_refs[pos].at[me], dst_ref=land_refs[pos].at[_chip_of(peer[0], peer[1])],
                    send_sem=ssem.at[N_PEER_CHIPS * pos + r], recv_sem=rsem.at[N_PEER_CHIPS * pos + r],
                    device_id=peer, device_id_type=MESH)
                cp.wait_send()
                cp.wait_recv()

    return pl.pallas_call(
        body, name=name, out_shape=tuple(_hbm_like(lands)),
        in_specs=[*[HBM_SPEC] * n, SEM_SPEC, SEM_SPEC, ANY_SPEC], out_specs=[HBM_SPEC] * n,
        input_output_aliases={k: k for k in range(n)},
        compiler_params=pltpu.CompilerParams(has_side_effects=DATAFLOW),
    )(*lands, send_sem, recv_sem, after)


def scatter_start(grads, name):
    n = len(grads)
    recvs = [lax.empty((N_PEER_CHIPS, *g.shape[1:]), g.dtype) for g in grads]

    def body(*refs):
        g_refs, r_refs = refs[:n], refs[n:2 * n]
        send_sems, recv_sems = refs[2 * n], refs[2 * n + 1]
        token = refs[-1]
        x, y, c = _place()
        for k in range(n):
            for r, peer in enumerate(_other_chips(x, y, c)):
                pltpu.make_async_remote_copy(
                    src_ref=g_refs[k].at[_chip_of(peer[0], peer[1])], dst_ref=r_refs[k].at[r],
                    send_sem=send_sems.at[N_PEER_CHIPS * k + r], recv_sem=recv_sems.at[N_PEER_CHIPS * k + r],
                    device_id=peer, device_id_type=MESH).start()
        token[...] = jnp.zeros(token.shape, F32)

    sem = pltpu.SemaphoreType.DMA((N_PEER_CHIPS * n,))
    res = pl.pallas_call(
        body, name=name,
        out_shape=(sem, sem, *_hbm_like(grads), *_hbm_like(recvs), jax.ShapeDtypeStruct(DEP_SPEC_SHAPE, F32)),
        in_specs=[HBM_SPEC] * (2 * n),
        out_specs=(SEM_SPEC, SEM_SPEC, *[HBM_SPEC] * (2 * n), pl.BlockSpec(memory_space=pltpu.VMEM)),
        input_output_aliases={k: 2 + k for k in range(2 * n)},
        compiler_params=pltpu.CompilerParams(has_side_effects=DATAFLOW),
    )(*[_hbm(a) for a in grads], *[_hbm(a) for a in recvs])
    return res[0], res[1], res[2:2 + n], res[2 + n:2 + 2 * n], res[-1]


def scatter_wait(send_sem, recv_sem, grads, recvs, after, name):
    n = len(grads)

    def body(*refs):
        g_refs, r_refs = refs[:n], refs[n:2 * n]
        ssem, rsem = refs[2 * n], refs[2 * n + 1]
        x, y, c = _place()
        for k in range(n):
            for r, peer in enumerate(_other_chips(x, y, c)):
                cp = pltpu.make_async_remote_copy(
                    src_ref=g_refs[k].at[_chip_of(peer[0], peer[1])], dst_ref=r_refs[k].at[r],
                    send_sem=ssem.at[N_PEER_CHIPS * k + r], recv_sem=rsem.at[N_PEER_CHIPS * k + r],
                    device_id=peer, device_id_type=MESH)
                cp.wait_send()
                cp.wait_recv()

    res = pl.pallas_call(
        body, name=name, out_shape=(*_hbm_like(grads), *_hbm_like(recvs)),
        in_specs=[*[HBM_SPEC] * (2 * n), SEM_SPEC, SEM_SPEC, ANY_SPEC], out_specs=[HBM_SPEC] * (2 * n),
        input_output_aliases={k: k for k in range(2 * n)},
        compiler_params=pltpu.CompilerParams(has_side_effects=DATAFLOW),
    )(*grads, *recvs, send_sem, recv_sem, after)
    return res[:n], res[n:]


def swap_start(parts, name):
    n = len(parts)
    lands = [lax.empty(p.shape, p.dtype) for p in parts]

    def body(*refs):
        p_refs, l_refs = refs[:n], refs[n:2 * n]
        sems = refs[2 * n:4 * n]
        token = refs[-1]
        x, y, c = _place()
        for k in range(n):
            pltpu.make_async_remote_copy(
                src_ref=p_refs[k], dst_ref=l_refs[k], send_sem=sems[k], recv_sem=sems[n + k],
                device_id=(x, y, 1 - c), device_id_type=MESH).start()
        token[...] = jnp.zeros(token.shape, F32)

    sem = pltpu.SemaphoreType.DMA(())
    res = pl.pallas_call(
        body, name=name,
        out_shape=(*[sem] * (2 * n), *_hbm_like(parts), *_hbm_like(lands), jax.ShapeDtypeStruct(DEP_SPEC_SHAPE, F32)),
        in_specs=[HBM_SPEC] * (2 * n),
        out_specs=(*[SEM_SPEC] * (2 * n), *[HBM_SPEC] * (2 * n), pl.BlockSpec(memory_space=pltpu.VMEM)),
        input_output_aliases={k: 2 * n + k for k in range(2 * n)},
        compiler_params=pltpu.CompilerParams(has_side_effects=DATAFLOW),
    )(*[_hbm(a) for a in parts], *[_hbm(a) for a in lands])
    return res[:n], res[n:2 * n], res[2 * n:3 * n], res[3 * n:4 * n], res[-1]


def swap_wait(send_sem, recv_sem, part, land, after, name):
    def body(p_ref, l_ref, ssem, rsem, after_ref, p_out, l_out):
        x, y, c = _place()
        cp = pltpu.make_async_remote_copy(src_ref=p_ref, dst_ref=l_ref, send_sem=ssem, recv_sem=rsem,
                                          device_id=(x, y, 1 - c), device_id_type=MESH)
        cp.wait_send()
        cp.wait_recv()

    return pl.pallas_call(
        body, name=name, out_shape=tuple(_hbm_like([part, land])),
        in_specs=[HBM_SPEC, HBM_SPEC, SEM_SPEC, SEM_SPEC, ANY_SPEC], out_specs=[HBM_SPEC, HBM_SPEC],
        input_output_aliases={0: 0, 1: 1},
        compiler_params=pltpu.CompilerParams(has_side_effects=DATAFLOW),
    )(part, land, send_sem, recv_sem, after)


def _xor_peer(x, y, c, k):
    px, py, pc = x ^ ((k >> 2) & 1), y ^ ((k >> 1) & 1), c ^ (k & 1)
    return (px, py, pc), 4 * px + 2 * py + pc


def small_start(land, name="small_start"):
    def body(l_ref, ssem, rsem, l_out, token):
        x, y, c = _place()
        me = 4 * x + 2 * y + c
        for k in range(1, N_DEV):
            peer, _ = _xor_peer(x, y, c, k)
            pltpu.make_async_remote_copy(
                src_ref=l_ref.at[me], dst_ref=l_ref.at[me], send_sem=ssem.at[k - 1], recv_sem=rsem.at[k - 1],
                device_id=peer, device_id_type=MESH).start()
        token[...] = jnp.zeros(token.shape, F32)

    sem = pltpu.SemaphoreType.DMA((N_DEV - 1,))
    return pl.pallas_call(
        body, name=name,
        out_shape=(sem, sem, pltpu.HBM(land.shape, land.dtype), jax.ShapeDtypeStruct(DEP_SPEC_SHAPE, F32)),
        in_specs=[HBM_SPEC], out_specs=(SEM_SPEC, SEM_SPEC, HBM_SPEC, pl.BlockSpec(memory_space=pltpu.VMEM)),
        input_output_aliases={0: 2}, compiler_params=pltpu.CompilerParams(has_side_effects=DATAFLOW),
    )(_hbm(land))


def small_wait(send_sem, recv_sem, land, after, name="small_wait"):
    def body(l_ref, ssem, rsem, after_ref, l_out):
        x, y, c = _place()
        me = 4 * x + 2 * y + c
        for k in range(1, N_DEV):
            peer, slot = _xor_peer(x, y, c, k)
            cp = pltpu.make_async_remote_copy(
                src_ref=l_ref.at[me], dst_ref=l_ref.at[slot], send_sem=ssem.at[k - 1], recv_sem=rsem.at[k - 1],
                device_id=peer, device_id_type=MESH)
            cp.wait_send()
            cp.wait_recv()

    return pl.pallas_call(
        body, name=name, out_shape=pltpu.HBM(land.shape, land.dtype),
        in_specs=[HBM_SPEC, SEM_SPEC, SEM_SPEC, ANY_SPEC], out_specs=HBM_SPEC, input_output_aliases={0: 0},
        compiler_params=pltpu.CompilerParams(has_side_effects=DATAFLOW),
    )(land, send_sem, recv_sem, after)


def sum_slots(land, name="sum_slots", tm=256):
    n, rows, c = land.shape
    tm = _row_tile(rows, tm)

    def body(l_ref, o_ref):
        acc = l_ref[0]
        for j in range(1, n):
            acc = acc + l_ref[j]
        o_ref[...] = acc

    return pl.pallas_call(
        body, name=name, grid=(rows // tm,), in_specs=[pl.BlockSpec((n, tm, c), lambda i: (0, i, 0))],
        out_specs=pl.BlockSpec((tm, c), lambda i: (i, 0)), out_shape=jax.ShapeDtypeStruct((rows, c), F32),
        compiler_params=_cp("parallel"),
    )(land)


PACK_ROW_TILE = 256


def _pack(arrays):
    flat = jnp.concatenate([a.reshape(-1).astype(F32) for a in arrays])
    n = flat.shape[0]
    rows = -(-n // LANES)
    rows = -(-rows // PACK_ROW_TILE) * PACK_ROW_TILE
    return jnp.pad(flat, (0, rows * LANES - n)).reshape(rows, LANES)


def _unpack(packed, shapes, lead=()):
    flat = packed.reshape(*lead, -1)
    out, off = [], 0
    for shp in shapes:
        n = math.prod(shp)
        out.append(flat[..., off:off + n].reshape(*lead, *shp))
        off += n
    return out


def _row(vec):
    return vec.reshape(1, -1)


def kernel(x, mix_pre_g, mix_post_g, ffn_pre_g, ffn_post_g, cm_w_in, cm_b_in, cm_dw, cm_dw_b, cm_ln_g, cm_ln_b, cm_w_out, cm_b_out, kv_norm_g, w_kv, w_q, w_o, ffn_w_in, ffn_dw, ffn_dw_b, ffn_w_out, loss_target, m_mix_pre_g, m_mix_post_g, m_ffn_pre_g, m_ffn_post_g, m_cm_w_in, m_cm_b_in, m_cm_dw, m_cm_dw_b, m_cm_ln_g, m_cm_ln_b, m_cm_w_out, m_cm_b_out, m_kv_norm_g, m_w_kv, m_w_q, m_w_o, m_ffn_w_in, m_ffn_dw, m_ffn_dw_b, m_ffn_w_out, v_mix_pre_g, v_mix_post_g, v_ffn_pre_g, v_ffn_post_g, v_cm_w_in, v_cm_b_in, v_cm_dw, v_cm_dw_b, v_cm_ln_g, v_cm_ln_b, v_cm_w_out, v_cm_b_out, v_kv_norm_g, v_w_kv, v_w_q, v_w_o, v_ffn_w_in, v_ffn_dw, v_ffn_dw_b, v_ffn_w_out):
    names = ["mix_pre_g", "mix_post_g", "ffn_pre_g", "ffn_post_g", "cm_w_in", "cm_b_in", "cm_dw", "cm_dw_b", "cm_ln_g",
             "cm_ln_b", "cm_w_out", "cm_b_out", "kv_norm_g", "w_kv", "w_q", "w_o", "ffn_w_in", "ffn_dw", "ffn_dw_b",
             "ffn_w_out"]
    w_in = dict(zip(names, [mix_pre_g, mix_post_g, ffn_pre_g, ffn_post_g, cm_w_in, cm_b_in, cm_dw, cm_dw_b, cm_ln_g,
                            cm_ln_b, cm_w_out, cm_b_out, kv_norm_g, w_kv, w_q, w_o, ffn_w_in, ffn_dw, ffn_dw_b, ffn_w_out]))
    m_in = dict(zip(names, [m_mix_pre_g, m_mix_post_g, m_ffn_pre_g, m_ffn_post_g, m_cm_w_in, m_cm_b_in, m_cm_dw, m_cm_dw_b,
                            m_cm_ln_g, m_cm_ln_b, m_cm_w_out, m_cm_b_out, m_kv_norm_g, m_w_kv, m_w_q, m_w_o, m_ffn_w_in,
                            m_ffn_dw, m_ffn_dw_b, m_ffn_w_out]))
    v_in = dict(zip(names, [v_mix_pre_g, v_mix_post_g, v_ffn_pre_g, v_ffn_post_g, v_cm_w_in, v_cm_b_in, v_cm_dw, v_cm_dw_b,
                            v_cm_ln_g, v_cm_ln_b, v_cm_w_out, v_cm_b_out, v_kv_norm_g, v_w_kv, v_w_q, v_w_o, v_ffn_w_in,
                            v_ffn_dw, v_ffn_dw_b, v_ffn_w_out]))

    bsz, seq, d = x.shape
    t = bsz * seq
    n_b = DEPTH - N_A
    hw = w_o.shape[-1]
    qw = N_GROUPS * hw
    f2 = ffn_dw_b.shape[-1]
    f = f2 // 2
    me_chip = _chip_of(lax.axis_index("x"), lax.axis_index("y"))

    big = ["cm_w_in", "cm_w_out", "w_kv", "w_q", "w_o", "ffn_w_in", "ffn_w_out"]
    row_sharded = ("cm_w_out", "w_o", "ffn_w_out")
    small_sharded = ["cm_b_in", "cm_dw", "cm_dw_b", "cm_ln_g", "cm_ln_b", "cm_b_out", "ffn_dw"]
    small_pack = _pack([w_in[n] for n in small_sharded])
    chunks = [
        [("cm_w_in", 0), ("small", None)],
        [("cm_w_out", 0)],
        [("ffn_w_in", 0), ("ffn_w_out", 0)],
        [("cm_w_in", 1), ("cm_w_out", 1)],
        [("ffn_w_in", 1), ("ffn_w_out", 1)],
        [("w_kv", None)],
        [("w_q", 0), ("w_o", 0)],
        [("ffn_w_in", 2), ("ffn_w_out", 2)],
        [("w_q", 1), ("w_o", 1)],
        [("ffn_w_in", 3), ("ffn_w_out", 3)],
    ]
    pieces = [pc for ch in chunks for pc in ch]
    chunk_of = {pc: ck for ck, ch in enumerate(chunks) for pc in ch}

    me_arr = me_chip.astype(jnp.int32).reshape(1)

    def land_of(pc, dep=None):
        n, l = pc
        if n == "small":
            return cast_place(small_pack, None, me_arr, F32, name="place_small", dep=dep)
        return cast_place(w_in[n], l, me_arr, BF16, name=f"place_{n}_{l}", dep=dep)

    groups = [[0, 1], [2], list(range(3, len(chunks)))]
    g_send, g_recv, lands_f = {}, {}, {}

    def start_group(gi, dep, name):
        cks = groups[gi]
        pcs = [pc for ck in cks for pc in chunks[ck]]
        lands = [land_of(pc, dep if k == 0 else None) for k, pc in enumerate(pcs)]
        send, recv, lands_thru, tok = gather_start(lands, [len(chunks[ck]) for ck in cks], name=name)
        pos = 0
        for j, ck in enumerate(cks):
            g_send[ck], g_recv[ck] = send[j], recv[j]
            lands_f[ck] = lands_thru[pos:pos + len(chunks[ck])]
            pos += len(chunks[ck])
        return tok

    token_a = start_group(0, None, "gather_start_a")
    token = start_group(1, token_a, "gather_start_b")
    weights = {}

    def finish_chunk(ck, after):
        got = gather_wait(g_send[ck], g_recv[ck], lands_f[ck], after, name=f"gather_wait{ck}")
        for pc, arr in zip(chunks[ck], got):
            weights[pc] = arr.reshape(1, -1, arr.shape[-1]) if pc[0] in row_sharded else arr

    def wmat(n, l=None, after=None):
        if (n, l) not in weights:
            finish_chunk(chunk_of[(n, l)], after)
        arr = weights[(n, l)]
        return arr, arr.shape[0]

    finish_chunk(0, token)
    small_full = {}
    for n, arr4 in zip(small_sharded, _unpack(weights[("small", None)], [w_in[n].shape for n in small_sharded], lead=(N_CHIPS,))):
        shp = w_in[n].shape
        small_full[n] = jnp.moveaxis(arr4, 0, -2).reshape(*shp[:-1], N_CHIPS * shp[-1])

    x2d = x.reshape(t, d)
    saved = []
    (h1,) = resid_norm_fwd(x2d, None, None, [_row(mix_pre_g[0])], name="norm_in", dep=token)
    xcur = x2d
    kv_state = None
    for i in range(DEPTH):
        sv = {"x_in": xcur, "h1": h1}
        if i < N_A:
            z = mm_nn(h1, *wmat("cm_w_in", i, h1), 1, 0, bias=_row(small_full["cm_b_in"][i]), name=f"cm_in{i}",
                      spg=N_CHIPS)
            u2 = glu_conv_fwd(z.reshape(bsz, seq, 2 * d), small_full["cm_dw"][i], _row(small_full["cm_dw_b"][i]),
                              name=f"glu_conv{i}").reshape(t, d)
            u4 = ln_silu_fwd(u2, _row(small_full["cm_ln_g"][i]), _row(small_full["cm_ln_b"][i]), name=f"ln_silu{i}")
            y = mm_nn(u4, *wmat("cm_w_out", i, u4), 1, 0, bias=_row(small_full["cm_b_out"][i]), out_dtype=BF16,
                      name=f"cm_out{i}")
            sv.update(z=z, u2=u2, u4=u4)
        else:
            j = i - N_A
            q = mm_nn(h1, *wmat("w_q", j, h1), 1, 0, name=f"q_proj{j}", head_major=True, spg=N_CHIPS).reshape(-1, bsz, seq, HEAD_DIM)
            outs, lses = [], []
            for g, dil in enumerate(DILATIONS):
                o_g, l_g = attn_fwd(q, kv_state["kv"], g, dil, hw, name=f"attn_fwd{j}_{g}")
                outs.append(o_g.reshape(t, hw))
                lses.append(l_g.reshape(t, LANES))
            merged, lsej = attn_merge(outs, lses, name=f"attn_merge{j}")
            y = mm_nn(merged, *wmat("w_o", j, merged), 1, 0, out_dtype=BF16, name=f"o_proj{j}")
            sv.update(q=q, merged=merged, lsej=lsej)
        x1, h2 = resid_norm_fwd(xcur, y, _row(mix_post_g[i]), [_row(ffn_pre_g[i])], name=f"resid_mix{i}")
        ffn_after = start_group(2, x1, "gather_start_c") if i == 0 else h2
        p = mm_nn(h2, *wmat("ffn_w_in", i, ffn_after), 1, 0, out_dtype=BF16, name=f"ffn_in{i}", spg=2)
        s_act = ffn_mid_fwd(p.reshape(bsz, seq, f2), small_full["ffn_dw"][i], _row(ffn_dw_b[i]), name=f"ffn_mid{i}").reshape(t, f)
        y2 = mm_nn(s_act, *wmat("ffn_w_out", i), 1, 0, out_dtype=BF16, name=f"ffn_out{i}")
        next_gains = []
        if i + 1 < DEPTH:
            next_gains.append(_row(mix_pre_g[i + 1]))
        if i == N_A - 1:
            next_gains.append(_row(kv_norm_g))
        res = resid_norm_fwd(x1, y2, _row(ffn_post_g[i]), next_gains, name=f"resid_ffn{i}")
        sv.update(y=y, x1=x1, h2=h2, p=p, s=s_act, y2=y2)
        saved.append(sv)
        xcur = res[0]
        if i + 1 < DEPTH:
            h1 = res[1]
        if i == N_A - 1:
            kvn = res[2]
            kv = mm_nn(kvn, *wmat("w_kv", None, kvn), 1, 0, name="kv_proj", head_major=True, spg=2).reshape(-1, bsz, seq, HEAD_DIM)
            kv_state = {"kv": kv, "kvn": kvn, "x_a": xcur}

    dx, loss_tile = loss_fwd_bwd(xcur, loss_target.reshape(t, d))
    loss = lax.psum(loss_tile[0, 0], ("x", "y", "c"))

    gsm = {n: [None] * w_in[n].shape[0] for n in
           ["mix_pre_g", "mix_post_g", "ffn_pre_g", "ffn_post_g", "cm_b_in", "cm_dw", "cm_dw_b", "cm_ln_g", "cm_ln_b",
            "cm_b_out", "ffn_dw", "ffn_dw_b"]}
    gbig = {}
    in_flight = []
    dep = None

    def start_scatter(pcs, tag):
        ssem, rsem, g_f, r_f, tok = scatter_start([gbig[pc] for pc in pcs], name=f"scatter_start_{tag}")
        in_flight.append((pcs, ssem, rsem, g_f, r_f))
        return tok

    dk_buf = dv_buf = None
    for i in range(DEPTH - 1, -1, -1):
        sv = saved[i]
        dy2, dg, _ = norm_bwd(sv["y2"], _row(ffn_post_g[i]), dx, out_dtype=BF16, name=f"bwd_ffn_post{i}", dep=dep)
        gsm["ffn_post_g"][i] = dg
        ds = mm_nt(dy2, *wmat("ffn_w_out", i), 1, 0, out_dtype=BF16, name=f"bwd_ffn_out_dx{i}")
        gbig[("ffn_w_out", i)] = mm_tn(sv["s"], dy2, 1, name=f"bwd_ffn_out_dw{i}").reshape(N_CHIPS, f // N_CHIPS, d)
        dpa, dpg, ddwa, ddwg, ddba, ddbg = ffn_mid_bwd(sv["p"].reshape(bsz, seq, f2), small_full["ffn_dw"][i], _row(ffn_dw_b[i]),
                                                       ds.reshape(bsz, seq, f), name=f"bwd_ffn_mid{i}")
        gsm["ffn_dw"][i] = jnp.concatenate([jnp.sum(ddwa, axis=1), jnp.sum(ddwg, axis=1)], axis=-1)
        gsm["ffn_dw_b"][i] = jnp.concatenate([ddba, ddbg], axis=-1)
        dp = [dpa.reshape(t, f), dpg.reshape(t, f)]
        dh2 = mm_nt(dp, *wmat("ffn_w_in", i), 1, 0, out_dtype=BF16, name=f"bwd_ffn_in_dx{i}", spg=2)
        gbig[("ffn_w_in", i)] = mm_tn(sv["h2"], dp, N_CHIPS, name=f"bwd_ffn_in_dw{i}", spg=2)
        dx1, dg, _ = norm_bwd(sv["x1"], _row(ffn_pre_g[i]), dh2, add=dx, name=f"bwd_ffn_pre{i}")
        gsm["ffn_pre_g"][i] = dg
        dep = start_scatter([("ffn_w_in", 0), ("ffn_w_out", 0)], "ffn0") if i == 0 else None
        dy, dg, dbias = norm_bwd(sv["y"], _row(mix_post_g[i]), dx1, out_dtype=BF16, name=f"bwd_mix_post{i}", dep=dep)
        gsm["mix_post_g"][i] = dg
        if i < N_A:
            gsm["cm_b_out"][i] = dbias
            du4 = mm_nt(dy, *wmat("cm_w_out", i), 1, 0, out_dtype=BF16, name=f"bwd_cm_out_dx{i}")
            gbig[("cm_w_out", i)] = mm_tn(sv["u4"], dy, 1, name=f"bwd_cm_out_dw{i}").reshape(N_CHIPS, d // N_CHIPS, d)
            du2, dlg, dlb = ln_silu_bwd(sv["u2"], _row(small_full["cm_ln_g"][i]), _row(small_full["cm_ln_b"][i]), du4,
                                        name=f"bwd_ln_silu{i}")
            gsm["cm_ln_g"][i], gsm["cm_ln_b"][i] = dlg, dlb
            dza, dzg, ddw, ddwb, dba, dbg = glu_conv_bwd(sv["z"].reshape(bsz, seq, 2 * d), small_full["cm_dw"][i],
                                                         du2.reshape(bsz, seq, d), name=f"bwd_glu_conv{i}")
            gsm["cm_dw"][i] = jnp.sum(ddw, axis=1)
            gsm["cm_dw_b"][i] = ddwb
            gsm["cm_b_in"][i] = jnp.concatenate([dba, dbg], axis=-1)
            dz = [dza.reshape(t, d), dzg.reshape(t, d)]
            dh1 = mm_nt(dz, *wmat("cm_w_in", i), 1, 0, out_dtype=BF16, name=f"bwd_cm_in_dx{i}", spg=2)
            gbig[("cm_w_in", i)] = mm_tn(sv["h1"], dz, N_CHIPS, name=f"bwd_cm_in_dw{i}", spg=2)
        else:
            j = i - N_A
            dmerged = mm_nt(dy, *wmat("w_o", j), 1, 0, name=f"bwd_o_proj_dx{j}")
            gbig[("w_o", j)] = mm_tn(sv["merged"], dy, 1, name=f"bwd_o_proj_dw{j}").reshape(N_CHIPS, hw // N_CHIPS, d)
            dmt = attn_bwd_prep(dmerged, sv["merged"], name=f"bwd_attn_prep{j}")
            dq_buf = None
            add_to_kv = dk_buf is not None
            for g, dil in enumerate(DILATIONS):
                dq_buf, dk_buf, dv_buf = attn_bwd(
                    sv["q"], kv_state["kv"], g, dil, dmerged.reshape(bsz, seq, hw), sv["lsej"].reshape(bsz, seq, LANES),
                    dmt.reshape(bsz, seq, LANES), dq_buf, dk_buf, dv_buf, add_to_kv, hw, name=f"attn_bwd{j}_{g}")
            dq = dq_buf.reshape(t, qw)
            dh1 = mm_nt(dq, *wmat("w_q", j), 1, 0, out_dtype=BF16, name=f"bwd_q_proj_dx{j}", tm=512, spg=N_CHIPS)
            gbig[("w_q", j)] = mm_tn(sv["h1"], dq, N_CHIPS, name=f"bwd_q_proj_dw{j}", tm=512, spg=N_CHIPS)
        dx, dg, _ = norm_bwd(sv["x_in"], _row(mix_pre_g[i]), dh1, add=dx1, name=f"bwd_mix_pre{i}")
        gsm["mix_pre_g"][i] = dg
        if i > N_A:
            dep = start_scatter([("ffn_w_in", i), ("ffn_w_out", i), ("w_q", i - N_A), ("w_o", i - N_A)], f"l{i}")
        elif 0 < i < N_A:
            dep = start_scatter([("ffn_w_in", i), ("ffn_w_out", i), ("cm_w_in", i), ("cm_w_out", i)], f"l{i}")
        elif i == 0:
            last_token = start_scatter([("cm_w_in", 0), ("cm_w_out", 0)], "cm0")
        if i == N_A:
            dkv = [dk_buf.reshape(t, qw), dv_buf.reshape(t, qw)]
            dkvn = mm_nt(dkv, *wmat("w_kv"), 1, 0, out_dtype=BF16, name="bwd_kv_proj_dx")
            gbig[("w_kv", None)] = mm_tn(kv_state["kvn"], dkv, N_CHIPS, name="bwd_kv_proj_dw")
            dx, dg_kv, _ = norm_bwd(kv_state["x_a"], _row(kv_norm_g), dkvn, add=dx, name="bwd_kv_norm")
            dep = start_scatter([("ffn_w_in", i), ("ffn_w_out", i), ("w_q", 0), ("w_o", 0), ("w_kv", None)], f"l{i}")
    grad_x = dx.reshape(bsz, seq, d)

    plane_of = {}
    outs_g, outs_d, outs_m, outs_v = {}, {}, {}, {}

    def finish_scatter(k, after):
        pcs, ssem, rsem, g_f, r_f = in_flight[k]
        g_done, r_done = scatter_wait(ssem, rsem, g_f, r_f, after, name=f"scatter_wait{k}")
        for (n, l), g_arr, r_arr in zip(pcs, g_done, r_done):
            n_layers = 1 if l is None else w_in[n].shape[0]
            plane_of[n] = sum_parts(g_arr, r_arr, me_arr, name=f"sum_chips_{n}_{l}", layer=l or 0, n_layers=n_layers,
                                    out_buf=plane_of.get(n))

    def update(group, tag, after):
        plane = [plane_of[n] for n in group]
        ssems, rsems, plane_f, land_f, _ = swap_start(plane, name=f"swap_start_{tag}")
        for k, n in enumerate(group):
            p_mine, p_other = swap_wait(ssems[k], rsems[k], plane_f[k], land_f[k], after, name=f"swap_wait_{n}")
            shp = w_in[n].shape
            flat = lambda a: a.reshape(-1, shp[-1])
            g_, d_, m_, v_ = adamw(flat(w_in[n]), flat(m_in[n]), flat(v_in[n]), [p_mine, p_other], name=f"adamw_{n}")
            outs_g[n], outs_d[n], outs_m[n], outs_v[n] = (a.reshape(shp) for a in (g_, d_, m_, v_))
            after = v_
        return after

    small_names = [n for n in names if n not in big]
    small_shapes_full = {}
    small_grads_full = []
    for n in small_names:
        if n == "kv_norm_g":
            gfull = dg_kv.reshape(-1)
        elif n in ("cm_dw", "ffn_dw"):
            gfull = jnp.stack(gsm[n], axis=0)
        else:
            gfull = jnp.stack([a.reshape(-1) for a in gsm[n]], axis=0)
        small_shapes_full[n] = gfull.shape
        small_grads_full.append(gfull)
    dev_arr = (4 * lax.axis_index("x") + 2 * lax.axis_index("y") + lax.axis_index("c")).astype(jnp.int32).reshape(1)
    small_land = cast_place(_pack(small_grads_full) + last_token[0, 0], None, dev_arr, F32, name="place_small_grads",
                            nslots=N_DEV)
    sm_send, sm_recv, small_land, small_token = small_start(small_land)

    for k in range(len(in_flight) - 1):
        finish_scatter(k, small_token)
    done = update(["w_kv", "w_q", "w_o", "ffn_w_in", "ffn_w_out"], "a", small_token)
    finish_scatter(len(in_flight) - 1, done)
    done = update(["cm_w_in", "cm_w_out"], "b", done)

    summed = sum_slots(small_wait(sm_send, sm_recv, small_land, done))
    g_full = dict(zip(small_names, _unpack(summed, [small_shapes_full[n] for n in small_names])))
    g_loc = {}
    for n in small_names:
        if n in small_sharded:
            width = w_in[n].shape[-1]
            g_loc[n] = lax.dynamic_slice_in_dim(g_full[n], me_chip * width, width, axis=g_full[n].ndim - 1)
        else:
            g_loc[n] = g_full[n]
    res = adamw(_pack([w_in[n] for n in small_names]), _pack([m_in[n] for n in small_names]),
                _pack([v_in[n] for n in small_names]), [_pack([g_loc[n] for n in small_names])], name="adamw_small")
    shapes_loc = [w_in[n].shape for n in small_names]
    for dst, packed in zip((outs_g, outs_d, outs_m, outs_v), res):
        for n, a in zip(small_names, _unpack(packed, shapes_loc)):
            dst[n] = a

    return (loss, grad_x, *[outs_g[n] for n in names], *[outs_d[n] for n in names],
            *[outs_m[n] for n in names], *[outs_v[n] for n in names])
```

```python
import functools
import math

import jax
import jax.numpy as jnp
from jax import lax
from jax.experimental import pallas as pl
from jax.experimental.pallas import tpu as pltpu

F32 = jnp.float32
BF16 = jnp.bfloat16
EPS = 1e-6
NEG_INF = -1e30
N_A = 2
DEPTH = 4
N_GROUPS = 3
DILATIONS = (1, 4, 16)
HEAD_DIM = 128
BLK = 128
LANES = 128
N_CHIPS = 4
N_DEV = 8
VMEM_LIMIT_V7X = 56 * 1024 * 1024

ADAM_LR = 0.001
ADAM_B1 = 0.9
ADAM_B2 = 0.999
ADAM_EPS = 1e-08
ADAM_WD = 0.01
ADAM_STEP = 10

MESH = pl.DeviceIdType.MESH


def _cp(*sem, **kw):
    return pltpu.CompilerParams(dimension_semantics=sem if sem else None, vmem_limit_bytes=VMEM_LIMIT_V7X, **kw)


def _dot(a, b):
    return jnp.dot(a, b, preferred_element_type=F32)


def _dot_nt(a, b):
    return lax.dot_general(a, b, (((1,), (1,)), ((), ())), preferred_element_type=F32)


def _dot_tn(a, b):
    return lax.dot_general(a, b, (((0,), (0,)), ((), ())), preferred_element_type=F32)


def _sigmoid(x):
    return 1.0 / (1.0 + jnp.exp(-x))


def _row_tile(n, want):
    if n <= want:
        return n
    for t in range(want - want % 8, 7, -8):
        if n % t == 0:
            return t
    raise ValueError(f"no row tile for {n} rows")


def mm_nn(a, w, nsh, stride, layer, bias=None, out_dtype=F32, name="mm_nn", tm=1024, head_major=False, spg=1):
    assert stride == 1 and layer == 0 and nsh % spg == 0
    m, k = a.shape
    _, k2, ns = w.shape
    assert k == k2
    tm = _row_tile(m, tm)
    has_bias = bias is not None
    hps = ns // HEAD_DIM

    def body(*refs):
        if has_bias:
            a_ref, w_ref, b_ref, o_ref = refs
        else:
            a_ref, w_ref, o_ref = refs
        av = a_ref[...].astype(BF16)
        for jj in range(spg):
            acc = _dot(av, w_ref[jj])
            if has_bias:
                acc = acc + b_ref[:, jj * ns:(jj + 1) * ns]
            if head_major:
                for hh in range(hps):
                    o_ref[jj * hps + hh] = acc[:, hh * HEAD_DIM:(hh + 1) * HEAD_DIM].astype(out_dtype)
            else:
                o_ref[:, jj * ns:(jj + 1) * ns] = acc.astype(out_dtype)

    in_specs = [
        pl.BlockSpec((tm, k), lambda j, i: (i, 0)),
        pl.BlockSpec((spg, k, ns), lambda j, i: (j, 0, 0)),
    ]
    args = [a, w]
    if has_bias:
        in_specs.append(pl.BlockSpec((1, spg * ns), lambda j, i: (0, j)))
        args.append(bias)
    return pl.pallas_call(
        body,
        name=name,
        grid=(nsh // spg, m // tm),
        in_specs=in_specs,
        out_specs=(pl.BlockSpec((spg * hps, tm, HEAD_DIM), lambda j, i: (j, i, 0)) if head_major
                   else pl.BlockSpec((tm, spg * ns), lambda j, i: (i, j))),
        out_shape=jax.ShapeDtypeStruct((nsh * hps, m, HEAD_DIM) if head_major else (m, nsh * ns), out_dtype),
        compiler_params=_cp("parallel", "parallel"),
    )(*args)


def _split_parts(dy, nsh, spg):
    dys = list(dy) if isinstance(dy, (list, tuple)) else [dy]
    per = nsh // len(dys)
    assert per % spg == 0
    return dys, per, per // spg


def mm_nt(dy, w, nsh, stride, layer, out_dtype=F32, name="mm_nt", tm=1024, spg=1):
    assert stride == 1 and layer == 0
    dys, per, gpp = _split_parts(dy, nsh, spg)
    npart = len(dys)
    m = dys[0].shape[0]
    _, k, ns = w.shape
    assert all(d.shape == (m, per * ns) for d in dys)
    tm = _row_tile(m, tm)
    ngrp = nsh // spg

    def body(*refs):
        dy_refs = refs[:npart]
        w_ref, o_ref, acc_ref = refs[npart:]
        j = pl.program_id(1)

        @pl.when(j == 0)
        def _():
            acc_ref[...] = jnp.zeros(acc_ref.shape, F32)

        for pi in range(npart):
            @pl.when(j // gpp == pi)
            def _(pi=pi):
                part = _dot_nt(dy_refs[pi][:, 0:ns].astype(BF16), w_ref[0])
                for jj in range(1, spg):
                    part = part + _dot_nt(dy_refs[pi][:, jj * ns:(jj + 1) * ns].astype(BF16), w_ref[jj])
                acc_ref[...] += part

        @pl.when(j == ngrp - 1)
        def _():
            o_ref[...] = acc_ref[...].astype(out_dtype)

    dy_specs = [pl.BlockSpec((tm, spg * ns), lambda i, j, pi=pi: (i, jnp.clip(j - pi * gpp, 0, gpp - 1)))
                for pi in range(npart)]
    return pl.pallas_call(
        body,
        name=name,
        grid=(m // tm, ngrp),
        in_specs=[*dy_specs, pl.BlockSpec((spg, k, ns), lambda i, j: (j, 0, 0))],
        out_specs=pl.BlockSpec((tm, k), lambda i, j: (i, 0)),
        out_shape=jax.ShapeDtypeStruct((m, k), out_dtype),
        scratch_shapes=[pltpu.VMEM((tm, k), F32)],
        compiler_params=_cp("parallel", "arbitrary"),
    )(*dys, w)


def mm_tn(a, dy, nsh, name="mm_tn", tm=1024, spg=1):
    dys, per, gpp = _split_parts(dy, nsh, spg)
    npart = len(dys)
    m, k = a.shape
    ns = dys[0].shape[1] // per
    assert all(d.shape == (m, per * ns) for d in dys)
    tm = _row_tile(m, tm)
    nt = m // tm

    def body(*refs):
        a_ref = refs[0]
        dy_refs = refs[1:1 + npart]
        o_ref, acc_ref = refs[1 + npart:]
        j = pl.program_id(0)
        i = pl.program_id(1)

        @pl.when(i == 0)
        def _():
            acc_ref[...] = jnp.zeros(acc_ref.shape, F32)

        for pi in range(npart):
            @pl.when(j // gpp == pi)
            def _(pi=pi):
                at = a_ref[...].astype(BF16).T
                for jj in range(spg):
                    acc_ref[jj] += _dot(at, dy_refs[pi][:, jj * ns:(jj + 1) * ns].astype(BF16))

        @pl.when(i == nt - 1)
        def _():
            o_ref[...] = acc_ref[...].astype(BF16)

    dy_specs = [
        pl.BlockSpec((tm, spg * ns),
                     lambda j, i, pi=pi: (jnp.where(j // gpp == pi, i, 0), jnp.clip(j - pi * gpp, 0, gpp - 1)))
        for pi in range(npart)
    ]
    return pl.pallas_call(
        body,
        name=name,
        grid=(nsh // spg, nt),
        in_specs=[pl.BlockSpec((tm, k), lambda j, i: (i, 0)), *dy_specs],
        out_specs=pl.BlockSpec((spg, k, ns), lambda j, i: (j, 0, 0)),
        out_shape=jax.ShapeDtypeStruct((nsh, k, ns), BF16),
        scratch_shapes=[pltpu.VMEM((spg, k, ns), F32)],
        compiler_params=_cp("parallel", "arbitrary"),
    )(a, *dys)


DEP_SPEC_SHAPE = (8, LANES)


def resid_norm_fwd(x, y, g_post, next_gains, name, tm=1024, dep=None):
    t, d = x.shape
    tm = _row_tile(t, tm)
    has_y = y is not None
    n_next = len(next_gains)
    n_dep = 0 if dep is None else 1

    def body(*refs):
        x_ref = refs[0]
        pos = 1
        if has_y:
            y_ref, gp_ref = refs[1], refs[2]
            pos = 3
        gn_refs = refs[pos:pos + n_next]
        outs = refs[pos + n_next + n_dep:]
        xv = x_ref[...]
        o = 0
        if has_y:
            yv = y_ref[...].astype(F32)
            r = lax.rsqrt(jnp.mean(yv * yv, axis=-1, keepdims=True) + EPS)
            xv = xv + (yv * r) * gp_ref[...]
            outs[0][...] = xv
            o = 1
        if n_next:
            xn = xv * lax.rsqrt(jnp.mean(xv * xv, axis=-1, keepdims=True) + EPS)
            for k in range(n_next):
                outs[o + k][...] = (xn * gn_refs[k][...]).astype(BF16)

    row = pl.BlockSpec((tm, d), lambda i: (i, 0))
    vec = pl.BlockSpec((1, d), lambda i: (0, 0))
    args, in_specs = [x], [row]
    if has_y:
        args += [y, g_post]
        in_specs += [row, vec]
    args += list(next_gains)
    in_specs += [vec] * n_next
    if n_dep:
        args.append(dep)
        in_specs.append(pl.BlockSpec(DEP_SPEC_SHAPE, lambda i: (0, 0)))
    out_shape, out_specs = [], []
    if has_y:
        out_shape.append(jax.ShapeDtypeStruct((t, d), F32))
        out_specs.append(row)
    for _ in range(n_next):
        out_shape.append(jax.ShapeDtypeStruct((t, d), BF16))
        out_specs.append(row)
    return pl.pallas_call(
        body, name=name, grid=(t // tm,), in_specs=in_specs, out_specs=out_specs, out_shape=out_shape,
        compiler_params=_cp("parallel"),
    )(*args)


def norm_bwd(x, g, dy, add=None, out_dtype=F32, name="norm_bwd", tm=1024, dep=None):
    t, d = x.shape
    tm = _row_tile(t, tm)
    has_add = add is not None

    def body(*refs):
        x_ref, g_ref, dy_ref = refs[:3]
        add_ref = refs[3] if has_add else None
        dx_ref, dg_ref, cs_ref = refs[-3:]
        i = pl.program_id(0)
        xv = x_ref[...].astype(F32)
        dyv = dy_ref[...].astype(F32)
        r = lax.rsqrt(jnp.mean(xv * xv, axis=-1, keepdims=True) + EPS)
        gd = dyv * g_ref[...]
        dx = r * gd - xv * ((r * r * r) * jnp.mean(xv * gd, axis=-1, keepdims=True))
        if has_add:
            dx = dx + add_ref[...]
        dx_ref[...] = dx.astype(out_dtype)
        dg = jnp.sum(dyv * (xv * r), axis=0, keepdims=True)
        cs = jnp.sum(dx, axis=0, keepdims=True)

        @pl.when(i == 0)
        def _():
            dg_ref[...] = dg
            cs_ref[...] = cs

        @pl.when(i > 0)
        def _():
            dg_ref[...] += dg
            cs_ref[...] += cs

    row = pl.BlockSpec((tm, d), lambda i: (i, 0))
    vec = pl.BlockSpec((1, d), lambda i: (0, 0))
    args, in_specs = [x, g, dy], [row, vec, row]
    if has_add:
        args.append(add)
        in_specs.append(row)
    if dep is not None:
        args.append(dep)
        in_specs.append(pl.BlockSpec(DEP_SPEC_SHAPE, lambda i: (0, 0)))
    return pl.pallas_call(
        body, name=name, grid=(t // tm,), in_specs=in_specs,
        out_specs=[row, vec, vec],
        out_shape=[jax.ShapeDtypeStruct((t, d), out_dtype), jax.ShapeDtypeStruct((1, d), F32),
                   jax.ShapeDtypeStruct((1, d), F32)],
        compiler_params=_cp("arbitrary"),
    )(*args)


def loss_fwd_bwd(x, target, name="loss", tm=1024):
    t, d = x.shape
    tm = _row_tile(t, tm)

    def body(x_ref, t_ref, dx_ref, l_ref):
        i = pl.program_id(0)
        err = x_ref[...] - t_ref[...]
        dx_ref[...] = err * (1.0 / d)
        part = 0.5 * jnp.sum(jnp.mean(err * err, axis=-1, keepdims=True), axis=0, keepdims=True)
        part = jnp.broadcast_to(part, l_ref.shape)

        @pl.when(i == 0)
        def _():
            l_ref[...] = part

        @pl.when(i > 0)
        def _():
            l_ref[...] += part

    row = pl.BlockSpec((tm, d), lambda i: (i, 0))
    return pl.pallas_call(
        body, name=name, grid=(t // tm,), in_specs=[row, row],
        out_specs=[row, pl.BlockSpec((8, LANES), lambda i: (0, 0))],
        out_shape=[jax.ShapeDtypeStruct((t, d), F32), jax.ShapeDtypeStruct((8, LANES), F32)],
        compiler_params=_cp("arbitrary"),
    )(x, target)


CONV_HALO = 32
CONV_CHUNK = 128


def glu_conv_fwd(z, dw, dwb, name, tc=128):
    b, s, c2 = z.shape
    c = c2 // 2
    kw = dw.shape[0]
    tc = min(tc, c)
    nc = c // tc
    ch = min(CONV_CHUNK, s)
    halo = CONV_HALO
    assert kw - 1 <= halo and s % ch == 0

    nch = s // ch

    def body(a_ref, g_ref, w_ref, b_ref, o_ref, pad_ref):
        _fill_glu_slabs(a_ref, g_ref, pad_ref, nch, ch, halo)

        def chunk(ci, carry):
            r0 = pl.multiple_of(ci * ch, ch)
            acc = b_ref[...]
            for k, tap in enumerate(_taps_front(pad_ref, ci, kw, ch, halo)):
                acc = acc + w_ref[k:k + 1, :] * tap
            o_ref[pl.ds(r0, ch), :] = acc
            return carry

        lax.fori_loop(0, nch, chunk, 0)

    return pl.pallas_call(
        body, name=name, grid=(b, nc),
        in_specs=[
            pl.BlockSpec((None, s, tc), lambda bi, i: (bi, 0, i)),
            pl.BlockSpec((None, s, tc), lambda bi, i: (bi, 0, i + nc)),
            pl.BlockSpec((kw, tc), lambda bi, i: (0, i)),
            pl.BlockSpec((1, tc), lambda bi, i: (0, i)),
        ],
        out_specs=pl.BlockSpec((None, s, tc), lambda bi, i: (bi, 0, i)),
        out_shape=jax.ShapeDtypeStruct((b, s, c), F32),
        scratch_shapes=[pltpu.VMEM((nch, ch + halo, tc), F32)],
        compiler_params=_cp("parallel", "parallel"),
    )(z, z, dw, dwb)


def glu_conv_bwd(z, dw, du2, name, tc=128):
    b, s, c2 = z.shape
    c = c2 // 2
    kw = dw.shape[0]
    tc = min(tc, c)
    nc = c // tc
    ch = min(CONV_CHUNK, s)
    nch = s // ch
    halo = CONV_HALO

    def body(a_ref, g_ref, w_ref, du_ref, dza_ref, dzg_ref, ddw_ref, ddwb_ref, dba_ref, dbg_ref, upad_ref, dpad_ref):
        bi = pl.program_id(1)

        @pl.when(bi == 0)
        def _():
            ddw_ref[...] = jnp.zeros(ddw_ref.shape, F32)
            ddwb_ref[...] = jnp.zeros(ddwb_ref.shape, F32)
            dba_ref[...] = jnp.zeros(dba_ref.shape, F32)
            dbg_ref[...] = jnp.zeros(dbg_ref.shape, F32)

        _fill_glu_slabs(a_ref, g_ref, upad_ref, nch, ch, halo)
        dpad_ref[nch - 1, ch:ch + halo, :] = jnp.zeros((halo, tc), F32)
        dpad_ref[nch - 1, 0:ch, :] = du_ref[s - ch:s, :]

        def fill(ci, carry):
            r0 = pl.multiple_of(ci * ch, ch)
            dpad_ref[ci, :, :] = du_ref[pl.ds(r0, ch + halo), :]
            return carry

        lax.fori_loop(0, nch - 1, fill, 0)

        def chunk(ci, carry):
            r0 = pl.multiple_of(ci * ch, ch)
            du_c = du_ref[pl.ds(r0, ch), :]
            taps_u = _taps_front(upad_ref, ci, kw, ch, halo)
            taps_d = _taps_at(dpad_ref, ci, list(range(kw)), ch)
            du1 = w_ref[kw - 1:kw, :] * du_c
            ddw_ref[kw - 1] += jnp.sum((du_c * taps_u[kw - 1]).reshape(ch // 8, 8, tc), axis=0)
            for j in range(1, kw):
                du1 = du1 + w_ref[kw - 1 - j:kw - j, :] * taps_d[j]
                ddw_ref[kw - 1 - j] += jnp.sum((du_c * taps_u[kw - 1 - j]).reshape(ch // 8, 8, tc), axis=0)
            av = a_ref[pl.ds(r0, ch), :]
            sg = _sigmoid(g_ref[pl.ds(r0, ch), :])
            dza = du1 * sg
            dzg = du1 * av * (sg * (1.0 - sg))
            dza_ref[pl.ds(r0, ch), :] = dza.astype(BF16)
            dzg_ref[pl.ds(r0, ch), :] = dzg.astype(BF16)
            dba_ref[...] += jnp.sum(dza, axis=0, keepdims=True)
            dbg_ref[...] += jnp.sum(dzg, axis=0, keepdims=True)
            ddwb_ref[...] += jnp.sum(du_c, axis=0, keepdims=True)
            return carry

        lax.fori_loop(0, s // ch, chunk, 0)

    blk = lambda off: pl.BlockSpec((None, s, tc), lambda i, bi: (bi, 0, i + off))
    vec = pl.BlockSpec((1, tc), lambda i, bi: (0, i))
    return pl.pallas_call(
        body, name=name, grid=(nc, b),
        in_specs=[blk(0), blk(nc), pl.BlockSpec((kw, tc), lambda i, bi: (0, i)), blk(0)],
        out_specs=[blk(0), blk(0), pl.BlockSpec((kw, 8, tc), lambda i, bi: (0, 0, i)), vec, vec, vec],
        out_shape=[
            jax.ShapeDtypeStruct((b, s, c), BF16), jax.ShapeDtypeStruct((b, s, c), BF16),
            jax.ShapeDtypeStruct((kw, 8, c), F32), jax.ShapeDtypeStruct((1, c), F32),
            jax.ShapeDtypeStruct((1, c), F32), jax.ShapeDtypeStruct((1, c), F32),
        ],
        scratch_shapes=[pltpu.VMEM((nch, ch + halo, tc), F32), pltpu.VMEM((nch, ch + halo, tc), F32)],
        compiler_params=_cp("parallel", "arbitrary"),
    )(z, z, dw, du2)


def _fill_glu_slabs(a_ref, g_ref, pad_ref, nch, ch, halo):
    tc = a_ref.shape[-1]
    pad_ref[0, 0:halo, :] = jnp.zeros((halo, tc), F32)
    pad_ref[0, halo:halo + ch, :] = a_ref[0:ch, :] * _sigmoid(g_ref[0:ch, :])

    def fill(ci, carry):
        r0 = pl.multiple_of(ci * ch, ch)
        pad_ref[ci, 0:halo, :] = pad_ref[ci - 1, ch:ch + halo, :]
        pad_ref[ci, halo:halo + ch, :] = a_ref[pl.ds(r0, ch), :] * _sigmoid(g_ref[pl.ds(r0, ch), :])
        return carry

    lax.fori_loop(1, nch, fill, 0)


def ln_silu_fwd(u, g, bvec, name, tm=1024):
    t, d = u.shape
    tm = _row_tile(t, tm)

    def body(u_ref, g_ref, b_ref, o_ref):
        uv = u_ref[...]
        mu = jnp.mean(uv, axis=-1, keepdims=True)
        xc = uv - mu
        var = jnp.mean(xc * xc, axis=-1, keepdims=True)
        v = (xc * lax.rsqrt(var + EPS)) * g_ref[...] + b_ref[...]
        o_ref[...] = (v * _sigmoid(v)).astype(BF16)

    row = pl.BlockSpec((tm, d), lambda i: (i, 0))
    vec = pl.BlockSpec((1, d), lambda i: (0, 0))
    return pl.pallas_call(
        body, name=name, grid=(t // tm,), in_specs=[row, vec, vec], out_specs=row,
        out_shape=jax.ShapeDtypeStruct((t, d), BF16), compiler_params=_cp("parallel"),
    )(u, g, bvec)


def ln_silu_bwd(u, g, bvec, dout, name, tm=1024):
    t, d = u.shape
    tm = _row_tile(t, tm)

    def body(u_ref, g_ref, b_ref, do_ref, du_ref, dg_ref, db_ref):
        i = pl.program_id(0)
        uv = u_ref[...]
        mu = jnp.mean(uv, axis=-1, keepdims=True)
        xc = uv - mu
        var = jnp.mean(xc * xc, axis=-1, keepdims=True)
        rstd = lax.rsqrt(var + EPS)
        n = xc * rstd
        v = n * g_ref[...] + b_ref[...]
        sg = _sigmoid(v)
        dv = do_ref[...].astype(F32) * (sg * (1.0 + v * (1.0 - sg)))
        dn = dv * g_ref[...]
        du_ref[...] = rstd * (dn - jnp.mean(dn, axis=-1, keepdims=True) - n * jnp.mean(dn * n, axis=-1, keepdims=True))
        dg = jnp.sum(dv * n, axis=0, keepdims=True)
        db = jnp.sum(dv, axis=0, keepdims=True)

        @pl.when(i == 0)
        def _():
            dg_ref[...] = dg
            db_ref[...] = db

        @pl.when(i > 0)
        def _():
            dg_ref[...] += dg
            db_ref[...] += db

    row = pl.BlockSpec((tm, d), lambda i: (i, 0))
    vec = pl.BlockSpec((1, d), lambda i: (0, 0))
    return pl.pallas_call(
        body, name=name, grid=(t // tm,), in_specs=[row, vec, vec, row], out_specs=[row, vec, vec],
        out_shape=[jax.ShapeDtypeStruct((t, d), F32), jax.ShapeDtypeStruct((1, d), F32), jax.ShapeDtypeStruct((1, d), F32)],
        compiler_params=_cp("arbitrary"),
    )(u, g, bvec, dout)


FFN_HALO = 8


def _fill_front_halo(src_ref, pad_ref, nch, ch, halo):
    tc = src_ref.shape[-1]
    pad_ref[0, 0:halo, :] = jnp.zeros((halo, tc), F32)
    pad_ref[0, halo:halo + ch, :] = src_ref[0:ch, :].astype(F32)

    def fill(ci, carry):
        r0 = pl.multiple_of(ci * ch, ch)
        pad_ref[ci, 0:halo, :] = src_ref[pl.ds(r0 - 2 * halo, 2 * halo), :].astype(F32)[halo:, :]
        pad_ref[ci, halo:halo + ch, :] = src_ref[pl.ds(r0, ch), :].astype(F32)
        return carry

    lax.fori_loop(1, nch, fill, 0)


def _taps_at(pad_ref, ci, offsets, ch):
    windows = {}
    for b in sorted({o % 8 for o in offsets}):
        top = max(o for o in offsets if o % 8 == b)
        windows[b] = pad_ref[ci, b:top + ch, :]
    return [windows[o % 8][o - o % 8:o - o % 8 + ch, :] for o in offsets]


def _taps_front(pad_ref, ci, kw, ch, halo):
    return _taps_at(pad_ref, ci, [halo - (kw - 1 - k) for k in range(kw)], ch)


def ffn_mid_fwd(p, dw, dwb, name, tc=256):
    b, s, f2 = p.shape
    f = f2 // 2
    kw = dw.shape[0]
    tc = min(tc, f)
    nf = f // tc
    ch = min(CONV_CHUNK, s)
    nch = s // ch
    halo = FFN_HALO

    def body(pa_ref, pg_ref, wa_ref, wg_ref, ba_ref, bg_ref, o_ref, apad_ref, gpad_ref):
        _fill_front_halo(pa_ref, apad_ref, nch, ch, halo)
        _fill_front_halo(pg_ref, gpad_ref, nch, ch, halo)

        def chunk(ci, carry):
            r0 = pl.multiple_of(ci * ch, ch)
            ca = ba_ref[...]
            cg = bg_ref[...]
            taps = zip(_taps_front(apad_ref, ci, kw, ch, halo), _taps_front(gpad_ref, ci, kw, ch, halo))
            for k, (ta, tg) in enumerate(taps):
                ca = ca + wa_ref[k:k + 1, :] * ta
                cg = cg + wg_ref[k:k + 1, :] * tg
            o_ref[pl.ds(r0, ch), :] = ((cg * _sigmoid(cg)) * ca).astype(BF16)
            return carry

        lax.fori_loop(0, nch, chunk, 0)

    blk = lambda off: pl.BlockSpec((None, s, tc), lambda bi, i: (bi, 0, i + off))
    wsp = lambda off: pl.BlockSpec((kw, tc), lambda bi, i: (0, i + off))
    bsp = lambda off: pl.BlockSpec((1, tc), lambda bi, i: (0, i + off))
    return pl.pallas_call(
        body, name=name, grid=(b, nf),
        in_specs=[blk(0), blk(nf), wsp(0), wsp(nf), bsp(0), bsp(nf)],
        out_specs=pl.BlockSpec((None, s, tc), lambda bi, i: (bi, 0, i)),
        out_shape=jax.ShapeDtypeStruct((b, s, f), BF16),
        scratch_shapes=[pltpu.VMEM((nch, ch + halo, tc), F32)] * 2,
        compiler_params=_cp("parallel", "parallel"),
    )(p, p, dw, dw, dwb, dwb)


def ffn_mid_bwd(p, dw, dwb, ds, name, tc=256):
    b, s, f2 = p.shape
    f = f2 // 2
    kw = dw.shape[0]
    tc = min(tc, f)
    nf = f // tc
    ch = min(CONV_CHUNK, s)
    nch = s // ch
    halo = FFN_HALO

    def sum8(v):
        return jnp.sum(v.reshape(ch // 8, 8, tc), axis=0)

    def body(pa_ref, pg_ref, wa_ref, wg_ref, ba_ref, bg_ref, ds_ref, dpa_ref, dpg_ref, ddwa_ref, ddwg_ref, dba_ref, dbg_ref,
             apad_ref, gpad_ref, dca_ref, dcg_ref):
        bi = pl.program_id(1)

        @pl.when(bi == 0)
        def _():
            ddwa_ref[...] = jnp.zeros(ddwa_ref.shape, F32)
            ddwg_ref[...] = jnp.zeros(ddwg_ref.shape, F32)
            dba_ref[...] = jnp.zeros(dba_ref.shape, F32)
            dbg_ref[...] = jnp.zeros(dbg_ref.shape, F32)

        _fill_front_halo(pa_ref, apad_ref, nch, ch, halo)
        _fill_front_halo(pg_ref, gpad_ref, nch, ch, halo)
        dca_ref[nch - 1, ch:ch + halo, :] = jnp.zeros((halo, tc), F32)
        dcg_ref[nch - 1, ch:ch + halo, :] = jnp.zeros((halo, tc), F32)

        def grads(ci, carry):
            acc_a, acc_g, sb_a, sb_g = carry
            r0 = pl.multiple_of(ci * ch, ch)
            taps_a = _taps_front(apad_ref, ci, kw, ch, halo)
            taps_g = _taps_front(gpad_ref, ci, kw, ch, halo)
            ca = ba_ref[...]
            cg = bg_ref[...]
            for k in range(kw):
                ca = ca + wa_ref[k:k + 1, :] * taps_a[k]
                cg = cg + wg_ref[k:k + 1, :] * taps_g[k]
            sg = _sigmoid(cg)
            dsv = ds_ref[pl.ds(r0, ch), :].astype(F32)
            dca = dsv * (cg * sg)
            dcg = dsv * ca * (sg * (1.0 + cg * (1.0 - sg)))
            dca_ref[ci, 0:ch, :] = dca
            dcg_ref[ci, 0:ch, :] = dcg

            prev = jnp.maximum(ci - 1, 0)

            @pl.when(ci > 0)
            def _():
                dca_ref[prev, ch:ch + halo, :] = dca[0:halo, :]
                dcg_ref[prev, ch:ch + halo, :] = dcg[0:halo, :]

            acc_a = tuple(acc_a[k] + sum8(dca * taps_a[k]) for k in range(kw))
            acc_g = tuple(acc_g[k] + sum8(dcg * taps_g[k]) for k in range(kw))
            return acc_a, acc_g, sb_a + sum8(dca), sb_g + sum8(dcg)

        z8 = jnp.zeros((8, tc), F32)
        acc_a, acc_g, sb_a, sb_g = lax.fori_loop(0, nch, grads, ((z8,) * kw, (z8,) * kw, z8, z8))
        for k in range(kw):
            ddwa_ref[k] += acc_a[k]
            ddwg_ref[k] += acc_g[k]
        dba_ref[...] += jnp.sum(sb_a, axis=0, keepdims=True)
        dbg_ref[...] += jnp.sum(sb_g, axis=0, keepdims=True)

        def back(ci, carry):
            r0 = pl.multiple_of(ci * ch, ch)
            da = wa_ref[kw - 1:kw, :] * dca_ref[ci, 0:ch, :]
            dg = wg_ref[kw - 1:kw, :] * dcg_ref[ci, 0:ch, :]
            for j in range(1, kw):
                da = da + wa_ref[kw - 1 - j:kw - j, :] * dca_ref[ci, j:j + ch, :]
                dg = dg + wg_ref[kw - 1 - j:kw - j, :] * dcg_ref[ci, j:j + ch, :]
            dpa_ref[pl.ds(r0, ch), :] = da.astype(BF16)
            dpg_ref[pl.ds(r0, ch), :] = dg.astype(BF16)
            return carry

        lax.fori_loop(0, nch, back, 0)

    blk = lambda off: pl.BlockSpec((None, s, tc), lambda i, bi: (bi, 0, i + off))
    wsp = lambda off: pl.BlockSpec((kw, tc), lambda i, bi: (0, i + off))
    bsp = lambda off: pl.BlockSpec((1, tc), lambda i, bi: (0, i + off))
    acc3 = pl.BlockSpec((kw, 8, tc), lambda i, bi: (0, 0, i))
    vec = pl.BlockSpec((1, tc), lambda i, bi: (0, i))
    return pl.pallas_call(
        body, name=name, grid=(nf, b),
        in_specs=[blk(0), blk(nf), wsp(0), wsp(nf), bsp(0), bsp(nf), blk(0)],
        out_specs=[blk(0), blk(0), acc3, acc3, vec, vec],
        out_shape=[jax.ShapeDtypeStruct((b, s, f), BF16), jax.ShapeDtypeStruct((b, s, f), BF16),
                   jax.ShapeDtypeStruct((kw, 8, f), F32), jax.ShapeDtypeStruct((kw, 8, f), F32),
                   jax.ShapeDtypeStruct((1, f), F32), jax.ShapeDtypeStruct((1, f), F32)],
        scratch_shapes=[pltpu.VMEM((nch, ch + halo, tc), F32)] * 4,
        compiler_params=_cp("parallel", "arbitrary"),
    )(p, p, dw, dw, dwb, dwb, ds)


def _tile_rows(r, n, dil):
    start = r + n * BLK * dil
    return pl.ds(start, BLK, stride=dil) if dil > 1 else pl.ds(start, BLK)


def _band_masks():
    qi = lax.broadcasted_iota(jnp.int32, (BLK, 2 * BLK), 0)
    kk = lax.broadcasted_iota(jnp.int32, (BLK, 2 * BLK), 1)
    both = jnp.logical_or(jnp.logical_and(kk < BLK, kk >= qi), jnp.logical_and(kk >= BLK, kk - BLK <= qi))
    return both, kk[:, :BLK] <= qi[:, :BLK]


def attn_fwd(q, kv, g, dil, hw, name):
    _, b, s, _ = q.shape
    nh = hw // HEAD_DIM
    nblk = s // dil // BLK
    scale = 1.0 / math.sqrt(HEAD_DIM)

    def body(q_ref, k_ref, v_ref, o_ref, lse_ref):
        h = pl.program_id(1)
        mask2, mask1 = _band_masks()
        mine = lax.broadcasted_iota(jnp.int32, (BLK, LANES), 1) == h

        @pl.when(h == 0)
        def _():
            lse_ref[...] = jnp.zeros(lse_ref.shape, F32)

        for r in range(dil):
            kp = vp = None
            for n in range(nblk):
                rs = _tile_rows(r, n, dil)
                qt = q_ref[rs, :].astype(BF16)
                kc = k_ref[rs, :].astype(BF16)
                vc = v_ref[rs, :].astype(BF16)
                if n == 0:
                    kcat, vcat, mask = kc, vc, mask1
                else:
                    kcat, vcat, mask = jnp.concatenate([kp, kc], axis=0), jnp.concatenate([vp, vc], axis=0), mask2
                sc = jnp.where(mask, _dot_nt(qt, kcat) * scale, NEG_INF)
                m = jnp.max(sc, axis=-1, keepdims=True)
                p = jnp.exp(sc - m)
                den = jnp.sum(p, axis=-1, keepdims=True)
                o_ref[rs, :] = _dot(p.astype(BF16), vcat) / den
                lse_ref[rs, :] = jnp.where(mine, m + jnp.log(den), lse_ref[rs, :])
                kp, vp = kc, vc

    col = lambda base: pl.BlockSpec((None, s, HEAD_DIM), lambda bi, h: (bi, 0, base + h))
    head = lambda base: pl.BlockSpec((None, None, s, HEAD_DIM), lambda bi, h: (base + h, bi, 0, 0))
    return pl.pallas_call(
        body, name=name, grid=(b, nh),
        in_specs=[head(g * nh), head(g * nh), head((N_GROUPS + g) * nh)],
        out_specs=[head(0), pl.BlockSpec((None, s, LANES), lambda bi, h: (bi, 0, 0))],
        out_shape=[jax.ShapeDtypeStruct((nh, b, s, HEAD_DIM), F32), jax.ShapeDtypeStruct((b, s, LANES), F32)],
        compiler_params=_cp("parallel", "arbitrary"),
    )(q, kv, kv)


def attn_merge(outs, lses, name, tm=512):
    nh, t, _ = outs[0].shape
    hw = nh * HEAD_DIM
    tm = _row_tile(t, tm)
    ng = len(outs)

    def body(*refs):
        o_refs, l_refs = refs[:ng], refs[ng:2 * ng]
        m_ref, lj_ref = refs[2 * ng:]
        ls = [l_refs[g][...] for g in range(ng)]
        mx = ls[0]
        for g in range(1, ng):
            mx = jnp.maximum(mx, ls[g])
        es = [jnp.exp(l - mx) for l in ls]
        tot = es[0]
        for g in range(1, ng):
            tot = tot + es[g]
        ws = [e / tot for e in es]
        lj_ref[...] = mx + jnp.log(tot)
        for h in range(nh):
            sl = slice(h * HEAD_DIM, (h + 1) * HEAD_DIM)
            acc = ws[0][:, h:h + 1] * o_refs[0][h]
            for g in range(1, ng):
                acc = acc + ws[g][:, h:h + 1] * o_refs[g][h]
            m_ref[:, sl] = acc.astype(BF16)

    row = pl.BlockSpec((tm, hw), lambda i: (i, 0))
    st = pl.BlockSpec((tm, LANES), lambda i: (i, 0))
    heads = pl.BlockSpec((nh, tm, HEAD_DIM), lambda i: (0, i, 0))
    return pl.pallas_call(
        body, name=name, grid=(t // tm,), in_specs=[heads] * ng + [st] * ng, out_specs=[row, st],
        out_shape=[jax.ShapeDtypeStruct((t, hw), BF16), jax.ShapeDtypeStruct((t, LANES), F32)],
        compiler_params=_cp("parallel"),
    )(*outs, *lses)


def attn_bwd_prep(dmerged, merged, name, tm=512):
    t, hw = merged.shape
    nh = hw // HEAD_DIM
    tm = _row_tile(t, tm)

    def body(d_ref, m_ref, o_ref):
        lane = lax.broadcasted_iota(jnp.int32, (tm, LANES), 1)
        acc = jnp.zeros((tm, LANES), F32)
        for h in range(nh):
            sl = slice(h * HEAD_DIM, (h + 1) * HEAD_DIM)
            dsum = jnp.sum(d_ref[:, sl] * m_ref[:, sl].astype(F32), axis=-1, keepdims=True)
            acc = jnp.where(lane == h, dsum, acc)
        o_ref[...] = acc

    row = pl.BlockSpec((tm, hw), lambda i: (i, 0))
    return pl.pallas_call(
        body, name=name, grid=(t // tm,), in_specs=[row, row], out_specs=pl.BlockSpec((tm, LANES), lambda i: (i, 0)),
        out_shape=jax.ShapeDtypeStruct((t, LANES), F32), compiler_params=_cp("parallel"),
    )(dmerged, merged)


def attn_bwd(q, kv, g, dil, do, lsej, dm, dq_buf, dk_buf, dv_buf, accumulate, hw, name):
    _, b, s, _ = q.shape
    nh = hw // HEAD_DIM
    nblk = s // dil // BLK
    scale = 1.0 / math.sqrt(HEAD_DIM)
    assert dk_buf is not None or not accumulate
    kv_at = 6 + (dq_buf is not None)

    def body(*refs):
        q_ref, k_ref, v_ref, do_ref, lj_ref, dm_ref = refs[:6]
        dq_ref, dk_ref, dv_ref = refs[-3:]
        dki_ref, dvi_ref = (refs[kv_at], refs[kv_at + 1]) if accumulate else (None, None)
        mask2, mask1 = _band_masks()
        mine = lax.broadcasted_iota(jnp.int32, (BLK, LANES), 1) == pl.program_id(1)

        def my_lane(v):
            return jnp.sum(jnp.where(mine, v, 0.0), axis=-1, keepdims=True)

        def put(rs, dk, dv):
            if accumulate:
                dk = dk + dki_ref[rs, :]
                dv = dv + dvi_ref[rs, :]
            dk_ref[rs, :] = dk
            dv_ref[rs, :] = dv

        for r in range(dil):
            kp = vp = hold_k = hold_v = rs_prev = None
            for n in range(nblk):
                rs = _tile_rows(r, n, dil)
                qt = q_ref[rs, :].astype(BF16)
                kc = k_ref[rs, :].astype(BF16)
                vc = v_ref[rs, :].astype(BF16)
                dot = do_ref[rs, :].astype(BF16)
                lm = my_lane(lj_ref[rs, :])
                dmm = my_lane(dm_ref[rs, :])
                if n == 0:
                    kcat, vcat, mask = kc, vc, mask1
                else:
                    kcat, vcat, mask = jnp.concatenate([kp, kc], axis=0), jnp.concatenate([vp, vc], axis=0), mask2
                p = jnp.exp(jnp.where(mask, _dot_nt(qt, kcat) * scale, NEG_INF) - lm)
                ds = (p * (_dot_nt(dot, vcat) - dmm)).astype(BF16)
                dq_ref[rs, :] = _dot(ds, kcat) * scale
                dkc = _dot_tn(ds, qt) * scale
                dvc = _dot_tn(p.astype(BF16), dot)
                if n > 0:
                    put(rs_prev, hold_k + dkc[:BLK, :], hold_v + dvc[:BLK, :])
                    dkc, dvc = dkc[BLK:, :], dvc[BLK:, :]
                hold_k, hold_v, kp, vp, rs_prev = dkc, dvc, kc, vc, rs
            put(rs_prev, hold_k, hold_v)

    col = lambda base: pl.BlockSpec((None, s, HEAD_DIM), lambda bi, h: (bi, 0, base + h))
    any_spec = pl.BlockSpec(memory_space=pl.ANY)
    stat = pl.BlockSpec((None, s, LANES), lambda bi, h: (bi, 0, 0))
    head = lambda base: pl.BlockSpec((None, None, s, HEAD_DIM), lambda bi, h: (base + h, bi, 0, 0))
    in_specs = [head(g * nh), head(g * nh), head((N_GROUPS + g) * nh), col(0), stat, stat]
    args = [q, kv, kv, do, lsej, dm]
    aliases = {}
    if dq_buf is not None:
        in_specs.append(any_spec)
        args.append(dq_buf)
        aliases[6] = 0
    if dk_buf is not None:
        in_specs += [col(g * nh) if accumulate else any_spec] * 2
        args += [dk_buf, dv_buf]
        aliases.update({kv_at: 1, kv_at + 1: 2})
    shape = jax.ShapeDtypeStruct((b, s, N_GROUPS * hw), F32)
    return pl.pallas_call(
        body, name=name, grid=(b, nh), in_specs=in_specs, out_specs=[col(g * nh)] * 3, out_shape=[shape] * 3,
        input_output_aliases=aliases, compiler_params=_cp("parallel", "parallel"),
    )(*args)


def sum_parts(g, recv, me, name, tm=512, layer=0, n_layers=1, out_buf=None):
    _, rows, c = g.shape
    n = recv.shape[0]
    tm = _row_tile(rows, tm)
    steps = rows // tm

    def body(me_ref, g_ref, r_ref, *rest):
        acc = g_ref[...].astype(F32)
        for j in range(n):
            acc = acc + r_ref[j].astype(F32)
        rest[-1][...] = acc

    in_specs = [pl.BlockSpec((None, tm, c), lambda i, me_ref: (me_ref[0], i, 0)),
                pl.BlockSpec((n, tm, c), lambda i, me_ref: (0, i, 0))]
    args = [me, g, recv]
    aliases = {}
    if out_buf is not None:
        in_specs.append(pl.BlockSpec(memory_space=pl.ANY))
        args.append(out_buf)
        aliases = {3: 0}
    return pl.pallas_call(
        body, name=name,
        grid_spec=pltpu.PrefetchScalarGridSpec(
            num_scalar_prefetch=1, grid=(steps,), in_specs=in_specs,
            out_specs=pl.BlockSpec((tm, c), lambda i, me_ref: (layer * steps + i, 0))),
        out_shape=jax.ShapeDtypeStruct((n_layers * rows, c), F32), input_output_aliases=aliases,
        compiler_params=_cp("parallel"),
    )(*args)


def adamw(w, m, v, g_parts, name, tm=256):
    rows, c = w.shape
    if c <= 8 * LANES:
        tm = 2 * tm
    tm = _row_tile(rows, tm)
    npart = len(g_parts)

    def body(*refs):
        w_ref, m_ref, v_ref = refs[:3]
        g_refs = refs[3:3 + npart]
        go_ref, d_ref, mo_ref, vo_ref = refs[3 + npart:]
        g = g_refs[0][...]
        for k in range(1, npart):
            g = g + g_refs[k][...]
        mn = ADAM_B1 * m_ref[...] + (1.0 - ADAM_B1) * g
        vn = ADAM_B2 * v_ref[...] + (1.0 - ADAM_B2) * (g * g)
        m_hat = mn / (1.0 - ADAM_B1 ** ADAM_STEP)
        v_hat = vn / (1.0 - ADAM_B2 ** ADAM_STEP)
        go_ref[...] = g
        d_ref[...] = -ADAM_LR * (m_hat / (jnp.sqrt(v_hat) + ADAM_EPS) + ADAM_WD * w_ref[...])
        mo_ref[...] = mn
        vo_ref[...] = vn

    row = pl.BlockSpec((tm, c), lambda i: (i, 0))
    return pl.pallas_call(
        body, name=name, grid=(rows // tm,), in_specs=[row] * (3 + npart), out_specs=[row] * 4,
        out_shape=[jax.ShapeDtypeStruct((rows, c), F32)] * 4, compiler_params=_cp("parallel"),
    )(w, m, v, *g_parts)


def _place():
    return lax.axis_index("x"), lax.axis_index("y"), lax.axis_index("c")


def _other_chips(x, y, c):
    return [(1 - x, y, c), (x, 1 - y, c), (1 - x, 1 - y, c)]


def _chip_of(px, py):
    return 2 * px + py


HBM_SPEC = pl.BlockSpec(memory_space=pltpu.HBM)
SEM_SPEC = pl.BlockSpec(memory_space=pltpu.SEMAPHORE)
ANY_SPEC = pl.BlockSpec(memory_space=pl.ANY)
DATAFLOW = pltpu.SideEffectType.DATAFLOW_SIDE_EFFECTING
N_PEER_CHIPS = N_CHIPS - 1


def _hbm(a):
    return pltpu.with_memory_space_constraint(a, pltpu.HBM)


def _hbm_like(arrays):
    return [pltpu.HBM(a.shape, a.dtype) for a in arrays]


def cast_place(w, layer, me, out_dtype, name, tm=512, nslots=N_CHIPS, dep=None):
    rows, c = w.shape[-2:]
    tm = _row_tile(rows, tm)

    def body(me_ref, w_ref, *rest):
        rest[-1][...] = w_ref[...].astype(out_dtype)

    if layer is None:
        in_specs = [pl.BlockSpec((tm, c), lambda i, me_ref: (i, 0))]
    else:
        in_specs = [pl.BlockSpec((None, tm, c), lambda i, me_ref: (layer, i, 0))]
    args = [me, w]
    if dep is not None:
        in_specs.append(pl.BlockSpec(DEP_SPEC_SHAPE, lambda i, me_ref: (0, 0)))
        args.append(dep)
    return pl.pallas_call(
        body, name=name,
        grid_spec=pltpu.PrefetchScalarGridSpec(
            num_scalar_prefetch=1, grid=(rows // tm,), in_specs=in_specs,
            out_specs=pl.BlockSpec((None, tm, c), lambda i, me_ref: (me_ref[0], i, 0))),
        out_shape=jax.ShapeDtypeStruct((nslots, rows, c), out_dtype), compiler_params=_cp("parallel"),
    )(*args)


def gather_start(lands, chunk_sizes, name="gather_start"):
    n = len(lands)
    nch = len(chunk_sizes)
    assert sum(chunk_sizes) == n

    def body(*refs):
        land_refs = refs[:n]
        outs = refs[n:]
        send_sems, recv_sems = outs[:nch], outs[nch:2 * nch]
        token = outs[-1]
        x, y, c = _place()
        me = _chip_of(x, y)
        peers = _other_chips(x, y, c)
        k = 0
        for ck, size in enumerate(chunk_sizes):
            for pos in range(size):
                for r, peer in enumerate(peers):
                    pltpu.make_async_remote_copy(
                        src_ref=land_refs[k].at[me], dst_ref=land_refs[k].at[me],
                        send_sem=send_sems[ck].at[N_PEER_CHIPS * pos + r], recv_sem=recv_sems[ck].at[N_PEER_CHIPS * pos + r],
                        device_id=peer, device_id_type=MESH).start()
                k += 1
        token[...] = jnp.zeros(token.shape, F32)

    sems = [pltpu.SemaphoreType.DMA((N_PEER_CHIPS * s,)) for s in chunk_sizes]
    res = pl.pallas_call(
        body, name=name,
        out_shape=(*sems, *sems, *_hbm_like(lands), jax.ShapeDtypeStruct(DEP_SPEC_SHAPE, F32)),
        in_specs=[HBM_SPEC] * n,
        out_specs=(*[SEM_SPEC] * (2 * nch), *[HBM_SPEC] * n, pl.BlockSpec(memory_space=pltpu.VMEM)),
        input_output_aliases={k: 2 * nch + k for k in range(n)},
        compiler_params=pltpu.CompilerParams(has_side_effects=DATAFLOW),
    )(*[_hbm(a) for a in lands])
    return res[:nch], res[nch:2 * nch], res[2 * nch:2 * nch + n], res[-1]


def gather_wait(send_sem, recv_sem, lands, after, name):
    n = len(lands)

    def body(*refs):
        land_refs = refs[:n]
        ssem, rsem = refs[n], refs[n + 1]
        x, y, c = _place()
        me = _chip_of(x, y)
        for pos in range(n):
            for r, peer in enumerate(_other_chips(x, y, c)):
                cp = pltpu.make_async_remote_copy(
                    src_ref=land_refs[pos].at[me], dst_ref=land_refs[pos].at[_chip_of(peer[0], peer[1])],
                    send_sem=ssem.at[N_PEER_CHIPS * pos + r], recv_sem=rsem.at[N_PEER_CHIPS * pos + r],
                    device_id=peer, device_id_type=MESH)
                cp.wait_send()
                cp.wait_recv()

    return pl.pallas_call(
        body, name=name, out_shape=tuple(_hbm_like(lands)),
        in_specs=[*[HBM_SPEC] * n, SEM_SPEC, SEM_SPEC, ANY_SPEC], out_specs=[HBM_SPEC] * n,
        input_output_aliases={k: k for k in range(n)},
        compiler_params=pltpu.CompilerParams(has_side_effects=DATAFLOW),
    )(*lands, send_sem, recv_sem, after)


def scatter_start(grads, name):
    n = len(grads)
    recvs = [lax.empty((N_PEER_CHIPS, *g.shape[1:]), g.dtype) for g in grads]

    def body(*refs):
        g_refs, r_refs = refs[:n], refs[n:2 * n]
        send_sems, recv_sems = refs[2 * n], refs[2 * n + 1]
        token = refs[-1]
        x, y, c = _place()
        for k in range(n):
            for r, peer in enumerate(_other_chips(x, y, c)):
                pltpu.make_async_remote_copy(
                    src_ref=g_refs[k].at[_chip_of(peer[0], peer[1])], dst_ref=r_refs[k].at[r],
                    send_sem=send_sems.at[N_PEER_CHIPS * k + r], recv_sem=recv_sems.at[N_PEER_CHIPS * k + r],
                    device_id=peer, device_id_type=MESH).start()
        token[...] = jnp.zeros(token.shape, F32)

    sem = pltpu.SemaphoreType.DMA((N_PEER_CHIPS * n,))
    res = pl.pallas_call(
        body, name=name,
        out_shape=(sem, sem, *_hbm_like(grads), *_hbm_like(recvs), jax.ShapeDtypeStruct(DEP_SPEC_SHAPE, F32)),
        in_specs=[HBM_SPEC] * (2 * n),
        out_specs=(SEM_SPEC, SEM_SPEC, *[HBM_SPEC] * (2 * n), pl.BlockSpec(memory_space=pltpu.VMEM)),
        input_output_aliases={k: 2 + k for k in range(2 * n)},
        compiler_params=pltpu.CompilerParams(has_side_effects=DATAFLOW),
    )(*[_hbm(a) for a in grads], *[_hbm(a) for a in recvs])
    return res[0], res[1], res[2:2 + n], res[2 + n:2 + 2 * n], res[-1]


def scatter_wait(send_sem, recv_sem, grads, recvs, after, name):
    n = len(grads)

    def body(*refs):
        g_refs, r_refs = refs[:n], refs[n:2 * n]
        ssem, rsem = refs[2 * n], refs[2 * n + 1]
        x, y, c = _place()
        for k in range(n):
            for r, peer in enumerate(_other_chips(x, y, c)):
                cp = pltpu.make_async_remote_copy(
                    src_ref=g_refs[k].at[_chip_of(peer[0], peer[1])], dst_ref=r_refs[k].at[r],
                    send_sem=ssem.at[N_PEER_CHIPS * k + r], recv_sem=rsem.at[N_PEER_CHIPS * k + r],
                    device_id=peer, device_id_type=MESH)
                cp.wait_send()
                cp.wait_recv()

    res = pl.pallas_call(
        body, name=name, out_shape=(*_hbm_like(grads), *_hbm_like(recvs)),
        in_specs=[*[HBM_SPEC] * (2 * n), SEM_SPEC, SEM_SPEC, ANY_SPEC], out_specs=[HBM_SPEC] * (2 * n),
        input_output_aliases={k: k for k in range(2 * n)},
        compiler_params=pltpu.CompilerParams(has_side_effects=DATAFLOW),
    )(*grads, *recvs, send_sem, recv_sem, after)
    return res[:n], res[n:]


def swap_start(parts, name):
    n = len(parts)
    lands = [lax.empty(p.shape, p.dtype) for p in parts]

    def body(*refs):
        p_refs, l_refs = refs[:n], refs[n:2 * n]
        sems = refs[2 * n:4 * n]
        token = refs[-1]
        x, y, c = _place()
        for k in range(n):
            pltpu.make_async_remote_copy(
                src_ref=p_refs[k], dst_ref=l_refs[k], send_sem=sems[k], recv_sem=sems[n + k],
                device_id=(x, y, 1 - c), device_id_type=MESH).start()
        token[...] = jnp.zeros(token.shape, F32)

    sem = pltpu.SemaphoreType.DMA(())
    res = pl.pallas_call(
        body, name=name,
        out_shape=(*[sem] * (2 * n), *_hbm_like(parts), *_hbm_like(lands), jax.ShapeDtypeStruct(DEP_SPEC_SHAPE, F32)),
        in_specs=[HBM_SPEC] * (2 * n),
        out_specs=(*[SEM_SPEC] * (2 * n), *[HBM_SPEC] * (2 * n), pl.BlockSpec(memory_space=pltpu.VMEM)),
        input_output_aliases={k: 2 * n + k for k in range(2 * n)},
        compiler_params=pltpu.CompilerParams(has_side_effects=DATAFLOW),
    )(*[_hbm(a) for a in parts], *[_hbm(a) for a in lands])
    return res[:n], res[n:2 * n], res[2 * n:3 * n], res[3 * n:4 * n], res[-1]


def swap_wait(send_sem, recv_sem, part, land, after, name):
    def body(p_ref, l_ref, ssem, rsem, after_ref, p_out, l_out):
        x, y, c = _place()
        cp = pltpu.make_async_remote_copy(src_ref=p_ref, dst_ref=l_ref, send_sem=ssem, recv_sem=rsem,
                                          device_id=(x, y, 1 - c), device_id_type=MESH)
        cp.wait_send()
        cp.wait_recv()

    return pl.pallas_call(
        body, name=name, out_shape=tuple(_hbm_like([part, land])),
        in_specs=[HBM_SPEC, HBM_SPEC, SEM_SPEC, SEM_SPEC, ANY_SPEC], out_specs=[HBM_SPEC, HBM_SPEC],
        input_output_aliases={0: 0, 1: 1},
        compiler_params=pltpu.CompilerParams(has_side_effects=DATAFLOW),
    )(part, land, send_sem, recv_sem, after)


def _xor_peer(x, y, c, k):
    px, py, pc = x ^ ((k >> 2) & 1), y ^ ((k >> 1) & 1), c ^ (k & 1)
    return (px, py, pc), 4 * px + 2 * py + pc


def small_start(land, name="small_start"):
    def body(l_ref, ssem, rsem, l_out, token):
        x, y, c = _place()
        me = 4 * x + 2 * y + c
        for k in range(1, N_DEV):
            peer, _ = _xor_peer(x, y, c, k)
            pltpu.make_async_remote_copy(
                src_ref=l_ref.at[me], dst_ref=l_ref.at[me], send_sem=ssem.at[k - 1], recv_sem=rsem.at[k - 1],
                device_id=peer, device_id_type=MESH).start()
        token[...] = jnp.zeros(token.shape, F32)

    sem = pltpu.SemaphoreType.DMA((N_DEV - 1,))
    return pl.pallas_call(
        body, name=name,
        out_shape=(sem, sem, pltpu.HBM(land.shape, land.dtype), jax.ShapeDtypeStruct(DEP_SPEC_SHAPE, F32)),
        in_specs=[HBM_SPEC], out_specs=(SEM_SPEC, SEM_SPEC, HBM_SPEC, pl.BlockSpec(memory_space=pltpu.VMEM)),
        input_output_aliases={0: 2}, compiler_params=pltpu.CompilerParams(has_side_effects=DATAFLOW),
    )(_hbm(land))


def small_wait(send_sem, recv_sem, land, after, name="small_wait"):
    def body(l_ref, ssem, rsem, after_ref, l_out):
        x, y, c = _place()
        me = 4 * x + 2 * y + c
        for k in range(1, N_DEV):
            peer, slot = _xor_peer(x, y, c, k)
            cp = pltpu.make_async_remote_copy(
                src_ref=l_ref.at[me], dst_ref=l_ref.at[slot], send_sem=ssem.at[k - 1], recv_sem=rsem.at[k - 1],
                device_id=peer, device_id_type=MESH)
            cp.wait_send()
            cp.wait_recv()

    return pl.pallas_call(
        body, name=name, out_shape=pltpu.HBM(land.shape, land.dtype),
        in_specs=[HBM_SPEC, SEM_SPEC, SEM_SPEC, ANY_SPEC], out_specs=HBM_SPEC, input_output_aliases={0: 0},
        compiler_params=pltpu.CompilerParams(has_side_effects=DATAFLOW),
    )(land, send_sem, recv_sem, after)


def sum_slots(land, name="sum_slots", tm=256):
    n, rows, c = land.shape
    tm = _row_tile(rows, tm)

    def body(l_ref, o_ref):
        acc = l_ref[0]
        for j in range(1, n):
            acc = acc + l_ref[j]
        o_ref[...] = acc

    return pl.pallas_call(
        body, name=name, grid=(rows // tm,), in_specs=[pl.BlockSpec((n, tm, c), lambda i: (0, i, 0))],
        out_specs=pl.BlockSpec((tm, c), lambda i: (i, 0)), out_shape=jax.ShapeDtypeStruct((rows, c), F32),
        compiler_params=_cp("parallel"),
    )(land)


PACK_ROW_TILE = 256


def _pack(arrays):
    flat = jnp.concatenate([a.reshape(-1).astype(F32) for a in arrays])
    n = flat.shape[0]
    rows = -(-n // LANES)
    rows = -(-rows // PACK_ROW_TILE) * PACK_ROW_TILE
    return jnp.pad(flat, (0, rows * LANES - n)).reshape(rows, LANES)


def _unpack(packed, shapes, lead=()):
    flat = packed.reshape(*lead, -1)
    out, off = [], 0
    for shp in shapes:
        n = math.prod(shp)
        out.append(flat[..., off:off + n].reshape(*lead, *shp))
        off += n
    return out


def _row(vec):
    return vec.reshape(1, -1)


def kernel(x, mix_pre_g, mix_post_g, ffn_pre_g, ffn_post_g, cm_w_in, cm_b_in, cm_dw, cm_dw_b, cm_ln_g, cm_ln_b, cm_w_out, cm_b_out, kv_norm_g, w_kv, w_q, w_o, ffn_w_in, ffn_dw, ffn_dw_b, ffn_w_out, loss_target, m_mix_pre_g, m_mix_post_g, m_ffn_pre_g, m_ffn_post_g, m_cm_w_in, m_cm_b_in, m_cm_dw, m_cm_dw_b, m_cm_ln_g, m_cm_ln_b, m_cm_w_out, m_cm_b_out, m_kv_norm_g, m_w_kv, m_w_q, m_w_o, m_ffn_w_in, m_ffn_dw, m_ffn_dw_b, m_ffn_w_out, v_mix_pre_g, v_mix_post_g, v_ffn_pre_g, v_ffn_post_g, v_cm_w_in, v_cm_b_in, v_cm_dw, v_cm_dw_b, v_cm_ln_g, v_cm_ln_b, v_cm_w_out, v_cm_b_out, v_kv_norm_g, v_w_kv, v_w_q, v_w_o, v_ffn_w_in, v_ffn_dw, v_ffn_dw_b, v_ffn_w_out):
    names = ["mix_pre_g", "mix_post_g", "ffn_pre_g", "ffn_post_g", "cm_w_in", "cm_b_in", "cm_dw", "cm_dw_b", "cm_ln_g",
             "cm_ln_b", "cm_w_out", "cm_b_out", "kv_norm_g", "w_kv", "w_q", "w_o", "ffn_w_in", "ffn_dw", "ffn_dw_b",
             "ffn_w_out"]
    w_in = dict(zip(names, [mix_pre_g, mix_post_g, ffn_pre_g, ffn_post_g, cm_w_in, cm_b_in, cm_dw, cm_dw_b, cm_ln_g,
                            cm_ln_b, cm_w_out, cm_b_out, kv_norm_g, w_kv, w_q, w_o, ffn_w_in, ffn_dw, ffn_dw_b, ffn_w_out]))
    m_in = dict(zip(names, [m_mix_pre_g, m_mix_post_g, m_ffn_pre_g, m_ffn_post_g, m_cm_w_in, m_cm_b_in, m_cm_dw, m_cm_dw_b,
                            m_cm_ln_g, m_cm_ln_b, m_cm_w_out, m_cm_b_out, m_kv_norm_g, m_w_kv, m_w_q, m_w_o, m_ffn_w_in,
                            m_ffn_dw, m_ffn_dw_b, m_ffn_w_out]))
    v_in = dict(zip(names, [v_mix_pre_g, v_mix_post_g, v_ffn_pre_g, v_ffn_post_g, v_cm_w_in, v_cm_b_in, v_cm_dw, v_cm_dw_b,
                            v_cm_ln_g, v_cm_ln_b, v_cm_w_out, v_cm_b_out, v_kv_norm_g, v_w_kv, v_w_q, v_w_o, v_ffn_w_in,
                            v_ffn_dw, v_ffn_dw_b, v_ffn_w_out]))

    bsz, seq, d = x.shape
    t = bsz * seq
    n_b = DEPTH - N_A
    hw = w_o.shape[-1]
    qw = N_GROUPS * hw
    f2 = ffn_dw_b.shape[-1]
    f = f2 // 2
    me_chip = _chip_of(lax.axis_index("x"), lax.axis_index("y"))

    big = ["cm_w_in", "cm_w_out", "w_kv", "w_q", "w_o", "ffn_w_in", "ffn_w_out"]
    row_sharded = ("cm_w_out", "w_o", "ffn_w_out")
    small_sharded = ["cm_b_in", "cm_dw", "cm_dw_b", "cm_ln_g", "cm_ln_b", "cm_b_out", "ffn_dw"]
    small_pack = _pack([w_in[n] for n in small_sharded])
    chunks = [
        [("cm_w_in", 0), ("small", None)],
        [("cm_w_out", 0)],
        [("ffn_w_in", 0), ("ffn_w_out", 0)],
        [("cm_w_in", 1), ("cm_w_out", 1)],
        [("ffn_w_in", 1), ("ffn_w_out", 1)],
        [("w_kv", None)],
        [("w_q", 0), ("w_o", 0)],
        [("ffn_w_in", 2), ("ffn_w_out", 2)],
        [("w_q", 1), ("w_o", 1)],
        [("ffn_w_in", 3), ("ffn_w_out", 3)],
    ]
    pieces = [pc for ch in chunks for pc in ch]
    chunk_of = {pc: ck for ck, ch in enumerate(chunks) for pc in ch}

    me_arr = me_chip.astype(jnp.int32).reshape(1)

    def land_of(pc, dep=None):
        n, l = pc
        if n == "small":
            return cast_place(small_pack, None, me_arr, F32, name="place_small", dep=dep)
        return cast_place(w_in[n], l, me_arr, BF16, name=f"place_{n}_{l}", dep=dep)

    groups = [[0, 1], [2], list(range(3, len(chunks)))]
    g_send, g_recv, lands_f = {}, {}, {}

    def start_group(gi, dep, name):
        cks = groups[gi]
        pcs = [pc for ck in cks for pc in chunks[ck]]
        lands = [land_of(pc, dep if k == 0 else None) for k, pc in enumerate(pcs)]
        send, recv, lands_thru, tok = gather_start(lands, [len(chunks[ck]) for ck in cks], name=name)
        pos = 0
        for j, ck in enumerate(cks):
            g_send[ck], g_recv[ck] = send[j], recv[j]
            lands_f[ck] = lands_thru[pos:pos + len(chunks[ck])]
            pos += len(chunks[ck])
        return tok

    token_a = start_group(0, None, "gather_start_a")
    token = start_group(1, token_a, "gather_start_b")
    weights = {}

    def finish_chunk(ck, after):
        got = gather_wait(g_send[ck], g_recv[ck], lands_f[ck], after, name=f"gather_wait{ck}")
        for pc, arr in zip(chunks[ck], got):
            weights[pc] = arr.reshape(1, -1, arr.shape[-1]) if pc[0] in row_sharded else arr

    def wmat(n, l=None, after=None):
        if (n, l) not in weights:
            finish_chunk(chunk_of[(n, l)], after)
        arr = weights[(n, l)]
        return arr, arr.shape[0]

    finish_chunk(0, token)
    small_full = {}
    for n, arr4 in zip(small_sharded, _unpack(weights[("small", None)], [w_in[n].shape for n in small_sharded], lead=(N_CHIPS,))):
        shp = w_in[n].shape
        small_full[n] = jnp.moveaxis(arr4, 0, -2).reshape(*shp[:-1], N_CHIPS * shp[-1])

    x2d = x.reshape(t, d)
    saved = []
    (h1,) = resid_norm_fwd(x2d, None, None, [_row(mix_pre_g[0])], name="norm_in", dep=token)
    xcur = x2d
    kv_state = None
    for i in range(DEPTH):
        sv = {"x_in": xcur, "h1": h1}
        if i < N_A:
            z = mm_nn(h1, *wmat("cm_w_in", i, h1), 1, 0, bias=_row(small_full["cm_b_in"][i]), name=f"cm_in{i}",
                      spg=N_CHIPS)
            u2 = glu_conv_fwd(z.reshape(bsz, seq, 2 * d), small_full["cm_dw"][i], _row(small_full["cm_dw_b"][i]),
                              name=f"glu_conv{i}").reshape(t, d)
            u4 = ln_silu_fwd(u2, _row(small_full["cm_ln_g"][i]), _row(small_full["cm_ln_b"][i]), name=f"ln_silu{i}")
            y = mm_nn(u4, *wmat("cm_w_out", i, u4), 1, 0, bias=_row(small_full["cm_b_out"][i]), out_dtype=BF16,
                      name=f"cm_out{i}")
            sv.update(z=z, u2=u2, u4=u4)
        else:
            j = i - N_A
            q = mm_nn(h1, *wmat("w_q", j, h1), 1, 0, name=f"q_proj{j}", head_major=True, spg=N_CHIPS).reshape(-1, bsz, seq, HEAD_DIM)
            outs, lses = [], []
            for g, dil in enumerate(DILATIONS):
                o_g, l_g = attn_fwd(q, kv_state["kv"], g, dil, hw, name=f"attn_fwd{j}_{g}")
                outs.append(o_g.reshape(-1, t, HEAD_DIM))
                lses.append(l_g.reshape(t, LANES))
            merged, lsej = attn_merge(outs, lses, name=f"attn_merge{j}")
            y = mm_nn(merged, *wmat("w_o", j, merged), 1, 0, out_dtype=BF16, name=f"o_proj{j}")
            sv.update(q=q, merged=merged, lsej=lsej)
        x1, h2 = resid_norm_fwd(xcur, y, _row(mix_post_g[i]), [_row(ffn_pre_g[i])], name=f"resid_mix{i}")
        ffn_after = start_group(2, x1, "gather_start_c") if i == 0 else h2
        p = mm_nn(h2, *wmat("ffn_w_in", i, ffn_after), 1, 0, out_dtype=BF16, name=f"ffn_in{i}", spg=2)
        s_act = ffn_mid_fwd(p.reshape(bsz, seq, f2), small_full["ffn_dw"][i], _row(ffn_dw_b[i]), name=f"ffn_mid{i}").reshape(t, f)
        y2 = mm_nn(s_act, *wmat("ffn_w_out", i), 1, 0, out_dtype=BF16, name=f"ffn_out{i}")
        next_gains = []
        if i + 1 < DEPTH:
            next_gains.append(_row(mix_pre_g[i + 1]))
        if i == N_A - 1:
            next_gains.append(_row(kv_norm_g))
        res = resid_norm_fwd(x1, y2, _row(ffn_post_g[i]), next_gains, name=f"resid_ffn{i}")
        sv.update(y=y, x1=x1, h2=h2, p=p, s=s_act, y2=y2)
        saved.append(sv)
        xcur = res[0]
        if i + 1 < DEPTH:
            h1 = res[1]
        if i == N_A - 1:
            kvn = res[2]
            kv = mm_nn(kvn, *wmat("w_kv", None, kvn), 1, 0, name="kv_proj", head_major=True, spg=2).reshape(-1, bsz, seq, HEAD_DIM)
            kv_state = {"kv": kv, "kvn": kvn, "x_a": xcur}

    dx, loss_tile = loss_fwd_bwd(xcur, loss_target.reshape(t, d))
    loss = lax.psum(loss_tile[0, 0], ("x", "y", "c"))

    gsm = {n: [None] * w_in[n].shape[0] for n in
           ["mix_pre_g", "mix_post_g", "ffn_pre_g", "ffn_post_g", "cm_b_in", "cm_dw", "cm_dw_b", "cm_ln_g", "cm_ln_b",
            "cm_b_out", "ffn_dw", "ffn_dw_b"]}
    gbig = {}
    in_flight = []
    dep = None

    def start_scatter(pcs, tag):
        ssem, rsem, g_f, r_f, tok = scatter_start([gbig[pc] for pc in pcs], name=f"scatter_start_{tag}")
        in_flight.append((pcs, ssem, rsem, g_f, r_f))
        return tok

    dk_buf = dv_buf = None
    for i in range(DEPTH - 1, -1, -1):
        sv = saved[i]
        dy2, dg, _ = norm_bwd(sv["y2"], _row(ffn_post_g[i]), dx, out_dtype=BF16, name=f"bwd_ffn_post{i}", dep=dep)
        gsm["ffn_post_g"][i] = dg
        ds = mm_nt(dy2, *wmat("ffn_w_out", i), 1, 0, out_dtype=BF16, name=f"bwd_ffn_out_dx{i}")
        gbig[("ffn_w_out", i)] = mm_tn(sv["s"], dy2, 1, name=f"bwd_ffn_out_dw{i}").reshape(N_CHIPS, f // N_CHIPS, d)
        dpa, dpg, ddwa, ddwg, ddba, ddbg = ffn_mid_bwd(sv["p"].reshape(bsz, seq, f2), small_full["ffn_dw"][i], _row(ffn_dw_b[i]),
                                                       ds.reshape(bsz, seq, f), name=f"bwd_ffn_mid{i}")
        gsm["ffn_dw"][i] = jnp.concatenate([jnp.sum(ddwa, axis=1), jnp.sum(ddwg, axis=1)], axis=-1)
        gsm["ffn_dw_b"][i] = jnp.concatenate([ddba, ddbg], axis=-1)
        dp = [dpa.reshape(t, f), dpg.reshape(t, f)]
        dh2 = mm_nt(dp, *wmat("ffn_w_in", i), 1, 0, out_dtype=BF16, name=f"bwd_ffn_in_dx{i}", spg=2)
        gbig[("ffn_w_in", i)] = mm_tn(sv["h2"], dp, N_CHIPS, name=f"bwd_ffn_in_dw{i}", spg=2)
        dx1, dg, _ = norm_bwd(sv["x1"], _row(ffn_pre_g[i]), dh2, add=dx, name=f"bwd_ffn_pre{i}")
        gsm["ffn_pre_g"][i] = dg
        dep = start_scatter([("ffn_w_in", 0), ("ffn_w_out", 0)], "ffn0") if i == 0 else None
        dy, dg, dbias = norm_bwd(sv["y"], _row(mix_post_g[i]), dx1, out_dtype=BF16, name=f"bwd_mix_post{i}", dep=dep)
        gsm["mix_post_g"][i] = dg
        if i < N_A:
            gsm["cm_b_out"][i] = dbias
            du4 = mm_nt(dy, *wmat("cm_w_out", i), 1, 0, out_dtype=BF16, name=f"bwd_cm_out_dx{i}")
            gbig[("cm_w_out", i)] = mm_tn(sv["u4"], dy, 1, name=f"bwd_cm_out_dw{i}").reshape(N_CHIPS, d // N_CHIPS, d)
            du2, dlg, dlb = ln_silu_bwd(sv["u2"], _row(small_full["cm_ln_g"][i]), _row(small_full["cm_ln_b"][i]), du4,
                                        name=f"bwd_ln_silu{i}")
            gsm["cm_ln_g"][i], gsm["cm_ln_b"][i] = dlg, dlb
            dza, dzg, ddw, ddwb, dba, dbg = glu_conv_bwd(sv["z"].reshape(bsz, seq, 2 * d), small_full["cm_dw"][i],
                                                         du2.reshape(bsz, seq, d), name=f"bwd_glu_conv{i}")
            gsm["cm_dw"][i] = jnp.sum(ddw, axis=1)
            gsm["cm_dw_b"][i] = ddwb
            gsm["cm_b_in"][i] = jnp.concatenate([dba, dbg], axis=-1)
            dz = [dza.reshape(t, d), dzg.reshape(t, d)]
            dh1 = mm_nt(dz, *wmat("cm_w_in", i), 1, 0, out_dtype=BF16, name=f"bwd_cm_in_dx{i}", spg=2)
            gbig[("cm_w_in", i)] = mm_tn(sv["h1"], dz, N_CHIPS, name=f"bwd_cm_in_dw{i}", spg=2)
        else:
            j = i - N_A
            dmerged = mm_nt(dy, *wmat("w_o", j), 1, 0, name=f"bwd_o_proj_dx{j}")
            gbig[("w_o", j)] = mm_tn(sv["merged"], dy, 1, name=f"bwd_o_proj_dw{j}").reshape(N_CHIPS, hw // N_CHIPS, d)
            dmt = attn_bwd_prep(dmerged, sv["merged"], name=f"bwd_attn_prep{j}")
            dq_buf = None
            add_to_kv = dk_buf is not None
            for g, dil in enumerate(DILATIONS):
                dq_buf, dk_buf, dv_buf = attn_bwd(
                    sv["q"], kv_state["kv"], g, dil, dmerged.reshape(bsz, seq, hw), sv["lsej"].reshape(bsz, seq, LANES),
                    dmt.reshape(bsz, seq, LANES), dq_buf, dk_buf, dv_buf, add_to_kv, hw, name=f"attn_bwd{j}_{g}")
            dq = dq_buf.reshape(t, qw)
            dh1 = mm_nt(dq, *wmat("w_q", j), 1, 0, out_dtype=BF16, name=f"bwd_q_proj_dx{j}", tm=512, spg=N_CHIPS)
            gbig[("w_q", j)] = mm_tn(sv["h1"], dq, N_CHIPS, name=f"bwd_q_proj_dw{j}", tm=512, spg=N_CHIPS)
        dx, dg, _ = norm_bwd(sv["x_in"], _row(mix_pre_g[i]), dh1, add=dx1, name=f"bwd_mix_pre{i}")
        gsm["mix_pre_g"][i] = dg
        if i > N_A:
            dep = start_scatter([("ffn_w_in", i), ("ffn_w_out", i), ("w_q", i - N_A), ("w_o", i - N_A)], f"l{i}")
        elif 0 < i < N_A:
            dep = start_scatter([("ffn_w_in", i), ("ffn_w_out", i), ("cm_w_in", i), ("cm_w_out", i)], f"l{i}")
        elif i == 0:
            last_token = start_scatter([("cm_w_in", 0), ("cm_w_out", 0)], "cm0")
        if i == N_A:
            dkv = [dk_buf.reshape(t, qw), dv_buf.reshape(t, qw)]
            dkvn = mm_nt(dkv, *wmat("w_kv"), 1, 0, out_dtype=BF16, name="bwd_kv_proj_dx")
            gbig[("w_kv", None)] = mm_tn(kv_state["kvn"], dkv, N_CHIPS, name="bwd_kv_proj_dw")
            dx, dg_kv, _ = norm_bwd(kv_state["x_a"], _row(kv_norm_g), dkvn, add=dx, name="bwd_kv_norm")
            dep = start_scatter([("ffn_w_in", i), ("ffn_w_out", i), ("w_q", 0), ("w_o", 0), ("w_kv", None)], f"l{i}")
    grad_x = dx.reshape(bsz, seq, d)

    plane_of = {}
    outs_g, outs_d, outs_m, outs_v = {}, {}, {}, {}

    def finish_scatter(k, after):
        pcs, ssem, rsem, g_f, r_f = in_flight[k]
        g_done, r_done = scatter_wait(ssem, rsem, g_f, r_f, after, name=f"scatter_wait{k}")
        for (n, l), g_arr, r_arr in zip(pcs, g_done, r_done):
            n_layers = 1 if l is None else w_in[n].shape[0]
            plane_of[n] = sum_parts(g_arr, r_arr, me_arr, name=f"sum_chips_{n}_{l}", layer=l or 0, n_layers=n_layers,
                                    out_buf=plane_of.get(n))

    def update(group, tag, after):
        plane = [plane_of[n] for n in group]
        ssems, rsems, plane_f, land_f, _ = swap_start(plane, name=f"swap_start_{tag}")
        for k, n in enumerate(group):
            p_mine, p_other = swap_wait(ssems[k], rsems[k], plane_f[k], land_f[k], after, name=f"swap_wait_{n}")
            shp = w_in[n].shape
            flat = lambda a: a.reshape(-1, shp[-1])
            g_, d_, m_, v_ = adamw(flat(w_in[n]), flat(m_in[n]), flat(v_in[n]), [p_mine, p_other], name=f"adamw_{n}")
            outs_g[n], outs_d[n], outs_m[n], outs_v[n] = (a.reshape(shp) for a in (g_, d_, m_, v_))
            after = v_
        return after

    small_names = [n for n in names if n not in big]
    small_shapes_full = {}
    small_grads_full = []
    for n in small_names:
        if n == "kv_norm_g":
            gfull = dg_kv.reshape(-1)
        elif n in ("cm_dw", "ffn_dw"):
            gfull = jnp.stack(gsm[n], axis=0)
        else:
            gfull = jnp.stack([a.reshape(-1) for a in gsm[n]], axis=0)
        small_shapes_full[n] = gfull.shape
        small_grads_full.append(gfull)
    dev_arr = (4 * lax.axis_index("x") + 2 * lax.axis_index("y") + lax.axis_index("c")).astype(jnp.int32).reshape(1)
    small_land = cast_place(_pack(small_grads_full) + last_token[0, 0], None, dev_arr, F32, name="place_small_grads",
                            nslots=N_DEV)
    sm_send, sm_recv, small_land, small_token = small_start(small_land)

    for k in range(len(in_flight) - 1):
        finish_scatter(k, small_token)
    done = update(["w_kv", "w_q", "w_o", "ffn_w_in", "ffn_w_out"], "a", small_token)
    finish_scatter(len(in_flight) - 1, done)
    done = update(["cm_w_in", "cm_w_out"], "b", done)

    summed = sum_slots(small_wait(sm_send, sm_recv, small_land, done))
    g_full = dict(zip(small_names, _unpack(summed, [small_shapes_full[n] for n in small_names])))
    g_loc = {}
    for n in small_names:
        if n in small_sharded:
            width = w_in[n].shape[-1]
            g_loc[n] = lax.dynamic_slice_in_dim(g_full[n], me_chip * width, width, axis=g_full[n].ndim - 1)
        else:
            g_loc[n] = g_full[n]
    res = adamw(_pack([w_in[n] for n in small_names]), _pack([m_in[n] for n in small_names]),
                _pack([v_in[n] for n in small_names]), [_pack([g_loc[n] for n in small_names])], name="adamw_small")
    shapes_loc = [w_in[n].shape for n in small_names]
    for dst, packed in zip((outs_g, outs_d, outs_m, outs_v), res):
        for n, a in zip(small_names, _unpack(packed, shapes_loc)):
            dst[n] = a

    return (loss, grad_x, *[outs_g[n] for n in names], *[outs_d[n] for n in names],
            *[outs_m[n] for n in names], *[outs_v[n] for n in names])
```

```python
import math

import jax
import jax.numpy as jnp
from jax import lax
from jax.experimental import pallas as pl
from jax.experimental.pallas import tpu as pltpu

F32 = jnp.float32
BF16 = jnp.bfloat16
EPS = 1e-6
NEG_INF = -1e30
N_A = 2
DEPTH = 4
N_GROUPS = 3
DILATIONS = (1, 4, 16)
HEAD_DIM = 128
BLK = 128
LANES = 128
N_CHIPS = 4
N_DEV = 8
VMEM_LIMIT_V7X = 56 * 1024 * 1024

ADAM_LR = 0.001
ADAM_B1 = 0.9
ADAM_B2 = 0.999
ADAM_EPS = 1e-08
ADAM_WD = 0.01
ADAM_STEP = 10

MESH = pl.DeviceIdType.MESH


def _cp(*sem, **kw):
    return pltpu.CompilerParams(dimension_semantics=sem if sem else None, vmem_limit_bytes=VMEM_LIMIT_V7X, **kw)


def _dot(a, b):
    return jnp.dot(a, b, preferred_element_type=F32)


def _dot_nt(a, b):
    return lax.dot_general(a, b, (((1,), (1,)), ((), ())), preferred_element_type=F32)


def _dot_tn(a, b):
    return lax.dot_general(a, b, (((0,), (0,)), ((), ())), preferred_element_type=F32)


def _sigmoid(x):
    return 1.0 / (1.0 + jnp.exp(-x))


def _row_tile(n, want):
    if n <= want:
        return n
    for t in range(want - want % 8, 7, -8):
        if n % t == 0:
            return t
    raise ValueError(f"no row tile for {n} rows")


def mm_nn(a, w, nsh, bias=None, out_dtype=F32, name="mm_nn", tm=1024, head_major=False, spg=1):
    assert nsh % spg == 0
    m, k = a.shape
    _, k2, ns = w.shape
    assert k == k2
    tm = _row_tile(m, tm)
    has_bias = bias is not None
    hps = ns // HEAD_DIM

    def body(*refs):
        if has_bias:
            a_ref, w_ref, b_ref, o_ref = refs
        else:
            a_ref, w_ref, o_ref = refs
        av = a_ref[...].astype(BF16)
        for jj in range(spg):
            acc = _dot(av, w_ref[jj])
            if has_bias:
                acc = acc + b_ref[:, jj * ns:(jj + 1) * ns]
            if head_major:
                for hh in range(hps):
                    o_ref[jj * hps + hh] = acc[:, hh * HEAD_DIM:(hh + 1) * HEAD_DIM].astype(out_dtype)
            else:
                o_ref[:, jj * ns:(jj + 1) * ns] = acc.astype(out_dtype)

    in_specs = [
        pl.BlockSpec((tm, k), lambda j, i: (i, 0)),
        pl.BlockSpec((spg, k, ns), lambda j, i: (j, 0, 0)),
    ]
    args = [a, w]
    if has_bias:
        in_specs.append(pl.BlockSpec((1, spg * ns), lambda j, i: (0, j)))
        args.append(bias)
    return pl.pallas_call(
        body,
        name=name,
        grid=(nsh // spg, m // tm),
        in_specs=in_specs,
        out_specs=(pl.BlockSpec((spg * hps, tm, HEAD_DIM), lambda j, i: (j, i, 0)) if head_major
                   else pl.BlockSpec((tm, spg * ns), lambda j, i: (i, j))),
        out_shape=jax.ShapeDtypeStruct((nsh * hps, m, HEAD_DIM) if head_major else (m, nsh * ns), out_dtype),
        compiler_params=_cp("parallel", "parallel"),
    )(*args)


def _split_parts(dy, nsh, spg):
    dys = list(dy) if isinstance(dy, (list, tuple)) else [dy]
    per = nsh // len(dys)
    assert per % spg == 0
    return dys, per, per // spg


def mm_nt(dy, w, nsh, out_dtype=F32, name="mm_nt", tm=1024, spg=1):
    dys, per, gpp = _split_parts(dy, nsh, spg)
    npart = len(dys)
    m = dys[0].shape[0]
    _, k, ns = w.shape
    assert all(d.shape == (m, per * ns) for d in dys)
    tm = _row_tile(m, tm)
    ngrp = nsh // spg

    def body(*refs):
        dy_refs = refs[:npart]
        w_ref, o_ref, acc_ref = refs[npart:]
        j = pl.program_id(1)

        @pl.when(j == 0)
        def _():
            acc_ref[...] = jnp.zeros(acc_ref.shape, F32)

        for pi in range(npart):
            @pl.when(j // gpp == pi)
            def _(pi=pi):
                part = _dot_nt(dy_refs[pi][:, 0:ns].astype(BF16), w_ref[0])
                for jj in range(1, spg):
                    part = part + _dot_nt(dy_refs[pi][:, jj * ns:(jj + 1) * ns].astype(BF16), w_ref[jj])
                acc_ref[...] += part

        @pl.when(j == ngrp - 1)
        def _():
            o_ref[...] = acc_ref[...].astype(out_dtype)

    dy_specs = [pl.BlockSpec((tm, spg * ns), lambda i, j, pi=pi: (i, jnp.clip(j - pi * gpp, 0, gpp - 1)))
                for pi in range(npart)]
    return pl.pallas_call(
        body,
        name=name,
        grid=(m // tm, ngrp),
        in_specs=[*dy_specs, pl.BlockSpec((spg, k, ns), lambda i, j: (j, 0, 0))],
        out_specs=pl.BlockSpec((tm, k), lambda i, j: (i, 0)),
        out_shape=jax.ShapeDtypeStruct((m, k), out_dtype),
        scratch_shapes=[pltpu.VMEM((tm, k), F32)],
        compiler_params=_cp("parallel", "arbitrary"),
    )(*dys, w)


def mm_tn(a, dy, nsh, name="mm_tn", tm=1024, spg=1):
    dys, per, gpp = _split_parts(dy, nsh, spg)
    npart = len(dys)
    m, k = a.shape
    ns = dys[0].shape[1] // per
    assert all(d.shape == (m, per * ns) for d in dys)
    tm = _row_tile(m, tm)
    nt = m // tm

    def body(*refs):
        a_ref = refs[0]
        dy_refs = refs[1:1 + npart]
        o_ref, acc_ref = refs[1 + npart:]
        j = pl.program_id(0)
        i = pl.program_id(1)

        @pl.when(i == 0)
        def _():
            acc_ref[...] = jnp.zeros(acc_ref.shape, F32)

        for pi in range(npart):
            @pl.when(j // gpp == pi)
            def _(pi=pi):
                at = a_ref[...].astype(BF16).T
                for jj in range(spg):
                    acc_ref[jj] += _dot(at, dy_refs[pi][:, jj * ns:(jj + 1) * ns].astype(BF16))

        @pl.when(i == nt - 1)
        def _():
            o_ref[...] = acc_ref[...].astype(BF16)

    dy_specs = [
        pl.BlockSpec((tm, spg * ns),
                     lambda j, i, pi=pi: (jnp.where(j // gpp == pi, i, 0), jnp.clip(j - pi * gpp, 0, gpp - 1)))
        for pi in range(npart)
    ]
    return pl.pallas_call(
        body,
        name=name,
        grid=(nsh // spg, nt),
        in_specs=[pl.BlockSpec((tm, k), lambda j, i: (i, 0)), *dy_specs],
        out_specs=pl.BlockSpec((spg, k, ns), lambda j, i: (j, 0, 0)),
        out_shape=jax.ShapeDtypeStruct((nsh, k, ns), BF16),
        scratch_shapes=[pltpu.VMEM((spg, k, ns), F32)],
        compiler_params=_cp("parallel", "arbitrary"),
    )(a, *dys)


DEP_SPEC_SHAPE = (8, LANES)


def resid_norm_fwd(x, y, g_post, next_gains, name, tm=1024, dep=None):
    t, d = x.shape
    tm = _row_tile(t, tm)
    has_y = y is not None
    n_next = len(next_gains)
    n_dep = 0 if dep is None else 1

    def body(*refs):
        x_ref = refs[0]
        pos = 1
        if has_y:
            y_ref, gp_ref = refs[1], refs[2]
            pos = 3
        gn_refs = refs[pos:pos + n_next]
        outs = refs[pos + n_next + n_dep:]
        xv = x_ref[...]
        o = 0
        if has_y:
            yv = y_ref[...].astype(F32)
            r = lax.rsqrt(jnp.mean(yv * yv, axis=-1, keepdims=True) + EPS)
            xv = xv + (yv * r) * gp_ref[...]
            outs[0][...] = xv
            o = 1
        if n_next:
            xn = xv * lax.rsqrt(jnp.mean(xv * xv, axis=-1, keepdims=True) + EPS)
            for k in range(n_next):
                outs[o + k][...] = (xn * gn_refs[k][...]).astype(BF16)

    row = pl.BlockSpec((tm, d), lambda i: (i, 0))
    vec = pl.BlockSpec((1, d), lambda i: (0, 0))
    args, in_specs = [x], [row]
    if has_y:
        args += [y, g_post]
        in_specs += [row, vec]
    args += list(next_gains)
    in_specs += [vec] * n_next
    if n_dep:
        args.append(dep)
        in_specs.append(pl.BlockSpec(DEP_SPEC_SHAPE, lambda i: (0, 0)))
    out_shape, out_specs = [], []
    if has_y:
        out_shape.append(jax.ShapeDtypeStruct((t, d), F32))
        out_specs.append(row)
    for _ in range(n_next):
        out_shape.append(jax.ShapeDtypeStruct((t, d), BF16))
        out_specs.append(row)
    return pl.pallas_call(
        body, name=name, grid=(t // tm,), in_specs=in_specs, out_specs=out_specs, out_shape=out_shape,
        compiler_params=_cp("parallel"),
    )(*args)


def norm_bwd(x, g, dy, add=None, out_dtype=F32, name="norm_bwd", tm=1024, dep=None):
    t, d = x.shape
    tm = _row_tile(t, tm)
    has_add = add is not None

    def body(*refs):
        x_ref, g_ref, dy_ref = refs[:3]
        add_ref = refs[3] if has_add else None
        dx_ref, dg_ref, cs_ref = refs[-3:]
        i = pl.program_id(0)
        xv = x_ref[...].astype(F32)
        dyv = dy_ref[...].astype(F32)
        r = lax.rsqrt(jnp.mean(xv * xv, axis=-1, keepdims=True) + EPS)
        gd = dyv * g_ref[...]
        dx = r * gd - xv * ((r * r * r) * jnp.mean(xv * gd, axis=-1, keepdims=True))
        if has_add:
            dx = dx + add_ref[...]
        dx_ref[...] = dx.astype(out_dtype)
        dg = jnp.sum(dyv * (xv * r), axis=0, keepdims=True)
        cs = jnp.sum(dx, axis=0, keepdims=True)

        @pl.when(i == 0)
        def _():
            dg_ref[...] = dg
            cs_ref[...] = cs

        @pl.when(i > 0)
        def _():
            dg_ref[...] += dg
            cs_ref[...] += cs

    row = pl.BlockSpec((tm, d), lambda i: (i, 0))
    vec = pl.BlockSpec((1, d), lambda i: (0, 0))
    args, in_specs = [x, g, dy], [row, vec, row]
    if has_add:
        args.append(add)
        in_specs.append(row)
    if dep is not None:
        args.append(dep)
        in_specs.append(pl.BlockSpec(DEP_SPEC_SHAPE, lambda i: (0, 0)))
    return pl.pallas_call(
        body, name=name, grid=(t // tm,), in_specs=in_specs,
        out_specs=[row, vec, vec],
        out_shape=[jax.ShapeDtypeStruct((t, d), out_dtype), jax.ShapeDtypeStruct((1, d), F32),
                   jax.ShapeDtypeStruct((1, d), F32)],
        compiler_params=_cp("arbitrary"),
    )(*args)


def loss_fwd_bwd(x, target, name="loss", tm=1024):
    t, d = x.shape
    tm = _row_tile(t, tm)

    def body(x_ref, t_ref, dx_ref, l_ref):
        i = pl.program_id(0)
        err = x_ref[...] - t_ref[...]
        dx_ref[...] = err * (1.0 / d)
        part = 0.5 * jnp.sum(jnp.mean(err * err, axis=-1, keepdims=True), axis=0, keepdims=True)
        part = jnp.broadcast_to(part, l_ref.shape)

        @pl.when(i == 0)
        def _():
            l_ref[...] = part

        @pl.when(i > 0)
        def _():
            l_ref[...] += part

    row = pl.BlockSpec((tm, d), lambda i: (i, 0))
    return pl.pallas_call(
        body, name=name, grid=(t // tm,), in_specs=[row, row],
        out_specs=[row, pl.BlockSpec((8, LANES), lambda i: (0, 0))],
        out_shape=[jax.ShapeDtypeStruct((t, d), F32), jax.ShapeDtypeStruct((8, LANES), F32)],
        compiler_params=_cp("arbitrary"),
    )(x, target)


CONV_HALO = 32
CONV_CHUNK = 128


def glu_conv_fwd(z, dw, dwb, name, tc=128):
    b, s, c2 = z.shape
    c = c2 // 2
    kw = dw.shape[0]
    tc = min(tc, c)
    nc = c // tc
    ch = min(CONV_CHUNK, s)
    halo = CONV_HALO
    assert kw - 1 <= halo and s % ch == 0

    nch = s // ch

    def body(a_ref, g_ref, w_ref, b_ref, o_ref, pad_ref):
        _fill_glu_slabs(a_ref, g_ref, pad_ref, nch, ch, halo)

        def chunk(ci, carry):
            r0 = pl.multiple_of(ci * ch, ch)
            acc = b_ref[...]
            for k, tap in enumerate(_taps_front(pad_ref, ci, kw, ch, halo)):
                acc = acc + w_ref[k:k + 1, :] * tap
            o_ref[pl.ds(r0, ch), :] = acc
            return carry

        lax.fori_loop(0, nch, chunk, 0)

    return pl.pallas_call(
        body, name=name, grid=(b, nc),
        in_specs=[
            pl.BlockSpec((None, s, tc), lambda bi, i: (bi, 0, i)),
            pl.BlockSpec((None, s, tc), lambda bi, i: (bi, 0, i + nc)),
            pl.BlockSpec((kw, tc), lambda bi, i: (0, i)),
            pl.BlockSpec((1, tc), lambda bi, i: (0, i)),
        ],
        out_specs=pl.BlockSpec((None, s, tc), lambda bi, i: (bi, 0, i)),
        out_shape=jax.ShapeDtypeStruct((b, s, c), F32),
        scratch_shapes=[pltpu.VMEM((nch, ch + halo, tc), F32)],
        compiler_params=_cp("parallel", "parallel"),
    )(z, z, dw, dwb)


def glu_conv_bwd(z, dw, du2, name, tc=128):
    b, s, c2 = z.shape
    c = c2 // 2
    kw = dw.shape[0]
    tc = min(tc, c)
    nc = c // tc
    ch = min(CONV_CHUNK, s)
    nch = s // ch
    halo = CONV_HALO

    def body(a_ref, g_ref, w_ref, du_ref, dza_ref, dzg_ref, ddw_ref, ddwb_ref, dba_ref, dbg_ref, upad_ref, dpad_ref):
        bi = pl.program_id(1)

        @pl.when(bi == 0)
        def _():
            ddw_ref[...] = jnp.zeros(ddw_ref.shape, F32)
            ddwb_ref[...] = jnp.zeros(ddwb_ref.shape, F32)
            dba_ref[...] = jnp.zeros(dba_ref.shape, F32)
            dbg_ref[...] = jnp.zeros(dbg_ref.shape, F32)

        _fill_glu_slabs(a_ref, g_ref, upad_ref, nch, ch, halo)
        dpad_ref[nch - 1, ch:ch + halo, :] = jnp.zeros((halo, tc), F32)
        dpad_ref[nch - 1, 0:ch, :] = du_ref[s - ch:s, :]

        def fill(ci, carry):
            r0 = pl.multiple_of(ci * ch, ch)
            dpad_ref[ci, :, :] = du_ref[pl.ds(r0, ch + halo), :]
            return carry

        lax.fori_loop(0, nch - 1, fill, 0)

        def chunk(ci, carry):
            r0 = pl.multiple_of(ci * ch, ch)
            du_c = du_ref[pl.ds(r0, ch), :]
            taps_u = _taps_front(upad_ref, ci, kw, ch, halo)
            taps_d = _taps_at(dpad_ref, ci, list(range(kw)), ch)
            du1 = w_ref[kw - 1:kw, :] * du_c
            ddw_ref[kw - 1] += jnp.sum((du_c * taps_u[kw - 1]).reshape(ch // 8, 8, tc), axis=0)
            for j in range(1, kw):
                du1 = du1 + w_ref[kw - 1 - j:kw - j, :] * taps_d[j]
                ddw_ref[kw - 1 - j] += jnp.sum((du_c * taps_u[kw - 1 - j]).reshape(ch // 8, 8, tc), axis=0)
            av = a_ref[pl.ds(r0, ch), :]
            sg = _sigmoid(g_ref[pl.ds(r0, ch), :])
            dza = du1 * sg
            dzg = du1 * av * (sg * (1.0 - sg))
            dza_ref[pl.ds(r0, ch), :] = dza.astype(BF16)
            dzg_ref[pl.ds(r0, ch), :] = dzg.astype(BF16)
            dba_ref[...] += jnp.sum(dza, axis=0, keepdims=True)
            dbg_ref[...] += jnp.sum(dzg, axis=0, keepdims=True)
            ddwb_ref[...] += jnp.sum(du_c, axis=0, keepdims=True)
            return carry

        lax.fori_loop(0, s // ch, chunk, 0)

    blk = lambda off: pl.BlockSpec((None, s, tc), lambda i, bi: (bi, 0, i + off))
    vec = pl.BlockSpec((1, tc), lambda i, bi: (0, i))
    return pl.pallas_call(
        body, name=name, grid=(nc, b),
        in_specs=[blk(0), blk(nc), pl.BlockSpec((kw, tc), lambda i, bi: (0, i)), blk(0)],
        out_specs=[blk(0), blk(0), pl.BlockSpec((kw, 8, tc), lambda i, bi: (0, 0, i)), vec, vec, vec],
        out_shape=[
            jax.ShapeDtypeStruct((b, s, c), BF16), jax.ShapeDtypeStruct((b, s, c), BF16),
            jax.ShapeDtypeStruct((kw, 8, c), F32), jax.ShapeDtypeStruct((1, c), F32),
            jax.ShapeDtypeStruct((1, c), F32), jax.ShapeDtypeStruct((1, c), F32),
        ],
        scratch_shapes=[pltpu.VMEM((nch, ch + halo, tc), F32), pltpu.VMEM((nch, ch + halo, tc), F32)],
        compiler_params=_cp("parallel", "arbitrary"),
    )(z, z, dw, du2)


def _fill_glu_slabs(a_ref, g_ref, pad_ref, nch, ch, halo):
    tc = a_ref.shape[-1]
    pad_ref[0, 0:halo, :] = jnp.zeros((halo, tc), F32)
    pad_ref[0, halo:halo + ch, :] = a_ref[0:ch, :] * _sigmoid(g_ref[0:ch, :])

    def fill(ci, carry):
        r0 = pl.multiple_of(ci * ch, ch)
        pad_ref[ci, 0:halo, :] = pad_ref[ci - 1, ch:ch + halo, :]
        pad_ref[ci, halo:halo + ch, :] = a_ref[pl.ds(r0, ch), :] * _sigmoid(g_ref[pl.ds(r0, ch), :])
        return carry

    lax.fori_loop(1, nch, fill, 0)


def ln_silu_fwd(u, g, bvec, name, tm=1024):
    t, d = u.shape
    tm = _row_tile(t, tm)

    def body(u_ref, g_ref, b_ref, o_ref):
        uv = u_ref[...]
        mu = jnp.mean(uv, axis=-1, keepdims=True)
        xc = uv - mu
        var = jnp.mean(xc * xc, axis=-1, keepdims=True)
        v = (xc * lax.rsqrt(var + EPS)) * g_ref[...] + b_ref[...]
        o_ref[...] = (v * _sigmoid(v)).astype(BF16)

    row = pl.BlockSpec((tm, d), lambda i: (i, 0))
    vec = pl.BlockSpec((1, d), lambda i: (0, 0))
    return pl.pallas_call(
        body, name=name, grid=(t // tm,), in_specs=[row, vec, vec], out_specs=row,
        out_shape=jax.ShapeDtypeStruct((t, d), BF16), compiler_params=_cp("parallel"),
    )(u, g, bvec)


def ln_silu_bwd(u, g, bvec, dout, name, tm=1024):
    t, d = u.shape
    tm = _row_tile(t, tm)

    def body(u_ref, g_ref, b_ref, do_ref, du_ref, dg_ref, db_ref):
        i = pl.program_id(0)
        uv = u_ref[...]
        mu = jnp.mean(uv, axis=-1, keepdims=True)
        xc = uv - mu
        var = jnp.mean(xc * xc, axis=-1, keepdims=True)
        rstd = lax.rsqrt(var + EPS)
        n = xc * rstd
        v = n * g_ref[...] + b_ref[...]
        sg = _sigmoid(v)
        dv = do_ref[...].astype(F32) * (sg * (1.0 + v * (1.0 - sg)))
        dn = dv * g_ref[...]
        du_ref[...] = rstd * (dn - jnp.mean(dn, axis=-1, keepdims=True) - n * jnp.mean(dn * n, axis=-1, keepdims=True))
        dg = jnp.sum(dv * n, axis=0, keepdims=True)
        db = jnp.sum(dv, axis=0, keepdims=True)

        @pl.when(i == 0)
        def _():
            dg_ref[...] = dg
            db_ref[...] = db

        @pl.when(i > 0)
        def _():
            dg_ref[...] += dg
            db_ref[...] += db

    row = pl.BlockSpec((tm, d), lambda i: (i, 0))
    vec = pl.BlockSpec((1, d), lambda i: (0, 0))
    return pl.pallas_call(
        body, name=name, grid=(t // tm,), in_specs=[row, vec, vec, row], out_specs=[row, vec, vec],
        out_shape=[jax.ShapeDtypeStruct((t, d), F32), jax.ShapeDtypeStruct((1, d), F32), jax.ShapeDtypeStruct((1, d), F32)],
        compiler_params=_cp("arbitrary"),
    )(u, g, bvec, dout)


FFN_HALO = 8


def _fill_front_halo(src_ref, pad_ref, nch, ch, halo):
    tc = src_ref.shape[-1]
    pad_ref[0, 0:halo, :] = jnp.zeros((halo, tc), F32)
    pad_ref[0, halo:halo + ch, :] = src_ref[0:ch, :].astype(F32)

    def fill(ci, carry):
        r0 = pl.multiple_of(ci * ch, ch)
        pad_ref[ci, 0:halo, :] = src_ref[pl.ds(r0 - 2 * halo, 2 * halo), :].astype(F32)[halo:, :]
        pad_ref[ci, halo:halo + ch, :] = src_ref[pl.ds(r0, ch), :].astype(F32)
        return carry

    lax.fori_loop(1, nch, fill, 0)


def _taps_at(pad_ref, ci, offsets, ch):
    windows = {}
    for b in sorted({o % 8 for o in offsets}):
        top = max(o for o in offsets if o % 8 == b)
        windows[b] = pad_ref[ci, b:top + ch, :]
    return [windows[o % 8][o - o % 8:o - o % 8 + ch, :] for o in offsets]


def _taps_front(pad_ref, ci, kw, ch, halo):
    return _taps_at(pad_ref, ci, [halo - (kw - 1 - k) for k in range(kw)], ch)


def ffn_mid_fwd(p, dw, dwb, name, tc=256):
    b, s, f2 = p.shape
    f = f2 // 2
    kw = dw.shape[0]
    tc = min(tc, f)
    nf = f // tc
    ch = min(CONV_CHUNK, s)
    nch = s // ch
    halo = FFN_HALO

    def body(pa_ref, pg_ref, wa_ref, wg_ref, ba_ref, bg_ref, o_ref, apad_ref, gpad_ref):
        _fill_front_halo(pa_ref, apad_ref, nch, ch, halo)
        _fill_front_halo(pg_ref, gpad_ref, nch, ch, halo)

        def chunk(ci, carry):
            r0 = pl.multiple_of(ci * ch, ch)
            ca = ba_ref[...]
            cg = bg_ref[...]
            taps = zip(_taps_front(apad_ref, ci, kw, ch, halo), _taps_front(gpad_ref, ci, kw, ch, halo))
            for k, (ta, tg) in enumerate(taps):
                ca = ca + wa_ref[k:k + 1, :] * ta
                cg = cg + wg_ref[k:k + 1, :] * tg
            o_ref[pl.ds(r0, ch), :] = ((cg * _sigmoid(cg)) * ca).astype(BF16)
            return carry

        lax.fori_loop(0, nch, chunk, 0)

    blk = lambda off: pl.BlockSpec((None, s, tc), lambda bi, i: (bi, 0, i + off))
    wsp = lambda off: pl.BlockSpec((kw, tc), lambda bi, i: (0, i + off))
    bsp = lambda off: pl.BlockSpec((1, tc), lambda bi, i: (0, i + off))
    return pl.pallas_call(
        body, name=name, grid=(b, nf),
        in_specs=[blk(0), blk(nf), wsp(0), wsp(nf), bsp(0), bsp(nf)],
        out_specs=pl.BlockSpec((None, s, tc), lambda bi, i: (bi, 0, i)),
        out_shape=jax.ShapeDtypeStruct((b, s, f), BF16),
        scratch_shapes=[pltpu.VMEM((nch, ch + halo, tc), F32)] * 2,
        compiler_params=_cp("parallel", "parallel"),
    )(p, p, dw, dw, dwb, dwb)


def ffn_mid_bwd(p, dw, dwb, ds, name, tc=256):
    b, s, f2 = p.shape
    f = f2 // 2
    kw = dw.shape[0]
    tc = min(tc, f)
    nf = f // tc
    ch = min(CONV_CHUNK, s)
    nch = s // ch
    halo = FFN_HALO

    def sum8(v):
        return jnp.sum(v.reshape(ch // 8, 8, tc), axis=0)

    def body(pa_ref, pg_ref, wa_ref, wg_ref, ba_ref, bg_ref, ds_ref, dpa_ref, dpg_ref, ddwa_ref, ddwg_ref, dba_ref, dbg_ref,
             apad_ref, gpad_ref, dca_ref, dcg_ref):
        bi = pl.program_id(1)

        @pl.when(bi == 0)
        def _():
            ddwa_ref[...] = jnp.zeros(ddwa_ref.shape, F32)
            ddwg_ref[...] = jnp.zeros(ddwg_ref.shape, F32)
            dba_ref[...] = jnp.zeros(dba_ref.shape, F32)
            dbg_ref[...] = jnp.zeros(dbg_ref.shape, F32)

        _fill_front_halo(pa_ref, apad_ref, nch, ch, halo)
        _fill_front_halo(pg_ref, gpad_ref, nch, ch, halo)
        dca_ref[nch - 1, ch:ch + halo, :] = jnp.zeros((halo, tc), F32)
        dcg_ref[nch - 1, ch:ch + halo, :] = jnp.zeros((halo, tc), F32)

        def grads(ci, carry):
            acc_a, acc_g, sb_a, sb_g = carry
            r0 = pl.multiple_of(ci * ch, ch)
            taps_a = _taps_front(apad_ref, ci, kw, ch, halo)
            taps_g = _taps_front(gpad_ref, ci, kw, ch, halo)
            ca = ba_ref[...]
            cg = bg_ref[...]
            for k in range(kw):
                ca = ca + wa_ref[k:k + 1, :] * taps_a[k]
                cg = cg + wg_ref[k:k + 1, :] * taps_g[k]
            sg = _sigmoid(cg)
            dsv = ds_ref[pl.ds(r0, ch), :].astype(F32)
            dca = dsv * (cg * sg)
            dcg = dsv * ca * (sg * (1.0 + cg * (1.0 - sg)))
            dca_ref[ci, 0:ch, :] = dca
            dcg_ref[ci, 0:ch, :] = dcg

            prev = jnp.maximum(ci - 1, 0)

            @pl.when(ci > 0)
            def _():
                dca_ref[prev, ch:ch + halo, :] = dca[0:halo, :]
                dcg_ref[prev, ch:ch + halo, :] = dcg[0:halo, :]

            acc_a = tuple(acc_a[k] + sum8(dca * taps_a[k]) for k in range(kw))
            acc_g = tuple(acc_g[k] + sum8(dcg * taps_g[k]) for k in range(kw))
            return acc_a, acc_g, sb_a + sum8(dca), sb_g + sum8(dcg)

        z8 = jnp.zeros((8, tc), F32)
        acc_a, acc_g, sb_a, sb_g = lax.fori_loop(0, nch, grads, ((z8,) * kw, (z8,) * kw, z8, z8))
        for k in range(kw):
            ddwa_ref[k] += acc_a[k]
            ddwg_ref[k] += acc_g[k]
        dba_ref[...] += jnp.sum(sb_a, axis=0, keepdims=True)
        dbg_ref[...] += jnp.sum(sb_g, axis=0, keepdims=True)

        def back(ci, carry):
            r0 = pl.multiple_of(ci * ch, ch)
            da = wa_ref[kw - 1:kw, :] * dca_ref[ci, 0:ch, :]
            dg = wg_ref[kw - 1:kw, :] * dcg_ref[ci, 0:ch, :]
            for j in range(1, kw):
                da = da + wa_ref[kw - 1 - j:kw - j, :] * dca_ref[ci, j:j + ch, :]
                dg = dg + wg_ref[kw - 1 - j:kw - j, :] * dcg_ref[ci, j:j + ch, :]
            dpa_ref[pl.ds(r0, ch), :] = da.astype(BF16)
            dpg_ref[pl.ds(r0, ch), :] = dg.astype(BF16)
            return carry

        lax.fori_loop(0, nch, back, 0)

    blk = lambda off: pl.BlockSpec((None, s, tc), lambda i, bi: (bi, 0, i + off))
    wsp = lambda off: pl.BlockSpec((kw, tc), lambda i, bi: (0, i + off))
    bsp = lambda off: pl.BlockSpec((1, tc), lambda i, bi: (0, i + off))
    acc3 = pl.BlockSpec((kw, 8, tc), lambda i, bi: (0, 0, i))
    vec = pl.BlockSpec((1, tc), lambda i, bi: (0, i))
    return pl.pallas_call(
        body, name=name, grid=(nf, b),
        in_specs=[blk(0), blk(nf), wsp(0), wsp(nf), bsp(0), bsp(nf), blk(0)],
        out_specs=[blk(0), blk(0), acc3, acc3, vec, vec],
        out_shape=[jax.ShapeDtypeStruct((b, s, f), BF16), jax.ShapeDtypeStruct((b, s, f), BF16),
                   jax.ShapeDtypeStruct((kw, 8, f), F32), jax.ShapeDtypeStruct((kw, 8, f), F32),
                   jax.ShapeDtypeStruct((1, f), F32), jax.ShapeDtypeStruct((1, f), F32)],
        scratch_shapes=[pltpu.VMEM((nch, ch + halo, tc), F32)] * 4,
        compiler_params=_cp("parallel", "arbitrary"),
    )(p, p, dw, dw, dwb, dwb, ds)


def _tile_rows(r, n, dil):
    start = r + n * BLK * dil
    return pl.ds(start, BLK, stride=dil) if dil > 1 else pl.ds(start, BLK)


def _band_masks():
    qi = lax.broadcasted_iota(jnp.int32, (BLK, 2 * BLK), 0)
    kk = lax.broadcasted_iota(jnp.int32, (BLK, 2 * BLK), 1)
    both = jnp.logical_or(jnp.logical_and(kk < BLK, kk >= qi), jnp.logical_and(kk >= BLK, kk - BLK <= qi))
    return both, kk[:, :BLK] <= qi[:, :BLK]


def attn_fwd(q, kv, g, dil, hw, name):
    _, b, s, _ = q.shape
    nh = hw // HEAD_DIM
    nblk = s // dil // BLK
    scale = 1.0 / math.sqrt(HEAD_DIM)

    def body(q_ref, k_ref, v_ref, o_ref, lse_ref):
        h = pl.program_id(1)
        mask2, mask1 = _band_masks()
        mine = lax.broadcasted_iota(jnp.int32, (BLK, LANES), 1) == h

        @pl.when(h == 0)
        def _():
            lse_ref[...] = jnp.zeros(lse_ref.shape, F32)

        for r in range(dil):
            kp = vp = None
            for n in range(nblk):
                rs = _tile_rows(r, n, dil)
                qt = q_ref[rs, :].astype(BF16)
                kc = k_ref[rs, :].astype(BF16)
                vc = v_ref[rs, :].astype(BF16)
                if n == 0:
                    kcat, vcat, mask = kc, vc, mask1
                else:
                    kcat, vcat, mask = jnp.concatenate([kp, kc], axis=0), jnp.concatenate([vp, vc], axis=0), mask2
                sc = jnp.where(mask, _dot_nt(qt, kcat) * scale, NEG_INF)
                m = jnp.max(sc, axis=-1, keepdims=True)
                p = jnp.exp(sc - m)
                den = jnp.sum(p, axis=-1, keepdims=True)
                o_ref[rs, :] = _dot(p.astype(BF16), vcat) / den
                lse_ref[rs, :] = jnp.where(mine, m + jnp.log(den), lse_ref[rs, :])
                kp, vp = kc, vc

    col = lambda base: pl.BlockSpec((None, s, HEAD_DIM), lambda bi, h: (bi, 0, base + h))
    head = lambda base: pl.BlockSpec((None, None, s, HEAD_DIM), lambda bi, h: (base + h, bi, 0, 0))
    return pl.pallas_call(
        body, name=name, grid=(b, nh),
        in_specs=[head(g * nh), head(g * nh), head((N_GROUPS + g) * nh)],
        out_specs=[head(0), pl.BlockSpec((None, s, LANES), lambda bi, h: (bi, 0, 0))],
        out_shape=[jax.ShapeDtypeStruct((nh, b, s, HEAD_DIM), F32), jax.ShapeDtypeStruct((b, s, LANES), F32)],
        compiler_params=_cp("parallel", "arbitrary"),
    )(q, kv, kv)


def attn_merge(outs, lses, name, tm=512):
    nh, t, _ = outs[0].shape
    hw = nh * HEAD_DIM
    tm = _row_tile(t, tm)
    ng = len(outs)

    def body(*refs):
        o_refs, l_refs = refs[:ng], refs[ng:2 * ng]
        m_ref, lj_ref = refs[2 * ng:]
        ls = [l_refs[g][...] for g in range(ng)]
        mx = ls[0]
        for g in range(1, ng):
            mx = jnp.maximum(mx, ls[g])
        es = [jnp.exp(l - mx) for l in ls]
        tot = es[0]
        for g in range(1, ng):
            tot = tot + es[g]
        ws = [e / tot for e in es]
        lj_ref[...] = mx + jnp.log(tot)
        for h in range(nh):
            sl = slice(h * HEAD_DIM, (h + 1) * HEAD_DIM)
            acc = ws[0][:, h:h + 1] * o_refs[0][h]
            for g in range(1, ng):
                acc = acc + ws[g][:, h:h + 1] * o_refs[g][h]
            m_ref[:, sl] = acc.astype(BF16)

    row = pl.BlockSpec((tm, hw), lambda i: (i, 0))
    st = pl.BlockSpec((tm, LANES), lambda i: (i, 0))
    heads = pl.BlockSpec((nh, tm, HEAD_DIM), lambda i: (0, i, 0))
    return pl.pallas_call(
        body, name=name, grid=(t // tm,), in_specs=[heads] * ng + [st] * ng, out_specs=[row, st],
        out_shape=[jax.ShapeDtypeStruct((t, hw), BF16), jax.ShapeDtypeStruct((t, LANES), F32)],
        compiler_params=_cp("parallel"),
    )(*outs, *lses)


def attn_bwd_prep(dmerged, merged, name, tm=512):
    t, hw = merged.shape
    nh = hw // HEAD_DIM
    tm = _row_tile(t, tm)

    def body(d_ref, m_ref, o_ref):
        lane = lax.broadcasted_iota(jnp.int32, (tm, LANES), 1)
        acc = jnp.zeros((tm, LANES), F32)
        for h in range(nh):
            sl = slice(h * HEAD_DIM, (h + 1) * HEAD_DIM)
            dsum = jnp.sum(d_ref[:, sl] * m_ref[:, sl].astype(F32), axis=-1, keepdims=True)
            acc = jnp.where(lane == h, dsum, acc)
        o_ref[...] = acc

    row = pl.BlockSpec((tm, hw), lambda i: (i, 0))
    return pl.pallas_call(
        body, name=name, grid=(t // tm,), in_specs=[row, row], out_specs=pl.BlockSpec((tm, LANES), lambda i: (i, 0)),
        out_shape=jax.ShapeDtypeStruct((t, LANES), F32), compiler_params=_cp("parallel"),
    )(dmerged, merged)


def attn_bwd(q, kv, g, dil, do, lsej, dm, dq_buf, dk_buf, dv_buf, accumulate, hw, name):
    _, b, s, _ = q.shape
    nh = hw // HEAD_DIM
    nblk = s // dil // BLK
    scale = 1.0 / math.sqrt(HEAD_DIM)
    assert dk_buf is not None or not accumulate
    kv_at = 6 + (dq_buf is not None)

    def body(*refs):
        q_ref, k_ref, v_ref, do_ref, lj_ref, dm_ref = refs[:6]
        dq_ref, dk_ref, dv_ref = refs[-3:]
        dki_ref, dvi_ref = (refs[kv_at], refs[kv_at + 1]) if accumulate else (None, None)
        mask2, mask1 = _band_masks()
        mine = lax.broadcasted_iota(jnp.int32, (BLK, LANES), 1) == pl.program_id(1)

        def my_lane(v):
            return jnp.sum(jnp.where(mine, v, 0.0), axis=-1, keepdims=True)

        def put(rs, dk, dv):
            if accumulate:
                dk = dk + dki_ref[rs, :]
                dv = dv + dvi_ref[rs, :]
            dk_ref[rs, :] = dk
            dv_ref[rs, :] = dv

        for r in range(dil):
            kp = vp = hold_k = hold_v = rs_prev = None
            for n in range(nblk):
                rs = _tile_rows(r, n, dil)
                qt = q_ref[rs, :].astype(BF16)
                kc = k_ref[rs, :].astype(BF16)
                vc = v_ref[rs, :].astype(BF16)
                dot = do_ref[rs, :].astype(BF16)
                lm = my_lane(lj_ref[rs, :])
                dmm = my_lane(dm_ref[rs, :])
                if n == 0:
                    kcat, vcat, mask = kc, vc, mask1
                else:
                    kcat, vcat, mask = jnp.concatenate([kp, kc], axis=0), jnp.concatenate([vp, vc], axis=0), mask2
                p = jnp.exp(jnp.where(mask, _dot_nt(qt, kcat) * scale, NEG_INF) - lm)
                ds = (p * (_dot_nt(dot, vcat) - dmm)).astype(BF16)
                dq_ref[rs, :] = _dot(ds, kcat) * scale
                dkc = _dot_tn(ds, qt) * scale
                dvc = _dot_tn(p.astype(BF16), dot)
                if n > 0:
                    put(rs_prev, hold_k + dkc[:BLK, :], hold_v + dvc[:BLK, :])
                    dkc, dvc = dkc[BLK:, :], dvc[BLK:, :]
                hold_k, hold_v, kp, vp, rs_prev = dkc, dvc, kc, vc, rs
            put(rs_prev, hold_k, hold_v)

    col = lambda base: pl.BlockSpec((None, s, HEAD_DIM), lambda bi, h: (bi, 0, base + h))
    any_spec = pl.BlockSpec(memory_space=pl.ANY)
    stat = pl.BlockSpec((None, s, LANES), lambda bi, h: (bi, 0, 0))
    head = lambda base: pl.BlockSpec((None, None, s, HEAD_DIM), lambda bi, h: (base + h, bi, 0, 0))
    in_specs = [head(g * nh), head(g * nh), head((N_GROUPS + g) * nh), col(0), stat, stat]
    args = [q, kv, kv, do, lsej, dm]
    aliases = {}
    if dq_buf is not None:
        in_specs.append(any_spec)
        args.append(dq_buf)
        aliases[6] = 0
    if dk_buf is not None:
        in_specs += [col(g * nh) if accumulate else any_spec] * 2
        args += [dk_buf, dv_buf]
        aliases.update({kv_at: 1, kv_at + 1: 2})
    shape = jax.ShapeDtypeStruct((b, s, N_GROUPS * hw), F32)
    return pl.pallas_call(
        body, name=name, grid=(b, nh), in_specs=in_specs, out_specs=[col(g * nh)] * 3, out_shape=[shape] * 3,
        input_output_aliases=aliases, compiler_params=_cp("parallel", "parallel"),
    )(*args)


def sum_parts(g, recv, me, name, tm=512, layer=0, n_layers=1, out_buf=None):
    _, rows, c = g.shape
    n = recv.shape[0]
    tm = _row_tile(rows, tm)
    steps = rows // tm

    def body(me_ref, g_ref, r_ref, *rest):
        acc = g_ref[...].astype(F32)
        for j in range(n):
            acc = acc + r_ref[j].astype(F32)
        rest[-1][...] = acc

    in_specs = [pl.BlockSpec((None, tm, c), lambda i, me_ref: (me_ref[0], i, 0)),
                pl.BlockSpec((n, tm, c), lambda i, me_ref: (0, i, 0))]
    args = [me, g, recv]
    aliases = {}
    if out_buf is not None:
        in_specs.append(pl.BlockSpec(memory_space=pl.ANY))
        args.append(out_buf)
        aliases = {3: 0}
    return pl.pallas_call(
        body, name=name,
        grid_spec=pltpu.PrefetchScalarGridSpec(
            num_scalar_prefetch=1, grid=(steps,), in_specs=in_specs,
            out_specs=pl.BlockSpec((tm, c), lambda i, me_ref: (layer * steps + i, 0))),
        out_shape=jax.ShapeDtypeStruct((n_layers * rows, c), F32), input_output_aliases=aliases,
        compiler_params=_cp("parallel"),
    )(*args)


def adamw(w, m, v, g_parts, name, tm=256):
    rows, c = w.shape
    if c <= 8 * LANES:
        tm = 2 * tm
    tm = _row_tile(rows, tm)
    npart = len(g_parts)

    def body(*refs):
        w_ref, m_ref, v_ref = refs[:3]
        g_refs = refs[3:3 + npart]
        go_ref, d_ref, mo_ref, vo_ref = refs[3 + npart:]
        g = g_refs[0][...]
        for k in range(1, npart):
            g = g + g_refs[k][...]
        mn = ADAM_B1 * m_ref[...] + (1.0 - ADAM_B1) * g
        vn = ADAM_B2 * v_ref[...] + (1.0 - ADAM_B2) * (g * g)
        m_hat = mn / (1.0 - ADAM_B1 ** ADAM_STEP)
        v_hat = vn / (1.0 - ADAM_B2 ** ADAM_STEP)
        go_ref[...] = g
        d_ref[...] = -ADAM_LR * (m_hat / (jnp.sqrt(v_hat) + ADAM_EPS) + ADAM_WD * w_ref[...])
        mo_ref[...] = mn
        vo_ref[...] = vn

    row = pl.BlockSpec((tm, c), lambda i: (i, 0))
    return pl.pallas_call(
        body, name=name, grid=(rows // tm,), in_specs=[row] * (3 + npart), out_specs=[row] * 4,
        out_shape=[jax.ShapeDtypeStruct((rows, c), F32)] * 4, compiler_params=_cp("parallel"),
    )(w, m, v, *g_parts)


def _place():
    return lax.axis_index("x"), lax.axis_index("y"), lax.axis_index("c")


def _other_chips(x, y, c):
    return [(1 - x, y, c), (x, 1 - y, c), (1 - x, 1 - y, c)]


def _chip_of(px, py):
    return 2 * px + py


HBM_SPEC = pl.BlockSpec(memory_space=pltpu.HBM)
SEM_SPEC = pl.BlockSpec(memory_space=pltpu.SEMAPHORE)
ANY_SPEC = pl.BlockSpec(memory_space=pl.ANY)
DATAFLOW = pltpu.SideEffectType.DATAFLOW_SIDE_EFFECTING
N_PEER_CHIPS = N_CHIPS - 1


def _hbm(a):
    return pltpu.with_memory_space_constraint(a, pltpu.HBM)


def _hbm_like(arrays):
    return [pltpu.HBM(a.shape, a.dtype) for a in arrays]


def cast_place(w, layer, me, out_dtype, name, tm=512, nslots=N_CHIPS, dep=None):
    rows, c = w.shape[-2:]
    tm = _row_tile(rows, tm)

    def body(me_ref, w_ref, *rest):
        rest[-1][...] = w_ref[...].astype(out_dtype)

    if layer is None:
        in_specs = [pl.BlockSpec((tm, c), lambda i, me_ref: (i, 0))]
    else:
        in_specs = [pl.BlockSpec((None, tm, c), lambda i, me_ref: (layer, i, 0))]
    args = [me, w]
    if dep is not None:
        in_specs.append(pl.BlockSpec(DEP_SPEC_SHAPE, lambda i, me_ref: (0, 0)))
        args.append(dep)
    return pl.pallas_call(
        body, name=name,
        grid_spec=pltpu.PrefetchScalarGridSpec(
            num_scalar_prefetch=1, grid=(rows // tm,), in_specs=in_specs,
            out_specs=pl.BlockSpec((None, tm, c), lambda i, me_ref: (me_ref[0], i, 0))),
        out_shape=jax.ShapeDtypeStruct((nslots, rows, c), out_dtype), compiler_params=_cp("parallel"),
    )(*args)


def gather_start(lands, chunk_sizes, name="gather_start"):
    n = len(lands)
    nch = len(chunk_sizes)
    assert sum(chunk_sizes) == n

    def body(*refs):
        land_refs = refs[:n]
        outs = refs[n:]
        send_sems, recv_sems = outs[:nch], outs[nch:2 * nch]
        token = outs[-1]
        x, y, c = _place()
        me = _chip_of(x, y)
        peers = _other_chips(x, y, c)
        k = 0
        for ck, size in enumerate(chunk_sizes):
            for pos in range(size):
                for r, peer in enumerate(peers):
                    pltpu.make_async_remote_copy(
                        src_ref=land_refs[k].at[me], dst_ref=land_refs[k].at[me],
                        send_sem=send_sems[ck].at[N_PEER_CHIPS * pos + r], recv_sem=recv_sems[ck].at[N_PEER_CHIPS * pos + r],
                        device_id=peer, device_id_type=MESH).start()
                k += 1
        token[...] = jnp.zeros(token.shape, F32)

    sems = [pltpu.SemaphoreType.DMA((N_PEER_CHIPS * s,)) for s in chunk_sizes]
    res = pl.pallas_call(
        body, name=name,
        out_shape=(*sems, *sems, *_hbm_like(lands), jax.ShapeDtypeStruct(DEP_SPEC_SHAPE, F32)),
        in_specs=[HBM_SPEC] * n,
        out_specs=(*[SEM_SPEC] * (2 * nch), *[HBM_SPEC] * n, pl.BlockSpec(memory_space=pltpu.VMEM)),
        input_output_aliases={k: 2 * nch + k for k in range(n)},
        compiler_params=pltpu.CompilerParams(has_side_effects=DATAFLOW),
    )(*[_hbm(a) for a in lands])
    return res[:nch], res[nch:2 * nch], res[2 * nch:2 * nch + n], res[-1]


def gather_wait(send_sem, recv_sem, lands, after, name):
    n = len(lands)

    def body(*refs):
        land_refs = refs[:n]
        ssem, rsem = refs[n], refs[n + 1]
        x, y, c = _place()
        me = _chip_of(x, y)
        for pos in range(n):
            for r, peer in enumerate(_other_chips(x, y, c)):
                cp = pltpu.make_async_remote_copy(
                    src_ref=land_refs[pos].at[me], dst_ref=land_refs[pos].at[_chip_of(peer[0], peer[1])],
                    send_sem=ssem.at[N_PEER_CHIPS * pos + r], recv_sem=rsem.at[N_PEER_CHIPS * pos + r],
                    device_id=peer, device_id_type=MESH)
                cp.wait_send()
                cp.wait_recv()

    return pl.pallas_call(
        body, name=name, out_shape=tuple(_hbm_like(lands)),
        in_specs=[*[HBM_SPEC] * n, SEM_SPEC, SEM_SPEC, ANY_SPEC], out_specs=[HBM_SPEC] * n,
        input_output_aliases={k: k for k in range(n)},
        compiler_params=pltpu.CompilerParams(has_side_effects=DATAFLOW),
    )(*lands, send_sem, recv_sem, after)


def scatter_start(grads, name):
    n = len(grads)
    recvs = [lax.empty((N_PEER_CHIPS, *g.shape[1:]), g.dtype) for g in grads]

    def body(*refs):
        g_refs, r_refs = refs[:n], refs[n:2 * n]
        send_sems, recv_sems = refs[2 * n], refs[2 * n + 1]
        token = refs[-1]
        x, y, c = _place()
        for k in range(n):
            for r, peer in enumerate(_other_chips(x, y, c)):
                pltpu.make_async_remote_copy(
                    src_ref=g_refs[k].at[_chip_of(peer[0], peer[1])], dst_ref=r_refs[k].at[r],
                    send_sem=send_sems.at[N_PEER_CHIPS * k + r], recv_sem=recv_sems.at[N_PEER_CHIPS * k + r],
                    device_id=peer, device_id_type=MESH).start()
        token[...] = jnp.zeros(token.shape, F32)

    sem = pltpu.SemaphoreType.DMA((N_PEER_CHIPS * n,))
    res = pl.pallas_call(
        body, name=name,
        out_shape=(sem, sem, *_hbm_like(grads), *_hbm_like(recvs), jax.ShapeDtypeStruct(DEP_SPEC_SHAPE, F32)),
        in_specs=[HBM_SPEC] * (2 * n),
        out_specs=(SEM_SPEC, SEM_SPEC, *[HBM_SPEC] * (2 * n), pl.BlockSpec(memory_space=pltpu.VMEM)),
        input_output_aliases={k: 2 + k for k in range(2 * n)},
        compiler_params=pltpu.CompilerParams(has_side_effects=DATAFLOW),
    )(*[_hbm(a) for a in grads], *[_hbm(a) for a in recvs])
    return res[0], res[1], res[2:2 + n], res[2 + n:2 + 2 * n], res[-1]


def scatter_wait(send_sem, recv_sem, grads, recvs, after, name):
    n = len(grads)

    def body(*refs):
        g_refs, r_refs = refs[:n], refs[n:2 * n]
        ssem, rsem = refs[2 * n], refs[2 * n + 1]
        x, y, c = _place()
        for k in range(n):
            for r, peer in enumerate(_other_chips(x, y, c)):
                cp = pltpu.make_async_remote_copy(
                    src_ref=g_refs[k].at[_chip_of(peer[0], peer[1])], dst_ref=r_refs[k].at[r],
                    send_sem=ssem.at[N_PEER_CHIPS * k + r], recv_sem=rsem.at[N_PEER_CHIPS * k + r],
                    device_id=peer, device_id_type=MESH)
                cp.wait_send()
                cp.wait_recv()

    res = pl.pallas_call(
        body, name=name, out_shape=(*_hbm_like(grads), *_hbm_like(recvs)),
        in_specs=[*[HBM_SPEC] * (2 * n), SEM_SPEC, SEM_SPEC, ANY_SPEC], out_specs=[HBM_SPEC] * (2 * n),
        input_output_aliases={k: k for k in range(2 * n)},
        compiler_params=pltpu.CompilerParams(has_side_effects=DATAFLOW),
    )(*grads, *recvs, send_sem, recv_sem, after)
    return res[:n], res[n:]


def swap_start(parts, name):
    n = len(parts)
    lands = [lax.empty(p.shape, p.dtype) for p in parts]

    def body(*refs):
        p_refs, l_refs = refs[:n], refs[n:2 * n]
        sems = refs[2 * n:4 * n]
        token = refs[-1]
        x, y, c = _place()
        for k in range(n):
            pltpu.make_async_remote_copy(
                src_ref=p_refs[k], dst_ref=l_refs[k], send_sem=sems[k], recv_sem=sems[n + k],
                device_id=(x, y, 1 - c), device_id_type=MESH).start()
        token[...] = jnp.zeros(token.shape, F32)

    sem = pltpu.SemaphoreType.DMA(())
    res = pl.pallas_call(
        body, name=name,
        out_shape=(*[sem] * (2 * n), *_hbm_like(parts), *_hbm_like(lands), jax.ShapeDtypeStruct(DEP_SPEC_SHAPE, F32)),
        in_specs=[HBM_SPEC] * (2 * n),
        out_specs=(*[SEM_SPEC] * (2 * n), *[HBM_SPEC] * (2 * n), pl.BlockSpec(memory_space=pltpu.VMEM)),
        input_output_aliases={k: 2 * n + k for k in range(2 * n)},
        compiler_params=pltpu.CompilerParams(has_side_effects=DATAFLOW),
    )(*[_hbm(a) for a in parts], *[_hbm(a) for a in lands])
    return res[:n], res[n:2 * n], res[2 * n:3 * n], res[3 * n:4 * n], res[-1]


def swap_wait(send_sem, recv_sem, part, land, after, name):
    def body(p_ref, l_ref, ssem, rsem, after_ref, p_out, l_out):
        x, y, c = _place()
        cp = pltpu.make_async_remote_copy(src_ref=p_ref, dst_ref=l_ref, send_sem=ssem, recv_sem=rsem,
                                          device_id=(x, y, 1 - c), device_id_type=MESH)
        cp.wait_send()
        cp.wait_recv()

    return pl.pallas_call(
        body, name=name, out_shape=tuple(_hbm_like([part, land])),
        in_specs=[HBM_SPEC, HBM_SPEC, SEM_SPEC, SEM_SPEC, ANY_SPEC], out_specs=[HBM_SPEC, HBM_SPEC],
        input_output_aliases={0: 0, 1: 1},
        compiler_params=pltpu.CompilerParams(has_side_effects=DATAFLOW),
    )(part, land, send_sem, recv_sem, after)


def _xor_peer(x, y, c, k):
    px, py, pc = x ^ ((k >> 2) & 1), y ^ ((k >> 1) & 1), c ^ (k & 1)
    return (px, py, pc), 4 * px + 2 * py + pc


def small_start(land, name="small_start"):
    def body(l_ref, ssem, rsem, l_out, token):
        x, y, c = _place()
        me = 4 * x + 2 * y + c
        for k in range(1, N_DEV):
            peer, _ = _xor_peer(x, y, c, k)
            pltpu.make_async_remote_copy(
                src_ref=l_ref.at[me], dst_ref=l_ref.at[me], send_sem=ssem.at[k - 1], recv_sem=rsem.at[k - 1],
                device_id=peer, device_id_type=MESH).start()
        token[...] = jnp.zeros(token.shape, F32)

    sem = pltpu.SemaphoreType.DMA((N_DEV - 1,))
    return pl.pallas_call(
        body, name=name,
        out_shape=(sem, sem, pltpu.HBM(land.shape, land.dtype), jax.ShapeDtypeStruct(DEP_SPEC_SHAPE, F32)),
        in_specs=[HBM_SPEC], out_specs=(SEM_SPEC, SEM_SPEC, HBM_SPEC, pl.BlockSpec(memory_space=pltpu.VMEM)),
        input_output_aliases={0: 2}, compiler_params=pltpu.CompilerParams(has_side_effects=DATAFLOW),
    )(_hbm(land))


def small_wait(send_sem, recv_sem, land, after, name="small_wait"):
    def body(l_ref, ssem, rsem, after_ref, l_out):
        x, y, c = _place()
        me = 4 * x + 2 * y + c
        for k in range(1, N_DEV):
            peer, slot = _xor_peer(x, y, c, k)
            cp = pltpu.make_async_remote_copy(
                src_ref=l_ref.at[me], dst_ref=l_ref.at[slot], send_sem=ssem.at[k - 1], recv_sem=rsem.at[k - 1],
                device_id=peer, device_id_type=MESH)
            cp.wait_send()
            cp.wait_recv()

    return pl.pallas_call(
        body, name=name, out_shape=pltpu.HBM(land.shape, land.dtype),
        in_specs=[HBM_SPEC, SEM_SPEC, SEM_SPEC, ANY_SPEC], out_specs=HBM_SPEC, input_output_aliases={0: 0},
        compiler_params=pltpu.CompilerParams(has_side_effects=DATAFLOW),
    )(land, send_sem, recv_sem, after)


def sum_slots(land, name="sum_slots", tm=256):
    n, rows, c = land.shape
    tm = _row_tile(rows, tm)

    def body(l_ref, o_ref):
        acc = l_ref[0]
        for j in range(1, n):
            acc = acc + l_ref[j]
        o_ref[...] = acc

    return pl.pallas_call(
        body, name=name, grid=(rows // tm,), in_specs=[pl.BlockSpec((n, tm, c), lambda i: (0, i, 0))],
        out_specs=pl.BlockSpec((tm, c), lambda i: (i, 0)), out_shape=jax.ShapeDtypeStruct((rows, c), F32),
        compiler_params=_cp("parallel"),
    )(land)


PACK_ROW_TILE = 256


def _pack(arrays):
    flat = jnp.concatenate([a.reshape(-1).astype(F32) for a in arrays])
    n = flat.shape[0]
    rows = -(-n // LANES)
    rows = -(-rows // PACK_ROW_TILE) * PACK_ROW_TILE
    return jnp.pad(flat, (0, rows * LANES - n)).reshape(rows, LANES)


def _unpack(packed, shapes, lead=()):
    flat = packed.reshape(*lead, -1)
    out, off = [], 0
    for shp in shapes:
        n = math.prod(shp)
        out.append(flat[..., off:off + n].reshape(*lead, *shp))
        off += n
    return out


def _row(vec):
    return vec.reshape(1, -1)


def kernel(x, mix_pre_g, mix_post_g, ffn_pre_g, ffn_post_g, cm_w_in, cm_b_in, cm_dw, cm_dw_b, cm_ln_g, cm_ln_b, cm_w_out, cm_b_out, kv_norm_g, w_kv, w_q, w_o, ffn_w_in, ffn_dw, ffn_dw_b, ffn_w_out, loss_target, m_mix_pre_g, m_mix_post_g, m_ffn_pre_g, m_ffn_post_g, m_cm_w_in, m_cm_b_in, m_cm_dw, m_cm_dw_b, m_cm_ln_g, m_cm_ln_b, m_cm_w_out, m_cm_b_out, m_kv_norm_g, m_w_kv, m_w_q, m_w_o, m_ffn_w_in, m_ffn_dw, m_ffn_dw_b, m_ffn_w_out, v_mix_pre_g, v_mix_post_g, v_ffn_pre_g, v_ffn_post_g, v_cm_w_in, v_cm_b_in, v_cm_dw, v_cm_dw_b, v_cm_ln_g, v_cm_ln_b, v_cm_w_out, v_cm_b_out, v_kv_norm_g, v_w_kv, v_w_q, v_w_o, v_ffn_w_in, v_ffn_dw, v_ffn_dw_b, v_ffn_w_out):
    names = ["mix_pre_g", "mix_post_g", "ffn_pre_g", "ffn_post_g", "cm_w_in", "cm_b_in", "cm_dw", "cm_dw_b", "cm_ln_g",
             "cm_ln_b", "cm_w_out", "cm_b_out", "kv_norm_g", "w_kv", "w_q", "w_o", "ffn_w_in", "ffn_dw", "ffn_dw_b",
             "ffn_w_out"]
    w_in = dict(zip(names, [mix_pre_g, mix_post_g, ffn_pre_g, ffn_post_g, cm_w_in, cm_b_in, cm_dw, cm_dw_b, cm_ln_g,
                            cm_ln_b, cm_w_out, cm_b_out, kv_norm_g, w_kv, w_q, w_o, ffn_w_in, ffn_dw, ffn_dw_b, ffn_w_out]))
    m_in = dict(zip(names, [m_mix_pre_g, m_mix_post_g, m_ffn_pre_g, m_ffn_post_g, m_cm_w_in, m_cm_b_in, m_cm_dw, m_cm_dw_b,
                            m_cm_ln_g, m_cm_ln_b, m_cm_w_out, m_cm_b_out, m_kv_norm_g, m_w_kv, m_w_q, m_w_o, m_ffn_w_in,
                            m_ffn_dw, m_ffn_dw_b, m_ffn_w_out]))
    v_in = dict(zip(names, [v_mix_pre_g, v_mix_post_g, v_ffn_pre_g, v_ffn_post_g, v_cm_w_in, v_cm_b_in, v_cm_dw, v_cm_dw_b,
                            v_cm_ln_g, v_cm_ln_b, v_cm_w_out, v_cm_b_out, v_kv_norm_g, v_w_kv, v_w_q, v_w_o, v_ffn_w_in,
                            v_ffn_dw, v_ffn_dw_b, v_ffn_w_out]))

    bsz, seq, d = x.shape
    t = bsz * seq
    n_b = DEPTH - N_A
    hw = w_o.shape[-1]
    qw = N_GROUPS * hw
    f2 = ffn_dw_b.shape[-1]
    f = f2 // 2
    me_chip = _chip_of(lax.axis_index("x"), lax.axis_index("y"))

    big = ["cm_w_in", "cm_w_out", "w_kv", "w_q", "w_o", "ffn_w_in", "ffn_w_out"]
    row_sharded = ("cm_w_out", "w_o", "ffn_w_out")
    small_sharded = ["cm_b_in", "cm_dw", "cm_dw_b", "cm_ln_g", "cm_ln_b", "cm_b_out", "ffn_dw"]
    small_pack = _pack([w_in[n] for n in small_sharded])
    chunks = [
        [("cm_w_in", 0), ("small", None)],
        [("cm_w_out", 0)],
        [("ffn_w_in", 0), ("ffn_w_out", 0)],
        [("cm_w_in", 1), ("cm_w_out", 1)],
        [("ffn_w_in", 1), ("ffn_w_out", 1)],
        [("w_kv", None)],
        [("w_q", 0), ("w_o", 0)],
        [("ffn_w_in", 2), ("ffn_w_out", 2)],
        [("w_q", 1), ("w_o", 1)],
        [("ffn_w_in", 3), ("ffn_w_out", 3)],
    ]
    chunk_of = {pc: ck for ck, ch in enumerate(chunks) for pc in ch}

    me_arr = me_chip.astype(jnp.int32).reshape(1)

    def land_of(pc, dep=None):
        n, l = pc
        if n == "small":
            return cast_place(small_pack, None, me_arr, F32, name="place_small", dep=dep)
        return cast_place(w_in[n], l, me_arr, BF16, name=f"place_{n}_{l}", dep=dep)

    groups = [[0, 1], [2], list(range(3, len(chunks)))]
    g_send, g_recv, lands_f = {}, {}, {}

    def start_group(gi, dep, name):
        cks = groups[gi]
        pcs = [pc for ck in cks for pc in chunks[ck]]
        lands = [land_of(pc, dep if k == 0 else None) for k, pc in enumerate(pcs)]
        send, recv, lands_thru, tok = gather_start(lands, [len(chunks[ck]) for ck in cks], name=name)
        pos = 0
        for j, ck in enumerate(cks):
            g_send[ck], g_recv[ck] = send[j], recv[j]
            lands_f[ck] = lands_thru[pos:pos + len(chunks[ck])]
            pos += len(chunks[ck])
        return tok

    token_a = start_group(0, None, "gather_start_a")
    token = start_group(1, token_a, "gather_start_b")
    weights = {}

    def finish_chunk(ck, after):
        got = gather_wait(g_send[ck], g_recv[ck], lands_f[ck], after, name=f"gather_wait{ck}")
        for pc, arr in zip(chunks[ck], got):
            weights[pc] = arr.reshape(1, -1, arr.shape[-1]) if pc[0] in row_sharded else arr

    def wmat(n, l=None, after=None):
        if (n, l) not in weights:
            finish_chunk(chunk_of[(n, l)], after)
        arr = weights[(n, l)]
        return arr, arr.shape[0]

    finish_chunk(0, token)
    small_full = {}
    for n, arr4 in zip(small_sharded, _unpack(weights[("small", None)], [w_in[n].shape for n in small_sharded], lead=(N_CHIPS,))):
        shp = w_in[n].shape
        small_full[n] = jnp.moveaxis(arr4, 0, -2).reshape(*shp[:-1], N_CHIPS * shp[-1])

    x2d = x.reshape(t, d)
    saved = []
    (h1,) = resid_norm_fwd(x2d, None, None, [_row(mix_pre_g[0])], name="norm_in", dep=token)
    xcur = x2d
    kv_state = None
    for i in range(DEPTH):
        sv = {"x_in": xcur, "h1": h1}
        if i < N_A:
            z = mm_nn(h1, *wmat("cm_w_in", i, h1), bias=_row(small_full["cm_b_in"][i]), name=f"cm_in{i}",
                      spg=N_CHIPS)
            u2 = glu_conv_fwd(z.reshape(bsz, seq, 2 * d), small_full["cm_dw"][i], _row(small_full["cm_dw_b"][i]),
                              name=f"glu_conv{i}").reshape(t, d)
            u4 = ln_silu_fwd(u2, _row(small_full["cm_ln_g"][i]), _row(small_full["cm_ln_b"][i]), name=f"ln_silu{i}")
            y = mm_nn(u4, *wmat("cm_w_out", i, u4), bias=_row(small_full["cm_b_out"][i]), out_dtype=BF16,
                      name=f"cm_out{i}")
            sv.update(z=z, u2=u2, u4=u4)
        else:
            j = i - N_A
            q = mm_nn(h1, *wmat("w_q", j, h1), name=f"q_proj{j}", head_major=True, spg=N_CHIPS).reshape(-1, bsz, seq, HEAD_DIM)
            outs, lses = [], []
            for g, dil in enumerate(DILATIONS):
                o_g, l_g = attn_fwd(q, kv_state["kv"], g, dil, hw, name=f"attn_fwd{j}_{g}")
                outs.append(o_g.reshape(-1, t, HEAD_DIM))
                lses.append(l_g.reshape(t, LANES))
            merged, lsej = attn_merge(outs, lses, name=f"attn_merge{j}")
            y = mm_nn(merged, *wmat("w_o", j, merged), out_dtype=BF16, name=f"o_proj{j}")
            sv.update(q=q, merged=merged, lsej=lsej)
        x1, h2 = resid_norm_fwd(xcur, y, _row(mix_post_g[i]), [_row(ffn_pre_g[i])], name=f"resid_mix{i}")
        ffn_after = start_group(2, x1, "gather_start_c") if i == 0 else h2
        p = mm_nn(h2, *wmat("ffn_w_in", i, ffn_after), out_dtype=BF16, name=f"ffn_in{i}", spg=2)
        s_act = ffn_mid_fwd(p.reshape(bsz, seq, f2), small_full["ffn_dw"][i], _row(ffn_dw_b[i]), name=f"ffn_mid{i}").reshape(t, f)
        y2 = mm_nn(s_act, *wmat("ffn_w_out", i), out_dtype=BF16, name=f"ffn_out{i}")
        next_gains = []
        if i + 1 < DEPTH:
            next_gains.append(_row(mix_pre_g[i + 1]))
        if i == N_A - 1:
            next_gains.append(_row(kv_norm_g))
        res = resid_norm_fwd(x1, y2, _row(ffn_post_g[i]), next_gains, name=f"resid_ffn{i}")
        sv.update(y=y, x1=x1, h2=h2, p=p, s=s_act, y2=y2)
        saved.append(sv)
        xcur = res[0]
        if i + 1 < DEPTH:
            h1 = res[1]
        if i == N_A - 1:
            kvn = res[2]
            kv = mm_nn(kvn, *wmat("w_kv", None, kvn), name="kv_proj", head_major=True, spg=2).reshape(-1, bsz, seq, HEAD_DIM)
            kv_state = {"kv": kv, "kvn": kvn, "x_a": xcur}

    dx, loss_tile = loss_fwd_bwd(xcur, loss_target.reshape(t, d))
    loss = lax.psum(loss_tile[0, 0], ("x", "y", "c"))

    gsm = {n: [None] * w_in[n].shape[0] for n in
           ["mix_pre_g", "mix_post_g", "ffn_pre_g", "ffn_post_g", "cm_b_in", "cm_dw", "cm_dw_b", "cm_ln_g", "cm_ln_b",
            "cm_b_out", "ffn_dw", "ffn_dw_b"]}
    gbig = {}
    in_flight = []
    dep = None

    def start_scatter(pcs, tag):
        ssem, rsem, g_f, r_f, tok = scatter_start([gbig[pc] for pc in pcs], name=f"scatter_start_{tag}")
        in_flight.append((pcs, ssem, rsem, g_f, r_f))
        return tok

    dk_buf = dv_buf = None
    for i in range(DEPTH - 1, -1, -1):
        sv = saved[i]
        dy2, dg, _ = norm_bwd(sv["y2"], _row(ffn_post_g[i]), dx, out_dtype=BF16, name=f"bwd_ffn_post{i}", dep=dep)
        gsm["ffn_post_g"][i] = dg
        ds = mm_nt(dy2, *wmat("ffn_w_out", i), out_dtype=BF16, name=f"bwd_ffn_out_dx{i}")
        gbig[("ffn_w_out", i)] = mm_tn(sv["s"], dy2, 1, name=f"bwd_ffn_out_dw{i}").reshape(N_CHIPS, f // N_CHIPS, d)
        dpa, dpg, ddwa, ddwg, ddba, ddbg = ffn_mid_bwd(sv["p"].reshape(bsz, seq, f2), small_full["ffn_dw"][i], _row(ffn_dw_b[i]),
                                                       ds.reshape(bsz, seq, f), name=f"bwd_ffn_mid{i}")
        gsm["ffn_dw"][i] = jnp.concatenate([jnp.sum(ddwa, axis=1), jnp.sum(ddwg, axis=1)], axis=-1)
        gsm["ffn_dw_b"][i] = jnp.concatenate([ddba, ddbg], axis=-1)
        dp = [dpa.reshape(t, f), dpg.reshape(t, f)]
        dh2 = mm_nt(dp, *wmat("ffn_w_in", i), out_dtype=BF16, name=f"bwd_ffn_in_dx{i}", spg=2)
        gbig[("ffn_w_in", i)] = mm_tn(sv["h2"], dp, N_CHIPS, name=f"bwd_ffn_in_dw{i}", tm=2048)
        dx1, dg, _ = norm_bwd(sv["x1"], _row(ffn_pre_g[i]), dh2, add=dx, name=f"bwd_ffn_pre{i}")
        gsm["ffn_pre_g"][i] = dg
        dep = start_scatter([("ffn_w_in", 0), ("ffn_w_out", 0)], "ffn0") if i == 0 else None
        dy, dg, dbias = norm_bwd(sv["y"], _row(mix_post_g[i]), dx1, out_dtype=BF16, name=f"bwd_mix_post{i}", dep=dep)
        gsm["mix_post_g"][i] = dg
        if i < N_A:
            gsm["cm_b_out"][i] = dbias
            du4 = mm_nt(dy, *wmat("cm_w_out", i), out_dtype=BF16, name=f"bwd_cm_out_dx{i}")
            gbig[("cm_w_out", i)] = mm_tn(sv["u4"], dy, 1, name=f"bwd_cm_out_dw{i}").reshape(N_CHIPS, d // N_CHIPS, d)
            du2, dlg, dlb = ln_silu_bwd(sv["u2"], _row(small_full["cm_ln_g"][i]), _row(small_full["cm_ln_b"][i]), du4,
                                        name=f"bwd_ln_silu{i}")
            gsm["cm_ln_g"][i], gsm["cm_ln_b"][i] = dlg, dlb
            dza, dzg, ddw, ddwb, dba, dbg = glu_conv_bwd(sv["z"].reshape(bsz, seq, 2 * d), small_full["cm_dw"][i],
                                                         du2.reshape(bsz, seq, d), name=f"bwd_glu_conv{i}")
            gsm["cm_dw"][i] = jnp.sum(ddw, axis=1)
            gsm["cm_dw_b"][i] = ddwb
            gsm["cm_b_in"][i] = jnp.concatenate([dba, dbg], axis=-1)
            dz = [dza.reshape(t, d), dzg.reshape(t, d)]
            dh1 = mm_nt(dz, *wmat("cm_w_in", i), out_dtype=BF16, name=f"bwd_cm_in_dx{i}", spg=2)
            gbig[("cm_w_in", i)] = mm_tn(sv["h1"], dz, N_CHIPS, name=f"bwd_cm_in_dw{i}", spg=2)
        else:
            j = i - N_A
            dmerged = mm_nt(dy, *wmat("w_o", j), name=f"bwd_o_proj_dx{j}")
            gbig[("w_o", j)] = mm_tn(sv["merged"], dy, 1, name=f"bwd_o_proj_dw{j}").reshape(N_CHIPS, hw // N_CHIPS, d)
            dmt = attn_bwd_prep(dmerged, sv["merged"], name=f"bwd_attn_prep{j}")
            dq_buf = None
            add_to_kv = dk_buf is not None
            for g, dil in enumerate(DILATIONS):
                dq_buf, dk_buf, dv_buf = attn_bwd(
                    sv["q"], kv_state["kv"], g, dil, dmerged.reshape(bsz, seq, hw), sv["lsej"].reshape(bsz, seq, LANES),
                    dmt.reshape(bsz, seq, LANES), dq_buf, dk_buf, dv_buf, add_to_kv, hw, name=f"attn_bwd{j}_{g}")
            dq = dq_buf.reshape(t, qw)
            dh1 = mm_nt(dq, *wmat("w_q", j), out_dtype=BF16, name=f"bwd_q_proj_dx{j}", tm=512, spg=N_CHIPS)
            gbig[("w_q", j)] = mm_tn(sv["h1"], dq, N_CHIPS, name=f"bwd_q_proj_dw{j}", tm=512, spg=N_CHIPS)
        dx, dg, _ = norm_bwd(sv["x_in"], _row(mix_pre_g[i]), dh1, add=dx1, name=f"bwd_mix_pre{i}")
        gsm["mix_pre_g"][i] = dg
        if i > N_A:
            dep = start_scatter([("ffn_w_in", i), ("ffn_w_out", i), ("w_q", i - N_A), ("w_o", i - N_A)], f"l{i}")
        elif 0 < i < N_A:
            dep = start_scatter([("ffn_w_in", i), ("ffn_w_out", i), ("cm_w_in", i), ("cm_w_out", i)], f"l{i}")
        elif i == 0:
            last_token = start_scatter([("cm_w_in", 0), ("cm_w_out", 0)], "cm0")
        if i == N_A:
            dkv = [dk_buf.reshape(t, qw), dv_buf.reshape(t, qw)]
            dkvn = mm_nt(dkv, *wmat("w_kv"), out_dtype=BF16, name="bwd_kv_proj_dx")
            gbig[("w_kv", None)] = mm_tn(kv_state["kvn"], dkv, N_CHIPS, name="bwd_kv_proj_dw")
            dx, dg_kv, _ = norm_bwd(kv_state["x_a"], _row(kv_norm_g), dkvn, add=dx, name="bwd_kv_norm")
            dep = start_scatter([("ffn_w_in", i), ("ffn_w_out", i), ("w_q", 0), ("w_o", 0), ("w_kv", None)], f"l{i}")
    grad_x = dx.reshape(bsz, seq, d)

    plane_of = {}
    outs_g, outs_d, outs_m, outs_v = {}, {}, {}, {}

    def finish_scatter(k, after):
        pcs, ssem, rsem, g_f, r_f = in_flight[k]
        g_done, r_done = scatter_wait(ssem, rsem, g_f, r_f, after, name=f"scatter_wait{k}")
        for (n, l), g_arr, r_arr in zip(pcs, g_done, r_done):
            n_layers = 1 if l is None else w_in[n].shape[0]
            plane_of[n] = sum_parts(g_arr, r_arr, me_arr, name=f"sum_chips_{n}_{l}", layer=l or 0, n_layers=n_layers,
                                    out_buf=plane_of.get(n))

    def update(group, tag, after):
        plane = [plane_of[n] for n in group]
        ssems, rsems, plane_f, land_f, _ = swap_start(plane, name=f"swap_start_{tag}")
        for k, n in enumerate(group):
            p_mine, p_other = swap_wait(ssems[k], rsems[k], plane_f[k], land_f[k], after, name=f"swap_wait_{n}")
            shp = w_in[n].shape
            flat = lambda a: a.reshape(-1, shp[-1])
            g_, d_, m_, v_ = adamw(flat(w_in[n]), flat(m_in[n]), flat(v_in[n]), [p_mine, p_other], name=f"adamw_{n}")
            outs_g[n], outs_d[n], outs_m[n], outs_v[n] = (a.reshape(shp) for a in (g_, d_, m_, v_))
            after = v_
        return after

    small_names = [n for n in names if n not in big]
    small_shapes_full = {}
    small_grads_full = []
    for n in small_names:
        if n == "kv_norm_g":
            gfull = dg_kv.reshape(-1)
        elif n in ("cm_dw", "ffn_dw"):
            gfull = jnp.stack(gsm[n], axis=0)
        else:
            gfull = jnp.stack([a.reshape(-1) for a in gsm[n]], axis=0)
        small_shapes_full[n] = gfull.shape
        small_grads_full.append(gfull)
    dev_arr = (4 * lax.axis_index("x") + 2 * lax.axis_index("y") + lax.axis_index("c")).astype(jnp.int32).reshape(1)
    small_land = cast_place(_pack(small_grads_full) + last_token[0, 0], None, dev_arr, F32, name="place_small_grads",
                            nslots=N_DEV)
    sm_send, sm_recv, small_land, small_token = small_start(small_land)

    for k in range(len(in_flight) - 1):
        finish_scatter(k, small_token)
    done = update(["w_kv", "w_q", "w_o", "ffn_w_in", "ffn_w_out"], "a", small_token)
    finish_scatter(len(in_flight) - 1, done)
    done = update(["cm_w_in", "cm_w_out"], "b", done)

    summed = sum_slots(small_wait(sm_send, sm_recv, small_land, done))
    g_full = dict(zip(small_names, _unpack(summed, [small_shapes_full[n] for n in small_names])))
    g_loc = {}
    for n in small_names:
        if n in small_sharded:
            width = w_in[n].shape[-1]
            g_loc[n] = lax.dynamic_slice_in_dim(g_full[n], me_chip * width, width, axis=g_full[n].ndim - 1)
        else:
            g_loc[n] = g_full[n]
    res = adamw(_pack([w_in[n] for n in small_names]), _pack([m_in[n] for n in small_names]),
                _pack([v_in[n] for n in small_names]), [_pack([g_loc[n] for n in small_names])], name="adamw_small")
    shapes_loc = [w_in[n].shape for n in small_names]
    for dst, packed in zip((outs_g, outs_d, outs_m, outs_v), res):
        for n, a in zip(small_names, _unpack(packed, shapes_loc)):
            dst[n] = a

    return (loss, grad_x, *[outs_g[n] for n in names], *[outs_d[n] for n in names],
            *[outs_m[n] for n in names], *[outs_v[n] for n in names])
```

```python
import math

import jax
import jax.numpy as jnp
from jax import lax
from jax.experimental import pallas as pl
from jax.experimental.pallas import tpu as pltpu

F32 = jnp.float32
BF16 = jnp.bfloat16
EPS = 1e-6
NEG_INF = -1e30
N_A = 2
DEPTH = 4
N_GROUPS = 3
DILATIONS = (1, 4, 16)
HEAD_DIM = 128
BLK = 128
LANES = 128
N_CHIPS = 4
N_DEV = 8
VMEM_LIMIT_V7X = 56 * 1024 * 1024

ADAM_LR = 0.001
ADAM_B1 = 0.9
ADAM_B2 = 0.999
ADAM_EPS = 1e-08
ADAM_WD = 0.01
ADAM_STEP = 10

MESH = pl.DeviceIdType.MESH


def _cp(*sem, **kw):
    return pltpu.CompilerParams(dimension_semantics=sem if sem else None, vmem_limit_bytes=VMEM_LIMIT_V7X, **kw)


def _dot(a, b):
    return jnp.dot(a, b, preferred_element_type=F32)


def _dot_nt(a, b):
    return lax.dot_general(a, b, (((1,), (1,)), ((), ())), preferred_element_type=F32)


def _dot_tn(a, b):
    return lax.dot_general(a, b, (((0,), (0,)), ((), ())), preferred_element_type=F32)


def _sigmoid(x):
    return 1.0 / (1.0 + jnp.exp(-x))


def _row_tile(n, want):
    if n <= want:
        return n
    for t in range(want - want % 8, 7, -8):
        if n % t == 0:
            return t
    raise ValueError(f"no row tile for {n} rows")


def mm_nn(a, w, nsh, bias=None, out_dtype=F32, name="mm_nn", tm=1024, head_major=False, spg=1):
    assert nsh % spg == 0
    m, k = a.shape
    _, k2, ns = w.shape
    assert k == k2
    tm = _row_tile(m, tm)
    has_bias = bias is not None
    hps = ns // HEAD_DIM

    def body(*refs):
        if has_bias:
            a_ref, w_ref, b_ref, o_ref = refs
        else:
            a_ref, w_ref, o_ref = refs
        av = a_ref[...].astype(BF16)
        for jj in range(spg):
            acc = _dot(av, w_ref[jj])
            if has_bias:
                acc = acc + b_ref[:, jj * ns:(jj + 1) * ns]
            if head_major:
                for hh in range(hps):
                    o_ref[jj * hps + hh] = acc[:, hh * HEAD_DIM:(hh + 1) * HEAD_DIM].astype(out_dtype)
            else:
                o_ref[:, jj * ns:(jj + 1) * ns] = acc.astype(out_dtype)

    in_specs = [
        pl.BlockSpec((tm, k), lambda j, i: (i, 0)),
        pl.BlockSpec((spg, k, ns), lambda j, i: (j, 0, 0)),
    ]
    args = [a, w]
    if has_bias:
        in_specs.append(pl.BlockSpec((1, spg * ns), lambda j, i: (0, j)))
        args.append(bias)
    return pl.pallas_call(
        body,
        name=name,
        grid=(nsh // spg, m // tm),
        in_specs=in_specs,
        out_specs=(pl.BlockSpec((spg * hps, tm, HEAD_DIM), lambda j, i: (j, i, 0)) if head_major
                   else pl.BlockSpec((tm, spg * ns), lambda j, i: (i, j))),
        out_shape=jax.ShapeDtypeStruct((nsh * hps, m, HEAD_DIM) if head_major else (m, nsh * ns), out_dtype),
        compiler_params=_cp("parallel", "parallel"),
    )(*args)


def _split_parts(dy, nsh, spg):
    dys = list(dy) if isinstance(dy, (list, tuple)) else [dy]
    per = nsh // len(dys)
    assert per % spg == 0
    return dys, per, per // spg


def mm_nt(dy, w, nsh, out_dtype=F32, name="mm_nt", tm=1024, spg=1):
    dys, per, gpp = _split_parts(dy, nsh, spg)
    npart = len(dys)
    m = dys[0].shape[0]
    _, k, ns = w.shape
    assert all(d.shape == (m, per * ns) for d in dys)
    tm = _row_tile(m, tm)
    ngrp = nsh // spg

    def group_product(dy_ref, w_ref):
        part = _dot_nt(dy_ref[:, 0:ns].astype(BF16), w_ref[0])
        for jj in range(1, spg):
            part = part + _dot_nt(dy_ref[:, jj * ns:(jj + 1) * ns].astype(BF16), w_ref[jj])
        return part

    def body(*refs):
        dy_refs = refs[:npart]
        w_ref, o_ref, acc_ref = refs[npart:]
        if ngrp == 1:
            o_ref[...] = group_product(dy_refs[0], w_ref).astype(out_dtype)
            return
        j = pl.program_id(1)

        @pl.when(j == 0)
        def _():
            acc_ref[...] = jnp.zeros(acc_ref.shape, F32)

        for pi in range(npart):
            @pl.when(j // gpp == pi)
            def _(pi=pi):
                acc_ref[...] += group_product(dy_refs[pi], w_ref)

        @pl.when(j == ngrp - 1)
        def _():
            o_ref[...] = acc_ref[...].astype(out_dtype)

    dy_specs = [pl.BlockSpec((tm, spg * ns), lambda i, j, pi=pi: (i, jnp.clip(j - pi * gpp, 0, gpp - 1)))
                for pi in range(npart)]
    return pl.pallas_call(
        body,
        name=name,
        grid=(m // tm, ngrp),
        in_specs=[*dy_specs, pl.BlockSpec((spg, k, ns), lambda i, j: (j, 0, 0))],
        out_specs=pl.BlockSpec((tm, k), lambda i, j: (i, 0)),
        out_shape=jax.ShapeDtypeStruct((m, k), out_dtype),
        scratch_shapes=[pltpu.VMEM((tm, k) if ngrp > 1 else (8, LANES), F32)],
        compiler_params=_cp("parallel", "arbitrary"),
    )(*dys, w)


def mm_tn(a, dy, nsh, name="mm_tn", tm=1024, spg=1):
    dys, per, gpp = _split_parts(dy, nsh, spg)
    npart = len(dys)
    m, k = a.shape
    ns = dys[0].shape[1] // per
    assert all(d.shape == (m, per * ns) for d in dys)
    tm = _row_tile(m, tm)
    nt = m // tm

    def body(*refs):
        a_ref = refs[0]
        dy_refs = refs[1:1 + npart]
        o_ref, acc_ref = refs[1 + npart:]
        j = pl.program_id(0)
        i = pl.program_id(1)

        @pl.when(i == 0)
        def _():
            acc_ref[...] = jnp.zeros(acc_ref.shape, F32)

        for pi in range(npart):
            @pl.when(j // gpp == pi)
            def _(pi=pi):
                at = a_ref[...].astype(BF16).T
                for jj in range(spg):
                    acc_ref[jj] += _dot(at, dy_refs[pi][:, jj * ns:(jj + 1) * ns].astype(BF16))

        @pl.when(i == nt - 1)
        def _():
            o_ref[...] = acc_ref[...].astype(BF16)

    dy_specs = [
        pl.BlockSpec((tm, spg * ns),
                     lambda j, i, pi=pi: (jnp.where(j // gpp == pi, i, 0), jnp.clip(j - pi * gpp, 0, gpp - 1)))
        for pi in range(npart)
    ]
    return pl.pallas_call(
        body,
        name=name,
        grid=(nsh // spg, nt),
        in_specs=[pl.BlockSpec((tm, k), lambda j, i: (i, 0)), *dy_specs],
        out_specs=pl.BlockSpec((spg, k, ns), lambda j, i: (j, 0, 0)),
        out_shape=jax.ShapeDtypeStruct((nsh, k, ns), BF16),
        scratch_shapes=[pltpu.VMEM((spg, k, ns), F32)],
        compiler_params=_cp("parallel", "arbitrary"),
    )(a, *dys)


DEP_SPEC_SHAPE = (8, LANES)


def resid_norm_fwd(x, y, g_post, next_gains, name, tm=1024, dep=None):
    t, d = x.shape
    tm = _row_tile(t, tm)
    has_y = y is not None
    n_next = len(next_gains)
    n_dep = 0 if dep is None else 1

    def body(*refs):
        x_ref = refs[0]
        pos = 1
        if has_y:
            y_ref, gp_ref = refs[1], refs[2]
            pos = 3
        gn_refs = refs[pos:pos + n_next]
        outs = refs[pos + n_next + n_dep:]
        xv = x_ref[...]
        o = 0
        if has_y:
            yv = y_ref[...].astype(F32)
            r = lax.rsqrt(jnp.mean(yv * yv, axis=-1, keepdims=True) + EPS)
            xv = xv + (yv * r) * gp_ref[...]
            outs[0][...] = xv
            o = 1
        if n_next:
            xn = xv * lax.rsqrt(jnp.mean(xv * xv, axis=-1, keepdims=True) + EPS)
            for k in range(n_next):
                outs[o + k][...] = (xn * gn_refs[k][...]).astype(BF16)

    row = pl.BlockSpec((tm, d), lambda i: (i, 0))
    vec = pl.BlockSpec((1, d), lambda i: (0, 0))
    args, in_specs = [x], [row]
    if has_y:
        args += [y, g_post]
        in_specs += [row, vec]
    args += list(next_gains)
    in_specs += [vec] * n_next
    if n_dep:
        args.append(dep)
        in_specs.append(pl.BlockSpec(DEP_SPEC_SHAPE, lambda i: (0, 0)))
    out_shape, out_specs = [], []
    if has_y:
        out_shape.append(jax.ShapeDtypeStruct((t, d), F32))
        out_specs.append(row)
    for _ in range(n_next):
        out_shape.append(jax.ShapeDtypeStruct((t, d), BF16))
        out_specs.append(row)
    return pl.pallas_call(
        body, name=name, grid=(t // tm,), in_specs=in_specs, out_specs=out_specs, out_shape=out_shape,
        compiler_params=_cp("parallel"),
    )(*args)


def norm_bwd(x, g, dy, add=None, out_dtype=F32, name="norm_bwd", tm=1024, dep=None):
    t, d = x.shape
    tm = _row_tile(t, tm)
    has_add = add is not None

    def body(*refs):
        x_ref, g_ref, dy_ref = refs[:3]
        add_ref = refs[3] if has_add else None
        dx_ref, dg_ref, cs_ref = refs[-3:]
        i = pl.program_id(0)
        xv = x_ref[...].astype(F32)
        dyv = dy_ref[...].astype(F32)
        r = lax.rsqrt(jnp.mean(xv * xv, axis=-1, keepdims=True) + EPS)
        gd = dyv * g_ref[...]
        dx = r * gd - xv * ((r * r * r) * jnp.mean(xv * gd, axis=-1, keepdims=True))
        if has_add:
            dx = dx + add_ref[...]
        dx_ref[...] = dx.astype(out_dtype)
        dg = jnp.sum(dyv * (xv * r), axis=0, keepdims=True)
        cs = jnp.sum(dx, axis=0, keepdims=True)

        @pl.when(i == 0)
        def _():
            dg_ref[...] = dg
            cs_ref[...] = cs

        @pl.when(i > 0)
        def _():
            dg_ref[...] += dg
            cs_ref[...] += cs

    row = pl.BlockSpec((tm, d), lambda i: (i, 0))
    vec = pl.BlockSpec((1, d), lambda i: (0, 0))
    args, in_specs = [x, g, dy], [row, vec, row]
    if has_add:
        args.append(add)
        in_specs.append(row)
    if dep is not None:
        args.append(dep)
        in_specs.append(pl.BlockSpec(DEP_SPEC_SHAPE, lambda i: (0, 0)))
    return pl.pallas_call(
        body, name=name, grid=(t // tm,), in_specs=in_specs,
        out_specs=[row, vec, vec],
        out_shape=[jax.ShapeDtypeStruct((t, d), out_dtype), jax.ShapeDtypeStruct((1, d), F32),
                   jax.ShapeDtypeStruct((1, d), F32)],
        compiler_params=_cp("arbitrary"),
    )(*args)


def loss_fwd_bwd(x, target, name="loss", tm=1024):
    t, d = x.shape
    tm = _row_tile(t, tm)

    def body(x_ref, t_ref, dx_ref, l_ref):
        i = pl.program_id(0)
        err = x_ref[...] - t_ref[...]
        dx_ref[...] = err * (1.0 / d)
        part = 0.5 * jnp.sum(jnp.mean(err * err, axis=-1, keepdims=True), axis=0, keepdims=True)
        part = jnp.broadcast_to(part, l_ref.shape)

        @pl.when(i == 0)
        def _():
            l_ref[...] = part

        @pl.when(i > 0)
        def _():
            l_ref[...] += part

    row = pl.BlockSpec((tm, d), lambda i: (i, 0))
    return pl.pallas_call(
        body, name=name, grid=(t // tm,), in_specs=[row, row],
        out_specs=[row, pl.BlockSpec((8, LANES), lambda i: (0, 0))],
        out_shape=[jax.ShapeDtypeStruct((t, d), F32), jax.ShapeDtypeStruct((8, LANES), F32)],
        compiler_params=_cp("arbitrary"),
    )(x, target)


CONV_HALO = 32
CONV_CHUNK = 128


def glu_conv_fwd(z, dw, dwb, name, tc=128):
    b, s, c2 = z.shape
    c = c2 // 2
    kw = dw.shape[0]
    tc = min(tc, c)
    nc = c // tc
    ch = min(CONV_CHUNK, s)
    halo = CONV_HALO
    assert kw - 1 <= halo and s % ch == 0

    nch = s // ch

    def body(a_ref, g_ref, w_ref, b_ref, o_ref, pad_ref):
        _fill_glu_slabs(a_ref, g_ref, pad_ref, nch, ch, halo)

        def chunk(ci, carry):
            r0 = pl.multiple_of(ci * ch, ch)
            acc = b_ref[...]
            for k, tap in enumerate(_taps_front(pad_ref, ci, kw, ch, halo)):
                acc = acc + w_ref[k:k + 1, :] * tap
            o_ref[pl.ds(r0, ch), :] = acc
            return carry

        lax.fori_loop(0, nch, chunk, 0)

    return pl.pallas_call(
        body, name=name, grid=(b, nc),
        in_specs=[
            pl.BlockSpec((None, s, tc), lambda bi, i: (bi, 0, i)),
            pl.BlockSpec((None, s, tc), lambda bi, i: (bi, 0, i + nc)),
            pl.BlockSpec((kw, tc), lambda bi, i: (0, i)),
            pl.BlockSpec((1, tc), lambda bi, i: (0, i)),
        ],
        out_specs=pl.BlockSpec((None, s, tc), lambda bi, i: (bi, 0, i)),
        out_shape=jax.ShapeDtypeStruct((b, s, c), F32),
        scratch_shapes=[pltpu.VMEM((nch, ch + halo, tc), F32)],
        compiler_params=_cp("parallel", "parallel"),
    )(z, z, dw, dwb)


def glu_conv_bwd(z, dw, du2, name, tc=128):
    b, s, c2 = z.shape
    c = c2 // 2
    kw = dw.shape[0]
    tc = min(tc, c)
    nc = c // tc
    ch = min(CONV_CHUNK, s)
    nch = s // ch
    halo = CONV_HALO

    def body(a_ref, g_ref, w_ref, du_ref, dza_ref, dzg_ref, ddw_ref, ddwb_ref, dba_ref, dbg_ref, upad_ref, dpad_ref):
        bi = pl.program_id(1)

        @pl.when(bi == 0)
        def _():
            ddw_ref[...] = jnp.zeros(ddw_ref.shape, F32)
            ddwb_ref[...] = jnp.zeros(ddwb_ref.shape, F32)
            dba_ref[...] = jnp.zeros(dba_ref.shape, F32)
            dbg_ref[...] = jnp.zeros(dbg_ref.shape, F32)

        _fill_glu_slabs(a_ref, g_ref, upad_ref, nch, ch, halo)
        dpad_ref[nch - 1, ch:ch + halo, :] = jnp.zeros((halo, tc), F32)
        dpad_ref[nch - 1, 0:ch, :] = du_ref[s - ch:s, :]

        def fill(ci, carry):
            r0 = pl.multiple_of(ci * ch, ch)
            dpad_ref[ci, :, :] = du_ref[pl.ds(r0, ch + halo), :]
            return carry

        lax.fori_loop(0, nch - 1, fill, 0)

        def chunk(ci, carry):
            r0 = pl.multiple_of(ci * ch, ch)
            du_c = du_ref[pl.ds(r0, ch), :]
            taps_u = _taps_front(upad_ref, ci, kw, ch, halo)
            taps_d = _taps_at(dpad_ref, ci, list(range(kw)), ch)
            du1 = w_ref[kw - 1:kw, :] * du_c
            ddw_ref[kw - 1] += jnp.sum((du_c * taps_u[kw - 1]).reshape(ch // 8, 8, tc), axis=0)
            for j in range(1, kw):
                du1 = du1 + w_ref[kw - 1 - j:kw - j, :] * taps_d[j]
                ddw_ref[kw - 1 - j] += jnp.sum((du_c * taps_u[kw - 1 - j]).reshape(ch // 8, 8, tc), axis=0)
            av = a_ref[pl.ds(r0, ch), :]
            sg = _sigmoid(g_ref[pl.ds(r0, ch), :])
            dza = du1 * sg
            dzg = du1 * av * (sg * (1.0 - sg))
            dza_ref[pl.ds(r0, ch), :] = dza.astype(BF16)
            dzg_ref[pl.ds(r0, ch), :] = dzg.astype(BF16)
            dba_ref[...] += jnp.sum(dza, axis=0, keepdims=True)
            dbg_ref[...] += jnp.sum(dzg, axis=0, keepdims=True)
            ddwb_ref[...] += jnp.sum(du_c, axis=0, keepdims=True)
            return carry

        lax.fori_loop(0, s // ch, chunk, 0)

    blk = lambda off: pl.BlockSpec((None, s, tc), lambda i, bi: (bi, 0, i + off))
    vec = pl.BlockSpec((1, tc), lambda i, bi: (0, i))
    return pl.pallas_call(
        body, name=name, grid=(nc, b),
        in_specs=[blk(0), blk(nc), pl.BlockSpec((kw, tc), lambda i, bi: (0, i)), blk(0)],
        out_specs=[blk(0), blk(0), pl.BlockSpec((kw, 8, tc), lambda i, bi: (0, 0, i)), vec, vec, vec],
        out_shape=[
            jax.ShapeDtypeStruct((b, s, c), BF16), jax.ShapeDtypeStruct((b, s, c), BF16),
            jax.ShapeDtypeStruct((kw, 8, c), F32), jax.ShapeDtypeStruct((1, c), F32),
            jax.ShapeDtypeStruct((1, c), F32), jax.ShapeDtypeStruct((1, c), F32),
        ],
        scratch_shapes=[pltpu.VMEM((nch, ch + halo, tc), F32), pltpu.VMEM((nch, ch + halo, tc), F32)],
        compiler_params=_cp("parallel", "arbitrary"),
    )(z, z, dw, du2)


def _fill_glu_slabs(a_ref, g_ref, pad_ref, nch, ch, halo):
    tc = a_ref.shape[-1]
    pad_ref[0, 0:halo, :] = jnp.zeros((halo, tc), F32)
    pad_ref[0, halo:halo + ch, :] = a_ref[0:ch, :] * _sigmoid(g_ref[0:ch, :])

    def fill(ci, carry):
        r0 = pl.multiple_of(ci * ch, ch)
        pad_ref[ci, 0:halo, :] = pad_ref[ci - 1, ch:ch + halo, :]
        pad_ref[ci, halo:halo + ch, :] = a_ref[pl.ds(r0, ch), :] * _sigmoid(g_ref[pl.ds(r0, ch), :])
        return carry

    lax.fori_loop(1, nch, fill, 0)


def ln_silu_fwd(u, g, bvec, name, tm=1024):
    t, d = u.shape
    tm = _row_tile(t, tm)

    def body(u_ref, g_ref, b_ref, o_ref):
        uv = u_ref[...]
        mu = jnp.mean(uv, axis=-1, keepdims=True)
        xc = uv - mu
        var = jnp.mean(xc * xc, axis=-1, keepdims=True)
        v = (xc * lax.rsqrt(var + EPS)) * g_ref[...] + b_ref[...]
        o_ref[...] = (v * _sigmoid(v)).astype(BF16)

    row = pl.BlockSpec((tm, d), lambda i: (i, 0))
    vec = pl.BlockSpec((1, d), lambda i: (0, 0))
    return pl.pallas_call(
        body, name=name, grid=(t // tm,), in_specs=[row, vec, vec], out_specs=row,
        out_shape=jax.ShapeDtypeStruct((t, d), BF16), compiler_params=_cp("parallel"),
    )(u, g, bvec)


def ln_silu_bwd(u, g, bvec, dout, name, tm=1024):
    t, d = u.shape
    tm = _row_tile(t, tm)

    def body(u_ref, g_ref, b_ref, do_ref, du_ref, dg_ref, db_ref):
        i = pl.program_id(0)
        uv = u_ref[...]
        mu = jnp.mean(uv, axis=-1, keepdims=True)
        xc = uv - mu
        var = jnp.mean(xc * xc, axis=-1, keepdims=True)
        rstd = lax.rsqrt(var + EPS)
        n = xc * rstd
        v = n * g_ref[...] + b_ref[...]
        sg = _sigmoid(v)
        dv = do_ref[...].astype(F32) * (sg * (1.0 + v * (1.0 - sg)))
        dn = dv * g_ref[...]
        du_ref[...] = rstd * (dn - jnp.mean(dn, axis=-1, keepdims=True) - n * jnp.mean(dn * n, axis=-1, keepdims=True))
        dg = jnp.sum(dv * n, axis=0, keepdims=True)
        db = jnp.sum(dv, axis=0, keepdims=True)

        @pl.when(i == 0)
        def _():
            dg_ref[...] = dg
            db_ref[...] = db

        @pl.when(i > 0)
        def _():
            dg_ref[...] += dg
            db_ref[...] += db

    row = pl.BlockSpec((tm, d), lambda i: (i, 0))
    vec = pl.BlockSpec((1, d), lambda i: (0, 0))
    return pl.pallas_call(
        body, name=name, grid=(t // tm,), in_specs=[row, vec, vec, row], out_specs=[row, vec, vec],
        out_shape=[jax.ShapeDtypeStruct((t, d), F32), jax.ShapeDtypeStruct((1, d), F32), jax.ShapeDtypeStruct((1, d), F32)],
        compiler_params=_cp("arbitrary"),
    )(u, g, bvec, dout)


FFN_HALO = 8


def _fill_front_halo(src_ref, pad_ref, nch, ch, halo):
    tc = src_ref.shape[-1]
    pad_ref[0, 0:halo, :] = jnp.zeros((halo, tc), F32)
    pad_ref[0, halo:halo + ch, :] = src_ref[0:ch, :].astype(F32)

    def fill(ci, carry):
        r0 = pl.multiple_of(ci * ch, ch)
        pad_ref[ci, 0:halo, :] = src_ref[pl.ds(r0 - 2 * halo, 2 * halo), :].astype(F32)[halo:, :]
        pad_ref[ci, halo:halo + ch, :] = src_ref[pl.ds(r0, ch), :].astype(F32)
        return carry

    lax.fori_loop(1, nch, fill, 0)


def _taps_at(pad_ref, ci, offsets, ch):
    windows = {}
    for b in sorted({o % 8 for o in offsets}):
        top = max(o for o in offsets if o % 8 == b)
        windows[b] = pad_ref[ci, b:top + ch, :]
    return [windows[o % 8][o - o % 8:o - o % 8 + ch, :] for o in offsets]


def _taps_front(pad_ref, ci, kw, ch, halo):
    return _taps_at(pad_ref, ci, [halo - (kw - 1 - k) for k in range(kw)], ch)


def ffn_mid_fwd(p, dw, dwb, name, tc=256):
    b, s, f2 = p.shape
    f = f2 // 2
    kw = dw.shape[0]
    tc = min(tc, f)
    nf = f // tc
    ch = min(CONV_CHUNK, s)
    nch = s // ch
    halo = FFN_HALO

    def body(pa_ref, pg_ref, wa_ref, wg_ref, ba_ref, bg_ref, o_ref, apad_ref, gpad_ref):
        _fill_front_halo(pa_ref, apad_ref, nch, ch, halo)
        _fill_front_halo(pg_ref, gpad_ref, nch, ch, halo)

        def chunk(ci, carry):
            r0 = pl.multiple_of(ci * ch, ch)
            ca = ba_ref[...]
            cg = bg_ref[...]
            taps = zip(_taps_front(apad_ref, ci, kw, ch, halo), _taps_front(gpad_ref, ci, kw, ch, halo))
            for k, (ta, tg) in enumerate(taps):
                ca = ca + wa_ref[k:k + 1, :] * ta
                cg = cg + wg_ref[k:k + 1, :] * tg
            o_ref[pl.ds(r0, ch), :] = ((cg * _sigmoid(cg)) * ca).astype(BF16)
            return carry

        lax.fori_loop(0, nch, chunk, 0)

    blk = lambda off: pl.BlockSpec((None, s, tc), lambda bi, i: (bi, 0, i + off))
    wsp = lambda off: pl.BlockSpec((kw, tc), lambda bi, i: (0, i + off))
    bsp = lambda off: pl.BlockSpec((1, tc), lambda bi, i: (0, i + off))
    return pl.pallas_call(
        body, name=name, grid=(b, nf),
        in_specs=[blk(0), blk(nf), wsp(0), wsp(nf), bsp(0), bsp(nf)],
        out_specs=pl.BlockSpec((None, s, tc), lambda bi, i: (bi, 0, i)),
        out_shape=jax.ShapeDtypeStruct((b, s, f), BF16),
        scratch_shapes=[pltpu.VMEM((nch, ch + halo, tc), F32)] * 2,
        compiler_params=_cp("parallel", "parallel"),
    )(p, p, dw, dw, dwb, dwb)


def ffn_mid_bwd(p, dw, dwb, ds, name, tc=256):
    b, s, f2 = p.shape
    f = f2 // 2
    kw = dw.shape[0]
    tc = min(tc, f)
    nf = f // tc
    ch = min(CONV_CHUNK, s)
    nch = s // ch
    halo = FFN_HALO

    def sum8(v):
        return jnp.sum(v.reshape(ch // 8, 8, tc), axis=0)

    def body(pa_ref, pg_ref, wa_ref, wg_ref, ba_ref, bg_ref, ds_ref, dpa_ref, dpg_ref, ddwa_ref, ddwg_ref, dba_ref, dbg_ref,
             apad_ref, gpad_ref, dca_ref, dcg_ref):
        bi = pl.program_id(1)

        @pl.when(bi == 0)
        def _():
            ddwa_ref[...] = jnp.zeros(ddwa_ref.shape, F32)
            ddwg_ref[...] = jnp.zeros(ddwg_ref.shape, F32)
            dba_ref[...] = jnp.zeros(dba_ref.shape, F32)
            dbg_ref[...] = jnp.zeros(dbg_ref.shape, F32)

        _fill_front_halo(pa_ref, apad_ref, nch, ch, halo)
        _fill_front_halo(pg_ref, gpad_ref, nch, ch, halo)
        dca_ref[nch - 1, ch:ch + halo, :] = jnp.zeros((halo, tc), F32)
        dcg_ref[nch - 1, ch:ch + halo, :] = jnp.zeros((halo, tc), F32)

        def grads(ci, carry):
            acc_a, acc_g, sb_a, sb_g = carry
            r0 = pl.multiple_of(ci * ch, ch)
            taps_a = _taps_front(apad_ref, ci, kw, ch, halo)
            taps_g = _taps_front(gpad_ref, ci, kw, ch, halo)
            ca = ba_ref[...]
            cg = bg_ref[...]
            for k in range(kw):
                ca = ca + wa_ref[k:k + 1, :] * taps_a[k]
                cg = cg + wg_ref[k:k + 1, :] * taps_g[k]
            sg = _sigmoid(cg)
            dsv = ds_ref[pl.ds(r0, ch), :].astype(F32)
            dca = dsv * (cg * sg)
            dcg = dsv * ca * (sg * (1.0 + cg * (1.0 - sg)))
            dca_ref[ci, 0:ch, :] = dca
            dcg_ref[ci, 0:ch, :] = dcg

            prev = jnp.maximum(ci - 1, 0)

            @pl.when(ci > 0)
            def _():
                dca_ref[prev, ch:ch + halo, :] = dca[0:halo, :]
                dcg_ref[prev, ch:ch + halo, :] = dcg[0:halo, :]

            acc_a = tuple(acc_a[k] + sum8(dca * taps_a[k]) for k in range(kw))
            acc_g = tuple(acc_g[k] + sum8(dcg * taps_g[k]) for k in range(kw))
            return acc_a, acc_g, sb_a + sum8(dca), sb_g + sum8(dcg)

        z8 = jnp.zeros((8, tc), F32)
        acc_a, acc_g, sb_a, sb_g = lax.fori_loop(0, nch, grads, ((z8,) * kw, (z8,) * kw, z8, z8))
        for k in range(kw):
            ddwa_ref[k] += acc_a[k]
            ddwg_ref[k] += acc_g[k]
        dba_ref[...] += jnp.sum(sb_a, axis=0, keepdims=True)
        dbg_ref[...] += jnp.sum(sb_g, axis=0, keepdims=True)

        def back(ci, carry):
            r0 = pl.multiple_of(ci * ch, ch)
            da = wa_ref[kw - 1:kw, :] * dca_ref[ci, 0:ch, :]
            dg = wg_ref[kw - 1:kw, :] * dcg_ref[ci, 0:ch, :]
            for j in range(1, kw):
                da = da + wa_ref[kw - 1 - j:kw - j, :] * dca_ref[ci, j:j + ch, :]
                dg = dg + wg_ref[kw - 1 - j:kw - j, :] * dcg_ref[ci, j:j + ch, :]
            dpa_ref[pl.ds(r0, ch), :] = da.astype(BF16)
            dpg_ref[pl.ds(r0, ch), :] = dg.astype(BF16)
            return carry

        lax.fori_loop(0, nch, back, 0)

    blk = lambda off: pl.BlockSpec((None, s, tc), lambda i, bi: (bi, 0, i + off))
    wsp = lambda off: pl.BlockSpec((kw, tc), lambda i, bi: (0, i + off))
    bsp = lambda off: pl.BlockSpec((1, tc), lambda i, bi: (0, i + off))
    acc3 = pl.BlockSpec((kw, 8, tc), lambda i, bi: (0, 0, i))
    vec = pl.BlockSpec((1, tc), lambda i, bi: (0, i))
    return pl.pallas_call(
        body, name=name, grid=(nf, b),
        in_specs=[blk(0), blk(nf), wsp(0), wsp(nf), bsp(0), bsp(nf), blk(0)],
        out_specs=[blk(0), blk(0), acc3, acc3, vec, vec],
        out_shape=[jax.ShapeDtypeStruct((b, s, f), BF16), jax.ShapeDtypeStruct((b, s, f), BF16),
                   jax.ShapeDtypeStruct((kw, 8, f), F32), jax.ShapeDtypeStruct((kw, 8, f), F32),
                   jax.ShapeDtypeStruct((1, f), F32), jax.ShapeDtypeStruct((1, f), F32)],
        scratch_shapes=[pltpu.VMEM((nch, ch + halo, tc), F32)] * 4,
        compiler_params=_cp("parallel", "arbitrary"),
    )(p, p, dw, dw, dwb, dwb, ds)


def _tile_rows(r, n, dil):
    start = r + n * BLK * dil
    return pl.ds(start, BLK, stride=dil) if dil > 1 else pl.ds(start, BLK)


def _band_masks():
    qi = lax.broadcasted_iota(jnp.int32, (BLK, 2 * BLK), 0)
    kk = lax.broadcasted_iota(jnp.int32, (BLK, 2 * BLK), 1)
    both = jnp.logical_or(jnp.logical_and(kk < BLK, kk >= qi), jnp.logical_and(kk >= BLK, kk - BLK <= qi))
    return both, kk[:, :BLK] <= qi[:, :BLK]


def attn_fwd(q, kv, g, dil, hw, name):
    _, b, s, _ = q.shape
    nh = hw // HEAD_DIM
    nblk = s // dil // BLK
    scale = 1.0 / math.sqrt(HEAD_DIM)

    def body(q_ref, k_ref, v_ref, o_ref, lse_ref):
        h = pl.program_id(1)
        mask2, mask1 = _band_masks()
        mine = lax.broadcasted_iota(jnp.int32, (BLK, LANES), 1) == h

        @pl.when(h == 0)
        def _():
            lse_ref[...] = jnp.zeros(lse_ref.shape, F32)

        for r in range(dil):
            kp = vp = None
            for n in range(nblk):
                rs = _tile_rows(r, n, dil)
                qt = q_ref[rs, :].astype(BF16)
                kc = k_ref[rs, :].astype(BF16)
                vc = v_ref[rs, :].astype(BF16)
                if n == 0:
                    kcat, vcat, mask = kc, vc, mask1
                else:
                    kcat, vcat, mask = jnp.concatenate([kp, kc], axis=0), jnp.concatenate([vp, vc], axis=0), mask2
                sc = jnp.where(mask, _dot_nt(qt, kcat) * scale, NEG_INF)
                m = jnp.max(sc, axis=-1, keepdims=True)
                p = jnp.exp(sc - m)
                den = jnp.sum(p, axis=-1, keepdims=True)
                o_ref[rs, :] = _dot(p.astype(BF16), vcat) / den
                lse_ref[rs, :] = jnp.where(mine, m + jnp.log(den), lse_ref[rs, :])
                kp, vp = kc, vc

    col = lambda base: pl.BlockSpec((None, s, HEAD_DIM), lambda bi, h: (bi, 0, base + h))
    head = lambda base: pl.BlockSpec((None, None, s, HEAD_DIM), lambda bi, h: (base + h, bi, 0, 0))
    return pl.pallas_call(
        body, name=name, grid=(b, nh),
        in_specs=[head(g * nh), head(g * nh), head((N_GROUPS + g) * nh)],
        out_specs=[head(0), pl.BlockSpec((None, s, LANES), lambda bi, h: (bi, 0, 0))],
        out_shape=[jax.ShapeDtypeStruct((nh, b, s, HEAD_DIM), F32), jax.ShapeDtypeStruct((b, s, LANES), F32)],
        compiler_params=_cp("parallel", "arbitrary"),
    )(q, kv, kv)


def attn_merge(outs, lses, name, tm=512):
    nh, t, _ = outs[0].shape
    hw = nh * HEAD_DIM
    tm = _row_tile(t, tm)
    ng = len(outs)

    def body(*refs):
        o_refs, l_refs = refs[:ng], refs[ng:2 * ng]
        m_ref, lj_ref = refs[2 * ng:]
        ls = [l_refs[g][...] for g in range(ng)]
        mx = ls[0]
        for g in range(1, ng):
            mx = jnp.maximum(mx, ls[g])
        es = [jnp.exp(l - mx) for l in ls]
        tot = es[0]
        for g in range(1, ng):
            tot = tot + es[g]
        ws = [e / tot for e in es]
        lj_ref[...] = mx + jnp.log(tot)
        for h in range(nh):
            sl = slice(h * HEAD_DIM, (h + 1) * HEAD_DIM)
            acc = ws[0][:, h:h + 1] * o_refs[0][h]
            for g in range(1, ng):
                acc = acc + ws[g][:, h:h + 1] * o_refs[g][h]
            m_ref[:, sl] = acc.astype(BF16)

    row = pl.BlockSpec((tm, hw), lambda i: (i, 0))
    st = pl.BlockSpec((tm, LANES), lambda i: (i, 0))
    heads = pl.BlockSpec((nh, tm, HEAD_DIM), lambda i: (0, i, 0))
    return pl.pallas_call(
        body, name=name, grid=(t // tm,), in_specs=[heads] * ng + [st] * ng, out_specs=[row, st],
        out_shape=[jax.ShapeDtypeStruct((t, hw), BF16), jax.ShapeDtypeStruct((t, LANES), F32)],
        compiler_params=_cp("parallel"),
    )(*outs, *lses)


def attn_bwd_prep(dmerged, merged, name, tm=512):
    t, hw = merged.shape
    nh = hw // HEAD_DIM
    tm = _row_tile(t, tm)

    def body(d_ref, m_ref, o_ref):
        lane = lax.broadcasted_iota(jnp.int32, (tm, LANES), 1)
        acc = jnp.zeros((tm, LANES), F32)
        for h in range(nh):
            sl = slice(h * HEAD_DIM, (h + 1) * HEAD_DIM)
            dsum = jnp.sum(d_ref[:, sl] * m_ref[:, sl].astype(F32), axis=-1, keepdims=True)
            acc = jnp.where(lane == h, dsum, acc)
        o_ref[...] = acc

    row = pl.BlockSpec((tm, hw), lambda i: (i, 0))
    return pl.pallas_call(
        body, name=name, grid=(t // tm,), in_specs=[row, row], out_specs=pl.BlockSpec((tm, LANES), lambda i: (i, 0)),
        out_shape=jax.ShapeDtypeStruct((t, LANES), F32), compiler_params=_cp("parallel"),
    )(dmerged, merged)


def attn_bwd(q, kv, g, dil, do, lsej, dm, dq_buf, dk_buf, dv_buf, accumulate, hw, name):
    _, b, s, _ = q.shape
    nh = hw // HEAD_DIM
    nblk = s // dil // BLK
    scale = 1.0 / math.sqrt(HEAD_DIM)
    assert dk_buf is not None or not accumulate
    kv_at = 6 + (dq_buf is not None)

    def body(*refs):
        q_ref, k_ref, v_ref, do_ref, lj_ref, dm_ref = refs[:6]
        dq_ref, dk_ref, dv_ref = refs[-3:]
        dki_ref, dvi_ref = (refs[kv_at], refs[kv_at + 1]) if accumulate else (None, None)
        mask2, mask1 = _band_masks()
        mine = lax.broadcasted_iota(jnp.int32, (BLK, LANES), 1) == pl.program_id(1)

        def my_lane(v):
            return jnp.sum(jnp.where(mine, v, 0.0), axis=-1, keepdims=True)

        def put(rs, dk, dv):
            if accumulate:
                dk = dk + dki_ref[rs, :]
                dv = dv + dvi_ref[rs, :]
            dk_ref[rs, :] = dk
            dv_ref[rs, :] = dv

        for r in range(dil):
            kp = vp = hold_k = hold_v = rs_prev = None
            for n in range(nblk):
                rs = _tile_rows(r, n, dil)
                qt = q_ref[rs, :].astype(BF16)
                kc = k_ref[rs, :].astype(BF16)
                vc = v_ref[rs, :].astype(BF16)
                dot = do_ref[rs, :].astype(BF16)
                lm = my_lane(lj_ref[rs, :])
                dmm = my_lane(dm_ref[rs, :])
                if n == 0:
                    kcat, vcat, mask = kc, vc, mask1
                else:
                    kcat, vcat, mask = jnp.concatenate([kp, kc], axis=0), jnp.concatenate([vp, vc], axis=0), mask2
                p = jnp.exp(jnp.where(mask, _dot_nt(qt, kcat) * scale, NEG_INF) - lm)
                ds = (p * (_dot_nt(dot, vcat) - dmm)).astype(BF16)
                dq_ref[rs, :] = _dot(ds, kcat) * scale
                dkc = _dot_tn(ds, qt) * scale
                dvc = _dot_tn(p.astype(BF16), dot)
                if n > 0:
                    put(rs_prev, hold_k + dkc[:BLK, :], hold_v + dvc[:BLK, :])
                    dkc, dvc = dkc[BLK:, :], dvc[BLK:, :]
                hold_k, hold_v, kp, vp, rs_prev = dkc, dvc, kc, vc, rs
            put(rs_prev, hold_k, hold_v)

    col = lambda base: pl.BlockSpec((None, s, HEAD_DIM), lambda bi, h: (bi, 0, base + h))
    any_spec = pl.BlockSpec(memory_space=pl.ANY)
    stat = pl.BlockSpec((None, s, LANES), lambda bi, h: (bi, 0, 0))
    head = lambda base: pl.BlockSpec((None, None, s, HEAD_DIM), lambda bi, h: (base + h, bi, 0, 0))
    in_specs = [head(g * nh), head(g * nh), head((N_GROUPS + g) * nh), col(0), stat, stat]
    args = [q, kv, kv, do, lsej, dm]
    aliases = {}
    if dq_buf is not None:
        in_specs.append(any_spec)
        args.append(dq_buf)
        aliases[6] = 0
    if dk_buf is not None:
        in_specs += [col(g * nh) if accumulate else any_spec] * 2
        args += [dk_buf, dv_buf]
        aliases.update({kv_at: 1, kv_at + 1: 2})
    shape = jax.ShapeDtypeStruct((b, s, N_GROUPS * hw), F32)
    return pl.pallas_call(
        body, name=name, grid=(b, nh), in_specs=in_specs, out_specs=[col(g * nh)] * 3, out_shape=[shape] * 3,
        input_output_aliases=aliases, compiler_params=_cp("parallel", "parallel"),
    )(*args)


def sum_parts(g, recv, me, name, tm=512, layer=0, n_layers=1, out_buf=None):
    _, rows, c = g.shape
    n = recv.shape[0]
    tm = _row_tile(rows, tm)
    steps = rows // tm

    def body(me_ref, g_ref, r_ref, *rest):
        acc = g_ref[...].astype(F32)
        for j in range(n):
            acc = acc + r_ref[j].astype(F32)
        rest[-1][...] = acc

    in_specs = [pl.BlockSpec((None, tm, c), lambda i, me_ref: (me_ref[0], i, 0)),
                pl.BlockSpec((n, tm, c), lambda i, me_ref: (0, i, 0))]
    args = [me, g, recv]
    aliases = {}
    if out_buf is not None:
        in_specs.append(pl.BlockSpec(memory_space=pl.ANY))
        args.append(out_buf)
        aliases = {3: 0}
    return pl.pallas_call(
        body, name=name,
        grid_spec=pltpu.PrefetchScalarGridSpec(
            num_scalar_prefetch=1, grid=(steps,), in_specs=in_specs,
            out_specs=pl.BlockSpec((tm, c), lambda i, me_ref: (layer * steps + i, 0))),
        out_shape=jax.ShapeDtypeStruct((n_layers * rows, c), F32), input_output_aliases=aliases,
        compiler_params=_cp("parallel"),
    )(*args)


def adamw(w, m, v, g_parts, name, tm=256):
    rows, c = w.shape
    if c <= 8 * LANES:
        tm = 2 * tm
    tm = _row_tile(rows, tm)
    npart = len(g_parts)

    def body(*refs):
        w_ref, m_ref, v_ref = refs[:3]
        g_refs = refs[3:3 + npart]
        go_ref, d_ref, mo_ref, vo_ref = refs[3 + npart:]
        g = g_refs[0][...]
        for k in range(1, npart):
            g = g + g_refs[k][...]
        mn = ADAM_B1 * m_ref[...] + (1.0 - ADAM_B1) * g
        vn = ADAM_B2 * v_ref[...] + (1.0 - ADAM_B2) * (g * g)
        m_hat = mn / (1.0 - ADAM_B1 ** ADAM_STEP)
        v_hat = vn / (1.0 - ADAM_B2 ** ADAM_STEP)
        go_ref[...] = g
        d_ref[...] = -ADAM_LR * (m_hat / (jnp.sqrt(v_hat) + ADAM_EPS) + ADAM_WD * w_ref[...])
        mo_ref[...] = mn
        vo_ref[...] = vn

    row = pl.BlockSpec((tm, c), lambda i: (i, 0))
    return pl.pallas_call(
        body, name=name, grid=(rows // tm,), in_specs=[row] * (3 + npart), out_specs=[row] * 4,
        out_shape=[jax.ShapeDtypeStruct((rows, c), F32)] * 4, compiler_params=_cp("parallel"),
    )(w, m, v, *g_parts)


def _place():
    return lax.axis_index("x"), lax.axis_index("y"), lax.axis_index("c")


def _other_chips(x, y, c):
    return [(1 - x, y, c), (x, 1 - y, c), (1 - x, 1 - y, c)]


def _chip_of(px, py):
    return 2 * px + py


HBM_SPEC = pl.BlockSpec(memory_space=pltpu.HBM)
SEM_SPEC = pl.BlockSpec(memory_space=pltpu.SEMAPHORE)
ANY_SPEC = pl.BlockSpec(memory_space=pl.ANY)
DATAFLOW = pltpu.SideEffectType.DATAFLOW_SIDE_EFFECTING
N_PEER_CHIPS = N_CHIPS - 1


def _hbm(a):
    return pltpu.with_memory_space_constraint(a, pltpu.HBM)


def _hbm_like(arrays):
    return [pltpu.HBM(a.shape, a.dtype) for a in arrays]


def cast_place(w, layer, me, out_dtype, name, tm=512, nslots=N_CHIPS, dep=None):
    rows, c = w.shape[-2:]
    tm = _row_tile(rows, tm)

    def body(me_ref, w_ref, *rest):
        rest[-1][...] = w_ref[...].astype(out_dtype)

    if layer is None:
        in_specs = [pl.BlockSpec((tm, c), lambda i, me_ref: (i, 0))]
    else:
        in_specs = [pl.BlockSpec((None, tm, c), lambda i, me_ref: (layer, i, 0))]
    args = [me, w]
    if dep is not None:
        in_specs.append(pl.BlockSpec(DEP_SPEC_SHAPE, lambda i, me_ref: (0, 0)))
        args.append(dep)
    return pl.pallas_call(
        body, name=name,
        grid_spec=pltpu.PrefetchScalarGridSpec(
            num_scalar_prefetch=1, grid=(rows // tm,), in_specs=in_specs,
            out_specs=pl.BlockSpec((None, tm, c), lambda i, me_ref: (me_ref[0], i, 0))),
        out_shape=jax.ShapeDtypeStruct((nslots, rows, c), out_dtype), compiler_params=_cp("parallel"),
    )(*args)


def gather_start(lands, chunk_sizes, name="gather_start"):
    n = len(lands)
    nch = len(chunk_sizes)
    assert sum(chunk_sizes) == n

    def body(*refs):
        land_refs = refs[:n]
        outs = refs[n:]
        send_sems, recv_sems = outs[:nch], outs[nch:2 * nch]
        token = outs[-1]
        x, y, c = _place()
        me = _chip_of(x, y)
        peers = _other_chips(x, y, c)
        k = 0
        for ck, size in enumerate(chunk_sizes):
            for pos in range(size):
                for r, peer in enumerate(peers):
                    pltpu.make_async_remote_copy(
                        src_ref=land_refs[k].at[me], dst_ref=land_refs[k].at[me],
                        send_sem=send_sems[ck].at[N_PEER_CHIPS * pos + r], recv_sem=recv_sems[ck].at[N_PEER_CHIPS * pos + r],
                        device_id=peer, device_id_type=MESH).start()
                k += 1
        token[...] = jnp.zeros(token.shape, F32)

    sems = [pltpu.SemaphoreType.DMA((N_PEER_CHIPS * s,)) for s in chunk_sizes]
    res = pl.pallas_call(
        body, name=name,
        out_shape=(*sems, *sems, *_hbm_like(lands), jax.ShapeDtypeStruct(DEP_SPEC_SHAPE, F32)),
        in_specs=[HBM_SPEC] * n,
        out_specs=(*[SEM_SPEC] * (2 * nch), *[HBM_SPEC] * n, pl.BlockSpec(memory_space=pltpu.VMEM)),
        input_output_aliases={k: 2 * nch + k for k in range(n)},
        compiler_params=pltpu.CompilerParams(has_side_effects=DATAFLOW),
    )(*[_hbm(a) for a in lands])
    return res[:nch], res[nch:2 * nch], res[2 * nch:2 * nch + n], res[-1]


def gather_wait(send_sem, recv_sem, lands, after, name):
    n = len(lands)

    def body(*refs):
        land_refs = refs[:n]
        ssem, rsem = refs[n], refs[n + 1]
        x, y, c = _place()
        me = _chip_of(x, y)
        for pos in range(n):
            for r, peer in enumerate(_other_chips(x, y, c)):
                cp = pltpu.make_async_remote_copy(
                    src_ref=land_refs[pos].at[me], dst_ref=land_refs[pos].at[_chip_of(peer[0], peer[1])],
                    send_sem=ssem.at[N_PEER_CHIPS * pos + r], recv_sem=rsem.at[N_PEER_CHIPS * pos + r],
                    device_id=peer, device_id_type=MESH)
                cp.wait_send()
                cp.wait_recv()

    return pl.pallas_call(
        body, name=name, out_shape=tuple(_hbm_like(lands)),
        in_specs=[*[HBM_SPEC] * n, SEM_SPEC, SEM_SPEC, ANY_SPEC], out_specs=[HBM_SPEC] * n,
        input_output_aliases={k: k for k in range(n)},
        compiler_params=pltpu.CompilerParams(has_side_effects=DATAFLOW),
    )(*lands, send_sem, recv_sem, after)


def scatter_start(grads, name):
    n = len(grads)
    recvs = [lax.empty((N_PEER_CHIPS, *g.shape[1:]), g.dtype) for g in grads]

    def body(*refs):
        g_refs, r_refs = refs[:n], refs[n:2 * n]
        send_sems, recv_sems = refs[2 * n], refs[2 * n + 1]
        token = refs[-1]
        x, y, c = _place()
        for k in range(n):
            for r, peer in enumerate(_other_chips(x, y, c)):
                pltpu.make_async_remote_copy(
                    src_ref=g_refs[k].at[_chip_of(peer[0], peer[1])], dst_ref=r_refs[k].at[r],
                    send_sem=send_sems.at[N_PEER_CHIPS * k + r], recv_sem=recv_sems.at[N_PEER_CHIPS * k + r],
                    device_id=peer, device_id_type=MESH).start()
        token[...] = jnp.zeros(token.shape, F32)

    sem = pltpu.SemaphoreType.DMA((N_PEER_CHIPS * n,))
    res = pl.pallas_call(
        body, name=name,
        out_shape=(sem, sem, *_hbm_like(grads), *_hbm_like(recvs), jax.ShapeDtypeStruct(DEP_SPEC_SHAPE, F32)),
        in_specs=[HBM_SPEC] * (2 * n),
        out_specs=(SEM_SPEC, SEM_SPEC, *[HBM_SPEC] * (2 * n), pl.BlockSpec(memory_space=pltpu.VMEM)),
        input_output_aliases={k: 2 + k for k in range(2 * n)},
        compiler_params=pltpu.CompilerParams(has_side_effects=DATAFLOW),
    )(*[_hbm(a) for a in grads], *[_hbm(a) for a in recvs])
    return res[0], res[1], res[2:2 + n], res[2 + n:2 + 2 * n], res[-1]


def scatter_wait(send_sem, recv_sem, grads, recvs, after, name):
    n = len(grads)

    def body(*refs):
        g_refs, r_refs = refs[:n], refs[n:2 * n]
        ssem, rsem = refs[2 * n], refs[2 * n + 1]
        x, y, c = _place()
        for k in range(n):
            for r, peer in enumerate(_other_chips(x, y, c)):
                cp = pltpu.make_async_remote_copy(
                    src_ref=g_refs[k].at[_chip_of(peer[0], peer[1])], dst_ref=r_refs[k].at[r],
                    send_sem=ssem.at[N_PEER_CHIPS * k + r], recv_sem=rsem.at[N_PEER_CHIPS * k + r],
                    device_id=peer, device_id_type=MESH)
                cp.wait_send()
                cp.wait_recv()

    res = pl.pallas_call(
        body, name=name, out_shape=(*_hbm_like(grads), *_hbm_like(recvs)),
        in_specs=[*[HBM_SPEC] * (2 * n), SEM_SPEC, SEM_SPEC, ANY_SPEC], out_specs=[HBM_SPEC] * (2 * n),
        input_output_aliases={k: k for k in range(2 * n)},
        compiler_params=pltpu.CompilerParams(has_side_effects=DATAFLOW),
    )(*grads, *recvs, send_sem, recv_sem, after)
    return res[:n], res[n:]


def swap_start(parts, name):
    n = len(parts)
    lands = [lax.empty(p.shape, p.dtype) for p in parts]

    def body(*refs):
        p_refs, l_refs = refs[:n], refs[n:2 * n]
        sems = refs[2 * n:4 * n]
        token = refs[-1]
        x, y, c = _place()
        for k in range(n):
            pltpu.make_async_remote_copy(
                src_ref=p_refs[k], dst_ref=l_refs[k], send_sem=sems[k], recv_sem=sems[n + k],
                device_id=(x, y, 1 - c), device_id_type=MESH).start()
        token[...] = jnp.zeros(token.shape, F32)

    sem = pltpu.SemaphoreType.DMA(())
    res = pl.pallas_call(
        body, name=name,
        out_shape=(*[sem] * (2 * n), *_hbm_like(parts), *_hbm_like(lands), jax.ShapeDtypeStruct(DEP_SPEC_SHAPE, F32)),
        in_specs=[HBM_SPEC] * (2 * n),
        out_specs=(*[SEM_SPEC] * (2 * n), *[HBM_SPEC] * (2 * n), pl.BlockSpec(memory_space=pltpu.VMEM)),
        input_output_aliases={k: 2 * n + k for k in range(2 * n)},
        compiler_params=pltpu.CompilerParams(has_side_effects=DATAFLOW),
    )(*[_hbm(a) for a in parts], *[_hbm(a) for a in lands])
    return res[:n], res[n:2 * n], res[2 * n:3 * n], res[3 * n:4 * n], res[-1]


def swap_wait(send_sem, recv_sem, part, land, after, name):
    def body(p_ref, l_ref, ssem, rsem, after_ref, p_out, l_out):
        x, y, c = _place()
        cp = pltpu.make_async_remote_copy(src_ref=p_ref, dst_ref=l_ref, send_sem=ssem, recv_sem=rsem,
                                          device_id=(x, y, 1 - c), device_id_type=MESH)
        cp.wait_send()
        cp.wait_recv()

    return pl.pallas_call(
        body, name=name, out_shape=tuple(_hbm_like([part, land])),
        in_specs=[HBM_SPEC, HBM_SPEC, SEM_SPEC, SEM_SPEC, ANY_SPEC], out_specs=[HBM_SPEC, HBM_SPEC],
        input_output_aliases={0: 0, 1: 1},
        compiler_params=pltpu.CompilerParams(has_side_effects=DATAFLOW),
    )(part, land, send_sem, recv_sem, after)


def _xor_peer(x, y, c, k):
    px, py, pc = x ^ ((k >> 2) & 1), y ^ ((k >> 1) & 1), c ^ (k & 1)
    return (px, py, pc), 4 * px + 2 * py + pc


def small_start(land, name="small_start"):
    def body(l_ref, ssem, rsem, l_out, token):
        x, y, c = _place()
        me = 4 * x + 2 * y + c
        for k in range(1, N_DEV):
            peer, _ = _xor_peer(x, y, c, k)
            pltpu.make_async_remote_copy(
                src_ref=l_ref.at[me], dst_ref=l_ref.at[me], send_sem=ssem.at[k - 1], recv_sem=rsem.at[k - 1],
                device_id=peer, device_id_type=MESH).start()
        token[...] = jnp.zeros(token.shape, F32)

    sem = pltpu.SemaphoreType.DMA((N_DEV - 1,))
    return pl.pallas_call(
        body, name=name,
        out_shape=(sem, sem, pltpu.HBM(land.shape, land.dtype), jax.ShapeDtypeStruct(DEP_SPEC_SHAPE, F32)),
        in_specs=[HBM_SPEC], out_specs=(SEM_SPEC, SEM_SPEC, HBM_SPEC, pl.BlockSpec(memory_space=pltpu.VMEM)),
        input_output_aliases={0: 2}, compiler_params=pltpu.CompilerParams(has_side_effects=DATAFLOW),
    )(_hbm(land))


def small_wait(send_sem, recv_sem, land, after, name="small_wait"):
    def body(l_ref, ssem, rsem, after_ref, l_out):
        x, y, c = _place()
        me = 4 * x + 2 * y + c
        for k in range(1, N_DEV):
            peer, slot = _xor_peer(x, y, c, k)
            cp = pltpu.make_async_remote_copy(
                src_ref=l_ref.at[me], dst_ref=l_ref.at[slot], send_sem=ssem.at[k - 1], recv_sem=rsem.at[k - 1],
                device_id=peer, device_id_type=MESH)
            cp.wait_send()
            cp.wait_recv()

    return pl.pallas_call(
        body, name=name, out_shape=pltpu.HBM(land.shape, land.dtype),
        in_specs=[HBM_SPEC, SEM_SPEC, SEM_SPEC, ANY_SPEC], out_specs=HBM_SPEC, input_output_aliases={0: 0},
        compiler_params=pltpu.CompilerParams(has_side_effects=DATAFLOW),
    )(land, send_sem, recv_sem, after)


def sum_slots(land, name="sum_slots", tm=256):
    n, rows, c = land.shape
    tm = _row_tile(rows, tm)

    def body(l_ref, o_ref):
        acc = l_ref[0]
        for j in range(1, n):
            acc = acc + l_ref[j]
        o_ref[...] = acc

    return pl.pallas_call(
        body, name=name, grid=(rows // tm,), in_specs=[pl.BlockSpec((n, tm, c), lambda i: (0, i, 0))],
        out_specs=pl.BlockSpec((tm, c), lambda i: (i, 0)), out_shape=jax.ShapeDtypeStruct((rows, c), F32),
        compiler_params=_cp("parallel"),
    )(land)


PACK_ROW_TILE = 256


def _pack(arrays):
    flat = jnp.concatenate([a.reshape(-1).astype(F32) for a in arrays])
    n = flat.shape[0]
    rows = -(-n // LANES)
    rows = -(-rows // PACK_ROW_TILE) * PACK_ROW_TILE
    return jnp.pad(flat, (0, rows * LANES - n)).reshape(rows, LANES)


def _unpack(packed, shapes, lead=()):
    flat = packed.reshape(*lead, -1)
    out, off = [], 0
    for shp in shapes:
        n = math.prod(shp)
        out.append(flat[..., off:off + n].reshape(*lead, *shp))
        off += n
    return out


def _row(vec):
    return vec.reshape(1, -1)


def kernel(x, mix_pre_g, mix_post_g, ffn_pre_g, ffn_post_g, cm_w_in, cm_b_in, cm_dw, cm_dw_b, cm_ln_g, cm_ln_b, cm_w_out, cm_b_out, kv_norm_g, w_kv, w_q, w_o, ffn_w_in, ffn_dw, ffn_dw_b, ffn_w_out, loss_target, m_mix_pre_g, m_mix_post_g, m_ffn_pre_g, m_ffn_post_g, m_cm_w_in, m_cm_b_in, m_cm_dw, m_cm_dw_b, m_cm_ln_g, m_cm_ln_b, m_cm_w_out, m_cm_b_out, m_kv_norm_g, m_w_kv, m_w_q, m_w_o, m_ffn_w_in, m_ffn_dw, m_ffn_dw_b, m_ffn_w_out, v_mix_pre_g, v_mix_post_g, v_ffn_pre_g, v_ffn_post_g, v_cm_w_in, v_cm_b_in, v_cm_dw, v_cm_dw_b, v_cm_ln_g, v_cm_ln_b, v_cm_w_out, v_cm_b_out, v_kv_norm_g, v_w_kv, v_w_q, v_w_o, v_ffn_w_in, v_ffn_dw, v_ffn_dw_b, v_ffn_w_out):
    names = ["mix_pre_g", "mix_post_g", "ffn_pre_g", "ffn_post_g", "cm_w_in", "cm_b_in", "cm_dw", "cm_dw_b", "cm_ln_g",
             "cm_ln_b", "cm_w_out", "cm_b_out", "kv_norm_g", "w_kv", "w_q", "w_o", "ffn_w_in", "ffn_dw", "ffn_dw_b",
             "ffn_w_out"]
    w_in = dict(zip(names, [mix_pre_g, mix_post_g, ffn_pre_g, ffn_post_g, cm_w_in, cm_b_in, cm_dw, cm_dw_b, cm_ln_g,
                            cm_ln_b, cm_w_out, cm_b_out, kv_norm_g, w_kv, w_q, w_o, ffn_w_in, ffn_dw, ffn_dw_b, ffn_w_out]))
    m_in = dict(zip(names, [m_mix_pre_g, m_mix_post_g, m_ffn_pre_g, m_ffn_post_g, m_cm_w_in, m_cm_b_in, m_cm_dw, m_cm_dw_b,
                            m_cm_ln_g, m_cm_ln_b, m_cm_w_out, m_cm_b_out, m_kv_norm_g, m_w_kv, m_w_q, m_w_o, m_ffn_w_in,
                            m_ffn_dw, m_ffn_dw_b, m_ffn_w_out]))
    v_in = dict(zip(names, [v_mix_pre_g, v_mix_post_g, v_ffn_pre_g, v_ffn_post_g, v_cm_w_in, v_cm_b_in, v_cm_dw, v_cm_dw_b,
                            v_cm_ln_g, v_cm_ln_b, v_cm_w_out, v_cm_b_out, v_kv_norm_g, v_w_kv, v_w_q, v_w_o, v_ffn_w_in,
                            v_ffn_dw, v_ffn_dw_b, v_ffn_w_out]))

    bsz, seq, d = x.shape
    t = bsz * seq
    n_b = DEPTH - N_A
    hw = w_o.shape[-1]
    qw = N_GROUPS * hw
    f2 = ffn_dw_b.shape[-1]
    f = f2 // 2
    me_chip = _chip_of(lax.axis_index("x"), lax.axis_index("y"))

    big = ["cm_w_in", "cm_w_out", "w_kv", "w_q", "w_o", "ffn_w_in", "ffn_w_out"]
    row_sharded = ("cm_w_out", "w_o", "ffn_w_out")
    small_sharded = ["cm_b_in", "cm_dw", "cm_dw_b", "cm_ln_g", "cm_ln_b", "cm_b_out", "ffn_dw"]
    small_pack = _pack([w_in[n] for n in small_sharded])
    chunks = [
        [("cm_w_in", 0), ("small", None)],
        [("cm_w_out", 0)],
        [("ffn_w_in", 0), ("ffn_w_out", 0)],
        [("cm_w_in", 1), ("cm_w_out", 1)],
        [("ffn_w_in", 1), ("ffn_w_out", 1)],
        [("w_kv", None)],
        [("w_q", 0), ("w_o", 0)],
        [("ffn_w_in", 2), ("ffn_w_out", 2)],
        [("w_q", 1), ("w_o", 1)],
        [("ffn_w_in", 3), ("ffn_w_out", 3)],
    ]
    chunk_of = {pc: ck for ck, ch in enumerate(chunks) for pc in ch}

    me_arr = me_chip.astype(jnp.int32).reshape(1)

    def land_of(pc, dep=None):
        n, l = pc
        if n == "small":
            return cast_place(small_pack, None, me_arr, F32, name="place_small", dep=dep)
        return cast_place(w_in[n], l, me_arr, BF16, name=f"place_{n}_{l}", dep=dep)

    groups = [[0, 1], [2], list(range(3, len(chunks)))]
    g_send, g_recv, lands_f = {}, {}, {}

    def start_group(gi, dep, name):
        cks = groups[gi]
        pcs = [pc for ck in cks for pc in chunks[ck]]
        lands = [land_of(pc, dep if k == 0 else None) for k, pc in enumerate(pcs)]
        send, recv, lands_thru, tok = gather_start(lands, [len(chunks[ck]) for ck in cks], name=name)
        pos = 0
        for j, ck in enumerate(cks):
            g_send[ck], g_recv[ck] = send[j], recv[j]
            lands_f[ck] = lands_thru[pos:pos + len(chunks[ck])]
            pos += len(chunks[ck])
        return tok

    token_a = start_group(0, None, "gather_start_a")
    token = start_group(1, token_a, "gather_start_b")
    weights = {}

    def finish_chunk(ck, after):
        got = gather_wait(g_send[ck], g_recv[ck], lands_f[ck], after, name=f"gather_wait{ck}")
        for pc, arr in zip(chunks[ck], got):
            weights[pc] = arr.reshape(1, -1, arr.shape[-1]) if pc[0] in row_sharded else arr

    def wmat(n, l=None, after=None):
        if (n, l) not in weights:
            finish_chunk(chunk_of[(n, l)], after)
        arr = weights[(n, l)]
        return arr, arr.shape[0]

    finish_chunk(0, token)
    small_full = {}
    for n, arr4 in zip(small_sharded, _unpack(weights[("small", None)], [w_in[n].shape for n in small_sharded], lead=(N_CHIPS,))):
        shp = w_in[n].shape
        small_full[n] = jnp.moveaxis(arr4, 0, -2).reshape(*shp[:-1], N_CHIPS * shp[-1])

    x2d = x.reshape(t, d)
    saved = []
    (h1,) = resid_norm_fwd(x2d, None, None, [_row(mix_pre_g[0])], name="norm_in", dep=token)
    xcur = x2d
    kv_state = None
    for i in range(DEPTH):
        sv = {"x_in": xcur, "h1": h1}
        if i < N_A:
            z = mm_nn(h1, *wmat("cm_w_in", i, h1), bias=_row(small_full["cm_b_in"][i]), name=f"cm_in{i}",
                      spg=N_CHIPS)
            u2 = glu_conv_fwd(z.reshape(bsz, seq, 2 * d), small_full["cm_dw"][i], _row(small_full["cm_dw_b"][i]),
                              name=f"glu_conv{i}").reshape(t, d)
            u4 = ln_silu_fwd(u2, _row(small_full["cm_ln_g"][i]), _row(small_full["cm_ln_b"][i]), name=f"ln_silu{i}")
            y = mm_nn(u4, *wmat("cm_w_out", i, u4), bias=_row(small_full["cm_b_out"][i]), out_dtype=BF16,
                      name=f"cm_out{i}")
            sv.update(z=z, u2=u2, u4=u4)
        else:
            j = i - N_A
            q = mm_nn(h1, *wmat("w_q", j, h1), name=f"q_proj{j}", head_major=True, spg=N_CHIPS).reshape(-1, bsz, seq, HEAD_DIM)
            outs, lses = [], []
            for g, dil in enumerate(DILATIONS):
                o_g, l_g = attn_fwd(q, kv_state["kv"], g, dil, hw, name=f"attn_fwd{j}_{g}")
                outs.append(o_g.reshape(-1, t, HEAD_DIM))
                lses.append(l_g.reshape(t, LANES))
            merged, lsej = attn_merge(outs, lses, name=f"attn_merge{j}")
            y = mm_nn(merged, *wmat("w_o", j, merged), out_dtype=BF16, name=f"o_proj{j}")
            sv.update(q=q, merged=merged, lsej=lsej)
        x1, h2 = resid_norm_fwd(xcur, y, _row(mix_post_g[i]), [_row(ffn_pre_g[i])], name=f"resid_mix{i}")
        ffn_after = start_group(2, x1, "gather_start_c") if i == 0 else h2
        p = mm_nn(h2, *wmat("ffn_w_in", i, ffn_after), out_dtype=BF16, name=f"ffn_in{i}", spg=2)
        s_act = ffn_mid_fwd(p.reshape(bsz, seq, f2), small_full["ffn_dw"][i], _row(ffn_dw_b[i]), name=f"ffn_mid{i}").reshape(t, f)
        y2 = mm_nn(s_act, *wmat("ffn_w_out", i), out_dtype=BF16, name=f"ffn_out{i}")
        next_gains = []
        if i + 1 < DEPTH:
            next_gains.append(_row(mix_pre_g[i + 1]))
        if i == N_A - 1:
            next_gains.append(_row(kv_norm_g))
        res = resid_norm_fwd(x1, y2, _row(ffn_post_g[i]), next_gains, name=f"resid_ffn{i}")
        sv.update(y=y, x1=x1, h2=h2, p=p, s=s_act, y2=y2)
        saved.append(sv)
        xcur = res[0]
        if i + 1 < DEPTH:
            h1 = res[1]
        if i == N_A - 1:
            kvn = res[2]
            kv = mm_nn(kvn, *wmat("w_kv", None, kvn), name="kv_proj", head_major=True, spg=2).reshape(-1, bsz, seq, HEAD_DIM)
            kv_state = {"kv": kv, "kvn": kvn, "x_a": xcur}

    dx, loss_tile = loss_fwd_bwd(xcur, loss_target.reshape(t, d))
    loss = lax.psum(loss_tile[0, 0], ("x", "y", "c"))

    gsm = {n: [None] * w_in[n].shape[0] for n in
           ["mix_pre_g", "mix_post_g", "ffn_pre_g", "ffn_post_g", "cm_b_in", "cm_dw", "cm_dw_b", "cm_ln_g", "cm_ln_b",
            "cm_b_out", "ffn_dw", "ffn_dw_b"]}
    gbig = {}
    in_flight = []
    dep = None

    def start_scatter(pcs, tag):
        ssem, rsem, g_f, r_f, tok = scatter_start([gbig[pc] for pc in pcs], name=f"scatter_start_{tag}")
        in_flight.append((pcs, ssem, rsem, g_f, r_f))
        return tok

    dk_buf = dv_buf = None
    for i in range(DEPTH - 1, -1, -1):
        sv = saved[i]
        dy2, dg, _ = norm_bwd(sv["y2"], _row(ffn_post_g[i]), dx, out_dtype=BF16, name=f"bwd_ffn_post{i}", dep=dep)
        gsm["ffn_post_g"][i] = dg
        ds = mm_nt(dy2, *wmat("ffn_w_out", i), out_dtype=BF16, name=f"bwd_ffn_out_dx{i}")
        gbig[("ffn_w_out", i)] = mm_tn(sv["s"], dy2, 1, name=f"bwd_ffn_out_dw{i}").reshape(N_CHIPS, f // N_CHIPS, d)
        dpa, dpg, ddwa, ddwg, ddba, ddbg = ffn_mid_bwd(sv["p"].reshape(bsz, seq, f2), small_full["ffn_dw"][i], _row(ffn_dw_b[i]),
                                                       ds.reshape(bsz, seq, f), name=f"bwd_ffn_mid{i}")
        gsm["ffn_dw"][i] = jnp.concatenate([jnp.sum(ddwa, axis=1), jnp.sum(ddwg, axis=1)], axis=-1)
        gsm["ffn_dw_b"][i] = jnp.concatenate([ddba, ddbg], axis=-1)
        dp = [dpa.reshape(t, f), dpg.reshape(t, f)]
        dh2 = mm_nt(dp, *wmat("ffn_w_in", i), out_dtype=BF16, name=f"bwd_ffn_in_dx{i}", spg=2)
        gbig[("ffn_w_in", i)] = mm_tn(sv["h2"], dp, N_CHIPS, name=f"bwd_ffn_in_dw{i}", tm=2048)
        dx1, dg, _ = norm_bwd(sv["x1"], _row(ffn_pre_g[i]), dh2, add=dx, name=f"bwd_ffn_pre{i}")
        gsm["ffn_pre_g"][i] = dg
        dep = start_scatter([("ffn_w_in", 0), ("ffn_w_out", 0)], "ffn0") if i == 0 else None
        dy, dg, dbias = norm_bwd(sv["y"], _row(mix_post_g[i]), dx1, out_dtype=BF16, name=f"bwd_mix_post{i}", dep=dep)
        gsm["mix_post_g"][i] = dg
        if i < N_A:
            gsm["cm_b_out"][i] = dbias
            du4 = mm_nt(dy, *wmat("cm_w_out", i), out_dtype=BF16, name=f"bwd_cm_out_dx{i}")
            gbig[("cm_w_out", i)] = mm_tn(sv["u4"], dy, 1, name=f"bwd_cm_out_dw{i}").reshape(N_CHIPS, d // N_CHIPS, d)
            du2, dlg, dlb = ln_silu_bwd(sv["u2"], _row(small_full["cm_ln_g"][i]), _row(small_full["cm_ln_b"][i]), du4,
                                        name=f"bwd_ln_silu{i}")
            gsm["cm_ln_g"][i], gsm["cm_ln_b"][i] = dlg, dlb
            dza, dzg, ddw, ddwb, dba, dbg = glu_conv_bwd(sv["z"].reshape(bsz, seq, 2 * d), small_full["cm_dw"][i],
                                                         du2.reshape(bsz, seq, d), name=f"bwd_glu_conv{i}")
            gsm["cm_dw"][i] = jnp.sum(ddw, axis=1)
            gsm["cm_dw_b"][i] = ddwb
            gsm["cm_b_in"][i] = jnp.concatenate([dba, dbg], axis=-1)
            dz = [dza.reshape(t, d), dzg.reshape(t, d)]
            dh1 = mm_nt(dz, *wmat("cm_w_in", i), out_dtype=BF16, name=f"bwd_cm_in_dx{i}", spg=2)
            gbig[("cm_w_in", i)] = mm_tn(sv["h1"], dz, N_CHIPS, name=f"bwd_cm_in_dw{i}", spg=2)
        else:
            j = i - N_A
            dmerged = mm_nt(dy, *wmat("w_o", j), name=f"bwd_o_proj_dx{j}")
            gbig[("w_o", j)] = mm_tn(sv["merged"], dy, 1, name=f"bwd_o_proj_dw{j}").reshape(N_CHIPS, hw // N_CHIPS, d)
            dmt = attn_bwd_prep(dmerged, sv["merged"], name=f"bwd_attn_prep{j}")
            dq_buf = None
            add_to_kv = dk_buf is not None
            for g, dil in enumerate(DILATIONS):
                dq_buf, dk_buf, dv_buf = attn_bwd(
                    sv["q"], kv_state["kv"], g, dil, dmerged.reshape(bsz, seq, hw), sv["lsej"].reshape(bsz, seq, LANES),
                    dmt.reshape(bsz, seq, LANES), dq_buf, dk_buf, dv_buf, add_to_kv, hw, name=f"attn_bwd{j}_{g}")
            dq = dq_buf.reshape(t, qw)
            dh1 = mm_nt(dq, *wmat("w_q", j), out_dtype=BF16, name=f"bwd_q_proj_dx{j}", tm=512, spg=N_CHIPS)
            gbig[("w_q", j)] = mm_tn(sv["h1"], dq, N_CHIPS, name=f"bwd_q_proj_dw{j}", tm=512, spg=N_CHIPS)
        dx, dg, _ = norm_bwd(sv["x_in"], _row(mix_pre_g[i]), dh1, add=dx1, name=f"bwd_mix_pre{i}")
        gsm["mix_pre_g"][i] = dg
        if i > N_A:
            dep = start_scatter([("ffn_w_in", i), ("ffn_w_out", i), ("w_q", i - N_A), ("w_o", i - N_A)], f"l{i}")
        elif 0 < i < N_A:
            dep = start_scatter([("ffn_w_in", i), ("ffn_w_out", i), ("cm_w_in", i), ("cm_w_out", i)], f"l{i}")
        elif i == 0:
            last_token = start_scatter([("cm_w_in", 0), ("cm_w_out", 0)], "cm0")
        if i == N_A:
            dkv = [dk_buf.reshape(t, qw), dv_buf.reshape(t, qw)]
            dkvn = mm_nt(dkv, *wmat("w_kv"), out_dtype=BF16, name="bwd_kv_proj_dx")
            gbig[("w_kv", None)] = mm_tn(kv_state["kvn"], dkv, N_CHIPS, name="bwd_kv_proj_dw")
            dx, dg_kv, _ = norm_bwd(kv_state["x_a"], _row(kv_norm_g), dkvn, add=dx, name="bwd_kv_norm")
            dep = start_scatter([("ffn_w_in", i), ("ffn_w_out", i), ("w_q", 0), ("w_o", 0), ("w_kv", None)], f"l{i}")
    grad_x = dx.reshape(bsz, seq, d)

    plane_of = {}
    outs_g, outs_d, outs_m, outs_v = {}, {}, {}, {}

    def finish_scatter(k, after):
        pcs, ssem, rsem, g_f, r_f = in_flight[k]
        g_done, r_done = scatter_wait(ssem, rsem, g_f, r_f, after, name=f"scatter_wait{k}")
        for (n, l), g_arr, r_arr in zip(pcs, g_done, r_done):
            n_layers = 1 if l is None else w_in[n].shape[0]
            plane_of[n] = sum_parts(g_arr, r_arr, me_arr, name=f"sum_chips_{n}_{l}", layer=l or 0, n_layers=n_layers,
                                    out_buf=plane_of.get(n))

    def update(group, tag, after):
        plane = [plane_of[n] for n in group]
        ssems, rsems, plane_f, land_f, _ = swap_start(plane, name=f"swap_start_{tag}")
        for k, n in enumerate(group):
            p_mine, p_other = swap_wait(ssems[k], rsems[k], plane_f[k], land_f[k], after, name=f"swap_wait_{n}")
            shp = w_in[n].shape
            flat = lambda a: a.reshape(-1, shp[-1])
            g_, d_, m_, v_ = adamw(flat(w_in[n]), flat(m_in[n]), flat(v_in[n]), [p_mine, p_other], name=f"adamw_{n}")
            outs_g[n], outs_d[n], outs_m[n], outs_v[n] = (a.reshape(shp) for a in (g_, d_, m_, v_))
            after = v_
        return after

    small_names = [n for n in names if n not in big]
    small_shapes_full = {}
    small_grads_full = []
    for n in small_names:
        if n == "kv_norm_g":
            gfull = dg_kv.reshape(-1)
        elif n in ("cm_dw", "ffn_dw"):
            gfull = jnp.stack(gsm[n], axis=0)
        else:
            gfull = jnp.stack([a.reshape(-1) for a in gsm[n]], axis=0)
        small_shapes_full[n] = gfull.shape
        small_grads_full.append(gfull)
    dev_arr = (4 * lax.axis_index("x") + 2 * lax.axis_index("y") + lax.axis_index("c")).astype(jnp.int32).reshape(1)
    small_land = cast_place(_pack(small_grads_full) + last_token[0, 0], None, dev_arr, F32, name="place_small_grads",
                            nslots=N_DEV)
    sm_send, sm_recv, small_land, small_token = small_start(small_land)

    for k in range(len(in_flight) - 1):
        finish_scatter(k, small_token)
    done = update(["w_kv", "w_q", "w_o", "ffn_w_in", "ffn_w_out"], "a", small_token)
    finish_scatter(len(in_flight) - 1, done)
    done = update(["cm_w_in", "cm_w_out"], "b", done)

    summed = sum_slots(small_wait(sm_send, sm_recv, small_land, done))
    g_full = dict(zip(small_names, _unpack(summed, [small_shapes_full[n] for n in small_names])))
    g_loc = {}
    for n in small_names:
        if n in small_sharded:
            width = w_in[n].shape[-1]
            g_loc[n] = lax.dynamic_slice_in_dim(g_full[n], me_chip * width, width, axis=g_full[n].ndim - 1)
        else:
            g_loc[n] = g_full[n]
    res = adamw(_pack([w_in[n] for n in small_names]), _pack([m_in[n] for n in small_names]),
                _pack([v_in[n] for n in small_names]), [_pack([g_loc[n] for n in small_names])], name="adamw_small")
    shapes_loc = [w_in[n].shape for n in small_names]
    for dst, packed in zip((outs_g, outs_d, outs_m, outs_v), res):
        for n, a in zip(small_names, _unpack(packed, shapes_loc)):
            dst[n] = a

    return (loss, grad_x, *[outs_g[n] for n in names], *[outs_d[n] for n in names],
            *[outs_m[n] for n in names], *[outs_v[n] for n in names])
```

```python
import math

import jax
import jax.numpy as jnp
from jax import lax
from jax.experimental import pallas as pl
from jax.experimental.pallas import tpu as pltpu

F32 = jnp.float32
BF16 = jnp.bfloat16
EPS = 1e-6
NEG_INF = -1e30
N_A = 2
DEPTH = 4
N_GROUPS = 3
DILATIONS = (1, 4, 16)
HEAD_DIM = 128
BLK = 128
LANES = 128
N_CHIPS = 4
N_DEV = 8
VMEM_LIMIT_V7X = 56 * 1024 * 1024

ADAM_LR = 0.001
ADAM_B1 = 0.9
ADAM_B2 = 0.999
ADAM_EPS = 1e-08
ADAM_WD = 0.01
ADAM_STEP = 10

MESH = pl.DeviceIdType.MESH


def _cp(*sem, **kw):
    return pltpu.CompilerParams(dimension_semantics=sem if sem else None, vmem_limit_bytes=VMEM_LIMIT_V7X, **kw)


def _dot(a, b):
    return jnp.dot(a, b, preferred_element_type=F32)


def _dot_nt(a, b):
    return lax.dot_general(a, b, (((1,), (1,)), ((), ())), preferred_element_type=F32)


def _dot_tn(a, b):
    return lax.dot_general(a, b, (((0,), (0,)), ((), ())), preferred_element_type=F32)


def _sigmoid(x):
    return 1.0 / (1.0 + jnp.exp(-x))


def _row_tile(n, want):
    if n <= want:
        return n
    for t in range(want - want % 8, 7, -8):
        if n % t == 0:
            return t
    raise ValueError(f"no row tile for {n} rows")


def mm_nn(a, w, nsh, bias=None, out_dtype=F32, name="mm_nn", tm=1024, head_major=False, spg=1):
    assert nsh % spg == 0
    m, k = a.shape
    _, k2, ns = w.shape
    assert k == k2
    tm = _row_tile(m, tm)
    has_bias = bias is not None
    hps = ns // HEAD_DIM

    def body(*refs):
        if has_bias:
            a_ref, w_ref, b_ref, o_ref = refs
        else:
            a_ref, w_ref, o_ref = refs
        av = a_ref[...].astype(BF16)
        for jj in range(spg):
            acc = _dot(av, w_ref[jj])
            if has_bias:
                acc = acc + b_ref[:, jj * ns:(jj + 1) * ns]
            if head_major:
                for hh in range(hps):
                    o_ref[jj * hps + hh] = acc[:, hh * HEAD_DIM:(hh + 1) * HEAD_DIM].astype(out_dtype)
            else:
                o_ref[:, jj * ns:(jj + 1) * ns] = acc.astype(out_dtype)

    in_specs = [
        pl.BlockSpec((tm, k), lambda j, i: (i, 0)),
        pl.BlockSpec((spg, k, ns), lambda j, i: (j, 0, 0)),
    ]
    args = [a, w]
    if has_bias:
        in_specs.append(pl.BlockSpec((1, spg * ns), lambda j, i: (0, j)))
        args.append(bias)
    return pl.pallas_call(
        body,
        name=name,
        grid=(nsh // spg, m // tm),
        in_specs=in_specs,
        out_specs=(pl.BlockSpec((spg * hps, tm, HEAD_DIM), lambda j, i: (j, i, 0)) if head_major
                   else pl.BlockSpec((tm, spg * ns), lambda j, i: (i, j))),
        out_shape=jax.ShapeDtypeStruct((nsh * hps, m, HEAD_DIM) if head_major else (m, nsh * ns), out_dtype),
        compiler_params=_cp("parallel", "parallel"),
    )(*args)


def _split_parts(dy, nsh, spg):
    dys = list(dy) if isinstance(dy, (list, tuple)) else [dy]
    per = nsh // len(dys)
    assert per % spg == 0
    return dys, per, per // spg


def mm_nt(dy, w, nsh, out_dtype=F32, name="mm_nt", tm=1024, spg=1):
    dys, per, gpp = _split_parts(dy, nsh, spg)
    npart = len(dys)
    m = dys[0].shape[0]
    _, k, ns = w.shape
    assert all(d.shape == (m, per * ns) for d in dys)
    tm = _row_tile(m, tm)
    ngrp = nsh // spg

    def body(*refs):
        dy_refs = refs[:npart]
        w_ref, o_ref, acc_ref = refs[npart:]
        j = pl.program_id(1)

        @pl.when(j == 0)
        def _():
            acc_ref[...] = jnp.zeros(acc_ref.shape, F32)

        for pi in range(npart):
            @pl.when(j // gpp == pi)
            def _(pi=pi):
                part = _dot_nt(dy_refs[pi][:, 0:ns].astype(BF16), w_ref[0])
                for jj in range(1, spg):
                    part = part + _dot_nt(dy_refs[pi][:, jj * ns:(jj + 1) * ns].astype(BF16), w_ref[jj])
                acc_ref[...] += part

        @pl.when(j == ngrp - 1)
        def _():
            o_ref[...] = acc_ref[...].astype(out_dtype)

    dy_specs = [pl.BlockSpec((tm, spg * ns), lambda i, j, pi=pi: (i, jnp.clip(j - pi * gpp, 0, gpp - 1)))
                for pi in range(npart)]
    return pl.pallas_call(
        body,
        name=name,
        grid=(m // tm, ngrp),
        in_specs=[*dy_specs, pl.BlockSpec((spg, k, ns), lambda i, j: (j, 0, 0))],
        out_specs=pl.BlockSpec((tm, k), lambda i, j: (i, 0)),
        out_shape=jax.ShapeDtypeStruct((m, k), out_dtype),
        scratch_shapes=[pltpu.VMEM((tm, k), F32)],
        compiler_params=_cp("parallel", "arbitrary"),
    )(*dys, w)


def mm_tn(a, dy, nsh, name="mm_tn", tm=1024, spg=1):
    dys, per, gpp = _split_parts(dy, nsh, spg)
    npart = len(dys)
    m, k = a.shape
    ns = dys[0].shape[1] // per
    assert all(d.shape == (m, per * ns) for d in dys)
    tm = _row_tile(m, tm)
    nt = m // tm

    def body(*refs):
        a_ref = refs[0]
        dy_refs = refs[1:1 + npart]
        o_ref, acc_ref = refs[1 + npart:]
        j = pl.program_id(0)
        i = pl.program_id(1)

        @pl.when(i == 0)
        def _():
            acc_ref[...] = jnp.zeros(acc_ref.shape, F32)

        for pi in range(npart):
            @pl.when(j // gpp == pi)
            def _(pi=pi):
                at = a_ref[...].astype(BF16).T
                for jj in range(spg):
                    acc_ref[jj] += _dot(at, dy_refs[pi][:, jj * ns:(jj + 1) * ns].astype(BF16))

        @pl.when(i == nt - 1)
        def _():
            o_ref[...] = acc_ref[...].astype(BF16)

    dy_specs = [
        pl.BlockSpec((tm, spg * ns),
                     lambda j, i, pi=pi: (jnp.where(j // gpp == pi, i, 0), jnp.clip(j - pi * gpp, 0, gpp - 1)))
        for pi in range(npart)
    ]
    return pl.pallas_call(
        body,
        name=name,
        grid=(nsh // spg, nt),
        in_specs=[pl.BlockSpec((tm, k), lambda j, i: (i, 0)), *dy_specs],
        out_specs=pl.BlockSpec((spg, k, ns), lambda j, i: (j, 0, 0)),
        out_shape=jax.ShapeDtypeStruct((nsh, k, ns), BF16),
        scratch_shapes=[pltpu.VMEM((spg, k, ns), F32)],
        compiler_params=_cp("parallel", "arbitrary"),
    )(a, *dys)


DEP_SPEC_SHAPE = (8, LANES)


def resid_norm_fwd(x, y, g_post, next_gains, name, tm=1024, dep=None):
    t, d = x.shape
    tm = _row_tile(t, tm)
    has_y = y is not None
    n_next = len(next_gains)
    n_dep = 0 if dep is None else 1

    def body(*refs):
        x_ref = refs[0]
        pos = 1
        if has_y:
            y_ref, gp_ref = refs[1], refs[2]
            pos = 3
        gn_refs = refs[pos:pos + n_next]
        outs = refs[pos + n_next + n_dep:]
        xv = x_ref[...]
        o = 0
        if has_y:
            yv = y_ref[...].astype(F32)
            r = lax.rsqrt(jnp.mean(yv * yv, axis=-1, keepdims=True) + EPS)
            xv = xv + (yv * r) * gp_ref[...]
            outs[0][...] = xv
            o = 1
        if n_next:
            xn = xv * lax.rsqrt(jnp.mean(xv * xv, axis=-1, keepdims=True) + EPS)
            for k in range(n_next):
                outs[o + k][...] = (xn * gn_refs[k][...]).astype(BF16)

    row = pl.BlockSpec((tm, d), lambda i: (i, 0))
    vec = pl.BlockSpec((1, d), lambda i: (0, 0))
    args, in_specs = [x], [row]
    if has_y:
        args += [y, g_post]
        in_specs += [row, vec]
    args += list(next_gains)
    in_specs += [vec] * n_next
    if n_dep:
        args.append(dep)
        in_specs.append(pl.BlockSpec(DEP_SPEC_SHAPE, lambda i: (0, 0)))
    out_shape, out_specs = [], []
    if has_y:
        out_shape.append(jax.ShapeDtypeStruct((t, d), F32))
        out_specs.append(row)
    for _ in range(n_next):
        out_shape.append(jax.ShapeDtypeStruct((t, d), BF16))
        out_specs.append(row)
    return pl.pallas_call(
        body, name=name, grid=(t // tm,), in_specs=in_specs, out_specs=out_specs, out_shape=out_shape,
        compiler_params=_cp("parallel"),
    )(*args)


def norm_bwd(x, g, dy, add=None, out_dtype=F32, name="norm_bwd", tm=1024, dep=None):
    t, d = x.shape
    tm = _row_tile(t, tm)
    has_add = add is not None

    def body(*refs):
        x_ref, g_ref, dy_ref = refs[:3]
        add_ref = refs[3] if has_add else None
        dx_ref, dg_ref, cs_ref = refs[-3:]
        i = pl.program_id(0)
        xv = x_ref[...].astype(F32)
        dyv = dy_ref[...].astype(F32)
        r = lax.rsqrt(jnp.mean(xv * xv, axis=-1, keepdims=True) + EPS)
        gd = dyv * g_ref[...]
        dx = r * gd - xv * ((r * r * r) * jnp.mean(xv * gd, axis=-1, keepdims=True))
        if has_add:
            dx = dx + add_ref[...]
        dx_ref[...] = dx.astype(out_dtype)
        dg = jnp.sum(dyv * (xv * r), axis=0, keepdims=True)
        cs = jnp.sum(dx, axis=0, keepdims=True)

        @pl.when(i == 0)
        def _():
            dg_ref[...] = dg
            cs_ref[...] = cs

        @pl.when(i > 0)
        def _():
            dg_ref[...] += dg
            cs_ref[...] += cs

    row = pl.BlockSpec((tm, d), lambda i: (i, 0))
    vec = pl.BlockSpec((1, d), lambda i: (0, 0))
    args, in_specs = [x, g, dy], [row, vec, row]
    if has_add:
        args.append(add)
        in_specs.append(row)
    if dep is not None:
        args.append(dep)
        in_specs.append(pl.BlockSpec(DEP_SPEC_SHAPE, lambda i: (0, 0)))
    return pl.pallas_call(
        body, name=name, grid=(t // tm,), in_specs=in_specs,
        out_specs=[row, vec, vec],
        out_shape=[jax.ShapeDtypeStruct((t, d), out_dtype), jax.ShapeDtypeStruct((1, d), F32),
                   jax.ShapeDtypeStruct((1, d), F32)],
        compiler_params=_cp("arbitrary"),
    )(*args)


def loss_fwd_bwd(x, target, name="loss", tm=1024):
    t, d = x.shape
    tm = _row_tile(t, tm)

    def body(x_ref, t_ref, dx_ref, l_ref):
        i = pl.program_id(0)
        err = x_ref[...] - t_ref[...]
        dx_ref[...] = err * (1.0 / d)
        part = 0.5 * jnp.sum(jnp.mean(err * err, axis=-1, keepdims=True), axis=0, keepdims=True)
        part = jnp.broadcast_to(part, l_ref.shape)

        @pl.when(i == 0)
        def _():
            l_ref[...] = part

        @pl.when(i > 0)
        def _():
            l_ref[...] += part

    row = pl.BlockSpec((tm, d), lambda i: (i, 0))
    return pl.pallas_call(
        body, name=name, grid=(t // tm,), in_specs=[row, row],
        out_specs=[row, pl.BlockSpec((8, LANES), lambda i: (0, 0))],
        out_shape=[jax.ShapeDtypeStruct((t, d), F32), jax.ShapeDtypeStruct((8, LANES), F32)],
        compiler_params=_cp("arbitrary"),
    )(x, target)


CONV_HALO = 32
CONV_CHUNK = 128


def glu_conv_fwd(z, dw, dwb, name, tc=128):
    b, s, c2 = z.shape
    c = c2 // 2
    kw = dw.shape[0]
    tc = min(tc, c)
    nc = c // tc
    ch = min(CONV_CHUNK, s)
    halo = CONV_HALO
    assert kw - 1 <= halo and s % ch == 0

    nch = s // ch

    def body(a_ref, g_ref, w_ref, b_ref, o_ref, pad_ref):
        _fill_glu_slabs(a_ref, g_ref, pad_ref, nch, ch, halo)

        def chunk(ci, carry):
            r0 = pl.multiple_of(ci * ch, ch)
            acc = b_ref[...]
            for k, tap in enumerate(_taps_front(pad_ref, ci, kw, ch, halo)):
                acc = acc + w_ref[k:k + 1, :] * tap
            o_ref[pl.ds(r0, ch), :] = acc
            return carry

        lax.fori_loop(0, nch, chunk, 0)

    return pl.pallas_call(
        body, name=name, grid=(b, nc),
        in_specs=[
            pl.BlockSpec((None, s, tc), lambda bi, i: (bi, 0, i)),
            pl.BlockSpec((None, s, tc), lambda bi, i: (bi, 0, i + nc)),
            pl.BlockSpec((kw, tc), lambda bi, i: (0, i)),
            pl.BlockSpec((1, tc), lambda bi, i: (0, i)),
        ],
        out_specs=pl.BlockSpec((None, s, tc), lambda bi, i: (bi, 0, i)),
        out_shape=jax.ShapeDtypeStruct((b, s, c), F32),
        scratch_shapes=[pltpu.VMEM((nch, ch + halo, tc), F32)],
        compiler_params=_cp("parallel", "parallel"),
    )(z, z, dw, dwb)


def glu_conv_bwd(z, dw, du2, name, tc=128):
    b, s, c2 = z.shape
    c = c2 // 2
    kw = dw.shape[0]
    tc = min(tc, c)
    nc = c // tc
    ch = min(CONV_CHUNK, s)
    nch = s // ch
    halo = CONV_HALO

    def body(a_ref, g_ref, w_ref, du_ref, dza_ref, dzg_ref, ddw_ref, ddwb_ref, dba_ref, dbg_ref, upad_ref, dpad_ref):
        bi = pl.program_id(1)

        @pl.when(bi == 0)
        def _():
            ddw_ref[...] = jnp.zeros(ddw_ref.shape, F32)
            ddwb_ref[...] = jnp.zeros(ddwb_ref.shape, F32)
            dba_ref[...] = jnp.zeros(dba_ref.shape, F32)
            dbg_ref[...] = jnp.zeros(dbg_ref.shape, F32)

        _fill_glu_slabs(a_ref, g_ref, upad_ref, nch, ch, halo)
        dpad_ref[nch - 1, ch:ch + halo, :] = jnp.zeros((halo, tc), F32)
        dpad_ref[nch - 1, 0:ch, :] = du_ref[s - ch:s, :]

        def fill(ci, carry):
            r0 = pl.multiple_of(ci * ch, ch)
            dpad_ref[ci, :, :] = du_ref[pl.ds(r0, ch + halo), :]
            return carry

        lax.fori_loop(0, nch - 1, fill, 0)

        def chunk(ci, carry):
            r0 = pl.multiple_of(ci * ch, ch)
            du_c = du_ref[pl.ds(r0, ch), :]
            taps_u = _taps_front(upad_ref, ci, kw, ch, halo)
            taps_d = _taps_at(dpad_ref, ci, list(range(kw)), ch)
            du1 = w_ref[kw - 1:kw, :] * du_c
            ddw_ref[kw - 1] += jnp.sum((du_c * taps_u[kw - 1]).reshape(ch // 8, 8, tc), axis=0)
            for j in range(1, kw):
                du1 = du1 + w_ref[kw - 1 - j:kw - j, :] * taps_d[j]
                ddw_ref[kw - 1 - j] += jnp.sum((du_c * taps_u[kw - 1 - j]).reshape(ch // 8, 8, tc), axis=0)
            av = a_ref[pl.ds(r0, ch), :]
            sg = _sigmoid(g_ref[pl.ds(r0, ch), :])
            dza = du1 * sg
            dzg = du1 * av * (sg * (1.0 - sg))
            dza_ref[pl.ds(r0, ch), :] = dza.astype(BF16)
            dzg_ref[pl.ds(r0, ch), :] = dzg.astype(BF16)
            dba_ref[...] += jnp.sum(dza, axis=0, keepdims=True)
            dbg_ref[...] += jnp.sum(dzg, axis=0, keepdims=True)
            ddwb_ref[...] += jnp.sum(du_c, axis=0, keepdims=True)
            return carry

        lax.fori_loop(0, s // ch, chunk, 0)

    blk = lambda off: pl.BlockSpec((None, s, tc), lambda i, bi: (bi, 0, i + off))
    vec = pl.BlockSpec((1, tc), lambda i, bi: (0, i))
    return pl.pallas_call(
        body, name=name, grid=(nc, b),
        in_specs=[blk(0), blk(nc), pl.BlockSpec((kw, tc), lambda i, bi: (0, i)), blk(0)],
        out_specs=[blk(0), blk(0), pl.BlockSpec((kw, 8, tc), lambda i, bi: (0, 0, i)), vec, vec, vec],
        out_shape=[
            jax.ShapeDtypeStruct((b, s, c), BF16), jax.ShapeDtypeStruct((b, s, c), BF16),
            jax.ShapeDtypeStruct((kw, 8, c), F32), jax.ShapeDtypeStruct((1, c), F32),
            jax.ShapeDtypeStruct((1, c), F32), jax.ShapeDtypeStruct((1, c), F32),
        ],
        scratch_shapes=[pltpu.VMEM((nch, ch + halo, tc), F32), pltpu.VMEM((nch, ch + halo, tc), F32)],
        compiler_params=_cp("parallel", "arbitrary"),
    )(z, z, dw, du2)


def _fill_glu_slabs(a_ref, g_ref, pad_ref, nch, ch, halo):
    tc = a_ref.shape[-1]
    pad_ref[0, 0:halo, :] = jnp.zeros((halo, tc), F32)
    pad_ref[0, halo:halo + ch, :] = a_ref[0:ch, :] * _sigmoid(g_ref[0:ch, :])

    def fill(ci, carry):
        r0 = pl.multiple_of(ci * ch, ch)
        pad_ref[ci, 0:halo, :] = pad_ref[ci - 1, ch:ch + halo, :]
        pad_ref[ci, halo:halo + ch, :] = a_ref[pl.ds(r0, ch), :] * _sigmoid(g_ref[pl.ds(r0, ch), :])
        return carry

    lax.fori_loop(1, nch, fill, 0)


def ln_silu_fwd(u, g, bvec, name, tm=1024):
    t, d = u.shape
    tm = _row_tile(t, tm)

    def body(u_ref, g_ref, b_ref, o_ref):
        uv = u_ref[...]
        mu = jnp.mean(uv, axis=-1, keepdims=True)
        xc = uv - mu
        var = jnp.mean(xc * xc, axis=-1, keepdims=True)
        v = (xc * lax.rsqrt(var + EPS)) * g_ref[...] + b_ref[...]
        o_ref[...] = (v * _sigmoid(v)).astype(BF16)

    row = pl.BlockSpec((tm, d), lambda i: (i, 0))
    vec = pl.BlockSpec((1, d), lambda i: (0, 0))
    return pl.pallas_call(
        body, name=name, grid=(t // tm,), in_specs=[row, vec, vec], out_specs=row,
        out_shape=jax.ShapeDtypeStruct((t, d), BF16), compiler_params=_cp("parallel"),
    )(u, g, bvec)


def ln_silu_bwd(u, g, bvec, dout, name, tm=1024):
    t, d = u.shape
    tm = _row_tile(t, tm)

    def body(u_ref, g_ref, b_ref, do_ref, du_ref, dg_ref, db_ref):
        i = pl.program_id(0)
        uv = u_ref[...]
        mu = jnp.mean(uv, axis=-1, keepdims=True)
        xc = uv - mu
        var = jnp.mean(xc * xc, axis=-1, keepdims=True)
        rstd = lax.rsqrt(var + EPS)
        n = xc * rstd
        v = n * g_ref[...] + b_ref[...]
        sg = _sigmoid(v)
        dv = do_ref[...].astype(F32) * (sg * (1.0 + v * (1.0 - sg)))
        dn = dv * g_ref[...]
        du_ref[...] = rstd * (dn - jnp.mean(dn, axis=-1, keepdims=True) - n * jnp.mean(dn * n, axis=-1, keepdims=True))
        dg = jnp.sum(dv * n, axis=0, keepdims=True)
        db = jnp.sum(dv, axis=0, keepdims=True)

        @pl.when(i == 0)
        def _():
            dg_ref[...] = dg
            db_ref[...] = db

        @pl.when(i > 0)
        def _():
            dg_ref[...] += dg
            db_ref[...] += db

    row = pl.BlockSpec((tm, d), lambda i: (i, 0))
    vec = pl.BlockSpec((1, d), lambda i: (0, 0))
    return pl.pallas_call(
        body, name=name, grid=(t // tm,), in_specs=[row, vec, vec, row], out_specs=[row, vec, vec],
        out_shape=[jax.ShapeDtypeStruct((t, d), F32), jax.ShapeDtypeStruct((1, d), F32), jax.ShapeDtypeStruct((1, d), F32)],
        compiler_params=_cp("arbitrary"),
    )(u, g, bvec, dout)


FFN_HALO = 8


def _fill_front_halo(src_ref, pad_ref, nch, ch, halo):
    tc = src_ref.shape[-1]
    pad_ref[0, 0:halo, :] = jnp.zeros((halo, tc), F32)
    pad_ref[0, halo:halo + ch, :] = src_ref[0:ch, :].astype(F32)

    def fill(ci, carry):
        r0 = pl.multiple_of(ci * ch, ch)
        pad_ref[ci, 0:halo, :] = src_ref[pl.ds(r0 - 2 * halo, 2 * halo), :].astype(F32)[halo:, :]
        pad_ref[ci, halo:halo + ch, :] = src_ref[pl.ds(r0, ch), :].astype(F32)
        return carry

    lax.fori_loop(1, nch, fill, 0)


def _taps_at(pad_ref, ci, offsets, ch):
    windows = {}
    for b in sorted({o % 8 for o in offsets}):
        top = max(o for o in offsets if o % 8 == b)
        windows[b] = pad_ref[ci, b:top + ch, :]
    return [windows[o % 8][o - o % 8:o - o % 8 + ch, :] for o in offsets]


def _taps_front(pad_ref, ci, kw, ch, halo):
    return _taps_at(pad_ref, ci, [halo - (kw - 1 - k) for k in range(kw)], ch)


def ffn_mid_fwd(p, dw, dwb, name, tc=256):
    b, s, f2 = p.shape
    f = f2 // 2
    kw = dw.shape[0]
    tc = min(tc, f)
    nf = f // tc
    ch = min(CONV_CHUNK, s)
    nch = s // ch
    halo = FFN_HALO

    def body(pa_ref, pg_ref, wa_ref, wg_ref, ba_ref, bg_ref, o_ref, ca_ref, cg_ref, apad_ref, gpad_ref):
        _fill_front_halo(pa_ref, apad_ref, nch, ch, halo)
        _fill_front_halo(pg_ref, gpad_ref, nch, ch, halo)

        def chunk(ci, carry):
            r0 = pl.multiple_of(ci * ch, ch)
            ca = ba_ref[...]
            cg = bg_ref[...]
            taps = zip(_taps_front(apad_ref, ci, kw, ch, halo), _taps_front(gpad_ref, ci, kw, ch, halo))
            for k, (ta, tg) in enumerate(taps):
                ca = ca + wa_ref[k:k + 1, :] * ta
                cg = cg + wg_ref[k:k + 1, :] * tg
            o_ref[pl.ds(r0, ch), :] = ((cg * _sigmoid(cg)) * ca).astype(BF16)
            ca_ref[pl.ds(r0, ch), :] = ca.astype(BF16)
            cg_ref[pl.ds(r0, ch), :] = cg.astype(BF16)
            return carry

        lax.fori_loop(0, nch, chunk, 0)

    blk = lambda off: pl.BlockSpec((None, s, tc), lambda bi, i: (bi, 0, i + off))
    wsp = lambda off: pl.BlockSpec((kw, tc), lambda bi, i: (0, i + off))
    bsp = lambda off: pl.BlockSpec((1, tc), lambda bi, i: (0, i + off))
    return pl.pallas_call(
        body, name=name, grid=(b, nf),
        in_specs=[blk(0), blk(nf), wsp(0), wsp(nf), bsp(0), bsp(nf)],
        out_specs=[blk(0)] * 3,
        out_shape=[jax.ShapeDtypeStruct((b, s, f), BF16)] * 3,
        scratch_shapes=[pltpu.VMEM((nch, ch + halo, tc), F32)] * 2,
        compiler_params=_cp("parallel", "parallel"),
    )(p, p, dw, dw, dwb, dwb)


def ffn_mid_bwd(p, dw, ca_sv, cg_sv, ds, name, tc=256):
    b, s, f2 = p.shape
    f = f2 // 2
    kw = dw.shape[0]
    tc = min(tc, f)
    nf = f // tc
    ch = min(CONV_CHUNK, s)
    nch = s // ch
    halo = FFN_HALO

    def sum8(v):
        return jnp.sum(v.reshape(ch // 8, 8, tc), axis=0)

    def body(pa_ref, pg_ref, wa_ref, wg_ref, ca_ref, cg_ref, ds_ref, dpa_ref, dpg_ref, ddwa_ref, ddwg_ref, dba_ref, dbg_ref,
             apad_ref, gpad_ref, dca_ref, dcg_ref):
        bi = pl.program_id(1)

        @pl.when(bi == 0)
        def _():
            ddwa_ref[...] = jnp.zeros(ddwa_ref.shape, F32)
            ddwg_ref[...] = jnp.zeros(ddwg_ref.shape, F32)
            dba_ref[...] = jnp.zeros(dba_ref.shape, F32)
            dbg_ref[...] = jnp.zeros(dbg_ref.shape, F32)

        _fill_front_halo(pa_ref, apad_ref, nch, ch, halo)
        _fill_front_halo(pg_ref, gpad_ref, nch, ch, halo)
        dca_ref[nch - 1, ch:ch + halo, :] = jnp.zeros((halo, tc), F32)
        dcg_ref[nch - 1, ch:ch + halo, :] = jnp.zeros((halo, tc), F32)

        def grads(ci, carry):
            acc_a, acc_g, sb_a, sb_g = carry
            r0 = pl.multiple_of(ci * ch, ch)
            taps_a = _taps_front(apad_ref, ci, kw, ch, halo)
            taps_g = _taps_front(gpad_ref, ci, kw, ch, halo)
            ca = ca_ref[pl.ds(r0, ch), :].astype(F32)
            cg = cg_ref[pl.ds(r0, ch), :].astype(F32)
            sg = _sigmoid(cg)
            dsv = ds_ref[pl.ds(r0, ch), :].astype(F32)
            dca = dsv * (cg * sg)
            dcg = dsv * ca * (sg * (1.0 + cg * (1.0 - sg)))
            dca_ref[ci, 0:ch, :] = dca
            dcg_ref[ci, 0:ch, :] = dcg

            prev = jnp.maximum(ci - 1, 0)

            @pl.when(ci > 0)
            def _():
                dca_ref[prev, ch:ch + halo, :] = dca[0:halo, :]
                dcg_ref[prev, ch:ch + halo, :] = dcg[0:halo, :]

            acc_a = tuple(acc_a[k] + sum8(dca * taps_a[k]) for k in range(kw))
            acc_g = tuple(acc_g[k] + sum8(dcg * taps_g[k]) for k in range(kw))
            return acc_a, acc_g, sb_a + sum8(dca), sb_g + sum8(dcg)

        z8 = jnp.zeros((8, tc), F32)
        acc_a, acc_g, sb_a, sb_g = lax.fori_loop(0, nch, grads, ((z8,) * kw, (z8,) * kw, z8, z8))
        for k in range(kw):
            ddwa_ref[k] += acc_a[k]
            ddwg_ref[k] += acc_g[k]
        dba_ref[...] += jnp.sum(sb_a, axis=0, keepdims=True)
        dbg_ref[...] += jnp.sum(sb_g, axis=0, keepdims=True)

        def back(ci, carry):
            r0 = pl.multiple_of(ci * ch, ch)
            da = wa_ref[kw - 1:kw, :] * dca_ref[ci, 0:ch, :]
            dg = wg_ref[kw - 1:kw, :] * dcg_ref[ci, 0:ch, :]
            for j in range(1, kw):
                da = da + wa_ref[kw - 1 - j:kw - j, :] * dca_ref[ci, j:j + ch, :]
                dg = dg + wg_ref[kw - 1 - j:kw - j, :] * dcg_ref[ci, j:j + ch, :]
            dpa_ref[pl.ds(r0, ch), :] = da.astype(BF16)
            dpg_ref[pl.ds(r0, ch), :] = dg.astype(BF16)
            return carry

        lax.fori_loop(0, nch, back, 0)

    blk = lambda off: pl.BlockSpec((None, s, tc), lambda i, bi: (bi, 0, i + off))
    wsp = lambda off: pl.BlockSpec((kw, tc), lambda i, bi: (0, i + off))
    acc3 = pl.BlockSpec((kw, 8, tc), lambda i, bi: (0, 0, i))
    vec = pl.BlockSpec((1, tc), lambda i, bi: (0, i))
    return pl.pallas_call(
        body, name=name, grid=(nf, b),
        in_specs=[blk(0), blk(nf), wsp(0), wsp(nf), blk(0), blk(0), blk(0)],
        out_specs=[blk(0), blk(0), acc3, acc3, vec, vec],
        out_shape=[jax.ShapeDtypeStruct((b, s, f), BF16), jax.ShapeDtypeStruct((b, s, f), BF16),
                   jax.ShapeDtypeStruct((kw, 8, f), F32), jax.ShapeDtypeStruct((kw, 8, f), F32),
                   jax.ShapeDtypeStruct((1, f), F32), jax.ShapeDtypeStruct((1, f), F32)],
        scratch_shapes=[pltpu.VMEM((nch, ch + halo, tc), F32)] * 4,
        compiler_params=_cp("parallel", "arbitrary"),
    )(p, p, dw, dw, ca_sv, cg_sv, ds)


def _tile_rows(r, n, dil):
    start = r + n * BLK * dil
    return pl.ds(start, BLK, stride=dil) if dil > 1 else pl.ds(start, BLK)


def _band_masks():
    qi = lax.broadcasted_iota(jnp.int32, (BLK, 2 * BLK), 0)
    kk = lax.broadcasted_iota(jnp.int32, (BLK, 2 * BLK), 1)
    both = jnp.logical_or(jnp.logical_and(kk < BLK, kk >= qi), jnp.logical_and(kk >= BLK, kk - BLK <= qi))
    return both, kk[:, :BLK] <= qi[:, :BLK]


def attn_fwd(q, kv, g, dil, hw, name):
    _, b, s, _ = q.shape
    nh = hw // HEAD_DIM
    nblk = s // dil // BLK
    scale = 1.0 / math.sqrt(HEAD_DIM)

    def body(q_ref, k_ref, v_ref, o_ref, lse_ref):
        h = pl.program_id(1)
        mask2, mask1 = _band_masks()
        mine = lax.broadcasted_iota(jnp.int32, (BLK, LANES), 1) == h

        @pl.when(h == 0)
        def _():
            lse_ref[...] = jnp.zeros(lse_ref.shape, F32)

        for r in range(dil):
            kp = vp = None
            for n in range(nblk):
                rs = _tile_rows(r, n, dil)
                qt = q_ref[rs, :].astype(BF16)
                kc = k_ref[rs, :].astype(BF16)
                vc = v_ref[rs, :].astype(BF16)
                if n == 0:
                    kcat, vcat, mask = kc, vc, mask1
                else:
                    kcat, vcat, mask = jnp.concatenate([kp, kc], axis=0), jnp.concatenate([vp, vc], axis=0), mask2
                sc = jnp.where(mask, _dot_nt(qt, kcat) * scale, NEG_INF)
                m = jnp.max(sc, axis=-1, keepdims=True)
                p = jnp.exp(sc - m)
                den = jnp.sum(p, axis=-1, keepdims=True)
                o_ref[rs, :] = _dot(p.astype(BF16), vcat) / den
                lse_ref[rs, :] = jnp.where(mine, m + jnp.log(den), lse_ref[rs, :])
                kp, vp = kc, vc

    col = lambda base: pl.BlockSpec((None, s, HEAD_DIM), lambda bi, h: (bi, 0, base + h))
    head = lambda base: pl.BlockSpec((None, None, s, HEAD_DIM), lambda bi, h: (base + h, bi, 0, 0))
    return pl.pallas_call(
        body, name=name, grid=(b, nh),
        in_specs=[head(g * nh), head(g * nh), head((N_GROUPS + g) * nh)],
        out_specs=[head(0), pl.BlockSpec((None, s, LANES), lambda bi, h: (bi, 0, 0))],
        out_shape=[jax.ShapeDtypeStruct((nh, b, s, HEAD_DIM), F32), jax.ShapeDtypeStruct((b, s, LANES), F32)],
        compiler_params=_cp("parallel", "arbitrary"),
    )(q, kv, kv)


def attn_merge(outs, lses, name, tm=512):
    nh, t, _ = outs[0].shape
    hw = nh * HEAD_DIM
    tm = _row_tile(t, tm)
    ng = len(outs)

    def body(*refs):
        o_refs, l_refs = refs[:ng], refs[ng:2 * ng]
        m_ref, lj_ref = refs[2 * ng:]
        ls = [l_refs[g][...] for g in range(ng)]
        mx = ls[0]
        for g in range(1, ng):
            mx = jnp.maximum(mx, ls[g])
        es = [jnp.exp(l - mx) for l in ls]
        tot = es[0]
        for g in range(1, ng):
            tot = tot + es[g]
        ws = [e / tot for e in es]
        lj_ref[...] = mx + jnp.log(tot)
        for h in range(nh):
            sl = slice(h * HEAD_DIM, (h + 1) * HEAD_DIM)
            acc = ws[0][:, h:h + 1] * o_refs[0][h]
            for g in range(1, ng):
                acc = acc + ws[g][:, h:h + 1] * o_refs[g][h]
            m_ref[:, sl] = acc.astype(BF16)

    row = pl.BlockSpec((tm, hw), lambda i: (i, 0))
    st = pl.BlockSpec((tm, LANES), lambda i: (i, 0))
    heads = pl.BlockSpec((nh, tm, HEAD_DIM), lambda i: (0, i, 0))
    return pl.pallas_call(
        body, name=name, grid=(t // tm,), in_specs=[heads] * ng + [st] * ng, out_specs=[row, st],
        out_shape=[jax.ShapeDtypeStruct((t, hw), BF16), jax.ShapeDtypeStruct((t, LANES), F32)],
        compiler_params=_cp("parallel"),
    )(*outs, *lses)


def attn_bwd_prep(dmerged, merged, name, tm=512):
    t, hw = merged.shape
    nh = hw // HEAD_DIM
    tm = _row_tile(t, tm)

    def body(d_ref, m_ref, o_ref):
        lane = lax.broadcasted_iota(jnp.int32, (tm, LANES), 1)
        acc = jnp.zeros((tm, LANES), F32)
        for h in range(nh):
            sl = slice(h * HEAD_DIM, (h + 1) * HEAD_DIM)
            dsum = jnp.sum(d_ref[:, sl] * m_ref[:, sl].astype(F32), axis=-1, keepdims=True)
            acc = jnp.where(lane == h, dsum, acc)
        o_ref[...] = acc

    row = pl.BlockSpec((tm, hw), lambda i: (i, 0))
    return pl.pallas_call(
        body, name=name, grid=(t // tm,), in_specs=[row, row], out_specs=pl.BlockSpec((tm, LANES), lambda i: (i, 0)),
        out_shape=jax.ShapeDtypeStruct((t, LANES), F32), compiler_params=_cp("parallel"),
    )(dmerged, merged)


def attn_bwd(q, kv, g, dil, do, lsej, dm, dq_buf, dk_buf, dv_buf, accumulate, hw, name):
    _, b, s, _ = q.shape
    nh = hw // HEAD_DIM
    nblk = s // dil // BLK
    scale = 1.0 / math.sqrt(HEAD_DIM)
    assert dk_buf is not None or not accumulate
    kv_at = 6 + (dq_buf is not None)

    def body(*refs):
        q_ref, k_ref, v_ref, do_ref, lj_ref, dm_ref = refs[:6]
        dq_ref, dk_ref, dv_ref = refs[-3:]
        dki_ref, dvi_ref = (refs[kv_at], refs[kv_at + 1]) if accumulate else (None, None)
        mask2, mask1 = _band_masks()
        mine = lax.broadcasted_iota(jnp.int32, (BLK, LANES), 1) == pl.program_id(1)

        def my_lane(v):
            return jnp.sum(jnp.where(mine, v, 0.0), axis=-1, keepdims=True)

        def put(rs, dk, dv):
            if accumulate:
                dk = dk + dki_ref[rs, :]
                dv = dv + dvi_ref[rs, :]
            dk_ref[rs, :] = dk
            dv_ref[rs, :] = dv

        for r in range(dil):
            kp = vp = hold_k = hold_v = rs_prev = None
            for n in range(nblk):
                rs = _tile_rows(r, n, dil)
                qt = q_ref[rs, :].astype(BF16)
                kc = k_ref[rs, :].astype(BF16)
                vc = v_ref[rs, :].astype(BF16)
                dot = do_ref[rs, :].astype(BF16)
                lm = my_lane(lj_ref[rs, :])
                dmm = my_lane(dm_ref[rs, :])
                if n == 0:
                    kcat, vcat, mask = kc, vc, mask1
                else:
                    kcat, vcat, mask = jnp.concatenate([kp, kc], axis=0), jnp.concatenate([vp, vc], axis=0), mask2
                p = jnp.exp(jnp.where(mask, _dot_nt(qt, kcat) * scale, NEG_INF) - lm)
                ds = (p * (_dot_nt(dot, vcat) - dmm)).astype(BF16)
                dq_ref[rs, :] = _dot(ds, kcat) * scale
                dkc = _dot_tn(ds, qt) * scale
                dvc = _dot_tn(p.astype(BF16), dot)
                if n > 0:
                    put(rs_prev, hold_k + dkc[:BLK, :], hold_v + dvc[:BLK, :])
                    dkc, dvc = dkc[BLK:, :], dvc[BLK:, :]
                hold_k, hold_v, kp, vp, rs_prev = dkc, dvc, kc, vc, rs
            put(rs_prev, hold_k, hold_v)

    col = lambda base: pl.BlockSpec((None, s, HEAD_DIM), lambda bi, h: (bi, 0, base + h))
    any_spec = pl.BlockSpec(memory_space=pl.ANY)
    stat = pl.BlockSpec((None, s, LANES), lambda bi, h: (bi, 0, 0))
    head = lambda base: pl.BlockSpec((None, None, s, HEAD_DIM), lambda bi, h: (base + h, bi, 0, 0))
    in_specs = [head(g * nh), head(g * nh), head((N_GROUPS + g) * nh), col(0), stat, stat]
    args = [q, kv, kv, do, lsej, dm]
    aliases = {}
    if dq_buf is not None:
        in_specs.append(any_spec)
        args.append(dq_buf)
        aliases[6] = 0
    if dk_buf is not None:
        in_specs += [col(g * nh) if accumulate else any_spec] * 2
        args += [dk_buf, dv_buf]
        aliases.update({kv_at: 1, kv_at + 1: 2})
    shape = jax.ShapeDtypeStruct((b, s, N_GROUPS * hw), F32)
    return pl.pallas_call(
        body, name=name, grid=(b, nh), in_specs=in_specs, out_specs=[col(g * nh)] * 3, out_shape=[shape] * 3,
        input_output_aliases=aliases, compiler_params=_cp("parallel", "parallel"),
    )(*args)


def sum_parts(g, recv, me, name, tm=512, layer=0, n_layers=1, out_buf=None):
    _, rows, c = g.shape
    n = recv.shape[0]
    tm = _row_tile(rows, tm)
    steps = rows // tm

    def body(me_ref, g_ref, r_ref, *rest):
        acc = g_ref[...].astype(F32)
        for j in range(n):
            acc = acc + r_ref[j].astype(F32)
        rest[-1][...] = acc

    in_specs = [pl.BlockSpec((None, tm, c), lambda i, me_ref: (me_ref[0], i, 0)),
                pl.BlockSpec((n, tm, c), lambda i, me_ref: (0, i, 0))]
    args = [me, g, recv]
    aliases = {}
    if out_buf is not None:
        in_specs.append(pl.BlockSpec(memory_space=pl.ANY))
        args.append(out_buf)
        aliases = {3: 0}
    return pl.pallas_call(
        body, name=name,
        grid_spec=pltpu.PrefetchScalarGridSpec(
            num_scalar_prefetch=1, grid=(steps,), in_specs=in_specs,
            out_specs=pl.BlockSpec((tm, c), lambda i, me_ref: (layer * steps + i, 0))),
        out_shape=jax.ShapeDtypeStruct((n_layers * rows, c), F32), input_output_aliases=aliases,
        compiler_params=_cp("parallel"),
    )(*args)


def adamw(w, m, v, g_parts, name, tm=256):
    rows, c = w.shape
    if c <= 8 * LANES:
        tm = 2 * tm
    tm = _row_tile(rows, tm)
    npart = len(g_parts)

    def body(*refs):
        w_ref, m_ref, v_ref = refs[:3]
        g_refs = refs[3:3 + npart]
        go_ref, d_ref, mo_ref, vo_ref = refs[3 + npart:]
        g = g_refs[0][...]
        for k in range(1, npart):
            g = g + g_refs[k][...]
        mn = ADAM_B1 * m_ref[...] + (1.0 - ADAM_B1) * g
        vn = ADAM_B2 * v_ref[...] + (1.0 - ADAM_B2) * (g * g)
        m_hat = mn / (1.0 - ADAM_B1 ** ADAM_STEP)
        v_hat = vn / (1.0 - ADAM_B2 ** ADAM_STEP)
        go_ref[...] = g
        d_ref[...] = -ADAM_LR * (m_hat / (jnp.sqrt(v_hat) + ADAM_EPS) + ADAM_WD * w_ref[...])
        mo_ref[...] = mn
        vo_ref[...] = vn

    row = pl.BlockSpec((tm, c), lambda i: (i, 0))
    return pl.pallas_call(
        body, name=name, grid=(rows // tm,), in_specs=[row] * (3 + npart), out_specs=[row] * 4,
        out_shape=[jax.ShapeDtypeStruct((rows, c), F32)] * 4, compiler_params=_cp("parallel"),
    )(w, m, v, *g_parts)


def _place():
    return lax.axis_index("x"), lax.axis_index("y"), lax.axis_index("c")


def _other_chips(x, y, c):
    return [(1 - x, y, c), (x, 1 - y, c), (1 - x, 1 - y, c)]


def _chip_of(px, py):
    return 2 * px + py


HBM_SPEC = pl.BlockSpec(memory_space=pltpu.HBM)
SEM_SPEC = pl.BlockSpec(memory_space=pltpu.SEMAPHORE)
ANY_SPEC = pl.BlockSpec(memory_space=pl.ANY)
DATAFLOW = pltpu.SideEffectType.DATAFLOW_SIDE_EFFECTING
N_PEER_CHIPS = N_CHIPS - 1


def _hbm(a):
    return pltpu.with_memory_space_constraint(a, pltpu.HBM)


def _hbm_like(arrays):
    return [pltpu.HBM(a.shape, a.dtype) for a in arrays]


def cast_place(w, layer, me, out_dtype, name, tm=512, nslots=N_CHIPS, dep=None):
    rows, c = w.shape[-2:]
    tm = _row_tile(rows, tm)

    def body(me_ref, w_ref, *rest):
        rest[-1][...] = w_ref[...].astype(out_dtype)

    if layer is None:
        in_specs = [pl.BlockSpec((tm, c), lambda i, me_ref: (i, 0))]
    else:
        in_specs = [pl.BlockSpec((None, tm, c), lambda i, me_ref: (layer, i, 0))]
    args = [me, w]
    if dep is not None:
        in_specs.append(pl.BlockSpec(DEP_SPEC_SHAPE, lambda i, me_ref: (0, 0)))
        args.append(dep)
    return pl.pallas_call(
        body, name=name,
        grid_spec=pltpu.PrefetchScalarGridSpec(
            num_scalar_prefetch=1, grid=(rows // tm,), in_specs=in_specs,
            out_specs=pl.BlockSpec((None, tm, c), lambda i, me_ref: (me_ref[0], i, 0))),
        out_shape=jax.ShapeDtypeStruct((nslots, rows, c), out_dtype), compiler_params=_cp("parallel"),
    )(*args)


def gather_start(lands, chunk_sizes, name="gather_start"):
    n = len(lands)
    nch = len(chunk_sizes)
    assert sum(chunk_sizes) == n

    def body(*refs):
        land_refs = refs[:n]
        outs = refs[n:]
        send_sems, recv_sems = outs[:nch], outs[nch:2 * nch]
        token = outs[-1]
        x, y, c = _place()
        me = _chip_of(x, y)
        peers = _other_chips(x, y, c)
        k = 0
        for ck, size in enumerate(chunk_sizes):
            for pos in range(size):
                for r, peer in enumerate(peers):
                    pltpu.make_async_remote_copy(
                        src_ref=land_refs[k].at[me], dst_ref=land_refs[k].at[me],
                        send_sem=send_sems[ck].at[N_PEER_CHIPS * pos + r], recv_sem=recv_sems[ck].at[N_PEER_CHIPS * pos + r],
                        device_id=peer, device_id_type=MESH).start()
                k += 1
        token[...] = jnp.zeros(token.shape, F32)

    sems = [pltpu.SemaphoreType.DMA((N_PEER_CHIPS * s,)) for s in chunk_sizes]
    res = pl.pallas_call(
        body, name=name,
        out_shape=(*sems, *sems, *_hbm_like(lands), jax.ShapeDtypeStruct(DEP_SPEC_SHAPE, F32)),
        in_specs=[HBM_SPEC] * n,
        out_specs=(*[SEM_SPEC] * (2 * nch), *[HBM_SPEC] * n, pl.BlockSpec(memory_space=pltpu.VMEM)),
        input_output_aliases={k: 2 * nch + k for k in range(n)},
        compiler_params=pltpu.CompilerParams(has_side_effects=DATAFLOW),
    )(*[_hbm(a) for a in lands])
    return res[:nch], res[nch:2 * nch], res[2 * nch:2 * nch + n], res[-1]


def gather_wait(send_sem, recv_sem, lands, after, name):
    n = len(lands)

    def body(*refs):
        land_refs = refs[:n]
        ssem, rsem = refs[n], refs[n + 1]
        x, y, c = _place()
        me = _chip_of(x, y)
        for pos in range(n):
            for r, peer in enumerate(_other_chips(x, y, c)):
                cp = pltpu.make_async_remote_copy(
                    src_ref=land_refs[pos].at[me], dst_ref=land_refs[pos].at[_chip_of(peer[0], peer[1])],
                    send_sem=ssem.at[N_PEER_CHIPS * pos + r], recv_sem=rsem.at[N_PEER_CHIPS * pos + r],
                    device_id=peer, device_id_type=MESH)
                cp.wait_send()
                cp.wait_recv()

    return pl.pallas_call(
        body, name=name, out_shape=tuple(_hbm_like(lands)),
        in_specs=[*[HBM_SPEC] * n, SEM_SPEC, SEM_SPEC, ANY_SPEC], out_specs=[HBM_SPEC] * n,
        input_output_aliases={k: k for k in range(n)},
        compiler_params=pltpu.CompilerParams(has_side_effects=DATAFLOW),
    )(*lands, send_sem, recv_sem, after)


def scatter_start(grads, name):
    n = len(grads)
    recvs = [lax.empty((N_PEER_CHIPS, *g.shape[1:]), g.dtype) for g in grads]

    def body(*refs):
        g_refs, r_refs = refs[:n], refs[n:2 * n]
        send_sems, recv_sems = refs[2 * n], refs[2 * n + 1]
        token = refs[-1]
        x, y, c = _place()
        for k in range(n):
            for r, peer in enumerate(_other_chips(x, y, c)):
                pltpu.make_async_remote_copy(
                    src_ref=g_refs[k].at[_chip_of(peer[0], peer[1])], dst_ref=r_refs[k].at[r],
                    send_sem=send_sems.at[N_PEER_CHIPS * k + r], recv_sem=recv_sems.at[N_PEER_CHIPS * k + r],
                    device_id=peer, device_id_type=MESH).start()
        token[...] = jnp.zeros(token.shape, F32)

    sem = pltpu.SemaphoreType.DMA((N_PEER_CHIPS * n,))
    res = pl.pallas_call(
        body, name=name,
        out_shape=(sem, sem, *_hbm_like(grads), *_hbm_like(recvs), jax.ShapeDtypeStruct(DEP_SPEC_SHAPE, F32)),
        in_specs=[HBM_SPEC] * (2 * n),
        out_specs=(SEM_SPEC, SEM_SPEC, *[HBM_SPEC] * (2 * n), pl.BlockSpec(memory_space=pltpu.VMEM)),
        input_output_aliases={k: 2 + k for k in range(2 * n)},
        compiler_params=pltpu.CompilerParams(has_side_effects=DATAFLOW),
    )(*[_hbm(a) for a in grads], *[_hbm(a) for a in recvs])
    return res[0], res[1], res[2:2 + n], res[2 + n:2 + 2 * n], res[-1]


def scatter_wait(send_sem, recv_sem, grads, recvs, after, name):
    n = len(grads)

    def body(*refs):
        g_refs, r_refs = refs[:n], refs[n:2 * n]
        ssem, rsem = refs[2 * n], refs[2 * n + 1]
        x, y, c = _place()
        for k in range(n):
            for r, peer in enumerate(_other_chips(x, y, c)):
                cp = pltpu.make_async_remote_copy(
                    src_ref=g_refs[k].at[_chip_of(peer[0], peer[1])], dst_ref=r_refs[k].at[r],
                    send_sem=ssem.at[N_PEER_CHIPS * k + r], recv_sem=rsem.at[N_PEER_CHIPS * k + r],
                    device_id=peer, device_id_type=MESH)
                cp.wait_send()
                cp.wait_recv()

    res = pl.pallas_call(
        body, name=name, out_shape=(*_hbm_like(grads), *_hbm_like(recvs)),
        in_specs=[*[HBM_SPEC] * (2 * n), SEM_SPEC, SEM_SPEC, ANY_SPEC], out_specs=[HBM_SPEC] * (2 * n),
        input_output_aliases={k: k for k in range(2 * n)},
        compiler_params=pltpu.CompilerParams(has_side_effects=DATAFLOW),
    )(*grads, *recvs, send_sem, recv_sem, after)
    return res[:n], res[n:]


def swap_start(parts, name):
    n = len(parts)
    lands = [lax.empty(p.shape, p.dtype) for p in parts]

    def body(*refs):
        p_refs, l_refs = refs[:n], refs[n:2 * n]
        sems = refs[2 * n:4 * n]
        token = refs[-1]
        x, y, c = _place()
        for k in range(n):
            pltpu.make_async_remote_copy(
                src_ref=p_refs[k], dst_ref=l_refs[k], send_sem=sems[k], recv_sem=sems[n + k],
                device_id=(x, y, 1 - c), device_id_type=MESH).start()
        token[...] = jnp.zeros(token.shape, F32)

    sem = pltpu.SemaphoreType.DMA(())
    res = pl.pallas_call(
        body, name=name,
        out_shape=(*[sem] * (2 * n), *_hbm_like(parts), *_hbm_like(lands), jax.ShapeDtypeStruct(DEP_SPEC_SHAPE, F32)),
        in_specs=[HBM_SPEC] * (2 * n),
        out_specs=(*[SEM_SPEC] * (2 * n), *[HBM_SPEC] * (2 * n), pl.BlockSpec(memory_space=pltpu.VMEM)),
        input_output_aliases={k: 2 * n + k for k in range(2 * n)},
        compiler_params=pltpu.CompilerParams(has_side_effects=DATAFLOW),
    )(*[_hbm(a) for a in parts], *[_hbm(a) for a in lands])
    return res[:n], res[n:2 * n], res[2 * n:3 * n], res[3 * n:4 * n], res[-1]


def swap_wait(send_sem, recv_sem, part, land, after, name):
    def body(p_ref, l_ref, ssem, rsem, after_ref, p_out, l_out):
        x, y, c = _place()
        cp = pltpu.make_async_remote_copy(src_ref=p_ref, dst_ref=l_ref, send_sem=ssem, recv_sem=rsem,
                                          device_id=(x, y, 1 - c), device_id_type=MESH)
        cp.wait_send()
        cp.wait_recv()

    return pl.pallas_call(
        body, name=name, out_shape=tuple(_hbm_like([part, land])),
        in_specs=[HBM_SPEC, HBM_SPEC, SEM_SPEC, SEM_SPEC, ANY_SPEC], out_specs=[HBM_SPEC, HBM_SPEC],
        input_output_aliases={0: 0, 1: 1},
        compiler_params=pltpu.CompilerParams(has_side_effects=DATAFLOW),
    )(part, land, send_sem, recv_sem, after)


def _xor_peer(x, y, c, k):
    px, py, pc = x ^ ((k >> 2) & 1), y ^ ((k >> 1) & 1), c ^ (k & 1)
    return (px, py, pc), 4 * px + 2 * py + pc


def small_start(land, name="small_start"):
    def body(l_ref, ssem, rsem, l_out, token):
        x, y, c = _place()
        me = 4 * x + 2 * y + c
        for k in range(1, N_DEV):
            peer, _ = _xor_peer(x, y, c, k)
            pltpu.make_async_remote_copy(
                src_ref=l_ref.at[me], dst_ref=l_ref.at[me], send_sem=ssem.at[k - 1], recv_sem=rsem.at[k - 1],
                device_id=peer, device_id_type=MESH).start()
        token[...] = jnp.zeros(token.shape, F32)

    sem = pltpu.SemaphoreType.DMA((N_DEV - 1,))
    return pl.pallas_call(
        body, name=name,
        out_shape=(sem, sem, pltpu.HBM(land.shape, land.dtype), jax.ShapeDtypeStruct(DEP_SPEC_SHAPE, F32)),
        in_specs=[HBM_SPEC], out_specs=(SEM_SPEC, SEM_SPEC, HBM_SPEC, pl.BlockSpec(memory_space=pltpu.VMEM)),
        input_output_aliases={0: 2}, compiler_params=pltpu.CompilerParams(has_side_effects=DATAFLOW),
    )(_hbm(land))


def small_wait(send_sem, recv_sem, land, after, name="small_wait"):
    def body(l_ref, ssem, rsem, after_ref, l_out):
        x, y, c = _place()
        me = 4 * x + 2 * y + c
        for k in range(1, N_DEV):
            peer, slot = _xor_peer(x, y, c, k)
            cp = pltpu.make_async_remote_copy(
                src_ref=l_ref.at[me], dst_ref=l_ref.at[slot], send_sem=ssem.at[k - 1], recv_sem=rsem.at[k - 1],
                device_id=peer, device_id_type=MESH)
            cp.wait_send()
            cp.wait_recv()

    return pl.pallas_call(
        body, name=name, out_shape=pltpu.HBM(land.shape, land.dtype),
        in_specs=[HBM_SPEC, SEM_SPEC, SEM_SPEC, ANY_SPEC], out_specs=HBM_SPEC, input_output_aliases={0: 0},
        compiler_params=pltpu.CompilerParams(has_side_effects=DATAFLOW),
    )(land, send_sem, recv_sem, after)


def sum_slots(land, name="sum_slots", tm=256):
    n, rows, c = land.shape
    tm = _row_tile(rows, tm)

    def body(l_ref, o_ref):
        acc = l_ref[0]
        for j in range(1, n):
            acc = acc + l_ref[j]
        o_ref[...] = acc

    return pl.pallas_call(
        body, name=name, grid=(rows // tm,), in_specs=[pl.BlockSpec((n, tm, c), lambda i: (0, i, 0))],
        out_specs=pl.BlockSpec((tm, c), lambda i: (i, 0)), out_shape=jax.ShapeDtypeStruct((rows, c), F32),
        compiler_params=_cp("parallel"),
    )(land)


PACK_ROW_TILE = 256


def _pack(arrays):
    flat = jnp.concatenate([a.reshape(-1).astype(F32) for a in arrays])
    n = flat.shape[0]
    rows = -(-n // LANES)
    rows = -(-rows // PACK_ROW_TILE) * PACK_ROW_TILE
    return jnp.pad(flat, (0, rows * LANES - n)).reshape(rows, LANES)


def _unpack(packed, shapes, lead=()):
    flat = packed.reshape(*lead, -1)
    out, off = [], 0
    for shp in shapes:
        n = math.prod(shp)
        out.append(flat[..., off:off + n].reshape(*lead, *shp))
        off += n
    return out


def _row(vec):
    return vec.reshape(1, -1)


def kernel(x, mix_pre_g, mix_post_g, ffn_pre_g, ffn_post_g, cm_w_in, cm_b_in, cm_dw, cm_dw_b, cm_ln_g, cm_ln_b, cm_w_out, cm_b_out, kv_norm_g, w_kv, w_q, w_o, ffn_w_in, ffn_dw, ffn_dw_b, ffn_w_out, loss_target, m_mix_pre_g, m_mix_post_g, m_ffn_pre_g, m_ffn_post_g, m_cm_w_in, m_cm_b_in, m_cm_dw, m_cm_dw_b, m_cm_ln_g, m_cm_ln_b, m_cm_w_out, m_cm_b_out, m_kv_norm_g, m_w_kv, m_w_q, m_w_o, m_ffn_w_in, m_ffn_dw, m_ffn_dw_b, m_ffn_w_out, v_mix_pre_g, v_mix_post_g, v_ffn_pre_g, v_ffn_post_g, v_cm_w_in, v_cm_b_in, v_cm_dw, v_cm_dw_b, v_cm_ln_g, v_cm_ln_b, v_cm_w_out, v_cm_b_out, v_kv_norm_g, v_w_kv, v_w_q, v_w_o, v_ffn_w_in, v_ffn_dw, v_ffn_dw_b, v_ffn_w_out):
    names = ["mix_pre_g", "mix_post_g", "ffn_pre_g", "ffn_post_g", "cm_w_in", "cm_b_in", "cm_dw", "cm_dw_b", "cm_ln_g",
             "cm_ln_b", "cm_w_out", "cm_b_out", "kv_norm_g", "w_kv", "w_q", "w_o", "ffn_w_in", "ffn_dw", "ffn_dw_b",
             "ffn_w_out"]
    w_in = dict(zip(names, [mix_pre_g, mix_post_g, ffn_pre_g, ffn_post_g, cm_w_in, cm_b_in, cm_dw, cm_dw_b, cm_ln_g,
                            cm_ln_b, cm_w_out, cm_b_out, kv_norm_g, w_kv, w_q, w_o, ffn_w_in, ffn_dw, ffn_dw_b, ffn_w_out]))
    m_in = dict(zip(names, [m_mix_pre_g, m_mix_post_g, m_ffn_pre_g, m_ffn_post_g, m_cm_w_in, m_cm_b_in, m_cm_dw, m_cm_dw_b,
                            m_cm_ln_g, m_cm_ln_b, m_cm_w_out, m_cm_b_out, m_kv_norm_g, m_w_kv, m_w_q, m_w_o, m_ffn_w_in,
                            m_ffn_dw, m_ffn_dw_b, m_ffn_w_out]))
    v_in = dict(zip(names, [v_mix_pre_g, v_mix_post_g, v_ffn_pre_g, v_ffn_post_g, v_cm_w_in, v_cm_b_in, v_cm_dw, v_cm_dw_b,
                            v_cm_ln_g, v_cm_ln_b, v_cm_w_out, v_cm_b_out, v_kv_norm_g, v_w_kv, v_w_q, v_w_o, v_ffn_w_in,
                            v_ffn_dw, v_ffn_dw_b, v_ffn_w_out]))

    bsz, seq, d = x.shape
    t = bsz * seq
    n_b = DEPTH - N_A
    hw = w_o.shape[-1]
    qw = N_GROUPS * hw
    f2 = ffn_dw_b.shape[-1]
    f = f2 // 2
    me_chip = _chip_of(lax.axis_index("x"), lax.axis_index("y"))

    big = ["cm_w_in", "cm_w_out", "w_kv", "w_q", "w_o", "ffn_w_in", "ffn_w_out"]
    row_sharded = ("cm_w_out", "w_o", "ffn_w_out")
    small_sharded = ["cm_b_in", "cm_dw", "cm_dw_b", "cm_ln_g", "cm_ln_b", "cm_b_out", "ffn_dw"]
    small_pack = _pack([w_in[n] for n in small_sharded])
    chunks = [
        [("cm_w_in", 0), ("small", None)],
        [("cm_w_out", 0)],
        [("ffn_w_in", 0), ("ffn_w_out", 0)],
        [("cm_w_in", 1), ("cm_w_out", 1)],
        [("ffn_w_in", 1), ("ffn_w_out", 1)],
        [("w_kv", None)],
        [("w_q", 0), ("w_o", 0)],
        [("ffn_w_in", 2), ("ffn_w_out", 2)],
        [("w_q", 1), ("w_o", 1)],
        [("ffn_w_in", 3), ("ffn_w_out", 3)],
    ]
    chunk_of = {pc: ck for ck, ch in enumerate(chunks) for pc in ch}

    me_arr = me_chip.astype(jnp.int32).reshape(1)

    def land_of(pc, dep=None):
        n, l = pc
        if n == "small":
            return cast_place(small_pack, None, me_arr, F32, name="place_small", dep=dep)
        return cast_place(w_in[n], l, me_arr, BF16, name=f"place_{n}_{l}", dep=dep)

    groups = [[0, 1], [2], list(range(3, len(chunks)))]
    g_send, g_recv, lands_f = {}, {}, {}

    def start_group(gi, dep, name):
        cks = groups[gi]
        pcs = [pc for ck in cks for pc in chunks[ck]]
        lands = [land_of(pc, dep if k == 0 else None) for k, pc in enumerate(pcs)]
        send, recv, lands_thru, tok = gather_start(lands, [len(chunks[ck]) for ck in cks], name=name)
        pos = 0
        for j, ck in enumerate(cks):
            g_send[ck], g_recv[ck] = send[j], recv[j]
            lands_f[ck] = lands_thru[pos:pos + len(chunks[ck])]
            pos += len(chunks[ck])
        return tok

    token_a = start_group(0, None, "gather_start_a")
    token = start_group(1, token_a, "gather_start_b")
    weights = {}

    def finish_chunk(ck, after):
        got = gather_wait(g_send[ck], g_recv[ck], lands_f[ck], after, name=f"gather_wait{ck}")
        for pc, arr in zip(chunks[ck], got):
            weights[pc] = arr.reshape(1, -1, arr.shape[-1]) if pc[0] in row_sharded else arr

    def wmat(n, l=None, after=None):
        if (n, l) not in weights:
            finish_chunk(chunk_of[(n, l)], after)
        arr = weights[(n, l)]
        return arr, arr.shape[0]

    finish_chunk(0, token)
    small_full = {}
    for n, arr4 in zip(small_sharded, _unpack(weights[("small", None)], [w_in[n].shape for n in small_sharded], lead=(N_CHIPS,))):
        shp = w_in[n].shape
        small_full[n] = jnp.moveaxis(arr4, 0, -2).reshape(*shp[:-1], N_CHIPS * shp[-1])

    x2d = x.reshape(t, d)
    saved = []
    (h1,) = resid_norm_fwd(x2d, None, None, [_row(mix_pre_g[0])], name="norm_in", dep=token)
    xcur = x2d
    kv_state = None
    for i in range(DEPTH):
        sv = {"x_in": xcur, "h1": h1}
        if i < N_A:
            z = mm_nn(h1, *wmat("cm_w_in", i, h1), bias=_row(small_full["cm_b_in"][i]), name=f"cm_in{i}",
                      spg=N_CHIPS)
            u2 = glu_conv_fwd(z.reshape(bsz, seq, 2 * d), small_full["cm_dw"][i], _row(small_full["cm_dw_b"][i]),
                              name=f"glu_conv{i}").reshape(t, d)
            u4 = ln_silu_fwd(u2, _row(small_full["cm_ln_g"][i]), _row(small_full["cm_ln_b"][i]), name=f"ln_silu{i}")
            y = mm_nn(u4, *wmat("cm_w_out", i, u4), bias=_row(small_full["cm_b_out"][i]), out_dtype=BF16,
                      name=f"cm_out{i}")
            sv.update(z=z, u2=u2, u4=u4)
        else:
            j = i - N_A
            q = mm_nn(h1, *wmat("w_q", j, h1), name=f"q_proj{j}", head_major=True, spg=N_CHIPS).reshape(-1, bsz, seq, HEAD_DIM)
            outs, lses = [], []
            for g, dil in enumerate(DILATIONS):
                o_g, l_g = attn_fwd(q, kv_state["kv"], g, dil, hw, name=f"attn_fwd{j}_{g}")
                outs.append(o_g.reshape(-1, t, HEAD_DIM))
                lses.append(l_g.reshape(t, LANES))
            merged, lsej = attn_merge(outs, lses, name=f"attn_merge{j}")
            y = mm_nn(merged, *wmat("w_o", j, merged), out_dtype=BF16, name=f"o_proj{j}")
            sv.update(q=q, merged=merged, lsej=lsej)
        x1, h2 = resid_norm_fwd(xcur, y, _row(mix_post_g[i]), [_row(ffn_pre_g[i])], name=f"resid_mix{i}")
        ffn_after = start_group(2, x1, "gather_start_c") if i == 0 else h2
        p = mm_nn(h2, *wmat("ffn_w_in", i, ffn_after), out_dtype=BF16, name=f"ffn_in{i}", spg=2)
        s_act, conv_a, conv_g = ffn_mid_fwd(p.reshape(bsz, seq, f2), small_full["ffn_dw"][i], _row(ffn_dw_b[i]), name=f"ffn_mid{i}")
        s_act = s_act.reshape(t, f)
        y2 = mm_nn(s_act, *wmat("ffn_w_out", i), out_dtype=BF16, name=f"ffn_out{i}")
        next_gains = []
        if i + 1 < DEPTH:
            next_gains.append(_row(mix_pre_g[i + 1]))
        if i == N_A - 1:
            next_gains.append(_row(kv_norm_g))
        res = resid_norm_fwd(x1, y2, _row(ffn_post_g[i]), next_gains, name=f"resid_ffn{i}")
        sv.update(y=y, x1=x1, h2=h2, p=p, s=s_act, y2=y2, conv_a=conv_a, conv_g=conv_g)
        saved.append(sv)
        xcur = res[0]
        if i + 1 < DEPTH:
            h1 = res[1]
        if i == N_A - 1:
            kvn = res[2]
            kv = mm_nn(kvn, *wmat("w_kv", None, kvn), name="kv_proj", head_major=True, spg=2).reshape(-1, bsz, seq, HEAD_DIM)
            kv_state = {"kv": kv, "kvn": kvn, "x_a": xcur}

    dx, loss_tile = loss_fwd_bwd(xcur, loss_target.reshape(t, d))
    loss = lax.psum(loss_tile[0, 0], ("x", "y", "c"))

    gsm = {n: [None] * w_in[n].shape[0] for n in
           ["mix_pre_g", "mix_post_g", "ffn_pre_g", "ffn_post_g", "cm_b_in", "cm_dw", "cm_dw_b", "cm_ln_g", "cm_ln_b",
            "cm_b_out", "ffn_dw", "ffn_dw_b"]}
    gbig = {}
    in_flight = []
    dep = None

    def start_scatter(pcs, tag):
        ssem, rsem, g_f, r_f, tok = scatter_start([gbig[pc] for pc in pcs], name=f"scatter_start_{tag}")
        in_flight.append((pcs, ssem, rsem, g_f, r_f))
        return tok

    dk_buf = dv_buf = None
    for i in range(DEPTH - 1, -1, -1):
        sv = saved[i]
        dy2, dg, _ = norm_bwd(sv["y2"], _row(ffn_post_g[i]), dx, out_dtype=BF16, name=f"bwd_ffn_post{i}", dep=dep)
        gsm["ffn_post_g"][i] = dg
        ds = mm_nt(dy2, *wmat("ffn_w_out", i), out_dtype=BF16, name=f"bwd_ffn_out_dx{i}")
        gbig[("ffn_w_out", i)] = mm_tn(sv["s"], dy2, 1, name=f"bwd_ffn_out_dw{i}").reshape(N_CHIPS, f // N_CHIPS, d)
        dpa, dpg, ddwa, ddwg, ddba, ddbg = ffn_mid_bwd(sv["p"].reshape(bsz, seq, f2), small_full["ffn_dw"][i], sv["conv_a"],
                                                       sv["conv_g"], ds.reshape(bsz, seq, f), name=f"bwd_ffn_mid{i}")
        gsm["ffn_dw"][i] = jnp.concatenate([jnp.sum(ddwa, axis=1), jnp.sum(ddwg, axis=1)], axis=-1)
        gsm["ffn_dw_b"][i] = jnp.concatenate([ddba, ddbg], axis=-1)
        dp = [dpa.reshape(t, f), dpg.reshape(t, f)]
        dh2 = mm_nt(dp, *wmat("ffn_w_in", i), out_dtype=BF16, name=f"bwd_ffn_in_dx{i}", spg=2)
        gbig[("ffn_w_in", i)] = mm_tn(sv["h2"], dp, N_CHIPS, name=f"bwd_ffn_in_dw{i}", tm=2048)
        dx1, dg, _ = norm_bwd(sv["x1"], _row(ffn_pre_g[i]), dh2, add=dx, name=f"bwd_ffn_pre{i}")
        gsm["ffn_pre_g"][i] = dg
        dep = start_scatter([("ffn_w_in", 0), ("ffn_w_out", 0)], "ffn0") if i == 0 else None
        dy, dg, dbias = norm_bwd(sv["y"], _row(mix_post_g[i]), dx1, out_dtype=BF16, name=f"bwd_mix_post{i}", dep=dep)
        gsm["mix_post_g"][i] = dg
        if i < N_A:
            gsm["cm_b_out"][i] = dbias
            du4 = mm_nt(dy, *wmat("cm_w_out", i), out_dtype=BF16, name=f"bwd_cm_out_dx{i}")
            gbig[("cm_w_out", i)] = mm_tn(sv["u4"], dy, 1, name=f"bwd_cm_out_dw{i}").reshape(N_CHIPS, d // N_CHIPS, d)
            du2, dlg, dlb = ln_silu_bwd(sv["u2"], _row(small_full["cm_ln_g"][i]), _row(small_full["cm_ln_b"][i]), du4,
                                        name=f"bwd_ln_silu{i}")
            gsm["cm_ln_g"][i], gsm["cm_ln_b"][i] = dlg, dlb
            dza, dzg, ddw, ddwb, dba, dbg = glu_conv_bwd(sv["z"].reshape(bsz, seq, 2 * d), small_full["cm_dw"][i],
                                                         du2.reshape(bsz, seq, d), name=f"bwd_glu_conv{i}")
            gsm["cm_dw"][i] = jnp.sum(ddw, axis=1)
            gsm["cm_dw_b"][i] = ddwb
            gsm["cm_b_in"][i] = jnp.concatenate([dba, dbg], axis=-1)
            dz = [dza.reshape(t, d), dzg.reshape(t, d)]
            dh1 = mm_nt(dz, *wmat("cm_w_in", i), out_dtype=BF16, name=f"bwd_cm_in_dx{i}", spg=2)
            gbig[("cm_w_in", i)] = mm_tn(sv["h1"], dz, N_CHIPS, name=f"bwd_cm_in_dw{i}", spg=2)
        else:
            j = i - N_A
            dmerged = mm_nt(dy, *wmat("w_o", j), name=f"bwd_o_proj_dx{j}")
            gbig[("w_o", j)] = mm_tn(sv["merged"], dy, 1, name=f"bwd_o_proj_dw{j}").reshape(N_CHIPS, hw // N_CHIPS, d)
            dmt = attn_bwd_prep(dmerged, sv["merged"], name=f"bwd_attn_prep{j}")
            dq_buf = None
            add_to_kv = dk_buf is not None
            for g, dil in enumerate(DILATIONS):
                dq_buf, dk_buf, dv_buf = attn_bwd(
                    sv["q"], kv_state["kv"], g, dil, dmerged.reshape(bsz, seq, hw), sv["lsej"].reshape(bsz, seq, LANES),
                    dmt.reshape(bsz, seq, LANES), dq_buf, dk_buf, dv_buf, add_to_kv, hw, name=f"attn_bwd{j}_{g}")
            dq = dq_buf.reshape(t, qw)
            dh1 = mm_nt(dq, *wmat("w_q", j), out_dtype=BF16, name=f"bwd_q_proj_dx{j}", tm=512, spg=N_CHIPS)
            gbig[("w_q", j)] = mm_tn(sv["h1"], dq, N_CHIPS, name=f"bwd_q_proj_dw{j}", tm=512, spg=N_CHIPS)
        dx, dg, _ = norm_bwd(sv["x_in"], _row(mix_pre_g[i]), dh1, add=dx1, name=f"bwd_mix_pre{i}")
        gsm["mix_pre_g"][i] = dg
        if i > N_A:
            dep = start_scatter([("ffn_w_in", i), ("ffn_w_out", i), ("w_q", i - N_A), ("w_o", i - N_A)], f"l{i}")
        elif 0 < i < N_A:
            dep = start_scatter([("ffn_w_in", i), ("ffn_w_out", i), ("cm_w_in", i), ("cm_w_out", i)], f"l{i}")
        elif i == 0:
            last_token = start_scatter([("cm_w_in", 0), ("cm_w_out", 0)], "cm0")
        if i == N_A:
            dkv = [dk_buf.reshape(t, qw), dv_buf.reshape(t, qw)]
            dkvn = mm_nt(dkv, *wmat("w_kv"), out_dtype=BF16, name="bwd_kv_proj_dx")
            gbig[("w_kv", None)] = mm_tn(kv_state["kvn"], dkv, N_CHIPS, name="bwd_kv_proj_dw")
            dx, dg_kv, _ = norm_bwd(kv_state["x_a"], _row(kv_norm_g), dkvn, add=dx, name="bwd_kv_norm")
            dep = start_scatter([("ffn_w_in", i), ("ffn_w_out", i), ("w_q", 0), ("w_o", 0), ("w_kv", None)], f"l{i}")
    grad_x = dx.reshape(bsz, seq, d)

    plane_of = {}
    outs_g, outs_d, outs_m, outs_v = {}, {}, {}, {}

    def finish_scatter(k, after):
        pcs, ssem, rsem, g_f, r_f = in_flight[k]
        g_done, r_done = scatter_wait(ssem, rsem, g_f, r_f, after, name=f"scatter_wait{k}")
        for (n, l), g_arr, r_arr in zip(pcs, g_done, r_done):
            n_layers = 1 if l is None else w_in[n].shape[0]
            plane_of[n] = sum_parts(g_arr, r_arr, me_arr, name=f"sum_chips_{n}_{l}", layer=l or 0, n_layers=n_layers,
                                    out_buf=plane_of.get(n))

    def update(group, tag, after):
        plane = [plane_of[n] for n in group]
        ssems, rsems, plane_f, land_f, _ = swap_start(plane, name=f"swap_start_{tag}")
        for k, n in enumerate(group):
            p_mine, p_other = swap_wait(ssems[k], rsems[k], plane_f[k], land_f[k], after, name=f"swap_wait_{n}")
            shp = w_in[n].shape
            flat = lambda a: a.reshape(-1, shp[-1])
            g_, d_, m_, v_ = adamw(flat(w_in[n]), flat(m_in[n]), flat(v_in[n]), [p_mine, p_other], name=f"adamw_{n}")
            outs_g[n], outs_d[n], outs_m[n], outs_v[n] = (a.reshape(shp) for a in (g_, d_, m_, v_))
            after = v_
        return after

    small_names = [n for n in names if n not in big]
    small_shapes_full = {}
    small_grads_full = []
    for n in small_names:
        if n == "kv_norm_g":
            gfull = dg_kv.reshape(-1)
        elif n in ("cm_dw", "ffn_dw"):
            gfull = jnp.stack(gsm[n], axis=0)
        else:
            gfull = jnp.stack([a.reshape(-1) for a in gsm[n]], axis=0)
        small_shapes_full[n] = gfull.shape
        small_grads_full.append(gfull)
    dev_arr = (4 * lax.axis_index("x") + 2 * lax.axis_index("y") + lax.axis_index("c")).astype(jnp.int32).reshape(1)
    small_land = cast_place(_pack(small_grads_full) + last_token[0, 0], None, dev_arr, F32, name="place_small_grads",
                            nslots=N_DEV)
    sm_send, sm_recv, small_land, small_token = small_start(small_land)

    for k in range(len(in_flight) - 1):
        finish_scatter(k, small_token)
    done = update(["w_kv", "w_q", "w_o", "ffn_w_in", "ffn_w_out"], "a", small_token)
    finish_scatter(len(in_flight) - 1, done)
    done = update(["cm_w_in", "cm_w_out"], "b", done)

    summed = sum_slots(small_wait(sm_send, sm_recv, small_land, done))
    g_full = dict(zip(small_names, _unpack(summed, [small_shapes_full[n] for n in small_names])))
    g_loc = {}
    for n in small_names:
        if n in small_sharded:
            width = w_in[n].shape[-1]
            g_loc[n] = lax.dynamic_slice_in_dim(g_full[n], me_chip * width, width, axis=g_full[n].ndim - 1)
        else:
            g_loc[n] = g_full[n]
    res = adamw(_pack([w_in[n] for n in small_names]), _pack([m_in[n] for n in small_names]),
                _pack([v_in[n] for n in small_names]), [_pack([g_loc[n] for n in small_names])], name="adamw_small")
    shapes_loc = [w_in[n].shape for n in small_names]
    for dst, packed in zip((outs_g, outs_d, outs_m, outs_v), res):
        for n, a in zip(small_names, _unpack(packed, shapes_loc)):
            dst[n] = a

    return (loss, grad_x, *[outs_g[n] for n in names], *[outs_d[n] for n in names],
            *[outs_m[n] for n in names], *[outs_v[n] for n in names])
```

```python
import math

import jax
import jax.numpy as jnp
from jax import lax
from jax.experimental import pallas as pl
from jax.experimental.pallas import tpu as pltpu

F32 = jnp.float32
BF16 = jnp.bfloat16
EPS = 1e-6
NEG_INF = -1e30
N_A = 2
DEPTH = 4
N_GROUPS = 3
DILATIONS = (1, 4, 16)
HEAD_DIM = 128
BLK = 128
LANES = 128
N_CHIPS = 4
N_DEV = 8
VMEM_LIMIT_V7X = 56 * 1024 * 1024

ADAM_LR = 0.001
ADAM_B1 = 0.9
ADAM_B2 = 0.999
ADAM_EPS = 1e-08
ADAM_WD = 0.01
ADAM_STEP = 10

MESH = pl.DeviceIdType.MESH


def _cp(*sem, **kw):
    return pltpu.CompilerParams(dimension_semantics=sem if sem else None, vmem_limit_bytes=VMEM_LIMIT_V7X, **kw)


def _dot(a, b):
    return jnp.dot(a, b, preferred_element_type=F32)


def _dot_nt(a, b):
    return lax.dot_general(a, b, (((1,), (1,)), ((), ())), preferred_element_type=F32)


def _dot_tn(a, b):
    return lax.dot_general(a, b, (((0,), (0,)), ((), ())), preferred_element_type=F32)


def _sigmoid(x):
    return 1.0 / (1.0 + jnp.exp(-x))


def _row_tile(n, want):
    if n <= want:
        return n
    for t in range(want - want % 8, 7, -8):
        if n % t == 0:
            return t
    raise ValueError(f"no row tile for {n} rows")


def mm_nn(a, w, nsh, bias=None, out_dtype=F32, name="mm_nn", tm=1024, head_major=False, spg=1):
    assert nsh % spg == 0
    m, k = a.shape
    _, k2, ns = w.shape
    assert k == k2
    tm = _row_tile(m, tm)
    has_bias = bias is not None
    hps = ns // HEAD_DIM

    def body(*refs):
        if has_bias:
            a_ref, w_ref, b_ref, o_ref = refs
        else:
            a_ref, w_ref, o_ref = refs
        av = a_ref[...].astype(BF16)
        for jj in range(spg):
            acc = _dot(av, w_ref[jj])
            if has_bias:
                acc = acc + b_ref[:, jj * ns:(jj + 1) * ns]
            if head_major:
                for hh in range(hps):
                    o_ref[jj * hps + hh] = acc[:, hh * HEAD_DIM:(hh + 1) * HEAD_DIM].astype(out_dtype)
            else:
                o_ref[:, jj * ns:(jj + 1) * ns] = acc.astype(out_dtype)

    in_specs = [
        pl.BlockSpec((tm, k), lambda j, i: (i, 0)),
        pl.BlockSpec((spg, k, ns), lambda j, i: (j, 0, 0)),
    ]
    args = [a, w]
    if has_bias:
        in_specs.append(pl.BlockSpec((1, spg * ns), lambda j, i: (0, j)))
        args.append(bias)
    return pl.pallas_call(
        body,
        name=name,
        grid=(nsh // spg, m // tm),
        in_specs=in_specs,
        out_specs=(pl.BlockSpec((spg * hps, tm, HEAD_DIM), lambda j, i: (j, i, 0)) if head_major
                   else pl.BlockSpec((tm, spg * ns), lambda j, i: (i, j))),
        out_shape=jax.ShapeDtypeStruct((nsh * hps, m, HEAD_DIM) if head_major else (m, nsh * ns), out_dtype),
        compiler_params=_cp("parallel", "parallel"),
    )(*args)


def _split_parts(dy, nsh, spg):
    dys = list(dy) if isinstance(dy, (list, tuple)) else [dy]
    per = nsh // len(dys)
    assert per % spg == 0
    return dys, per, per // spg


def mm_nt(dy, w, nsh, out_dtype=F32, name="mm_nt", tm=1024, spg=1):
    dys, per, gpp = _split_parts(dy, nsh, spg)
    npart = len(dys)
    m = dys[0].shape[0]
    _, k, ns = w.shape
    assert all(d.shape == (m, per * ns) for d in dys)
    tm = _row_tile(m, tm)
    ngrp = nsh // spg

    def body(*refs):
        dy_refs = refs[:npart]
        w_ref, o_ref, acc_ref = refs[npart:]
        j = pl.program_id(1)

        @pl.when(j == 0)
        def _():
            acc_ref[...] = jnp.zeros(acc_ref.shape, F32)

        for pi in range(npart):
            @pl.when(j // gpp == pi)
            def _(pi=pi):
                part = _dot_nt(dy_refs[pi][:, 0:ns].astype(BF16), w_ref[0])
                for jj in range(1, spg):
                    part = part + _dot_nt(dy_refs[pi][:, jj * ns:(jj + 1) * ns].astype(BF16), w_ref[jj])
                acc_ref[...] += part

        @pl.when(j == ngrp - 1)
        def _():
            o_ref[...] = acc_ref[...].astype(out_dtype)

    dy_specs = [pl.BlockSpec((tm, spg * ns), lambda i, j, pi=pi: (i, jnp.clip(j - pi * gpp, 0, gpp - 1)))
                for pi in range(npart)]
    return pl.pallas_call(
        body,
        name=name,
        grid=(m // tm, ngrp),
        in_specs=[*dy_specs, pl.BlockSpec((spg, k, ns), lambda i, j: (j, 0, 0))],
        out_specs=pl.BlockSpec((tm, k), lambda i, j: (i, 0)),
        out_shape=jax.ShapeDtypeStruct((m, k), out_dtype),
        scratch_shapes=[pltpu.VMEM((tm, k), F32)],
        compiler_params=_cp("parallel", "arbitrary"),
    )(*dys, w)


def mm_tn(a, dy, nsh, name="mm_tn", tm=1024, spg=1):
    dys, per, gpp = _split_parts(dy, nsh, spg)
    npart = len(dys)
    m, k = a.shape
    ns = dys[0].shape[1] // per
    assert all(d.shape == (m, per * ns) for d in dys)
    tm = _row_tile(m, tm)
    nt = m // tm

    def body(*refs):
        a_ref = refs[0]
        dy_refs = refs[1:1 + npart]
        o_ref, acc_ref = refs[1 + npart:]
        j = pl.program_id(0)
        i = pl.program_id(1)

        @pl.when(i == 0)
        def _():
            acc_ref[...] = jnp.zeros(acc_ref.shape, F32)

        for pi in range(npart):
            @pl.when(j // gpp == pi)
            def _(pi=pi):
                at = a_ref[...].astype(BF16).T
                for jj in range(spg):
                    acc_ref[jj] += _dot(at, dy_refs[pi][:, jj * ns:(jj + 1) * ns].astype(BF16))

        @pl.when(i == nt - 1)
        def _():
            o_ref[...] = acc_ref[...].astype(BF16)

    dy_specs = [
        pl.BlockSpec((tm, spg * ns),
                     lambda j, i, pi=pi: (jnp.where(j // gpp == pi, i, 0), jnp.clip(j - pi * gpp, 0, gpp - 1)))
        for pi in range(npart)
    ]
    return pl.pallas_call(
        body,
        name=name,
        grid=(nsh // spg, nt),
        in_specs=[pl.BlockSpec((tm, k), lambda j, i: (i, 0)), *dy_specs],
        out_specs=pl.BlockSpec((spg, k, ns), lambda j, i: (j, 0, 0)),
        out_shape=jax.ShapeDtypeStruct((nsh, k, ns), BF16),
        scratch_shapes=[pltpu.VMEM((spg, k, ns), F32)],
        compiler_params=_cp("parallel", "arbitrary"),
    )(a, *dys)


DEP_SPEC_SHAPE = (8, LANES)


def resid_norm_fwd(x, y, g_post, next_gains, name, tm=1024, dep=None):
    t, d = x.shape
    tm = _row_tile(t, tm)
    has_y = y is not None
    n_next = len(next_gains)
    n_dep = 0 if dep is None else 1

    def body(*refs):
        x_ref = refs[0]
        pos = 1
        if has_y:
            y_ref, gp_ref = refs[1], refs[2]
            pos = 3
        gn_refs = refs[pos:pos + n_next]
        outs = refs[pos + n_next + n_dep:]
        xv = x_ref[...]
        o = 0
        if has_y:
            yv = y_ref[...].astype(F32)
            r = lax.rsqrt(jnp.mean(yv * yv, axis=-1, keepdims=True) + EPS)
            xv = xv + (yv * r) * gp_ref[...]
            outs[0][...] = xv
            o = 1
        if n_next:
            xn = xv * lax.rsqrt(jnp.mean(xv * xv, axis=-1, keepdims=True) + EPS)
            for k in range(n_next):
                outs[o + k][...] = (xn * gn_refs[k][...]).astype(BF16)

    row = pl.BlockSpec((tm, d), lambda i: (i, 0))
    vec = pl.BlockSpec((1, d), lambda i: (0, 0))
    args, in_specs = [x], [row]
    if has_y:
        args += [y, g_post]
        in_specs += [row, vec]
    args += list(next_gains)
    in_specs += [vec] * n_next
    if n_dep:
        args.append(dep)
        in_specs.append(pl.BlockSpec(DEP_SPEC_SHAPE, lambda i: (0, 0)))
    out_shape, out_specs = [], []
    if has_y:
        out_shape.append(jax.ShapeDtypeStruct((t, d), F32))
        out_specs.append(row)
    for _ in range(n_next):
        out_shape.append(jax.ShapeDtypeStruct((t, d), BF16))
        out_specs.append(row)
    return pl.pallas_call(
        body, name=name, grid=(t // tm,), in_specs=in_specs, out_specs=out_specs, out_shape=out_shape,
        compiler_params=_cp("parallel"),
    )(*args)


def norm_bwd(x, g, dy, add=None, out_dtype=F32, name="norm_bwd", tm=1024, dep=None):
    t, d = x.shape
    tm = _row_tile(t, tm)
    has_add = add is not None

    def body(*refs):
        x_ref, g_ref, dy_ref = refs[:3]
        add_ref = refs[3] if has_add else None
        dx_ref, dg_ref, cs_ref = refs[-3:]
        i = pl.program_id(0)
        xv = x_ref[...].astype(F32)
        dyv = dy_ref[...].astype(F32)
        r = lax.rsqrt(jnp.mean(xv * xv, axis=-1, keepdims=True) + EPS)
        gd = dyv * g_ref[...]
        dx = r * gd - xv * ((r * r * r) * jnp.mean(xv * gd, axis=-1, keepdims=True))
        if has_add:
            dx = dx + add_ref[...]
        dx_ref[...] = dx.astype(out_dtype)
        dg = jnp.sum(dyv * (xv * r), axis=0, keepdims=True)
        cs = jnp.sum(dx, axis=0, keepdims=True)

        @pl.when(i == 0)
        def _():
            dg_ref[...] = dg
            cs_ref[...] = cs

        @pl.when(i > 0)
        def _():
            dg_ref[...] += dg
            cs_ref[...] += cs

    row = pl.BlockSpec((tm, d), lambda i: (i, 0))
    vec = pl.BlockSpec((1, d), lambda i: (0, 0))
    args, in_specs = [x, g, dy], [row, vec, row]
    if has_add:
        args.append(add)
        in_specs.append(row)
    if dep is not None:
        args.append(dep)
        in_specs.append(pl.BlockSpec(DEP_SPEC_SHAPE, lambda i: (0, 0)))
    return pl.pallas_call(
        body, name=name, grid=(t // tm,), in_specs=in_specs,
        out_specs=[row, vec, vec],
        out_shape=[jax.ShapeDtypeStruct((t, d), out_dtype), jax.ShapeDtypeStruct((1, d), F32),
                   jax.ShapeDtypeStruct((1, d), F32)],
        compiler_params=_cp("arbitrary"),
    )(*args)


def loss_fwd_bwd(x, target, name="loss", tm=1024):
    t, d = x.shape
    tm = _row_tile(t, tm)

    def body(x_ref, t_ref, dx_ref, l_ref):
        i = pl.program_id(0)
        err = x_ref[...] - t_ref[...]
        dx_ref[...] = err * (1.0 / d)
        part = 0.5 * jnp.sum(jnp.mean(err * err, axis=-1, keepdims=True), axis=0, keepdims=True)
        part = jnp.broadcast_to(part, l_ref.shape)

        @pl.when(i == 0)
        def _():
            l_ref[...] = part

        @pl.when(i > 0)
        def _():
            l_ref[...] += part

    row = pl.BlockSpec((tm, d), lambda i: (i, 0))
    return pl.pallas_call(
        body, name=name, grid=(t // tm,), in_specs=[row, row],
        out_specs=[row, pl.BlockSpec((8, LANES), lambda i: (0, 0))],
        out_shape=[jax.ShapeDtypeStruct((t, d), F32), jax.ShapeDtypeStruct((8, LANES), F32)],
        compiler_params=_cp("arbitrary"),
    )(x, target)


CONV_HALO = 32
CONV_CHUNK = 128


def glu_conv_fwd(z, dw, dwb, name, tc=128):
    b, s, c2 = z.shape
    c = c2 // 2
    kw = dw.shape[0]
    tc = min(tc, c)
    nc = c // tc
    ch = min(CONV_CHUNK, s)
    halo = CONV_HALO
    assert kw - 1 <= halo and s % ch == 0

    nch = s // ch

    def body(a_ref, g_ref, w_ref, b_ref, o_ref, pad_ref):
        _fill_glu_slabs(a_ref, g_ref, pad_ref, nch, ch, halo)

        def chunk(ci, carry):
            r0 = pl.multiple_of(ci * ch, ch)
            acc = b_ref[...]
            for k, tap in enumerate(_taps_front(pad_ref, ci, kw, ch, halo)):
                acc = acc + w_ref[k:k + 1, :] * tap
            o_ref[pl.ds(r0, ch), :] = acc
            return carry

        lax.fori_loop(0, nch, chunk, 0)

    return pl.pallas_call(
        body, name=name, grid=(b, nc),
        in_specs=[
            pl.BlockSpec((None, s, tc), lambda bi, i: (bi, 0, i)),
            pl.BlockSpec((None, s, tc), lambda bi, i: (bi, 0, i + nc)),
            pl.BlockSpec((kw, tc), lambda bi, i: (0, i)),
            pl.BlockSpec((1, tc), lambda bi, i: (0, i)),
        ],
        out_specs=pl.BlockSpec((None, s, tc), lambda bi, i: (bi, 0, i)),
        out_shape=jax.ShapeDtypeStruct((b, s, c), F32),
        scratch_shapes=[pltpu.VMEM((nch, ch + halo, tc), F32)],
        compiler_params=_cp("parallel", "parallel"),
    )(z, z, dw, dwb)


def glu_conv_bwd(z, dw, du2, name, tc=128):
    b, s, c2 = z.shape
    c = c2 // 2
    kw = dw.shape[0]
    tc = min(tc, c)
    nc = c // tc
    ch = min(CONV_CHUNK, s)
    nch = s // ch
    halo = CONV_HALO

    def body(a_ref, g_ref, w_ref, du_ref, dza_ref, dzg_ref, ddw_ref, ddwb_ref, dba_ref, dbg_ref, upad_ref, dpad_ref):
        bi = pl.program_id(1)

        @pl.when(bi == 0)
        def _():
            ddw_ref[...] = jnp.zeros(ddw_ref.shape, F32)
            ddwb_ref[...] = jnp.zeros(ddwb_ref.shape, F32)
            dba_ref[...] = jnp.zeros(dba_ref.shape, F32)
            dbg_ref[...] = jnp.zeros(dbg_ref.shape, F32)

        _fill_glu_slabs(a_ref, g_ref, upad_ref, nch, ch, halo)
        dpad_ref[nch - 1, ch:ch + halo, :] = jnp.zeros((halo, tc), F32)
        dpad_ref[nch - 1, 0:ch, :] = du_ref[s - ch:s, :]

        def fill(ci, carry):
            r0 = pl.multiple_of(ci * ch, ch)
            dpad_ref[ci, :, :] = du_ref[pl.ds(r0, ch + halo), :]
            return carry

        lax.fori_loop(0, nch - 1, fill, 0)

        def chunk(ci, carry):
            r0 = pl.multiple_of(ci * ch, ch)
            du_c = du_ref[pl.ds(r0, ch), :]
            taps_u = _taps_front(upad_ref, ci, kw, ch, halo)
            taps_d = _taps_at(dpad_ref, ci, list(range(kw)), ch)
            du1 = w_ref[kw - 1:kw, :] * du_c
            ddw_ref[kw - 1] += jnp.sum((du_c * taps_u[kw - 1]).reshape(ch // 8, 8, tc), axis=0)
            for j in range(1, kw):
                du1 = du1 + w_ref[kw - 1 - j:kw - j, :] * taps_d[j]
                ddw_ref[kw - 1 - j] += jnp.sum((du_c * taps_u[kw - 1 - j]).reshape(ch // 8, 8, tc), axis=0)
            av = a_ref[pl.ds(r0, ch), :]
            sg = _sigmoid(g_ref[pl.ds(r0, ch), :])
            dza = du1 * sg
            dzg = du1 * av * (sg * (1.0 - sg))
            dza_ref[pl.ds(r0, ch), :] = dza.astype(BF16)
            dzg_ref[pl.ds(r0, ch), :] = dzg.astype(BF16)
            dba_ref[...] += jnp.sum(dza, axis=0, keepdims=True)
            dbg_ref[...] += jnp.sum(dzg, axis=0, keepdims=True)
            ddwb_ref[...] += jnp.sum(du_c, axis=0, keepdims=True)
            return carry

        lax.fori_loop(0, s // ch, chunk, 0)

    blk = lambda off: pl.BlockSpec((None, s, tc), lambda i, bi: (bi, 0, i + off))
    vec = pl.BlockSpec((1, tc), lambda i, bi: (0, i))
    return pl.pallas_call(
        body, name=name, grid=(nc, b),
        in_specs=[blk(0), blk(nc), pl.BlockSpec((kw, tc), lambda i, bi: (0, i)), blk(0)],
        out_specs=[blk(0), blk(0), pl.BlockSpec((kw, 8, tc), lambda i, bi: (0, 0, i)), vec, vec, vec],
        out_shape=[
            jax.ShapeDtypeStruct((b, s, c), BF16), jax.ShapeDtypeStruct((b, s, c), BF16),
            jax.ShapeDtypeStruct((kw, 8, c), F32), jax.ShapeDtypeStruct((1, c), F32),
            jax.ShapeDtypeStruct((1, c), F32), jax.ShapeDtypeStruct((1, c), F32),
        ],
        scratch_shapes=[pltpu.VMEM((nch, ch + halo, tc), F32), pltpu.VMEM((nch, ch + halo, tc), F32)],
        compiler_params=_cp("parallel", "arbitrary"),
    )(z, z, dw, du2)


def _fill_glu_slabs(a_ref, g_ref, pad_ref, nch, ch, halo):
    tc = a_ref.shape[-1]
    pad_ref[0, 0:halo, :] = jnp.zeros((halo, tc), F32)
    pad_ref[0, halo:halo + ch, :] = a_ref[0:ch, :] * _sigmoid(g_ref[0:ch, :])

    def fill(ci, carry):
        r0 = pl.multiple_of(ci * ch, ch)
        pad_ref[ci, 0:halo, :] = pad_ref[ci - 1, ch:ch + halo, :]
        pad_ref[ci, halo:halo + ch, :] = a_ref[pl.ds(r0, ch), :] * _sigmoid(g_ref[pl.ds(r0, ch), :])
        return carry

    lax.fori_loop(1, nch, fill, 0)


def ln_silu_fwd(u, g, bvec, name, tm=1024):
    t, d = u.shape
    tm = _row_tile(t, tm)

    def body(u_ref, g_ref, b_ref, o_ref):
        uv = u_ref[...]
        mu = jnp.mean(uv, axis=-1, keepdims=True)
        xc = uv - mu
        var = jnp.mean(xc * xc, axis=-1, keepdims=True)
        v = (xc * lax.rsqrt(var + EPS)) * g_ref[...] + b_ref[...]
        o_ref[...] = (v * _sigmoid(v)).astype(BF16)

    row = pl.BlockSpec((tm, d), lambda i: (i, 0))
    vec = pl.BlockSpec((1, d), lambda i: (0, 0))
    return pl.pallas_call(
        body, name=name, grid=(t // tm,), in_specs=[row, vec, vec], out_specs=row,
        out_shape=jax.ShapeDtypeStruct((t, d), BF16), compiler_params=_cp("parallel"),
    )(u, g, bvec)


def ln_silu_bwd(u, g, bvec, dout, name, tm=1024):
    t, d = u.shape
    tm = _row_tile(t, tm)

    def body(u_ref, g_ref, b_ref, do_ref, du_ref, dg_ref, db_ref):
        i = pl.program_id(0)
        uv = u_ref[...]
        mu = jnp.mean(uv, axis=-1, keepdims=True)
        xc = uv - mu
        var = jnp.mean(xc * xc, axis=-1, keepdims=True)
        rstd = lax.rsqrt(var + EPS)
        n = xc * rstd
        v = n * g_ref[...] + b_ref[...]
        sg = _sigmoid(v)
        dv = do_ref[...].astype(F32) * (sg * (1.0 + v * (1.0 - sg)))
        dn = dv * g_ref[...]
        du_ref[...] = rstd * (dn - jnp.mean(dn, axis=-1, keepdims=True) - n * jnp.mean(dn * n, axis=-1, keepdims=True))
        dg = jnp.sum(dv * n, axis=0, keepdims=True)
        db = jnp.sum(dv, axis=0, keepdims=True)

        @pl.when(i == 0)
        def _():
            dg_ref[...] = dg
            db_ref[...] = db

        @pl.when(i > 0)
        def _():
            dg_ref[...] += dg
            db_ref[...] += db

    row = pl.BlockSpec((tm, d), lambda i: (i, 0))
    vec = pl.BlockSpec((1, d), lambda i: (0, 0))
    return pl.pallas_call(
        body, name=name, grid=(t // tm,), in_specs=[row, vec, vec, row], out_specs=[row, vec, vec],
        out_shape=[jax.ShapeDtypeStruct((t, d), F32), jax.ShapeDtypeStruct((1, d), F32), jax.ShapeDtypeStruct((1, d), F32)],
        compiler_params=_cp("arbitrary"),
    )(u, g, bvec, dout)


FFN_HALO = 8


def _fill_front_halo(src_ref, pad_ref, nch, ch, halo):
    tc = src_ref.shape[-1]
    pad_ref[0, 0:halo, :] = jnp.zeros((halo, tc), F32)
    pad_ref[0, halo:halo + ch, :] = src_ref[0:ch, :].astype(F32)

    def fill(ci, carry):
        r0 = pl.multiple_of(ci * ch, ch)
        pad_ref[ci, 0:halo, :] = src_ref[pl.ds(r0 - 2 * halo, 2 * halo), :].astype(F32)[halo:, :]
        pad_ref[ci, halo:halo + ch, :] = src_ref[pl.ds(r0, ch), :].astype(F32)
        return carry

    lax.fori_loop(1, nch, fill, 0)


def _taps_at(pad_ref, ci, offsets, ch):
    windows = {}
    for b in sorted({o % 8 for o in offsets}):
        top = max(o for o in offsets if o % 8 == b)
        windows[b] = pad_ref[ci, b:top + ch, :]
    return [windows[o % 8][o - o % 8:o - o % 8 + ch, :] for o in offsets]


def _taps_front(pad_ref, ci, kw, ch, halo):
    return _taps_at(pad_ref, ci, [halo - (kw - 1 - k) for k in range(kw)], ch)


def ffn_mid_fwd(p, dw, dwb, name, tc=256):
    b, s, f2 = p.shape
    f = f2 // 2
    kw = dw.shape[0]
    tc = min(tc, f)
    nf = f // tc
    ch = min(CONV_CHUNK, s)
    nch = s // ch
    halo = FFN_HALO

    def body(pa_ref, pg_ref, wa_ref, wg_ref, ba_ref, bg_ref, o_ref, ca_ref, cg_ref, apad_ref, gpad_ref):
        _fill_front_halo(pa_ref, apad_ref, nch, ch, halo)
        _fill_front_halo(pg_ref, gpad_ref, nch, ch, halo)

        def chunk(ci, carry):
            r0 = pl.multiple_of(ci * ch, ch)
            ca = ba_ref[...]
            cg = bg_ref[...]
            taps = zip(_taps_front(apad_ref, ci, kw, ch, halo), _taps_front(gpad_ref, ci, kw, ch, halo))
            for k, (ta, tg) in enumerate(taps):
                ca = ca + wa_ref[k:k + 1, :] * ta
                cg = cg + wg_ref[k:k + 1, :] * tg
            sg = _sigmoid(cg)
            act = cg * sg
            o_ref[pl.ds(r0, ch), :] = (act * ca).astype(BF16)
            ca_ref[pl.ds(r0, ch), :] = act.astype(BF16)
            cg_ref[pl.ds(r0, ch), :] = (ca * (sg * (1.0 + cg * (1.0 - sg)))).astype(BF16)
            return carry

        lax.fori_loop(0, nch, chunk, 0)

    blk = lambda off: pl.BlockSpec((None, s, tc), lambda bi, i: (bi, 0, i + off))
    wsp = lambda off: pl.BlockSpec((kw, tc), lambda bi, i: (0, i + off))
    bsp = lambda off: pl.BlockSpec((1, tc), lambda bi, i: (0, i + off))
    return pl.pallas_call(
        body, name=name, grid=(b, nf),
        in_specs=[blk(0), blk(nf), wsp(0), wsp(nf), bsp(0), bsp(nf)],
        out_specs=[blk(0)] * 3,
        out_shape=[jax.ShapeDtypeStruct((b, s, f), BF16)] * 3,
        scratch_shapes=[pltpu.VMEM((nch, ch + halo, tc), F32)] * 2,
        compiler_params=_cp("parallel", "parallel"),
    )(p, p, dw, dw, dwb, dwb)


def ffn_mid_bwd(p, dw, ca_sv, cg_sv, ds, name, tc=256):
    b, s, f2 = p.shape
    f = f2 // 2
    kw = dw.shape[0]
    tc = min(tc, f)
    nf = f // tc
    ch = min(CONV_CHUNK, s)
    nch = s // ch
    halo = FFN_HALO

    def sum8(v):
        return jnp.sum(v.reshape(ch // 8, 8, tc), axis=0)

    def body(pa_ref, pg_ref, wa_ref, wg_ref, ca_ref, cg_ref, ds_ref, dpa_ref, dpg_ref, ddwa_ref, ddwg_ref, dba_ref, dbg_ref,
             apad_ref, gpad_ref, dca_ref, dcg_ref):
        bi = pl.program_id(1)

        @pl.when(bi == 0)
        def _():
            ddwa_ref[...] = jnp.zeros(ddwa_ref.shape, F32)
            ddwg_ref[...] = jnp.zeros(ddwg_ref.shape, F32)
            dba_ref[...] = jnp.zeros(dba_ref.shape, F32)
            dbg_ref[...] = jnp.zeros(dbg_ref.shape, F32)

        _fill_front_halo(pa_ref, apad_ref, nch, ch, halo)
        _fill_front_halo(pg_ref, gpad_ref, nch, ch, halo)
        dca_ref[nch - 1, ch:ch + halo, :] = jnp.zeros((halo, tc), F32)
        dcg_ref[nch - 1, ch:ch + halo, :] = jnp.zeros((halo, tc), F32)

        def grads(ci, carry):
            acc_a, acc_g, sb_a, sb_g = carry
            r0 = pl.multiple_of(ci * ch, ch)
            taps_a = _taps_front(apad_ref, ci, kw, ch, halo)
            taps_g = _taps_front(gpad_ref, ci, kw, ch, halo)
            dsv = ds_ref[pl.ds(r0, ch), :].astype(F32)
            dca = dsv * ca_ref[pl.ds(r0, ch), :].astype(F32)
            dcg = dsv * cg_ref[pl.ds(r0, ch), :].astype(F32)
            dca_ref[ci, 0:ch, :] = dca
            dcg_ref[ci, 0:ch, :] = dcg

            prev = jnp.maximum(ci - 1, 0)

            @pl.when(ci > 0)
            def _():
                dca_ref[prev, ch:ch + halo, :] = dca[0:halo, :]
                dcg_ref[prev, ch:ch + halo, :] = dcg[0:halo, :]

            acc_a = tuple(acc_a[k] + sum8(dca * taps_a[k]) for k in range(kw))
            acc_g = tuple(acc_g[k] + sum8(dcg * taps_g[k]) for k in range(kw))
            return acc_a, acc_g, sb_a + sum8(dca), sb_g + sum8(dcg)

        z8 = jnp.zeros((8, tc), F32)
        acc_a, acc_g, sb_a, sb_g = lax.fori_loop(0, nch, grads, ((z8,) * kw, (z8,) * kw, z8, z8))
        for k in range(kw):
            ddwa_ref[k] += acc_a[k]
            ddwg_ref[k] += acc_g[k]
        dba_ref[...] += jnp.sum(sb_a, axis=0, keepdims=True)
        dbg_ref[...] += jnp.sum(sb_g, axis=0, keepdims=True)

        def back(ci, carry):
            r0 = pl.multiple_of(ci * ch, ch)
            da = wa_ref[kw - 1:kw, :] * dca_ref[ci, 0:ch, :]
            dg = wg_ref[kw - 1:kw, :] * dcg_ref[ci, 0:ch, :]
            for j in range(1, kw):
                da = da + wa_ref[kw - 1 - j:kw - j, :] * dca_ref[ci, j:j + ch, :]
                dg = dg + wg_ref[kw - 1 - j:kw - j, :] * dcg_ref[ci, j:j + ch, :]
            dpa_ref[pl.ds(r0, ch), :] = da.astype(BF16)
            dpg_ref[pl.ds(r0, ch), :] = dg.astype(BF16)
            return carry

        lax.fori_loop(0, nch, back, 0)

    blk = lambda off: pl.BlockSpec((None, s, tc), lambda i, bi: (bi, 0, i + off))
    wsp = lambda off: pl.BlockSpec((kw, tc), lambda i, bi: (0, i + off))
    acc3 = pl.BlockSpec((kw, 8, tc), lambda i, bi: (0, 0, i))
    vec = pl.BlockSpec((1, tc), lambda i, bi: (0, i))
    return pl.pallas_call(
        body, name=name, grid=(nf, b),
        in_specs=[blk(0), blk(nf), wsp(0), wsp(nf), blk(0), blk(0), blk(0)],
        out_specs=[blk(0), blk(0), acc3, acc3, vec, vec],
        out_shape=[jax.ShapeDtypeStruct((b, s, f), BF16), jax.ShapeDtypeStruct((b, s, f), BF16),
                   jax.ShapeDtypeStruct((kw, 8, f), F32), jax.ShapeDtypeStruct((kw, 8, f), F32),
                   jax.ShapeDtypeStruct((1, f), F32), jax.ShapeDtypeStruct((1, f), F32)],
        scratch_shapes=[pltpu.VMEM((nch, ch + halo, tc), F32)] * 4,
        compiler_params=_cp("parallel", "arbitrary"),
    )(p, p, dw, dw, ca_sv, cg_sv, ds)


def _tile_rows(r, n, dil):
    start = r + n * BLK * dil
    return pl.ds(start, BLK, stride=dil) if dil > 1 else pl.ds(start, BLK)


def _band_masks():
    qi = lax.broadcasted_iota(jnp.int32, (BLK, 2 * BLK), 0)
    kk = lax.broadcasted_iota(jnp.int32, (BLK, 2 * BLK), 1)
    both = jnp.logical_or(jnp.logical_and(kk < BLK, kk >= qi), jnp.logical_and(kk >= BLK, kk - BLK <= qi))
    return both, kk[:, :BLK] <= qi[:, :BLK]


def attn_fwd(q, kv, g, dil, hw, name):
    _, b, s, _ = q.shape
    nh = hw // HEAD_DIM
    nblk = s // dil // BLK
    scale = 1.0 / math.sqrt(HEAD_DIM)

    def body(q_ref, k_ref, v_ref, o_ref, lse_ref):
        h = pl.program_id(1)
        mask2, mask1 = _band_masks()
        mine = lax.broadcasted_iota(jnp.int32, (BLK, LANES), 1) == h

        @pl.when(h == 0)
        def _():
            lse_ref[...] = jnp.zeros(lse_ref.shape, F32)

        for r in range(dil):
            kp = vp = None
            for n in range(nblk):
                rs = _tile_rows(r, n, dil)
                qt = q_ref[rs, :].astype(BF16)
                kc = k_ref[rs, :].astype(BF16)
                vc = v_ref[rs, :].astype(BF16)
                if n == 0:
                    kcat, vcat, mask = kc, vc, mask1
                else:
                    kcat, vcat, mask = jnp.concatenate([kp, kc], axis=0), jnp.concatenate([vp, vc], axis=0), mask2
                sc = jnp.where(mask, _dot_nt(qt, kcat) * scale, NEG_INF)
                m = jnp.max(sc, axis=-1, keepdims=True)
                p = jnp.exp(sc - m)
                den = jnp.sum(p, axis=-1, keepdims=True)
                o_ref[rs, :] = _dot(p.astype(BF16), vcat) / den
                lse_ref[rs, :] = jnp.where(mine, m + jnp.log(den), lse_ref[rs, :])
                kp, vp = kc, vc

    col = lambda base: pl.BlockSpec((None, s, HEAD_DIM), lambda bi, h: (bi, 0, base + h))
    head = lambda base: pl.BlockSpec((None, None, s, HEAD_DIM), lambda bi, h: (base + h, bi, 0, 0))
    return pl.pallas_call(
        body, name=name, grid=(b, nh),
        in_specs=[head(g * nh), head(g * nh), head((N_GROUPS + g) * nh)],
        out_specs=[head(0), pl.BlockSpec((None, s, LANES), lambda bi, h: (bi, 0, 0))],
        out_shape=[jax.ShapeDtypeStruct((nh, b, s, HEAD_DIM), F32), jax.ShapeDtypeStruct((b, s, LANES), F32)],
        compiler_params=_cp("parallel", "arbitrary"),
    )(q, kv, kv)


def attn_merge(outs, lses, name, tm=512):
    nh, t, _ = outs[0].shape
    hw = nh * HEAD_DIM
    tm = _row_tile(t, tm)
    ng = len(outs)

    def body(*refs):
        o_refs, l_refs = refs[:ng], refs[ng:2 * ng]
        m_ref, lj_ref = refs[2 * ng:]
        ls = [l_refs[g][...] for g in range(ng)]
        mx = ls[0]
        for g in range(1, ng):
            mx = jnp.maximum(mx, ls[g])
        es = [jnp.exp(l - mx) for l in ls]
        tot = es[0]
        for g in range(1, ng):
            tot = tot + es[g]
        ws = [e / tot for e in es]
        lj_ref[...] = mx + jnp.log(tot)
        for h in range(nh):
            sl = slice(h * HEAD_DIM, (h + 1) * HEAD_DIM)
            acc = ws[0][:, h:h + 1] * o_refs[0][h]
            for g in range(1, ng):
                acc = acc + ws[g][:, h:h + 1] * o_refs[g][h]
            m_ref[:, sl] = acc.astype(BF16)

    row = pl.BlockSpec((tm, hw), lambda i: (i, 0))
    st = pl.BlockSpec((tm, LANES), lambda i: (i, 0))
    heads = pl.BlockSpec((nh, tm, HEAD_DIM), lambda i: (0, i, 0))
    return pl.pallas_call(
        body, name=name, grid=(t // tm,), in_specs=[heads] * ng + [st] * ng, out_specs=[row, st],
        out_shape=[jax.ShapeDtypeStruct((t, hw), BF16), jax.ShapeDtypeStruct((t, LANES), F32)],
        compiler_params=_cp("parallel"),
    )(*outs, *lses)


def attn_bwd_prep(dmerged, merged, name, tm=512):
    t, hw = merged.shape
    nh = hw // HEAD_DIM
    tm = _row_tile(t, tm)

    def body(d_ref, m_ref, o_ref):
        lane = lax.broadcasted_iota(jnp.int32, (tm, LANES), 1)
        acc = jnp.zeros((tm, LANES), F32)
        for h in range(nh):
            sl = slice(h * HEAD_DIM, (h + 1) * HEAD_DIM)
            dsum = jnp.sum(d_ref[:, sl] * m_ref[:, sl].astype(F32), axis=-1, keepdims=True)
            acc = jnp.where(lane == h, dsum, acc)
        o_ref[...] = acc

    row = pl.BlockSpec((tm, hw), lambda i: (i, 0))
    return pl.pallas_call(
        body, name=name, grid=(t // tm,), in_specs=[row, row], out_specs=pl.BlockSpec((tm, LANES), lambda i: (i, 0)),
        out_shape=jax.ShapeDtypeStruct((t, LANES), F32), compiler_params=_cp("parallel"),
    )(dmerged, merged)


def attn_bwd(q, kv, g, dil, do, lsej, dm, dq_buf, dk_buf, dv_buf, accumulate, hw, name):
    _, b, s, _ = q.shape
    nh = hw // HEAD_DIM
    nblk = s // dil // BLK
    scale = 1.0 / math.sqrt(HEAD_DIM)
    assert dk_buf is not None or not accumulate
    kv_at = 6 + (dq_buf is not None)

    def body(*refs):
        q_ref, k_ref, v_ref, do_ref, lj_ref, dm_ref = refs[:6]
        dq_ref, dk_ref, dv_ref = refs[-3:]
        dki_ref, dvi_ref = (refs[kv_at], refs[kv_at + 1]) if accumulate else (None, None)
        mask2, mask1 = _band_masks()
        mine = lax.broadcasted_iota(jnp.int32, (BLK, LANES), 1) == pl.program_id(1)

        def my_lane(v):
            return jnp.sum(jnp.where(mine, v, 0.0), axis=-1, keepdims=True)

        def put(rs, dk, dv):
            if accumulate:
                dk = dk + dki_ref[rs, :]
                dv = dv + dvi_ref[rs, :]
            dk_ref[rs, :] = dk
            dv_ref[rs, :] = dv

        for r in range(dil):
            kp = vp = hold_k = hold_v = rs_prev = None
            for n in range(nblk):
                rs = _tile_rows(r, n, dil)
                qt = q_ref[rs, :].astype(BF16)
                kc = k_ref[rs, :].astype(BF16)
                vc = v_ref[rs, :].astype(BF16)
                dot = do_ref[rs, :].astype(BF16)
                lm = my_lane(lj_ref[rs, :])
                dmm = my_lane(dm_ref[rs, :])
                if n == 0:
                    kcat, vcat, mask = kc, vc, mask1
                else:
                    kcat, vcat, mask = jnp.concatenate([kp, kc], axis=0), jnp.concatenate([vp, vc], axis=0), mask2
                p = jnp.exp(jnp.where(mask, _dot_nt(qt, kcat) * scale, NEG_INF) - lm)
                ds = (p * (_dot_nt(dot, vcat) - dmm)).astype(BF16)
                dq_ref[rs, :] = _dot(ds, kcat) * scale
                dkc = _dot_tn(ds, qt) * scale
                dvc = _dot_tn(p.astype(BF16), dot)
                if n > 0:
                    put(rs_prev, hold_k + dkc[:BLK, :], hold_v + dvc[:BLK, :])
                    dkc, dvc = dkc[BLK:, :], dvc[BLK:, :]
                hold_k, hold_v, kp, vp, rs_prev = dkc, dvc, kc, vc, rs
            put(rs_prev, hold_k, hold_v)

    col = lambda base: pl.BlockSpec((None, s, HEAD_DIM), lambda bi, h: (bi, 0, base + h))
    any_spec = pl.BlockSpec(memory_space=pl.ANY)
    stat = pl.BlockSpec((None, s, LANES), lambda bi, h: (bi, 0, 0))
    head = lambda base: pl.BlockSpec((None, None, s, HEAD_DIM), lambda bi, h: (base + h, bi, 0, 0))
    in_specs = [head(g * nh), head(g * nh), head((N_GROUPS + g) * nh), col(0), stat, stat]
    args = [q, kv, kv, do, lsej, dm]
    aliases = {}
    if dq_buf is not None:
        in_specs.append(any_spec)
        args.append(dq_buf)
        aliases[6] = 0
    if dk_buf is not None:
        in_specs += [col(g * nh) if accumulate else any_spec] * 2
        args += [dk_buf, dv_buf]
        aliases.update({kv_at: 1, kv_at + 1: 2})
    shape = jax.ShapeDtypeStruct((b, s, N_GROUPS * hw), F32)
    return pl.pallas_call(
        body, name=name, grid=(b, nh), in_specs=in_specs, out_specs=[col(g * nh)] * 3, out_shape=[shape] * 3,
        input_output_aliases=aliases, compiler_params=_cp("parallel", "parallel"),
    )(*args)


def sum_parts(g, recv, me, name, tm=512, layer=0, n_layers=1, out_buf=None):
    _, rows, c = g.shape
    n = recv.shape[0]
    tm = _row_tile(rows, tm)
    steps = rows // tm

    def body(me_ref, g_ref, r_ref, *rest):
        acc = g_ref[...].astype(F32)
        for j in range(n):
            acc = acc + r_ref[j].astype(F32)
        rest[-1][...] = acc

    in_specs = [pl.BlockSpec((None, tm, c), lambda i, me_ref: (me_ref[0], i, 0)),
                pl.BlockSpec((n, tm, c), lambda i, me_ref: (0, i, 0))]
    args = [me, g, recv]
    aliases = {}
    if out_buf is not None:
        in_specs.append(pl.BlockSpec(memory_space=pl.ANY))
        args.append(out_buf)
        aliases = {3: 0}
    return pl.pallas_call(
        body, name=name,
        grid_spec=pltpu.PrefetchScalarGridSpec(
            num_scalar_prefetch=1, grid=(steps,), in_specs=in_specs,
            out_specs=pl.BlockSpec((tm, c), lambda i, me_ref: (layer * steps + i, 0))),
        out_shape=jax.ShapeDtypeStruct((n_layers * rows, c), F32), input_output_aliases=aliases,
        compiler_params=_cp("parallel"),
    )(*args)


def adamw(w, m, v, g_parts, name, tm=256):
    rows, c = w.shape
    if c <= 8 * LANES:
        tm = 2 * tm
    tm = _row_tile(rows, tm)
    npart = len(g_parts)

    def body(*refs):
        w_ref, m_ref, v_ref = refs[:3]
        g_refs = refs[3:3 + npart]
        go_ref, d_ref, mo_ref, vo_ref = refs[3 + npart:]
        g = g_refs[0][...]
        for k in range(1, npart):
            g = g + g_refs[k][...]
        mn = ADAM_B1 * m_ref[...] + (1.0 - ADAM_B1) * g
        vn = ADAM_B2 * v_ref[...] + (1.0 - ADAM_B2) * (g * g)
        m_hat = mn / (1.0 - ADAM_B1 ** ADAM_STEP)
        v_hat = vn / (1.0 - ADAM_B2 ** ADAM_STEP)
        go_ref[...] = g
        d_ref[...] = -ADAM_LR * (m_hat / (jnp.sqrt(v_hat) + ADAM_EPS) + ADAM_WD * w_ref[...])
        mo_ref[...] = mn
        vo_ref[...] = vn

    row = pl.BlockSpec((tm, c), lambda i: (i, 0))
    return pl.pallas_call(
        body, name=name, grid=(rows // tm,), in_specs=[row] * (3 + npart), out_specs=[row] * 4,
        out_shape=[jax.ShapeDtypeStruct((rows, c), F32)] * 4, compiler_params=_cp("parallel"),
    )(w, m, v, *g_parts)


def _place():
    return lax.axis_index("x"), lax.axis_index("y"), lax.axis_index("c")


def _other_chips(x, y, c):
    return [(1 - x, y, c), (x, 1 - y, c), (1 - x, 1 - y, c)]


def _chip_of(px, py):
    return 2 * px + py


HBM_SPEC = pl.BlockSpec(memory_space=pltpu.HBM)
SEM_SPEC = pl.BlockSpec(memory_space=pltpu.SEMAPHORE)
ANY_SPEC = pl.BlockSpec(memory_space=pl.ANY)
DATAFLOW = pltpu.SideEffectType.DATAFLOW_SIDE_EFFECTING
N_PEER_CHIPS = N_CHIPS - 1


def _hbm(a):
    return pltpu.with_memory_space_constraint(a, pltpu.HBM)


def _hbm_like(arrays):
    return [pltpu.HBM(a.shape, a.dtype) for a in arrays]


def cast_place(w, layer, me, out_dtype, name, tm=512, nslots=N_CHIPS, dep=None):
    rows, c = w.shape[-2:]
    tm = _row_tile(rows, tm)

    def body(me_ref, w_ref, *rest):
        rest[-1][...] = w_ref[...].astype(out_dtype)

    if layer is None:
        in_specs = [pl.BlockSpec((tm, c), lambda i, me_ref: (i, 0))]
    else:
        in_specs = [pl.BlockSpec((None, tm, c), lambda i, me_ref: (layer, i, 0))]
    args = [me, w]
    if dep is not None:
        in_specs.append(pl.BlockSpec(DEP_SPEC_SHAPE, lambda i, me_ref: (0, 0)))
        args.append(dep)
    return pl.pallas_call(
        body, name=name,
        grid_spec=pltpu.PrefetchScalarGridSpec(
            num_scalar_prefetch=1, grid=(rows // tm,), in_specs=in_specs,
            out_specs=pl.BlockSpec((None, tm, c), lambda i, me_ref: (me_ref[0], i, 0))),
        out_shape=jax.ShapeDtypeStruct((nslots, rows, c), out_dtype), compiler_params=_cp("parallel"),
    )(*args)


def gather_start(lands, chunk_sizes, name="gather_start"):
    n = len(lands)
    nch = len(chunk_sizes)
    assert sum(chunk_sizes) == n

    def body(*refs):
        land_refs = refs[:n]
        outs = refs[n:]
        send_sems, recv_sems = outs[:nch], outs[nch:2 * nch]
        token = outs[-1]
        x, y, c = _place()
        me = _chip_of(x, y)
        peers = _other_chips(x, y, c)
        k = 0
        for ck, size in enumerate(chunk_sizes):
            for pos in range(size):
                for r, peer in enumerate(peers):
                    pltpu.make_async_remote_copy(
                        src_ref=land_refs[k].at[me], dst_ref=land_refs[k].at[me],
                        send_sem=send_sems[ck].at[N_PEER_CHIPS * pos + r], recv_sem=recv_sems[ck].at[N_PEER_CHIPS * pos + r],
                        device_id=peer, device_id_type=MESH).start()
                k += 1
        token[...] = jnp.zeros(token.shape, F32)

    sems = [pltpu.SemaphoreType.DMA((N_PEER_CHIPS * s,)) for s in chunk_sizes]
    res = pl.pallas_call(
        body, name=name,
        out_shape=(*sems, *sems, *_hbm_like(lands), jax.ShapeDtypeStruct(DEP_SPEC_SHAPE, F32)),
        in_specs=[HBM_SPEC] * n,
        out_specs=(*[SEM_SPEC] * (2 * nch), *[HBM_SPEC] * n, pl.BlockSpec(memory_space=pltpu.VMEM)),
        input_output_aliases={k: 2 * nch + k for k in range(n)},
        compiler_params=pltpu.CompilerParams(has_side_effects=DATAFLOW),
    )(*[_hbm(a) for a in lands])
    return res[:nch], res[nch:2 * nch], res[2 * nch:2 * nch + n], res[-1]


def gather_wait(send_sem, recv_sem, lands, after, name):
    n = len(lands)

    def body(*refs):
        land_refs = refs[:n]
        ssem, rsem = refs[n], refs[n + 1]
        x, y, c = _place()
        me = _chip_of(x, y)
        for pos in range(n):
            for r, peer in enumerate(_other_chips(x, y, c)):
                cp = pltpu.make_async_remote_copy(
                    src_ref=land_refs[pos].at[me], dst_ref=land_refs[pos].at[_chip_of(peer[0], peer[1])],
                    send_sem=ssem.at[N_PEER_CHIPS * pos + r], recv_sem=rsem.at[N_PEER_CHIPS * pos + r],
                    device_id=peer, device_id_type=MESH)
                cp.wait_send()
                cp.wait_recv()

    return pl.pallas_call(
        body, name=name, out_shape=tuple(_hbm_like(lands)),
        in_specs=[*[HBM_SPEC] * n, SEM_SPEC, SEM_SPEC, ANY_SPEC], out_specs=[HBM_SPEC] * n,
        input_output_aliases={k: k for k in range(n)},
        compiler_params=pltpu.CompilerParams(has_side_effects=DATAFLOW),
    )(*lands, send_sem, recv_sem, after)


def scatter_start(grads, name):
    n = len(grads)
    recvs = [lax.empty((N_PEER_CHIPS, *g.shape[1:]), g.dtype) for g in grads]

    def body(*refs):
        g_refs, r_refs = refs[:n], refs[n:2 * n]
        send_sems, recv_sems = refs[2 * n], refs[2 * n + 1]
        token = refs[-1]
        x, y, c = _place()
        for k in range(n):
            for r, peer in enumerate(_other_chips(x, y, c)):
                pltpu.make_async_remote_copy(
                    src_ref=g_refs[k].at[_chip_of(peer[0], peer[1])], dst_ref=r_refs[k].at[r],
                    send_sem=send_sems.at[N_PEER_CHIPS * k + r], recv_sem=recv_sems.at[N_PEER_CHIPS * k + r],
                    device_id=peer, device_id_type=MESH).start()
        token[...] = jnp.zeros(token.shape, F32)

    sem = pltpu.SemaphoreType.DMA((N_PEER_CHIPS * n,))
    res = pl.pallas_call(
        body, name=name,
        out_shape=(sem, sem, *_hbm_like(grads), *_hbm_like(recvs), jax.ShapeDtypeStruct(DEP_SPEC_SHAPE, F32)),
        in_specs=[HBM_SPEC] * (2 * n),
        out_specs=(SEM_SPEC, SEM_SPEC, *[HBM_SPEC] * (2 * n), pl.BlockSpec(memory_space=pltpu.VMEM)),
        input_output_aliases={k: 2 + k for k in range(2 * n)},
        compiler_params=pltpu.CompilerParams(has_side_effects=DATAFLOW),
    )(*[_hbm(a) for a in grads], *[_hbm(a) for a in recvs])
    return res[0], res[1], res[2:2 + n], res[2 + n:2 + 2 * n], res[-1]


def scatter_wait(send_sem, recv_sem, grads, recvs, after, name):
    n = len(grads)

    def body(*refs):
        g_refs, r_refs = refs[:n], refs[n:2 * n]
        ssem, rsem = refs[2 * n], refs[2 * n + 1]
        x, y, c = _place()
        for k in range(n):
            for r, peer in enumerate(_other_chips(x, y, c)):
                cp = pltpu.make_async_remote_copy(
                    src_ref=g_refs[k].at[_chip_of(peer[0], peer[1])], dst_ref=r_refs[k].at[r],
                    send_sem=ssem.at[N_PEER_CHIPS * k + r], recv_sem=rsem.at[N_PEER_CHIPS * k + r],
                    device_id=peer, device_id_type=MESH)
                cp.wait_send()
                cp.wait_recv()

    res = pl.pallas_call(
        body, name=name, out_shape=(*_hbm_like(grads), *_hbm_like(recvs)),
        in_specs=[*[HBM_SPEC] * (2 * n), SEM_SPEC, SEM_SPEC, ANY_SPEC], out_specs=[HBM_SPEC] * (2 * n),
        input_output_aliases={k: k for k in range(2 * n)},
        compiler_params=pltpu.CompilerParams(has_side_effects=DATAFLOW),
    )(*grads, *recvs, send_sem, recv_sem, after)
    return res[:n], res[n:]


def swap_start(parts, name):
    n = len(parts)
    lands = [lax.empty(p.shape, p.dtype) for p in parts]

    def body(*refs):
        p_refs, l_refs = refs[:n], refs[n:2 * n]
        sems = refs[2 * n:4 * n]
        token = refs[-1]
        x, y, c = _place()
        for k in range(n):
            pltpu.make_async_remote_copy(
                src_ref=p_refs[k], dst_ref=l_refs[k], send_sem=sems[k], recv_sem=sems[n + k],
                device_id=(x, y, 1 - c), device_id_type=MESH).start()
        token[...] = jnp.zeros(token.shape, F32)

    sem = pltpu.SemaphoreType.DMA(())
    res = pl.pallas_call(
        body, name=name,
        out_shape=(*[sem] * (2 * n), *_hbm_like(parts), *_hbm_like(lands), jax.ShapeDtypeStruct(DEP_SPEC_SHAPE, F32)),
        in_specs=[HBM_SPEC] * (2 * n),
        out_specs=(*[SEM_SPEC] * (2 * n), *[HBM_SPEC] * (2 * n), pl.BlockSpec(memory_space=pltpu.VMEM)),
        input_output_aliases={k: 2 * n + k for k in range(2 * n)},
        compiler_params=pltpu.CompilerParams(has_side_effects=DATAFLOW),
    )(*[_hbm(a) for a in parts], *[_hbm(a) for a in lands])
    return res[:n], res[n:2 * n], res[2 * n:3 * n], res[3 * n:4 * n], res[-1]


def swap_wait(send_sem, recv_sem, part, land, after, name):
    def body(p_ref, l_ref, ssem, rsem, after_ref, p_out, l_out):
        x, y, c = _place()
        cp = pltpu.make_async_remote_copy(src_ref=p_ref, dst_ref=l_ref, send_sem=ssem, recv_sem=rsem,
                                          device_id=(x, y, 1 - c), device_id_type=MESH)
        cp.wait_send()
        cp.wait_recv()

    return pl.pallas_call(
        body, name=name, out_shape=tuple(_hbm_like([part, land])),
        in_specs=[HBM_SPEC, HBM_SPEC, SEM_SPEC, SEM_SPEC, ANY_SPEC], out_specs=[HBM_SPEC, HBM_SPEC],
        input_output_aliases={0: 0, 1: 1},
        compiler_params=pltpu.CompilerParams(has_side_effects=DATAFLOW),
    )(part, land, send_sem, recv_sem, after)


def _xor_peer(x, y, c, k):
    px, py, pc = x ^ ((k >> 2) & 1), y ^ ((k >> 1) & 1), c ^ (k & 1)
    return (px, py, pc), 4 * px + 2 * py + pc


def small_start(land, name="small_start"):
    def body(l_ref, ssem, rsem, l_out, token):
        x, y, c = _place()
        me = 4 * x + 2 * y + c
        for k in range(1, N_DEV):
            peer, _ = _xor_peer(x, y, c, k)
            pltpu.make_async_remote_copy(
                src_ref=l_ref.at[me], dst_ref=l_ref.at[me], send_sem=ssem.at[k - 1], recv_sem=rsem.at[k - 1],
                device_id=peer, device_id_type=MESH).start()
        token[...] = jnp.zeros(token.shape, F32)

    sem = pltpu.SemaphoreType.DMA((N_DEV - 1,))
    return pl.pallas_call(
        body, name=name,
        out_shape=(sem, sem, pltpu.HBM(land.shape, land.dtype), jax.ShapeDtypeStruct(DEP_SPEC_SHAPE, F32)),
        in_specs=[HBM_SPEC], out_specs=(SEM_SPEC, SEM_SPEC, HBM_SPEC, pl.BlockSpec(memory_space=pltpu.VMEM)),
        input_output_aliases={0: 2}, compiler_params=pltpu.CompilerParams(has_side_effects=DATAFLOW),
    )(_hbm(land))


def small_wait(send_sem, recv_sem, land, after, name="small_wait"):
    def body(l_ref, ssem, rsem, after_ref, l_out):
        x, y, c = _place()
        me = 4 * x + 2 * y + c
        for k in range(1, N_DEV):
            peer, slot = _xor_peer(x, y, c, k)
            cp = pltpu.make_async_remote_copy(
                src_ref=l_ref.at[me], dst_ref=l_ref.at[slot], send_sem=ssem.at[k - 1], recv_sem=rsem.at[k - 1],
                device_id=peer, device_id_type=MESH)
            cp.wait_send()
            cp.wait_recv()

    return pl.pallas_call(
        body, name=name, out_shape=pltpu.HBM(land.shape, land.dtype),
        in_specs=[HBM_SPEC, SEM_SPEC, SEM_SPEC, ANY_SPEC], out_specs=HBM_SPEC, input_output_aliases={0: 0},
        compiler_params=pltpu.CompilerParams(has_side_effects=DATAFLOW),
    )(land, send_sem, recv_sem, after)


def sum_slots(land, name="sum_slots", tm=256):
    n, rows, c = land.shape
    tm = _row_tile(rows, tm)

    def body(l_ref, o_ref):
        acc = l_ref[0]
        for j in range(1, n):
            acc = acc + l_ref[j]
        o_ref[...] = acc

    return pl.pallas_call(
        body, name=name, grid=(rows // tm,), in_specs=[pl.BlockSpec((n, tm, c), lambda i: (0, i, 0))],
        out_specs=pl.BlockSpec((tm, c), lambda i: (i, 0)), out_shape=jax.ShapeDtypeStruct((rows, c), F32),
        compiler_params=_cp("parallel"),
    )(land)


PACK_ROW_TILE = 256


def _pack(arrays):
    flat = jnp.concatenate([a.reshape(-1).astype(F32) for a in arrays])
    n = flat.shape[0]
    rows = -(-n // LANES)
    rows = -(-rows // PACK_ROW_TILE) * PACK_ROW_TILE
    return jnp.pad(flat, (0, rows * LANES - n)).reshape(rows, LANES)


def _unpack(packed, shapes, lead=()):
    flat = packed.reshape(*lead, -1)
    out, off = [], 0
    for shp in shapes:
        n = math.prod(shp)
        out.append(flat[..., off:off + n].reshape(*lead, *shp))
        off += n
    return out


def _row(vec):
    return vec.reshape(1, -1)


def kernel(x, mix_pre_g, mix_post_g, ffn_pre_g, ffn_post_g, cm_w_in, cm_b_in, cm_dw, cm_dw_b, cm_ln_g, cm_ln_b, cm_w_out, cm_b_out, kv_norm_g, w_kv, w_q, w_o, ffn_w_in, ffn_dw, ffn_dw_b, ffn_w_out, loss_target, m_mix_pre_g, m_mix_post_g, m_ffn_pre_g, m_ffn_post_g, m_cm_w_in, m_cm_b_in, m_cm_dw, m_cm_dw_b, m_cm_ln_g, m_cm_ln_b, m_cm_w_out, m_cm_b_out, m_kv_norm_g, m_w_kv, m_w_q, m_w_o, m_ffn_w_in, m_ffn_dw, m_ffn_dw_b, m_ffn_w_out, v_mix_pre_g, v_mix_post_g, v_ffn_pre_g, v_ffn_post_g, v_cm_w_in, v_cm_b_in, v_cm_dw, v_cm_dw_b, v_cm_ln_g, v_cm_ln_b, v_cm_w_out, v_cm_b_out, v_kv_norm_g, v_w_kv, v_w_q, v_w_o, v_ffn_w_in, v_ffn_dw, v_ffn_dw_b, v_ffn_w_out):
    names = ["mix_pre_g", "mix_post_g", "ffn_pre_g", "ffn_post_g", "cm_w_in", "cm_b_in", "cm_dw", "cm_dw_b", "cm_ln_g",
             "cm_ln_b", "cm_w_out", "cm_b_out", "kv_norm_g", "w_kv", "w_q", "w_o", "ffn_w_in", "ffn_dw", "ffn_dw_b",
             "ffn_w_out"]
    w_in = dict(zip(names, [mix_pre_g, mix_post_g, ffn_pre_g, ffn_post_g, cm_w_in, cm_b_in, cm_dw, cm_dw_b, cm_ln_g,
                            cm_ln_b, cm_w_out, cm_b_out, kv_norm_g, w_kv, w_q, w_o, ffn_w_in, ffn_dw, ffn_dw_b, ffn_w_out]))
    m_in = dict(zip(names, [m_mix_pre_g, m_mix_post_g, m_ffn_pre_g, m_ffn_post_g, m_cm_w_in, m_cm_b_in, m_cm_dw, m_cm_dw_b,
                            m_cm_ln_g, m_cm_ln_b, m_cm_w_out, m_cm_b_out, m_kv_norm_g, m_w_kv, m_w_q, m_w_o, m_ffn_w_in,
                            m_ffn_dw, m_ffn_dw_b, m_ffn_w_out]))
    v_in = dict(zip(names, [v_mix_pre_g, v_mix_post_g, v_ffn_pre_g, v_ffn_post_g, v_cm_w_in, v_cm_b_in, v_cm_dw, v_cm_dw_b,
                            v_cm_ln_g, v_cm_ln_b, v_cm_w_out, v_cm_b_out, v_kv_norm_g, v_w_kv, v_w_q, v_w_o, v_ffn_w_in,
                            v_ffn_dw, v_ffn_dw_b, v_ffn_w_out]))

    bsz, seq, d = x.shape
    t = bsz * seq
    n_b = DEPTH - N_A
    hw = w_o.shape[-1]
    qw = N_GROUPS * hw
    f2 = ffn_dw_b.shape[-1]
    f = f2 // 2
    me_chip = _chip_of(lax.axis_index("x"), lax.axis_index("y"))

    big = ["cm_w_in", "cm_w_out", "w_kv", "w_q", "w_o", "ffn_w_in", "ffn_w_out"]
    row_sharded = ("cm_w_out", "w_o", "ffn_w_out")
    small_sharded = ["cm_b_in", "cm_dw", "cm_dw_b", "cm_ln_g", "cm_ln_b", "cm_b_out", "ffn_dw"]
    small_pack = _pack([w_in[n] for n in small_sharded])
    chunks = [
        [("cm_w_in", 0), ("small", None)],
        [("cm_w_out", 0)],
        [("ffn_w_in", 0), ("ffn_w_out", 0)],
        [("cm_w_in", 1), ("cm_w_out", 1)],
        [("ffn_w_in", 1), ("ffn_w_out", 1)],
        [("w_kv", None)],
        [("w_q", 0), ("w_o", 0)],
        [("ffn_w_in", 2), ("ffn_w_out", 2)],
        [("w_q", 1), ("w_o", 1)],
        [("ffn_w_in", 3), ("ffn_w_out", 3)],
    ]
    chunk_of = {pc: ck for ck, ch in enumerate(chunks) for pc in ch}

    me_arr = me_chip.astype(jnp.int32).reshape(1)

    def land_of(pc, dep=None):
        n, l = pc
        if n == "small":
            return cast_place(small_pack, None, me_arr, F32, name="place_small", dep=dep)
        return cast_place(w_in[n], l, me_arr, BF16, name=f"place_{n}_{l}", dep=dep)

    groups = [[0, 1], [2], list(range(3, len(chunks)))]
    g_send, g_recv, lands_f = {}, {}, {}

    def start_group(gi, dep, name):
        cks = groups[gi]
        pcs = [pc for ck in cks for pc in chunks[ck]]
        lands = [land_of(pc, dep if k == 0 else None) for k, pc in enumerate(pcs)]
        send, recv, lands_thru, tok = gather_start(lands, [len(chunks[ck]) for ck in cks], name=name)
        pos = 0
        for j, ck in enumerate(cks):
            g_send[ck], g_recv[ck] = send[j], recv[j]
            lands_f[ck] = lands_thru[pos:pos + len(chunks[ck])]
            pos += len(chunks[ck])
        return tok

    token_a = start_group(0, None, "gather_start_a")
    token = start_group(1, token_a, "gather_start_b")
    weights = {}

    def finish_chunk(ck, after):
        got = gather_wait(g_send[ck], g_recv[ck], lands_f[ck], after, name=f"gather_wait{ck}")
        for pc, arr in zip(chunks[ck], got):
            weights[pc] = arr.reshape(1, -1, arr.shape[-1]) if pc[0] in row_sharded else arr

    def wmat(n, l=None, after=None):
        if (n, l) not in weights:
            finish_chunk(chunk_of[(n, l)], after)
        arr = weights[(n, l)]
        return arr, arr.shape[0]

    finish_chunk(0, token)
    small_full = {}
    for n, arr4 in zip(small_sharded, _unpack(weights[("small", None)], [w_in[n].shape for n in small_sharded], lead=(N_CHIPS,))):
        shp = w_in[n].shape
        small_full[n] = jnp.moveaxis(arr4, 0, -2).reshape(*shp[:-1], N_CHIPS * shp[-1])

    x2d = x.reshape(t, d)
    saved = []
    (h1,) = resid_norm_fwd(x2d, None, None, [_row(mix_pre_g[0])], name="norm_in", dep=token)
    xcur = x2d
    kv_state = None
    for i in range(DEPTH):
        sv = {"x_in": xcur, "h1": h1}
        if i < N_A:
            z = mm_nn(h1, *wmat("cm_w_in", i, h1), bias=_row(small_full["cm_b_in"][i]), name=f"cm_in{i}",
                      spg=N_CHIPS)
            u2 = glu_conv_fwd(z.reshape(bsz, seq, 2 * d), small_full["cm_dw"][i], _row(small_full["cm_dw_b"][i]),
                              name=f"glu_conv{i}").reshape(t, d)
            u4 = ln_silu_fwd(u2, _row(small_full["cm_ln_g"][i]), _row(small_full["cm_ln_b"][i]), name=f"ln_silu{i}")
            y = mm_nn(u4, *wmat("cm_w_out", i, u4), bias=_row(small_full["cm_b_out"][i]), out_dtype=BF16,
                      name=f"cm_out{i}")
            sv.update(z=z, u2=u2, u4=u4)
        else:
            j = i - N_A
            q = mm_nn(h1, *wmat("w_q", j, h1), name=f"q_proj{j}", head_major=True, spg=N_CHIPS).reshape(-1, bsz, seq, HEAD_DIM)
            outs, lses = [], []
            for g, dil in enumerate(DILATIONS):
                o_g, l_g = attn_fwd(q, kv_state["kv"], g, dil, hw, name=f"attn_fwd{j}_{g}")
                outs.append(o_g.reshape(-1, t, HEAD_DIM))
                lses.append(l_g.reshape(t, LANES))
            merged, lsej = attn_merge(outs, lses, name=f"attn_merge{j}")
            y = mm_nn(merged, *wmat("w_o", j, merged), out_dtype=BF16, name=f"o_proj{j}")
            sv.update(q=q, merged=merged, lsej=lsej)
        x1, h2 = resid_norm_fwd(xcur, y, _row(mix_post_g[i]), [_row(ffn_pre_g[i])], name=f"resid_mix{i}")
        ffn_after = start_group(2, x1, "gather_start_c") if i == 0 else h2
        p = mm_nn(h2, *wmat("ffn_w_in", i, ffn_after), out_dtype=BF16, name=f"ffn_in{i}", spg=2)
        s_act, conv_a, conv_g = ffn_mid_fwd(p.reshape(bsz, seq, f2), small_full["ffn_dw"][i], _row(ffn_dw_b[i]), name=f"ffn_mid{i}")
        s_act = s_act.reshape(t, f)
        y2 = mm_nn(s_act, *wmat("ffn_w_out", i), out_dtype=BF16, name=f"ffn_out{i}")
        next_gains = []
        if i + 1 < DEPTH:
            next_gains.append(_row(mix_pre_g[i + 1]))
        if i == N_A - 1:
            next_gains.append(_row(kv_norm_g))
        res = resid_norm_fwd(x1, y2, _row(ffn_post_g[i]), next_gains, name=f"resid_ffn{i}")
        sv.update(y=y, x1=x1, h2=h2, p=p, s=s_act, y2=y2, conv_a=conv_a, conv_g=conv_g)
        saved.append(sv)
        xcur = res[0]
        if i + 1 < DEPTH:
            h1 = res[1]
        if i == N_A - 1:
            kvn = res[2]
            kv = mm_nn(kvn, *wmat("w_kv", None, kvn), name="kv_proj", head_major=True, spg=2).reshape(-1, bsz, seq, HEAD_DIM)
            kv_state = {"kv": kv, "kvn": kvn, "x_a": xcur}

    dx, loss_tile = loss_fwd_bwd(xcur, loss_target.reshape(t, d))
    loss = lax.psum(loss_tile[0, 0], ("x", "y", "c"))

    gsm = {n: [None] * w_in[n].shape[0] for n in
           ["mix_pre_g", "mix_post_g", "ffn_pre_g", "ffn_post_g", "cm_b_in", "cm_dw", "cm_dw_b", "cm_ln_g", "cm_ln_b",
            "cm_b_out", "ffn_dw", "ffn_dw_b"]}
    gbig = {}
    in_flight = []
    dep = None

    def start_scatter(pcs, tag):
        ssem, rsem, g_f, r_f, tok = scatter_start([gbig[pc] for pc in pcs], name=f"scatter_start_{tag}")
        in_flight.append((pcs, ssem, rsem, g_f, r_f))
        return tok

    dk_buf = dv_buf = None
    for i in range(DEPTH - 1, -1, -1):
        sv = saved[i]
        dy2, dg, _ = norm_bwd(sv["y2"], _row(ffn_post_g[i]), dx, out_dtype=BF16, name=f"bwd_ffn_post{i}", dep=dep)
        gsm["ffn_post_g"][i] = dg
        ds = mm_nt(dy2, *wmat("ffn_w_out", i), out_dtype=BF16, name=f"bwd_ffn_out_dx{i}")
        gbig[("ffn_w_out", i)] = mm_tn(sv["s"], dy2, 1, name=f"bwd_ffn_out_dw{i}").reshape(N_CHIPS, f // N_CHIPS, d)
        dpa, dpg, ddwa, ddwg, ddba, ddbg = ffn_mid_bwd(sv["p"].reshape(bsz, seq, f2), small_full["ffn_dw"][i], sv["conv_a"],
                                                       sv["conv_g"], ds.reshape(bsz, seq, f), name=f"bwd_ffn_mid{i}")
        gsm["ffn_dw"][i] = jnp.concatenate([jnp.sum(ddwa, axis=1), jnp.sum(ddwg, axis=1)], axis=-1)
        gsm["ffn_dw_b"][i] = jnp.concatenate([ddba, ddbg], axis=-1)
        dp = [dpa.reshape(t, f), dpg.reshape(t, f)]
        dh2 = mm_nt(dp, *wmat("ffn_w_in", i), out_dtype=BF16, name=f"bwd_ffn_in_dx{i}", spg=2)
        gbig[("ffn_w_in", i)] = mm_tn(sv["h2"], dp, N_CHIPS, name=f"bwd_ffn_in_dw{i}", tm=2048)
        dx1, dg, _ = norm_bwd(sv["x1"], _row(ffn_pre_g[i]), dh2, add=dx, name=f"bwd_ffn_pre{i}")
        gsm["ffn_pre_g"][i] = dg
        dep = start_scatter([("ffn_w_in", 0), ("ffn_w_out", 0)], "ffn0") if i == 0 else None
        dy, dg, dbias = norm_bwd(sv["y"], _row(mix_post_g[i]), dx1, out_dtype=BF16, name=f"bwd_mix_post{i}", dep=dep)
        gsm["mix_post_g"][i] = dg
        if i < N_A:
            gsm["cm_b_out"][i] = dbias
            du4 = mm_nt(dy, *wmat("cm_w_out", i), out_dtype=BF16, name=f"bwd_cm_out_dx{i}")
            gbig[("cm_w_out", i)] = mm_tn(sv["u4"], dy, 1, name=f"bwd_cm_out_dw{i}").reshape(N_CHIPS, d // N_CHIPS, d)
            du2, dlg, dlb = ln_silu_bwd(sv["u2"], _row(small_full["cm_ln_g"][i]), _row(small_full["cm_ln_b"][i]), du4,
                                        name=f"bwd_ln_silu{i}")
            gsm["cm_ln_g"][i], gsm["cm_ln_b"][i] = dlg, dlb
            dza, dzg, ddw, ddwb, dba, dbg = glu_conv_bwd(sv["z"].reshape(bsz, seq, 2 * d), small_full["cm_dw"][i],
                                                         du2.reshape(bsz, seq, d), name=f"bwd_glu_conv{i}")
            gsm["cm_dw"][i] = jnp.sum(ddw, axis=1)
            gsm["cm_dw_b"][i] = ddwb
            gsm["cm_b_in"][i] = jnp.concatenate([dba, dbg], axis=-1)
            dz = [dza.reshape(t, d), dzg.reshape(t, d)]
            dh1 = mm_nt(dz, *wmat("cm_w_in", i), out_dtype=BF16, name=f"bwd_cm_in_dx{i}", spg=2)
            gbig[("cm_w_in", i)] = mm_tn(sv["h1"], dz, N_CHIPS, name=f"bwd_cm_in_dw{i}", spg=2)
        else:
            j = i - N_A
            dmerged = mm_nt(dy, *wmat("w_o", j), name=f"bwd_o_proj_dx{j}")
            gbig[("w_o", j)] = mm_tn(sv["merged"], dy, 1, name=f"bwd_o_proj_dw{j}").reshape(N_CHIPS, hw // N_CHIPS, d)
            dmt = attn_bwd_prep(dmerged, sv["merged"], name=f"bwd_attn_prep{j}")
            dq_buf = None
            add_to_kv = dk_buf is not None
            for g, dil in enumerate(DILATIONS):
                dq_buf, dk_buf, dv_buf = attn_bwd(
                    sv["q"], kv_state["kv"], g, dil, dmerged.reshape(bsz, seq, hw), sv["lsej"].reshape(bsz, seq, LANES),
                    dmt.reshape(bsz, seq, LANES), dq_buf, dk_buf, dv_buf, add_to_kv, hw, name=f"attn_bwd{j}_{g}")
            dq = dq_buf.reshape(t, qw)
            dh1 = mm_nt(dq, *wmat("w_q", j), out_dtype=BF16, name=f"bwd_q_proj_dx{j}", tm=512, spg=N_CHIPS)
            gbig[("w_q", j)] = mm_tn(sv["h1"], dq, N_CHIPS, name=f"bwd_q_proj_dw{j}", tm=512, spg=N_CHIPS)
        dx, dg, _ = norm_bwd(sv["x_in"], _row(mix_pre_g[i]), dh1, add=dx1, name=f"bwd_mix_pre{i}")
        gsm["mix_pre_g"][i] = dg
        if i > N_A:
            dep = start_scatter([("ffn_w_in", i), ("ffn_w_out", i), ("w_q", i - N_A), ("w_o", i - N_A)], f"l{i}")
        elif 0 < i < N_A:
            dep = start_scatter([("ffn_w_in", i), ("ffn_w_out", i), ("cm_w_in", i), ("cm_w_out", i)], f"l{i}")
        elif i == 0:
            last_token = start_scatter([("cm_w_in", 0), ("cm_w_out", 0)], "cm0")
        if i == N_A:
            dkv = [dk_buf.reshape(t, qw), dv_buf.reshape(t, qw)]
            dkvn = mm_nt(dkv, *wmat("w_kv"), out_dtype=BF16, name="bwd_kv_proj_dx")
            gbig[("w_kv", None)] = mm_tn(kv_state["kvn"], dkv, N_CHIPS, name="bwd_kv_proj_dw")
            dx, dg_kv, _ = norm_bwd(kv_state["x_a"], _row(kv_norm_g), dkvn, add=dx, name="bwd_kv_norm")
            dep = start_scatter([("ffn_w_in", i), ("ffn_w_out", i), ("w_q", 0), ("w_o", 0), ("w_kv", None)], f"l{i}")
    grad_x = dx.reshape(bsz, seq, d)

    plane_of = {}
    outs_g, outs_d, outs_m, outs_v = {}, {}, {}, {}

    def finish_scatter(k, after):
        pcs, ssem, rsem, g_f, r_f = in_flight[k]
        g_done, r_done = scatter_wait(ssem, rsem, g_f, r_f, after, name=f"scatter_wait{k}")
        for (n, l), g_arr, r_arr in zip(pcs, g_done, r_done):
            n_layers = 1 if l is None else w_in[n].shape[0]
            plane_of[n] = sum_parts(g_arr, r_arr, me_arr, name=f"sum_chips_{n}_{l}", layer=l or 0, n_layers=n_layers,
                                    out_buf=plane_of.get(n))

    def update(group, tag, after):
        plane = [plane_of[n] for n in group]
        ssems, rsems, plane_f, land_f, _ = swap_start(plane, name=f"swap_start_{tag}")
        for k, n in enumerate(group):
            p_mine, p_other = swap_wait(ssems[k], rsems[k], plane_f[k], land_f[k], after, name=f"swap_wait_{n}")
            shp = w_in[n].shape
            flat = lambda a: a.reshape(-1, shp[-1])
            g_, d_, m_, v_ = adamw(flat(w_in[n]), flat(m_in[n]), flat(v_in[n]), [p_mine, p_other], name=f"adamw_{n}")
            outs_g[n], outs_d[n], outs_m[n], outs_v[n] = (a.reshape(shp) for a in (g_, d_, m_, v_))
            after = v_
        return after

    small_names = [n for n in names if n not in big]
    small_shapes_full = {}
    small_grads_full = []
    for n in small_names:
        if n == "kv_norm_g":
            gfull = dg_kv.reshape(-1)
        elif n in ("cm_dw", "ffn_dw"):
            gfull = jnp.stack(gsm[n], axis=0)
        else:
            gfull = jnp.stack([a.reshape(-1) for a in gsm[n]], axis=0)
        small_shapes_full[n] = gfull.shape
        small_grads_full.append(gfull)
    dev_arr = (4 * lax.axis_index("x") + 2 * lax.axis_index("y") + lax.axis_index("c")).astype(jnp.int32).reshape(1)
    small_land = cast_place(_pack(small_grads_full) + last_token[0, 0], None, dev_arr, F32, name="place_small_grads",
                            nslots=N_DEV)
    sm_send, sm_recv, small_land, small_token = small_start(small_land)

    for k in range(len(in_flight) - 1):
        finish_scatter(k, small_token)
    done = update(["w_kv", "w_q", "w_o", "ffn_w_in", "ffn_w_out"], "a", small_token)
    finish_scatter(len(in_flight) - 1, done)
    done = update(["cm_w_in", "cm_w_out"], "b", done)

    summed = sum_slots(small_wait(sm_send, sm_recv, small_land, done))
    g_full = dict(zip(small_names, _unpack(summed, [small_shapes_full[n] for n in small_names])))
    g_loc = {}
    for n in small_names:
        if n in small_sharded:
            width = w_in[n].shape[-1]
            g_loc[n] = lax.dynamic_slice_in_dim(g_full[n], me_chip * width, width, axis=g_full[n].ndim - 1)
        else:
            g_loc[n] = g_full[n]
    res = adamw(_pack([w_in[n] for n in small_names]), _pack([m_in[n] for n in small_names]),
                _pack([v_in[n] for n in small_names]), [_pack([g_loc[n] for n in small_names])], name="adamw_small")
    shapes_loc = [w_in[n].shape for n in small_names]
    for dst, packed in zip((outs_g, outs_d, outs_m, outs_v), res):
        for n, a in zip(small_names, _unpack(packed, shapes_loc)):
            dst[n] = a

    return (loss, grad_x, *[outs_g[n] for n in names], *[outs_d[n] for n in names],
            *[outs_m[n] for n in names], *[outs_v[n] for n in names])
```

```python
import math

import jax
import jax.numpy as jnp
from jax import lax
from jax.experimental import pallas as pl
from jax.experimental.pallas import tpu as pltpu

F32 = jnp.float32
BF16 = jnp.bfloat16
EPS = 1e-6
NEG_INF = -1e30
N_A = 2
DEPTH = 4
N_GROUPS = 3
DILATIONS = (1, 4, 16)
HEAD_DIM = 128
BLK = 128
LANES = 128
N_CHIPS = 4
N_DEV = 8
VMEM_LIMIT_V7X = 56 * 1024 * 1024

ADAM_LR = 0.001
ADAM_B1 = 0.9
ADAM_B2 = 0.999
ADAM_EPS = 1e-08
ADAM_WD = 0.01
ADAM_STEP = 10

MESH = pl.DeviceIdType.MESH


def _cp(*sem, **kw):
    return pltpu.CompilerParams(dimension_semantics=sem if sem else None, vmem_limit_bytes=VMEM_LIMIT_V7X, **kw)


def _dot(a, b):
    return jnp.dot(a, b, preferred_element_type=F32)


def _dot_nt(a, b):
    return lax.dot_general(a, b, (((1,), (1,)), ((), ())), preferred_element_type=F32)


def _dot_tn(a, b):
    return lax.dot_general(a, b, (((0,), (0,)), ((), ())), preferred_element_type=F32)


def _sigmoid(x):
    return 1.0 / (1.0 + jnp.exp(-x))


def _row_tile(n, want):
    if n <= want:
        return n
    for t in range(want - want % 8, 7, -8):
        if n % t == 0:
            return t
    raise ValueError(f"no row tile for {n} rows")


def mm_nn(a, w, nsh, bias=None, out_dtype=F32, name="mm_nn", tm=1024, head_major=False, spg=1):
    assert nsh % spg == 0
    m, k = a.shape
    _, k2, ns = w.shape
    assert k == k2
    tm = _row_tile(m, tm)
    has_bias = bias is not None
    hps = ns // HEAD_DIM

    def body(*refs):
        if has_bias:
            a_ref, w_ref, b_ref, o_ref = refs
        else:
            a_ref, w_ref, o_ref = refs
        av = a_ref[...].astype(BF16)
        for jj in range(spg):
            acc = _dot(av, w_ref[jj])
            if has_bias:
                acc = acc + b_ref[:, jj * ns:(jj + 1) * ns]
            if head_major:
                for hh in range(hps):
                    o_ref[jj * hps + hh] = acc[:, hh * HEAD_DIM:(hh + 1) * HEAD_DIM].astype(out_dtype)
            else:
                o_ref[:, jj * ns:(jj + 1) * ns] = acc.astype(out_dtype)

    in_specs = [
        pl.BlockSpec((tm, k), lambda j, i: (i, 0)),
        pl.BlockSpec((spg, k, ns), lambda j, i: (j, 0, 0)),
    ]
    args = [a, w]
    if has_bias:
        in_specs.append(pl.BlockSpec((1, spg * ns), lambda j, i: (0, j)))
        args.append(bias)
    return pl.pallas_call(
        body,
        name=name,
        grid=(nsh // spg, m // tm),
        in_specs=in_specs,
        out_specs=(pl.BlockSpec((spg * hps, tm, HEAD_DIM), lambda j, i: (j, i, 0)) if head_major
                   else pl.BlockSpec((tm, spg * ns), lambda j, i: (i, j))),
        out_shape=jax.ShapeDtypeStruct((nsh * hps, m, HEAD_DIM) if head_major else (m, nsh * ns), out_dtype),
        compiler_params=_cp("parallel", "parallel"),
    )(*args)


def _split_parts(dy, nsh, spg):
    dys = list(dy) if isinstance(dy, (list, tuple)) else [dy]
    per = nsh // len(dys)
    assert per % spg == 0
    return dys, per, per // spg


def mm_nt(dy, w, nsh, out_dtype=F32, name="mm_nt", tm=1024, spg=1):
    dys, per, gpp = _split_parts(dy, nsh, spg)
    npart = len(dys)
    m = dys[0].shape[0]
    _, k, ns = w.shape
    assert all(d.shape == (m, per * ns) for d in dys)
    tm = _row_tile(m, tm)
    ngrp = nsh // spg

    def body(*refs):
        dy_refs = refs[:npart]
        w_ref, o_ref, acc_ref = refs[npart:]
        j = pl.program_id(1)

        @pl.when(j == 0)
        def _():
            acc_ref[...] = jnp.zeros(acc_ref.shape, F32)

        for pi in range(npart):
            @pl.when(j // gpp == pi)
            def _(pi=pi):
                part = _dot_nt(dy_refs[pi][:, 0:ns].astype(BF16), w_ref[0])
                for jj in range(1, spg):
                    part = part + _dot_nt(dy_refs[pi][:, jj * ns:(jj + 1) * ns].astype(BF16), w_ref[jj])
                acc_ref[...] += part

        @pl.when(j == ngrp - 1)
        def _():
            o_ref[...] = acc_ref[...].astype(out_dtype)

    dy_specs = [pl.BlockSpec((tm, spg * ns), lambda i, j, pi=pi: (i, jnp.clip(j - pi * gpp, 0, gpp - 1)))
                for pi in range(npart)]
    return pl.pallas_call(
        body,
        name=name,
        grid=(m // tm, ngrp),
        in_specs=[*dy_specs, pl.BlockSpec((spg, k, ns), lambda i, j: (j, 0, 0))],
        out_specs=pl.BlockSpec((tm, k), lambda i, j: (i, 0)),
        out_shape=jax.ShapeDtypeStruct((m, k), out_dtype),
        scratch_shapes=[pltpu.VMEM((tm, k), F32)],
        compiler_params=_cp("parallel", "arbitrary"),
    )(*dys, w)


def mm_tn(a, dy, nsh, name="mm_tn", tm=1024, spg=1):
    dys, per, gpp = _split_parts(dy, nsh, spg)
    npart = len(dys)
    m, k = a.shape
    ns = dys[0].shape[1] // per
    assert all(d.shape == (m, per * ns) for d in dys)
    tm = _row_tile(m, tm)
    nt = m // tm

    def body(*refs):
        a_ref = refs[0]
        dy_refs = refs[1:1 + npart]
        o_ref, acc_ref = refs[1 + npart:]
        j = pl.program_id(0)
        i = pl.program_id(1)

        @pl.when(i == 0)
        def _():
            acc_ref[...] = jnp.zeros(acc_ref.shape, F32)

        for pi in range(npart):
            @pl.when(j // gpp == pi)
            def _(pi=pi):
                at = a_ref[...].astype(BF16).T
                for jj in range(spg):
                    acc_ref[jj] += _dot(at, dy_refs[pi][:, jj * ns:(jj + 1) * ns].astype(BF16))

        @pl.when(i == nt - 1)
        def _():
            o_ref[...] = acc_ref[...].astype(BF16)

    dy_specs = [
        pl.BlockSpec((tm, spg * ns),
                     lambda j, i, pi=pi: (jnp.where(j // gpp == pi, i, 0), jnp.clip(j - pi * gpp, 0, gpp - 1)))
        for pi in range(npart)
    ]
    return pl.pallas_call(
        body,
        name=name,
        grid=(nsh // spg, nt),
        in_specs=[pl.BlockSpec((tm, k), lambda j, i: (i, 0)), *dy_specs],
        out_specs=pl.BlockSpec((spg, k, ns), lambda j, i: (j, 0, 0)),
        out_shape=jax.ShapeDtypeStruct((nsh, k, ns), BF16),
        scratch_shapes=[pltpu.VMEM((spg, k, ns), F32)],
        compiler_params=_cp("parallel", "arbitrary"),
    )(a, *dys)


DEP_SPEC_SHAPE = (8, LANES)


def resid_norm_fwd(x, y, g_post, next_gains, name, tm=1024, dep=None):
    t, d = x.shape
    tm = _row_tile(t, tm)
    has_y = y is not None
    n_next = len(next_gains)
    n_dep = 0 if dep is None else 1

    def body(*refs):
        x_ref = refs[0]
        pos = 1
        if has_y:
            y_ref, gp_ref = refs[1], refs[2]
            pos = 3
        gn_refs = refs[pos:pos + n_next]
        outs = refs[pos + n_next + n_dep:]
        xv = x_ref[...]
        o = 0
        if has_y:
            yv = y_ref[...].astype(F32)
            r = lax.rsqrt(jnp.mean(yv * yv, axis=-1, keepdims=True) + EPS)
            xv = xv + (yv * r) * gp_ref[...]
            outs[0][...] = xv
            o = 1
        if n_next:
            xn = xv * lax.rsqrt(jnp.mean(xv * xv, axis=-1, keepdims=True) + EPS)
            for k in range(n_next):
                outs[o + k][...] = (xn * gn_refs[k][...]).astype(BF16)

    row = pl.BlockSpec((tm, d), lambda i: (i, 0))
    vec = pl.BlockSpec((1, d), lambda i: (0, 0))
    args, in_specs = [x], [row]
    if has_y:
        args += [y, g_post]
        in_specs += [row, vec]
    args += list(next_gains)
    in_specs += [vec] * n_next
    if n_dep:
        args.append(dep)
        in_specs.append(pl.BlockSpec(DEP_SPEC_SHAPE, lambda i: (0, 0)))
    out_shape, out_specs = [], []
    if has_y:
        out_shape.append(jax.ShapeDtypeStruct((t, d), F32))
        out_specs.append(row)
    for _ in range(n_next):
        out_shape.append(jax.ShapeDtypeStruct((t, d), BF16))
        out_specs.append(row)
    return pl.pallas_call(
        body, name=name, grid=(t // tm,), in_specs=in_specs, out_specs=out_specs, out_shape=out_shape,
        compiler_params=_cp("parallel"),
    )(*args)


def norm_bwd(x, g, dy, add=None, out_dtype=F32, name="norm_bwd", tm=1024, dep=None):
    t, d = x.shape
    tm = _row_tile(t, tm)
    has_add = add is not None

    def body(*refs):
        x_ref, g_ref, dy_ref = refs[:3]
        add_ref = refs[3] if has_add else None
        dx_ref, dg_ref, cs_ref = refs[-3:]
        i = pl.program_id(0)
        xv = x_ref[...].astype(F32)
        dyv = dy_ref[...].astype(F32)
        r = lax.rsqrt(jnp.mean(xv * xv, axis=-1, keepdims=True) + EPS)
        gd = dyv * g_ref[...]
        dx = r * gd - xv * ((r * r * r) * jnp.mean(xv * gd, axis=-1, keepdims=True))
        if has_add:
            dx = dx + add_ref[...]
        dx_ref[...] = dx.astype(out_dtype)
        dg = jnp.sum(dyv * (xv * r), axis=0, keepdims=True)
        cs = jnp.sum(dx, axis=0, keepdims=True)

        @pl.when(i == 0)
        def _():
            dg_ref[...] = dg
            cs_ref[...] = cs

        @pl.when(i > 0)
        def _():
            dg_ref[...] += dg
            cs_ref[...] += cs

    row = pl.BlockSpec((tm, d), lambda i: (i, 0))
    vec = pl.BlockSpec((1, d), lambda i: (0, 0))
    args, in_specs = [x, g, dy], [row, vec, row]
    if has_add:
        args.append(add)
        in_specs.append(row)
    if dep is not None:
        args.append(dep)
        in_specs.append(pl.BlockSpec(DEP_SPEC_SHAPE, lambda i: (0, 0)))
    return pl.pallas_call(
        body, name=name, grid=(t // tm,), in_specs=in_specs,
        out_specs=[row, vec, vec],
        out_shape=[jax.ShapeDtypeStruct((t, d), out_dtype), jax.ShapeDtypeStruct((1, d), F32),
                   jax.ShapeDtypeStruct((1, d), F32)],
        compiler_params=_cp("arbitrary"),
    )(*args)


def loss_fwd_bwd(x, target, name="loss", tm=1024):
    t, d = x.shape
    tm = _row_tile(t, tm)

    def body(x_ref, t_ref, dx_ref, l_ref):
        i = pl.program_id(0)
        err = x_ref[...] - t_ref[...]
        dx_ref[...] = err * (1.0 / d)
        part = 0.5 * jnp.sum(jnp.mean(err * err, axis=-1, keepdims=True), axis=0, keepdims=True)
        part = jnp.broadcast_to(part, l_ref.shape)

        @pl.when(i == 0)
        def _():
            l_ref[...] = part

        @pl.when(i > 0)
        def _():
            l_ref[...] += part

    row = pl.BlockSpec((tm, d), lambda i: (i, 0))
    return pl.pallas_call(
        body, name=name, grid=(t // tm,), in_specs=[row, row],
        out_specs=[row, pl.BlockSpec((8, LANES), lambda i: (0, 0))],
        out_shape=[jax.ShapeDtypeStruct((t, d), F32), jax.ShapeDtypeStruct((8, LANES), F32)],
        compiler_params=_cp("arbitrary"),
    )(x, target)


CONV_HALO = 32
CONV_CHUNK = 128


def glu_conv_fwd(z, dw, dwb, name, tc=128):
    b, s, c2 = z.shape
    c = c2 // 2
    kw = dw.shape[0]
    tc = min(tc, c)
    nc = c // tc
    ch = min(CONV_CHUNK, s)
    halo = CONV_HALO
    assert kw - 1 <= halo and s % ch == 0

    nch = s // ch

    def body(a_ref, g_ref, w_ref, b_ref, o_ref, pad_ref):
        _fill_glu_slabs(a_ref, g_ref, pad_ref, nch, ch, halo)

        def chunk(ci, carry):
            r0 = pl.multiple_of(ci * ch, ch)
            acc = b_ref[...]
            for k, tap in enumerate(_taps_front(pad_ref, ci, kw, ch, halo)):
                acc = acc + w_ref[k:k + 1, :] * tap
            o_ref[pl.ds(r0, ch), :] = acc
            return carry

        lax.fori_loop(0, nch, chunk, 0)

    return pl.pallas_call(
        body, name=name, grid=(b, nc),
        in_specs=[
            pl.BlockSpec((None, s, tc), lambda bi, i: (bi, 0, i)),
            pl.BlockSpec((None, s, tc), lambda bi, i: (bi, 0, i + nc)),
            pl.BlockSpec((kw, tc), lambda bi, i: (0, i)),
            pl.BlockSpec((1, tc), lambda bi, i: (0, i)),
        ],
        out_specs=pl.BlockSpec((None, s, tc), lambda bi, i: (bi, 0, i)),
        out_shape=jax.ShapeDtypeStruct((b, s, c), F32),
        scratch_shapes=[pltpu.VMEM((nch, ch + halo, tc), F32)],
        compiler_params=_cp("parallel", "parallel"),
    )(z, z, dw, dwb)


def glu_conv_bwd(z, dw, du2, name, tc=128):
    b, s, c2 = z.shape
    c = c2 // 2
    kw = dw.shape[0]
    tc = min(tc, c)
    nc = c // tc
    ch = min(CONV_CHUNK, s)
    nch = s // ch
    halo = CONV_HALO

    def body(a_ref, g_ref, w_ref, du_ref, dza_ref, dzg_ref, ddw_ref, ddwb_ref, dba_ref, dbg_ref, upad_ref, dpad_ref):
        bi = pl.program_id(1)

        @pl.when(bi == 0)
        def _():
            ddw_ref[...] = jnp.zeros(ddw_ref.shape, F32)
            ddwb_ref[...] = jnp.zeros(ddwb_ref.shape, F32)
            dba_ref[...] = jnp.zeros(dba_ref.shape, F32)
            dbg_ref[...] = jnp.zeros(dbg_ref.shape, F32)

        _fill_glu_slabs(a_ref, g_ref, upad_ref, nch, ch, halo)
        dpad_ref[nch - 1, ch:ch + halo, :] = jnp.zeros((halo, tc), F32)
        dpad_ref[nch - 1, 0:ch, :] = du_ref[s - ch:s, :]

        def fill(ci, carry):
            r0 = pl.multiple_of(ci * ch, ch)
            dpad_ref[ci, :, :] = du_ref[pl.ds(r0, ch + halo), :]
            return carry

        lax.fori_loop(0, nch - 1, fill, 0)

        def chunk(ci, carry):
            r0 = pl.multiple_of(ci * ch, ch)
            du_c = du_ref[pl.ds(r0, ch), :]
            taps_u = _taps_front(upad_ref, ci, kw, ch, halo)
            taps_d = _taps_at(dpad_ref, ci, list(range(kw)), ch)
            du1 = w_ref[kw - 1:kw, :] * du_c
            ddw_ref[kw - 1] += jnp.sum((du_c * taps_u[kw - 1]).reshape(ch // 8, 8, tc), axis=0)
            for j in range(1, kw):
                du1 = du1 + w_ref[kw - 1 - j:kw - j, :] * taps_d[j]
                ddw_ref[kw - 1 - j] += jnp.sum((du_c * taps_u[kw - 1 - j]).reshape(ch // 8, 8, tc), axis=0)
            av = a_ref[pl.ds(r0, ch), :]
            sg = _sigmoid(g_ref[pl.ds(r0, ch), :])
            dza = du1 * sg
            dzg = du1 * av * (sg * (1.0 - sg))
            dza_ref[pl.ds(r0, ch), :] = dza.astype(BF16)
            dzg_ref[pl.ds(r0, ch), :] = dzg.astype(BF16)
            dba_ref[...] += jnp.sum(dza, axis=0, keepdims=True)
            dbg_ref[...] += jnp.sum(dzg, axis=0, keepdims=True)
            ddwb_ref[...] += jnp.sum(du_c, axis=0, keepdims=True)
            return carry

        lax.fori_loop(0, s // ch, chunk, 0)

    blk = lambda off: pl.BlockSpec((None, s, tc), lambda i, bi: (bi, 0, i + off))
    vec = pl.BlockSpec((1, tc), lambda i, bi: (0, i))
    return pl.pallas_call(
        body, name=name, grid=(nc, b),
        in_specs=[blk(0), blk(nc), pl.BlockSpec((kw, tc), lambda i, bi: (0, i)), blk(0)],
        out_specs=[blk(0), blk(0), pl.BlockSpec((kw, 8, tc), lambda i, bi: (0, 0, i)), vec, vec, vec],
        out_shape=[
            jax.ShapeDtypeStruct((b, s, c), BF16), jax.ShapeDtypeStruct((b, s, c), BF16),
            jax.ShapeDtypeStruct((kw, 8, c), F32), jax.ShapeDtypeStruct((1, c), F32),
            jax.ShapeDtypeStruct((1, c), F32), jax.ShapeDtypeStruct((1, c), F32),
        ],
        scratch_shapes=[pltpu.VMEM((nch, ch + halo, tc), F32), pltpu.VMEM((nch, ch + halo, tc), F32)],
        compiler_params=_cp("parallel", "arbitrary"),
    )(z, z, dw, du2)


def _fill_glu_slabs(a_ref, g_ref, pad_ref, nch, ch, halo):
    tc = a_ref.shape[-1]
    pad_ref[0, 0:halo, :] = jnp.zeros((halo, tc), F32)
    pad_ref[0, halo:halo + ch, :] = a_ref[0:ch, :] * _sigmoid(g_ref[0:ch, :])

    def fill(ci, carry):
        r0 = pl.multiple_of(ci * ch, ch)
        pad_ref[ci, 0:halo, :] = pad_ref[ci - 1, ch:ch + halo, :]
        pad_ref[ci, halo:halo + ch, :] = a_ref[pl.ds(r0, ch), :] * _sigmoid(g_ref[pl.ds(r0, ch), :])
        return carry

    lax.fori_loop(1, nch, fill, 0)


def ln_silu_fwd(u, g, bvec, name, tm=1024):
    t, d = u.shape
    tm = _row_tile(t, tm)

    def body(u_ref, g_ref, b_ref, o_ref):
        uv = u_ref[...]
        mu = jnp.mean(uv, axis=-1, keepdims=True)
        xc = uv - mu
        var = jnp.mean(xc * xc, axis=-1, keepdims=True)
        v = (xc * lax.rsqrt(var + EPS)) * g_ref[...] + b_ref[...]
        o_ref[...] = (v * _sigmoid(v)).astype(BF16)

    row = pl.BlockSpec((tm, d), lambda i: (i, 0))
    vec = pl.BlockSpec((1, d), lambda i: (0, 0))
    return pl.pallas_call(
        body, name=name, grid=(t // tm,), in_specs=[row, vec, vec], out_specs=row,
        out_shape=jax.ShapeDtypeStruct((t, d), BF16), compiler_params=_cp("parallel"),
    )(u, g, bvec)


def ln_silu_bwd(u, g, bvec, dout, name, tm=1024):
    t, d = u.shape
    tm = _row_tile(t, tm)

    def body(u_ref, g_ref, b_ref, do_ref, du_ref, dg_ref, db_ref):
        i = pl.program_id(0)
        uv = u_ref[...]
        mu = jnp.mean(uv, axis=-1, keepdims=True)
        xc = uv - mu
        var = jnp.mean(xc * xc, axis=-1, keepdims=True)
        rstd = lax.rsqrt(var + EPS)
        n = xc * rstd
        v = n * g_ref[...] + b_ref[...]
        sg = _sigmoid(v)
        dv = do_ref[...].astype(F32) * (sg * (1.0 + v * (1.0 - sg)))
        dn = dv * g_ref[...]
        du_ref[...] = rstd * (dn - jnp.mean(dn, axis=-1, keepdims=True) - n * jnp.mean(dn * n, axis=-1, keepdims=True))
        dg = jnp.sum(dv * n, axis=0, keepdims=True)
        db = jnp.sum(dv, axis=0, keepdims=True)

        @pl.when(i == 0)
        def _():
            dg_ref[...] = dg
            db_ref[...] = db

        @pl.when(i > 0)
        def _():
            dg_ref[...] += dg
            db_ref[...] += db

    row = pl.BlockSpec((tm, d), lambda i: (i, 0))
    vec = pl.BlockSpec((1, d), lambda i: (0, 0))
    return pl.pallas_call(
        body, name=name, grid=(t // tm,), in_specs=[row, vec, vec, row], out_specs=[row, vec, vec],
        out_shape=[jax.ShapeDtypeStruct((t, d), F32), jax.ShapeDtypeStruct((1, d), F32), jax.ShapeDtypeStruct((1, d), F32)],
        compiler_params=_cp("arbitrary"),
    )(u, g, bvec, dout)


FFN_HALO = 8


def _fill_front_halo(src_ref, pad_ref, nch, ch, halo):
    tc = src_ref.shape[-1]
    pad_ref[0, 0:halo, :] = jnp.zeros((halo, tc), F32)
    pad_ref[0, halo:halo + ch, :] = src_ref[0:ch, :].astype(F32)

    def fill(ci, carry):
        r0 = pl.multiple_of(ci * ch, ch)
        pad_ref[ci, 0:halo, :] = src_ref[pl.ds(r0 - 2 * halo, 2 * halo), :].astype(F32)[halo:, :]
        pad_ref[ci, halo:halo + ch, :] = src_ref[pl.ds(r0, ch), :].astype(F32)
        return carry

    lax.fori_loop(1, nch, fill, 0)


def _taps_at(pad_ref, ci, offsets, ch):
    windows = {}
    for b in sorted({o % 8 for o in offsets}):
        top = max(o for o in offsets if o % 8 == b)
        windows[b] = pad_ref[ci, b:top + ch, :]
    return [windows[o % 8][o - o % 8:o - o % 8 + ch, :] for o in offsets]


def _taps_front(pad_ref, ci, kw, ch, halo):
    return _taps_at(pad_ref, ci, [halo - (kw - 1 - k) for k in range(kw)], ch)


def ffn_mid_fwd(p, dw, dwb, name, tc=256):
    b, s, f2 = p.shape
    f = f2 // 2
    kw = dw.shape[0]
    tc = min(tc, f)
    nf = f // tc
    ch = min(CONV_CHUNK, s)
    nch = s // ch
    halo = FFN_HALO

    def body(pa_ref, pg_ref, wa_ref, wg_ref, ba_ref, bg_ref, o_ref, ca_ref, cg_ref, apad_ref, gpad_ref):
        _fill_front_halo(pa_ref, apad_ref, nch, ch, halo)
        _fill_front_halo(pg_ref, gpad_ref, nch, ch, halo)

        def chunk(ci, carry):
            r0 = pl.multiple_of(ci * ch, ch)
            ca = ba_ref[...]
            cg = bg_ref[...]
            taps = zip(_taps_front(apad_ref, ci, kw, ch, halo), _taps_front(gpad_ref, ci, kw, ch, halo))
            for k, (ta, tg) in enumerate(taps):
                ca = ca + wa_ref[k:k + 1, :] * ta
                cg = cg + wg_ref[k:k + 1, :] * tg
            sg = _sigmoid(cg)
            act = cg * sg
            o_ref[pl.ds(r0, ch), :] = (act * ca).astype(BF16)
            ca_ref[pl.ds(r0, ch), :] = act.astype(BF16)
            cg_ref[pl.ds(r0, ch), :] = (ca * (sg * (1.0 + cg * (1.0 - sg)))).astype(BF16)
            return carry

        lax.fori_loop(0, nch, chunk, 0)

    blk = lambda off: pl.BlockSpec((None, s, tc), lambda bi, i: (bi, 0, i + off))
    wsp = lambda off: pl.BlockSpec((kw, tc), lambda bi, i: (0, i + off))
    bsp = lambda off: pl.BlockSpec((1, tc), lambda bi, i: (0, i + off))
    return pl.pallas_call(
        body, name=name, grid=(b, nf),
        in_specs=[blk(0), blk(nf), wsp(0), wsp(nf), bsp(0), bsp(nf)],
        out_specs=[blk(0)] * 3,
        out_shape=[jax.ShapeDtypeStruct((b, s, f), BF16)] * 3,
        scratch_shapes=[pltpu.VMEM((nch, ch + halo, tc), F32)] * 2,
        compiler_params=_cp("parallel", "parallel"),
    )(p, p, dw, dw, dwb, dwb)


def ffn_mid_bwd(p, dw, ca_sv, cg_sv, ds, name, tc=256):
    b, s, f2 = p.shape
    f = f2 // 2
    kw = dw.shape[0]
    tc = min(tc, f)
    nf = f // tc
    ch = min(CONV_CHUNK, s)
    nch = s // ch
    halo = FFN_HALO

    def sum8(v):
        return jnp.sum(v.reshape(ch // 8, 8, tc), axis=0)

    def body(pa_ref, pg_ref, wa_ref, wg_ref, ca_ref, cg_ref, ds_ref, dpa_ref, dpg_ref, ddwa_ref, ddwg_ref, dba_ref, dbg_ref,
             dca_ref, dcg_ref):
        bi = pl.program_id(1)

        @pl.when(bi == 0)
        def _():
            ddwa_ref[...] = jnp.zeros(ddwa_ref.shape, F32)
            ddwg_ref[...] = jnp.zeros(ddwg_ref.shape, F32)
            dba_ref[...] = jnp.zeros(dba_ref.shape, F32)
            dbg_ref[...] = jnp.zeros(dbg_ref.shape, F32)

        dca_ref[nch - 1, ch:ch + halo, :] = jnp.zeros((halo, tc), F32)
        dcg_ref[nch - 1, ch:ch + halo, :] = jnp.zeros((halo, tc), F32)

        def grads(ci, carry):
            sb_a, sb_g = carry
            r0 = pl.multiple_of(ci * ch, ch)
            dsv = ds_ref[pl.ds(r0, ch), :].astype(F32)
            dca = dsv * ca_ref[pl.ds(r0, ch), :].astype(F32)
            dcg = dsv * cg_ref[pl.ds(r0, ch), :].astype(F32)
            dca_ref[ci, 0:ch, :] = dca
            dcg_ref[ci, 0:ch, :] = dcg

            prev = jnp.maximum(ci - 1, 0)

            @pl.when(ci > 0)
            def _():
                dca_ref[prev, ch:ch + halo, :] = dca[0:halo, :]
                dcg_ref[prev, ch:ch + halo, :] = dcg[0:halo, :]

            return sb_a + sum8(dca), sb_g + sum8(dcg)

        z8 = jnp.zeros((8, tc), F32)
        sb_a, sb_g = lax.fori_loop(0, nch, grads, (z8, z8))
        dba_ref[...] += jnp.sum(sb_a, axis=0, keepdims=True)
        dbg_ref[...] += jnp.sum(sb_g, axis=0, keepdims=True)

        def back(ci, carry):
            acc_a, acc_g = carry
            r0 = pl.multiple_of(ci * ch, ch)
            pa = pa_ref[pl.ds(r0, ch), :].astype(F32)
            pg = pg_ref[pl.ds(r0, ch), :].astype(F32)
            da = dg = None
            new_a = [None] * kw
            new_g = [None] * kw
            for j in range(kw):
                k = kw - 1 - j
                ta = dca_ref[ci, j:j + ch, :]
                tg = dcg_ref[ci, j:j + ch, :]
                ma = wa_ref[k:k + 1, :] * ta
                mg = wg_ref[k:k + 1, :] * tg
                da = ma if da is None else da + ma
                dg = mg if dg is None else dg + mg
                new_a[k] = acc_a[k] + sum8(ta * pa)
                new_g[k] = acc_g[k] + sum8(tg * pg)
            dpa_ref[pl.ds(r0, ch), :] = da.astype(BF16)
            dpg_ref[pl.ds(r0, ch), :] = dg.astype(BF16)
            return tuple(new_a), tuple(new_g)

        acc_a, acc_g = lax.fori_loop(0, nch, back, ((z8,) * kw, (z8,) * kw))
        for k in range(kw):
            ddwa_ref[k] += acc_a[k]
            ddwg_ref[k] += acc_g[k]

    blk = lambda off: pl.BlockSpec((None, s, tc), lambda i, bi: (bi, 0, i + off))
    wsp = lambda off: pl.BlockSpec((kw, tc), lambda i, bi: (0, i + off))
    acc3 = pl.BlockSpec((kw, 8, tc), lambda i, bi: (0, 0, i))
    vec = pl.BlockSpec((1, tc), lambda i, bi: (0, i))
    return pl.pallas_call(
        body, name=name, grid=(nf, b),
        in_specs=[blk(0), blk(nf), wsp(0), wsp(nf), blk(0), blk(0), blk(0)],
        out_specs=[blk(0), blk(0), acc3, acc3, vec, vec],
        out_shape=[jax.ShapeDtypeStruct((b, s, f), BF16), jax.ShapeDtypeStruct((b, s, f), BF16),
                   jax.ShapeDtypeStruct((kw, 8, f), F32), jax.ShapeDtypeStruct((kw, 8, f), F32),
                   jax.ShapeDtypeStruct((1, f), F32), jax.ShapeDtypeStruct((1, f), F32)],
        scratch_shapes=[pltpu.VMEM((nch, ch + halo, tc), F32)] * 2,
        compiler_params=_cp("parallel", "arbitrary"),
    )(p, p, dw, dw, ca_sv, cg_sv, ds)


def _tile_rows(r, n, dil):
    start = r + n * BLK * dil
    return pl.ds(start, BLK, stride=dil) if dil > 1 else pl.ds(start, BLK)


def _band_masks():
    qi = lax.broadcasted_iota(jnp.int32, (BLK, 2 * BLK), 0)
    kk = lax.broadcasted_iota(jnp.int32, (BLK, 2 * BLK), 1)
    both = jnp.logical_or(jnp.logical_and(kk < BLK, kk >= qi), jnp.logical_and(kk >= BLK, kk - BLK <= qi))
    return both, kk[:, :BLK] <= qi[:, :BLK]


def attn_fwd(q, kv, g, dil, hw, name):
    _, b, s, _ = q.shape
    nh = hw // HEAD_DIM
    nblk = s // dil // BLK
    scale = 1.0 / math.sqrt(HEAD_DIM)

    def body(q_ref, k_ref, v_ref, o_ref, lse_ref):
        h = pl.program_id(1)
        mask2, mask1 = _band_masks()
        mine = lax.broadcasted_iota(jnp.int32, (BLK, LANES), 1) == h

        @pl.when(h == 0)
        def _():
            lse_ref[...] = jnp.zeros(lse_ref.shape, F32)

        for r in range(dil):
            kp = vp = None
            for n in range(nblk):
                rs = _tile_rows(r, n, dil)
                qt = q_ref[rs, :].astype(BF16)
                kc = k_ref[rs, :].astype(BF16)
                vc = v_ref[rs, :].astype(BF16)
                if n == 0:
                    kcat, vcat, mask = kc, vc, mask1
                else:
                    kcat, vcat, mask = jnp.concatenate([kp, kc], axis=0), jnp.concatenate([vp, vc], axis=0), mask2
                sc = jnp.where(mask, _dot_nt(qt, kcat) * scale, NEG_INF)
                m = jnp.max(sc, axis=-1, keepdims=True)
                p = jnp.exp(sc - m)
                den = jnp.sum(p, axis=-1, keepdims=True)
                o_ref[rs, :] = _dot(p.astype(BF16), vcat) / den
                lse_ref[rs, :] = jnp.where(mine, m + jnp.log(den), lse_ref[rs, :])
                kp, vp = kc, vc

    col = lambda base: pl.BlockSpec((None, s, HEAD_DIM), lambda bi, h: (bi, 0, base + h))
    head = lambda base: pl.BlockSpec((None, None, s, HEAD_DIM), lambda bi, h: (base + h, bi, 0, 0))
    return pl.pallas_call(
        body, name=name, grid=(b, nh),
        in_specs=[head(g * nh), head(g * nh), head((N_GROUPS + g) * nh)],
        out_specs=[head(0), pl.BlockSpec((None, s, LANES), lambda bi, h: (bi, 0, 0))],
        out_shape=[jax.ShapeDtypeStruct((nh, b, s, HEAD_DIM), F32), jax.ShapeDtypeStruct((b, s, LANES), F32)],
        compiler_params=_cp("parallel", "arbitrary"),
    )(q, kv, kv)


def attn_merge(outs, lses, name, tm=512):
    nh, t, _ = outs[0].shape
    hw = nh * HEAD_DIM
    tm = _row_tile(t, tm)
    ng = len(outs)

    def body(*refs):
        o_refs, l_refs = refs[:ng], refs[ng:2 * ng]
        m_ref, lj_ref = refs[2 * ng:]
        ls = [l_refs[g][...] for g in range(ng)]
        mx = ls[0]
        for g in range(1, ng):
            mx = jnp.maximum(mx, ls[g])
        es = [jnp.exp(l - mx) for l in ls]
        tot = es[0]
        for g in range(1, ng):
            tot = tot + es[g]
        ws = [e / tot for e in es]
        lj_ref[...] = mx + jnp.log(tot)
        for h in range(nh):
            sl = slice(h * HEAD_DIM, (h + 1) * HEAD_DIM)
            acc = ws[0][:, h:h + 1] * o_refs[0][h]
            for g in range(1, ng):
                acc = acc + ws[g][:, h:h + 1] * o_refs[g][h]
            m_ref[:, sl] = acc.astype(BF16)

    row = pl.BlockSpec((tm, hw), lambda i: (i, 0))
    st = pl.BlockSpec((tm, LANES), lambda i: (i, 0))
    heads = pl.BlockSpec((nh, tm, HEAD_DIM), lambda i: (0, i, 0))
    return pl.pallas_call(
        body, name=name, grid=(t // tm,), in_specs=[heads] * ng + [st] * ng, out_specs=[row, st],
        out_shape=[jax.ShapeDtypeStruct((t, hw), BF16), jax.ShapeDtypeStruct((t, LANES), F32)],
        compiler_params=_cp("parallel"),
    )(*outs, *lses)


def attn_bwd_prep(dmerged, merged, name, tm=512):
    t, hw = merged.shape
    nh = hw // HEAD_DIM
    tm = _row_tile(t, tm)

    def body(d_ref, m_ref, o_ref):
        lane = lax.broadcasted_iota(jnp.int32, (tm, LANES), 1)
        acc = jnp.zeros((tm, LANES), F32)
        for h in range(nh):
            sl = slice(h * HEAD_DIM, (h + 1) * HEAD_DIM)
            dsum = jnp.sum(d_ref[:, sl] * m_ref[:, sl].astype(F32), axis=-1, keepdims=True)
            acc = jnp.where(lane == h, dsum, acc)
        o_ref[...] = acc

    row = pl.BlockSpec((tm, hw), lambda i: (i, 0))
    return pl.pallas_call(
        body, name=name, grid=(t // tm,), in_specs=[row, row], out_specs=pl.BlockSpec((tm, LANES), lambda i: (i, 0)),
        out_shape=jax.ShapeDtypeStruct((t, LANES), F32), compiler_params=_cp("parallel"),
    )(dmerged, merged)


def attn_bwd(q, kv, g, dil, do, lsej, dm, dq_buf, dk_buf, dv_buf, accumulate, hw, name):
    _, b, s, _ = q.shape
    nh = hw // HEAD_DIM
    nblk = s // dil // BLK
    scale = 1.0 / math.sqrt(HEAD_DIM)
    assert dk_buf is not None or not accumulate
    kv_at = 6 + (dq_buf is not None)

    def body(*refs):
        q_ref, k_ref, v_ref, do_ref, lj_ref, dm_ref = refs[:6]
        dq_ref, dk_ref, dv_ref = refs[-3:]
        dki_ref, dvi_ref = (refs[kv_at], refs[kv_at + 1]) if accumulate else (None, None)
        mask2, mask1 = _band_masks()
        mine = lax.broadcasted_iota(jnp.int32, (BLK, LANES), 1) == pl.program_id(1)

        def my_lane(v):
            return jnp.sum(jnp.where(mine, v, 0.0), axis=-1, keepdims=True)

        def put(rs, dk, dv):
            if accumulate:
                dk = dk + dki_ref[rs, :]
                dv = dv + dvi_ref[rs, :]
            dk_ref[rs, :] = dk
            dv_ref[rs, :] = dv

        for r in range(dil):
            kp = vp = hold_k = hold_v = rs_prev = None
            for n in range(nblk):
                rs = _tile_rows(r, n, dil)
                qt = q_ref[rs, :].astype(BF16)
                kc = k_ref[rs, :].astype(BF16)
                vc = v_ref[rs, :].astype(BF16)
                dot = do_ref[rs, :].astype(BF16)
                lm = my_lane(lj_ref[rs, :])
                dmm = my_lane(dm_ref[rs, :])
                if n == 0:
                    kcat, vcat, mask = kc, vc, mask1
                else:
                    kcat, vcat, mask = jnp.concatenate([kp, kc], axis=0), jnp.concatenate([vp, vc], axis=0), mask2
                p = jnp.exp(jnp.where(mask, _dot_nt(qt, kcat) * scale, NEG_INF) - lm)
                ds = (p * (_dot_nt(dot, vcat) - dmm)).astype(BF16)
                dq_ref[rs, :] = _dot(ds, kcat) * scale
                dkc = _dot_tn(ds, qt) * scale
                dvc = _dot_tn(p.astype(BF16), dot)
                if n > 0:
                    put(rs_prev, hold_k + dkc[:BLK, :], hold_v + dvc[:BLK, :])
                    dkc, dvc = dkc[BLK:, :], dvc[BLK:, :]
                hold_k, hold_v, kp, vp, rs_prev = dkc, dvc, kc, vc, rs
            put(rs_prev, hold_k, hold_v)

    col = lambda base: pl.BlockSpec((None, s, HEAD_DIM), lambda bi, h: (bi, 0, base + h))
    any_spec = pl.BlockSpec(memory_space=pl.ANY)
    stat = pl.BlockSpec((None, s, LANES), lambda bi, h: (bi, 0, 0))
    head = lambda base: pl.BlockSpec((None, None, s, HEAD_DIM), lambda bi, h: (base + h, bi, 0, 0))
    in_specs = [head(g * nh), head(g * nh), head((N_GROUPS + g) * nh), col(0), stat, stat]
    args = [q, kv, kv, do, lsej, dm]
    aliases = {}
    if dq_buf is not None:
        in_specs.append(any_spec)
        args.append(dq_buf)
        aliases[6] = 0
    if dk_buf is not None:
        in_specs += [col(g * nh) if accumulate else any_spec] * 2
        args += [dk_buf, dv_buf]
        aliases.update({kv_at: 1, kv_at + 1: 2})
    shape = jax.ShapeDtypeStruct((b, s, N_GROUPS * hw), F32)
    return pl.pallas_call(
        body, name=name, grid=(b, nh), in_specs=in_specs, out_specs=[col(g * nh)] * 3, out_shape=[shape] * 3,
        input_output_aliases=aliases, compiler_params=_cp("parallel", "parallel"),
    )(*args)


def sum_parts(g, recv, me, name, tm=512, layer=0, n_layers=1, out_buf=None):
    _, rows, c = g.shape
    n = recv.shape[0]
    tm = _row_tile(rows, tm)
    steps = rows // tm

    def body(me_ref, g_ref, r_ref, *rest):
        acc = g_ref[...].astype(F32)
        for j in range(n):
            acc = acc + r_ref[j].astype(F32)
        rest[-1][...] = acc

    in_specs = [pl.BlockSpec((None, tm, c), lambda i, me_ref: (me_ref[0], i, 0)),
                pl.BlockSpec((n, tm, c), lambda i, me_ref: (0, i, 0))]
    args = [me, g, recv]
    aliases = {}
    if out_buf is not None:
        in_specs.append(pl.BlockSpec(memory_space=pl.ANY))
        args.append(out_buf)
        aliases = {3: 0}
    return pl.pallas_call(
        body, name=name,
        grid_spec=pltpu.PrefetchScalarGridSpec(
            num_scalar_prefetch=1, grid=(steps,), in_specs=in_specs,
            out_specs=pl.BlockSpec((tm, c), lambda i, me_ref: (layer * steps + i, 0))),
        out_shape=jax.ShapeDtypeStruct((n_layers * rows, c), F32), input_output_aliases=aliases,
        compiler_params=_cp("parallel"),
    )(*args)


def adamw(w, m, v, g_parts, name, tm=256):
    rows, c = w.shape
    if c <= 8 * LANES:
        tm = 2 * tm
    tm = _row_tile(rows, tm)
    npart = len(g_parts)

    def body(*refs):
        w_ref, m_ref, v_ref = refs[:3]
        g_refs = refs[3:3 + npart]
        go_ref, d_ref, mo_ref, vo_ref = refs[3 + npart:]
        g = g_refs[0][...]
        for k in range(1, npart):
            g = g + g_refs[k][...]
        mn = ADAM_B1 * m_ref[...] + (1.0 - ADAM_B1) * g
        vn = ADAM_B2 * v_ref[...] + (1.0 - ADAM_B2) * (g * g)
        m_hat = mn / (1.0 - ADAM_B1 ** ADAM_STEP)
        v_hat = vn / (1.0 - ADAM_B2 ** ADAM_STEP)
        go_ref[...] = g
        d_ref[...] = -ADAM_LR * (m_hat / (jnp.sqrt(v_hat) + ADAM_EPS) + ADAM_WD * w_ref[...])
        mo_ref[...] = mn
        vo_ref[...] = vn

    row = pl.BlockSpec((tm, c), lambda i: (i, 0))
    return pl.pallas_call(
        body, name=name, grid=(rows // tm,), in_specs=[row] * (3 + npart), out_specs=[row] * 4,
        out_shape=[jax.ShapeDtypeStruct((rows, c), F32)] * 4, compiler_params=_cp("parallel"),
    )(w, m, v, *g_parts)


def _place():
    return lax.axis_index("x"), lax.axis_index("y"), lax.axis_index("c")


def _other_chips(x, y, c):
    return [(1 - x, y, c), (x, 1 - y, c), (1 - x, 1 - y, c)]


def _chip_of(px, py):
    return 2 * px + py


HBM_SPEC = pl.BlockSpec(memory_space=pltpu.HBM)
SEM_SPEC = pl.BlockSpec(memory_space=pltpu.SEMAPHORE)
ANY_SPEC = pl.BlockSpec(memory_space=pl.ANY)
DATAFLOW = pltpu.SideEffectType.DATAFLOW_SIDE_EFFECTING
N_PEER_CHIPS = N_CHIPS - 1


def _hbm(a):
    return pltpu.with_memory_space_constraint(a, pltpu.HBM)


def _hbm_like(arrays):
    return [pltpu.HBM(a.shape, a.dtype) for a in arrays]


def cast_place(w, layer, me, out_dtype, name, tm=512, nslots=N_CHIPS, dep=None):
    rows, c = w.shape[-2:]
    tm = _row_tile(rows, tm)

    def body(me_ref, w_ref, *rest):
        rest[-1][...] = w_ref[...].astype(out_dtype)

    if layer is None:
        in_specs = [pl.BlockSpec((tm, c), lambda i, me_ref: (i, 0))]
    else:
        in_specs = [pl.BlockSpec((None, tm, c), lambda i, me_ref: (layer, i, 0))]
    args = [me, w]
    if dep is not None:
        in_specs.append(pl.BlockSpec(DEP_SPEC_SHAPE, lambda i, me_ref: (0, 0)))
        args.append(dep)
    return pl.pallas_call(
        body, name=name,
        grid_spec=pltpu.PrefetchScalarGridSpec(
            num_scalar_prefetch=1, grid=(rows // tm,), in_specs=in_specs,
            out_specs=pl.BlockSpec((None, tm, c), lambda i, me_ref: (me_ref[0], i, 0))),
        out_shape=jax.ShapeDtypeStruct((nslots, rows, c), out_dtype), compiler_params=_cp("parallel"),
    )(*args)


def gather_start(lands, chunk_sizes, name="gather_start"):
    n = len(lands)
    nch = len(chunk_sizes)
    assert sum(chunk_sizes) == n

    def body(*refs):
        land_refs = refs[:n]
        outs = refs[n:]
        send_sems, recv_sems = outs[:nch], outs[nch:2 * nch]
        token = outs[-1]
        x, y, c = _place()
        me = _chip_of(x, y)
        peers = _other_chips(x, y, c)
        k = 0
        for ck, size in enumerate(chunk_sizes):
            for pos in range(size):
                for r, peer in enumerate(peers):
                    pltpu.make_async_remote_copy(
                        src_ref=land_refs[k].at[me], dst_ref=land_refs[k].at[me],
                        send_sem=send_sems[ck].at[N_PEER_CHIPS * pos + r], recv_sem=recv_sems[ck].at[N_PEER_CHIPS * pos + r],
                        device_id=peer, device_id_type=MESH).start()
                k += 1
        token[...] = jnp.zeros(token.shape, F32)

    sems = [pltpu.SemaphoreType.DMA((N_PEER_CHIPS * s,)) for s in chunk_sizes]
    res = pl.pallas_call(
        body, name=name,
        out_shape=(*sems, *sems, *_hbm_like(lands), jax.ShapeDtypeStruct(DEP_SPEC_SHAPE, F32)),
        in_specs=[HBM_SPEC] * n,
        out_specs=(*[SEM_SPEC] * (2 * nch), *[HBM_SPEC] * n, pl.BlockSpec(memory_space=pltpu.VMEM)),
        input_output_aliases={k: 2 * nch + k for k in range(n)},
        compiler_params=pltpu.CompilerParams(has_side_effects=DATAFLOW),
    )(*[_hbm(a) for a in lands])
    return res[:nch], res[nch:2 * nch], res[2 * nch:2 * nch + n], res[-1]


def gather_wait(send_sem, recv_sem, lands, after, name):
    n = len(lands)

    def body(*refs):
        land_refs = refs[:n]
        ssem, rsem = refs[n], refs[n + 1]
        x, y, c = _place()
        me = _chip_of(x, y)
        for pos in range(n):
            for r, peer in enumerate(_other_chips(x, y, c)):
                cp = pltpu.make_async_remote_copy(
                    src_ref=land_refs[pos].at[me], dst_ref=land_refs[pos].at[_chip_of(peer[0], peer[1])],
                    send_sem=ssem.at[N_PEER_CHIPS * pos + r], recv_sem=rsem.at[N_PEER_CHIPS * pos + r],
                    device_id=peer, device_id_type=MESH)
                cp.wait_send()
                cp.wait_recv()

    return pl.pallas_call(
        body, name=name, out_shape=tuple(_hbm_like(lands)),
        in_specs=[*[HBM_SPEC] * n, SEM_SPEC, SEM_SPEC, ANY_SPEC], out_specs=[HBM_SPEC] * n,
        input_output_aliases={k: k for k in range(n)},
        compiler_params=pltpu.CompilerParams(has_side_effects=DATAFLOW),
    )(*lands, send_sem, recv_sem, after)


def scatter_start(grads, name):
    n = len(grads)
    recvs = [lax.empty((N_PEER_CHIPS, *g.shape[1:]), g.dtype) for g in grads]

    def body(*refs):
        g_refs, r_refs = refs[:n], refs[n:2 * n]
        send_sems, recv_sems = refs[2 * n], refs[2 * n + 1]
        token = refs[-1]
        x, y, c = _place()
        for k in range(n):
            for r, peer in enumerate(_other_chips(x, y, c)):
                pltpu.make_async_remote_copy(
                    src_ref=g_refs[k].at[_chip_of(peer[0], peer[1])], dst_ref=r_refs[k].at[r],
                    send_sem=send_sems.at[N_PEER_CHIPS * k + r], recv_sem=recv_sems.at[N_PEER_CHIPS * k + r],
                    device_id=peer, device_id_type=MESH).start()
        token[...] = jnp.zeros(token.shape, F32)

    sem = pltpu.SemaphoreType.DMA((N_PEER_CHIPS * n,))
    res = pl.pallas_call(
        body, name=name,
        out_shape=(sem, sem, *_hbm_like(grads), *_hbm_like(recvs), jax.ShapeDtypeStruct(DEP_SPEC_SHAPE, F32)),
        in_specs=[HBM_SPEC] * (2 * n),
        out_specs=(SEM_SPEC, SEM_SPEC, *[HBM_SPEC] * (2 * n), pl.BlockSpec(memory_space=pltpu.VMEM)),
        input_output_aliases={k: 2 + k for k in range(2 * n)},
        compiler_params=pltpu.CompilerParams(has_side_effects=DATAFLOW),
    )(*[_hbm(a) for a in grads], *[_hbm(a) for a in recvs])
    return res[0], res[1], res[2:2 + n], res[2 + n:2 + 2 * n], res[-1]


def scatter_wait(send_sem, recv_sem, grads, recvs, after, name):
    n = len(grads)

    def body(*refs):
        g_refs, r_refs = refs[:n], refs[n:2 * n]
        ssem, rsem = refs[2 * n], refs[2 * n + 1]
        x, y, c = _place()
        for k in range(n):
            for r, peer in enumerate(_other_chips(x, y, c)):
                cp = pltpu.make_async_remote_copy(
                    src_ref=g_refs[k].at[_chip_of(peer[0], peer[1])], dst_ref=r_refs[k].at[r],
                    send_sem=ssem.at[N_PEER_CHIPS * k + r], recv_sem=rsem.at[N_PEER_CHIPS * k + r],
                    device_id=peer, device_id_type=MESH)
                cp.wait_send()
                cp.wait_recv()

    res = pl.pallas_call(
        body, name=name, out_shape=(*_hbm_like(grads), *_hbm_like(recvs)),
        in_specs=[*[HBM_SPEC] * (2 * n), SEM_SPEC, SEM_SPEC, ANY_SPEC], out_specs=[HBM_SPEC] * (2 * n),
        input_output_aliases={k: k for k in range(2 * n)},
        compiler_params=pltpu.CompilerParams(has_side_effects=DATAFLOW),
    )(*grads, *recvs, send_sem, recv_sem, after)
    return res[:n], res[n:]


def swap_start(parts, name):
    n = len(parts)
    lands = [lax.empty(p.shape, p.dtype) for p in parts]

    def body(*refs):
        p_refs, l_refs = refs[:n], refs[n:2 * n]
        sems = refs[2 * n:4 * n]
        token = refs[-1]
        x, y, c = _place()
        for k in range(n):
            pltpu.make_async_remote_copy(
                src_ref=p_refs[k], dst_ref=l_refs[k], send_sem=sems[k], recv_sem=sems[n + k],
                device_id=(x, y, 1 - c), device_id_type=MESH).start()
        token[...] = jnp.zeros(token.shape, F32)

    sem = pltpu.SemaphoreType.DMA(())
    res = pl.pallas_call(
        body, name=name,
        out_shape=(*[sem] * (2 * n), *_hbm_like(parts), *_hbm_like(lands), jax.ShapeDtypeStruct(DEP_SPEC_SHAPE, F32)),
        in_specs=[HBM_SPEC] * (2 * n),
        out_specs=(*[SEM_SPEC] * (2 * n), *[HBM_SPEC] * (2 * n), pl.BlockSpec(memory_space=pltpu.VMEM)),
        input_output_aliases={k: 2 * n + k for k in range(2 * n)},
        compiler_params=pltpu.CompilerParams(has_side_effects=DATAFLOW),
    )(*[_hbm(a) for a in parts], *[_hbm(a) for a in lands])
    return res[:n], res[n:2 * n], res[2 * n:3 * n], res[3 * n:4 * n], res[-1]


def swap_wait(send_sem, recv_sem, part, land, after, name):
    def body(p_ref, l_ref, ssem, rsem, after_ref, p_out, l_out):
        x, y, c = _place()
        cp = pltpu.make_async_remote_copy(src_ref=p_ref, dst_ref=l_ref, send_sem=ssem, recv_sem=rsem,
                                          device_id=(x, y, 1 - c), device_id_type=MESH)
        cp.wait_send()
        cp.wait_recv()

    return pl.pallas_call(
        body, name=name, out_shape=tuple(_hbm_like([part, land])),
        in_specs=[HBM_SPEC, HBM_SPEC, SEM_SPEC, SEM_SPEC, ANY_SPEC], out_specs=[HBM_SPEC, HBM_SPEC],
        input_output_aliases={0: 0, 1: 1},
        compiler_params=pltpu.CompilerParams(has_side_effects=DATAFLOW),
    )(part, land, send_sem, recv_sem, after)


def _xor_peer(x, y, c, k):
    px, py, pc = x ^ ((k >> 2) & 1), y ^ ((k >> 1) & 1), c ^ (k & 1)
    return (px, py, pc), 4 * px + 2 * py + pc


def small_start(land, name="small_start"):
    def body(l_ref, ssem, rsem, l_out, token):
        x, y, c = _place()
        me = 4 * x + 2 * y + c
        for k in range(1, N_DEV):
            peer, _ = _xor_peer(x, y, c, k)
            pltpu.make_async_remote_copy(
                src_ref=l_ref.at[me], dst_ref=l_ref.at[me], send_sem=ssem.at[k - 1], recv_sem=rsem.at[k - 1],
                device_id=peer, device_id_type=MESH).start()
        token[...] = jnp.zeros(token.shape, F32)

    sem = pltpu.SemaphoreType.DMA((N_DEV - 1,))
    return pl.pallas_call(
        body, name=name,
        out_shape=(sem, sem, pltpu.HBM(land.shape, land.dtype), jax.ShapeDtypeStruct(DEP_SPEC_SHAPE, F32)),
        in_specs=[HBM_SPEC], out_specs=(SEM_SPEC, SEM_SPEC, HBM_SPEC, pl.BlockSpec(memory_space=pltpu.VMEM)),
        input_output_aliases={0: 2}, compiler_params=pltpu.CompilerParams(has_side_effects=DATAFLOW),
    )(_hbm(land))


def small_wait(send_sem, recv_sem, land, after, name="small_wait"):
    def body(l_ref, ssem, rsem, after_ref, l_out):
        x, y, c = _place()
        me = 4 * x + 2 * y + c
        for k in range(1, N_DEV):
            peer, slot = _xor_peer(x, y, c, k)
            cp = pltpu.make_async_remote_copy(
                src_ref=l_ref.at[me], dst_ref=l_ref.at[slot], send_sem=ssem.at[k - 1], recv_sem=rsem.at[k - 1],
                device_id=peer, device_id_type=MESH)
            cp.wait_send()
            cp.wait_recv()

    return pl.pallas_call(
        body, name=name, out_shape=pltpu.HBM(land.shape, land.dtype),
        in_specs=[HBM_SPEC, SEM_SPEC, SEM_SPEC, ANY_SPEC], out_specs=HBM_SPEC, input_output_aliases={0: 0},
        compiler_params=pltpu.CompilerParams(has_side_effects=DATAFLOW),
    )(land, send_sem, recv_sem, after)


def sum_slots(land, name="sum_slots", tm=256):
    n, rows, c = land.shape
    tm = _row_tile(rows, tm)

    def body(l_ref, o_ref):
        acc = l_ref[0]
        for j in range(1, n):
            acc = acc + l_ref[j]
        o_ref[...] = acc

    return pl.pallas_call(
        body, name=name, grid=(rows // tm,), in_specs=[pl.BlockSpec((n, tm, c), lambda i: (0, i, 0))],
        out_specs=pl.BlockSpec((tm, c), lambda i: (i, 0)), out_shape=jax.ShapeDtypeStruct((rows, c), F32),
        compiler_params=_cp("parallel"),
    )(land)


PACK_ROW_TILE = 256


def _pack(arrays):
    flat = jnp.concatenate([a.reshape(-1).astype(F32) for a in arrays])
    n = flat.shape[0]
    rows = -(-n // LANES)
    rows = -(-rows // PACK_ROW_TILE) * PACK_ROW_TILE
    return jnp.pad(flat, (0, rows * LANES - n)).reshape(rows, LANES)


def _unpack(packed, shapes, lead=()):
    flat = packed.reshape(*lead, -1)
    out, off = [], 0
    for shp in shapes:
        n = math.prod(shp)
        out.append(flat[..., off:off + n].reshape(*lead, *shp))
        off += n
    return out


def _row(vec):
    return vec.reshape(1, -1)


def kernel(x, mix_pre_g, mix_post_g, ffn_pre_g, ffn_post_g, cm_w_in, cm_b_in, cm_dw, cm_dw_b, cm_ln_g, cm_ln_b, cm_w_out, cm_b_out, kv_norm_g, w_kv, w_q, w_o, ffn_w_in, ffn_dw, ffn_dw_b, ffn_w_out, loss_target, m_mix_pre_g, m_mix_post_g, m_ffn_pre_g, m_ffn_post_g, m_cm_w_in, m_cm_b_in, m_cm_dw, m_cm_dw_b, m_cm_ln_g, m_cm_ln_b, m_cm_w_out, m_cm_b_out, m_kv_norm_g, m_w_kv, m_w_q, m_w_o, m_ffn_w_in, m_ffn_dw, m_ffn_dw_b, m_ffn_w_out, v_mix_pre_g, v_mix_post_g, v_ffn_pre_g, v_ffn_post_g, v_cm_w_in, v_cm_b_in, v_cm_dw, v_cm_dw_b, v_cm_ln_g, v_cm_ln_b, v_cm_w_out, v_cm_b_out, v_kv_norm_g, v_w_kv, v_w_q, v_w_o, v_ffn_w_in, v_ffn_dw, v_ffn_dw_b, v_ffn_w_out):
    names = ["mix_pre_g", "mix_post_g", "ffn_pre_g", "ffn_post_g", "cm_w_in", "cm_b_in", "cm_dw", "cm_dw_b", "cm_ln_g",
             "cm_ln_b", "cm_w_out", "cm_b_out", "kv_norm_g", "w_kv", "w_q", "w_o", "ffn_w_in", "ffn_dw", "ffn_dw_b",
             "ffn_w_out"]
    w_in = dict(zip(names, [mix_pre_g, mix_post_g, ffn_pre_g, ffn_post_g, cm_w_in, cm_b_in, cm_dw, cm_dw_b, cm_ln_g,
                            cm_ln_b, cm_w_out, cm_b_out, kv_norm_g, w_kv, w_q, w_o, ffn_w_in, ffn_dw, ffn_dw_b, ffn_w_out]))
    m_in = dict(zip(names, [m_mix_pre_g, m_mix_post_g, m_ffn_pre_g, m_ffn_post_g, m_cm_w_in, m_cm_b_in, m_cm_dw, m_cm_dw_b,
                            m_cm_ln_g, m_cm_ln_b, m_cm_w_out, m_cm_b_out, m_kv_norm_g, m_w_kv, m_w_q, m_w_o, m_ffn_w_in,
                            m_ffn_dw, m_ffn_dw_b, m_ffn_w_out]))
    v_in = dict(zip(names, [v_mix_pre_g, v_mix_post_g, v_ffn_pre_g, v_ffn_post_g, v_cm_w_in, v_cm_b_in, v_cm_dw, v_cm_dw_b,
                            v_cm_ln_g, v_cm_ln_b, v_cm_w_out, v_cm_b_out, v_kv_norm_g, v_w_kv, v_w_q, v_w_o, v_ffn_w_in,
                            v_ffn_dw, v_ffn_dw_b, v_ffn_w_out]))

    bsz, seq, d = x.shape
    t = bsz * seq
    n_b = DEPTH - N_A
    hw = w_o.shape[-1]
    qw = N_GROUPS * hw
    f2 = ffn_dw_b.shape[-1]
    f = f2 // 2
    me_chip = _chip_of(lax.axis_index("x"), lax.axis_index("y"))

    big = ["cm_w_in", "cm_w_out", "w_kv", "w_q", "w_o", "ffn_w_in", "ffn_w_out"]
    row_sharded = ("cm_w_out", "w_o", "ffn_w_out")
    small_sharded = ["cm_b_in", "cm_dw", "cm_dw_b", "cm_ln_g", "cm_ln_b", "cm_b_out", "ffn_dw"]
    small_pack = _pack([w_in[n] for n in small_sharded])
    chunks = [
        [("cm_w_in", 0), ("small", None)],
        [("cm_w_out", 0)],
        [("ffn_w_in", 0), ("ffn_w_out", 0)],
        [("cm_w_in", 1), ("cm_w_out", 1)],
        [("ffn_w_in", 1), ("ffn_w_out", 1)],
        [("w_kv", None)],
        [("w_q", 0), ("w_o", 0)],
        [("ffn_w_in", 2), ("ffn_w_out", 2)],
        [("w_q", 1), ("w_o", 1)],
        [("ffn_w_in", 3), ("ffn_w_out", 3)],
    ]
    chunk_of = {pc: ck for ck, ch in enumerate(chunks) for pc in ch}

    me_arr = me_chip.astype(jnp.int32).reshape(1)

    def land_of(pc, dep=None):
        n, l = pc
        if n == "small":
            return cast_place(small_pack, None, me_arr, F32, name="place_small", dep=dep)
        return cast_place(w_in[n], l, me_arr, BF16, name=f"place_{n}_{l}", dep=dep)

    groups = [[0, 1], [2], list(range(3, len(chunks)))]
    g_send, g_recv, lands_f = {}, {}, {}

    def start_group(gi, dep, name):
        cks = groups[gi]
        pcs = [pc for ck in cks for pc in chunks[ck]]
        lands = [land_of(pc, dep if k == 0 else None) for k, pc in enumerate(pcs)]
        send, recv, lands_thru, tok = gather_start(lands, [len(chunks[ck]) for ck in cks], name=name)
        pos = 0
        for j, ck in enumerate(cks):
            g_send[ck], g_recv[ck] = send[j], recv[j]
            lands_f[ck] = lands_thru[pos:pos + len(chunks[ck])]
            pos += len(chunks[ck])
        return tok

    token_a = start_group(0, None, "gather_start_a")
    token = start_group(1, token_a, "gather_start_b")
    weights = {}

    def finish_chunk(ck, after):
        got = gather_wait(g_send[ck], g_recv[ck], lands_f[ck], after, name=f"gather_wait{ck}")
        for pc, arr in zip(chunks[ck], got):
            weights[pc] = arr.reshape(1, -1, arr.shape[-1]) if pc[0] in row_sharded else arr

    def wmat(n, l=None, after=None):
        if (n, l) not in weights:
            finish_chunk(chunk_of[(n, l)], after)
        arr = weights[(n, l)]
        return arr, arr.shape[0]

    finish_chunk(0, token)
    small_full = {}
    for n, arr4 in zip(small_sharded, _unpack(weights[("small", None)], [w_in[n].shape for n in small_sharded], lead=(N_CHIPS,))):
        shp = w_in[n].shape
        small_full[n] = jnp.moveaxis(arr4, 0, -2).reshape(*shp[:-1], N_CHIPS * shp[-1])

    x2d = x.reshape(t, d)
    saved = []
    (h1,) = resid_norm_fwd(x2d, None, None, [_row(mix_pre_g[0])], name="norm_in", dep=token)
    xcur = x2d
    kv_state = None
    for i in range(DEPTH):
        sv = {"x_in": xcur, "h1": h1}
        if i < N_A:
            z = mm_nn(h1, *wmat("cm_w_in", i, h1), bias=_row(small_full["cm_b_in"][i]), name=f"cm_in{i}",
                      spg=N_CHIPS)
            u2 = glu_conv_fwd(z.reshape(bsz, seq, 2 * d), small_full["cm_dw"][i], _row(small_full["cm_dw_b"][i]),
                              name=f"glu_conv{i}").reshape(t, d)
            u4 = ln_silu_fwd(u2, _row(small_full["cm_ln_g"][i]), _row(small_full["cm_ln_b"][i]), name=f"ln_silu{i}")
            y = mm_nn(u4, *wmat("cm_w_out", i, u4), bias=_row(small_full["cm_b_out"][i]), out_dtype=BF16,
                      name=f"cm_out{i}")
            sv.update(z=z, u2=u2, u4=u4)
        else:
            j = i - N_A
            q = mm_nn(h1, *wmat("w_q", j, h1), name=f"q_proj{j}", head_major=True, spg=N_CHIPS).reshape(-1, bsz, seq, HEAD_DIM)
            outs, lses = [], []
            for g, dil in enumerate(DILATIONS):
                o_g, l_g = attn_fwd(q, kv_state["kv"], g, dil, hw, name=f"attn_fwd{j}_{g}")
                outs.append(o_g.reshape(-1, t, HEAD_DIM))
                lses.append(l_g.reshape(t, LANES))
            merged, lsej = attn_merge(outs, lses, name=f"attn_merge{j}")
            y = mm_nn(merged, *wmat("w_o", j, merged), out_dtype=BF16, name=f"o_proj{j}")
            sv.update(q=q, merged=merged, lsej=lsej)
        x1, h2 = resid_norm_fwd(xcur, y, _row(mix_post_g[i]), [_row(ffn_pre_g[i])], name=f"resid_mix{i}")
        ffn_after = start_group(2, x1, "gather_start_c") if i == 0 else h2
        p = mm_nn(h2, *wmat("ffn_w_in", i, ffn_after), out_dtype=BF16, name=f"ffn_in{i}", spg=2)
        s_act, conv_a, conv_g = ffn_mid_fwd(p.reshape(bsz, seq, f2), small_full["ffn_dw"][i], _row(ffn_dw_b[i]), name=f"ffn_mid{i}")
        s_act = s_act.reshape(t, f)
        y2 = mm_nn(s_act, *wmat("ffn_w_out", i), out_dtype=BF16, name=f"ffn_out{i}")
        next_gains = []
        if i + 1 < DEPTH:
            next_gains.append(_row(mix_pre_g[i + 1]))
        if i == N_A - 1:
            next_gains.append(_row(kv_norm_g))
        res = resid_norm_fwd(x1, y2, _row(ffn_post_g[i]), next_gains, name=f"resid_ffn{i}")
        sv.update(y=y, x1=x1, h2=h2, p=p, s=s_act, y2=y2, conv_a=conv_a, conv_g=conv_g)
        saved.append(sv)
        xcur = res[0]
        if i + 1 < DEPTH:
            h1 = res[1]
        if i == N_A - 1:
            kvn = res[2]
            kv = mm_nn(kvn, *wmat("w_kv", None, kvn), name="kv_proj", head_major=True, spg=2).reshape(-1, bsz, seq, HEAD_DIM)
            kv_state = {"kv": kv, "kvn": kvn, "x_a": xcur}

    dx, loss_tile = loss_fwd_bwd(xcur, loss_target.reshape(t, d))
    loss = lax.psum(loss_tile[0, 0], ("x", "y", "c"))

    gsm = {n: [None] * w_in[n].shape[0] for n in
           ["mix_pre_g", "mix_post_g", "ffn_pre_g", "ffn_post_g", "cm_b_in", "cm_dw", "cm_dw_b", "cm_ln_g", "cm_ln_b",
            "cm_b_out", "ffn_dw", "ffn_dw_b"]}
    gbig = {}
    in_flight = []
    dep = None

    def start_scatter(pcs, tag):
        ssem, rsem, g_f, r_f, tok = scatter_start([gbig[pc] for pc in pcs], name=f"scatter_start_{tag}")
        in_flight.append((pcs, ssem, rsem, g_f, r_f))
        return tok

    dk_buf = dv_buf = None
    for i in range(DEPTH - 1, -1, -1):
        sv = saved[i]
        dy2, dg, _ = norm_bwd(sv["y2"], _row(ffn_post_g[i]), dx, out_dtype=BF16, name=f"bwd_ffn_post{i}", dep=dep)
        gsm["ffn_post_g"][i] = dg
        ds = mm_nt(dy2, *wmat("ffn_w_out", i), out_dtype=BF16, name=f"bwd_ffn_out_dx{i}")
        gbig[("ffn_w_out", i)] = mm_tn(sv["s"], dy2, 1, name=f"bwd_ffn_out_dw{i}").reshape(N_CHIPS, f // N_CHIPS, d)
        dpa, dpg, ddwa, ddwg, ddba, ddbg = ffn_mid_bwd(sv["p"].reshape(bsz, seq, f2), small_full["ffn_dw"][i], sv["conv_a"],
                                                       sv["conv_g"], ds.reshape(bsz, seq, f), name=f"bwd_ffn_mid{i}")
        gsm["ffn_dw"][i] = jnp.concatenate([jnp.sum(ddwa, axis=1), jnp.sum(ddwg, axis=1)], axis=-1)
        gsm["ffn_dw_b"][i] = jnp.concatenate([ddba, ddbg], axis=-1)
        dp = [dpa.reshape(t, f), dpg.reshape(t, f)]
        dh2 = mm_nt(dp, *wmat("ffn_w_in", i), out_dtype=BF16, name=f"bwd_ffn_in_dx{i}", spg=2)
        gbig[("ffn_w_in", i)] = mm_tn(sv["h2"], dp, N_CHIPS, name=f"bwd_ffn_in_dw{i}", tm=2048)
        dx1, dg, _ = norm_bwd(sv["x1"], _row(ffn_pre_g[i]), dh2, add=dx, name=f"bwd_ffn_pre{i}")
        gsm["ffn_pre_g"][i] = dg
        dep = start_scatter([("ffn_w_in", 0), ("ffn_w_out", 0)], "ffn0") if i == 0 else None
        dy, dg, dbias = norm_bwd(sv["y"], _row(mix_post_g[i]), dx1, out_dtype=BF16, name=f"bwd_mix_post{i}", dep=dep)
        gsm["mix_post_g"][i] = dg
        if i < N_A:
            gsm["cm_b_out"][i] = dbias
            du4 = mm_nt(dy, *wmat("cm_w_out", i), out_dtype=BF16, name=f"bwd_cm_out_dx{i}")
            gbig[("cm_w_out", i)] = mm_tn(sv["u4"], dy, 1, name=f"bwd_cm_out_dw{i}").reshape(N_CHIPS, d // N_CHIPS, d)
            du2, dlg, dlb = ln_silu_bwd(sv["u2"], _row(small_full["cm_ln_g"][i]), _row(small_full["cm_ln_b"][i]), du4,
                                        name=f"bwd_ln_silu{i}")
            gsm["cm_ln_g"][i], gsm["cm_ln_b"][i] = dlg, dlb
            dza, dzg, ddw, ddwb, dba, dbg = glu_conv_bwd(sv["z"].reshape(bsz, seq, 2 * d), small_full["cm_dw"][i],
                                                         du2.reshape(bsz, seq, d), name=f"bwd_glu_conv{i}")
            gsm["cm_dw"][i] = jnp.sum(ddw, axis=1)
            gsm["cm_dw_b"][i] = ddwb
            gsm["cm_b_in"][i] = jnp.concatenate([dba, dbg], axis=-1)
            dz = [dza.reshape(t, d), dzg.reshape(t, d)]
            dh1 = mm_nt(dz, *wmat("cm_w_in", i), out_dtype=BF16, name=f"bwd_cm_in_dx{i}", spg=2)
            gbig[("cm_w_in", i)] = mm_tn(sv["h1"], dz, N_CHIPS, name=f"bwd_cm_in_dw{i}", spg=2)
        else:
            j = i - N_A
            dmerged = mm_nt(dy, *wmat("w_o", j), name=f"bwd_o_proj_dx{j}")
            gbig[("w_o", j)] = mm_tn(sv["merged"], dy, 1, name=f"bwd_o_proj_dw{j}").reshape(N_CHIPS, hw // N_CHIPS, d)
            dmt = attn_bwd_prep(dmerged, sv["merged"], name=f"bwd_attn_prep{j}")
            dq_buf = None
            add_to_kv = dk_buf is not None
            for g, dil in enumerate(DILATIONS):
                dq_buf, dk_buf, dv_buf = attn_bwd(
                    sv["q"], kv_state["kv"], g, dil, dmerged.reshape(bsz, seq, hw), sv["lsej"].reshape(bsz, seq, LANES),
                    dmt.reshape(bsz, seq, LANES), dq_buf, dk_buf, dv_buf, add_to_kv, hw, name=f"attn_bwd{j}_{g}")
            dq = dq_buf.reshape(t, qw)
            dh1 = mm_nt(dq, *wmat("w_q", j), out_dtype=BF16, name=f"bwd_q_proj_dx{j}", tm=512, spg=N_CHIPS)
            gbig[("w_q", j)] = mm_tn(sv["h1"], dq, N_CHIPS, name=f"bwd_q_proj_dw{j}", tm=512, spg=N_CHIPS)
        dx, dg, _ = norm_bwd(sv["x_in"], _row(mix_pre_g[i]), dh1, add=dx1, name=f"bwd_mix_pre{i}")
        gsm["mix_pre_g"][i] = dg
        if i > N_A:
            dep = start_scatter([("ffn_w_in", i), ("ffn_w_out", i), ("w_q", i - N_A), ("w_o", i - N_A)], f"l{i}")
        elif 0 < i < N_A:
            dep = start_scatter([("ffn_w_in", i), ("ffn_w_out", i), ("cm_w_in", i), ("cm_w_out", i)], f"l{i}")
        elif i == 0:
            last_token = start_scatter([("cm_w_in", 0), ("cm_w_out", 0)], "cm0")
        if i == N_A:
            dkv = [dk_buf.reshape(t, qw), dv_buf.reshape(t, qw)]
            dkvn = mm_nt(dkv, *wmat("w_kv"), out_dtype=BF16, name="bwd_kv_proj_dx")
            gbig[("w_kv", None)] = mm_tn(kv_state["kvn"], dkv, N_CHIPS, name="bwd_kv_proj_dw")
            dx, dg_kv, _ = norm_bwd(kv_state["x_a"], _row(kv_norm_g), dkvn, add=dx, name="bwd_kv_norm")
            dep = start_scatter([("ffn_w_in", i), ("ffn_w_out", i), ("w_q", 0), ("w_o", 0), ("w_kv", None)], f"l{i}")
    grad_x = dx.reshape(bsz, seq, d)

    plane_of = {}
    outs_g, outs_d, outs_m, outs_v = {}, {}, {}, {}

    def finish_scatter(k, after):
        pcs, ssem, rsem, g_f, r_f = in_flight[k]
        g_done, r_done = scatter_wait(ssem, rsem, g_f, r_f, after, name=f"scatter_wait{k}")
        for (n, l), g_arr, r_arr in zip(pcs, g_done, r_done):
            n_layers = 1 if l is None else w_in[n].shape[0]
            plane_of[n] = sum_parts(g_arr, r_arr, me_arr, name=f"sum_chips_{n}_{l}", layer=l or 0, n_layers=n_layers,
                                    out_buf=plane_of.get(n))

    def update(group, tag, after):
        plane = [plane_of[n] for n in group]
        ssems, rsems, plane_f, land_f, _ = swap_start(plane, name=f"swap_start_{tag}")
        for k, n in enumerate(group):
            p_mine, p_other = swap_wait(ssems[k], rsems[k], plane_f[k], land_f[k], after, name=f"swap_wait_{n}")
            shp = w_in[n].shape
            flat = lambda a: a.reshape(-1, shp[-1])
            g_, d_, m_, v_ = adamw(flat(w_in[n]), flat(m_in[n]), flat(v_in[n]), [p_mine, p_other], name=f"adamw_{n}")
            outs_g[n], outs_d[n], outs_m[n], outs_v[n] = (a.reshape(shp) for a in (g_, d_, m_, v_))
            after = v_
        return after

    small_names = [n for n in names if n not in big]
    small_shapes_full = {}
    small_grads_full = []
    for n in small_names:
        if n == "kv_norm_g":
            gfull = dg_kv.reshape(-1)
        elif n in ("cm_dw", "ffn_dw"):
            gfull = jnp.stack(gsm[n], axis=0)
        else:
            gfull = jnp.stack([a.reshape(-1) for a in gsm[n]], axis=0)
        small_shapes_full[n] = gfull.shape
        small_grads_full.append(gfull)
    dev_arr = (4 * lax.axis_index("x") + 2 * lax.axis_index("y") + lax.axis_index("c")).astype(jnp.int32).reshape(1)
    small_land = cast_place(_pack(small_grads_full) + last_token[0, 0], None, dev_arr, F32, name="place_small_grads",
                            nslots=N_DEV)
    sm_send, sm_recv, small_land, small_token = small_start(small_land)

    for k in range(len(in_flight) - 1):
        finish_scatter(k, small_token)
    done = update(["w_kv", "w_q", "w_o", "ffn_w_in", "ffn_w_out"], "a", small_token)
    finish_scatter(len(in_flight) - 1, done)
    done = update(["cm_w_in", "cm_w_out"], "b", done)

    summed = sum_slots(small_wait(sm_send, sm_recv, small_land, done))
    g_full = dict(zip(small_names, _unpack(summed, [small_shapes_full[n] for n in small_names])))
    g_loc = {}
    for n in small_names:
        if n in small_sharded:
            width = w_in[n].shape[-1]
            g_loc[n] = lax.dynamic_slice_in_dim(g_full[n], me_chip * width, width, axis=g_full[n].ndim - 1)
        else:
            g_loc[n] = g_full[n]
    res = adamw(_pack([w_in[n] for n in small_names]), _pack([m_in[n] for n in small_names]),
                _pack([v_in[n] for n in small_names]), [_pack([g_loc[n] for n in small_names])], name="adamw_small")
    shapes_loc = [w_in[n].shape for n in small_names]
    for dst, packed in zip((outs_g, outs_d, outs_m, outs_v), res):
        for n, a in zip(small_names, _unpack(packed, shapes_loc)):
            dst[n] = a

    return (loss, grad_x, *[outs_g[n] for n in names], *[outs_d[n] for n in names],
            *[outs_m[n] for n in names], *[outs_v[n] for n in names])
```
